```python
import math
import jax
import jax.numpy as jnp
from jax import lax
import numpy as np

D_MODEL = 1024
BATCH = 8
SEQ = 4096
DEPTH = 2

EPS = 1e-6
PLE_DIM = 256
D_FF = 2816

CONV_CH = 1024
CONV_WIDTH = 31

SSM_HEADS = 16
SSM_HEAD_DIM = 64
SSM_INNER = SSM_HEADS * SSM_HEAD_DIM
SSM_GROUPS = 2
SSM_STATE = 128
SSM_CONV = 4
SSM_CHUNK = 128
SSM_XBC = SSM_INNER + 2 * SSM_GROUPS * SSM_STATE

HYB_IN = 2 * CONV_CH + SSM_INNER + SSM_XBC + SSM_HEADS
HYB_MIX = CONV_CH + SSM_INNER

ATT_HEADS = 16
ATT_KV_HEADS = 4
ATT_HEAD_DIM = 64
ATT_QKV = (ATT_HEADS + 2 * ATT_KV_HEADS) * ATT_HEAD_DIM
WINDOW = 128
ROPE_THETA = 10000.0

N_EVEN = (DEPTH + 1) // 2
N_ODD = DEPTH // 2

kernel_name = 'macaron_conv_ssd_swa_hybrid'


def rms_norm(x, g):
    xf = x.astype(jnp.float32)
    y = xf * lax.rsqrt(jnp.mean(xf * xf, axis=-1, keepdims=True) + EPS)
    return (y * g.astype(jnp.float32)).astype(x.dtype)


def layer_norm(x, g, b):
    xf = x.astype(jnp.float32)
    mu = jnp.mean(xf, axis=-1, keepdims=True)
    xc = xf - mu
    var = jnp.mean(xc * xc, axis=-1, keepdims=True)
    y = xc * lax.rsqrt(var + EPS) * g.astype(jnp.float32) + b.astype(jnp.float32)
    return y.astype(x.dtype)


def grouped_rms_norm(y, g):
    bsz, seqlen, ch = y.shape
    yg = y.reshape(bsz, seqlen, SSM_GROUPS, ch // SSM_GROUPS)
    yg = yg * lax.rsqrt(jnp.mean(yg * yg, axis=-1, keepdims=True) + EPS)
    return yg.reshape(bsz, seqlen, ch) * g.astype(jnp.float32)


def swiglu(x, w_in, w_out):
    gate, up = jnp.split(x @ w_in, 2, axis=-1)
    return (jax.nn.silu(gate) * up) @ w_out


def causal_depthwise_conv(x, w, b):
    width, ch = w.shape
    y = lax.conv_general_dilated(x, w[:, None, :], window_strides=(1,), padding=[(width - 1, 0)],
                                 dimension_numbers=('NWC', 'WIO', 'NWC'), feature_group_count=ch)
    return y + b


def segsum_exp(a):
    t = a.shape[-1]
    cs = jnp.cumsum(a, axis=-1)
    diff = cs[..., :, None] - cs[..., None, :]
    mask = jnp.tril(jnp.ones((t, t), dtype=bool))
    return jnp.exp(jnp.where(mask, diff, -jnp.inf))


def ssd_chunked(x, dt, a, b, c):
    bsz, seqlen = x.shape[:2]
    nc = seqlen // SSM_CHUNK
    r = SSM_HEADS // SSM_GROUPS
    ln = SSM_CHUNK
    xdt = (x * dt[..., None]).reshape(bsz, nc, ln, SSM_GROUPS, r, SSM_HEAD_DIM)
    adt = (dt * a).reshape(bsz, nc, ln, SSM_GROUPS, r).transpose(0, 3, 4, 1, 2)
    a_cs = jnp.cumsum(adt, axis=-1)
    bc = b.reshape(bsz, nc, ln, SSM_GROUPS, SSM_STATE)
    cc = c.reshape(bsz, nc, ln, SSM_GROUPS, SSM_STATE)
    cb = jnp.einsum('bclgn,bcsgn->bcgls', cc, bc)
    y_diag = jnp.einsum('bcgls,bgrcls,bcsgrp->bclgrp', cb, segsum_exp(adt), xdt)
    decay_to_end = jnp.exp(a_cs[..., -1:] - a_cs)
    states = jnp.einsum('bclgn,bgrcl,bclgrp->bcgrpn', bc, decay_to_end, xdt)
    chunk_decay = jnp.exp(a_cs[..., -1])

    def step(h, inp):
        s_c, d_c = inp
        return h * d_c[..., None, None] + s_c, h

    h0 = jnp.zeros((bsz, SSM_GROUPS, r, SSM_HEAD_DIM, SSM_STATE), jnp.float32)
    _, prev = lax.scan(step, h0, (jnp.moveaxis(states, 1, 0), jnp.moveaxis(chunk_decay, 3, 0)))
    y_off = jnp.einsum('bclgn,cbgrpn,bgrcl->bclgrp', cc, prev, jnp.exp(a_cs))
    return (y_diag + y_off).reshape(bsz, seqlen, SSM_HEADS, SSM_HEAD_DIM)


def conv_ssd_mixer(hn, w_in, cv_w, cv_b, cv_g, cv_beta, sc_w, sc_b, dt_bias, a_log, d_skip, ssm_norm, w_out):
    bsz, seqlen, _ = hn.shape
    f32 = jnp.float32
    cut = np.cumsum([CONV_CH, CONV_CH, SSM_INNER, SSM_XBC]).tolist()
    cv_val, cv_gate, z, xbc, dt_raw = jnp.split(hn @ w_in, cut, axis=-1)
    u = cv_val * jax.nn.sigmoid(cv_gate)
    u = causal_depthwise_conv(u, cv_w, cv_b)
    u = jax.nn.silu(layer_norm(u, cv_g, cv_beta))
    xbc = jax.nn.silu(causal_depthwise_conv(xbc, sc_w, sc_b))
    xs, bs, cs = jnp.split(xbc, [SSM_INNER, SSM_INNER + SSM_GROUPS * SSM_STATE], axis=-1)
    xs = xs.reshape(bsz, seqlen, SSM_HEADS, SSM_HEAD_DIM).astype(f32)
    dt = jax.nn.softplus(dt_raw.astype(f32) + dt_bias.astype(f32))
    a = -jnp.exp(a_log.astype(f32))
    y = ssd_chunked(xs, dt, a,
                    bs.reshape(bsz, seqlen, SSM_GROUPS, SSM_STATE).astype(f32),
                    cs.reshape(bsz, seqlen, SSM_GROUPS, SSM_STATE).astype(f32))
    y = y + d_skip.astype(f32)[:, None] * xs
    y = y.reshape(bsz, seqlen, SSM_INNER) * jax.nn.silu(z.astype(f32))
    y = grouped_rms_norm(y, ssm_norm).astype(hn.dtype)
    return jnp.concatenate([u, y], axis=-1) @ w_out


def rope_tables(seqlen):
    inv = ROPE_THETA ** (-jnp.arange(0, ATT_HEAD_DIM, 2, dtype=jnp.float32) / ATT_HEAD_DIM)
    ang = jnp.arange(seqlen, dtype=jnp.float32)[:, None] * inv[None, :]
    return jnp.cos(ang), jnp.sin(ang)


def apply_rope(t, cos, sin):
    half = t.shape[-1] // 2
    t1, t2 = t[..., :half], t[..., half:]
    c = cos[None, :, None, :].astype(t.dtype)
    s = sin[None, :, None, :].astype(t.dtype)
    return jnp.concatenate([t1 * c - t2 * s, t2 * c + t1 * s], axis=-1)


def swa_sink_attention(hn, w_qkv, b_qkv, sinks, w_o, b_o, cos, sin):
    bsz, seqlen, _ = hn.shape
    nb = seqlen // WINDOW
    grp = ATT_HEADS // ATT_KV_HEADS
    q, k, v = jnp.split(hn @ w_qkv + b_qkv,
                        [ATT_HEADS * ATT_HEAD_DIM, (ATT_HEADS + ATT_KV_HEADS) * ATT_HEAD_DIM], axis=-1)
    q = apply_rope(q.reshape(bsz, seqlen, ATT_HEADS, ATT_HEAD_DIM), cos, sin)
    k = apply_rope(k.reshape(bsz, seqlen, ATT_KV_HEADS, ATT_HEAD_DIM), cos, sin)
    v = v.reshape(bsz, seqlen, ATT_KV_HEADS, ATT_HEAD_DIM)
    qb = q.reshape(bsz, nb, WINDOW, ATT_KV_HEADS, grp, ATT_HEAD_DIM)

    def band(t):
        tp = jnp.pad(t, ((0, 0), (WINDOW, 0), (0, 0), (0, 0)))
        tp = tp.reshape(bsz, nb + 1, WINDOW, ATT_KV_HEADS, ATT_HEAD_DIM)
        return jnp.concatenate([tp[:, :-1], tp[:, 1:]], axis=2)

    kb, vb = band(k), band(v)
    logits = jnp.einsum('bnqkgd,bnskd->bnkgqs', qb, kb).astype(jnp.float32) * (ATT_HEAD_DIM ** -0.5)
    qpos = jnp.arange(nb)[:, None, None] * WINDOW + jnp.arange(WINDOW)[None, :, None]
    kpos = jnp.arange(nb)[:, None, None] * WINDOW - WINDOW + jnp.arange(2 * WINDOW)[None, None, :]
    rel = qpos - kpos
    valid = (kpos >= 0) & (rel >= 0) & (rel < WINDOW)
    logits = jnp.where(valid[None, :, None, None], logits, -jnp.inf)
    sink = sinks.astype(jnp.float32).reshape(ATT_KV_HEADS, grp)[None, None, :, :, None, None]
    m = jnp.maximum(jnp.max(logits, axis=-1, keepdims=True), sink)
    e = jnp.exp(logits - m)
    probs = e / (jnp.sum(e, axis=-1, keepdims=True) + jnp.exp(sink - m))
    o = jnp.einsum('bnkgqs,bnskd->bnqkgd', probs.astype(vb.dtype), vb)
    return o.reshape(bsz, seqlen, ATT_HEADS * ATT_HEAD_DIM) @ w_o + b_o


def _fwd_setup_inputs(seed: int = 0) -> dict:
    key = jax.random.key(seed)
    ks = iter(jax.random.split(key, 40))
    D = D_MODEL

    def nrm(shape, scale):
        return scale * jax.random.normal(next(ks), shape, jnp.float32)

    def gain(shape):
        return 1.0 + nrm(shape, 0.02)

    x = jax.random.normal(next(ks), (BATCH, SEQ, D), jnp.float32)
    p = jax.random.normal(next(ks), (DEPTH, BATCH, SEQ, PLE_DIM), jnp.float32)
    dt0 = jnp.exp(jax.random.uniform(next(ks), (N_EVEN, SSM_HEADS), jnp.float32,
                                     minval=math.log(1e-3), maxval=math.log(1e-1)))
    a_init = jax.random.uniform(next(ks), (N_EVEN, SSM_HEADS), jnp.float32, minval=1.0, maxval=16.0)
    return {
        'x': x,
        'p': p,
        'norm_ffn1': gain((DEPTH, D)),
        'ffn1_w_in': nrm((DEPTH, D, 2 * D_FF), D ** -0.5),
        'ffn1_w_out': nrm((DEPTH, D_FF, D), D_FF ** -0.5),
        'norm_mix': gain((DEPTH, D)),
        'norm_ffn2': gain((DEPTH, D)),
        'ffn2_w_in': nrm((DEPTH, D, 2 * D_FF), D ** -0.5),
        'ffn2_w_out': nrm((DEPTH, D_FF, D), D_FF ** -0.5),
        'ple_norm': gain((DEPTH, D)),
        'ple_gate_w': nrm((DEPTH, D, D), D ** -0.5),
        'ple_proj_w': nrm((DEPTH, PLE_DIM, D), PLE_DIM ** -0.5),
        'hyb_w_in': nrm((N_EVEN, D, HYB_IN), D ** -0.5),
        'conv_dw_w': nrm((N_EVEN, CONV_WIDTH, CONV_CH), CONV_WIDTH ** -0.5),
        'conv_dw_b': nrm((N_EVEN, CONV_CH), 0.02),
        'conv_ln_g': gain((N_EVEN, CONV_CH)),
        'conv_ln_b': nrm((N_EVEN, CONV_CH), 0.02),
        'ssm_conv_w': nrm((N_EVEN, SSM_CONV, SSM_XBC), SSM_CONV ** -0.5),
        'ssm_conv_b': nrm((N_EVEN, SSM_XBC), 0.02),
        'ssm_dt_bias': dt0 + jnp.log(-jnp.expm1(-dt0)),
        'ssm_a_log': jnp.log(a_init),
        'ssm_d': gain((N_EVEN, SSM_HEADS)),
        'ssm_norm': gain((N_EVEN, SSM_INNER)),
        'hyb_w_out': nrm((N_EVEN, HYB_MIX, D), HYB_MIX ** -0.5),
        'att_w_qkv': nrm((N_ODD, D, ATT_QKV), D ** -0.5),
        'att_b_qkv': nrm((N_ODD, ATT_QKV), 0.02),
        'att_sinks': nrm((N_ODD, ATT_HEADS), 0.5),
        'att_w_o': nrm((N_ODD, ATT_HEADS * ATT_HEAD_DIM, D), (ATT_HEADS * ATT_HEAD_DIM) ** -0.5),
        'att_b_o': nrm((N_ODD, D), 0.02),
        'final_norm': gain((D,)),
    }


def _fwd_reference(x, p, norm_ffn1, ffn1_w_in, ffn1_w_out, norm_mix, norm_ffn2, ffn2_w_in, ffn2_w_out,
              ple_norm, ple_gate_w, ple_proj_w, hyb_w_in, conv_dw_w, conv_dw_b, conv_ln_g, conv_ln_b,
              ssm_conv_w, ssm_conv_b, ssm_dt_bias, ssm_a_log, ssm_d, ssm_norm, hyb_w_out,
              att_w_qkv, att_b_qkv, att_sinks, att_w_o, att_b_o, final_norm):
    cos, sin = rope_tables(x.shape[1])
    h = x
    for i in range(DEPTH):
        j = i // 2
        h = h + 0.5 * swiglu(rms_norm(h, norm_ffn1[i]), ffn1_w_in[i], ffn1_w_out[i])
        hn = rms_norm(h, norm_mix[i])
        if i % 2 == 0:
            h = h + conv_ssd_mixer(hn, hyb_w_in[j], conv_dw_w[j], conv_dw_b[j], conv_ln_g[j], conv_ln_b[j],
                                   ssm_conv_w[j], ssm_conv_b[j], ssm_dt_bias[j], ssm_a_log[j], ssm_d[j],
                                   ssm_norm[j], hyb_w_out[j])
        else:
            h = h + swa_sink_attention(hn, att_w_qkv[j], att_b_qkv[j], att_sinks[j], att_w_o[j], att_b_o[j],
                                       cos, sin)
        h = h + 0.5 * swiglu(rms_norm(h, norm_ffn2[i]), ffn2_w_in[i], ffn2_w_out[i])
        gate = jax.nn.sigmoid(rms_norm(h, ple_norm[i]) @ ple_gate_w[i])
        h = h + gate * (p[i] @ ple_proj_w[i])
    return rms_norm(h, final_norm)


import jax as _jax
import jax.numpy as _jnp

TWIN_FORMAT = 'train_step'
FWD_PARAMS = ['x', 'p', 'norm_ffn1', 'ffn1_w_in', 'ffn1_w_out', 'norm_mix', 'norm_ffn2', 'ffn2_w_in', 'ffn2_w_out', 'ple_norm', 'ple_gate_w', 'ple_proj_w', 'hyb_w_in', 'conv_dw_w', 'conv_dw_b', 'conv_ln_g', 'conv_ln_b', 'ssm_conv_w', 'ssm_conv_b', 'ssm_dt_bias', 'ssm_a_log', 'ssm_d', 'ssm_norm', 'hyb_w_out', 'att_w_qkv', 'att_b_qkv', 'att_sinks', 'att_w_o', 'att_b_o', 'final_norm']
TWIN_WEIGHTS = ['norm_ffn1', 'ffn1_w_in', 'ffn1_w_out', 'norm_mix', 'norm_ffn2', 'ffn2_w_in', 'ffn2_w_out', 'ple_norm', 'ple_gate_w', 'ple_proj_w', 'hyb_w_in', 'conv_dw_w', 'conv_dw_b', 'conv_ln_g', 'conv_ln_b', 'ssm_conv_w', 'ssm_conv_b', 'ssm_dt_bias', 'ssm_a_log', 'ssm_d', 'ssm_norm', 'hyb_w_out', 'att_w_qkv', 'att_b_qkv', 'att_sinks', 'att_w_o', 'att_b_o', 'final_norm']
TWIN_DIFF_INPUT = 'x'
TWIN_INPUTS = ['x', 'p', 'norm_ffn1', 'ffn1_w_in', 'ffn1_w_out', 'norm_mix', 'norm_ffn2', 'ffn2_w_in', 'ffn2_w_out', 'ple_norm', 'ple_gate_w', 'ple_proj_w', 'hyb_w_in', 'conv_dw_w', 'conv_dw_b', 'conv_ln_g', 'conv_ln_b', 'ssm_conv_w', 'ssm_conv_b', 'ssm_dt_bias', 'ssm_a_log', 'ssm_d', 'ssm_norm', 'hyb_w_out', 'att_w_qkv', 'att_b_qkv', 'att_sinks', 'att_w_o', 'att_b_o', 'final_norm', 'loss_target', 'm_norm_ffn1', 'm_ffn1_w_in', 'm_ffn1_w_out', 'm_norm_mix', 'm_norm_ffn2', 'm_ffn2_w_in', 'm_ffn2_w_out', 'm_ple_norm', 'm_ple_gate_w', 'm_ple_proj_w', 'm_hyb_w_in', 'm_conv_dw_w', 'm_conv_dw_b', 'm_conv_ln_g', 'm_conv_ln_b', 'm_ssm_conv_w', 'm_ssm_conv_b', 'm_ssm_dt_bias', 'm_ssm_a_log', 'm_ssm_d', 'm_ssm_norm', 'm_hyb_w_out', 'm_att_w_qkv', 'm_att_b_qkv', 'm_att_sinks', 'm_att_w_o', 'm_att_b_o', 'm_final_norm', 'v_norm_ffn1', 'v_ffn1_w_in', 'v_ffn1_w_out', 'v_norm_mix', 'v_norm_ffn2', 'v_ffn2_w_in', 'v_ffn2_w_out', 'v_ple_norm', 'v_ple_gate_w', 'v_ple_proj_w', 'v_hyb_w_in', 'v_conv_dw_w', 'v_conv_dw_b', 'v_conv_ln_g', 'v_conv_ln_b', 'v_ssm_conv_w', 'v_ssm_conv_b', 'v_ssm_dt_bias', 'v_ssm_a_log', 'v_ssm_d', 'v_ssm_norm', 'v_hyb_w_out', 'v_att_w_qkv', 'v_att_b_qkv', 'v_att_sinks', 'v_att_w_o', 'v_att_b_o', 'v_final_norm']
TWIN_OUTPUTS = ['loss', 'grad_x', 'grad_norm_ffn1', 'grad_ffn1_w_in', 'grad_ffn1_w_out', 'grad_norm_mix', 'grad_norm_ffn2', 'grad_ffn2_w_in', 'grad_ffn2_w_out', 'grad_ple_norm', 'grad_ple_gate_w', 'grad_ple_proj_w', 'grad_hyb_w_in', 'grad_conv_dw_w', 'grad_conv_dw_b', 'grad_conv_ln_g', 'grad_conv_ln_b', 'grad_ssm_conv_w', 'grad_ssm_conv_b', 'grad_ssm_dt_bias', 'grad_ssm_a_log', 'grad_ssm_d', 'grad_ssm_norm', 'grad_hyb_w_out', 'grad_att_w_qkv', 'grad_att_b_qkv', 'grad_att_sinks', 'grad_att_w_o', 'grad_att_b_o', 'grad_final_norm', 'delta_norm_ffn1', 'delta_ffn1_w_in', 'delta_ffn1_w_out', 'delta_norm_mix', 'delta_norm_ffn2', 'delta_ffn2_w_in', 'delta_ffn2_w_out', 'delta_ple_norm', 'delta_ple_gate_w', 'delta_ple_proj_w', 'delta_hyb_w_in', 'delta_conv_dw_w', 'delta_conv_dw_b', 'delta_conv_ln_g', 'delta_conv_ln_b', 'delta_ssm_conv_w', 'delta_ssm_conv_b', 'delta_ssm_dt_bias', 'delta_ssm_a_log', 'delta_ssm_d', 'delta_ssm_norm', 'delta_hyb_w_out', 'delta_att_w_qkv', 'delta_att_b_qkv', 'delta_att_sinks', 'delta_att_w_o', 'delta_att_b_o', 'delta_final_norm', 'new_m_norm_ffn1', 'new_m_ffn1_w_in', 'new_m_ffn1_w_out', 'new_m_norm_mix', 'new_m_norm_ffn2', 'new_m_ffn2_w_in', 'new_m_ffn2_w_out', 'new_m_ple_norm', 'new_m_ple_gate_w', 'new_m_ple_proj_w', 'new_m_hyb_w_in', 'new_m_conv_dw_w', 'new_m_conv_dw_b', 'new_m_conv_ln_g', 'new_m_conv_ln_b', 'new_m_ssm_conv_w', 'new_m_ssm_conv_b', 'new_m_ssm_dt_bias', 'new_m_ssm_a_log', 'new_m_ssm_d', 'new_m_ssm_norm', 'new_m_hyb_w_out', 'new_m_att_w_qkv', 'new_m_att_b_qkv', 'new_m_att_sinks', 'new_m_att_w_o', 'new_m_att_b_o', 'new_m_final_norm', 'new_v_norm_ffn1', 'new_v_ffn1_w_in', 'new_v_ffn1_w_out', 'new_v_norm_mix', 'new_v_norm_ffn2', 'new_v_ffn2_w_in', 'new_v_ffn2_w_out', 'new_v_ple_norm', 'new_v_ple_gate_w', 'new_v_ple_proj_w', 'new_v_hyb_w_in', 'new_v_conv_dw_w', 'new_v_conv_dw_b', 'new_v_conv_ln_g', 'new_v_conv_ln_b', 'new_v_ssm_conv_w', 'new_v_ssm_conv_b', 'new_v_ssm_dt_bias', 'new_v_ssm_a_log', 'new_v_ssm_d', 'new_v_ssm_norm', 'new_v_hyb_w_out', 'new_v_att_w_qkv', 'new_v_att_b_qkv', 'new_v_att_sinks', 'new_v_att_w_o', 'new_v_att_b_o', 'new_v_final_norm']
TWIN_LEAF_KINDS = {'loss': 'loss', 'grad_x': 'grad_x', 'grad_norm_ffn1': 'grad_w', 'grad_ffn1_w_in': 'grad_w', 'grad_ffn1_w_out': 'grad_w', 'grad_norm_mix': 'grad_w', 'grad_norm_ffn2': 'grad_w', 'grad_ffn2_w_in': 'grad_w', 'grad_ffn2_w_out': 'grad_w', 'grad_ple_norm': 'grad_w', 'grad_ple_gate_w': 'grad_w', 'grad_ple_proj_w': 'grad_w', 'grad_hyb_w_in': 'grad_w', 'grad_conv_dw_w': 'grad_w', 'grad_conv_dw_b': 'grad_w', 'grad_conv_ln_g': 'grad_w', 'grad_conv_ln_b': 'grad_w', 'grad_ssm_conv_w': 'grad_w', 'grad_ssm_conv_b': 'grad_w', 'grad_ssm_dt_bias': 'grad_w', 'grad_ssm_a_log': 'grad_w', 'grad_ssm_d': 'grad_w', 'grad_ssm_norm': 'grad_w', 'grad_hyb_w_out': 'grad_w', 'grad_att_w_qkv': 'grad_w', 'grad_att_b_qkv': 'grad_w', 'grad_att_sinks': 'grad_w', 'grad_att_w_o': 'grad_w', 'grad_att_b_o': 'grad_w', 'grad_final_norm': 'grad_w', 'delta_norm_ffn1': 'delta_w', 'delta_ffn1_w_in': 'delta_w', 'delta_ffn1_w_out': 'delta_w', 'delta_norm_mix': 'delta_w', 'delta_norm_ffn2': 'delta_w', 'delta_ffn2_w_in': 'delta_w', 'delta_ffn2_w_out': 'delta_w', 'delta_ple_norm': 'delta_w', 'delta_ple_gate_w': 'delta_w', 'delta_ple_proj_w': 'delta_w', 'delta_hyb_w_in': 'delta_w', 'delta_conv_dw_w': 'delta_w', 'delta_conv_dw_b': 'delta_w', 'delta_conv_ln_g': 'delta_w', 'delta_conv_ln_b': 'delta_w', 'delta_ssm_conv_w': 'delta_w', 'delta_ssm_conv_b': 'delta_w', 'delta_ssm_dt_bias': 'delta_w', 'delta_ssm_a_log': 'delta_w', 'delta_ssm_d': 'delta_w', 'delta_ssm_norm': 'delta_w', 'delta_hyb_w_out': 'delta_w', 'delta_att_w_qkv': 'delta_w', 'delta_att_b_qkv': 'delta_w', 'delta_att_sinks': 'delta_w', 'delta_att_w_o': 'delta_w', 'delta_att_b_o': 'delta_w', 'delta_final_norm': 'delta_w', 'new_m_norm_ffn1': 'new_m', 'new_m_ffn1_w_in': 'new_m', 'new_m_ffn1_w_out': 'new_m', 'new_m_norm_mix': 'new_m', 'new_m_norm_ffn2': 'new_m', 'new_m_ffn2_w_in': 'new_m', 'new_m_ffn2_w_out': 'new_m', 'new_m_ple_norm': 'new_m', 'new_m_ple_gate_w': 'new_m', 'new_m_ple_proj_w': 'new_m', 'new_m_hyb_w_in': 'new_m', 'new_m_conv_dw_w': 'new_m', 'new_m_conv_dw_b': 'new_m', 'new_m_conv_ln_g': 'new_m', 'new_m_conv_ln_b': 'new_m', 'new_m_ssm_conv_w': 'new_m', 'new_m_ssm_conv_b': 'new_m', 'new_m_ssm_dt_bias': 'new_m', 'new_m_ssm_a_log': 'new_m', 'new_m_ssm_d': 'new_m', 'new_m_ssm_norm': 'new_m', 'new_m_hyb_w_out': 'new_m', 'new_m_att_w_qkv': 'new_m', 'new_m_att_b_qkv': 'new_m', 'new_m_att_sinks': 'new_m', 'new_m_att_w_o': 'new_m', 'new_m_att_b_o': 'new_m', 'new_m_final_norm': 'new_m', 'new_v_norm_ffn1': 'new_v', 'new_v_ffn1_w_in': 'new_v', 'new_v_ffn1_w_out': 'new_v', 'new_v_norm_mix': 'new_v', 'new_v_norm_ffn2': 'new_v', 'new_v_ffn2_w_in': 'new_v', 'new_v_ffn2_w_out': 'new_v', 'new_v_ple_norm': 'new_v', 'new_v_ple_gate_w': 'new_v', 'new_v_ple_proj_w': 'new_v', 'new_v_hyb_w_in': 'new_v', 'new_v_conv_dw_w': 'new_v', 'new_v_conv_dw_b': 'new_v', 'new_v_conv_ln_g': 'new_v', 'new_v_conv_ln_b': 'new_v', 'new_v_ssm_conv_w': 'new_v', 'new_v_ssm_conv_b': 'new_v', 'new_v_ssm_dt_bias': 'new_v', 'new_v_ssm_a_log': 'new_v', 'new_v_ssm_d': 'new_v', 'new_v_ssm_norm': 'new_v', 'new_v_hyb_w_out': 'new_v', 'new_v_att_w_qkv': 'new_v', 'new_v_att_b_qkv': 'new_v', 'new_v_att_sinks': 'new_v', 'new_v_att_w_o': 'new_v', 'new_v_att_b_o': 'new_v', 'new_v_final_norm': 'new_v'}


def _forward(args):
    return _fwd_reference(*[args[k] for k in FWD_PARAMS])


def _output_shape():
    out = _jax.eval_shape(lambda: _forward(_fwd_setup_inputs(0)))
    return out.shape, out.dtype

N_MICROBATCH = 1
ADAM_LR = 0.001
ADAM_B1 = 0.9
ADAM_B2 = 0.999
ADAM_EPS = 1e-08
ADAM_WD = 0.01
ADAM_STEP = 10
PER_EXAMPLE_BATCH_AXIS = {'x': 0, 'p': 1, 'loss_target': 0}
SHARED_INPUTS = []
_WEIGHT_DTYPES = {'norm_ffn1': _jnp.float32, 'ffn1_w_in': _jnp.float32, 'ffn1_w_out': _jnp.float32, 'norm_mix': _jnp.float32, 'norm_ffn2': _jnp.float32, 'ffn2_w_in': _jnp.float32, 'ffn2_w_out': _jnp.float32, 'ple_norm': _jnp.float32, 'ple_gate_w': _jnp.float32, 'ple_proj_w': _jnp.float32, 'hyb_w_in': _jnp.float32, 'conv_dw_w': _jnp.float32, 'conv_dw_b': _jnp.float32, 'conv_ln_g': _jnp.float32, 'conv_ln_b': _jnp.float32, 'ssm_conv_w': _jnp.float32, 'ssm_conv_b': _jnp.float32, 'ssm_dt_bias': _jnp.float32, 'ssm_a_log': _jnp.float32, 'ssm_d': _jnp.float32, 'ssm_norm': _jnp.float32, 'hyb_w_out': _jnp.float32, 'att_w_qkv': _jnp.float32, 'att_b_qkv': _jnp.float32, 'att_sinks': _jnp.float32, 'att_w_o': _jnp.float32, 'att_b_o': _jnp.float32, 'final_norm': _jnp.float32}
MOMENT_SCALE = {'norm_ffn1': 6.807211e-02, 'ffn1_w_in': 2.895128e-02, 'ffn1_w_out': 4.727759e-02, 'norm_mix': 1.122301e-01, 'norm_ffn2': 5.090342e-02, 'ffn2_w_in': 2.119882e-02, 'ffn2_w_out': 3.457534e-02, 'ple_norm': 2.462885e-02, 'ple_gate_w': 2.457980e-02, 'ple_proj_w': 6.214928e-02, 'hyb_w_in': 6.976733e-02, 'conv_dw_w': 5.646153e-02, 'conv_dw_b': 1.387470e-01, 'conv_ln_g': 7.053624e-02, 'conv_ln_b': 6.843645e-02, 'ssm_conv_w': 7.974142e-02, 'ssm_conv_b': 1.183781e-01, 'ssm_dt_bias': 2.705793e-01, 'ssm_a_log': 3.322848e-01, 'ssm_d': 3.776351e-01, 'ssm_norm': 9.328527e-02, 'hyb_w_out': 1.072230e-01, 'att_w_qkv': 3.273939e-02, 'att_b_qkv': 1.011332e-01, 'att_sinks': 1.220754e-02, 'att_w_o': 2.766379e-02, 'att_b_o': 1.302072e-01, 'final_norm': 3.201917e+01}


def _to_microbatches(a, axis):
    t = _jnp.moveaxis(a, axis, 0)
    t = t.reshape((N_MICROBATCH, t.shape[0] // N_MICROBATCH) + t.shape[1:])
    return _jnp.moveaxis(t, 1, axis + 1)


def setup_inputs(seed: int = 0) -> dict:
    inp = _fwd_setup_inputs(seed)
    key = _jax.random.fold_in(_jax.random.key(seed), 7919)
    shape, _ = _output_shape()
    out = dict(inp)
    out["loss_target"] = _jax.random.normal(_jax.random.fold_in(key, 0), shape, _jnp.float32)
    for i, name in enumerate(TWIN_WEIGHTS):
        w = inp[name].astype(_jnp.float32)
        if MOMENT_SCALE is None:
            s = _jnp.sqrt(_jnp.mean(_jnp.square(w)) + 1e-30)
        else:
            s = MOMENT_SCALE[name]
        km, kv = _jax.random.split(_jax.random.fold_in(key, i + 1))
        out[name] = w
        out["m_" + name] = s * _jax.random.normal(km, w.shape, _jnp.float32)
        out["v_" + name] = (s * s) * _jax.random.uniform(kv, w.shape, _jnp.float32, 0.5, 1.5)
    if N_MICROBATCH > 1:
        for name, axis in PER_EXAMPLE_BATCH_AXIS.items():
            out[name] = _to_microbatches(out[name], axis)
    return {'x': out['x'], 'p': out['p'], 'norm_ffn1': out['norm_ffn1'], 'ffn1_w_in': out['ffn1_w_in'], 'ffn1_w_out': out['ffn1_w_out'], 'norm_mix': out['norm_mix'], 'norm_ffn2': out['norm_ffn2'], 'ffn2_w_in': out['ffn2_w_in'], 'ffn2_w_out': out['ffn2_w_out'], 'ple_norm': out['ple_norm'], 'ple_gate_w': out['ple_gate_w'], 'ple_proj_w': out['ple_proj_w'], 'hyb_w_in': out['hyb_w_in'], 'conv_dw_w': out['conv_dw_w'], 'conv_dw_b': out['conv_dw_b'], 'conv_ln_g': out['conv_ln_g'], 'conv_ln_b': out['conv_ln_b'], 'ssm_conv_w': out['ssm_conv_w'], 'ssm_conv_b': out['ssm_conv_b'], 'ssm_dt_bias': out['ssm_dt_bias'], 'ssm_a_log': out['ssm_a_log'], 'ssm_d': out['ssm_d'], 'ssm_norm': out['ssm_norm'], 'hyb_w_out': out['hyb_w_out'], 'att_w_qkv': out['att_w_qkv'], 'att_b_qkv': out['att_b_qkv'], 'att_sinks': out['att_sinks'], 'att_w_o': out['att_w_o'], 'att_b_o': out['att_b_o'], 'final_norm': out['final_norm'], 'loss_target': out['loss_target'], 'm_norm_ffn1': out['m_norm_ffn1'], 'm_ffn1_w_in': out['m_ffn1_w_in'], 'm_ffn1_w_out': out['m_ffn1_w_out'], 'm_norm_mix': out['m_norm_mix'], 'm_norm_ffn2': out['m_norm_ffn2'], 'm_ffn2_w_in': out['m_ffn2_w_in'], 'm_ffn2_w_out': out['m_ffn2_w_out'], 'm_ple_norm': out['m_ple_norm'], 'm_ple_gate_w': out['m_ple_gate_w'], 'm_ple_proj_w': out['m_ple_proj_w'], 'm_hyb_w_in': out['m_hyb_w_in'], 'm_conv_dw_w': out['m_conv_dw_w'], 'm_conv_dw_b': out['m_conv_dw_b'], 'm_conv_ln_g': out['m_conv_ln_g'], 'm_conv_ln_b': out['m_conv_ln_b'], 'm_ssm_conv_w': out['m_ssm_conv_w'], 'm_ssm_conv_b': out['m_ssm_conv_b'], 'm_ssm_dt_bias': out['m_ssm_dt_bias'], 'm_ssm_a_log': out['m_ssm_a_log'], 'm_ssm_d': out['m_ssm_d'], 'm_ssm_norm': out['m_ssm_norm'], 'm_hyb_w_out': out['m_hyb_w_out'], 'm_att_w_qkv': out['m_att_w_qkv'], 'm_att_b_qkv': out['m_att_b_qkv'], 'm_att_sinks': out['m_att_sinks'], 'm_att_w_o': out['m_att_w_o'], 'm_att_b_o': out['m_att_b_o'], 'm_final_norm': out['m_final_norm'], 'v_norm_ffn1': out['v_norm_ffn1'], 'v_ffn1_w_in': out['v_ffn1_w_in'], 'v_ffn1_w_out': out['v_ffn1_w_out'], 'v_norm_mix': out['v_norm_mix'], 'v_norm_ffn2': out['v_norm_ffn2'], 'v_ffn2_w_in': out['v_ffn2_w_in'], 'v_ffn2_w_out': out['v_ffn2_w_out'], 'v_ple_norm': out['v_ple_norm'], 'v_ple_gate_w': out['v_ple_gate_w'], 'v_ple_proj_w': out['v_ple_proj_w'], 'v_hyb_w_in': out['v_hyb_w_in'], 'v_conv_dw_w': out['v_conv_dw_w'], 'v_conv_dw_b': out['v_conv_dw_b'], 'v_conv_ln_g': out['v_conv_ln_g'], 'v_conv_ln_b': out['v_conv_ln_b'], 'v_ssm_conv_w': out['v_ssm_conv_w'], 'v_ssm_conv_b': out['v_ssm_conv_b'], 'v_ssm_dt_bias': out['v_ssm_dt_bias'], 'v_ssm_a_log': out['v_ssm_a_log'], 'v_ssm_d': out['v_ssm_d'], 'v_ssm_norm': out['v_ssm_norm'], 'v_hyb_w_out': out['v_hyb_w_out'], 'v_att_w_qkv': out['v_att_w_qkv'], 'v_att_b_qkv': out['v_att_b_qkv'], 'v_att_sinks': out['v_att_sinks'], 'v_att_w_o': out['v_att_w_o'], 'v_att_b_o': out['v_att_b_o'], 'v_final_norm': out['v_final_norm']}


def _loss(weights, diff, rest, loss_target):
    with _jax.named_scope("forward"):
        args = {**rest, TWIN_DIFF_INPUT: diff, **{k: w.astype(_WEIGHT_DTYPES[k]) for k, w in weights.items()}}
        y = _forward(args)
    with _jax.named_scope("loss_head"):
        err = _jnp.square(y.astype(_jnp.float32) - loss_target)
        return 0.5 * _jnp.sum(_jnp.mean(err, axis=-1)) if err.ndim else 0.5 * err


def _adamw(w, g, m, v):
    m = ADAM_B1 * m + (1.0 - ADAM_B1) * g
    v = ADAM_B2 * v + (1.0 - ADAM_B2) * _jnp.square(g)
    m_hat = m / (1.0 - ADAM_B1 ** ADAM_STEP)
    v_hat = v / (1.0 - ADAM_B2 ** ADAM_STEP)
    delta = -ADAM_LR * (m_hat / (_jnp.sqrt(v_hat) + ADAM_EPS) + ADAM_WD * w)
    return delta, m, v


def reference(x, p, norm_ffn1, ffn1_w_in, ffn1_w_out, norm_mix, norm_ffn2, ffn2_w_in, ffn2_w_out, ple_norm, ple_gate_w, ple_proj_w, hyb_w_in, conv_dw_w, conv_dw_b, conv_ln_g, conv_ln_b, ssm_conv_w, ssm_conv_b, ssm_dt_bias, ssm_a_log, ssm_d, ssm_norm, hyb_w_out, att_w_qkv, att_b_qkv, att_sinks, att_w_o, att_b_o, final_norm, loss_target, m_norm_ffn1, m_ffn1_w_in, m_ffn1_w_out, m_norm_mix, m_norm_ffn2, m_ffn2_w_in, m_ffn2_w_out, m_ple_norm, m_ple_gate_w, m_ple_proj_w, m_hyb_w_in, m_conv_dw_w, m_conv_dw_b, m_conv_ln_g, m_conv_ln_b, m_ssm_conv_w, m_ssm_conv_b, m_ssm_dt_bias, m_ssm_a_log, m_ssm_d, m_ssm_norm, m_hyb_w_out, m_att_w_qkv, m_att_b_qkv, m_att_sinks, m_att_w_o, m_att_b_o, m_final_norm, v_norm_ffn1, v_ffn1_w_in, v_ffn1_w_out, v_norm_mix, v_norm_ffn2, v_ffn2_w_in, v_ffn2_w_out, v_ple_norm, v_ple_gate_w, v_ple_proj_w, v_hyb_w_in, v_conv_dw_w, v_conv_dw_b, v_conv_ln_g, v_conv_ln_b, v_ssm_conv_w, v_ssm_conv_b, v_ssm_dt_bias, v_ssm_a_log, v_ssm_d, v_ssm_norm, v_hyb_w_out, v_att_w_qkv, v_att_b_qkv, v_att_sinks, v_att_w_o, v_att_b_o, v_final_norm):
    given = dict(x=x, p=p, norm_ffn1=norm_ffn1, ffn1_w_in=ffn1_w_in, ffn1_w_out=ffn1_w_out, norm_mix=norm_mix, norm_ffn2=norm_ffn2, ffn2_w_in=ffn2_w_in, ffn2_w_out=ffn2_w_out, ple_norm=ple_norm, ple_gate_w=ple_gate_w, ple_proj_w=ple_proj_w, hyb_w_in=hyb_w_in, conv_dw_w=conv_dw_w, conv_dw_b=conv_dw_b, conv_ln_g=conv_ln_g, conv_ln_b=conv_ln_b, ssm_conv_w=ssm_conv_w, ssm_conv_b=ssm_conv_b, ssm_dt_bias=ssm_dt_bias, ssm_a_log=ssm_a_log, ssm_d=ssm_d, ssm_norm=ssm_norm, hyb_w_out=hyb_w_out, att_w_qkv=att_w_qkv, att_b_qkv=att_b_qkv, att_sinks=att_sinks, att_w_o=att_w_o, att_b_o=att_b_o, final_norm=final_norm, loss_target=loss_target, m_norm_ffn1=m_norm_ffn1, m_ffn1_w_in=m_ffn1_w_in, m_ffn1_w_out=m_ffn1_w_out, m_norm_mix=m_norm_mix, m_norm_ffn2=m_norm_ffn2, m_ffn2_w_in=m_ffn2_w_in, m_ffn2_w_out=m_ffn2_w_out, m_ple_norm=m_ple_norm, m_ple_gate_w=m_ple_gate_w, m_ple_proj_w=m_ple_proj_w, m_hyb_w_in=m_hyb_w_in, m_conv_dw_w=m_conv_dw_w, m_conv_dw_b=m_conv_dw_b, m_conv_ln_g=m_conv_ln_g, m_conv_ln_b=m_conv_ln_b, m_ssm_conv_w=m_ssm_conv_w, m_ssm_conv_b=m_ssm_conv_b, m_ssm_dt_bias=m_ssm_dt_bias, m_ssm_a_log=m_ssm_a_log, m_ssm_d=m_ssm_d, m_ssm_norm=m_ssm_norm, m_hyb_w_out=m_hyb_w_out, m_att_w_qkv=m_att_w_qkv, m_att_b_qkv=m_att_b_qkv, m_att_sinks=m_att_sinks, m_att_w_o=m_att_w_o, m_att_b_o=m_att_b_o, m_final_norm=m_final_norm, v_norm_ffn1=v_norm_ffn1, v_ffn1_w_in=v_ffn1_w_in, v_ffn1_w_out=v_ffn1_w_out, v_norm_mix=v_norm_mix, v_norm_ffn2=v_norm_ffn2, v_ffn2_w_in=v_ffn2_w_in, v_ffn2_w_out=v_ffn2_w_out, v_ple_norm=v_ple_norm, v_ple_gate_w=v_ple_gate_w, v_ple_proj_w=v_ple_proj_w, v_hyb_w_in=v_hyb_w_in, v_conv_dw_w=v_conv_dw_w, v_conv_dw_b=v_conv_dw_b, v_conv_ln_g=v_conv_ln_g, v_conv_ln_b=v_conv_ln_b, v_ssm_conv_w=v_ssm_conv_w, v_ssm_conv_b=v_ssm_conv_b, v_ssm_dt_bias=v_ssm_dt_bias, v_ssm_a_log=v_ssm_a_log, v_ssm_d=v_ssm_d, v_ssm_norm=v_ssm_norm, v_hyb_w_out=v_hyb_w_out, v_att_w_qkv=v_att_w_qkv, v_att_b_qkv=v_att_b_qkv, v_att_sinks=v_att_sinks, v_att_w_o=v_att_w_o, v_att_b_o=v_att_b_o, v_final_norm=v_final_norm)
    weights = {n: given[n] for n in TWIN_WEIGHTS}
    shared = {n: given[n] for n in SHARED_INPUTS}
    per_example = {n: given[n] for n in ['x', 'p']}
    grad_fn = _jax.value_and_grad(_loss, argnums=(0, 1))

    def one_microbatch(ex, loss_target):
        ex = dict(ex)
        diff = ex.pop(TWIN_DIFF_INPUT)
        return grad_fn(weights, diff, {**shared, **ex}, loss_target)

    if N_MICROBATCH == 1:
        loss, (grad_w, grad_x) = one_microbatch(per_example, given["loss_target"])
    else:
        def body(carry, xs):
            loss_sum, grad_sum = carry
            l_k, (gw_k, gx_k) = one_microbatch(xs[0], xs[1])
            with _jax.named_scope("update"):
                return (loss_sum + l_k, _jax.tree.map(_jnp.add, grad_sum, gw_k)), gx_k

        init = (_jnp.zeros((), _jnp.float32), _jax.tree.map(_jnp.zeros_like, weights))
        (loss, grad_w), grad_x = _jax.lax.scan(body, init, (per_example, given["loss_target"]))
    with _jax.named_scope("update"):
        delta_w, new_m, new_v = {}, {}, {}
        for n in TWIN_WEIGHTS:
            delta_w[n], new_m[n], new_v[n] = _adamw(weights[n], grad_w[n], given["m_" + n], given["v_" + n])
    return (loss, grad_x, *[grad_w[n] for n in TWIN_WEIGHTS], *[delta_w[n] for n in TWIN_WEIGHTS],
            *[new_m[n] for n in TWIN_WEIGHTS], *[new_v[n] for n in TWIN_WEIGHTS])
```

```python
import functools
import math

import numpy as np
import jax
import jax.numpy as jnp
from jax import lax
from jax.experimental import pallas as pl
from jax.experimental.pallas import tpu as pltpu

F32 = jnp.float32
BF16 = jnp.bfloat16
MXU_DTYPE = jnp.bfloat16
S = jax.ShapeDtypeStruct
MESH = pl.DeviceIdType.MESH

V7X_VMEM_BYTES = 64 * 2**20
VMEM_LIMIT = 48 * 2**20
LANE = 128

EPS = 1e-6
SSM_HEADS = 16
HEAD_DIM = 64
SSM_GROUPS = 2
SSM_STATE = 128
SSM_CONV = 4
CHUNK = 128
CONV_WIDTH = 31
ATT_HEADS = 16
ATT_KV_HEADS = 4
WINDOW = 128
ROPE_THETA = 10000.0
ADAM_LR = 0.001
ADAM_B1 = 0.9
ADAM_B2 = 0.999
ADAM_EPS = 1e-08
ADAM_WD = 0.01
ADAM_STEP = 10

N_CHIPS = 4
N_DEV = 8

NN = ((1,), (0,))
NT = ((1,), (1,))
TN = ((0,), (0,))


def _mm(a, b, dims=NN):
    return lax.dot_general(a.astype(MXU_DTYPE), b.astype(MXU_DTYPE), (dims, ((), ())), preferred_element_type=F32)


def _split3(a):
    hi = a.astype(BF16)
    r = a - hi.astype(F32)
    mid = r.astype(BF16)
    lo = (r - mid.astype(F32)).astype(BF16)
    return hi, mid, lo


def _mm01(a, onehot, dims=NN):
    o = onehot.astype(BF16)
    out = None
    for part in _split3(a):
        t = lax.dot_general(part, o, (dims, ((), ())), preferred_element_type=F32)
        out = t if out is None else out + t
    return out


def _01mm(onehot, a):
    o = onehot.astype(BF16)
    out = None
    for part in _split3(a):
        t = lax.dot_general(o, part, (NN, ((), ())), preferred_element_type=F32)
        out = t if out is None else out + t
    return out


def _sigmoid(x):
    return 1.0 / (1.0 + jnp.exp(-x))


def _softplus(x):
    return jnp.maximum(x, 0.0) + jnp.log(1.0 + jnp.exp(-jnp.abs(x)))


def _iota(shape, axis):
    return lax.broadcasted_iota(jnp.int32, shape, axis)


def _head_indicator(width, heads, transposed=False):
    per = width // heads
    if transposed:
        return (_iota((heads, width), 1) // per == _iota((heads, width), 0)).astype(F32)
    return (_iota((width, heads), 0) // per == _iota((width, heads), 1)).astype(F32)


def _acc(ref, i, val):
    @pl.when(i == 0)
    def _():
        ref[...] = val

    @pl.when(i > 0)
    def _():
        ref[...] += val


def _rs(tile, width, col=0, shift=0, n=None):
    if shift == 0:
        return pl.BlockSpec((tile, width), lambda i: (i, col))
    if shift < 0:
        return pl.BlockSpec((tile, width), lambda i: (jnp.maximum(i - 1, 0), col))
    return pl.BlockSpec((tile, width), lambda i: (jnp.minimum(i + 1, n - 1), col))


def _ps(shape):
    return pl.BlockSpec(shape, lambda i: (0,) * len(shape))


def _call(body, name, grid, in_specs, out_specs, out_shape, scratch=(), sem=None):
    return pl.pallas_call(
        body, name=name, grid=grid, in_specs=in_specs, out_specs=out_specs, out_shape=out_shape,
        scratch_shapes=list(scratch),
        compiler_params=pltpu.CompilerParams(dimension_semantics=sem, vmem_limit_bytes=VMEM_LIMIT))


def _row_tile(n, target):
    t = min(n, target)
    assert n % t == 0, (n, t)
    return t


def _pick_tile(dim, target):
    if dim <= target:
        return dim
    t = (target // LANE) * LANE
    while t >= LANE:
        if dim % t == 0:
            return t
        t -= LANE
    return dim


def matmul(a, b, mode, name, *, out_dtype=F32, scale=None, res=None, bias=None, tm=1024, tn=512, tk=1024):
    if mode == "nn":
        (m, k), (k2, n) = a.shape, b.shape
    elif mode == "nt":
        (m, k), (n, k2) = a.shape, b.shape
    else:
        (k, m), (k2, n) = a.shape, b.shape
    assert k == k2, (a.shape, b.shape, mode)
    tm, tn, tk = _pick_tile(m, tm), _pick_tile(n, tn), _pick_tile(k, tk)
    nk = k // tk
    dims = {"nn": NN, "nt": NT, "tn": TN}[mode]
    a_spec = (pl.BlockSpec((tk, tm), lambda i, j, kk: (kk, i)) if mode == "tn"
              else pl.BlockSpec((tm, tk), lambda i, j, kk: (i, kk)))
    b_spec = (pl.BlockSpec((tn, tk), lambda i, j, kk: (j, kk)) if mode == "nt"
              else pl.BlockSpec((tk, tn), lambda i, j, kk: (kk, j)))
    o_spec = pl.BlockSpec((tm, tn), lambda i, j, kk: (i, j))
    ins, in_specs = [a, b], [a_spec, b_spec]
    if bias is not None:
        ins.append(bias)
        in_specs.append(pl.BlockSpec((1, tn), lambda i, j, kk: (0, j)))
    if res is not None:
        ins.append(res)
        in_specs.append(o_spec)

    def body(*refs):
        a_ref, b_ref = refs[0], refs[1]
        o_ref, acc_ref = refs[-2], refs[-1]
        kk = pl.program_id(2)

        @pl.when(kk == 0)
        def _():
            acc_ref[...] = jnp.zeros_like(acc_ref)

        acc_ref[...] += _mm(a_ref[...], b_ref[...], dims)

        @pl.when(kk == nk - 1)
        def _():
            out = acc_ref[...]
            if scale is not None:
                out = out * scale
            pos = 2
            if bias is not None:
                out = out + refs[pos][...]
                pos += 1
            if res is not None:
                out = out + refs[pos][...]
            o_ref[...] = out.astype(o_ref.dtype)

    return _call(body, name, (m // tm, n // tn, nk), in_specs, o_spec, S((m, n), out_dtype),
                 scratch=[pltpu.VMEM((tm, tn), F32)], sem=("parallel", "parallel", "arbitrary"))(*ins)


def rms_fwd(h, g, name):
    n, d = h.shape
    tile = _row_tile(n, 512)

    def body(h_ref, g_ref, o_ref):
        x = h_ref[...]
        r = lax.rsqrt(jnp.mean(x * x, axis=-1, keepdims=True) + EPS)
        o_ref[...] = (x * r * g_ref[...]).astype(o_ref.dtype)

    return _call(body, name, (n // tile,), [_rs(tile, d), _ps((1, d))], _rs(tile, d), S((n, d), MXU_DTYPE),
                 sem=("parallel",))(h, g)


def _rms_bwd_math(x, g, dy):
    r = lax.rsqrt(jnp.mean(x * x, axis=-1, keepdims=True) + EPS)
    xh = x * r
    dg = jnp.sum(dy * xh, axis=0, keepdims=True)
    dxh = dy * g
    dx = r * (dxh - xh * jnp.mean(dxh * xh, axis=-1, keepdims=True))
    return dx, dg


def rms_bwd(h, g, dxn, dh_in, name, colsum=False):
    n, d = h.shape
    tile = _row_tile(n, 256)

    def body(h_ref, g_ref, dxn_ref, dh_ref, o_ref, dg_ref, *cs_ref):
        i = pl.program_id(0)
        dx, dg = _rms_bwd_math(h_ref[...], g_ref[...], dxn_ref[...].astype(F32))
        out = dh_ref[...] + dx
        o_ref[...] = out
        _acc(dg_ref, i, dg)
        if colsum:
            _acc(cs_ref[0], i, jnp.sum(out, axis=0, keepdims=True))

    outs = [S((n, d), F32), S((1, d), F32)] + ([S((1, d), F32)] if colsum else [])
    ospecs = [_rs(tile, d), _ps((1, d))] + ([_ps((1, d))] if colsum else [])
    return _call(body, name, (n // tile,), [_rs(tile, d), _ps((1, d)), _rs(tile, d), _rs(tile, d)], ospecs, outs,
                 sem=("arbitrary",))(h, g, dxn, dh_in)


def swiglu_fwd(u, name):
    n, f2 = u.shape
    f = f2 // 2
    tile = _row_tile(n, 256)

    def body(u_ref, o_ref):
        a = u_ref[:, :f]
        o_ref[...] = (a * _sigmoid(a) * u_ref[:, f:]).astype(o_ref.dtype)

    return _call(body, name, (n // tile,), [_rs(tile, f2)], _rs(tile, f), S((n, f), MXU_DTYPE), sem=("parallel",))(u)


def swiglu_bwd(u, dhm, name):
    n, f2 = u.shape
    f = f2 // 2
    tile = _row_tile(n, 256)

    def body(u_ref, d_ref, o_ref):
        a = u_ref[:, :f]
        b = u_ref[:, f:]
        d = d_ref[...]
        s = _sigmoid(a)
        o_ref[:, :f] = (d * b * s * (1.0 + a * (1.0 - s))).astype(o_ref.dtype)
        o_ref[:, f:] = (d * a * s).astype(o_ref.dtype)

    return _call(body, name, (n // tile,), [_rs(tile, f2), _rs(tile, f)], _rs(tile, f2), S((n, f2), MXU_DTYPE),
                 sem=("parallel",))(u, dhm)


def ple_fwd(h, gl, pp, name):
    n, d = h.shape
    tile = _row_tile(n, 512)

    def body(h_ref, gl_ref, pp_ref, o_ref):
        o_ref[...] = h_ref[...] + _sigmoid(gl_ref[...]) * pp_ref[...]

    return _call(body, name, (n // tile,), [_rs(tile, d)] * 3, _rs(tile, d), S((n, d), F32), sem=("parallel",))(h, gl, pp)


def ple_bwd(dh, gl, pp, name):
    n, d = dh.shape
    tile = _row_tile(n, 512)

    def body(dh_ref, gl_ref, pp_ref, dpp_ref, dgl_ref):
        g = _sigmoid(gl_ref[...])
        dh_ = dh_ref[...]
        dpp_ref[...] = (dh_ * g).astype(dpp_ref.dtype)
        dgl_ref[...] = (dh_ * pp_ref[...] * g * (1.0 - g)).astype(dgl_ref.dtype)

    return _call(body, name, (n // tile,), [_rs(tile, d)] * 3, [_rs(tile, d)] * 2, [S((n, d), MXU_DTYPE)] * 2,
                 sem=("parallel",))(dh, gl, pp)


def loss_head(h, g, target, name):
    n, d = h.shape
    tile = _row_tile(n, 256)

    def body(h_ref, g_ref, t_ref, dh_ref, dg_ref, loss_ref):
        i = pl.program_id(0)
        x = h_ref[...]
        gg = g_ref[...]
        r = lax.rsqrt(jnp.mean(x * x, axis=-1, keepdims=True) + EPS)
        err = x * r * gg - t_ref[...]
        part = 0.5 * jnp.sum(jnp.mean(err * err, axis=-1, keepdims=True), axis=0, keepdims=True)
        dx, dg = _rms_bwd_math(x, gg, err * (1.0 / d))
        dh_ref[...] = dx
        _acc(dg_ref, i, dg)
        _acc(loss_ref, i, jnp.broadcast_to(part, (8, LANE)))

    return _call(body, name, (n // tile,), [_rs(tile, d), _ps((1, d)), _rs(tile, d)],
                 [_rs(tile, d), _ps((1, d)), _ps((8, LANE))], [S((n, d), F32), S((1, d), F32), S((8, LANE), F32)],
                 sem=("arbitrary",))(h, g, target)


def adamw(w, g, m, v, name):
    r, c = w.shape
    tile = r
    for t in (512, 256, 128, 64, 32, 16, 8):
        if r % t == 0 and t * c * 4 <= 2**21:
            tile = t
            break
    c1 = np.float32(1.0 - ADAM_B1 ** ADAM_STEP)
    c2 = np.float32(1.0 - ADAM_B2 ** ADAM_STEP)

    def body(w_ref, g_ref, m_ref, v_ref, d_ref, mo_ref, vo_ref):
        gg = g_ref[...]
        mm = ADAM_B1 * m_ref[...] + (1.0 - ADAM_B1) * gg
        vv = ADAM_B2 * v_ref[...] + (1.0 - ADAM_B2) * (gg * gg)
        mo_ref[...] = mm
        vo_ref[...] = vv
        d_ref[...] = -ADAM_LR * ((mm / c1) / (jnp.sqrt(vv / c2) + ADAM_EPS) + ADAM_WD * w_ref[...])

    return _call(body, name, (r // tile,), [_rs(tile, c)] * 4, [_rs(tile, c)] * 3, [S((r, c), F32)] * 3,
                 sem=("parallel",))(w, g, m, v)


def _taps_fwd(sc, w_ref, width, halo, tile, acc):
    for k in range(width):
        o = halo - (width - 1) + k
        acc = acc + w_ref[k:k + 1, :] * sc[o:o + tile, :]
    return acc


def _taps_bwd_x(sc_d, w_ref, width, tile, acc):
    for k in range(width):
        o = (width - 1) - k
        acc = acc + w_ref[k:k + 1, :] * sc_d[o:o + tile, :]
    return acc


def _taps_bwd_w(dy, sc, dw_ref, width, halo, tile, i):
    @pl.when(i == 0)
    def _():
        dw_ref[...] = jnp.zeros_like(dw_ref)

    for k in range(width):
        o = halo - (width - 1) + k
        dw_ref[k:k + 1, :] += jnp.sum(dy * sc[o:o + tile, :], axis=0, keepdims=True)


def _ln_stats(x):
    mu = jnp.mean(x, axis=-1, keepdims=True)
    xc = x - mu
    r = lax.rsqrt(jnp.mean(xc * xc, axis=-1, keepdims=True) + EPS)
    return xc * r, r


def conv_group_fwd(proj, cw, cb, lg, lb, name):
    n = proj.shape[0]
    d = cw.shape[1]
    tile = _row_tile(n, 256)
    halo = 32

    def body(v_ref, g_ref, vp_ref, gp_ref, cw_ref, cb_ref, lg_ref, lb_ref, u_ref, u1_ref, sc):
        i = pl.program_id(0)
        first = (i > 0).astype(F32)
        sc[0:halo, :] = vp_ref[tile - halo:, :] * _sigmoid(gp_ref[tile - halo:, :]) * first
        sc[halo:, :] = v_ref[...] * _sigmoid(g_ref[...])
        u1 = _taps_fwd(sc, cw_ref, CONV_WIDTH, halo, tile, jnp.zeros((tile, d), F32) + cb_ref[...])
        u1_ref[...] = u1
        xh, _ = _ln_stats(u1)
        y = xh * lg_ref[...] + lb_ref[...]
        u_ref[...] = (y * _sigmoid(y)).astype(u_ref.dtype)

    return _call(body, name, (n // tile,),
                 [_rs(tile, d, 0), _rs(tile, d, 1), _rs(tile, d, 0, -1), _rs(tile, d, 1, -1),
                  _ps(cw.shape), _ps((1, d)), _ps((1, d)), _ps((1, d))],
                 [_rs(tile, d), _rs(tile, d)], [S((n, d), MXU_DTYPE), S((n, d), F32)],
                 scratch=[pltpu.VMEM((halo + tile, d), F32)], sem=("arbitrary",))(proj, proj, proj, proj, cw, cb, lg, lb)


def conv_group_bwd(du, u1, proj, cw, lg, lb, name):
    n = proj.shape[0]
    d = cw.shape[1]
    tile = _row_tile(n, 256)
    halo = 32
    nt = n // tile

    def body(du_ref, dun_ref, u1_ref, u1n_ref, v_ref, g_ref, vp_ref, gp_ref, cw_ref, lg_ref, lb_ref,
             dv_ref, dg_ref, dcw_ref, dcb_ref, dlg_ref, dlb_ref, sc, sc_d):
        i = pl.program_id(0)

        def ln_swish_bwd(dy_, u1_):
            xh, r = _ln_stats(u1_)
            y = xh * lg_ref[...] + lb_ref[...]
            s = _sigmoid(y)
            dyy = dy_ * s * (1.0 + y * (1.0 - s))
            dxh = dyy * lg_ref[...]
            dx = r * (dxh - jnp.mean(dxh, axis=-1, keepdims=True) - xh * jnp.mean(dxh * xh, axis=-1, keepdims=True))
            return dx, jnp.sum(dyy * xh, axis=0, keepdims=True), jnp.sum(dyy, axis=0, keepdims=True)

        du1, dlg, dlb = ln_swish_bwd(du_ref[...].astype(F32), u1_ref[...])
        du1n, _, _ = ln_swish_bwd(dun_ref[0:halo, :].astype(F32), u1n_ref[0:halo, :])
        sc_d[0:tile, :] = du1
        sc_d[tile:, :] = du1n * (i < nt - 1).astype(F32)
        sig = _sigmoid(g_ref[...])
        val = v_ref[...]
        sc[0:halo, :] = vp_ref[tile - halo:, :] * _sigmoid(gp_ref[tile - halo:, :]) * (i > 0).astype(F32)
        sc[halo:, :] = val * sig
        du0 = _taps_bwd_x(sc_d, cw_ref, CONV_WIDTH, tile, jnp.zeros((tile, d), F32))
        _taps_bwd_w(du1, sc, dcw_ref, CONV_WIDTH, halo, tile, i)
        _acc(dcb_ref, i, jnp.sum(du1, axis=0, keepdims=True))
        _acc(dlg_ref, i, dlg)
        _acc(dlb_ref, i, dlb)
        dv_ref[...] = (du0 * sig).astype(dv_ref.dtype)
        dg_ref[...] = (du0 * val * sig * (1.0 - sig)).astype(dg_ref.dtype)

    return _call(body, name, (nt,),
                 [_rs(tile, d), _rs(tile, d, 0, 1, nt), _rs(tile, d), _rs(tile, d, 0, 1, nt),
                  _rs(tile, d, 0), _rs(tile, d, 1), _rs(tile, d, 0, -1), _rs(tile, d, 1, -1),
                  _ps(cw.shape), _ps((1, d)), _ps((1, d))],
                 [_rs(tile, d), _rs(tile, d), _ps(cw.shape), _ps((1, d)), _ps((1, d)), _ps((1, d))],
                 [S((n, d), MXU_DTYPE), S((n, d), MXU_DTYPE), S(cw.shape, F32), S((1, d), F32), S((1, d), F32), S((1, d), F32)],
                 scratch=[pltpu.VMEM((halo + tile, d), F32), pltpu.VMEM((tile + halo, d), F32)],
                 sem=("arbitrary",))(du, du, u1, u1, proj, proj, proj, proj, cw, lg, lb)


def ssm_conv_fwd(proj, dtr, sw, sb, dtb, name):
    n = proj.shape[0]
    w = sw.shape[1]
    inner = SSM_HEADS * HEAD_DIM
    tile = _row_tile(n, 256)
    halo = 8

    def body(x_ref, xp_ref, dtr_ref, sw_ref, sb_ref, dtb_ref, pre_ref, xs_ref, bc_ref, dt_ref, sc):
        i = pl.program_id(0)
        sc[0:halo, :] = xp_ref[tile - halo:, :] * (i > 0).astype(F32)
        sc[halo:, :] = x_ref[...]
        pre = _taps_fwd(sc, sw_ref, SSM_CONV, halo, tile, jnp.zeros((tile, w), F32) + sb_ref[...])
        pre_ref[...] = pre
        act = pre * _sigmoid(pre)
        xs_ref[...] = act[:, :inner]
        bc_ref[...] = act[:, inner:]
        dt = _softplus(dtr_ref[...] + dtb_ref[...])
        dt_ref[...] = jnp.where(_iota(dt.shape, 1) < SSM_HEADS, dt, 0.0)

    return _call(body, name, (n // tile,),
                 [_rs(tile, w, 2), _rs(tile, w, 2, -1), _rs(tile, LANE), _ps(sw.shape), _ps((1, w)), _ps((1, LANE))],
                 [_rs(tile, w), _rs(tile, inner), _rs(tile, w - inner), _rs(tile, LANE)],
                 [S((n, w), F32), S((n, inner), F32), S((n, w - inner), F32), S((n, LANE), F32)],
                 scratch=[pltpu.VMEM((halo + tile, w), F32)], sem=("arbitrary",))(proj, proj, dtr, sw, sb, dtb)


def ssm_conv_bwd(dxs, dbc, pre, proj, sw, name):
    n = proj.shape[0]
    w = sw.shape[1]
    inner = SSM_HEADS * HEAD_DIM
    tile = _row_tile(n, 256)
    halo = 8
    nt = n // tile

    def body(dxs_ref, dxsn_ref, dbc_ref, dbcn_ref, pre_ref, pren_ref, x_ref, xp_ref, sw_ref,
             dx_ref, dsw_ref, dsb_ref, sc, sc_d):
        i = pl.program_id(0)

        def silu_bwd(d_, p_):
            s = _sigmoid(p_)
            return d_ * s * (1.0 + p_ * (1.0 - s))

        sc_d[0:tile, :inner] = silu_bwd(dxs_ref[...], pre_ref[:, :inner])
        sc_d[0:tile, inner:] = silu_bwd(dbc_ref[...], pre_ref[:, inner:])
        last = (i < nt - 1).astype(F32)
        sc_d[tile:, :inner] = silu_bwd(dxsn_ref[0:halo, :], pren_ref[0:halo, :inner]) * last
        sc_d[tile:, inner:] = silu_bwd(dbcn_ref[0:halo, :], pren_ref[0:halo, inner:]) * last
        sc[0:halo, :] = xp_ref[tile - halo:, :] * (i > 0).astype(F32)
        sc[halo:, :] = x_ref[...]
        dpre = sc_d[0:tile, :]
        dx_ref[...] = _taps_bwd_x(sc_d, sw_ref, SSM_CONV, tile, jnp.zeros((tile, w), F32)).astype(dx_ref.dtype)
        _taps_bwd_w(dpre, sc, dsw_ref, SSM_CONV, halo, tile, i)
        _acc(dsb_ref, i, jnp.sum(dpre, axis=0, keepdims=True))

    return _call(body, name, (nt,),
                 [_rs(tile, inner), _rs(tile, inner, 0, 1, nt), _rs(tile, w - inner), _rs(tile, w - inner, 0, 1, nt),
                  _rs(tile, w), _rs(tile, w, 0, 1, nt), _rs(tile, w, 2), _rs(tile, w, 2, -1), _ps(sw.shape)],
                 [_rs(tile, w), _ps(sw.shape), _ps((1, w))],
                 [S((n, w), MXU_DTYPE), S(sw.shape, F32), S((1, w), F32)],
                 scratch=[pltpu.VMEM((halo + tile, w), F32), pltpu.VMEM((tile + halo, w), F32)],
                 sem=("arbitrary",))(dxs, dxs, dbc, dbc, pre, pre, proj, proj, sw)


def _ssd_prologue(dt_ref, dtT_ref, al_ref, alc_ref):
    row = _iota((CHUNK, CHUNK), 0)
    col = _iota((CHUNK, CHUNK), 1)
    dt = dt_ref[:, :SSM_HEADS]
    a_row = -jnp.exp(al_ref[:, :SSM_HEADS])
    a_col = -jnp.exp(alc_ref[...])
    cs = _01mm((row >= col).astype(F32), dt * a_row)
    csT = _mm01(dtT_ref[...] * a_col, (row <= col).astype(F32))
    return dt, a_row, cs, csT, row, col


def _decay(cs, csT, h, row, col):
    lm = jnp.exp(jnp.where(row >= col, cs[:, h:h + 1] - csT[h:h + 1, :], -1e30))
    lmT = jnp.exp(jnp.where(col >= row, csT[h:h + 1, :] - cs[:, h:h + 1], -1e30))
    return lm, lmT


def ssd_fwd(xs, bc, dt, dtT, alog_row, alog_col, name):
    n, width = xs.shape
    nc = n // CHUNK
    gw = width // SSM_GROUPS
    hpg = SSM_HEADS // SSM_GROUPS
    ns = SSM_STATE

    def body(xs_ref, bc_ref, dt_ref, dtT_ref, al_ref, alc_ref, y_ref, hs_ref, h_sc):
        i = pl.program_id(0)

        @pl.when(i == 0)
        def _():
            h_sc[...] = jnp.zeros_like(h_sc)

        dt, a_row, cs, csT, row, col = _ssd_prologue(dt_ref, dtT_ref, al_ref, alc_ref)
        indT = _head_indicator(width, SSM_HEADS, transposed=True)
        dt_full = _mm01(dt, indT)
        e_full = jnp.exp(_mm01(cs, indT))
        dte_full = jnp.exp(_mm01(cs[CHUNK - 1:CHUNK, :] - cs, indT))
        xt = xs_ref[...] * dt_full
        hs_ref[0] = h_sc[...]
        lo = _iota((CHUNK, 2 * HEAD_DIM), 1) < HEAD_DIM
        for g in range(SSM_GROUPS):
            bg = bc_ref[:, g * ns:(g + 1) * ns]
            cg = bc_ref[:, (SSM_GROUPS + g) * ns:(SSM_GROUPS + g + 1) * ns]
            gm = _mm(cg, bg, NT)
            hg = h_sc[g * gw:(g + 1) * gw, :]
            yoff = e_full[:, g * gw:(g + 1) * gw] * _mm(cg, hg, NT)
            for pr in range(hpg // 2):
                h0 = g * hpg + 2 * pr
                c0 = h0 * HEAD_DIM
                xp = xt[:, c0:c0 + 2 * HEAD_DIM]
                m0 = gm * _decay(cs, csT, h0, row, col)[0]
                m1 = gm * _decay(cs, csT, h0 + 1, row, col)[0]
                yd = jnp.where(lo, _mm(m0, xp), _mm(m1, xp))
                y_ref[:, c0:c0 + 2 * HEAD_DIM] = yd + yoff[:, 2 * pr * HEAD_DIM:(2 * pr + 2) * HEAD_DIM]
            sg = _mm(xt[:, g * gw:(g + 1) * gw] * dte_full[:, g * gw:(g + 1) * gw], bg, TN)
            for hh in range(hpg):
                h = g * hpg + hh
                r0 = h * HEAD_DIM
                h_sc[r0:r0 + HEAD_DIM, :] = (h_sc[r0:r0 + HEAD_DIM, :] * jnp.exp(csT[h:h + 1, CHUNK - 1:CHUNK])
                                             + sg[hh * HEAD_DIM:(hh + 1) * HEAD_DIM, :])

    bcw = bc.shape[1]
    return _call(body, name, (nc,),
                 [_rs(CHUNK, width), _rs(CHUNK, bcw), _rs(CHUNK, LANE), pl.BlockSpec((SSM_HEADS, CHUNK), lambda i: (0, i)),
                  _ps((1, LANE)), _ps((SSM_HEADS, 1))],
                 [_rs(CHUNK, width), pl.BlockSpec((1, width, ns), lambda i: (i, 0, 0))],
                 [S((n, width), F32), S((nc, width, ns), F32)],
                 scratch=[pltpu.VMEM((width, ns), F32)], sem=("arbitrary",))(xs, bc, dt, dtT, alog_row, alog_col)


def ssd_bwd(xs, bc, dt, dtT, alog_row, alog_col, hs, dy, dxs_skip, name):
    n, width = xs.shape
    nc = n // CHUNK
    gw = width // SSM_GROUPS
    hpg = SSM_HEADS // SSM_GROUPS
    ns = SSM_STATE
    bcw = bc.shape[1]

    def body(xs_ref, bc_ref, dt_ref, dtT_ref, al_ref, alc_ref, hs_ref, dy_ref, skip_ref,
             dxs_ref, dbc_ref, ddtr_ref, dal_ref, ddtb_ref, dh_sc, dxt_sc):
        i = pl.program_id(0)

        @pl.when(i == 0)
        def _():
            dh_sc[...] = jnp.zeros_like(dh_sc)

        dt, a_row, cs, csT, row, col = _ssd_prologue(dt_ref, dtT_ref, al_ref, alc_ref)
        indT = _head_indicator(width, SSM_HEADS, transposed=True)
        ind = _head_indicator(width, SSM_HEADS)
        dt_full = _mm01(dt, indT)
        e_full = jnp.exp(_mm01(cs, indT))
        cs_last = cs[CHUNK - 1:CHUNK, :]
        dte = jnp.exp(cs_last - cs)
        dte_full = _mm01(dte, indT)
        xs_ = xs_ref[...]
        xt = xs_ * dt_full
        dy_ = dy_ref[...]
        hmat = hs_ref[0]
        ds = dh_sc[...]
        lo = _iota((CHUNK, 2 * HEAD_DIM), 1) < HEAD_DIM
        head_lane = _iota((1, SSM_HEADS), 1)
        dcs = jnp.zeros((CHUNK, SSM_HEADS), F32)
        ddte = jnp.zeros((CHUNK, SSM_HEADS), F32)
        for g in range(SSM_GROUPS):
            sl = slice(g * gw, (g + 1) * gw)
            bg = bc_ref[:, g * ns:(g + 1) * ns]
            cg = bc_ref[:, (SSM_GROUPS + g) * ns:(SSM_GROUPS + g + 1) * ns]
            gm = _mm(cg, bg, NT)
            gmT = _mm(bg, cg, NT)
            hg = hmat[sl, :]
            dsg = ds[sl, :]
            dyg = dy_[:, sl]
            xtg = xt[:, sl]
            yoff = e_full[:, sl] * _mm(cg, hg, NT)
            edy = e_full[:, sl] * dyg
            bds = _mm(bg, dsg, NT)
            dxt_g = dte_full[:, sl] * bds
            ddte = ddte + _mm01(xtg * bds, ind[sl, :])
            dcs = dcs + _mm01(dyg * yoff, ind[sl, :])
            db = _mm(xtg * dte_full[:, sl], dsg)
            dc = _mm(edy, hg)
            dhc = _mm(edy, cg, TN)
            dgs = jnp.zeros((CHUNK, CHUNK), F32)
            dgTs = jnp.zeros((CHUNK, CHUNK), F32)
            for pr in range(hpg // 2):
                h0 = g * hpg + 2 * pr
                c0 = 2 * pr * HEAD_DIM
                xp = xtg[:, c0:c0 + 2 * HEAD_DIM]
                dyp = dyg[:, c0:c0 + 2 * HEAD_DIM]
                rr = []
                for h, half in ((h0, lo), (h0 + 1, jnp.logical_not(lo))):
                    lm, lmT = _decay(cs, csT, h, row, col)
                    xm = jnp.where(half, xp, 0.0)
                    dm = _mm(dyp, xm, NT)
                    dmT = _mm(xm, dyp, NT)
                    mT = gmT * lmT
                    z = jnp.sum(dm * (gm * lm), axis=1, keepdims=True) - jnp.sum(dmT * mT, axis=1, keepdims=True)
                    dcs = dcs + z * (head_lane == h).astype(F32)
                    dgs = dgs + dm * lm
                    dgTs = dgTs + dmT * lmT
                    rr.append(_mm(mT, dyp))
                dxt_sc[:, g * gw + c0:g * gw + c0 + 2 * HEAD_DIM] = jnp.where(lo, rr[0], rr[1]) + dxt_g[:, c0:c0 + 2 * HEAD_DIM]
            dbc_ref[:, g * ns:(g + 1) * ns] = db + _mm(dgTs, cg)
            dbc_ref[:, (SSM_GROUPS + g) * ns:(SSM_GROUPS + g + 1) * ns] = dc + _mm(dgs, bg)
            for hh in range(hpg):
                h = g * hpg + hh
                r0 = h * HEAD_DIM
                dh_sc[r0:r0 + HEAD_DIM, :] = (dhc[hh * HEAD_DIM:(hh + 1) * HEAD_DIM, :]
                                              + jnp.exp(csT[h:h + 1, CHUNK - 1:CHUNK]) * ds[r0:r0 + HEAD_DIM, :])
        t = ddte * dte
        per_head = jnp.sum(jnp.sum(ds * hmat, axis=1, keepdims=True) * ind, axis=0, keepdims=True)
        last_add = jnp.sum(t, axis=0, keepdims=True) + jnp.exp(cs_last) * per_head
        dcs = dcs - t + jnp.where(_iota((CHUNK, SSM_HEADS), 0) == CHUNK - 1, last_add, 0.0)
        dadt = _01mm((row <= col).astype(F32), dcs)
        dxt = dxt_sc[...]
        ddt = dadt * a_row + _mm01(dxt * xs_, ind)
        dxs_ref[...] = dxt * dt_full + skip_ref[...]
        ddtr = ddt * (1.0 - jnp.exp(-dt))
        ddtr_ref[...] = jnp.zeros_like(ddtr_ref)
        ddtr_ref[:, :SSM_HEADS] = ddtr.astype(ddtr_ref.dtype)
        _acc(dal_ref, i, jnp.sum(dadt * dt, axis=0, keepdims=True) * a_row)
        _acc(ddtb_ref, i, jnp.sum(ddtr, axis=0, keepdims=True))

    rev = lambda i: (nc - 1 - i, 0)
    return _call(body, name, (nc,),
                 [pl.BlockSpec((CHUNK, width), rev), pl.BlockSpec((CHUNK, bcw), rev), pl.BlockSpec((CHUNK, LANE), rev),
                  pl.BlockSpec((SSM_HEADS, CHUNK), lambda i: (0, nc - 1 - i)), _ps((1, LANE)), _ps((SSM_HEADS, 1)),
                  pl.BlockSpec((1, width, ns), lambda i: (nc - 1 - i, 0, 0)), pl.BlockSpec((CHUNK, width), rev),
                  pl.BlockSpec((CHUNK, width), rev)],
                 [pl.BlockSpec((CHUNK, width), rev), pl.BlockSpec((CHUNK, bcw), rev), pl.BlockSpec((CHUNK, LANE), rev),
                  _ps((1, SSM_HEADS)), _ps((1, SSM_HEADS))],
                 [S((n, width), F32), S((n, bcw), F32), S((n, LANE), MXU_DTYPE), S((1, SSM_HEADS), F32), S((1, SSM_HEADS), F32)],
                 scratch=[pltpu.VMEM((width, ns), F32), pltpu.VMEM((CHUNK, width), F32)],
                 sem=("arbitrary",))(xs, bc, dt, dtT, alog_row, alog_col, hs, dy, dxs_skip)


def ssm_gate_fwd(yssd, xs, proj, dfull, gamma, name):
    n, d = yssd.shape
    tile = _row_tile(n, 256)
    gw = d // SSM_GROUPS

    def body(y_ref, xs_ref, z_ref, df_ref, gm_ref, o_ref):
        z = z_ref[...]
        y2 = (y_ref[...] + df_ref[...] * xs_ref[...]) * (z * _sigmoid(z))
        for g in range(SSM_GROUPS):
            yg = y2[:, g * gw:(g + 1) * gw]
            r = lax.rsqrt(jnp.mean(yg * yg, axis=-1, keepdims=True) + EPS)
            o_ref[:, g * gw:(g + 1) * gw] = (yg * r * gm_ref[:, g * gw:(g + 1) * gw]).astype(o_ref.dtype)

    return _call(body, name, (n // tile,), [_rs(tile, d), _rs(tile, d), _rs(tile, d, 2), _ps((1, d)), _ps((1, d))],
                 _rs(tile, d), S((n, d), MXU_DTYPE), sem=("parallel",))(yssd, xs, proj, dfull, gamma)


def ssm_gate_bwd(dy3, yssd, xs, proj, dfull, gamma, name):
    n, d = yssd.shape
    tile = _row_tile(n, 256)
    gw = d // SSM_GROUPS

    def body(dy_ref, y_ref, xs_ref, z_ref, df_ref, gm_ref, dys_ref, dxs_ref, dz_ref, dgm_ref, dd_ref):
        i = pl.program_id(0)
        z = z_ref[...]
        s = _sigmoid(z)
        xs_ = xs_ref[...]
        y1 = y_ref[...] + df_ref[...] * xs_
        y2 = y1 * (z * s)
        dy_ = dy_ref[...].astype(F32)
        dgm = []
        dy2 = []
        for g in range(SSM_GROUPS):
            sl = slice(g * gw, (g + 1) * gw)
            dxg, dgg = _rms_bwd_math(y2[:, sl], gm_ref[:, sl], dy_[:, sl])
            dy2.append(dxg)
            dgm.append(dgg)
        dy2 = jnp.concatenate(dy2, axis=1)
        dy1 = dy2 * (z * s)
        dys_ref[...] = dy1
        dxs_ref[...] = dy1 * df_ref[...]
        dz_ref[...] = (dy2 * y1 * s * (1.0 + z * (1.0 - s))).astype(dz_ref.dtype)
        _acc(dgm_ref, i, jnp.concatenate(dgm, axis=1))
        colsum = jnp.broadcast_to(jnp.sum(dy1 * xs_, axis=0, keepdims=True), (8, d))
        _acc(dd_ref, i, _mm01(colsum, _head_indicator(d, SSM_HEADS))[0:1, :])

    return _call(body, name, (n // tile,),
                 [_rs(tile, d), _rs(tile, d), _rs(tile, d), _rs(tile, d, 2), _ps((1, d)), _ps((1, d))],
                 [_rs(tile, d), _rs(tile, d), _rs(tile, d), _ps((1, d)), _ps((1, SSM_HEADS))],
                 [S((n, d), F32), S((n, d), F32), S((n, d), MXU_DTYPE), S((1, d), F32), S((1, SSM_HEADS), F32)],
                 sem=("arbitrary",))(dy3, yssd, xs, proj, dfull, gamma)


def _rope128(x, cos, sin_signed):
    half = HEAD_DIM // 2
    lane = _iota(x.shape, 1)
    partner = jnp.where((lane % HEAD_DIM) < half, pltpu.roll(x, LANE - half, 1), pltpu.roll(x, half, 1))
    return x * cos + partner * sin_signed


def rope_fwd(qkv, cos, sin, name):
    n, w = qkv.shape
    qw = ATT_HEADS * HEAD_DIM
    kw = ATT_KV_HEADS * HEAD_DIM
    tile = _row_tile(n, 256)

    def body(x_ref, c_ref, s_ref, q_ref, k_ref, v_ref):
        c, s = c_ref[...], s_ref[...]
        for j in range(qw // LANE):
            q_ref[:, j * LANE:(j + 1) * LANE] = _rope128(x_ref[:, j * LANE:(j + 1) * LANE], c, s).astype(q_ref.dtype)
        for j in range(kw // LANE):
            k_ref[:, j * LANE:(j + 1) * LANE] = _rope128(x_ref[:, qw + j * LANE:qw + (j + 1) * LANE], c, s).astype(k_ref.dtype)
        v_ref[...] = x_ref[:, qw + kw:].astype(v_ref.dtype)

    return _call(body, name, (n // tile,), [_rs(tile, w), _rs(tile, LANE), _rs(tile, LANE)],
                 [_rs(tile, qw), _rs(tile, kw), _rs(tile, kw)],
                 [S((n, qw), MXU_DTYPE), S((n, kw), MXU_DTYPE), S((n, kw), MXU_DTYPE)], sem=("parallel",))(qkv, cos, sin)


def _attn_mask(i):
    row = _iota((WINDOW, 2 * WINDOW), 0)
    s = _iota((WINDOW, 2 * WINDOW), 1)
    return (s > row) & (s <= row + WINDOW) & ((s >= WINDOW) | (i > 0))


def _attn_heads():
    grp = ATT_HEADS // ATT_KV_HEADS
    return [(h, h // 2, h % 2, (h // grp) // 2, (h // grp) % 2) for h in range(ATT_HEADS)]


def attn_fwd(q, k, v, sinks, name):
    n, qw = q.shape
    kw = k.shape[1]
    nb = n // WINDOW
    scale = HEAD_DIM ** -0.5

    def body(q_ref, kc_ref, kp_ref, vc_ref, vp_ref, sk_ref, o_ref, lse_ref):
        i = pl.program_id(0)
        valid = _attn_mask(i)
        lo = _iota((WINDOW, LANE), 1) < HEAD_DIM
        k2 = jnp.concatenate([kp_ref[...], kc_ref[...]], axis=0)
        v2 = jnp.concatenate([vp_ref[...], vc_ref[...]], axis=0)
        lane1 = _iota((1, LANE), 1)
        lse = jnp.zeros((WINDOW, LANE), F32)
        res = {}
        for h, qb, qh, kb, kh in _attn_heads():
            qp = q_ref[:, qb * LANE:(qb + 1) * LANE]
            qm = jnp.where(lo if qh == 0 else jnp.logical_not(lo), qp, jnp.zeros_like(qp))
            kk = k2[:, kb * LANE:(kb + 1) * LANE]
            if kh != qh:
                kk = pltpu.roll(kk, HEAD_DIM, 1)
            logits = jnp.where(valid, _mm(qm, kk, NT) * scale, -1e30)
            sk = sk_ref[:, h:h + 1]
            m = jnp.maximum(jnp.max(logits, axis=-1, keepdims=True), sk)
            e = jnp.exp(logits - m)
            den = jnp.sum(e, axis=-1, keepdims=True) + jnp.exp(sk - m)
            lse = lse + (m + jnp.log(den)) * (lane1 == h).astype(F32)
            r = _mm(e / den, v2[:, kb * LANE:(kb + 1) * LANE])
            res[h] = r if kh == qh else pltpu.roll(r, HEAD_DIM, 1)
            if qh == 1:
                o_ref[:, qb * LANE:(qb + 1) * LANE] = jnp.where(lo, res[h - 1], res[h]).astype(o_ref.dtype)
        lse_ref[...] = lse

    return _call(body, name, (nb,),
                 [_rs(WINDOW, qw), _rs(WINDOW, kw), _rs(WINDOW, kw, 0, -1), _rs(WINDOW, kw), _rs(WINDOW, kw, 0, -1), _ps((1, LANE))],
                 [_rs(WINDOW, qw), _rs(WINDOW, LANE)], [S((n, qw), MXU_DTYPE), S((n, LANE), F32)],
                 sem=("parallel",))(q, k, k, v, v, sinks)


def attn_bwd(q, k, v, o, do, lse, sinks, name):
    n, qw = q.shape
    kw = k.shape[1]
    nb = n // WINDOW
    scale = HEAD_DIM ** -0.5

    def body(q_ref, kc_ref, kp_ref, vc_ref, vp_ref, o_ref, do_ref, lse_ref, sk_ref,
             dq_ref, dka_ref, dkb_ref, dva_ref, dvb_ref, dsk_ref):
        i = pl.program_id(0)
        valid = _attn_mask(i)
        lo = _iota((WINDOW, LANE), 1) < HEAD_DIM
        k2 = jnp.concatenate([kp_ref[...], kc_ref[...]], axis=0)
        v2 = jnp.concatenate([vp_ref[...], vc_ref[...]], axis=0)
        lane1 = _iota((1, LANE), 1)
        do_ = do_ref[...].astype(F32)
        delta = _mm01(do_ * o_ref[...].astype(F32), _head_indicator(qw, ATT_HEADS))
        dk2 = [jnp.zeros((2 * WINDOW, LANE), F32) for _ in range(kw // LANE)]
        dv2 = [jnp.zeros((2 * WINDOW, LANE), F32) for _ in range(kw // LANE)]
        dsk = jnp.zeros((1, LANE), F32)
        res = {}
        for h, qb, qh, kb, kh in _attn_heads():
            half = lo if qh == 0 else jnp.logical_not(lo)
            qp = q_ref[:, qb * LANE:(qb + 1) * LANE]
            qm = jnp.where(half, qp, jnp.zeros_like(qp))
            dop = do_[:, qb * LANE:(qb + 1) * LANE]
            dom = jnp.where(half, dop, 0.0)
            kk = k2[:, kb * LANE:(kb + 1) * LANE]
            vv = v2[:, kb * LANE:(kb + 1) * LANE]
            if kh != qh:
                kk = pltpu.roll(kk, HEAD_DIM, 1)
                vv = pltpu.roll(vv, HEAD_DIM, 1)
            logits = jnp.where(valid, _mm(qm, kk, NT) * scale, -1e30)
            lse_h = lse_ref[:, h:h + 1]
            p = jnp.exp(logits - lse_h)
            dl = delta[:, h:h + 1]
            ds = p * (_mm(dom, vv, NT) - dl) * scale
            psink = jnp.exp(sk_ref[:, h:h + 1] - lse_h)
            dsk = dsk - jnp.sum(psink * dl, axis=0, keepdims=True) * (lane1 == h).astype(F32)
            res[h] = _mm(ds, kk)
            dkh = _mm(ds, qm, TN)
            dvh = _mm(p, dom, TN)
            if kh != qh:
                dkh = pltpu.roll(dkh, HEAD_DIM, 1)
                dvh = pltpu.roll(dvh, HEAD_DIM, 1)
            dk2[kb] = dk2[kb] + dkh
            dv2[kb] = dv2[kb] + dvh
            if qh == 1:
                dq_ref[:, qb * LANE:(qb + 1) * LANE] = jnp.where(lo, res[h - 1], res[h])
        for kb in range(kw // LANE):
            dkb_ref[:, kb * LANE:(kb + 1) * LANE] = dk2[kb][0:WINDOW, :]
            dka_ref[:, kb * LANE:(kb + 1) * LANE] = dk2[kb][WINDOW:, :]
            dvb_ref[:, kb * LANE:(kb + 1) * LANE] = dv2[kb][0:WINDOW, :]
            dva_ref[:, kb * LANE:(kb + 1) * LANE] = dv2[kb][WINDOW:, :]
        _acc(dsk_ref, i, dsk)

    return _call(body, name, (nb,),
                 [_rs(WINDOW, qw), _rs(WINDOW, kw), _rs(WINDOW, kw, 0, -1), _rs(WINDOW, kw), _rs(WINDOW, kw, 0, -1),
                  _rs(WINDOW, qw), _rs(WINDOW, qw), _rs(WINDOW, LANE), _ps((1, LANE))],
                 [_rs(WINDOW, qw)] + [_rs(WINDOW, kw)] * 4 + [_ps((1, LANE))],
                 [S((n, qw), F32)] + [S((n, kw), F32)] * 4 + [S((1, LANE), F32)],
                 sem=("arbitrary",))(q, k, k, v, v, o, do, lse, sinks)


def attn_grad_merge(dq, dka, dkb, dva, dvb, cos, sin, name):
    n, qw = dq.shape
    kw = dka.shape[1]
    nb = n // WINDOW
    w = qw + 2 * kw

    def body(dq_ref, dka_ref, dkb_ref, dva_ref, dvb_ref, c_ref, s_ref, o_ref, db_ref):
        i = pl.program_id(0)
        c, s = c_ref[...], -s_ref[...]
        nxt = (i < nb - 1).astype(F32)

        @pl.when(i == 0)
        def _():
            db_ref[...] = jnp.zeros_like(db_ref)

        def put(c0, val):
            o_ref[:, c0:c0 + val.shape[1]] = val.astype(o_ref.dtype)
            db_ref[:, c0:c0 + val.shape[1]] += jnp.sum(val, axis=0, keepdims=True)

        for j in range(qw // LANE):
            put(j * LANE, _rope128(dq_ref[:, j * LANE:(j + 1) * LANE], c, s))
        for j in range(kw // LANE):
            sl = slice(j * LANE, (j + 1) * LANE)
            put(qw + j * LANE, _rope128(dka_ref[:, sl] + dkb_ref[:, sl] * nxt, c, s))
        put(qw + kw, dva_ref[...] + dvb_ref[...] * nxt)

    return _call(body, name, (nb,),
                 [_rs(WINDOW, qw), _rs(WINDOW, kw), _rs(WINDOW, kw, 0, 1, nb), _rs(WINDOW, kw), _rs(WINDOW, kw, 0, 1, nb),
                  _rs(WINDOW, LANE), _rs(WINDOW, LANE)],
                 [_rs(WINDOW, w), _ps((1, w))], [S((n, w), MXU_DTYPE), S((1, w), F32)],
                 sem=("arbitrary",))(dq, dka, dkb, dva, dvb, cos, sin)


def _row(v):
    return v.reshape(1, -1)


def _pad_lanes(v, width=LANE):
    return jnp.pad(v.reshape(1, -1), ((0, 0), (0, width - v.size)))


def ffn_fwd(h, g, w_in, w_out, tag):
    xn = rms_fwd(h, _row(g), f"{tag}_rms")
    u = matmul(xn, w_in, "nn", f"{tag}_in")
    hm = swiglu_fwd(u, f"{tag}_act")
    return matmul(hm, w_out, "nn", f"{tag}_out", scale=0.5, res=h), (h, xn, u, hm)


def ffn_bwd(dh, g, w_in, w_out, saved, tag, colsum=False):
    h, xn, u, hm = saved
    dw_out = matmul(hm, dh, "tn", f"{tag}_dwout", scale=0.5)
    dhm = matmul(dh, w_out, "nt", f"{tag}_dhm", scale=0.5)
    du = swiglu_bwd(u, dhm, f"{tag}_dact")
    dw_in = matmul(xn, du, "tn", f"{tag}_dwin")
    dxn = matmul(du, w_in, "nt", f"{tag}_dxn")
    outs = rms_bwd(h, _row(g), dxn, dh, f"{tag}_drms", colsum=colsum)
    return (outs[0], dw_in, dw_out, outs[1].reshape(-1)) + ((outs[2],) if colsum else ())


def _hyb_params(w):
    d = w["conv_dw_b"].size
    inner = SSM_HEADS * HEAD_DIM
    main = 3 * d + w["ssm_conv_b"].size
    return dict(
        w_main=w["hyb_w_in"][:, :main], w_dt=jnp.pad(w["hyb_w_in"][:, main:], ((0, 0), (0, LANE - SSM_HEADS))),
        cw=jnp.pad(w["conv_dw_w"], ((0, 32 - CONV_WIDTH), (0, 0))), cb=_row(w["conv_dw_b"]),
        lg=_row(w["conv_ln_g"]), lb=_row(w["conv_ln_b"]),
        sw=jnp.pad(w["ssm_conv_w"], ((0, 8 - SSM_CONV), (0, 0))), sb=_row(w["ssm_conv_b"]),
        dtb=_pad_lanes(w["ssm_dt_bias"]), al_row=_pad_lanes(w["ssm_a_log"]), al_col=w["ssm_a_log"].reshape(-1, 1),
        dfull=_row(jnp.repeat(w["ssm_d"], HEAD_DIM)), gamma=_row(w["ssm_norm"]),
        wo_top=w["hyb_w_out"][:d], wo_bot=w["hyb_w_out"][d:], d=d, inner=inner, main=main)


def hyb_fwd(h, w, tag):
    q = _hyb_params(w)
    xn = rms_fwd(h, _row(w["norm_mix"]), f"{tag}_rms")
    proj = matmul(xn, q["w_main"], "nn", f"{tag}_in")
    dtr = matmul(xn, q["w_dt"], "nn", f"{tag}_in_dt")
    u, u1 = conv_group_fwd(proj, q["cw"], q["cb"], q["lg"], q["lb"], f"{tag}_conv")
    pre, xs, bc, dt = ssm_conv_fwd(proj, dtr, q["sw"], q["sb"], q["dtb"], f"{tag}_sconv")
    dtT = dt[:, :SSM_HEADS].T
    yssd, hs = ssd_fwd(xs, bc, dt, dtT, q["al_row"], q["al_col"], f"{tag}_ssd")
    y = ssm_gate_fwd(yssd, xs, proj, q["dfull"], q["gamma"], f"{tag}_gate")
    h2 = matmul(u, q["wo_top"], "nn", f"{tag}_out_a", res=h)
    h2 = matmul(y, q["wo_bot"], "nn", f"{tag}_out_b", res=h2)
    return h2, (h, xn, proj, u, u1, pre, xs, bc, dt, dtT, yssd, hs, y)


def hyb_bwd(dh, w, saved, tag):
    q = _hyb_params(w)
    h, xn, proj, u, u1, pre, xs, bc, dt, dtT, yssd, hs, y = saved
    du = matmul(dh, q["wo_top"], "nt", f"{tag}_du")
    dy3 = matmul(dh, q["wo_bot"], "nt", f"{tag}_dy")
    dwo = jnp.concatenate([matmul(u, dh, "tn", f"{tag}_dwo_a"), matmul(y, dh, "tn", f"{tag}_dwo_b")], axis=0)
    dval, dgate, dcw, dcb, dlg, dlb = conv_group_bwd(du, u1, proj, q["cw"], q["lg"], q["lb"], f"{tag}_dconv")
    dyssd, dxs_skip, dz, dgamma, dd = ssm_gate_bwd(dy3, yssd, xs, proj, q["dfull"], q["gamma"], f"{tag}_dgate")
    dxs, dbc, ddtr, dalog, ddtb = ssd_bwd(xs, bc, dt, dtT, q["al_row"], q["al_col"], hs, dyssd, dxs_skip, f"{tag}_dssd")
    dxbc, dsw, dsb = ssm_conv_bwd(dxs, dbc, pre, proj, q["sw"], f"{tag}_dsconv")
    dproj = jnp.concatenate([dval, dgate, dz, dxbc], axis=1)
    dw_in = jnp.concatenate([matmul(xn, dproj, "tn", f"{tag}_dwin"),
                             matmul(xn, ddtr, "tn", f"{tag}_dwin_dt")[:, :SSM_HEADS]], axis=1)
    dxn = matmul(dproj, q["w_main"], "nt", f"{tag}_dxn")
    dxn = matmul(ddtr, q["w_dt"], "nt", f"{tag}_dxn_dt", res=dxn)
    dh2, dg = rms_bwd(h, _row(w["norm_mix"]), dxn, dh, f"{tag}_drms")
    grads = dict(norm_mix=dg.reshape(-1), hyb_w_in=dw_in, conv_dw_w=dcw[:CONV_WIDTH], conv_dw_b=dcb.reshape(-1),
                 conv_ln_g=dlg.reshape(-1), conv_ln_b=dlb.reshape(-1), ssm_conv_w=dsw[:SSM_CONV], ssm_conv_b=dsb.reshape(-1),
                 ssm_dt_bias=ddtb.reshape(-1), ssm_a_log=dalog.reshape(-1), ssm_d=dd.reshape(-1), ssm_norm=dgamma.reshape(-1),
                 hyb_w_out=dwo)
    return dh2, grads


def rope_tables(n):
    half = HEAD_DIM // 2
    inv = ROPE_THETA ** (-jnp.arange(0, HEAD_DIM, 2, dtype=F32) / HEAD_DIM)
    ang = jnp.arange(n, dtype=F32)[:, None] * inv[None, :]
    cos, sin = jnp.cos(ang), jnp.sin(ang)
    reps = LANE // HEAD_DIM
    return jnp.tile(jnp.concatenate([cos, cos], axis=1), (1, reps)), jnp.tile(jnp.concatenate([-sin, sin], axis=1), (1, reps))


def att_fwd(h, w, tables, tag):
    cos, sin = tables
    xn = rms_fwd(h, _row(w["norm_mix"]), f"{tag}_rms")
    qkv = matmul(xn, w["att_w_qkv"], "nn", f"{tag}_qkv", bias=_row(w["att_b_qkv"]))
    q, k, v = rope_fwd(qkv, cos, sin, f"{tag}_rope")
    sinks = _pad_lanes(w["att_sinks"])
    o, lse = attn_fwd(q, k, v, sinks, f"{tag}_attn")
    h2 = matmul(o, w["att_w_o"], "nn", f"{tag}_o", bias=_row(w["att_b_o"]), res=h)
    return h2, (h, xn, q, k, v, o, lse, sinks)


def att_bwd(dh, dh_colsum, w, saved, tables, tag):
    cos, sin = tables
    h, xn, q, k, v, o, lse, sinks = saved
    do = matmul(dh, w["att_w_o"], "nt", f"{tag}_do")
    dwo = matmul(o, dh, "tn", f"{tag}_dwo")
    dq, dka, dkb, dva, dvb, dsk = attn_bwd(q, k, v, o, do, lse, sinks, f"{tag}_dattn")
    dqkv, dbqkv = attn_grad_merge(dq, dka, dkb, dva, dvb, cos, sin, f"{tag}_drope")
    dwqkv = matmul(xn, dqkv, "tn", f"{tag}_dwqkv")
    dxn = matmul(dqkv, w["att_w_qkv"], "nt", f"{tag}_dxn")
    dh2, dg = rms_bwd(h, _row(w["norm_mix"]), dxn, dh, f"{tag}_drms")
    grads = dict(norm_mix=dg.reshape(-1), att_w_qkv=dwqkv, att_b_qkv=dbqkv.reshape(-1), att_sinks=dsk[0, :ATT_HEADS],
                 att_w_o=dwo, att_b_o=dh_colsum.reshape(-1))
    return dh2, grads


def ple_block_fwd(h, pe, w, tag):
    xn = rms_fwd(h, _row(w["ple_norm"]), f"{tag}_rms")
    gl = matmul(xn, w["ple_gate_w"], "nn", f"{tag}_gate")
    pp = matmul(pe, w["ple_proj_w"], "nn", f"{tag}_proj")
    return ple_fwd(h, gl, pp, f"{tag}_mix"), (h, xn, gl, pp, pe)


def ple_block_bwd(dh, w, saved, tag):
    h, xn, gl, pp, pe = saved
    dpp, dgl = ple_bwd(dh, gl, pp, f"{tag}_dmix")
    dwp = matmul(pe, dpp, "tn", f"{tag}_dwp")
    dwg = matmul(xn, dgl, "tn", f"{tag}_dwg")
    dxn = matmul(dgl, w["ple_gate_w"], "nt", f"{tag}_dxn")
    dh2, dg = rms_bwd(h, _row(w["ple_norm"]), dxn, dh, f"{tag}_drms")
    return dh2, dict(ple_norm=dg.reshape(-1), ple_gate_w=dwg, ple_proj_w=dwp)


PER_LAYER = ("norm_ffn1", "ffn1_w_in", "ffn1_w_out", "norm_mix", "norm_ffn2", "ffn2_w_in", "ffn2_w_out",
             "ple_norm", "ple_gate_w", "ple_proj_w")
EVEN_ONLY = ("hyb_w_in", "conv_dw_w", "conv_dw_b", "conv_ln_g", "conv_ln_b", "ssm_conv_w", "ssm_conv_b",
             "ssm_dt_bias", "ssm_a_log", "ssm_d", "ssm_norm", "hyb_w_out")
ODD_ONLY = ("att_w_qkv", "att_b_qkv", "att_sinks", "att_w_o", "att_b_o")


def _layer_weights(full, i):
    w = {k: full[k][i] for k in PER_LAYER}
    for k in (EVEN_ONLY if i % 2 == 0 else ODD_ONLY):
        w[k] = full[k][i // 2]
    return w


def trunk_fwd_bwd(x, pe, target, full):
    depth = full["norm_ffn1"].shape[0]
    tables = rope_tables(x.shape[0])
    h = x
    saved = []
    for i in range(depth):
        w = _layer_weights(full, i)
        h, s1 = ffn_fwd(h, w["norm_ffn1"], w["ffn1_w_in"], w["ffn1_w_out"], f"l{i}_ffn1")
        if i % 2 == 0:
            h, s2 = hyb_fwd(h, w, f"l{i}_hyb")
        else:
            h, s2 = att_fwd(h, w, tables, f"l{i}_att")
        h, s3 = ffn_fwd(h, w["norm_ffn2"], w["ffn2_w_in"], w["ffn2_w_out"], f"l{i}_ffn2")
        h, s4 = ple_block_fwd(h, pe[i], w, f"l{i}_ple")
        saved.append((s1, s2, s3, s4))
    dh, dgf, loss = loss_head(h, _row(full["final_norm"]), target, "loss_head")
    grads = {k: [None] * full[k].shape[0] for k in full if k != "final_norm"}
    grads["final_norm"] = dgf.reshape(-1)
    for i in reversed(range(depth)):
        w = _layer_weights(full, i)
        s1, s2, s3, s4 = saved[i]
        dh, g = ple_block_bwd(dh, w, s4, f"l{i}_ple")
        odd = i % 2 == 1
        out = ffn_bwd(dh, w["norm_ffn2"], w["ffn2_w_in"], w["ffn2_w_out"], s3, f"l{i}_ffn2", colsum=odd)
        dh = out[0]
        g.update(ffn2_w_in=out[1], ffn2_w_out=out[2], norm_ffn2=out[3])
        if odd:
            dh, gm = att_bwd(dh, out[4], w, s2, tables, f"l{i}_att")
        else:
            dh, gm = hyb_bwd(dh, w, s2, f"l{i}_hyb")
        g.update(gm)
        out = ffn_bwd(dh, w["norm_ffn1"], w["ffn1_w_in"], w["ffn1_w_out"], s1, f"l{i}_ffn1")
        dh = out[0]
        g.update(ffn1_w_in=out[1], ffn1_w_out=out[2], norm_ffn1=out[3])
        for k, v in g.items():
            grads[k][i if k in PER_LAYER else i // 2] = v
    return loss, dh, grads


ANY = pl.BlockSpec(memory_space=pl.ANY)


def _me():
    return lax.axis_index("x"), lax.axis_index("y"), lax.axis_index("c")


def _flip(v, f):
    return 1 - v if f else v


def _remote(src, dst, send_sems, recv_sems, k, dev):
    return pltpu.make_async_remote_copy(src_ref=src, dst_ref=dst, send_sem=send_sems.at[k], recv_sem=recv_sems.at[k],
                                        device_id=dev, device_id_type=MESH)


CHIP_FLIPS = ((1, 0), (0, 1), (1, 1))
DEV_FLIPS = tuple((fx, fy, fc) for fx in (0, 1) for fy in (0, 1) for fc in (0, 1))[1:]


def all_gather_chips(x, name):
    r, l = x.shape
    half = r // 2
    assert r % 2 == 0

    def body(x_ref, out_ref, send_sems, recv_sems, local_sem):
        mx, my, mc = _me()
        chip = 2 * mx + my

        def rows(ch, hc):
            return out_ref.at[ch, pl.ds(hc * half, half), :]

        mine = pltpu.make_async_copy(x_ref, out_ref.at[chip], local_sem)
        mine.start()
        src = x_ref.at[pl.ds(mc * half, half), :]
        peers = [(_flip(mx, fx), _flip(my, fy)) for fx, fy in CHIP_FLIPS]
        first = [_remote(src, rows(chip, mc), send_sems, recv_sems, j, (px, py, mc)) for j, (px, py) in enumerate(peers)]
        for cp in first:
            cp.start()
        passed = []
        for j, (px, py) in enumerate(peers):
            landed = rows(2 * px + py, mc)
            _remote(src, landed, send_sems, recv_sems, j, (px, py, mc)).wait_recv()
            fw = _remote(landed, landed, send_sems, recv_sems, 3 + j, (mx, my, 1 - mc))
            fw.start()
            passed.append(fw)
        for j, (px, py) in enumerate(peers):
            _remote(src, rows(2 * px + py, 1 - mc), send_sems, recv_sems, 3 + j, (mx, my, 1 - mc)).wait_recv()
        for cp in first + passed:
            cp.wait_send()
        mine.wait()

    return pl.pallas_call(
        body, name=name, out_shape=S((N_CHIPS, r, l), x.dtype), in_specs=[ANY], out_specs=ANY,
        scratch_shapes=[pltpu.SemaphoreType.DMA((6,)), pltpu.SemaphoreType.DMA((6,)), pltpu.SemaphoreType.DMA(())])(x)


def all_gather_devices(v, name):
    r, l = v.shape

    def body(v_ref, out_ref, send_sems, recv_sems, local_sem):
        mx, my, mc = _me()
        me = 4 * mx + 2 * my + mc
        mine = pltpu.make_async_copy(v_ref, out_ref.at[me], local_sem)
        mine.start()
        peers = [(_flip(mx, fx), _flip(my, fy), _flip(mc, fc)) for fx, fy, fc in DEV_FLIPS]
        sends = [_remote(v_ref, out_ref.at[me], send_sems, recv_sems, j, p) for j, p in enumerate(peers)]
        for cp in sends:
            cp.start()
        for j, (px, py, pc) in enumerate(peers):
            _remote(v_ref, out_ref.at[4 * px + 2 * py + pc], send_sems, recv_sems, j, (px, py, pc)).wait_recv()
        for cp in sends:
            cp.wait_send()
        mine.wait()

    return pl.pallas_call(
        body, name=name, out_shape=S((N_DEV, r, l), v.dtype), in_specs=[ANY], out_specs=ANY,
        scratch_shapes=[pltpu.SemaphoreType.DMA((7,)), pltpu.SemaphoreType.DMA((7,)), pltpu.SemaphoreType.DMA(())])(v)


def sum_devices(g8, name):
    nd, r, l = g8.shape
    tile = r
    for t in (512, 256, 128, 64, 32, 16, 8):
        if r % t == 0:
            tile = t
            break

    def body(g_ref, o_ref):
        acc = g_ref[0]
        for d in range(1, nd):
            acc = acc + g_ref[d]
        o_ref[...] = acc

    return _call(body, name, (r // tile,), [pl.BlockSpec((nd, tile, l), lambda i: (0, i, 0))], _rs(tile, l), S((r, l), F32),
                 sem=("parallel",))(g8)


def exchange_halves(g4, name):
    nch, r, l = g4.shape
    half = r // 2

    def body(g_ref, out_ref, send_sems, recv_sems):
        mx, my, mc = _me()
        sib = (mx, my, 1 - mc)
        cps = [_remote(g_ref.at[j, pl.ds((1 - mc) * half, half), :], out_ref.at[j], send_sems, recv_sems, j, sib)
               for j in range(nch)]
        for cp in cps:
            cp.start()
        for cp in cps:
            cp.wait_recv()
        for cp in cps:
            cp.wait_send()

    return pl.pallas_call(
        body, name=name, out_shape=S((nch, half, l), g4.dtype), in_specs=[ANY], out_specs=ANY,
        scratch_shapes=[pltpu.SemaphoreType.DMA((nch,)), pltpu.SemaphoreType.DMA((nch,))])(g4)


def add_halves(g4, got, name):
    nch, r, l = g4.shape
    half = r // 2
    tile = _pick_rows(half)
    nt = half // tile

    def body(g_ref, r_ref, a_ref, own_ref):
        j = pl.program_id(1)
        chip = 2 * lax.axis_index("x") + lax.axis_index("y")
        val = g_ref[0] + r_ref[0]
        a_ref[0] = val.astype(a_ref.dtype)

        @pl.when(j == chip)
        def _():
            own_ref[...] = val

    return pl.pallas_call(
        body, name=name, grid=(nt, nch),
        in_specs=[pl.BlockSpec((1, tile, l), lambda i, j: (j, lax.axis_index("c") * nt + i, 0)),
                  pl.BlockSpec((1, tile, l), lambda i, j: (j, i, 0))],
        out_specs=[pl.BlockSpec((1, tile, l), lambda i, j: (j, i, 0)), pl.BlockSpec((tile, l), lambda i, j: (i, 0))],
        out_shape=[S((nch, half, l), MXU_DTYPE), S((half, l), F32)],
        compiler_params=pltpu.CompilerParams(dimension_semantics=("parallel", "arbitrary"), vmem_limit_bytes=VMEM_LIMIT))(g4, got)


def _pick_rows(r, cap=512):
    for t in (512, 256, 128, 64, 32, 16):
        if t <= cap and r % t == 0:
            return t
    return r


def exchange_chips(a, name):
    nch, h, l = a.shape

    def body(a_ref, out_ref, send_sems, recv_sems):
        mx, my, mc = _me()
        peers = [(_flip(mx, fx), _flip(my, fy)) for fx, fy in CHIP_FLIPS]
        cps = [_remote(a_ref.at[2 * px + py], out_ref.at[j], send_sems, recv_sems, j, (px, py, mc))
               for j, (px, py) in enumerate(peers)]
        for cp in cps:
            cp.start()
        for cp in cps:
            cp.wait_recv()
        for cp in cps:
            cp.wait_send()

    return pl.pallas_call(
        body, name=name, out_shape=S((3, h, l), a.dtype), in_specs=[ANY], out_specs=ANY,
        scratch_shapes=[pltpu.SemaphoreType.DMA((3,)), pltpu.SemaphoreType.DMA((3,))])(a)


def add_chips(own, got, name):
    h, l = own.shape
    tile = _pick_rows(h)

    def body(o_ref, g_ref, out_ref):
        out_ref[...] = ((o_ref[...] + g_ref[0].astype(F32)) + g_ref[1].astype(F32)) + g_ref[2].astype(F32)

    return _call(body, name, (h // tile,), [_rs(tile, l), pl.BlockSpec((3, tile, l), lambda i: (0, i, 0))], _rs(tile, l),
                 S((h, l), F32), sem=("parallel",))(own, got)


def join_halves(part, name):
    h, l = part.shape

    def body(p_ref, out_ref, send_sems, recv_sems, local_sem):
        mx, my, mc = _me()
        mine = pltpu.make_async_copy(p_ref, out_ref.at[pl.ds(mc * h, h), :], local_sem)
        mine.start()
        cp = _remote(p_ref, out_ref.at[pl.ds(mc * h, h), :], send_sems, recv_sems, 0, (mx, my, 1 - mc))
        cp.start()
        _remote(p_ref, out_ref.at[pl.ds((1 - mc) * h, h), :], send_sems, recv_sems, 0, (mx, my, 1 - mc)).wait_recv()
        cp.wait_send()
        mine.wait()

    return pl.pallas_call(
        body, name=name, out_shape=S((2 * h, l), part.dtype), in_specs=[ANY], out_specs=ANY,
        scratch_shapes=[pltpu.SemaphoreType.DMA((1,)), pltpu.SemaphoreType.DMA((1,)), pltpu.SemaphoreType.DMA(())])(part)


def reduce_scatter(g4, tag):
    got = exchange_halves(g4, f"{tag}_d2d")
    a, own = add_halves(g4, got, f"{tag}_add1")
    got2 = exchange_chips(a, f"{tag}_ici")
    part = add_chips(own, got2, f"{tag}_add2")
    return join_halves(part, f"{tag}_join")


PACK_L = 1024
BIG_ROW_MULT = 512


def _pack(arrs, dtype, row_mult, lead=None):
    lead_shape = () if lead is None else arrs[0].shape[:lead]
    flat = jnp.concatenate([a.astype(dtype).reshape(lead_shape + (-1,)) for a in arrs], axis=-1)
    n = flat.shape[-1]
    unit = row_mult * PACK_L
    total = -(-n // unit) * unit
    flat = jnp.pad(flat, [(0, 0)] * len(lead_shape) + [(0, total - n)])
    return flat.reshape(lead_shape + (total // PACK_L, PACK_L))


def _unpack(packed, shapes, lead=None):
    lead_shape = () if lead is None else packed.shape[:lead]
    flat = packed.reshape(lead_shape + (-1,))
    out, off = [], 0
    for shp in shapes:
        n = int(np.prod(shp))
        out.append(flat[..., off:off + n].reshape(lead_shape + tuple(shp)))
        off += n
    return out


def _to_full(gathered, axis):
    t = jnp.moveaxis(gathered, 0, axis)
    shp = t.shape
    return t.reshape(shp[:axis] + (shp[axis] * shp[axis + 1],) + shp[axis + 2:])


def _to_chip_major(full, axis):
    shp = full.shape
    t = full.reshape(shp[:axis] + (N_CHIPS, shp[axis] // N_CHIPS) + shp[axis + 1:])
    return jnp.moveaxis(t, axis, 0)


WEIGHTS = ("norm_ffn1", "ffn1_w_in", "ffn1_w_out", "norm_mix", "norm_ffn2", "ffn2_w_in", "ffn2_w_out", "ple_norm",
           "ple_gate_w", "ple_proj_w", "hyb_w_in", "conv_dw_w", "conv_dw_b", "conv_ln_g", "conv_ln_b", "ssm_conv_w",
           "ssm_conv_b", "ssm_dt_bias", "ssm_a_log", "ssm_d", "ssm_norm", "hyb_w_out", "att_w_qkv", "att_b_qkv",
           "att_sinks", "att_w_o", "att_b_o", "final_norm")
SHARD_AXIS = dict(ffn1_w_in=2, ffn1_w_out=1, ffn2_w_in=2, ffn2_w_out=1, ple_gate_w=1, ple_proj_w=2, hyb_w_in=2,
                  conv_dw_w=2, ssm_conv_w=2, hyb_w_out=1, att_w_qkv=2, att_b_qkv=1, att_w_o=1, att_b_o=1)
BIG = ("ffn1_w_in", "ffn1_w_out", "ffn2_w_in", "ffn2_w_out", "ple_gate_w", "ple_proj_w", "hyb_w_in", "hyb_w_out",
       "att_w_qkv", "att_w_o")
SMALL_SHARDED = ("conv_dw_w", "ssm_conv_w", "att_b_qkv", "att_b_o")
SMALL = tuple(k for k in WEIGHTS if k not in BIG)


def _step(x, p, target, w, m, v):
    mx, my = lax.axis_index("x"), lax.axis_index("y")
    chip = 2 * mx + my

    big_g = all_gather_chips(_pack([w[k] for k in BIG], MXU_DTYPE, BIG_ROW_MULT), "gather_weights")
    full = {k: _to_full(g, SHARD_AXIS[k]) for k, g in zip(BIG, _unpack(big_g, [w[k].shape for k in BIG], lead=1))}
    small_g = all_gather_devices(_pack([w[k] for k in SMALL_SHARDED], F32, 8), "gather_small")[0::2]
    for k, g in zip(SMALL_SHARDED, _unpack(small_g, [w[k].shape for k in SMALL_SHARDED], lead=1)):
        full[k] = _to_full(g, SHARD_AXIS[k])
    for k in WEIGHTS:
        full.setdefault(k, w[k])

    loss, dx, grads = trunk_fwd_bwd(x[0], p[:, 0], target[0], full)
    grads = {k: (g if k == "final_norm" else jnp.stack(g)) for k, g in grads.items()}

    big4 = _pack([_to_chip_major(grads[k], SHARD_AXIS[k]) for k in BIG], F32, BIG_ROW_MULT, lead=1)
    big_sum = reduce_scatter(big4, "grads")
    g_out = dict(zip(BIG, _unpack(big_sum, [w[k].shape for k in BIG])))
    vec = _pack([loss[0:1, 0:1]] + [grads[k] for k in SMALL], F32, 8)
    vec = sum_devices(all_gather_devices(vec, "gather_vectors"), "sum_vectors")
    parts = _unpack(vec, [(1, 1)] + [grads[k].shape for k in SMALL])
    loss_out = parts[0].reshape(())
    for k, g in zip(SMALL, parts[1:]):
        if k in SHARD_AXIS:
            ax = SHARD_AXIS[k]
            g = lax.dynamic_slice_in_dim(g, chip * w[k].shape[ax], w[k].shape[ax], axis=ax)
        g_out[k] = g

    delta, new_m, new_v = {}, {}, {}
    for k in BIG:
        shp = w[k].shape
        two_d = lambda a: a.reshape(-1, shp[-1])
        d_, m_, v_ = adamw(two_d(w[k]), two_d(g_out[k]), two_d(m[k]), two_d(v[k]), f"adamw_{k}")
        delta[k], new_m[k], new_v[k] = d_.reshape(shp), m_.reshape(shp), v_.reshape(shp)
    shapes = [w[k].shape for k in SMALL]
    packed = [_pack([src[k] for k in SMALL], F32, 8) for src in (w, g_out, m, v)]
    outs = adamw(*packed, "adamw_small")
    for dst, o in zip((delta, new_m, new_v), outs):
        for k, a in zip(SMALL, _unpack(o, shapes)):
            dst[k] = a
    return ((loss_out, dx[None]) + tuple(g_out[k] for k in WEIGHTS) + tuple(delta[k] for k in WEIGHTS)
            + tuple(new_m[k] for k in WEIGHTS) + tuple(new_v[k] for k in WEIGHTS))


def kernel(x, p, norm_ffn1, ffn1_w_in, ffn1_w_out, norm_mix, norm_ffn2, ffn2_w_in, ffn2_w_out, ple_norm, ple_gate_w, ple_proj_w, hyb_w_in, conv_dw_w, conv_dw_b, conv_ln_g, conv_ln_b, ssm_conv_w, ssm_conv_b, ssm_dt_bias, ssm_a_log, ssm_d, ssm_norm, hyb_w_out, att_w_qkv, att_b_qkv, att_sinks, att_w_o, att_b_o, final_norm, loss_target, m_norm_ffn1, m_ffn1_w_in, m_ffn1_w_out, m_norm_mix, m_norm_ffn2, m_ffn2_w_in, m_ffn2_w_out, m_ple_norm, m_ple_gate_w, m_ple_proj_w, m_hyb_w_in, m_conv_dw_w, m_conv_dw_b, m_conv_ln_g, m_conv_ln_b, m_ssm_conv_w, m_ssm_conv_b, m_ssm_dt_bias, m_ssm_a_log, m_ssm_d, m_ssm_norm, m_hyb_w_out, m_att_w_qkv, m_att_b_qkv, m_att_sinks, m_att_w_o, m_att_b_o, m_final_norm, v_norm_ffn1, v_ffn1_w_in, v_ffn1_w_out, v_norm_mix, v_norm_ffn2, v_ffn2_w_in, v_ffn2_w_out, v_ple_norm, v_ple_gate_w, v_ple_proj_w, v_hyb_w_in, v_conv_dw_w, v_conv_dw_b, v_conv_ln_g, v_conv_ln_b, v_ssm_conv_w, v_ssm_conv_b, v_ssm_dt_bias, v_ssm_a_log, v_ssm_d, v_ssm_norm, v_hyb_w_out, v_att_w_qkv, v_att_b_qkv, v_att_sinks, v_att_w_o, v_att_b_o, v_final_norm):
    given = locals()
    w = {k: given[k] for k in WEIGHTS}
    m = {k: given["m_" + k] for k in WEIGHTS}
    v = {k: given["v_" + k] for k in WEIGHTS}
    return _step(x, p, loss_target, w, m, v)
```

```python
import functools
import math

import numpy as np
import jax
import jax.numpy as jnp
from jax import lax
from jax.experimental import pallas as pl
from jax.experimental.pallas import tpu as pltpu

F32 = jnp.float32
BF16 = jnp.bfloat16
MXU_DTYPE = jnp.bfloat16
S = jax.ShapeDtypeStruct
MESH = pl.DeviceIdType.MESH

V7X_VMEM_BYTES = 64 * 2**20
VMEM_LIMIT = 48 * 2**20
LANE = 128

EPS = 1e-6
SSM_HEADS = 16
HEAD_DIM = 64
SSM_GROUPS = 2
SSM_STATE = 128
SSM_CONV = 4
CHUNK = 128
CONV_WIDTH = 31
ATT_HEADS = 16
ATT_KV_HEADS = 4
WINDOW = 128
ROPE_THETA = 10000.0
ADAM_LR = 0.001
ADAM_B1 = 0.9
ADAM_B2 = 0.999
ADAM_EPS = 1e-08
ADAM_WD = 0.01
ADAM_STEP = 10

N_CHIPS = 4
N_DEV = 8

NN = ((1,), (0,))
NT = ((1,), (1,))
TN = ((0,), (0,))


def _mm(a, b, dims=NN):
    return lax.dot_general(a.astype(MXU_DTYPE), b.astype(MXU_DTYPE), (dims, ((), ())), preferred_element_type=F32)


def _split3(a):
    hi = a.astype(BF16)
    r = a - hi.astype(F32)
    mid = r.astype(BF16)
    lo = (r - mid.astype(F32)).astype(BF16)
    return hi, mid, lo


def _mm01(a, onehot, dims=NN):
    o = onehot.astype(BF16)
    out = None
    for part in _split3(a):
        t = lax.dot_general(part, o, (dims, ((), ())), preferred_element_type=F32)
        out = t if out is None else out + t
    return out


def _01mm(onehot, a):
    o = onehot.astype(BF16)
    out = None
    for part in _split3(a):
        t = lax.dot_general(o, part, (NN, ((), ())), preferred_element_type=F32)
        out = t if out is None else out + t
    return out


def _sigmoid(x):
    return 1.0 / (1.0 + jnp.exp(-x))


def _softplus(x):
    return jnp.maximum(x, 0.0) + jnp.log(1.0 + jnp.exp(-jnp.abs(x)))


def _iota(shape, axis):
    return lax.broadcasted_iota(jnp.int32, shape, axis)


def _head_indicator(width, heads, transposed=False):
    per = width // heads
    if transposed:
        return (_iota((heads, width), 1) // per == _iota((heads, width), 0)).astype(F32)
    return (_iota((width, heads), 0) // per == _iota((width, heads), 1)).astype(F32)


def _acc(ref, i, val):
    @pl.when(i == 0)
    def _():
        ref[...] = val

    @pl.when(i > 0)
    def _():
        ref[...] += val


def _rs(tile, width, col=0, shift=0, n=None):
    if shift == 0:
        return pl.BlockSpec((tile, width), lambda i: (i, col))
    if shift < 0:
        return pl.BlockSpec((tile, width), lambda i: (jnp.maximum(i - 1, 0), col))
    return pl.BlockSpec((tile, width), lambda i: (jnp.minimum(i + 1, n - 1), col))


def _ps(shape):
    return pl.BlockSpec(shape, lambda i: (0,) * len(shape))


def _call(body, name, grid, in_specs, out_specs, out_shape, scratch=(), sem=None):
    return pl.pallas_call(
        body, name=name, grid=grid, in_specs=in_specs, out_specs=out_specs, out_shape=out_shape,
        scratch_shapes=list(scratch),
        compiler_params=pltpu.CompilerParams(dimension_semantics=sem, vmem_limit_bytes=VMEM_LIMIT))


def _row_tile(n, target):
    t = min(n, target)
    assert n % t == 0, (n, t)
    return t


def _pick_tile(dim, target):
    if dim <= target:
        return dim
    t = (int(1.4 * target) // LANE) * LANE
    while t >= LANE:
        if dim % t == 0:
            return t
        t -= LANE
    return dim


def matmul(a, b, mode, name, *, out_dtype=F32, scale=None, res=None, bias=None, tm=1024, tn=1024, tk=1024):
    if mode == "nn":
        (m, k), (k2, n) = a.shape, b.shape
    elif mode == "nt":
        (m, k), (n, k2) = a.shape, b.shape
    else:
        (k, m), (k2, n) = a.shape, b.shape
    assert k == k2, (a.shape, b.shape, mode)
    tm, tn, tk = _pick_tile(m, tm), _pick_tile(n, tn), _pick_tile(k, tk)
    nk = k // tk
    dims = {"nn": NN, "nt": NT, "tn": TN}[mode]
    a_spec = (pl.BlockSpec((tk, tm), lambda i, j, kk: (kk, i)) if mode == "tn"
              else pl.BlockSpec((tm, tk), lambda i, j, kk: (i, kk)))
    b_spec = (pl.BlockSpec((tn, tk), lambda i, j, kk: (j, kk)) if mode == "nt"
              else pl.BlockSpec((tk, tn), lambda i, j, kk: (kk, j)))
    o_spec = pl.BlockSpec((tm, tn), lambda i, j, kk: (i, j))
    ins, in_specs = [a, b], [a_spec, b_spec]
    if bias is not None:
        ins.append(bias)
        in_specs.append(pl.BlockSpec((1, tn), lambda i, j, kk: (0, j)))
    if res is not None:
        ins.append(res)
        in_specs.append(o_spec)

    def body(*refs):
        a_ref, b_ref = refs[0], refs[1]
        o_ref, acc_ref = refs[-2], refs[-1]
        kk = pl.program_id(2)

        @pl.when(kk == 0)
        def _():
            acc_ref[...] = jnp.zeros_like(acc_ref)

        acc_ref[...] += _mm(a_ref[...], b_ref[...], dims)

        @pl.when(kk == nk - 1)
        def _():
            out = acc_ref[...]
            if scale is not None:
                out = out * scale
            pos = 2
            if bias is not None:
                out = out + refs[pos][...]
                pos += 1
            if res is not None:
                out = out + refs[pos][...]
            o_ref[...] = out.astype(o_ref.dtype)

    return _call(body, name, (m // tm, n // tn, nk), in_specs, o_spec, S((m, n), out_dtype),
                 scratch=[pltpu.VMEM((tm, tn), F32)], sem=("parallel", "parallel", "arbitrary"))(*ins)


def rms_fwd(h, g, name):
    n, d = h.shape
    tile = _row_tile(n, 512)

    def body(h_ref, g_ref, o_ref):
        x = h_ref[...]
        r = lax.rsqrt(jnp.mean(x * x, axis=-1, keepdims=True) + EPS)
        o_ref[...] = (x * r * g_ref[...]).astype(o_ref.dtype)

    return _call(body, name, (n // tile,), [_rs(tile, d), _ps((1, d))], _rs(tile, d), S((n, d), MXU_DTYPE),
                 sem=("parallel",))(h, g)


def _rms_bwd_math(x, g, dy):
    r = lax.rsqrt(jnp.mean(x * x, axis=-1, keepdims=True) + EPS)
    xh = x * r
    dg = jnp.sum(dy * xh, axis=0, keepdims=True)
    dxh = dy * g
    dx = r * (dxh - xh * jnp.mean(dxh * xh, axis=-1, keepdims=True))
    return dx, dg


def rms_bwd(h, g, dxn, dh_in, name, colsum=False):
    n, d = h.shape
    tile = _row_tile(n, 256)

    def body(h_ref, g_ref, dxn_ref, dh_ref, o_ref, dg_ref, *cs_ref):
        i = pl.program_id(0)
        dx, dg = _rms_bwd_math(h_ref[...], g_ref[...], dxn_ref[...].astype(F32))
        out = dh_ref[...] + dx
        o_ref[...] = out
        _acc(dg_ref, i, dg)
        if colsum:
            _acc(cs_ref[0], i, jnp.sum(out, axis=0, keepdims=True))

    outs = [S((n, d), F32), S((1, d), F32)] + ([S((1, d), F32)] if colsum else [])
    ospecs = [_rs(tile, d), _ps((1, d))] + ([_ps((1, d))] if colsum else [])
    return _call(body, name, (n // tile,), [_rs(tile, d), _ps((1, d)), _rs(tile, d), _rs(tile, d)], ospecs, outs,
                 sem=("arbitrary",))(h, g, dxn, dh_in)


def swiglu_fwd(u, name):
    n, f2 = u.shape
    f = f2 // 2
    tile = _row_tile(n, 256)

    def body(u_ref, o_ref):
        a = u_ref[:, :f]
        o_ref[...] = (a * _sigmoid(a) * u_ref[:, f:]).astype(o_ref.dtype)

    return _call(body, name, (n // tile,), [_rs(tile, f2)], _rs(tile, f), S((n, f), MXU_DTYPE), sem=("parallel",))(u)


def swiglu_bwd(u, dhm, name):
    n, f2 = u.shape
    f = f2 // 2
    tile = _row_tile(n, 256)

    def body(u_ref, d_ref, o_ref):
        a = u_ref[:, :f]
        b = u_ref[:, f:]
        d = d_ref[...]
        s = _sigmoid(a)
        o_ref[:, :f] = (d * b * s * (1.0 + a * (1.0 - s))).astype(o_ref.dtype)
        o_ref[:, f:] = (d * a * s).astype(o_ref.dtype)

    return _call(body, name, (n // tile,), [_rs(tile, f2), _rs(tile, f)], _rs(tile, f2), S((n, f2), MXU_DTYPE),
                 sem=("parallel",))(u, dhm)


def ple_fwd(h, gl, pp, name):
    n, d = h.shape
    tile = _row_tile(n, 512)

    def body(h_ref, gl_ref, pp_ref, o_ref):
        o_ref[...] = h_ref[...] + _sigmoid(gl_ref[...]) * pp_ref[...]

    return _call(body, name, (n // tile,), [_rs(tile, d)] * 3, _rs(tile, d), S((n, d), F32), sem=("parallel",))(h, gl, pp)


def ple_bwd(dh, gl, pp, name):
    n, d = dh.shape
    tile = _row_tile(n, 512)

    def body(dh_ref, gl_ref, pp_ref, dpp_ref, dgl_ref):
        g = _sigmoid(gl_ref[...])
        dh_ = dh_ref[...]
        dpp_ref[...] = (dh_ * g).astype(dpp_ref.dtype)
        dgl_ref[...] = (dh_ * pp_ref[...] * g * (1.0 - g)).astype(dgl_ref.dtype)

    return _call(body, name, (n // tile,), [_rs(tile, d)] * 3, [_rs(tile, d)] * 2, [S((n, d), MXU_DTYPE)] * 2,
                 sem=("parallel",))(dh, gl, pp)


def loss_head(h, g, target, name):
    n, d = h.shape
    tile = _row_tile(n, 256)

    def body(h_ref, g_ref, t_ref, dh_ref, dg_ref, loss_ref):
        i = pl.program_id(0)
        x = h_ref[...]
        gg = g_ref[...]
        r = lax.rsqrt(jnp.mean(x * x, axis=-1, keepdims=True) + EPS)
        err = x * r * gg - t_ref[...]
        part = 0.5 * jnp.sum(jnp.mean(err * err, axis=-1, keepdims=True), axis=0, keepdims=True)
        dx, dg = _rms_bwd_math(x, gg, err * (1.0 / d))
        dh_ref[...] = dx
        _acc(dg_ref, i, dg)
        _acc(loss_ref, i, jnp.broadcast_to(part, (8, LANE)))

    return _call(body, name, (n // tile,), [_rs(tile, d), _ps((1, d)), _rs(tile, d)],
                 [_rs(tile, d), _ps((1, d)), _ps((8, LANE))], [S((n, d), F32), S((1, d), F32), S((8, LANE), F32)],
                 sem=("arbitrary",))(h, g, target)


def adamw(w, g, m, v, name):
    r, c = w.shape
    tile = r
    for t in (512, 256, 128, 64, 32, 16, 8):
        if r % t == 0 and t * c * 4 <= 2**21:
            tile = t
            break
    c1 = np.float32(1.0 - ADAM_B1 ** ADAM_STEP)
    c2 = np.float32(1.0 - ADAM_B2 ** ADAM_STEP)

    def body(w_ref, g_ref, m_ref, v_ref, d_ref, mo_ref, vo_ref):
        gg = g_ref[...]
        mm = ADAM_B1 * m_ref[...] + (1.0 - ADAM_B1) * gg
        vv = ADAM_B2 * v_ref[...] + (1.0 - ADAM_B2) * (gg * gg)
        mo_ref[...] = mm
        vo_ref[...] = vv
        d_ref[...] = -ADAM_LR * ((mm / c1) / (jnp.sqrt(vv / c2) + ADAM_EPS) + ADAM_WD * w_ref[...])

    return _call(body, name, (r // tile,), [_rs(tile, c)] * 4, [_rs(tile, c)] * 3, [S((r, c), F32)] * 3,
                 sem=("parallel",))(w, g, m, v)


def _taps_fwd(sc, w_ref, width, halo, tile, acc):
    for k in range(width):
        o = halo - (width - 1) + k
        acc = acc + w_ref[k:k + 1, :] * sc[o:o + tile, :]
    return acc


def _taps_bwd_x(sc_d, w_ref, width, tile, acc):
    for k in range(width):
        o = (width - 1) - k
        acc = acc + w_ref[k:k + 1, :] * sc_d[o:o + tile, :]
    return acc


def _taps_bwd_w(dy, sc, dw_ref, width, halo, tile, i):
    @pl.when(i == 0)
    def _():
        dw_ref[...] = jnp.zeros_like(dw_ref)

    for k in range(width):
        o = halo - (width - 1) + k
        dw_ref[k:k + 1, :] += jnp.sum(dy * sc[o:o + tile, :], axis=0, keepdims=True)


def _ln_stats(x):
    mu = jnp.mean(x, axis=-1, keepdims=True)
    xc = x - mu
    r = lax.rsqrt(jnp.mean(xc * xc, axis=-1, keepdims=True) + EPS)
    return xc * r, r


def conv_group_fwd(proj, cw, cb, lg, lb, name):
    n = proj.shape[0]
    d = cw.shape[1]
    tile = _row_tile(n, 256)
    halo = 32

    def body(v_ref, g_ref, vp_ref, gp_ref, cw_ref, cb_ref, lg_ref, lb_ref, u_ref, u1_ref, sc):
        i = pl.program_id(0)
        first = (i > 0).astype(F32)
        sc[0:halo, :] = vp_ref[tile - halo:, :] * _sigmoid(gp_ref[tile - halo:, :]) * first
        sc[halo:, :] = v_ref[...] * _sigmoid(g_ref[...])
        u1 = _taps_fwd(sc, cw_ref, CONV_WIDTH, halo, tile, jnp.zeros((tile, d), F32) + cb_ref[...])
        u1_ref[...] = u1
        xh, _ = _ln_stats(u1)
        y = xh * lg_ref[...] + lb_ref[...]
        u_ref[...] = (y * _sigmoid(y)).astype(u_ref.dtype)

    return _call(body, name, (n // tile,),
                 [_rs(tile, d, 0), _rs(tile, d, 1), _rs(tile, d, 0, -1), _rs(tile, d, 1, -1),
                  _ps(cw.shape), _ps((1, d)), _ps((1, d)), _ps((1, d))],
                 [_rs(tile, d), _rs(tile, d)], [S((n, d), MXU_DTYPE), S((n, d), F32)],
                 scratch=[pltpu.VMEM((halo + tile, d), F32)], sem=("arbitrary",))(proj, proj, proj, proj, cw, cb, lg, lb)


def conv_group_bwd(du, u1, proj, cw, lg, lb, name):
    n = proj.shape[0]
    d = cw.shape[1]
    tile = _row_tile(n, 256)
    halo = 32
    nt = n // tile

    def body(du_ref, dun_ref, u1_ref, u1n_ref, v_ref, g_ref, vp_ref, gp_ref, cw_ref, lg_ref, lb_ref,
             dv_ref, dg_ref, dcw_ref, dcb_ref, dlg_ref, dlb_ref, sc, sc_d):
        i = pl.program_id(0)

        def ln_swish_bwd(dy_, u1_):
            xh, r = _ln_stats(u1_)
            y = xh * lg_ref[...] + lb_ref[...]
            s = _sigmoid(y)
            dyy = dy_ * s * (1.0 + y * (1.0 - s))
            dxh = dyy * lg_ref[...]
            dx = r * (dxh - jnp.mean(dxh, axis=-1, keepdims=True) - xh * jnp.mean(dxh * xh, axis=-1, keepdims=True))
            return dx, jnp.sum(dyy * xh, axis=0, keepdims=True), jnp.sum(dyy, axis=0, keepdims=True)

        du1, dlg, dlb = ln_swish_bwd(du_ref[...].astype(F32), u1_ref[...])
        du1n, _, _ = ln_swish_bwd(dun_ref[0:halo, :].astype(F32), u1n_ref[0:halo, :])
        sc_d[0:tile, :] = du1
        sc_d[tile:, :] = du1n * (i < nt - 1).astype(F32)
        sig = _sigmoid(g_ref[...])
        val = v_ref[...]
        sc[0:halo, :] = vp_ref[tile - halo:, :] * _sigmoid(gp_ref[tile - halo:, :]) * (i > 0).astype(F32)
        sc[halo:, :] = val * sig
        du0 = _taps_bwd_x(sc_d, cw_ref, CONV_WIDTH, tile, jnp.zeros((tile, d), F32))
        _taps_bwd_w(du1, sc, dcw_ref, CONV_WIDTH, halo, tile, i)
        _acc(dcb_ref, i, jnp.sum(du1, axis=0, keepdims=True))
        _acc(dlg_ref, i, dlg)
        _acc(dlb_ref, i, dlb)
        dv_ref[...] = (du0 * sig).astype(dv_ref.dtype)
        dg_ref[...] = (du0 * val * sig * (1.0 - sig)).astype(dg_ref.dtype)

    return _call(body, name, (nt,),
                 [_rs(tile, d), _rs(tile, d, 0, 1, nt), _rs(tile, d), _rs(tile, d, 0, 1, nt),
                  _rs(tile, d, 0), _rs(tile, d, 1), _rs(tile, d, 0, -1), _rs(tile, d, 1, -1),
                  _ps(cw.shape), _ps((1, d)), _ps((1, d))],
                 [_rs(tile, d), _rs(tile, d), _ps(cw.shape), _ps((1, d)), _ps((1, d)), _ps((1, d))],
                 [S((n, d), MXU_DTYPE), S((n, d), MXU_DTYPE), S(cw.shape, F32), S((1, d), F32), S((1, d), F32), S((1, d), F32)],
                 scratch=[pltpu.VMEM((halo + tile, d), F32), pltpu.VMEM((tile + halo, d), F32)],
                 sem=("arbitrary",))(du, du, u1, u1, proj, proj, proj, proj, cw, lg, lb)


def ssm_conv_fwd(proj, dtr, sw, sb, dtb, name):
    n = proj.shape[0]
    w = sw.shape[1]
    inner = SSM_HEADS * HEAD_DIM
    tile = _row_tile(n, 256)
    halo = 8

    def body(x_ref, xp_ref, dtr_ref, sw_ref, sb_ref, dtb_ref, pre_ref, xs_ref, bc_ref, dt_ref, sc):
        i = pl.program_id(0)
        sc[0:halo, :] = xp_ref[tile - halo:, :] * (i > 0).astype(F32)
        sc[halo:, :] = x_ref[...]
        pre = _taps_fwd(sc, sw_ref, SSM_CONV, halo, tile, jnp.zeros((tile, w), F32) + sb_ref[...])
        pre_ref[...] = pre
        act = pre * _sigmoid(pre)
        xs_ref[...] = act[:, :inner]
        bc_ref[...] = act[:, inner:]
        dt = _softplus(dtr_ref[...] + dtb_ref[...])
        dt_ref[...] = jnp.where(_iota(dt.shape, 1) < SSM_HEADS, dt, 0.0)

    return _call(body, name, (n // tile,),
                 [_rs(tile, w, 2), _rs(tile, w, 2, -1), _rs(tile, LANE), _ps(sw.shape), _ps((1, w)), _ps((1, LANE))],
                 [_rs(tile, w), _rs(tile, inner), _rs(tile, w - inner), _rs(tile, LANE)],
                 [S((n, w), F32), S((n, inner), F32), S((n, w - inner), F32), S((n, LANE), F32)],
                 scratch=[pltpu.VMEM((halo + tile, w), F32)], sem=("arbitrary",))(proj, proj, dtr, sw, sb, dtb)


def ssm_conv_bwd(dxs, dbc, pre, proj, sw, name):
    n = proj.shape[0]
    w = sw.shape[1]
    inner = SSM_HEADS * HEAD_DIM
    tile = _row_tile(n, 256)
    halo = 8
    nt = n // tile

    def body(dxs_ref, dxsn_ref, dbc_ref, dbcn_ref, pre_ref, pren_ref, x_ref, xp_ref, sw_ref,
             dx_ref, dsw_ref, dsb_ref, sc, sc_d):
        i = pl.program_id(0)

        def silu_bwd(d_, p_):
            s = _sigmoid(p_)
            return d_ * s * (1.0 + p_ * (1.0 - s))

        sc_d[0:tile, :inner] = silu_bwd(dxs_ref[...], pre_ref[:, :inner])
        sc_d[0:tile, inner:] = silu_bwd(dbc_ref[...], pre_ref[:, inner:])
        last = (i < nt - 1).astype(F32)
        sc_d[tile:, :inner] = silu_bwd(dxsn_ref[0:halo, :], pren_ref[0:halo, :inner]) * last
        sc_d[tile:, inner:] = silu_bwd(dbcn_ref[0:halo, :], pren_ref[0:halo, inner:]) * last
        sc[0:halo, :] = xp_ref[tile - halo:, :] * (i > 0).astype(F32)
        sc[halo:, :] = x_ref[...]
        dpre = sc_d[0:tile, :]
        dx_ref[...] = _taps_bwd_x(sc_d, sw_ref, SSM_CONV, tile, jnp.zeros((tile, w), F32)).astype(dx_ref.dtype)
        _taps_bwd_w(dpre, sc, dsw_ref, SSM_CONV, halo, tile, i)
        _acc(dsb_ref, i, jnp.sum(dpre, axis=0, keepdims=True))

    return _call(body, name, (nt,),
                 [_rs(tile, inner), _rs(tile, inner, 0, 1, nt), _rs(tile, w - inner), _rs(tile, w - inner, 0, 1, nt),
                  _rs(tile, w), _rs(tile, w, 0, 1, nt), _rs(tile, w, 2), _rs(tile, w, 2, -1), _ps(sw.shape)],
                 [_rs(tile, w), _ps(sw.shape), _ps((1, w))],
                 [S((n, w), MXU_DTYPE), S(sw.shape, F32), S((1, w), F32)],
                 scratch=[pltpu.VMEM((halo + tile, w), F32), pltpu.VMEM((tile + halo, w), F32)],
                 sem=("arbitrary",))(dxs, dxs, dbc, dbc, pre, pre, proj, proj, sw)


def _ssd_prologue(dt_ref, dtT_ref, al_ref, alc_ref):
    row = _iota((CHUNK, CHUNK), 0)
    col = _iota((CHUNK, CHUNK), 1)
    dt = dt_ref[:, :SSM_HEADS]
    a_row = -jnp.exp(al_ref[:, :SSM_HEADS])
    a_col = -jnp.exp(alc_ref[...])
    cs = _01mm((row >= col).astype(F32), dt * a_row)
    csT = _mm01(dtT_ref[...] * a_col, (row <= col).astype(F32))
    return dt, a_row, cs, csT, row, col


def _decay(cs, csT, h, row, col):
    lm = jnp.exp(jnp.where(row >= col, cs[:, h:h + 1] - csT[h:h + 1, :], -1e30))
    lmT = jnp.exp(jnp.where(col >= row, csT[h:h + 1, :] - cs[:, h:h + 1], -1e30))
    return lm, lmT


def ssd_fwd(xs, bc, dt, dtT, alog_row, alog_col, name):
    n, width = xs.shape
    nc = n // CHUNK
    gw = width // SSM_GROUPS
    hpg = SSM_HEADS // SSM_GROUPS
    ns = SSM_STATE

    def body(xs_ref, bc_ref, dt_ref, dtT_ref, al_ref, alc_ref, y_ref, hs_ref, h_sc):
        i = pl.program_id(0)

        @pl.when(i == 0)
        def _():
            h_sc[...] = jnp.zeros_like(h_sc)

        dt, a_row, cs, csT, row, col = _ssd_prologue(dt_ref, dtT_ref, al_ref, alc_ref)
        indT = _head_indicator(width, SSM_HEADS, transposed=True)
        dt_full = _mm01(dt, indT)
        e_full = jnp.exp(_mm01(cs, indT))
        dte_full = jnp.exp(_mm01(cs[CHUNK - 1:CHUNK, :] - cs, indT))
        xt = xs_ref[...] * dt_full
        hs_ref[0] = h_sc[...]
        lo = _iota((CHUNK, 2 * HEAD_DIM), 1) < HEAD_DIM
        for g in range(SSM_GROUPS):
            bg = bc_ref[:, g * ns:(g + 1) * ns]
            cg = bc_ref[:, (SSM_GROUPS + g) * ns:(SSM_GROUPS + g + 1) * ns]
            gm = _mm(cg, bg, NT)
            hg = h_sc[g * gw:(g + 1) * gw, :]
            yoff = e_full[:, g * gw:(g + 1) * gw] * _mm(cg, hg, NT)
            for pr in range(hpg // 2):
                h0 = g * hpg + 2 * pr
                c0 = h0 * HEAD_DIM
                xp = xt[:, c0:c0 + 2 * HEAD_DIM]
                m0 = gm * _decay(cs, csT, h0, row, col)[0]
                m1 = gm * _decay(cs, csT, h0 + 1, row, col)[0]
                yd = jnp.where(lo, _mm(m0, xp), _mm(m1, xp))
                y_ref[:, c0:c0 + 2 * HEAD_DIM] = yd + yoff[:, 2 * pr * HEAD_DIM:(2 * pr + 2) * HEAD_DIM]
            sg = _mm(xt[:, g * gw:(g + 1) * gw] * dte_full[:, g * gw:(g + 1) * gw], bg, TN)
            for hh in range(hpg):
                h = g * hpg + hh
                r0 = h * HEAD_DIM
                h_sc[r0:r0 + HEAD_DIM, :] = (h_sc[r0:r0 + HEAD_DIM, :] * jnp.exp(csT[h:h + 1, CHUNK - 1:CHUNK])
                                             + sg[hh * HEAD_DIM:(hh + 1) * HEAD_DIM, :])

    bcw = bc.shape[1]
    return _call(body, name, (nc,),
                 [_rs(CHUNK, width), _rs(CHUNK, bcw), _rs(CHUNK, LANE), pl.BlockSpec((SSM_HEADS, CHUNK), lambda i: (0, i)),
                  _ps((1, LANE)), _ps((SSM_HEADS, 1))],
                 [_rs(CHUNK, width), pl.BlockSpec((1, width, ns), lambda i: (i, 0, 0))],
                 [S((n, width), F32), S((nc, width, ns), F32)],
                 scratch=[pltpu.VMEM((width, ns), F32)], sem=("arbitrary",))(xs, bc, dt, dtT, alog_row, alog_col)


def ssd_bwd(xs, bc, dt, dtT, alog_row, alog_col, hs, dy, dxs_skip, name):
    n, width = xs.shape
    nc = n // CHUNK
    gw = width // SSM_GROUPS
    hpg = SSM_HEADS // SSM_GROUPS
    ns = SSM_STATE
    bcw = bc.shape[1]

    def body(xs_ref, bc_ref, dt_ref, dtT_ref, al_ref, alc_ref, hs_ref, dy_ref, skip_ref,
             dxs_ref, dbc_ref, ddtr_ref, dal_ref, ddtb_ref, dh_sc, dxt_sc):
        i = pl.program_id(0)

        @pl.when(i == 0)
        def _():
            dh_sc[...] = jnp.zeros_like(dh_sc)

        dt, a_row, cs, csT, row, col = _ssd_prologue(dt_ref, dtT_ref, al_ref, alc_ref)
        indT = _head_indicator(width, SSM_HEADS, transposed=True)
        ind = _head_indicator(width, SSM_HEADS)
        dt_full = _mm01(dt, indT)
        e_full = jnp.exp(_mm01(cs, indT))
        cs_last = cs[CHUNK - 1:CHUNK, :]
        dte = jnp.exp(cs_last - cs)
        dte_full = _mm01(dte, indT)
        xs_ = xs_ref[...]
        xt = xs_ * dt_full
        dy_ = dy_ref[...]
        hmat = hs_ref[0]
        ds = dh_sc[...]
        lo = _iota((CHUNK, 2 * HEAD_DIM), 1) < HEAD_DIM
        head_lane = _iota((1, SSM_HEADS), 1)
        dcs = jnp.zeros((CHUNK, SSM_HEADS), F32)
        ddte = jnp.zeros((CHUNK, SSM_HEADS), F32)
        for g in range(SSM_GROUPS):
            sl = slice(g * gw, (g + 1) * gw)
            bg = bc_ref[:, g * ns:(g + 1) * ns]
            cg = bc_ref[:, (SSM_GROUPS + g) * ns:(SSM_GROUPS + g + 1) * ns]
            gm = _mm(cg, bg, NT)
            gmT = _mm(bg, cg, NT)
            hg = hmat[sl, :]
            dsg = ds[sl, :]
            dyg = dy_[:, sl]
            xtg = xt[:, sl]
            yoff = e_full[:, sl] * _mm(cg, hg, NT)
            edy = e_full[:, sl] * dyg
            bds = _mm(bg, dsg, NT)
            dxt_g = dte_full[:, sl] * bds
            ddte = ddte + _mm01(xtg * bds, ind[sl, :])
            dcs = dcs + _mm01(dyg * yoff, ind[sl, :])
            db = _mm(xtg * dte_full[:, sl], dsg)
            dc = _mm(edy, hg)
            dhc = _mm(edy, cg, TN)
            dgs = jnp.zeros((CHUNK, CHUNK), F32)
            dgTs = jnp.zeros((CHUNK, CHUNK), F32)
            for pr in range(hpg // 2):
                h0 = g * hpg + 2 * pr
                c0 = 2 * pr * HEAD_DIM
                xp = xtg[:, c0:c0 + 2 * HEAD_DIM]
                dyp = dyg[:, c0:c0 + 2 * HEAD_DIM]
                rr = []
                for h, half in ((h0, lo), (h0 + 1, jnp.logical_not(lo))):
                    lm, lmT = _decay(cs, csT, h, row, col)
                    xm = jnp.where(half, xp, 0.0)
                    dm = _mm(dyp, xm, NT)
                    dmT = _mm(xm, dyp, NT)
                    mT = gmT * lmT
                    z = jnp.sum(dm * (gm * lm), axis=1, keepdims=True) - jnp.sum(dmT * mT, axis=1, keepdims=True)
                    dcs = dcs + z * (head_lane == h).astype(F32)
                    dgs = dgs + dm * lm
                    dgTs = dgTs + dmT * lmT
                    rr.append(_mm(mT, dyp))
                dxt_sc[:, g * gw + c0:g * gw + c0 + 2 * HEAD_DIM] = jnp.where(lo, rr[0], rr[1]) + dxt_g[:, c0:c0 + 2 * HEAD_DIM]
            dbc_ref[:, g * ns:(g + 1) * ns] = db + _mm(dgTs, cg)
            dbc_ref[:, (SSM_GROUPS + g) * ns:(SSM_GROUPS + g + 1) * ns] = dc + _mm(dgs, bg)
            for hh in range(hpg):
                h = g * hpg + hh
                r0 = h * HEAD_DIM
                dh_sc[r0:r0 + HEAD_DIM, :] = (dhc[hh * HEAD_DIM:(hh + 1) * HEAD_DIM, :]
                                              + jnp.exp(csT[h:h + 1, CHUNK - 1:CHUNK]) * ds[r0:r0 + HEAD_DIM, :])
        t = ddte * dte
        per_head = jnp.sum(jnp.sum(ds * hmat, axis=1, keepdims=True) * ind, axis=0, keepdims=True)
        last_add = jnp.sum(t, axis=0, keepdims=True) + jnp.exp(cs_last) * per_head
        dcs = dcs - t + jnp.where(_iota((CHUNK, SSM_HEADS), 0) == CHUNK - 1, last_add, 0.0)
        dadt = _01mm((row <= col).astype(F32), dcs)
        dxt = dxt_sc[...]
        ddt = dadt * a_row + _mm01(dxt * xs_, ind)
        dxs_ref[...] = dxt * dt_full + skip_ref[...]
        ddtr = ddt * (1.0 - jnp.exp(-dt))
        ddtr_ref[...] = jnp.zeros_like(ddtr_ref)
        ddtr_ref[:, :SSM_HEADS] = ddtr.astype(ddtr_ref.dtype)
        _acc(dal_ref, i, jnp.sum(dadt * dt, axis=0, keepdims=True) * a_row)
        _acc(ddtb_ref, i, jnp.sum(ddtr, axis=0, keepdims=True))

    rev = lambda i: (nc - 1 - i, 0)
    return _call(body, name, (nc,),
                 [pl.BlockSpec((CHUNK, width), rev), pl.BlockSpec((CHUNK, bcw), rev), pl.BlockSpec((CHUNK, LANE), rev),
                  pl.BlockSpec((SSM_HEADS, CHUNK), lambda i: (0, nc - 1 - i)), _ps((1, LANE)), _ps((SSM_HEADS, 1)),
                  pl.BlockSpec((1, width, ns), lambda i: (nc - 1 - i, 0, 0)), pl.BlockSpec((CHUNK, width), rev),
                  pl.BlockSpec((CHUNK, width), rev)],
                 [pl.BlockSpec((CHUNK, width), rev), pl.BlockSpec((CHUNK, bcw), rev), pl.BlockSpec((CHUNK, LANE), rev),
                  _ps((1, SSM_HEADS)), _ps((1, SSM_HEADS))],
                 [S((n, width), F32), S((n, bcw), F32), S((n, LANE), MXU_DTYPE), S((1, SSM_HEADS), F32), S((1, SSM_HEADS), F32)],
                 scratch=[pltpu.VMEM((width, ns), F32), pltpu.VMEM((CHUNK, width), F32)],
                 sem=("arbitrary",))(xs, bc, dt, dtT, alog_row, alog_col, hs, dy, dxs_skip)


def ssm_gate_fwd(yssd, xs, proj, dfull, gamma, name):
    n, d = yssd.shape
    tile = _row_tile(n, 256)
    gw = d // SSM_GROUPS

    def body(y_ref, xs_ref, z_ref, df_ref, gm_ref, o_ref):
        z = z_ref[...]
        y2 = (y_ref[...] + df_ref[...] * xs_ref[...]) * (z * _sigmoid(z))
        for g in range(SSM_GROUPS):
            yg = y2[:, g * gw:(g + 1) * gw]
            r = lax.rsqrt(jnp.mean(yg * yg, axis=-1, keepdims=True) + EPS)
            o_ref[:, g * gw:(g + 1) * gw] = (yg * r * gm_ref[:, g * gw:(g + 1) * gw]).astype(o_ref.dtype)

    return _call(body, name, (n // tile,), [_rs(tile, d), _rs(tile, d), _rs(tile, d, 2), _ps((1, d)), _ps((1, d))],
                 _rs(tile, d), S((n, d), MXU_DTYPE), sem=("parallel",))(yssd, xs, proj, dfull, gamma)


def ssm_gate_bwd(dy3, yssd, xs, proj, dfull, gamma, name):
    n, d = yssd.shape
    tile = _row_tile(n, 256)
    gw = d // SSM_GROUPS

    def body(dy_ref, y_ref, xs_ref, z_ref, df_ref, gm_ref, dys_ref, dxs_ref, dz_ref, dgm_ref, dd_ref):
        i = pl.program_id(0)
        z = z_ref[...]
        s = _sigmoid(z)
        xs_ = xs_ref[...]
        y1 = y_ref[...] + df_ref[...] * xs_
        y2 = y1 * (z * s)
        dy_ = dy_ref[...].astype(F32)
        dgm = []
        dy2 = []
        for g in range(SSM_GROUPS):
            sl = slice(g * gw, (g + 1) * gw)
            dxg, dgg = _rms_bwd_math(y2[:, sl], gm_ref[:, sl], dy_[:, sl])
            dy2.append(dxg)
            dgm.append(dgg)
        dy2 = jnp.concatenate(dy2, axis=1)
        dy1 = dy2 * (z * s)
        dys_ref[...] = dy1
        dxs_ref[...] = dy1 * df_ref[...]
        dz_ref[...] = (dy2 * y1 * s * (1.0 + z * (1.0 - s))).astype(dz_ref.dtype)
        _acc(dgm_ref, i, jnp.concatenate(dgm, axis=1))
        colsum = jnp.broadcast_to(jnp.sum(dy1 * xs_, axis=0, keepdims=True), (8, d))
        _acc(dd_ref, i, _mm01(colsum, _head_indicator(d, SSM_HEADS))[0:1, :])

    return _call(body, name, (n // tile,),
                 [_rs(tile, d), _rs(tile, d), _rs(tile, d), _rs(tile, d, 2), _ps((1, d)), _ps((1, d))],
                 [_rs(tile, d), _rs(tile, d), _rs(tile, d), _ps((1, d)), _ps((1, SSM_HEADS))],
                 [S((n, d), F32), S((n, d), F32), S((n, d), MXU_DTYPE), S((1, d), F32), S((1, SSM_HEADS), F32)],
                 sem=("arbitrary",))(dy3, yssd, xs, proj, dfull, gamma)


def _rope128(x, cos, sin_signed):
    half = HEAD_DIM // 2
    lane = _iota(x.shape, 1)
    partner = jnp.where((lane % HEAD_DIM) < half, pltpu.roll(x, LANE - half, 1), pltpu.roll(x, half, 1))
    return x * cos + partner * sin_signed


def rope_fwd(qkv, cos, sin, name):
    n, w = qkv.shape
    qw = ATT_HEADS * HEAD_DIM
    kw = ATT_KV_HEADS * HEAD_DIM
    tile = _row_tile(n, 256)

    def body(x_ref, c_ref, s_ref, q_ref, k_ref, v_ref):
        c, s = c_ref[...], s_ref[...]
        for j in range(qw // LANE):
            q_ref[:, j * LANE:(j + 1) * LANE] = _rope128(x_ref[:, j * LANE:(j + 1) * LANE], c, s).astype(q_ref.dtype)
        for j in range(kw // LANE):
            k_ref[:, j * LANE:(j + 1) * LANE] = _rope128(x_ref[:, qw + j * LANE:qw + (j + 1) * LANE], c, s).astype(k_ref.dtype)
        v_ref[...] = x_ref[:, qw + kw:].astype(v_ref.dtype)

    return _call(body, name, (n // tile,), [_rs(tile, w), _rs(tile, LANE), _rs(tile, LANE)],
                 [_rs(tile, qw), _rs(tile, kw), _rs(tile, kw)],
                 [S((n, qw), MXU_DTYPE), S((n, kw), MXU_DTYPE), S((n, kw), MXU_DTYPE)], sem=("parallel",))(qkv, cos, sin)


def _attn_mask(i):
    row = _iota((WINDOW, 2 * WINDOW), 0)
    s = _iota((WINDOW, 2 * WINDOW), 1)
    return (s > row) & (s <= row + WINDOW) & ((s >= WINDOW) | (i > 0))


def _attn_heads():
    grp = ATT_HEADS // ATT_KV_HEADS
    return [(h, h // 2, h % 2, (h // grp) // 2, (h // grp) % 2) for h in range(ATT_HEADS)]


def attn_fwd(q, k, v, sinks, name):
    n, qw = q.shape
    kw = k.shape[1]
    nb = n // WINDOW
    scale = HEAD_DIM ** -0.5

    def body(q_ref, kc_ref, kp_ref, vc_ref, vp_ref, sk_ref, o_ref, lse_ref):
        i = pl.program_id(0)
        valid = _attn_mask(i)
        lo = _iota((WINDOW, LANE), 1) < HEAD_DIM
        k2 = jnp.concatenate([kp_ref[...], kc_ref[...]], axis=0)
        v2 = jnp.concatenate([vp_ref[...], vc_ref[...]], axis=0)
        lane1 = _iota((1, LANE), 1)
        lse = jnp.zeros((WINDOW, LANE), F32)
        res = {}
        for h, qb, qh, kb, kh in _attn_heads():
            qp = q_ref[:, qb * LANE:(qb + 1) * LANE]
            qm = jnp.where(lo if qh == 0 else jnp.logical_not(lo), qp, jnp.zeros_like(qp))
            kk = k2[:, kb * LANE:(kb + 1) * LANE]
            if kh != qh:
                kk = pltpu.roll(kk, HEAD_DIM, 1)
            logits = jnp.where(valid, _mm(qm, kk, NT) * scale, -1e30)
            sk = sk_ref[:, h:h + 1]
            m = jnp.maximum(jnp.max(logits, axis=-1, keepdims=True), sk)
            e = jnp.exp(logits - m)
            den = jnp.sum(e, axis=-1, keepdims=True) + jnp.exp(sk - m)
            lse = lse + (m + jnp.log(den)) * (lane1 == h).astype(F32)
            r = _mm(e / den, v2[:, kb * LANE:(kb + 1) * LANE])
            res[h] = r if kh == qh else pltpu.roll(r, HEAD_DIM, 1)
            if qh == 1:
                o_ref[:, qb * LANE:(qb + 1) * LANE] = jnp.where(lo, res[h - 1], res[h]).astype(o_ref.dtype)
        lse_ref[...] = lse

    return _call(body, name, (nb,),
                 [_rs(WINDOW, qw), _rs(WINDOW, kw), _rs(WINDOW, kw, 0, -1), _rs(WINDOW, kw), _rs(WINDOW, kw, 0, -1), _ps((1, LANE))],
                 [_rs(WINDOW, qw), _rs(WINDOW, LANE)], [S((n, qw), MXU_DTYPE), S((n, LANE), F32)],
                 sem=("parallel",))(q, k, k, v, v, sinks)


def attn_bwd(q, k, v, o, do, lse, sinks, name):
    n, qw = q.shape
    kw = k.shape[1]
    nb = n // WINDOW
    scale = HEAD_DIM ** -0.5

    def body(q_ref, kc_ref, kp_ref, vc_ref, vp_ref, o_ref, do_ref, lse_ref, sk_ref,
             dq_ref, dka_ref, dkb_ref, dva_ref, dvb_ref, dsk_ref):
        i = pl.program_id(0)
        valid = _attn_mask(i)
        lo = _iota((WINDOW, LANE), 1) < HEAD_DIM
        k2 = jnp.concatenate([kp_ref[...], kc_ref[...]], axis=0)
        v2 = jnp.concatenate([vp_ref[...], vc_ref[...]], axis=0)
        lane1 = _iota((1, LANE), 1)
        do_ = do_ref[...].astype(F32)
        delta = _mm01(do_ * o_ref[...].astype(F32), _head_indicator(qw, ATT_HEADS))
        dk2 = [jnp.zeros((2 * WINDOW, LANE), F32) for _ in range(kw // LANE)]
        dv2 = [jnp.zeros((2 * WINDOW, LANE), F32) for _ in range(kw // LANE)]
        dsk = jnp.zeros((1, LANE), F32)
        res = {}
        for h, qb, qh, kb, kh in _attn_heads():
            half = lo if qh == 0 else jnp.logical_not(lo)
            qp = q_ref[:, qb * LANE:(qb + 1) * LANE]
            qm = jnp.where(half, qp, jnp.zeros_like(qp))
            dop = do_[:, qb * LANE:(qb + 1) * LANE]
            dom = jnp.where(half, dop, 0.0)
            kk = k2[:, kb * LANE:(kb + 1) * LANE]
            vv = v2[:, kb * LANE:(kb + 1) * LANE]
            if kh != qh:
                kk = pltpu.roll(kk, HEAD_DIM, 1)
                vv = pltpu.roll(vv, HEAD_DIM, 1)
            logits = jnp.where(valid, _mm(qm, kk, NT) * scale, -1e30)
            lse_h = lse_ref[:, h:h + 1]
            p = jnp.exp(logits - lse_h)
            dl = delta[:, h:h + 1]
            ds = p * (_mm(dom, vv, NT) - dl) * scale
            psink = jnp.exp(sk_ref[:, h:h + 1] - lse_h)
            dsk = dsk - jnp.sum(psink * dl, axis=0, keepdims=True) * (lane1 == h).astype(F32)
            res[h] = _mm(ds, kk)
            dkh = _mm(ds, qm, TN)
            dvh = _mm(p, dom, TN)
            if kh != qh:
                dkh = pltpu.roll(dkh, HEAD_DIM, 1)
                dvh = pltpu.roll(dvh, HEAD_DIM, 1)
            dk2[kb] = dk2[kb] + dkh
            dv2[kb] = dv2[kb] + dvh
            if qh == 1:
                dq_ref[:, qb * LANE:(qb + 1) * LANE] = jnp.where(lo, res[h - 1], res[h])
        for kb in range(kw // LANE):
            dkb_ref[:, kb * LANE:(kb + 1) * LANE] = dk2[kb][0:WINDOW, :]
            dka_ref[:, kb * LANE:(kb + 1) * LANE] = dk2[kb][WINDOW:, :]
            dvb_ref[:, kb * LANE:(kb + 1) * LANE] = dv2[kb][0:WINDOW, :]
            dva_ref[:, kb * LANE:(kb + 1) * LANE] = dv2[kb][WINDOW:, :]
        _acc(dsk_ref, i, dsk)

    return _call(body, name, (nb,),
                 [_rs(WINDOW, qw), _rs(WINDOW, kw), _rs(WINDOW, kw, 0, -1), _rs(WINDOW, kw), _rs(WINDOW, kw, 0, -1),
                  _rs(WINDOW, qw), _rs(WINDOW, qw), _rs(WINDOW, LANE), _ps((1, LANE))],
                 [_rs(WINDOW, qw)] + [_rs(WINDOW, kw)] * 4 + [_ps((1, LANE))],
                 [S((n, qw), F32)] + [S((n, kw), F32)] * 4 + [S((1, LANE), F32)],
                 sem=("arbitrary",))(q, k, k, v, v, o, do, lse, sinks)


def attn_grad_merge(dq, dka, dkb, dva, dvb, cos, sin, name):
    n, qw = dq.shape
    kw = dka.shape[1]
    nb = n // WINDOW
    w = qw + 2 * kw

    def body(dq_ref, dka_ref, dkb_ref, dva_ref, dvb_ref, c_ref, s_ref, o_ref, db_ref):
        i = pl.program_id(0)
        c, s = c_ref[...], -s_ref[...]
        nxt = (i < nb - 1).astype(F32)

        @pl.when(i == 0)
        def _():
            db_ref[...] = jnp.zeros_like(db_ref)

        def put(c0, val):
            o_ref[:, c0:c0 + val.shape[1]] = val.astype(o_ref.dtype)
            db_ref[:, c0:c0 + val.shape[1]] += jnp.sum(val, axis=0, keepdims=True)

        for j in range(qw // LANE):
            put(j * LANE, _rope128(dq_ref[:, j * LANE:(j + 1) * LANE], c, s))
        for j in range(kw // LANE):
            sl = slice(j * LANE, (j + 1) * LANE)
            put(qw + j * LANE, _rope128(dka_ref[:, sl] + dkb_ref[:, sl] * nxt, c, s))
        put(qw + kw, dva_ref[...] + dvb_ref[...] * nxt)

    return _call(body, name, (nb,),
                 [_rs(WINDOW, qw), _rs(WINDOW, kw), _rs(WINDOW, kw, 0, 1, nb), _rs(WINDOW, kw), _rs(WINDOW, kw, 0, 1, nb),
                  _rs(WINDOW, LANE), _rs(WINDOW, LANE)],
                 [_rs(WINDOW, w), _ps((1, w))], [S((n, w), MXU_DTYPE), S((1, w), F32)],
                 sem=("arbitrary",))(dq, dka, dkb, dva, dvb, cos, sin)


def _row(v):
    return v.reshape(1, -1)


def _pad_lanes(v, width=LANE):
    return jnp.pad(v.reshape(1, -1), ((0, 0), (0, width - v.size)))


def ffn_fwd(h, g, w_in, w_out, tag):
    xn = rms_fwd(h, _row(g), f"{tag}_rms")
    u = matmul(xn, w_in, "nn", f"{tag}_in")
    hm = swiglu_fwd(u, f"{tag}_act")
    return matmul(hm, w_out, "nn", f"{tag}_out", scale=0.5, res=h), (h, xn, u, hm)


def ffn_bwd(dh, g, w_in, w_out, saved, tag, colsum=False):
    h, xn, u, hm = saved
    dw_out = matmul(hm, dh, "tn", f"{tag}_dwout", scale=0.5)
    dhm = matmul(dh, w_out, "nt", f"{tag}_dhm", scale=0.5)
    du = swiglu_bwd(u, dhm, f"{tag}_dact")
    dw_in = matmul(xn, du, "tn", f"{tag}_dwin")
    dxn = matmul(du, w_in, "nt", f"{tag}_dxn")
    outs = rms_bwd(h, _row(g), dxn, dh, f"{tag}_drms", colsum=colsum)
    return (outs[0], dw_in, dw_out, outs[1].reshape(-1)) + ((outs[2],) if colsum else ())


def _hyb_params(w):
    d = w["conv_dw_b"].size
    inner = SSM_HEADS * HEAD_DIM
    main = 3 * d + w["ssm_conv_b"].size
    return dict(
        w_main=w["hyb_w_in"][:, :main], w_dt=jnp.pad(w["hyb_w_in"][:, main:], ((0, 0), (0, LANE - SSM_HEADS))),
        cw=jnp.pad(w["conv_dw_w"], ((0, 32 - CONV_WIDTH), (0, 0))), cb=_row(w["conv_dw_b"]),
        lg=_row(w["conv_ln_g"]), lb=_row(w["conv_ln_b"]),
        sw=jnp.pad(w["ssm_conv_w"], ((0, 8 - SSM_CONV), (0, 0))), sb=_row(w["ssm_conv_b"]),
        dtb=_pad_lanes(w["ssm_dt_bias"]), al_row=_pad_lanes(w["ssm_a_log"]), al_col=w["ssm_a_log"].reshape(-1, 1),
        dfull=_row(jnp.repeat(w["ssm_d"], HEAD_DIM)), gamma=_row(w["ssm_norm"]),
        wo_top=w["hyb_w_out"][:d], wo_bot=w["hyb_w_out"][d:], d=d, inner=inner, main=main)


def hyb_fwd(h, w, tag):
    q = _hyb_params(w)
    xn = rms_fwd(h, _row(w["norm_mix"]), f"{tag}_rms")
    proj = matmul(xn, q["w_main"], "nn", f"{tag}_in")
    dtr = matmul(xn, q["w_dt"], "nn", f"{tag}_in_dt")
    u, u1 = conv_group_fwd(proj, q["cw"], q["cb"], q["lg"], q["lb"], f"{tag}_conv")
    pre, xs, bc, dt = ssm_conv_fwd(proj, dtr, q["sw"], q["sb"], q["dtb"], f"{tag}_sconv")
    dtT = dt[:, :SSM_HEADS].T
    yssd, hs = ssd_fwd(xs, bc, dt, dtT, q["al_row"], q["al_col"], f"{tag}_ssd")
    y = ssm_gate_fwd(yssd, xs, proj, q["dfull"], q["gamma"], f"{tag}_gate")
    h2 = matmul(u, q["wo_top"], "nn", f"{tag}_out_a", res=h)
    h2 = matmul(y, q["wo_bot"], "nn", f"{tag}_out_b", res=h2)
    return h2, (h, xn, proj, u, u1, pre, xs, bc, dt, dtT, yssd, hs, y)


def hyb_bwd(dh, w, saved, tag):
    q = _hyb_params(w)
    h, xn, proj, u, u1, pre, xs, bc, dt, dtT, yssd, hs, y = saved
    du = matmul(dh, q["wo_top"], "nt", f"{tag}_du")
    dy3 = matmul(dh, q["wo_bot"], "nt", f"{tag}_dy")
    dwo = jnp.concatenate([matmul(u, dh, "tn", f"{tag}_dwo_a"), matmul(y, dh, "tn", f"{tag}_dwo_b")], axis=0)
    dval, dgate, dcw, dcb, dlg, dlb = conv_group_bwd(du, u1, proj, q["cw"], q["lg"], q["lb"], f"{tag}_dconv")
    dyssd, dxs_skip, dz, dgamma, dd = ssm_gate_bwd(dy3, yssd, xs, proj, q["dfull"], q["gamma"], f"{tag}_dgate")
    dxs, dbc, ddtr, dalog, ddtb = ssd_bwd(xs, bc, dt, dtT, q["al_row"], q["al_col"], hs, dyssd, dxs_skip, f"{tag}_dssd")
    dxbc, dsw, dsb = ssm_conv_bwd(dxs, dbc, pre, proj, q["sw"], f"{tag}_dsconv")
    dproj = jnp.concatenate([dval, dgate, dz, dxbc], axis=1)
    dw_in = jnp.concatenate([matmul(xn, dproj, "tn", f"{tag}_dwin"),
                             matmul(xn, ddtr, "tn", f"{tag}_dwin_dt")[:, :SSM_HEADS]], axis=1)
    dxn = matmul(dproj, q["w_main"], "nt", f"{tag}_dxn")
    dxn = matmul(ddtr, q["w_dt"], "nt", f"{tag}_dxn_dt", res=dxn)
    dh2, dg = rms_bwd(h, _row(w["norm_mix"]), dxn, dh, f"{tag}_drms")
    grads = dict(norm_mix=dg.reshape(-1), hyb_w_in=dw_in, conv_dw_w=dcw[:CONV_WIDTH], conv_dw_b=dcb.reshape(-1),
                 conv_ln_g=dlg.reshape(-1), conv_ln_b=dlb.reshape(-1), ssm_conv_w=dsw[:SSM_CONV], ssm_conv_b=dsb.reshape(-1),
                 ssm_dt_bias=ddtb.reshape(-1), ssm_a_log=dalog.reshape(-1), ssm_d=dd.reshape(-1), ssm_norm=dgamma.reshape(-1),
                 hyb_w_out=dwo)
    return dh2, grads


def rope_tables(n):
    half = HEAD_DIM // 2
    inv = ROPE_THETA ** (-jnp.arange(0, HEAD_DIM, 2, dtype=F32) / HEAD_DIM)
    ang = jnp.arange(n, dtype=F32)[:, None] * inv[None, :]
    cos, sin = jnp.cos(ang), jnp.sin(ang)
    reps = LANE // HEAD_DIM
    return jnp.tile(jnp.concatenate([cos, cos], axis=1), (1, reps)), jnp.tile(jnp.concatenate([-sin, sin], axis=1), (1, reps))


def att_fwd(h, w, tables, tag):
    cos, sin = tables
    xn = rms_fwd(h, _row(w["norm_mix"]), f"{tag}_rms")
    qkv = matmul(xn, w["att_w_qkv"], "nn", f"{tag}_qkv", bias=_row(w["att_b_qkv"]))
    q, k, v = rope_fwd(qkv, cos, sin, f"{tag}_rope")
    sinks = _pad_lanes(w["att_sinks"])
    o, lse = attn_fwd(q, k, v, sinks, f"{tag}_attn")
    h2 = matmul(o, w["att_w_o"], "nn", f"{tag}_o", bias=_row(w["att_b_o"]), res=h)
    return h2, (h, xn, q, k, v, o, lse, sinks)


def att_bwd(dh, dh_colsum, w, saved, tables, tag):
    cos, sin = tables
    h, xn, q, k, v, o, lse, sinks = saved
    do = matmul(dh, w["att_w_o"], "nt", f"{tag}_do")
    dwo = matmul(o, dh, "tn", f"{tag}_dwo")
    dq, dka, dkb, dva, dvb, dsk = attn_bwd(q, k, v, o, do, lse, sinks, f"{tag}_dattn")
    dqkv, dbqkv = attn_grad_merge(dq, dka, dkb, dva, dvb, cos, sin, f"{tag}_drope")
    dwqkv = matmul(xn, dqkv, "tn", f"{tag}_dwqkv")
    dxn = matmul(dqkv, w["att_w_qkv"], "nt", f"{tag}_dxn")
    dh2, dg = rms_bwd(h, _row(w["norm_mix"]), dxn, dh, f"{tag}_drms")
    grads = dict(norm_mix=dg.reshape(-1), att_w_qkv=dwqkv, att_b_qkv=dbqkv.reshape(-1), att_sinks=dsk[0, :ATT_HEADS],
                 att_w_o=dwo, att_b_o=dh_colsum.reshape(-1))
    return dh2, grads


def ple_block_fwd(h, pe, w, tag):
    xn = rms_fwd(h, _row(w["ple_norm"]), f"{tag}_rms")
    gl = matmul(xn, w["ple_gate_w"], "nn", f"{tag}_gate")
    pp = matmul(pe, w["ple_proj_w"], "nn", f"{tag}_proj")
    return ple_fwd(h, gl, pp, f"{tag}_mix"), (h, xn, gl, pp, pe)


def ple_block_bwd(dh, w, saved, tag):
    h, xn, gl, pp, pe = saved
    dpp, dgl = ple_bwd(dh, gl, pp, f"{tag}_dmix")
    dwp = matmul(pe, dpp, "tn", f"{tag}_dwp")
    dwg = matmul(xn, dgl, "tn", f"{tag}_dwg")
    dxn = matmul(dgl, w["ple_gate_w"], "nt", f"{tag}_dxn")
    dh2, dg = rms_bwd(h, _row(w["ple_norm"]), dxn, dh, f"{tag}_drms")
    return dh2, dict(ple_norm=dg.reshape(-1), ple_gate_w=dwg, ple_proj_w=dwp)


PER_LAYER = ("norm_ffn1", "ffn1_w_in", "ffn1_w_out", "norm_mix", "norm_ffn2", "ffn2_w_in", "ffn2_w_out",
             "ple_norm", "ple_gate_w", "ple_proj_w")
EVEN_ONLY = ("hyb_w_in", "conv_dw_w", "conv_dw_b", "conv_ln_g", "conv_ln_b", "ssm_conv_w", "ssm_conv_b",
             "ssm_dt_bias", "ssm_a_log", "ssm_d", "ssm_norm", "hyb_w_out")
ODD_ONLY = ("att_w_qkv", "att_b_qkv", "att_sinks", "att_w_o", "att_b_o")


def _layer_weights(full, i):
    w = {k: full[k][i] for k in PER_LAYER}
    for k in (EVEN_ONLY if i % 2 == 0 else ODD_ONLY):
        w[k] = full[k][i // 2]
    return w


def trunk_fwd_bwd(x, pe, target, full):
    depth = full["norm_ffn1"].shape[0]
    tables = rope_tables(x.shape[0])
    h = x
    saved = []
    for i in range(depth):
        w = _layer_weights(full, i)
        h, s1 = ffn_fwd(h, w["norm_ffn1"], w["ffn1_w_in"], w["ffn1_w_out"], f"l{i}_ffn1")
        if i % 2 == 0:
            h, s2 = hyb_fwd(h, w, f"l{i}_hyb")
        else:
            h, s2 = att_fwd(h, w, tables, f"l{i}_att")
        h, s3 = ffn_fwd(h, w["norm_ffn2"], w["ffn2_w_in"], w["ffn2_w_out"], f"l{i}_ffn2")
        h, s4 = ple_block_fwd(h, pe[i], w, f"l{i}_ple")
        saved.append((s1, s2, s3, s4))
    dh, dgf, loss = loss_head(h, _row(full["final_norm"]), target, "loss_head")
    grads = {k: [None] * full[k].shape[0] for k in full if k != "final_norm"}
    grads["final_norm"] = dgf.reshape(-1)
    for i in reversed(range(depth)):
        w = _layer_weights(full, i)
        s1, s2, s3, s4 = saved[i]
        dh, g = ple_block_bwd(dh, w, s4, f"l{i}_ple")
        odd = i % 2 == 1
        out = ffn_bwd(dh, w["norm_ffn2"], w["ffn2_w_in"], w["ffn2_w_out"], s3, f"l{i}_ffn2", colsum=odd)
        dh = out[0]
        g.update(ffn2_w_in=out[1], ffn2_w_out=out[2], norm_ffn2=out[3])
        if odd:
            dh, gm = att_bwd(dh, out[4], w, s2, tables, f"l{i}_att")
        else:
            dh, gm = hyb_bwd(dh, w, s2, f"l{i}_hyb")
        g.update(gm)
        out = ffn_bwd(dh, w["norm_ffn1"], w["ffn1_w_in"], w["ffn1_w_out"], s1, f"l{i}_ffn1")
        dh = out[0]
        g.update(ffn1_w_in=out[1], ffn1_w_out=out[2], norm_ffn1=out[3])
        for k, v in g.items():
            grads[k][i if k in PER_LAYER else i // 2] = v
    return loss, dh, grads


ANY = pl.BlockSpec(memory_space=pl.ANY)


def _me():
    return lax.axis_index("x"), lax.axis_index("y"), lax.axis_index("c")


def _flip(v, f):
    return 1 - v if f else v


def _remote(src, dst, send_sems, recv_sems, k, dev):
    return pltpu.make_async_remote_copy(src_ref=src, dst_ref=dst, send_sem=send_sems.at[k], recv_sem=recv_sems.at[k],
                                        device_id=dev, device_id_type=MESH)


CHIP_FLIPS = ((1, 0), (0, 1), (1, 1))
DEV_FLIPS = tuple((fx, fy, fc) for fx in (0, 1) for fy in (0, 1) for fc in (0, 1))[1:]


def all_gather_chips(x, name):
    r, l = x.shape
    half = r // 2
    assert r % 2 == 0

    def body(x_ref, out_ref, send_sems, recv_sems):
        mx, my, mc = _me()
        chip = 2 * mx + my

        def rows(ch, hc):
            return out_ref.at[ch, pl.ds(hc * half, half), :]

        src = x_ref.at[pl.ds(mc * half, half), :]
        peers = [(_flip(mx, fx), _flip(my, fy)) for fx, fy in CHIP_FLIPS]
        first = [_remote(src, rows(chip, mc), send_sems, recv_sems, j, (px, py, mc)) for j, (px, py) in enumerate(peers)]
        for cp in first:
            cp.start()
        passed = []
        for j, (px, py) in enumerate(peers):
            landed = rows(2 * px + py, mc)
            _remote(src, landed, send_sems, recv_sems, j, (px, py, mc)).wait_recv()
            fw = _remote(landed, landed, send_sems, recv_sems, 3 + j, (mx, my, 1 - mc))
            fw.start()
            passed.append(fw)
        for j, (px, py) in enumerate(peers):
            _remote(src, rows(2 * px + py, 1 - mc), send_sems, recv_sems, 3 + j, (mx, my, 1 - mc)).wait_recv()
        for cp in first + passed:
            cp.wait_send()

    out = pl.pallas_call(
        body, name=name, out_shape=S((N_CHIPS, r, l), x.dtype), in_specs=[ANY], out_specs=ANY,
        scratch_shapes=[pltpu.SemaphoreType.DMA((6,)), pltpu.SemaphoreType.DMA((6,))])(x)
    chip = 2 * lax.axis_index("x") + lax.axis_index("y")
    return lax.dynamic_update_slice_in_dim(out, x[None], chip, axis=0)


def all_gather_devices(v, name):
    r, l = v.shape

    def body(v_ref, out_ref, send_sems, recv_sems):
        mx, my, mc = _me()
        me = 4 * mx + 2 * my + mc
        peers = [(_flip(mx, fx), _flip(my, fy), _flip(mc, fc)) for fx, fy, fc in DEV_FLIPS]
        sends = [_remote(v_ref, out_ref.at[me], send_sems, recv_sems, j, p) for j, p in enumerate(peers)]
        for cp in sends:
            cp.start()
        for j, (px, py, pc) in enumerate(peers):
            _remote(v_ref, out_ref.at[4 * px + 2 * py + pc], send_sems, recv_sems, j, (px, py, pc)).wait_recv()
        for cp in sends:
            cp.wait_send()

    out = pl.pallas_call(
        body, name=name, out_shape=S((N_DEV, r, l), v.dtype), in_specs=[ANY], out_specs=ANY,
        scratch_shapes=[pltpu.SemaphoreType.DMA((7,)), pltpu.SemaphoreType.DMA((7,))])(v)
    me = 4 * lax.axis_index("x") + 2 * lax.axis_index("y") + lax.axis_index("c")
    return lax.dynamic_update_slice_in_dim(out, v[None], me, axis=0)


def sum_devices(g8, name):
    nd, r, l = g8.shape
    tile = r
    for t in (512, 256, 128, 64, 32, 16, 8):
        if r % t == 0:
            tile = t
            break

    def body(g_ref, o_ref):
        acc = g_ref[0]
        for d in range(1, nd):
            acc = acc + g_ref[d]
        o_ref[...] = acc

    return _call(body, name, (r // tile,), [pl.BlockSpec((nd, tile, l), lambda i: (0, i, 0))], _rs(tile, l), S((r, l), F32),
                 sem=("parallel",))(g8)


def exchange_halves(g4, name):
    nch, r, l = g4.shape
    half = r // 2

    def body(g_ref, out_ref, send_sems, recv_sems):
        mx, my, mc = _me()
        sib = (mx, my, 1 - mc)
        cps = [_remote(g_ref.at[j, pl.ds((1 - mc) * half, half), :], out_ref.at[j], send_sems, recv_sems, j, sib)
               for j in range(nch)]
        for cp in cps:
            cp.start()
        for cp in cps:
            cp.wait_recv()
        for cp in cps:
            cp.wait_send()

    return pl.pallas_call(
        body, name=name, out_shape=S((nch, half, l), g4.dtype), in_specs=[ANY], out_specs=ANY,
        scratch_shapes=[pltpu.SemaphoreType.DMA((nch,)), pltpu.SemaphoreType.DMA((nch,))])(g4)


def add_halves(g4, got, name):
    nch, r, l = g4.shape
    half = r // 2
    tile = _pick_rows(half)
    nt = half // tile

    def body(g_ref, r_ref, a_ref, own_ref):
        j = pl.program_id(1)
        chip = 2 * lax.axis_index("x") + lax.axis_index("y")
        val = g_ref[0] + r_ref[0]
        a_ref[0] = val.astype(a_ref.dtype)

        @pl.when(j == chip)
        def _():
            own_ref[...] = val

    return pl.pallas_call(
        body, name=name, grid=(nt, nch),
        in_specs=[pl.BlockSpec((1, tile, l), lambda i, j: (j, lax.axis_index("c") * nt + i, 0)),
                  pl.BlockSpec((1, tile, l), lambda i, j: (j, i, 0))],
        out_specs=[pl.BlockSpec((1, tile, l), lambda i, j: (j, i, 0)), pl.BlockSpec((tile, l), lambda i, j: (i, 0))],
        out_shape=[S((nch, half, l), MXU_DTYPE), S((half, l), F32)],
        compiler_params=pltpu.CompilerParams(dimension_semantics=("parallel", "arbitrary"), vmem_limit_bytes=VMEM_LIMIT))(g4, got)


def _pick_rows(r, cap=512):
    for t in (512, 256, 128, 64, 32, 16):
        if t <= cap and r % t == 0:
            return t
    return r


def exchange_chips(a, name):
    nch, h, l = a.shape

    def body(a_ref, out_ref, send_sems, recv_sems):
        mx, my, mc = _me()
        peers = [(_flip(mx, fx), _flip(my, fy)) for fx, fy in CHIP_FLIPS]
        cps = [_remote(a_ref.at[2 * px + py], out_ref.at[j], send_sems, recv_sems, j, (px, py, mc))
               for j, (px, py) in enumerate(peers)]
        for cp in cps:
            cp.start()
        for cp in cps:
            cp.wait_recv()
        for cp in cps:
            cp.wait_send()

    return pl.pallas_call(
        body, name=name, out_shape=S((3, h, l), a.dtype), in_specs=[ANY], out_specs=ANY,
        scratch_shapes=[pltpu.SemaphoreType.DMA((3,)), pltpu.SemaphoreType.DMA((3,))])(a)


def add_chips(own, got, name):
    h, l = own.shape
    tile = _pick_rows(h)

    def body(o_ref, g_ref, out_ref):
        out_ref[...] = ((o_ref[...] + g_ref[0].astype(F32)) + g_ref[1].astype(F32)) + g_ref[2].astype(F32)

    nt = h // tile
    return _call(body, name, (nt,), [_rs(tile, l), pl.BlockSpec((3, tile, l), lambda i: (0, i, 0))],
                 pl.BlockSpec((tile, l), lambda i: (lax.axis_index("c") * nt + i, 0)),
                 S((2 * h, l), F32), sem=("parallel",))(own, got)


def join_halves(buf, name):
    h2, l = buf.shape
    h = h2 // 2

    def body(in_ref, out_ref, send_sems, recv_sems):
        del in_ref
        mx, my, mc = _me()
        sib = (mx, my, 1 - mc)
        cp = _remote(out_ref.at[pl.ds(mc * h, h), :], out_ref.at[pl.ds(mc * h, h), :], send_sems, recv_sems, 0, sib)
        cp.start()
        _remote(out_ref.at[pl.ds(mc * h, h), :], out_ref.at[pl.ds((1 - mc) * h, h), :], send_sems, recv_sems, 0, sib).wait_recv()
        cp.wait_send()

    return pl.pallas_call(
        body, name=name, out_shape=S((h2, l), buf.dtype), in_specs=[ANY], out_specs=ANY, input_output_aliases={0: 0},
        scratch_shapes=[pltpu.SemaphoreType.DMA((1,)), pltpu.SemaphoreType.DMA((1,))])(buf)


def reduce_scatter(g4, tag):
    got = exchange_halves(g4, f"{tag}_d2d")
    a, own = add_halves(g4, got, f"{tag}_add1")
    got2 = exchange_chips(a, f"{tag}_ici")
    part = add_chips(own, got2, f"{tag}_add2")
    return join_halves(part, f"{tag}_join")


PACK_L = 1024
BIG_ROW_MULT = 512


def _pack(arrs, dtype, row_mult, lead=None):
    lead_shape = () if lead is None else arrs[0].shape[:lead]
    flat = jnp.concatenate([a.astype(dtype).reshape(lead_shape + (-1,)) for a in arrs], axis=-1)
    n = flat.shape[-1]
    unit = row_mult * PACK_L
    total = -(-n // unit) * unit
    flat = jnp.pad(flat, [(0, 0)] * len(lead_shape) + [(0, total - n)])
    return flat.reshape(lead_shape + (total // PACK_L, PACK_L))


def _unpack(packed, shapes, lead=None):
    lead_shape = () if lead is None else packed.shape[:lead]
    flat = packed.reshape(lead_shape + (-1,))
    out, off = [], 0
    for shp in shapes:
        n = int(np.prod(shp))
        out.append(flat[..., off:off + n].reshape(lead_shape + tuple(shp)))
        off += n
    return out


def _to_full(gathered, axis):
    t = jnp.moveaxis(gathered, 0, axis)
    shp = t.shape
    return t.reshape(shp[:axis] + (shp[axis] * shp[axis + 1],) + shp[axis + 2:])


def _to_chip_major(full, axis):
    shp = full.shape
    t = full.reshape(shp[:axis] + (N_CHIPS, shp[axis] // N_CHIPS) + shp[axis + 1:])
    return jnp.moveaxis(t, axis, 0)


WEIGHTS = ("norm_ffn1", "ffn1_w_in", "ffn1_w_out", "norm_mix", "norm_ffn2", "ffn2_w_in", "ffn2_w_out", "ple_norm",
           "ple_gate_w", "ple_proj_w", "hyb_w_in", "conv_dw_w", "conv_dw_b", "conv_ln_g", "conv_ln_b", "ssm_conv_w",
           "ssm_conv_b", "ssm_dt_bias", "ssm_a_log", "ssm_d", "ssm_norm", "hyb_w_out", "att_w_qkv", "att_b_qkv",
           "att_sinks", "att_w_o", "att_b_o", "final_norm")
SHARD_AXIS = dict(ffn1_w_in=2, ffn1_w_out=1, ffn2_w_in=2, ffn2_w_out=1, ple_gate_w=1, ple_proj_w=2, hyb_w_in=2,
                  conv_dw_w=2, ssm_conv_w=2, hyb_w_out=1, att_w_qkv=2, att_b_qkv=1, att_w_o=1, att_b_o=1)
BIG = ("ffn1_w_in", "ffn1_w_out", "ffn2_w_in", "ffn2_w_out", "ple_gate_w", "ple_proj_w", "hyb_w_in", "hyb_w_out",
       "att_w_qkv", "att_w_o")
ODD_WIDTH = "hyb_w_in"
BIG_FLAT = tuple(k for k in BIG if k != ODD_WIDTH)
SMALL_SHARDED = ("conv_dw_w", "ssm_conv_w", "att_b_qkv", "att_b_o")
SMALL = tuple(k for k in WEIGHTS if k not in BIG)


def _step(x, p, target, w, m, v):
    mx, my = lax.axis_index("x"), lax.axis_index("y")
    chip = 2 * mx + my

    big_g = all_gather_chips(_pack([w[k] for k in BIG_FLAT], MXU_DTYPE, BIG_ROW_MULT), "gather_weights")
    full = {k: _to_full(g, SHARD_AXIS[k]) for k, g in zip(BIG_FLAT, _unpack(big_g, [w[k].shape for k in BIG_FLAT], lead=1))}
    full[ODD_WIDTH] = _to_full(all_gather_chips(w[ODD_WIDTH][0].astype(MXU_DTYPE), "gather_hyb_in"), 1)[None]
    small_g = all_gather_devices(_pack([w[k] for k in SMALL_SHARDED], F32, 8), "gather_small")[0::2]
    for k, g in zip(SMALL_SHARDED, _unpack(small_g, [w[k].shape for k in SMALL_SHARDED], lead=1)):
        full[k] = _to_full(g, SHARD_AXIS[k])
    for k in WEIGHTS:
        full.setdefault(k, w[k])

    loss, dx, grads = trunk_fwd_bwd(x[0], p[:, 0], target[0], full)
    grads = {k: (g if k == "final_norm" else jnp.stack(g)) for k, g in grads.items()}

    big4 = _pack([_to_chip_major(grads[k], SHARD_AXIS[k]) for k in BIG_FLAT], F32, BIG_ROW_MULT, lead=1)
    big_sum = reduce_scatter(big4, "grads")
    g_out = dict(zip(BIG_FLAT, _unpack(big_sum, [w[k].shape for k in BIG_FLAT])))
    g_out[ODD_WIDTH] = reduce_scatter(_to_chip_major(grads[ODD_WIDTH][0], 1), "grads_hyb_in")[None]
    vec = _pack([loss[0:1, 0:1]] + [grads[k] for k in SMALL], F32, 8)
    vec = sum_devices(all_gather_devices(vec, "gather_vectors"), "sum_vectors")
    parts = _unpack(vec, [(1, 1)] + [grads[k].shape for k in SMALL])
    loss_out = parts[0].reshape(())
    for k, g in zip(SMALL, parts[1:]):
        if k in SHARD_AXIS:
            ax = SHARD_AXIS[k]
            g = lax.dynamic_slice_in_dim(g, chip * w[k].shape[ax], w[k].shape[ax], axis=ax)
        g_out[k] = g

    delta, new_m, new_v = {}, {}, {}
    for k in BIG:
        shp = w[k].shape
        two_d = lambda a: a.reshape(-1, shp[-1])
        d_, m_, v_ = adamw(two_d(w[k]), two_d(g_out[k]), two_d(m[k]), two_d(v[k]), f"adamw_{k}")
        delta[k], new_m[k], new_v[k] = d_.reshape(shp), m_.reshape(shp), v_.reshape(shp)
    shapes = [w[k].shape for k in SMALL]
    packed = [_pack([src[k] for k in SMALL], F32, 8) for src in (w, g_out, m, v)]
    outs = adamw(*packed, "adamw_small")
    for dst, o in zip((delta, new_m, new_v), outs):
        for k, a in zip(SMALL, _unpack(o, shapes)):
            dst[k] = a
    return ((loss_out, dx[None]) + tuple(g_out[k] for k in WEIGHTS) + tuple(delta[k] for k in WEIGHTS)
            + tuple(new_m[k] for k in WEIGHTS) + tuple(new_v[k] for k in WEIGHTS))


def kernel(x, p, norm_ffn1, ffn1_w_in, ffn1_w_out, norm_mix, norm_ffn2, ffn2_w_in, ffn2_w_out, ple_norm, ple_gate_w, ple_proj_w, hyb_w_in, conv_dw_w, conv_dw_b, conv_ln_g, conv_ln_b, ssm_conv_w, ssm_conv_b, ssm_dt_bias, ssm_a_log, ssm_d, ssm_norm, hyb_w_out, att_w_qkv, att_b_qkv, att_sinks, att_w_o, att_b_o, final_norm, loss_target, m_norm_ffn1, m_ffn1_w_in, m_ffn1_w_out, m_norm_mix, m_norm_ffn2, m_ffn2_w_in, m_ffn2_w_out, m_ple_norm, m_ple_gate_w, m_ple_proj_w, m_hyb_w_in, m_conv_dw_w, m_conv_dw_b, m_conv_ln_g, m_conv_ln_b, m_ssm_conv_w, m_ssm_conv_b, m_ssm_dt_bias, m_ssm_a_log, m_ssm_d, m_ssm_norm, m_hyb_w_out, m_att_w_qkv, m_att_b_qkv, m_att_sinks, m_att_w_o, m_att_b_o, m_final_norm, v_norm_ffn1, v_ffn1_w_in, v_ffn1_w_out, v_norm_mix, v_norm_ffn2, v_ffn2_w_in, v_ffn2_w_out, v_ple_norm, v_ple_gate_w, v_ple_proj_w, v_hyb_w_in, v_conv_dw_w, v_conv_dw_b, v_conv_ln_g, v_conv_ln_b, v_ssm_conv_w, v_ssm_conv_b, v_ssm_dt_bias, v_ssm_a_log, v_ssm_d, v_ssm_norm, v_hyb_w_out, v_att_w_qkv, v_att_b_qkv, v_att_sinks, v_att_w_o, v_att_b_o, v_final_norm):
    given = locals()
    w = {k: given[k] for k in WEIGHTS}
    m = {k: given["m_" + k] for k in WEIGHTS}
    v = {k: given["v_" + k] for k in WEIGHTS}
    return _step(x, p, loss_target, w, m, v)
```

```python
import functools
import math

import numpy as np
import jax
import jax.numpy as jnp
from jax import lax
from jax.experimental import pallas as pl
from jax.experimental.pallas import tpu as pltpu

F32 = jnp.float32
BF16 = jnp.bfloat16
MXU_DTYPE = jnp.bfloat16
S = jax.ShapeDtypeStruct
MESH = pl.DeviceIdType.MESH

V7X_VMEM_BYTES = 64 * 2**20
VMEM_LIMIT = 48 * 2**20
LANE = 128

EPS = 1e-6
SSM_HEADS = 16
HEAD_DIM = 64
SSM_GROUPS = 2
SSM_STATE = 128
SSM_CONV = 4
CHUNK = 128
CONV_WIDTH = 31
ATT_HEADS = 16
ATT_KV_HEADS = 4
WINDOW = 128
ROPE_THETA = 10000.0
ADAM_LR = 0.001
ADAM_B1 = 0.9
ADAM_B2 = 0.999
ADAM_EPS = 1e-08
ADAM_WD = 0.01
ADAM_STEP = 10

N_CHIPS = 4
N_DEV = 8

NN = ((1,), (0,))
NT = ((1,), (1,))
TN = ((0,), (0,))


def _mm(a, b, dims=NN):
    return lax.dot_general(a.astype(MXU_DTYPE), b.astype(MXU_DTYPE), (dims, ((), ())), preferred_element_type=F32)


def _split3(a):
    hi = a.astype(BF16)
    r = a - hi.astype(F32)
    mid = r.astype(BF16)
    lo = (r - mid.astype(F32)).astype(BF16)
    return hi, mid, lo


def _mm01(a, onehot, dims=NN):
    o = onehot.astype(BF16)
    out = None
    for part in _split3(a):
        t = lax.dot_general(part, o, (dims, ((), ())), preferred_element_type=F32)
        out = t if out is None else out + t
    return out


def _01mm(onehot, a):
    o = onehot.astype(BF16)
    out = None
    for part in _split3(a):
        t = lax.dot_general(o, part, (NN, ((), ())), preferred_element_type=F32)
        out = t if out is None else out + t
    return out


def _sigmoid(x):
    return 0.5 * jnp.tanh(0.5 * x) + 0.5


def _softplus(x):
    return jnp.maximum(x, 0.0) + jnp.log(1.0 + jnp.exp(-jnp.abs(x)))


def _iota(shape, axis):
    return lax.broadcasted_iota(jnp.int32, shape, axis)


def _head_indicator(width, heads, transposed=False):
    per = width // heads
    if transposed:
        return (_iota((heads, width), 1) // per == _iota((heads, width), 0)).astype(F32)
    return (_iota((width, heads), 0) // per == _iota((width, heads), 1)).astype(F32)


def _acc(ref, i, val):
    @pl.when(i == 0)
    def _():
        ref[...] = val

    @pl.when(i > 0)
    def _():
        ref[...] += val


def _rs(tile, width, col=0, shift=0, n=None):
    if shift == 0:
        return pl.BlockSpec((tile, width), lambda i: (i, col))
    if shift < 0:
        return pl.BlockSpec((tile, width), lambda i: (jnp.maximum(i - 1, 0), col))
    return pl.BlockSpec((tile, width), lambda i: (jnp.minimum(i + 1, n - 1), col))


def _ps(shape):
    return pl.BlockSpec(shape, lambda i: (0,) * len(shape))


def _call(body, name, grid, in_specs, out_specs, out_shape, scratch=(), sem=None):
    return pl.pallas_call(
        body, name=name, grid=grid, in_specs=in_specs, out_specs=out_specs, out_shape=out_shape,
        scratch_shapes=list(scratch),
        compiler_params=pltpu.CompilerParams(dimension_semantics=sem, vmem_limit_bytes=VMEM_LIMIT))


def _row_tile(n, target):
    t = min(n, target)
    assert n % t == 0, (n, t)
    return t


def _pick_tile(dim, target):
    if dim <= target:
        return dim
    t = (int(1.4 * target) // LANE) * LANE
    while t >= LANE:
        if dim % t == 0:
            return t
        t -= LANE
    return dim


ANY = pl.BlockSpec(memory_space=pl.ANY)


class ColSharded:
    def __init__(self, arr):
        self.arr = arr
        self.nch, self.rows, self.per = arr.shape
        self.shape = (self.rows, self.nch * self.per)


class Slot:
    def __init__(self, buf, kind, per, off, c0=0):
        self.buf, self.kind, self.per, self.off, self.c0 = buf, kind, per, off, c0


def matmul(a, b, mode, name, *, out_dtype=F32, scale=None, res=None, bias=None, into=None, tm=1024, tn=1024, tk=1024):
    bshape = b.shape
    if mode == "nn":
        (m, k), (k2, n) = a.shape, bshape
    elif mode == "nt":
        (m, k), (n, k2) = a.shape, bshape
    else:
        (k, m), (k2, n) = a.shape, bshape
    assert k == k2, (a.shape, bshape, mode)
    tm, tn, tk = _pick_tile(m, tm), _pick_tile(n, tn), _pick_tile(k, tk)
    if isinstance(b, ColSharded):
        if mode == "nn":
            tn = b.per
        else:
            assert mode == "nt"
            tk = b.per
    if into is not None:
        if into.kind == "c":
            tn = into.per
            assert into.off % tm == 0 and n == N_CHIPS * into.per
        else:
            tm = max(1, min(m, int(1.4 * 1024)) // into.per) * into.per
            assert m % tm == 0 and into.off % into.per == 0 and into.c0 % (tm // into.per) == 0
    nk = k // tk
    dims = {"nn": NN, "nt": NT, "tn": TN}[mode]
    a_spec = (pl.BlockSpec((tk, tm), lambda i, j, kk: (kk, i)) if mode == "tn"
              else pl.BlockSpec((tm, tk), lambda i, j, kk: (i, kk)))
    if isinstance(b, ColSharded):
        b_spec = (pl.BlockSpec((None, tk, tn), lambda i, j, kk: (j, kk, 0)) if mode == "nn"
                  else pl.BlockSpec((None, tn, tk), lambda i, j, kk: (kk, j, 0)))
        b = b.arr
    else:
        b_spec = (pl.BlockSpec((tn, tk), lambda i, j, kk: (j, kk)) if mode == "nt"
                  else pl.BlockSpec((tk, tn), lambda i, j, kk: (kk, j)))
    plain_o = pl.BlockSpec((tm, tn), lambda i, j, kk: (i, j))
    ins, in_specs = [a, b], [a_spec, b_spec]
    if bias is not None:
        ins.append(bias)
        in_specs.append(pl.BlockSpec((1, tn), lambda i, j, kk: (0, j)))
    if res is not None:
        ins.append(res)
        in_specs.append(plain_o)
    aliases = {}
    if into is None:
        o_spec, o_shape = plain_o, S((m, n), out_dtype)
    else:
        aliases = {len(ins): 0}
        ins.append(into.buf)
        in_specs.append(ANY)
        o_shape = S(into.buf.shape, into.buf.dtype)
        if into.kind == "c":
            ob = into.off // tm
            o_spec = pl.BlockSpec((None, tm, tn), lambda i, j, kk: (j, ob + i, 0))
        else:
            q, ob = tm // into.per, into.off // into.per
            cb = into.c0 // q
            o_spec = pl.BlockSpec((q, into.per, tn), lambda i, j, kk: (cb + i, ob, j))

    def body(*refs):
        a_ref, b_ref = refs[0], refs[1]
        o_ref, acc_ref = refs[-2], refs[-1]
        kk = pl.program_id(2)

        @pl.when(kk == 0)
        def _():
            acc_ref[...] = jnp.zeros_like(acc_ref)

        acc_ref[...] += _mm(a_ref[...], b_ref[...], dims)

        @pl.when(kk == nk - 1)
        def _():
            out = acc_ref[...]
            if scale is not None:
                out = out * scale
            pos = 2
            if bias is not None:
                out = out + refs[pos][...]
                pos += 1
            if res is not None:
                out = out + refs[pos][...]
            o_ref[...] = out.astype(o_ref.dtype).reshape(o_ref.shape)

    return pl.pallas_call(
        body, name=name, grid=(m // tm, n // tn, nk), in_specs=in_specs, out_specs=o_spec, out_shape=o_shape,
        scratch_shapes=[pltpu.VMEM((tm, tn), F32)], input_output_aliases=aliases,
        compiler_params=pltpu.CompilerParams(dimension_semantics=("parallel", "parallel", "arbitrary"),
                                             vmem_limit_bytes=VMEM_LIMIT))(*ins)


def rms_fwd(h, g, name):
    n, d = h.shape
    tile = _row_tile(n, 512)

    def body(h_ref, g_ref, o_ref):
        x = h_ref[...]
        r = lax.rsqrt(jnp.mean(x * x, axis=-1, keepdims=True) + EPS)
        o_ref[...] = (x * r * g_ref[...]).astype(o_ref.dtype)

    return _call(body, name, (n // tile,), [_rs(tile, d), _ps((1, d))], _rs(tile, d), S((n, d), MXU_DTYPE),
                 sem=("parallel",))(h, g)


def _rms_bwd_math(x, g, dy):
    r = lax.rsqrt(jnp.mean(x * x, axis=-1, keepdims=True) + EPS)
    xh = x * r
    dg = jnp.sum(dy * xh, axis=0, keepdims=True)
    dxh = dy * g
    dx = r * (dxh - xh * jnp.mean(dxh * xh, axis=-1, keepdims=True))
    return dx, dg


def rms_bwd(h, g, dxn, dh_in, name, colsum=False):
    n, d = h.shape
    tile = _row_tile(n, 256)

    def body(h_ref, g_ref, dxn_ref, dh_ref, o_ref, dg_ref, *cs_ref):
        i = pl.program_id(0)
        dx, dg = _rms_bwd_math(h_ref[...], g_ref[...], dxn_ref[...].astype(F32))
        out = dh_ref[...] + dx
        o_ref[...] = out
        _acc(dg_ref, i, dg)
        if colsum:
            _acc(cs_ref[0], i, jnp.sum(out, axis=0, keepdims=True))

    outs = [S((n, d), F32), S((1, d), F32)] + ([S((1, d), F32)] if colsum else [])
    ospecs = [_rs(tile, d), _ps((1, d))] + ([_ps((1, d))] if colsum else [])
    return _call(body, name, (n // tile,), [_rs(tile, d), _ps((1, d)), _rs(tile, d), _rs(tile, d)], ospecs, outs,
                 sem=("arbitrary",))(h, g, dxn, dh_in)


def swiglu_fwd(u, name):
    n, f2 = u.shape
    f = f2 // 2
    tile = _row_tile(n, 256)

    def body(u_ref, o_ref):
        a = u_ref[:, :f]
        o_ref[...] = (a * _sigmoid(a) * u_ref[:, f:]).astype(o_ref.dtype)

    return _call(body, name, (n // tile,), [_rs(tile, f2)], _rs(tile, f), S((n, f), MXU_DTYPE), sem=("parallel",))(u)


def swiglu_bwd(u, dhm, name):
    n, f2 = u.shape
    f = f2 // 2
    tile = _row_tile(n, 256)

    def body(u_ref, d_ref, o_ref):
        a = u_ref[:, :f]
        b = u_ref[:, f:]
        d = d_ref[...]
        s = _sigmoid(a)
        o_ref[:, :f] = (d * b * s * (1.0 + a * (1.0 - s))).astype(o_ref.dtype)
        o_ref[:, f:] = (d * a * s).astype(o_ref.dtype)

    return _call(body, name, (n // tile,), [_rs(tile, f2), _rs(tile, f)], _rs(tile, f2), S((n, f2), MXU_DTYPE),
                 sem=("parallel",))(u, dhm)


def ple_fwd(h, gl, pp, name):
    n, d = h.shape
    tile = _row_tile(n, 512)

    def body(h_ref, gl_ref, pp_ref, o_ref):
        o_ref[...] = h_ref[...] + _sigmoid(gl_ref[...]) * pp_ref[...]

    return _call(body, name, (n // tile,), [_rs(tile, d)] * 3, _rs(tile, d), S((n, d), F32), sem=("parallel",))(h, gl, pp)


def ple_bwd(dh, gl, pp, name):
    n, d = dh.shape
    tile = _row_tile(n, 512)

    def body(dh_ref, gl_ref, pp_ref, dpp_ref, dgl_ref):
        g = _sigmoid(gl_ref[...])
        dh_ = dh_ref[...]
        dpp_ref[...] = (dh_ * g).astype(dpp_ref.dtype)
        dgl_ref[...] = (dh_ * pp_ref[...] * g * (1.0 - g)).astype(dgl_ref.dtype)

    return _call(body, name, (n // tile,), [_rs(tile, d)] * 3, [_rs(tile, d)] * 2, [S((n, d), MXU_DTYPE)] * 2,
                 sem=("parallel",))(dh, gl, pp)


def loss_head(h, g, target, name):
    n, d = h.shape
    tile = _row_tile(n, 256)

    def body(h_ref, g_ref, t_ref, dh_ref, dg_ref, loss_ref):
        i = pl.program_id(0)
        x = h_ref[...]
        gg = g_ref[...]
        r = lax.rsqrt(jnp.mean(x * x, axis=-1, keepdims=True) + EPS)
        err = x * r * gg - t_ref[...]
        part = 0.5 * jnp.sum(jnp.mean(err * err, axis=-1, keepdims=True), axis=0, keepdims=True)
        dx, dg = _rms_bwd_math(x, gg, err * (1.0 / d))
        dh_ref[...] = dx
        _acc(dg_ref, i, dg)
        _acc(loss_ref, i, jnp.broadcast_to(part, (8, LANE)))

    return _call(body, name, (n // tile,), [_rs(tile, d), _ps((1, d)), _rs(tile, d)],
                 [_rs(tile, d), _ps((1, d)), _ps((8, LANE))], [S((n, d), F32), S((1, d), F32), S((8, LANE), F32)],
                 sem=("arbitrary",))(h, g, target)


def adamw(w, g, m, v, name):
    r, c = w.shape
    tile = r
    for t in (512, 256, 128, 64, 32, 16, 8):
        if r % t == 0 and t * c * 4 <= 2**21:
            tile = t
            break
    c1 = np.float32(1.0 - ADAM_B1 ** ADAM_STEP)
    c2 = np.float32(1.0 - ADAM_B2 ** ADAM_STEP)

    def body(w_ref, g_ref, m_ref, v_ref, d_ref, mo_ref, vo_ref):
        gg = g_ref[...]
        mm = ADAM_B1 * m_ref[...] + (1.0 - ADAM_B1) * gg
        vv = ADAM_B2 * v_ref[...] + (1.0 - ADAM_B2) * (gg * gg)
        mo_ref[...] = mm
        vo_ref[...] = vv
        d_ref[...] = -ADAM_LR * ((mm / c1) / (jnp.sqrt(vv / c2) + ADAM_EPS) + ADAM_WD * w_ref[...])

    return _call(body, name, (r // tile,), [_rs(tile, c)] * 4, [_rs(tile, c)] * 3, [S((r, c), F32)] * 3,
                 sem=("parallel",))(w, g, m, v)


def _taps_fwd(sc, w_ref, width, halo, tile, acc):
    for k in range(width):
        o = halo - (width - 1) + k
        acc = acc + w_ref[k:k + 1, :] * sc[o:o + tile, :]
    return acc


def _taps_bwd_x(sc_d, w_ref, width, tile, acc):
    for k in range(width):
        o = (width - 1) - k
        acc = acc + w_ref[k:k + 1, :] * sc_d[o:o + tile, :]
    return acc


def _taps_bwd_w(dy, sc, dw_ref, width, halo, tile, i):
    @pl.when(i == 0)
    def _():
        dw_ref[...] = jnp.zeros_like(dw_ref)

    for k in range(width):
        o = halo - (width - 1) + k
        dw_ref[k:k + 1, :] += jnp.sum(dy * sc[o:o + tile, :], axis=0, keepdims=True)


def _ln_stats(x):
    mu = jnp.mean(x, axis=-1, keepdims=True)
    xc = x - mu
    r = lax.rsqrt(jnp.mean(xc * xc, axis=-1, keepdims=True) + EPS)
    return xc * r, r


def conv_group_fwd(proj, cw, cb, lg, lb, name):
    n = proj.shape[0]
    d = cw.shape[1]
    tile = _row_tile(n, 256)
    halo = 32

    def body(v_ref, g_ref, vp_ref, gp_ref, cw_ref, cb_ref, lg_ref, lb_ref, u_ref, u1_ref, sc):
        i = pl.program_id(0)
        first = (i > 0).astype(F32)
        sc[0:halo, :] = vp_ref[tile - halo:, :] * _sigmoid(gp_ref[tile - halo:, :]) * first
        sc[halo:, :] = v_ref[...] * _sigmoid(g_ref[...])
        u1 = _taps_fwd(sc, cw_ref, CONV_WIDTH, halo, tile, jnp.zeros((tile, d), F32) + cb_ref[...])
        u1_ref[...] = u1
        xh, _ = _ln_stats(u1)
        y = xh * lg_ref[...] + lb_ref[...]
        u_ref[...] = (y * _sigmoid(y)).astype(u_ref.dtype)

    return _call(body, name, (n // tile,),
                 [_rs(tile, d, 0), _rs(tile, d, 1), _rs(tile, d, 0, -1), _rs(tile, d, 1, -1),
                  _ps(cw.shape), _ps((1, d)), _ps((1, d)), _ps((1, d))],
                 [_rs(tile, d), _rs(tile, d)], [S((n, d), MXU_DTYPE), S((n, d), F32)],
                 scratch=[pltpu.VMEM((halo + tile, d), F32)], sem=("arbitrary",))(proj, proj, proj, proj, cw, cb, lg, lb)


def conv_group_bwd(du, u1, proj, cw, lg, lb, name):
    n = proj.shape[0]
    d = cw.shape[1]
    tile = _row_tile(n, 256)
    halo = 32
    nt = n // tile

    def body(du_ref, dun_ref, u1_ref, u1n_ref, v_ref, g_ref, vp_ref, gp_ref, cw_ref, lg_ref, lb_ref,
             dp_ref, dcw_ref, dcb_ref, dlg_ref, dlb_ref, sc, sc_d):
        i = pl.program_id(0)

        def ln_swish_bwd(dy_, u1_):
            xh, r = _ln_stats(u1_)
            y = xh * lg_ref[...] + lb_ref[...]
            s = _sigmoid(y)
            dyy = dy_ * s * (1.0 + y * (1.0 - s))
            dxh = dyy * lg_ref[...]
            dx = r * (dxh - jnp.mean(dxh, axis=-1, keepdims=True) - xh * jnp.mean(dxh * xh, axis=-1, keepdims=True))
            return dx, jnp.sum(dyy * xh, axis=0, keepdims=True), jnp.sum(dyy, axis=0, keepdims=True)

        du1, dlg, dlb = ln_swish_bwd(du_ref[...].astype(F32), u1_ref[...])
        du1n, _, _ = ln_swish_bwd(dun_ref[0:halo, :].astype(F32), u1n_ref[0:halo, :])
        sc_d[0:tile, :] = du1
        sc_d[tile:, :] = du1n * (i < nt - 1).astype(F32)
        sig = _sigmoid(g_ref[...])
        val = v_ref[...]
        sc[0:halo, :] = vp_ref[tile - halo:, :] * _sigmoid(gp_ref[tile - halo:, :]) * (i > 0).astype(F32)
        sc[halo:, :] = val * sig
        du0 = _taps_bwd_x(sc_d, cw_ref, CONV_WIDTH, tile, jnp.zeros((tile, d), F32))
        _taps_bwd_w(du1, sc, dcw_ref, CONV_WIDTH, halo, tile, i)
        _acc(dcb_ref, i, jnp.sum(du1, axis=0, keepdims=True))
        _acc(dlg_ref, i, dlg)
        _acc(dlb_ref, i, dlb)
        dp_ref[:, :d] = (du0 * sig).astype(dp_ref.dtype)
        dp_ref[:, d:] = (du0 * val * sig * (1.0 - sig)).astype(dp_ref.dtype)

    return _call(body, name, (nt,),
                 [_rs(tile, d), _rs(tile, d, 0, 1, nt), _rs(tile, d), _rs(tile, d, 0, 1, nt),
                  _rs(tile, d, 0), _rs(tile, d, 1), _rs(tile, d, 0, -1), _rs(tile, d, 1, -1),
                  _ps(cw.shape), _ps((1, d)), _ps((1, d))],
                 [_rs(tile, 2 * d), _ps(cw.shape), _ps((1, d)), _ps((1, d)), _ps((1, d))],
                 [S((n, proj.shape[1]), MXU_DTYPE), S(cw.shape, F32), S((1, d), F32), S((1, d), F32), S((1, d), F32)],
                 scratch=[pltpu.VMEM((halo + tile, d), F32), pltpu.VMEM((tile + halo, d), F32)],
                 sem=("arbitrary",))(du, du, u1, u1, proj, proj, proj, proj, cw, lg, lb)


def ssm_conv_fwd(proj, dtr, sw, sb, dtb, name):
    n = proj.shape[0]
    w = sw.shape[1]
    inner = SSM_HEADS * HEAD_DIM
    tile = _row_tile(n, 256)
    halo = 8

    def body(x_ref, xp_ref, dtr_ref, sw_ref, sb_ref, dtb_ref, pre_ref, xs_ref, bc_ref, dt_ref, sc):
        i = pl.program_id(0)
        sc[0:halo, :] = xp_ref[tile - halo:, :] * (i > 0).astype(F32)
        sc[halo:, :] = x_ref[...]
        pre = _taps_fwd(sc, sw_ref, SSM_CONV, halo, tile, jnp.zeros((tile, w), F32) + sb_ref[...])
        pre_ref[...] = pre
        act = pre * _sigmoid(pre)
        xs_ref[...] = act[:, :inner]
        bc_ref[...] = act[:, inner:]
        dt = _softplus(dtr_ref[...] + dtb_ref[...])
        dt_ref[...] = jnp.where(_iota(dt.shape, 1) < SSM_HEADS, dt, 0.0)

    return _call(body, name, (n // tile,),
                 [_rs(tile, w, 2), _rs(tile, w, 2, -1), _rs(tile, LANE), _ps(sw.shape), _ps((1, w)), _ps((1, LANE))],
                 [_rs(tile, w), _rs(tile, inner), _rs(tile, w - inner), _rs(tile, LANE)],
                 [S((n, w), F32), S((n, inner), F32), S((n, w - inner), F32), S((n, LANE), F32)],
                 scratch=[pltpu.VMEM((halo + tile, w), F32)], sem=("arbitrary",))(proj, proj, dtr, sw, sb, dtb)


def ssm_conv_bwd(dxs, dbc, pre, proj, sw, dproj, name):
    n = proj.shape[0]
    w = sw.shape[1]
    inner = SSM_HEADS * HEAD_DIM
    tile = _row_tile(n, 256)
    halo = 8
    nt = n // tile

    def body(dxs_ref, dxsn_ref, dbc_ref, dbcn_ref, pre_ref, pren_ref, x_ref, xp_ref, sw_ref, dp_in_ref,
             dx_ref, dsw_ref, dsb_ref, sc, sc_d):
        i = pl.program_id(0)

        def silu_bwd(d_, p_):
            s = _sigmoid(p_)
            return d_ * s * (1.0 + p_ * (1.0 - s))

        sc_d[0:tile, :inner] = silu_bwd(dxs_ref[...], pre_ref[:, :inner])
        sc_d[0:tile, inner:] = silu_bwd(dbc_ref[...], pre_ref[:, inner:])
        last = (i < nt - 1).astype(F32)
        sc_d[tile:, :inner] = silu_bwd(dxsn_ref[0:halo, :], pren_ref[0:halo, :inner]) * last
        sc_d[tile:, inner:] = silu_bwd(dbcn_ref[0:halo, :], pren_ref[0:halo, inner:]) * last
        sc[0:halo, :] = xp_ref[tile - halo:, :] * (i > 0).astype(F32)
        sc[halo:, :] = x_ref[...]
        dpre = sc_d[0:tile, :]
        dx_ref[...] = _taps_bwd_x(sc_d, sw_ref, SSM_CONV, tile, jnp.zeros((tile, w), F32)).astype(dx_ref.dtype)
        _taps_bwd_w(dpre, sc, dsw_ref, SSM_CONV, halo, tile, i)
        _acc(dsb_ref, i, jnp.sum(dpre, axis=0, keepdims=True))

    return pl.pallas_call(
        body, name=name, grid=(nt,),
        in_specs=[_rs(tile, inner), _rs(tile, inner, 0, 1, nt), _rs(tile, w - inner), _rs(tile, w - inner, 0, 1, nt),
                  _rs(tile, w), _rs(tile, w, 0, 1, nt), _rs(tile, w, 2), _rs(tile, w, 2, -1), _ps(sw.shape), ANY],
        out_specs=[_rs(tile, w, 2), _ps(sw.shape), _ps((1, w))],
        out_shape=[S(dproj.shape, dproj.dtype), S(sw.shape, F32), S((1, w), F32)],
        scratch_shapes=[pltpu.VMEM((halo + tile, w), F32), pltpu.VMEM((tile + halo, w), F32)],
        input_output_aliases={9: 0},
        compiler_params=pltpu.CompilerParams(dimension_semantics=("arbitrary",), vmem_limit_bytes=VMEM_LIMIT),
    )(dxs, dxs, dbc, dbc, pre, pre, proj, proj, sw, dproj)


def _ssd_prologue(dt_ref, dtT_ref, al_ref, alc_ref):
    row = _iota((CHUNK, CHUNK), 0)
    col = _iota((CHUNK, CHUNK), 1)
    dt = dt_ref[:, :SSM_HEADS]
    a_row = -jnp.exp(al_ref[:, :SSM_HEADS])
    a_col = -jnp.exp(alc_ref[...])
    cs = _01mm((row >= col).astype(F32), dt * a_row)
    csT = _mm01(dtT_ref[...] * a_col, (row <= col).astype(F32))
    return dt, a_row, cs, csT, row, col


def _decay(cs, csT, h, row, col):
    lm = jnp.exp(jnp.where(row >= col, cs[:, h:h + 1] - csT[h:h + 1, :], -1e30))
    lmT = jnp.exp(jnp.where(col >= row, csT[h:h + 1, :] - cs[:, h:h + 1], -1e30))
    return lm, lmT


def ssd_fwd(xs, bc, dt, dtT, alog_row, alog_col, name):
    n, width = xs.shape
    nc = n // CHUNK
    gw = width // SSM_GROUPS
    hpg = SSM_HEADS // SSM_GROUPS
    ns = SSM_STATE

    def body(xs_ref, bc_ref, dt_ref, dtT_ref, al_ref, alc_ref, y_ref, hs_ref, h_sc):
        i = pl.program_id(0)

        @pl.when(i == 0)
        def _():
            h_sc[...] = jnp.zeros_like(h_sc)

        dt, a_row, cs, csT, row, col = _ssd_prologue(dt_ref, dtT_ref, al_ref, alc_ref)
        indT = _head_indicator(width, SSM_HEADS, transposed=True)
        dt_full = _mm01(dt, indT)
        e_full = jnp.exp(_mm01(cs, indT))
        dte_full = jnp.exp(_mm01(cs[CHUNK - 1:CHUNK, :] - cs, indT))
        xt = xs_ref[...] * dt_full
        hs_ref[0] = h_sc[...]
        lo = _iota((CHUNK, 2 * HEAD_DIM), 1) < HEAD_DIM
        for g in range(SSM_GROUPS):
            bg = bc_ref[:, g * ns:(g + 1) * ns]
            cg = bc_ref[:, (SSM_GROUPS + g) * ns:(SSM_GROUPS + g + 1) * ns]
            gm = _mm(cg, bg, NT)
            hg = h_sc[g * gw:(g + 1) * gw, :]
            yoff = e_full[:, g * gw:(g + 1) * gw] * _mm(cg, hg, NT)
            for pr in range(hpg // 2):
                h0 = g * hpg + 2 * pr
                c0 = h0 * HEAD_DIM
                xp = xt[:, c0:c0 + 2 * HEAD_DIM]
                m0 = gm * _decay(cs, csT, h0, row, col)[0]
                m1 = gm * _decay(cs, csT, h0 + 1, row, col)[0]
                yd = jnp.where(lo, _mm(m0, xp), _mm(m1, xp))
                y_ref[:, c0:c0 + 2 * HEAD_DIM] = yd + yoff[:, 2 * pr * HEAD_DIM:(2 * pr + 2) * HEAD_DIM]
            sg = _mm(xt[:, g * gw:(g + 1) * gw] * dte_full[:, g * gw:(g + 1) * gw], bg, TN)
            for hh in range(hpg):
                h = g * hpg + hh
                r0 = h * HEAD_DIM
                h_sc[r0:r0 + HEAD_DIM, :] = (h_sc[r0:r0 + HEAD_DIM, :] * jnp.exp(csT[h:h + 1, CHUNK - 1:CHUNK])
                                             + sg[hh * HEAD_DIM:(hh + 1) * HEAD_DIM, :])

    bcw = bc.shape[1]
    return _call(body, name, (nc,),
                 [_rs(CHUNK, width), _rs(CHUNK, bcw), _rs(CHUNK, LANE), pl.BlockSpec((SSM_HEADS, CHUNK), lambda i: (0, i)),
                  _ps((1, LANE)), _ps((SSM_HEADS, 1))],
                 [_rs(CHUNK, width), pl.BlockSpec((1, width, ns), lambda i: (i, 0, 0))],
                 [S((n, width), F32), S((nc, width, ns), F32)],
                 scratch=[pltpu.VMEM((width, ns), F32)], sem=("arbitrary",))(xs, bc, dt, dtT, alog_row, alog_col)


def ssd_bwd(xs, bc, dt, dtT, alog_row, alog_col, hs, dy, dxs_skip, name):
    n, width = xs.shape
    nc = n // CHUNK
    gw = width // SSM_GROUPS
    hpg = SSM_HEADS // SSM_GROUPS
    ns = SSM_STATE
    bcw = bc.shape[1]

    def body(xs_ref, bc_ref, dt_ref, dtT_ref, al_ref, alc_ref, hs_ref, dy_ref, skip_ref,
             dxs_ref, dbc_ref, ddtr_ref, dal_ref, ddtb_ref, dh_sc, dxt_sc):
        i = pl.program_id(0)

        @pl.when(i == 0)
        def _():
            dh_sc[...] = jnp.zeros_like(dh_sc)

        dt, a_row, cs, csT, row, col = _ssd_prologue(dt_ref, dtT_ref, al_ref, alc_ref)
        indT = _head_indicator(width, SSM_HEADS, transposed=True)
        ind = _head_indicator(width, SSM_HEADS)
        dt_full = _mm01(dt, indT)
        e_full = jnp.exp(_mm01(cs, indT))
        cs_last = cs[CHUNK - 1:CHUNK, :]
        dte = jnp.exp(cs_last - cs)
        dte_full = _mm01(dte, indT)
        xs_ = xs_ref[...]
        xt = xs_ * dt_full
        dy_ = dy_ref[...]
        hmat = hs_ref[0]
        ds = dh_sc[...]
        lo = _iota((CHUNK, 2 * HEAD_DIM), 1) < HEAD_DIM
        head_lane = _iota((1, SSM_HEADS), 1)
        dcs = jnp.zeros((CHUNK, SSM_HEADS), F32)
        ddte = jnp.zeros((CHUNK, SSM_HEADS), F32)
        for g in range(SSM_GROUPS):
            sl = slice(g * gw, (g + 1) * gw)
            bg = bc_ref[:, g * ns:(g + 1) * ns]
            cg = bc_ref[:, (SSM_GROUPS + g) * ns:(SSM_GROUPS + g + 1) * ns]
            gm = _mm(cg, bg, NT)
            gmT = _mm(bg, cg, NT)
            hg = hmat[sl, :]
            dsg = ds[sl, :]
            dyg = dy_[:, sl]
            xtg = xt[:, sl]
            yoff = e_full[:, sl] * _mm(cg, hg, NT)
            edy = e_full[:, sl] * dyg
            bds = _mm(bg, dsg, NT)
            dxt_g = dte_full[:, sl] * bds
            ddte = ddte + _mm01(xtg * bds, ind[sl, :])
            dcs = dcs + _mm01(dyg * yoff, ind[sl, :])
            db = _mm(xtg * dte_full[:, sl], dsg)
            dc = _mm(edy, hg)
            dhc = _mm(edy, cg, TN)
            dgs = jnp.zeros((CHUNK, CHUNK), F32)
            dgTs = jnp.zeros((CHUNK, CHUNK), F32)
            for pr in range(hpg // 2):
                h0 = g * hpg + 2 * pr
                c0 = 2 * pr * HEAD_DIM
                xp = xtg[:, c0:c0 + 2 * HEAD_DIM]
                dyp = dyg[:, c0:c0 + 2 * HEAD_DIM]
                rr = []
                for h, half in ((h0, lo), (h0 + 1, jnp.logical_not(lo))):
                    lm, lmT = _decay(cs, csT, h, row, col)
                    xm = jnp.where(half, xp, 0.0)
                    dm = _mm(dyp, xm, NT)
                    dmT = _mm(xm, dyp, NT)
                    mT = gmT * lmT
                    z = jnp.sum(dm * (gm * lm), axis=1, keepdims=True) - jnp.sum(dmT * mT, axis=1, keepdims=True)
                    dcs = dcs + z * (head_lane == h).astype(F32)
                    dgs = dgs + dm * lm
                    dgTs = dgTs + dmT * lmT
                    rr.append(_mm(mT, dyp))
                dxt_sc[:, g * gw + c0:g * gw + c0 + 2 * HEAD_DIM] = jnp.where(lo, rr[0], rr[1]) + dxt_g[:, c0:c0 + 2 * HEAD_DIM]
            dbc_ref[:, g * ns:(g + 1) * ns] = db + _mm(dgTs, cg)
            dbc_ref[:, (SSM_GROUPS + g) * ns:(SSM_GROUPS + g + 1) * ns] = dc + _mm(dgs, bg)
            for hh in range(hpg):
                h = g * hpg + hh
                r0 = h * HEAD_DIM
                dh_sc[r0:r0 + HEAD_DIM, :] = (dhc[hh * HEAD_DIM:(hh + 1) * HEAD_DIM, :]
                                              + jnp.exp(csT[h:h + 1, CHUNK - 1:CHUNK]) * ds[r0:r0 + HEAD_DIM, :])
        t = ddte * dte
        per_head = jnp.sum(jnp.sum(ds * hmat, axis=1, keepdims=True) * ind, axis=0, keepdims=True)
        last_add = jnp.sum(t, axis=0, keepdims=True) + jnp.exp(cs_last) * per_head
        dcs = dcs - t + jnp.where(_iota((CHUNK, SSM_HEADS), 0) == CHUNK - 1, last_add, 0.0)
        dadt = _01mm((row <= col).astype(F32), dcs)
        dxt = dxt_sc[...]
        ddt = dadt * a_row + _mm01(dxt * xs_, ind)
        dxs_ref[...] = dxt * dt_full + skip_ref[...]
        ddtr = ddt * (1.0 - jnp.exp(-dt))
        ddtr_ref[...] = jnp.zeros_like(ddtr_ref)
        ddtr_ref[:, :SSM_HEADS] = ddtr.astype(ddtr_ref.dtype)
        _acc(dal_ref, i, jnp.sum(dadt * dt, axis=0, keepdims=True) * a_row)
        _acc(ddtb_ref, i, jnp.sum(ddtr, axis=0, keepdims=True))

    rev = lambda i: (nc - 1 - i, 0)
    return _call(body, name, (nc,),
                 [pl.BlockSpec((CHUNK, width), rev), pl.BlockSpec((CHUNK, bcw), rev), pl.BlockSpec((CHUNK, LANE), rev),
                  pl.BlockSpec((SSM_HEADS, CHUNK), lambda i: (0, nc - 1 - i)), _ps((1, LANE)), _ps((SSM_HEADS, 1)),
                  pl.BlockSpec((1, width, ns), lambda i: (nc - 1 - i, 0, 0)), pl.BlockSpec((CHUNK, width), rev),
                  pl.BlockSpec((CHUNK, width), rev)],
                 [pl.BlockSpec((CHUNK, width), rev), pl.BlockSpec((CHUNK, bcw), rev), pl.BlockSpec((CHUNK, LANE), rev),
                  _ps((1, SSM_HEADS)), _ps((1, SSM_HEADS))],
                 [S((n, width), F32), S((n, bcw), F32), S((n, LANE), MXU_DTYPE), S((1, SSM_HEADS), F32), S((1, SSM_HEADS), F32)],
                 scratch=[pltpu.VMEM((width, ns), F32), pltpu.VMEM((CHUNK, width), F32)],
                 sem=("arbitrary",))(xs, bc, dt, dtT, alog_row, alog_col, hs, dy, dxs_skip)


def ssm_gate_fwd(yssd, xs, proj, dfull, gamma, name):
    n, d = yssd.shape
    tile = _row_tile(n, 256)
    gw = d // SSM_GROUPS

    def body(y_ref, xs_ref, z_ref, df_ref, gm_ref, o_ref):
        z = z_ref[...]
        y2 = (y_ref[...] + df_ref[...] * xs_ref[...]) * (z * _sigmoid(z))
        for g in range(SSM_GROUPS):
            yg = y2[:, g * gw:(g + 1) * gw]
            r = lax.rsqrt(jnp.mean(yg * yg, axis=-1, keepdims=True) + EPS)
            o_ref[:, g * gw:(g + 1) * gw] = (yg * r * gm_ref[:, g * gw:(g + 1) * gw]).astype(o_ref.dtype)

    return _call(body, name, (n // tile,), [_rs(tile, d), _rs(tile, d), _rs(tile, d, 2), _ps((1, d)), _ps((1, d))],
                 _rs(tile, d), S((n, d), MXU_DTYPE), sem=("parallel",))(yssd, xs, proj, dfull, gamma)


def ssm_gate_bwd(dy3, yssd, xs, proj, dfull, gamma, dproj, name):
    n, d = yssd.shape
    tile = _row_tile(n, 256)
    gw = d // SSM_GROUPS

    def body(dy_ref, y_ref, xs_ref, z_ref, df_ref, gm_ref, dp_in_ref, dys_ref, dxs_ref, dz_ref, dgm_ref, dd_ref):
        i = pl.program_id(0)
        z = z_ref[...]
        s = _sigmoid(z)
        xs_ = xs_ref[...]
        y1 = y_ref[...] + df_ref[...] * xs_
        y2 = y1 * (z * s)
        dy_ = dy_ref[...].astype(F32)
        dgm = []
        dy2 = []
        for g in range(SSM_GROUPS):
            sl = slice(g * gw, (g + 1) * gw)
            dxg, dgg = _rms_bwd_math(y2[:, sl], gm_ref[:, sl], dy_[:, sl])
            dy2.append(dxg)
            dgm.append(dgg)
        dy2 = jnp.concatenate(dy2, axis=1)
        dy1 = dy2 * (z * s)
        dys_ref[...] = dy1
        dxs_ref[...] = dy1 * df_ref[...]
        dz_ref[...] = (dy2 * y1 * s * (1.0 + z * (1.0 - s))).astype(dz_ref.dtype)
        _acc(dgm_ref, i, jnp.concatenate(dgm, axis=1))
        colsum = jnp.broadcast_to(jnp.sum(dy1 * xs_, axis=0, keepdims=True), (8, d))
        _acc(dd_ref, i, _mm01(colsum, _head_indicator(d, SSM_HEADS))[0:1, :])

    return pl.pallas_call(
        body, name=name, grid=(n // tile,),
        in_specs=[_rs(tile, d), _rs(tile, d), _rs(tile, d), _rs(tile, d, 2), _ps((1, d)), _ps((1, d)), ANY],
        out_specs=[_rs(tile, d), _rs(tile, d), _rs(tile, d, 2), _ps((1, d)), _ps((1, SSM_HEADS))],
        out_shape=[S((n, d), F32), S((n, d), F32), S(dproj.shape, dproj.dtype), S((1, d), F32), S((1, SSM_HEADS), F32)],
        input_output_aliases={6: 2},
        compiler_params=pltpu.CompilerParams(dimension_semantics=("arbitrary",), vmem_limit_bytes=VMEM_LIMIT),
    )(dy3, yssd, xs, proj, dfull, gamma, dproj)


def _rope128(x, cos, sin_signed):
    half = HEAD_DIM // 2
    lane = _iota(x.shape, 1)
    partner = jnp.where((lane % HEAD_DIM) < half, pltpu.roll(x, LANE - half, 1), pltpu.roll(x, half, 1))
    return x * cos + partner * sin_signed


def rope_fwd(qkv, cos, sin, name):
    n, w = qkv.shape
    qw = ATT_HEADS * HEAD_DIM
    kw = ATT_KV_HEADS * HEAD_DIM
    tile = _row_tile(n, 256)

    def body(x_ref, c_ref, s_ref, q_ref, k_ref, v_ref):
        c, s = c_ref[...], s_ref[...]
        for j in range(qw // LANE):
            q_ref[:, j * LANE:(j + 1) * LANE] = _rope128(x_ref[:, j * LANE:(j + 1) * LANE], c, s).astype(q_ref.dtype)
        for j in range(kw // LANE):
            k_ref[:, j * LANE:(j + 1) * LANE] = _rope128(x_ref[:, qw + j * LANE:qw + (j + 1) * LANE], c, s).astype(k_ref.dtype)
        v_ref[...] = x_ref[:, qw + kw:].astype(v_ref.dtype)

    return _call(body, name, (n // tile,), [_rs(tile, w), _rs(tile, LANE), _rs(tile, LANE)],
                 [_rs(tile, qw), _rs(tile, kw), _rs(tile, kw)],
                 [S((n, qw), MXU_DTYPE), S((n, kw), MXU_DTYPE), S((n, kw), MXU_DTYPE)], sem=("parallel",))(qkv, cos, sin)


def _attn_mask(i):
    row = _iota((WINDOW, 2 * WINDOW), 0)
    s = _iota((WINDOW, 2 * WINDOW), 1)
    return (s > row) & (s <= row + WINDOW) & ((s >= WINDOW) | (i > 0))


def _attn_heads():
    grp = ATT_HEADS // ATT_KV_HEADS
    return [(h, h // 2, h % 2, (h // grp) // 2, (h // grp) % 2) for h in range(ATT_HEADS)]


def attn_fwd(q, k, v, sinks, name):
    n, qw = q.shape
    kw = k.shape[1]
    nb = n // WINDOW
    scale = HEAD_DIM ** -0.5

    def body(q_ref, kc_ref, kp_ref, vc_ref, vp_ref, sk_ref, o_ref, lse_ref):
        i = pl.program_id(0)
        valid = _attn_mask(i)
        lo = _iota((WINDOW, LANE), 1) < HEAD_DIM
        k2 = jnp.concatenate([kp_ref[...], kc_ref[...]], axis=0)
        v2 = jnp.concatenate([vp_ref[...], vc_ref[...]], axis=0)
        lane1 = _iota((1, LANE), 1)
        lse = jnp.zeros((WINDOW, LANE), F32)
        res = {}
        for h, qb, qh, kb, kh in _attn_heads():
            qp = q_ref[:, qb * LANE:(qb + 1) * LANE]
            qm = jnp.where(lo if qh == 0 else jnp.logical_not(lo), qp, jnp.zeros_like(qp))
            kk = k2[:, kb * LANE:(kb + 1) * LANE]
            if kh != qh:
                kk = pltpu.roll(kk, HEAD_DIM, 1)
            logits = jnp.where(valid, _mm(qm, kk, NT) * scale, -1e30)
            sk = sk_ref[:, h:h + 1]
            m = jnp.maximum(jnp.max(logits, axis=-1, keepdims=True), sk)
            e = jnp.exp(logits - m)
            den = jnp.sum(e, axis=-1, keepdims=True) + jnp.exp(sk - m)
            lse = lse + (m + jnp.log(den)) * (lane1 == h).astype(F32)
            r = _mm(e * (1.0 / den), v2[:, kb * LANE:(kb + 1) * LANE])
            res[h] = r if kh == qh else pltpu.roll(r, HEAD_DIM, 1)
            if qh == 1:
                o_ref[:, qb * LANE:(qb + 1) * LANE] = jnp.where(lo, res[h - 1], res[h]).astype(o_ref.dtype)
        lse_ref[...] = lse

    return _call(body, name, (nb,),
                 [_rs(WINDOW, qw), _rs(WINDOW, kw), _rs(WINDOW, kw, 0, -1), _rs(WINDOW, kw), _rs(WINDOW, kw, 0, -1), _ps((1, LANE))],
                 [_rs(WINDOW, qw), _rs(WINDOW, LANE)], [S((n, qw), MXU_DTYPE), S((n, LANE), F32)],
                 sem=("parallel",))(q, k, k, v, v, sinks)


def attn_bwd(q, k, v, o, do, lse, sinks, name):
    n, qw = q.shape
    kw = k.shape[1]
    nb = n // WINDOW
    scale = HEAD_DIM ** -0.5

    def body(q_ref, kc_ref, kp_ref, vc_ref, vp_ref, o_ref, do_ref, lse_ref, sk_ref,
             dq_ref, dka_ref, dkb_ref, dva_ref, dvb_ref, dsk_ref):
        i = pl.program_id(0)
        valid = _attn_mask(i)
        lo = _iota((WINDOW, LANE), 1) < HEAD_DIM
        k2 = jnp.concatenate([kp_ref[...], kc_ref[...]], axis=0)
        v2 = jnp.concatenate([vp_ref[...], vc_ref[...]], axis=0)
        lane1 = _iota((1, LANE), 1)
        do_ = do_ref[...].astype(F32)
        delta = _mm01(do_ * o_ref[...].astype(F32), _head_indicator(qw, ATT_HEADS))
        dk2 = [jnp.zeros((2 * WINDOW, LANE), F32) for _ in range(kw // LANE)]
        dv2 = [jnp.zeros((2 * WINDOW, LANE), F32) for _ in range(kw // LANE)]
        dsk = jnp.zeros((1, LANE), F32)
        res = {}
        for h, qb, qh, kb, kh in _attn_heads():
            half = lo if qh == 0 else jnp.logical_not(lo)
            qp = q_ref[:, qb * LANE:(qb + 1) * LANE]
            qm = jnp.where(half, qp, jnp.zeros_like(qp))
            dop = do_[:, qb * LANE:(qb + 1) * LANE]
            dom = jnp.where(half, dop, 0.0)
            kk = k2[:, kb * LANE:(kb + 1) * LANE]
            vv = v2[:, kb * LANE:(kb + 1) * LANE]
            if kh != qh:
                kk = pltpu.roll(kk, HEAD_DIM, 1)
                vv = pltpu.roll(vv, HEAD_DIM, 1)
            logits = jnp.where(valid, _mm(qm, kk, NT) * scale, -1e30)
            lse_h = lse_ref[:, h:h + 1]
            p = jnp.exp(logits - lse_h)
            dl = delta[:, h:h + 1]
            ds = p * (_mm(dom, vv, NT) - dl) * scale
            psink = jnp.exp(sk_ref[:, h:h + 1] - lse_h)
            dsk = dsk - jnp.sum(psink * dl, axis=0, keepdims=True) * (lane1 == h).astype(F32)
            res[h] = _mm(ds, kk)
            dkh = _mm(ds, qm, TN)
            dvh = _mm(p, dom, TN)
            if kh != qh:
                dkh = pltpu.roll(dkh, HEAD_DIM, 1)
                dvh = pltpu.roll(dvh, HEAD_DIM, 1)
            dk2[kb] = dk2[kb] + dkh
            dv2[kb] = dv2[kb] + dvh
            if qh == 1:
                dq_ref[:, qb * LANE:(qb + 1) * LANE] = jnp.where(lo, res[h - 1], res[h])
        for kb in range(kw // LANE):
            dkb_ref[:, kb * LANE:(kb + 1) * LANE] = dk2[kb][0:WINDOW, :]
            dka_ref[:, kb * LANE:(kb + 1) * LANE] = dk2[kb][WINDOW:, :]
            dvb_ref[:, kb * LANE:(kb + 1) * LANE] = dv2[kb][0:WINDOW, :]
            dva_ref[:, kb * LANE:(kb + 1) * LANE] = dv2[kb][WINDOW:, :]
        _acc(dsk_ref, i, dsk)

    return _call(body, name, (nb,),
                 [_rs(WINDOW, qw), _rs(WINDOW, kw), _rs(WINDOW, kw, 0, -1), _rs(WINDOW, kw), _rs(WINDOW, kw, 0, -1),
                  _rs(WINDOW, qw), _rs(WINDOW, qw), _rs(WINDOW, LANE), _ps((1, LANE))],
                 [_rs(WINDOW, qw)] + [_rs(WINDOW, kw)] * 4 + [_ps((1, LANE))],
                 [S((n, qw), F32)] + [S((n, kw), F32)] * 4 + [S((1, LANE), F32)],
                 sem=("arbitrary",))(q, k, k, v, v, o, do, lse, sinks)


def attn_grad_merge(dq, dka, dkb, dva, dvb, cos, sin, name):
    n, qw = dq.shape
    kw = dka.shape[1]
    nb = n // WINDOW
    w = qw + 2 * kw

    def body(dq_ref, dka_ref, dkb_ref, dva_ref, dvb_ref, c_ref, s_ref, o_ref, db_ref):
        i = pl.program_id(0)
        c, s = c_ref[...], -s_ref[...]
        nxt = (i < nb - 1).astype(F32)

        @pl.when(i == 0)
        def _():
            db_ref[...] = jnp.zeros_like(db_ref)

        def put(c0, val):
            o_ref[:, c0:c0 + val.shape[1]] = val.astype(o_ref.dtype)
            db_ref[:, c0:c0 + val.shape[1]] += jnp.sum(val, axis=0, keepdims=True)

        for j in range(qw // LANE):
            put(j * LANE, _rope128(dq_ref[:, j * LANE:(j + 1) * LANE], c, s))
        for j in range(kw // LANE):
            sl = slice(j * LANE, (j + 1) * LANE)
            put(qw + j * LANE, _rope128(dka_ref[:, sl] + dkb_ref[:, sl] * nxt, c, s))
        put(qw + kw, dva_ref[...] + dvb_ref[...] * nxt)

    return _call(body, name, (nb,),
                 [_rs(WINDOW, qw), _rs(WINDOW, kw), _rs(WINDOW, kw, 0, 1, nb), _rs(WINDOW, kw), _rs(WINDOW, kw, 0, 1, nb),
                  _rs(WINDOW, LANE), _rs(WINDOW, LANE)],
                 [_rs(WINDOW, w), _ps((1, w))], [S((n, w), MXU_DTYPE), S((1, w), F32)],
                 sem=("arbitrary",))(dq, dka, dkb, dva, dvb, cos, sin)


def _row(v):
    return v.reshape(1, -1)


def _pad_lanes(v, width=LANE):
    return jnp.pad(v.reshape(1, -1), ((0, 0), (0, width - v.size)))


def ffn_fwd(h, g, w_in, w_out, tag):
    xn = rms_fwd(h, _row(g), f"{tag}_rms")
    u = matmul(xn, w_in, "nn", f"{tag}_in")
    hm = swiglu_fwd(u, f"{tag}_act")
    return matmul(hm, w_out, "nn", f"{tag}_out", scale=0.5, res=h), (h, xn, u, hm)


class GradSink:
    ORDER = ("ffn1_w_out", "ffn2_w_out", "ple_gate_w", "att_w_o", "hyb_w_out", "ffn1_w_in", "ffn2_w_in", "att_w_qkv",
             "ple_proj_w", "hyb_w_in")

    def __init__(self, shard_shapes):
        self.where, rows = {}, {}
        for k in self.ORDER:
            depth, r, c = shard_shapes[k]
            offs = [rows.get(c, 0) + i * r for i in range(depth)]
            rows[c] = offs[-1] + r
            self.where[k] = (c, "r" if SHARD_AXIS[k] == 1 else "c", offs, r)
        self.bufs = {c: lax.empty((N_CHIPS, r, c), F32) for c, r in rows.items()}

    def mm(self, k, layer, a, b, name, scale=None, c0=0):
        c, kind, offs, r = self.where[k]
        slot = Slot(self.bufs[c], kind, r if kind == "r" else c, offs[layer], c0)
        self.bufs[c] = matmul(a, b, "tn", name, scale=scale, into=slot)

    def put(self, k, chip_major):
        self.bufs[self.where[k][0]] = chip_major

    def take(self, k, reduced):
        c, _, offs, r = self.where[k]
        return reduced[c][offs[0]:offs[0] + len(offs) * r].reshape(len(offs), r, c)


def ffn_bwd(dh, g, w_in, w_out, saved, tag, sink, keys, layer, colsum=False):
    h, xn, u, hm = saved
    sink.mm(keys[1], layer, hm, dh, f"{tag}_dwout", scale=0.5)
    dhm = matmul(dh, w_out, "nt", f"{tag}_dhm", scale=0.5)
    du = swiglu_bwd(u, dhm, f"{tag}_dact")
    sink.mm(keys[0], layer, xn, du, f"{tag}_dwin")
    dxn = matmul(du, w_in, "nt", f"{tag}_dxn")
    outs = rms_bwd(h, _row(g), dxn, dh, f"{tag}_drms", colsum=colsum)
    return (outs[0], outs[1].reshape(-1)) + ((outs[2],) if colsum else ())


def _hyb_params(w):
    d = w["conv_dw_b"].size
    inner = SSM_HEADS * HEAD_DIM
    main = 3 * d + w["ssm_conv_b"].size
    return dict(
        w_main=w["hyb_w_in"][:, :main], w_dt=jnp.pad(w["hyb_w_in"][:, main:], ((0, 0), (0, LANE - SSM_HEADS))),
        cw=jnp.pad(w["conv_dw_w"], ((0, 32 - CONV_WIDTH), (0, 0))), cb=_row(w["conv_dw_b"]),
        lg=_row(w["conv_ln_g"]), lb=_row(w["conv_ln_b"]),
        sw=jnp.pad(w["ssm_conv_w"], ((0, 8 - SSM_CONV), (0, 0))), sb=_row(w["ssm_conv_b"]),
        dtb=_pad_lanes(w["ssm_dt_bias"]), al_row=_pad_lanes(w["ssm_a_log"]), al_col=w["ssm_a_log"].reshape(-1, 1),
        dfull=_row(jnp.repeat(w["ssm_d"], HEAD_DIM)), gamma=_row(w["ssm_norm"]),
        wo_top=w["hyb_w_out"][:d], wo_bot=w["hyb_w_out"][d:], d=d, inner=inner, main=main)


def hyb_fwd(h, w, tag):
    q = _hyb_params(w)
    xn = rms_fwd(h, _row(w["norm_mix"]), f"{tag}_rms")
    proj = matmul(xn, q["w_main"], "nn", f"{tag}_in")
    dtr = matmul(xn, q["w_dt"], "nn", f"{tag}_in_dt")
    u, u1 = conv_group_fwd(proj, q["cw"], q["cb"], q["lg"], q["lb"], f"{tag}_conv")
    pre, xs, bc, dt = ssm_conv_fwd(proj, dtr, q["sw"], q["sb"], q["dtb"], f"{tag}_sconv")
    dtT = dt[:, :SSM_HEADS].T
    yssd, hs = ssd_fwd(xs, bc, dt, dtT, q["al_row"], q["al_col"], f"{tag}_ssd")
    y = ssm_gate_fwd(yssd, xs, proj, q["dfull"], q["gamma"], f"{tag}_gate")
    h2 = matmul(u, q["wo_top"], "nn", f"{tag}_out_a", res=h)
    h2 = matmul(y, q["wo_bot"], "nn", f"{tag}_out_b", res=h2)
    return h2, (h, xn, proj, u, u1, pre, xs, bc, dt, dtT, yssd, hs, y)


def hyb_bwd(dh, w, saved, tag, sink, layer):
    q = _hyb_params(w)
    h, xn, proj, u, u1, pre, xs, bc, dt, dtT, yssd, hs, y = saved
    du = matmul(dh, q["wo_top"], "nt", f"{tag}_du")
    dy3 = matmul(dh, q["wo_bot"], "nt", f"{tag}_dy")
    sink.mm("hyb_w_out", layer, u, dh, f"{tag}_dwo_a", c0=0)
    sink.mm("hyb_w_out", layer, y, dh, f"{tag}_dwo_b", c0=N_CHIPS // 2)
    dproj, dcw, dcb, dlg, dlb = conv_group_bwd(du, u1, proj, q["cw"], q["lg"], q["lb"], f"{tag}_dconv")
    dyssd, dxs_skip, dproj, dgamma, dd = ssm_gate_bwd(dy3, yssd, xs, proj, q["dfull"], q["gamma"], dproj, f"{tag}_dgate")
    dxs, dbc, ddtr, dalog, ddtb = ssd_bwd(xs, bc, dt, dtT, q["al_row"], q["al_col"], hs, dyssd, dxs_skip, f"{tag}_dssd")
    dproj, dsw, dsb = ssm_conv_bwd(dxs, dbc, pre, proj, q["sw"], dproj, f"{tag}_dsconv")
    dw_in = jnp.concatenate([matmul(xn, dproj, "tn", f"{tag}_dwin"),
                             matmul(xn, ddtr, "tn", f"{tag}_dwin_dt")[:, :SSM_HEADS]], axis=1)
    sink.put("hyb_w_in", _to_chip_major(dw_in, 1))
    dxn = matmul(dproj, q["w_main"], "nt", f"{tag}_dxn")
    dxn = matmul(ddtr, q["w_dt"], "nt", f"{tag}_dxn_dt", res=dxn)
    dh2, dg = rms_bwd(h, _row(w["norm_mix"]), dxn, dh, f"{tag}_drms")
    grads = dict(norm_mix=dg.reshape(-1), conv_dw_w=dcw[:CONV_WIDTH], conv_dw_b=dcb.reshape(-1),
                 conv_ln_g=dlg.reshape(-1), conv_ln_b=dlb.reshape(-1), ssm_conv_w=dsw[:SSM_CONV], ssm_conv_b=dsb.reshape(-1),
                 ssm_dt_bias=ddtb.reshape(-1), ssm_a_log=dalog.reshape(-1), ssm_d=dd.reshape(-1), ssm_norm=dgamma.reshape(-1))
    return dh2, grads


def rope_tables(n):
    half = HEAD_DIM // 2
    inv = ROPE_THETA ** (-jnp.arange(0, HEAD_DIM, 2, dtype=F32) / HEAD_DIM)
    ang = jnp.arange(n, dtype=F32)[:, None] * inv[None, :]
    cos, sin = jnp.cos(ang), jnp.sin(ang)
    reps = LANE // HEAD_DIM
    return jnp.tile(jnp.concatenate([cos, cos], axis=1), (1, reps)), jnp.tile(jnp.concatenate([-sin, sin], axis=1), (1, reps))


def att_fwd(h, w, tables, tag):
    cos, sin = tables
    xn = rms_fwd(h, _row(w["norm_mix"]), f"{tag}_rms")
    qkv = matmul(xn, w["att_w_qkv"], "nn", f"{tag}_qkv", bias=_row(w["att_b_qkv"]))
    q, k, v = rope_fwd(qkv, cos, sin, f"{tag}_rope")
    sinks = _pad_lanes(w["att_sinks"])
    o, lse = attn_fwd(q, k, v, sinks, f"{tag}_attn")
    h2 = matmul(o, w["att_w_o"], "nn", f"{tag}_o", bias=_row(w["att_b_o"]), res=h)
    return h2, (h, xn, q, k, v, o, lse, sinks)


def att_bwd(dh, dh_colsum, w, saved, tables, tag, sink, layer):
    cos, sin = tables
    h, xn, q, k, v, o, lse, sinks = saved
    do = matmul(dh, w["att_w_o"], "nt", f"{tag}_do")
    sink.mm("att_w_o", layer, o, dh, f"{tag}_dwo")
    dq, dka, dkb, dva, dvb, dsk = attn_bwd(q, k, v, o, do, lse, sinks, f"{tag}_dattn")
    dqkv, dbqkv = attn_grad_merge(dq, dka, dkb, dva, dvb, cos, sin, f"{tag}_drope")
    sink.mm("att_w_qkv", layer, xn, dqkv, f"{tag}_dwqkv")
    dxn = matmul(dqkv, w["att_w_qkv"], "nt", f"{tag}_dxn")
    dh2, dg = rms_bwd(h, _row(w["norm_mix"]), dxn, dh, f"{tag}_drms")
    grads = dict(norm_mix=dg.reshape(-1), att_b_qkv=dbqkv.reshape(-1), att_sinks=dsk[0, :ATT_HEADS],
                 att_b_o=dh_colsum.reshape(-1))
    return dh2, grads


def ple_block_fwd(h, pe, w, tag):
    xn = rms_fwd(h, _row(w["ple_norm"]), f"{tag}_rms")
    gl = matmul(xn, w["ple_gate_w"], "nn", f"{tag}_gate")
    pp = matmul(pe, w["ple_proj_w"], "nn", f"{tag}_proj")
    return ple_fwd(h, gl, pp, f"{tag}_mix"), (h, xn, gl, pp, pe)


def ple_block_bwd(dh, w, saved, tag, sink, layer):
    h, xn, gl, pp, pe = saved
    dpp, dgl = ple_bwd(dh, gl, pp, f"{tag}_dmix")
    sink.mm("ple_proj_w", layer, pe, dpp, f"{tag}_dwp")
    sink.mm("ple_gate_w", layer, xn, dgl, f"{tag}_dwg")
    dxn = matmul(dgl, w["ple_gate_w"], "nt", f"{tag}_dxn")
    dh2, dg = rms_bwd(h, _row(w["ple_norm"]), dxn, dh, f"{tag}_drms")
    return dh2, dict(ple_norm=dg.reshape(-1))


PER_LAYER = ("norm_ffn1", "ffn1_w_in", "ffn1_w_out", "norm_mix", "norm_ffn2", "ffn2_w_in", "ffn2_w_out",
             "ple_norm", "ple_gate_w", "ple_proj_w")
EVEN_ONLY = ("hyb_w_in", "conv_dw_w", "conv_dw_b", "conv_ln_g", "conv_ln_b", "ssm_conv_w", "ssm_conv_b",
             "ssm_dt_bias", "ssm_a_log", "ssm_d", "ssm_norm", "hyb_w_out")
ODD_ONLY = ("att_w_qkv", "att_b_qkv", "att_sinks", "att_w_o", "att_b_o")


def _layer_index(k, i):
    if k in PER_LAYER:
        return i
    if k in (EVEN_ONLY if i % 2 == 0 else ODD_ONLY):
        return i // 2
    return None


def trunk_fwd_bwd(x, pe, target, layers, final_norm, sink):
    depth = len(layers)
    tables = rope_tables(x.shape[0])
    h = x
    saved = []
    for i, w in enumerate(layers):
        h, s1 = ffn_fwd(h, w["norm_ffn1"], w["ffn1_w_in"], w["ffn1_w_out"], f"l{i}_ffn1")
        if i % 2 == 0:
            h, s2 = hyb_fwd(h, w, f"l{i}_hyb")
        else:
            h, s2 = att_fwd(h, w, tables, f"l{i}_att")
        h, s3 = ffn_fwd(h, w["norm_ffn2"], w["ffn2_w_in"], w["ffn2_w_out"], f"l{i}_ffn2")
        h, s4 = ple_block_fwd(h, pe[i], w, f"l{i}_ple")
        saved.append((s1, s2, s3, s4))
    dh, dgf, loss = loss_head(h, _row(final_norm), target, "loss_head")
    grads = {}
    for i in reversed(range(depth)):
        w = layers[i]
        s1, s2, s3, s4 = saved[i]
        dh, g = ple_block_bwd(dh, w, s4, f"l{i}_ple", sink, i)
        odd = i % 2 == 1
        out = ffn_bwd(dh, w["norm_ffn2"], w["ffn2_w_in"], w["ffn2_w_out"], s3, f"l{i}_ffn2", sink,
                      ("ffn2_w_in", "ffn2_w_out"), i, colsum=odd)
        dh = out[0]
        g.update(norm_ffn2=out[1])
        if odd:
            dh, gm = att_bwd(dh, out[2], w, s2, tables, f"l{i}_att", sink, i // 2)
        else:
            dh, gm = hyb_bwd(dh, w, s2, f"l{i}_hyb", sink, i // 2)
        g.update(gm)
        out = ffn_bwd(dh, w["norm_ffn1"], w["ffn1_w_in"], w["ffn1_w_out"], s1, f"l{i}_ffn1", sink,
                      ("ffn1_w_in", "ffn1_w_out"), i)
        dh = out[0]
        g.update(norm_ffn1=out[1])
        for k, v in g.items():
            grads.setdefault(k, []).insert(0, v)
    grads = {k: jnp.stack(v) for k, v in grads.items()}
    grads["final_norm"] = dgf.reshape(-1)
    return loss, dh, grads


def _me():
    return lax.axis_index("x"), lax.axis_index("y"), lax.axis_index("c")


def _flip(v, f):
    return 1 - v if f else v


def _remote(src, dst, send_sems, recv_sems, k, dev):
    return pltpu.make_async_remote_copy(src_ref=src, dst_ref=dst, send_sem=send_sems.at[k], recv_sem=recv_sems.at[k],
                                        device_id=dev, device_id_type=MESH)


CHIP_FLIPS = ((1, 0), (0, 1), (1, 1))
DEV_FLIPS = tuple((fx, fy, fc) for fx in (0, 1) for fy in (0, 1) for fc in (0, 1))[1:]


def all_gather_chips(xs, name):
    na = len(xs)
    halves = [x.shape[0] // 2 for x in xs]
    assert all(x.shape[0] % 2 == 0 for x in xs)

    def body(*refs):
        x_refs, out_refs = refs[:na], refs[na:2 * na]
        send_sems, recv_sems = refs[2 * na:]
        mx, my, mc = _me()
        chip = 2 * mx + my
        sib = (mx, my, 1 - mc)
        peers = [(_flip(mx, fx), _flip(my, fy)) for fx, fy in CHIP_FLIPS]

        def rows(a, ch, hc):
            return out_refs[a].at[ch, pl.ds(hc * halves[a], halves[a]), :]

        def src(a):
            return x_refs[a].at[pl.ds(mc * halves[a], halves[a]), :]

        first = [_remote(src(a), rows(a, chip, mc), send_sems, recv_sems, 6 * a + j, (px, py, mc))
                 for j, (px, py) in enumerate(peers) for a in range(na)]
        for cp in first:
            cp.start()
        passed = []
        for j, (px, py) in enumerate(peers):
            for a in range(na):
                landed = rows(a, 2 * px + py, mc)
                _remote(src(a), landed, send_sems, recv_sems, 6 * a + j, (px, py, mc)).wait_recv()
                fw = _remote(landed, landed, send_sems, recv_sems, 6 * a + 3 + j, sib)
                fw.start()
                passed.append(fw)
        for j, (px, py) in enumerate(peers):
            for a in range(na):
                _remote(src(a), rows(a, 2 * px + py, 1 - mc), send_sems, recv_sems, 6 * a + 3 + j, sib).wait_recv()
        for cp in first + passed:
            cp.wait_send()

    outs = pl.pallas_call(
        body, name=name, out_shape=[S((N_CHIPS,) + x.shape, x.dtype) for x in xs], in_specs=[ANY] * na, out_specs=[ANY] * na,
        scratch_shapes=[pltpu.SemaphoreType.DMA((6 * na,)), pltpu.SemaphoreType.DMA((6 * na,))])(*xs)
    chip = 2 * lax.axis_index("x") + lax.axis_index("y")
    return [lax.dynamic_update_slice_in_dim(o, x[None], chip, axis=0) for o, x in zip(outs, xs)]


def all_gather_devices(v, name):
    r, l = v.shape

    def body(v_ref, out_ref, send_sems, recv_sems):
        mx, my, mc = _me()
        me = 4 * mx + 2 * my + mc
        peers = [(_flip(mx, fx), _flip(my, fy), _flip(mc, fc)) for fx, fy, fc in DEV_FLIPS]
        sends = [_remote(v_ref, out_ref.at[me], send_sems, recv_sems, j, p) for j, p in enumerate(peers)]
        for cp in sends:
            cp.start()
        for j, (px, py, pc) in enumerate(peers):
            _remote(v_ref, out_ref.at[4 * px + 2 * py + pc], send_sems, recv_sems, j, (px, py, pc)).wait_recv()
        for cp in sends:
            cp.wait_send()

    out = pl.pallas_call(
        body, name=name, out_shape=S((N_DEV, r, l), v.dtype), in_specs=[ANY], out_specs=ANY,
        scratch_shapes=[pltpu.SemaphoreType.DMA((7,)), pltpu.SemaphoreType.DMA((7,))])(v)
    me = 4 * lax.axis_index("x") + 2 * lax.axis_index("y") + lax.axis_index("c")
    return lax.dynamic_update_slice_in_dim(out, v[None], me, axis=0)


def sum_devices(g8, name):
    nd, r, l = g8.shape
    tile = r
    for t in (512, 256, 128, 64, 32, 16, 8):
        if r % t == 0:
            tile = t
            break

    def body(g_ref, o_ref):
        acc = g_ref[0]
        for d in range(1, nd):
            acc = acc + g_ref[d]
        o_ref[...] = acc

    return _call(body, name, (r // tile,), [pl.BlockSpec((nd, tile, l), lambda i: (0, i, 0))], _rs(tile, l), S((r, l), F32),
                 sem=("parallel",))(g8)


def exchange_halves(gs, name):
    na = len(gs)
    nch = gs[0].shape[0]

    def body(*refs):
        g_refs, out_refs = refs[:na], refs[na:2 * na]
        send_sems, recv_sems = refs[2 * na:]
        mx, my, mc = _me()
        sib = (mx, my, 1 - mc)
        cps = []
        for a in range(na):
            half = gs[a].shape[1] // 2
            for j in range(nch):
                cps.append(_remote(g_refs[a].at[j, pl.ds((1 - mc) * half, half), :], out_refs[a].at[j],
                                   send_sems, recv_sems, nch * a + j, sib))
        for cp in cps:
            cp.start()
        for cp in cps:
            cp.wait_recv()
        for cp in cps:
            cp.wait_send()

    return pl.pallas_call(
        body, name=name, out_shape=[S((nch, g.shape[1] // 2, g.shape[2]), g.dtype) for g in gs],
        in_specs=[ANY] * na, out_specs=[ANY] * na,
        scratch_shapes=[pltpu.SemaphoreType.DMA((nch * na,)), pltpu.SemaphoreType.DMA((nch * na,))])(*gs)


def add_halves(g4, got, name):
    nch, r, l = g4.shape
    half = r // 2
    tile = _pick_rows(half)
    nt = half // tile

    def body(g_ref, r_ref, a_ref, own_ref):
        j = pl.program_id(1)
        chip = 2 * lax.axis_index("x") + lax.axis_index("y")
        val = g_ref[0] + r_ref[0]
        a_ref[0] = val.astype(a_ref.dtype)

        @pl.when(j == chip)
        def _():
            own_ref[...] = val

    return pl.pallas_call(
        body, name=name, grid=(nt, nch),
        in_specs=[pl.BlockSpec((1, tile, l), lambda i, j: (j, lax.axis_index("c") * nt + i, 0)),
                  pl.BlockSpec((1, tile, l), lambda i, j: (j, i, 0))],
        out_specs=[pl.BlockSpec((1, tile, l), lambda i, j: (j, i, 0)), pl.BlockSpec((tile, l), lambda i, j: (i, 0))],
        out_shape=[S((nch, half, l), MXU_DTYPE), S((half, l), F32)],
        compiler_params=pltpu.CompilerParams(dimension_semantics=("parallel", "arbitrary"), vmem_limit_bytes=VMEM_LIMIT))(g4, got)


def _pick_rows(r, cap=512):
    for t in (512, 256, 128, 64, 32, 16):
        if t <= cap and r % t == 0:
            return t
    return r


def exchange_chips(parts, name):
    na = len(parts)

    def body(*refs):
        a_refs, out_refs = refs[:na], refs[na:2 * na]
        send_sems, recv_sems = refs[2 * na:]
        mx, my, mc = _me()
        peers = [(_flip(mx, fx), _flip(my, fy)) for fx, fy in CHIP_FLIPS]
        cps = [_remote(a_refs[a].at[2 * px + py], out_refs[a].at[j], send_sems, recv_sems, 3 * a + j, (px, py, mc))
               for j, (px, py) in enumerate(peers) for a in range(na)]
        for cp in cps:
            cp.start()
        for cp in cps:
            cp.wait_recv()
        for cp in cps:
            cp.wait_send()

    return pl.pallas_call(
        body, name=name, out_shape=[S((3,) + p.shape[1:], p.dtype) for p in parts], in_specs=[ANY] * na, out_specs=[ANY] * na,
        scratch_shapes=[pltpu.SemaphoreType.DMA((3 * na,)), pltpu.SemaphoreType.DMA((3 * na,))])(*parts)


def add_chips(own, got, name):
    h, l = own.shape
    tile = _pick_rows(h)

    def body(o_ref, g_ref, out_ref):
        out_ref[...] = ((o_ref[...] + g_ref[0].astype(F32)) + g_ref[1].astype(F32)) + g_ref[2].astype(F32)

    nt = h // tile
    return _call(body, name, (nt,), [_rs(tile, l), pl.BlockSpec((3, tile, l), lambda i: (0, i, 0))],
                 pl.BlockSpec((tile, l), lambda i: (lax.axis_index("c") * nt + i, 0)),
                 S((2 * h, l), F32), sem=("parallel",))(own, got)


def join_halves(bufs, name):
    na = len(bufs)

    def body(*refs):
        out_refs = refs[na:2 * na]
        send_sems, recv_sems = refs[2 * na:]
        mx, my, mc = _me()
        sib = (mx, my, 1 - mc)

        def half(a, hc):
            h = bufs[a].shape[0] // 2
            return out_refs[a].at[pl.ds(hc * h, h), :]

        cps = [_remote(half(a, mc), half(a, mc), send_sems, recv_sems, a, sib) for a in range(na)]
        for cp in cps:
            cp.start()
        for a in range(na):
            _remote(half(a, mc), half(a, 1 - mc), send_sems, recv_sems, a, sib).wait_recv()
        for cp in cps:
            cp.wait_send()

    return pl.pallas_call(
        body, name=name, out_shape=[S(b.shape, b.dtype) for b in bufs], in_specs=[ANY] * na, out_specs=[ANY] * na,
        input_output_aliases={a: a for a in range(na)},
        scratch_shapes=[pltpu.SemaphoreType.DMA((na,)), pltpu.SemaphoreType.DMA((na,))])(*bufs)


def reduce_scatter(gs, tag):
    got = exchange_halves(gs, f"{tag}_d2d")
    sums = [add_halves(g, r, f"{tag}_add1_{i}") for i, (g, r) in enumerate(zip(gs, got))]
    got2 = exchange_chips([a for a, _ in sums], f"{tag}_ici")
    parts = [add_chips(own, r, f"{tag}_add2_{i}") for i, ((_, own), r) in enumerate(zip(sums, got2))]
    return join_halves(parts, f"{tag}_join")


PACK_L = 1024
BIG_ROW_MULT = 512


def _pack(arrs, dtype, row_mult, lead=None):
    lead_shape = () if lead is None else arrs[0].shape[:lead]
    flat = jnp.concatenate([a.astype(dtype).reshape(lead_shape + (-1,)) for a in arrs], axis=-1)
    n = flat.shape[-1]
    unit = row_mult * PACK_L
    total = -(-n // unit) * unit
    flat = jnp.pad(flat, [(0, 0)] * len(lead_shape) + [(0, total - n)])
    return flat.reshape(lead_shape + (total // PACK_L, PACK_L))


def _unpack(packed, shapes, lead=None):
    lead_shape = () if lead is None else packed.shape[:lead]
    flat = packed.reshape(lead_shape + (-1,))
    out, off = [], 0
    for shp in shapes:
        n = int(np.prod(shp))
        out.append(flat[..., off:off + n].reshape(lead_shape + tuple(shp)))
        off += n
    return out


def _to_full(gathered, axis):
    t = jnp.moveaxis(gathered, 0, axis)
    shp = t.shape
    return t.reshape(shp[:axis] + (shp[axis] * shp[axis + 1],) + shp[axis + 2:])


def _to_chip_major(full, axis):
    shp = full.shape
    t = full.reshape(shp[:axis] + (N_CHIPS, shp[axis] // N_CHIPS) + shp[axis + 1:])
    return jnp.moveaxis(t, axis, 0)


WEIGHTS = ("norm_ffn1", "ffn1_w_in", "ffn1_w_out", "norm_mix", "norm_ffn2", "ffn2_w_in", "ffn2_w_out", "ple_norm",
           "ple_gate_w", "ple_proj_w", "hyb_w_in", "conv_dw_w", "conv_dw_b", "conv_ln_g", "conv_ln_b", "ssm_conv_w",
           "ssm_conv_b", "ssm_dt_bias", "ssm_a_log", "ssm_d", "ssm_norm", "hyb_w_out", "att_w_qkv", "att_b_qkv",
           "att_sinks", "att_w_o", "att_b_o", "final_norm")
SHARD_AXIS = dict(ffn1_w_in=2, ffn1_w_out=1, ffn2_w_in=2, ffn2_w_out=1, ple_gate_w=1, ple_proj_w=2, hyb_w_in=2,
                  conv_dw_w=2, ssm_conv_w=2, hyb_w_out=1, att_w_qkv=2, att_b_qkv=1, att_w_o=1, att_b_o=1)
BIG = ("ffn1_w_in", "ffn1_w_out", "ffn2_w_in", "ffn2_w_out", "ple_gate_w", "ple_proj_w", "hyb_w_in", "hyb_w_out",
       "att_w_qkv", "att_w_o")
ODD_WIDTH = "hyb_w_in"
BIG_FLAT = tuple(k for k in BIG if k != ODD_WIDTH)
SMALL_SHARDED = ("conv_dw_w", "ssm_conv_w", "att_b_qkv", "att_b_o")
SMALL = tuple(k for k in WEIGHTS if k not in BIG)


def _step(x, p, target, w, m, v):
    mx, my = lax.axis_index("x"), lax.axis_index("y")
    chip = 2 * mx + my

    depth = w["norm_ffn1"].shape[0]
    order = [(k, i) for i in range(depth) for k in BIG if _layer_index(k, i) is not None]
    gathered = all_gather_chips([w[k][_layer_index(k, i)].astype(MXU_DTYPE) for k, i in order], "gather_weights")
    small_g = all_gather_devices(_pack([w[k] for k in SMALL_SHARDED], F32, 8), "gather_small")[0::2]
    small_full = {k: _to_full(g, SHARD_AXIS[k])
                  for k, g in zip(SMALL_SHARDED, _unpack(small_g, [w[k].shape for k in SMALL_SHARDED], lead=1))}
    layers = [{k: small_full.get(k, w[k])[_layer_index(k, i)] for k in SMALL if _layer_index(k, i) is not None}
              for i in range(depth)]
    for (k, i), g in zip(order, gathered):
        if k == ODD_WIDTH:
            layers[i][k] = _to_full(g, 1)
        elif SHARD_AXIS[k] == 2:
            layers[i][k] = ColSharded(g)
        else:
            layers[i][k] = g.reshape(-1, g.shape[-1])

    sink = GradSink({k: w[k].shape for k in BIG})
    loss, dx, grads = trunk_fwd_bwd(x[0], p[:, 0], target[0], layers, w["final_norm"], sink)

    widths = list(sink.bufs)
    reduced = dict(zip(widths, reduce_scatter([sink.bufs[c] for c in widths], "grads")))
    g_out = {k: sink.take(k, reduced) for k in BIG}
    vec = _pack([loss[0:1, 0:1]] + [grads[k] for k in SMALL], F32, 8)
    vec = sum_devices(all_gather_devices(vec, "gather_vectors"), "sum_vectors")
    parts = _unpack(vec, [(1, 1)] + [grads[k].shape for k in SMALL])
    loss_out = parts[0].reshape(())
    for k, g in zip(SMALL, parts[1:]):
        if k in SHARD_AXIS:
            ax = SHARD_AXIS[k]
            g = lax.dynamic_slice_in_dim(g, chip * w[k].shape[ax], w[k].shape[ax], axis=ax)
        g_out[k] = g

    delta, new_m, new_v = {}, {}, {}
    for k in BIG:
        shp = w[k].shape
        two_d = lambda a: a.reshape(-1, shp[-1])
        d_, m_, v_ = adamw(two_d(w[k]), two_d(g_out[k]), two_d(m[k]), two_d(v[k]), f"adamw_{k}")
        delta[k], new_m[k], new_v[k] = d_.reshape(shp), m_.reshape(shp), v_.reshape(shp)
    shapes = [w[k].shape for k in SMALL]
    packed = [_pack([src[k] for k in SMALL], F32, 8) for src in (w, g_out, m, v)]
    outs = adamw(*packed, "adamw_small")
    for dst, o in zip((delta, new_m, new_v), outs):
        for k, a in zip(SMALL, _unpack(o, shapes)):
            dst[k] = a
    return ((loss_out, dx[None]) + tuple(g_out[k] for k in WEIGHTS) + tuple(delta[k] for k in WEIGHTS)
            + tuple(new_m[k] for k in WEIGHTS) + tuple(new_v[k] for k in WEIGHTS))


def kernel(x, p, norm_ffn1, ffn1_w_in, ffn1_w_out, norm_mix, norm_ffn2, ffn2_w_in, ffn2_w_out, ple_norm, ple_gate_w, ple_proj_w, hyb_w_in, conv_dw_w, conv_dw_b, conv_ln_g, conv_ln_b, ssm_conv_w, ssm_conv_b, ssm_dt_bias, ssm_a_log, ssm_d, ssm_norm, hyb_w_out, att_w_qkv, att_b_qkv, att_sinks, att_w_o, att_b_o, final_norm, loss_target, m_norm_ffn1, m_ffn1_w_in, m_ffn1_w_out, m_norm_mix, m_norm_ffn2, m_ffn2_w_in, m_ffn2_w_out, m_ple_norm, m_ple_gate_w, m_ple_proj_w, m_hyb_w_in, m_conv_dw_w, m_conv_dw_b, m_conv_ln_g, m_conv_ln_b, m_ssm_conv_w, m_ssm_conv_b, m_ssm_dt_bias, m_ssm_a_log, m_ssm_d, m_ssm_norm, m_hyb_w_out, m_att_w_qkv, m_att_b_qkv, m_att_sinks, m_att_w_o, m_att_b_o, m_final_norm, v_norm_ffn1, v_ffn1_w_in, v_ffn1_w_out, v_norm_mix, v_norm_ffn2, v_ffn2_w_in, v_ffn2_w_out, v_ple_norm, v_ple_gate_w, v_ple_proj_w, v_hyb_w_in, v_conv_dw_w, v_conv_dw_b, v_conv_ln_g, v_conv_ln_b, v_ssm_conv_w, v_ssm_conv_b, v_ssm_dt_bias, v_ssm_a_log, v_ssm_d, v_ssm_norm, v_hyb_w_out, v_att_w_qkv, v_att_b_qkv, v_att_sinks, v_att_w_o, v_att_b_o, v_final_norm):
    given = locals()
    w = {k: given[k] for k in WEIGHTS}
    m = {k: given["m_" + k] for k in WEIGHTS}
    v = {k: given["v_" + k] for k in WEIGHTS}
    return _step(x, p, loss_target, w, m, v)
```

```python
import functools
import math

import numpy as np
import jax
import jax.numpy as jnp
from jax import lax
from jax.experimental import pallas as pl
from jax.experimental.pallas import tpu as pltpu

F32 = jnp.float32
BF16 = jnp.bfloat16
MXU_DTYPE = jnp.bfloat16
S = jax.ShapeDtypeStruct
MESH = pl.DeviceIdType.MESH

V7X_VMEM_BYTES = 64 * 2**20
VMEM_LIMIT = 48 * 2**20
LANE = 128

EPS = 1e-6
SSM_HEADS = 16
HEAD_DIM = 64
SSM_GROUPS = 2
SSM_STATE = 128
SSM_CONV = 4
CHUNK = 128
CONV_WIDTH = 31
ATT_HEADS = 16
ATT_KV_HEADS = 4
WINDOW = 128
ROPE_THETA = 10000.0
ADAM_LR = 0.001
ADAM_B1 = 0.9
ADAM_B2 = 0.999
ADAM_EPS = 1e-08
ADAM_WD = 0.01
ADAM_STEP = 10

N_CHIPS = 4
N_DEV = 8

NN = ((1,), (0,))
NT = ((1,), (1,))
TN = ((0,), (0,))


def _mm(a, b, dims=NN):
    return lax.dot_general(a.astype(MXU_DTYPE), b.astype(MXU_DTYPE), (dims, ((), ())), preferred_element_type=F32)


def _split3(a):
    hi = a.astype(BF16)
    r = a - hi.astype(F32)
    mid = r.astype(BF16)
    lo = (r - mid.astype(F32)).astype(BF16)
    return hi, mid, lo


def _mm01(a, onehot, dims=NN):
    o = onehot.astype(BF16)
    out = None
    for part in _split3(a):
        t = lax.dot_general(part, o, (dims, ((), ())), preferred_element_type=F32)
        out = t if out is None else out + t
    return out


def _01mm(onehot, a):
    o = onehot.astype(BF16)
    out = None
    for part in _split3(a):
        t = lax.dot_general(o, part, (NN, ((), ())), preferred_element_type=F32)
        out = t if out is None else out + t
    return out


def _sigmoid(x):
    return 0.5 * jnp.tanh(0.5 * x) + 0.5


def _softplus(x):
    return jnp.maximum(x, 0.0) + jnp.log(1.0 + jnp.exp(-jnp.abs(x)))


def _iota(shape, axis):
    return lax.broadcasted_iota(jnp.int32, shape, axis)


def _head_indicator(width, heads, transposed=False):
    per = width // heads
    if transposed:
        return (_iota((heads, width), 1) // per == _iota((heads, width), 0)).astype(F32)
    return (_iota((width, heads), 0) // per == _iota((width, heads), 1)).astype(F32)


def _acc(ref, i, val):
    @pl.when(i == 0)
    def _():
        ref[...] = val

    @pl.when(i > 0)
    def _():
        ref[...] += val


def _rs(tile, width, col=0, shift=0, n=None):
    if shift == 0:
        return pl.BlockSpec((tile, width), lambda i: (i, col))
    if shift < 0:
        return pl.BlockSpec((tile, width), lambda i: (jnp.maximum(i - 1, 0), col))
    return pl.BlockSpec((tile, width), lambda i: (jnp.minimum(i + 1, n - 1), col))


def _ps(shape):
    return pl.BlockSpec(shape, lambda i: (0,) * len(shape))


def _call(body, name, grid, in_specs, out_specs, out_shape, scratch=(), sem=None):
    return pl.pallas_call(
        body, name=name, grid=grid, in_specs=in_specs, out_specs=out_specs, out_shape=out_shape,
        scratch_shapes=list(scratch),
        compiler_params=pltpu.CompilerParams(dimension_semantics=sem, vmem_limit_bytes=VMEM_LIMIT))


def _row_tile(n, target):
    t = min(n, target)
    assert n % t == 0, (n, t)
    return t


def _pick_tile(dim, target):
    if dim <= target:
        return dim
    t = (int(1.4 * target) // LANE) * LANE
    while t >= LANE:
        if dim % t == 0:
            return t
        t -= LANE
    return dim


ANY = pl.BlockSpec(memory_space=pl.ANY)


def _paired(j):
    return (j % 2) * 2 + j // 2


class ColSharded:
    def __init__(self, arr, paired=False):
        self.arr, self.paired = arr, paired
        self.nch, self.rows, self.per = arr.shape
        self.shape = (self.rows, self.nch * self.per)

    def chip(self, j):
        return _paired(j) if self.paired else j


class Slot:
    def __init__(self, buf, kind, per, off, c0=0, paired=False):
        self.buf, self.kind, self.per, self.off, self.c0, self.paired = buf, kind, per, off, c0, paired

    def chip(self, j):
        return _paired(j) if self.paired else j


def matmul(a, b, mode, name, *, out_dtype=F32, scale=None, res=None, bias=None, into=None, tm=1024, tn=1024, tk=1024):
    bshape = b.shape
    if mode == "nn":
        (m, k), (k2, n) = a.shape, bshape
    elif mode == "nt":
        (m, k), (n, k2) = a.shape, bshape
    else:
        (k, m), (k2, n) = a.shape, bshape
    assert k == k2, (a.shape, bshape, mode)
    tm, tn, tk = _pick_tile(m, tm), _pick_tile(n, tn), _pick_tile(k, tk)
    if isinstance(b, ColSharded):
        if mode == "nn":
            tn = b.per
        else:
            assert mode == "nt"
            tk = b.per
    if into is not None:
        if into.kind == "c":
            tn = into.per
            assert into.off % tm == 0 and n == N_CHIPS * into.per
        else:
            tm = max(1, min(m, int(1.4 * 1024)) // into.per) * into.per
            assert m % tm == 0 and into.off % into.per == 0 and into.c0 % (tm // into.per) == 0
    nk = k // tk
    dims = {"nn": NN, "nt": NT, "tn": TN}[mode]
    a_spec = (pl.BlockSpec((tk, tm), lambda i, j, kk: (kk, i)) if mode == "tn"
              else pl.BlockSpec((tm, tk), lambda i, j, kk: (i, kk)))
    if isinstance(b, ColSharded):
        bchip = b.chip
        b_spec = (pl.BlockSpec((None, tk, tn), lambda i, j, kk: (bchip(j), kk, 0)) if mode == "nn"
                  else pl.BlockSpec((None, tn, tk), lambda i, j, kk: (bchip(kk), j, 0)))
        b = b.arr
    else:
        b_spec = (pl.BlockSpec((tn, tk), lambda i, j, kk: (j, kk)) if mode == "nt"
                  else pl.BlockSpec((tk, tn), lambda i, j, kk: (kk, j)))
    plain_o = pl.BlockSpec((tm, tn), lambda i, j, kk: (i, j))
    ins, in_specs = [a, b], [a_spec, b_spec]
    if bias is not None:
        ins.append(bias)
        in_specs.append(pl.BlockSpec((1, tn), lambda i, j, kk: (0, j)))
    if res is not None:
        ins.append(res)
        in_specs.append(plain_o)
    aliases = {}
    if into is None:
        o_spec, o_shape = plain_o, S((m, n), out_dtype)
    else:
        aliases = {len(ins): 0}
        ins.append(into.buf)
        in_specs.append(ANY)
        o_shape = S(into.buf.shape, into.buf.dtype)
        if into.kind == "c":
            ob, ochip = into.off // tm, into.chip
            o_spec = pl.BlockSpec((None, tm, tn), lambda i, j, kk: (ochip(j), ob + i, 0))
        else:
            q, ob = tm // into.per, into.off // into.per
            cb = into.c0 // q
            o_spec = pl.BlockSpec((q, into.per, tn), lambda i, j, kk: (cb + i, ob, j))

    def body(*refs):
        a_ref, b_ref = refs[0], refs[1]
        o_ref, acc_ref = refs[-2], refs[-1]
        kk = pl.program_id(2)

        @pl.when(kk == 0)
        def _():
            acc_ref[...] = jnp.zeros_like(acc_ref)

        acc_ref[...] += _mm(a_ref[...], b_ref[...], dims)

        @pl.when(kk == nk - 1)
        def _():
            out = acc_ref[...]
            if scale is not None:
                out = out * scale
            pos = 2
            if bias is not None:
                out = out + refs[pos][...]
                pos += 1
            if res is not None:
                out = out + refs[pos][...]
            o_ref[...] = out.astype(o_ref.dtype).reshape(o_ref.shape)

    return pl.pallas_call(
        body, name=name, grid=(m // tm, n // tn, nk), in_specs=in_specs, out_specs=o_spec, out_shape=o_shape,
        scratch_shapes=[pltpu.VMEM((tm, tn), F32)], input_output_aliases=aliases,
        compiler_params=pltpu.CompilerParams(dimension_semantics=("parallel", "parallel", "arbitrary"),
                                             vmem_limit_bytes=VMEM_LIMIT))(*ins)


def rms_fwd(h, g, name):
    n, d = h.shape
    tile = _row_tile(n, 512)

    def body(h_ref, g_ref, o_ref):
        x = h_ref[...]
        r = lax.rsqrt(jnp.mean(x * x, axis=-1, keepdims=True) + EPS)
        o_ref[...] = (x * r * g_ref[...]).astype(o_ref.dtype)

    return _call(body, name, (n // tile,), [_rs(tile, d), _ps((1, d))], _rs(tile, d), S((n, d), MXU_DTYPE),
                 sem=("parallel",))(h, g)


def _rms_bwd_math(x, g, dy):
    r = lax.rsqrt(jnp.mean(x * x, axis=-1, keepdims=True) + EPS)
    xh = x * r
    dg = jnp.sum(dy * xh, axis=0, keepdims=True)
    dxh = dy * g
    dx = r * (dxh - xh * jnp.mean(dxh * xh, axis=-1, keepdims=True))
    return dx, dg


def rms_bwd(h, g, dxn, dh_in, name, colsum=False):
    n, d = h.shape
    tile = _row_tile(n, 256)

    def body(h_ref, g_ref, dxn_ref, dh_ref, o_ref, dg_ref, *cs_ref):
        i = pl.program_id(0)
        dx, dg = _rms_bwd_math(h_ref[...], g_ref[...], dxn_ref[...].astype(F32))
        out = dh_ref[...] + dx
        o_ref[...] = out
        _acc(dg_ref, i, dg)
        if colsum:
            _acc(cs_ref[0], i, jnp.sum(out, axis=0, keepdims=True))

    outs = [S((n, d), F32), S((1, d), F32)] + ([S((1, d), F32)] if colsum else [])
    ospecs = [_rs(tile, d), _ps((1, d))] + ([_ps((1, d))] if colsum else [])
    return _call(body, name, (n // tile,), [_rs(tile, d), _ps((1, d)), _rs(tile, d), _rs(tile, d)], ospecs, outs,
                 sem=("arbitrary",))(h, g, dxn, dh_in)


def swiglu_in(h, g, w_in, name):
    n, d = h.shape
    per = w_in.per
    nj = w_in.nch // 2
    tile = _row_tile(n, 512)

    def body(h_ref, g_ref, wg_ref, wu_ref, xn_ref, u_ref, hm_ref):
        x = h_ref[...]
        r = lax.rsqrt(jnp.mean(x * x, axis=-1, keepdims=True) + EPS)
        xn = (x * r * g_ref[...]).astype(xn_ref.dtype)

        @pl.when(pl.program_id(1) == 0)
        def _():
            xn_ref[...] = xn

        a = _mm(xn, wg_ref[...])
        b = _mm(xn, wu_ref[...])
        u_ref[:, :per] = a
        u_ref[:, per:] = b
        hm_ref[...] = (a * _sigmoid(a) * b).astype(hm_ref.dtype)

    return pl.pallas_call(
        body, name=name, grid=(n // tile, nj),
        in_specs=[pl.BlockSpec((tile, d), lambda i, j: (i, 0)), pl.BlockSpec((1, d), lambda i, j: (0, 0)),
                  pl.BlockSpec((None, d, per), lambda i, j: (j, 0, 0)), pl.BlockSpec((None, d, per), lambda i, j: (nj + j, 0, 0))],
        out_specs=[pl.BlockSpec((tile, d), lambda i, j: (i, 0)), pl.BlockSpec((tile, 2 * per), lambda i, j: (i, j)),
                   pl.BlockSpec((tile, per), lambda i, j: (i, j))],
        out_shape=[S((n, d), MXU_DTYPE), S((n, 2 * nj * per), F32), S((n, nj * per), MXU_DTYPE)],
        compiler_params=pltpu.CompilerParams(dimension_semantics=("parallel", "arbitrary"), vmem_limit_bytes=VMEM_LIMIT),
    )(h, g, w_in.arr, w_in.arr)


def swiglu_out_bwd(dh, w_out, u, name):
    n, d = dh.shape
    f = w_out.shape[0]
    per = u.shape[1] // 4
    nj = f // per
    tile = _row_tile(n, 512)

    def body(dh_ref, w_ref, u_ref, du_ref):
        dm = 0.5 * _mm(dh_ref[...], w_ref[...], NT)
        a = u_ref[:, :per]
        b = u_ref[:, per:]
        s = _sigmoid(a)
        du_ref[:, :per] = (dm * b * s * (1.0 + a * (1.0 - s))).astype(du_ref.dtype)
        du_ref[:, per:] = (dm * a * s).astype(du_ref.dtype)

    return pl.pallas_call(
        body, name=name, grid=(n // tile, nj),
        in_specs=[pl.BlockSpec((tile, d), lambda i, j: (i, 0)), pl.BlockSpec((per, d), lambda i, j: (j, 0)),
                  pl.BlockSpec((tile, 2 * per), lambda i, j: (i, j))],
        out_specs=pl.BlockSpec((tile, 2 * per), lambda i, j: (i, j)),
        out_shape=S(u.shape, MXU_DTYPE),
        compiler_params=pltpu.CompilerParams(dimension_semantics=("parallel", "parallel"), vmem_limit_bytes=VMEM_LIMIT),
    )(dh, w_out, u)


def ple_fwd(h, gl, pp, name):
    n, d = h.shape
    tile = _row_tile(n, 512)

    def body(h_ref, gl_ref, pp_ref, o_ref):
        o_ref[...] = h_ref[...] + _sigmoid(gl_ref[...]) * pp_ref[...]

    return _call(body, name, (n // tile,), [_rs(tile, d)] * 3, _rs(tile, d), S((n, d), F32), sem=("parallel",))(h, gl, pp)


def ple_bwd(dh, gl, pp, name):
    n, d = dh.shape
    tile = _row_tile(n, 512)

    def body(dh_ref, gl_ref, pp_ref, dpp_ref, dgl_ref):
        g = _sigmoid(gl_ref[...])
        dh_ = dh_ref[...]
        dpp_ref[...] = (dh_ * g).astype(dpp_ref.dtype)
        dgl_ref[...] = (dh_ * pp_ref[...] * g * (1.0 - g)).astype(dgl_ref.dtype)

    return _call(body, name, (n // tile,), [_rs(tile, d)] * 3, [_rs(tile, d)] * 2, [S((n, d), MXU_DTYPE)] * 2,
                 sem=("parallel",))(dh, gl, pp)


def loss_head(h, g, target, name):
    n, d = h.shape
    tile = _row_tile(n, 256)

    def body(h_ref, g_ref, t_ref, dh_ref, dg_ref, loss_ref):
        i = pl.program_id(0)
        x = h_ref[...]
        gg = g_ref[...]
        r = lax.rsqrt(jnp.mean(x * x, axis=-1, keepdims=True) + EPS)
        err = x * r * gg - t_ref[...]
        part = 0.5 * jnp.sum(jnp.mean(err * err, axis=-1, keepdims=True), axis=0, keepdims=True)
        dx, dg = _rms_bwd_math(x, gg, err * (1.0 / d))
        dh_ref[...] = dx
        _acc(dg_ref, i, dg)
        _acc(loss_ref, i, jnp.broadcast_to(part, (8, LANE)))

    return _call(body, name, (n // tile,), [_rs(tile, d), _ps((1, d)), _rs(tile, d)],
                 [_rs(tile, d), _ps((1, d)), _ps((8, LANE))], [S((n, d), F32), S((1, d), F32), S((8, LANE), F32)],
                 sem=("arbitrary",))(h, g, target)


def adamw(w, g, m, v, name):
    r, c = w.shape
    tile = r
    for t in (512, 256, 128, 64, 32, 16, 8):
        if r % t == 0 and t * c * 4 <= 2**21:
            tile = t
            break
    c1 = np.float32(1.0 - ADAM_B1 ** ADAM_STEP)
    c2 = np.float32(1.0 - ADAM_B2 ** ADAM_STEP)

    def body(w_ref, g_ref, m_ref, v_ref, d_ref, mo_ref, vo_ref):
        gg = g_ref[...]
        mm = ADAM_B1 * m_ref[...] + (1.0 - ADAM_B1) * gg
        vv = ADAM_B2 * v_ref[...] + (1.0 - ADAM_B2) * (gg * gg)
        mo_ref[...] = mm
        vo_ref[...] = vv
        d_ref[...] = -ADAM_LR * ((mm / c1) / (jnp.sqrt(vv / c2) + ADAM_EPS) + ADAM_WD * w_ref[...])

    return _call(body, name, (r // tile,), [_rs(tile, c)] * 4, [_rs(tile, c)] * 3, [S((r, c), F32)] * 3,
                 sem=("parallel",))(w, g, m, v)


def _taps_fwd(sc, w_ref, width, halo, tile, acc):
    for k in range(width):
        o = halo - (width - 1) + k
        acc = acc + w_ref[k:k + 1, :] * sc[o:o + tile, :]
    return acc


def _taps_bwd_x(sc_d, w_ref, width, tile, acc):
    for k in range(width):
        o = (width - 1) - k
        acc = acc + w_ref[k:k + 1, :] * sc_d[o:o + tile, :]
    return acc


def _taps_bwd_w(dy, sc, dw_ref, width, halo, tile, i):
    @pl.when(i == 0)
    def _():
        dw_ref[...] = jnp.zeros_like(dw_ref)

    for k in range(width):
        o = halo - (width - 1) + k
        dw_ref[k:k + 1, :] += jnp.sum(dy * sc[o:o + tile, :], axis=0, keepdims=True)


def _ln_stats(x):
    mu = jnp.mean(x, axis=-1, keepdims=True)
    xc = x - mu
    r = lax.rsqrt(jnp.mean(xc * xc, axis=-1, keepdims=True) + EPS)
    return xc * r, r


def conv_group_fwd(proj, cw, cb, lg, lb, name):
    n = proj.shape[0]
    d = cw.shape[1]
    tile = _row_tile(n, 256)
    halo = 32

    def body(v_ref, g_ref, vp_ref, gp_ref, cw_ref, cb_ref, lg_ref, lb_ref, u_ref, u1_ref, sc):
        i = pl.program_id(0)
        first = (i > 0).astype(F32)
        sc[0:halo, :] = vp_ref[tile - halo:, :] * _sigmoid(gp_ref[tile - halo:, :]) * first
        sc[halo:, :] = v_ref[...] * _sigmoid(g_ref[...])
        u1 = _taps_fwd(sc, cw_ref, CONV_WIDTH, halo, tile, jnp.zeros((tile, d), F32) + cb_ref[...])
        u1_ref[...] = u1
        xh, _ = _ln_stats(u1)
        y = xh * lg_ref[...] + lb_ref[...]
        u_ref[...] = (y * _sigmoid(y)).astype(u_ref.dtype)

    return _call(body, name, (n // tile,),
                 [_rs(tile, d, 0), _rs(tile, d, 1), _rs(tile, d, 0, -1), _rs(tile, d, 1, -1),
                  _ps(cw.shape), _ps((1, d)), _ps((1, d)), _ps((1, d))],
                 [_rs(tile, d), _rs(tile, d)], [S((n, d), MXU_DTYPE), S((n, d), F32)],
                 scratch=[pltpu.VMEM((halo + tile, d), F32)], sem=("arbitrary",))(proj, proj, proj, proj, cw, cb, lg, lb)


def conv_group_bwd(du, u1, proj, cw, lg, lb, name):
    n = proj.shape[0]
    d = cw.shape[1]
    tile = _row_tile(n, 256)
    halo = 32
    nt = n // tile

    def body(du_ref, dun_ref, u1_ref, u1n_ref, v_ref, g_ref, vp_ref, gp_ref, cw_ref, lg_ref, lb_ref,
             dp_ref, dcw_ref, dcb_ref, dlg_ref, dlb_ref, sc, sc_d):
        i = pl.program_id(0)

        def ln_swish_bwd(dy_, u1_):
            xh, r = _ln_stats(u1_)
            y = xh * lg_ref[...] + lb_ref[...]
            s = _sigmoid(y)
            dyy = dy_ * s * (1.0 + y * (1.0 - s))
            dxh = dyy * lg_ref[...]
            dx = r * (dxh - jnp.mean(dxh, axis=-1, keepdims=True) - xh * jnp.mean(dxh * xh, axis=-1, keepdims=True))
            return dx, jnp.sum(dyy * xh, axis=0, keepdims=True), jnp.sum(dyy, axis=0, keepdims=True)

        du1, dlg, dlb = ln_swish_bwd(du_ref[...].astype(F32), u1_ref[...])
        du1n, _, _ = ln_swish_bwd(dun_ref[0:halo, :].astype(F32), u1n_ref[0:halo, :])
        sc_d[0:tile, :] = du1
        sc_d[tile:, :] = du1n * (i < nt - 1).astype(F32)
        sig = _sigmoid(g_ref[...])
        val = v_ref[...]
        sc[0:halo, :] = vp_ref[tile - halo:, :] * _sigmoid(gp_ref[tile - halo:, :]) * (i > 0).astype(F32)
        sc[halo:, :] = val * sig
        du0 = _taps_bwd_x(sc_d, cw_ref, CONV_WIDTH, tile, jnp.zeros((tile, d), F32))
        _taps_bwd_w(du1, sc, dcw_ref, CONV_WIDTH, halo, tile, i)
        _acc(dcb_ref, i, jnp.sum(du1, axis=0, keepdims=True))
        _acc(dlg_ref, i, dlg)
        _acc(dlb_ref, i, dlb)
        dp_ref[:, :d] = (du0 * sig).astype(dp_ref.dtype)
        dp_ref[:, d:] = (du0 * val * sig * (1.0 - sig)).astype(dp_ref.dtype)

    return _call(body, name, (nt,),
                 [_rs(tile, d), _rs(tile, d, 0, 1, nt), _rs(tile, d), _rs(tile, d, 0, 1, nt),
                  _rs(tile, d, 0), _rs(tile, d, 1), _rs(tile, d, 0, -1), _rs(tile, d, 1, -1),
                  _ps(cw.shape), _ps((1, d)), _ps((1, d))],
                 [_rs(tile, 2 * d), _ps(cw.shape), _ps((1, d)), _ps((1, d)), _ps((1, d))],
                 [S((n, proj.shape[1]), MXU_DTYPE), S(cw.shape, F32), S((1, d), F32), S((1, d), F32), S((1, d), F32)],
                 scratch=[pltpu.VMEM((halo + tile, d), F32), pltpu.VMEM((tile + halo, d), F32)],
                 sem=("arbitrary",))(du, du, u1, u1, proj, proj, proj, proj, cw, lg, lb)


def ssm_conv_fwd(proj, dtr, sw, sb, dtb, name):
    n = proj.shape[0]
    w = sw.shape[1]
    inner = SSM_HEADS * HEAD_DIM
    tile = _row_tile(n, 256)
    halo = 8

    def body(x_ref, xp_ref, dtr_ref, sw_ref, sb_ref, dtb_ref, pre_ref, xs_ref, bc_ref, dt_ref, sc):
        i = pl.program_id(0)
        sc[0:halo, :] = xp_ref[tile - halo:, :] * (i > 0).astype(F32)
        sc[halo:, :] = x_ref[...]
        pre = _taps_fwd(sc, sw_ref, SSM_CONV, halo, tile, jnp.zeros((tile, w), F32) + sb_ref[...])
        pre_ref[...] = pre
        act = pre * _sigmoid(pre)
        xs_ref[...] = act[:, :inner]
        bc_ref[...] = act[:, inner:]
        dt = _softplus(dtr_ref[...] + dtb_ref[...])
        dt_ref[...] = jnp.where(_iota(dt.shape, 1) < SSM_HEADS, dt, 0.0)

    return _call(body, name, (n // tile,),
                 [_rs(tile, w, 2), _rs(tile, w, 2, -1), _rs(tile, LANE), _ps(sw.shape), _ps((1, w)), _ps((1, LANE))],
                 [_rs(tile, w), _rs(tile, inner), _rs(tile, w - inner), _rs(tile, LANE)],
                 [S((n, w), F32), S((n, inner), F32), S((n, w - inner), F32), S((n, LANE), F32)],
                 scratch=[pltpu.VMEM((halo + tile, w), F32)], sem=("arbitrary",))(proj, proj, dtr, sw, sb, dtb)


def ssm_conv_bwd(dxs, dbc, pre, proj, sw, dproj, name):
    n = proj.shape[0]
    w = sw.shape[1]
    inner = SSM_HEADS * HEAD_DIM
    tile = _row_tile(n, 256)
    halo = 8
    nt = n // tile

    def body(dxs_ref, dxsn_ref, dbc_ref, dbcn_ref, pre_ref, pren_ref, x_ref, xp_ref, sw_ref, dp_in_ref,
             dx_ref, dsw_ref, dsb_ref, sc, sc_d):
        i = pl.program_id(0)

        def silu_bwd(d_, p_):
            s = _sigmoid(p_)
            return d_ * s * (1.0 + p_ * (1.0 - s))

        sc_d[0:tile, :inner] = silu_bwd(dxs_ref[...], pre_ref[:, :inner])
        sc_d[0:tile, inner:] = silu_bwd(dbc_ref[...], pre_ref[:, inner:])
        last = (i < nt - 1).astype(F32)
        sc_d[tile:, :inner] = silu_bwd(dxsn_ref[0:halo, :], pren_ref[0:halo, :inner]) * last
        sc_d[tile:, inner:] = silu_bwd(dbcn_ref[0:halo, :], pren_ref[0:halo, inner:]) * last
        sc[0:halo, :] = xp_ref[tile - halo:, :] * (i > 0).astype(F32)
        sc[halo:, :] = x_ref[...]
        dpre = sc_d[0:tile, :]
        dx_ref[...] = _taps_bwd_x(sc_d, sw_ref, SSM_CONV, tile, jnp.zeros((tile, w), F32)).astype(dx_ref.dtype)
        _taps_bwd_w(dpre, sc, dsw_ref, SSM_CONV, halo, tile, i)
        _acc(dsb_ref, i, jnp.sum(dpre, axis=0, keepdims=True))

    return pl.pallas_call(
        body, name=name, grid=(nt,),
        in_specs=[_rs(tile, inner), _rs(tile, inner, 0, 1, nt), _rs(tile, w - inner), _rs(tile, w - inner, 0, 1, nt),
                  _rs(tile, w), _rs(tile, w, 0, 1, nt), _rs(tile, w, 2), _rs(tile, w, 2, -1), _ps(sw.shape), ANY],
        out_specs=[_rs(tile, w, 2), _ps(sw.shape), _ps((1, w))],
        out_shape=[S(dproj.shape, dproj.dtype), S(sw.shape, F32), S((1, w), F32)],
        scratch_shapes=[pltpu.VMEM((halo + tile, w), F32), pltpu.VMEM((tile + halo, w), F32)],
        input_output_aliases={9: 0},
        compiler_params=pltpu.CompilerParams(dimension_semantics=("arbitrary",), vmem_limit_bytes=VMEM_LIMIT),
    )(dxs, dxs, dbc, dbc, pre, pre, proj, proj, sw, dproj)


def _ssd_prologue(dt_ref, dtT_ref, al_ref, alc_ref):
    row = _iota((CHUNK, CHUNK), 0)
    col = _iota((CHUNK, CHUNK), 1)
    dt = dt_ref[:, :SSM_HEADS]
    a_row = -jnp.exp(al_ref[:, :SSM_HEADS])
    a_col = -jnp.exp(alc_ref[...])
    cs = _01mm((row >= col).astype(F32), dt * a_row)
    csT = _mm01(dtT_ref[...] * a_col, (row <= col).astype(F32))
    return dt, a_row, cs, csT, row, col


def _decay(cs, csT, h, row, col):
    lm = jnp.exp(jnp.where(row >= col, cs[:, h:h + 1] - csT[h:h + 1, :], -1e30))
    lmT = jnp.exp(jnp.where(col >= row, csT[h:h + 1, :] - cs[:, h:h + 1], -1e30))
    return lm, lmT


def ssd_fwd(xs, bc, dt, dtT, alog_row, alog_col, name):
    n, width = xs.shape
    nc = n // CHUNK
    gw = width // SSM_GROUPS
    hpg = SSM_HEADS // SSM_GROUPS
    ns = SSM_STATE

    def body(xs_ref, bc_ref, dt_ref, dtT_ref, al_ref, alc_ref, y_ref, hs_ref, h_sc):
        i = pl.program_id(0)

        @pl.when(i == 0)
        def _():
            h_sc[...] = jnp.zeros_like(h_sc)

        dt, a_row, cs, csT, row, col = _ssd_prologue(dt_ref, dtT_ref, al_ref, alc_ref)
        indT = _head_indicator(width, SSM_HEADS, transposed=True)
        dt_full = _mm01(dt, indT)
        e_full = jnp.exp(_mm01(cs, indT))
        dte_full = jnp.exp(_mm01(cs[CHUNK - 1:CHUNK, :] - cs, indT))
        xt = xs_ref[...] * dt_full
        hs_ref[0] = h_sc[...]
        lo = _iota((CHUNK, 2 * HEAD_DIM), 1) < HEAD_DIM
        for g in range(SSM_GROUPS):
            bg = bc_ref[:, g * ns:(g + 1) * ns]
            cg = bc_ref[:, (SSM_GROUPS + g) * ns:(SSM_GROUPS + g + 1) * ns]
            gm = _mm(cg, bg, NT)
            hg = h_sc[g * gw:(g + 1) * gw, :]
            yoff = e_full[:, g * gw:(g + 1) * gw] * _mm(cg, hg, NT)
            for pr in range(hpg // 2):
                h0 = g * hpg + 2 * pr
                c0 = h0 * HEAD_DIM
                xp = xt[:, c0:c0 + 2 * HEAD_DIM]
                m0 = gm * _decay(cs, csT, h0, row, col)[0]
                m1 = gm * _decay(cs, csT, h0 + 1, row, col)[0]
                yd = jnp.where(lo, _mm(m0, xp), _mm(m1, xp))
                y_ref[:, c0:c0 + 2 * HEAD_DIM] = yd + yoff[:, 2 * pr * HEAD_DIM:(2 * pr + 2) * HEAD_DIM]
            sg = _mm(xt[:, g * gw:(g + 1) * gw] * dte_full[:, g * gw:(g + 1) * gw], bg, TN)
            for hh in range(hpg):
                h = g * hpg + hh
                r0 = h * HEAD_DIM
                h_sc[r0:r0 + HEAD_DIM, :] = (h_sc[r0:r0 + HEAD_DIM, :] * jnp.exp(csT[h:h + 1, CHUNK - 1:CHUNK])
                                             + sg[hh * HEAD_DIM:(hh + 1) * HEAD_DIM, :])

    bcw = bc.shape[1]
    return _call(body, name, (nc,),
                 [_rs(CHUNK, width), _rs(CHUNK, bcw), _rs(CHUNK, LANE), pl.BlockSpec((SSM_HEADS, CHUNK), lambda i: (0, i)),
                  _ps((1, LANE)), _ps((SSM_HEADS, 1))],
                 [_rs(CHUNK, width), pl.BlockSpec((1, width, ns), lambda i: (i, 0, 0))],
                 [S((n, width), F32), S((nc, width, ns), F32)],
                 scratch=[pltpu.VMEM((width, ns), F32)], sem=("arbitrary",))(xs, bc, dt, dtT, alog_row, alog_col)


def ssd_bwd(xs, bc, dt, dtT, alog_row, alog_col, hs, dy, dxs_skip, name):
    n, width = xs.shape
    nc = n // CHUNK
    gw = width // SSM_GROUPS
    hpg = SSM_HEADS // SSM_GROUPS
    ns = SSM_STATE
    bcw = bc.shape[1]

    def body(xs_ref, bc_ref, dt_ref, dtT_ref, al_ref, alc_ref, hs_ref, dy_ref, skip_ref,
             dxs_ref, dbc_ref, ddtr_ref, dal_ref, ddtb_ref, dh_sc, dxt_sc):
        i = pl.program_id(0)

        @pl.when(i == 0)
        def _():
            dh_sc[...] = jnp.zeros_like(dh_sc)

        dt, a_row, cs, csT, row, col = _ssd_prologue(dt_ref, dtT_ref, al_ref, alc_ref)
        indT = _head_indicator(width, SSM_HEADS, transposed=True)
        ind = _head_indicator(width, SSM_HEADS)
        dt_full = _mm01(dt, indT)
        e_full = jnp.exp(_mm01(cs, indT))
        cs_last = cs[CHUNK - 1:CHUNK, :]
        dte = jnp.exp(cs_last - cs)
        dte_full = _mm01(dte, indT)
        xs_ = xs_ref[...]
        xt = xs_ * dt_full
        dy_ = dy_ref[...]
        hmat = hs_ref[0]
        ds = dh_sc[...]
        lo = _iota((CHUNK, 2 * HEAD_DIM), 1) < HEAD_DIM
        head_lane = _iota((1, SSM_HEADS), 1)
        dcs = jnp.zeros((CHUNK, SSM_HEADS), F32)
        ddte = jnp.zeros((CHUNK, SSM_HEADS), F32)
        for g in range(SSM_GROUPS):
            sl = slice(g * gw, (g + 1) * gw)
            bg = bc_ref[:, g * ns:(g + 1) * ns]
            cg = bc_ref[:, (SSM_GROUPS + g) * ns:(SSM_GROUPS + g + 1) * ns]
            gm = _mm(cg, bg, NT)
            gmT = _mm(bg, cg, NT)
            hg = hmat[sl, :]
            dsg = ds[sl, :]
            dyg = dy_[:, sl]
            xtg = xt[:, sl]
            yoff = e_full[:, sl] * _mm(cg, hg, NT)
            edy = e_full[:, sl] * dyg
            bds = _mm(bg, dsg, NT)
            dxt_g = dte_full[:, sl] * bds
            ddte = ddte + _mm01(xtg * bds, ind[sl, :])
            dcs = dcs + _mm01(dyg * yoff, ind[sl, :])
            db = _mm(xtg * dte_full[:, sl], dsg)
            dc = _mm(edy, hg)
            dhc = _mm(edy, cg, TN)
            dgs = jnp.zeros((CHUNK, CHUNK), F32)
            dgTs = jnp.zeros((CHUNK, CHUNK), F32)
            for pr in range(hpg // 2):
                h0 = g * hpg + 2 * pr
                c0 = 2 * pr * HEAD_DIM
                xp = xtg[:, c0:c0 + 2 * HEAD_DIM]
                dyp = dyg[:, c0:c0 + 2 * HEAD_DIM]
                rr = []
                for h, half in ((h0, lo), (h0 + 1, jnp.logical_not(lo))):
                    lm, lmT = _decay(cs, csT, h, row, col)
                    xm = jnp.where(half, xp, 0.0)
                    dm = _mm(dyp, xm, NT)
                    dmT = _mm(xm, dyp, NT)
                    mT = gmT * lmT
                    z = jnp.sum(dm * (gm * lm), axis=1, keepdims=True) - jnp.sum(dmT * mT, axis=1, keepdims=True)
                    dcs = dcs + z * (head_lane == h).astype(F32)
                    dgs = dgs + dm * lm
                    dgTs = dgTs + dmT * lmT
                    rr.append(_mm(mT, dyp))
                dxt_sc[:, g * gw + c0:g * gw + c0 + 2 * HEAD_DIM] = jnp.where(lo, rr[0], rr[1]) + dxt_g[:, c0:c0 + 2 * HEAD_DIM]
            dbc_ref[:, g * ns:(g + 1) * ns] = db + _mm(dgTs, cg)
            dbc_ref[:, (SSM_GROUPS + g) * ns:(SSM_GROUPS + g + 1) * ns] = dc + _mm(dgs, bg)
            for hh in range(hpg):
                h = g * hpg + hh
                r0 = h * HEAD_DIM
                dh_sc[r0:r0 + HEAD_DIM, :] = (dhc[hh * HEAD_DIM:(hh + 1) * HEAD_DIM, :]
                                              + jnp.exp(csT[h:h + 1, CHUNK - 1:CHUNK]) * ds[r0:r0 + HEAD_DIM, :])
        t = ddte * dte
        per_head = jnp.sum(jnp.sum(ds * hmat, axis=1, keepdims=True) * ind, axis=0, keepdims=True)
        last_add = jnp.sum(t, axis=0, keepdims=True) + jnp.exp(cs_last) * per_head
        dcs = dcs - t + jnp.where(_iota((CHUNK, SSM_HEADS), 0) == CHUNK - 1, last_add, 0.0)
        dadt = _01mm((row <= col).astype(F32), dcs)
        dxt = dxt_sc[...]
        ddt = dadt * a_row + _mm01(dxt * xs_, ind)
        dxs_ref[...] = dxt * dt_full + skip_ref[...]
        ddtr = ddt * (1.0 - jnp.exp(-dt))
        ddtr_ref[...] = jnp.zeros_like(ddtr_ref)
        ddtr_ref[:, :SSM_HEADS] = ddtr.astype(ddtr_ref.dtype)
        _acc(dal_ref, i, jnp.sum(dadt * dt, axis=0, keepdims=True) * a_row)
        _acc(ddtb_ref, i, jnp.sum(ddtr, axis=0, keepdims=True))

    rev = lambda i: (nc - 1 - i, 0)
    return _call(body, name, (nc,),
                 [pl.BlockSpec((CHUNK, width), rev), pl.BlockSpec((CHUNK, bcw), rev), pl.BlockSpec((CHUNK, LANE), rev),
                  pl.BlockSpec((SSM_HEADS, CHUNK), lambda i: (0, nc - 1 - i)), _ps((1, LANE)), _ps((SSM_HEADS, 1)),
                  pl.BlockSpec((1, width, ns), lambda i: (nc - 1 - i, 0, 0)), pl.BlockSpec((CHUNK, width), rev),
                  pl.BlockSpec((CHUNK, width), rev)],
                 [pl.BlockSpec((CHUNK, width), rev), pl.BlockSpec((CHUNK, bcw), rev), pl.BlockSpec((CHUNK, LANE), rev),
                  _ps((1, SSM_HEADS)), _ps((1, SSM_HEADS))],
                 [S((n, width), F32), S((n, bcw), F32), S((n, LANE), MXU_DTYPE), S((1, SSM_HEADS), F32), S((1, SSM_HEADS), F32)],
                 scratch=[pltpu.VMEM((width, ns), F32), pltpu.VMEM((CHUNK, width), F32)],
                 sem=("arbitrary",))(xs, bc, dt, dtT, alog_row, alog_col, hs, dy, dxs_skip)


def ssm_gate_fwd(yssd, xs, proj, dfull, gamma, name):
    n, d = yssd.shape
    tile = _row_tile(n, 256)
    gw = d // SSM_GROUPS

    def body(y_ref, xs_ref, z_ref, df_ref, gm_ref, o_ref):
        z = z_ref[...]
        y2 = (y_ref[...] + df_ref[...] * xs_ref[...]) * (z * _sigmoid(z))
        for g in range(SSM_GROUPS):
            yg = y2[:, g * gw:(g + 1) * gw]
            r = lax.rsqrt(jnp.mean(yg * yg, axis=-1, keepdims=True) + EPS)
            o_ref[:, g * gw:(g + 1) * gw] = (yg * r * gm_ref[:, g * gw:(g + 1) * gw]).astype(o_ref.dtype)

    return _call(body, name, (n // tile,), [_rs(tile, d), _rs(tile, d), _rs(tile, d, 2), _ps((1, d)), _ps((1, d))],
                 _rs(tile, d), S((n, d), MXU_DTYPE), sem=("parallel",))(yssd, xs, proj, dfull, gamma)


def ssm_gate_bwd(dy3, yssd, xs, proj, dfull, gamma, dproj, name):
    n, d = yssd.shape
    tile = _row_tile(n, 256)
    gw = d // SSM_GROUPS

    def body(dy_ref, y_ref, xs_ref, z_ref, df_ref, gm_ref, dp_in_ref, dys_ref, dxs_ref, dz_ref, dgm_ref, dd_ref):
        i = pl.program_id(0)
        z = z_ref[...]
        s = _sigmoid(z)
        xs_ = xs_ref[...]
        y1 = y_ref[...] + df_ref[...] * xs_
        y2 = y1 * (z * s)
        dy_ = dy_ref[...].astype(F32)
        dgm = []
        dy2 = []
        for g in range(SSM_GROUPS):
            sl = slice(g * gw, (g + 1) * gw)
            dxg, dgg = _rms_bwd_math(y2[:, sl], gm_ref[:, sl], dy_[:, sl])
            dy2.append(dxg)
            dgm.append(dgg)
        dy2 = jnp.concatenate(dy2, axis=1)
        dy1 = dy2 * (z * s)
        dys_ref[...] = dy1
        dxs_ref[...] = dy1 * df_ref[...]
        dz_ref[...] = (dy2 * y1 * s * (1.0 + z * (1.0 - s))).astype(dz_ref.dtype)
        _acc(dgm_ref, i, jnp.concatenate(dgm, axis=1))
        colsum = jnp.broadcast_to(jnp.sum(dy1 * xs_, axis=0, keepdims=True), (8, d))
        _acc(dd_ref, i, _mm01(colsum, _head_indicator(d, SSM_HEADS))[0:1, :])

    return pl.pallas_call(
        body, name=name, grid=(n // tile,),
        in_specs=[_rs(tile, d), _rs(tile, d), _rs(tile, d), _rs(tile, d, 2), _ps((1, d)), _ps((1, d)), ANY],
        out_specs=[_rs(tile, d), _rs(tile, d), _rs(tile, d, 2), _ps((1, d)), _ps((1, SSM_HEADS))],
        out_shape=[S((n, d), F32), S((n, d), F32), S(dproj.shape, dproj.dtype), S((1, d), F32), S((1, SSM_HEADS), F32)],
        input_output_aliases={6: 2},
        compiler_params=pltpu.CompilerParams(dimension_semantics=("arbitrary",), vmem_limit_bytes=VMEM_LIMIT),
    )(dy3, yssd, xs, proj, dfull, gamma, dproj)


def _rope128(x, cos, sin_signed):
    half = HEAD_DIM // 2
    lane = _iota(x.shape, 1)
    partner = jnp.where((lane % HEAD_DIM) < half, pltpu.roll(x, LANE - half, 1), pltpu.roll(x, half, 1))
    return x * cos + partner * sin_signed


def rope_fwd(qkv, cos, sin, name):
    n, w = qkv.shape
    qw = ATT_HEADS * HEAD_DIM
    kw = ATT_KV_HEADS * HEAD_DIM
    tile = _row_tile(n, 256)

    def body(x_ref, c_ref, s_ref, q_ref, k_ref, v_ref):
        c, s = c_ref[...], s_ref[...]
        for j in range(qw // LANE):
            q_ref[:, j * LANE:(j + 1) * LANE] = _rope128(x_ref[:, j * LANE:(j + 1) * LANE], c, s).astype(q_ref.dtype)
        for j in range(kw // LANE):
            k_ref[:, j * LANE:(j + 1) * LANE] = _rope128(x_ref[:, qw + j * LANE:qw + (j + 1) * LANE], c, s).astype(k_ref.dtype)
        v_ref[...] = x_ref[:, qw + kw:].astype(v_ref.dtype)

    return _call(body, name, (n // tile,), [_rs(tile, w), _rs(tile, LANE), _rs(tile, LANE)],
                 [_rs(tile, qw), _rs(tile, kw), _rs(tile, kw)],
                 [S((n, qw), MXU_DTYPE), S((n, kw), MXU_DTYPE), S((n, kw), MXU_DTYPE)], sem=("parallel",))(qkv, cos, sin)


ATT_GROUP = ATT_HEADS // ATT_KV_HEADS


def _attn_mask(i):
    row = _iota((ATT_GROUP * WINDOW, 2 * WINDOW), 0) % WINDOW
    s = _iota((ATT_GROUP * WINDOW, 2 * WINDOW), 1)
    return (s > row) & (s <= row + WINDOW) & ((s >= WINDOW) | (i > 0))


def _stack_heads(ref, j, kh, lo):
    parts = []
    for t in range(ATT_GROUP):
        h = ATT_GROUP * j + t
        blk = ref[:, (h // 2) * LANE:(h // 2 + 1) * LANE]
        blk = jnp.where(lo if h % 2 == 0 else jnp.logical_not(lo), blk, jnp.zeros_like(blk))
        parts.append(blk if h % 2 == kh else pltpu.roll(blk, HEAD_DIM, 1))
    return jnp.concatenate(parts, axis=0)


def _unstack_heads(stacked, j, kh, lo, put):
    for t in range(0, ATT_GROUP, 2):
        h = ATT_GROUP * j + t
        even = stacked[t * WINDOW:(t + 1) * WINDOW, :]
        odd = stacked[(t + 1) * WINDOW:(t + 2) * WINDOW, :]
        even = even if kh == 0 else pltpu.roll(even, HEAD_DIM, 1)
        odd = odd if kh == 1 else pltpu.roll(odd, HEAD_DIM, 1)
        put(h // 2, jnp.where(lo, even, odd))


def _per_head_rows(ref, j):
    return jnp.concatenate([ref[:, ATT_GROUP * j + t:ATT_GROUP * j + t + 1] for t in range(ATT_GROUP)], axis=0)


def _per_head_scalar(ref, j):
    rows = _iota((ATT_GROUP * WINDOW, 1), 0) // WINDOW
    out = jnp.zeros((ATT_GROUP * WINDOW, 1), F32)
    for t in range(ATT_GROUP):
        out = out + jnp.where(rows == t, ref[:, ATT_GROUP * j + t:ATT_GROUP * j + t + 1], 0.0)
    return out


def attn_fwd(q, k, v, sinks, name):
    n, qw = q.shape
    kw = k.shape[1]
    nb = n // WINDOW
    scale = HEAD_DIM ** -0.5

    def body(q_ref, kc_ref, kp_ref, vc_ref, vp_ref, sk_ref, o_ref, lse_ref):
        i = pl.program_id(0)
        valid = _attn_mask(i)
        lo = _iota((WINDOW, LANE), 1) < HEAD_DIM
        k2 = jnp.concatenate([kp_ref[...], kc_ref[...]], axis=0)
        v2 = jnp.concatenate([vp_ref[...], vc_ref[...]], axis=0)
        lane1 = _iota((1, LANE), 1)
        lse = jnp.zeros((WINDOW, LANE), F32)

        def put_o(qb, val):
            o_ref[:, qb * LANE:(qb + 1) * LANE] = val.astype(o_ref.dtype)

        for j in range(ATT_KV_HEADS):
            kb, kh = j // 2, j % 2
            q4 = _stack_heads(q_ref, j, kh, lo)
            logits = jnp.where(valid, _mm(q4, k2[:, kb * LANE:(kb + 1) * LANE], NT) * scale, -1e30)
            sk = _per_head_scalar(sk_ref, j)
            m = jnp.maximum(jnp.max(logits, axis=-1, keepdims=True), sk)
            e = jnp.exp(logits - m)
            den = jnp.sum(e, axis=-1, keepdims=True) + jnp.exp(sk - m)
            lse4 = m + jnp.log(den)
            for t in range(ATT_GROUP):
                lse = lse + lse4[t * WINDOW:(t + 1) * WINDOW, :] * (lane1 == ATT_GROUP * j + t).astype(F32)
            _unstack_heads(_mm(e * (1.0 / den), v2[:, kb * LANE:(kb + 1) * LANE]), j, kh, lo, put_o)
        lse_ref[...] = lse

    return _call(body, name, (nb,),
                 [_rs(WINDOW, qw), _rs(WINDOW, kw), _rs(WINDOW, kw, 0, -1), _rs(WINDOW, kw), _rs(WINDOW, kw, 0, -1), _ps((1, LANE))],
                 [_rs(WINDOW, qw), _rs(WINDOW, LANE)], [S((n, qw), MXU_DTYPE), S((n, LANE), F32)],
                 sem=("parallel",))(q, k, k, v, v, sinks)


def attn_bwd(q, k, v, o, do, lse, sinks, name):
    n, qw = q.shape
    kw = k.shape[1]
    nb = n // WINDOW
    scale = HEAD_DIM ** -0.5

    def body(q_ref, kc_ref, kp_ref, vc_ref, vp_ref, o_ref, do_ref, lse_ref, sk_ref,
             dq_ref, dka_ref, dkb_ref, dva_ref, dvb_ref, dsk_ref):
        i = pl.program_id(0)
        valid = _attn_mask(i)
        lo = _iota((WINDOW, LANE), 1) < HEAD_DIM
        k2 = jnp.concatenate([kp_ref[...], kc_ref[...]], axis=0)
        v2 = jnp.concatenate([vp_ref[...], vc_ref[...]], axis=0)
        lane1 = _iota((1, LANE), 1)
        do_ = do_ref[...].astype(F32)
        delta = _mm01(do_ * o_ref[...].astype(F32), _head_indicator(qw, ATT_HEADS))
        dk2 = [jnp.zeros((2 * WINDOW, LANE), F32) for _ in range(kw // LANE)]
        dv2 = [jnp.zeros((2 * WINDOW, LANE), F32) for _ in range(kw // LANE)]
        dsk = jnp.zeros((1, LANE), F32)

        def put_dq(qb, val):
            dq_ref[:, qb * LANE:(qb + 1) * LANE] = val

        for j in range(ATT_KV_HEADS):
            kb, kh = j // 2, j % 2
            q4 = _stack_heads(q_ref, j, kh, lo)
            do4 = _stack_heads(do_ref, j, kh, lo)
            kk = k2[:, kb * LANE:(kb + 1) * LANE]
            vv = v2[:, kb * LANE:(kb + 1) * LANE]
            logits = jnp.where(valid, _mm(q4, kk, NT) * scale, -1e30)
            lse4 = _per_head_rows(lse_ref, j)
            p = jnp.exp(logits - lse4)
            dl = jnp.concatenate([delta[:, ATT_GROUP * j + t:ATT_GROUP * j + t + 1] for t in range(ATT_GROUP)], axis=0)
            ds = p * (_mm(do4, vv, NT) - dl) * scale
            sd = jnp.exp(_per_head_scalar(sk_ref, j) - lse4) * dl
            for t in range(ATT_GROUP):
                dsk = dsk - (jnp.sum(sd[t * WINDOW:(t + 1) * WINDOW, :], axis=0, keepdims=True)
                             * (lane1 == ATT_GROUP * j + t).astype(F32))
            _unstack_heads(_mm(ds, kk), j, kh, lo, put_dq)
            dk2[kb] = dk2[kb] + _mm(ds, q4, TN)
            dv2[kb] = dv2[kb] + _mm(p, do4, TN)
        for kb in range(kw // LANE):
            dkb_ref[:, kb * LANE:(kb + 1) * LANE] = dk2[kb][0:WINDOW, :]
            dka_ref[:, kb * LANE:(kb + 1) * LANE] = dk2[kb][WINDOW:, :]
            dvb_ref[:, kb * LANE:(kb + 1) * LANE] = dv2[kb][0:WINDOW, :]
            dva_ref[:, kb * LANE:(kb + 1) * LANE] = dv2[kb][WINDOW:, :]
        _acc(dsk_ref, i, dsk)

    return _call(body, name, (nb,),
                 [_rs(WINDOW, qw), _rs(WINDOW, kw), _rs(WINDOW, kw, 0, -1), _rs(WINDOW, kw), _rs(WINDOW, kw, 0, -1),
                  _rs(WINDOW, qw), _rs(WINDOW, qw), _rs(WINDOW, LANE), _ps((1, LANE))],
                 [_rs(WINDOW, qw)] + [_rs(WINDOW, kw)] * 4 + [_ps((1, LANE))],
                 [S((n, qw), F32)] + [S((n, kw), F32)] * 4 + [S((1, LANE), F32)],
                 sem=("arbitrary",))(q, k, k, v, v, o, do, lse, sinks)


def attn_grad_merge(dq, dka, dkb, dva, dvb, cos, sin, name):
    n, qw = dq.shape
    kw = dka.shape[1]
    nb = n // WINDOW
    w = qw + 2 * kw

    def body(dq_ref, dka_ref, dkb_ref, dva_ref, dvb_ref, c_ref, s_ref, o_ref, db_ref):
        i = pl.program_id(0)
        c, s = c_ref[...], -s_ref[...]
        nxt = (i < nb - 1).astype(F32)

        @pl.when(i == 0)
        def _():
            db_ref[...] = jnp.zeros_like(db_ref)

        def put(c0, val):
            o_ref[:, c0:c0 + val.shape[1]] = val.astype(o_ref.dtype)
            db_ref[:, c0:c0 + val.shape[1]] += jnp.sum(val, axis=0, keepdims=True)

        for j in range(qw // LANE):
            put(j * LANE, _rope128(dq_ref[:, j * LANE:(j + 1) * LANE], c, s))
        for j in range(kw // LANE):
            sl = slice(j * LANE, (j + 1) * LANE)
            put(qw + j * LANE, _rope128(dka_ref[:, sl] + dkb_ref[:, sl] * nxt, c, s))
        put(qw + kw, dva_ref[...] + dvb_ref[...] * nxt)

    return _call(body, name, (nb,),
                 [_rs(WINDOW, qw), _rs(WINDOW, kw), _rs(WINDOW, kw, 0, 1, nb), _rs(WINDOW, kw), _rs(WINDOW, kw, 0, 1, nb),
                  _rs(WINDOW, LANE), _rs(WINDOW, LANE)],
                 [_rs(WINDOW, w), _ps((1, w))], [S((n, w), MXU_DTYPE), S((1, w), F32)],
                 sem=("arbitrary",))(dq, dka, dkb, dva, dvb, cos, sin)


def _row(v):
    return v.reshape(1, -1)


def _pad_lanes(v, width=LANE):
    return jnp.pad(v.reshape(1, -1), ((0, 0), (0, width - v.size)))


def ffn_fwd(h, g, w_in, w_out, tag):
    xn, u, hm = swiglu_in(h, _row(g), w_in, f"{tag}_in")
    return matmul(hm, w_out, "nn", f"{tag}_out", scale=0.5, res=h), (h, xn, u, hm)


class GradSink:
    ORDER = ("ffn1_w_out", "ffn2_w_out", "ple_gate_w", "att_w_o", "hyb_w_out", "ffn1_w_in", "ffn2_w_in", "att_w_qkv",
             "ple_proj_w", "hyb_w_in")

    def __init__(self, shard_shapes):
        self.where, rows = {}, {}
        for k in self.ORDER:
            depth, r, c = shard_shapes[k]
            offs = [rows.get(c, 0) + i * r for i in range(depth)]
            rows[c] = offs[-1] + r
            self.where[k] = (c, "r" if SHARD_AXIS[k] == 1 else "c", offs, r)
        self.bufs = {c: lax.empty((N_CHIPS, r, c), F32) for c, r in rows.items()}

    def mm(self, k, layer, a, b, name, scale=None, c0=0, paired=False):
        c, kind, offs, r = self.where[k]
        slot = Slot(self.bufs[c], kind, r if kind == "r" else c, offs[layer], c0, paired)
        self.bufs[c] = matmul(a, b, "tn", name, scale=scale, into=slot)

    def put(self, k, chip_major):
        self.bufs[self.where[k][0]] = chip_major

    def take(self, k, reduced):
        c, _, offs, r = self.where[k]
        return reduced[c][offs[0]:offs[0] + len(offs) * r].reshape(len(offs), r, c)


def ffn_bwd(dh, g, w_in, w_out, saved, tag, sink, keys, layer, colsum=False):
    h, xn, u, hm = saved
    sink.mm(keys[1], layer, hm, dh, f"{tag}_dwout", scale=0.5)
    du = swiglu_out_bwd(dh, w_out, u, f"{tag}_dhm")
    sink.mm(keys[0], layer, xn, du, f"{tag}_dwin", paired=True)
    dxn = matmul(du, ColSharded(w_in.arr, paired=True), "nt", f"{tag}_dxn")
    outs = rms_bwd(h, _row(g), dxn, dh, f"{tag}_drms", colsum=colsum)
    return (outs[0], outs[1].reshape(-1)) + ((outs[2],) if colsum else ())


def _hyb_params(w):
    d = w["conv_dw_b"].size
    inner = SSM_HEADS * HEAD_DIM
    main = 3 * d + w["ssm_conv_b"].size
    return dict(
        w_main=w["hyb_w_in"][:, :main], w_dt=jnp.pad(w["hyb_w_in"][:, main:], ((0, 0), (0, LANE - SSM_HEADS))),
        cw=jnp.pad(w["conv_dw_w"], ((0, 32 - CONV_WIDTH), (0, 0))), cb=_row(w["conv_dw_b"]),
        lg=_row(w["conv_ln_g"]), lb=_row(w["conv_ln_b"]),
        sw=jnp.pad(w["ssm_conv_w"], ((0, 8 - SSM_CONV), (0, 0))), sb=_row(w["ssm_conv_b"]),
        dtb=_pad_lanes(w["ssm_dt_bias"]), al_row=_pad_lanes(w["ssm_a_log"]), al_col=w["ssm_a_log"].reshape(-1, 1),
        dfull=_row(jnp.repeat(w["ssm_d"], HEAD_DIM)), gamma=_row(w["ssm_norm"]),
        wo_top=w["hyb_w_out"][:d], wo_bot=w["hyb_w_out"][d:], d=d, inner=inner, main=main)


def hyb_fwd(h, w, tag):
    q = _hyb_params(w)
    xn = rms_fwd(h, _row(w["norm_mix"]), f"{tag}_rms")
    proj = matmul(xn, q["w_main"], "nn", f"{tag}_in")
    dtr = matmul(xn, q["w_dt"], "nn", f"{tag}_in_dt")
    u, u1 = conv_group_fwd(proj, q["cw"], q["cb"], q["lg"], q["lb"], f"{tag}_conv")
    pre, xs, bc, dt = ssm_conv_fwd(proj, dtr, q["sw"], q["sb"], q["dtb"], f"{tag}_sconv")
    dtT = dt[:, :SSM_HEADS].T
    yssd, hs = ssd_fwd(xs, bc, dt, dtT, q["al_row"], q["al_col"], f"{tag}_ssd")
    y = ssm_gate_fwd(yssd, xs, proj, q["dfull"], q["gamma"], f"{tag}_gate")
    h2 = matmul(u, q["wo_top"], "nn", f"{tag}_out_a", res=h)
    h2 = matmul(y, q["wo_bot"], "nn", f"{tag}_out_b", res=h2)
    return h2, (h, xn, proj, u, u1, pre, xs, bc, dt, dtT, yssd, hs, y)


def hyb_bwd(dh, w, saved, tag, sink, layer):
    q = _hyb_params(w)
    h, xn, proj, u, u1, pre, xs, bc, dt, dtT, yssd, hs, y = saved
    du = matmul(dh, q["wo_top"], "nt", f"{tag}_du")
    dy3 = matmul(dh, q["wo_bot"], "nt", f"{tag}_dy")
    sink.mm("hyb_w_out", layer, u, dh, f"{tag}_dwo_a", c0=0)
    sink.mm("hyb_w_out", layer, y, dh, f"{tag}_dwo_b", c0=N_CHIPS // 2)
    dproj, dcw, dcb, dlg, dlb = conv_group_bwd(du, u1, proj, q["cw"], q["lg"], q["lb"], f"{tag}_dconv")
    dyssd, dxs_skip, dproj, dgamma, dd = ssm_gate_bwd(dy3, yssd, xs, proj, q["dfull"], q["gamma"], dproj, f"{tag}_dgate")
    dxs, dbc, ddtr, dalog, ddtb = ssd_bwd(xs, bc, dt, dtT, q["al_row"], q["al_col"], hs, dyssd, dxs_skip, f"{tag}_dssd")
    dproj, dsw, dsb = ssm_conv_bwd(dxs, dbc, pre, proj, q["sw"], dproj, f"{tag}_dsconv")
    dw_in = jnp.concatenate([matmul(xn, dproj, "tn", f"{tag}_dwin"),
                             matmul(xn, ddtr, "tn", f"{tag}_dwin_dt")[:, :SSM_HEADS]], axis=1)
    sink.put("hyb_w_in", _to_chip_major(dw_in, 1))
    dxn = matmul(dproj, q["w_main"], "nt", f"{tag}_dxn")
    dxn = matmul(ddtr, q["w_dt"], "nt", f"{tag}_dxn_dt", res=dxn)
    dh2, dg = rms_bwd(h, _row(w["norm_mix"]), dxn, dh, f"{tag}_drms")
    grads = dict(norm_mix=dg.reshape(-1), conv_dw_w=dcw[:CONV_WIDTH], conv_dw_b=dcb.reshape(-1),
                 conv_ln_g=dlg.reshape(-1), conv_ln_b=dlb.reshape(-1), ssm_conv_w=dsw[:SSM_CONV], ssm_conv_b=dsb.reshape(-1),
                 ssm_dt_bias=ddtb.reshape(-1), ssm_a_log=dalog.reshape(-1), ssm_d=dd.reshape(-1), ssm_norm=dgamma.reshape(-1))
    return dh2, grads


def rope_tables(n):
    half = HEAD_DIM // 2
    inv = ROPE_THETA ** (-jnp.arange(0, HEAD_DIM, 2, dtype=F32) / HEAD_DIM)
    ang = jnp.arange(n, dtype=F32)[:, None] * inv[None, :]
    cos, sin = jnp.cos(ang), jnp.sin(ang)
    reps = LANE // HEAD_DIM
    return jnp.tile(jnp.concatenate([cos, cos], axis=1), (1, reps)), jnp.tile(jnp.concatenate([-sin, sin], axis=1), (1, reps))


def att_fwd(h, w, tables, tag):
    cos, sin = tables
    xn = rms_fwd(h, _row(w["norm_mix"]), f"{tag}_rms")
    qkv = matmul(xn, w["att_w_qkv"], "nn", f"{tag}_qkv", bias=_row(w["att_b_qkv"]))
    q, k, v = rope_fwd(qkv, cos, sin, f"{tag}_rope")
    sinks = _pad_lanes(w["att_sinks"])
    o, lse = attn_fwd(q, k, v, sinks, f"{tag}_attn")
    h2 = matmul(o, w["att_w_o"], "nn", f"{tag}_o", bias=_row(w["att_b_o"]), res=h)
    return h2, (h, xn, q, k, v, o, lse, sinks)


def att_bwd(dh, dh_colsum, w, saved, tables, tag, sink, layer):
    cos, sin = tables
    h, xn, q, k, v, o, lse, sinks = saved
    do = matmul(dh, w["att_w_o"], "nt", f"{tag}_do")
    sink.mm("att_w_o", layer, o, dh, f"{tag}_dwo")
    dq, dka, dkb, dva, dvb, dsk = attn_bwd(q, k, v, o, do, lse, sinks, f"{tag}_dattn")
    dqkv, dbqkv = attn_grad_merge(dq, dka, dkb, dva, dvb, cos, sin, f"{tag}_drope")
    sink.mm("att_w_qkv", layer, xn, dqkv, f"{tag}_dwqkv")
    dxn = matmul(dqkv, w["att_w_qkv"], "nt", f"{tag}_dxn")
    dh2, dg = rms_bwd(h, _row(w["norm_mix"]), dxn, dh, f"{tag}_drms")
    grads = dict(norm_mix=dg.reshape(-1), att_b_qkv=dbqkv.reshape(-1), att_sinks=dsk[0, :ATT_HEADS],
                 att_b_o=dh_colsum.reshape(-1))
    return dh2, grads


def ple_block_fwd(h, pe, w, tag):
    xn = rms_fwd(h, _row(w["ple_norm"]), f"{tag}_rms")
    gl = matmul(xn, w["ple_gate_w"], "nn", f"{tag}_gate")
    pp = matmul(pe, w["ple_proj_w"], "nn", f"{tag}_proj")
    return ple_fwd(h, gl, pp, f"{tag}_mix"), (h, xn, gl, pp, pe)


def ple_block_bwd(dh, w, saved, tag, sink, layer):
    h, xn, gl, pp, pe = saved
    dpp, dgl = ple_bwd(dh, gl, pp, f"{tag}_dmix")
    sink.mm("ple_proj_w", layer, pe, dpp, f"{tag}_dwp")
    sink.mm("ple_gate_w", layer, xn, dgl, f"{tag}_dwg")
    dxn = matmul(dgl, w["ple_gate_w"], "nt", f"{tag}_dxn")
    dh2, dg = rms_bwd(h, _row(w["ple_norm"]), dxn, dh, f"{tag}_drms")
    return dh2, dict(ple_norm=dg.reshape(-1))


PER_LAYER = ("norm_ffn1", "ffn1_w_in", "ffn1_w_out", "norm_mix", "norm_ffn2", "ffn2_w_in", "ffn2_w_out",
             "ple_norm", "ple_gate_w", "ple_proj_w")
EVEN_ONLY = ("hyb_w_in", "conv_dw_w", "conv_dw_b", "conv_ln_g", "conv_ln_b", "ssm_conv_w", "ssm_conv_b",
             "ssm_dt_bias", "ssm_a_log", "ssm_d", "ssm_norm", "hyb_w_out")
ODD_ONLY = ("att_w_qkv", "att_b_qkv", "att_sinks", "att_w_o", "att_b_o")


def _layer_index(k, i):
    if k in PER_LAYER:
        return i
    if k in (EVEN_ONLY if i % 2 == 0 else ODD_ONLY):
        return i // 2
    return None


def trunk_fwd_bwd(x, pe, target, layers, final_norm, sink):
    depth = len(layers)
    tables = rope_tables(x.shape[0])
    h = x
    saved = []
    for i, w in enumerate(layers):
        h, s1 = ffn_fwd(h, w["norm_ffn1"], w["ffn1_w_in"], w["ffn1_w_out"], f"l{i}_ffn1")
        if i % 2 == 0:
            h, s2 = hyb_fwd(h, w, f"l{i}_hyb")
        else:
            h, s2 = att_fwd(h, w, tables, f"l{i}_att")
        h, s3 = ffn_fwd(h, w["norm_ffn2"], w["ffn2_w_in"], w["ffn2_w_out"], f"l{i}_ffn2")
        h, s4 = ple_block_fwd(h, pe[i], w, f"l{i}_ple")
        saved.append((s1, s2, s3, s4))
    dh, dgf, loss = loss_head(h, _row(final_norm), target, "loss_head")
    grads = {}
    for i in reversed(range(depth)):
        w = layers[i]
        s1, s2, s3, s4 = saved[i]
        dh, g = ple_block_bwd(dh, w, s4, f"l{i}_ple", sink, i)
        odd = i % 2 == 1
        out = ffn_bwd(dh, w["norm_ffn2"], w["ffn2_w_in"], w["ffn2_w_out"], s3, f"l{i}_ffn2", sink,
                      ("ffn2_w_in", "ffn2_w_out"), i, colsum=odd)
        dh = out[0]
        g.update(norm_ffn2=out[1])
        if odd:
            dh, gm = att_bwd(dh, out[2], w, s2, tables, f"l{i}_att", sink, i // 2)
        else:
            dh, gm = hyb_bwd(dh, w, s2, f"l{i}_hyb", sink, i // 2)
        g.update(gm)
        out = ffn_bwd(dh, w["norm_ffn1"], w["ffn1_w_in"], w["ffn1_w_out"], s1, f"l{i}_ffn1", sink,
                      ("ffn1_w_in", "ffn1_w_out"), i)
        dh = out[0]
        g.update(norm_ffn1=out[1])
        for k, v in g.items():
            grads.setdefault(k, []).insert(0, v)
    grads = {k: jnp.stack(v) for k, v in grads.items()}
    grads["final_norm"] = dgf.reshape(-1)
    return loss, dh, grads


def _me():
    return lax.axis_index("x"), lax.axis_index("y"), lax.axis_index("c")


def _flip(v, f):
    return 1 - v if f else v


def _remote(src, dst, send_sems, recv_sems, k, dev):
    return pltpu.make_async_remote_copy(src_ref=src, dst_ref=dst, send_sem=send_sems.at[k], recv_sem=recv_sems.at[k],
                                        device_id=dev, device_id_type=MESH)


CHIP_FLIPS = ((1, 0), (0, 1), (1, 1))
DEV_FLIPS = tuple((fx, fy, fc) for fx in (0, 1) for fy in (0, 1) for fc in (0, 1))[1:]


def all_gather_chips(xs, name):
    na = len(xs)
    halves = [x.shape[0] // 2 for x in xs]
    assert all(x.shape[0] % 2 == 0 for x in xs)

    def body(*refs):
        x_refs, out_refs = refs[:na], refs[na:2 * na]
        send_sems, recv_sems = refs[2 * na:]
        mx, my, mc = _me()
        chip = 2 * mx + my
        sib = (mx, my, 1 - mc)
        peers = [(_flip(mx, fx), _flip(my, fy)) for fx, fy in CHIP_FLIPS]

        def rows(a, ch, hc):
            return out_refs[a].at[ch, pl.ds(hc * halves[a], halves[a]), :]

        def src(a):
            return x_refs[a].at[pl.ds(mc * halves[a], halves[a]), :]

        first = [_remote(src(a), rows(a, chip, mc), send_sems, recv_sems, 6 * a + j, (px, py, mc))
                 for j, (px, py) in enumerate(peers) for a in range(na)]
        for cp in first:
            cp.start()
        passed = []
        for j, (px, py) in enumerate(peers):
            for a in range(na):
                landed = rows(a, 2 * px + py, mc)
                _remote(src(a), landed, send_sems, recv_sems, 6 * a + j, (px, py, mc)).wait_recv()
                fw = _remote(landed, landed, send_sems, recv_sems, 6 * a + 3 + j, sib)
                fw.start()
                passed.append(fw)
        for j, (px, py) in enumerate(peers):
            for a in range(na):
                _remote(src(a), rows(a, 2 * px + py, 1 - mc), send_sems, recv_sems, 6 * a + 3 + j, sib).wait_recv()
        for cp in first + passed:
            cp.wait_send()

    outs = pl.pallas_call(
        body, name=name, out_shape=[S((N_CHIPS,) + x.shape, x.dtype) for x in xs], in_specs=[ANY] * na, out_specs=[ANY] * na,
        scratch_shapes=[pltpu.SemaphoreType.DMA((6 * na,)), pltpu.SemaphoreType.DMA((6 * na,))])(*xs)
    chip = 2 * lax.axis_index("x") + lax.axis_index("y")
    return [lax.dynamic_update_slice_in_dim(o, x[None], chip, axis=0) for o, x in zip(outs, xs)]


def all_gather_devices(v, name):
    r, l = v.shape

    def body(v_ref, out_ref, send_sems, recv_sems):
        mx, my, mc = _me()
        me = 4 * mx + 2 * my + mc
        peers = [(_flip(mx, fx), _flip(my, fy), _flip(mc, fc)) for fx, fy, fc in DEV_FLIPS]
        sends = [_remote(v_ref, out_ref.at[me], send_sems, recv_sems, j, p) for j, p in enumerate(peers)]
        for cp in sends:
            cp.start()
        for j, (px, py, pc) in enumerate(peers):
            _remote(v_ref, out_ref.at[4 * px + 2 * py + pc], send_sems, recv_sems, j, (px, py, pc)).wait_recv()
        for cp in sends:
            cp.wait_send()

    out = pl.pallas_call(
        body, name=name, out_shape=S((N_DEV, r, l), v.dtype), in_specs=[ANY], out_specs=ANY,
        scratch_shapes=[pltpu.SemaphoreType.DMA((7,)), pltpu.SemaphoreType.DMA((7,))])(v)
    me = 4 * lax.axis_index("x") + 2 * lax.axis_index("y") + lax.axis_index("c")
    return lax.dynamic_update_slice_in_dim(out, v[None], me, axis=0)


def sum_devices(g8, name):
    nd, r, l = g8.shape
    tile = r
    for t in (512, 256, 128, 64, 32, 16, 8):
        if r % t == 0:
            tile = t
            break

    def body(g_ref, o_ref):
        acc = g_ref[0]
        for d in range(1, nd):
            acc = acc + g_ref[d]
        o_ref[...] = acc

    return _call(body, name, (r // tile,), [pl.BlockSpec((nd, tile, l), lambda i: (0, i, 0))], _rs(tile, l), S((r, l), F32),
                 sem=("parallel",))(g8)


def exchange_halves(gs, name):
    na = len(gs)
    nch = gs[0].shape[0]

    def body(*refs):
        g_refs, out_refs = refs[:na], refs[na:2 * na]
        send_sems, recv_sems = refs[2 * na:]
        mx, my, mc = _me()
        sib = (mx, my, 1 - mc)
        cps = []
        for a in range(na):
            half = gs[a].shape[1] // 2
            for j in range(nch):
                cps.append(_remote(g_refs[a].at[j, pl.ds((1 - mc) * half, half), :], out_refs[a].at[j],
                                   send_sems, recv_sems, nch * a + j, sib))
        for cp in cps:
            cp.start()
        for cp in cps:
            cp.wait_recv()
        for cp in cps:
            cp.wait_send()

    return pl.pallas_call(
        body, name=name, out_shape=[S((nch, g.shape[1] // 2, g.shape[2]), g.dtype) for g in gs],
        in_specs=[ANY] * na, out_specs=[ANY] * na,
        scratch_shapes=[pltpu.SemaphoreType.DMA((nch * na,)), pltpu.SemaphoreType.DMA((nch * na,))])(*gs)


def add_halves(g4, got, name):
    nch, r, l = g4.shape
    half = r // 2
    tile = _pick_rows(half)
    nt = half // tile

    def body(g_ref, r_ref, a_ref, own_ref):
        j = pl.program_id(1)
        chip = 2 * lax.axis_index("x") + lax.axis_index("y")
        val = g_ref[0] + r_ref[0]
        a_ref[0] = val.astype(a_ref.dtype)

        @pl.when(j == chip)
        def _():
            own_ref[...] = val

    return pl.pallas_call(
        body, name=name, grid=(nt, nch),
        in_specs=[pl.BlockSpec((1, tile, l), lambda i, j: (j, lax.axis_index("c") * nt + i, 0)),
                  pl.BlockSpec((1, tile, l), lambda i, j: (j, i, 0))],
        out_specs=[pl.BlockSpec((1, tile, l), lambda i, j: (j, i, 0)), pl.BlockSpec((tile, l), lambda i, j: (i, 0))],
        out_shape=[S((nch, half, l), MXU_DTYPE), S((half, l), F32)],
        compiler_params=pltpu.CompilerParams(dimension_semantics=("parallel", "arbitrary"), vmem_limit_bytes=VMEM_LIMIT))(g4, got)


def _pick_rows(r, cap=512):
    for t in (512, 256, 128, 64, 32, 16):
        if t <= cap and r % t == 0:
            return t
    return r


def exchange_chips(parts, name):
    na = len(parts)

    def body(*refs):
        a_refs, out_refs = refs[:na], refs[na:2 * na]
        send_sems, recv_sems = refs[2 * na:]
        mx, my, mc = _me()
        peers = [(_flip(mx, fx), _flip(my, fy)) for fx, fy in CHIP_FLIPS]
        cps = [_remote(a_refs[a].at[2 * px + py], out_refs[a].at[j], send_sems, recv_sems, 3 * a + j, (px, py, mc))
               for j, (px, py) in enumerate(peers) for a in range(na)]
        for cp in cps:
            cp.start()
        for cp in cps:
            cp.wait_recv()
        for cp in cps:
            cp.wait_send()

    return pl.pallas_call(
        body, name=name, out_shape=[S((3,) + p.shape[1:], p.dtype) for p in parts], in_specs=[ANY] * na, out_specs=[ANY] * na,
        scratch_shapes=[pltpu.SemaphoreType.DMA((3 * na,)), pltpu.SemaphoreType.DMA((3 * na,))])(*parts)


def add_chips(own, got, name):
    h, l = own.shape
    tile = _pick_rows(h)

    def body(o_ref, g_ref, out_ref):
        out_ref[...] = ((o_ref[...] + g_ref[0].astype(F32)) + g_ref[1].astype(F32)) + g_ref[2].astype(F32)

    nt = h // tile
    return _call(body, name, (nt,), [_rs(tile, l), pl.BlockSpec((3, tile, l), lambda i: (0, i, 0))],
                 pl.BlockSpec((tile, l), lambda i: (lax.axis_index("c") * nt + i, 0)),
                 S((2 * h, l), F32), sem=("parallel",))(own, got)


def join_halves(bufs, name):
    na = len(bufs)

    def body(*refs):
        out_refs = refs[na:2 * na]
        send_sems, recv_sems = refs[2 * na:]
        mx, my, mc = _me()
        sib = (mx, my, 1 - mc)

        def half(a, hc):
            h = bufs[a].shape[0] // 2
            return out_refs[a].at[pl.ds(hc * h, h), :]

        cps = [_remote(half(a, mc), half(a, mc), send_sems, recv_sems, a, sib) for a in range(na)]
        for cp in cps:
            cp.start()
        for a in range(na):
            _remote(half(a, mc), half(a, 1 - mc), send_sems, recv_sems, a, sib).wait_recv()
        for cp in cps:
            cp.wait_send()

    return pl.pallas_call(
        body, name=name, out_shape=[S(b.shape, b.dtype) for b in bufs], in_specs=[ANY] * na, out_specs=[ANY] * na,
        input_output_aliases={a: a for a in range(na)},
        scratch_shapes=[pltpu.SemaphoreType.DMA((na,)), pltpu.SemaphoreType.DMA((na,))])(*bufs)


def reduce_scatter(gs, tag):
    got = exchange_halves(gs, f"{tag}_d2d")
    sums = [add_halves(g, r, f"{tag}_add1_{i}") for i, (g, r) in enumerate(zip(gs, got))]
    got2 = exchange_chips([a for a, _ in sums], f"{tag}_ici")
    parts = [add_chips(own, r, f"{tag}_add2_{i}") for i, ((_, own), r) in enumerate(zip(sums, got2))]
    return join_halves(parts, f"{tag}_join")


PACK_L = 1024
BIG_ROW_MULT = 512


def _pack(arrs, dtype, row_mult, lead=None):
    lead_shape = () if lead is None else arrs[0].shape[:lead]
    flat = jnp.concatenate([a.astype(dtype).reshape(lead_shape + (-1,)) for a in arrs], axis=-1)
    n = flat.shape[-1]
    unit = row_mult * PACK_L
    total = -(-n // unit) * unit
    flat = jnp.pad(flat, [(0, 0)] * len(lead_shape) + [(0, total - n)])
    return flat.reshape(lead_shape + (total // PACK_L, PACK_L))


def _unpack(packed, shapes, lead=None):
    lead_shape = () if lead is None else packed.shape[:lead]
    flat = packed.reshape(lead_shape + (-1,))
    out, off = [], 0
    for shp in shapes:
        n = int(np.prod(shp))
        out.append(flat[..., off:off + n].reshape(lead_shape + tuple(shp)))
        off += n
    return out


def _to_full(gathered, axis):
    t = jnp.moveaxis(gathered, 0, axis)
    shp = t.shape
    return t.reshape(shp[:axis] + (shp[axis] * shp[axis + 1],) + shp[axis + 2:])


def _to_chip_major(full, axis):
    shp = full.shape
    t = full.reshape(shp[:axis] + (N_CHIPS, shp[axis] // N_CHIPS) + shp[axis + 1:])
    return jnp.moveaxis(t, axis, 0)


WEIGHTS = ("norm_ffn1", "ffn1_w_in", "ffn1_w_out", "norm_mix", "norm_ffn2", "ffn2_w_in", "ffn2_w_out", "ple_norm",
           "ple_gate_w", "ple_proj_w", "hyb_w_in", "conv_dw_w", "conv_dw_b", "conv_ln_g", "conv_ln_b", "ssm_conv_w",
           "ssm_conv_b", "ssm_dt_bias", "ssm_a_log", "ssm_d", "ssm_norm", "hyb_w_out", "att_w_qkv", "att_b_qkv",
           "att_sinks", "att_w_o", "att_b_o", "final_norm")
SHARD_AXIS = dict(ffn1_w_in=2, ffn1_w_out=1, ffn2_w_in=2, ffn2_w_out=1, ple_gate_w=1, ple_proj_w=2, hyb_w_in=2,
                  conv_dw_w=2, ssm_conv_w=2, hyb_w_out=1, att_w_qkv=2, att_b_qkv=1, att_w_o=1, att_b_o=1)
BIG = ("ffn1_w_in", "ffn1_w_out", "ffn2_w_in", "ffn2_w_out", "ple_gate_w", "ple_proj_w", "hyb_w_in", "hyb_w_out",
       "att_w_qkv", "att_w_o")
ODD_WIDTH = "hyb_w_in"
BIG_FLAT = tuple(k for k in BIG if k != ODD_WIDTH)
SMALL_SHARDED = ("conv_dw_w", "ssm_conv_w", "att_b_qkv", "att_b_o")
SMALL = tuple(k for k in WEIGHTS if k not in BIG)


def _step(x, p, target, w, m, v):
    mx, my = lax.axis_index("x"), lax.axis_index("y")
    chip = 2 * mx + my

    depth = w["norm_ffn1"].shape[0]
    order = [(k, i) for i in range(depth) for k in BIG if _layer_index(k, i) is not None]
    gathered = all_gather_chips([w[k][_layer_index(k, i)].astype(MXU_DTYPE) for k, i in order], "gather_weights")
    small_g = all_gather_devices(_pack([w[k] for k in SMALL_SHARDED], F32, 8), "gather_small")[0::2]
    small_full = {k: _to_full(g, SHARD_AXIS[k])
                  for k, g in zip(SMALL_SHARDED, _unpack(small_g, [w[k].shape for k in SMALL_SHARDED], lead=1))}
    layers = [{k: small_full.get(k, w[k])[_layer_index(k, i)] for k in SMALL if _layer_index(k, i) is not None}
              for i in range(depth)]
    for (k, i), g in zip(order, gathered):
        if k == ODD_WIDTH:
            layers[i][k] = _to_full(g, 1)
        elif SHARD_AXIS[k] == 2:
            layers[i][k] = ColSharded(g)
        else:
            layers[i][k] = g.reshape(-1, g.shape[-1])

    sink = GradSink({k: w[k].shape for k in BIG})
    loss, dx, grads = trunk_fwd_bwd(x[0], p[:, 0], target[0], layers, w["final_norm"], sink)

    widths = list(sink.bufs)
    reduced = dict(zip(widths, reduce_scatter([sink.bufs[c] for c in widths], "grads")))
    g_out = {k: sink.take(k, reduced) for k in BIG}
    vec = _pack([loss[0:1, 0:1]] + [grads[k] for k in SMALL], F32, 8)
    vec = sum_devices(all_gather_devices(vec, "gather_vectors"), "sum_vectors")
    parts = _unpack(vec, [(1, 1)] + [grads[k].shape for k in SMALL])
    loss_out = parts[0].reshape(())
    for k, g in zip(SMALL, parts[1:]):
        if k in SHARD_AXIS:
            ax = SHARD_AXIS[k]
            g = lax.dynamic_slice_in_dim(g, chip * w[k].shape[ax], w[k].shape[ax], axis=ax)
        g_out[k] = g

    delta, new_m, new_v = {}, {}, {}
    for k in BIG:
        shp = w[k].shape
        two_d = lambda a: a.reshape(-1, shp[-1])
        d_, m_, v_ = adamw(two_d(w[k]), two_d(g_out[k]), two_d(m[k]), two_d(v[k]), f"adamw_{k}")
        delta[k], new_m[k], new_v[k] = d_.reshape(shp), m_.reshape(shp), v_.reshape(shp)
    shapes = [w[k].shape for k in SMALL]
    packed = [_pack([src[k] for k in SMALL], F32, 8) for src in (w, g_out, m, v)]
    outs = adamw(*packed, "adamw_small")
    for dst, o in zip((delta, new_m, new_v), outs):
        for k, a in zip(SMALL, _unpack(o, shapes)):
            dst[k] = a
    return ((loss_out, dx[None]) + tuple(g_out[k] for k in WEIGHTS) + tuple(delta[k] for k in WEIGHTS)
            + tuple(new_m[k] for k in WEIGHTS) + tuple(new_v[k] for k in WEIGHTS))


def kernel(x, p, norm_ffn1, ffn1_w_in, ffn1_w_out, norm_mix, norm_ffn2, ffn2_w_in, ffn2_w_out, ple_norm, ple_gate_w, ple_proj_w, hyb_w_in, conv_dw_w, conv_dw_b, conv_ln_g, conv_ln_b, ssm_conv_w, ssm_conv_b, ssm_dt_bias, ssm_a_log, ssm_d, ssm_norm, hyb_w_out, att_w_qkv, att_b_qkv, att_sinks, att_w_o, att_b_o, final_norm, loss_target, m_norm_ffn1, m_ffn1_w_in, m_ffn1_w_out, m_norm_mix, m_norm_ffn2, m_ffn2_w_in, m_ffn2_w_out, m_ple_norm, m_ple_gate_w, m_ple_proj_w, m_hyb_w_in, m_conv_dw_w, m_conv_dw_b, m_conv_ln_g, m_conv_ln_b, m_ssm_conv_w, m_ssm_conv_b, m_ssm_dt_bias, m_ssm_a_log, m_ssm_d, m_ssm_norm, m_hyb_w_out, m_att_w_qkv, m_att_b_qkv, m_att_sinks, m_att_w_o, m_att_b_o, m_final_norm, v_norm_ffn1, v_ffn1_w_in, v_ffn1_w_out, v_norm_mix, v_norm_ffn2, v_ffn2_w_in, v_ffn2_w_out, v_ple_norm, v_ple_gate_w, v_ple_proj_w, v_hyb_w_in, v_conv_dw_w, v_conv_dw_b, v_conv_ln_g, v_conv_ln_b, v_ssm_conv_w, v_ssm_conv_b, v_ssm_dt_bias, v_ssm_a_log, v_ssm_d, v_ssm_norm, v_hyb_w_out, v_att_w_qkv, v_att_b_qkv, v_att_sinks, v_att_w_o, v_att_b_o, v_final_norm):
    given = locals()
    w = {k: given[k] for k in WEIGHTS}
    m = {k: given["m_" + k] for k in WEIGHTS}
    v = {k: given["v_" + k] for k in WEIGHTS}
    return _step(x, p, loss_target, w, m, v)
```

```python
import functools
import math

import numpy as np
import jax
import jax.numpy as jnp
from jax import lax
from jax.experimental import pallas as pl
from jax.experimental.pallas import tpu as pltpu

F32 = jnp.float32
BF16 = jnp.bfloat16
MXU_DTYPE = jnp.bfloat16
S = jax.ShapeDtypeStruct
MESH = pl.DeviceIdType.MESH

V7X_VMEM_BYTES = 64 * 2**20
VMEM_LIMIT = 48 * 2**20
LANE = 128

EPS = 1e-6
SSM_HEADS = 16
HEAD_DIM = 64
SSM_GROUPS = 2
SSM_STATE = 128
SSM_CONV = 4
CHUNK = 128
CONV_WIDTH = 31
ATT_HEADS = 16
ATT_KV_HEADS = 4
WINDOW = 128
ROPE_THETA = 10000.0
ADAM_LR = 0.001
ADAM_B1 = 0.9
ADAM_B2 = 0.999
ADAM_EPS = 1e-08
ADAM_WD = 0.01
ADAM_STEP = 10

N_CHIPS = 4
N_DEV = 8

NN = ((1,), (0,))
NT = ((1,), (1,))
TN = ((0,), (0,))


def _mm(a, b, dims=NN):
    return lax.dot_general(a.astype(MXU_DTYPE), b.astype(MXU_DTYPE), (dims, ((), ())), preferred_element_type=F32)


def _split3(a):
    hi = a.astype(BF16)
    r = a - hi.astype(F32)
    mid = r.astype(BF16)
    lo = (r - mid.astype(F32)).astype(BF16)
    return hi, mid, lo


def _mm01(a, onehot, dims=NN):
    o = onehot.astype(BF16)
    out = None
    for part in _split3(a):
        t = lax.dot_general(part, o, (dims, ((), ())), preferred_element_type=F32)
        out = t if out is None else out + t
    return out


def _01mm(onehot, a):
    o = onehot.astype(BF16)
    out = None
    for part in _split3(a):
        t = lax.dot_general(o, part, (NN, ((), ())), preferred_element_type=F32)
        out = t if out is None else out + t
    return out


def _sigmoid(x):
    return 0.5 * jnp.tanh(0.5 * x) + 0.5


def _softplus(x):
    return jnp.maximum(x, 0.0) + jnp.log(1.0 + jnp.exp(-jnp.abs(x)))


def _iota(shape, axis):
    return lax.broadcasted_iota(jnp.int32, shape, axis)


def _head_indicator(width, heads, transposed=False):
    per = width // heads
    if transposed:
        return (_iota((heads, width), 1) // per == _iota((heads, width), 0)).astype(F32)
    return (_iota((width, heads), 0) // per == _iota((width, heads), 1)).astype(F32)


def _acc(ref, i, val):
    @pl.when(i == 0)
    def _():
        ref[...] = val

    @pl.when(i > 0)
    def _():
        ref[...] += val


def _rs(tile, width, col=0, shift=0, n=None):
    if shift == 0:
        return pl.BlockSpec((tile, width), lambda i: (i, col))
    if shift < 0:
        return pl.BlockSpec((tile, width), lambda i: (jnp.maximum(i - 1, 0), col))
    return pl.BlockSpec((tile, width), lambda i: (jnp.minimum(i + 1, n - 1), col))


def _ps(shape):
    return pl.BlockSpec(shape, lambda i: (0,) * len(shape))


def _call(body, name, grid, in_specs, out_specs, out_shape, scratch=(), sem=None):
    return pl.pallas_call(
        body, name=name, grid=grid, in_specs=in_specs, out_specs=out_specs, out_shape=out_shape,
        scratch_shapes=list(scratch),
        compiler_params=pltpu.CompilerParams(dimension_semantics=sem, vmem_limit_bytes=VMEM_LIMIT))


def _row_tile(n, target):
    t = min(n, target)
    assert n % t == 0, (n, t)
    return t


def _pick_tile(dim, target):
    if dim <= target:
        return dim
    t = (int(1.4 * target) // LANE) * LANE
    while t >= LANE:
        if dim % t == 0:
            return t
        t -= LANE
    return dim


ANY = pl.BlockSpec(memory_space=pl.ANY)


def _paired(j):
    return (j % 2) * 2 + j // 2


class ColSharded:
    def __init__(self, arr, paired=False):
        self.arr, self.paired = arr, paired
        self.nch, self.rows, self.per = arr.shape
        self.shape = (self.rows, self.nch * self.per)

    def chip(self, j):
        return _paired(j) if self.paired else j


class Slot:
    def __init__(self, buf, kind, per, off, c0=0, paired=False):
        self.buf, self.kind, self.per, self.off, self.c0, self.paired = buf, kind, per, off, c0, paired

    def chip(self, j):
        return _paired(j) if self.paired else j


def matmul(a, b, mode, name, *, out_dtype=F32, scale=None, res=None, bias=None, into=None, tm=1024, tn=1024, tk=1024):
    bshape = b.shape
    if mode == "nn":
        (m, k), (k2, n) = a.shape, bshape
    elif mode == "nt":
        (m, k), (n, k2) = a.shape, bshape
    else:
        (k, m), (k2, n) = a.shape, bshape
    assert k == k2, (a.shape, bshape, mode)
    tm, tn, tk = _pick_tile(m, tm), _pick_tile(n, tn), _pick_tile(k, tk)
    if isinstance(b, ColSharded):
        if mode == "nn":
            tn = b.per
        else:
            assert mode == "nt"
            tk = b.per
    if into is not None:
        if into.kind == "c":
            tn = into.per
            assert into.off % tm == 0 and n == N_CHIPS * into.per
        else:
            tm = max(1, min(m, int(1.4 * 1024)) // into.per) * into.per
            assert m % tm == 0 and into.off % into.per == 0 and into.c0 % (tm // into.per) == 0
    nk = k // tk
    dims = {"nn": NN, "nt": NT, "tn": TN}[mode]
    a_spec = (pl.BlockSpec((tk, tm), lambda i, j, kk: (kk, i)) if mode == "tn"
              else pl.BlockSpec((tm, tk), lambda i, j, kk: (i, kk)))
    if isinstance(b, ColSharded):
        bchip = b.chip
        b_spec = (pl.BlockSpec((None, tk, tn), lambda i, j, kk: (bchip(j), kk, 0)) if mode == "nn"
                  else pl.BlockSpec((None, tn, tk), lambda i, j, kk: (bchip(kk), j, 0)))
        b = b.arr
    else:
        b_spec = (pl.BlockSpec((tn, tk), lambda i, j, kk: (j, kk)) if mode == "nt"
                  else pl.BlockSpec((tk, tn), lambda i, j, kk: (kk, j)))
    plain_o = pl.BlockSpec((tm, tn), lambda i, j, kk: (i, j))
    ins, in_specs = [a, b], [a_spec, b_spec]
    if bias is not None:
        ins.append(bias)
        in_specs.append(pl.BlockSpec((1, tn), lambda i, j, kk: (0, j)))
    if res is not None:
        ins.append(res)
        in_specs.append(plain_o)
    aliases = {}
    if into is None:
        o_spec, o_shape = plain_o, S((m, n), out_dtype)
    else:
        aliases = {len(ins): 0}
        ins.append(into.buf)
        in_specs.append(ANY)
        o_shape = S(into.buf.shape, into.buf.dtype)
        if into.kind == "c":
            ob, ochip = into.off // tm, into.chip
            o_spec = pl.BlockSpec((None, tm, tn), lambda i, j, kk: (ochip(j), ob + i, 0))
        else:
            q, ob = tm // into.per, into.off // into.per
            cb = into.c0 // q
            o_spec = pl.BlockSpec((q, into.per, tn), lambda i, j, kk: (cb + i, ob, j))

    def body(*refs):
        a_ref, b_ref = refs[0], refs[1]
        o_ref, acc_ref = refs[-2], refs[-1]
        kk = pl.program_id(2)

        @pl.when(kk == 0)
        def _():
            acc_ref[...] = jnp.zeros_like(acc_ref)

        acc_ref[...] += _mm(a_ref[...], b_ref[...], dims)

        @pl.when(kk == nk - 1)
        def _():
            out = acc_ref[...]
            if scale is not None:
                out = out * scale
            pos = 2
            if bias is not None:
                out = out + refs[pos][...]
                pos += 1
            if res is not None:
                out = out + refs[pos][...]
            o_ref[...] = out.astype(o_ref.dtype).reshape(o_ref.shape)

    return pl.pallas_call(
        body, name=name, grid=(m // tm, n // tn, nk), in_specs=in_specs, out_specs=o_spec, out_shape=o_shape,
        scratch_shapes=[pltpu.VMEM((tm, tn), F32)], input_output_aliases=aliases,
        compiler_params=pltpu.CompilerParams(dimension_semantics=("parallel", "parallel", "arbitrary"),
                                             vmem_limit_bytes=VMEM_LIMIT))(*ins)


def rms_fwd(h, g, name):
    n, d = h.shape
    tile = _row_tile(n, 512)

    def body(h_ref, g_ref, o_ref):
        x = h_ref[...]
        r = lax.rsqrt(jnp.mean(x * x, axis=-1, keepdims=True) + EPS)
        o_ref[...] = (x * r * g_ref[...]).astype(o_ref.dtype)

    return _call(body, name, (n // tile,), [_rs(tile, d), _ps((1, d))], _rs(tile, d), S((n, d), MXU_DTYPE),
                 sem=("parallel",))(h, g)


def _rms_bwd_math(x, g, dy):
    r = lax.rsqrt(jnp.mean(x * x, axis=-1, keepdims=True) + EPS)
    xh = x * r
    dg = jnp.sum(dy * xh, axis=0, keepdims=True)
    dxh = dy * g
    dx = r * (dxh - xh * jnp.mean(dxh * xh, axis=-1, keepdims=True))
    return dx, dg


def rms_bwd(h, g, dxn, dh_in, name, colsum=False):
    n, d = h.shape
    tile = _row_tile(n, 256)

    def body(h_ref, g_ref, dxn_ref, dh_ref, o_ref, dg_ref, *cs_ref):
        i = pl.program_id(0)
        dx, dg = _rms_bwd_math(h_ref[...], g_ref[...], dxn_ref[...].astype(F32))
        out = dh_ref[...] + dx
        o_ref[...] = out
        _acc(dg_ref, i, dg)
        if colsum:
            _acc(cs_ref[0], i, jnp.sum(out, axis=0, keepdims=True))

    outs = [S((n, d), F32), S((1, d), F32)] + ([S((1, d), F32)] if colsum else [])
    ospecs = [_rs(tile, d), _ps((1, d))] + ([_ps((1, d))] if colsum else [])
    return _call(body, name, (n // tile,), [_rs(tile, d), _ps((1, d)), _rs(tile, d), _rs(tile, d)], ospecs, outs,
                 sem=("arbitrary",))(h, g, dxn, dh_in)


def swiglu_in(h, g, w_in, name):
    n, d = h.shape
    per = w_in.per
    nj = w_in.nch // 2
    tile = _row_tile(n, 512)

    def body(h_ref, g_ref, wg_ref, wu_ref, xn_ref, u_ref, hm_ref):
        x = h_ref[...]
        r = lax.rsqrt(jnp.mean(x * x, axis=-1, keepdims=True) + EPS)
        xn = (x * r * g_ref[...]).astype(xn_ref.dtype)

        @pl.when(pl.program_id(1) == 0)
        def _():
            xn_ref[...] = xn

        a = _mm(xn, wg_ref[...])
        b = _mm(xn, wu_ref[...])
        u_ref[:, :per] = a
        u_ref[:, per:] = b
        hm_ref[...] = (a * _sigmoid(a) * b).astype(hm_ref.dtype)

    return pl.pallas_call(
        body, name=name, grid=(n // tile, nj),
        in_specs=[pl.BlockSpec((tile, d), lambda i, j: (i, 0)), pl.BlockSpec((1, d), lambda i, j: (0, 0)),
                  pl.BlockSpec((None, d, per), lambda i, j: (j, 0, 0)), pl.BlockSpec((None, d, per), lambda i, j: (nj + j, 0, 0))],
        out_specs=[pl.BlockSpec((tile, d), lambda i, j: (i, 0)), pl.BlockSpec((tile, 2 * per), lambda i, j: (i, j)),
                   pl.BlockSpec((tile, per), lambda i, j: (i, j))],
        out_shape=[S((n, d), MXU_DTYPE), S((n, 2 * nj * per), F32), S((n, nj * per), MXU_DTYPE)],
        compiler_params=pltpu.CompilerParams(dimension_semantics=("parallel", "arbitrary"), vmem_limit_bytes=VMEM_LIMIT),
    )(h, g, w_in.arr, w_in.arr)


def swiglu_out_bwd(dh, w_out, u, name):
    n, d = dh.shape
    f = w_out.shape[0]
    per = u.shape[1] // 4
    nj = f // per
    tile = _row_tile(n, 512)

    def body(dh_ref, w_ref, u_ref, du_ref):
        dm = 0.5 * _mm(dh_ref[...], w_ref[...], NT)
        a = u_ref[:, :per]
        b = u_ref[:, per:]
        s = _sigmoid(a)
        du_ref[:, :per] = (dm * b * s * (1.0 + a * (1.0 - s))).astype(du_ref.dtype)
        du_ref[:, per:] = (dm * a * s).astype(du_ref.dtype)

    return pl.pallas_call(
        body, name=name, grid=(n // tile, nj),
        in_specs=[pl.BlockSpec((tile, d), lambda i, j: (i, 0)), pl.BlockSpec((per, d), lambda i, j: (j, 0)),
                  pl.BlockSpec((tile, 2 * per), lambda i, j: (i, j))],
        out_specs=pl.BlockSpec((tile, 2 * per), lambda i, j: (i, j)),
        out_shape=S(u.shape, MXU_DTYPE),
        compiler_params=pltpu.CompilerParams(dimension_semantics=("parallel", "parallel"), vmem_limit_bytes=VMEM_LIMIT),
    )(dh, w_out, u)


def ple_fwd(h, gl, pp, name):
    n, d = h.shape
    tile = _row_tile(n, 512)

    def body(h_ref, gl_ref, pp_ref, o_ref):
        o_ref[...] = h_ref[...] + _sigmoid(gl_ref[...]) * pp_ref[...]

    return _call(body, name, (n // tile,), [_rs(tile, d)] * 3, _rs(tile, d), S((n, d), F32), sem=("parallel",))(h, gl, pp)


def ple_bwd(dh, gl, pp, name):
    n, d = dh.shape
    tile = _row_tile(n, 512)

    def body(dh_ref, gl_ref, pp_ref, dpp_ref, dgl_ref):
        g = _sigmoid(gl_ref[...])
        dh_ = dh_ref[...]
        dpp_ref[...] = (dh_ * g).astype(dpp_ref.dtype)
        dgl_ref[...] = (dh_ * pp_ref[...] * g * (1.0 - g)).astype(dgl_ref.dtype)

    return _call(body, name, (n // tile,), [_rs(tile, d)] * 3, [_rs(tile, d)] * 2, [S((n, d), MXU_DTYPE)] * 2,
                 sem=("parallel",))(dh, gl, pp)


def loss_head(h, g, target, name):
    n, d = h.shape
    tile = _row_tile(n, 256)

    def body(h_ref, g_ref, t_ref, dh_ref, dg_ref, loss_ref):
        i = pl.program_id(0)
        x = h_ref[...]
        gg = g_ref[...]
        r = lax.rsqrt(jnp.mean(x * x, axis=-1, keepdims=True) + EPS)
        err = x * r * gg - t_ref[...]
        part = 0.5 * jnp.sum(jnp.mean(err * err, axis=-1, keepdims=True), axis=0, keepdims=True)
        dx, dg = _rms_bwd_math(x, gg, err * (1.0 / d))
        dh_ref[...] = dx
        _acc(dg_ref, i, dg)
        _acc(loss_ref, i, jnp.broadcast_to(part, (8, LANE)))

    return _call(body, name, (n // tile,), [_rs(tile, d), _ps((1, d)), _rs(tile, d)],
                 [_rs(tile, d), _ps((1, d)), _ps((8, LANE))], [S((n, d), F32), S((1, d), F32), S((8, LANE), F32)],
                 sem=("arbitrary",))(h, g, target)


def adamw(w, g, m, v, name):
    r, c = w.shape
    tile = r
    for t in (512, 256, 128, 64, 32, 16, 8):
        if r % t == 0 and t * c * 4 <= 2**21:
            tile = t
            break
    c1 = np.float32(1.0 - ADAM_B1 ** ADAM_STEP)
    c2 = np.float32(1.0 - ADAM_B2 ** ADAM_STEP)

    def body(w_ref, g_ref, m_ref, v_ref, d_ref, mo_ref, vo_ref):
        gg = g_ref[...]
        mm = ADAM_B1 * m_ref[...] + (1.0 - ADAM_B1) * gg
        vv = ADAM_B2 * v_ref[...] + (1.0 - ADAM_B2) * (gg * gg)
        mo_ref[...] = mm
        vo_ref[...] = vv
        d_ref[...] = -ADAM_LR * ((mm / c1) / (jnp.sqrt(vv / c2) + ADAM_EPS) + ADAM_WD * w_ref[...])

    return _call(body, name, (r // tile,), [_rs(tile, c)] * 4, [_rs(tile, c)] * 3, [S((r, c), F32)] * 3,
                 sem=("parallel",))(w, g, m, v)


def _taps_fwd(sc, w_ref, width, halo, tile, acc):
    for k in range(width):
        o = halo - (width - 1) + k
        acc = acc + w_ref[k:k + 1, :] * sc[o:o + tile, :]
    return acc


def _taps_bwd_x(sc_d, w_ref, width, tile, acc):
    for k in range(width):
        o = (width - 1) - k
        acc = acc + w_ref[k:k + 1, :] * sc_d[o:o + tile, :]
    return acc


def _taps_bwd_w(dy, sc, dw_ref, width, halo, tile, i):
    @pl.when(i == 0)
    def _():
        dw_ref[...] = jnp.zeros_like(dw_ref)

    for k in range(width):
        o = halo - (width - 1) + k
        dw_ref[k:k + 1, :] += jnp.sum(dy * sc[o:o + tile, :], axis=0, keepdims=True)


def _ln_stats(x):
    mu = jnp.mean(x, axis=-1, keepdims=True)
    xc = x - mu
    r = lax.rsqrt(jnp.mean(xc * xc, axis=-1, keepdims=True) + EPS)
    return xc * r, r


def conv_group_fwd(proj, cw, cb, lg, lb, name):
    n = proj.shape[0]
    d = cw.shape[1]
    tile = _row_tile(n, 256)
    halo = 32

    def body(v_ref, g_ref, vp_ref, gp_ref, cw_ref, cb_ref, lg_ref, lb_ref, u_ref, u1_ref, sc):
        i = pl.program_id(0)
        first = (i > 0).astype(F32)
        sc[0:halo, :] = vp_ref[tile - halo:, :] * _sigmoid(gp_ref[tile - halo:, :]) * first
        sc[halo:, :] = v_ref[...] * _sigmoid(g_ref[...])
        u1 = _taps_fwd(sc, cw_ref, CONV_WIDTH, halo, tile, jnp.zeros((tile, d), F32) + cb_ref[...])
        u1_ref[...] = u1
        xh, _ = _ln_stats(u1)
        y = xh * lg_ref[...] + lb_ref[...]
        u_ref[...] = (y * _sigmoid(y)).astype(u_ref.dtype)

    return _call(body, name, (n // tile,),
                 [_rs(tile, d, 0), _rs(tile, d, 1), _rs(tile, d, 0, -1), _rs(tile, d, 1, -1),
                  _ps(cw.shape), _ps((1, d)), _ps((1, d)), _ps((1, d))],
                 [_rs(tile, d), _rs(tile, d)], [S((n, d), MXU_DTYPE), S((n, d), F32)],
                 scratch=[pltpu.VMEM((halo + tile, d), F32)], sem=("arbitrary",))(proj, proj, proj, proj, cw, cb, lg, lb)


def conv_group_bwd(du, u1, proj, cw, lg, lb, name):
    n = proj.shape[0]
    d = cw.shape[1]
    tile = _row_tile(n, 256)
    halo = 32
    nt = n // tile

    def body(du_ref, dun_ref, u1_ref, u1n_ref, v_ref, g_ref, vp_ref, gp_ref, cw_ref, lg_ref, lb_ref,
             dp_ref, dcw_ref, dcb_ref, dlg_ref, dlb_ref, sc, sc_d):
        i = pl.program_id(0)

        def ln_swish_bwd(dy_, u1_):
            xh, r = _ln_stats(u1_)
            y = xh * lg_ref[...] + lb_ref[...]
            s = _sigmoid(y)
            dyy = dy_ * s * (1.0 + y * (1.0 - s))
            dxh = dyy * lg_ref[...]
            dx = r * (dxh - jnp.mean(dxh, axis=-1, keepdims=True) - xh * jnp.mean(dxh * xh, axis=-1, keepdims=True))
            return dx, jnp.sum(dyy * xh, axis=0, keepdims=True), jnp.sum(dyy, axis=0, keepdims=True)

        du1, dlg, dlb = ln_swish_bwd(du_ref[...].astype(F32), u1_ref[...])
        du1n, _, _ = ln_swish_bwd(dun_ref[0:halo, :].astype(F32), u1n_ref[0:halo, :])
        sc_d[0:tile, :] = du1
        sc_d[tile:, :] = du1n * (i < nt - 1).astype(F32)
        sig = _sigmoid(g_ref[...])
        val = v_ref[...]
        sc[0:halo, :] = vp_ref[tile - halo:, :] * _sigmoid(gp_ref[tile - halo:, :]) * (i > 0).astype(F32)
        sc[halo:, :] = val * sig
        du0 = _taps_bwd_x(sc_d, cw_ref, CONV_WIDTH, tile, jnp.zeros((tile, d), F32))
        _taps_bwd_w(du1, sc, dcw_ref, CONV_WIDTH, halo, tile, i)
        _acc(dcb_ref, i, jnp.sum(du1, axis=0, keepdims=True))
        _acc(dlg_ref, i, dlg)
        _acc(dlb_ref, i, dlb)
        dp_ref[:, :d] = (du0 * sig).astype(dp_ref.dtype)
        dp_ref[:, d:] = (du0 * val * sig * (1.0 - sig)).astype(dp_ref.dtype)

    return _call(body, name, (nt,),
                 [_rs(tile, d), _rs(tile, d, 0, 1, nt), _rs(tile, d), _rs(tile, d, 0, 1, nt),
                  _rs(tile, d, 0), _rs(tile, d, 1), _rs(tile, d, 0, -1), _rs(tile, d, 1, -1),
                  _ps(cw.shape), _ps((1, d)), _ps((1, d))],
                 [_rs(tile, 2 * d), _ps(cw.shape), _ps((1, d)), _ps((1, d)), _ps((1, d))],
                 [S((n, proj.shape[1]), MXU_DTYPE), S(cw.shape, F32), S((1, d), F32), S((1, d), F32), S((1, d), F32)],
                 scratch=[pltpu.VMEM((halo + tile, d), F32), pltpu.VMEM((tile + halo, d), F32)],
                 sem=("arbitrary",))(du, du, u1, u1, proj, proj, proj, proj, cw, lg, lb)


def ssm_conv_fwd(proj, dtr, sw, sb, dtb, name):
    n = proj.shape[0]
    w = sw.shape[1]
    inner = SSM_HEADS * HEAD_DIM
    tile = _row_tile(n, 256)
    halo = 8

    def body(x_ref, xp_ref, dtr_ref, sw_ref, sb_ref, dtb_ref, pre_ref, xs_ref, bc_ref, dt_ref, sc):
        i = pl.program_id(0)
        sc[0:halo, :] = xp_ref[tile - halo:, :] * (i > 0).astype(F32)
        sc[halo:, :] = x_ref[...]
        pre = _taps_fwd(sc, sw_ref, SSM_CONV, halo, tile, jnp.zeros((tile, w), F32) + sb_ref[...])
        pre_ref[...] = pre
        act = pre * _sigmoid(pre)
        xs_ref[...] = act[:, :inner]
        bc_ref[...] = act[:, inner:]
        dt = _softplus(dtr_ref[...] + dtb_ref[...])
        dt_ref[...] = jnp.where(_iota(dt.shape, 1) < SSM_HEADS, dt, 0.0)

    return _call(body, name, (n // tile,),
                 [_rs(tile, w, 2), _rs(tile, w, 2, -1), _rs(tile, LANE), _ps(sw.shape), _ps((1, w)), _ps((1, LANE))],
                 [_rs(tile, w), _rs(tile, inner), _rs(tile, w - inner), _rs(tile, LANE)],
                 [S((n, w), F32), S((n, inner), F32), S((n, w - inner), F32), S((n, LANE), F32)],
                 scratch=[pltpu.VMEM((halo + tile, w), F32)], sem=("arbitrary",))(proj, proj, dtr, sw, sb, dtb)


def ssm_conv_bwd(dxs, dbc, pre, proj, sw, dproj, name):
    n = proj.shape[0]
    w = sw.shape[1]
    inner = SSM_HEADS * HEAD_DIM
    tile = _row_tile(n, 256)
    halo = 8
    nt = n // tile

    def body(dxs_ref, dxsn_ref, dbc_ref, dbcn_ref, pre_ref, pren_ref, x_ref, xp_ref, sw_ref, dp_in_ref,
             dx_ref, dsw_ref, dsb_ref, sc, sc_d):
        i = pl.program_id(0)

        def silu_bwd(d_, p_):
            s = _sigmoid(p_)
            return d_ * s * (1.0 + p_ * (1.0 - s))

        sc_d[0:tile, :inner] = silu_bwd(dxs_ref[...], pre_ref[:, :inner])
        sc_d[0:tile, inner:] = silu_bwd(dbc_ref[...], pre_ref[:, inner:])
        last = (i < nt - 1).astype(F32)
        sc_d[tile:, :inner] = silu_bwd(dxsn_ref[0:halo, :], pren_ref[0:halo, :inner]) * last
        sc_d[tile:, inner:] = silu_bwd(dbcn_ref[0:halo, :], pren_ref[0:halo, inner:]) * last
        sc[0:halo, :] = xp_ref[tile - halo:, :] * (i > 0).astype(F32)
        sc[halo:, :] = x_ref[...]
        dpre = sc_d[0:tile, :]
        dx_ref[...] = _taps_bwd_x(sc_d, sw_ref, SSM_CONV, tile, jnp.zeros((tile, w), F32)).astype(dx_ref.dtype)
        _taps_bwd_w(dpre, sc, dsw_ref, SSM_CONV, halo, tile, i)
        _acc(dsb_ref, i, jnp.sum(dpre, axis=0, keepdims=True))

    return pl.pallas_call(
        body, name=name, grid=(nt,),
        in_specs=[_rs(tile, inner), _rs(tile, inner, 0, 1, nt), _rs(tile, w - inner), _rs(tile, w - inner, 0, 1, nt),
                  _rs(tile, w), _rs(tile, w, 0, 1, nt), _rs(tile, w, 2), _rs(tile, w, 2, -1), _ps(sw.shape), ANY],
        out_specs=[_rs(tile, w, 2), _ps(sw.shape), _ps((1, w))],
        out_shape=[S(dproj.shape, dproj.dtype), S(sw.shape, F32), S((1, w), F32)],
        scratch_shapes=[pltpu.VMEM((halo + tile, w), F32), pltpu.VMEM((tile + halo, w), F32)],
        input_output_aliases={9: 0},
        compiler_params=pltpu.CompilerParams(dimension_semantics=("arbitrary",), vmem_limit_bytes=VMEM_LIMIT),
    )(dxs, dxs, dbc, dbc, pre, pre, proj, proj, sw, dproj)


def _ssd_prologue(dt_ref, dtT_ref, al_ref, alc_ref):
    row = _iota((CHUNK, CHUNK), 0)
    col = _iota((CHUNK, CHUNK), 1)
    dt = dt_ref[:, :SSM_HEADS]
    a_row = -jnp.exp(al_ref[:, :SSM_HEADS])
    a_col = -jnp.exp(alc_ref[...])
    cs = _01mm((row >= col).astype(F32), dt * a_row)
    csT = _mm01(dtT_ref[...] * a_col, (row <= col).astype(F32))
    return dt, a_row, cs, csT, row, col


def _decay(cs, csT, h, row, col):
    lm = jnp.exp(jnp.where(row >= col, cs[:, h:h + 1] - csT[h:h + 1, :], -1e30))
    lmT = jnp.exp(jnp.where(col >= row, csT[h:h + 1, :] - cs[:, h:h + 1], -1e30))
    return lm, lmT


def ssd_fwd(xs, bc, dt, dtT, alog_row, alog_col, name):
    n, width = xs.shape
    nc = n // CHUNK
    gw = width // SSM_GROUPS
    hpg = SSM_HEADS // SSM_GROUPS
    ns = SSM_STATE

    def body(xs_ref, bc_ref, dt_ref, dtT_ref, al_ref, alc_ref, y_ref, hs_ref, h_sc):
        i = pl.program_id(0)

        @pl.when(i == 0)
        def _():
            h_sc[...] = jnp.zeros_like(h_sc)

        dt, a_row, cs, csT, row, col = _ssd_prologue(dt_ref, dtT_ref, al_ref, alc_ref)
        indT = _head_indicator(width, SSM_HEADS, transposed=True)
        dt_full = _mm01(dt, indT)
        e_full = jnp.exp(_mm01(cs, indT))
        dte_full = jnp.exp(_mm01(cs[CHUNK - 1:CHUNK, :] - cs, indT))
        xt = xs_ref[...] * dt_full
        hs_ref[0] = h_sc[...]
        lo = _iota((CHUNK, 2 * HEAD_DIM), 1) < HEAD_DIM
        for g in range(SSM_GROUPS):
            bg = bc_ref[:, g * ns:(g + 1) * ns]
            cg = bc_ref[:, (SSM_GROUPS + g) * ns:(SSM_GROUPS + g + 1) * ns]
            gm = _mm(cg, bg, NT)
            hg = h_sc[g * gw:(g + 1) * gw, :]
            yoff = e_full[:, g * gw:(g + 1) * gw] * _mm(cg, hg, NT)
            for pr in range(hpg // 2):
                h0 = g * hpg + 2 * pr
                c0 = h0 * HEAD_DIM
                xp = xt[:, c0:c0 + 2 * HEAD_DIM]
                m0 = gm * _decay(cs, csT, h0, row, col)[0]
                m1 = gm * _decay(cs, csT, h0 + 1, row, col)[0]
                yd = jnp.where(lo, _mm(m0, xp), _mm(m1, xp))
                y_ref[:, c0:c0 + 2 * HEAD_DIM] = yd + yoff[:, 2 * pr * HEAD_DIM:(2 * pr + 2) * HEAD_DIM]
            sg = _mm(xt[:, g * gw:(g + 1) * gw] * dte_full[:, g * gw:(g + 1) * gw], bg, TN)
            for hh in range(hpg):
                h = g * hpg + hh
                r0 = h * HEAD_DIM
                h_sc[r0:r0 + HEAD_DIM, :] = (h_sc[r0:r0 + HEAD_DIM, :] * jnp.exp(csT[h:h + 1, CHUNK - 1:CHUNK])
                                             + sg[hh * HEAD_DIM:(hh + 1) * HEAD_DIM, :])

    bcw = bc.shape[1]
    return _call(body, name, (nc,),
                 [_rs(CHUNK, width), _rs(CHUNK, bcw), _rs(CHUNK, LANE), pl.BlockSpec((SSM_HEADS, CHUNK), lambda i: (0, i)),
                  _ps((1, LANE)), _ps((SSM_HEADS, 1))],
                 [_rs(CHUNK, width), pl.BlockSpec((1, width, ns), lambda i: (i, 0, 0))],
                 [S((n, width), F32), S((nc, width, ns), F32)],
                 scratch=[pltpu.VMEM((width, ns), F32)], sem=("arbitrary",))(xs, bc, dt, dtT, alog_row, alog_col)


def ssd_bwd(xs, bc, dt, dtT, alog_row, alog_col, hs, dy, dxs_skip, name):
    n, width = xs.shape
    nc = n // CHUNK
    gw = width // SSM_GROUPS
    hpg = SSM_HEADS // SSM_GROUPS
    ns = SSM_STATE
    bcw = bc.shape[1]

    def body(xs_ref, bc_ref, dt_ref, dtT_ref, al_ref, alc_ref, hs_ref, dy_ref, skip_ref,
             dxs_ref, dbc_ref, ddtr_ref, dal_ref, ddtb_ref, dh_sc, dxt_sc):
        i = pl.program_id(0)

        @pl.when(i == 0)
        def _():
            dh_sc[...] = jnp.zeros_like(dh_sc)

        dt, a_row, cs, csT, row, col = _ssd_prologue(dt_ref, dtT_ref, al_ref, alc_ref)
        indT = _head_indicator(width, SSM_HEADS, transposed=True)
        ind = _head_indicator(width, SSM_HEADS)
        dt_full = _mm01(dt, indT)
        e_full = jnp.exp(_mm01(cs, indT))
        cs_last = cs[CHUNK - 1:CHUNK, :]
        dte = jnp.exp(cs_last - cs)
        dte_full = _mm01(dte, indT)
        xs_ = xs_ref[...]
        xt = xs_ * dt_full
        dy_ = dy_ref[...]
        hmat = hs_ref[0]
        ds = dh_sc[...]
        lo = _iota((CHUNK, 2 * HEAD_DIM), 1) < HEAD_DIM
        head_lane = _iota((1, SSM_HEADS), 1)
        dcs = jnp.zeros((CHUNK, SSM_HEADS), F32)
        ddte = jnp.zeros((CHUNK, SSM_HEADS), F32)
        for g in range(SSM_GROUPS):
            sl = slice(g * gw, (g + 1) * gw)
            bg = bc_ref[:, g * ns:(g + 1) * ns]
            cg = bc_ref[:, (SSM_GROUPS + g) * ns:(SSM_GROUPS + g + 1) * ns]
            gm = _mm(cg, bg, NT)
            gmT = _mm(bg, cg, NT)
            hg = hmat[sl, :]
            dsg = ds[sl, :]
            dyg = dy_[:, sl]
            xtg = xt[:, sl]
            yoff = e_full[:, sl] * _mm(cg, hg, NT)
            edy = e_full[:, sl] * dyg
            bds = _mm(bg, dsg, NT)
            dxt_g = dte_full[:, sl] * bds
            ddte = ddte + _mm01(xtg * bds, ind[sl, :])
            dcs = dcs + _mm01(dyg * yoff, ind[sl, :])
            db = _mm(xtg * dte_full[:, sl], dsg)
            dc = _mm(edy, hg)
            dhc = _mm(edy, cg, TN)
            dgs = jnp.zeros((CHUNK, CHUNK), F32)
            dgTs = jnp.zeros((CHUNK, CHUNK), F32)
            for pr in range(hpg // 2):
                h0 = g * hpg + 2 * pr
                c0 = 2 * pr * HEAD_DIM
                xp = xtg[:, c0:c0 + 2 * HEAD_DIM]
                dyp = dyg[:, c0:c0 + 2 * HEAD_DIM]
                rr = []
                for h, half in ((h0, lo), (h0 + 1, jnp.logical_not(lo))):
                    lm, lmT = _decay(cs, csT, h, row, col)
                    xm = jnp.where(half, xp, 0.0)
                    dm = _mm(dyp, xm, NT)
                    dmT = _mm(xm, dyp, NT)
                    mT = gmT * lmT
                    z = jnp.sum(dm * (gm * lm), axis=1, keepdims=True) - jnp.sum(dmT * mT, axis=1, keepdims=True)
                    dcs = dcs + z * (head_lane == h).astype(F32)
                    dgs = dgs + dm * lm
                    dgTs = dgTs + dmT * lmT
                    rr.append(_mm(mT, dyp))
                dxt_sc[:, g * gw + c0:g * gw + c0 + 2 * HEAD_DIM] = jnp.where(lo, rr[0], rr[1]) + dxt_g[:, c0:c0 + 2 * HEAD_DIM]
            dbc_ref[:, g * ns:(g + 1) * ns] = db + _mm(dgTs, cg)
            dbc_ref[:, (SSM_GROUPS + g) * ns:(SSM_GROUPS + g + 1) * ns] = dc + _mm(dgs, bg)
            for hh in range(hpg):
                h = g * hpg + hh
                r0 = h * HEAD_DIM
                dh_sc[r0:r0 + HEAD_DIM, :] = (dhc[hh * HEAD_DIM:(hh + 1) * HEAD_DIM, :]
                                              + jnp.exp(csT[h:h + 1, CHUNK - 1:CHUNK]) * ds[r0:r0 + HEAD_DIM, :])
        t = ddte * dte
        per_head = jnp.sum(jnp.sum(ds * hmat, axis=1, keepdims=True) * ind, axis=0, keepdims=True)
        last_add = jnp.sum(t, axis=0, keepdims=True) + jnp.exp(cs_last) * per_head
        dcs = dcs - t + jnp.where(_iota((CHUNK, SSM_HEADS), 0) == CHUNK - 1, last_add, 0.0)
        dadt = _01mm((row <= col).astype(F32), dcs)
        dxt = dxt_sc[...]
        ddt = dadt * a_row + _mm01(dxt * xs_, ind)
        dxs_ref[...] = dxt * dt_full + skip_ref[...]
        ddtr = ddt * (1.0 - jnp.exp(-dt))
        ddtr_ref[...] = jnp.zeros_like(ddtr_ref)
        ddtr_ref[:, :SSM_HEADS] = ddtr.astype(ddtr_ref.dtype)
        _acc(dal_ref, i, jnp.sum(dadt * dt, axis=0, keepdims=True) * a_row)
        _acc(ddtb_ref, i, jnp.sum(ddtr, axis=0, keepdims=True))

    rev = lambda i: (nc - 1 - i, 0)
    return _call(body, name, (nc,),
                 [pl.BlockSpec((CHUNK, width), rev), pl.BlockSpec((CHUNK, bcw), rev), pl.BlockSpec((CHUNK, LANE), rev),
                  pl.BlockSpec((SSM_HEADS, CHUNK), lambda i: (0, nc - 1 - i)), _ps((1, LANE)), _ps((SSM_HEADS, 1)),
                  pl.BlockSpec((1, width, ns), lambda i: (nc - 1 - i, 0, 0)), pl.BlockSpec((CHUNK, width), rev),
                  pl.BlockSpec((CHUNK, width), rev)],
                 [pl.BlockSpec((CHUNK, width), rev), pl.BlockSpec((CHUNK, bcw), rev), pl.BlockSpec((CHUNK, LANE), rev),
                  _ps((1, SSM_HEADS)), _ps((1, SSM_HEADS))],
                 [S((n, width), F32), S((n, bcw), F32), S((n, LANE), MXU_DTYPE), S((1, SSM_HEADS), F32), S((1, SSM_HEADS), F32)],
                 scratch=[pltpu.VMEM((width, ns), F32), pltpu.VMEM((CHUNK, width), F32)],
                 sem=("arbitrary",))(xs, bc, dt, dtT, alog_row, alog_col, hs, dy, dxs_skip)


def ssm_gate_fwd(yssd, xs, proj, dfull, gamma, name):
    n, d = yssd.shape
    tile = _row_tile(n, 256)
    gw = d // SSM_GROUPS

    def body(y_ref, xs_ref, z_ref, df_ref, gm_ref, o_ref):
        z = z_ref[...]
        y2 = (y_ref[...] + df_ref[...] * xs_ref[...]) * (z * _sigmoid(z))
        for g in range(SSM_GROUPS):
            yg = y2[:, g * gw:(g + 1) * gw]
            r = lax.rsqrt(jnp.mean(yg * yg, axis=-1, keepdims=True) + EPS)
            o_ref[:, g * gw:(g + 1) * gw] = (yg * r * gm_ref[:, g * gw:(g + 1) * gw]).astype(o_ref.dtype)

    return _call(body, name, (n // tile,), [_rs(tile, d), _rs(tile, d), _rs(tile, d, 2), _ps((1, d)), _ps((1, d))],
                 _rs(tile, d), S((n, d), MXU_DTYPE), sem=("parallel",))(yssd, xs, proj, dfull, gamma)


def ssm_gate_bwd(dy3, yssd, xs, proj, dfull, gamma, dproj, name):
    n, d = yssd.shape
    tile = _row_tile(n, 256)
    gw = d // SSM_GROUPS

    def body(dy_ref, y_ref, xs_ref, z_ref, df_ref, gm_ref, dp_in_ref, dys_ref, dxs_ref, dz_ref, dgm_ref, dd_ref):
        i = pl.program_id(0)
        z = z_ref[...]
        s = _sigmoid(z)
        xs_ = xs_ref[...]
        y1 = y_ref[...] + df_ref[...] * xs_
        y2 = y1 * (z * s)
        dy_ = dy_ref[...].astype(F32)
        dgm = []
        dy2 = []
        for g in range(SSM_GROUPS):
            sl = slice(g * gw, (g + 1) * gw)
            dxg, dgg = _rms_bwd_math(y2[:, sl], gm_ref[:, sl], dy_[:, sl])
            dy2.append(dxg)
            dgm.append(dgg)
        dy2 = jnp.concatenate(dy2, axis=1)
        dy1 = dy2 * (z * s)
        dys_ref[...] = dy1
        dxs_ref[...] = dy1 * df_ref[...]
        dz_ref[...] = (dy2 * y1 * s * (1.0 + z * (1.0 - s))).astype(dz_ref.dtype)
        _acc(dgm_ref, i, jnp.concatenate(dgm, axis=1))
        colsum = jnp.broadcast_to(jnp.sum(dy1 * xs_, axis=0, keepdims=True), (8, d))
        _acc(dd_ref, i, _mm01(colsum, _head_indicator(d, SSM_HEADS))[0:1, :])

    return pl.pallas_call(
        body, name=name, grid=(n // tile,),
        in_specs=[_rs(tile, d), _rs(tile, d), _rs(tile, d), _rs(tile, d, 2), _ps((1, d)), _ps((1, d)), ANY],
        out_specs=[_rs(tile, d), _rs(tile, d), _rs(tile, d, 2), _ps((1, d)), _ps((1, SSM_HEADS))],
        out_shape=[S((n, d), F32), S((n, d), F32), S(dproj.shape, dproj.dtype), S((1, d), F32), S((1, SSM_HEADS), F32)],
        input_output_aliases={6: 2},
        compiler_params=pltpu.CompilerParams(dimension_semantics=("arbitrary",), vmem_limit_bytes=VMEM_LIMIT),
    )(dy3, yssd, xs, proj, dfull, gamma, dproj)


def _rope128(x, cos, sin_signed):
    half = HEAD_DIM // 2
    lane = _iota(x.shape, 1)
    partner = jnp.where((lane % HEAD_DIM) < half, pltpu.roll(x, LANE - half, 1), pltpu.roll(x, half, 1))
    return x * cos + partner * sin_signed


def rope_fwd(qkv, cos, sin, name):
    n, w = qkv.shape
    qw = ATT_HEADS * HEAD_DIM
    kw = ATT_KV_HEADS * HEAD_DIM
    tile = _row_tile(n, 256)

    def body(x_ref, c_ref, s_ref, q_ref, k_ref, v_ref):
        c, s = c_ref[...], s_ref[...]
        for j in range(qw // LANE):
            q_ref[:, j * LANE:(j + 1) * LANE] = _rope128(x_ref[:, j * LANE:(j + 1) * LANE], c, s).astype(q_ref.dtype)
        for j in range(kw // LANE):
            k_ref[:, j * LANE:(j + 1) * LANE] = _rope128(x_ref[:, qw + j * LANE:qw + (j + 1) * LANE], c, s).astype(k_ref.dtype)
        v_ref[...] = x_ref[:, qw + kw:].astype(v_ref.dtype)

    return _call(body, name, (n // tile,), [_rs(tile, w), _rs(tile, LANE), _rs(tile, LANE)],
                 [_rs(tile, qw), _rs(tile, kw), _rs(tile, kw)],
                 [S((n, qw), MXU_DTYPE), S((n, kw), MXU_DTYPE), S((n, kw), MXU_DTYPE)], sem=("parallel",))(qkv, cos, sin)


ATT_GROUP = ATT_HEADS // ATT_KV_HEADS


def _attn_mask(i):
    row = _iota((ATT_GROUP * WINDOW, 2 * WINDOW), 0) % WINDOW
    s = _iota((ATT_GROUP * WINDOW, 2 * WINDOW), 1)
    return (s > row) & (s <= row + WINDOW) & ((s >= WINDOW) | (i > 0))


def _stack_heads(ref, j, kh, lo):
    parts = []
    for t in range(ATT_GROUP):
        h = ATT_GROUP * j + t
        blk = ref[:, (h // 2) * LANE:(h // 2 + 1) * LANE]
        blk = jnp.where(lo if h % 2 == 0 else jnp.logical_not(lo), blk, jnp.zeros_like(blk))
        parts.append(blk if h % 2 == kh else pltpu.roll(blk, HEAD_DIM, 1))
    return jnp.concatenate(parts, axis=0)


def _unstack_heads(stacked, j, kh, lo, put):
    for t in range(0, ATT_GROUP, 2):
        h = ATT_GROUP * j + t
        even = stacked[t * WINDOW:(t + 1) * WINDOW, :]
        odd = stacked[(t + 1) * WINDOW:(t + 2) * WINDOW, :]
        even = even if kh == 0 else pltpu.roll(even, HEAD_DIM, 1)
        odd = odd if kh == 1 else pltpu.roll(odd, HEAD_DIM, 1)
        put(h // 2, jnp.where(lo, even, odd))


def _per_head_rows(ref, j):
    return jnp.concatenate([ref[:, ATT_GROUP * j + t:ATT_GROUP * j + t + 1] for t in range(ATT_GROUP)], axis=0)


def _per_head_scalar(ref, j):
    rows = _iota((ATT_GROUP * WINDOW, 1), 0) // WINDOW
    out = jnp.zeros((ATT_GROUP * WINDOW, 1), F32)
    for t in range(ATT_GROUP):
        out = out + jnp.where(rows == t, ref[:, ATT_GROUP * j + t:ATT_GROUP * j + t + 1], 0.0)
    return out


def attn_fwd(q, k, v, sinks, name):
    n, qw = q.shape
    kw = k.shape[1]
    nb = n // WINDOW
    scale = HEAD_DIM ** -0.5

    def body(q_ref, kc_ref, kp_ref, vc_ref, vp_ref, sk_ref, o_ref, lse_ref):
        i = pl.program_id(0)
        valid = _attn_mask(i)
        lo = _iota((WINDOW, LANE), 1) < HEAD_DIM
        k2 = jnp.concatenate([kp_ref[...], kc_ref[...]], axis=0)
        v2 = jnp.concatenate([vp_ref[...], vc_ref[...]], axis=0)
        lane1 = _iota((1, LANE), 1)
        lse = jnp.zeros((WINDOW, LANE), F32)

        def put_o(qb, val):
            o_ref[:, qb * LANE:(qb + 1) * LANE] = val.astype(o_ref.dtype)

        for j in range(ATT_KV_HEADS):
            kb, kh = j // 2, j % 2
            q4 = _stack_heads(q_ref, j, kh, lo)
            logits = jnp.where(valid, _mm(q4, k2[:, kb * LANE:(kb + 1) * LANE], NT) * scale, -1e30)
            sk = _per_head_scalar(sk_ref, j)
            m = jnp.maximum(jnp.max(logits, axis=-1, keepdims=True), sk)
            e = jnp.exp(logits - m)
            den = jnp.sum(e, axis=-1, keepdims=True) + jnp.exp(sk - m)
            lse4 = m + jnp.log(den)
            for t in range(ATT_GROUP):
                lse = lse + lse4[t * WINDOW:(t + 1) * WINDOW, :] * (lane1 == ATT_GROUP * j + t).astype(F32)
            _unstack_heads(_mm(e * (1.0 / den), v2[:, kb * LANE:(kb + 1) * LANE]), j, kh, lo, put_o)
        lse_ref[...] = lse

    return _call(body, name, (nb,),
                 [_rs(WINDOW, qw), _rs(WINDOW, kw), _rs(WINDOW, kw, 0, -1), _rs(WINDOW, kw), _rs(WINDOW, kw, 0, -1), _ps((1, LANE))],
                 [_rs(WINDOW, qw), _rs(WINDOW, LANE)], [S((n, qw), MXU_DTYPE), S((n, LANE), F32)],
                 sem=("parallel",))(q, k, k, v, v, sinks)


def attn_bwd(q, k, v, o, do, lse, sinks, name):
    n, qw = q.shape
    kw = k.shape[1]
    nb = n // WINDOW
    scale = HEAD_DIM ** -0.5

    def body(q_ref, kc_ref, kp_ref, vc_ref, vp_ref, o_ref, do_ref, lse_ref, sk_ref,
             dq_ref, dka_ref, dkb_ref, dva_ref, dvb_ref, dsk_ref):
        i = pl.program_id(0)
        valid = _attn_mask(i)
        lo = _iota((WINDOW, LANE), 1) < HEAD_DIM
        k2 = jnp.concatenate([kp_ref[...], kc_ref[...]], axis=0)
        v2 = jnp.concatenate([vp_ref[...], vc_ref[...]], axis=0)
        lane1 = _iota((1, LANE), 1)
        do_ = do_ref[...].astype(F32)
        delta = _mm01(do_ * o_ref[...].astype(F32), _head_indicator(qw, ATT_HEADS))
        dk2 = [jnp.zeros((2 * WINDOW, LANE), F32) for _ in range(kw // LANE)]
        dv2 = [jnp.zeros((2 * WINDOW, LANE), F32) for _ in range(kw // LANE)]
        dsk = jnp.zeros((1, LANE), F32)

        def put_dq(qb, val):
            dq_ref[:, qb * LANE:(qb + 1) * LANE] = val

        for j in range(ATT_KV_HEADS):
            kb, kh = j // 2, j % 2
            q4 = _stack_heads(q_ref, j, kh, lo)
            do4 = _stack_heads(do_ref, j, kh, lo)
            kk = k2[:, kb * LANE:(kb + 1) * LANE]
            vv = v2[:, kb * LANE:(kb + 1) * LANE]
            logits = jnp.where(valid, _mm(q4, kk, NT) * scale, -1e30)
            lse4 = _per_head_rows(lse_ref, j)
            p = jnp.exp(logits - lse4)
            dl = jnp.concatenate([delta[:, ATT_GROUP * j + t:ATT_GROUP * j + t + 1] for t in range(ATT_GROUP)], axis=0)
            ds = p * (_mm(do4, vv, NT) - dl) * scale
            sd = jnp.exp(_per_head_scalar(sk_ref, j) - lse4) * dl
            for t in range(ATT_GROUP):
                dsk = dsk - (jnp.sum(sd[t * WINDOW:(t + 1) * WINDOW, :], axis=0, keepdims=True)
                             * (lane1 == ATT_GROUP * j + t).astype(F32))
            _unstack_heads(_mm(ds, kk), j, kh, lo, put_dq)
            dk2[kb] = dk2[kb] + _mm(ds, q4, TN)
            dv2[kb] = dv2[kb] + _mm(p, do4, TN)
        for kb in range(kw // LANE):
            dkb_ref[:, kb * LANE:(kb + 1) * LANE] = dk2[kb][0:WINDOW, :]
            dka_ref[:, kb * LANE:(kb + 1) * LANE] = dk2[kb][WINDOW:, :]
            dvb_ref[:, kb * LANE:(kb + 1) * LANE] = dv2[kb][0:WINDOW, :]
            dva_ref[:, kb * LANE:(kb + 1) * LANE] = dv2[kb][WINDOW:, :]
        _acc(dsk_ref, i, dsk)

    return _call(body, name, (nb,),
                 [_rs(WINDOW, qw), _rs(WINDOW, kw), _rs(WINDOW, kw, 0, -1), _rs(WINDOW, kw), _rs(WINDOW, kw, 0, -1),
                  _rs(WINDOW, qw), _rs(WINDOW, qw), _rs(WINDOW, LANE), _ps((1, LANE))],
                 [_rs(WINDOW, qw)] + [_rs(WINDOW, kw)] * 4 + [_ps((1, LANE))],
                 [S((n, qw), F32)] + [S((n, kw), F32)] * 4 + [S((1, LANE), F32)],
                 sem=("arbitrary",))(q, k, k, v, v, o, do, lse, sinks)


def attn_grad_merge(dq, dka, dkb, dva, dvb, cos, sin, name):
    n, qw = dq.shape
    kw = dka.shape[1]
    nb = n // WINDOW
    w = qw + 2 * kw

    def body(dq_ref, dka_ref, dkb_ref, dva_ref, dvb_ref, c_ref, s_ref, o_ref, db_ref):
        i = pl.program_id(0)
        c, s = c_ref[...], -s_ref[...]
        nxt = (i < nb - 1).astype(F32)

        @pl.when(i == 0)
        def _():
            db_ref[...] = jnp.zeros_like(db_ref)

        def put(c0, val):
            o_ref[:, c0:c0 + val.shape[1]] = val.astype(o_ref.dtype)
            db_ref[:, c0:c0 + val.shape[1]] += jnp.sum(val, axis=0, keepdims=True)

        for j in range(qw // LANE):
            put(j * LANE, _rope128(dq_ref[:, j * LANE:(j + 1) * LANE], c, s))
        for j in range(kw // LANE):
            sl = slice(j * LANE, (j + 1) * LANE)
            put(qw + j * LANE, _rope128(dka_ref[:, sl] + dkb_ref[:, sl] * nxt, c, s))
        put(qw + kw, dva_ref[...] + dvb_ref[...] * nxt)

    return _call(body, name, (nb,),
                 [_rs(WINDOW, qw), _rs(WINDOW, kw), _rs(WINDOW, kw, 0, 1, nb), _rs(WINDOW, kw), _rs(WINDOW, kw, 0, 1, nb),
                  _rs(WINDOW, LANE), _rs(WINDOW, LANE)],
                 [_rs(WINDOW, w), _ps((1, w))], [S((n, w), MXU_DTYPE), S((1, w), F32)],
                 sem=("arbitrary",))(dq, dka, dkb, dva, dvb, cos, sin)


def _row(v):
    return v.reshape(1, -1)


def _pad_lanes(v, width=LANE):
    return jnp.pad(v.reshape(1, -1), ((0, 0), (0, width - v.size)))


def ffn_fwd(h, g, w_in, w_out, tag):
    xn, u, hm = swiglu_in(h, _row(g), w_in, f"{tag}_in")
    return matmul(hm, w_out, "nn", f"{tag}_out", scale=0.5, res=h), (h, xn, u, hm)


class GradSink:
    ORDER = ("ffn1_w_out", "ffn2_w_out", "ple_gate_w", "att_w_o", "hyb_w_out", "ffn1_w_in", "ffn2_w_in", "att_w_qkv",
             "ple_proj_w", "hyb_w_in")

    def __init__(self, shard_shapes):
        self.where, rows = {}, {}
        for k in self.ORDER:
            depth, r, c = shard_shapes[k]
            offs = [rows.get(c, 0) + i * r for i in range(depth)]
            rows[c] = offs[-1] + r
            self.where[k] = (c, "r" if SHARD_AXIS[k] == 1 else "c", offs, r)
        self.bufs = {c: lax.empty((N_CHIPS, r, c), F32) for c, r in rows.items()}

    def mm(self, k, layer, a, b, name, scale=None, c0=0, paired=False):
        c, kind, offs, r = self.where[k]
        slot = Slot(self.bufs[c], kind, r if kind == "r" else c, offs[layer], c0, paired)
        self.bufs[c] = matmul(a, b, "tn", name, scale=scale, into=slot)

    def put(self, k, chip_major):
        self.bufs[self.where[k][0]] = chip_major

    def take(self, k, reduced):
        c, _, offs, r = self.where[k]
        return reduced[c][offs[0]:offs[0] + len(offs) * r].reshape(len(offs), r, c)


def ffn_bwd(dh, g, w_in, w_out, saved, tag, sink, keys, layer, colsum=False):
    h, xn, u, hm = saved
    sink.mm(keys[1], layer, hm, dh, f"{tag}_dwout", scale=0.5)
    du = swiglu_out_bwd(dh, w_out, u, f"{tag}_dhm")
    sink.mm(keys[0], layer, xn, du, f"{tag}_dwin", paired=True)
    dxn = matmul(du, ColSharded(w_in.arr, paired=True), "nt", f"{tag}_dxn")
    outs = rms_bwd(h, _row(g), dxn, dh, f"{tag}_drms", colsum=colsum)
    return (outs[0], outs[1].reshape(-1)) + ((outs[2],) if colsum else ())


def _hyb_params(w):
    d = w["conv_dw_b"].size
    inner = SSM_HEADS * HEAD_DIM
    main = 3 * d + w["ssm_conv_b"].size
    return dict(
        w_main=w["hyb_w_in"][:, :main], w_dt=jnp.pad(w["hyb_w_in"][:, main:], ((0, 0), (0, LANE - SSM_HEADS))),
        cw=jnp.pad(w["conv_dw_w"], ((0, 32 - CONV_WIDTH), (0, 0))), cb=_row(w["conv_dw_b"]),
        lg=_row(w["conv_ln_g"]), lb=_row(w["conv_ln_b"]),
        sw=jnp.pad(w["ssm_conv_w"], ((0, 8 - SSM_CONV), (0, 0))), sb=_row(w["ssm_conv_b"]),
        dtb=_pad_lanes(w["ssm_dt_bias"]), al_row=_pad_lanes(w["ssm_a_log"]), al_col=w["ssm_a_log"].reshape(-1, 1),
        dfull=_row(jnp.repeat(w["ssm_d"], HEAD_DIM)), gamma=_row(w["ssm_norm"]),
        wo_top=w["hyb_w_out"][:d], wo_bot=w["hyb_w_out"][d:], d=d, inner=inner, main=main)


def hyb_fwd(h, w, tag):
    q = _hyb_params(w)
    xn = rms_fwd(h, _row(w["norm_mix"]), f"{tag}_rms")
    proj = matmul(xn, q["w_main"], "nn", f"{tag}_in")
    dtr = matmul(xn, q["w_dt"], "nn", f"{tag}_in_dt")
    u, u1 = conv_group_fwd(proj, q["cw"], q["cb"], q["lg"], q["lb"], f"{tag}_conv")
    pre, xs, bc, dt = ssm_conv_fwd(proj, dtr, q["sw"], q["sb"], q["dtb"], f"{tag}_sconv")
    dtT = dt[:, :SSM_HEADS].T
    yssd, hs = ssd_fwd(xs, bc, dt, dtT, q["al_row"], q["al_col"], f"{tag}_ssd")
    y = ssm_gate_fwd(yssd, xs, proj, q["dfull"], q["gamma"], f"{tag}_gate")
    h2 = matmul(u, q["wo_top"], "nn", f"{tag}_out_a", res=h)
    h2 = matmul(y, q["wo_bot"], "nn", f"{tag}_out_b", res=h2)
    return h2, (h, xn, proj, u, u1, pre, xs, bc, dt, dtT, yssd, hs, y)


def hyb_bwd(dh, w, saved, tag, sink, layer):
    q = _hyb_params(w)
    h, xn, proj, u, u1, pre, xs, bc, dt, dtT, yssd, hs, y = saved
    du = matmul(dh, q["wo_top"], "nt", f"{tag}_du")
    dy3 = matmul(dh, q["wo_bot"], "nt", f"{tag}_dy")
    sink.mm("hyb_w_out", layer, u, dh, f"{tag}_dwo_a", c0=0)
    sink.mm("hyb_w_out", layer, y, dh, f"{tag}_dwo_b", c0=N_CHIPS // 2)
    dproj, dcw, dcb, dlg, dlb = conv_group_bwd(du, u1, proj, q["cw"], q["lg"], q["lb"], f"{tag}_dconv")
    dyssd, dxs_skip, dproj, dgamma, dd = ssm_gate_bwd(dy3, yssd, xs, proj, q["dfull"], q["gamma"], dproj, f"{tag}_dgate")
    dxs, dbc, ddtr, dalog, ddtb = ssd_bwd(xs, bc, dt, dtT, q["al_row"], q["al_col"], hs, dyssd, dxs_skip, f"{tag}_dssd")
    dproj, dsw, dsb = ssm_conv_bwd(dxs, dbc, pre, proj, q["sw"], dproj, f"{tag}_dsconv")
    dw_in = jnp.concatenate([matmul(xn, dproj, "tn", f"{tag}_dwin"),
                             matmul(xn, ddtr, "tn", f"{tag}_dwin_dt")[:, :SSM_HEADS]], axis=1)
    sink.put("hyb_w_in", _to_chip_major(dw_in, 1))
    dxn = matmul(dproj, q["w_main"], "nt", f"{tag}_dxn")
    dxn = matmul(ddtr, q["w_dt"], "nt", f"{tag}_dxn_dt", res=dxn)
    dh2, dg = rms_bwd(h, _row(w["norm_mix"]), dxn, dh, f"{tag}_drms")
    grads = dict(norm_mix=dg.reshape(-1), conv_dw_w=dcw[:CONV_WIDTH], conv_dw_b=dcb.reshape(-1),
                 conv_ln_g=dlg.reshape(-1), conv_ln_b=dlb.reshape(-1), ssm_conv_w=dsw[:SSM_CONV], ssm_conv_b=dsb.reshape(-1),
                 ssm_dt_bias=ddtb.reshape(-1), ssm_a_log=dalog.reshape(-1), ssm_d=dd.reshape(-1), ssm_norm=dgamma.reshape(-1))
    return dh2, grads


def rope_tables(n):
    half = HEAD_DIM // 2
    inv = ROPE_THETA ** (-jnp.arange(0, HEAD_DIM, 2, dtype=F32) / HEAD_DIM)
    ang = jnp.arange(n, dtype=F32)[:, None] * inv[None, :]
    cos, sin = jnp.cos(ang), jnp.sin(ang)
    reps = LANE // HEAD_DIM
    return jnp.tile(jnp.concatenate([cos, cos], axis=1), (1, reps)), jnp.tile(jnp.concatenate([-sin, sin], axis=1), (1, reps))


def att_fwd(h, w, tables, tag):
    cos, sin = tables
    xn = rms_fwd(h, _row(w["norm_mix"]), f"{tag}_rms")
    qkv = matmul(xn, w["att_w_qkv"], "nn", f"{tag}_qkv", bias=_row(w["att_b_qkv"]))
    q, k, v = rope_fwd(qkv, cos, sin, f"{tag}_rope")
    sinks = _pad_lanes(w["att_sinks"])
    o, lse = attn_fwd(q, k, v, sinks, f"{tag}_attn")
    h2 = matmul(o, w["att_w_o"], "nn", f"{tag}_o", bias=_row(w["att_b_o"]), res=h)
    return h2, (h, xn, q, k, v, o, lse, sinks)


def att_bwd(dh, dh_colsum, w, saved, tables, tag, sink, layer):
    cos, sin = tables
    h, xn, q, k, v, o, lse, sinks = saved
    do = matmul(dh, w["att_w_o"], "nt", f"{tag}_do")
    sink.mm("att_w_o", layer, o, dh, f"{tag}_dwo")
    dq, dka, dkb, dva, dvb, dsk = attn_bwd(q, k, v, o, do, lse, sinks, f"{tag}_dattn")
    dqkv, dbqkv = attn_grad_merge(dq, dka, dkb, dva, dvb, cos, sin, f"{tag}_drope")
    sink.mm("att_w_qkv", layer, xn, dqkv, f"{tag}_dwqkv")
    dxn = matmul(dqkv, w["att_w_qkv"], "nt", f"{tag}_dxn")
    dh2, dg = rms_bwd(h, _row(w["norm_mix"]), dxn, dh, f"{tag}_drms")
    grads = dict(norm_mix=dg.reshape(-1), att_b_qkv=dbqkv.reshape(-1), att_sinks=dsk[0, :ATT_HEADS],
                 att_b_o=dh_colsum.reshape(-1))
    return dh2, grads


def ple_block_fwd(h, pe, w, tag):
    xn = rms_fwd(h, _row(w["ple_norm"]), f"{tag}_rms")
    gl = matmul(xn, w["ple_gate_w"], "nn", f"{tag}_gate")
    pp = matmul(pe, w["ple_proj_w"], "nn", f"{tag}_proj")
    return ple_fwd(h, gl, pp, f"{tag}_mix"), (h, xn, gl, pp, pe)


def ple_block_bwd(dh, w, saved, tag, sink, layer):
    h, xn, gl, pp, pe = saved
    dpp, dgl = ple_bwd(dh, gl, pp, f"{tag}_dmix")
    sink.mm("ple_proj_w", layer, pe, dpp, f"{tag}_dwp")
    sink.mm("ple_gate_w", layer, xn, dgl, f"{tag}_dwg")
    dxn = matmul(dgl, w["ple_gate_w"], "nt", f"{tag}_dxn")
    dh2, dg = rms_bwd(h, _row(w["ple_norm"]), dxn, dh, f"{tag}_drms")
    return dh2, dict(ple_norm=dg.reshape(-1))


PER_LAYER = ("norm_ffn1", "ffn1_w_in", "ffn1_w_out", "norm_mix", "norm_ffn2", "ffn2_w_in", "ffn2_w_out",
             "ple_norm", "ple_gate_w", "ple_proj_w")
EVEN_ONLY = ("hyb_w_in", "conv_dw_w", "conv_dw_b", "conv_ln_g", "conv_ln_b", "ssm_conv_w", "ssm_conv_b",
             "ssm_dt_bias", "ssm_a_log", "ssm_d", "ssm_norm", "hyb_w_out")
ODD_ONLY = ("att_w_qkv", "att_b_qkv", "att_sinks", "att_w_o", "att_b_o")


def _layer_index(k, i):
    if k in PER_LAYER:
        return i
    if k in (EVEN_ONLY if i % 2 == 0 else ODD_ONLY):
        return i // 2
    return None


def _stage(k):
    return 0 if k.startswith("ffn1") else (2 if k.startswith(("ffn2", "ple")) else 1)


def trunk_fwd_bwd(x, pe, target, layers, final_norm, sink, fetch):
    depth = len(layers)
    tables = rope_tables(x.shape[0])
    h = x
    saved = []
    for i, w in enumerate(layers):
        w.update(fetch(i, 0, h))
        h, s1 = ffn_fwd(h, w["norm_ffn1"], w["ffn1_w_in"], w["ffn1_w_out"], f"l{i}_ffn1")
        w.update(fetch(i, 1, h))
        if i % 2 == 0:
            h, s2 = hyb_fwd(h, w, f"l{i}_hyb")
        else:
            h, s2 = att_fwd(h, w, tables, f"l{i}_att")
        w.update(fetch(i, 2, h))
        h, s3 = ffn_fwd(h, w["norm_ffn2"], w["ffn2_w_in"], w["ffn2_w_out"], f"l{i}_ffn2")
        h, s4 = ple_block_fwd(h, pe[i], w, f"l{i}_ple")
        saved.append((s1, s2, s3, s4))
    dh, dgf, loss = loss_head(h, _row(final_norm), target, "loss_head")
    grads = {}
    for i in reversed(range(depth)):
        w = layers[i]
        s1, s2, s3, s4 = saved[i]
        dh, g = ple_block_bwd(dh, w, s4, f"l{i}_ple", sink, i)
        odd = i % 2 == 1
        out = ffn_bwd(dh, w["norm_ffn2"], w["ffn2_w_in"], w["ffn2_w_out"], s3, f"l{i}_ffn2", sink,
                      ("ffn2_w_in", "ffn2_w_out"), i, colsum=odd)
        dh = out[0]
        g.update(norm_ffn2=out[1])
        if odd:
            dh, gm = att_bwd(dh, out[2], w, s2, tables, f"l{i}_att", sink, i // 2)
        else:
            dh, gm = hyb_bwd(dh, w, s2, f"l{i}_hyb", sink, i // 2)
        g.update(gm)
        out = ffn_bwd(dh, w["norm_ffn1"], w["ffn1_w_in"], w["ffn1_w_out"], s1, f"l{i}_ffn1", sink,
                      ("ffn1_w_in", "ffn1_w_out"), i)
        dh = out[0]
        g.update(norm_ffn1=out[1])
        for k, v in g.items():
            grads.setdefault(k, []).insert(0, v)
    grads = {k: jnp.stack(v) for k, v in grads.items()}
    grads["final_norm"] = dgf.reshape(-1)
    return loss, dh, grads


def _me():
    return lax.axis_index("x"), lax.axis_index("y"), lax.axis_index("c")


def _flip(v, f):
    return 1 - v if f else v


def _remote(src, dst, send_sems, recv_sems, k, dev):
    return pltpu.make_async_remote_copy(src_ref=src, dst_ref=dst, send_sem=send_sems.at[k], recv_sem=recv_sems.at[k],
                                        device_id=dev, device_id_type=MESH)


CHIP_FLIPS = ((1, 0), (0, 1), (1, 1))
DEV_FLIPS = tuple((fx, fy, fc) for fx in (0, 1) for fy in (0, 1) for fc in (0, 1))[1:]


def all_gather_chips(xs, name):
    na = len(xs)
    halves = [x.shape[0] // 2 for x in xs]
    assert all(x.shape[0] % 2 == 0 for x in xs)

    def body(*refs):
        x_refs, out_refs = refs[:na], refs[na:2 * na]
        send_sems, recv_sems = refs[2 * na:]
        mx, my, mc = _me()
        chip = 2 * mx + my
        sib = (mx, my, 1 - mc)
        peers = [(_flip(mx, fx), _flip(my, fy)) for fx, fy in CHIP_FLIPS]

        def rows(a, ch, hc):
            return out_refs[a].at[ch, pl.ds(hc * halves[a], halves[a]), :]

        def src(a):
            return x_refs[a].at[pl.ds(mc * halves[a], halves[a]), :]

        first = [_remote(src(a), rows(a, chip, mc), send_sems, recv_sems, 6 * a + j, (px, py, mc))
                 for j, (px, py) in enumerate(peers) for a in range(na)]
        for cp in first:
            cp.start()
        passed = []
        for j, (px, py) in enumerate(peers):
            for a in range(na):
                landed = rows(a, 2 * px + py, mc)
                _remote(src(a), landed, send_sems, recv_sems, 6 * a + j, (px, py, mc)).wait_recv()
                fw = _remote(landed, landed, send_sems, recv_sems, 6 * a + 3 + j, sib)
                fw.start()
                passed.append(fw)
        for j, (px, py) in enumerate(peers):
            for a in range(na):
                _remote(src(a), rows(a, 2 * px + py, 1 - mc), send_sems, recv_sems, 6 * a + 3 + j, sib).wait_recv()
        for cp in first + passed:
            cp.wait_send()

    outs = pl.pallas_call(
        body, name=name, out_shape=[S((N_CHIPS,) + x.shape, x.dtype) for x in xs], in_specs=[ANY] * na, out_specs=[ANY] * na,
        scratch_shapes=[pltpu.SemaphoreType.DMA((6 * na,)), pltpu.SemaphoreType.DMA((6 * na,))])(*xs)
    chip = 2 * lax.axis_index("x") + lax.axis_index("y")
    return [lax.dynamic_update_slice_in_dim(o, x[None], chip, axis=0) for o, x in zip(outs, xs)]


HBM = pl.BlockSpec(memory_space=pltpu.HBM)
SEM = pl.BlockSpec(memory_space=pltpu.SEMAPHORE)
DATAFLOW = pltpu.SideEffectType.DATAFLOW_SIDE_EFFECTING


def gather_start(xs, lands, after, name):
    na = len(xs)

    def body(*refs):
        x_refs, land_refs = refs[:na], refs[na:2 * na]
        send_sems, recv_sems = refs[2 * na + 1], refs[2 * na + 2]
        token = refs[-1]
        mx, my, mc = _me()
        chip = 2 * mx + my
        for a in range(na):
            for j, (fx, fy) in enumerate(CHIP_FLIPS):
                _remote(x_refs[a], land_refs[a].at[chip], send_sems, recv_sems, 3 * a + j,
                        (_flip(mx, fx), _flip(my, fy), mc)).start()
        token[...] = jnp.zeros_like(token)

    outs = pl.pallas_call(
        body, name=name,
        out_shape=(pltpu.SemaphoreType.DMA((3 * na,)), pltpu.SemaphoreType.DMA((3 * na,)))
        + tuple(pltpu.HBM(x.shape, x.dtype) for x in xs) + tuple(pltpu.HBM(l.shape, l.dtype) for l in lands)
        + (S((8, LANE), F32),),
        in_specs=[HBM] * (2 * na) + [pl.BlockSpec(memory_space=pl.ANY)],
        out_specs=(SEM, SEM) + (HBM,) * (2 * na) + (pl.BlockSpec(memory_space=pltpu.VMEM),),
        input_output_aliases={a: 2 + a for a in range(2 * na)},
        compiler_params=pltpu.CompilerParams(has_side_effects=DATAFLOW),
    )(*[pltpu.with_memory_space_constraint(t, pltpu.HBM) for t in list(xs) + list(lands)], after)
    return outs[0], outs[1], list(outs[2:2 + na]), list(outs[2 + na:2 + 2 * na])


def gather_wait(send_sems, recv_sems, xs, lands, first, after, name):
    na = len(xs)

    def body(*refs):
        x_refs, land_refs = refs[:na], refs[na:2 * na]
        send_sems, recv_sems = refs[2 * na], refs[2 * na + 1]
        mx, my, mc = _me()
        for a in range(na):
            for j, (fx, fy) in enumerate(CHIP_FLIPS):
                px, py = _flip(mx, fx), _flip(my, fy)
                cp = _remote(x_refs[a], land_refs[a].at[2 * px + py], send_sems, recv_sems, 3 * (first + a) + j, (px, py, mc))
                cp.wait_send()
                cp.wait_recv()

    outs = pl.pallas_call(
        body, name=name,
        out_shape=tuple(pltpu.HBM(x.shape, x.dtype) for x in xs) + tuple(pltpu.HBM(l.shape, l.dtype) for l in lands),
        in_specs=[HBM] * (2 * na) + [SEM, SEM, pl.BlockSpec(memory_space=pl.ANY)], out_specs=(HBM,) * (2 * na),
        input_output_aliases={a: a for a in range(2 * na)},
        compiler_params=pltpu.CompilerParams(has_side_effects=DATAFLOW),
    )(*xs, *lands, send_sems, recv_sems, after)
    return list(outs[na:])


def all_gather_devices(v, name):
    r, l = v.shape

    def body(v_ref, out_ref, send_sems, recv_sems):
        mx, my, mc = _me()
        me = 4 * mx + 2 * my + mc
        peers = [(_flip(mx, fx), _flip(my, fy), _flip(mc, fc)) for fx, fy, fc in DEV_FLIPS]
        sends = [_remote(v_ref, out_ref.at[me], send_sems, recv_sems, j, p) for j, p in enumerate(peers)]
        for cp in sends:
            cp.start()
        for j, (px, py, pc) in enumerate(peers):
            _remote(v_ref, out_ref.at[4 * px + 2 * py + pc], send_sems, recv_sems, j, (px, py, pc)).wait_recv()
        for cp in sends:
            cp.wait_send()

    out = pl.pallas_call(
        body, name=name, out_shape=S((N_DEV, r, l), v.dtype), in_specs=[ANY], out_specs=ANY,
        scratch_shapes=[pltpu.SemaphoreType.DMA((7,)), pltpu.SemaphoreType.DMA((7,))])(v)
    me = 4 * lax.axis_index("x") + 2 * lax.axis_index("y") + lax.axis_index("c")
    return lax.dynamic_update_slice_in_dim(out, v[None], me, axis=0)


def sum_devices(g8, name):
    nd, r, l = g8.shape
    tile = r
    for t in (512, 256, 128, 64, 32, 16, 8):
        if r % t == 0:
            tile = t
            break

    def body(g_ref, o_ref):
        acc = g_ref[0]
        for d in range(1, nd):
            acc = acc + g_ref[d]
        o_ref[...] = acc

    return _call(body, name, (r // tile,), [pl.BlockSpec((nd, tile, l), lambda i: (0, i, 0))], _rs(tile, l), S((r, l), F32),
                 sem=("parallel",))(g8)


def exchange_halves(gs, name):
    na = len(gs)
    nch = gs[0].shape[0]

    def body(*refs):
        g_refs, out_refs = refs[:na], refs[na:2 * na]
        send_sems, recv_sems = refs[2 * na:]
        mx, my, mc = _me()
        sib = (mx, my, 1 - mc)
        cps = []
        for a in range(na):
            half = gs[a].shape[1] // 2
            for j in range(nch):
                cps.append(_remote(g_refs[a].at[j, pl.ds((1 - mc) * half, half), :], out_refs[a].at[j],
                                   send_sems, recv_sems, nch * a + j, sib))
        for cp in cps:
            cp.start()
        for cp in cps:
            cp.wait_recv()
        for cp in cps:
            cp.wait_send()

    return pl.pallas_call(
        body, name=name, out_shape=[S((nch, g.shape[1] // 2, g.shape[2]), g.dtype) for g in gs],
        in_specs=[ANY] * na, out_specs=[ANY] * na,
        scratch_shapes=[pltpu.SemaphoreType.DMA((nch * na,)), pltpu.SemaphoreType.DMA((nch * na,))])(*gs)


def add_halves(g4, got, name):
    nch, r, l = g4.shape
    half = r // 2
    tile = _pick_rows(half)
    nt = half // tile

    def body(g_ref, r_ref, a_ref, own_ref):
        j = pl.program_id(1)
        chip = 2 * lax.axis_index("x") + lax.axis_index("y")
        val = g_ref[0] + r_ref[0]
        a_ref[0] = val.astype(a_ref.dtype)

        @pl.when(j == chip)
        def _():
            own_ref[...] = val

    return pl.pallas_call(
        body, name=name, grid=(nt, nch),
        in_specs=[pl.BlockSpec((1, tile, l), lambda i, j: (j, lax.axis_index("c") * nt + i, 0)),
                  pl.BlockSpec((1, tile, l), lambda i, j: (j, i, 0))],
        out_specs=[pl.BlockSpec((1, tile, l), lambda i, j: (j, i, 0)), pl.BlockSpec((tile, l), lambda i, j: (i, 0))],
        out_shape=[S((nch, half, l), MXU_DTYPE), S((half, l), F32)],
        compiler_params=pltpu.CompilerParams(dimension_semantics=("parallel", "arbitrary"), vmem_limit_bytes=VMEM_LIMIT))(g4, got)


def _pick_rows(r, cap=512):
    for t in (512, 256, 128, 64, 32, 16):
        if t <= cap and r % t == 0:
            return t
    return r


def exchange_chips(parts, name):
    na = len(parts)

    def body(*refs):
        a_refs, out_refs = refs[:na], refs[na:2 * na]
        send_sems, recv_sems = refs[2 * na:]
        mx, my, mc = _me()
        peers = [(_flip(mx, fx), _flip(my, fy)) for fx, fy in CHIP_FLIPS]
        cps = [_remote(a_refs[a].at[2 * px + py], out_refs[a].at[j], send_sems, recv_sems, 3 * a + j, (px, py, mc))
               for j, (px, py) in enumerate(peers) for a in range(na)]
        for cp in cps:
            cp.start()
        for cp in cps:
            cp.wait_recv()
        for cp in cps:
            cp.wait_send()

    return pl.pallas_call(
        body, name=name, out_shape=[S((3,) + p.shape[1:], p.dtype) for p in parts], in_specs=[ANY] * na, out_specs=[ANY] * na,
        scratch_shapes=[pltpu.SemaphoreType.DMA((3 * na,)), pltpu.SemaphoreType.DMA((3 * na,))])(*parts)


def add_chips(own, got, name):
    h, l = own.shape
    tile = _pick_rows(h)

    def body(o_ref, g_ref, out_ref):
        out_ref[...] = ((o_ref[...] + g_ref[0].astype(F32)) + g_ref[1].astype(F32)) + g_ref[2].astype(F32)

    nt = h // tile
    return _call(body, name, (nt,), [_rs(tile, l), pl.BlockSpec((3, tile, l), lambda i: (0, i, 0))],
                 pl.BlockSpec((tile, l), lambda i: (lax.axis_index("c") * nt + i, 0)),
                 S((2 * h, l), F32), sem=("parallel",))(own, got)


def join_halves(bufs, name):
    na = len(bufs)

    def body(*refs):
        out_refs = refs[na:2 * na]
        send_sems, recv_sems = refs[2 * na:]
        mx, my, mc = _me()
        sib = (mx, my, 1 - mc)

        def half(a, hc):
            h = bufs[a].shape[0] // 2
            return out_refs[a].at[pl.ds(hc * h, h), :]

        cps = [_remote(half(a, mc), half(a, mc), send_sems, recv_sems, a, sib) for a in range(na)]
        for cp in cps:
            cp.start()
        for a in range(na):
            _remote(half(a, mc), half(a, 1 - mc), send_sems, recv_sems, a, sib).wait_recv()
        for cp in cps:
            cp.wait_send()

    return pl.pallas_call(
        body, name=name, out_shape=[S(b.shape, b.dtype) for b in bufs], in_specs=[ANY] * na, out_specs=[ANY] * na,
        input_output_aliases={a: a for a in range(na)},
        scratch_shapes=[pltpu.SemaphoreType.DMA((na,)), pltpu.SemaphoreType.DMA((na,))])(*bufs)


def reduce_scatter(gs, tag):
    got = exchange_halves(gs, f"{tag}_d2d")
    sums = [add_halves(g, r, f"{tag}_add1_{i}") for i, (g, r) in enumerate(zip(gs, got))]
    got2 = exchange_chips([a for a, _ in sums], f"{tag}_ici")
    parts = [add_chips(own, r, f"{tag}_add2_{i}") for i, ((_, own), r) in enumerate(zip(sums, got2))]
    return join_halves(parts, f"{tag}_join")


PACK_L = 1024
BIG_ROW_MULT = 512


def _pack(arrs, dtype, row_mult, lead=None):
    lead_shape = () if lead is None else arrs[0].shape[:lead]
    flat = jnp.concatenate([a.astype(dtype).reshape(lead_shape + (-1,)) for a in arrs], axis=-1)
    n = flat.shape[-1]
    unit = row_mult * PACK_L
    total = -(-n // unit) * unit
    flat = jnp.pad(flat, [(0, 0)] * len(lead_shape) + [(0, total - n)])
    return flat.reshape(lead_shape + (total // PACK_L, PACK_L))


def _unpack(packed, shapes, lead=None):
    lead_shape = () if lead is None else packed.shape[:lead]
    flat = packed.reshape(lead_shape + (-1,))
    out, off = [], 0
    for shp in shapes:
        n = int(np.prod(shp))
        out.append(flat[..., off:off + n].reshape(lead_shape + tuple(shp)))
        off += n
    return out


def _to_full(gathered, axis):
    t = jnp.moveaxis(gathered, 0, axis)
    shp = t.shape
    return t.reshape(shp[:axis] + (shp[axis] * shp[axis + 1],) + shp[axis + 2:])


def _to_chip_major(full, axis):
    shp = full.shape
    t = full.reshape(shp[:axis] + (N_CHIPS, shp[axis] // N_CHIPS) + shp[axis + 1:])
    return jnp.moveaxis(t, axis, 0)


WEIGHTS = ("norm_ffn1", "ffn1_w_in", "ffn1_w_out", "norm_mix", "norm_ffn2", "ffn2_w_in", "ffn2_w_out", "ple_norm",
           "ple_gate_w", "ple_proj_w", "hyb_w_in", "conv_dw_w", "conv_dw_b", "conv_ln_g", "conv_ln_b", "ssm_conv_w",
           "ssm_conv_b", "ssm_dt_bias", "ssm_a_log", "ssm_d", "ssm_norm", "hyb_w_out", "att_w_qkv", "att_b_qkv",
           "att_sinks", "att_w_o", "att_b_o", "final_norm")
SHARD_AXIS = dict(ffn1_w_in=2, ffn1_w_out=1, ffn2_w_in=2, ffn2_w_out=1, ple_gate_w=1, ple_proj_w=2, hyb_w_in=2,
                  conv_dw_w=2, ssm_conv_w=2, hyb_w_out=1, att_w_qkv=2, att_b_qkv=1, att_w_o=1, att_b_o=1)
BIG = ("ffn1_w_in", "ffn1_w_out", "ffn2_w_in", "ffn2_w_out", "ple_gate_w", "ple_proj_w", "hyb_w_in", "hyb_w_out",
       "att_w_qkv", "att_w_o")
ODD_WIDTH = "hyb_w_in"
BIG_FLAT = tuple(k for k in BIG if k != ODD_WIDTH)
SMALL_SHARDED = ("conv_dw_w", "ssm_conv_w", "att_b_qkv", "att_b_o")
SMALL = tuple(k for k in WEIGHTS if k not in BIG)


def _step(x, p, target, w, m, v):
    mx, my = lax.axis_index("x"), lax.axis_index("y")
    chip = 2 * mx + my

    depth = w["norm_ffn1"].shape[0]
    order = sorted([(k, i) for i in range(depth) for k in BIG if _layer_index(k, i) is not None],
                   key=lambda t: (t[1], _stage(t[0])))
    small_g = all_gather_devices(_pack([w[k] for k in SMALL_SHARDED], F32, 8), "gather_small")
    shards = [w[k][_layer_index(k, i)].astype(MXU_DTYPE) for k, i in order]
    lands = [lax.dynamic_update_slice_in_dim(lax.empty((N_CHIPS,) + s.shape, s.dtype), s[None], chip, axis=0) for s in shards]
    send_sems, recv_sems, shards, lands = gather_start(shards, lands, small_g, "gather_start")

    def fetch(i, stage, after):
        pos = [p for p, (k, li) in enumerate(order) if li == i and _stage(k) == stage]
        got = gather_wait(send_sems, recv_sems, [shards[p] for p in pos], [lands[p] for p in pos], pos[0], after,
                          f"gather_wait_l{i}_{stage}")
        out = {}
        for p, g in zip(pos, got):
            k = order[p][0]
            if k == ODD_WIDTH:
                out[k] = _to_full(g, 1)
            elif SHARD_AXIS[k] == 2:
                out[k] = ColSharded(g)
            else:
                out[k] = g.reshape(-1, g.shape[-1])
        return out

    small_g = small_g[0::2]
    small_full = {k: _to_full(g, SHARD_AXIS[k])
                  for k, g in zip(SMALL_SHARDED, _unpack(small_g, [w[k].shape for k in SMALL_SHARDED], lead=1))}
    layers = [{k: small_full.get(k, w[k])[_layer_index(k, i)] for k in SMALL if _layer_index(k, i) is not None}
              for i in range(depth)]

    sink = GradSink({k: w[k].shape for k in BIG})
    loss, dx, grads = trunk_fwd_bwd(x[0], p[:, 0], target[0], layers, w["final_norm"], sink, fetch)

    widths = list(sink.bufs)
    reduced = dict(zip(widths, reduce_scatter([sink.bufs[c] for c in widths], "grads")))
    g_out = {k: sink.take(k, reduced) for k in BIG}
    vec = _pack([loss[0:1, 0:1]] + [grads[k] for k in SMALL], F32, 8)
    vec = sum_devices(all_gather_devices(vec, "gather_vectors"), "sum_vectors")
    parts = _unpack(vec, [(1, 1)] + [grads[k].shape for k in SMALL])
    loss_out = parts[0].reshape(())
    for k, g in zip(SMALL, parts[1:]):
        if k in SHARD_AXIS:
            ax = SHARD_AXIS[k]
            g = lax.dynamic_slice_in_dim(g, chip * w[k].shape[ax], w[k].shape[ax], axis=ax)
        g_out[k] = g

    delta, new_m, new_v = {}, {}, {}
    for k in BIG:
        shp = w[k].shape
        two_d = lambda a: a.reshape(-1, shp[-1])
        d_, m_, v_ = adamw(two_d(w[k]), two_d(g_out[k]), two_d(m[k]), two_d(v[k]), f"adamw_{k}")
        delta[k], new_m[k], new_v[k] = d_.reshape(shp), m_.reshape(shp), v_.reshape(shp)
    shapes = [w[k].shape for k in SMALL]
    packed = [_pack([src[k] for k in SMALL], F32, 8) for src in (w, g_out, m, v)]
    outs = adamw(*packed, "adamw_small")
    for dst, o in zip((delta, new_m, new_v), outs):
        for k, a in zip(SMALL, _unpack(o, shapes)):
            dst[k] = a
    return ((loss_out, dx[None]) + tuple(g_out[k] for k in WEIGHTS) + tuple(delta[k] for k in WEIGHTS)
            + tuple(new_m[k] for k in WEIGHTS) + tuple(new_v[k] for k in WEIGHTS))


def kernel(x, p, norm_ffn1, ffn1_w_in, ffn1_w_out, norm_mix, norm_ffn2, ffn2_w_in, ffn2_w_out, ple_norm, ple_gate_w, ple_proj_w, hyb_w_in, conv_dw_w, conv_dw_b, conv_ln_g, conv_ln_b, ssm_conv_w, ssm_conv_b, ssm_dt_bias, ssm_a_log, ssm_d, ssm_norm, hyb_w_out, att_w_qkv, att_b_qkv, att_sinks, att_w_o, att_b_o, final_norm, loss_target, m_norm_ffn1, m_ffn1_w_in, m_ffn1_w_out, m_norm_mix, m_norm_ffn2, m_ffn2_w_in, m_ffn2_w_out, m_ple_norm, m_ple_gate_w, m_ple_proj_w, m_hyb_w_in, m_conv_dw_w, m_conv_dw_b, m_conv_ln_g, m_conv_ln_b, m_ssm_conv_w, m_ssm_conv_b, m_ssm_dt_bias, m_ssm_a_log, m_ssm_d, m_ssm_norm, m_hyb_w_out, m_att_w_qkv, m_att_b_qkv, m_att_sinks, m_att_w_o, m_att_b_o, m_final_norm, v_norm_ffn1, v_ffn1_w_in, v_ffn1_w_out, v_norm_mix, v_norm_ffn2, v_ffn2_w_in, v_ffn2_w_out, v_ple_norm, v_ple_gate_w, v_ple_proj_w, v_hyb_w_in, v_conv_dw_w, v_conv_dw_b, v_conv_ln_g, v_conv_ln_b, v_ssm_conv_w, v_ssm_conv_b, v_ssm_dt_bias, v_ssm_a_log, v_ssm_d, v_ssm_norm, v_hyb_w_out, v_att_w_qkv, v_att_b_qkv, v_att_sinks, v_att_w_o, v_att_b_o, v_final_norm):
    given = locals()
    w = {k: given[k] for k in WEIGHTS}
    m = {k: given["m_" + k] for k in WEIGHTS}
    v = {k: given["v_" + k] for k in WEIGHTS}
    return _step(x, p, loss_target, w, m, v)
```

```python
import functools
import math

import numpy as np
import jax
import jax.numpy as jnp
from jax import lax
from jax.experimental import pallas as pl
from jax.experimental.pallas import tpu as pltpu

F32 = jnp.float32
BF16 = jnp.bfloat16
MXU_DTYPE = jnp.bfloat16
S = jax.ShapeDtypeStruct
MESH = pl.DeviceIdType.MESH

V7X_VMEM_BYTES = 64 * 2**20
VMEM_LIMIT = 48 * 2**20
LANE = 128

EPS = 1e-6
SSM_HEADS = 16
HEAD_DIM = 64
SSM_GROUPS = 2
SSM_STATE = 128
SSM_CONV = 4
CHUNK = 128
CONV_WIDTH = 31
ATT_HEADS = 16
ATT_KV_HEADS = 4
WINDOW = 128
ROPE_THETA = 10000.0
ADAM_LR = 0.001
ADAM_B1 = 0.9
ADAM_B2 = 0.999
ADAM_EPS = 1e-08
ADAM_WD = 0.01
ADAM_STEP = 10

N_CHIPS = 4
N_DEV = 8

NN = ((1,), (0,))
NT = ((1,), (1,))
TN = ((0,), (0,))


def _mm(a, b, dims=NN):
    return lax.dot_general(a.astype(MXU_DTYPE), b.astype(MXU_DTYPE), (dims, ((), ())), preferred_element_type=F32)


def _split3(a):
    hi = a.astype(BF16)
    r = a - hi.astype(F32)
    mid = r.astype(BF16)
    lo = (r - mid.astype(F32)).astype(BF16)
    return hi, mid, lo


def _mm01(a, onehot, dims=NN):
    o = onehot.astype(BF16)
    out = None
    for part in _split3(a):
        t = lax.dot_general(part, o, (dims, ((), ())), preferred_element_type=F32)
        out = t if out is None else out + t
    return out


def _01mm(onehot, a):
    o = onehot.astype(BF16)
    out = None
    for part in _split3(a):
        t = lax.dot_general(o, part, (NN, ((), ())), preferred_element_type=F32)
        out = t if out is None else out + t
    return out


def _sigmoid(x):
    return 0.5 * jnp.tanh(0.5 * x) + 0.5


def _softplus(x):
    return jnp.maximum(x, 0.0) + jnp.log(1.0 + jnp.exp(-jnp.abs(x)))


def _iota(shape, axis):
    return lax.broadcasted_iota(jnp.int32, shape, axis)


def _head_indicator(width, heads, transposed=False):
    per = width // heads
    if transposed:
        return (_iota((heads, width), 1) // per == _iota((heads, width), 0)).astype(F32)
    return (_iota((width, heads), 0) // per == _iota((width, heads), 1)).astype(F32)


def _acc(ref, i, val):
    @pl.when(i == 0)
    def _():
        ref[...] = val

    @pl.when(i > 0)
    def _():
        ref[...] += val


def _rs(tile, width, col=0, shift=0, n=None):
    if shift == 0:
        return pl.BlockSpec((tile, width), lambda i: (i, col))
    if shift < 0:
        return pl.BlockSpec((tile, width), lambda i: (jnp.maximum(i - 1, 0), col))
    return pl.BlockSpec((tile, width), lambda i: (jnp.minimum(i + 1, n - 1), col))


def _ps(shape):
    return pl.BlockSpec(shape, lambda i: (0,) * len(shape))


def _call(body, name, grid, in_specs, out_specs, out_shape, scratch=(), sem=None):
    return pl.pallas_call(
        body, name=name, grid=grid, in_specs=in_specs, out_specs=out_specs, out_shape=out_shape,
        scratch_shapes=list(scratch),
        compiler_params=pltpu.CompilerParams(dimension_semantics=sem, vmem_limit_bytes=VMEM_LIMIT))


def _row_tile(n, target):
    t = min(n, target)
    assert n % t == 0, (n, t)
    return t


def _pick_tile(dim, target):
    if dim <= target:
        return dim
    t = (int(1.4 * target) // LANE) * LANE
    while t >= LANE:
        if dim % t == 0:
            return t
        t -= LANE
    return dim


ANY = pl.BlockSpec(memory_space=pl.ANY)


def _paired(j):
    return (j % 2) * 2 + j // 2


class ColSharded:
    def __init__(self, arr, paired=False):
        self.arr, self.paired = arr, paired
        self.nch, self.rows, self.per = arr.shape
        self.shape = (self.rows, self.nch * self.per)

    def chip(self, j):
        return _paired(j) if self.paired else j


class Slot:
    def __init__(self, buf, kind, per, off, c0=0, paired=False):
        self.buf, self.kind, self.per, self.off, self.c0, self.paired = buf, kind, per, off, c0, paired

    def chip(self, j):
        return _paired(j) if self.paired else j


def matmul(a, b, mode, name, *, out_dtype=F32, scale=None, res=None, bias=None, into=None, tm=1024, tn=1024, tk=1024):
    bshape = b.shape
    if mode == "nn":
        (m, k), (k2, n) = a.shape, bshape
    elif mode == "nt":
        (m, k), (n, k2) = a.shape, bshape
    else:
        (k, m), (k2, n) = a.shape, bshape
    assert k == k2, (a.shape, bshape, mode)
    tm, tn, tk = _pick_tile(m, tm), _pick_tile(n, tn), _pick_tile(k, tk)
    if isinstance(b, ColSharded):
        if mode == "nn":
            tn = b.per
        else:
            assert mode == "nt"
            tk = b.per
    if into is not None:
        if into.kind == "c":
            tn = into.per
            assert into.off % tm == 0 and n == N_CHIPS * into.per
        else:
            tm = max(1, min(m, int(1.4 * 1024)) // into.per) * into.per
            assert m % tm == 0 and into.off % into.per == 0 and into.c0 % (tm // into.per) == 0
    nk = k // tk
    dims = {"nn": NN, "nt": NT, "tn": TN}[mode]
    a_spec = (pl.BlockSpec((tk, tm), lambda i, j, kk: (kk, i)) if mode == "tn"
              else pl.BlockSpec((tm, tk), lambda i, j, kk: (i, kk)))
    if isinstance(b, ColSharded):
        bchip = b.chip
        b_spec = (pl.BlockSpec((None, tk, tn), lambda i, j, kk: (bchip(j), kk, 0)) if mode == "nn"
                  else pl.BlockSpec((None, tn, tk), lambda i, j, kk: (bchip(kk), j, 0)))
        b = b.arr
    else:
        b_spec = (pl.BlockSpec((tn, tk), lambda i, j, kk: (j, kk)) if mode == "nt"
                  else pl.BlockSpec((tk, tn), lambda i, j, kk: (kk, j)))
    plain_o = pl.BlockSpec((tm, tn), lambda i, j, kk: (i, j))
    ins, in_specs = [a, b], [a_spec, b_spec]
    if bias is not None:
        ins.append(bias)
        in_specs.append(pl.BlockSpec((1, tn), lambda i, j, kk: (0, j)))
    if res is not None:
        ins.append(res)
        in_specs.append(plain_o)
    aliases = {}
    if into is None:
        o_spec, o_shape = plain_o, S((m, n), out_dtype)
    else:
        aliases = {len(ins): 0}
        ins.append(into.buf)
        in_specs.append(ANY)
        o_shape = S(into.buf.shape, into.buf.dtype)
        if into.kind == "c":
            ob, ochip = into.off // tm, into.chip
            o_spec = pl.BlockSpec((None, tm, tn), lambda i, j, kk: (ochip(j), ob + i, 0))
        else:
            q, ob = tm // into.per, into.off // into.per
            cb = into.c0 // q
            o_spec = pl.BlockSpec((q, into.per, tn), lambda i, j, kk: (cb + i, ob, j))

    def body(*refs):
        a_ref, b_ref = refs[0], refs[1]
        o_ref, acc_ref = refs[-2], refs[-1]
        kk = pl.program_id(2)

        @pl.when(kk == 0)
        def _():
            acc_ref[...] = jnp.zeros_like(acc_ref)

        acc_ref[...] += _mm(a_ref[...], b_ref[...], dims)

        @pl.when(kk == nk - 1)
        def _():
            out = acc_ref[...]
            if scale is not None:
                out = out * scale
            pos = 2
            if bias is not None:
                out = out + refs[pos][...]
                pos += 1
            if res is not None:
                out = out + refs[pos][...]
            o_ref[...] = out.astype(o_ref.dtype).reshape(o_ref.shape)

    return pl.pallas_call(
        body, name=name, grid=(m // tm, n // tn, nk), in_specs=in_specs, out_specs=o_spec, out_shape=o_shape,
        scratch_shapes=[pltpu.VMEM((tm, tn), F32)], input_output_aliases=aliases,
        compiler_params=pltpu.CompilerParams(dimension_semantics=("parallel", "parallel", "arbitrary"),
                                             vmem_limit_bytes=VMEM_LIMIT))(*ins)


def rms_fwd(h, g, name):
    n, d = h.shape
    tile = _row_tile(n, 512)

    def body(h_ref, g_ref, o_ref):
        x = h_ref[...]
        r = lax.rsqrt(jnp.mean(x * x, axis=-1, keepdims=True) + EPS)
        o_ref[...] = (x * r * g_ref[...]).astype(o_ref.dtype)

    return _call(body, name, (n // tile,), [_rs(tile, d), _ps((1, d))], _rs(tile, d), S((n, d), MXU_DTYPE),
                 sem=("parallel",))(h, g)


def _rms_bwd_math(x, g, dy):
    r = lax.rsqrt(jnp.mean(x * x, axis=-1, keepdims=True) + EPS)
    xh = x * r
    dg = jnp.sum(dy * xh, axis=0, keepdims=True)
    dxh = dy * g
    dx = r * (dxh - xh * jnp.mean(dxh * xh, axis=-1, keepdims=True))
    return dx, dg


def rms_bwd(h, g, dxn, dh_in, name, colsum=False):
    n, d = h.shape
    tile = _row_tile(n, 256)

    def body(h_ref, g_ref, dxn_ref, dh_ref, o_ref, dg_ref, *cs_ref):
        i = pl.program_id(0)
        dx, dg = _rms_bwd_math(h_ref[...], g_ref[...], dxn_ref[...].astype(F32))
        out = dh_ref[...] + dx
        o_ref[...] = out
        _acc(dg_ref, i, dg)
        if colsum:
            _acc(cs_ref[0], i, jnp.sum(out, axis=0, keepdims=True))

    outs = [S((n, d), F32), S((1, d), F32)] + ([S((1, d), F32)] if colsum else [])
    ospecs = [_rs(tile, d), _ps((1, d))] + ([_ps((1, d))] if colsum else [])
    return _call(body, name, (n // tile,), [_rs(tile, d), _ps((1, d)), _rs(tile, d), _rs(tile, d)], ospecs, outs,
                 sem=("arbitrary",))(h, g, dxn, dh_in)


def swiglu_in(h, g, w_in, name):
    n, d = h.shape
    per = w_in.per
    nj = w_in.nch // 2
    tile = _row_tile(n, 512)

    def body(h_ref, g_ref, wg_ref, wu_ref, xn_ref, u_ref, hm_ref):
        x = h_ref[...]
        r = lax.rsqrt(jnp.mean(x * x, axis=-1, keepdims=True) + EPS)
        xn = (x * r * g_ref[...]).astype(xn_ref.dtype)

        @pl.when(pl.program_id(1) == 0)
        def _():
            xn_ref[...] = xn

        a = _mm(xn, wg_ref[...])
        b = _mm(xn, wu_ref[...])
        u_ref[:, :per] = a
        u_ref[:, per:] = b
        hm_ref[...] = (a * _sigmoid(a) * b).astype(hm_ref.dtype)

    return pl.pallas_call(
        body, name=name, grid=(n // tile, nj),
        in_specs=[pl.BlockSpec((tile, d), lambda i, j: (i, 0)), pl.BlockSpec((1, d), lambda i, j: (0, 0)),
                  pl.BlockSpec((None, d, per), lambda i, j: (j, 0, 0)), pl.BlockSpec((None, d, per), lambda i, j: (nj + j, 0, 0))],
        out_specs=[pl.BlockSpec((tile, d), lambda i, j: (i, 0)), pl.BlockSpec((tile, 2 * per), lambda i, j: (i, j)),
                   pl.BlockSpec((tile, per), lambda i, j: (i, j))],
        out_shape=[S((n, d), MXU_DTYPE), S((n, 2 * nj * per), F32), S((n, nj * per), MXU_DTYPE)],
        compiler_params=pltpu.CompilerParams(dimension_semantics=("parallel", "arbitrary"), vmem_limit_bytes=VMEM_LIMIT),
    )(h, g, w_in.arr, w_in.arr)


def swiglu_out_bwd(dh, w_out, u, name):
    n, d = dh.shape
    f = w_out.shape[0]
    per = u.shape[1] // 4
    nj = f // per
    tile = _row_tile(n, 512)

    def body(dh_ref, w_ref, u_ref, du_ref):
        dm = 0.5 * _mm(dh_ref[...], w_ref[...], NT)
        a = u_ref[:, :per]
        b = u_ref[:, per:]
        s = _sigmoid(a)
        du_ref[:, :per] = (dm * b * s * (1.0 + a * (1.0 - s))).astype(du_ref.dtype)
        du_ref[:, per:] = (dm * a * s).astype(du_ref.dtype)

    return pl.pallas_call(
        body, name=name, grid=(n // tile, nj),
        in_specs=[pl.BlockSpec((tile, d), lambda i, j: (i, 0)), pl.BlockSpec((per, d), lambda i, j: (j, 0)),
                  pl.BlockSpec((tile, 2 * per), lambda i, j: (i, j))],
        out_specs=pl.BlockSpec((tile, 2 * per), lambda i, j: (i, j)),
        out_shape=S(u.shape, MXU_DTYPE),
        compiler_params=pltpu.CompilerParams(dimension_semantics=("parallel", "parallel"), vmem_limit_bytes=VMEM_LIMIT),
    )(dh, w_out, u)


def ple_fwd(h, gl, pp, name):
    n, d = h.shape
    tile = _row_tile(n, 512)

    def body(h_ref, gl_ref, pp_ref, o_ref):
        o_ref[...] = h_ref[...] + _sigmoid(gl_ref[...]) * pp_ref[...]

    return _call(body, name, (n // tile,), [_rs(tile, d)] * 3, _rs(tile, d), S((n, d), F32), sem=("parallel",))(h, gl, pp)


def ple_bwd(dh, gl, pp, name):
    n, d = dh.shape
    tile = _row_tile(n, 512)

    def body(dh_ref, gl_ref, pp_ref, dpp_ref, dgl_ref):
        g = _sigmoid(gl_ref[...])
        dh_ = dh_ref[...]
        dpp_ref[...] = (dh_ * g).astype(dpp_ref.dtype)
        dgl_ref[...] = (dh_ * pp_ref[...] * g * (1.0 - g)).astype(dgl_ref.dtype)

    return _call(body, name, (n // tile,), [_rs(tile, d)] * 3, [_rs(tile, d)] * 2, [S((n, d), MXU_DTYPE)] * 2,
                 sem=("parallel",))(dh, gl, pp)


def loss_head(h, g, target, name):
    n, d = h.shape
    tile = _row_tile(n, 256)

    def body(h_ref, g_ref, t_ref, dh_ref, dg_ref, loss_ref):
        i = pl.program_id(0)
        x = h_ref[...]
        gg = g_ref[...]
        r = lax.rsqrt(jnp.mean(x * x, axis=-1, keepdims=True) + EPS)
        err = x * r * gg - t_ref[...]
        part = 0.5 * jnp.sum(jnp.mean(err * err, axis=-1, keepdims=True), axis=0, keepdims=True)
        dx, dg = _rms_bwd_math(x, gg, err * (1.0 / d))
        dh_ref[...] = dx
        _acc(dg_ref, i, dg)
        _acc(loss_ref, i, jnp.broadcast_to(part, (8, LANE)))

    return _call(body, name, (n // tile,), [_rs(tile, d), _ps((1, d)), _rs(tile, d)],
                 [_rs(tile, d), _ps((1, d)), _ps((8, LANE))], [S((n, d), F32), S((1, d), F32), S((8, LANE), F32)],
                 sem=("arbitrary",))(h, g, target)


def adamw(w, g, m, v, name):
    r, c = w.shape
    tile = r
    for t in (512, 256, 128, 64, 32, 16, 8):
        if r % t == 0 and t * c * 4 <= 2**21:
            tile = t
            break
    c1 = np.float32(1.0 - ADAM_B1 ** ADAM_STEP)
    c2 = np.float32(1.0 - ADAM_B2 ** ADAM_STEP)

    def body(w_ref, g_ref, m_ref, v_ref, d_ref, mo_ref, vo_ref):
        gg = g_ref[...]
        mm = ADAM_B1 * m_ref[...] + (1.0 - ADAM_B1) * gg
        vv = ADAM_B2 * v_ref[...] + (1.0 - ADAM_B2) * (gg * gg)
        mo_ref[...] = mm
        vo_ref[...] = vv
        d_ref[...] = -ADAM_LR * ((mm / c1) / (jnp.sqrt(vv / c2) + ADAM_EPS) + ADAM_WD * w_ref[...])

    return _call(body, name, (r // tile,), [_rs(tile, c)] * 4, [_rs(tile, c)] * 3, [S((r, c), F32)] * 3,
                 sem=("parallel",))(w, g, m, v)


def _taps_fwd(sc, w_ref, width, halo, tile, acc):
    for k in range(width):
        o = halo - (width - 1) + k
        acc = acc + w_ref[k:k + 1, :] * sc[o:o + tile, :]
    return acc


def _taps_bwd_x(sc_d, w_ref, width, tile, acc):
    for k in range(width):
        o = (width - 1) - k
        acc = acc + w_ref[k:k + 1, :] * sc_d[o:o + tile, :]
    return acc


def _taps_bwd_w(dy, sc, dw_ref, width, halo, tile, i):
    @pl.when(i == 0)
    def _():
        dw_ref[...] = jnp.zeros_like(dw_ref)

    for k in range(width):
        o = halo - (width - 1) + k
        dw_ref[k:k + 1, :] += jnp.sum(dy * sc[o:o + tile, :], axis=0, keepdims=True)


def _ln_stats(x):
    mu = jnp.mean(x, axis=-1, keepdims=True)
    xc = x - mu
    r = lax.rsqrt(jnp.mean(xc * xc, axis=-1, keepdims=True) + EPS)
    return xc * r, r


def conv_group_fwd(proj, cw, cb, lg, lb, name):
    n = proj.shape[0]
    d = cw.shape[1]
    tile = _row_tile(n, 256)
    halo = 32

    def body(v_ref, g_ref, vp_ref, gp_ref, cw_ref, cb_ref, lg_ref, lb_ref, u_ref, u1_ref, sc):
        i = pl.program_id(0)
        first = (i > 0).astype(F32)
        sc[0:halo, :] = vp_ref[tile - halo:, :] * _sigmoid(gp_ref[tile - halo:, :]) * first
        sc[halo:, :] = v_ref[...] * _sigmoid(g_ref[...])
        u1 = _taps_fwd(sc, cw_ref, CONV_WIDTH, halo, tile, jnp.zeros((tile, d), F32) + cb_ref[...])
        u1_ref[...] = u1
        xh, _ = _ln_stats(u1)
        y = xh * lg_ref[...] + lb_ref[...]
        u_ref[...] = (y * _sigmoid(y)).astype(u_ref.dtype)

    return _call(body, name, (n // tile,),
                 [_rs(tile, d, 0), _rs(tile, d, 1), _rs(tile, d, 0, -1), _rs(tile, d, 1, -1),
                  _ps(cw.shape), _ps((1, d)), _ps((1, d)), _ps((1, d))],
                 [_rs(tile, d), _rs(tile, d)], [S((n, d), MXU_DTYPE), S((n, d), F32)],
                 scratch=[pltpu.VMEM((halo + tile, d), F32)], sem=("arbitrary",))(proj, proj, proj, proj, cw, cb, lg, lb)


def conv_group_bwd(du, u1, proj, cw, lg, lb, name):
    n = proj.shape[0]
    d = cw.shape[1]
    tile = _row_tile(n, 256)
    halo = 32
    nt = n // tile

    def body(du_ref, dun_ref, u1_ref, u1n_ref, v_ref, g_ref, vp_ref, gp_ref, cw_ref, lg_ref, lb_ref,
             dp_ref, dcw_ref, dcb_ref, dlg_ref, dlb_ref, sc, sc_d):
        i = pl.program_id(0)

        def ln_swish_bwd(dy_, u1_):
            xh, r = _ln_stats(u1_)
            y = xh * lg_ref[...] + lb_ref[...]
            s = _sigmoid(y)
            dyy = dy_ * s * (1.0 + y * (1.0 - s))
            dxh = dyy * lg_ref[...]
            dx = r * (dxh - jnp.mean(dxh, axis=-1, keepdims=True) - xh * jnp.mean(dxh * xh, axis=-1, keepdims=True))
            return dx, jnp.sum(dyy * xh, axis=0, keepdims=True), jnp.sum(dyy, axis=0, keepdims=True)

        du1, dlg, dlb = ln_swish_bwd(du_ref[...].astype(F32), u1_ref[...])
        du1n, _, _ = ln_swish_bwd(dun_ref[0:halo, :].astype(F32), u1n_ref[0:halo, :])
        sc_d[0:tile, :] = du1
        sc_d[tile:, :] = du1n * (i < nt - 1).astype(F32)
        sig = _sigmoid(g_ref[...])
        val = v_ref[...]
        sc[0:halo, :] = vp_ref[tile - halo:, :] * _sigmoid(gp_ref[tile - halo:, :]) * (i > 0).astype(F32)
        sc[halo:, :] = val * sig
        du0 = _taps_bwd_x(sc_d, cw_ref, CONV_WIDTH, tile, jnp.zeros((tile, d), F32))
        _taps_bwd_w(du1, sc, dcw_ref, CONV_WIDTH, halo, tile, i)
        _acc(dcb_ref, i, jnp.sum(du1, axis=0, keepdims=True))
        _acc(dlg_ref, i, dlg)
        _acc(dlb_ref, i, dlb)
        dp_ref[:, :d] = (du0 * sig).astype(dp_ref.dtype)
        dp_ref[:, d:] = (du0 * val * sig * (1.0 - sig)).astype(dp_ref.dtype)

    return _call(body, name, (nt,),
                 [_rs(tile, d), _rs(tile, d, 0, 1, nt), _rs(tile, d), _rs(tile, d, 0, 1, nt),
                  _rs(tile, d, 0), _rs(tile, d, 1), _rs(tile, d, 0, -1), _rs(tile, d, 1, -1),
                  _ps(cw.shape), _ps((1, d)), _ps((1, d))],
                 [_rs(tile, 2 * d), _ps(cw.shape), _ps((1, d)), _ps((1, d)), _ps((1, d))],
                 [S((n, proj.shape[1]), MXU_DTYPE), S(cw.shape, F32), S((1, d), F32), S((1, d), F32), S((1, d), F32)],
                 scratch=[pltpu.VMEM((halo + tile, d), F32), pltpu.VMEM((tile + halo, d), F32)],
                 sem=("arbitrary",))(du, du, u1, u1, proj, proj, proj, proj, cw, lg, lb)


def ssm_conv_fwd(proj, dtr, sw, sb, dtb, name):
    n = proj.shape[0]
    w = sw.shape[1]
    inner = SSM_HEADS * HEAD_DIM
    tile = _row_tile(n, 256)
    halo = 8

    def body(x_ref, xp_ref, dtr_ref, sw_ref, sb_ref, dtb_ref, pre_ref, xs_ref, bc_ref, dt_ref, sc):
        i = pl.program_id(0)
        sc[0:halo, :] = xp_ref[tile - halo:, :] * (i > 0).astype(F32)
        sc[halo:, :] = x_ref[...]
        pre = _taps_fwd(sc, sw_ref, SSM_CONV, halo, tile, jnp.zeros((tile, w), F32) + sb_ref[...])
        pre_ref[...] = pre
        act = pre * _sigmoid(pre)
        xs_ref[...] = act[:, :inner]
        bc_ref[...] = act[:, inner:]
        dt = _softplus(dtr_ref[...] + dtb_ref[...])
        dt_ref[...] = jnp.where(_iota(dt.shape, 1) < SSM_HEADS, dt, 0.0)

    return _call(body, name, (n // tile,),
                 [_rs(tile, w, 2), _rs(tile, w, 2, -1), _rs(tile, LANE), _ps(sw.shape), _ps((1, w)), _ps((1, LANE))],
                 [_rs(tile, w), _rs(tile, inner), _rs(tile, w - inner), _rs(tile, LANE)],
                 [S((n, w), F32), S((n, inner), F32), S((n, w - inner), F32), S((n, LANE), F32)],
                 scratch=[pltpu.VMEM((halo + tile, w), F32)], sem=("arbitrary",))(proj, proj, dtr, sw, sb, dtb)


def ssm_conv_bwd(dxs, dbc, pre, proj, sw, dproj, name):
    n = proj.shape[0]
    w = sw.shape[1]
    inner = SSM_HEADS * HEAD_DIM
    tile = _row_tile(n, 256)
    halo = 8
    nt = n // tile

    def body(dxs_ref, dxsn_ref, dbc_ref, dbcn_ref, pre_ref, pren_ref, x_ref, xp_ref, sw_ref, dp_in_ref,
             dx_ref, dsw_ref, dsb_ref, sc, sc_d):
        i = pl.program_id(0)

        def silu_bwd(d_, p_):
            s = _sigmoid(p_)
            return d_ * s * (1.0 + p_ * (1.0 - s))

        sc_d[0:tile, :inner] = silu_bwd(dxs_ref[...], pre_ref[:, :inner])
        sc_d[0:tile, inner:] = silu_bwd(dbc_ref[...], pre_ref[:, inner:])
        last = (i < nt - 1).astype(F32)
        sc_d[tile:, :inner] = silu_bwd(dxsn_ref[0:halo, :], pren_ref[0:halo, :inner]) * last
        sc_d[tile:, inner:] = silu_bwd(dbcn_ref[0:halo, :], pren_ref[0:halo, inner:]) * last
        sc[0:halo, :] = xp_ref[tile - halo:, :] * (i > 0).astype(F32)
        sc[halo:, :] = x_ref[...]
        dpre = sc_d[0:tile, :]
        dx_ref[...] = _taps_bwd_x(sc_d, sw_ref, SSM_CONV, tile, jnp.zeros((tile, w), F32)).astype(dx_ref.dtype)
        _taps_bwd_w(dpre, sc, dsw_ref, SSM_CONV, halo, tile, i)
        _acc(dsb_ref, i, jnp.sum(dpre, axis=0, keepdims=True))

    return pl.pallas_call(
        body, name=name, grid=(nt,),
        in_specs=[_rs(tile, inner), _rs(tile, inner, 0, 1, nt), _rs(tile, w - inner), _rs(tile, w - inner, 0, 1, nt),
                  _rs(tile, w), _rs(tile, w, 0, 1, nt), _rs(tile, w, 2), _rs(tile, w, 2, -1), _ps(sw.shape), ANY],
        out_specs=[_rs(tile, w, 2), _ps(sw.shape), _ps((1, w))],
        out_shape=[S(dproj.shape, dproj.dtype), S(sw.shape, F32), S((1, w), F32)],
        scratch_shapes=[pltpu.VMEM((halo + tile, w), F32), pltpu.VMEM((tile + halo, w), F32)],
        input_output_aliases={9: 0},
        compiler_params=pltpu.CompilerParams(dimension_semantics=("arbitrary",), vmem_limit_bytes=VMEM_LIMIT),
    )(dxs, dxs, dbc, dbc, pre, pre, proj, proj, sw, dproj)


def _ssd_prologue(dt_ref, dtT_ref, al_ref, alc_ref):
    row = _iota((CHUNK, CHUNK), 0)
    col = _iota((CHUNK, CHUNK), 1)
    dt = dt_ref[:, :SSM_HEADS]
    a_row = -jnp.exp(al_ref[:, :SSM_HEADS])
    a_col = -jnp.exp(alc_ref[...])
    cs = _01mm((row >= col).astype(F32), dt * a_row)
    csT = _mm01(dtT_ref[...] * a_col, (row <= col).astype(F32))
    return dt, a_row, cs, csT, row, col


def _decay(cs, csT, h, row, col):
    lm = jnp.exp(jnp.where(row >= col, cs[:, h:h + 1] - csT[h:h + 1, :], -1e30))
    lmT = jnp.exp(jnp.where(col >= row, csT[h:h + 1, :] - cs[:, h:h + 1], -1e30))
    return lm, lmT


def ssd_fwd(xs, bc, dt, dtT, alog_row, alog_col, name):
    n, width = xs.shape
    nc = n // CHUNK
    gw = width // SSM_GROUPS
    hpg = SSM_HEADS // SSM_GROUPS
    ns = SSM_STATE

    def body(xs_ref, bc_ref, dt_ref, dtT_ref, al_ref, alc_ref, y_ref, hs_ref, h_sc):
        i = pl.program_id(0)

        @pl.when(i == 0)
        def _():
            h_sc[...] = jnp.zeros_like(h_sc)

        dt, a_row, cs, csT, row, col = _ssd_prologue(dt_ref, dtT_ref, al_ref, alc_ref)
        indT = _head_indicator(width, SSM_HEADS, transposed=True)
        dt_full = _mm01(dt, indT)
        e_full = jnp.exp(_mm01(cs, indT))
        dte_full = jnp.exp(_mm01(cs[CHUNK - 1:CHUNK, :] - cs, indT))
        xt = xs_ref[...] * dt_full
        hs_ref[0] = h_sc[...]
        lo = _iota((CHUNK, 2 * HEAD_DIM), 1) < HEAD_DIM
        for g in range(SSM_GROUPS):
            bg = bc_ref[:, g * ns:(g + 1) * ns]
            cg = bc_ref[:, (SSM_GROUPS + g) * ns:(SSM_GROUPS + g + 1) * ns]
            gm = _mm(cg, bg, NT)
            hg = h_sc[g * gw:(g + 1) * gw, :]
            yoff = e_full[:, g * gw:(g + 1) * gw] * _mm(cg, hg, NT)
            for pr in range(hpg // 2):
                h0 = g * hpg + 2 * pr
                c0 = h0 * HEAD_DIM
                xp = xt[:, c0:c0 + 2 * HEAD_DIM]
                m0 = gm * _decay(cs, csT, h0, row, col)[0]
                m1 = gm * _decay(cs, csT, h0 + 1, row, col)[0]
                yd = jnp.where(lo, _mm(m0, xp), _mm(m1, xp))
                y_ref[:, c0:c0 + 2 * HEAD_DIM] = yd + yoff[:, 2 * pr * HEAD_DIM:(2 * pr + 2) * HEAD_DIM]
            sg = _mm(xt[:, g * gw:(g + 1) * gw] * dte_full[:, g * gw:(g + 1) * gw], bg, TN)
            for hh in range(hpg):
                h = g * hpg + hh
                r0 = h * HEAD_DIM
                h_sc[r0:r0 + HEAD_DIM, :] = (h_sc[r0:r0 + HEAD_DIM, :] * jnp.exp(csT[h:h + 1, CHUNK - 1:CHUNK])
                                             + sg[hh * HEAD_DIM:(hh + 1) * HEAD_DIM, :])

    bcw = bc.shape[1]
    return _call(body, name, (nc,),
                 [_rs(CHUNK, width), _rs(CHUNK, bcw), _rs(CHUNK, LANE), pl.BlockSpec((SSM_HEADS, CHUNK), lambda i: (0, i)),
                  _ps((1, LANE)), _ps((SSM_HEADS, 1))],
                 [_rs(CHUNK, width), pl.BlockSpec((1, width, ns), lambda i: (i, 0, 0))],
                 [S((n, width), F32), S((nc, width, ns), F32)],
                 scratch=[pltpu.VMEM((width, ns), F32)], sem=("arbitrary",))(xs, bc, dt, dtT, alog_row, alog_col)


def ssd_bwd(xs, bc, dt, dtT, alog_row, alog_col, hs, dy, dxs_skip, name):
    n, width = xs.shape
    nc = n // CHUNK
    gw = width // SSM_GROUPS
    hpg = SSM_HEADS // SSM_GROUPS
    ns = SSM_STATE
    bcw = bc.shape[1]

    def body(xs_ref, bc_ref, dt_ref, dtT_ref, al_ref, alc_ref, hs_ref, dy_ref, skip_ref,
             dxs_ref, dbc_ref, ddtr_ref, dal_ref, ddtb_ref, dh_sc, dxt_sc):
        i = pl.program_id(0)

        @pl.when(i == 0)
        def _():
            dh_sc[...] = jnp.zeros_like(dh_sc)

        dt, a_row, cs, csT, row, col = _ssd_prologue(dt_ref, dtT_ref, al_ref, alc_ref)
        indT = _head_indicator(width, SSM_HEADS, transposed=True)
        ind = _head_indicator(width, SSM_HEADS)
        dt_full = _mm01(dt, indT)
        e_full = jnp.exp(_mm01(cs, indT))
        cs_last = cs[CHUNK - 1:CHUNK, :]
        dte = jnp.exp(cs_last - cs)
        dte_full = _mm01(dte, indT)
        xs_ = xs_ref[...]
        xt = xs_ * dt_full
        dy_ = dy_ref[...]
        hmat = hs_ref[0]
        ds = dh_sc[...]
        lo = _iota((CHUNK, 2 * HEAD_DIM), 1) < HEAD_DIM
        head_lane = _iota((1, SSM_HEADS), 1)
        dcs = jnp.zeros((CHUNK, SSM_HEADS), F32)
        ddte = jnp.zeros((CHUNK, SSM_HEADS), F32)
        for g in range(SSM_GROUPS):
            sl = slice(g * gw, (g + 1) * gw)
            bg = bc_ref[:, g * ns:(g + 1) * ns]
            cg = bc_ref[:, (SSM_GROUPS + g) * ns:(SSM_GROUPS + g + 1) * ns]
            gm = _mm(cg, bg, NT)
            gmT = _mm(bg, cg, NT)
            hg = hmat[sl, :]
            dsg = ds[sl, :]
            dyg = dy_[:, sl]
            xtg = xt[:, sl]
            yoff = e_full[:, sl] * _mm(cg, hg, NT)
            edy = e_full[:, sl] * dyg
            bds = _mm(bg, dsg, NT)
            dxt_g = dte_full[:, sl] * bds
            ddte = ddte + _mm01(xtg * bds, ind[sl, :])
            dcs = dcs + _mm01(dyg * yoff, ind[sl, :])
            db = _mm(xtg * dte_full[:, sl], dsg)
            dc = _mm(edy, hg)
            dhc = _mm(edy, cg, TN)
            dgs = jnp.zeros((CHUNK, CHUNK), F32)
            dgTs = jnp.zeros((CHUNK, CHUNK), F32)
            for pr in range(hpg // 2):
                h0 = g * hpg + 2 * pr
                c0 = 2 * pr * HEAD_DIM
                xp = xtg[:, c0:c0 + 2 * HEAD_DIM]
                dyp = dyg[:, c0:c0 + 2 * HEAD_DIM]
                rr = []
                for h, half in ((h0, lo), (h0 + 1, jnp.logical_not(lo))):
                    lm, lmT = _decay(cs, csT, h, row, col)
                    xm = jnp.where(half, xp, 0.0)
                    dm = _mm(dyp, xm, NT)
                    dmT = _mm(xm, dyp, NT)
                    mT = gmT * lmT
                    z = jnp.sum(dm * (gm * lm), axis=1, keepdims=True) - jnp.sum(dmT * mT, axis=1, keepdims=True)
                    dcs = dcs + z * (head_lane == h).astype(F32)
                    dgs = dgs + dm * lm
                    dgTs = dgTs + dmT * lmT
                    rr.append(_mm(mT, dyp))
                dxt_sc[:, g * gw + c0:g * gw + c0 + 2 * HEAD_DIM] = jnp.where(lo, rr[0], rr[1]) + dxt_g[:, c0:c0 + 2 * HEAD_DIM]
            dbc_ref[:, g * ns:(g + 1) * ns] = db + _mm(dgTs, cg)
            dbc_ref[:, (SSM_GROUPS + g) * ns:(SSM_GROUPS + g + 1) * ns] = dc + _mm(dgs, bg)
            for hh in range(hpg):
                h = g * hpg + hh
                r0 = h * HEAD_DIM
                dh_sc[r0:r0 + HEAD_DIM, :] = (dhc[hh * HEAD_DIM:(hh + 1) * HEAD_DIM, :]
                                              + jnp.exp(csT[h:h + 1, CHUNK - 1:CHUNK]) * ds[r0:r0 + HEAD_DIM, :])
        t = ddte * dte
        per_head = jnp.sum(jnp.sum(ds * hmat, axis=1, keepdims=True) * ind, axis=0, keepdims=True)
        last_add = jnp.sum(t, axis=0, keepdims=True) + jnp.exp(cs_last) * per_head
        dcs = dcs - t + jnp.where(_iota((CHUNK, SSM_HEADS), 0) == CHUNK - 1, last_add, 0.0)
        dadt = _01mm((row <= col).astype(F32), dcs)
        dxt = dxt_sc[...]
        ddt = dadt * a_row + _mm01(dxt * xs_, ind)
        dxs_ref[...] = dxt * dt_full + skip_ref[...]
        ddtr = ddt * (1.0 - jnp.exp(-dt))
        ddtr_ref[...] = jnp.zeros_like(ddtr_ref)
        ddtr_ref[:, :SSM_HEADS] = ddtr.astype(ddtr_ref.dtype)
        _acc(dal_ref, i, jnp.sum(dadt * dt, axis=0, keepdims=True) * a_row)
        _acc(ddtb_ref, i, jnp.sum(ddtr, axis=0, keepdims=True))

    rev = lambda i: (nc - 1 - i, 0)
    return _call(body, name, (nc,),
                 [pl.BlockSpec((CHUNK, width), rev), pl.BlockSpec((CHUNK, bcw), rev), pl.BlockSpec((CHUNK, LANE), rev),
                  pl.BlockSpec((SSM_HEADS, CHUNK), lambda i: (0, nc - 1 - i)), _ps((1, LANE)), _ps((SSM_HEADS, 1)),
                  pl.BlockSpec((1, width, ns), lambda i: (nc - 1 - i, 0, 0)), pl.BlockSpec((CHUNK, width), rev),
                  pl.BlockSpec((CHUNK, width), rev)],
                 [pl.BlockSpec((CHUNK, width), rev), pl.BlockSpec((CHUNK, bcw), rev), pl.BlockSpec((CHUNK, LANE), rev),
                  _ps((1, SSM_HEADS)), _ps((1, SSM_HEADS))],
                 [S((n, width), F32), S((n, bcw), F32), S((n, LANE), MXU_DTYPE), S((1, SSM_HEADS), F32), S((1, SSM_HEADS), F32)],
                 scratch=[pltpu.VMEM((width, ns), F32), pltpu.VMEM((CHUNK, width), F32)],
                 sem=("arbitrary",))(xs, bc, dt, dtT, alog_row, alog_col, hs, dy, dxs_skip)


def ssm_gate_fwd(yssd, xs, proj, dfull, gamma, name):
    n, d = yssd.shape
    tile = _row_tile(n, 256)
    gw = d // SSM_GROUPS

    def body(y_ref, xs_ref, z_ref, df_ref, gm_ref, o_ref):
        z = z_ref[...]
        y2 = (y_ref[...] + df_ref[...] * xs_ref[...]) * (z * _sigmoid(z))
        for g in range(SSM_GROUPS):
            yg = y2[:, g * gw:(g + 1) * gw]
            r = lax.rsqrt(jnp.mean(yg * yg, axis=-1, keepdims=True) + EPS)
            o_ref[:, g * gw:(g + 1) * gw] = (yg * r * gm_ref[:, g * gw:(g + 1) * gw]).astype(o_ref.dtype)

    return _call(body, name, (n // tile,), [_rs(tile, d), _rs(tile, d), _rs(tile, d, 2), _ps((1, d)), _ps((1, d))],
                 _rs(tile, d), S((n, d), MXU_DTYPE), sem=("parallel",))(yssd, xs, proj, dfull, gamma)


def ssm_gate_bwd(dy3, yssd, xs, proj, dfull, gamma, dproj, name):
    n, d = yssd.shape
    tile = _row_tile(n, 256)
    gw = d // SSM_GROUPS

    def body(dy_ref, y_ref, xs_ref, z_ref, df_ref, gm_ref, dp_in_ref, dys_ref, dxs_ref, dz_ref, dgm_ref, dd_ref):
        i = pl.program_id(0)
        z = z_ref[...]
        s = _sigmoid(z)
        xs_ = xs_ref[...]
        y1 = y_ref[...] + df_ref[...] * xs_
        y2 = y1 * (z * s)
        dy_ = dy_ref[...].astype(F32)
        dgm = []
        dy2 = []
        for g in range(SSM_GROUPS):
            sl = slice(g * gw, (g + 1) * gw)
            dxg, dgg = _rms_bwd_math(y2[:, sl], gm_ref[:, sl], dy_[:, sl])
            dy2.append(dxg)
            dgm.append(dgg)
        dy2 = jnp.concatenate(dy2, axis=1)
        dy1 = dy2 * (z * s)
        dys_ref[...] = dy1
        dxs_ref[...] = dy1 * df_ref[...]
        dz_ref[...] = (dy2 * y1 * s * (1.0 + z * (1.0 - s))).astype(dz_ref.dtype)
        _acc(dgm_ref, i, jnp.concatenate(dgm, axis=1))
        colsum = jnp.broadcast_to(jnp.sum(dy1 * xs_, axis=0, keepdims=True), (8, d))
        _acc(dd_ref, i, _mm01(colsum, _head_indicator(d, SSM_HEADS))[0:1, :])

    return pl.pallas_call(
        body, name=name, grid=(n // tile,),
        in_specs=[_rs(tile, d), _rs(tile, d), _rs(tile, d), _rs(tile, d, 2), _ps((1, d)), _ps((1, d)), ANY],
        out_specs=[_rs(tile, d), _rs(tile, d), _rs(tile, d, 2), _ps((1, d)), _ps((1, SSM_HEADS))],
        out_shape=[S((n, d), F32), S((n, d), F32), S(dproj.shape, dproj.dtype), S((1, d), F32), S((1, SSM_HEADS), F32)],
        input_output_aliases={6: 2},
        compiler_params=pltpu.CompilerParams(dimension_semantics=("arbitrary",), vmem_limit_bytes=VMEM_LIMIT),
    )(dy3, yssd, xs, proj, dfull, gamma, dproj)


def _rope128(x, cos, sin_signed):
    half = HEAD_DIM // 2
    lane = _iota(x.shape, 1)
    partner = jnp.where((lane % HEAD_DIM) < half, pltpu.roll(x, LANE - half, 1), pltpu.roll(x, half, 1))
    return x * cos + partner * sin_signed


def rope_fwd(qkv, cos, sin, name):
    n, w = qkv.shape
    qw = ATT_HEADS * HEAD_DIM
    kw = ATT_KV_HEADS * HEAD_DIM
    tile = _row_tile(n, 256)

    def body(x_ref, c_ref, s_ref, q_ref, k_ref, v_ref):
        c, s = c_ref[...], s_ref[...]
        for j in range(qw // LANE):
            q_ref[:, j * LANE:(j + 1) * LANE] = _rope128(x_ref[:, j * LANE:(j + 1) * LANE], c, s).astype(q_ref.dtype)
        for j in range(kw // LANE):
            k_ref[:, j * LANE:(j + 1) * LANE] = _rope128(x_ref[:, qw + j * LANE:qw + (j + 1) * LANE], c, s).astype(k_ref.dtype)
        v_ref[...] = x_ref[:, qw + kw:].astype(v_ref.dtype)

    return _call(body, name, (n // tile,), [_rs(tile, w), _rs(tile, LANE), _rs(tile, LANE)],
                 [_rs(tile, qw), _rs(tile, kw), _rs(tile, kw)],
                 [S((n, qw), MXU_DTYPE), S((n, kw), MXU_DTYPE), S((n, kw), MXU_DTYPE)], sem=("parallel",))(qkv, cos, sin)


ATT_GROUP = ATT_HEADS // ATT_KV_HEADS


def _attn_mask(i):
    row = _iota((ATT_GROUP * WINDOW, 2 * WINDOW), 0) % WINDOW
    s = _iota((ATT_GROUP * WINDOW, 2 * WINDOW), 1)
    return (s > row) & (s <= row + WINDOW) & ((s >= WINDOW) | (i > 0))


def _stack_heads(ref, j, kh, lo):
    parts = []
    for t in range(ATT_GROUP):
        h = ATT_GROUP * j + t
        blk = ref[:, (h // 2) * LANE:(h // 2 + 1) * LANE]
        blk = jnp.where(lo if h % 2 == 0 else jnp.logical_not(lo), blk, jnp.zeros_like(blk))
        parts.append(blk if h % 2 == kh else pltpu.roll(blk, HEAD_DIM, 1))
    return jnp.concatenate(parts, axis=0)


def _unstack_heads(stacked, j, kh, lo, put):
    for t in range(0, ATT_GROUP, 2):
        h = ATT_GROUP * j + t
        even = stacked[t * WINDOW:(t + 1) * WINDOW, :]
        odd = stacked[(t + 1) * WINDOW:(t + 2) * WINDOW, :]
        even = even if kh == 0 else pltpu.roll(even, HEAD_DIM, 1)
        odd = odd if kh == 1 else pltpu.roll(odd, HEAD_DIM, 1)
        put(h // 2, jnp.where(lo, even, odd))


def _per_head_rows(ref, j):
    return jnp.concatenate([ref[:, ATT_GROUP * j + t:ATT_GROUP * j + t + 1] for t in range(ATT_GROUP)], axis=0)


def _per_head_scalar(ref, j):
    rows = _iota((ATT_GROUP * WINDOW, 1), 0) // WINDOW
    out = jnp.zeros((ATT_GROUP * WINDOW, 1), F32)
    for t in range(ATT_GROUP):
        out = out + jnp.where(rows == t, ref[:, ATT_GROUP * j + t:ATT_GROUP * j + t + 1], 0.0)
    return out


def attn_fwd(q, k, v, sinks, name):
    n, qw = q.shape
    kw = k.shape[1]
    nb = n // WINDOW
    scale = HEAD_DIM ** -0.5

    def body(q_ref, kc_ref, kp_ref, vc_ref, vp_ref, sk_ref, o_ref, lse_ref):
        i = pl.program_id(0)
        valid = _attn_mask(i)
        lo = _iota((WINDOW, LANE), 1) < HEAD_DIM
        k2 = jnp.concatenate([kp_ref[...], kc_ref[...]], axis=0)
        v2 = jnp.concatenate([vp_ref[...], vc_ref[...]], axis=0)
        lane1 = _iota((1, LANE), 1)
        lse = jnp.zeros((WINDOW, LANE), F32)

        def put_o(qb, val):
            o_ref[:, qb * LANE:(qb + 1) * LANE] = val.astype(o_ref.dtype)

        for j in range(ATT_KV_HEADS):
            kb, kh = j // 2, j % 2
            q4 = _stack_heads(q_ref, j, kh, lo)
            logits = jnp.where(valid, _mm(q4, k2[:, kb * LANE:(kb + 1) * LANE], NT) * scale, -1e30)
            sk = _per_head_scalar(sk_ref, j)
            m = jnp.maximum(jnp.max(logits, axis=-1, keepdims=True), sk)
            e = jnp.exp(logits - m)
            den = jnp.sum(e, axis=-1, keepdims=True) + jnp.exp(sk - m)
            lse4 = m + jnp.log(den)
            for t in range(ATT_GROUP):
                lse = lse + lse4[t * WINDOW:(t + 1) * WINDOW, :] * (lane1 == ATT_GROUP * j + t).astype(F32)
            _unstack_heads(_mm(e * (1.0 / den), v2[:, kb * LANE:(kb + 1) * LANE]), j, kh, lo, put_o)
        lse_ref[...] = lse

    return _call(body, name, (nb,),
                 [_rs(WINDOW, qw), _rs(WINDOW, kw), _rs(WINDOW, kw, 0, -1), _rs(WINDOW, kw), _rs(WINDOW, kw, 0, -1), _ps((1, LANE))],
                 [_rs(WINDOW, qw), _rs(WINDOW, LANE)], [S((n, qw), MXU_DTYPE), S((n, LANE), F32)],
                 sem=("parallel",))(q, k, k, v, v, sinks)


def attn_bwd(q, k, v, o, do, lse, sinks, name):
    n, qw = q.shape
    kw = k.shape[1]
    nb = n // WINDOW
    scale = HEAD_DIM ** -0.5

    def body(q_ref, kc_ref, kp_ref, vc_ref, vp_ref, o_ref, do_ref, lse_ref, sk_ref,
             dq_ref, dka_ref, dkb_ref, dva_ref, dvb_ref, dsk_ref):
        i = pl.program_id(0)
        valid = _attn_mask(i)
        lo = _iota((WINDOW, LANE), 1) < HEAD_DIM
        k2 = jnp.concatenate([kp_ref[...], kc_ref[...]], axis=0)
        v2 = jnp.concatenate([vp_ref[...], vc_ref[...]], axis=0)
        lane1 = _iota((1, LANE), 1)
        do_ = do_ref[...].astype(F32)
        delta = _mm01(do_ * o_ref[...].astype(F32), _head_indicator(qw, ATT_HEADS))
        dk2 = [jnp.zeros((2 * WINDOW, LANE), F32) for _ in range(kw // LANE)]
        dv2 = [jnp.zeros((2 * WINDOW, LANE), F32) for _ in range(kw // LANE)]
        dsk = jnp.zeros((1, LANE), F32)

        def put_dq(qb, val):
            dq_ref[:, qb * LANE:(qb + 1) * LANE] = val

        for j in range(ATT_KV_HEADS):
            kb, kh = j // 2, j % 2
            q4 = _stack_heads(q_ref, j, kh, lo)
            do4 = _stack_heads(do_ref, j, kh, lo)
            kk = k2[:, kb * LANE:(kb + 1) * LANE]
            vv = v2[:, kb * LANE:(kb + 1) * LANE]
            logits = jnp.where(valid, _mm(q4, kk, NT) * scale, -1e30)
            lse4 = _per_head_rows(lse_ref, j)
            p = jnp.exp(logits - lse4)
            dl = jnp.concatenate([delta[:, ATT_GROUP * j + t:ATT_GROUP * j + t + 1] for t in range(ATT_GROUP)], axis=0)
            ds = p * (_mm(do4, vv, NT) - dl) * scale
            sd = jnp.exp(_per_head_scalar(sk_ref, j) - lse4) * dl
            for t in range(ATT_GROUP):
                dsk = dsk - (jnp.sum(sd[t * WINDOW:(t + 1) * WINDOW, :], axis=0, keepdims=True)
                             * (lane1 == ATT_GROUP * j + t).astype(F32))
            _unstack_heads(_mm(ds, kk), j, kh, lo, put_dq)
            dk2[kb] = dk2[kb] + _mm(ds, q4, TN)
            dv2[kb] = dv2[kb] + _mm(p, do4, TN)
        for kb in range(kw // LANE):
            dkb_ref[:, kb * LANE:(kb + 1) * LANE] = dk2[kb][0:WINDOW, :]
            dka_ref[:, kb * LANE:(kb + 1) * LANE] = dk2[kb][WINDOW:, :]
            dvb_ref[:, kb * LANE:(kb + 1) * LANE] = dv2[kb][0:WINDOW, :]
            dva_ref[:, kb * LANE:(kb + 1) * LANE] = dv2[kb][WINDOW:, :]
        _acc(dsk_ref, i, dsk)

    return _call(body, name, (nb,),
                 [_rs(WINDOW, qw), _rs(WINDOW, kw), _rs(WINDOW, kw, 0, -1), _rs(WINDOW, kw), _rs(WINDOW, kw, 0, -1),
                  _rs(WINDOW, qw), _rs(WINDOW, qw), _rs(WINDOW, LANE), _ps((1, LANE))],
                 [_rs(WINDOW, qw)] + [_rs(WINDOW, kw)] * 4 + [_ps((1, LANE))],
                 [S((n, qw), F32)] + [S((n, kw), F32)] * 4 + [S((1, LANE), F32)],
                 sem=("arbitrary",))(q, k, k, v, v, o, do, lse, sinks)


def attn_grad_merge(dq, dka, dkb, dva, dvb, cos, sin, name):
    n, qw = dq.shape
    kw = dka.shape[1]
    nb = n // WINDOW
    w = qw + 2 * kw

    def body(dq_ref, dka_ref, dkb_ref, dva_ref, dvb_ref, c_ref, s_ref, o_ref, db_ref):
        i = pl.program_id(0)
        c, s = c_ref[...], -s_ref[...]
        nxt = (i < nb - 1).astype(F32)

        @pl.when(i == 0)
        def _():
            db_ref[...] = jnp.zeros_like(db_ref)

        def put(c0, val):
            o_ref[:, c0:c0 + val.shape[1]] = val.astype(o_ref.dtype)
            db_ref[:, c0:c0 + val.shape[1]] += jnp.sum(val, axis=0, keepdims=True)

        for j in range(qw // LANE):
            put(j * LANE, _rope128(dq_ref[:, j * LANE:(j + 1) * LANE], c, s))
        for j in range(kw // LANE):
            sl = slice(j * LANE, (j + 1) * LANE)
            put(qw + j * LANE, _rope128(dka_ref[:, sl] + dkb_ref[:, sl] * nxt, c, s))
        put(qw + kw, dva_ref[...] + dvb_ref[...] * nxt)

    return _call(body, name, (nb,),
                 [_rs(WINDOW, qw), _rs(WINDOW, kw), _rs(WINDOW, kw, 0, 1, nb), _rs(WINDOW, kw), _rs(WINDOW, kw, 0, 1, nb),
                  _rs(WINDOW, LANE), _rs(WINDOW, LANE)],
                 [_rs(WINDOW, w), _ps((1, w))], [S((n, w), MXU_DTYPE), S((1, w), F32)],
                 sem=("arbitrary",))(dq, dka, dkb, dva, dvb, cos, sin)


def _row(v):
    return v.reshape(1, -1)


def _pad_lanes(v, width=LANE):
    return jnp.pad(v.reshape(1, -1), ((0, 0), (0, width - v.size)))


def ffn_fwd(h, g, w_in, w_out, tag):
    xn, u, hm = swiglu_in(h, _row(g), w_in, f"{tag}_in")
    return matmul(hm, w_out, "nn", f"{tag}_out", scale=0.5, res=h), (h, xn, u, hm)


class GradSink:
    ORDER = ("ffn1_w_out", "ffn2_w_out", "ple_gate_w", "att_w_o", "hyb_w_out", "ffn1_w_in", "ffn2_w_in", "att_w_qkv",
             "ple_proj_w", "hyb_w_in")

    def __init__(self, shard_shapes, depth):
        self.where, rows = {}, [{} for _ in range(depth)]
        for k in self.ORDER:
            n, r, c = shard_shapes[k]
            for li in range(n):
                layer = li if k in PER_LAYER else 2 * li + (0 if k in EVEN_ONLY else 1)
                off = -(-rows[layer].get(c, 0) // r) * r
                rows[layer][c] = off + r
                self.where[k, li] = (layer, c, "r" if SHARD_AXIS[k] == 1 else "c", off, r)
        self.bufs = [{c: lax.empty((N_CHIPS, r, c), F32) for c, r in rows[layer].items()} for layer in range(depth)]

    def mm(self, k, li, a, b, name, scale=None, c0=0, paired=False):
        layer, c, kind, off, r = self.where[k, li]
        slot = Slot(self.bufs[layer][c], kind, r if kind == "r" else c, off, c0, paired)
        self.bufs[layer][c] = matmul(a, b, "tn", name, scale=scale, into=slot)

    def put(self, k, li, chip_major):
        layer, c = self.where[k, li][:2]
        self.bufs[layer][c] = chip_major

    def take(self, k, n, reduced):
        parts = []
        for li in range(n):
            layer, c, _, off, r = self.where[k, li]
            parts.append(reduced[layer][c][off:off + r])
        return jnp.stack(parts)


def ffn_bwd(dh, g, w_in, w_out, saved, tag, sink, keys, layer, colsum=False):
    h, xn, u, hm = saved
    sink.mm(keys[1], layer, hm, dh, f"{tag}_dwout", scale=0.5)
    du = swiglu_out_bwd(dh, w_out, u, f"{tag}_dhm")
    sink.mm(keys[0], layer, xn, du, f"{tag}_dwin", paired=True)
    dxn = matmul(du, ColSharded(w_in.arr, paired=True), "nt", f"{tag}_dxn")
    outs = rms_bwd(h, _row(g), dxn, dh, f"{tag}_drms", colsum=colsum)
    return (outs[0], outs[1].reshape(-1)) + ((outs[2],) if colsum else ())


def _hyb_params(w):
    d = w["conv_dw_b"].size
    inner = SSM_HEADS * HEAD_DIM
    main = 3 * d + w["ssm_conv_b"].size
    return dict(
        w_main=w["hyb_w_in"][:, :main], w_dt=jnp.pad(w["hyb_w_in"][:, main:], ((0, 0), (0, LANE - SSM_HEADS))),
        cw=jnp.pad(w["conv_dw_w"], ((0, 32 - CONV_WIDTH), (0, 0))), cb=_row(w["conv_dw_b"]),
        lg=_row(w["conv_ln_g"]), lb=_row(w["conv_ln_b"]),
        sw=jnp.pad(w["ssm_conv_w"], ((0, 8 - SSM_CONV), (0, 0))), sb=_row(w["ssm_conv_b"]),
        dtb=_pad_lanes(w["ssm_dt_bias"]), al_row=_pad_lanes(w["ssm_a_log"]), al_col=w["ssm_a_log"].reshape(-1, 1),
        dfull=_row(jnp.repeat(w["ssm_d"], HEAD_DIM)), gamma=_row(w["ssm_norm"]),
        wo_top=w["hyb_w_out"][:d], wo_bot=w["hyb_w_out"][d:], d=d, inner=inner, main=main)


def hyb_fwd(h, w, tag):
    q = _hyb_params(w)
    xn = rms_fwd(h, _row(w["norm_mix"]), f"{tag}_rms")
    proj = matmul(xn, q["w_main"], "nn", f"{tag}_in")
    dtr = matmul(xn, q["w_dt"], "nn", f"{tag}_in_dt")
    u, u1 = conv_group_fwd(proj, q["cw"], q["cb"], q["lg"], q["lb"], f"{tag}_conv")
    pre, xs, bc, dt = ssm_conv_fwd(proj, dtr, q["sw"], q["sb"], q["dtb"], f"{tag}_sconv")
    dtT = dt[:, :SSM_HEADS].T
    yssd, hs = ssd_fwd(xs, bc, dt, dtT, q["al_row"], q["al_col"], f"{tag}_ssd")
    y = ssm_gate_fwd(yssd, xs, proj, q["dfull"], q["gamma"], f"{tag}_gate")
    h2 = matmul(u, q["wo_top"], "nn", f"{tag}_out_a", res=h)
    h2 = matmul(y, q["wo_bot"], "nn", f"{tag}_out_b", res=h2)
    return h2, (h, xn, proj, u, u1, pre, xs, bc, dt, dtT, yssd, hs, y)


def hyb_bwd(dh, w, saved, tag, sink, layer):
    q = _hyb_params(w)
    h, xn, proj, u, u1, pre, xs, bc, dt, dtT, yssd, hs, y = saved
    du = matmul(dh, q["wo_top"], "nt", f"{tag}_du")
    dy3 = matmul(dh, q["wo_bot"], "nt", f"{tag}_dy")
    sink.mm("hyb_w_out", layer, u, dh, f"{tag}_dwo_a", c0=0)
    sink.mm("hyb_w_out", layer, y, dh, f"{tag}_dwo_b", c0=N_CHIPS // 2)
    dproj, dcw, dcb, dlg, dlb = conv_group_bwd(du, u1, proj, q["cw"], q["lg"], q["lb"], f"{tag}_dconv")
    dyssd, dxs_skip, dproj, dgamma, dd = ssm_gate_bwd(dy3, yssd, xs, proj, q["dfull"], q["gamma"], dproj, f"{tag}_dgate")
    dxs, dbc, ddtr, dalog, ddtb = ssd_bwd(xs, bc, dt, dtT, q["al_row"], q["al_col"], hs, dyssd, dxs_skip, f"{tag}_dssd")
    dproj, dsw, dsb = ssm_conv_bwd(dxs, dbc, pre, proj, q["sw"], dproj, f"{tag}_dsconv")
    dw_in = jnp.concatenate([matmul(xn, dproj, "tn", f"{tag}_dwin"),
                             matmul(xn, ddtr, "tn", f"{tag}_dwin_dt")[:, :SSM_HEADS]], axis=1)
    sink.put("hyb_w_in", layer, _to_chip_major(dw_in, 1))
    dxn = matmul(dproj, q["w_main"], "nt", f"{tag}_dxn")
    dxn = matmul(ddtr, q["w_dt"], "nt", f"{tag}_dxn_dt", res=dxn)
    dh2, dg = rms_bwd(h, _row(w["norm_mix"]), dxn, dh, f"{tag}_drms")
    grads = dict(norm_mix=dg.reshape(-1), conv_dw_w=dcw[:CONV_WIDTH], conv_dw_b=dcb.reshape(-1),
                 conv_ln_g=dlg.reshape(-1), conv_ln_b=dlb.reshape(-1), ssm_conv_w=dsw[:SSM_CONV], ssm_conv_b=dsb.reshape(-1),
                 ssm_dt_bias=ddtb.reshape(-1), ssm_a_log=dalog.reshape(-1), ssm_d=dd.reshape(-1), ssm_norm=dgamma.reshape(-1))
    return dh2, grads


def rope_tables(n):
    half = HEAD_DIM // 2
    inv = ROPE_THETA ** (-jnp.arange(0, HEAD_DIM, 2, dtype=F32) / HEAD_DIM)
    ang = jnp.arange(n, dtype=F32)[:, None] * inv[None, :]
    cos, sin = jnp.cos(ang), jnp.sin(ang)
    reps = LANE // HEAD_DIM
    return jnp.tile(jnp.concatenate([cos, cos], axis=1), (1, reps)), jnp.tile(jnp.concatenate([-sin, sin], axis=1), (1, reps))


def att_fwd(h, w, tables, tag):
    cos, sin = tables
    xn = rms_fwd(h, _row(w["norm_mix"]), f"{tag}_rms")
    qkv = matmul(xn, w["att_w_qkv"], "nn", f"{tag}_qkv", bias=_row(w["att_b_qkv"]))
    q, k, v = rope_fwd(qkv, cos, sin, f"{tag}_rope")
    sinks = _pad_lanes(w["att_sinks"])
    o, lse = attn_fwd(q, k, v, sinks, f"{tag}_attn")
    h2 = matmul(o, w["att_w_o"], "nn", f"{tag}_o", bias=_row(w["att_b_o"]), res=h)
    return h2, (h, xn, q, k, v, o, lse, sinks)


def att_bwd(dh, dh_colsum, w, saved, tables, tag, sink, layer):
    cos, sin = tables
    h, xn, q, k, v, o, lse, sinks = saved
    do = matmul(dh, w["att_w_o"], "nt", f"{tag}_do")
    sink.mm("att_w_o", layer, o, dh, f"{tag}_dwo")
    dq, dka, dkb, dva, dvb, dsk = attn_bwd(q, k, v, o, do, lse, sinks, f"{tag}_dattn")
    dqkv, dbqkv = attn_grad_merge(dq, dka, dkb, dva, dvb, cos, sin, f"{tag}_drope")
    sink.mm("att_w_qkv", layer, xn, dqkv, f"{tag}_dwqkv")
    dxn = matmul(dqkv, w["att_w_qkv"], "nt", f"{tag}_dxn")
    dh2, dg = rms_bwd(h, _row(w["norm_mix"]), dxn, dh, f"{tag}_drms")
    grads = dict(norm_mix=dg.reshape(-1), att_b_qkv=dbqkv.reshape(-1), att_sinks=dsk[0, :ATT_HEADS],
                 att_b_o=dh_colsum.reshape(-1))
    return dh2, grads


def ple_block_fwd(h, pe, w, tag):
    xn = rms_fwd(h, _row(w["ple_norm"]), f"{tag}_rms")
    gl = matmul(xn, w["ple_gate_w"], "nn", f"{tag}_gate")
    pp = matmul(pe, w["ple_proj_w"], "nn", f"{tag}_proj")
    return ple_fwd(h, gl, pp, f"{tag}_mix"), (h, xn, gl, pp, pe)


def ple_block_bwd(dh, w, saved, tag, sink, layer):
    h, xn, gl, pp, pe = saved
    dpp, dgl = ple_bwd(dh, gl, pp, f"{tag}_dmix")
    sink.mm("ple_proj_w", layer, pe, dpp, f"{tag}_dwp")
    sink.mm("ple_gate_w", layer, xn, dgl, f"{tag}_dwg")
    dxn = matmul(dgl, w["ple_gate_w"], "nt", f"{tag}_dxn")
    dh2, dg = rms_bwd(h, _row(w["ple_norm"]), dxn, dh, f"{tag}_drms")
    return dh2, dict(ple_norm=dg.reshape(-1))


PER_LAYER = ("norm_ffn1", "ffn1_w_in", "ffn1_w_out", "norm_mix", "norm_ffn2", "ffn2_w_in", "ffn2_w_out",
             "ple_norm", "ple_gate_w", "ple_proj_w")
EVEN_ONLY = ("hyb_w_in", "conv_dw_w", "conv_dw_b", "conv_ln_g", "conv_ln_b", "ssm_conv_w", "ssm_conv_b",
             "ssm_dt_bias", "ssm_a_log", "ssm_d", "ssm_norm", "hyb_w_out")
ODD_ONLY = ("att_w_qkv", "att_b_qkv", "att_sinks", "att_w_o", "att_b_o")


def _layer_index(k, i):
    if k in PER_LAYER:
        return i
    if k in (EVEN_ONLY if i % 2 == 0 else ODD_ONLY):
        return i // 2
    return None


def _stage(k):
    return 0 if k.startswith("ffn1") else (2 if k.startswith(("ffn2", "ple")) else 1)


def trunk_fwd_bwd(x, pe, target, layers, final_norm, sink, fetch, layer_done):
    depth = len(layers)
    tables = rope_tables(x.shape[0])
    h = x
    saved = []
    for i, w in enumerate(layers):
        w.update(fetch(i, 0, h))
        h, s1 = ffn_fwd(h, w["norm_ffn1"], w["ffn1_w_in"], w["ffn1_w_out"], f"l{i}_ffn1")
        w.update(fetch(i, 1, h))
        if i % 2 == 0:
            h, s2 = hyb_fwd(h, w, f"l{i}_hyb")
        else:
            h, s2 = att_fwd(h, w, tables, f"l{i}_att")
        w.update(fetch(i, 2, h))
        h, s3 = ffn_fwd(h, w["norm_ffn2"], w["ffn2_w_in"], w["ffn2_w_out"], f"l{i}_ffn2")
        h, s4 = ple_block_fwd(h, pe[i], w, f"l{i}_ple")
        saved.append((s1, s2, s3, s4))
    dh, dgf, loss = loss_head(h, _row(final_norm), target, "loss_head")
    grads = {}
    for i in reversed(range(depth)):
        w = layers[i]
        s1, s2, s3, s4 = saved[i]
        dh, g = ple_block_bwd(dh, w, s4, f"l{i}_ple", sink, i)
        odd = i % 2 == 1
        out = ffn_bwd(dh, w["norm_ffn2"], w["ffn2_w_in"], w["ffn2_w_out"], s3, f"l{i}_ffn2", sink,
                      ("ffn2_w_in", "ffn2_w_out"), i, colsum=odd)
        dh = out[0]
        g.update(norm_ffn2=out[1])
        if odd:
            dh, gm = att_bwd(dh, out[2], w, s2, tables, f"l{i}_att", sink, i // 2)
        else:
            dh, gm = hyb_bwd(dh, w, s2, f"l{i}_hyb", sink, i // 2)
        g.update(gm)
        out = ffn_bwd(dh, w["norm_ffn1"], w["ffn1_w_in"], w["ffn1_w_out"], s1, f"l{i}_ffn1", sink,
                      ("ffn1_w_in", "ffn1_w_out"), i)
        dh = out[0]
        g.update(norm_ffn1=out[1])
        layer_done(i)
        for k, v in g.items():
            grads.setdefault(k, []).insert(0, v)
    grads = {k: jnp.stack(v) for k, v in grads.items()}
    grads["final_norm"] = dgf.reshape(-1)
    return loss, dh, grads


def _me():
    return lax.axis_index("x"), lax.axis_index("y"), lax.axis_index("c")


def _flip(v, f):
    return 1 - v if f else v


def _remote(src, dst, send_sems, recv_sems, k, dev):
    return pltpu.make_async_remote_copy(src_ref=src, dst_ref=dst, send_sem=send_sems.at[k], recv_sem=recv_sems.at[k],
                                        device_id=dev, device_id_type=MESH)


CHIP_FLIPS = ((1, 0), (0, 1), (1, 1))
DEV_FLIPS = tuple((fx, fy, fc) for fx in (0, 1) for fy in (0, 1) for fc in (0, 1))[1:]


def all_gather_chips(xs, name):
    na = len(xs)
    halves = [x.shape[0] // 2 for x in xs]
    assert all(x.shape[0] % 2 == 0 for x in xs)

    def body(*refs):
        x_refs, out_refs = refs[:na], refs[na:2 * na]
        send_sems, recv_sems = refs[2 * na:]
        mx, my, mc = _me()
        chip = 2 * mx + my
        sib = (mx, my, 1 - mc)
        peers = [(_flip(mx, fx), _flip(my, fy)) for fx, fy in CHIP_FLIPS]

        def rows(a, ch, hc):
            return out_refs[a].at[ch, pl.ds(hc * halves[a], halves[a]), :]

        def src(a):
            return x_refs[a].at[pl.ds(mc * halves[a], halves[a]), :]

        first = [_remote(src(a), rows(a, chip, mc), send_sems, recv_sems, 6 * a + j, (px, py, mc))
                 for j, (px, py) in enumerate(peers) for a in range(na)]
        for cp in first:
            cp.start()
        passed = []
        for j, (px, py) in enumerate(peers):
            for a in range(na):
                landed = rows(a, 2 * px + py, mc)
                _remote(src(a), landed, send_sems, recv_sems, 6 * a + j, (px, py, mc)).wait_recv()
                fw = _remote(landed, landed, send_sems, recv_sems, 6 * a + 3 + j, sib)
                fw.start()
                passed.append(fw)
        for j, (px, py) in enumerate(peers):
            for a in range(na):
                _remote(src(a), rows(a, 2 * px + py, 1 - mc), send_sems, recv_sems, 6 * a + 3 + j, sib).wait_recv()
        for cp in first + passed:
            cp.wait_send()

    outs = pl.pallas_call(
        body, name=name, out_shape=[S((N_CHIPS,) + x.shape, x.dtype) for x in xs], in_specs=[ANY] * na, out_specs=[ANY] * na,
        scratch_shapes=[pltpu.SemaphoreType.DMA((6 * na,)), pltpu.SemaphoreType.DMA((6 * na,))])(*xs)
    chip = 2 * lax.axis_index("x") + lax.axis_index("y")
    return [lax.dynamic_update_slice_in_dim(o, x[None], chip, axis=0) for o, x in zip(outs, xs)]


HBM = pl.BlockSpec(memory_space=pltpu.HBM)
SEM = pl.BlockSpec(memory_space=pltpu.SEMAPHORE)
DATAFLOW = pltpu.SideEffectType.DATAFLOW_SIDE_EFFECTING


def gather_start(xs, lands, after, name):
    na = len(xs)

    def body(*refs):
        x_refs, land_refs = refs[:na], refs[na:2 * na]
        send_sems, recv_sems = refs[2 * na + 1], refs[2 * na + 2]
        token = refs[-1]
        mx, my, mc = _me()
        chip = 2 * mx + my
        for a in range(na):
            for j, (fx, fy) in enumerate(CHIP_FLIPS):
                _remote(x_refs[a], land_refs[a].at[chip], send_sems, recv_sems, 3 * a + j,
                        (_flip(mx, fx), _flip(my, fy), mc)).start()
        token[...] = jnp.zeros_like(token)

    outs = pl.pallas_call(
        body, name=name,
        out_shape=(pltpu.SemaphoreType.DMA((3 * na,)), pltpu.SemaphoreType.DMA((3 * na,)))
        + tuple(pltpu.HBM(x.shape, x.dtype) for x in xs) + tuple(pltpu.HBM(l.shape, l.dtype) for l in lands)
        + (S((8, LANE), F32),),
        in_specs=[HBM] * (2 * na) + [pl.BlockSpec(memory_space=pl.ANY)],
        out_specs=(SEM, SEM) + (HBM,) * (2 * na) + (pl.BlockSpec(memory_space=pltpu.VMEM),),
        input_output_aliases={a: 2 + a for a in range(2 * na)},
        compiler_params=pltpu.CompilerParams(has_side_effects=DATAFLOW),
    )(*[pltpu.with_memory_space_constraint(t, pltpu.HBM) for t in list(xs) + list(lands)], after)
    return outs[0], outs[1], list(outs[2:2 + na]), list(outs[2 + na:2 + 2 * na])


def gather_wait(send_sems, recv_sems, xs, lands, first, after, name):
    na = len(xs)

    def body(*refs):
        x_refs, land_refs = refs[:na], refs[na:2 * na]
        send_sems, recv_sems = refs[2 * na], refs[2 * na + 1]
        mx, my, mc = _me()
        for a in range(na):
            for j, (fx, fy) in enumerate(CHIP_FLIPS):
                px, py = _flip(mx, fx), _flip(my, fy)
                cp = _remote(x_refs[a], land_refs[a].at[2 * px + py], send_sems, recv_sems, 3 * (first + a) + j, (px, py, mc))
                cp.wait_send()
                cp.wait_recv()

    outs = pl.pallas_call(
        body, name=name,
        out_shape=tuple(pltpu.HBM(x.shape, x.dtype) for x in xs) + tuple(pltpu.HBM(l.shape, l.dtype) for l in lands),
        in_specs=[HBM] * (2 * na) + [SEM, SEM, pl.BlockSpec(memory_space=pl.ANY)], out_specs=(HBM,) * (2 * na),
        input_output_aliases={a: a for a in range(2 * na)},
        compiler_params=pltpu.CompilerParams(has_side_effects=DATAFLOW),
    )(*xs, *lands, send_sems, recv_sems, after)
    return list(outs[na:])


def all_gather_devices(v, name):
    r, l = v.shape

    def body(v_ref, out_ref, send_sems, recv_sems):
        mx, my, mc = _me()
        me = 4 * mx + 2 * my + mc
        peers = [(_flip(mx, fx), _flip(my, fy), _flip(mc, fc)) for fx, fy, fc in DEV_FLIPS]
        sends = [_remote(v_ref, out_ref.at[me], send_sems, recv_sems, j, p) for j, p in enumerate(peers)]
        for cp in sends:
            cp.start()
        for j, (px, py, pc) in enumerate(peers):
            _remote(v_ref, out_ref.at[4 * px + 2 * py + pc], send_sems, recv_sems, j, (px, py, pc)).wait_recv()
        for cp in sends:
            cp.wait_send()

    out = pl.pallas_call(
        body, name=name, out_shape=S((N_DEV, r, l), v.dtype), in_specs=[ANY], out_specs=ANY,
        scratch_shapes=[pltpu.SemaphoreType.DMA((7,)), pltpu.SemaphoreType.DMA((7,))])(v)
    me = 4 * lax.axis_index("x") + 2 * lax.axis_index("y") + lax.axis_index("c")
    return lax.dynamic_update_slice_in_dim(out, v[None], me, axis=0)


def sum_devices(g8, name):
    nd, r, l = g8.shape
    tile = r
    for t in (512, 256, 128, 64, 32, 16, 8):
        if r % t == 0:
            tile = t
            break

    def body(g_ref, o_ref):
        acc = g_ref[0]
        for d in range(1, nd):
            acc = acc + g_ref[d]
        o_ref[...] = acc

    return _call(body, name, (r // tile,), [pl.BlockSpec((nd, tile, l), lambda i: (0, i, 0))], _rs(tile, l), S((r, l), F32),
                 sem=("parallel",))(g8)


def exchange_halves(gs, name):
    na = len(gs)
    nch = gs[0].shape[0]

    def body(*refs):
        g_refs, out_refs = refs[:na], refs[na:2 * na]
        send_sems, recv_sems = refs[2 * na:]
        mx, my, mc = _me()
        sib = (mx, my, 1 - mc)
        cps = []
        for a in range(na):
            half = gs[a].shape[1] // 2
            for j in range(nch):
                cps.append(_remote(g_refs[a].at[j, pl.ds((1 - mc) * half, half), :], out_refs[a].at[j],
                                   send_sems, recv_sems, nch * a + j, sib))
        for cp in cps:
            cp.start()
        for cp in cps:
            cp.wait_recv()
        for cp in cps:
            cp.wait_send()

    return pl.pallas_call(
        body, name=name, out_shape=[S((nch, g.shape[1] // 2, g.shape[2]), g.dtype) for g in gs],
        in_specs=[ANY] * na, out_specs=[ANY] * na,
        scratch_shapes=[pltpu.SemaphoreType.DMA((nch * na,)), pltpu.SemaphoreType.DMA((nch * na,))])(*gs)


def add_halves(g4, got, name):
    nch, r, l = g4.shape
    half = r // 2
    tile = _pick_rows(half)
    nt = half // tile

    def body(g_ref, r_ref, a_ref, own_ref):
        j = pl.program_id(1)
        chip = 2 * lax.axis_index("x") + lax.axis_index("y")
        val = g_ref[0] + r_ref[0]
        a_ref[0] = val.astype(a_ref.dtype)

        @pl.when(j == chip)
        def _():
            own_ref[...] = val

    return pl.pallas_call(
        body, name=name, grid=(nt, nch),
        in_specs=[pl.BlockSpec((1, tile, l), lambda i, j: (j, lax.axis_index("c") * nt + i, 0)),
                  pl.BlockSpec((1, tile, l), lambda i, j: (j, i, 0))],
        out_specs=[pl.BlockSpec((1, tile, l), lambda i, j: (j, i, 0)), pl.BlockSpec((tile, l), lambda i, j: (i, 0))],
        out_shape=[S((nch, half, l), MXU_DTYPE), S((half, l), F32)],
        compiler_params=pltpu.CompilerParams(dimension_semantics=("parallel", "arbitrary"), vmem_limit_bytes=VMEM_LIMIT))(g4, got)


def _pick_rows(r, cap=512):
    for t in (512, 256, 128, 64, 32, 16):
        if t <= cap and r % t == 0:
            return t
    return r


def exchange_chips(parts, name):
    na = len(parts)

    def body(*refs):
        a_refs, out_refs = refs[:na], refs[na:2 * na]
        send_sems, recv_sems = refs[2 * na:]
        mx, my, mc = _me()
        peers = [(_flip(mx, fx), _flip(my, fy)) for fx, fy in CHIP_FLIPS]
        cps = [_remote(a_refs[a].at[2 * px + py], out_refs[a].at[j], send_sems, recv_sems, 3 * a + j, (px, py, mc))
               for j, (px, py) in enumerate(peers) for a in range(na)]
        for cp in cps:
            cp.start()
        for cp in cps:
            cp.wait_recv()
        for cp in cps:
            cp.wait_send()

    return pl.pallas_call(
        body, name=name, out_shape=[S((3,) + p.shape[1:], p.dtype) for p in parts], in_specs=[ANY] * na, out_specs=[ANY] * na,
        scratch_shapes=[pltpu.SemaphoreType.DMA((3 * na,)), pltpu.SemaphoreType.DMA((3 * na,))])(*parts)


def add_chips(own, got, name):
    h, l = own.shape
    tile = _pick_rows(h)

    def body(o_ref, g_ref, out_ref):
        out_ref[...] = ((o_ref[...] + g_ref[0].astype(F32)) + g_ref[1].astype(F32)) + g_ref[2].astype(F32)

    nt = h // tile
    return _call(body, name, (nt,), [_rs(tile, l), pl.BlockSpec((3, tile, l), lambda i: (0, i, 0))],
                 pl.BlockSpec((tile, l), lambda i: (lax.axis_index("c") * nt + i, 0)),
                 S((2 * h, l), F32), sem=("parallel",))(own, got)


def join_halves(bufs, name):
    na = len(bufs)

    def body(*refs):
        out_refs = refs[na:2 * na]
        send_sems, recv_sems = refs[2 * na:]
        mx, my, mc = _me()
        sib = (mx, my, 1 - mc)

        def half(a, hc):
            h = bufs[a].shape[0] // 2
            return out_refs[a].at[pl.ds(hc * h, h), :]

        cps = [_remote(half(a, mc), half(a, mc), send_sems, recv_sems, a, sib) for a in range(na)]
        for cp in cps:
            cp.start()
        for a in range(na):
            _remote(half(a, mc), half(a, 1 - mc), send_sems, recv_sems, a, sib).wait_recv()
        for cp in cps:
            cp.wait_send()

    return pl.pallas_call(
        body, name=name, out_shape=[S(b.shape, b.dtype) for b in bufs], in_specs=[ANY] * na, out_specs=[ANY] * na,
        input_output_aliases={a: a for a in range(na)},
        scratch_shapes=[pltpu.SemaphoreType.DMA((na,)), pltpu.SemaphoreType.DMA((na,))])(*bufs)


def exchange_chips_start(parts, name):
    na = len(parts)
    lands = [lax.empty((3,) + p.shape[1:], p.dtype) for p in parts]

    def body(*refs):
        a_refs, land_refs = refs[:na], refs[na:2 * na]
        send_sems, recv_sems = refs[2 * na], refs[2 * na + 1]
        mx, my, mc = _me()
        for j, (fx, fy) in enumerate(CHIP_FLIPS):
            px, py = _flip(mx, fx), _flip(my, fy)
            for a in range(na):
                _remote(a_refs[a].at[2 * px + py], land_refs[a].at[j], send_sems, recv_sems, 3 * a + j, (px, py, mc)).start()
        refs[-1][...] = jnp.zeros_like(refs[-1])

    outs = pl.pallas_call(
        body, name=name,
        out_shape=(pltpu.SemaphoreType.DMA((3 * na,)), pltpu.SemaphoreType.DMA((3 * na,)))
        + tuple(pltpu.HBM(t.shape, t.dtype) for t in list(parts) + lands) + (S((8, LANE), F32),),
        in_specs=[HBM] * (2 * na), out_specs=(SEM, SEM) + (HBM,) * (2 * na) + (pl.BlockSpec(memory_space=pltpu.VMEM),),
        input_output_aliases={a: 2 + a for a in range(2 * na)},
        compiler_params=pltpu.CompilerParams(has_side_effects=DATAFLOW),
    )(*[pltpu.with_memory_space_constraint(t, pltpu.HBM) for t in list(parts) + lands])
    return outs[0], outs[1], list(outs[2:2 + na]), list(outs[2 + na:2 + 2 * na])


def exchange_chips_wait(send_sems, recv_sems, parts, lands, after, name):
    na = len(parts)

    def body(*refs):
        a_refs, land_refs = refs[:na], refs[na:2 * na]
        send_sems, recv_sems = refs[2 * na], refs[2 * na + 1]
        mx, my, mc = _me()
        for j, (fx, fy) in enumerate(CHIP_FLIPS):
            px, py = _flip(mx, fx), _flip(my, fy)
            for a in range(na):
                cp = _remote(a_refs[a].at[2 * px + py], land_refs[a].at[j], send_sems, recv_sems, 3 * a + j, (px, py, mc))
                cp.wait_send()
                cp.wait_recv()

    outs = pl.pallas_call(
        body, name=name, out_shape=tuple(pltpu.HBM(t.shape, t.dtype) for t in list(parts) + list(lands)),
        in_specs=[HBM] * (2 * na) + [SEM, SEM, pl.BlockSpec(memory_space=pl.ANY)], out_specs=(HBM,) * (2 * na),
        input_output_aliases={a: a for a in range(2 * na)},
        compiler_params=pltpu.CompilerParams(has_side_effects=DATAFLOW),
    )(*parts, *lands, send_sems, recv_sems, after)
    return list(outs[na:])


def reduce_begin(gs, tag):
    got = exchange_halves(gs, f"{tag}_d2d")
    sums = [add_halves(g, r, f"{tag}_add1_{i}") for i, (g, r) in enumerate(zip(gs, got))]
    return [own for _, own in sums], exchange_chips_start([a for a, _ in sums], f"{tag}_ici_start")


def reduce_end(state, after, tag):
    owns, (send_sems, recv_sems, parts, lands) = state
    got = exchange_chips_wait(send_sems, recv_sems, parts, lands, after, f"{tag}_ici_wait")
    return join_halves([add_chips(own, r, f"{tag}_add2_{i}") for i, (own, r) in enumerate(zip(owns, got))], f"{tag}_join")


PACK_L = 1024
BIG_ROW_MULT = 512


def _pack(arrs, dtype, row_mult, lead=None):
    lead_shape = () if lead is None else arrs[0].shape[:lead]
    flat = jnp.concatenate([a.astype(dtype).reshape(lead_shape + (-1,)) for a in arrs], axis=-1)
    n = flat.shape[-1]
    unit = row_mult * PACK_L
    total = -(-n // unit) * unit
    flat = jnp.pad(flat, [(0, 0)] * len(lead_shape) + [(0, total - n)])
    return flat.reshape(lead_shape + (total // PACK_L, PACK_L))


def _unpack(packed, shapes, lead=None):
    lead_shape = () if lead is None else packed.shape[:lead]
    flat = packed.reshape(lead_shape + (-1,))
    out, off = [], 0
    for shp in shapes:
        n = int(np.prod(shp))
        out.append(flat[..., off:off + n].reshape(lead_shape + tuple(shp)))
        off += n
    return out


def _to_full(gathered, axis):
    t = jnp.moveaxis(gathered, 0, axis)
    shp = t.shape
    return t.reshape(shp[:axis] + (shp[axis] * shp[axis + 1],) + shp[axis + 2:])


def _to_chip_major(full, axis):
    shp = full.shape
    t = full.reshape(shp[:axis] + (N_CHIPS, shp[axis] // N_CHIPS) + shp[axis + 1:])
    return jnp.moveaxis(t, axis, 0)


WEIGHTS = ("norm_ffn1", "ffn1_w_in", "ffn1_w_out", "norm_mix", "norm_ffn2", "ffn2_w_in", "ffn2_w_out", "ple_norm",
           "ple_gate_w", "ple_proj_w", "hyb_w_in", "conv_dw_w", "conv_dw_b", "conv_ln_g", "conv_ln_b", "ssm_conv_w",
           "ssm_conv_b", "ssm_dt_bias", "ssm_a_log", "ssm_d", "ssm_norm", "hyb_w_out", "att_w_qkv", "att_b_qkv",
           "att_sinks", "att_w_o", "att_b_o", "final_norm")
SHARD_AXIS = dict(ffn1_w_in=2, ffn1_w_out=1, ffn2_w_in=2, ffn2_w_out=1, ple_gate_w=1, ple_proj_w=2, hyb_w_in=2,
                  conv_dw_w=2, ssm_conv_w=2, hyb_w_out=1, att_w_qkv=2, att_b_qkv=1, att_w_o=1, att_b_o=1)
BIG = ("ffn1_w_in", "ffn1_w_out", "ffn2_w_in", "ffn2_w_out", "ple_gate_w", "ple_proj_w", "hyb_w_in", "hyb_w_out",
       "att_w_qkv", "att_w_o")
ODD_WIDTH = "hyb_w_in"
BIG_FLAT = tuple(k for k in BIG if k != ODD_WIDTH)
SMALL_SHARDED = ("conv_dw_w", "ssm_conv_w", "att_b_qkv", "att_b_o")
SMALL = tuple(k for k in WEIGHTS if k not in BIG)


def _step(x, p, target, w, m, v):
    mx, my = lax.axis_index("x"), lax.axis_index("y")
    chip = 2 * mx + my

    depth = w["norm_ffn1"].shape[0]
    order = sorted([(k, i) for i in range(depth) for k in BIG if _layer_index(k, i) is not None],
                   key=lambda t: (t[1], _stage(t[0])))
    small_g = all_gather_devices(_pack([w[k] for k in SMALL_SHARDED], F32, 8), "gather_small")
    shards = [w[k][_layer_index(k, i)].astype(MXU_DTYPE) for k, i in order]
    lands = [lax.dynamic_update_slice_in_dim(lax.empty((N_CHIPS,) + s.shape, s.dtype), s[None], chip, axis=0) for s in shards]
    send_sems, recv_sems, shards, lands = gather_start(shards, lands, small_g, "gather_start")

    def fetch(i, stage, after):
        pos = [p for p, (k, li) in enumerate(order) if li == i and _stage(k) == stage]
        got = gather_wait(send_sems, recv_sems, [shards[p] for p in pos], [lands[p] for p in pos], pos[0], after,
                          f"gather_wait_l{i}_{stage}")
        out = {}
        for p, g in zip(pos, got):
            k = order[p][0]
            if k == ODD_WIDTH:
                out[k] = _to_full(g, 1)
            elif SHARD_AXIS[k] == 2:
                out[k] = ColSharded(g)
            else:
                out[k] = g.reshape(-1, g.shape[-1])
        return out

    small_g = small_g[0::2]
    small_full = {k: _to_full(g, SHARD_AXIS[k])
                  for k, g in zip(SMALL_SHARDED, _unpack(small_g, [w[k].shape for k in SMALL_SHARDED], lead=1))}
    layers = [{k: small_full.get(k, w[k])[_layer_index(k, i)] for k in SMALL if _layer_index(k, i) is not None}
              for i in range(depth)]

    sink = GradSink({k: w[k].shape for k in BIG}, depth)
    begun = {}

    def layer_done(i):
        begun[i] = reduce_begin(list(sink.bufs[i].values()), f"grads_l{i}")

    loss, dx, grads = trunk_fwd_bwd(x[0], p[:, 0], target[0], layers, w["final_norm"], sink, fetch, layer_done)

    reduced = {i: dict(zip(sink.bufs[i], reduce_end(begun[i], dx, f"grads_l{i}"))) for i in reversed(range(depth))}
    g_out = {k: sink.take(k, w[k].shape[0], reduced) for k in BIG}
    vec = _pack([loss[0:1, 0:1]] + [grads[k] for k in SMALL], F32, 8)
    vec = sum_devices(all_gather_devices(vec, "gather_vectors"), "sum_vectors")
    parts = _unpack(vec, [(1, 1)] + [grads[k].shape for k in SMALL])
    loss_out = parts[0].reshape(())
    for k, g in zip(SMALL, parts[1:]):
        if k in SHARD_AXIS:
            ax = SHARD_AXIS[k]
            g = lax.dynamic_slice_in_dim(g, chip * w[k].shape[ax], w[k].shape[ax], axis=ax)
        g_out[k] = g

    delta, new_m, new_v = {}, {}, {}
    for k in BIG:
        shp = w[k].shape
        two_d = lambda a: a.reshape(-1, shp[-1])
        d_, m_, v_ = adamw(two_d(w[k]), two_d(g_out[k]), two_d(m[k]), two_d(v[k]), f"adamw_{k}")
        delta[k], new_m[k], new_v[k] = d_.reshape(shp), m_.reshape(shp), v_.reshape(shp)
    shapes = [w[k].shape for k in SMALL]
    packed = [_pack([src[k] for k in SMALL], F32, 8) for src in (w, g_out, m, v)]
    outs = adamw(*packed, "adamw_small")
    for dst, o in zip((delta, new_m, new_v), outs):
        for k, a in zip(SMALL, _unpack(o, shapes)):
            dst[k] = a
    return ((loss_out, dx[None]) + tuple(g_out[k] for k in WEIGHTS) + tuple(delta[k] for k in WEIGHTS)
            + tuple(new_m[k] for k in WEIGHTS) + tuple(new_v[k] for k in WEIGHTS))


def kernel(x, p, norm_ffn1, ffn1_w_in, ffn1_w_out, norm_mix, norm_ffn2, ffn2_w_in, ffn2_w_out, ple_norm, ple_gate_w, ple_proj_w, hyb_w_in, conv_dw_w, conv_dw_b, conv_ln_g, conv_ln_b, ssm_conv_w, ssm_conv_b, ssm_dt_bias, ssm_a_log, ssm_d, ssm_norm, hyb_w_out, att_w_qkv, att_b_qkv, att_sinks, att_w_o, att_b_o, final_norm, loss_target, m_norm_ffn1, m_ffn1_w_in, m_ffn1_w_out, m_norm_mix, m_norm_ffn2, m_ffn2_w_in, m_ffn2_w_out, m_ple_norm, m_ple_gate_w, m_ple_proj_w, m_hyb_w_in, m_conv_dw_w, m_conv_dw_b, m_conv_ln_g, m_conv_ln_b, m_ssm_conv_w, m_ssm_conv_b, m_ssm_dt_bias, m_ssm_a_log, m_ssm_d, m_ssm_norm, m_hyb_w_out, m_att_w_qkv, m_att_b_qkv, m_att_sinks, m_att_w_o, m_att_b_o, m_final_norm, v_norm_ffn1, v_ffn1_w_in, v_ffn1_w_out, v_norm_mix, v_norm_ffn2, v_ffn2_w_in, v_ffn2_w_out, v_ple_norm, v_ple_gate_w, v_ple_proj_w, v_hyb_w_in, v_conv_dw_w, v_conv_dw_b, v_conv_ln_g, v_conv_ln_b, v_ssm_conv_w, v_ssm_conv_b, v_ssm_dt_bias, v_ssm_a_log, v_ssm_d, v_ssm_norm, v_hyb_w_out, v_att_w_qkv, v_att_b_qkv, v_att_sinks, v_att_w_o, v_att_b_o, v_final_norm):
    given = locals()
    w = {k: given[k] for k in WEIGHTS}
    m = {k: given["m_" + k] for k in WEIGHTS}
    v = {k: given["v_" + k] for k in WEIGHTS}
    return _step(x, p, loss_target, w, m, v)
```

```python
import functools
import math

import numpy as np
import jax
import jax.numpy as jnp
from jax import lax
from jax.experimental import pallas as pl
from jax.experimental.pallas import tpu as pltpu

F32 = jnp.float32
BF16 = jnp.bfloat16
MXU_DTYPE = jnp.bfloat16
S = jax.ShapeDtypeStruct
MESH = pl.DeviceIdType.MESH

V7X_VMEM_BYTES = 64 * 2**20
VMEM_LIMIT = 48 * 2**20
LANE = 128

EPS = 1e-6
SSM_HEADS = 16
HEAD_DIM = 64
SSM_GROUPS = 2
SSM_STATE = 128
SSM_CONV = 4
CHUNK = 128
CONV_WIDTH = 31
ATT_HEADS = 16
ATT_KV_HEADS = 4
WINDOW = 128
ROPE_THETA = 10000.0
ADAM_LR = 0.001
ADAM_B1 = 0.9
ADAM_B2 = 0.999
ADAM_EPS = 1e-08
ADAM_WD = 0.01
ADAM_STEP = 10

N_CHIPS = 4
N_DEV = 8

NN = ((1,), (0,))
NT = ((1,), (1,))
TN = ((0,), (0,))


def _mm(a, b, dims=NN):
    return lax.dot_general(a.astype(MXU_DTYPE), b.astype(MXU_DTYPE), (dims, ((), ())), preferred_element_type=F32)


def _split3(a):
    hi = a.astype(BF16)
    r = a - hi.astype(F32)
    mid = r.astype(BF16)
    lo = (r - mid.astype(F32)).astype(BF16)
    return hi, mid, lo


def _mm01(a, onehot, dims=NN):
    o = onehot.astype(BF16)
    out = None
    for part in _split3(a):
        t = lax.dot_general(part, o, (dims, ((), ())), preferred_element_type=F32)
        out = t if out is None else out + t
    return out


def _01mm(onehot, a):
    o = onehot.astype(BF16)
    out = None
    for part in _split3(a):
        t = lax.dot_general(o, part, (NN, ((), ())), preferred_element_type=F32)
        out = t if out is None else out + t
    return out


def _sigmoid(x):
    return 0.5 * jnp.tanh(0.5 * x) + 0.5


def _softplus(x):
    return jnp.maximum(x, 0.0) + jnp.log(1.0 + jnp.exp(-jnp.abs(x)))


def _iota(shape, axis):
    return lax.broadcasted_iota(jnp.int32, shape, axis)


def _head_indicator(width, heads, transposed=False):
    per = width // heads
    if transposed:
        return (_iota((heads, width), 1) // per == _iota((heads, width), 0)).astype(F32)
    return (_iota((width, heads), 0) // per == _iota((width, heads), 1)).astype(F32)


def _acc(ref, i, val):
    @pl.when(i == 0)
    def _():
        ref[...] = val

    @pl.when(i > 0)
    def _():
        ref[...] += val


def _rs(tile, width, col=0, shift=0, n=None):
    if shift == 0:
        return pl.BlockSpec((tile, width), lambda i: (i, col))
    if shift < 0:
        return pl.BlockSpec((tile, width), lambda i: (jnp.maximum(i - 1, 0), col))
    return pl.BlockSpec((tile, width), lambda i: (jnp.minimum(i + 1, n - 1), col))


def _ps(shape):
    return pl.BlockSpec(shape, lambda i: (0,) * len(shape))


def _call(body, name, grid, in_specs, out_specs, out_shape, scratch=(), sem=None):
    return pl.pallas_call(
        body, name=name, grid=grid, in_specs=in_specs, out_specs=out_specs, out_shape=out_shape,
        scratch_shapes=list(scratch),
        compiler_params=pltpu.CompilerParams(dimension_semantics=sem, vmem_limit_bytes=VMEM_LIMIT))


def _row_tile(n, target):
    t = min(n, target)
    assert n % t == 0, (n, t)
    return t


def _pick_tile(dim, target):
    if dim <= target:
        return dim
    t = (int(1.4 * target) // LANE) * LANE
    while t >= LANE:
        if dim % t == 0:
            return t
        t -= LANE
    return dim


ANY = pl.BlockSpec(memory_space=pl.ANY)


def _paired(j):
    return (j % 2) * 2 + j // 2


class ColSharded:
    def __init__(self, arr, paired=False):
        self.arr, self.paired = arr, paired
        self.nch, self.rows, self.per = arr.shape
        self.shape = (self.rows, self.nch * self.per)

    def chip(self, j):
        return _paired(j) if self.paired else j


class Slot:
    def __init__(self, buf, kind, per, off, c0=0, paired=False):
        self.buf, self.kind, self.per, self.off, self.c0, self.paired = buf, kind, per, off, c0, paired

    def chip(self, j):
        return _paired(j) if self.paired else j


def matmul(a, b, mode, name, *, out_dtype=F32, scale=None, res=None, bias=None, into=None, tm=1024, tn=1024, tk=1024):
    bshape = b.shape
    if mode == "nn":
        (m, k), (k2, n) = a.shape, bshape
    elif mode == "nt":
        (m, k), (n, k2) = a.shape, bshape
    else:
        (k, m), (k2, n) = a.shape, bshape
    assert k == k2, (a.shape, bshape, mode)
    tm, tn, tk = _pick_tile(m, tm), _pick_tile(n, tn), _pick_tile(k, tk)
    if isinstance(b, ColSharded):
        if mode == "nn":
            tn = b.per
        else:
            assert mode == "nt"
            tk = b.per
    if into is not None:
        if into.kind == "c":
            tn = into.per
            assert into.off % tm == 0 and n == N_CHIPS * into.per
        else:
            tm = max(1, min(m, int(1.4 * 1024)) // into.per) * into.per
            assert m % tm == 0 and into.off % into.per == 0 and into.c0 % (tm // into.per) == 0
    nk = k // tk
    dims = {"nn": NN, "nt": NT, "tn": TN}[mode]
    a_spec = (pl.BlockSpec((tk, tm), lambda i, j, kk: (kk, i)) if mode == "tn"
              else pl.BlockSpec((tm, tk), lambda i, j, kk: (i, kk)))
    if isinstance(b, ColSharded):
        bchip = b.chip
        b_spec = (pl.BlockSpec((None, tk, tn), lambda i, j, kk: (bchip(j), kk, 0)) if mode == "nn"
                  else pl.BlockSpec((None, tn, tk), lambda i, j, kk: (bchip(kk), j, 0)))
        b = b.arr
    else:
        b_spec = (pl.BlockSpec((tn, tk), lambda i, j, kk: (j, kk)) if mode == "nt"
                  else pl.BlockSpec((tk, tn), lambda i, j, kk: (kk, j)))
    plain_o = pl.BlockSpec((tm, tn), lambda i, j, kk: (i, j))
    ins, in_specs = [a, b], [a_spec, b_spec]
    if bias is not None:
        ins.append(bias)
        in_specs.append(pl.BlockSpec((1, tn), lambda i, j, kk: (0, j)))
    if res is not None:
        ins.append(res)
        in_specs.append(plain_o)
    aliases = {}
    if into is None:
        o_spec, o_shape = plain_o, S((m, n), out_dtype)
    else:
        aliases = {len(ins): 0}
        ins.append(into.buf)
        in_specs.append(ANY)
        o_shape = S(into.buf.shape, into.buf.dtype)
        if into.kind == "c":
            ob, ochip = into.off // tm, into.chip
            o_spec = pl.BlockSpec((None, tm, tn), lambda i, j, kk: (ochip(j), ob + i, 0))
        else:
            q, ob = tm // into.per, into.off // into.per
            cb = into.c0 // q
            o_spec = pl.BlockSpec((q, into.per, tn), lambda i, j, kk: (cb + i, ob, j))

    def body(*refs):
        a_ref, b_ref = refs[0], refs[1]
        o_ref, acc_ref = refs[-2], refs[-1]
        kk = pl.program_id(2)

        @pl.when(kk == 0)
        def _():
            acc_ref[...] = jnp.zeros_like(acc_ref)

        acc_ref[...] += _mm(a_ref[...], b_ref[...], dims)

        @pl.when(kk == nk - 1)
        def _():
            out = acc_ref[...]
            if scale is not None:
                out = out * scale
            pos = 2
            if bias is not None:
                out = out + refs[pos][...]
                pos += 1
            if res is not None:
                out = out + refs[pos][...]
            o_ref[...] = out.astype(o_ref.dtype).reshape(o_ref.shape)

    return pl.pallas_call(
        body, name=name, grid=(m // tm, n // tn, nk), in_specs=in_specs, out_specs=o_spec, out_shape=o_shape,
        scratch_shapes=[pltpu.VMEM((tm, tn), F32)], input_output_aliases=aliases,
        compiler_params=pltpu.CompilerParams(dimension_semantics=("parallel", "parallel", "arbitrary"),
                                             vmem_limit_bytes=VMEM_LIMIT))(*ins)


def rms_fwd(h, g, name):
    n, d = h.shape
    tile = _row_tile(n, 512)

    def body(h_ref, g_ref, o_ref):
        x = h_ref[...]
        r = lax.rsqrt(jnp.mean(x * x, axis=-1, keepdims=True) + EPS)
        o_ref[...] = (x * r * g_ref[...]).astype(o_ref.dtype)

    return _call(body, name, (n // tile,), [_rs(tile, d), _ps((1, d))], _rs(tile, d), S((n, d), MXU_DTYPE),
                 sem=("parallel",))(h, g)


def _rms_bwd_math(x, g, dy):
    r = lax.rsqrt(jnp.mean(x * x, axis=-1, keepdims=True) + EPS)
    xh = x * r
    dg = jnp.sum(dy * xh, axis=0, keepdims=True)
    dxh = dy * g
    dx = r * (dxh - xh * jnp.mean(dxh * xh, axis=-1, keepdims=True))
    return dx, dg


def rms_bwd(h, g, dxn, dh_in, name, colsum=False):
    n, d = h.shape
    tile = _row_tile(n, 256)

    def body(h_ref, g_ref, dxn_ref, dh_ref, o_ref, dg_ref, *cs_ref):
        i = pl.program_id(0)
        dx, dg = _rms_bwd_math(h_ref[...], g_ref[...], dxn_ref[...].astype(F32))
        out = dh_ref[...] + dx
        o_ref[...] = out
        _acc(dg_ref, i, dg)
        if colsum:
            _acc(cs_ref[0], i, jnp.sum(out, axis=0, keepdims=True))

    outs = [S((n, d), F32), S((1, d), F32)] + ([S((1, d), F32)] if colsum else [])
    ospecs = [_rs(tile, d), _ps((1, d))] + ([_ps((1, d))] if colsum else [])
    return _call(body, name, (n // tile,), [_rs(tile, d), _ps((1, d)), _rs(tile, d), _rs(tile, d)], ospecs, outs,
                 sem=("arbitrary",))(h, g, dxn, dh_in)


def swiglu_in(h, g, w_in, name):
    n, d = h.shape
    per = w_in.per
    nj = w_in.nch // 2
    tile = _row_tile(n, 512)

    def body(h_ref, g_ref, wg_ref, wu_ref, xn_ref, u_ref, hm_ref):
        x = h_ref[...]
        r = lax.rsqrt(jnp.mean(x * x, axis=-1, keepdims=True) + EPS)
        xn = (x * r * g_ref[...]).astype(xn_ref.dtype)

        @pl.when(pl.program_id(1) == 0)
        def _():
            xn_ref[...] = xn

        a = _mm(xn, wg_ref[...])
        b = _mm(xn, wu_ref[...])
        u_ref[:, :per] = a
        u_ref[:, per:] = b
        hm_ref[...] = (a * _sigmoid(a) * b).astype(hm_ref.dtype)

    return pl.pallas_call(
        body, name=name, grid=(n // tile, nj),
        in_specs=[pl.BlockSpec((tile, d), lambda i, j: (i, 0)), pl.BlockSpec((1, d), lambda i, j: (0, 0)),
                  pl.BlockSpec((None, d, per), lambda i, j: (j, 0, 0)), pl.BlockSpec((None, d, per), lambda i, j: (nj + j, 0, 0))],
        out_specs=[pl.BlockSpec((tile, d), lambda i, j: (i, 0)), pl.BlockSpec((tile, 2 * per), lambda i, j: (i, j)),
                   pl.BlockSpec((tile, per), lambda i, j: (i, j))],
        out_shape=[S((n, d), MXU_DTYPE), S((n, 2 * nj * per), F32), S((n, nj * per), MXU_DTYPE)],
        compiler_params=pltpu.CompilerParams(dimension_semantics=("parallel", "arbitrary"), vmem_limit_bytes=VMEM_LIMIT),
    )(h, g, w_in.arr, w_in.arr)


def swiglu_out_bwd(dh, w_out, u, name):
    n, d = dh.shape
    f = w_out.shape[0]
    per = u.shape[1] // 4
    nj = f // per
    tile = _row_tile(n, 512)

    def body(dh_ref, w_ref, u_ref, du_ref):
        dm = 0.5 * _mm(dh_ref[...], w_ref[...], NT)
        a = u_ref[:, :per]
        b = u_ref[:, per:]
        s = _sigmoid(a)
        du_ref[:, :per] = (dm * b * s * (1.0 + a * (1.0 - s))).astype(du_ref.dtype)
        du_ref[:, per:] = (dm * a * s).astype(du_ref.dtype)

    return pl.pallas_call(
        body, name=name, grid=(n // tile, nj),
        in_specs=[pl.BlockSpec((tile, d), lambda i, j: (i, 0)), pl.BlockSpec((per, d), lambda i, j: (j, 0)),
                  pl.BlockSpec((tile, 2 * per), lambda i, j: (i, j))],
        out_specs=pl.BlockSpec((tile, 2 * per), lambda i, j: (i, j)),
        out_shape=S(u.shape, MXU_DTYPE),
        compiler_params=pltpu.CompilerParams(dimension_semantics=("parallel", "parallel"), vmem_limit_bytes=VMEM_LIMIT),
    )(dh, w_out, u)


def ple_fwd(h, gl, pp, name):
    n, d = h.shape
    tile = _row_tile(n, 512)

    def body(h_ref, gl_ref, pp_ref, o_ref):
        o_ref[...] = h_ref[...] + _sigmoid(gl_ref[...]) * pp_ref[...]

    return _call(body, name, (n // tile,), [_rs(tile, d)] * 3, _rs(tile, d), S((n, d), F32), sem=("parallel",))(h, gl, pp)


def ple_bwd(dh, gl, pp, after, name):
    n, d = dh.shape
    tile = _row_tile(n, 512)

    def body(dh_ref, gl_ref, pp_ref, after_ref, dpp_ref, dgl_ref):
        g = _sigmoid(gl_ref[...])
        dh_ = dh_ref[...]
        dpp_ref[...] = (dh_ * g).astype(dpp_ref.dtype)
        dgl_ref[...] = (dh_ * pp_ref[...] * g * (1.0 - g)).astype(dgl_ref.dtype)

    return _call(body, name, (n // tile,), [_rs(tile, d)] * 3 + [ANY], [_rs(tile, d)] * 2, [S((n, d), MXU_DTYPE)] * 2,
                 sem=("parallel",))(dh, gl, pp, after)


def loss_head(h, g, target, name):
    n, d = h.shape
    tile = _row_tile(n, 256)

    def body(h_ref, g_ref, t_ref, dh_ref, dg_ref, loss_ref):
        i = pl.program_id(0)
        x = h_ref[...]
        gg = g_ref[...]
        r = lax.rsqrt(jnp.mean(x * x, axis=-1, keepdims=True) + EPS)
        err = x * r * gg - t_ref[...]
        part = 0.5 * jnp.sum(jnp.mean(err * err, axis=-1, keepdims=True), axis=0, keepdims=True)
        dx, dg = _rms_bwd_math(x, gg, err * (1.0 / d))
        dh_ref[...] = dx
        _acc(dg_ref, i, dg)
        _acc(loss_ref, i, jnp.broadcast_to(part, (8, LANE)))

    return _call(body, name, (n // tile,), [_rs(tile, d), _ps((1, d)), _rs(tile, d)],
                 [_rs(tile, d), _ps((1, d)), _ps((8, LANE))], [S((n, d), F32), S((1, d), F32), S((8, LANE), F32)],
                 sem=("arbitrary",))(h, g, target)


def adamw(w, g, m, v, name):
    r, c = w.shape
    tile = r
    for t in (512, 256, 128, 64, 32, 16, 8):
        if r % t == 0 and t * c * 4 <= 2**21:
            tile = t
            break
    c1 = np.float32(1.0 - ADAM_B1 ** ADAM_STEP)
    c2 = np.float32(1.0 - ADAM_B2 ** ADAM_STEP)

    def body(w_ref, g_ref, m_ref, v_ref, d_ref, mo_ref, vo_ref):
        gg = g_ref[...]
        mm = ADAM_B1 * m_ref[...] + (1.0 - ADAM_B1) * gg
        vv = ADAM_B2 * v_ref[...] + (1.0 - ADAM_B2) * (gg * gg)
        mo_ref[...] = mm
        vo_ref[...] = vv
        d_ref[...] = -ADAM_LR * ((mm / c1) / (jnp.sqrt(vv / c2) + ADAM_EPS) + ADAM_WD * w_ref[...])

    return _call(body, name, (r // tile,), [_rs(tile, c)] * 4, [_rs(tile, c)] * 3, [S((r, c), F32)] * 3,
                 sem=("parallel",))(w, g, m, v)


def _taps_fwd(sc, w_ref, width, halo, tile, acc):
    for k in range(width):
        o = halo - (width - 1) + k
        acc = acc + w_ref[k:k + 1, :] * sc[o:o + tile, :]
    return acc


def _taps_bwd_x(sc_d, w_ref, width, tile, acc):
    for k in range(width):
        o = (width - 1) - k
        acc = acc + w_ref[k:k + 1, :] * sc_d[o:o + tile, :]
    return acc


def _taps_bwd_w(dy, sc, dw_ref, width, halo, tile, i):
    @pl.when(i == 0)
    def _():
        dw_ref[...] = jnp.zeros_like(dw_ref)

    for k in range(width):
        o = halo - (width - 1) + k
        dw_ref[k:k + 1, :] += jnp.sum(dy * sc[o:o + tile, :], axis=0, keepdims=True)


def _ln_stats(x):
    mu = jnp.mean(x, axis=-1, keepdims=True)
    xc = x - mu
    r = lax.rsqrt(jnp.mean(xc * xc, axis=-1, keepdims=True) + EPS)
    return xc * r, r


def conv_group_fwd(proj, cw, cb, lg, lb, name):
    n = proj.shape[0]
    d = cw.shape[1]
    tile = _row_tile(n, 256)
    halo = 32

    def body(v_ref, g_ref, vp_ref, gp_ref, cw_ref, cb_ref, lg_ref, lb_ref, u_ref, u1_ref, sc):
        i = pl.program_id(0)
        first = (i > 0).astype(F32)
        sc[0:halo, :] = vp_ref[tile - halo:, :] * _sigmoid(gp_ref[tile - halo:, :]) * first
        sc[halo:, :] = v_ref[...] * _sigmoid(g_ref[...])
        u1 = _taps_fwd(sc, cw_ref, CONV_WIDTH, halo, tile, jnp.zeros((tile, d), F32) + cb_ref[...])
        u1_ref[...] = u1
        xh, _ = _ln_stats(u1)
        y = xh * lg_ref[...] + lb_ref[...]
        u_ref[...] = (y * _sigmoid(y)).astype(u_ref.dtype)

    return _call(body, name, (n // tile,),
                 [_rs(tile, d, 0), _rs(tile, d, 1), _rs(tile, d, 0, -1), _rs(tile, d, 1, -1),
                  _ps(cw.shape), _ps((1, d)), _ps((1, d)), _ps((1, d))],
                 [_rs(tile, d), _rs(tile, d)], [S((n, d), MXU_DTYPE), S((n, d), F32)],
                 scratch=[pltpu.VMEM((halo + tile, d), F32)], sem=("arbitrary",))(proj, proj, proj, proj, cw, cb, lg, lb)


def conv_group_bwd(du, u1, proj, cw, lg, lb, name):
    n = proj.shape[0]
    d = cw.shape[1]
    tile = _row_tile(n, 256)
    halo = 32
    nt = n // tile

    def body(du_ref, dun_ref, u1_ref, u1n_ref, v_ref, g_ref, vp_ref, gp_ref, cw_ref, lg_ref, lb_ref,
             dp_ref, dcw_ref, dcb_ref, dlg_ref, dlb_ref, sc, sc_d):
        i = pl.program_id(0)

        def ln_swish_bwd(dy_, u1_):
            xh, r = _ln_stats(u1_)
            y = xh * lg_ref[...] + lb_ref[...]
            s = _sigmoid(y)
            dyy = dy_ * s * (1.0 + y * (1.0 - s))
            dxh = dyy * lg_ref[...]
            dx = r * (dxh - jnp.mean(dxh, axis=-1, keepdims=True) - xh * jnp.mean(dxh * xh, axis=-1, keepdims=True))
            return dx, jnp.sum(dyy * xh, axis=0, keepdims=True), jnp.sum(dyy, axis=0, keepdims=True)

        du1, dlg, dlb = ln_swish_bwd(du_ref[...].astype(F32), u1_ref[...])
        du1n, _, _ = ln_swish_bwd(dun_ref[0:halo, :].astype(F32), u1n_ref[0:halo, :])
        sc_d[0:tile, :] = du1
        sc_d[tile:, :] = du1n * (i < nt - 1).astype(F32)
        sig = _sigmoid(g_ref[...])
        val = v_ref[...]
        sc[0:halo, :] = vp_ref[tile - halo:, :] * _sigmoid(gp_ref[tile - halo:, :]) * (i > 0).astype(F32)
        sc[halo:, :] = val * sig
        du0 = _taps_bwd_x(sc_d, cw_ref, CONV_WIDTH, tile, jnp.zeros((tile, d), F32))
        _taps_bwd_w(du1, sc, dcw_ref, CONV_WIDTH, halo, tile, i)
        _acc(dcb_ref, i, jnp.sum(du1, axis=0, keepdims=True))
        _acc(dlg_ref, i, dlg)
        _acc(dlb_ref, i, dlb)
        dp_ref[:, :d] = (du0 * sig).astype(dp_ref.dtype)
        dp_ref[:, d:] = (du0 * val * sig * (1.0 - sig)).astype(dp_ref.dtype)

    return _call(body, name, (nt,),
                 [_rs(tile, d), _rs(tile, d, 0, 1, nt), _rs(tile, d), _rs(tile, d, 0, 1, nt),
                  _rs(tile, d, 0), _rs(tile, d, 1), _rs(tile, d, 0, -1), _rs(tile, d, 1, -1),
                  _ps(cw.shape), _ps((1, d)), _ps((1, d))],
                 [_rs(tile, 2 * d), _ps(cw.shape), _ps((1, d)), _ps((1, d)), _ps((1, d))],
                 [S((n, proj.shape[1]), MXU_DTYPE), S(cw.shape, F32), S((1, d), F32), S((1, d), F32), S((1, d), F32)],
                 scratch=[pltpu.VMEM((halo + tile, d), F32), pltpu.VMEM((tile + halo, d), F32)],
                 sem=("arbitrary",))(du, du, u1, u1, proj, proj, proj, proj, cw, lg, lb)


def ssm_conv_fwd(proj, dtr, sw, sb, dtb, name):
    n = proj.shape[0]
    w = sw.shape[1]
    inner = SSM_HEADS * HEAD_DIM
    tile = _row_tile(n, 256)
    halo = 8

    def body(x_ref, xp_ref, dtr_ref, sw_ref, sb_ref, dtb_ref, pre_ref, xs_ref, bc_ref, dt_ref, sc):
        i = pl.program_id(0)
        sc[0:halo, :] = xp_ref[tile - halo:, :] * (i > 0).astype(F32)
        sc[halo:, :] = x_ref[...]
        pre = _taps_fwd(sc, sw_ref, SSM_CONV, halo, tile, jnp.zeros((tile, w), F32) + sb_ref[...])
        pre_ref[...] = pre
        act = pre * _sigmoid(pre)
        xs_ref[...] = act[:, :inner]
        bc_ref[...] = act[:, inner:]
        dt = _softplus(dtr_ref[...] + dtb_ref[...])
        dt_ref[...] = jnp.where(_iota(dt.shape, 1) < SSM_HEADS, dt, 0.0)

    return _call(body, name, (n // tile,),
                 [_rs(tile, w, 2), _rs(tile, w, 2, -1), _rs(tile, LANE), _ps(sw.shape), _ps((1, w)), _ps((1, LANE))],
                 [_rs(tile, w), _rs(tile, inner), _rs(tile, w - inner), _rs(tile, LANE)],
                 [S((n, w), F32), S((n, inner), F32), S((n, w - inner), F32), S((n, LANE), F32)],
                 scratch=[pltpu.VMEM((halo + tile, w), F32)], sem=("arbitrary",))(proj, proj, dtr, sw, sb, dtb)


def ssm_conv_bwd(dxs, dbc, pre, proj, sw, dproj, name):
    n = proj.shape[0]
    w = sw.shape[1]
    inner = SSM_HEADS * HEAD_DIM
    tile = _row_tile(n, 256)
    halo = 8
    nt = n // tile

    def body(dxs_ref, dxsn_ref, dbc_ref, dbcn_ref, pre_ref, pren_ref, x_ref, xp_ref, sw_ref, dp_in_ref,
             dx_ref, dsw_ref, dsb_ref, sc, sc_d):
        i = pl.program_id(0)

        def silu_bwd(d_, p_):
            s = _sigmoid(p_)
            return d_ * s * (1.0 + p_ * (1.0 - s))

        sc_d[0:tile, :inner] = silu_bwd(dxs_ref[...], pre_ref[:, :inner])
        sc_d[0:tile, inner:] = silu_bwd(dbc_ref[...], pre_ref[:, inner:])
        last = (i < nt - 1).astype(F32)
        sc_d[tile:, :inner] = silu_bwd(dxsn_ref[0:halo, :], pren_ref[0:halo, :inner]) * last
        sc_d[tile:, inner:] = silu_bwd(dbcn_ref[0:halo, :], pren_ref[0:halo, inner:]) * last
        sc[0:halo, :] = xp_ref[tile - halo:, :] * (i > 0).astype(F32)
        sc[halo:, :] = x_ref[...]
        dpre = sc_d[0:tile, :]
        dx_ref[...] = _taps_bwd_x(sc_d, sw_ref, SSM_CONV, tile, jnp.zeros((tile, w), F32)).astype(dx_ref.dtype)
        _taps_bwd_w(dpre, sc, dsw_ref, SSM_CONV, halo, tile, i)
        _acc(dsb_ref, i, jnp.sum(dpre, axis=0, keepdims=True))

    return pl.pallas_call(
        body, name=name, grid=(nt,),
        in_specs=[_rs(tile, inner), _rs(tile, inner, 0, 1, nt), _rs(tile, w - inner), _rs(tile, w - inner, 0, 1, nt),
                  _rs(tile, w), _rs(tile, w, 0, 1, nt), _rs(tile, w, 2), _rs(tile, w, 2, -1), _ps(sw.shape), ANY],
        out_specs=[_rs(tile, w, 2), _ps(sw.shape), _ps((1, w))],
        out_shape=[S(dproj.shape, dproj.dtype), S(sw.shape, F32), S((1, w), F32)],
        scratch_shapes=[pltpu.VMEM((halo + tile, w), F32), pltpu.VMEM((tile + halo, w), F32)],
        input_output_aliases={9: 0},
        compiler_params=pltpu.CompilerParams(dimension_semantics=("arbitrary",), vmem_limit_bytes=VMEM_LIMIT),
    )(dxs, dxs, dbc, dbc, pre, pre, proj, proj, sw, dproj)


def _ssd_prologue(dt_ref, dtT_ref, al_ref, alc_ref):
    row = _iota((CHUNK, CHUNK), 0)
    col = _iota((CHUNK, CHUNK), 1)
    dt = dt_ref[:, :SSM_HEADS]
    a_row = -jnp.exp(al_ref[:, :SSM_HEADS])
    a_col = -jnp.exp(alc_ref[...])
    cs = _01mm((row >= col).astype(F32), dt * a_row)
    csT = _mm01(dtT_ref[...] * a_col, (row <= col).astype(F32))
    return dt, a_row, cs, csT, row, col


def _decay(cs, csT, h, row, col):
    lm = jnp.exp(jnp.where(row >= col, cs[:, h:h + 1] - csT[h:h + 1, :], -1e30))
    lmT = jnp.exp(jnp.where(col >= row, csT[h:h + 1, :] - cs[:, h:h + 1], -1e30))
    return lm, lmT


def ssd_fwd(xs, bc, dt, dtT, alog_row, alog_col, name):
    n, width = xs.shape
    nc = n // CHUNK
    gw = width // SSM_GROUPS
    hpg = SSM_HEADS // SSM_GROUPS
    ns = SSM_STATE

    def body(xs_ref, bc_ref, dt_ref, dtT_ref, al_ref, alc_ref, y_ref, hs_ref, h_sc):
        i = pl.program_id(0)

        @pl.when(i == 0)
        def _():
            h_sc[...] = jnp.zeros_like(h_sc)

        dt, a_row, cs, csT, row, col = _ssd_prologue(dt_ref, dtT_ref, al_ref, alc_ref)
        indT = _head_indicator(width, SSM_HEADS, transposed=True)
        dt_full = _mm01(dt, indT)
        e_full = jnp.exp(_mm01(cs, indT))
        dte_full = jnp.exp(_mm01(cs[CHUNK - 1:CHUNK, :] - cs, indT))
        xt = xs_ref[...] * dt_full
        hs_ref[0] = h_sc[...]
        lo = _iota((CHUNK, 2 * HEAD_DIM), 1) < HEAD_DIM
        for g in range(SSM_GROUPS):
            bg = bc_ref[:, g * ns:(g + 1) * ns]
            cg = bc_ref[:, (SSM_GROUPS + g) * ns:(SSM_GROUPS + g + 1) * ns]
            gm = _mm(cg, bg, NT)
            hg = h_sc[g * gw:(g + 1) * gw, :]
            yoff = e_full[:, g * gw:(g + 1) * gw] * _mm(cg, hg, NT)
            for pr in range(hpg // 2):
                h0 = g * hpg + 2 * pr
                c0 = h0 * HEAD_DIM
                xp = xt[:, c0:c0 + 2 * HEAD_DIM]
                m0 = gm * _decay(cs, csT, h0, row, col)[0]
                m1 = gm * _decay(cs, csT, h0 + 1, row, col)[0]
                yd = jnp.where(lo, _mm(m0, xp), _mm(m1, xp))
                y_ref[:, c0:c0 + 2 * HEAD_DIM] = yd + yoff[:, 2 * pr * HEAD_DIM:(2 * pr + 2) * HEAD_DIM]
            sg = _mm(xt[:, g * gw:(g + 1) * gw] * dte_full[:, g * gw:(g + 1) * gw], bg, TN)
            for hh in range(hpg):
                h = g * hpg + hh
                r0 = h * HEAD_DIM
                h_sc[r0:r0 + HEAD_DIM, :] = (h_sc[r0:r0 + HEAD_DIM, :] * jnp.exp(csT[h:h + 1, CHUNK - 1:CHUNK])
                                             + sg[hh * HEAD_DIM:(hh + 1) * HEAD_DIM, :])

    bcw = bc.shape[1]
    return _call(body, name, (nc,),
                 [_rs(CHUNK, width), _rs(CHUNK, bcw), _rs(CHUNK, LANE), pl.BlockSpec((SSM_HEADS, CHUNK), lambda i: (0, i)),
                  _ps((1, LANE)), _ps((SSM_HEADS, 1))],
                 [_rs(CHUNK, width), pl.BlockSpec((1, width, ns), lambda i: (i, 0, 0))],
                 [S((n, width), F32), S((nc, width, ns), F32)],
                 scratch=[pltpu.VMEM((width, ns), F32)], sem=("arbitrary",))(xs, bc, dt, dtT, alog_row, alog_col)


def ssd_bwd(xs, bc, dt, dtT, alog_row, alog_col, hs, dy, dxs_skip, name):
    n, width = xs.shape
    nc = n // CHUNK
    gw = width // SSM_GROUPS
    hpg = SSM_HEADS // SSM_GROUPS
    ns = SSM_STATE
    bcw = bc.shape[1]

    def body(xs_ref, bc_ref, dt_ref, dtT_ref, al_ref, alc_ref, hs_ref, dy_ref, skip_ref,
             dxs_ref, dbc_ref, ddtr_ref, dal_ref, ddtb_ref, dh_sc, dxt_sc):
        i = pl.program_id(0)

        @pl.when(i == 0)
        def _():
            dh_sc[...] = jnp.zeros_like(dh_sc)

        dt, a_row, cs, csT, row, col = _ssd_prologue(dt_ref, dtT_ref, al_ref, alc_ref)
        indT = _head_indicator(width, SSM_HEADS, transposed=True)
        ind = _head_indicator(width, SSM_HEADS)
        dt_full = _mm01(dt, indT)
        e_full = jnp.exp(_mm01(cs, indT))
        cs_last = cs[CHUNK - 1:CHUNK, :]
        dte = jnp.exp(cs_last - cs)
        dte_full = _mm01(dte, indT)
        xs_ = xs_ref[...]
        xt = xs_ * dt_full
        dy_ = dy_ref[...]
        hmat = hs_ref[0]
        ds = dh_sc[...]
        lo = _iota((CHUNK, 2 * HEAD_DIM), 1) < HEAD_DIM
        head_lane = _iota((1, SSM_HEADS), 1)
        dcs = jnp.zeros((CHUNK, SSM_HEADS), F32)
        ddte = jnp.zeros((CHUNK, SSM_HEADS), F32)
        for g in range(SSM_GROUPS):
            sl = slice(g * gw, (g + 1) * gw)
            bg = bc_ref[:, g * ns:(g + 1) * ns]
            cg = bc_ref[:, (SSM_GROUPS + g) * ns:(SSM_GROUPS + g + 1) * ns]
            gm = _mm(cg, bg, NT)
            gmT = _mm(bg, cg, NT)
            hg = hmat[sl, :]
            dsg = ds[sl, :]
            dyg = dy_[:, sl]
            xtg = xt[:, sl]
            yoff = e_full[:, sl] * _mm(cg, hg, NT)
            edy = e_full[:, sl] * dyg
            bds = _mm(bg, dsg, NT)
            dxt_g = dte_full[:, sl] * bds
            ddte = ddte + _mm01(xtg * bds, ind[sl, :])
            dcs = dcs + _mm01(dyg * yoff, ind[sl, :])
            db = _mm(xtg * dte_full[:, sl], dsg)
            dc = _mm(edy, hg)
            dhc = _mm(edy, cg, TN)
            dgs = jnp.zeros((CHUNK, CHUNK), F32)
            dgTs = jnp.zeros((CHUNK, CHUNK), F32)
            for pr in range(hpg // 2):
                h0 = g * hpg + 2 * pr
                c0 = 2 * pr * HEAD_DIM
                xp = xtg[:, c0:c0 + 2 * HEAD_DIM]
                dyp = dyg[:, c0:c0 + 2 * HEAD_DIM]
                rr = []
                for h, half in ((h0, lo), (h0 + 1, jnp.logical_not(lo))):
                    lm, lmT = _decay(cs, csT, h, row, col)
                    xm = jnp.where(half, xp, 0.0)
                    dm = _mm(dyp, xm, NT)
                    dmT = _mm(xm, dyp, NT)
                    mT = gmT * lmT
                    z = jnp.sum(dm * (gm * lm), axis=1, keepdims=True) - jnp.sum(dmT * mT, axis=1, keepdims=True)
                    dcs = dcs + z * (head_lane == h).astype(F32)
                    dgs = dgs + dm * lm
                    dgTs = dgTs + dmT * lmT
                    rr.append(_mm(mT, dyp))
                dxt_sc[:, g * gw + c0:g * gw + c0 + 2 * HEAD_DIM] = jnp.where(lo, rr[0], rr[1]) + dxt_g[:, c0:c0 + 2 * HEAD_DIM]
            dbc_ref[:, g * ns:(g + 1) * ns] = db + _mm(dgTs, cg)
            dbc_ref[:, (SSM_GROUPS + g) * ns:(SSM_GROUPS + g + 1) * ns] = dc + _mm(dgs, bg)
            for hh in range(hpg):
                h = g * hpg + hh
                r0 = h * HEAD_DIM
                dh_sc[r0:r0 + HEAD_DIM, :] = (dhc[hh * HEAD_DIM:(hh + 1) * HEAD_DIM, :]
                                              + jnp.exp(csT[h:h + 1, CHUNK - 1:CHUNK]) * ds[r0:r0 + HEAD_DIM, :])
        t = ddte * dte
        per_head = jnp.sum(jnp.sum(ds * hmat, axis=1, keepdims=True) * ind, axis=0, keepdims=True)
        last_add = jnp.sum(t, axis=0, keepdims=True) + jnp.exp(cs_last) * per_head
        dcs = dcs - t + jnp.where(_iota((CHUNK, SSM_HEADS), 0) == CHUNK - 1, last_add, 0.0)
        dadt = _01mm((row <= col).astype(F32), dcs)
        dxt = dxt_sc[...]
        ddt = dadt * a_row + _mm01(dxt * xs_, ind)
        dxs_ref[...] = dxt * dt_full + skip_ref[...]
        ddtr = ddt * (1.0 - jnp.exp(-dt))
        ddtr_ref[...] = jnp.zeros_like(ddtr_ref)
        ddtr_ref[:, :SSM_HEADS] = ddtr.astype(ddtr_ref.dtype)
        _acc(dal_ref, i, jnp.sum(dadt * dt, axis=0, keepdims=True) * a_row)
        _acc(ddtb_ref, i, jnp.sum(ddtr, axis=0, keepdims=True))

    rev = lambda i: (nc - 1 - i, 0)
    return _call(body, name, (nc,),
                 [pl.BlockSpec((CHUNK, width), rev), pl.BlockSpec((CHUNK, bcw), rev), pl.BlockSpec((CHUNK, LANE), rev),
                  pl.BlockSpec((SSM_HEADS, CHUNK), lambda i: (0, nc - 1 - i)), _ps((1, LANE)), _ps((SSM_HEADS, 1)),
                  pl.BlockSpec((1, width, ns), lambda i: (nc - 1 - i, 0, 0)), pl.BlockSpec((CHUNK, width), rev),
                  pl.BlockSpec((CHUNK, width), rev)],
                 [pl.BlockSpec((CHUNK, width), rev), pl.BlockSpec((CHUNK, bcw), rev), pl.BlockSpec((CHUNK, LANE), rev),
                  _ps((1, SSM_HEADS)), _ps((1, SSM_HEADS))],
                 [S((n, width), F32), S((n, bcw), F32), S((n, LANE), MXU_DTYPE), S((1, SSM_HEADS), F32), S((1, SSM_HEADS), F32)],
                 scratch=[pltpu.VMEM((width, ns), F32), pltpu.VMEM((CHUNK, width), F32)],
                 sem=("arbitrary",))(xs, bc, dt, dtT, alog_row, alog_col, hs, dy, dxs_skip)


def ssm_gate_fwd(yssd, xs, proj, dfull, gamma, name):
    n, d = yssd.shape
    tile = _row_tile(n, 256)
    gw = d // SSM_GROUPS

    def body(y_ref, xs_ref, z_ref, df_ref, gm_ref, o_ref):
        z = z_ref[...]
        y2 = (y_ref[...] + df_ref[...] * xs_ref[...]) * (z * _sigmoid(z))
        for g in range(SSM_GROUPS):
            yg = y2[:, g * gw:(g + 1) * gw]
            r = lax.rsqrt(jnp.mean(yg * yg, axis=-1, keepdims=True) + EPS)
            o_ref[:, g * gw:(g + 1) * gw] = (yg * r * gm_ref[:, g * gw:(g + 1) * gw]).astype(o_ref.dtype)

    return _call(body, name, (n // tile,), [_rs(tile, d), _rs(tile, d), _rs(tile, d, 2), _ps((1, d)), _ps((1, d))],
                 _rs(tile, d), S((n, d), MXU_DTYPE), sem=("parallel",))(yssd, xs, proj, dfull, gamma)


def ssm_gate_bwd(dy3, yssd, xs, proj, dfull, gamma, dproj, name):
    n, d = yssd.shape
    tile = _row_tile(n, 256)
    gw = d // SSM_GROUPS

    def body(dy_ref, y_ref, xs_ref, z_ref, df_ref, gm_ref, dp_in_ref, dys_ref, dxs_ref, dz_ref, dgm_ref, dd_ref):
        i = pl.program_id(0)
        z = z_ref[...]
        s = _sigmoid(z)
        xs_ = xs_ref[...]
        y1 = y_ref[...] + df_ref[...] * xs_
        y2 = y1 * (z * s)
        dy_ = dy_ref[...].astype(F32)
        dgm = []
        dy2 = []
        for g in range(SSM_GROUPS):
            sl = slice(g * gw, (g + 1) * gw)
            dxg, dgg = _rms_bwd_math(y2[:, sl], gm_ref[:, sl], dy_[:, sl])
            dy2.append(dxg)
            dgm.append(dgg)
        dy2 = jnp.concatenate(dy2, axis=1)
        dy1 = dy2 * (z * s)
        dys_ref[...] = dy1
        dxs_ref[...] = dy1 * df_ref[...]
        dz_ref[...] = (dy2 * y1 * s * (1.0 + z * (1.0 - s))).astype(dz_ref.dtype)
        _acc(dgm_ref, i, jnp.concatenate(dgm, axis=1))
        colsum = jnp.broadcast_to(jnp.sum(dy1 * xs_, axis=0, keepdims=True), (8, d))
        _acc(dd_ref, i, _mm01(colsum, _head_indicator(d, SSM_HEADS))[0:1, :])

    return pl.pallas_call(
        body, name=name, grid=(n // tile,),
        in_specs=[_rs(tile, d), _rs(tile, d), _rs(tile, d), _rs(tile, d, 2), _ps((1, d)), _ps((1, d)), ANY],
        out_specs=[_rs(tile, d), _rs(tile, d), _rs(tile, d, 2), _ps((1, d)), _ps((1, SSM_HEADS))],
        out_shape=[S((n, d), F32), S((n, d), F32), S(dproj.shape, dproj.dtype), S((1, d), F32), S((1, SSM_HEADS), F32)],
        input_output_aliases={6: 2},
        compiler_params=pltpu.CompilerParams(dimension_semantics=("arbitrary",), vmem_limit_bytes=VMEM_LIMIT),
    )(dy3, yssd, xs, proj, dfull, gamma, dproj)


def _rope128(x, cos, sin_signed):
    half = HEAD_DIM // 2
    lane = _iota(x.shape, 1)
    partner = jnp.where((lane % HEAD_DIM) < half, pltpu.roll(x, LANE - half, 1), pltpu.roll(x, half, 1))
    return x * cos + partner * sin_signed


def rope_fwd(qkv, cos, sin, name):
    n, w = qkv.shape
    qw = ATT_HEADS * HEAD_DIM
    kw = ATT_KV_HEADS * HEAD_DIM
    tile = _row_tile(n, 256)

    def body(x_ref, c_ref, s_ref, q_ref, k_ref, v_ref):
        c, s = c_ref[...], s_ref[...]
        for j in range(qw // LANE):
            q_ref[:, j * LANE:(j + 1) * LANE] = _rope128(x_ref[:, j * LANE:(j + 1) * LANE], c, s).astype(q_ref.dtype)
        for j in range(kw // LANE):
            k_ref[:, j * LANE:(j + 1) * LANE] = _rope128(x_ref[:, qw + j * LANE:qw + (j + 1) * LANE], c, s).astype(k_ref.dtype)
        v_ref[...] = x_ref[:, qw + kw:].astype(v_ref.dtype)

    return _call(body, name, (n // tile,), [_rs(tile, w), _rs(tile, LANE), _rs(tile, LANE)],
                 [_rs(tile, qw), _rs(tile, kw), _rs(tile, kw)],
                 [S((n, qw), MXU_DTYPE), S((n, kw), MXU_DTYPE), S((n, kw), MXU_DTYPE)], sem=("parallel",))(qkv, cos, sin)


ATT_GROUP = ATT_HEADS // ATT_KV_HEADS


def _attn_mask(i):
    row = _iota((ATT_GROUP * WINDOW, 2 * WINDOW), 0) % WINDOW
    s = _iota((ATT_GROUP * WINDOW, 2 * WINDOW), 1)
    return (s > row) & (s <= row + WINDOW) & ((s >= WINDOW) | (i > 0))


def _stack_heads(ref, j, kh, lo):
    parts = []
    for t in range(ATT_GROUP):
        h = ATT_GROUP * j + t
        blk = ref[:, (h // 2) * LANE:(h // 2 + 1) * LANE]
        blk = jnp.where(lo if h % 2 == 0 else jnp.logical_not(lo), blk, jnp.zeros_like(blk))
        parts.append(blk if h % 2 == kh else pltpu.roll(blk, HEAD_DIM, 1))
    return jnp.concatenate(parts, axis=0)


def _unstack_heads(stacked, j, kh, lo, put):
    for t in range(0, ATT_GROUP, 2):
        h = ATT_GROUP * j + t
        even = stacked[t * WINDOW:(t + 1) * WINDOW, :]
        odd = stacked[(t + 1) * WINDOW:(t + 2) * WINDOW, :]
        even = even if kh == 0 else pltpu.roll(even, HEAD_DIM, 1)
        odd = odd if kh == 1 else pltpu.roll(odd, HEAD_DIM, 1)
        put(h // 2, jnp.where(lo, even, odd))


def _per_head_rows(ref, j):
    return jnp.concatenate([ref[:, ATT_GROUP * j + t:ATT_GROUP * j + t + 1] for t in range(ATT_GROUP)], axis=0)


def _per_head_scalar(ref, j):
    rows = _iota((ATT_GROUP * WINDOW, 1), 0) // WINDOW
    out = jnp.zeros((ATT_GROUP * WINDOW, 1), F32)
    for t in range(ATT_GROUP):
        out = out + jnp.where(rows == t, ref[:, ATT_GROUP * j + t:ATT_GROUP * j + t + 1], 0.0)
    return out


def attn_fwd(q, k, v, sinks, name):
    n, qw = q.shape
    kw = k.shape[1]
    nb = n // WINDOW
    scale = HEAD_DIM ** -0.5

    def body(q_ref, kc_ref, kp_ref, vc_ref, vp_ref, sk_ref, o_ref, lse_ref):
        i = pl.program_id(0)
        valid = _attn_mask(i)
        lo = _iota((WINDOW, LANE), 1) < HEAD_DIM
        k2 = jnp.concatenate([kp_ref[...], kc_ref[...]], axis=0)
        v2 = jnp.concatenate([vp_ref[...], vc_ref[...]], axis=0)
        lane1 = _iota((1, LANE), 1)
        lse = jnp.zeros((WINDOW, LANE), F32)

        def put_o(qb, val):
            o_ref[:, qb * LANE:(qb + 1) * LANE] = val.astype(o_ref.dtype)

        for j in range(ATT_KV_HEADS):
            kb, kh = j // 2, j % 2
            q4 = _stack_heads(q_ref, j, kh, lo)
            logits = jnp.where(valid, _mm(q4, k2[:, kb * LANE:(kb + 1) * LANE], NT) * scale, -1e30)
            sk = _per_head_scalar(sk_ref, j)
            m = jnp.maximum(jnp.max(logits, axis=-1, keepdims=True), sk)
            e = jnp.exp(logits - m)
            den = jnp.sum(e, axis=-1, keepdims=True) + jnp.exp(sk - m)
            lse4 = m + jnp.log(den)
            for t in range(ATT_GROUP):
                lse = lse + lse4[t * WINDOW:(t + 1) * WINDOW, :] * (lane1 == ATT_GROUP * j + t).astype(F32)
            _unstack_heads(_mm(e * (1.0 / den), v2[:, kb * LANE:(kb + 1) * LANE]), j, kh, lo, put_o)
        lse_ref[...] = lse

    return _call(body, name, (nb,),
                 [_rs(WINDOW, qw), _rs(WINDOW, kw), _rs(WINDOW, kw, 0, -1), _rs(WINDOW, kw), _rs(WINDOW, kw, 0, -1), _ps((1, LANE))],
                 [_rs(WINDOW, qw), _rs(WINDOW, LANE)], [S((n, qw), MXU_DTYPE), S((n, LANE), F32)],
                 sem=("parallel",))(q, k, k, v, v, sinks)


def attn_bwd(q, k, v, o, do, lse, sinks, name):
    n, qw = q.shape
    kw = k.shape[1]
    nb = n // WINDOW
    scale = HEAD_DIM ** -0.5

    def body(q_ref, kc_ref, kp_ref, vc_ref, vp_ref, o_ref, do_ref, lse_ref, sk_ref,
             dq_ref, dka_ref, dkb_ref, dva_ref, dvb_ref, dsk_ref):
        i = pl.program_id(0)
        valid = _attn_mask(i)
        lo = _iota((WINDOW, LANE), 1) < HEAD_DIM
        k2 = jnp.concatenate([kp_ref[...], kc_ref[...]], axis=0)
        v2 = jnp.concatenate([vp_ref[...], vc_ref[...]], axis=0)
        lane1 = _iota((1, LANE), 1)
        do_ = do_ref[...].astype(F32)
        delta = _mm01(do_ * o_ref[...].astype(F32), _head_indicator(qw, ATT_HEADS))
        dk2 = [jnp.zeros((2 * WINDOW, LANE), F32) for _ in range(kw // LANE)]
        dv2 = [jnp.zeros((2 * WINDOW, LANE), F32) for _ in range(kw // LANE)]
        dsk = jnp.zeros((1, LANE), F32)

        def put_dq(qb, val):
            dq_ref[:, qb * LANE:(qb + 1) * LANE] = val

        for j in range(ATT_KV_HEADS):
            kb, kh = j // 2, j % 2
            q4 = _stack_heads(q_ref, j, kh, lo)
            do4 = _stack_heads(do_ref, j, kh, lo)
            kk = k2[:, kb * LANE:(kb + 1) * LANE]
            vv = v2[:, kb * LANE:(kb + 1) * LANE]
            logits = jnp.where(valid, _mm(q4, kk, NT) * scale, -1e30)
            lse4 = _per_head_rows(lse_ref, j)
            p = jnp.exp(logits - lse4)
            dl = jnp.concatenate([delta[:, ATT_GROUP * j + t:ATT_GROUP * j + t + 1] for t in range(ATT_GROUP)], axis=0)
            ds = p * (_mm(do4, vv, NT) - dl) * scale
            sd = jnp.exp(_per_head_scalar(sk_ref, j) - lse4) * dl
            for t in range(ATT_GROUP):
                dsk = dsk - (jnp.sum(sd[t * WINDOW:(t + 1) * WINDOW, :], axis=0, keepdims=True)
                             * (lane1 == ATT_GROUP * j + t).astype(F32))
            _unstack_heads(_mm(ds, kk), j, kh, lo, put_dq)
            dk2[kb] = dk2[kb] + _mm(ds, q4, TN)
            dv2[kb] = dv2[kb] + _mm(p, do4, TN)
        for kb in range(kw // LANE):
            dkb_ref[:, kb * LANE:(kb + 1) * LANE] = dk2[kb][0:WINDOW, :]
            dka_ref[:, kb * LANE:(kb + 1) * LANE] = dk2[kb][WINDOW:, :]
            dvb_ref[:, kb * LANE:(kb + 1) * LANE] = dv2[kb][0:WINDOW, :]
            dva_ref[:, kb * LANE:(kb + 1) * LANE] = dv2[kb][WINDOW:, :]
        _acc(dsk_ref, i, dsk)

    return _call(body, name, (nb,),
                 [_rs(WINDOW, qw), _rs(WINDOW, kw), _rs(WINDOW, kw, 0, -1), _rs(WINDOW, kw), _rs(WINDOW, kw, 0, -1),
                  _rs(WINDOW, qw), _rs(WINDOW, qw), _rs(WINDOW, LANE), _ps((1, LANE))],
                 [_rs(WINDOW, qw)] + [_rs(WINDOW, kw)] * 4 + [_ps((1, LANE))],
                 [S((n, qw), F32)] + [S((n, kw), F32)] * 4 + [S((1, LANE), F32)],
                 sem=("arbitrary",))(q, k, k, v, v, o, do, lse, sinks)


def attn_grad_merge(dq, dka, dkb, dva, dvb, cos, sin, name):
    n, qw = dq.shape
    kw = dka.shape[1]
    nb = n // WINDOW
    w = qw + 2 * kw

    def body(dq_ref, dka_ref, dkb_ref, dva_ref, dvb_ref, c_ref, s_ref, o_ref, db_ref):
        i = pl.program_id(0)
        c, s = c_ref[...], -s_ref[...]
        nxt = (i < nb - 1).astype(F32)

        @pl.when(i == 0)
        def _():
            db_ref[...] = jnp.zeros_like(db_ref)

        def put(c0, val):
            o_ref[:, c0:c0 + val.shape[1]] = val.astype(o_ref.dtype)
            db_ref[:, c0:c0 + val.shape[1]] += jnp.sum(val, axis=0, keepdims=True)

        for j in range(qw // LANE):
            put(j * LANE, _rope128(dq_ref[:, j * LANE:(j + 1) * LANE], c, s))
        for j in range(kw // LANE):
            sl = slice(j * LANE, (j + 1) * LANE)
            put(qw + j * LANE, _rope128(dka_ref[:, sl] + dkb_ref[:, sl] * nxt, c, s))
        put(qw + kw, dva_ref[...] + dvb_ref[...] * nxt)

    return _call(body, name, (nb,),
                 [_rs(WINDOW, qw), _rs(WINDOW, kw), _rs(WINDOW, kw, 0, 1, nb), _rs(WINDOW, kw), _rs(WINDOW, kw, 0, 1, nb),
                  _rs(WINDOW, LANE), _rs(WINDOW, LANE)],
                 [_rs(WINDOW, w), _ps((1, w))], [S((n, w), MXU_DTYPE), S((1, w), F32)],
                 sem=("arbitrary",))(dq, dka, dkb, dva, dvb, cos, sin)


def _row(v):
    return v.reshape(1, -1)


def _pad_lanes(v, width=LANE):
    return jnp.pad(v.reshape(1, -1), ((0, 0), (0, width - v.size)))


class LayerWeights(dict):
    def __init__(self, small, fetch):
        super().__init__(small)
        self.fetch = fetch

    def need(self, k, after):
        if k not in self:
            self[k] = self.fetch(k, after)
        return self[k]


def ffn_fwd(h, g, w, keys, tag):
    xn, u, hm = swiglu_in(h, _row(g), w.need(keys[0], h), f"{tag}_in")
    return matmul(hm, w.need(keys[1], hm), "nn", f"{tag}_out", scale=0.5, res=h), (h, xn, u, hm)


class GradSink:
    ORDER = ("ffn1_w_out", "ffn2_w_out", "ple_gate_w", "att_w_o", "hyb_w_out", "ffn1_w_in", "ffn2_w_in", "att_w_qkv",
             "ple_proj_w", "hyb_w_in")

    def __init__(self, shard_shapes, depth):
        self.where, rows = {}, [{} for _ in range(depth)]
        for k in self.ORDER:
            n, r, c = shard_shapes[k]
            for li in range(n):
                layer = li if k in PER_LAYER else 2 * li + (0 if k in EVEN_ONLY else 1)
                off = -(-rows[layer].get(c, 0) // r) * r
                rows[layer][c] = off + r
                self.where[k, li] = (layer, c, "r" if SHARD_AXIS[k] == 1 else "c", off, r)
        self.bufs = [{c: lax.empty((N_CHIPS, r, c), F32) for c, r in rows[layer].items()} for layer in range(depth)]

    def mm(self, k, li, a, b, name, scale=None, c0=0, paired=False):
        layer, c, kind, off, r = self.where[k, li]
        slot = Slot(self.bufs[layer][c], kind, r if kind == "r" else c, off, c0, paired)
        self.bufs[layer][c] = matmul(a, b, "tn", name, scale=scale, into=slot)

    def put(self, k, li, chip_major):
        layer, c = self.where[k, li][:2]
        self.bufs[layer][c] = chip_major

    def take(self, k, n, reduced):
        parts = []
        for li in range(n):
            layer, c, _, off, r = self.where[k, li]
            parts.append(reduced[layer][c][off:off + r])
        return jnp.stack(parts)


def ffn_bwd(dh, g, w_in, w_out, saved, tag, sink, keys, layer, colsum=False):
    h, xn, u, hm = saved
    sink.mm(keys[1], layer, hm, dh, f"{tag}_dwout", scale=0.5)
    du = swiglu_out_bwd(dh, w_out, u, f"{tag}_dhm")
    sink.mm(keys[0], layer, xn, du, f"{tag}_dwin", paired=True)
    dxn = matmul(du, ColSharded(w_in.arr, paired=True), "nt", f"{tag}_dxn")
    outs = rms_bwd(h, _row(g), dxn, dh, f"{tag}_drms", colsum=colsum)
    return (outs[0], outs[1].reshape(-1)) + ((outs[2],) if colsum else ())


def _hyb_params(w):
    d = w["conv_dw_b"].size
    inner = SSM_HEADS * HEAD_DIM
    main = 3 * d + w["ssm_conv_b"].size
    return dict(
        w_main=w["hyb_w_in"][:, :main], w_dt=jnp.pad(w["hyb_w_in"][:, main:], ((0, 0), (0, LANE - SSM_HEADS))),
        cw=jnp.pad(w["conv_dw_w"], ((0, 32 - CONV_WIDTH), (0, 0))), cb=_row(w["conv_dw_b"]),
        lg=_row(w["conv_ln_g"]), lb=_row(w["conv_ln_b"]),
        sw=jnp.pad(w["ssm_conv_w"], ((0, 8 - SSM_CONV), (0, 0))), sb=_row(w["ssm_conv_b"]),
        dtb=_pad_lanes(w["ssm_dt_bias"]), al_row=_pad_lanes(w["ssm_a_log"]), al_col=w["ssm_a_log"].reshape(-1, 1),
        dfull=_row(jnp.repeat(w["ssm_d"], HEAD_DIM)), gamma=_row(w["ssm_norm"]), d=d, inner=inner, main=main)


def hyb_fwd(h, w, tag):
    w.need("hyb_w_in", h)
    q = _hyb_params(w)
    xn = rms_fwd(h, _row(w["norm_mix"]), f"{tag}_rms")
    proj = matmul(xn, q["w_main"], "nn", f"{tag}_in")
    dtr = matmul(xn, q["w_dt"], "nn", f"{tag}_in_dt")
    u, u1 = conv_group_fwd(proj, q["cw"], q["cb"], q["lg"], q["lb"], f"{tag}_conv")
    pre, xs, bc, dt = ssm_conv_fwd(proj, dtr, q["sw"], q["sb"], q["dtb"], f"{tag}_sconv")
    dtT = dt[:, :SSM_HEADS].T
    yssd, hs = ssd_fwd(xs, bc, dt, dtT, q["al_row"], q["al_col"], f"{tag}_ssd")
    y = ssm_gate_fwd(yssd, xs, proj, q["dfull"], q["gamma"], f"{tag}_gate")
    wo = w.need("hyb_w_out", u)
    h2 = matmul(u, wo[:q["d"]], "nn", f"{tag}_out_a", res=h)
    h2 = matmul(y, wo[q["d"]:], "nn", f"{tag}_out_b", res=h2)
    return h2, (h, xn, proj, u, u1, pre, xs, bc, dt, dtT, yssd, hs, y)


def hyb_bwd(dh, w, saved, tag, sink, layer):
    q = _hyb_params(w)
    h, xn, proj, u, u1, pre, xs, bc, dt, dtT, yssd, hs, y = saved
    du = matmul(dh, w["hyb_w_out"][:q["d"]], "nt", f"{tag}_du")
    dy3 = matmul(dh, w["hyb_w_out"][q["d"]:], "nt", f"{tag}_dy")
    sink.mm("hyb_w_out", layer, u, dh, f"{tag}_dwo_a", c0=0)
    sink.mm("hyb_w_out", layer, y, dh, f"{tag}_dwo_b", c0=N_CHIPS // 2)
    dproj, dcw, dcb, dlg, dlb = conv_group_bwd(du, u1, proj, q["cw"], q["lg"], q["lb"], f"{tag}_dconv")
    dyssd, dxs_skip, dproj, dgamma, dd = ssm_gate_bwd(dy3, yssd, xs, proj, q["dfull"], q["gamma"], dproj, f"{tag}_dgate")
    dxs, dbc, ddtr, dalog, ddtb = ssd_bwd(xs, bc, dt, dtT, q["al_row"], q["al_col"], hs, dyssd, dxs_skip, f"{tag}_dssd")
    dproj, dsw, dsb = ssm_conv_bwd(dxs, dbc, pre, proj, q["sw"], dproj, f"{tag}_dsconv")
    dw_in = jnp.concatenate([matmul(xn, dproj, "tn", f"{tag}_dwin"),
                             matmul(xn, ddtr, "tn", f"{tag}_dwin_dt")[:, :SSM_HEADS]], axis=1)
    sink.put("hyb_w_in", layer, _to_chip_major(dw_in, 1))
    dxn = matmul(dproj, q["w_main"], "nt", f"{tag}_dxn")
    dxn = matmul(ddtr, q["w_dt"], "nt", f"{tag}_dxn_dt", res=dxn)
    dh2, dg = rms_bwd(h, _row(w["norm_mix"]), dxn, dh, f"{tag}_drms")
    grads = dict(norm_mix=dg.reshape(-1), conv_dw_w=dcw[:CONV_WIDTH], conv_dw_b=dcb.reshape(-1),
                 conv_ln_g=dlg.reshape(-1), conv_ln_b=dlb.reshape(-1), ssm_conv_w=dsw[:SSM_CONV], ssm_conv_b=dsb.reshape(-1),
                 ssm_dt_bias=ddtb.reshape(-1), ssm_a_log=dalog.reshape(-1), ssm_d=dd.reshape(-1), ssm_norm=dgamma.reshape(-1))
    return dh2, grads


def rope_tables(n):
    half = HEAD_DIM // 2
    inv = ROPE_THETA ** (-jnp.arange(0, HEAD_DIM, 2, dtype=F32) / HEAD_DIM)
    ang = jnp.arange(n, dtype=F32)[:, None] * inv[None, :]
    cos, sin = jnp.cos(ang), jnp.sin(ang)
    reps = LANE // HEAD_DIM
    return jnp.tile(jnp.concatenate([cos, cos], axis=1), (1, reps)), jnp.tile(jnp.concatenate([-sin, sin], axis=1), (1, reps))


def att_fwd(h, w, tables, tag):
    cos, sin = tables
    xn = rms_fwd(h, _row(w["norm_mix"]), f"{tag}_rms")
    qkv = matmul(xn, w.need("att_w_qkv", h), "nn", f"{tag}_qkv", bias=_row(w["att_b_qkv"]))
    q, k, v = rope_fwd(qkv, cos, sin, f"{tag}_rope")
    sinks = _pad_lanes(w["att_sinks"])
    o, lse = attn_fwd(q, k, v, sinks, f"{tag}_attn")
    h2 = matmul(o, w.need("att_w_o", o), "nn", f"{tag}_o", bias=_row(w["att_b_o"]), res=h)
    return h2, (h, xn, q, k, v, o, lse, sinks)


def att_bwd(dh, dh_colsum, w, saved, tables, tag, sink, layer):
    cos, sin = tables
    h, xn, q, k, v, o, lse, sinks = saved
    do = matmul(dh, w["att_w_o"], "nt", f"{tag}_do")
    sink.mm("att_w_o", layer, o, dh, f"{tag}_dwo")
    dq, dka, dkb, dva, dvb, dsk = attn_bwd(q, k, v, o, do, lse, sinks, f"{tag}_dattn")
    dqkv, dbqkv = attn_grad_merge(dq, dka, dkb, dva, dvb, cos, sin, f"{tag}_drope")
    sink.mm("att_w_qkv", layer, xn, dqkv, f"{tag}_dwqkv")
    dxn = matmul(dqkv, w["att_w_qkv"], "nt", f"{tag}_dxn")
    dh2, dg = rms_bwd(h, _row(w["norm_mix"]), dxn, dh, f"{tag}_drms")
    grads = dict(norm_mix=dg.reshape(-1), att_b_qkv=dbqkv.reshape(-1), att_sinks=dsk[0, :ATT_HEADS],
                 att_b_o=dh_colsum.reshape(-1))
    return dh2, grads


def ple_block_fwd(h, pe, w, tag):
    xn = rms_fwd(h, _row(w["ple_norm"]), f"{tag}_rms")
    gl = matmul(xn, w.need("ple_gate_w", h), "nn", f"{tag}_gate")
    pp = matmul(pe, w.need("ple_proj_w", h), "nn", f"{tag}_proj")
    return ple_fwd(h, gl, pp, f"{tag}_mix"), (h, xn, gl, pp, pe)


def ple_block_bwd(dh, w, saved, tag, sink, layer, after):
    h, xn, gl, pp, pe = saved
    dpp, dgl = ple_bwd(dh, gl, pp, after, f"{tag}_dmix")
    sink.mm("ple_proj_w", layer, pe, dpp, f"{tag}_dwp")
    sink.mm("ple_gate_w", layer, xn, dgl, f"{tag}_dwg")
    dxn = matmul(dgl, w["ple_gate_w"], "nt", f"{tag}_dxn")
    dh2, dg = rms_bwd(h, _row(w["ple_norm"]), dxn, dh, f"{tag}_drms")
    return dh2, dict(ple_norm=dg.reshape(-1))


PER_LAYER = ("norm_ffn1", "ffn1_w_in", "ffn1_w_out", "norm_mix", "norm_ffn2", "ffn2_w_in", "ffn2_w_out",
             "ple_norm", "ple_gate_w", "ple_proj_w")
EVEN_ONLY = ("hyb_w_in", "conv_dw_w", "conv_dw_b", "conv_ln_g", "conv_ln_b", "ssm_conv_w", "ssm_conv_b",
             "ssm_dt_bias", "ssm_a_log", "ssm_d", "ssm_norm", "hyb_w_out")
ODD_ONLY = ("att_w_qkv", "att_b_qkv", "att_sinks", "att_w_o", "att_b_o")


def _layer_index(k, i):
    if k in PER_LAYER:
        return i
    if k in (EVEN_ONLY if i % 2 == 0 else ODD_ONLY):
        return i // 2
    return None


def _stage(k):
    return 0 if k.startswith("ffn1") else (2 if k.startswith(("ffn2", "ple")) else 1)


def trunk_fwd_bwd(x, pe, target, layers, final_norm, sink, layer_done):
    depth = len(layers)
    tables = rope_tables(x.shape[0])
    h = x
    saved = []
    for i, w in enumerate(layers):
        h, s1 = ffn_fwd(h, w["norm_ffn1"], w, ("ffn1_w_in", "ffn1_w_out"), f"l{i}_ffn1")
        if i % 2 == 0:
            h, s2 = hyb_fwd(h, w, f"l{i}_hyb")
        else:
            h, s2 = att_fwd(h, w, tables, f"l{i}_att")
        h, s3 = ffn_fwd(h, w["norm_ffn2"], w, ("ffn2_w_in", "ffn2_w_out"), f"l{i}_ffn2")
        h, s4 = ple_block_fwd(h, pe[i], w, f"l{i}_ple")
        saved.append((s1, s2, s3, s4))
    dh, dgf, loss = loss_head(h, _row(final_norm), target, "loss_head")
    grads = {}
    tie = dgf
    for i in reversed(range(depth)):
        w = layers[i]
        s1, s2, s3, s4 = saved[i]
        dh, g = ple_block_bwd(dh, w, s4, f"l{i}_ple", sink, i, tie)
        odd = i % 2 == 1
        out = ffn_bwd(dh, w["norm_ffn2"], w["ffn2_w_in"], w["ffn2_w_out"], s3, f"l{i}_ffn2", sink,
                      ("ffn2_w_in", "ffn2_w_out"), i, colsum=odd)
        dh = out[0]
        g.update(norm_ffn2=out[1])
        if odd:
            dh, gm = att_bwd(dh, out[2], w, s2, tables, f"l{i}_att", sink, i // 2)
        else:
            dh, gm = hyb_bwd(dh, w, s2, f"l{i}_hyb", sink, i // 2)
        g.update(gm)
        out = ffn_bwd(dh, w["norm_ffn1"], w["ffn1_w_in"], w["ffn1_w_out"], s1, f"l{i}_ffn1", sink,
                      ("ffn1_w_in", "ffn1_w_out"), i)
        dh = out[0]
        g.update(norm_ffn1=out[1])
        tie = layer_done(i)
        for k, v in g.items():
            grads.setdefault(k, []).insert(0, v)
    grads = {k: jnp.stack(v) for k, v in grads.items()}
    grads["final_norm"] = dgf.reshape(-1)
    return loss, dh, grads


def _me():
    return lax.axis_index("x"), lax.axis_index("y"), lax.axis_index("c")


def _flip(v, f):
    return 1 - v if f else v


def _remote(src, dst, send_sems, recv_sems, k, dev):
    return pltpu.make_async_remote_copy(src_ref=src, dst_ref=dst, send_sem=send_sems.at[k], recv_sem=recv_sems.at[k],
                                        device_id=dev, device_id_type=MESH)


CHIP_FLIPS = ((1, 0), (0, 1), (1, 1))
DEV_FLIPS = tuple((fx, fy, fc) for fx in (0, 1) for fy in (0, 1) for fc in (0, 1))[1:]


def all_gather_chips(xs, name):
    na = len(xs)
    halves = [x.shape[0] // 2 for x in xs]
    assert all(x.shape[0] % 2 == 0 for x in xs)

    def body(*refs):
        x_refs, out_refs = refs[:na], refs[na:2 * na]
        send_sems, recv_sems = refs[2 * na:]
        mx, my, mc = _me()
        chip = 2 * mx + my
        sib = (mx, my, 1 - mc)
        peers = [(_flip(mx, fx), _flip(my, fy)) for fx, fy in CHIP_FLIPS]

        def rows(a, ch, hc):
            return out_refs[a].at[ch, pl.ds(hc * halves[a], halves[a]), :]

        def src(a):
            return x_refs[a].at[pl.ds(mc * halves[a], halves[a]), :]

        first = [_remote(src(a), rows(a, chip, mc), send_sems, recv_sems, 6 * a + j, (px, py, mc))
                 for j, (px, py) in enumerate(peers) for a in range(na)]
        for cp in first:
            cp.start()
        passed = []
        for j, (px, py) in enumerate(peers):
            for a in range(na):
                landed = rows(a, 2 * px + py, mc)
                _remote(src(a), landed, send_sems, recv_sems, 6 * a + j, (px, py, mc)).wait_recv()
                fw = _remote(landed, landed, send_sems, recv_sems, 6 * a + 3 + j, sib)
                fw.start()
                passed.append(fw)
        for j, (px, py) in enumerate(peers):
            for a in range(na):
                _remote(src(a), rows(a, 2 * px + py, 1 - mc), send_sems, recv_sems, 6 * a + 3 + j, sib).wait_recv()
        for cp in first + passed:
            cp.wait_send()

    outs = pl.pallas_call(
        body, name=name, out_shape=[S((N_CHIPS,) + x.shape, x.dtype) for x in xs], in_specs=[ANY] * na, out_specs=[ANY] * na,
        scratch_shapes=[pltpu.SemaphoreType.DMA((6 * na,)), pltpu.SemaphoreType.DMA((6 * na,))])(*xs)
    chip = 2 * lax.axis_index("x") + lax.axis_index("y")
    return [lax.dynamic_update_slice_in_dim(o, x[None], chip, axis=0) for o, x in zip(outs, xs)]


HBM = pl.BlockSpec(memory_space=pltpu.HBM)
SEM = pl.BlockSpec(memory_space=pltpu.SEMAPHORE)
DATAFLOW = pltpu.SideEffectType.DATAFLOW_SIDE_EFFECTING


def gather_start(xs, lands, after, name):
    na = len(xs)

    def body(*refs):
        x_refs, land_refs = refs[:na], refs[na:2 * na]
        send_sems, recv_sems = refs[2 * na + 1], refs[2 * na + 2]
        token = refs[-1]
        mx, my, mc = _me()
        chip = 2 * mx + my
        for a in range(na):
            for j, (fx, fy) in enumerate(CHIP_FLIPS):
                _remote(x_refs[a], land_refs[a].at[chip], send_sems, recv_sems, 3 * a + j,
                        (_flip(mx, fx), _flip(my, fy), mc)).start()
        token[...] = jnp.zeros_like(token)

    outs = pl.pallas_call(
        body, name=name,
        out_shape=(pltpu.SemaphoreType.DMA((3 * na,)), pltpu.SemaphoreType.DMA((3 * na,)))
        + tuple(pltpu.HBM(x.shape, x.dtype) for x in xs) + tuple(pltpu.HBM(l.shape, l.dtype) for l in lands)
        + (S((8, LANE), F32),),
        in_specs=[HBM] * (2 * na) + [pl.BlockSpec(memory_space=pl.ANY)],
        out_specs=(SEM, SEM) + (HBM,) * (2 * na) + (pl.BlockSpec(memory_space=pltpu.VMEM),),
        input_output_aliases={a: 2 + a for a in range(2 * na)},
        compiler_params=pltpu.CompilerParams(has_side_effects=DATAFLOW),
    )(*[pltpu.with_memory_space_constraint(t, pltpu.HBM) for t in list(xs) + list(lands)], after)
    return outs[0], outs[1], list(outs[2:2 + na]), list(outs[2 + na:2 + 2 * na])


def gather_wait(send_sems, recv_sems, xs, lands, first, after, name):
    na = len(xs)

    def body(*refs):
        x_refs, land_refs = refs[:na], refs[na:2 * na]
        send_sems, recv_sems = refs[2 * na], refs[2 * na + 1]
        mx, my, mc = _me()
        for a in range(na):
            for j, (fx, fy) in enumerate(CHIP_FLIPS):
                px, py = _flip(mx, fx), _flip(my, fy)
                cp = _remote(x_refs[a], land_refs[a].at[2 * px + py], send_sems, recv_sems, 3 * (first + a) + j, (px, py, mc))
                cp.wait_send()
                cp.wait_recv()

    outs = pl.pallas_call(
        body, name=name,
        out_shape=tuple(pltpu.HBM(x.shape, x.dtype) for x in xs) + tuple(pltpu.HBM(l.shape, l.dtype) for l in lands),
        in_specs=[HBM] * (2 * na) + [SEM, SEM, pl.BlockSpec(memory_space=pl.ANY)], out_specs=(HBM,) * (2 * na),
        input_output_aliases={a: a for a in range(2 * na)},
        compiler_params=pltpu.CompilerParams(has_side_effects=DATAFLOW),
    )(*xs, *lands, send_sems, recv_sems, after)
    return list(outs[na:])


def all_gather_devices(v, name):
    r, l = v.shape

    def body(v_ref, out_ref, send_sems, recv_sems):
        mx, my, mc = _me()
        me = 4 * mx + 2 * my + mc
        peers = [(_flip(mx, fx), _flip(my, fy), _flip(mc, fc)) for fx, fy, fc in DEV_FLIPS]
        sends = [_remote(v_ref, out_ref.at[me], send_sems, recv_sems, j, p) for j, p in enumerate(peers)]
        for cp in sends:
            cp.start()
        for j, (px, py, pc) in enumerate(peers):
            _remote(v_ref, out_ref.at[4 * px + 2 * py + pc], send_sems, recv_sems, j, (px, py, pc)).wait_recv()
        for cp in sends:
            cp.wait_send()

    out = pl.pallas_call(
        body, name=name, out_shape=S((N_DEV, r, l), v.dtype), in_specs=[ANY], out_specs=ANY,
        scratch_shapes=[pltpu.SemaphoreType.DMA((7,)), pltpu.SemaphoreType.DMA((7,))])(v)
    me = 4 * lax.axis_index("x") + 2 * lax.axis_index("y") + lax.axis_index("c")
    return lax.dynamic_update_slice_in_dim(out, v[None], me, axis=0)


def sum_devices(g8, name):
    nd, r, l = g8.shape
    tile = r
    for t in (512, 256, 128, 64, 32, 16, 8):
        if r % t == 0:
            tile = t
            break

    def body(g_ref, o_ref):
        acc = g_ref[0]
        for d in range(1, nd):
            acc = acc + g_ref[d]
        o_ref[...] = acc

    return _call(body, name, (r // tile,), [pl.BlockSpec((nd, tile, l), lambda i: (0, i, 0))], _rs(tile, l), S((r, l), F32),
                 sem=("parallel",))(g8)


def exchange_halves(gs, name):
    na = len(gs)
    nch = gs[0].shape[0]

    def body(*refs):
        g_refs, out_refs = refs[:na], refs[na:2 * na]
        send_sems, recv_sems = refs[2 * na:]
        mx, my, mc = _me()
        sib = (mx, my, 1 - mc)
        cps = []
        for a in range(na):
            half = gs[a].shape[1] // 2
            for j in range(nch):
                cps.append(_remote(g_refs[a].at[j, pl.ds((1 - mc) * half, half), :], out_refs[a].at[j],
                                   send_sems, recv_sems, nch * a + j, sib))
        for cp in cps:
            cp.start()
        for cp in cps:
            cp.wait_recv()
        for cp in cps:
            cp.wait_send()

    return pl.pallas_call(
        body, name=name, out_shape=[S((nch, g.shape[1] // 2, g.shape[2]), g.dtype) for g in gs],
        in_specs=[ANY] * na, out_specs=[ANY] * na,
        scratch_shapes=[pltpu.SemaphoreType.DMA((nch * na,)), pltpu.SemaphoreType.DMA((nch * na,))])(*gs)


def add_halves(g4, got, name):
    nch, r, l = g4.shape
    half = r // 2
    tile = _pick_rows(half)
    nt = half // tile

    def body(g_ref, r_ref, a_ref, own_ref):
        j = pl.program_id(1)
        chip = 2 * lax.axis_index("x") + lax.axis_index("y")
        val = g_ref[0] + r_ref[0]
        a_ref[0] = val.astype(a_ref.dtype)

        @pl.when(j == chip)
        def _():
            own_ref[...] = val

    return pl.pallas_call(
        body, name=name, grid=(nt, nch),
        in_specs=[pl.BlockSpec((1, tile, l), lambda i, j: (j, lax.axis_index("c") * nt + i, 0)),
                  pl.BlockSpec((1, tile, l), lambda i, j: (j, i, 0))],
        out_specs=[pl.BlockSpec((1, tile, l), lambda i, j: (j, i, 0)), pl.BlockSpec((tile, l), lambda i, j: (i, 0))],
        out_shape=[S((nch, half, l), MXU_DTYPE), S((half, l), F32)],
        compiler_params=pltpu.CompilerParams(dimension_semantics=("parallel", "arbitrary"), vmem_limit_bytes=VMEM_LIMIT))(g4, got)


def _pick_rows(r, cap=512):
    for t in (512, 256, 128, 64, 32, 16):
        if t <= cap and r % t == 0:
            return t
    return r


def exchange_chips(parts, name):
    na = len(parts)

    def body(*refs):
        a_refs, out_refs = refs[:na], refs[na:2 * na]
        send_sems, recv_sems = refs[2 * na:]
        mx, my, mc = _me()
        peers = [(_flip(mx, fx), _flip(my, fy)) for fx, fy in CHIP_FLIPS]
        cps = [_remote(a_refs[a].at[2 * px + py], out_refs[a].at[j], send_sems, recv_sems, 3 * a + j, (px, py, mc))
               for j, (px, py) in enumerate(peers) for a in range(na)]
        for cp in cps:
            cp.start()
        for cp in cps:
            cp.wait_recv()
        for cp in cps:
            cp.wait_send()

    return pl.pallas_call(
        body, name=name, out_shape=[S((3,) + p.shape[1:], p.dtype) for p in parts], in_specs=[ANY] * na, out_specs=[ANY] * na,
        scratch_shapes=[pltpu.SemaphoreType.DMA((3 * na,)), pltpu.SemaphoreType.DMA((3 * na,))])(*parts)


def add_chips(own, got, name):
    h, l = own.shape
    tile = _pick_rows(h)

    def body(o_ref, g_ref, out_ref):
        out_ref[...] = ((o_ref[...] + g_ref[0].astype(F32)) + g_ref[1].astype(F32)) + g_ref[2].astype(F32)

    nt = h // tile
    return _call(body, name, (nt,), [_rs(tile, l), pl.BlockSpec((3, tile, l), lambda i: (0, i, 0))],
                 pl.BlockSpec((tile, l), lambda i: (lax.axis_index("c") * nt + i, 0)),
                 S((2 * h, l), F32), sem=("parallel",))(own, got)


def join_halves(bufs, name):
    na = len(bufs)

    def body(*refs):
        out_refs = refs[na:2 * na]
        send_sems, recv_sems = refs[2 * na:]
        mx, my, mc = _me()
        sib = (mx, my, 1 - mc)

        def half(a, hc):
            h = bufs[a].shape[0] // 2
            return out_refs[a].at[pl.ds(hc * h, h), :]

        cps = [_remote(half(a, mc), half(a, mc), send_sems, recv_sems, a, sib) for a in range(na)]
        for cp in cps:
            cp.start()
        for a in range(na):
            _remote(half(a, mc), half(a, 1 - mc), send_sems, recv_sems, a, sib).wait_recv()
        for cp in cps:
            cp.wait_send()

    return pl.pallas_call(
        body, name=name, out_shape=[S(b.shape, b.dtype) for b in bufs], in_specs=[ANY] * na, out_specs=[ANY] * na,
        input_output_aliases={a: a for a in range(na)},
        scratch_shapes=[pltpu.SemaphoreType.DMA((na,)), pltpu.SemaphoreType.DMA((na,))])(*bufs)


def exchange_chips_start(parts, name):
    na = len(parts)
    lands = [lax.empty((3,) + p.shape[1:], p.dtype) for p in parts]

    def body(*refs):
        a_refs, land_refs = refs[:na], refs[na:2 * na]
        send_sems, recv_sems = refs[2 * na], refs[2 * na + 1]
        mx, my, mc = _me()
        for j, (fx, fy) in enumerate(CHIP_FLIPS):
            px, py = _flip(mx, fx), _flip(my, fy)
            for a in range(na):
                _remote(a_refs[a].at[2 * px + py], land_refs[a].at[j], send_sems, recv_sems, 3 * a + j, (px, py, mc)).start()
        refs[-1][...] = jnp.zeros_like(refs[-1])

    outs = pl.pallas_call(
        body, name=name,
        out_shape=(pltpu.SemaphoreType.DMA((3 * na,)), pltpu.SemaphoreType.DMA((3 * na,)))
        + tuple(pltpu.HBM(t.shape, t.dtype) for t in list(parts) + lands) + (S((8, LANE), F32),),
        in_specs=[HBM] * (2 * na), out_specs=(SEM, SEM) + (HBM,) * (2 * na) + (pl.BlockSpec(memory_space=pltpu.VMEM),),
        input_output_aliases={a: 2 + a for a in range(2 * na)},
        compiler_params=pltpu.CompilerParams(has_side_effects=DATAFLOW),
    )(*[pltpu.with_memory_space_constraint(t, pltpu.HBM) for t in list(parts) + lands])
    return outs[0], outs[1], list(outs[2:2 + na]), list(outs[2 + na:2 + 2 * na]), outs[-1]


def exchange_chips_wait(send_sems, recv_sems, parts, lands, after, name):
    na = len(parts)

    def body(*refs):
        a_refs, land_refs = refs[:na], refs[na:2 * na]
        send_sems, recv_sems = refs[2 * na], refs[2 * na + 1]
        mx, my, mc = _me()
        for j, (fx, fy) in enumerate(CHIP_FLIPS):
            px, py = _flip(mx, fx), _flip(my, fy)
            for a in range(na):
                cp = _remote(a_refs[a].at[2 * px + py], land_refs[a].at[j], send_sems, recv_sems, 3 * a + j, (px, py, mc))
                cp.wait_send()
                cp.wait_recv()

    outs = pl.pallas_call(
        body, name=name, out_shape=tuple(pltpu.HBM(t.shape, t.dtype) for t in list(parts) + list(lands)),
        in_specs=[HBM] * (2 * na) + [SEM, SEM, pl.BlockSpec(memory_space=pl.ANY)], out_specs=(HBM,) * (2 * na),
        input_output_aliases={a: a for a in range(2 * na)},
        compiler_params=pltpu.CompilerParams(has_side_effects=DATAFLOW),
    )(*parts, *lands, send_sems, recv_sems, after)
    return list(outs[na:])


def reduce_begin(gs, tag):
    got = exchange_halves(gs, f"{tag}_d2d")
    sums = [add_halves(g, r, f"{tag}_add1_{i}") for i, (g, r) in enumerate(zip(gs, got))]
    return [own for _, own in sums], exchange_chips_start([a for a, _ in sums], f"{tag}_ici_start")


def reduce_end(state, after, tag):
    owns, (send_sems, recv_sems, parts, lands, _) = state
    got = exchange_chips_wait(send_sems, recv_sems, parts, lands, after, f"{tag}_ici_wait")
    return join_halves([add_chips(own, r, f"{tag}_add2_{i}") for i, (own, r) in enumerate(zip(owns, got))], f"{tag}_join")


PACK_L = 1024
BIG_ROW_MULT = 512


def _pack(arrs, dtype, row_mult, lead=None):
    lead_shape = () if lead is None else arrs[0].shape[:lead]
    flat = jnp.concatenate([a.astype(dtype).reshape(lead_shape + (-1,)) for a in arrs], axis=-1)
    n = flat.shape[-1]
    unit = row_mult * PACK_L
    total = -(-n // unit) * unit
    flat = jnp.pad(flat, [(0, 0)] * len(lead_shape) + [(0, total - n)])
    return flat.reshape(lead_shape + (total // PACK_L, PACK_L))


def _unpack(packed, shapes, lead=None):
    lead_shape = () if lead is None else packed.shape[:lead]
    flat = packed.reshape(lead_shape + (-1,))
    out, off = [], 0
    for shp in shapes:
        n = int(np.prod(shp))
        out.append(flat[..., off:off + n].reshape(lead_shape + tuple(shp)))
        off += n
    return out


def _to_full(gathered, axis):
    t = jnp.moveaxis(gathered, 0, axis)
    shp = t.shape
    return t.reshape(shp[:axis] + (shp[axis] * shp[axis + 1],) + shp[axis + 2:])


def _to_chip_major(full, axis):
    shp = full.shape
    t = full.reshape(shp[:axis] + (N_CHIPS, shp[axis] // N_CHIPS) + shp[axis + 1:])
    return jnp.moveaxis(t, axis, 0)


WEIGHTS = ("norm_ffn1", "ffn1_w_in", "ffn1_w_out", "norm_mix", "norm_ffn2", "ffn2_w_in", "ffn2_w_out", "ple_norm",
           "ple_gate_w", "ple_proj_w", "hyb_w_in", "conv_dw_w", "conv_dw_b", "conv_ln_g", "conv_ln_b", "ssm_conv_w",
           "ssm_conv_b", "ssm_dt_bias", "ssm_a_log", "ssm_d", "ssm_norm", "hyb_w_out", "att_w_qkv", "att_b_qkv",
           "att_sinks", "att_w_o", "att_b_o", "final_norm")
SHARD_AXIS = dict(ffn1_w_in=2, ffn1_w_out=1, ffn2_w_in=2, ffn2_w_out=1, ple_gate_w=1, ple_proj_w=2, hyb_w_in=2,
                  conv_dw_w=2, ssm_conv_w=2, hyb_w_out=1, att_w_qkv=2, att_b_qkv=1, att_w_o=1, att_b_o=1)
BIG = ("ffn1_w_in", "ffn1_w_out", "ffn2_w_in", "ffn2_w_out", "ple_gate_w", "ple_proj_w", "hyb_w_in", "hyb_w_out",
       "att_w_qkv", "att_w_o")
ODD_WIDTH = "hyb_w_in"
BIG_FLAT = tuple(k for k in BIG if k != ODD_WIDTH)
SMALL_SHARDED = ("conv_dw_w", "ssm_conv_w", "att_b_qkv", "att_b_o")
SMALL = tuple(k for k in WEIGHTS if k not in BIG)


def _step(x, p, target, w, m, v):
    mx, my = lax.axis_index("x"), lax.axis_index("y")
    chip = 2 * mx + my

    depth = w["norm_ffn1"].shape[0]
    order = sorted([(k, i) for i in range(depth) for k in BIG if _layer_index(k, i) is not None],
                   key=lambda t: (t[1], _stage(t[0])))
    small_g = all_gather_devices(_pack([w[k] for k in SMALL_SHARDED], F32, 8), "gather_small")
    shards = [w[k][_layer_index(k, i)].astype(MXU_DTYPE) for k, i in order]
    lands = [lax.dynamic_update_slice_in_dim(lax.empty((N_CHIPS,) + s.shape, s.dtype), s[None], chip, axis=0) for s in shards]
    send_sems, recv_sems, shards, lands = gather_start(shards, lands, small_g, "gather_start")

    def fetch(i, k, after):
        p = order.index((k, i))
        g, = gather_wait(send_sems, recv_sems, [shards[p]], [lands[p]], p, after, f"gather_wait_l{i}_{k}")
        if k == ODD_WIDTH:
            return _to_full(g, 1)
        if SHARD_AXIS[k] == 2:
            return ColSharded(g)
        return g.reshape(-1, g.shape[-1])

    small_g = small_g[0::2]
    small_full = {k: _to_full(g, SHARD_AXIS[k])
                  for k, g in zip(SMALL_SHARDED, _unpack(small_g, [w[k].shape for k in SMALL_SHARDED], lead=1))}
    layers = [LayerWeights({k: small_full.get(k, w[k])[_layer_index(k, i)] for k in SMALL if _layer_index(k, i) is not None},
                           functools.partial(fetch, i)) for i in range(depth)]

    sink = GradSink({k: w[k].shape for k in BIG}, depth)
    begun = {}

    def layer_done(i):
        begun[i] = reduce_begin(list(sink.bufs[i].values()), f"grads_l{i}")
        return begun[i][1][-1]

    loss, dx, grads = trunk_fwd_bwd(x[0], p[:, 0], target[0], layers, w["final_norm"], sink, layer_done)

    reduced = {i: dict(zip(sink.bufs[i], reduce_end(begun[i], dx, f"grads_l{i}"))) for i in reversed(range(depth))}
    g_out = {k: sink.take(k, w[k].shape[0], reduced) for k in BIG}
    vec = _pack([loss[0:1, 0:1]] + [grads[k] for k in SMALL], F32, 8)
    vec = sum_devices(all_gather_devices(vec, "gather_vectors"), "sum_vectors")
    parts = _unpack(vec, [(1, 1)] + [grads[k].shape for k in SMALL])
    loss_out = parts[0].reshape(())
    for k, g in zip(SMALL, parts[1:]):
        if k in SHARD_AXIS:
            ax = SHARD_AXIS[k]
            g = lax.dynamic_slice_in_dim(g, chip * w[k].shape[ax], w[k].shape[ax], axis=ax)
        g_out[k] = g

    delta, new_m, new_v = {}, {}, {}
    for k in BIG:
        shp = w[k].shape
        two_d = lambda a: a.reshape(-1, shp[-1])
        d_, m_, v_ = adamw(two_d(w[k]), two_d(g_out[k]), two_d(m[k]), two_d(v[k]), f"adamw_{k}")
        delta[k], new_m[k], new_v[k] = d_.reshape(shp), m_.reshape(shp), v_.reshape(shp)
    shapes = [w[k].shape for k in SMALL]
    packed = [_pack([src[k] for k in SMALL], F32, 8) for src in (w, g_out, m, v)]
    outs = adamw(*packed, "adamw_small")
    for dst, o in zip((delta, new_m, new_v), outs):
        for k, a in zip(SMALL, _unpack(o, shapes)):
            dst[k] = a
    return ((loss_out, dx[None]) + tuple(g_out[k] for k in WEIGHTS) + tuple(delta[k] for k in WEIGHTS)
            + tuple(new_m[k] for k in WEIGHTS) + tuple(new_v[k] for k in WEIGHTS))


def kernel(x, p, norm_ffn1, ffn1_w_in, ffn1_w_out, norm_mix, norm_ffn2, ffn2_w_in, ffn2_w_out, ple_norm, ple_gate_w, ple_proj_w, hyb_w_in, conv_dw_w, conv_dw_b, conv_ln_g, conv_ln_b, ssm_conv_w, ssm_conv_b, ssm_dt_bias, ssm_a_log, ssm_d, ssm_norm, hyb_w_out, att_w_qkv, att_b_qkv, att_sinks, att_w_o, att_b_o, final_norm, loss_target, m_norm_ffn1, m_ffn1_w_in, m_ffn1_w_out, m_norm_mix, m_norm_ffn2, m_ffn2_w_in, m_ffn2_w_out, m_ple_norm, m_ple_gate_w, m_ple_proj_w, m_hyb_w_in, m_conv_dw_w, m_conv_dw_b, m_conv_ln_g, m_conv_ln_b, m_ssm_conv_w, m_ssm_conv_b, m_ssm_dt_bias, m_ssm_a_log, m_ssm_d, m_ssm_norm, m_hyb_w_out, m_att_w_qkv, m_att_b_qkv, m_att_sinks, m_att_w_o, m_att_b_o, m_final_norm, v_norm_ffn1, v_ffn1_w_in, v_ffn1_w_out, v_norm_mix, v_norm_ffn2, v_ffn2_w_in, v_ffn2_w_out, v_ple_norm, v_ple_gate_w, v_ple_proj_w, v_hyb_w_in, v_conv_dw_w, v_conv_dw_b, v_conv_ln_g, v_conv_ln_b, v_ssm_conv_w, v_ssm_conv_b, v_ssm_dt_bias, v_ssm_a_log, v_ssm_d, v_ssm_norm, v_hyb_w_out, v_att_w_qkv, v_att_b_qkv, v_att_sinks, v_att_w_o, v_att_b_o, v_final_norm):
    given = locals()
    w = {k: given[k] for k in WEIGHTS}
    m = {k: given["m_" + k] for k in WEIGHTS}
    v = {k: given["v_" + k] for k in WEIGHTS}
    return _step(x, p, loss_target, w, m, v)
```

```python
import functools
import math

import numpy as np
import jax
import jax.numpy as jnp
from jax import lax
from jax.experimental import pallas as pl
from jax.experimental.pallas import tpu as pltpu

F32 = jnp.float32
BF16 = jnp.bfloat16
MXU_DTYPE = jnp.bfloat16
S = jax.ShapeDtypeStruct
MESH = pl.DeviceIdType.MESH

V7X_VMEM_BYTES = 64 * 2**20
VMEM_LIMIT = 48 * 2**20
LANE = 128

EPS = 1e-6
SSM_HEADS = 16
HEAD_DIM = 64
SSM_GROUPS = 2
SSM_STATE = 128
SSM_CONV = 4
CHUNK = 128
CONV_WIDTH = 31
ATT_HEADS = 16
ATT_KV_HEADS = 4
WINDOW = 128
ROPE_THETA = 10000.0
ADAM_LR = 0.001
ADAM_B1 = 0.9
ADAM_B2 = 0.999
ADAM_EPS = 1e-08
ADAM_WD = 0.01
ADAM_STEP = 10

N_CHIPS = 4
N_DEV = 8

NN = ((1,), (0,))
NT = ((1,), (1,))
TN = ((0,), (0,))


def _mm(a, b, dims=NN):
    return lax.dot_general(a.astype(MXU_DTYPE), b.astype(MXU_DTYPE), (dims, ((), ())), preferred_element_type=F32)


def _split3(a):
    hi = a.astype(BF16)
    r = a - hi.astype(F32)
    mid = r.astype(BF16)
    lo = (r - mid.astype(F32)).astype(BF16)
    return hi, mid, lo


def _mm01(a, onehot, dims=NN):
    o = onehot.astype(BF16)
    out = None
    for part in _split3(a):
        t = lax.dot_general(part, o, (dims, ((), ())), preferred_element_type=F32)
        out = t if out is None else out + t
    return out


def _01mm(onehot, a):
    o = onehot.astype(BF16)
    out = None
    for part in _split3(a):
        t = lax.dot_general(o, part, (NN, ((), ())), preferred_element_type=F32)
        out = t if out is None else out + t
    return out


def _sigmoid(x):
    return 0.5 * jnp.tanh(0.5 * x) + 0.5


def _softplus(x):
    return jnp.maximum(x, 0.0) + jnp.log(1.0 + jnp.exp(-jnp.abs(x)))


def _iota(shape, axis):
    return lax.broadcasted_iota(jnp.int32, shape, axis)


def _head_indicator(width, heads, transposed=False):
    per = width // heads
    if transposed:
        return (_iota((heads, width), 1) // per == _iota((heads, width), 0)).astype(F32)
    return (_iota((width, heads), 0) // per == _iota((width, heads), 1)).astype(F32)


def _acc(ref, i, val):
    @pl.when(i == 0)
    def _():
        ref[...] = val

    @pl.when(i > 0)
    def _():
        ref[...] += val


def _rs(tile, width, col=0, shift=0, n=None):
    if shift == 0:
        return pl.BlockSpec((tile, width), lambda i: (i, col))
    if shift < 0:
        return pl.BlockSpec((tile, width), lambda i: (jnp.maximum(i - 1, 0), col))
    return pl.BlockSpec((tile, width), lambda i: (jnp.minimum(i + 1, n - 1), col))


def _ps(shape):
    return pl.BlockSpec(shape, lambda i: (0,) * len(shape))


def _call(body, name, grid, in_specs, out_specs, out_shape, scratch=(), sem=None):
    return pl.pallas_call(
        body, name=name, grid=grid, in_specs=in_specs, out_specs=out_specs, out_shape=out_shape,
        scratch_shapes=list(scratch),
        compiler_params=pltpu.CompilerParams(dimension_semantics=sem, vmem_limit_bytes=VMEM_LIMIT))


def _row_tile(n, target):
    t = min(n, target)
    assert n % t == 0, (n, t)
    return t


def _pick_tile(dim, target):
    if dim <= target:
        return dim
    t = (int(1.4 * target) // LANE) * LANE
    while t >= LANE:
        if dim % t == 0:
            return t
        t -= LANE
    return dim


ANY = pl.BlockSpec(memory_space=pl.ANY)


def _paired(j):
    return (j % 2) * 2 + j // 2


class ColSharded:
    def __init__(self, arr, paired=False):
        self.arr, self.paired = arr, paired
        self.nch, self.rows, self.per = arr.shape
        self.shape = (self.rows, self.nch * self.per)

    def chip(self, j):
        return _paired(j) if self.paired else j


class Slot:
    def __init__(self, buf, kind, per, off, c0=0, paired=False):
        self.buf, self.kind, self.per, self.off, self.c0, self.paired = buf, kind, per, off, c0, paired

    def chip(self, j):
        return _paired(j) if self.paired else j


def matmul(a, b, mode, name, *, out_dtype=F32, scale=None, res=None, bias=None, into=None, tm=1024, tn=1024, tk=1024):
    bshape = b.shape
    if mode == "nn":
        (m, k), (k2, n) = a.shape, bshape
    elif mode == "nt":
        (m, k), (n, k2) = a.shape, bshape
    else:
        (k, m), (k2, n) = a.shape, bshape
    assert k == k2, (a.shape, bshape, mode)
    tm, tn, tk = _pick_tile(m, tm), _pick_tile(n, tn), _pick_tile(k, tk)
    if isinstance(b, ColSharded):
        if mode == "nn":
            tn = b.per
        else:
            assert mode == "nt"
            tk = b.per
    if into is not None:
        if into.kind == "c":
            tn = into.per
            assert into.off % tm == 0 and n == N_CHIPS * into.per
        else:
            tm = max(1, min(m, int(1.4 * 1024)) // into.per) * into.per
            assert m % tm == 0 and into.off % into.per == 0 and into.c0 % (tm // into.per) == 0
    nk = k // tk
    dims = {"nn": NN, "nt": NT, "tn": TN}[mode]
    a_spec = (pl.BlockSpec((tk, tm), lambda i, j, kk: (kk, i)) if mode == "tn"
              else pl.BlockSpec((tm, tk), lambda i, j, kk: (i, kk)))
    if isinstance(b, ColSharded):
        bchip = b.chip
        b_spec = (pl.BlockSpec((None, tk, tn), lambda i, j, kk: (bchip(j), kk, 0)) if mode == "nn"
                  else pl.BlockSpec((None, tn, tk), lambda i, j, kk: (bchip(kk), j, 0)))
        b = b.arr
    else:
        b_spec = (pl.BlockSpec((tn, tk), lambda i, j, kk: (j, kk)) if mode == "nt"
                  else pl.BlockSpec((tk, tn), lambda i, j, kk: (kk, j)))
    plain_o = pl.BlockSpec((tm, tn), lambda i, j, kk: (i, j))
    ins, in_specs = [a, b], [a_spec, b_spec]
    if bias is not None:
        ins.append(bias)
        in_specs.append(pl.BlockSpec((1, tn), lambda i, j, kk: (0, j)))
    if res is not None:
        ins.append(res)
        in_specs.append(plain_o)
    aliases = {}
    if into is None:
        o_spec, o_shape = plain_o, S((m, n), out_dtype)
    else:
        aliases = {len(ins): 0}
        ins.append(into.buf)
        in_specs.append(ANY)
        o_shape = S(into.buf.shape, into.buf.dtype)
        if into.kind == "c":
            ob, ochip = into.off // tm, into.chip
            o_spec = pl.BlockSpec((None, tm, tn), lambda i, j, kk: (ochip(j), ob + i, 0))
        else:
            q, ob = tm // into.per, into.off // into.per
            cb = into.c0 // q
            o_spec = pl.BlockSpec((q, into.per, tn), lambda i, j, kk: (cb + i, ob, j))

    def body(*refs):
        a_ref, b_ref = refs[0], refs[1]
        o_ref, acc_ref = refs[-2], refs[-1]
        kk = pl.program_id(2)

        @pl.when(kk == 0)
        def _():
            acc_ref[...] = jnp.zeros_like(acc_ref)

        acc_ref[...] += _mm(a_ref[...], b_ref[...], dims)

        @pl.when(kk == nk - 1)
        def _():
            out = acc_ref[...]
            if scale is not None:
                out = out * scale
            pos = 2
            if bias is not None:
                out = out + refs[pos][...]
                pos += 1
            if res is not None:
                out = out + refs[pos][...]
            o_ref[...] = out.astype(o_ref.dtype).reshape(o_ref.shape)

    return pl.pallas_call(
        body, name=name, grid=(m // tm, n // tn, nk), in_specs=in_specs, out_specs=o_spec, out_shape=o_shape,
        scratch_shapes=[pltpu.VMEM((tm, tn), F32)], input_output_aliases=aliases,
        compiler_params=pltpu.CompilerParams(dimension_semantics=("parallel", "parallel", "arbitrary"),
                                             vmem_limit_bytes=VMEM_LIMIT))(*ins)


def rms_fwd(h, g, name):
    n, d = h.shape
    tile = _row_tile(n, 512)

    def body(h_ref, g_ref, o_ref):
        x = h_ref[...]
        r = lax.rsqrt(jnp.mean(x * x, axis=-1, keepdims=True) + EPS)
        o_ref[...] = (x * r * g_ref[...]).astype(o_ref.dtype)

    return _call(body, name, (n // tile,), [_rs(tile, d), _ps((1, d))], _rs(tile, d), S((n, d), MXU_DTYPE),
                 sem=("parallel",))(h, g)


def _rms_bwd_math(x, g, dy):
    r = lax.rsqrt(jnp.mean(x * x, axis=-1, keepdims=True) + EPS)
    xh = x * r
    dg = jnp.sum(dy * xh, axis=0, keepdims=True)
    dxh = dy * g
    dx = r * (dxh - xh * jnp.mean(dxh * xh, axis=-1, keepdims=True))
    return dx, dg


def rms_bwd(h, g, dxn, dh_in, name, colsum=False):
    n, d = h.shape
    tile = _row_tile(n, 256)

    def body(h_ref, g_ref, dxn_ref, dh_ref, o_ref, dg_ref, *cs_ref):
        i = pl.program_id(0)
        dx, dg = _rms_bwd_math(h_ref[...], g_ref[...], dxn_ref[...].astype(F32))
        out = dh_ref[...] + dx
        o_ref[...] = out
        _acc(dg_ref, i, dg)
        if colsum:
            _acc(cs_ref[0], i, jnp.sum(out, axis=0, keepdims=True))

    outs = [S((n, d), F32), S((1, d), F32)] + ([S((1, d), F32)] if colsum else [])
    ospecs = [_rs(tile, d), _ps((1, d))] + ([_ps((1, d))] if colsum else [])
    return _call(body, name, (n // tile,), [_rs(tile, d), _ps((1, d)), _rs(tile, d), _rs(tile, d)], ospecs, outs,
                 sem=("arbitrary",))(h, g, dxn, dh_in)


def swiglu_in(h, g, w_in, name):
    n, d = h.shape
    per = w_in.per
    nj = w_in.nch // 2
    tile = _row_tile(n, 512)

    def body(h_ref, g_ref, wg_ref, wu_ref, xn_ref, u_ref, hm_ref):
        x = h_ref[...]
        r = lax.rsqrt(jnp.mean(x * x, axis=-1, keepdims=True) + EPS)
        xn = (x * r * g_ref[...]).astype(xn_ref.dtype)

        @pl.when(pl.program_id(1) == 0)
        def _():
            xn_ref[...] = xn

        a = _mm(xn, wg_ref[...])
        b = _mm(xn, wu_ref[...])
        u_ref[:, :per] = a
        u_ref[:, per:] = b
        hm_ref[...] = (a * _sigmoid(a) * b).astype(hm_ref.dtype)

    return pl.pallas_call(
        body, name=name, grid=(n // tile, nj),
        in_specs=[pl.BlockSpec((tile, d), lambda i, j: (i, 0)), pl.BlockSpec((1, d), lambda i, j: (0, 0)),
                  pl.BlockSpec((None, d, per), lambda i, j: (j, 0, 0)), pl.BlockSpec((None, d, per), lambda i, j: (nj + j, 0, 0))],
        out_specs=[pl.BlockSpec((tile, d), lambda i, j: (i, 0)), pl.BlockSpec((tile, 2 * per), lambda i, j: (i, j)),
                   pl.BlockSpec((tile, per), lambda i, j: (i, j))],
        out_shape=[S((n, d), MXU_DTYPE), S((n, 2 * nj * per), F32), S((n, nj * per), MXU_DTYPE)],
        compiler_params=pltpu.CompilerParams(dimension_semantics=("parallel", "arbitrary"), vmem_limit_bytes=VMEM_LIMIT),
    )(h, g, w_in.arr, w_in.arr)


def swiglu_out_bwd(dh, w_out, u, after, name):
    n, d = dh.shape
    f = w_out.shape[0]
    per = u.shape[1] // 4
    nj = f // per
    tile = _row_tile(n, 512)

    def body(dh_ref, w_ref, u_ref, after_ref, du_ref):
        dm = 0.5 * _mm(dh_ref[...], w_ref[...], NT)
        a = u_ref[:, :per]
        b = u_ref[:, per:]
        s = _sigmoid(a)
        du_ref[:, :per] = (dm * b * s * (1.0 + a * (1.0 - s))).astype(du_ref.dtype)
        du_ref[:, per:] = (dm * a * s).astype(du_ref.dtype)

    return pl.pallas_call(
        body, name=name, grid=(n // tile, nj),
        in_specs=[pl.BlockSpec((tile, d), lambda i, j: (i, 0)), pl.BlockSpec((per, d), lambda i, j: (j, 0)),
                  pl.BlockSpec((tile, 2 * per), lambda i, j: (i, j)), ANY],
        out_specs=pl.BlockSpec((tile, 2 * per), lambda i, j: (i, j)),
        out_shape=S(u.shape, MXU_DTYPE),
        compiler_params=pltpu.CompilerParams(dimension_semantics=("parallel", "parallel"), vmem_limit_bytes=VMEM_LIMIT),
    )(dh, w_out, u, after)


def ple_fwd(h, gl, pp, name):
    n, d = h.shape
    tile = _row_tile(n, 512)

    def body(h_ref, gl_ref, pp_ref, o_ref):
        o_ref[...] = h_ref[...] + _sigmoid(gl_ref[...]) * pp_ref[...]

    return _call(body, name, (n // tile,), [_rs(tile, d)] * 3, _rs(tile, d), S((n, d), F32), sem=("parallel",))(h, gl, pp)


def ple_bwd(dh, gl, pp, after, name):
    n, d = dh.shape
    tile = _row_tile(n, 512)

    def body(dh_ref, gl_ref, pp_ref, after_ref, dpp_ref, dgl_ref):
        g = _sigmoid(gl_ref[...])
        dh_ = dh_ref[...]
        dpp_ref[...] = (dh_ * g).astype(dpp_ref.dtype)
        dgl_ref[...] = (dh_ * pp_ref[...] * g * (1.0 - g)).astype(dgl_ref.dtype)

    return _call(body, name, (n // tile,), [_rs(tile, d)] * 3 + [ANY], [_rs(tile, d)] * 2, [S((n, d), MXU_DTYPE)] * 2,
                 sem=("parallel",))(dh, gl, pp, after)


def loss_head(h, g, target, name):
    n, d = h.shape
    tile = _row_tile(n, 256)

    def body(h_ref, g_ref, t_ref, dh_ref, dg_ref, loss_ref):
        i = pl.program_id(0)
        x = h_ref[...]
        gg = g_ref[...]
        r = lax.rsqrt(jnp.mean(x * x, axis=-1, keepdims=True) + EPS)
        err = x * r * gg - t_ref[...]
        part = 0.5 * jnp.sum(jnp.mean(err * err, axis=-1, keepdims=True), axis=0, keepdims=True)
        dx, dg = _rms_bwd_math(x, gg, err * (1.0 / d))
        dh_ref[...] = dx
        _acc(dg_ref, i, dg)
        _acc(loss_ref, i, jnp.broadcast_to(part, (8, LANE)))

    return _call(body, name, (n // tile,), [_rs(tile, d), _ps((1, d)), _rs(tile, d)],
                 [_rs(tile, d), _ps((1, d)), _ps((8, LANE))], [S((n, d), F32), S((1, d), F32), S((8, LANE), F32)],
                 sem=("arbitrary",))(h, g, target)


def adamw(w, g, m, v, name):
    r, c = w.shape
    tile = r
    for t in (512, 256, 128, 64, 32, 16, 8):
        if r % t == 0 and t * c * 4 <= 2**21:
            tile = t
            break
    c1 = np.float32(1.0 - ADAM_B1 ** ADAM_STEP)
    c2 = np.float32(1.0 - ADAM_B2 ** ADAM_STEP)

    def body(w_ref, g_ref, m_ref, v_ref, d_ref, mo_ref, vo_ref):
        gg = g_ref[...]
        mm = ADAM_B1 * m_ref[...] + (1.0 - ADAM_B1) * gg
        vv = ADAM_B2 * v_ref[...] + (1.0 - ADAM_B2) * (gg * gg)
        mo_ref[...] = mm
        vo_ref[...] = vv
        d_ref[...] = -ADAM_LR * ((mm / c1) / (jnp.sqrt(vv / c2) + ADAM_EPS) + ADAM_WD * w_ref[...])

    return _call(body, name, (r // tile,), [_rs(tile, c)] * 4, [_rs(tile, c)] * 3, [S((r, c), F32)] * 3,
                 sem=("parallel",))(w, g, m, v)


def _taps_fwd(sc, w_ref, width, halo, tile, acc):
    for k in range(width):
        o = halo - (width - 1) + k
        acc = acc + w_ref[k:k + 1, :] * sc[o:o + tile, :]
    return acc


def _taps_bwd_x(sc_d, w_ref, width, tile, acc):
    for k in range(width):
        o = (width - 1) - k
        acc = acc + w_ref[k:k + 1, :] * sc_d[o:o + tile, :]
    return acc


def _taps_bwd_w(dy, sc, dw_ref, width, halo, tile, i):
    @pl.when(i == 0)
    def _():
        dw_ref[...] = jnp.zeros_like(dw_ref)

    for k in range(width):
        o = halo - (width - 1) + k
        dw_ref[k:k + 1, :] += jnp.sum(dy * sc[o:o + tile, :], axis=0, keepdims=True)


def _ln_stats(x):
    mu = jnp.mean(x, axis=-1, keepdims=True)
    xc = x - mu
    r = lax.rsqrt(jnp.mean(xc * xc, axis=-1, keepdims=True) + EPS)
    return xc * r, r


def conv_group_fwd(proj, cw, cb, lg, lb, name):
    n = proj.shape[0]
    d = cw.shape[1]
    tile = _row_tile(n, 256)
    halo = 32

    def body(v_ref, g_ref, vp_ref, gp_ref, cw_ref, cb_ref, lg_ref, lb_ref, u_ref, u1_ref, sc):
        i = pl.program_id(0)
        first = (i > 0).astype(F32)
        sc[0:halo, :] = vp_ref[tile - halo:, :] * _sigmoid(gp_ref[tile - halo:, :]) * first
        sc[halo:, :] = v_ref[...] * _sigmoid(g_ref[...])
        u1 = _taps_fwd(sc, cw_ref, CONV_WIDTH, halo, tile, jnp.zeros((tile, d), F32) + cb_ref[...])
        u1_ref[...] = u1
        xh, _ = _ln_stats(u1)
        y = xh * lg_ref[...] + lb_ref[...]
        u_ref[...] = (y * _sigmoid(y)).astype(u_ref.dtype)

    return _call(body, name, (n // tile,),
                 [_rs(tile, d, 0), _rs(tile, d, 1), _rs(tile, d, 0, -1), _rs(tile, d, 1, -1),
                  _ps(cw.shape), _ps((1, d)), _ps((1, d)), _ps((1, d))],
                 [_rs(tile, d), _rs(tile, d)], [S((n, d), MXU_DTYPE), S((n, d), F32)],
                 scratch=[pltpu.VMEM((halo + tile, d), F32)], sem=("arbitrary",))(proj, proj, proj, proj, cw, cb, lg, lb)


def conv_group_bwd(du, u1, proj, cw, lg, lb, name):
    n = proj.shape[0]
    d = cw.shape[1]
    tile = _row_tile(n, 256)
    halo = 32
    nt = n // tile

    def body(du_ref, dun_ref, u1_ref, u1n_ref, v_ref, g_ref, vp_ref, gp_ref, cw_ref, lg_ref, lb_ref,
             dp_ref, dcw_ref, dcb_ref, dlg_ref, dlb_ref, sc, sc_d):
        i = pl.program_id(0)

        def ln_swish_bwd(dy_, u1_):
            xh, r = _ln_stats(u1_)
            y = xh * lg_ref[...] + lb_ref[...]
            s = _sigmoid(y)
            dyy = dy_ * s * (1.0 + y * (1.0 - s))
            dxh = dyy * lg_ref[...]
            dx = r * (dxh - jnp.mean(dxh, axis=-1, keepdims=True) - xh * jnp.mean(dxh * xh, axis=-1, keepdims=True))
            return dx, jnp.sum(dyy * xh, axis=0, keepdims=True), jnp.sum(dyy, axis=0, keepdims=True)

        du1, dlg, dlb = ln_swish_bwd(du_ref[...].astype(F32), u1_ref[...])
        du1n, _, _ = ln_swish_bwd(dun_ref[0:halo, :].astype(F32), u1n_ref[0:halo, :])
        sc_d[0:tile, :] = du1
        sc_d[tile:, :] = du1n * (i < nt - 1).astype(F32)
        sig = _sigmoid(g_ref[...])
        val = v_ref[...]
        sc[0:halo, :] = vp_ref[tile - halo:, :] * _sigmoid(gp_ref[tile - halo:, :]) * (i > 0).astype(F32)
        sc[halo:, :] = val * sig
        du0 = _taps_bwd_x(sc_d, cw_ref, CONV_WIDTH, tile, jnp.zeros((tile, d), F32))
        _taps_bwd_w(du1, sc, dcw_ref, CONV_WIDTH, halo, tile, i)
        _acc(dcb_ref, i, jnp.sum(du1, axis=0, keepdims=True))
        _acc(dlg_ref, i, dlg)
        _acc(dlb_ref, i, dlb)
        dp_ref[:, :d] = (du0 * sig).astype(dp_ref.dtype)
        dp_ref[:, d:] = (du0 * val * sig * (1.0 - sig)).astype(dp_ref.dtype)

    return _call(body, name, (nt,),
                 [_rs(tile, d), _rs(tile, d, 0, 1, nt), _rs(tile, d), _rs(tile, d, 0, 1, nt),
                  _rs(tile, d, 0), _rs(tile, d, 1), _rs(tile, d, 0, -1), _rs(tile, d, 1, -1),
                  _ps(cw.shape), _ps((1, d)), _ps((1, d))],
                 [_rs(tile, 2 * d), _ps(cw.shape), _ps((1, d)), _ps((1, d)), _ps((1, d))],
                 [S((n, proj.shape[1]), MXU_DTYPE), S(cw.shape, F32), S((1, d), F32), S((1, d), F32), S((1, d), F32)],
                 scratch=[pltpu.VMEM((halo + tile, d), F32), pltpu.VMEM((tile + halo, d), F32)],
                 sem=("arbitrary",))(du, du, u1, u1, proj, proj, proj, proj, cw, lg, lb)


def ssm_conv_fwd(proj, dtr, sw, sb, dtb, name):
    n = proj.shape[0]
    w = sw.shape[1]
    inner = SSM_HEADS * HEAD_DIM
    tile = _row_tile(n, 256)
    halo = 8

    def body(x_ref, xp_ref, dtr_ref, sw_ref, sb_ref, dtb_ref, pre_ref, xs_ref, bc_ref, dt_ref, sc):
        i = pl.program_id(0)
        sc[0:halo, :] = xp_ref[tile - halo:, :] * (i > 0).astype(F32)
        sc[halo:, :] = x_ref[...]
        pre = _taps_fwd(sc, sw_ref, SSM_CONV, halo, tile, jnp.zeros((tile, w), F32) + sb_ref[...])
        pre_ref[...] = pre
        act = pre * _sigmoid(pre)
        xs_ref[...] = act[:, :inner]
        bc_ref[...] = act[:, inner:]
        dt = _softplus(dtr_ref[...] + dtb_ref[...])
        dt_ref[...] = jnp.where(_iota(dt.shape, 1) < SSM_HEADS, dt, 0.0)

    return _call(body, name, (n // tile,),
                 [_rs(tile, w, 2), _rs(tile, w, 2, -1), _rs(tile, LANE), _ps(sw.shape), _ps((1, w)), _ps((1, LANE))],
                 [_rs(tile, w), _rs(tile, inner), _rs(tile, w - inner), _rs(tile, LANE)],
                 [S((n, w), F32), S((n, inner), F32), S((n, w - inner), F32), S((n, LANE), F32)],
                 scratch=[pltpu.VMEM((halo + tile, w), F32)], sem=("arbitrary",))(proj, proj, dtr, sw, sb, dtb)


def ssm_conv_bwd(dxs, dbc, pre, proj, sw, dproj, name):
    n = proj.shape[0]
    w = sw.shape[1]
    inner = SSM_HEADS * HEAD_DIM
    tile = _row_tile(n, 256)
    halo = 8
    nt = n // tile

    def body(dxs_ref, dxsn_ref, dbc_ref, dbcn_ref, pre_ref, pren_ref, x_ref, xp_ref, sw_ref, dp_in_ref,
             dx_ref, dsw_ref, dsb_ref, sc, sc_d):
        i = pl.program_id(0)

        def silu_bwd(d_, p_):
            s = _sigmoid(p_)
            return d_ * s * (1.0 + p_ * (1.0 - s))

        sc_d[0:tile, :inner] = silu_bwd(dxs_ref[...], pre_ref[:, :inner])
        sc_d[0:tile, inner:] = silu_bwd(dbc_ref[...], pre_ref[:, inner:])
        last = (i < nt - 1).astype(F32)
        sc_d[tile:, :inner] = silu_bwd(dxsn_ref[0:halo, :], pren_ref[0:halo, :inner]) * last
        sc_d[tile:, inner:] = silu_bwd(dbcn_ref[0:halo, :], pren_ref[0:halo, inner:]) * last
        sc[0:halo, :] = xp_ref[tile - halo:, :] * (i > 0).astype(F32)
        sc[halo:, :] = x_ref[...]
        dpre = sc_d[0:tile, :]
        dx_ref[...] = _taps_bwd_x(sc_d, sw_ref, SSM_CONV, tile, jnp.zeros((tile, w), F32)).astype(dx_ref.dtype)
        _taps_bwd_w(dpre, sc, dsw_ref, SSM_CONV, halo, tile, i)
        _acc(dsb_ref, i, jnp.sum(dpre, axis=0, keepdims=True))

    return pl.pallas_call(
        body, name=name, grid=(nt,),
        in_specs=[_rs(tile, inner), _rs(tile, inner, 0, 1, nt), _rs(tile, w - inner), _rs(tile, w - inner, 0, 1, nt),
                  _rs(tile, w), _rs(tile, w, 0, 1, nt), _rs(tile, w, 2), _rs(tile, w, 2, -1), _ps(sw.shape), ANY],
        out_specs=[_rs(tile, w, 2), _ps(sw.shape), _ps((1, w))],
        out_shape=[S(dproj.shape, dproj.dtype), S(sw.shape, F32), S((1, w), F32)],
        scratch_shapes=[pltpu.VMEM((halo + tile, w), F32), pltpu.VMEM((tile + halo, w), F32)],
        input_output_aliases={9: 0},
        compiler_params=pltpu.CompilerParams(dimension_semantics=("arbitrary",), vmem_limit_bytes=VMEM_LIMIT),
    )(dxs, dxs, dbc, dbc, pre, pre, proj, proj, sw, dproj)


def _ssd_prologue(dt_ref, dtT_ref, al_ref, alc_ref):
    row = _iota((CHUNK, CHUNK), 0)
    col = _iota((CHUNK, CHUNK), 1)
    dt = dt_ref[:, :SSM_HEADS]
    a_row = -jnp.exp(al_ref[:, :SSM_HEADS])
    a_col = -jnp.exp(alc_ref[...])
    cs = _01mm((row >= col).astype(F32), dt * a_row)
    csT = _mm01(dtT_ref[...] * a_col, (row <= col).astype(F32))
    return dt, a_row, cs, csT, row, col


def _decay(cs, csT, h, row, col):
    lm = jnp.exp(jnp.where(row >= col, cs[:, h:h + 1] - csT[h:h + 1, :], -1e30))
    lmT = jnp.exp(jnp.where(col >= row, csT[h:h + 1, :] - cs[:, h:h + 1], -1e30))
    return lm, lmT


def ssd_fwd(xs, bc, dt, dtT, alog_row, alog_col, name):
    n, width = xs.shape
    nc = n // CHUNK
    gw = width // SSM_GROUPS
    hpg = SSM_HEADS // SSM_GROUPS
    ns = SSM_STATE

    def body(xs_ref, bc_ref, dt_ref, dtT_ref, al_ref, alc_ref, y_ref, hs_ref, h_sc):
        i = pl.program_id(0)

        @pl.when(i == 0)
        def _():
            h_sc[...] = jnp.zeros_like(h_sc)

        dt, a_row, cs, csT, row, col = _ssd_prologue(dt_ref, dtT_ref, al_ref, alc_ref)
        indT = _head_indicator(width, SSM_HEADS, transposed=True)
        dt_full = _mm01(dt, indT)
        e_full = jnp.exp(_mm01(cs, indT))
        dte_full = jnp.exp(_mm01(cs[CHUNK - 1:CHUNK, :] - cs, indT))
        xt = xs_ref[...] * dt_full
        hs_ref[0] = h_sc[...]
        lo = _iota((CHUNK, 2 * HEAD_DIM), 1) < HEAD_DIM
        for g in range(SSM_GROUPS):
            bg = bc_ref[:, g * ns:(g + 1) * ns]
            cg = bc_ref[:, (SSM_GROUPS + g) * ns:(SSM_GROUPS + g + 1) * ns]
            gm = _mm(cg, bg, NT)
            hg = h_sc[g * gw:(g + 1) * gw, :]
            yoff = e_full[:, g * gw:(g + 1) * gw] * _mm(cg, hg, NT)
            for pr in range(hpg // 2):
                h0 = g * hpg + 2 * pr
                c0 = h0 * HEAD_DIM
                xp = xt[:, c0:c0 + 2 * HEAD_DIM]
                m0 = gm * _decay(cs, csT, h0, row, col)[0]
                m1 = gm * _decay(cs, csT, h0 + 1, row, col)[0]
                yd = jnp.where(lo, _mm(m0, xp), _mm(m1, xp))
                y_ref[:, c0:c0 + 2 * HEAD_DIM] = yd + yoff[:, 2 * pr * HEAD_DIM:(2 * pr + 2) * HEAD_DIM]
            sg = _mm(xt[:, g * gw:(g + 1) * gw] * dte_full[:, g * gw:(g + 1) * gw], bg, TN)
            for hh in range(hpg):
                h = g * hpg + hh
                r0 = h * HEAD_DIM
                h_sc[r0:r0 + HEAD_DIM, :] = (h_sc[r0:r0 + HEAD_DIM, :] * jnp.exp(csT[h:h + 1, CHUNK - 1:CHUNK])
                                             + sg[hh * HEAD_DIM:(hh + 1) * HEAD_DIM, :])

    bcw = bc.shape[1]
    return _call(body, name, (nc,),
                 [_rs(CHUNK, width), _rs(CHUNK, bcw), _rs(CHUNK, LANE), pl.BlockSpec((SSM_HEADS, CHUNK), lambda i: (0, i)),
                  _ps((1, LANE)), _ps((SSM_HEADS, 1))],
                 [_rs(CHUNK, width), pl.BlockSpec((1, width, ns), lambda i: (i, 0, 0))],
                 [S((n, width), F32), S((nc, width, ns), F32)],
                 scratch=[pltpu.VMEM((width, ns), F32)], sem=("arbitrary",))(xs, bc, dt, dtT, alog_row, alog_col)


def ssd_bwd(xs, bc, dt, dtT, alog_row, alog_col, hs, dy, dxs_skip, name):
    n, width = xs.shape
    nc = n // CHUNK
    gw = width // SSM_GROUPS
    hpg = SSM_HEADS // SSM_GROUPS
    ns = SSM_STATE
    bcw = bc.shape[1]

    def body(xs_ref, bc_ref, dt_ref, dtT_ref, al_ref, alc_ref, hs_ref, dy_ref, skip_ref,
             dxs_ref, dbc_ref, ddtr_ref, dal_ref, ddtb_ref, dh_sc, dxt_sc):
        i = pl.program_id(0)

        @pl.when(i == 0)
        def _():
            dh_sc[...] = jnp.zeros_like(dh_sc)

        dt, a_row, cs, csT, row, col = _ssd_prologue(dt_ref, dtT_ref, al_ref, alc_ref)
        indT = _head_indicator(width, SSM_HEADS, transposed=True)
        ind = _head_indicator(width, SSM_HEADS)
        dt_full = _mm01(dt, indT)
        e_full = jnp.exp(_mm01(cs, indT))
        cs_last = cs[CHUNK - 1:CHUNK, :]
        dte = jnp.exp(cs_last - cs)
        dte_full = _mm01(dte, indT)
        xs_ = xs_ref[...]
        xt = xs_ * dt_full
        dy_ = dy_ref[...]
        hmat = hs_ref[0]
        ds = dh_sc[...]
        lo = _iota((CHUNK, 2 * HEAD_DIM), 1) < HEAD_DIM
        head_lane = _iota((1, SSM_HEADS), 1)
        dcs = jnp.zeros((CHUNK, SSM_HEADS), F32)
        ddte = jnp.zeros((CHUNK, SSM_HEADS), F32)
        for g in range(SSM_GROUPS):
            sl = slice(g * gw, (g + 1) * gw)
            bg = bc_ref[:, g * ns:(g + 1) * ns]
            cg = bc_ref[:, (SSM_GROUPS + g) * ns:(SSM_GROUPS + g + 1) * ns]
            gm = _mm(cg, bg, NT)
            gmT = _mm(bg, cg, NT)
            hg = hmat[sl, :]
            dsg = ds[sl, :]
            dyg = dy_[:, sl]
            xtg = xt[:, sl]
            yoff = e_full[:, sl] * _mm(cg, hg, NT)
            edy = e_full[:, sl] * dyg
            bds = _mm(bg, dsg, NT)
            dxt_g = dte_full[:, sl] * bds
            ddte = ddte + _mm01(xtg * bds, ind[sl, :])
            dcs = dcs + _mm01(dyg * yoff, ind[sl, :])
            db = _mm(xtg * dte_full[:, sl], dsg)
            dc = _mm(edy, hg)
            dhc = _mm(edy, cg, TN)
            dgs = jnp.zeros((CHUNK, CHUNK), F32)
            dgTs = jnp.zeros((CHUNK, CHUNK), F32)
            for pr in range(hpg // 2):
                h0 = g * hpg + 2 * pr
                c0 = 2 * pr * HEAD_DIM
                xp = xtg[:, c0:c0 + 2 * HEAD_DIM]
                dyp = dyg[:, c0:c0 + 2 * HEAD_DIM]
                rr = []
                for h, half in ((h0, lo), (h0 + 1, jnp.logical_not(lo))):
                    lm, lmT = _decay(cs, csT, h, row, col)
                    xm = jnp.where(half, xp, 0.0)
                    dm = _mm(dyp, xm, NT)
                    dmT = _mm(xm, dyp, NT)
                    mT = gmT * lmT
                    z = jnp.sum(dm * (gm * lm), axis=1, keepdims=True) - jnp.sum(dmT * mT, axis=1, keepdims=True)
                    dcs = dcs + z * (head_lane == h).astype(F32)
                    dgs = dgs + dm * lm
                    dgTs = dgTs + dmT * lmT
                    rr.append(_mm(mT, dyp))
                dxt_sc[:, g * gw + c0:g * gw + c0 + 2 * HEAD_DIM] = jnp.where(lo, rr[0], rr[1]) + dxt_g[:, c0:c0 + 2 * HEAD_DIM]
            dbc_ref[:, g * ns:(g + 1) * ns] = db + _mm(dgTs, cg)
            dbc_ref[:, (SSM_GROUPS + g) * ns:(SSM_GROUPS + g + 1) * ns] = dc + _mm(dgs, bg)
            for hh in range(hpg):
                h = g * hpg + hh
                r0 = h * HEAD_DIM
                dh_sc[r0:r0 + HEAD_DIM, :] = (dhc[hh * HEAD_DIM:(hh + 1) * HEAD_DIM, :]
                                              + jnp.exp(csT[h:h + 1, CHUNK - 1:CHUNK]) * ds[r0:r0 + HEAD_DIM, :])
        t = ddte * dte
        per_head = jnp.sum(jnp.sum(ds * hmat, axis=1, keepdims=True) * ind, axis=0, keepdims=True)
        last_add = jnp.sum(t, axis=0, keepdims=True) + jnp.exp(cs_last) * per_head
        dcs = dcs - t + jnp.where(_iota((CHUNK, SSM_HEADS), 0) == CHUNK - 1, last_add, 0.0)
        dadt = _01mm((row <= col).astype(F32), dcs)
        dxt = dxt_sc[...]
        ddt = dadt * a_row + _mm01(dxt * xs_, ind)
        dxs_ref[...] = dxt * dt_full + skip_ref[...]
        ddtr = ddt * (1.0 - jnp.exp(-dt))
        ddtr_ref[...] = jnp.zeros_like(ddtr_ref)
        ddtr_ref[:, :SSM_HEADS] = ddtr.astype(ddtr_ref.dtype)
        _acc(dal_ref, i, jnp.sum(dadt * dt, axis=0, keepdims=True) * a_row)
        _acc(ddtb_ref, i, jnp.sum(ddtr, axis=0, keepdims=True))

    rev = lambda i: (nc - 1 - i, 0)
    return _call(body, name, (nc,),
                 [pl.BlockSpec((CHUNK, width), rev), pl.BlockSpec((CHUNK, bcw), rev), pl.BlockSpec((CHUNK, LANE), rev),
                  pl.BlockSpec((SSM_HEADS, CHUNK), lambda i: (0, nc - 1 - i)), _ps((1, LANE)), _ps((SSM_HEADS, 1)),
                  pl.BlockSpec((1, width, ns), lambda i: (nc - 1 - i, 0, 0)), pl.BlockSpec((CHUNK, width), rev),
                  pl.BlockSpec((CHUNK, width), rev)],
                 [pl.BlockSpec((CHUNK, width), rev), pl.BlockSpec((CHUNK, bcw), rev), pl.BlockSpec((CHUNK, LANE), rev),
                  _ps((1, SSM_HEADS)), _ps((1, SSM_HEADS))],
                 [S((n, width), F32), S((n, bcw), F32), S((n, LANE), MXU_DTYPE), S((1, SSM_HEADS), F32), S((1, SSM_HEADS), F32)],
                 scratch=[pltpu.VMEM((width, ns), F32), pltpu.VMEM((CHUNK, width), F32)],
                 sem=("arbitrary",))(xs, bc, dt, dtT, alog_row, alog_col, hs, dy, dxs_skip)


def ssm_gate_fwd(yssd, xs, proj, dfull, gamma, name):
    n, d = yssd.shape
    tile = _row_tile(n, 256)
    gw = d // SSM_GROUPS

    def body(y_ref, xs_ref, z_ref, df_ref, gm_ref, o_ref):
        z = z_ref[...]
        y2 = (y_ref[...] + df_ref[...] * xs_ref[...]) * (z * _sigmoid(z))
        for g in range(SSM_GROUPS):
            yg = y2[:, g * gw:(g + 1) * gw]
            r = lax.rsqrt(jnp.mean(yg * yg, axis=-1, keepdims=True) + EPS)
            o_ref[:, g * gw:(g + 1) * gw] = (yg * r * gm_ref[:, g * gw:(g + 1) * gw]).astype(o_ref.dtype)

    return _call(body, name, (n // tile,), [_rs(tile, d), _rs(tile, d), _rs(tile, d, 2), _ps((1, d)), _ps((1, d))],
                 _rs(tile, d), S((n, d), MXU_DTYPE), sem=("parallel",))(yssd, xs, proj, dfull, gamma)


def ssm_gate_bwd(dy3, yssd, xs, proj, dfull, gamma, dproj, name):
    n, d = yssd.shape
    tile = _row_tile(n, 256)
    gw = d // SSM_GROUPS

    def body(dy_ref, y_ref, xs_ref, z_ref, df_ref, gm_ref, dp_in_ref, dys_ref, dxs_ref, dz_ref, dgm_ref, dd_ref):
        i = pl.program_id(0)
        z = z_ref[...]
        s = _sigmoid(z)
        xs_ = xs_ref[...]
        y1 = y_ref[...] + df_ref[...] * xs_
        y2 = y1 * (z * s)
        dy_ = dy_ref[...].astype(F32)
        dgm = []
        dy2 = []
        for g in range(SSM_GROUPS):
            sl = slice(g * gw, (g + 1) * gw)
            dxg, dgg = _rms_bwd_math(y2[:, sl], gm_ref[:, sl], dy_[:, sl])
            dy2.append(dxg)
            dgm.append(dgg)
        dy2 = jnp.concatenate(dy2, axis=1)
        dy1 = dy2 * (z * s)
        dys_ref[...] = dy1
        dxs_ref[...] = dy1 * df_ref[...]
        dz_ref[...] = (dy2 * y1 * s * (1.0 + z * (1.0 - s))).astype(dz_ref.dtype)
        _acc(dgm_ref, i, jnp.concatenate(dgm, axis=1))
        colsum = jnp.broadcast_to(jnp.sum(dy1 * xs_, axis=0, keepdims=True), (8, d))
        _acc(dd_ref, i, _mm01(colsum, _head_indicator(d, SSM_HEADS))[0:1, :])

    return pl.pallas_call(
        body, name=name, grid=(n // tile,),
        in_specs=[_rs(tile, d), _rs(tile, d), _rs(tile, d), _rs(tile, d, 2), _ps((1, d)), _ps((1, d)), ANY],
        out_specs=[_rs(tile, d), _rs(tile, d), _rs(tile, d, 2), _ps((1, d)), _ps((1, SSM_HEADS))],
        out_shape=[S((n, d), F32), S((n, d), F32), S(dproj.shape, dproj.dtype), S((1, d), F32), S((1, SSM_HEADS), F32)],
        input_output_aliases={6: 2},
        compiler_params=pltpu.CompilerParams(dimension_semantics=("arbitrary",), vmem_limit_bytes=VMEM_LIMIT),
    )(dy3, yssd, xs, proj, dfull, gamma, dproj)


def _rope128(x, cos, sin_signed):
    half = HEAD_DIM // 2
    lane = _iota(x.shape, 1)
    partner = jnp.where((lane % HEAD_DIM) < half, pltpu.roll(x, LANE - half, 1), pltpu.roll(x, half, 1))
    return x * cos + partner * sin_signed


def rope_fwd(qkv, cos, sin, name):
    n, w = qkv.shape
    qw = ATT_HEADS * HEAD_DIM
    kw = ATT_KV_HEADS * HEAD_DIM
    tile = _row_tile(n, 256)

    def body(x_ref, c_ref, s_ref, q_ref, k_ref, v_ref):
        c, s = c_ref[...], s_ref[...]
        for j in range(qw // LANE):
            q_ref[:, j * LANE:(j + 1) * LANE] = _rope128(x_ref[:, j * LANE:(j + 1) * LANE], c, s).astype(q_ref.dtype)
        for j in range(kw // LANE):
            k_ref[:, j * LANE:(j + 1) * LANE] = _rope128(x_ref[:, qw + j * LANE:qw + (j + 1) * LANE], c, s).astype(k_ref.dtype)
        v_ref[...] = x_ref[:, qw + kw:].astype(v_ref.dtype)

    return _call(body, name, (n // tile,), [_rs(tile, w), _rs(tile, LANE), _rs(tile, LANE)],
                 [_rs(tile, qw), _rs(tile, kw), _rs(tile, kw)],
                 [S((n, qw), MXU_DTYPE), S((n, kw), MXU_DTYPE), S((n, kw), MXU_DTYPE)], sem=("parallel",))(qkv, cos, sin)


ATT_GROUP = ATT_HEADS // ATT_KV_HEADS


def _attn_mask(i):
    row = _iota((ATT_GROUP * WINDOW, 2 * WINDOW), 0) % WINDOW
    s = _iota((ATT_GROUP * WINDOW, 2 * WINDOW), 1)
    return (s > row) & (s <= row + WINDOW) & ((s >= WINDOW) | (i > 0))


def _stack_heads(ref, j, kh, lo):
    parts = []
    for t in range(ATT_GROUP):
        h = ATT_GROUP * j + t
        blk = ref[:, (h // 2) * LANE:(h // 2 + 1) * LANE]
        blk = jnp.where(lo if h % 2 == 0 else jnp.logical_not(lo), blk, jnp.zeros_like(blk))
        parts.append(blk if h % 2 == kh else pltpu.roll(blk, HEAD_DIM, 1))
    return jnp.concatenate(parts, axis=0)


def _unstack_heads(stacked, j, kh, lo, put):
    for t in range(0, ATT_GROUP, 2):
        h = ATT_GROUP * j + t
        even = stacked[t * WINDOW:(t + 1) * WINDOW, :]
        odd = stacked[(t + 1) * WINDOW:(t + 2) * WINDOW, :]
        even = even if kh == 0 else pltpu.roll(even, HEAD_DIM, 1)
        odd = odd if kh == 1 else pltpu.roll(odd, HEAD_DIM, 1)
        put(h // 2, jnp.where(lo, even, odd))


def _per_head_rows(ref, j):
    return jnp.concatenate([ref[:, ATT_GROUP * j + t:ATT_GROUP * j + t + 1] for t in range(ATT_GROUP)], axis=0)


def _per_head_scalar(ref, j):
    rows = _iota((ATT_GROUP * WINDOW, 1), 0) // WINDOW
    out = jnp.zeros((ATT_GROUP * WINDOW, 1), F32)
    for t in range(ATT_GROUP):
        out = out + jnp.where(rows == t, ref[:, ATT_GROUP * j + t:ATT_GROUP * j + t + 1], 0.0)
    return out


def attn_fwd(q, k, v, sinks, name):
    n, qw = q.shape
    kw = k.shape[1]
    nb = n // WINDOW
    scale = HEAD_DIM ** -0.5

    def body(q_ref, kc_ref, kp_ref, vc_ref, vp_ref, sk_ref, o_ref, lse_ref):
        i = pl.program_id(0)
        valid = _attn_mask(i)
        lo = _iota((WINDOW, LANE), 1) < HEAD_DIM
        k2 = jnp.concatenate([kp_ref[...], kc_ref[...]], axis=0)
        v2 = jnp.concatenate([vp_ref[...], vc_ref[...]], axis=0)
        lane1 = _iota((1, LANE), 1)
        lse = jnp.zeros((WINDOW, LANE), F32)

        def put_o(qb, val):
            o_ref[:, qb * LANE:(qb + 1) * LANE] = val.astype(o_ref.dtype)

        for j in range(ATT_KV_HEADS):
            kb, kh = j // 2, j % 2
            q4 = _stack_heads(q_ref, j, kh, lo)
            logits = jnp.where(valid, _mm(q4, k2[:, kb * LANE:(kb + 1) * LANE], NT) * scale, -1e30)
            sk = _per_head_scalar(sk_ref, j)
            m = jnp.maximum(jnp.max(logits, axis=-1, keepdims=True), sk)
            e = jnp.exp(logits - m)
            den = jnp.sum(e, axis=-1, keepdims=True) + jnp.exp(sk - m)
            lse4 = m + jnp.log(den)
            for t in range(ATT_GROUP):
                lse = lse + lse4[t * WINDOW:(t + 1) * WINDOW, :] * (lane1 == ATT_GROUP * j + t).astype(F32)
            _unstack_heads(_mm(e * (1.0 / den), v2[:, kb * LANE:(kb + 1) * LANE]), j, kh, lo, put_o)
        lse_ref[...] = lse

    return _call(body, name, (nb,),
                 [_rs(WINDOW, qw), _rs(WINDOW, kw), _rs(WINDOW, kw, 0, -1), _rs(WINDOW, kw), _rs(WINDOW, kw, 0, -1), _ps((1, LANE))],
                 [_rs(WINDOW, qw), _rs(WINDOW, LANE)], [S((n, qw), MXU_DTYPE), S((n, LANE), F32)],
                 sem=("parallel",))(q, k, k, v, v, sinks)


def attn_bwd(q, k, v, o, do, lse, sinks, name):
    n, qw = q.shape
    kw = k.shape[1]
    nb = n // WINDOW
    scale = HEAD_DIM ** -0.5

    def body(q_ref, kc_ref, kp_ref, vc_ref, vp_ref, o_ref, do_ref, lse_ref, sk_ref,
             dq_ref, dka_ref, dkb_ref, dva_ref, dvb_ref, dsk_ref):
        i = pl.program_id(0)
        valid = _attn_mask(i)
        lo = _iota((WINDOW, LANE), 1) < HEAD_DIM
        k2 = jnp.concatenate([kp_ref[...], kc_ref[...]], axis=0)
        v2 = jnp.concatenate([vp_ref[...], vc_ref[...]], axis=0)
        lane1 = _iota((1, LANE), 1)
        do_ = do_ref[...].astype(F32)
        delta = _mm01(do_ * o_ref[...].astype(F32), _head_indicator(qw, ATT_HEADS))
        dk2 = [jnp.zeros((2 * WINDOW, LANE), F32) for _ in range(kw // LANE)]
        dv2 = [jnp.zeros((2 * WINDOW, LANE), F32) for _ in range(kw // LANE)]
        dsk = jnp.zeros((1, LANE), F32)

        def put_dq(qb, val):
            dq_ref[:, qb * LANE:(qb + 1) * LANE] = val

        for j in range(ATT_KV_HEADS):
            kb, kh = j // 2, j % 2
            q4 = _stack_heads(q_ref, j, kh, lo)
            do4 = _stack_heads(do_ref, j, kh, lo)
            kk = k2[:, kb * LANE:(kb + 1) * LANE]
            vv = v2[:, kb * LANE:(kb + 1) * LANE]
            logits = jnp.where(valid, _mm(q4, kk, NT) * scale, -1e30)
            lse4 = _per_head_rows(lse_ref, j)
            p = jnp.exp(logits - lse4)
            dl = jnp.concatenate([delta[:, ATT_GROUP * j + t:ATT_GROUP * j + t + 1] for t in range(ATT_GROUP)], axis=0)
            ds = p * (_mm(do4, vv, NT) - dl) * scale
            sd = jnp.exp(_per_head_scalar(sk_ref, j) - lse4) * dl
            for t in range(ATT_GROUP):
                dsk = dsk - (jnp.sum(sd[t * WINDOW:(t + 1) * WINDOW, :], axis=0, keepdims=True)
                             * (lane1 == ATT_GROUP * j + t).astype(F32))
            _unstack_heads(_mm(ds, kk), j, kh, lo, put_dq)
            dk2[kb] = dk2[kb] + _mm(ds, q4, TN)
            dv2[kb] = dv2[kb] + _mm(p, do4, TN)
        for kb in range(kw // LANE):
            dkb_ref[:, kb * LANE:(kb + 1) * LANE] = dk2[kb][0:WINDOW, :]
            dka_ref[:, kb * LANE:(kb + 1) * LANE] = dk2[kb][WINDOW:, :]
            dvb_ref[:, kb * LANE:(kb + 1) * LANE] = dv2[kb][0:WINDOW, :]
            dva_ref[:, kb * LANE:(kb + 1) * LANE] = dv2[kb][WINDOW:, :]
        _acc(dsk_ref, i, dsk)

    return _call(body, name, (nb,),
                 [_rs(WINDOW, qw), _rs(WINDOW, kw), _rs(WINDOW, kw, 0, -1), _rs(WINDOW, kw), _rs(WINDOW, kw, 0, -1),
                  _rs(WINDOW, qw), _rs(WINDOW, qw), _rs(WINDOW, LANE), _ps((1, LANE))],
                 [_rs(WINDOW, qw)] + [_rs(WINDOW, kw)] * 4 + [_ps((1, LANE))],
                 [S((n, qw), F32)] + [S((n, kw), F32)] * 4 + [S((1, LANE), F32)],
                 sem=("arbitrary",))(q, k, k, v, v, o, do, lse, sinks)


def attn_grad_merge(dq, dka, dkb, dva, dvb, cos, sin, name):
    n, qw = dq.shape
    kw = dka.shape[1]
    nb = n // WINDOW
    w = qw + 2 * kw

    def body(dq_ref, dka_ref, dkb_ref, dva_ref, dvb_ref, c_ref, s_ref, o_ref, db_ref):
        i = pl.program_id(0)
        c, s = c_ref[...], -s_ref[...]
        nxt = (i < nb - 1).astype(F32)

        @pl.when(i == 0)
        def _():
            db_ref[...] = jnp.zeros_like(db_ref)

        def put(c0, val):
            o_ref[:, c0:c0 + val.shape[1]] = val.astype(o_ref.dtype)
            db_ref[:, c0:c0 + val.shape[1]] += jnp.sum(val, axis=0, keepdims=True)

        for j in range(qw // LANE):
            put(j * LANE, _rope128(dq_ref[:, j * LANE:(j + 1) * LANE], c, s))
        for j in range(kw // LANE):
            sl = slice(j * LANE, (j + 1) * LANE)
            put(qw + j * LANE, _rope128(dka_ref[:, sl] + dkb_ref[:, sl] * nxt, c, s))
        put(qw + kw, dva_ref[...] + dvb_ref[...] * nxt)

    return _call(body, name, (nb,),
                 [_rs(WINDOW, qw), _rs(WINDOW, kw), _rs(WINDOW, kw, 0, 1, nb), _rs(WINDOW, kw), _rs(WINDOW, kw, 0, 1, nb),
                  _rs(WINDOW, LANE), _rs(WINDOW, LANE)],
                 [_rs(WINDOW, w), _ps((1, w))], [S((n, w), MXU_DTYPE), S((1, w), F32)],
                 sem=("arbitrary",))(dq, dka, dkb, dva, dvb, cos, sin)


def _row(v):
    return v.reshape(1, -1)


def _pad_lanes(v, width=LANE):
    return jnp.pad(v.reshape(1, -1), ((0, 0), (0, width - v.size)))


class LayerWeights(dict):
    def __init__(self, small, fetch):
        super().__init__(small)
        self.fetch = fetch

    def need(self, k, after):
        if k not in self:
            self[k] = self.fetch(k, after)
        return self[k]


def ffn_fwd(h, g, w, keys, tag):
    xn, u, hm = swiglu_in(h, _row(g), w.need(keys[0], h), f"{tag}_in")
    return matmul(hm, w.need(keys[1], hm), "nn", f"{tag}_out", scale=0.5, res=h), (h, xn, u, hm)


class GradSink:
    ORDER = ("ffn1_w_out", "ffn2_w_out", "ple_gate_w", "att_w_o", "hyb_w_out", "ffn1_w_in", "ffn2_w_in", "att_w_qkv",
             "ple_proj_w", "hyb_w_in")

    def __init__(self, shard_shapes, bucket_of):
        self.where, rows = {}, {}
        for k in self.ORDER:
            n, r, c = shard_shapes[k]
            for li in range(n):
                layer = li if k in PER_LAYER else 2 * li + (0 if k in EVEN_ONLY else 1)
                rows_b = rows.setdefault(bucket_of(layer, _stage(k)), {})
                off = -(-rows_b.get(c, 0) // r) * r
                rows_b[c] = off + r
                self.where[k, li] = (bucket_of(layer, _stage(k)), c, "r" if SHARD_AXIS[k] == 1 else "c", off, r)
        self.bufs = {b: {c: lax.empty((N_CHIPS, r, c), F32) for c, r in rows_b.items()} for b, rows_b in rows.items()}

    def mm(self, k, li, a, b, name, scale=None, c0=0, paired=False):
        layer, c, kind, off, r = self.where[k, li]
        slot = Slot(self.bufs[layer][c], kind, r if kind == "r" else c, off, c0, paired)
        self.bufs[layer][c] = matmul(a, b, "tn", name, scale=scale, into=slot)

    def put(self, k, li, chip_major):
        layer, c = self.where[k, li][:2]
        self.bufs[layer][c] = chip_major

    def take(self, k, n, reduced):
        parts = []
        for li in range(n):
            layer, c, _, off, r = self.where[k, li]
            parts.append(reduced[layer][c][off:off + r])
        return jnp.stack(parts)


def ffn_bwd(dh, g, w_in, w_out, saved, tag, sink, keys, layer, after, colsum=False):
    h, xn, u, hm = saved
    sink.mm(keys[1], layer, hm, dh, f"{tag}_dwout", scale=0.5)
    du = swiglu_out_bwd(dh, w_out, u, after, f"{tag}_dhm")
    sink.mm(keys[0], layer, xn, du, f"{tag}_dwin", paired=True)
    dxn = matmul(du, ColSharded(w_in.arr, paired=True), "nt", f"{tag}_dxn")
    outs = rms_bwd(h, _row(g), dxn, dh, f"{tag}_drms", colsum=colsum)
    return (outs[0], outs[1].reshape(-1)) + ((outs[2],) if colsum else ())


def _hyb_params(w):
    d = w["conv_dw_b"].size
    inner = SSM_HEADS * HEAD_DIM
    main = 3 * d + w["ssm_conv_b"].size
    return dict(
        w_main=w["hyb_w_in"][:, :main], w_dt=jnp.pad(w["hyb_w_in"][:, main:], ((0, 0), (0, LANE - SSM_HEADS))),
        cw=jnp.pad(w["conv_dw_w"], ((0, 32 - CONV_WIDTH), (0, 0))), cb=_row(w["conv_dw_b"]),
        lg=_row(w["conv_ln_g"]), lb=_row(w["conv_ln_b"]),
        sw=jnp.pad(w["ssm_conv_w"], ((0, 8 - SSM_CONV), (0, 0))), sb=_row(w["ssm_conv_b"]),
        dtb=_pad_lanes(w["ssm_dt_bias"]), al_row=_pad_lanes(w["ssm_a_log"]), al_col=w["ssm_a_log"].reshape(-1, 1),
        dfull=_row(jnp.repeat(w["ssm_d"], HEAD_DIM)), gamma=_row(w["ssm_norm"]), d=d, inner=inner, main=main)


def hyb_fwd(h, w, tag):
    w.need("hyb_w_in", h)
    q = _hyb_params(w)
    xn = rms_fwd(h, _row(w["norm_mix"]), f"{tag}_rms")
    proj = matmul(xn, q["w_main"], "nn", f"{tag}_in")
    dtr = matmul(xn, q["w_dt"], "nn", f"{tag}_in_dt")
    u, u1 = conv_group_fwd(proj, q["cw"], q["cb"], q["lg"], q["lb"], f"{tag}_conv")
    pre, xs, bc, dt = ssm_conv_fwd(proj, dtr, q["sw"], q["sb"], q["dtb"], f"{tag}_sconv")
    dtT = dt[:, :SSM_HEADS].T
    yssd, hs = ssd_fwd(xs, bc, dt, dtT, q["al_row"], q["al_col"], f"{tag}_ssd")
    y = ssm_gate_fwd(yssd, xs, proj, q["dfull"], q["gamma"], f"{tag}_gate")
    wo = w.need("hyb_w_out", u)
    h2 = matmul(u, wo[:q["d"]], "nn", f"{tag}_out_a", res=h)
    h2 = matmul(y, wo[q["d"]:], "nn", f"{tag}_out_b", res=h2)
    return h2, (h, xn, proj, u, u1, pre, xs, bc, dt, dtT, yssd, hs, y)


def hyb_bwd(dh, w, saved, tag, sink, layer):
    q = _hyb_params(w)
    h, xn, proj, u, u1, pre, xs, bc, dt, dtT, yssd, hs, y = saved
    du = matmul(dh, w["hyb_w_out"][:q["d"]], "nt", f"{tag}_du")
    dy3 = matmul(dh, w["hyb_w_out"][q["d"]:], "nt", f"{tag}_dy")
    sink.mm("hyb_w_out", layer, u, dh, f"{tag}_dwo_a", c0=0)
    sink.mm("hyb_w_out", layer, y, dh, f"{tag}_dwo_b", c0=N_CHIPS // 2)
    dproj, dcw, dcb, dlg, dlb = conv_group_bwd(du, u1, proj, q["cw"], q["lg"], q["lb"], f"{tag}_dconv")
    dyssd, dxs_skip, dproj, dgamma, dd = ssm_gate_bwd(dy3, yssd, xs, proj, q["dfull"], q["gamma"], dproj, f"{tag}_dgate")
    dxs, dbc, ddtr, dalog, ddtb = ssd_bwd(xs, bc, dt, dtT, q["al_row"], q["al_col"], hs, dyssd, dxs_skip, f"{tag}_dssd")
    dproj, dsw, dsb = ssm_conv_bwd(dxs, dbc, pre, proj, q["sw"], dproj, f"{tag}_dsconv")
    dw_in = jnp.concatenate([matmul(xn, dproj, "tn", f"{tag}_dwin"),
                             matmul(xn, ddtr, "tn", f"{tag}_dwin_dt")[:, :SSM_HEADS]], axis=1)
    sink.put("hyb_w_in", layer, _to_chip_major(dw_in, 1))
    dxn = matmul(dproj, q["w_main"], "nt", f"{tag}_dxn")
    dxn = matmul(ddtr, q["w_dt"], "nt", f"{tag}_dxn_dt", res=dxn)
    dh2, dg = rms_bwd(h, _row(w["norm_mix"]), dxn, dh, f"{tag}_drms")
    grads = dict(norm_mix=dg.reshape(-1), conv_dw_w=dcw[:CONV_WIDTH], conv_dw_b=dcb.reshape(-1),
                 conv_ln_g=dlg.reshape(-1), conv_ln_b=dlb.reshape(-1), ssm_conv_w=dsw[:SSM_CONV], ssm_conv_b=dsb.reshape(-1),
                 ssm_dt_bias=ddtb.reshape(-1), ssm_a_log=dalog.reshape(-1), ssm_d=dd.reshape(-1), ssm_norm=dgamma.reshape(-1))
    return dh2, grads


def rope_tables(n):
    half = HEAD_DIM // 2
    inv = ROPE_THETA ** (-jnp.arange(0, HEAD_DIM, 2, dtype=F32) / HEAD_DIM)
    ang = jnp.arange(n, dtype=F32)[:, None] * inv[None, :]
    cos, sin = jnp.cos(ang), jnp.sin(ang)
    reps = LANE // HEAD_DIM
    return jnp.tile(jnp.concatenate([cos, cos], axis=1), (1, reps)), jnp.tile(jnp.concatenate([-sin, sin], axis=1), (1, reps))


def att_fwd(h, w, tables, tag):
    cos, sin = tables
    xn = rms_fwd(h, _row(w["norm_mix"]), f"{tag}_rms")
    qkv = matmul(xn, w.need("att_w_qkv", h), "nn", f"{tag}_qkv", bias=_row(w["att_b_qkv"]))
    q, k, v = rope_fwd(qkv, cos, sin, f"{tag}_rope")
    sinks = _pad_lanes(w["att_sinks"])
    o, lse = attn_fwd(q, k, v, sinks, f"{tag}_attn")
    h2 = matmul(o, w.need("att_w_o", o), "nn", f"{tag}_o", bias=_row(w["att_b_o"]), res=h)
    return h2, (h, xn, q, k, v, o, lse, sinks)


def att_bwd(dh, dh_colsum, w, saved, tables, tag, sink, layer):
    cos, sin = tables
    h, xn, q, k, v, o, lse, sinks = saved
    do = matmul(dh, w["att_w_o"], "nt", f"{tag}_do")
    sink.mm("att_w_o", layer, o, dh, f"{tag}_dwo")
    dq, dka, dkb, dva, dvb, dsk = attn_bwd(q, k, v, o, do, lse, sinks, f"{tag}_dattn")
    dqkv, dbqkv = attn_grad_merge(dq, dka, dkb, dva, dvb, cos, sin, f"{tag}_drope")
    sink.mm("att_w_qkv", layer, xn, dqkv, f"{tag}_dwqkv")
    dxn = matmul(dqkv, w["att_w_qkv"], "nt", f"{tag}_dxn")
    dh2, dg = rms_bwd(h, _row(w["norm_mix"]), dxn, dh, f"{tag}_drms")
    grads = dict(norm_mix=dg.reshape(-1), att_b_qkv=dbqkv.reshape(-1), att_sinks=dsk[0, :ATT_HEADS],
                 att_b_o=dh_colsum.reshape(-1))
    return dh2, grads


def ple_block_fwd(h, pe, w, tag):
    xn = rms_fwd(h, _row(w["ple_norm"]), f"{tag}_rms")
    gl = matmul(xn, w.need("ple_gate_w", h), "nn", f"{tag}_gate")
    pp = matmul(pe, w.need("ple_proj_w", h), "nn", f"{tag}_proj")
    return ple_fwd(h, gl, pp, f"{tag}_mix"), (h, xn, gl, pp, pe)


def ple_block_bwd(dh, w, saved, tag, sink, layer, after):
    h, xn, gl, pp, pe = saved
    dpp, dgl = ple_bwd(dh, gl, pp, after, f"{tag}_dmix")
    sink.mm("ple_proj_w", layer, pe, dpp, f"{tag}_dwp")
    sink.mm("ple_gate_w", layer, xn, dgl, f"{tag}_dwg")
    dxn = matmul(dgl, w["ple_gate_w"], "nt", f"{tag}_dxn")
    dh2, dg = rms_bwd(h, _row(w["ple_norm"]), dxn, dh, f"{tag}_drms")
    return dh2, dict(ple_norm=dg.reshape(-1))


PER_LAYER = ("norm_ffn1", "ffn1_w_in", "ffn1_w_out", "norm_mix", "norm_ffn2", "ffn2_w_in", "ffn2_w_out",
             "ple_norm", "ple_gate_w", "ple_proj_w")
EVEN_ONLY = ("hyb_w_in", "conv_dw_w", "conv_dw_b", "conv_ln_g", "conv_ln_b", "ssm_conv_w", "ssm_conv_b",
             "ssm_dt_bias", "ssm_a_log", "ssm_d", "ssm_norm", "hyb_w_out")
ODD_ONLY = ("att_w_qkv", "att_b_qkv", "att_sinks", "att_w_o", "att_b_o")


def _layer_index(k, i):
    if k in PER_LAYER:
        return i
    if k in (EVEN_ONLY if i % 2 == 0 else ODD_ONLY):
        return i // 2
    return None


def _stage(k):
    return 0 if k.startswith("ffn1") else (2 if k.startswith(("ffn2", "ple")) else 1)


def trunk_fwd_bwd(x, pe, target, layers, final_norm, sink, stage_done):
    depth = len(layers)
    tables = rope_tables(x.shape[0])
    h = x
    saved = []
    for i, w in enumerate(layers):
        h, s1 = ffn_fwd(h, w["norm_ffn1"], w, ("ffn1_w_in", "ffn1_w_out"), f"l{i}_ffn1")
        if i % 2 == 0:
            h, s2 = hyb_fwd(h, w, f"l{i}_hyb")
        else:
            h, s2 = att_fwd(h, w, tables, f"l{i}_att")
        h, s3 = ffn_fwd(h, w["norm_ffn2"], w, ("ffn2_w_in", "ffn2_w_out"), f"l{i}_ffn2")
        h, s4 = ple_block_fwd(h, pe[i], w, f"l{i}_ple")
        saved.append((s1, s2, s3, s4))
    dh, dgf, loss = loss_head(h, _row(final_norm), target, "loss_head")
    grads = {}
    tie = dgf
    for i in reversed(range(depth)):
        w = layers[i]
        s1, s2, s3, s4 = saved[i]
        dh, g = ple_block_bwd(dh, w, s4, f"l{i}_ple", sink, i, tie)
        odd = i % 2 == 1
        out = ffn_bwd(dh, w["norm_ffn2"], w["ffn2_w_in"], w["ffn2_w_out"], s3, f"l{i}_ffn2", sink,
                      ("ffn2_w_in", "ffn2_w_out"), i, tie, colsum=odd)
        dh = out[0]
        g.update(norm_ffn2=out[1])
        if odd:
            dh, gm = att_bwd(dh, out[2], w, s2, tables, f"l{i}_att", sink, i // 2)
        else:
            dh, gm = hyb_bwd(dh, w, s2, f"l{i}_hyb", sink, i // 2)
        g.update(gm)
        tie = stage_done(i, 1, tie)
        out = ffn_bwd(dh, w["norm_ffn1"], w["ffn1_w_in"], w["ffn1_w_out"], s1, f"l{i}_ffn1", sink,
                      ("ffn1_w_in", "ffn1_w_out"), i, tie)
        dh = out[0]
        g.update(norm_ffn1=out[1])
        tie = stage_done(i, 0, tie)
        for k, v in g.items():
            grads.setdefault(k, []).insert(0, v)
    grads = {k: jnp.stack(v) for k, v in grads.items()}
    grads["final_norm"] = dgf.reshape(-1)
    return loss, dh, grads


def _me():
    return lax.axis_index("x"), lax.axis_index("y"), lax.axis_index("c")


def _flip(v, f):
    return 1 - v if f else v


def _remote(src, dst, send_sems, recv_sems, k, dev):
    return pltpu.make_async_remote_copy(src_ref=src, dst_ref=dst, send_sem=send_sems.at[k], recv_sem=recv_sems.at[k],
                                        device_id=dev, device_id_type=MESH)


CHIP_FLIPS = ((1, 0), (0, 1), (1, 1))
DEV_FLIPS = tuple((fx, fy, fc) for fx in (0, 1) for fy in (0, 1) for fc in (0, 1))[1:]


def all_gather_chips(xs, name):
    na = len(xs)
    halves = [x.shape[0] // 2 for x in xs]
    assert all(x.shape[0] % 2 == 0 for x in xs)

    def body(*refs):
        x_refs, out_refs = refs[:na], refs[na:2 * na]
        send_sems, recv_sems = refs[2 * na:]
        mx, my, mc = _me()
        chip = 2 * mx + my
        sib = (mx, my, 1 - mc)
        peers = [(_flip(mx, fx), _flip(my, fy)) for fx, fy in CHIP_FLIPS]

        def rows(a, ch, hc):
            return out_refs[a].at[ch, pl.ds(hc * halves[a], halves[a]), :]

        def src(a):
            return x_refs[a].at[pl.ds(mc * halves[a], halves[a]), :]

        first = [_remote(src(a), rows(a, chip, mc), send_sems, recv_sems, 6 * a + j, (px, py, mc))
                 for j, (px, py) in enumerate(peers) for a in range(na)]
        for cp in first:
            cp.start()
        passed = []
        for j, (px, py) in enumerate(peers):
            for a in range(na):
                landed = rows(a, 2 * px + py, mc)
                _remote(src(a), landed, send_sems, recv_sems, 6 * a + j, (px, py, mc)).wait_recv()
                fw = _remote(landed, landed, send_sems, recv_sems, 6 * a + 3 + j, sib)
                fw.start()
                passed.append(fw)
        for j, (px, py) in enumerate(peers):
            for a in range(na):
                _remote(src(a), rows(a, 2 * px + py, 1 - mc), send_sems, recv_sems, 6 * a + 3 + j, sib).wait_recv()
        for cp in first + passed:
            cp.wait_send()

    outs = pl.pallas_call(
        body, name=name, out_shape=[S((N_CHIPS,) + x.shape, x.dtype) for x in xs], in_specs=[ANY] * na, out_specs=[ANY] * na,
        scratch_shapes=[pltpu.SemaphoreType.DMA((6 * na,)), pltpu.SemaphoreType.DMA((6 * na,))])(*xs)
    chip = 2 * lax.axis_index("x") + lax.axis_index("y")
    return [lax.dynamic_update_slice_in_dim(o, x[None], chip, axis=0) for o, x in zip(outs, xs)]


HBM = pl.BlockSpec(memory_space=pltpu.HBM)
SEM = pl.BlockSpec(memory_space=pltpu.SEMAPHORE)
DATAFLOW = pltpu.SideEffectType.DATAFLOW_SIDE_EFFECTING


def gather_start(xs, lands, after, name):
    na = len(xs)

    def body(*refs):
        x_refs, land_refs = refs[:na], refs[na:2 * na]
        send_sems, recv_sems = refs[2 * na + 1], refs[2 * na + 2]
        token = refs[-1]
        mx, my, mc = _me()
        chip = 2 * mx + my
        for a in range(na):
            for j, (fx, fy) in enumerate(CHIP_FLIPS):
                _remote(x_refs[a], land_refs[a].at[chip], send_sems, recv_sems, 3 * a + j,
                        (_flip(mx, fx), _flip(my, fy), mc)).start()
        token[...] = jnp.zeros_like(token)

    outs = pl.pallas_call(
        body, name=name,
        out_shape=(pltpu.SemaphoreType.DMA((3 * na,)), pltpu.SemaphoreType.DMA((3 * na,)))
        + tuple(pltpu.HBM(x.shape, x.dtype) for x in xs) + tuple(pltpu.HBM(l.shape, l.dtype) for l in lands)
        + (S((8, LANE), F32),),
        in_specs=[HBM] * (2 * na) + [pl.BlockSpec(memory_space=pl.ANY)],
        out_specs=(SEM, SEM) + (HBM,) * (2 * na) + (pl.BlockSpec(memory_space=pltpu.VMEM),),
        input_output_aliases={a: 2 + a for a in range(2 * na)},
        compiler_params=pltpu.CompilerParams(has_side_effects=DATAFLOW),
    )(*[pltpu.with_memory_space_constraint(t, pltpu.HBM) for t in list(xs) + list(lands)], after)
    return outs[0], outs[1], list(outs[2:2 + na]), list(outs[2 + na:2 + 2 * na])


def gather_wait(send_sems, recv_sems, xs, lands, first, after, name):
    na = len(xs)

    def body(*refs):
        x_refs, land_refs = refs[:na], refs[na:2 * na]
        send_sems, recv_sems = refs[2 * na], refs[2 * na + 1]
        mx, my, mc = _me()
        for a in range(na):
            for j, (fx, fy) in enumerate(CHIP_FLIPS):
                px, py = _flip(mx, fx), _flip(my, fy)
                cp = _remote(x_refs[a], land_refs[a].at[2 * px + py], send_sems, recv_sems, 3 * (first + a) + j, (px, py, mc))
                cp.wait_send()
                cp.wait_recv()

    outs = pl.pallas_call(
        body, name=name,
        out_shape=tuple(pltpu.HBM(x.shape, x.dtype) for x in xs) + tuple(pltpu.HBM(l.shape, l.dtype) for l in lands),
        in_specs=[HBM] * (2 * na) + [SEM, SEM, pl.BlockSpec(memory_space=pl.ANY)], out_specs=(HBM,) * (2 * na),
        input_output_aliases={a: a for a in range(2 * na)},
        compiler_params=pltpu.CompilerParams(has_side_effects=DATAFLOW),
    )(*xs, *lands, send_sems, recv_sems, after)
    return list(outs[na:])


def all_gather_devices(v, name):
    r, l = v.shape

    def body(v_ref, out_ref, send_sems, recv_sems):
        mx, my, mc = _me()
        me = 4 * mx + 2 * my + mc
        peers = [(_flip(mx, fx), _flip(my, fy), _flip(mc, fc)) for fx, fy, fc in DEV_FLIPS]
        sends = [_remote(v_ref, out_ref.at[me], send_sems, recv_sems, j, p) for j, p in enumerate(peers)]
        for cp in sends:
            cp.start()
        for j, (px, py, pc) in enumerate(peers):
            _remote(v_ref, out_ref.at[4 * px + 2 * py + pc], send_sems, recv_sems, j, (px, py, pc)).wait_recv()
        for cp in sends:
            cp.wait_send()

    out = pl.pallas_call(
        body, name=name, out_shape=S((N_DEV, r, l), v.dtype), in_specs=[ANY], out_specs=ANY,
        scratch_shapes=[pltpu.SemaphoreType.DMA((7,)), pltpu.SemaphoreType.DMA((7,))])(v)
    me = 4 * lax.axis_index("x") + 2 * lax.axis_index("y") + lax.axis_index("c")
    return lax.dynamic_update_slice_in_dim(out, v[None], me, axis=0)


def sum_devices(g8, name):
    nd, r, l = g8.shape
    tile = r
    for t in (512, 256, 128, 64, 32, 16, 8):
        if r % t == 0:
            tile = t
            break

    def body(g_ref, o_ref):
        acc = g_ref[0]
        for d in range(1, nd):
            acc = acc + g_ref[d]
        o_ref[...] = acc

    return _call(body, name, (r // tile,), [pl.BlockSpec((nd, tile, l), lambda i: (0, i, 0))], _rs(tile, l), S((r, l), F32),
                 sem=("parallel",))(g8)


def exchange_halves(gs, name):
    na = len(gs)
    nch = gs[0].shape[0]

    def body(*refs):
        g_refs, out_refs = refs[:na], refs[na:2 * na]
        send_sems, recv_sems = refs[2 * na:]
        mx, my, mc = _me()
        sib = (mx, my, 1 - mc)
        cps = []
        for a in range(na):
            half = gs[a].shape[1] // 2
            for j in range(nch):
                cps.append(_remote(g_refs[a].at[j, pl.ds((1 - mc) * half, half), :], out_refs[a].at[j],
                                   send_sems, recv_sems, nch * a + j, sib))
        for cp in cps:
            cp.start()
        for cp in cps:
            cp.wait_recv()
        for cp in cps:
            cp.wait_send()

    return pl.pallas_call(
        body, name=name, out_shape=[S((nch, g.shape[1] // 2, g.shape[2]), g.dtype) for g in gs],
        in_specs=[ANY] * na, out_specs=[ANY] * na,
        scratch_shapes=[pltpu.SemaphoreType.DMA((nch * na,)), pltpu.SemaphoreType.DMA((nch * na,))])(*gs)


def add_halves(g4, got, name):
    nch, r, l = g4.shape
    half = r // 2
    tile = _pick_rows(half)
    nt = half // tile

    def body(g_ref, r_ref, a_ref, own_ref):
        j = pl.program_id(1)
        chip = 2 * lax.axis_index("x") + lax.axis_index("y")
        val = g_ref[0] + r_ref[0]
        a_ref[0] = val.astype(a_ref.dtype)

        @pl.when(j == chip)
        def _():
            own_ref[...] = val

    return pl.pallas_call(
        body, name=name, grid=(nt, nch),
        in_specs=[pl.BlockSpec((1, tile, l), lambda i, j: (j, lax.axis_index("c") * nt + i, 0)),
                  pl.BlockSpec((1, tile, l), lambda i, j: (j, i, 0))],
        out_specs=[pl.BlockSpec((1, tile, l), lambda i, j: (j, i, 0)), pl.BlockSpec((tile, l), lambda i, j: (i, 0))],
        out_shape=[S((nch, half, l), MXU_DTYPE), S((half, l), F32)],
        compiler_params=pltpu.CompilerParams(dimension_semantics=("parallel", "arbitrary"), vmem_limit_bytes=VMEM_LIMIT))(g4, got)


def _pick_rows(r, cap=512):
    for t in (512, 256, 128, 64, 32, 16):
        if t <= cap and r % t == 0:
            return t
    return r


def exchange_chips(parts, name):
    na = len(parts)

    def body(*refs):
        a_refs, out_refs = refs[:na], refs[na:2 * na]
        send_sems, recv_sems = refs[2 * na:]
        mx, my, mc = _me()
        peers = [(_flip(mx, fx), _flip(my, fy)) for fx, fy in CHIP_FLIPS]
        cps = [_remote(a_refs[a].at[2 * px + py], out_refs[a].at[j], send_sems, recv_sems, 3 * a + j, (px, py, mc))
               for j, (px, py) in enumerate(peers) for a in range(na)]
        for cp in cps:
            cp.start()
        for cp in cps:
            cp.wait_recv()
        for cp in cps:
            cp.wait_send()

    return pl.pallas_call(
        body, name=name, out_shape=[S((3,) + p.shape[1:], p.dtype) for p in parts], in_specs=[ANY] * na, out_specs=[ANY] * na,
        scratch_shapes=[pltpu.SemaphoreType.DMA((3 * na,)), pltpu.SemaphoreType.DMA((3 * na,))])(*parts)


def add_chips(own, got, name):
    h, l = own.shape
    tile = _pick_rows(h)

    def body(o_ref, g_ref, out_ref):
        out_ref[...] = ((o_ref[...] + g_ref[0].astype(F32)) + g_ref[1].astype(F32)) + g_ref[2].astype(F32)

    nt = h // tile
    return _call(body, name, (nt,), [_rs(tile, l), pl.BlockSpec((3, tile, l), lambda i: (0, i, 0))],
                 pl.BlockSpec((tile, l), lambda i: (lax.axis_index("c") * nt + i, 0)),
                 S((2 * h, l), F32), sem=("parallel",))(own, got)


def join_halves(bufs, name):
    na = len(bufs)

    def body(*refs):
        out_refs = refs[na:2 * na]
        send_sems, recv_sems = refs[2 * na:]
        mx, my, mc = _me()
        sib = (mx, my, 1 - mc)

        def half(a, hc):
            h = bufs[a].shape[0] // 2
            return out_refs[a].at[pl.ds(hc * h, h), :]

        cps = [_remote(half(a, mc), half(a, mc), send_sems, recv_sems, a, sib) for a in range(na)]
        for cp in cps:
            cp.start()
        for a in range(na):
            _remote(half(a, mc), half(a, 1 - mc), send_sems, recv_sems, a, sib).wait_recv()
        for cp in cps:
            cp.wait_send()

    return pl.pallas_call(
        body, name=name, out_shape=[S(b.shape, b.dtype) for b in bufs], in_specs=[ANY] * na, out_specs=[ANY] * na,
        input_output_aliases={a: a for a in range(na)},
        scratch_shapes=[pltpu.SemaphoreType.DMA((na,)), pltpu.SemaphoreType.DMA((na,))])(*bufs)


def exchange_chips_start(parts, name):
    na = len(parts)
    lands = [lax.empty((3,) + p.shape[1:], p.dtype) for p in parts]

    def body(*refs):
        a_refs, land_refs = refs[:na], refs[na:2 * na]
        send_sems, recv_sems = refs[2 * na], refs[2 * na + 1]
        mx, my, mc = _me()
        for j, (fx, fy) in enumerate(CHIP_FLIPS):
            px, py = _flip(mx, fx), _flip(my, fy)
            for a in range(na):
                _remote(a_refs[a].at[2 * px + py], land_refs[a].at[j], send_sems, recv_sems, 3 * a + j, (px, py, mc)).start()
        refs[-1][...] = jnp.zeros_like(refs[-1])

    outs = pl.pallas_call(
        body, name=name,
        out_shape=(pltpu.SemaphoreType.DMA((3 * na,)), pltpu.SemaphoreType.DMA((3 * na,)))
        + tuple(pltpu.HBM(t.shape, t.dtype) for t in list(parts) + lands) + (S((8, LANE), F32),),
        in_specs=[HBM] * (2 * na), out_specs=(SEM, SEM) + (HBM,) * (2 * na) + (pl.BlockSpec(memory_space=pltpu.VMEM),),
        input_output_aliases={a: 2 + a for a in range(2 * na)},
        compiler_params=pltpu.CompilerParams(has_side_effects=DATAFLOW),
    )(*[pltpu.with_memory_space_constraint(t, pltpu.HBM) for t in list(parts) + lands])
    return outs[0], outs[1], list(outs[2:2 + na]), list(outs[2 + na:2 + 2 * na]), outs[-1]


def exchange_chips_wait(send_sems, recv_sems, parts, lands, after, name):
    na = len(parts)

    def body(*refs):
        a_refs, land_refs = refs[:na], refs[na:2 * na]
        send_sems, recv_sems = refs[2 * na], refs[2 * na + 1]
        mx, my, mc = _me()
        for j, (fx, fy) in enumerate(CHIP_FLIPS):
            px, py = _flip(mx, fx), _flip(my, fy)
            for a in range(na):
                cp = _remote(a_refs[a].at[2 * px + py], land_refs[a].at[j], send_sems, recv_sems, 3 * a + j, (px, py, mc))
                cp.wait_send()
                cp.wait_recv()

    outs = pl.pallas_call(
        body, name=name, out_shape=tuple(pltpu.HBM(t.shape, t.dtype) for t in list(parts) + list(lands)),
        in_specs=[HBM] * (2 * na) + [SEM, SEM, pl.BlockSpec(memory_space=pl.ANY)], out_specs=(HBM,) * (2 * na),
        input_output_aliases={a: a for a in range(2 * na)},
        compiler_params=pltpu.CompilerParams(has_side_effects=DATAFLOW),
    )(*parts, *lands, send_sems, recv_sems, after)
    return list(outs[na:])


def reduce_begin(gs, tag):
    got = exchange_halves(gs, f"{tag}_d2d")
    sums = [add_halves(g, r, f"{tag}_add1_{i}") for i, (g, r) in enumerate(zip(gs, got))]
    return [own for _, own in sums], exchange_chips_start([a for a, _ in sums], f"{tag}_ici_start")


def reduce_end(state, after, tag):
    owns, (send_sems, recv_sems, parts, lands, _) = state
    got = exchange_chips_wait(send_sems, recv_sems, parts, lands, after, f"{tag}_ici_wait")
    return [add_chips(own, r, f"{tag}_add2_{i}") for i, (own, r) in enumerate(zip(owns, got))]


PACK_L = 1024
BIG_ROW_MULT = 512


def _pack(arrs, dtype, row_mult, lead=None):
    lead_shape = () if lead is None else arrs[0].shape[:lead]
    flat = jnp.concatenate([a.astype(dtype).reshape(lead_shape + (-1,)) for a in arrs], axis=-1)
    n = flat.shape[-1]
    unit = row_mult * PACK_L
    total = -(-n // unit) * unit
    flat = jnp.pad(flat, [(0, 0)] * len(lead_shape) + [(0, total - n)])
    return flat.reshape(lead_shape + (total // PACK_L, PACK_L))


def _unpack(packed, shapes, lead=None):
    lead_shape = () if lead is None else packed.shape[:lead]
    flat = packed.reshape(lead_shape + (-1,))
    out, off = [], 0
    for shp in shapes:
        n = int(np.prod(shp))
        out.append(flat[..., off:off + n].reshape(lead_shape + tuple(shp)))
        off += n
    return out


def _to_full(gathered, axis):
    t = jnp.moveaxis(gathered, 0, axis)
    shp = t.shape
    return t.reshape(shp[:axis] + (shp[axis] * shp[axis + 1],) + shp[axis + 2:])


def _to_chip_major(full, axis):
    shp = full.shape
    t = full.reshape(shp[:axis] + (N_CHIPS, shp[axis] // N_CHIPS) + shp[axis + 1:])
    return jnp.moveaxis(t, axis, 0)


WEIGHTS = ("norm_ffn1", "ffn1_w_in", "ffn1_w_out", "norm_mix", "norm_ffn2", "ffn2_w_in", "ffn2_w_out", "ple_norm",
           "ple_gate_w", "ple_proj_w", "hyb_w_in", "conv_dw_w", "conv_dw_b", "conv_ln_g", "conv_ln_b", "ssm_conv_w",
           "ssm_conv_b", "ssm_dt_bias", "ssm_a_log", "ssm_d", "ssm_norm", "hyb_w_out", "att_w_qkv", "att_b_qkv",
           "att_sinks", "att_w_o", "att_b_o", "final_norm")
SHARD_AXIS = dict(ffn1_w_in=2, ffn1_w_out=1, ffn2_w_in=2, ffn2_w_out=1, ple_gate_w=1, ple_proj_w=2, hyb_w_in=2,
                  conv_dw_w=2, ssm_conv_w=2, hyb_w_out=1, att_w_qkv=2, att_b_qkv=1, att_w_o=1, att_b_o=1)
BIG = ("ffn1_w_in", "ffn1_w_out", "ffn2_w_in", "ffn2_w_out", "ple_gate_w", "ple_proj_w", "hyb_w_in", "hyb_w_out",
       "att_w_qkv", "att_w_o")
ODD_WIDTH = "hyb_w_in"
BIG_FLAT = tuple(k for k in BIG if k != ODD_WIDTH)
SMALL_SHARDED = ("conv_dw_w", "ssm_conv_w", "att_b_qkv", "att_b_o")
SMALL = tuple(k for k in WEIGHTS if k not in BIG)


def _step(x, p, target, w, m, v):
    mx, my = lax.axis_index("x"), lax.axis_index("y")
    chip = 2 * mx + my

    depth = w["norm_ffn1"].shape[0]
    order = sorted([(k, i) for i in range(depth) for k in BIG if _layer_index(k, i) is not None],
                   key=lambda t: (t[1], _stage(t[0])))
    small_g = all_gather_devices(_pack([w[k] for k in SMALL_SHARDED], F32, 8), "gather_small")
    shards = [w[k][_layer_index(k, i)].astype(MXU_DTYPE) for k, i in order]
    lands = [lax.dynamic_update_slice_in_dim(lax.empty((N_CHIPS,) + s.shape, s.dtype), s[None], chip, axis=0) for s in shards]
    send_sems, recv_sems, shards, lands = gather_start(shards, lands, small_g, "gather_start")

    def fetch(i, k, after):
        p = order.index((k, i))
        g, = gather_wait(send_sems, recv_sems, [shards[p]], [lands[p]], p, after, f"gather_wait_l{i}_{k}")
        if k == ODD_WIDTH:
            return _to_full(g, 1)
        if SHARD_AXIS[k] == 2:
            return ColSharded(g)
        return g.reshape(-1, g.shape[-1])

    small_g = small_g[0::2]
    small_full = {k: _to_full(g, SHARD_AXIS[k])
                  for k, g in zip(SMALL_SHARDED, _unpack(small_g, [w[k].shape for k in SMALL_SHARDED], lead=1))}
    layers = [LayerWeights({k: small_full.get(k, w[k])[_layer_index(k, i)] for k in SMALL if _layer_index(k, i) is not None},
                           functools.partial(fetch, i)) for i in range(depth)]

    def bucket_of(layer, stage):
        return (layer, 0) if layer > 0 else (0, min(stage, 1))

    sink = GradSink({k: w[k].shape for k in BIG}, bucket_of)
    begun = {}

    def stage_done(i, stage, tie):
        b = bucket_of(i, stage)
        if stage > 0 and bucket_of(i, stage - 1) == b:
            return tie
        begun[b] = reduce_begin(list(sink.bufs[b].values()), f"grads_l{b[0]}_{b[1]}")
        return begun[b][1][-1]

    loss, dx, grads = trunk_fwd_bwd(x[0], p[:, 0], target[0], layers, w["final_norm"], sink, stage_done)

    halves = {b: reduce_end(begun[b], dx, f"grads_l{b[0]}_{b[1]}") for b in begun}
    joined = iter(join_halves([h for b in halves for h in halves[b]], "grads_join"))
    reduced = {b: {c: next(joined) for c in sink.bufs[b]} for b in halves}
    g_out = {k: sink.take(k, w[k].shape[0], reduced) for k in BIG}
    vec = _pack([loss[0:1, 0:1]] + [grads[k] for k in SMALL], F32, 8)
    vec = sum_devices(all_gather_devices(vec, "gather_vectors"), "sum_vectors")
    parts = _unpack(vec, [(1, 1)] + [grads[k].shape for k in SMALL])
    loss_out = parts[0].reshape(())
    for k, g in zip(SMALL, parts[1:]):
        if k in SHARD_AXIS:
            ax = SHARD_AXIS[k]
            g = lax.dynamic_slice_in_dim(g, chip * w[k].shape[ax], w[k].shape[ax], axis=ax)
        g_out[k] = g

    delta, new_m, new_v = {}, {}, {}
    for k in BIG:
        shp = w[k].shape
        two_d = lambda a: a.reshape(-1, shp[-1])
        d_, m_, v_ = adamw(two_d(w[k]), two_d(g_out[k]), two_d(m[k]), two_d(v[k]), f"adamw_{k}")
        delta[k], new_m[k], new_v[k] = d_.reshape(shp), m_.reshape(shp), v_.reshape(shp)
    shapes = [w[k].shape for k in SMALL]
    packed = [_pack([src[k] for k in SMALL], F32, 8) for src in (w, g_out, m, v)]
    outs = adamw(*packed, "adamw_small")
    for dst, o in zip((delta, new_m, new_v), outs):
        for k, a in zip(SMALL, _unpack(o, shapes)):
            dst[k] = a
    return ((loss_out, dx[None]) + tuple(g_out[k] for k in WEIGHTS) + tuple(delta[k] for k in WEIGHTS)
            + tuple(new_m[k] for k in WEIGHTS) + tuple(new_v[k] for k in WEIGHTS))


def kernel(x, p, norm_ffn1, ffn1_w_in, ffn1_w_out, norm_mix, norm_ffn2, ffn2_w_in, ffn2_w_out, ple_norm, ple_gate_w, ple_proj_w, hyb_w_in, conv_dw_w, conv_dw_b, conv_ln_g, conv_ln_b, ssm_conv_w, ssm_conv_b, ssm_dt_bias, ssm_a_log, ssm_d, ssm_norm, hyb_w_out, att_w_qkv, att_b_qkv, att_sinks, att_w_o, att_b_o, final_norm, loss_target, m_norm_ffn1, m_ffn1_w_in, m_ffn1_w_out, m_norm_mix, m_norm_ffn2, m_ffn2_w_in, m_ffn2_w_out, m_ple_norm, m_ple_gate_w, m_ple_proj_w, m_hyb_w_in, m_conv_dw_w, m_conv_dw_b, m_conv_ln_g, m_conv_ln_b, m_ssm_conv_w, m_ssm_conv_b, m_ssm_dt_bias, m_ssm_a_log, m_ssm_d, m_ssm_norm, m_hyb_w_out, m_att_w_qkv, m_att_b_qkv, m_att_sinks, m_att_w_o, m_att_b_o, m_final_norm, v_norm_ffn1, v_ffn1_w_in, v_ffn1_w_out, v_norm_mix, v_norm_ffn2, v_ffn2_w_in, v_ffn2_w_out, v_ple_norm, v_ple_gate_w, v_ple_proj_w, v_hyb_w_in, v_conv_dw_w, v_conv_dw_b, v_conv_ln_g, v_conv_ln_b, v_ssm_conv_w, v_ssm_conv_b, v_ssm_dt_bias, v_ssm_a_log, v_ssm_d, v_ssm_norm, v_hyb_w_out, v_att_w_qkv, v_att_b_qkv, v_att_sinks, v_att_w_o, v_att_b_o, v_final_norm):
    given = locals()
    w = {k: given[k] for k in WEIGHTS}
    m = {k: given["m_" + k] for k in WEIGHTS}
    v = {k: given["v_" + k] for k in WEIGHTS}
    return _step(x, p, loss_target, w, m, v)
```

```python
import functools
import math

import numpy as np
import jax
import jax.numpy as jnp
from jax import lax
from jax.experimental import pallas as pl
from jax.experimental.pallas import tpu as pltpu

F32 = jnp.float32
BF16 = jnp.bfloat16
MXU_DTYPE = jnp.bfloat16
S = jax.ShapeDtypeStruct
MESH = pl.DeviceIdType.MESH

V7X_VMEM_BYTES = 64 * 2**20
VMEM_LIMIT = 48 * 2**20
LANE = 128

EPS = 1e-6
SSM_HEADS = 16
HEAD_DIM = 64
SSM_GROUPS = 2
SSM_STATE = 128
SSM_CONV = 4
CHUNK = 128
CONV_WIDTH = 31
ATT_HEADS = 16
ATT_KV_HEADS = 4
WINDOW = 128
ROPE_THETA = 10000.0
ADAM_LR = 0.001
ADAM_B1 = 0.9
ADAM_B2 = 0.999
ADAM_EPS = 1e-08
ADAM_WD = 0.01
ADAM_STEP = 10

N_CHIPS = 4
N_DEV = 8

NN = ((1,), (0,))
NT = ((1,), (1,))
TN = ((0,), (0,))


def _mm(a, b, dims=NN):
    return lax.dot_general(a.astype(MXU_DTYPE), b.astype(MXU_DTYPE), (dims, ((), ())), preferred_element_type=F32)


def _split3(a):
    hi = a.astype(BF16)
    r = a - hi.astype(F32)
    mid = r.astype(BF16)
    lo = (r - mid.astype(F32)).astype(BF16)
    return hi, mid, lo


def _mm01(a, onehot, dims=NN):
    o = onehot.astype(BF16)
    out = None
    for part in _split3(a):
        t = lax.dot_general(part, o, (dims, ((), ())), preferred_element_type=F32)
        out = t if out is None else out + t
    return out


def _01mm(onehot, a):
    o = onehot.astype(BF16)
    out = None
    for part in _split3(a):
        t = lax.dot_general(o, part, (NN, ((), ())), preferred_element_type=F32)
        out = t if out is None else out + t
    return out


def _sigmoid(x):
    return 0.5 * jnp.tanh(0.5 * x) + 0.5


def _softplus(x):
    return jnp.maximum(x, 0.0) + jnp.log(1.0 + jnp.exp(-jnp.abs(x)))


def _iota(shape, axis):
    return lax.broadcasted_iota(jnp.int32, shape, axis)


def _head_indicator(width, heads, transposed=False):
    per = width // heads
    if transposed:
        return (_iota((heads, width), 1) // per == _iota((heads, width), 0)).astype(F32)
    return (_iota((width, heads), 0) // per == _iota((width, heads), 1)).astype(F32)


def _acc(ref, i, val):
    @pl.when(i == 0)
    def _():
        ref[...] = val

    @pl.when(i > 0)
    def _():
        ref[...] += val


def _rs(tile, width, col=0, shift=0, n=None):
    if shift == 0:
        return pl.BlockSpec((tile, width), lambda i: (i, col))
    if shift < 0:
        return pl.BlockSpec((tile, width), lambda i: (jnp.maximum(i - 1, 0), col))
    return pl.BlockSpec((tile, width), lambda i: (jnp.minimum(i + 1, n - 1), col))


def _ps(shape):
    return pl.BlockSpec(shape, lambda i: (0,) * len(shape))


def _call(body, name, grid, in_specs, out_specs, out_shape, scratch=(), sem=None):
    return pl.pallas_call(
        body, name=name, grid=grid, in_specs=in_specs, out_specs=out_specs, out_shape=out_shape,
        scratch_shapes=list(scratch),
        compiler_params=pltpu.CompilerParams(dimension_semantics=sem, vmem_limit_bytes=VMEM_LIMIT))


def _row_tile(n, target):
    t = min(n, target)
    assert n % t == 0, (n, t)
    return t


def _pick_tile(dim, target):
    if dim <= target:
        return dim
    t = (int(1.4 * target) // LANE) * LANE
    while t >= LANE:
        if dim % t == 0:
            return t
        t -= LANE
    return dim


ANY = pl.BlockSpec(memory_space=pl.ANY)


def _paired(j):
    return (j % 2) * 2 + j // 2


class ColSharded:
    def __init__(self, arr, paired=False):
        self.arr, self.paired = arr, paired
        self.nch, self.rows, self.per = arr.shape
        self.shape = (self.rows, self.nch * self.per)

    def chip(self, j):
        return _paired(j) if self.paired else j


class Slot:
    def __init__(self, buf, kind, per, off, c0=0, paired=False):
        self.buf, self.kind, self.per, self.off, self.c0, self.paired = buf, kind, per, off, c0, paired

    def chip(self, j):
        return _paired(j) if self.paired else j


def matmul(a, b, mode, name, *, out_dtype=F32, scale=None, res=None, bias=None, into=None, tm=1024, tn=1024, tk=1024):
    bshape = b.shape
    if mode == "nn":
        (m, k), (k2, n) = a.shape, bshape
    elif mode == "nt":
        (m, k), (n, k2) = a.shape, bshape
    else:
        (k, m), (k2, n) = a.shape, bshape
    assert k == k2, (a.shape, bshape, mode)
    tm, tn, tk = _pick_tile(m, tm), _pick_tile(n, tn), _pick_tile(k, tk)
    if isinstance(b, ColSharded):
        if mode == "nn":
            tn = b.per
        else:
            assert mode == "nt"
            tk = b.per
    if into is not None:
        if into.kind == "c":
            tn = into.per
            assert into.off % tm == 0 and n == N_CHIPS * into.per
        else:
            tm = max(1, min(m, int(1.4 * 1024)) // into.per) * into.per
            assert m % tm == 0 and into.off % into.per == 0 and into.c0 % (tm // into.per) == 0
    nk = k // tk
    dims = {"nn": NN, "nt": NT, "tn": TN}[mode]
    a_spec = (pl.BlockSpec((tk, tm), lambda i, j, kk: (kk, i)) if mode == "tn"
              else pl.BlockSpec((tm, tk), lambda i, j, kk: (i, kk)))
    if isinstance(b, ColSharded):
        bchip = b.chip
        b_spec = (pl.BlockSpec((None, tk, tn), lambda i, j, kk: (bchip(j), kk, 0)) if mode == "nn"
                  else pl.BlockSpec((None, tn, tk), lambda i, j, kk: (bchip(kk), j, 0)))
        b = b.arr
    else:
        b_spec = (pl.BlockSpec((tn, tk), lambda i, j, kk: (j, kk)) if mode == "nt"
                  else pl.BlockSpec((tk, tn), lambda i, j, kk: (kk, j)))
    plain_o = pl.BlockSpec((tm, tn), lambda i, j, kk: (i, j))
    ins, in_specs = [a, b], [a_spec, b_spec]
    if bias is not None:
        ins.append(bias)
        in_specs.append(pl.BlockSpec((1, tn), lambda i, j, kk: (0, j)))
    if res is not None:
        ins.append(res)
        in_specs.append(plain_o)
    aliases = {}
    if into is None:
        o_spec, o_shape = plain_o, S((m, n), out_dtype)
    else:
        aliases = {len(ins): 0}
        ins.append(into.buf)
        in_specs.append(ANY)
        o_shape = S(into.buf.shape, into.buf.dtype)
        if into.kind == "c":
            ob, ochip = into.off // tm, into.chip
            o_spec = pl.BlockSpec((None, tm, tn), lambda i, j, kk: (ochip(j), ob + i, 0))
        else:
            q, ob = tm // into.per, into.off // into.per
            cb = into.c0 // q
            o_spec = pl.BlockSpec((q, into.per, tn), lambda i, j, kk: (cb + i, ob, j))

    def body(*refs):
        a_ref, b_ref = refs[0], refs[1]
        o_ref, acc_ref = refs[-2], refs[-1]
        kk = pl.program_id(2)

        @pl.when(kk == 0)
        def _():
            acc_ref[...] = jnp.zeros_like(acc_ref)

        acc_ref[...] += _mm(a_ref[...], b_ref[...], dims)

        @pl.when(kk == nk - 1)
        def _():
            out = acc_ref[...]
            if scale is not None:
                out = out * scale
            pos = 2
            if bias is not None:
                out = out + refs[pos][...]
                pos += 1
            if res is not None:
                out = out + refs[pos][...]
            o_ref[...] = out.astype(o_ref.dtype).reshape(o_ref.shape)

    return pl.pallas_call(
        body, name=name, grid=(m // tm, n // tn, nk), in_specs=in_specs, out_specs=o_spec, out_shape=o_shape,
        scratch_shapes=[pltpu.VMEM((tm, tn), F32)], input_output_aliases=aliases,
        compiler_params=pltpu.CompilerParams(dimension_semantics=("parallel", "parallel", "arbitrary"),
                                             vmem_limit_bytes=VMEM_LIMIT))(*ins)


def rms_fwd(h, g, name):
    n, d = h.shape
    tile = _row_tile(n, 512)

    def body(h_ref, g_ref, o_ref):
        x = h_ref[...]
        r = lax.rsqrt(jnp.mean(x * x, axis=-1, keepdims=True) + EPS)
        o_ref[...] = (x * r * g_ref[...]).astype(o_ref.dtype)

    return _call(body, name, (n // tile,), [_rs(tile, d), _ps((1, d))], _rs(tile, d), S((n, d), MXU_DTYPE),
                 sem=("parallel",))(h, g)


def _rms_bwd_math(x, g, dy):
    r = lax.rsqrt(jnp.mean(x * x, axis=-1, keepdims=True) + EPS)
    xh = x * r
    dg = jnp.sum(dy * xh, axis=0, keepdims=True)
    dxh = dy * g
    dx = r * (dxh - xh * jnp.mean(dxh * xh, axis=-1, keepdims=True))
    return dx, dg


def rms_bwd(h, g, dxn, dh_in, name, colsum=False):
    n, d = h.shape
    tile = _row_tile(n, 256)

    def body(h_ref, g_ref, dxn_ref, dh_ref, o_ref, dg_ref, *cs_ref):
        i = pl.program_id(0)
        dx, dg = _rms_bwd_math(h_ref[...], g_ref[...], dxn_ref[...].astype(F32))
        out = dh_ref[...] + dx
        o_ref[...] = out
        _acc(dg_ref, i, dg)
        if colsum:
            _acc(cs_ref[0], i, jnp.sum(out, axis=0, keepdims=True))

    outs = [S((n, d), F32), S((1, d), F32)] + ([S((1, d), F32)] if colsum else [])
    ospecs = [_rs(tile, d), _ps((1, d))] + ([_ps((1, d))] if colsum else [])
    return _call(body, name, (n // tile,), [_rs(tile, d), _ps((1, d)), _rs(tile, d), _rs(tile, d)], ospecs, outs,
                 sem=("arbitrary",))(h, g, dxn, dh_in)


def swiglu_in(h, g, w_in, name):
    n, d = h.shape
    per = w_in.per
    nj = w_in.nch // 2
    tile = _row_tile(n, 512)

    def body(h_ref, g_ref, wg_ref, wu_ref, xn_ref, u_ref, hm_ref):
        x = h_ref[...]
        r = lax.rsqrt(jnp.mean(x * x, axis=-1, keepdims=True) + EPS)
        xn = (x * r * g_ref[...]).astype(xn_ref.dtype)

        @pl.when(pl.program_id(1) == 0)
        def _():
            xn_ref[...] = xn

        a = _mm(xn, wg_ref[...])
        b = _mm(xn, wu_ref[...])
        u_ref[:, :per] = a
        u_ref[:, per:] = b
        hm_ref[...] = (a * _sigmoid(a) * b).astype(hm_ref.dtype)

    return pl.pallas_call(
        body, name=name, grid=(n // tile, nj),
        in_specs=[pl.BlockSpec((tile, d), lambda i, j: (i, 0)), pl.BlockSpec((1, d), lambda i, j: (0, 0)),
                  pl.BlockSpec((None, d, per), lambda i, j: (j, 0, 0)), pl.BlockSpec((None, d, per), lambda i, j: (nj + j, 0, 0))],
        out_specs=[pl.BlockSpec((tile, d), lambda i, j: (i, 0)), pl.BlockSpec((tile, 2 * per), lambda i, j: (i, j)),
                   pl.BlockSpec((tile, per), lambda i, j: (i, j))],
        out_shape=[S((n, d), MXU_DTYPE), S((n, 2 * nj * per), F32), S((n, nj * per), MXU_DTYPE)],
        compiler_params=pltpu.CompilerParams(dimension_semantics=("parallel", "arbitrary"), vmem_limit_bytes=VMEM_LIMIT),
    )(h, g, w_in.arr, w_in.arr)


def swiglu_out_bwd(dh, w_out, u, after, name):
    n, d = dh.shape
    f = w_out.shape[0]
    per = u.shape[1] // 4
    nj = f // per
    tile = _row_tile(n, 512)

    def body(dh_ref, w_ref, u_ref, after_ref, du_ref):
        dm = 0.5 * _mm(dh_ref[...], w_ref[...], NT)
        a = u_ref[:, :per]
        b = u_ref[:, per:]
        s = _sigmoid(a)
        du_ref[:, :per] = (dm * b * s * (1.0 + a * (1.0 - s))).astype(du_ref.dtype)
        du_ref[:, per:] = (dm * a * s).astype(du_ref.dtype)

    return pl.pallas_call(
        body, name=name, grid=(n // tile, nj),
        in_specs=[pl.BlockSpec((tile, d), lambda i, j: (i, 0)), pl.BlockSpec((per, d), lambda i, j: (j, 0)),
                  pl.BlockSpec((tile, 2 * per), lambda i, j: (i, j)), ANY],
        out_specs=pl.BlockSpec((tile, 2 * per), lambda i, j: (i, j)),
        out_shape=S(u.shape, MXU_DTYPE),
        compiler_params=pltpu.CompilerParams(dimension_semantics=("parallel", "parallel"), vmem_limit_bytes=VMEM_LIMIT),
    )(dh, w_out, u, after)


def ple_fwd(h, gl, pp, name):
    n, d = h.shape
    tile = _row_tile(n, 512)

    def body(h_ref, gl_ref, pp_ref, o_ref):
        o_ref[...] = h_ref[...] + _sigmoid(gl_ref[...]) * pp_ref[...]

    return _call(body, name, (n // tile,), [_rs(tile, d)] * 3, _rs(tile, d), S((n, d), F32), sem=("parallel",))(h, gl, pp)


def ple_bwd(dh, gl, pp, after, name):
    n, d = dh.shape
    tile = _row_tile(n, 512)

    def body(dh_ref, gl_ref, pp_ref, after_ref, dpp_ref, dgl_ref):
        g = _sigmoid(gl_ref[...])
        dh_ = dh_ref[...]
        dpp_ref[...] = (dh_ * g).astype(dpp_ref.dtype)
        dgl_ref[...] = (dh_ * pp_ref[...] * g * (1.0 - g)).astype(dgl_ref.dtype)

    return _call(body, name, (n // tile,), [_rs(tile, d)] * 3 + [ANY], [_rs(tile, d)] * 2, [S((n, d), MXU_DTYPE)] * 2,
                 sem=("parallel",))(dh, gl, pp, after)


def loss_head(h, g, target, name):
    n, d = h.shape
    tile = _row_tile(n, 256)

    def body(h_ref, g_ref, t_ref, dh_ref, dg_ref, loss_ref):
        i = pl.program_id(0)
        x = h_ref[...]
        gg = g_ref[...]
        r = lax.rsqrt(jnp.mean(x * x, axis=-1, keepdims=True) + EPS)
        err = x * r * gg - t_ref[...]
        part = 0.5 * jnp.sum(jnp.mean(err * err, axis=-1, keepdims=True), axis=0, keepdims=True)
        dx, dg = _rms_bwd_math(x, gg, err * (1.0 / d))
        dh_ref[...] = dx
        _acc(dg_ref, i, dg)
        _acc(loss_ref, i, jnp.broadcast_to(part, (8, LANE)))

    return _call(body, name, (n // tile,), [_rs(tile, d), _ps((1, d)), _rs(tile, d)],
                 [_rs(tile, d), _ps((1, d)), _ps((8, LANE))], [S((n, d), F32), S((1, d), F32), S((8, LANE), F32)],
                 sem=("arbitrary",))(h, g, target)


def _adamw_math(w, g, m, v):
    c1 = np.float32(1.0 - ADAM_B1 ** ADAM_STEP)
    c2 = np.float32(1.0 - ADAM_B2 ** ADAM_STEP)
    mm = ADAM_B1 * m + (1.0 - ADAM_B1) * g
    vv = ADAM_B2 * v + (1.0 - ADAM_B2) * (g * g)
    return -ADAM_LR * ((mm / c1) / (jnp.sqrt(vv / c2) + ADAM_EPS) + ADAM_WD * w), mm, vv


def adamw_layer(w, pack, off, m, v, li, prev, name):
    n, r, c = w.shape
    tile = next(t for t in (512, 256, 128, 64, 32, 16, 8) if r % t == 0 and off % t == 0 and t * c * 4 <= 2**21)
    ob = off // tile

    def body(w_ref, g_ref, m_ref, v_ref, *refs):
        go_ref, d_ref, mo_ref, vo_ref = refs[-4:]
        g = g_ref[...]
        go_ref[...] = g
        d_ref[...], mo_ref[...], vo_ref[...] = _adamw_math(w_ref[...], g, m_ref[...], v_ref[...])

    blk = pl.BlockSpec((None, tile, c), lambda i: (li, i, 0))
    prev = list(prev) if prev is not None else []
    return pl.pallas_call(
        body, name=name, grid=(r // tile,),
        in_specs=[blk, pl.BlockSpec((tile, c), lambda i: (ob + i, 0)), blk, blk] + [ANY] * len(prev),
        out_specs=[blk] * 4, out_shape=[S((n, r, c), F32)] * 4,
        input_output_aliases={4 + j: j for j in range(len(prev))},
        compiler_params=pltpu.CompilerParams(dimension_semantics=("parallel",), vmem_limit_bytes=VMEM_LIMIT),
    )(w, pack, m, v, *prev)


def adamw(w, g, m, v, name):
    r, c = w.shape
    tile = r
    for t in (512, 256, 128, 64, 32, 16, 8):
        if r % t == 0 and t * c * 4 <= 2**21:
            tile = t
            break

    def body(w_ref, g_ref, m_ref, v_ref, d_ref, mo_ref, vo_ref):
        d_ref[...], mo_ref[...], vo_ref[...] = _adamw_math(w_ref[...], g_ref[...], m_ref[...], v_ref[...])

    return _call(body, name, (r // tile,), [_rs(tile, c)] * 4, [_rs(tile, c)] * 3, [S((r, c), F32)] * 3,
                 sem=("parallel",))(w, g, m, v)


def _taps_fwd(sc, w_ref, width, halo, tile, acc):
    for k in range(width):
        o = halo - (width - 1) + k
        acc = acc + w_ref[k:k + 1, :] * sc[o:o + tile, :]
    return acc


def _taps_bwd_x(sc_d, w_ref, width, tile, acc):
    for k in range(width):
        o = (width - 1) - k
        acc = acc + w_ref[k:k + 1, :] * sc_d[o:o + tile, :]
    return acc


def _taps_bwd_w(dy, sc, dw_ref, width, halo, tile, i):
    @pl.when(i == 0)
    def _():
        dw_ref[...] = jnp.zeros_like(dw_ref)

    for k in range(width):
        o = halo - (width - 1) + k
        dw_ref[k:k + 1, :] += jnp.sum(dy * sc[o:o + tile, :], axis=0, keepdims=True)


def _ln_stats(x):
    mu = jnp.mean(x, axis=-1, keepdims=True)
    xc = x - mu
    r = lax.rsqrt(jnp.mean(xc * xc, axis=-1, keepdims=True) + EPS)
    return xc * r, r


def conv_group_fwd(proj, cw, cb, lg, lb, name):
    n = proj.shape[0]
    d = cw.shape[1]
    tile = _row_tile(n, 256)
    halo = 32

    def body(v_ref, g_ref, vp_ref, gp_ref, cw_ref, cb_ref, lg_ref, lb_ref, u_ref, u1_ref, sc):
        i = pl.program_id(0)
        first = (i > 0).astype(F32)
        sc[0:halo, :] = vp_ref[tile - halo:, :] * _sigmoid(gp_ref[tile - halo:, :]) * first
        sc[halo:, :] = v_ref[...] * _sigmoid(g_ref[...])
        u1 = _taps_fwd(sc, cw_ref, CONV_WIDTH, halo, tile, jnp.zeros((tile, d), F32) + cb_ref[...])
        u1_ref[...] = u1
        xh, _ = _ln_stats(u1)
        y = xh * lg_ref[...] + lb_ref[...]
        u_ref[...] = (y * _sigmoid(y)).astype(u_ref.dtype)

    return _call(body, name, (n // tile,),
                 [_rs(tile, d, 0), _rs(tile, d, 1), _rs(tile, d, 0, -1), _rs(tile, d, 1, -1),
                  _ps(cw.shape), _ps((1, d)), _ps((1, d)), _ps((1, d))],
                 [_rs(tile, d), _rs(tile, d)], [S((n, d), MXU_DTYPE), S((n, d), F32)],
                 scratch=[pltpu.VMEM((halo + tile, d), F32)], sem=("arbitrary",))(proj, proj, proj, proj, cw, cb, lg, lb)


def conv_group_bwd(du, u1, proj, cw, lg, lb, name):
    n = proj.shape[0]
    d = cw.shape[1]
    tile = _row_tile(n, 256)
    halo = 32
    nt = n // tile

    def body(du_ref, dun_ref, u1_ref, u1n_ref, v_ref, g_ref, vp_ref, gp_ref, cw_ref, lg_ref, lb_ref,
             dp_ref, dcw_ref, dcb_ref, dlg_ref, dlb_ref, sc, sc_d):
        i = pl.program_id(0)

        def ln_swish_bwd(dy_, u1_):
            xh, r = _ln_stats(u1_)
            y = xh * lg_ref[...] + lb_ref[...]
            s = _sigmoid(y)
            dyy = dy_ * s * (1.0 + y * (1.0 - s))
            dxh = dyy * lg_ref[...]
            dx = r * (dxh - jnp.mean(dxh, axis=-1, keepdims=True) - xh * jnp.mean(dxh * xh, axis=-1, keepdims=True))
            return dx, jnp.sum(dyy * xh, axis=0, keepdims=True), jnp.sum(dyy, axis=0, keepdims=True)

        du1, dlg, dlb = ln_swish_bwd(du_ref[...].astype(F32), u1_ref[...])
        du1n, _, _ = ln_swish_bwd(dun_ref[0:halo, :].astype(F32), u1n_ref[0:halo, :])
        sc_d[0:tile, :] = du1
        sc_d[tile:, :] = du1n * (i < nt - 1).astype(F32)
        sig = _sigmoid(g_ref[...])
        val = v_ref[...]
        sc[0:halo, :] = vp_ref[tile - halo:, :] * _sigmoid(gp_ref[tile - halo:, :]) * (i > 0).astype(F32)
        sc[halo:, :] = val * sig
        du0 = _taps_bwd_x(sc_d, cw_ref, CONV_WIDTH, tile, jnp.zeros((tile, d), F32))
        _taps_bwd_w(du1, sc, dcw_ref, CONV_WIDTH, halo, tile, i)
        _acc(dcb_ref, i, jnp.sum(du1, axis=0, keepdims=True))
        _acc(dlg_ref, i, dlg)
        _acc(dlb_ref, i, dlb)
        dp_ref[:, :d] = (du0 * sig).astype(dp_ref.dtype)
        dp_ref[:, d:] = (du0 * val * sig * (1.0 - sig)).astype(dp_ref.dtype)

    return _call(body, name, (nt,),
                 [_rs(tile, d), _rs(tile, d, 0, 1, nt), _rs(tile, d), _rs(tile, d, 0, 1, nt),
                  _rs(tile, d, 0), _rs(tile, d, 1), _rs(tile, d, 0, -1), _rs(tile, d, 1, -1),
                  _ps(cw.shape), _ps((1, d)), _ps((1, d))],
                 [_rs(tile, 2 * d), _ps(cw.shape), _ps((1, d)), _ps((1, d)), _ps((1, d))],
                 [S((n, proj.shape[1]), MXU_DTYPE), S(cw.shape, F32), S((1, d), F32), S((1, d), F32), S((1, d), F32)],
                 scratch=[pltpu.VMEM((halo + tile, d), F32), pltpu.VMEM((tile + halo, d), F32)],
                 sem=("arbitrary",))(du, du, u1, u1, proj, proj, proj, proj, cw, lg, lb)


def ssm_conv_fwd(proj, dtr, sw, sb, dtb, name):
    n = proj.shape[0]
    w = sw.shape[1]
    inner = SSM_HEADS * HEAD_DIM
    tile = _row_tile(n, 256)
    halo = 8

    def body(x_ref, xp_ref, dtr_ref, sw_ref, sb_ref, dtb_ref, pre_ref, xs_ref, bc_ref, dt_ref, sc):
        i = pl.program_id(0)
        sc[0:halo, :] = xp_ref[tile - halo:, :] * (i > 0).astype(F32)
        sc[halo:, :] = x_ref[...]
        pre = _taps_fwd(sc, sw_ref, SSM_CONV, halo, tile, jnp.zeros((tile, w), F32) + sb_ref[...])
        pre_ref[...] = pre
        act = pre * _sigmoid(pre)
        xs_ref[...] = act[:, :inner]
        bc_ref[...] = act[:, inner:]
        dt = _softplus(dtr_ref[...] + dtb_ref[...])
        dt_ref[...] = jnp.where(_iota(dt.shape, 1) < SSM_HEADS, dt, 0.0)

    return _call(body, name, (n // tile,),
                 [_rs(tile, w, 2), _rs(tile, w, 2, -1), _rs(tile, LANE), _ps(sw.shape), _ps((1, w)), _ps((1, LANE))],
                 [_rs(tile, w), _rs(tile, inner), _rs(tile, w - inner), _rs(tile, LANE)],
                 [S((n, w), F32), S((n, inner), F32), S((n, w - inner), F32), S((n, LANE), F32)],
                 scratch=[pltpu.VMEM((halo + tile, w), F32)], sem=("arbitrary",))(proj, proj, dtr, sw, sb, dtb)


def ssm_conv_bwd(dxs, dbc, pre, proj, sw, dproj, name):
    n = proj.shape[0]
    w = sw.shape[1]
    inner = SSM_HEADS * HEAD_DIM
    tile = _row_tile(n, 256)
    halo = 8
    nt = n // tile

    def body(dxs_ref, dxsn_ref, dbc_ref, dbcn_ref, pre_ref, pren_ref, x_ref, xp_ref, sw_ref, dp_in_ref,
             dx_ref, dsw_ref, dsb_ref, sc, sc_d):
        i = pl.program_id(0)

        def silu_bwd(d_, p_):
            s = _sigmoid(p_)
            return d_ * s * (1.0 + p_ * (1.0 - s))

        sc_d[0:tile, :inner] = silu_bwd(dxs_ref[...], pre_ref[:, :inner])
        sc_d[0:tile, inner:] = silu_bwd(dbc_ref[...], pre_ref[:, inner:])
        last = (i < nt - 1).astype(F32)
        sc_d[tile:, :inner] = silu_bwd(dxsn_ref[0:halo, :], pren_ref[0:halo, :inner]) * last
        sc_d[tile:, inner:] = silu_bwd(dbcn_ref[0:halo, :], pren_ref[0:halo, inner:]) * last
        sc[0:halo, :] = xp_ref[tile - halo:, :] * (i > 0).astype(F32)
        sc[halo:, :] = x_ref[...]
        dpre = sc_d[0:tile, :]
        dx_ref[...] = _taps_bwd_x(sc_d, sw_ref, SSM_CONV, tile, jnp.zeros((tile, w), F32)).astype(dx_ref.dtype)
        _taps_bwd_w(dpre, sc, dsw_ref, SSM_CONV, halo, tile, i)
        _acc(dsb_ref, i, jnp.sum(dpre, axis=0, keepdims=True))

    return pl.pallas_call(
        body, name=name, grid=(nt,),
        in_specs=[_rs(tile, inner), _rs(tile, inner, 0, 1, nt), _rs(tile, w - inner), _rs(tile, w - inner, 0, 1, nt),
                  _rs(tile, w), _rs(tile, w, 0, 1, nt), _rs(tile, w, 2), _rs(tile, w, 2, -1), _ps(sw.shape), ANY],
        out_specs=[_rs(tile, w, 2), _ps(sw.shape), _ps((1, w))],
        out_shape=[S(dproj.shape, dproj.dtype), S(sw.shape, F32), S((1, w), F32)],
        scratch_shapes=[pltpu.VMEM((halo + tile, w), F32), pltpu.VMEM((tile + halo, w), F32)],
        input_output_aliases={9: 0},
        compiler_params=pltpu.CompilerParams(dimension_semantics=("arbitrary",), vmem_limit_bytes=VMEM_LIMIT),
    )(dxs, dxs, dbc, dbc, pre, pre, proj, proj, sw, dproj)


def _ssd_prologue(dt_ref, dtT_ref, al_ref, alc_ref):
    row = _iota((CHUNK, CHUNK), 0)
    col = _iota((CHUNK, CHUNK), 1)
    dt = dt_ref[:, :SSM_HEADS]
    a_row = -jnp.exp(al_ref[:, :SSM_HEADS])
    a_col = -jnp.exp(alc_ref[...])
    cs = _01mm((row >= col).astype(F32), dt * a_row)
    csT = _mm01(dtT_ref[...] * a_col, (row <= col).astype(F32))
    return dt, a_row, cs, csT, row, col


def _decay(cs, csT, h, row, col):
    lm = jnp.exp(jnp.where(row >= col, cs[:, h:h + 1] - csT[h:h + 1, :], -1e30))
    lmT = jnp.exp(jnp.where(col >= row, csT[h:h + 1, :] - cs[:, h:h + 1], -1e30))
    return lm, lmT


def ssd_fwd(xs, bc, dt, dtT, alog_row, alog_col, name):
    n, width = xs.shape
    nc = n // CHUNK
    gw = width // SSM_GROUPS
    hpg = SSM_HEADS // SSM_GROUPS
    ns = SSM_STATE

    def body(xs_ref, bc_ref, dt_ref, dtT_ref, al_ref, alc_ref, y_ref, hs_ref, h_sc):
        i = pl.program_id(0)

        @pl.when(i == 0)
        def _():
            h_sc[...] = jnp.zeros_like(h_sc)

        dt, a_row, cs, csT, row, col = _ssd_prologue(dt_ref, dtT_ref, al_ref, alc_ref)
        indT = _head_indicator(width, SSM_HEADS, transposed=True)
        dt_full = _mm01(dt, indT)
        e_full = jnp.exp(_mm01(cs, indT))
        dte_full = jnp.exp(_mm01(cs[CHUNK - 1:CHUNK, :] - cs, indT))
        xt = xs_ref[...] * dt_full
        hs_ref[0] = h_sc[...]
        lo = _iota((CHUNK, 2 * HEAD_DIM), 1) < HEAD_DIM
        for g in range(SSM_GROUPS):
            bg = bc_ref[:, g * ns:(g + 1) * ns]
            cg = bc_ref[:, (SSM_GROUPS + g) * ns:(SSM_GROUPS + g + 1) * ns]
            gm = _mm(cg, bg, NT)
            hg = h_sc[g * gw:(g + 1) * gw, :]
            yoff = e_full[:, g * gw:(g + 1) * gw] * _mm(cg, hg, NT)
            for pr in range(hpg // 2):
                h0 = g * hpg + 2 * pr
                c0 = h0 * HEAD_DIM
                xp = xt[:, c0:c0 + 2 * HEAD_DIM]
                m0 = gm * _decay(cs, csT, h0, row, col)[0]
                m1 = gm * _decay(cs, csT, h0 + 1, row, col)[0]
                yd = jnp.where(lo, _mm(m0, xp), _mm(m1, xp))
                y_ref[:, c0:c0 + 2 * HEAD_DIM] = yd + yoff[:, 2 * pr * HEAD_DIM:(2 * pr + 2) * HEAD_DIM]
            sg = _mm(xt[:, g * gw:(g + 1) * gw] * dte_full[:, g * gw:(g + 1) * gw], bg, TN)
            for hh in range(hpg):
                h = g * hpg + hh
                r0 = h * HEAD_DIM
                h_sc[r0:r0 + HEAD_DIM, :] = (h_sc[r0:r0 + HEAD_DIM, :] * jnp.exp(csT[h:h + 1, CHUNK - 1:CHUNK])
                                             + sg[hh * HEAD_DIM:(hh + 1) * HEAD_DIM, :])

    bcw = bc.shape[1]
    return _call(body, name, (nc,),
                 [_rs(CHUNK, width), _rs(CHUNK, bcw), _rs(CHUNK, LANE), pl.BlockSpec((SSM_HEADS, CHUNK), lambda i: (0, i)),
                  _ps((1, LANE)), _ps((SSM_HEADS, 1))],
                 [_rs(CHUNK, width), pl.BlockSpec((1, width, ns), lambda i: (i, 0, 0))],
                 [S((n, width), F32), S((nc, width, ns), F32)],
                 scratch=[pltpu.VMEM((width, ns), F32)], sem=("arbitrary",))(xs, bc, dt, dtT, alog_row, alog_col)


def ssd_bwd(xs, bc, dt, dtT, alog_row, alog_col, hs, dy, dxs_skip, name):
    n, width = xs.shape
    nc = n // CHUNK
    gw = width // SSM_GROUPS
    hpg = SSM_HEADS // SSM_GROUPS
    ns = SSM_STATE
    bcw = bc.shape[1]

    def body(xs_ref, bc_ref, dt_ref, dtT_ref, al_ref, alc_ref, hs_ref, dy_ref, skip_ref,
             dxs_ref, dbc_ref, ddtr_ref, dal_ref, ddtb_ref, dh_sc, dxt_sc):
        i = pl.program_id(0)

        @pl.when(i == 0)
        def _():
            dh_sc[...] = jnp.zeros_like(dh_sc)

        dt, a_row, cs, csT, row, col = _ssd_prologue(dt_ref, dtT_ref, al_ref, alc_ref)
        indT = _head_indicator(width, SSM_HEADS, transposed=True)
        ind = _head_indicator(width, SSM_HEADS)
        dt_full = _mm01(dt, indT)
        e_full = jnp.exp(_mm01(cs, indT))
        cs_last = cs[CHUNK - 1:CHUNK, :]
        dte = jnp.exp(cs_last - cs)
        dte_full = _mm01(dte, indT)
        xs_ = xs_ref[...]
        xt = xs_ * dt_full
        dy_ = dy_ref[...]
        hmat = hs_ref[0]
        ds = dh_sc[...]
        lo = _iota((CHUNK, 2 * HEAD_DIM), 1) < HEAD_DIM
        head_lane = _iota((1, SSM_HEADS), 1)
        dcs = jnp.zeros((CHUNK, SSM_HEADS), F32)
        ddte = jnp.zeros((CHUNK, SSM_HEADS), F32)
        for g in range(SSM_GROUPS):
            sl = slice(g * gw, (g + 1) * gw)
            bg = bc_ref[:, g * ns:(g + 1) * ns]
            cg = bc_ref[:, (SSM_GROUPS + g) * ns:(SSM_GROUPS + g + 1) * ns]
            gm = _mm(cg, bg, NT)
            gmT = _mm(bg, cg, NT)
            hg = hmat[sl, :]
            dsg = ds[sl, :]
            dyg = dy_[:, sl]
            xtg = xt[:, sl]
            yoff = e_full[:, sl] * _mm(cg, hg, NT)
            edy = e_full[:, sl] * dyg
            bds = _mm(bg, dsg, NT)
            dxt_g = dte_full[:, sl] * bds
            ddte = ddte + _mm01(xtg * bds, ind[sl, :])
            dcs = dcs + _mm01(dyg * yoff, ind[sl, :])
            db = _mm(xtg * dte_full[:, sl], dsg)
            dc = _mm(edy, hg)
            dhc = _mm(edy, cg, TN)
            dgs = jnp.zeros((CHUNK, CHUNK), F32)
            dgTs = jnp.zeros((CHUNK, CHUNK), F32)
            for pr in range(hpg // 2):
                h0 = g * hpg + 2 * pr
                c0 = 2 * pr * HEAD_DIM
                xp = xtg[:, c0:c0 + 2 * HEAD_DIM]
                dyp = dyg[:, c0:c0 + 2 * HEAD_DIM]
                rr = []
                for h, half in ((h0, lo), (h0 + 1, jnp.logical_not(lo))):
                    lm, lmT = _decay(cs, csT, h, row, col)
                    xm = jnp.where(half, xp, 0.0)
                    dm = _mm(dyp, xm, NT)
                    dmT = _mm(xm, dyp, NT)
                    mT = gmT * lmT
                    z = jnp.sum(dm * (gm * lm), axis=1, keepdims=True) - jnp.sum(dmT * mT, axis=1, keepdims=True)
                    dcs = dcs + z * (head_lane == h).astype(F32)
                    dgs = dgs + dm * lm
                    dgTs = dgTs + dmT * lmT
                    rr.append(_mm(mT, dyp))
                dxt_sc[:, g * gw + c0:g * gw + c0 + 2 * HEAD_DIM] = jnp.where(lo, rr[0], rr[1]) + dxt_g[:, c0:c0 + 2 * HEAD_DIM]
            dbc_ref[:, g * ns:(g + 1) * ns] = db + _mm(dgTs, cg)
            dbc_ref[:, (SSM_GROUPS + g) * ns:(SSM_GROUPS + g + 1) * ns] = dc + _mm(dgs, bg)
            for hh in range(hpg):
                h = g * hpg + hh
                r0 = h * HEAD_DIM
                dh_sc[r0:r0 + HEAD_DIM, :] = (dhc[hh * HEAD_DIM:(hh + 1) * HEAD_DIM, :]
                                              + jnp.exp(csT[h:h + 1, CHUNK - 1:CHUNK]) * ds[r0:r0 + HEAD_DIM, :])
        t = ddte * dte
        per_head = jnp.sum(jnp.sum(ds * hmat, axis=1, keepdims=True) * ind, axis=0, keepdims=True)
        last_add = jnp.sum(t, axis=0, keepdims=True) + jnp.exp(cs_last) * per_head
        dcs = dcs - t + jnp.where(_iota((CHUNK, SSM_HEADS), 0) == CHUNK - 1, last_add, 0.0)
        dadt = _01mm((row <= col).astype(F32), dcs)
        dxt = dxt_sc[...]
        ddt = dadt * a_row + _mm01(dxt * xs_, ind)
        dxs_ref[...] = dxt * dt_full + skip_ref[...]
        ddtr = ddt * (1.0 - jnp.exp(-dt))
        ddtr_ref[...] = jnp.zeros_like(ddtr_ref)
        ddtr_ref[:, :SSM_HEADS] = ddtr.astype(ddtr_ref.dtype)
        _acc(dal_ref, i, jnp.sum(dadt * dt, axis=0, keepdims=True) * a_row)
        _acc(ddtb_ref, i, jnp.sum(ddtr, axis=0, keepdims=True))

    rev = lambda i: (nc - 1 - i, 0)
    return _call(body, name, (nc,),
                 [pl.BlockSpec((CHUNK, width), rev), pl.BlockSpec((CHUNK, bcw), rev), pl.BlockSpec((CHUNK, LANE), rev),
                  pl.BlockSpec((SSM_HEADS, CHUNK), lambda i: (0, nc - 1 - i)), _ps((1, LANE)), _ps((SSM_HEADS, 1)),
                  pl.BlockSpec((1, width, ns), lambda i: (nc - 1 - i, 0, 0)), pl.BlockSpec((CHUNK, width), rev),
                  pl.BlockSpec((CHUNK, width), rev)],
                 [pl.BlockSpec((CHUNK, width), rev), pl.BlockSpec((CHUNK, bcw), rev), pl.BlockSpec((CHUNK, LANE), rev),
                  _ps((1, SSM_HEADS)), _ps((1, SSM_HEADS))],
                 [S((n, width), F32), S((n, bcw), F32), S((n, LANE), MXU_DTYPE), S((1, SSM_HEADS), F32), S((1, SSM_HEADS), F32)],
                 scratch=[pltpu.VMEM((width, ns), F32), pltpu.VMEM((CHUNK, width), F32)],
                 sem=("arbitrary",))(xs, bc, dt, dtT, alog_row, alog_col, hs, dy, dxs_skip)


def ssm_gate_fwd(yssd, xs, proj, dfull, gamma, name):
    n, d = yssd.shape
    tile = _row_tile(n, 256)
    gw = d // SSM_GROUPS

    def body(y_ref, xs_ref, z_ref, df_ref, gm_ref, o_ref):
        z = z_ref[...]
        y2 = (y_ref[...] + df_ref[...] * xs_ref[...]) * (z * _sigmoid(z))
        for g in range(SSM_GROUPS):
            yg = y2[:, g * gw:(g + 1) * gw]
            r = lax.rsqrt(jnp.mean(yg * yg, axis=-1, keepdims=True) + EPS)
            o_ref[:, g * gw:(g + 1) * gw] = (yg * r * gm_ref[:, g * gw:(g + 1) * gw]).astype(o_ref.dtype)

    return _call(body, name, (n // tile,), [_rs(tile, d), _rs(tile, d), _rs(tile, d, 2), _ps((1, d)), _ps((1, d))],
                 _rs(tile, d), S((n, d), MXU_DTYPE), sem=("parallel",))(yssd, xs, proj, dfull, gamma)


def ssm_gate_bwd(dy3, yssd, xs, proj, dfull, gamma, dproj, name):
    n, d = yssd.shape
    tile = _row_tile(n, 256)
    gw = d // SSM_GROUPS

    def body(dy_ref, y_ref, xs_ref, z_ref, df_ref, gm_ref, dp_in_ref, dys_ref, dxs_ref, dz_ref, dgm_ref, dd_ref):
        i = pl.program_id(0)
        z = z_ref[...]
        s = _sigmoid(z)
        xs_ = xs_ref[...]
        y1 = y_ref[...] + df_ref[...] * xs_
        y2 = y1 * (z * s)
        dy_ = dy_ref[...].astype(F32)
        dgm = []
        dy2 = []
        for g in range(SSM_GROUPS):
            sl = slice(g * gw, (g + 1) * gw)
            dxg, dgg = _rms_bwd_math(y2[:, sl], gm_ref[:, sl], dy_[:, sl])
            dy2.append(dxg)
            dgm.append(dgg)
        dy2 = jnp.concatenate(dy2, axis=1)
        dy1 = dy2 * (z * s)
        dys_ref[...] = dy1
        dxs_ref[...] = dy1 * df_ref[...]
        dz_ref[...] = (dy2 * y1 * s * (1.0 + z * (1.0 - s))).astype(dz_ref.dtype)
        _acc(dgm_ref, i, jnp.concatenate(dgm, axis=1))
        colsum = jnp.broadcast_to(jnp.sum(dy1 * xs_, axis=0, keepdims=True), (8, d))
        _acc(dd_ref, i, _mm01(colsum, _head_indicator(d, SSM_HEADS))[0:1, :])

    return pl.pallas_call(
        body, name=name, grid=(n // tile,),
        in_specs=[_rs(tile, d), _rs(tile, d), _rs(tile, d), _rs(tile, d, 2), _ps((1, d)), _ps((1, d)), ANY],
        out_specs=[_rs(tile, d), _rs(tile, d), _rs(tile, d, 2), _ps((1, d)), _ps((1, SSM_HEADS))],
        out_shape=[S((n, d), F32), S((n, d), F32), S(dproj.shape, dproj.dtype), S((1, d), F32), S((1, SSM_HEADS), F32)],
        input_output_aliases={6: 2},
        compiler_params=pltpu.CompilerParams(dimension_semantics=("arbitrary",), vmem_limit_bytes=VMEM_LIMIT),
    )(dy3, yssd, xs, proj, dfull, gamma, dproj)


def _rope128(x, cos, sin_signed):
    half = HEAD_DIM // 2
    lane = _iota(x.shape, 1)
    partner = jnp.where((lane % HEAD_DIM) < half, pltpu.roll(x, LANE - half, 1), pltpu.roll(x, half, 1))
    return x * cos + partner * sin_signed


def rope_fwd(qkv, cos, sin, name):
    n, w = qkv.shape
    qw = ATT_HEADS * HEAD_DIM
    kw = ATT_KV_HEADS * HEAD_DIM
    tile = _row_tile(n, 256)

    def body(x_ref, c_ref, s_ref, q_ref, k_ref, v_ref):
        c, s = c_ref[...], s_ref[...]
        for j in range(qw // LANE):
            q_ref[:, j * LANE:(j + 1) * LANE] = _rope128(x_ref[:, j * LANE:(j + 1) * LANE], c, s).astype(q_ref.dtype)
        for j in range(kw // LANE):
            k_ref[:, j * LANE:(j + 1) * LANE] = _rope128(x_ref[:, qw + j * LANE:qw + (j + 1) * LANE], c, s).astype(k_ref.dtype)
        v_ref[...] = x_ref[:, qw + kw:].astype(v_ref.dtype)

    return _call(body, name, (n // tile,), [_rs(tile, w), _rs(tile, LANE), _rs(tile, LANE)],
                 [_rs(tile, qw), _rs(tile, kw), _rs(tile, kw)],
                 [S((n, qw), MXU_DTYPE), S((n, kw), MXU_DTYPE), S((n, kw), MXU_DTYPE)], sem=("parallel",))(qkv, cos, sin)


ATT_GROUP = ATT_HEADS // ATT_KV_HEADS


def _attn_mask(i):
    row = _iota((ATT_GROUP * WINDOW, 2 * WINDOW), 0) % WINDOW
    s = _iota((ATT_GROUP * WINDOW, 2 * WINDOW), 1)
    return (s > row) & (s <= row + WINDOW) & ((s >= WINDOW) | (i > 0))


def _stack_heads(ref, j, kh, lo):
    parts = []
    for t in range(ATT_GROUP):
        h = ATT_GROUP * j + t
        blk = ref[:, (h // 2) * LANE:(h // 2 + 1) * LANE]
        blk = jnp.where(lo if h % 2 == 0 else jnp.logical_not(lo), blk, jnp.zeros_like(blk))
        parts.append(blk if h % 2 == kh else pltpu.roll(blk, HEAD_DIM, 1))
    return jnp.concatenate(parts, axis=0)


def _unstack_heads(stacked, j, kh, lo, put):
    for t in range(0, ATT_GROUP, 2):
        h = ATT_GROUP * j + t
        even = stacked[t * WINDOW:(t + 1) * WINDOW, :]
        odd = stacked[(t + 1) * WINDOW:(t + 2) * WINDOW, :]
        even = even if kh == 0 else pltpu.roll(even, HEAD_DIM, 1)
        odd = odd if kh == 1 else pltpu.roll(odd, HEAD_DIM, 1)
        put(h // 2, jnp.where(lo, even, odd))


def _per_head_rows(ref, j):
    return jnp.concatenate([ref[:, ATT_GROUP * j + t:ATT_GROUP * j + t + 1] for t in range(ATT_GROUP)], axis=0)


def _per_head_scalar(ref, j):
    rows = _iota((ATT_GROUP * WINDOW, 1), 0) // WINDOW
    out = jnp.zeros((ATT_GROUP * WINDOW, 1), F32)
    for t in range(ATT_GROUP):
        out = out + jnp.where(rows == t, ref[:, ATT_GROUP * j + t:ATT_GROUP * j + t + 1], 0.0)
    return out


def attn_fwd(q, k, v, sinks, name):
    n, qw = q.shape
    kw = k.shape[1]
    nb = n // WINDOW
    scale = HEAD_DIM ** -0.5

    def body(q_ref, kc_ref, kp_ref, vc_ref, vp_ref, sk_ref, o_ref, lse_ref):
        i = pl.program_id(0)
        valid = _attn_mask(i)
        lo = _iota((WINDOW, LANE), 1) < HEAD_DIM
        k2 = jnp.concatenate([kp_ref[...], kc_ref[...]], axis=0)
        v2 = jnp.concatenate([vp_ref[...], vc_ref[...]], axis=0)
        lane1 = _iota((1, LANE), 1)
        lse = jnp.zeros((WINDOW, LANE), F32)

        def put_o(qb, val):
            o_ref[:, qb * LANE:(qb + 1) * LANE] = val.astype(o_ref.dtype)

        for j in range(ATT_KV_HEADS):
            kb, kh = j // 2, j % 2
            q4 = _stack_heads(q_ref, j, kh, lo)
            logits = jnp.where(valid, _mm(q4, k2[:, kb * LANE:(kb + 1) * LANE], NT) * scale, -1e30)
            sk = _per_head_scalar(sk_ref, j)
            m = jnp.maximum(jnp.max(logits, axis=-1, keepdims=True), sk)
            e = jnp.exp(logits - m)
            den = jnp.sum(e, axis=-1, keepdims=True) + jnp.exp(sk - m)
            lse4 = m + jnp.log(den)
            for t in range(ATT_GROUP):
                lse = lse + lse4[t * WINDOW:(t + 1) * WINDOW, :] * (lane1 == ATT_GROUP * j + t).astype(F32)
            _unstack_heads(_mm(e * (1.0 / den), v2[:, kb * LANE:(kb + 1) * LANE]), j, kh, lo, put_o)
        lse_ref[...] = lse

    return _call(body, name, (nb,),
                 [_rs(WINDOW, qw), _rs(WINDOW, kw), _rs(WINDOW, kw, 0, -1), _rs(WINDOW, kw), _rs(WINDOW, kw, 0, -1), _ps((1, LANE))],
                 [_rs(WINDOW, qw), _rs(WINDOW, LANE)], [S((n, qw), MXU_DTYPE), S((n, LANE), F32)],
                 sem=("parallel",))(q, k, k, v, v, sinks)


def attn_bwd(q, k, v, o, do, lse, sinks, name):
    n, qw = q.shape
    kw = k.shape[1]
    nb = n // WINDOW
    scale = HEAD_DIM ** -0.5

    def body(q_ref, kc_ref, kp_ref, vc_ref, vp_ref, o_ref, do_ref, lse_ref, sk_ref,
             dq_ref, dka_ref, dkb_ref, dva_ref, dvb_ref, dsk_ref):
        i = pl.program_id(0)
        valid = _attn_mask(i)
        lo = _iota((WINDOW, LANE), 1) < HEAD_DIM
        k2 = jnp.concatenate([kp_ref[...], kc_ref[...]], axis=0)
        v2 = jnp.concatenate([vp_ref[...], vc_ref[...]], axis=0)
        lane1 = _iota((1, LANE), 1)
        do_ = do_ref[...].astype(F32)
        delta = _mm01(do_ * o_ref[...].astype(F32), _head_indicator(qw, ATT_HEADS))
        dk2 = [jnp.zeros((2 * WINDOW, LANE), F32) for _ in range(kw // LANE)]
        dv2 = [jnp.zeros((2 * WINDOW, LANE), F32) for _ in range(kw // LANE)]
        dsk = jnp.zeros((1, LANE), F32)

        def put_dq(qb, val):
            dq_ref[:, qb * LANE:(qb + 1) * LANE] = val

        for j in range(ATT_KV_HEADS):
            kb, kh = j // 2, j % 2
            q4 = _stack_heads(q_ref, j, kh, lo)
            do4 = _stack_heads(do_ref, j, kh, lo)
            kk = k2[:, kb * LANE:(kb + 1) * LANE]
            vv = v2[:, kb * LANE:(kb + 1) * LANE]
            logits = jnp.where(valid, _mm(q4, kk, NT) * scale, -1e30)
            lse4 = _per_head_rows(lse_ref, j)
            p = jnp.exp(logits - lse4)
            dl = jnp.concatenate([delta[:, ATT_GROUP * j + t:ATT_GROUP * j + t + 1] for t in range(ATT_GROUP)], axis=0)
            ds = p * (_mm(do4, vv, NT) - dl) * scale
            sd = jnp.exp(_per_head_scalar(sk_ref, j) - lse4) * dl
            for t in range(ATT_GROUP):
                dsk = dsk - (jnp.sum(sd[t * WINDOW:(t + 1) * WINDOW, :], axis=0, keepdims=True)
                             * (lane1 == ATT_GROUP * j + t).astype(F32))
            _unstack_heads(_mm(ds, kk), j, kh, lo, put_dq)
            dk2[kb] = dk2[kb] + _mm(ds, q4, TN)
            dv2[kb] = dv2[kb] + _mm(p, do4, TN)
        for kb in range(kw // LANE):
            dkb_ref[:, kb * LANE:(kb + 1) * LANE] = dk2[kb][0:WINDOW, :]
            dka_ref[:, kb * LANE:(kb + 1) * LANE] = dk2[kb][WINDOW:, :]
            dvb_ref[:, kb * LANE:(kb + 1) * LANE] = dv2[kb][0:WINDOW, :]
            dva_ref[:, kb * LANE:(kb + 1) * LANE] = dv2[kb][WINDOW:, :]
        _acc(dsk_ref, i, dsk)

    return _call(body, name, (nb,),
                 [_rs(WINDOW, qw), _rs(WINDOW, kw), _rs(WINDOW, kw, 0, -1), _rs(WINDOW, kw), _rs(WINDOW, kw, 0, -1),
                  _rs(WINDOW, qw), _rs(WINDOW, qw), _rs(WINDOW, LANE), _ps((1, LANE))],
                 [_rs(WINDOW, qw)] + [_rs(WINDOW, kw)] * 4 + [_ps((1, LANE))],
                 [S((n, qw), F32)] + [S((n, kw), F32)] * 4 + [S((1, LANE), F32)],
                 sem=("arbitrary",))(q, k, k, v, v, o, do, lse, sinks)


def attn_grad_merge(dq, dka, dkb, dva, dvb, cos, sin, name):
    n, qw = dq.shape
    kw = dka.shape[1]
    nb = n // WINDOW
    w = qw + 2 * kw

    def body(dq_ref, dka_ref, dkb_ref, dva_ref, dvb_ref, c_ref, s_ref, o_ref, db_ref):
        i = pl.program_id(0)
        c, s = c_ref[...], -s_ref[...]
        nxt = (i < nb - 1).astype(F32)

        @pl.when(i == 0)
        def _():
            db_ref[...] = jnp.zeros_like(db_ref)

        def put(c0, val):
            o_ref[:, c0:c0 + val.shape[1]] = val.astype(o_ref.dtype)
            db_ref[:, c0:c0 + val.shape[1]] += jnp.sum(val, axis=0, keepdims=True)

        for j in range(qw // LANE):
            put(j * LANE, _rope128(dq_ref[:, j * LANE:(j + 1) * LANE], c, s))
        for j in range(kw // LANE):
            sl = slice(j * LANE, (j + 1) * LANE)
            put(qw + j * LANE, _rope128(dka_ref[:, sl] + dkb_ref[:, sl] * nxt, c, s))
        put(qw + kw, dva_ref[...] + dvb_ref[...] * nxt)

    return _call(body, name, (nb,),
                 [_rs(WINDOW, qw), _rs(WINDOW, kw), _rs(WINDOW, kw, 0, 1, nb), _rs(WINDOW, kw), _rs(WINDOW, kw, 0, 1, nb),
                  _rs(WINDOW, LANE), _rs(WINDOW, LANE)],
                 [_rs(WINDOW, w), _ps((1, w))], [S((n, w), MXU_DTYPE), S((1, w), F32)],
                 sem=("arbitrary",))(dq, dka, dkb, dva, dvb, cos, sin)


def _row(v):
    return v.reshape(1, -1)


def _pad_lanes(v, width=LANE):
    return jnp.pad(v.reshape(1, -1), ((0, 0), (0, width - v.size)))


class LayerWeights(dict):
    def __init__(self, small, fetch):
        super().__init__(small)
        self.fetch = fetch

    def need(self, k, after):
        if k not in self:
            self[k] = self.fetch(k, after)
        return self[k]


def ffn_fwd(h, g, w, keys, tag):
    xn, u, hm = swiglu_in(h, _row(g), w.need(keys[0], h), f"{tag}_in")
    return matmul(hm, w.need(keys[1], hm), "nn", f"{tag}_out", scale=0.5, res=h), (h, xn, u, hm)


class GradSink:
    ORDER = ("ffn1_w_out", "ffn2_w_out", "ple_gate_w", "att_w_o", "hyb_w_out", "ffn1_w_in", "ffn2_w_in", "att_w_qkv",
             "ple_proj_w", "hyb_w_in")

    def __init__(self, shard_shapes, bucket_of):
        self.where, rows = {}, {}
        for k in self.ORDER:
            n, r, c = shard_shapes[k]
            for li in range(n):
                layer = li if k in PER_LAYER else 2 * li + (0 if k in EVEN_ONLY else 1)
                rows_b = rows.setdefault(bucket_of(layer, _stage(k)), {})
                off = -(-rows_b.get(c, 0) // r) * r
                rows_b[c] = off + r
                self.where[k, li] = (bucket_of(layer, _stage(k)), c, "r" if SHARD_AXIS[k] == 1 else "c", off, r)
        self.bufs = {b: {c: lax.empty((N_CHIPS, r, c), F32) for c, r in rows_b.items()} for b, rows_b in rows.items()}

    def mm(self, k, li, a, b, name, scale=None, c0=0, paired=False):
        layer, c, kind, off, r = self.where[k, li]
        slot = Slot(self.bufs[layer][c], kind, r if kind == "r" else c, off, c0, paired)
        self.bufs[layer][c] = matmul(a, b, "tn", name, scale=scale, into=slot)

    def put(self, k, li, chip_major):
        layer, c = self.where[k, li][:2]
        self.bufs[layer][c] = chip_major

    def take(self, k, n, reduced):
        parts = []
        for li in range(n):
            layer, c, _, off, r = self.where[k, li]
            parts.append(reduced[layer][c][off:off + r])
        return jnp.stack(parts)


def ffn_bwd(dh, g, w_in, w_out, saved, tag, sink, keys, layer, after, colsum=False):
    h, xn, u, hm = saved
    sink.mm(keys[1], layer, hm, dh, f"{tag}_dwout", scale=0.5)
    du = swiglu_out_bwd(dh, w_out, u, after, f"{tag}_dhm")
    sink.mm(keys[0], layer, xn, du, f"{tag}_dwin", paired=True)
    dxn = matmul(du, ColSharded(w_in.arr, paired=True), "nt", f"{tag}_dxn")
    outs = rms_bwd(h, _row(g), dxn, dh, f"{tag}_drms", colsum=colsum)
    return (outs[0], outs[1].reshape(-1)) + ((outs[2],) if colsum else ())


def _hyb_params(w):
    d = w["conv_dw_b"].size
    inner = SSM_HEADS * HEAD_DIM
    main = 3 * d + w["ssm_conv_b"].size
    return dict(
        w_main=w["hyb_w_in"][:, :main], w_dt=jnp.pad(w["hyb_w_in"][:, main:], ((0, 0), (0, LANE - SSM_HEADS))),
        cw=jnp.pad(w["conv_dw_w"], ((0, 32 - CONV_WIDTH), (0, 0))), cb=_row(w["conv_dw_b"]),
        lg=_row(w["conv_ln_g"]), lb=_row(w["conv_ln_b"]),
        sw=jnp.pad(w["ssm_conv_w"], ((0, 8 - SSM_CONV), (0, 0))), sb=_row(w["ssm_conv_b"]),
        dtb=_pad_lanes(w["ssm_dt_bias"]), al_row=_pad_lanes(w["ssm_a_log"]), al_col=w["ssm_a_log"].reshape(-1, 1),
        dfull=_row(jnp.repeat(w["ssm_d"], HEAD_DIM)), gamma=_row(w["ssm_norm"]), d=d, inner=inner, main=main)


def hyb_fwd(h, w, tag):
    w.need("hyb_w_in", h)
    q = _hyb_params(w)
    xn = rms_fwd(h, _row(w["norm_mix"]), f"{tag}_rms")
    proj = matmul(xn, q["w_main"], "nn", f"{tag}_in")
    dtr = matmul(xn, q["w_dt"], "nn", f"{tag}_in_dt")
    u, u1 = conv_group_fwd(proj, q["cw"], q["cb"], q["lg"], q["lb"], f"{tag}_conv")
    pre, xs, bc, dt = ssm_conv_fwd(proj, dtr, q["sw"], q["sb"], q["dtb"], f"{tag}_sconv")
    dtT = dt[:, :SSM_HEADS].T
    yssd, hs = ssd_fwd(xs, bc, dt, dtT, q["al_row"], q["al_col"], f"{tag}_ssd")
    y = ssm_gate_fwd(yssd, xs, proj, q["dfull"], q["gamma"], f"{tag}_gate")
    wo = w.need("hyb_w_out", u)
    h2 = matmul(u, wo[:q["d"]], "nn", f"{tag}_out_a", res=h)
    h2 = matmul(y, wo[q["d"]:], "nn", f"{tag}_out_b", res=h2)
    return h2, (h, xn, proj, u, u1, pre, xs, bc, dt, dtT, yssd, hs, y)


def hyb_bwd(dh, w, saved, tag, sink, layer):
    q = _hyb_params(w)
    h, xn, proj, u, u1, pre, xs, bc, dt, dtT, yssd, hs, y = saved
    du = matmul(dh, w["hyb_w_out"][:q["d"]], "nt", f"{tag}_du")
    dy3 = matmul(dh, w["hyb_w_out"][q["d"]:], "nt", f"{tag}_dy")
    sink.mm("hyb_w_out", layer, u, dh, f"{tag}_dwo_a", c0=0)
    sink.mm("hyb_w_out", layer, y, dh, f"{tag}_dwo_b", c0=N_CHIPS // 2)
    dproj, dcw, dcb, dlg, dlb = conv_group_bwd(du, u1, proj, q["cw"], q["lg"], q["lb"], f"{tag}_dconv")
    dyssd, dxs_skip, dproj, dgamma, dd = ssm_gate_bwd(dy3, yssd, xs, proj, q["dfull"], q["gamma"], dproj, f"{tag}_dgate")
    dxs, dbc, ddtr, dalog, ddtb = ssd_bwd(xs, bc, dt, dtT, q["al_row"], q["al_col"], hs, dyssd, dxs_skip, f"{tag}_dssd")
    dproj, dsw, dsb = ssm_conv_bwd(dxs, dbc, pre, proj, q["sw"], dproj, f"{tag}_dsconv")
    dw_in = jnp.concatenate([matmul(xn, dproj, "tn", f"{tag}_dwin"),
                             matmul(xn, ddtr, "tn", f"{tag}_dwin_dt")[:, :SSM_HEADS]], axis=1)
    sink.put("hyb_w_in", layer, _to_chip_major(dw_in, 1))
    dxn = matmul(dproj, q["w_main"], "nt", f"{tag}_dxn")
    dxn = matmul(ddtr, q["w_dt"], "nt", f"{tag}_dxn_dt", res=dxn)
    dh2, dg = rms_bwd(h, _row(w["norm_mix"]), dxn, dh, f"{tag}_drms")
    grads = dict(norm_mix=dg.reshape(-1), conv_dw_w=dcw[:CONV_WIDTH], conv_dw_b=dcb.reshape(-1),
                 conv_ln_g=dlg.reshape(-1), conv_ln_b=dlb.reshape(-1), ssm_conv_w=dsw[:SSM_CONV], ssm_conv_b=dsb.reshape(-1),
                 ssm_dt_bias=ddtb.reshape(-1), ssm_a_log=dalog.reshape(-1), ssm_d=dd.reshape(-1), ssm_norm=dgamma.reshape(-1))
    return dh2, grads


def rope_tables(n):
    half = HEAD_DIM // 2
    inv = ROPE_THETA ** (-jnp.arange(0, HEAD_DIM, 2, dtype=F32) / HEAD_DIM)
    ang = jnp.arange(n, dtype=F32)[:, None] * inv[None, :]
    cos, sin = jnp.cos(ang), jnp.sin(ang)
    reps = LANE // HEAD_DIM
    return jnp.tile(jnp.concatenate([cos, cos], axis=1), (1, reps)), jnp.tile(jnp.concatenate([-sin, sin], axis=1), (1, reps))


def att_fwd(h, w, tables, tag):
    cos, sin = tables
    xn = rms_fwd(h, _row(w["norm_mix"]), f"{tag}_rms")
    qkv = matmul(xn, w.need("att_w_qkv", h), "nn", f"{tag}_qkv", bias=_row(w["att_b_qkv"]))
    q, k, v = rope_fwd(qkv, cos, sin, f"{tag}_rope")
    sinks = _pad_lanes(w["att_sinks"])
    o, lse = attn_fwd(q, k, v, sinks, f"{tag}_attn")
    h2 = matmul(o, w.need("att_w_o", o), "nn", f"{tag}_o", bias=_row(w["att_b_o"]), res=h)
    return h2, (h, xn, q, k, v, o, lse, sinks)


def att_bwd(dh, dh_colsum, w, saved, tables, tag, sink, layer):
    cos, sin = tables
    h, xn, q, k, v, o, lse, sinks = saved
    do = matmul(dh, w["att_w_o"], "nt", f"{tag}_do")
    sink.mm("att_w_o", layer, o, dh, f"{tag}_dwo")
    dq, dka, dkb, dva, dvb, dsk = attn_bwd(q, k, v, o, do, lse, sinks, f"{tag}_dattn")
    dqkv, dbqkv = attn_grad_merge(dq, dka, dkb, dva, dvb, cos, sin, f"{tag}_drope")
    sink.mm("att_w_qkv", layer, xn, dqkv, f"{tag}_dwqkv")
    dxn = matmul(dqkv, w["att_w_qkv"], "nt", f"{tag}_dxn")
    dh2, dg = rms_bwd(h, _row(w["norm_mix"]), dxn, dh, f"{tag}_drms")
    grads = dict(norm_mix=dg.reshape(-1), att_b_qkv=dbqkv.reshape(-1), att_sinks=dsk[0, :ATT_HEADS],
                 att_b_o=dh_colsum.reshape(-1))
    return dh2, grads


def ple_block_fwd(h, pe, w, tag):
    xn = rms_fwd(h, _row(w["ple_norm"]), f"{tag}_rms")
    gl = matmul(xn, w.need("ple_gate_w", h), "nn", f"{tag}_gate")
    pp = matmul(pe, w.need("ple_proj_w", h), "nn", f"{tag}_proj")
    return ple_fwd(h, gl, pp, f"{tag}_mix"), (h, xn, gl, pp, pe)


def ple_block_bwd(dh, w, saved, tag, sink, layer, after):
    h, xn, gl, pp, pe = saved
    dpp, dgl = ple_bwd(dh, gl, pp, after, f"{tag}_dmix")
    sink.mm("ple_proj_w", layer, pe, dpp, f"{tag}_dwp")
    sink.mm("ple_gate_w", layer, xn, dgl, f"{tag}_dwg")
    dxn = matmul(dgl, w["ple_gate_w"], "nt", f"{tag}_dxn")
    dh2, dg = rms_bwd(h, _row(w["ple_norm"]), dxn, dh, f"{tag}_drms")
    return dh2, dict(ple_norm=dg.reshape(-1))


PER_LAYER = ("norm_ffn1", "ffn1_w_in", "ffn1_w_out", "norm_mix", "norm_ffn2", "ffn2_w_in", "ffn2_w_out",
             "ple_norm", "ple_gate_w", "ple_proj_w")
EVEN_ONLY = ("hyb_w_in", "conv_dw_w", "conv_dw_b", "conv_ln_g", "conv_ln_b", "ssm_conv_w", "ssm_conv_b",
             "ssm_dt_bias", "ssm_a_log", "ssm_d", "ssm_norm", "hyb_w_out")
ODD_ONLY = ("att_w_qkv", "att_b_qkv", "att_sinks", "att_w_o", "att_b_o")


def _layer_index(k, i):
    if k in PER_LAYER:
        return i
    if k in (EVEN_ONLY if i % 2 == 0 else ODD_ONLY):
        return i // 2
    return None


def _stage(k):
    return 0 if k.startswith("ffn1") else (2 if k.startswith(("ffn2", "ple")) else 1)


def trunk_fwd_bwd(x, pe, target, layers, final_norm, sink, stage_done):
    depth = len(layers)
    tables = rope_tables(x.shape[0])
    h = x
    saved = []
    for i, w in enumerate(layers):
        h, s1 = ffn_fwd(h, w["norm_ffn1"], w, ("ffn1_w_in", "ffn1_w_out"), f"l{i}_ffn1")
        if i % 2 == 0:
            h, s2 = hyb_fwd(h, w, f"l{i}_hyb")
        else:
            h, s2 = att_fwd(h, w, tables, f"l{i}_att")
        h, s3 = ffn_fwd(h, w["norm_ffn2"], w, ("ffn2_w_in", "ffn2_w_out"), f"l{i}_ffn2")
        h, s4 = ple_block_fwd(h, pe[i], w, f"l{i}_ple")
        saved.append((s1, s2, s3, s4))
    dh, dgf, loss = loss_head(h, _row(final_norm), target, "loss_head")
    grads = {}
    tie = dgf
    for i in reversed(range(depth)):
        w = layers[i]
        s1, s2, s3, s4 = saved[i]
        dh, g = ple_block_bwd(dh, w, s4, f"l{i}_ple", sink, i, tie)
        odd = i % 2 == 1
        out = ffn_bwd(dh, w["norm_ffn2"], w["ffn2_w_in"], w["ffn2_w_out"], s3, f"l{i}_ffn2", sink,
                      ("ffn2_w_in", "ffn2_w_out"), i, tie, colsum=odd)
        dh = out[0]
        g.update(norm_ffn2=out[1])
        if odd:
            dh, gm = att_bwd(dh, out[2], w, s2, tables, f"l{i}_att", sink, i // 2)
        else:
            dh, gm = hyb_bwd(dh, w, s2, f"l{i}_hyb", sink, i // 2)
        g.update(gm)
        tie = stage_done(i, 1, tie)
        out = ffn_bwd(dh, w["norm_ffn1"], w["ffn1_w_in"], w["ffn1_w_out"], s1, f"l{i}_ffn1", sink,
                      ("ffn1_w_in", "ffn1_w_out"), i, tie)
        dh = out[0]
        g.update(norm_ffn1=out[1])
        tie = stage_done(i, 0, tie)
        for k, v in g.items():
            grads.setdefault(k, []).insert(0, v)
    grads = {k: jnp.stack(v) for k, v in grads.items()}
    grads["final_norm"] = dgf.reshape(-1)
    return loss, dh, grads


def _me():
    return lax.axis_index("x"), lax.axis_index("y"), lax.axis_index("c")


def _flip(v, f):
    return 1 - v if f else v


def _remote(src, dst, send_sems, recv_sems, k, dev):
    return pltpu.make_async_remote_copy(src_ref=src, dst_ref=dst, send_sem=send_sems.at[k], recv_sem=recv_sems.at[k],
                                        device_id=dev, device_id_type=MESH)


CHIP_FLIPS = ((1, 0), (0, 1), (1, 1))
DEV_FLIPS = tuple((fx, fy, fc) for fx in (0, 1) for fy in (0, 1) for fc in (0, 1))[1:]


def all_gather_chips(xs, name):
    na = len(xs)
    halves = [x.shape[0] // 2 for x in xs]
    assert all(x.shape[0] % 2 == 0 for x in xs)

    def body(*refs):
        x_refs, out_refs = refs[:na], refs[na:2 * na]
        send_sems, recv_sems = refs[2 * na:]
        mx, my, mc = _me()
        chip = 2 * mx + my
        sib = (mx, my, 1 - mc)
        peers = [(_flip(mx, fx), _flip(my, fy)) for fx, fy in CHIP_FLIPS]

        def rows(a, ch, hc):
            return out_refs[a].at[ch, pl.ds(hc * halves[a], halves[a]), :]

        def src(a):
            return x_refs[a].at[pl.ds(mc * halves[a], halves[a]), :]

        first = [_remote(src(a), rows(a, chip, mc), send_sems, recv_sems, 6 * a + j, (px, py, mc))
                 for j, (px, py) in enumerate(peers) for a in range(na)]
        for cp in first:
            cp.start()
        passed = []
        for j, (px, py) in enumerate(peers):
            for a in range(na):
                landed = rows(a, 2 * px + py, mc)
                _remote(src(a), landed, send_sems, recv_sems, 6 * a + j, (px, py, mc)).wait_recv()
                fw = _remote(landed, landed, send_sems, recv_sems, 6 * a + 3 + j, sib)
                fw.start()
                passed.append(fw)
        for j, (px, py) in enumerate(peers):
            for a in range(na):
                _remote(src(a), rows(a, 2 * px + py, 1 - mc), send_sems, recv_sems, 6 * a + 3 + j, sib).wait_recv()
        for cp in first + passed:
            cp.wait_send()

    outs = pl.pallas_call(
        body, name=name, out_shape=[S((N_CHIPS,) + x.shape, x.dtype) for x in xs], in_specs=[ANY] * na, out_specs=[ANY] * na,
        scratch_shapes=[pltpu.SemaphoreType.DMA((6 * na,)), pltpu.SemaphoreType.DMA((6 * na,))])(*xs)
    chip = 2 * lax.axis_index("x") + lax.axis_index("y")
    return [lax.dynamic_update_slice_in_dim(o, x[None], chip, axis=0) for o, x in zip(outs, xs)]


HBM = pl.BlockSpec(memory_space=pltpu.HBM)
SEM = pl.BlockSpec(memory_space=pltpu.SEMAPHORE)
DATAFLOW = pltpu.SideEffectType.DATAFLOW_SIDE_EFFECTING


def gather_start(xs, lands, after, name):
    na = len(xs)

    def body(*refs):
        x_refs, land_refs = refs[:na], refs[na:2 * na]
        send_sems, recv_sems = refs[2 * na + 1], refs[2 * na + 2]
        token = refs[-1]
        mx, my, mc = _me()
        chip = 2 * mx + my
        for a in range(na):
            for j, (fx, fy) in enumerate(CHIP_FLIPS):
                _remote(x_refs[a], land_refs[a].at[chip], send_sems, recv_sems, 3 * a + j,
                        (_flip(mx, fx), _flip(my, fy), mc)).start()
        token[...] = jnp.zeros_like(token)

    outs = pl.pallas_call(
        body, name=name,
        out_shape=(pltpu.SemaphoreType.DMA((3 * na,)), pltpu.SemaphoreType.DMA((3 * na,)))
        + tuple(pltpu.HBM(x.shape, x.dtype) for x in xs) + tuple(pltpu.HBM(l.shape, l.dtype) for l in lands)
        + (S((8, LANE), F32),),
        in_specs=[HBM] * (2 * na) + [pl.BlockSpec(memory_space=pl.ANY)],
        out_specs=(SEM, SEM) + (HBM,) * (2 * na) + (pl.BlockSpec(memory_space=pltpu.VMEM),),
        input_output_aliases={a: 2 + a for a in range(2 * na)},
        compiler_params=pltpu.CompilerParams(has_side_effects=DATAFLOW),
    )(*[pltpu.with_memory_space_constraint(t, pltpu.HBM) for t in list(xs) + list(lands)], after)
    return outs[0], outs[1], list(outs[2:2 + na]), list(outs[2 + na:2 + 2 * na])


def gather_wait(send_sems, recv_sems, xs, lands, first, after, name):
    na = len(xs)

    def body(*refs):
        x_refs, land_refs = refs[:na], refs[na:2 * na]
        send_sems, recv_sems = refs[2 * na], refs[2 * na + 1]
        mx, my, mc = _me()
        for a in range(na):
            for j, (fx, fy) in enumerate(CHIP_FLIPS):
                px, py = _flip(mx, fx), _flip(my, fy)
                cp = _remote(x_refs[a], land_refs[a].at[2 * px + py], send_sems, recv_sems, 3 * (first + a) + j, (px, py, mc))
                cp.wait_send()
                cp.wait_recv()

    outs = pl.pallas_call(
        body, name=name,
        out_shape=tuple(pltpu.HBM(x.shape, x.dtype) for x in xs) + tuple(pltpu.HBM(l.shape, l.dtype) for l in lands),
        in_specs=[HBM] * (2 * na) + [SEM, SEM, pl.BlockSpec(memory_space=pl.ANY)], out_specs=(HBM,) * (2 * na),
        input_output_aliases={a: a for a in range(2 * na)},
        compiler_params=pltpu.CompilerParams(has_side_effects=DATAFLOW),
    )(*xs, *lands, send_sems, recv_sems, after)
    return list(outs[na:])


def all_gather_devices(v, name):
    r, l = v.shape

    def body(v_ref, out_ref, send_sems, recv_sems):
        mx, my, mc = _me()
        me = 4 * mx + 2 * my + mc
        peers = [(_flip(mx, fx), _flip(my, fy), _flip(mc, fc)) for fx, fy, fc in DEV_FLIPS]
        sends = [_remote(v_ref, out_ref.at[me], send_sems, recv_sems, j, p) for j, p in enumerate(peers)]
        for cp in sends:
            cp.start()
        for j, (px, py, pc) in enumerate(peers):
            _remote(v_ref, out_ref.at[4 * px + 2 * py + pc], send_sems, recv_sems, j, (px, py, pc)).wait_recv()
        for cp in sends:
            cp.wait_send()

    out = pl.pallas_call(
        body, name=name, out_shape=S((N_DEV, r, l), v.dtype), in_specs=[ANY], out_specs=ANY,
        scratch_shapes=[pltpu.SemaphoreType.DMA((7,)), pltpu.SemaphoreType.DMA((7,))])(v)
    me = 4 * lax.axis_index("x") + 2 * lax.axis_index("y") + lax.axis_index("c")
    return lax.dynamic_update_slice_in_dim(out, v[None], me, axis=0)


def sum_devices(g8, name):
    nd, r, l = g8.shape
    tile = r
    for t in (512, 256, 128, 64, 32, 16, 8):
        if r % t == 0:
            tile = t
            break

    def body(g_ref, o_ref):
        acc = g_ref[0]
        for d in range(1, nd):
            acc = acc + g_ref[d]
        o_ref[...] = acc

    return _call(body, name, (r // tile,), [pl.BlockSpec((nd, tile, l), lambda i: (0, i, 0))], _rs(tile, l), S((r, l), F32),
                 sem=("parallel",))(g8)


def exchange_halves(gs, name):
    na = len(gs)
    nch = gs[0].shape[0]

    def body(*refs):
        g_refs, out_refs = refs[:na], refs[na:2 * na]
        send_sems, recv_sems = refs[2 * na:]
        mx, my, mc = _me()
        sib = (mx, my, 1 - mc)
        cps = []
        for a in range(na):
            half = gs[a].shape[1] // 2
            for j in range(nch):
                cps.append(_remote(g_refs[a].at[j, pl.ds((1 - mc) * half, half), :], out_refs[a].at[j],
                                   send_sems, recv_sems, nch * a + j, sib))
        for cp in cps:
            cp.start()
        for cp in cps:
            cp.wait_recv()
        for cp in cps:
            cp.wait_send()

    return pl.pallas_call(
        body, name=name, out_shape=[S((nch, g.shape[1] // 2, g.shape[2]), g.dtype) for g in gs],
        in_specs=[ANY] * na, out_specs=[ANY] * na,
        scratch_shapes=[pltpu.SemaphoreType.DMA((nch * na,)), pltpu.SemaphoreType.DMA((nch * na,))])(*gs)


def add_halves(g4, got, name):
    nch, r, l = g4.shape
    half = r // 2
    tile = _pick_rows(half)
    nt = half // tile

    def body(g_ref, r_ref, a_ref, own_ref):
        j = pl.program_id(1)
        chip = 2 * lax.axis_index("x") + lax.axis_index("y")
        val = g_ref[0] + r_ref[0]
        a_ref[0] = val.astype(a_ref.dtype)

        @pl.when(j == chip)
        def _():
            own_ref[...] = val

    return pl.pallas_call(
        body, name=name, grid=(nt, nch),
        in_specs=[pl.BlockSpec((1, tile, l), lambda i, j: (j, lax.axis_index("c") * nt + i, 0)),
                  pl.BlockSpec((1, tile, l), lambda i, j: (j, i, 0))],
        out_specs=[pl.BlockSpec((1, tile, l), lambda i, j: (j, i, 0)), pl.BlockSpec((tile, l), lambda i, j: (i, 0))],
        out_shape=[S((nch, half, l), MXU_DTYPE), S((half, l), F32)],
        compiler_params=pltpu.CompilerParams(dimension_semantics=("parallel", "arbitrary"), vmem_limit_bytes=VMEM_LIMIT))(g4, got)


def _pick_rows(r, cap=512):
    for t in (512, 256, 128, 64, 32, 16):
        if t <= cap and r % t == 0:
            return t
    return r


def exchange_chips(parts, name):
    na = len(parts)

    def body(*refs):
        a_refs, out_refs = refs[:na], refs[na:2 * na]
        send_sems, recv_sems = refs[2 * na:]
        mx, my, mc = _me()
        peers = [(_flip(mx, fx), _flip(my, fy)) for fx, fy in CHIP_FLIPS]
        cps = [_remote(a_refs[a].at[2 * px + py], out_refs[a].at[j], send_sems, recv_sems, 3 * a + j, (px, py, mc))
               for j, (px, py) in enumerate(peers) for a in range(na)]
        for cp in cps:
            cp.start()
        for cp in cps:
            cp.wait_recv()
        for cp in cps:
            cp.wait_send()

    return pl.pallas_call(
        body, name=name, out_shape=[S((3,) + p.shape[1:], p.dtype) for p in parts], in_specs=[ANY] * na, out_specs=[ANY] * na,
        scratch_shapes=[pltpu.SemaphoreType.DMA((3 * na,)), pltpu.SemaphoreType.DMA((3 * na,))])(*parts)


def add_chips(own, got, name):
    h, l = own.shape
    tile = _pick_rows(h)

    def body(o_ref, g_ref, out_ref):
        out_ref[...] = ((o_ref[...] + g_ref[0].astype(F32)) + g_ref[1].astype(F32)) + g_ref[2].astype(F32)

    nt = h // tile
    return _call(body, name, (nt,), [_rs(tile, l), pl.BlockSpec((3, tile, l), lambda i: (0, i, 0))],
                 pl.BlockSpec((tile, l), lambda i: (lax.axis_index("c") * nt + i, 0)),
                 S((2 * h, l), F32), sem=("parallel",))(own, got)


def join_halves(bufs, name):
    na = len(bufs)

    def body(*refs):
        out_refs = refs[na:2 * na]
        send_sems, recv_sems = refs[2 * na:]
        mx, my, mc = _me()
        sib = (mx, my, 1 - mc)

        def half(a, hc):
            h = bufs[a].shape[0] // 2
            return out_refs[a].at[pl.ds(hc * h, h), :]

        cps = [_remote(half(a, mc), half(a, mc), send_sems, recv_sems, a, sib) for a in range(na)]
        for cp in cps:
            cp.start()
        for a in range(na):
            _remote(half(a, mc), half(a, 1 - mc), send_sems, recv_sems, a, sib).wait_recv()
        for cp in cps:
            cp.wait_send()

    return pl.pallas_call(
        body, name=name, out_shape=[S(b.shape, b.dtype) for b in bufs], in_specs=[ANY] * na, out_specs=[ANY] * na,
        input_output_aliases={a: a for a in range(na)},
        scratch_shapes=[pltpu.SemaphoreType.DMA((na,)), pltpu.SemaphoreType.DMA((na,))])(*bufs)


def exchange_chips_start(parts, name):
    na = len(parts)
    lands = [lax.empty((3,) + p.shape[1:], p.dtype) for p in parts]

    def body(*refs):
        a_refs, land_refs = refs[:na], refs[na:2 * na]
        send_sems, recv_sems = refs[2 * na], refs[2 * na + 1]
        mx, my, mc = _me()
        for j, (fx, fy) in enumerate(CHIP_FLIPS):
            px, py = _flip(mx, fx), _flip(my, fy)
            for a in range(na):
                _remote(a_refs[a].at[2 * px + py], land_refs[a].at[j], send_sems, recv_sems, 3 * a + j, (px, py, mc)).start()
        refs[-1][...] = jnp.zeros_like(refs[-1])

    outs = pl.pallas_call(
        body, name=name,
        out_shape=(pltpu.SemaphoreType.DMA((3 * na,)), pltpu.SemaphoreType.DMA((3 * na,)))
        + tuple(pltpu.HBM(t.shape, t.dtype) for t in list(parts) + lands) + (S((8, LANE), F32),),
        in_specs=[HBM] * (2 * na), out_specs=(SEM, SEM) + (HBM,) * (2 * na) + (pl.BlockSpec(memory_space=pltpu.VMEM),),
        input_output_aliases={a: 2 + a for a in range(2 * na)},
        compiler_params=pltpu.CompilerParams(has_side_effects=DATAFLOW),
    )(*[pltpu.with_memory_space_constraint(t, pltpu.HBM) for t in list(parts) + lands])
    return outs[0], outs[1], list(outs[2:2 + na]), list(outs[2 + na:2 + 2 * na]), outs[-1]


def exchange_chips_wait(send_sems, recv_sems, parts, lands, after, name):
    na = len(parts)

    def body(*refs):
        a_refs, land_refs = refs[:na], refs[na:2 * na]
        send_sems, recv_sems = refs[2 * na], refs[2 * na + 1]
        mx, my, mc = _me()
        for j, (fx, fy) in enumerate(CHIP_FLIPS):
            px, py = _flip(mx, fx), _flip(my, fy)
            for a in range(na):
                cp = _remote(a_refs[a].at[2 * px + py], land_refs[a].at[j], send_sems, recv_sems, 3 * a + j, (px, py, mc))
                cp.wait_send()
                cp.wait_recv()

    outs = pl.pallas_call(
        body, name=name, out_shape=tuple(pltpu.HBM(t.shape, t.dtype) for t in list(parts) + list(lands)),
        in_specs=[HBM] * (2 * na) + [SEM, SEM, pl.BlockSpec(memory_space=pl.ANY)], out_specs=(HBM,) * (2 * na),
        input_output_aliases={a: a for a in range(2 * na)},
        compiler_params=pltpu.CompilerParams(has_side_effects=DATAFLOW),
    )(*parts, *lands, send_sems, recv_sems, after)
    return list(outs[na:])


def reduce_begin(gs, tag):
    got = exchange_halves(gs, f"{tag}_d2d")
    sums = [add_halves(g, r, f"{tag}_add1_{i}") for i, (g, r) in enumerate(zip(gs, got))]
    return [own for _, own in sums], exchange_chips_start([a for a, _ in sums], f"{tag}_ici_start")


def reduce_end(state, after, tag):
    owns, (send_sems, recv_sems, parts, lands, _) = state
    got = exchange_chips_wait(send_sems, recv_sems, parts, lands, after, f"{tag}_ici_wait")
    return [add_chips(own, r, f"{tag}_add2_{i}") for i, (own, r) in enumerate(zip(owns, got))]


PACK_L = 1024
BIG_ROW_MULT = 512


def _pack(arrs, dtype, row_mult, lead=None):
    lead_shape = () if lead is None else arrs[0].shape[:lead]
    flat = jnp.concatenate([a.astype(dtype).reshape(lead_shape + (-1,)) for a in arrs], axis=-1)
    n = flat.shape[-1]
    unit = row_mult * PACK_L
    total = -(-n // unit) * unit
    flat = jnp.pad(flat, [(0, 0)] * len(lead_shape) + [(0, total - n)])
    return flat.reshape(lead_shape + (total // PACK_L, PACK_L))


def _unpack(packed, shapes, lead=None):
    lead_shape = () if lead is None else packed.shape[:lead]
    flat = packed.reshape(lead_shape + (-1,))
    out, off = [], 0
    for shp in shapes:
        n = int(np.prod(shp))
        out.append(flat[..., off:off + n].reshape(lead_shape + tuple(shp)))
        off += n
    return out


def _to_full(gathered, axis):
    t = jnp.moveaxis(gathered, 0, axis)
    shp = t.shape
    return t.reshape(shp[:axis] + (shp[axis] * shp[axis + 1],) + shp[axis + 2:])


def _to_chip_major(full, axis):
    shp = full.shape
    t = full.reshape(shp[:axis] + (N_CHIPS, shp[axis] // N_CHIPS) + shp[axis + 1:])
    return jnp.moveaxis(t, axis, 0)


WEIGHTS = ("norm_ffn1", "ffn1_w_in", "ffn1_w_out", "norm_mix", "norm_ffn2", "ffn2_w_in", "ffn2_w_out", "ple_norm",
           "ple_gate_w", "ple_proj_w", "hyb_w_in", "conv_dw_w", "conv_dw_b", "conv_ln_g", "conv_ln_b", "ssm_conv_w",
           "ssm_conv_b", "ssm_dt_bias", "ssm_a_log", "ssm_d", "ssm_norm", "hyb_w_out", "att_w_qkv", "att_b_qkv",
           "att_sinks", "att_w_o", "att_b_o", "final_norm")
SHARD_AXIS = dict(ffn1_w_in=2, ffn1_w_out=1, ffn2_w_in=2, ffn2_w_out=1, ple_gate_w=1, ple_proj_w=2, hyb_w_in=2,
                  conv_dw_w=2, ssm_conv_w=2, hyb_w_out=1, att_w_qkv=2, att_b_qkv=1, att_w_o=1, att_b_o=1)
BIG = ("ffn1_w_in", "ffn1_w_out", "ffn2_w_in", "ffn2_w_out", "ple_gate_w", "ple_proj_w", "hyb_w_in", "hyb_w_out",
       "att_w_qkv", "att_w_o")
ODD_WIDTH = "hyb_w_in"
BIG_FLAT = tuple(k for k in BIG if k != ODD_WIDTH)
SMALL_SHARDED = ("conv_dw_w", "ssm_conv_w", "att_b_qkv", "att_b_o")
SMALL = tuple(k for k in WEIGHTS if k not in BIG)


def _step(x, p, target, w, m, v):
    mx, my = lax.axis_index("x"), lax.axis_index("y")
    chip = 2 * mx + my

    depth = w["norm_ffn1"].shape[0]
    order = sorted([(k, i) for i in range(depth) for k in BIG if _layer_index(k, i) is not None],
                   key=lambda t: (t[1], _stage(t[0])))
    small_g = all_gather_devices(_pack([w[k] for k in SMALL_SHARDED], F32, 8), "gather_small")
    shards = [w[k][_layer_index(k, i)].astype(MXU_DTYPE) for k, i in order]
    lands = [lax.dynamic_update_slice_in_dim(lax.empty((N_CHIPS,) + s.shape, s.dtype), s[None], chip, axis=0) for s in shards]
    send_sems, recv_sems, shards, lands = gather_start(shards, lands, small_g, "gather_start")

    def fetch(i, k, after):
        p = order.index((k, i))
        g, = gather_wait(send_sems, recv_sems, [shards[p]], [lands[p]], p, after, f"gather_wait_l{i}_{k}")
        if k == ODD_WIDTH:
            return _to_full(g, 1)
        if SHARD_AXIS[k] == 2:
            return ColSharded(g)
        return g.reshape(-1, g.shape[-1])

    small_g = small_g[0::2]
    small_full = {k: _to_full(g, SHARD_AXIS[k])
                  for k, g in zip(SMALL_SHARDED, _unpack(small_g, [w[k].shape for k in SMALL_SHARDED], lead=1))}
    layers = [LayerWeights({k: small_full.get(k, w[k])[_layer_index(k, i)] for k in SMALL if _layer_index(k, i) is not None},
                           functools.partial(fetch, i)) for i in range(depth)]

    def bucket_of(layer, stage):
        return (layer, 0) if layer > 0 else (0, min(stage, 1))

    sink = GradSink({k: w[k].shape for k in BIG}, bucket_of)
    begun = {}

    def stage_done(i, stage, tie):
        b = bucket_of(i, stage)
        if stage > 0 and bucket_of(i, stage - 1) == b:
            return tie
        begun[b] = reduce_begin(list(sink.bufs[b].values()), f"grads_l{b[0]}_{b[1]}")
        return begun[b][1][-1]

    loss, dx, grads = trunk_fwd_bwd(x[0], p[:, 0], target[0], layers, w["final_norm"], sink, stage_done)

    results = {}

    def finish(buckets, after, tag):
        halves = {b: reduce_end(begun[b], after, f"grads_l{b[0]}_{b[1]}") for b in buckets}
        joined = iter(join_halves([h for b in buckets for h in halves[b]], f"grads_join_{tag}"))
        reduced = {b: {c: next(joined) for c in sink.bufs[b]} for b in buckets}
        last = after
        for (k, li), (b, c, _, off, _) in sink.where.items():
            if b in buckets:
                results[k] = adamw_layer(w[k], reduced[b][c], off, m[k], v[k], li, results.get(k), f"adamw_{k}_{li}")
                last = results[k][1]
        return last

    order_b = list(begun)
    started_last = begun[order_b[-1]][1][-1]
    finish(order_b[-1:], finish(order_b[:-1], started_last, "early") if len(order_b) > 1 else dx, "last")
    g_out = {k: results[k][0] for k in BIG}
    vec = _pack([loss[0:1, 0:1]] + [grads[k] for k in SMALL], F32, 8)
    vec = sum_devices(all_gather_devices(vec, "gather_vectors"), "sum_vectors")
    parts = _unpack(vec, [(1, 1)] + [grads[k].shape for k in SMALL])
    loss_out = parts[0].reshape(())
    for k, g in zip(SMALL, parts[1:]):
        if k in SHARD_AXIS:
            ax = SHARD_AXIS[k]
            g = lax.dynamic_slice_in_dim(g, chip * w[k].shape[ax], w[k].shape[ax], axis=ax)
        g_out[k] = g

    delta, new_m, new_v = ({k: results[k][j] for k in BIG} for j in (1, 2, 3))
    shapes = [w[k].shape for k in SMALL]
    packed = [_pack([src[k] for k in SMALL], F32, 8) for src in (w, g_out, m, v)]
    outs = adamw(*packed, "adamw_small")
    for dst, o in zip((delta, new_m, new_v), outs):
        for k, a in zip(SMALL, _unpack(o, shapes)):
            dst[k] = a
    return ((loss_out, dx[None]) + tuple(g_out[k] for k in WEIGHTS) + tuple(delta[k] for k in WEIGHTS)
            + tuple(new_m[k] for k in WEIGHTS) + tuple(new_v[k] for k in WEIGHTS))


def kernel(x, p, norm_ffn1, ffn1_w_in, ffn1_w_out, norm_mix, norm_ffn2, ffn2_w_in, ffn2_w_out, ple_norm, ple_gate_w, ple_proj_w, hyb_w_in, conv_dw_w, conv_dw_b, conv_ln_g, conv_ln_b, ssm_conv_w, ssm_conv_b, ssm_dt_bias, ssm_a_log, ssm_d, ssm_norm, hyb_w_out, att_w_qkv, att_b_qkv, att_sinks, att_w_o, att_b_o, final_norm, loss_target, m_norm_ffn1, m_ffn1_w_in, m_ffn1_w_out, m_norm_mix, m_norm_ffn2, m_ffn2_w_in, m_ffn2_w_out, m_ple_norm, m_ple_gate_w, m_ple_proj_w, m_hyb_w_in, m_conv_dw_w, m_conv_dw_b, m_conv_ln_g, m_conv_ln_b, m_ssm_conv_w, m_ssm_conv_b, m_ssm_dt_bias, m_ssm_a_log, m_ssm_d, m_ssm_norm, m_hyb_w_out, m_att_w_qkv, m_att_b_qkv, m_att_sinks, m_att_w_o, m_att_b_o, m_final_norm, v_norm_ffn1, v_ffn1_w_in, v_ffn1_w_out, v_norm_mix, v_norm_ffn2, v_ffn2_w_in, v_ffn2_w_out, v_ple_norm, v_ple_gate_w, v_ple_proj_w, v_hyb_w_in, v_conv_dw_w, v_conv_dw_b, v_conv_ln_g, v_conv_ln_b, v_ssm_conv_w, v_ssm_conv_b, v_ssm_dt_bias, v_ssm_a_log, v_ssm_d, v_ssm_norm, v_hyb_w_out, v_att_w_qkv, v_att_b_qkv, v_att_sinks, v_att_w_o, v_att_b_o, v_final_norm):
    given = locals()
    w = {k: given[k] for k in WEIGHTS}
    m = {k: given["m_" + k] for k in WEIGHTS}
    v = {k: given["v_" + k] for k in WEIGHTS}
    return _step(x, p, loss_target, w, m, v)
```

```python
import functools
import math

import numpy as np
import jax
import jax.numpy as jnp
from jax import lax
from jax.experimental import pallas as pl
from jax.experimental.pallas import tpu as pltpu

F32 = jnp.float32
BF16 = jnp.bfloat16
MXU_DTYPE = jnp.bfloat16
S = jax.ShapeDtypeStruct
MESH = pl.DeviceIdType.MESH

V7X_VMEM_BYTES = 64 * 2**20
VMEM_LIMIT = 48 * 2**20
LANE = 128

EPS = 1e-6
SSM_HEADS = 16
HEAD_DIM = 64
SSM_GROUPS = 2
SSM_STATE = 128
SSM_CONV = 4
CHUNK = 128
CONV_WIDTH = 31
ATT_HEADS = 16
ATT_KV_HEADS = 4
WINDOW = 128
ROPE_THETA = 10000.0
ADAM_LR = 0.001
ADAM_B1 = 0.9
ADAM_B2 = 0.999
ADAM_EPS = 1e-08
ADAM_WD = 0.01
ADAM_STEP = 10

N_CHIPS = 4
N_DEV = 8

NN = ((1,), (0,))
NT = ((1,), (1,))
TN = ((0,), (0,))


def _mm(a, b, dims=NN):
    return lax.dot_general(a.astype(MXU_DTYPE), b.astype(MXU_DTYPE), (dims, ((), ())), preferred_element_type=F32)


def _split3(a):
    hi = a.astype(BF16)
    r = a - hi.astype(F32)
    mid = r.astype(BF16)
    lo = (r - mid.astype(F32)).astype(BF16)
    return hi, mid, lo


def _mm01(a, onehot, dims=NN):
    o = onehot.astype(BF16)
    out = None
    for part in _split3(a):
        t = lax.dot_general(part, o, (dims, ((), ())), preferred_element_type=F32)
        out = t if out is None else out + t
    return out


def _01mm(onehot, a):
    o = onehot.astype(BF16)
    out = None
    for part in _split3(a):
        t = lax.dot_general(o, part, (NN, ((), ())), preferred_element_type=F32)
        out = t if out is None else out + t
    return out


def _sigmoid(x):
    return 0.5 * jnp.tanh(0.5 * x) + 0.5


def _softplus(x):
    return jnp.maximum(x, 0.0) + jnp.log(1.0 + jnp.exp(-jnp.abs(x)))


def _iota(shape, axis):
    return lax.broadcasted_iota(jnp.int32, shape, axis)


def _head_indicator(width, heads, transposed=False):
    per = width // heads
    if transposed:
        return (_iota((heads, width), 1) // per == _iota((heads, width), 0)).astype(F32)
    return (_iota((width, heads), 0) // per == _iota((width, heads), 1)).astype(F32)


def _acc(ref, i, val):
    @pl.when(i == 0)
    def _():
        ref[...] = val

    @pl.when(i > 0)
    def _():
        ref[...] += val


def _rs(tile, width, col=0, shift=0, n=None):
    if shift == 0:
        return pl.BlockSpec((tile, width), lambda i: (i, col))
    if shift < 0:
        return pl.BlockSpec((tile, width), lambda i: (jnp.maximum(i - 1, 0), col))
    return pl.BlockSpec((tile, width), lambda i: (jnp.minimum(i + 1, n - 1), col))


def _ps(shape):
    return pl.BlockSpec(shape, lambda i: (0,) * len(shape))


def _call(body, name, grid, in_specs, out_specs, out_shape, scratch=(), sem=None):
    return pl.pallas_call(
        body, name=name, grid=grid, in_specs=in_specs, out_specs=out_specs, out_shape=out_shape,
        scratch_shapes=list(scratch),
        compiler_params=pltpu.CompilerParams(dimension_semantics=sem, vmem_limit_bytes=VMEM_LIMIT))


def _row_tile(n, target):
    t = min(n, target)
    assert n % t == 0, (n, t)
    return t


def _pick_tile(dim, target):
    if dim <= target:
        return dim
    t = (int(1.4 * target) // LANE) * LANE
    while t >= LANE:
        if dim % t == 0:
            return t
        t -= LANE
    return dim


ANY = pl.BlockSpec(memory_space=pl.ANY)


def _paired(j):
    return (j % 2) * 2 + j // 2


class ColSharded:
    def __init__(self, arr, paired=False):
        self.arr, self.paired = arr, paired
        self.nch, self.rows, self.per = arr.shape
        self.shape = (self.rows, self.nch * self.per)

    def chip(self, j):
        return _paired(j) if self.paired else j


class Slot:
    def __init__(self, buf, kind, per, off, c0=0, paired=False):
        self.buf, self.kind, self.per, self.off, self.c0, self.paired = buf, kind, per, off, c0, paired

    def chip(self, j):
        return _paired(j) if self.paired else j


def matmul(a, b, mode, name, *, out_dtype=F32, scale=None, res=None, bias=None, into=None, tm=1024, tn=1024, tk=1024):
    bshape = b.shape
    if mode == "nn":
        (m, k), (k2, n) = a.shape, bshape
    elif mode == "nt":
        (m, k), (n, k2) = a.shape, bshape
    else:
        (k, m), (k2, n) = a.shape, bshape
    assert k == k2, (a.shape, bshape, mode)
    tm, tn, tk = _pick_tile(m, tm), _pick_tile(n, tn), _pick_tile(k, tk)
    if isinstance(b, ColSharded):
        if mode == "nn":
            tn = b.per
        else:
            assert mode == "nt"
            tk = b.per
    if into is not None:
        if into.kind == "c":
            tn = into.per
            assert into.off % tm == 0 and n == N_CHIPS * into.per
        else:
            tm = max(1, min(m, int(1.4 * 1024)) // into.per) * into.per
            assert m % tm == 0 and into.off % into.per == 0 and into.c0 % (tm // into.per) == 0
    nk = k // tk
    dims = {"nn": NN, "nt": NT, "tn": TN}[mode]
    a_spec = (pl.BlockSpec((tk, tm), lambda i, j, kk: (kk, i)) if mode == "tn"
              else pl.BlockSpec((tm, tk), lambda i, j, kk: (i, kk)))
    if isinstance(b, ColSharded):
        bchip = b.chip
        b_spec = (pl.BlockSpec((None, tk, tn), lambda i, j, kk: (bchip(j), kk, 0)) if mode == "nn"
                  else pl.BlockSpec((None, tn, tk), lambda i, j, kk: (bchip(kk), j, 0)))
        b = b.arr
    else:
        b_spec = (pl.BlockSpec((tn, tk), lambda i, j, kk: (j, kk)) if mode == "nt"
                  else pl.BlockSpec((tk, tn), lambda i, j, kk: (kk, j)))
    plain_o = pl.BlockSpec((tm, tn), lambda i, j, kk: (i, j))
    ins, in_specs = [a, b], [a_spec, b_spec]
    if bias is not None:
        ins.append(bias)
        in_specs.append(pl.BlockSpec((1, tn), lambda i, j, kk: (0, j)))
    if res is not None:
        ins.append(res)
        in_specs.append(plain_o)
    aliases = {}
    if into is None:
        o_spec, o_shape = plain_o, S((m, n), out_dtype)
    else:
        aliases = {len(ins): 0}
        ins.append(into.buf)
        in_specs.append(ANY)
        o_shape = S(into.buf.shape, into.buf.dtype)
        if into.kind == "c":
            ob, ochip = into.off // tm, into.chip
            o_spec = pl.BlockSpec((None, tm, tn), lambda i, j, kk: (ochip(j), ob + i, 0))
        else:
            q, ob = tm // into.per, into.off // into.per
            cb = into.c0 // q
            o_spec = pl.BlockSpec((q, into.per, tn), lambda i, j, kk: (cb + i, ob, j))

    def body(*refs):
        a_ref, b_ref = refs[0], refs[1]
        o_ref, acc_ref = refs[-2], refs[-1]
        kk = pl.program_id(2)

        @pl.when(kk == 0)
        def _():
            acc_ref[...] = jnp.zeros_like(acc_ref)

        acc_ref[...] += _mm(a_ref[...], b_ref[...], dims)

        @pl.when(kk == nk - 1)
        def _():
            out = acc_ref[...]
            if scale is not None:
                out = out * scale
            pos = 2
            if bias is not None:
                out = out + refs[pos][...]
                pos += 1
            if res is not None:
                out = out + refs[pos][...]
            o_ref[...] = out.astype(o_ref.dtype).reshape(o_ref.shape)

    return pl.pallas_call(
        body, name=name, grid=(m // tm, n // tn, nk), in_specs=in_specs, out_specs=o_spec, out_shape=o_shape,
        scratch_shapes=[pltpu.VMEM((tm, tn), F32)], input_output_aliases=aliases,
        compiler_params=pltpu.CompilerParams(dimension_semantics=("parallel", "parallel", "arbitrary"),
                                             vmem_limit_bytes=VMEM_LIMIT))(*ins)


def rms_fwd(h, g, name):
    n, d = h.shape
    tile = _row_tile(n, 512)

    def body(h_ref, g_ref, o_ref):
        x = h_ref[...]
        r = lax.rsqrt(jnp.mean(x * x, axis=-1, keepdims=True) + EPS)
        o_ref[...] = (x * r * g_ref[...]).astype(o_ref.dtype)

    return _call(body, name, (n // tile,), [_rs(tile, d), _ps((1, d))], _rs(tile, d), S((n, d), MXU_DTYPE),
                 sem=("parallel",))(h, g)


def _rms_bwd_math(x, g, dy):
    r = lax.rsqrt(jnp.mean(x * x, axis=-1, keepdims=True) + EPS)
    xh = x * r
    dg = jnp.sum(dy * xh, axis=0, keepdims=True)
    dxh = dy * g
    dx = r * (dxh - xh * jnp.mean(dxh * xh, axis=-1, keepdims=True))
    return dx, dg


def nt_rms_bwd(a, b, h, g, dh_in, name, extra=None, colsum=False):
    n, k = a.shape
    d = h.shape[1]
    tm = _row_tile(n, 512)
    sharded = isinstance(b, ColSharded)
    tk = b.per if sharded else _pick_tile(k, 1024)
    nk = k // tk
    if sharded:
        bchip = b.chip
        b_spec = pl.BlockSpec((None, d, tk), lambda i, kk: (bchip(kk), 0, 0))
        b = b.arr
    else:
        b_spec = pl.BlockSpec((d, tk), lambda i, kk: (0, kk))
    row = pl.BlockSpec((tm, d), lambda i, kk: (i, 0))
    vec = pl.BlockSpec((1, d), lambda i, kk: (0, 0))
    ins, in_specs = [a, b, h, g, dh_in], [pl.BlockSpec((tm, tk), lambda i, kk: (i, kk)), b_spec, row, vec, row]
    if extra is not None:
        k2 = extra[0].shape[1]
        ins += list(extra)
        in_specs += [pl.BlockSpec((tm, k2), lambda i, kk: (i, 0)), pl.BlockSpec((d, k2), lambda i, kk: (0, 0))]
    n_in = len(ins)

    def body(*refs):
        a_ref, b_ref, h_ref, g_ref, dh_ref = refs[:5]
        o_ref, dg_ref = refs[n_in], refs[n_in + 1]
        acc_ref = refs[-1]
        i, kk = pl.program_id(0), pl.program_id(1)

        @pl.when(kk == 0)
        def _():
            acc_ref[...] = _mm(refs[5][...], refs[6][...], NT) if extra is not None else jnp.zeros_like(acc_ref)

        acc_ref[...] += _mm(a_ref[...], b_ref[...], NT)

        @pl.when(kk == nk - 1)
        def _():
            dx, dg = _rms_bwd_math(h_ref[...], g_ref[...], acc_ref[...])
            out = dh_ref[...] + dx
            o_ref[...] = out
            _acc(dg_ref, i, dg)
            if colsum:
                _acc(refs[n_in + 2], i, jnp.sum(out, axis=0, keepdims=True))

    n_vec = 2 if colsum else 1
    return pl.pallas_call(
        body, name=name, grid=(n // tm, nk), in_specs=in_specs, out_specs=[row] + [vec] * n_vec,
        out_shape=[S((n, d), F32)] + [S((1, d), F32)] * n_vec, scratch_shapes=[pltpu.VMEM((tm, d), F32)],
        compiler_params=pltpu.CompilerParams(dimension_semantics=("arbitrary", "arbitrary"), vmem_limit_bytes=VMEM_LIMIT),
    )(*ins)


def rms_bwd(h, g, dxn, dh_in, name, colsum=False):
    n, d = h.shape
    tile = _row_tile(n, 256)

    def body(h_ref, g_ref, dxn_ref, dh_ref, o_ref, dg_ref, *cs_ref):
        i = pl.program_id(0)
        dx, dg = _rms_bwd_math(h_ref[...], g_ref[...], dxn_ref[...].astype(F32))
        out = dh_ref[...] + dx
        o_ref[...] = out
        _acc(dg_ref, i, dg)
        if colsum:
            _acc(cs_ref[0], i, jnp.sum(out, axis=0, keepdims=True))

    outs = [S((n, d), F32), S((1, d), F32)] + ([S((1, d), F32)] if colsum else [])
    ospecs = [_rs(tile, d), _ps((1, d))] + ([_ps((1, d))] if colsum else [])
    return _call(body, name, (n // tile,), [_rs(tile, d), _ps((1, d)), _rs(tile, d), _rs(tile, d)], ospecs, outs,
                 sem=("arbitrary",))(h, g, dxn, dh_in)


def swiglu_in(h, g, w_in, name):
    n, d = h.shape
    per = w_in.per
    nj = w_in.nch // 2
    tile = _row_tile(n, 512)

    def body(h_ref, g_ref, wg_ref, wu_ref, xn_ref, u_ref, hm_ref):
        x = h_ref[...]
        r = lax.rsqrt(jnp.mean(x * x, axis=-1, keepdims=True) + EPS)
        xn = (x * r * g_ref[...]).astype(xn_ref.dtype)

        @pl.when(pl.program_id(1) == 0)
        def _():
            xn_ref[...] = xn

        a = _mm(xn, wg_ref[...])
        b = _mm(xn, wu_ref[...])
        u_ref[:, :per] = a.astype(u_ref.dtype)
        u_ref[:, per:] = b.astype(u_ref.dtype)
        hm_ref[...] = (a * _sigmoid(a) * b).astype(hm_ref.dtype)

    return pl.pallas_call(
        body, name=name, grid=(n // tile, nj),
        in_specs=[pl.BlockSpec((tile, d), lambda i, j: (i, 0)), pl.BlockSpec((1, d), lambda i, j: (0, 0)),
                  pl.BlockSpec((None, d, per), lambda i, j: (j, 0, 0)), pl.BlockSpec((None, d, per), lambda i, j: (nj + j, 0, 0))],
        out_specs=[pl.BlockSpec((tile, d), lambda i, j: (i, 0)), pl.BlockSpec((tile, 2 * per), lambda i, j: (i, j)),
                   pl.BlockSpec((tile, per), lambda i, j: (i, j))],
        out_shape=[S((n, d), MXU_DTYPE), S((n, 2 * nj * per), MXU_DTYPE), S((n, nj * per), MXU_DTYPE)],
        compiler_params=pltpu.CompilerParams(dimension_semantics=("parallel", "arbitrary"), vmem_limit_bytes=VMEM_LIMIT),
    )(h, g, w_in.arr, w_in.arr)


def swiglu_out_bwd(dh, w_out, u, after, name):
    n, d = dh.shape
    f = w_out.shape[0]
    per = u.shape[1] // 4
    nj = f // per
    tile = _row_tile(n, 512)

    def body(dh_ref, w_ref, u_ref, after_ref, du_ref):
        dm = 0.5 * _mm(dh_ref[...], w_ref[...], NT)
        a = u_ref[:, :per].astype(F32)
        b = u_ref[:, per:].astype(F32)
        s = _sigmoid(a)
        du_ref[:, :per] = (dm * b * s * (1.0 + a * (1.0 - s))).astype(du_ref.dtype)
        du_ref[:, per:] = (dm * a * s).astype(du_ref.dtype)

    return pl.pallas_call(
        body, name=name, grid=(n // tile, nj),
        in_specs=[pl.BlockSpec((tile, d), lambda i, j: (i, 0)), pl.BlockSpec((per, d), lambda i, j: (j, 0)),
                  pl.BlockSpec((tile, 2 * per), lambda i, j: (i, j)), ANY],
        out_specs=pl.BlockSpec((tile, 2 * per), lambda i, j: (i, j)),
        out_shape=S(u.shape, MXU_DTYPE),
        compiler_params=pltpu.CompilerParams(dimension_semantics=("parallel", "parallel"), vmem_limit_bytes=VMEM_LIMIT),
    )(dh, w_out, u, after)


def ple_fwd(h, gl, pp, name):
    n, d = h.shape
    tile = _row_tile(n, 512)

    def body(h_ref, gl_ref, pp_ref, o_ref):
        o_ref[...] = h_ref[...] + _sigmoid(gl_ref[...]) * pp_ref[...]

    return _call(body, name, (n // tile,), [_rs(tile, d)] * 3, _rs(tile, d), S((n, d), F32), sem=("parallel",))(h, gl, pp)


def ple_bwd(dh, gl, pp, after, name):
    n, d = dh.shape
    tile = _row_tile(n, 512)

    def body(dh_ref, gl_ref, pp_ref, after_ref, dpp_ref, dgl_ref):
        g = _sigmoid(gl_ref[...])
        dh_ = dh_ref[...]
        dpp_ref[...] = (dh_ * g).astype(dpp_ref.dtype)
        dgl_ref[...] = (dh_ * pp_ref[...] * g * (1.0 - g)).astype(dgl_ref.dtype)

    return _call(body, name, (n // tile,), [_rs(tile, d)] * 3 + [ANY], [_rs(tile, d)] * 2, [S((n, d), MXU_DTYPE)] * 2,
                 sem=("parallel",))(dh, gl, pp, after)


def loss_head(h, g, target, name):
    n, d = h.shape
    tile = _row_tile(n, 256)

    def body(h_ref, g_ref, t_ref, dh_ref, dg_ref, loss_ref):
        i = pl.program_id(0)
        x = h_ref[...]
        gg = g_ref[...]
        r = lax.rsqrt(jnp.mean(x * x, axis=-1, keepdims=True) + EPS)
        err = x * r * gg - t_ref[...]
        part = 0.5 * jnp.sum(jnp.mean(err * err, axis=-1, keepdims=True), axis=0, keepdims=True)
        dx, dg = _rms_bwd_math(x, gg, err * (1.0 / d))
        dh_ref[...] = dx
        _acc(dg_ref, i, dg)
        _acc(loss_ref, i, jnp.broadcast_to(part, (8, LANE)))

    return _call(body, name, (n // tile,), [_rs(tile, d), _ps((1, d)), _rs(tile, d)],
                 [_rs(tile, d), _ps((1, d)), _ps((8, LANE))], [S((n, d), F32), S((1, d), F32), S((8, LANE), F32)],
                 sem=("arbitrary",))(h, g, target)


def _adamw_math(w, g, m, v):
    c1 = np.float32(1.0 - ADAM_B1 ** ADAM_STEP)
    c2 = np.float32(1.0 - ADAM_B2 ** ADAM_STEP)
    mm = ADAM_B1 * m + (1.0 - ADAM_B1) * g
    vv = ADAM_B2 * v + (1.0 - ADAM_B2) * (g * g)
    return -ADAM_LR * ((mm / c1) / (jnp.sqrt(vv / c2) + ADAM_EPS) + ADAM_WD * w), mm, vv


def adamw_layer(w, pack, off, m, v, li, prev, name):
    n, r, c = w.shape
    tile = next(t for t in (512, 256, 128, 64, 32, 16, 8) if r % t == 0 and off % t == 0 and t * c * 4 <= 2**21)
    ob = off // tile

    def body(w_ref, g_ref, m_ref, v_ref, *refs):
        go_ref, d_ref, mo_ref, vo_ref = refs[-4:]
        g = g_ref[...]
        go_ref[...] = g
        d_ref[...], mo_ref[...], vo_ref[...] = _adamw_math(w_ref[...], g, m_ref[...], v_ref[...])

    blk = pl.BlockSpec((None, tile, c), lambda i: (li, i, 0))
    prev = list(prev) if prev is not None else []
    return pl.pallas_call(
        body, name=name, grid=(r // tile,),
        in_specs=[blk, pl.BlockSpec((tile, c), lambda i: (ob + i, 0)), blk, blk] + [ANY] * len(prev),
        out_specs=[blk] * 4, out_shape=[S((n, r, c), F32)] * 4,
        input_output_aliases={4 + j: j for j in range(len(prev))},
        compiler_params=pltpu.CompilerParams(dimension_semantics=("parallel",), vmem_limit_bytes=VMEM_LIMIT),
    )(w, pack, m, v, *prev)


def adamw(w, g, m, v, name):
    r, c = w.shape
    tile = r
    for t in (512, 256, 128, 64, 32, 16, 8):
        if r % t == 0 and t * c * 4 <= 2**21:
            tile = t
            break

    def body(w_ref, g_ref, m_ref, v_ref, d_ref, mo_ref, vo_ref):
        d_ref[...], mo_ref[...], vo_ref[...] = _adamw_math(w_ref[...], g_ref[...], m_ref[...], v_ref[...])

    return _call(body, name, (r // tile,), [_rs(tile, c)] * 4, [_rs(tile, c)] * 3, [S((r, c), F32)] * 3,
                 sem=("parallel",))(w, g, m, v)


def _taps_fwd(sc, w_ref, width, halo, tile, acc):
    for k in range(width):
        o = halo - (width - 1) + k
        acc = acc + w_ref[k:k + 1, :] * sc[o:o + tile, :]
    return acc


def _taps_bwd_x(sc_d, w_ref, width, tile, acc):
    for k in range(width):
        o = (width - 1) - k
        acc = acc + w_ref[k:k + 1, :] * sc_d[o:o + tile, :]
    return acc


def _taps_bwd_w(dy, sc, dw_ref, width, halo, tile, i):
    @pl.when(i == 0)
    def _():
        dw_ref[...] = jnp.zeros_like(dw_ref)

    for k in range(width):
        o = halo - (width - 1) + k
        dw_ref[k:k + 1, :] += jnp.sum(dy * sc[o:o + tile, :], axis=0, keepdims=True)


def _ln_stats(x):
    mu = jnp.mean(x, axis=-1, keepdims=True)
    xc = x - mu
    r = lax.rsqrt(jnp.mean(xc * xc, axis=-1, keepdims=True) + EPS)
    return xc * r, r


def conv_group_fwd(proj, cw, cb, lg, lb, name):
    n = proj.shape[0]
    d = cw.shape[1]
    tile = _row_tile(n, 256)
    halo = 32

    def body(v_ref, g_ref, vp_ref, gp_ref, cw_ref, cb_ref, lg_ref, lb_ref, u_ref, u1_ref, sc):
        i = pl.program_id(0)
        first = (i > 0).astype(F32)
        sc[0:halo, :] = vp_ref[tile - halo:, :] * _sigmoid(gp_ref[tile - halo:, :]) * first
        sc[halo:, :] = v_ref[...] * _sigmoid(g_ref[...])
        u1 = _taps_fwd(sc, cw_ref, CONV_WIDTH, halo, tile, jnp.zeros((tile, d), F32) + cb_ref[...])
        u1_ref[...] = u1
        xh, _ = _ln_stats(u1)
        y = xh * lg_ref[...] + lb_ref[...]
        u_ref[...] = (y * _sigmoid(y)).astype(u_ref.dtype)

    return _call(body, name, (n // tile,),
                 [_rs(tile, d, 0), _rs(tile, d, 1), _rs(tile, d, 0, -1), _rs(tile, d, 1, -1),
                  _ps(cw.shape), _ps((1, d)), _ps((1, d)), _ps((1, d))],
                 [_rs(tile, d), _rs(tile, d)], [S((n, d), MXU_DTYPE), S((n, d), F32)],
                 scratch=[pltpu.VMEM((halo + tile, d), F32)], sem=("arbitrary",))(proj, proj, proj, proj, cw, cb, lg, lb)


def conv_group_bwd(du, u1, proj, cw, lg, lb, name):
    n = proj.shape[0]
    d = cw.shape[1]
    tile = _row_tile(n, 256)
    halo = 32
    nt = n // tile

    def body(du_ref, dun_ref, u1_ref, u1n_ref, v_ref, g_ref, vp_ref, gp_ref, cw_ref, lg_ref, lb_ref,
             dp_ref, dcw_ref, dcb_ref, dlg_ref, dlb_ref, sc, sc_d):
        i = pl.program_id(0)

        def ln_swish_bwd(dy_, u1_):
            xh, r = _ln_stats(u1_)
            y = xh * lg_ref[...] + lb_ref[...]
            s = _sigmoid(y)
            dyy = dy_ * s * (1.0 + y * (1.0 - s))
            dxh = dyy * lg_ref[...]
            dx = r * (dxh - jnp.mean(dxh, axis=-1, keepdims=True) - xh * jnp.mean(dxh * xh, axis=-1, keepdims=True))
            return dx, jnp.sum(dyy * xh, axis=0, keepdims=True), jnp.sum(dyy, axis=0, keepdims=True)

        du1, dlg, dlb = ln_swish_bwd(du_ref[...].astype(F32), u1_ref[...])
        du1n, _, _ = ln_swish_bwd(dun_ref[0:halo, :].astype(F32), u1n_ref[0:halo, :])
        sc_d[0:tile, :] = du1
        sc_d[tile:, :] = du1n * (i < nt - 1).astype(F32)
        sig = _sigmoid(g_ref[...])
        val = v_ref[...]
        sc[0:halo, :] = vp_ref[tile - halo:, :] * _sigmoid(gp_ref[tile - halo:, :]) * (i > 0).astype(F32)
        sc[halo:, :] = val * sig
        du0 = _taps_bwd_x(sc_d, cw_ref, CONV_WIDTH, tile, jnp.zeros((tile, d), F32))
        _taps_bwd_w(du1, sc, dcw_ref, CONV_WIDTH, halo, tile, i)
        _acc(dcb_ref, i, jnp.sum(du1, axis=0, keepdims=True))
        _acc(dlg_ref, i, dlg)
        _acc(dlb_ref, i, dlb)
        dp_ref[:, :d] = (du0 * sig).astype(dp_ref.dtype)
        dp_ref[:, d:] = (du0 * val * sig * (1.0 - sig)).astype(dp_ref.dtype)

    return _call(body, name, (nt,),
                 [_rs(tile, d), _rs(tile, d, 0, 1, nt), _rs(tile, d), _rs(tile, d, 0, 1, nt),
                  _rs(tile, d, 0), _rs(tile, d, 1), _rs(tile, d, 0, -1), _rs(tile, d, 1, -1),
                  _ps(cw.shape), _ps((1, d)), _ps((1, d))],
                 [_rs(tile, 2 * d), _ps(cw.shape), _ps((1, d)), _ps((1, d)), _ps((1, d))],
                 [S((n, proj.shape[1]), MXU_DTYPE), S(cw.shape, F32), S((1, d), F32), S((1, d), F32), S((1, d), F32)],
                 scratch=[pltpu.VMEM((halo + tile, d), F32), pltpu.VMEM((tile + halo, d), F32)],
                 sem=("arbitrary",))(du, du, u1, u1, proj, proj, proj, proj, cw, lg, lb)


def ssm_conv_fwd(proj, dtr, sw, sb, dtb, name):
    n = proj.shape[0]
    w = sw.shape[1]
    inner = SSM_HEADS * HEAD_DIM
    tile = _row_tile(n, 256)
    halo = 8

    def body(x_ref, xp_ref, dtr_ref, sw_ref, sb_ref, dtb_ref, pre_ref, xs_ref, bc_ref, dt_ref, sc):
        i = pl.program_id(0)
        sc[0:halo, :] = xp_ref[tile - halo:, :] * (i > 0).astype(F32)
        sc[halo:, :] = x_ref[...]
        pre = _taps_fwd(sc, sw_ref, SSM_CONV, halo, tile, jnp.zeros((tile, w), F32) + sb_ref[...])
        pre_ref[...] = pre
        act = pre * _sigmoid(pre)
        xs_ref[...] = act[:, :inner]
        bc_ref[...] = act[:, inner:]
        dt = _softplus(dtr_ref[...] + dtb_ref[...])
        dt_ref[...] = jnp.where(_iota(dt.shape, 1) < SSM_HEADS, dt, 0.0)

    return _call(body, name, (n // tile,),
                 [_rs(tile, w, 2), _rs(tile, w, 2, -1), _rs(tile, LANE), _ps(sw.shape), _ps((1, w)), _ps((1, LANE))],
                 [_rs(tile, w), _rs(tile, inner), _rs(tile, w - inner), _rs(tile, LANE)],
                 [S((n, w), F32), S((n, inner), F32), S((n, w - inner), F32), S((n, LANE), F32)],
                 scratch=[pltpu.VMEM((halo + tile, w), F32)], sem=("arbitrary",))(proj, proj, dtr, sw, sb, dtb)


def ssm_conv_bwd(dxs, dbc, pre, proj, sw, dproj, name):
    n = proj.shape[0]
    w = sw.shape[1]
    inner = SSM_HEADS * HEAD_DIM
    tile = _row_tile(n, 256)
    halo = 8
    nt = n // tile

    def body(dxs_ref, dxsn_ref, dbc_ref, dbcn_ref, pre_ref, pren_ref, x_ref, xp_ref, sw_ref, dp_in_ref,
             dx_ref, dsw_ref, dsb_ref, sc, sc_d):
        i = pl.program_id(0)

        def silu_bwd(d_, p_):
            s = _sigmoid(p_)
            return d_ * s * (1.0 + p_ * (1.0 - s))

        sc_d[0:tile, :inner] = silu_bwd(dxs_ref[...], pre_ref[:, :inner])
        sc_d[0:tile, inner:] = silu_bwd(dbc_ref[...], pre_ref[:, inner:])
        last = (i < nt - 1).astype(F32)
        sc_d[tile:, :inner] = silu_bwd(dxsn_ref[0:halo, :], pren_ref[0:halo, :inner]) * last
        sc_d[tile:, inner:] = silu_bwd(dbcn_ref[0:halo, :], pren_ref[0:halo, inner:]) * last
        sc[0:halo, :] = xp_ref[tile - halo:, :] * (i > 0).astype(F32)
        sc[halo:, :] = x_ref[...]
        dpre = sc_d[0:tile, :]
        dx_ref[...] = _taps_bwd_x(sc_d, sw_ref, SSM_CONV, tile, jnp.zeros((tile, w), F32)).astype(dx_ref.dtype)
        _taps_bwd_w(dpre, sc, dsw_ref, SSM_CONV, halo, tile, i)
        _acc(dsb_ref, i, jnp.sum(dpre, axis=0, keepdims=True))

    return pl.pallas_call(
        body, name=name, grid=(nt,),
        in_specs=[_rs(tile, inner), _rs(tile, inner, 0, 1, nt), _rs(tile, w - inner), _rs(tile, w - inner, 0, 1, nt),
                  _rs(tile, w), _rs(tile, w, 0, 1, nt), _rs(tile, w, 2), _rs(tile, w, 2, -1), _ps(sw.shape), ANY],
        out_specs=[_rs(tile, w, 2), _ps(sw.shape), _ps((1, w))],
        out_shape=[S(dproj.shape, dproj.dtype), S(sw.shape, F32), S((1, w), F32)],
        scratch_shapes=[pltpu.VMEM((halo + tile, w), F32), pltpu.VMEM((tile + halo, w), F32)],
        input_output_aliases={9: 0},
        compiler_params=pltpu.CompilerParams(dimension_semantics=("arbitrary",), vmem_limit_bytes=VMEM_LIMIT),
    )(dxs, dxs, dbc, dbc, pre, pre, proj, proj, sw, dproj)


def _ssd_prologue(dt_ref, dtT_ref, al_ref, alc_ref):
    row = _iota((CHUNK, CHUNK), 0)
    col = _iota((CHUNK, CHUNK), 1)
    dt = dt_ref[:, :SSM_HEADS]
    a_row = -jnp.exp(al_ref[:, :SSM_HEADS])
    a_col = -jnp.exp(alc_ref[...])
    cs = _01mm((row >= col).astype(F32), dt * a_row)
    csT = _mm01(dtT_ref[...] * a_col, (row <= col).astype(F32))
    return dt, a_row, cs, csT, row, col


def _decay(cs, csT, h, row, col):
    lm = jnp.exp(jnp.where(row >= col, cs[:, h:h + 1] - csT[h:h + 1, :], -1e30))
    lmT = jnp.exp(jnp.where(col >= row, csT[h:h + 1, :] - cs[:, h:h + 1], -1e30))
    return lm, lmT


def ssd_fwd(xs, bc, dt, dtT, alog_row, alog_col, name):
    n, width = xs.shape
    nc = n // CHUNK
    gw = width // SSM_GROUPS
    hpg = SSM_HEADS // SSM_GROUPS
    ns = SSM_STATE

    def body(xs_ref, bc_ref, dt_ref, dtT_ref, al_ref, alc_ref, y_ref, hs_ref, h_sc):
        i = pl.program_id(0)

        @pl.when(i == 0)
        def _():
            h_sc[...] = jnp.zeros_like(h_sc)

        dt, a_row, cs, csT, row, col = _ssd_prologue(dt_ref, dtT_ref, al_ref, alc_ref)
        indT = _head_indicator(width, SSM_HEADS, transposed=True)
        dt_full = _mm01(dt, indT)
        e_full = jnp.exp(_mm01(cs, indT))
        dte_full = jnp.exp(_mm01(cs[CHUNK - 1:CHUNK, :] - cs, indT))
        xt = xs_ref[...] * dt_full
        hs_ref[0] = h_sc[...]
        lo = _iota((CHUNK, 2 * HEAD_DIM), 1) < HEAD_DIM
        for g in range(SSM_GROUPS):
            bg = bc_ref[:, g * ns:(g + 1) * ns]
            cg = bc_ref[:, (SSM_GROUPS + g) * ns:(SSM_GROUPS + g + 1) * ns]
            gm = _mm(cg, bg, NT)
            hg = h_sc[g * gw:(g + 1) * gw, :]
            yoff = e_full[:, g * gw:(g + 1) * gw] * _mm(cg, hg, NT)
            for pr in range(hpg // 2):
                h0 = g * hpg + 2 * pr
                c0 = h0 * HEAD_DIM
                xp = xt[:, c0:c0 + 2 * HEAD_DIM]
                m0 = gm * _decay(cs, csT, h0, row, col)[0]
                m1 = gm * _decay(cs, csT, h0 + 1, row, col)[0]
                yd = jnp.where(lo, _mm(m0, xp), _mm(m1, xp))
                y_ref[:, c0:c0 + 2 * HEAD_DIM] = yd + yoff[:, 2 * pr * HEAD_DIM:(2 * pr + 2) * HEAD_DIM]
            sg = _mm(xt[:, g * gw:(g + 1) * gw] * dte_full[:, g * gw:(g + 1) * gw], bg, TN)
            for hh in range(hpg):
                h = g * hpg + hh
                r0 = h * HEAD_DIM
                h_sc[r0:r0 + HEAD_DIM, :] = (h_sc[r0:r0 + HEAD_DIM, :] * jnp.exp(csT[h:h + 1, CHUNK - 1:CHUNK])
                                             + sg[hh * HEAD_DIM:(hh + 1) * HEAD_DIM, :])

    bcw = bc.shape[1]
    return _call(body, name, (nc,),
                 [_rs(CHUNK, width), _rs(CHUNK, bcw), _rs(CHUNK, LANE), pl.BlockSpec((SSM_HEADS, CHUNK), lambda i: (0, i)),
                  _ps((1, LANE)), _ps((SSM_HEADS, 1))],
                 [_rs(CHUNK, width), pl.BlockSpec((1, width, ns), lambda i: (i, 0, 0))],
                 [S((n, width), F32), S((nc, width, ns), F32)],
                 scratch=[pltpu.VMEM((width, ns), F32)], sem=("arbitrary",))(xs, bc, dt, dtT, alog_row, alog_col)


def ssd_bwd(xs, bc, dt, dtT, alog_row, alog_col, hs, dy, dxs_skip, name):
    n, width = xs.shape
    nc = n // CHUNK
    gw = width // SSM_GROUPS
    hpg = SSM_HEADS // SSM_GROUPS
    ns = SSM_STATE
    bcw = bc.shape[1]

    def body(xs_ref, bc_ref, dt_ref, dtT_ref, al_ref, alc_ref, hs_ref, dy_ref, skip_ref,
             dxs_ref, dbc_ref, ddtr_ref, dal_ref, ddtb_ref, dh_sc, dxt_sc):
        i = pl.program_id(0)

        @pl.when(i == 0)
        def _():
            dh_sc[...] = jnp.zeros_like(dh_sc)

        dt, a_row, cs, csT, row, col = _ssd_prologue(dt_ref, dtT_ref, al_ref, alc_ref)
        indT = _head_indicator(width, SSM_HEADS, transposed=True)
        ind = _head_indicator(width, SSM_HEADS)
        dt_full = _mm01(dt, indT)
        e_full = jnp.exp(_mm01(cs, indT))
        cs_last = cs[CHUNK - 1:CHUNK, :]
        dte = jnp.exp(cs_last - cs)
        dte_full = _mm01(dte, indT)
        xs_ = xs_ref[...]
        xt = xs_ * dt_full
        dy_ = dy_ref[...]
        hmat = hs_ref[0]
        ds = dh_sc[...]
        lo = _iota((CHUNK, 2 * HEAD_DIM), 1) < HEAD_DIM
        head_lane = _iota((1, SSM_HEADS), 1)
        dcs = jnp.zeros((CHUNK, SSM_HEADS), F32)
        ddte = jnp.zeros((CHUNK, SSM_HEADS), F32)
        for g in range(SSM_GROUPS):
            sl = slice(g * gw, (g + 1) * gw)
            bg = bc_ref[:, g * ns:(g + 1) * ns]
            cg = bc_ref[:, (SSM_GROUPS + g) * ns:(SSM_GROUPS + g + 1) * ns]
            gm = _mm(cg, bg, NT)
            gmT = _mm(bg, cg, NT)
            hg = hmat[sl, :]
            dsg = ds[sl, :]
            dyg = dy_[:, sl]
            xtg = xt[:, sl]
            yoff = e_full[:, sl] * _mm(cg, hg, NT)
            edy = e_full[:, sl] * dyg
            bds = _mm(bg, dsg, NT)
            dxt_g = dte_full[:, sl] * bds
            ddte = ddte + _mm01(xtg * bds, ind[sl, :])
            dcs = dcs + _mm01(dyg * yoff, ind[sl, :])
            db = _mm(xtg * dte_full[:, sl], dsg)
            dc = _mm(edy, hg)
            dhc = _mm(edy, cg, TN)
            dgs = jnp.zeros((CHUNK, CHUNK), F32)
            dgTs = jnp.zeros((CHUNK, CHUNK), F32)
            for pr in range(hpg // 2):
                h0 = g * hpg + 2 * pr
                c0 = 2 * pr * HEAD_DIM
                xp = xtg[:, c0:c0 + 2 * HEAD_DIM]
                dyp = dyg[:, c0:c0 + 2 * HEAD_DIM]
                rr = []
                for h, half in ((h0, lo), (h0 + 1, jnp.logical_not(lo))):
                    lm, lmT = _decay(cs, csT, h, row, col)
                    xm = jnp.where(half, xp, 0.0)
                    dm = _mm(dyp, xm, NT)
                    dmT = _mm(xm, dyp, NT)
                    mT = gmT * lmT
                    z = jnp.sum(dm * (gm * lm), axis=1, keepdims=True) - jnp.sum(dmT * mT, axis=1, keepdims=True)
                    dcs = dcs + z * (head_lane == h).astype(F32)
                    dgs = dgs + dm * lm
                    dgTs = dgTs + dmT * lmT
                    rr.append(_mm(mT, dyp))
                dxt_sc[:, g * gw + c0:g * gw + c0 + 2 * HEAD_DIM] = jnp.where(lo, rr[0], rr[1]) + dxt_g[:, c0:c0 + 2 * HEAD_DIM]
            dbc_ref[:, g * ns:(g + 1) * ns] = db + _mm(dgTs, cg)
            dbc_ref[:, (SSM_GROUPS + g) * ns:(SSM_GROUPS + g + 1) * ns] = dc + _mm(dgs, bg)
            for hh in range(hpg):
                h = g * hpg + hh
                r0 = h * HEAD_DIM
                dh_sc[r0:r0 + HEAD_DIM, :] = (dhc[hh * HEAD_DIM:(hh + 1) * HEAD_DIM, :]
                                              + jnp.exp(csT[h:h + 1, CHUNK - 1:CHUNK]) * ds[r0:r0 + HEAD_DIM, :])
        t = ddte * dte
        per_head = jnp.sum(jnp.sum(ds * hmat, axis=1, keepdims=True) * ind, axis=0, keepdims=True)
        last_add = jnp.sum(t, axis=0, keepdims=True) + jnp.exp(cs_last) * per_head
        dcs = dcs - t + jnp.where(_iota((CHUNK, SSM_HEADS), 0) == CHUNK - 1, last_add, 0.0)
        dadt = _01mm((row <= col).astype(F32), dcs)
        dxt = dxt_sc[...]
        ddt = dadt * a_row + _mm01(dxt * xs_, ind)
        dxs_ref[...] = dxt * dt_full + skip_ref[...]
        ddtr = ddt * (1.0 - jnp.exp(-dt))
        ddtr_ref[...] = jnp.zeros_like(ddtr_ref)
        ddtr_ref[:, :SSM_HEADS] = ddtr.astype(ddtr_ref.dtype)
        _acc(dal_ref, i, jnp.sum(dadt * dt, axis=0, keepdims=True) * a_row)
        _acc(ddtb_ref, i, jnp.sum(ddtr, axis=0, keepdims=True))

    rev = lambda i: (nc - 1 - i, 0)
    return _call(body, name, (nc,),
                 [pl.BlockSpec((CHUNK, width), rev), pl.BlockSpec((CHUNK, bcw), rev), pl.BlockSpec((CHUNK, LANE), rev),
                  pl.BlockSpec((SSM_HEADS, CHUNK), lambda i: (0, nc - 1 - i)), _ps((1, LANE)), _ps((SSM_HEADS, 1)),
                  pl.BlockSpec((1, width, ns), lambda i: (nc - 1 - i, 0, 0)), pl.BlockSpec((CHUNK, width), rev),
                  pl.BlockSpec((CHUNK, width), rev)],
                 [pl.BlockSpec((CHUNK, width), rev), pl.BlockSpec((CHUNK, bcw), rev), pl.BlockSpec((CHUNK, LANE), rev),
                  _ps((1, SSM_HEADS)), _ps((1, SSM_HEADS))],
                 [S((n, width), F32), S((n, bcw), F32), S((n, LANE), MXU_DTYPE), S((1, SSM_HEADS), F32), S((1, SSM_HEADS), F32)],
                 scratch=[pltpu.VMEM((width, ns), F32), pltpu.VMEM((CHUNK, width), F32)],
                 sem=("arbitrary",))(xs, bc, dt, dtT, alog_row, alog_col, hs, dy, dxs_skip)


def ssm_gate_fwd(yssd, xs, proj, dfull, gamma, name):
    n, d = yssd.shape
    tile = _row_tile(n, 256)
    gw = d // SSM_GROUPS

    def body(y_ref, xs_ref, z_ref, df_ref, gm_ref, o_ref):
        z = z_ref[...]
        y2 = (y_ref[...] + df_ref[...] * xs_ref[...]) * (z * _sigmoid(z))
        for g in range(SSM_GROUPS):
            yg = y2[:, g * gw:(g + 1) * gw]
            r = lax.rsqrt(jnp.mean(yg * yg, axis=-1, keepdims=True) + EPS)
            o_ref[:, g * gw:(g + 1) * gw] = (yg * r * gm_ref[:, g * gw:(g + 1) * gw]).astype(o_ref.dtype)

    return _call(body, name, (n // tile,), [_rs(tile, d), _rs(tile, d), _rs(tile, d, 2), _ps((1, d)), _ps((1, d))],
                 _rs(tile, d), S((n, d), MXU_DTYPE), sem=("parallel",))(yssd, xs, proj, dfull, gamma)


def ssm_gate_bwd(dy3, yssd, xs, proj, dfull, gamma, dproj, name):
    n, d = yssd.shape
    tile = _row_tile(n, 256)
    gw = d // SSM_GROUPS

    def body(dy_ref, y_ref, xs_ref, z_ref, df_ref, gm_ref, dp_in_ref, dys_ref, dxs_ref, dz_ref, dgm_ref, dd_ref):
        i = pl.program_id(0)
        z = z_ref[...]
        s = _sigmoid(z)
        xs_ = xs_ref[...]
        y1 = y_ref[...] + df_ref[...] * xs_
        y2 = y1 * (z * s)
        dy_ = dy_ref[...].astype(F32)
        dgm = []
        dy2 = []
        for g in range(SSM_GROUPS):
            sl = slice(g * gw, (g + 1) * gw)
            dxg, dgg = _rms_bwd_math(y2[:, sl], gm_ref[:, sl], dy_[:, sl])
            dy2.append(dxg)
            dgm.append(dgg)
        dy2 = jnp.concatenate(dy2, axis=1)
        dy1 = dy2 * (z * s)
        dys_ref[...] = dy1
        dxs_ref[...] = dy1 * df_ref[...]
        dz_ref[...] = (dy2 * y1 * s * (1.0 + z * (1.0 - s))).astype(dz_ref.dtype)
        _acc(dgm_ref, i, jnp.concatenate(dgm, axis=1))
        colsum = jnp.broadcast_to(jnp.sum(dy1 * xs_, axis=0, keepdims=True), (8, d))
        _acc(dd_ref, i, _mm01(colsum, _head_indicator(d, SSM_HEADS))[0:1, :])

    return pl.pallas_call(
        body, name=name, grid=(n // tile,),
        in_specs=[_rs(tile, d), _rs(tile, d), _rs(tile, d), _rs(tile, d, 2), _ps((1, d)), _ps((1, d)), ANY],
        out_specs=[_rs(tile, d), _rs(tile, d), _rs(tile, d, 2), _ps((1, d)), _ps((1, SSM_HEADS))],
        out_shape=[S((n, d), F32), S((n, d), F32), S(dproj.shape, dproj.dtype), S((1, d), F32), S((1, SSM_HEADS), F32)],
        input_output_aliases={6: 2},
        compiler_params=pltpu.CompilerParams(dimension_semantics=("arbitrary",), vmem_limit_bytes=VMEM_LIMIT),
    )(dy3, yssd, xs, proj, dfull, gamma, dproj)


def _rope128(x, cos, sin_signed):
    half = HEAD_DIM // 2
    lane = _iota(x.shape, 1)
    partner = jnp.where((lane % HEAD_DIM) < half, pltpu.roll(x, LANE - half, 1), pltpu.roll(x, half, 1))
    return x * cos + partner * sin_signed


def rope_fwd(qkv, cos, sin, name):
    n, w = qkv.shape
    qw = ATT_HEADS * HEAD_DIM
    kw = ATT_KV_HEADS * HEAD_DIM
    tile = _row_tile(n, 256)

    def body(x_ref, c_ref, s_ref, q_ref, k_ref, v_ref):
        c, s = c_ref[...], s_ref[...]
        for j in range(qw // LANE):
            q_ref[:, j * LANE:(j + 1) * LANE] = _rope128(x_ref[:, j * LANE:(j + 1) * LANE], c, s).astype(q_ref.dtype)
        for j in range(kw // LANE):
            k_ref[:, j * LANE:(j + 1) * LANE] = _rope128(x_ref[:, qw + j * LANE:qw + (j + 1) * LANE], c, s).astype(k_ref.dtype)
        v_ref[...] = x_ref[:, qw + kw:].astype(v_ref.dtype)

    return _call(body, name, (n // tile,), [_rs(tile, w), _rs(tile, LANE), _rs(tile, LANE)],
                 [_rs(tile, qw), _rs(tile, kw), _rs(tile, kw)],
                 [S((n, qw), MXU_DTYPE), S((n, kw), MXU_DTYPE), S((n, kw), MXU_DTYPE)], sem=("parallel",))(qkv, cos, sin)


ATT_GROUP = ATT_HEADS // ATT_KV_HEADS


def _attn_mask(i):
    row = _iota((ATT_GROUP * WINDOW, 2 * WINDOW), 0) % WINDOW
    s = _iota((ATT_GROUP * WINDOW, 2 * WINDOW), 1)
    return (s > row) & (s <= row + WINDOW) & ((s >= WINDOW) | (i > 0))


def _stack_heads(ref, j, kh, lo):
    parts = []
    for t in range(ATT_GROUP):
        h = ATT_GROUP * j + t
        blk = ref[:, (h // 2) * LANE:(h // 2 + 1) * LANE]
        blk = jnp.where(lo if h % 2 == 0 else jnp.logical_not(lo), blk, jnp.zeros_like(blk))
        parts.append(blk if h % 2 == kh else pltpu.roll(blk, HEAD_DIM, 1))
    return jnp.concatenate(parts, axis=0)


def _unstack_heads(stacked, j, kh, lo, put):
    for t in range(0, ATT_GROUP, 2):
        h = ATT_GROUP * j + t
        even = stacked[t * WINDOW:(t + 1) * WINDOW, :]
        odd = stacked[(t + 1) * WINDOW:(t + 2) * WINDOW, :]
        even = even if kh == 0 else pltpu.roll(even, HEAD_DIM, 1)
        odd = odd if kh == 1 else pltpu.roll(odd, HEAD_DIM, 1)
        put(h // 2, jnp.where(lo, even, odd))


def _per_head_rows(ref, j):
    return jnp.concatenate([ref[:, ATT_GROUP * j + t:ATT_GROUP * j + t + 1] for t in range(ATT_GROUP)], axis=0)


def _per_head_scalar(ref, j):
    rows = _iota((ATT_GROUP * WINDOW, 1), 0) // WINDOW
    out = jnp.zeros((ATT_GROUP * WINDOW, 1), F32)
    for t in range(ATT_GROUP):
        out = out + jnp.where(rows == t, ref[:, ATT_GROUP * j + t:ATT_GROUP * j + t + 1], 0.0)
    return out


def attn_fwd(q, k, v, sinks, name):
    n, qw = q.shape
    kw = k.shape[1]
    nb = n // WINDOW
    scale = HEAD_DIM ** -0.5

    def body(q_ref, kc_ref, kp_ref, vc_ref, vp_ref, sk_ref, o_ref, lse_ref):
        i = pl.program_id(0)
        valid = _attn_mask(i)
        lo = _iota((WINDOW, LANE), 1) < HEAD_DIM
        k2 = jnp.concatenate([kp_ref[...], kc_ref[...]], axis=0)
        v2 = jnp.concatenate([vp_ref[...], vc_ref[...]], axis=0)
        lane1 = _iota((1, LANE), 1)
        lse = jnp.zeros((WINDOW, LANE), F32)

        def put_o(qb, val):
            o_ref[:, qb * LANE:(qb + 1) * LANE] = val.astype(o_ref.dtype)

        for j in range(ATT_KV_HEADS):
            kb, kh = j // 2, j % 2
            q4 = _stack_heads(q_ref, j, kh, lo)
            logits = jnp.where(valid, _mm(q4, k2[:, kb * LANE:(kb + 1) * LANE], NT) * scale, -1e30)
            sk = _per_head_scalar(sk_ref, j)
            m = jnp.maximum(jnp.max(logits, axis=-1, keepdims=True), sk)
            e = jnp.exp(logits - m)
            den = jnp.sum(e, axis=-1, keepdims=True) + jnp.exp(sk - m)
            lse4 = m + jnp.log(den)
            for t in range(ATT_GROUP):
                lse = lse + lse4[t * WINDOW:(t + 1) * WINDOW, :] * (lane1 == ATT_GROUP * j + t).astype(F32)
            _unstack_heads(_mm(e * (1.0 / den), v2[:, kb * LANE:(kb + 1) * LANE]), j, kh, lo, put_o)
        lse_ref[...] = lse

    return _call(body, name, (nb,),
                 [_rs(WINDOW, qw), _rs(WINDOW, kw), _rs(WINDOW, kw, 0, -1), _rs(WINDOW, kw), _rs(WINDOW, kw, 0, -1), _ps((1, LANE))],
                 [_rs(WINDOW, qw), _rs(WINDOW, LANE)], [S((n, qw), MXU_DTYPE), S((n, LANE), F32)],
                 sem=("parallel",))(q, k, k, v, v, sinks)


def attn_bwd(q, k, v, o, do, lse, sinks, name):
    n, qw = q.shape
    kw = k.shape[1]
    nb = n // WINDOW
    scale = HEAD_DIM ** -0.5

    def body(q_ref, kc_ref, kp_ref, vc_ref, vp_ref, o_ref, do_ref, lse_ref, sk_ref,
             dq_ref, dka_ref, dkb_ref, dva_ref, dvb_ref, dsk_ref):
        i = pl.program_id(0)
        valid = _attn_mask(i)
        lo = _iota((WINDOW, LANE), 1) < HEAD_DIM
        k2 = jnp.concatenate([kp_ref[...], kc_ref[...]], axis=0)
        v2 = jnp.concatenate([vp_ref[...], vc_ref[...]], axis=0)
        lane1 = _iota((1, LANE), 1)
        do_ = do_ref[...].astype(F32)
        delta = _mm01(do_ * o_ref[...].astype(F32), _head_indicator(qw, ATT_HEADS))
        dk2 = [jnp.zeros((2 * WINDOW, LANE), F32) for _ in range(kw // LANE)]
        dv2 = [jnp.zeros((2 * WINDOW, LANE), F32) for _ in range(kw // LANE)]
        dsk = jnp.zeros((1, LANE), F32)

        def put_dq(qb, val):
            dq_ref[:, qb * LANE:(qb + 1) * LANE] = val

        for j in range(ATT_KV_HEADS):
            kb, kh = j // 2, j % 2
            q4 = _stack_heads(q_ref, j, kh, lo)
            do4 = _stack_heads(do_ref, j, kh, lo)
            kk = k2[:, kb * LANE:(kb + 1) * LANE]
            vv = v2[:, kb * LANE:(kb + 1) * LANE]
            logits = jnp.where(valid, _mm(q4, kk, NT) * scale, -1e30)
            lse4 = _per_head_rows(lse_ref, j)
            p = jnp.exp(logits - lse4)
            dl = jnp.concatenate([delta[:, ATT_GROUP * j + t:ATT_GROUP * j + t + 1] for t in range(ATT_GROUP)], axis=0)
            ds = p * (_mm(do4, vv, NT) - dl) * scale
            sd = jnp.exp(_per_head_scalar(sk_ref, j) - lse4) * dl
            for t in range(ATT_GROUP):
                dsk = dsk - (jnp.sum(sd[t * WINDOW:(t + 1) * WINDOW, :], axis=0, keepdims=True)
                             * (lane1 == ATT_GROUP * j + t).astype(F32))
            _unstack_heads(_mm(ds, kk), j, kh, lo, put_dq)
            dk2[kb] = dk2[kb] + _mm(ds, q4, TN)
            dv2[kb] = dv2[kb] + _mm(p, do4, TN)
        for kb in range(kw // LANE):
            dkb_ref[:, kb * LANE:(kb + 1) * LANE] = dk2[kb][0:WINDOW, :]
            dka_ref[:, kb * LANE:(kb + 1) * LANE] = dk2[kb][WINDOW:, :]
            dvb_ref[:, kb * LANE:(kb + 1) * LANE] = dv2[kb][0:WINDOW, :]
            dva_ref[:, kb * LANE:(kb + 1) * LANE] = dv2[kb][WINDOW:, :]
        _acc(dsk_ref, i, dsk)

    return _call(body, name, (nb,),
                 [_rs(WINDOW, qw), _rs(WINDOW, kw), _rs(WINDOW, kw, 0, -1), _rs(WINDOW, kw), _rs(WINDOW, kw, 0, -1),
                  _rs(WINDOW, qw), _rs(WINDOW, qw), _rs(WINDOW, LANE), _ps((1, LANE))],
                 [_rs(WINDOW, qw)] + [_rs(WINDOW, kw)] * 4 + [_ps((1, LANE))],
                 [S((n, qw), F32)] + [S((n, kw), F32)] * 4 + [S((1, LANE), F32)],
                 sem=("arbitrary",))(q, k, k, v, v, o, do, lse, sinks)


def attn_grad_merge(dq, dka, dkb, dva, dvb, cos, sin, name):
    n, qw = dq.shape
    kw = dka.shape[1]
    nb = n // WINDOW
    w = qw + 2 * kw

    def body(dq_ref, dka_ref, dkb_ref, dva_ref, dvb_ref, c_ref, s_ref, o_ref, db_ref):
        i = pl.program_id(0)
        c, s = c_ref[...], -s_ref[...]
        nxt = (i < nb - 1).astype(F32)

        @pl.when(i == 0)
        def _():
            db_ref[...] = jnp.zeros_like(db_ref)

        def put(c0, val):
            o_ref[:, c0:c0 + val.shape[1]] = val.astype(o_ref.dtype)
            db_ref[:, c0:c0 + val.shape[1]] += jnp.sum(val, axis=0, keepdims=True)

        for j in range(qw // LANE):
            put(j * LANE, _rope128(dq_ref[:, j * LANE:(j + 1) * LANE], c, s))
        for j in range(kw // LANE):
            sl = slice(j * LANE, (j + 1) * LANE)
            put(qw + j * LANE, _rope128(dka_ref[:, sl] + dkb_ref[:, sl] * nxt, c, s))
        put(qw + kw, dva_ref[...] + dvb_ref[...] * nxt)

    return _call(body, name, (nb,),
                 [_rs(WINDOW, qw), _rs(WINDOW, kw), _rs(WINDOW, kw, 0, 1, nb), _rs(WINDOW, kw), _rs(WINDOW, kw, 0, 1, nb),
                  _rs(WINDOW, LANE), _rs(WINDOW, LANE)],
                 [_rs(WINDOW, w), _ps((1, w))], [S((n, w), MXU_DTYPE), S((1, w), F32)],
                 sem=("arbitrary",))(dq, dka, dkb, dva, dvb, cos, sin)


def _row(v):
    return v.reshape(1, -1)


def _pad_lanes(v, width=LANE):
    return jnp.pad(v.reshape(1, -1), ((0, 0), (0, width - v.size)))


class LayerWeights(dict):
    def __init__(self, small, fetch):
        super().__init__(small)
        self.fetch = fetch

    def need(self, k, after):
        if k not in self:
            self[k] = self.fetch(k, after)
        return self[k]


def ffn_fwd(h, g, w, keys, tag):
    xn, u, hm = swiglu_in(h, _row(g), w.need(keys[0], h), f"{tag}_in")
    return matmul(hm, w.need(keys[1], hm), "nn", f"{tag}_out", scale=0.5, res=h), (h, xn, u, hm)


class GradSink:
    ORDER = ("ffn1_w_out", "ffn2_w_out", "ple_gate_w", "att_w_o", "hyb_w_out", "ffn1_w_in", "ffn2_w_in", "att_w_qkv",
             "ple_proj_w", "hyb_w_in")

    def __init__(self, shard_shapes, bucket_of):
        self.where, rows = {}, {}
        for k in self.ORDER:
            n, r, c = shard_shapes[k]
            for li in range(n):
                layer = li if k in PER_LAYER else 2 * li + (0 if k in EVEN_ONLY else 1)
                rows_b = rows.setdefault(bucket_of(layer, _stage(k)), {})
                off = -(-rows_b.get(c, 0) // r) * r
                rows_b[c] = off + r
                self.where[k, li] = (bucket_of(layer, _stage(k)), c, "r" if SHARD_AXIS[k] == 1 else "c", off, r)
        self.bufs = {b: {c: lax.empty((N_CHIPS, r, c), F32) for c, r in rows_b.items()} for b, rows_b in rows.items()}

    def mm(self, k, li, a, b, name, scale=None, c0=0, paired=False):
        layer, c, kind, off, r = self.where[k, li]
        slot = Slot(self.bufs[layer][c], kind, r if kind == "r" else c, off, c0, paired)
        self.bufs[layer][c] = matmul(a, b, "tn", name, scale=scale, into=slot)

    def put(self, k, li, chip_major):
        layer, c = self.where[k, li][:2]
        self.bufs[layer][c] = chip_major

    def take(self, k, n, reduced):
        parts = []
        for li in range(n):
            layer, c, _, off, r = self.where[k, li]
            parts.append(reduced[layer][c][off:off + r])
        return jnp.stack(parts)


def ffn_bwd(dh, g, w_in, w_out, saved, tag, sink, keys, layer, after, colsum=False):
    h, xn, u, hm = saved
    sink.mm(keys[1], layer, hm, dh, f"{tag}_dwout", scale=0.5)
    du = swiglu_out_bwd(dh, w_out, u, after, f"{tag}_dhm")
    sink.mm(keys[0], layer, xn, du, f"{tag}_dwin", paired=True)
    outs = nt_rms_bwd(du, ColSharded(w_in.arr, paired=True), h, _row(g), dh, f"{tag}_dxn", colsum=colsum)
    return (outs[0], outs[1].reshape(-1)) + ((outs[2],) if colsum else ())


def _hyb_params(w):
    d = w["conv_dw_b"].size
    inner = SSM_HEADS * HEAD_DIM
    main = 3 * d + w["ssm_conv_b"].size
    return dict(
        w_main=w["hyb_w_in"][:, :main], w_dt=jnp.pad(w["hyb_w_in"][:, main:], ((0, 0), (0, LANE - SSM_HEADS))),
        cw=jnp.pad(w["conv_dw_w"], ((0, 32 - CONV_WIDTH), (0, 0))), cb=_row(w["conv_dw_b"]),
        lg=_row(w["conv_ln_g"]), lb=_row(w["conv_ln_b"]),
        sw=jnp.pad(w["ssm_conv_w"], ((0, 8 - SSM_CONV), (0, 0))), sb=_row(w["ssm_conv_b"]),
        dtb=_pad_lanes(w["ssm_dt_bias"]), al_row=_pad_lanes(w["ssm_a_log"]), al_col=w["ssm_a_log"].reshape(-1, 1),
        dfull=_row(jnp.repeat(w["ssm_d"], HEAD_DIM)), gamma=_row(w["ssm_norm"]), d=d, inner=inner, main=main)


def hyb_fwd(h, w, tag):
    w.need("hyb_w_in", h)
    q = _hyb_params(w)
    xn = rms_fwd(h, _row(w["norm_mix"]), f"{tag}_rms")
    proj = matmul(xn, q["w_main"], "nn", f"{tag}_in")
    dtr = matmul(xn, q["w_dt"], "nn", f"{tag}_in_dt")
    u, u1 = conv_group_fwd(proj, q["cw"], q["cb"], q["lg"], q["lb"], f"{tag}_conv")
    pre, xs, bc, dt = ssm_conv_fwd(proj, dtr, q["sw"], q["sb"], q["dtb"], f"{tag}_sconv")
    dtT = dt[:, :SSM_HEADS].T
    yssd, hs = ssd_fwd(xs, bc, dt, dtT, q["al_row"], q["al_col"], f"{tag}_ssd")
    y = ssm_gate_fwd(yssd, xs, proj, q["dfull"], q["gamma"], f"{tag}_gate")
    wo = w.need("hyb_w_out", u)
    h2 = matmul(u, wo[:q["d"]], "nn", f"{tag}_out_a", res=h)
    h2 = matmul(y, wo[q["d"]:], "nn", f"{tag}_out_b", res=h2)
    return h2, (h, xn, proj, u, u1, pre, xs, bc, dt, dtT, yssd, hs, y)


def hyb_bwd(dh, w, saved, tag, sink, layer):
    q = _hyb_params(w)
    h, xn, proj, u, u1, pre, xs, bc, dt, dtT, yssd, hs, y = saved
    du = matmul(dh, w["hyb_w_out"][:q["d"]], "nt", f"{tag}_du")
    dy3 = matmul(dh, w["hyb_w_out"][q["d"]:], "nt", f"{tag}_dy")
    sink.mm("hyb_w_out", layer, u, dh, f"{tag}_dwo_a", c0=0)
    sink.mm("hyb_w_out", layer, y, dh, f"{tag}_dwo_b", c0=N_CHIPS // 2)
    dproj, dcw, dcb, dlg, dlb = conv_group_bwd(du, u1, proj, q["cw"], q["lg"], q["lb"], f"{tag}_dconv")
    dyssd, dxs_skip, dproj, dgamma, dd = ssm_gate_bwd(dy3, yssd, xs, proj, q["dfull"], q["gamma"], dproj, f"{tag}_dgate")
    dxs, dbc, ddtr, dalog, ddtb = ssd_bwd(xs, bc, dt, dtT, q["al_row"], q["al_col"], hs, dyssd, dxs_skip, f"{tag}_dssd")
    dproj, dsw, dsb = ssm_conv_bwd(dxs, dbc, pre, proj, q["sw"], dproj, f"{tag}_dsconv")
    dw_in = jnp.concatenate([matmul(xn, dproj, "tn", f"{tag}_dwin"),
                             matmul(xn, ddtr, "tn", f"{tag}_dwin_dt")[:, :SSM_HEADS]], axis=1)
    sink.put("hyb_w_in", layer, _to_chip_major(dw_in, 1))
    dh2, dg = nt_rms_bwd(dproj, q["w_main"], h, _row(w["norm_mix"]), dh, f"{tag}_dxn", extra=(ddtr, q["w_dt"]))
    grads = dict(norm_mix=dg.reshape(-1), conv_dw_w=dcw[:CONV_WIDTH], conv_dw_b=dcb.reshape(-1),
                 conv_ln_g=dlg.reshape(-1), conv_ln_b=dlb.reshape(-1), ssm_conv_w=dsw[:SSM_CONV], ssm_conv_b=dsb.reshape(-1),
                 ssm_dt_bias=ddtb.reshape(-1), ssm_a_log=dalog.reshape(-1), ssm_d=dd.reshape(-1), ssm_norm=dgamma.reshape(-1))
    return dh2, grads


def rope_tables(n):
    half = HEAD_DIM // 2
    inv = ROPE_THETA ** (-jnp.arange(0, HEAD_DIM, 2, dtype=F32) / HEAD_DIM)
    ang = jnp.arange(n, dtype=F32)[:, None] * inv[None, :]
    cos, sin = jnp.cos(ang), jnp.sin(ang)
    reps = LANE // HEAD_DIM
    return jnp.tile(jnp.concatenate([cos, cos], axis=1), (1, reps)), jnp.tile(jnp.concatenate([-sin, sin], axis=1), (1, reps))


def att_fwd(h, w, tables, tag):
    cos, sin = tables
    xn = rms_fwd(h, _row(w["norm_mix"]), f"{tag}_rms")
    qkv = matmul(xn, w.need("att_w_qkv", h), "nn", f"{tag}_qkv", bias=_row(w["att_b_qkv"]))
    q, k, v = rope_fwd(qkv, cos, sin, f"{tag}_rope")
    sinks = _pad_lanes(w["att_sinks"])
    o, lse = attn_fwd(q, k, v, sinks, f"{tag}_attn")
    h2 = matmul(o, w.need("att_w_o", o), "nn", f"{tag}_o", bias=_row(w["att_b_o"]), res=h)
    return h2, (h, xn, q, k, v, o, lse, sinks)


def att_bwd(dh, dh_colsum, w, saved, tables, tag, sink, layer):
    cos, sin = tables
    h, xn, q, k, v, o, lse, sinks = saved
    do = matmul(dh, w["att_w_o"], "nt", f"{tag}_do")
    sink.mm("att_w_o", layer, o, dh, f"{tag}_dwo")
    dq, dka, dkb, dva, dvb, dsk = attn_bwd(q, k, v, o, do, lse, sinks, f"{tag}_dattn")
    dqkv, dbqkv = attn_grad_merge(dq, dka, dkb, dva, dvb, cos, sin, f"{tag}_drope")
    sink.mm("att_w_qkv", layer, xn, dqkv, f"{tag}_dwqkv")
    dh2, dg = nt_rms_bwd(dqkv, w["att_w_qkv"], h, _row(w["norm_mix"]), dh, f"{tag}_dxn")
    grads = dict(norm_mix=dg.reshape(-1), att_b_qkv=dbqkv.reshape(-1), att_sinks=dsk[0, :ATT_HEADS],
                 att_b_o=dh_colsum.reshape(-1))
    return dh2, grads


def ple_block_fwd(h, pe, w, tag):
    xn = rms_fwd(h, _row(w["ple_norm"]), f"{tag}_rms")
    gl = matmul(xn, w.need("ple_gate_w", h), "nn", f"{tag}_gate")
    pp = matmul(pe, w.need("ple_proj_w", h), "nn", f"{tag}_proj")
    return ple_fwd(h, gl, pp, f"{tag}_mix"), (h, xn, gl, pp, pe)


def ple_block_bwd(dh, w, saved, tag, sink, layer, after):
    h, xn, gl, pp, pe = saved
    dpp, dgl = ple_bwd(dh, gl, pp, after, f"{tag}_dmix")
    sink.mm("ple_proj_w", layer, pe, dpp, f"{tag}_dwp")
    sink.mm("ple_gate_w", layer, xn, dgl, f"{tag}_dwg")
    dh2, dg = nt_rms_bwd(dgl, w["ple_gate_w"], h, _row(w["ple_norm"]), dh, f"{tag}_dxn")
    return dh2, dict(ple_norm=dg.reshape(-1))


PER_LAYER = ("norm_ffn1", "ffn1_w_in", "ffn1_w_out", "norm_mix", "norm_ffn2", "ffn2_w_in", "ffn2_w_out",
             "ple_norm", "ple_gate_w", "ple_proj_w")
EVEN_ONLY = ("hyb_w_in", "conv_dw_w", "conv_dw_b", "conv_ln_g", "conv_ln_b", "ssm_conv_w", "ssm_conv_b",
             "ssm_dt_bias", "ssm_a_log", "ssm_d", "ssm_norm", "hyb_w_out")
ODD_ONLY = ("att_w_qkv", "att_b_qkv", "att_sinks", "att_w_o", "att_b_o")


def _layer_index(k, i):
    if k in PER_LAYER:
        return i
    if k in (EVEN_ONLY if i % 2 == 0 else ODD_ONLY):
        return i // 2
    return None


def _stage(k):
    return 0 if k.startswith("ffn1") else (2 if k.startswith(("ffn2", "ple")) else 1)


def trunk_fwd_bwd(x, pe, target, layers, final_norm, sink, stage_done):
    depth = len(layers)
    tables = rope_tables(x.shape[0])
    h = x
    saved = []
    for i, w in enumerate(layers):
        h, s1 = ffn_fwd(h, w["norm_ffn1"], w, ("ffn1_w_in", "ffn1_w_out"), f"l{i}_ffn1")
        if i % 2 == 0:
            h, s2 = hyb_fwd(h, w, f"l{i}_hyb")
        else:
            h, s2 = att_fwd(h, w, tables, f"l{i}_att")
        h, s3 = ffn_fwd(h, w["norm_ffn2"], w, ("ffn2_w_in", "ffn2_w_out"), f"l{i}_ffn2")
        h, s4 = ple_block_fwd(h, pe[i], w, f"l{i}_ple")
        saved.append((s1, s2, s3, s4))
    dh, dgf, loss = loss_head(h, _row(final_norm), target, "loss_head")
    grads = {}
    tie = dgf
    for i in reversed(range(depth)):
        w = layers[i]
        s1, s2, s3, s4 = saved[i]
        dh, g = ple_block_bwd(dh, w, s4, f"l{i}_ple", sink, i, tie)
        odd = i % 2 == 1
        out = ffn_bwd(dh, w["norm_ffn2"], w["ffn2_w_in"], w["ffn2_w_out"], s3, f"l{i}_ffn2", sink,
                      ("ffn2_w_in", "ffn2_w_out"), i, tie, colsum=odd)
        dh = out[0]
        g.update(norm_ffn2=out[1])
        if odd:
            dh, gm = att_bwd(dh, out[2], w, s2, tables, f"l{i}_att", sink, i // 2)
        else:
            dh, gm = hyb_bwd(dh, w, s2, f"l{i}_hyb", sink, i // 2)
        g.update(gm)
        tie = stage_done(i, 1, tie)
        out = ffn_bwd(dh, w["norm_ffn1"], w["ffn1_w_in"], w["ffn1_w_out"], s1, f"l{i}_ffn1", sink,
                      ("ffn1_w_in", "ffn1_w_out"), i, tie)
        dh = out[0]
        g.update(norm_ffn1=out[1])
        tie = stage_done(i, 0, tie)
        for k, v in g.items():
            grads.setdefault(k, []).insert(0, v)
    grads = {k: jnp.stack(v) for k, v in grads.items()}
    grads["final_norm"] = dgf.reshape(-1)
    return loss, dh, grads


def _me():
    return lax.axis_index("x"), lax.axis_index("y"), lax.axis_index("c")


def _flip(v, f):
    return 1 - v if f else v


def _remote(src, dst, send_sems, recv_sems, k, dev):
    return pltpu.make_async_remote_copy(src_ref=src, dst_ref=dst, send_sem=send_sems.at[k], recv_sem=recv_sems.at[k],
                                        device_id=dev, device_id_type=MESH)


CHIP_FLIPS = ((1, 0), (0, 1), (1, 1))
DEV_FLIPS = tuple((fx, fy, fc) for fx in (0, 1) for fy in (0, 1) for fc in (0, 1))[1:]


def all_gather_chips(xs, name):
    na = len(xs)
    halves = [x.shape[0] // 2 for x in xs]
    assert all(x.shape[0] % 2 == 0 for x in xs)

    def body(*refs):
        x_refs, out_refs = refs[:na], refs[na:2 * na]
        send_sems, recv_sems = refs[2 * na:]
        mx, my, mc = _me()
        chip = 2 * mx + my
        sib = (mx, my, 1 - mc)
        peers = [(_flip(mx, fx), _flip(my, fy)) for fx, fy in CHIP_FLIPS]

        def rows(a, ch, hc):
            return out_refs[a].at[ch, pl.ds(hc * halves[a], halves[a]), :]

        def src(a):
            return x_refs[a].at[pl.ds(mc * halves[a], halves[a]), :]

        first = [_remote(src(a), rows(a, chip, mc), send_sems, recv_sems, 6 * a + j, (px, py, mc))
                 for j, (px, py) in enumerate(peers) for a in range(na)]
        for cp in first:
            cp.start()
        passed = []
        for j, (px, py) in enumerate(peers):
            for a in range(na):
                landed = rows(a, 2 * px + py, mc)
                _remote(src(a), landed, send_sems, recv_sems, 6 * a + j, (px, py, mc)).wait_recv()
                fw = _remote(landed, landed, send_sems, recv_sems, 6 * a + 3 + j, sib)
                fw.start()
                passed.append(fw)
        for j, (px, py) in enumerate(peers):
            for a in range(na):
                _remote(src(a), rows(a, 2 * px + py, 1 - mc), send_sems, recv_sems, 6 * a + 3 + j, sib).wait_recv()
        for cp in first + passed:
            cp.wait_send()

    outs = pl.pallas_call(
        body, name=name, out_shape=[S((N_CHIPS,) + x.shape, x.dtype) for x in xs], in_specs=[ANY] * na, out_specs=[ANY] * na,
        scratch_shapes=[pltpu.SemaphoreType.DMA((6 * na,)), pltpu.SemaphoreType.DMA((6 * na,))])(*xs)
    chip = 2 * lax.axis_index("x") + lax.axis_index("y")
    return [lax.dynamic_update_slice_in_dim(o, x[None], chip, axis=0) for o, x in zip(outs, xs)]


HBM = pl.BlockSpec(memory_space=pltpu.HBM)
SEM = pl.BlockSpec(memory_space=pltpu.SEMAPHORE)
DATAFLOW = pltpu.SideEffectType.DATAFLOW_SIDE_EFFECTING


def gather_start(xs, lands, after, name):
    na = len(xs)

    def body(*refs):
        x_refs, land_refs = refs[:na], refs[na:2 * na]
        send_sems, recv_sems = refs[2 * na + 1], refs[2 * na + 2]
        token = refs[-1]
        mx, my, mc = _me()
        chip = 2 * mx + my
        for a in range(na):
            for j, (fx, fy) in enumerate(CHIP_FLIPS):
                _remote(x_refs[a], land_refs[a].at[chip], send_sems, recv_sems, 3 * a + j,
                        (_flip(mx, fx), _flip(my, fy), mc)).start()
        token[...] = jnp.zeros_like(token)

    outs = pl.pallas_call(
        body, name=name,
        out_shape=(pltpu.SemaphoreType.DMA((3 * na,)), pltpu.SemaphoreType.DMA((3 * na,)))
        + tuple(pltpu.HBM(x.shape, x.dtype) for x in xs) + tuple(pltpu.HBM(l.shape, l.dtype) for l in lands)
        + (S((8, LANE), F32),),
        in_specs=[HBM] * (2 * na) + [pl.BlockSpec(memory_space=pl.ANY)],
        out_specs=(SEM, SEM) + (HBM,) * (2 * na) + (pl.BlockSpec(memory_space=pltpu.VMEM),),
        input_output_aliases={a: 2 + a for a in range(2 * na)},
        compiler_params=pltpu.CompilerParams(has_side_effects=DATAFLOW),
    )(*[pltpu.with_memory_space_constraint(t, pltpu.HBM) for t in list(xs) + list(lands)], after)
    return outs[0], outs[1], list(outs[2:2 + na]), list(outs[2 + na:2 + 2 * na])


def gather_wait(send_sems, recv_sems, xs, lands, first, after, name):
    na = len(xs)

    def body(*refs):
        x_refs, land_refs = refs[:na], refs[na:2 * na]
        send_sems, recv_sems = refs[2 * na], refs[2 * na + 1]
        mx, my, mc = _me()
        for a in range(na):
            for j, (fx, fy) in enumerate(CHIP_FLIPS):
                px, py = _flip(mx, fx), _flip(my, fy)
                cp = _remote(x_refs[a], land_refs[a].at[2 * px + py], send_sems, recv_sems, 3 * (first + a) + j, (px, py, mc))
                cp.wait_send()
                cp.wait_recv()

    outs = pl.pallas_call(
        body, name=name,
        out_shape=tuple(pltpu.HBM(x.shape, x.dtype) for x in xs) + tuple(pltpu.HBM(l.shape, l.dtype) for l in lands),
        in_specs=[HBM] * (2 * na) + [SEM, SEM, pl.BlockSpec(memory_space=pl.ANY)], out_specs=(HBM,) * (2 * na),
        input_output_aliases={a: a for a in range(2 * na)},
        compiler_params=pltpu.CompilerParams(has_side_effects=DATAFLOW),
    )(*xs, *lands, send_sems, recv_sems, after)
    return list(outs[na:])


def all_gather_devices(v, name):
    r, l = v.shape

    def body(v_ref, out_ref, send_sems, recv_sems):
        mx, my, mc = _me()
        me = 4 * mx + 2 * my + mc
        peers = [(_flip(mx, fx), _flip(my, fy), _flip(mc, fc)) for fx, fy, fc in DEV_FLIPS]
        sends = [_remote(v_ref, out_ref.at[me], send_sems, recv_sems, j, p) for j, p in enumerate(peers)]
        for cp in sends:
            cp.start()
        for j, (px, py, pc) in enumerate(peers):
            _remote(v_ref, out_ref.at[4 * px + 2 * py + pc], send_sems, recv_sems, j, (px, py, pc)).wait_recv()
        for cp in sends:
            cp.wait_send()

    out = pl.pallas_call(
        body, name=name, out_shape=S((N_DEV, r, l), v.dtype), in_specs=[ANY], out_specs=ANY,
        scratch_shapes=[pltpu.SemaphoreType.DMA((7,)), pltpu.SemaphoreType.DMA((7,))])(v)
    me = 4 * lax.axis_index("x") + 2 * lax.axis_index("y") + lax.axis_index("c")
    return lax.dynamic_update_slice_in_dim(out, v[None], me, axis=0)


def sum_devices(g8, name):
    nd, r, l = g8.shape
    tile = r
    for t in (512, 256, 128, 64, 32, 16, 8):
        if r % t == 0:
            tile = t
            break

    def body(g_ref, o_ref):
        acc = g_ref[0]
        for d in range(1, nd):
            acc = acc + g_ref[d]
        o_ref[...] = acc

    return _call(body, name, (r // tile,), [pl.BlockSpec((nd, tile, l), lambda i: (0, i, 0))], _rs(tile, l), S((r, l), F32),
                 sem=("parallel",))(g8)


def exchange_halves(gs, name):
    na = len(gs)
    nch = gs[0].shape[0]

    def body(*refs):
        g_refs, out_refs = refs[:na], refs[na:2 * na]
        send_sems, recv_sems = refs[2 * na:]
        mx, my, mc = _me()
        sib = (mx, my, 1 - mc)
        cps = []
        for a in range(na):
            half = gs[a].shape[1] // 2
            for j in range(nch):
                cps.append(_remote(g_refs[a].at[j, pl.ds((1 - mc) * half, half), :], out_refs[a].at[j],
                                   send_sems, recv_sems, nch * a + j, sib))
        for cp in cps:
            cp.start()
        for cp in cps:
            cp.wait_recv()
        for cp in cps:
            cp.wait_send()

    return pl.pallas_call(
        body, name=name, out_shape=[S((nch, g.shape[1] // 2, g.shape[2]), g.dtype) for g in gs],
        in_specs=[ANY] * na, out_specs=[ANY] * na,
        scratch_shapes=[pltpu.SemaphoreType.DMA((nch * na,)), pltpu.SemaphoreType.DMA((nch * na,))])(*gs)


def add_halves(g4, got, name):
    nch, r, l = g4.shape
    half = r // 2
    tile = _pick_rows(half)
    nt = half // tile

    def body(g_ref, r_ref, a_ref, own_ref):
        j = pl.program_id(1)
        chip = 2 * lax.axis_index("x") + lax.axis_index("y")
        val = g_ref[0] + r_ref[0]
        a_ref[0] = val.astype(a_ref.dtype)

        @pl.when(j == chip)
        def _():
            own_ref[...] = val

    return pl.pallas_call(
        body, name=name, grid=(nt, nch),
        in_specs=[pl.BlockSpec((1, tile, l), lambda i, j: (j, lax.axis_index("c") * nt + i, 0)),
                  pl.BlockSpec((1, tile, l), lambda i, j: (j, i, 0))],
        out_specs=[pl.BlockSpec((1, tile, l), lambda i, j: (j, i, 0)), pl.BlockSpec((tile, l), lambda i, j: (i, 0))],
        out_shape=[S((nch, half, l), MXU_DTYPE), S((half, l), F32)],
        compiler_params=pltpu.CompilerParams(dimension_semantics=("parallel", "arbitrary"), vmem_limit_bytes=VMEM_LIMIT))(g4, got)


def _pick_rows(r, cap=512):
    for t in (512, 256, 128, 64, 32, 16):
        if t <= cap and r % t == 0:
            return t
    return r


def exchange_chips(parts, name):
    na = len(parts)

    def body(*refs):
        a_refs, out_refs = refs[:na], refs[na:2 * na]
        send_sems, recv_sems = refs[2 * na:]
        mx, my, mc = _me()
        peers = [(_flip(mx, fx), _flip(my, fy)) for fx, fy in CHIP_FLIPS]
        cps = [_remote(a_refs[a].at[2 * px + py], out_refs[a].at[j], send_sems, recv_sems, 3 * a + j, (px, py, mc))
               for j, (px, py) in enumerate(peers) for a in range(na)]
        for cp in cps:
            cp.start()
        for cp in cps:
            cp.wait_recv()
        for cp in cps:
            cp.wait_send()

    return pl.pallas_call(
        body, name=name, out_shape=[S((3,) + p.shape[1:], p.dtype) for p in parts], in_specs=[ANY] * na, out_specs=[ANY] * na,
        scratch_shapes=[pltpu.SemaphoreType.DMA((3 * na,)), pltpu.SemaphoreType.DMA((3 * na,))])(*parts)


def add_chips(own, got, name):
    h, l = own.shape
    tile = _pick_rows(h)

    def body(o_ref, g_ref, out_ref):
        out_ref[...] = ((o_ref[...] + g_ref[0].astype(F32)) + g_ref[1].astype(F32)) + g_ref[2].astype(F32)

    nt = h // tile
    return _call(body, name, (nt,), [_rs(tile, l), pl.BlockSpec((3, tile, l), lambda i: (0, i, 0))],
                 pl.BlockSpec((tile, l), lambda i: (lax.axis_index("c") * nt + i, 0)),
                 S((2 * h, l), F32), sem=("parallel",))(own, got)


def join_halves(bufs, name):
    na = len(bufs)

    def body(*refs):
        out_refs = refs[na:2 * na]
        send_sems, recv_sems = refs[2 * na:]
        mx, my, mc = _me()
        sib = (mx, my, 1 - mc)

        def half(a, hc):
            h = bufs[a].shape[0] // 2
            return out_refs[a].at[pl.ds(hc * h, h), :]

        cps = [_remote(half(a, mc), half(a, mc), send_sems, recv_sems, a, sib) for a in range(na)]
        for cp in cps:
            cp.start()
        for a in range(na):
            _remote(half(a, mc), half(a, 1 - mc), send_sems, recv_sems, a, sib).wait_recv()
        for cp in cps:
            cp.wait_send()

    return pl.pallas_call(
        body, name=name, out_shape=[S(b.shape, b.dtype) for b in bufs], in_specs=[ANY] * na, out_specs=[ANY] * na,
        input_output_aliases={a: a for a in range(na)},
        scratch_shapes=[pltpu.SemaphoreType.DMA((na,)), pltpu.SemaphoreType.DMA((na,))])(*bufs)


def exchange_chips_start(parts, name):
    na = len(parts)
    lands = [lax.empty((3,) + p.shape[1:], p.dtype) for p in parts]

    def body(*refs):
        a_refs, land_refs = refs[:na], refs[na:2 * na]
        send_sems, recv_sems = refs[2 * na], refs[2 * na + 1]
        mx, my, mc = _me()
        for j, (fx, fy) in enumerate(CHIP_FLIPS):
            px, py = _flip(mx, fx), _flip(my, fy)
            for a in range(na):
                _remote(a_refs[a].at[2 * px + py], land_refs[a].at[j], send_sems, recv_sems, 3 * a + j, (px, py, mc)).start()
        refs[-1][...] = jnp.zeros_like(refs[-1])

    outs = pl.pallas_call(
        body, name=name,
        out_shape=(pltpu.SemaphoreType.DMA((3 * na,)), pltpu.SemaphoreType.DMA((3 * na,)))
        + tuple(pltpu.HBM(t.shape, t.dtype) for t in list(parts) + lands) + (S((8, LANE), F32),),
        in_specs=[HBM] * (2 * na), out_specs=(SEM, SEM) + (HBM,) * (2 * na) + (pl.BlockSpec(memory_space=pltpu.VMEM),),
        input_output_aliases={a: 2 + a for a in range(2 * na)},
        compiler_params=pltpu.CompilerParams(has_side_effects=DATAFLOW),
    )(*[pltpu.with_memory_space_constraint(t, pltpu.HBM) for t in list(parts) + lands])
    return outs[0], outs[1], list(outs[2:2 + na]), list(outs[2 + na:2 + 2 * na]), outs[-1]


def exchange_chips_wait(send_sems, recv_sems, parts, lands, after, name):
    na = len(parts)

    def body(*refs):
        a_refs, land_refs = refs[:na], refs[na:2 * na]
        send_sems, recv_sems = refs[2 * na], refs[2 * na + 1]
        mx, my, mc = _me()
        for j, (fx, fy) in enumerate(CHIP_FLIPS):
            px, py = _flip(mx, fx), _flip(my, fy)
            for a in range(na):
                cp = _remote(a_refs[a].at[2 * px + py], land_refs[a].at[j], send_sems, recv_sems, 3 * a + j, (px, py, mc))
                cp.wait_send()
                cp.wait_recv()

    outs = pl.pallas_call(
        body, name=name, out_shape=tuple(pltpu.HBM(t.shape, t.dtype) for t in list(parts) + list(lands)),
        in_specs=[HBM] * (2 * na) + [SEM, SEM, pl.BlockSpec(memory_space=pl.ANY)], out_specs=(HBM,) * (2 * na),
        input_output_aliases={a: a for a in range(2 * na)},
        compiler_params=pltpu.CompilerParams(has_side_effects=DATAFLOW),
    )(*parts, *lands, send_sems, recv_sems, after)
    return list(outs[na:])


def reduce_begin(gs, tag):
    got = exchange_halves(gs, f"{tag}_d2d")
    sums = [add_halves(g, r, f"{tag}_add1_{i}") for i, (g, r) in enumerate(zip(gs, got))]
    return [own for _, own in sums], exchange_chips_start([a for a, _ in sums], f"{tag}_ici_start")


def reduce_end(state, after, tag):
    owns, (send_sems, recv_sems, parts, lands, _) = state
    got = exchange_chips_wait(send_sems, recv_sems, parts, lands, after, f"{tag}_ici_wait")
    return [add_chips(own, r, f"{tag}_add2_{i}") for i, (own, r) in enumerate(zip(owns, got))]


PACK_L = 1024
BIG_ROW_MULT = 512


def _pack(arrs, dtype, row_mult, lead=None):
    lead_shape = () if lead is None else arrs[0].shape[:lead]
    flat = jnp.concatenate([a.astype(dtype).reshape(lead_shape + (-1,)) for a in arrs], axis=-1)
    n = flat.shape[-1]
    unit = row_mult * PACK_L
    total = -(-n // unit) * unit
    flat = jnp.pad(flat, [(0, 0)] * len(lead_shape) + [(0, total - n)])
    return flat.reshape(lead_shape + (total // PACK_L, PACK_L))


def _unpack(packed, shapes, lead=None):
    lead_shape = () if lead is None else packed.shape[:lead]
    flat = packed.reshape(lead_shape + (-1,))
    out, off = [], 0
    for shp in shapes:
        n = int(np.prod(shp))
        out.append(flat[..., off:off + n].reshape(lead_shape + tuple(shp)))
        off += n
    return out


def _to_full(gathered, axis):
    t = jnp.moveaxis(gathered, 0, axis)
    shp = t.shape
    return t.reshape(shp[:axis] + (shp[axis] * shp[axis + 1],) + shp[axis + 2:])


def _to_chip_major(full, axis):
    shp = full.shape
    t = full.reshape(shp[:axis] + (N_CHIPS, shp[axis] // N_CHIPS) + shp[axis + 1:])
    return jnp.moveaxis(t, axis, 0)


WEIGHTS = ("norm_ffn1", "ffn1_w_in", "ffn1_w_out", "norm_mix", "norm_ffn2", "ffn2_w_in", "ffn2_w_out", "ple_norm",
           "ple_gate_w", "ple_proj_w", "hyb_w_in", "conv_dw_w", "conv_dw_b", "conv_ln_g", "conv_ln_b", "ssm_conv_w",
           "ssm_conv_b", "ssm_dt_bias", "ssm_a_log", "ssm_d", "ssm_norm", "hyb_w_out", "att_w_qkv", "att_b_qkv",
           "att_sinks", "att_w_o", "att_b_o", "final_norm")
SHARD_AXIS = dict(ffn1_w_in=2, ffn1_w_out=1, ffn2_w_in=2, ffn2_w_out=1, ple_gate_w=1, ple_proj_w=2, hyb_w_in=2,
                  conv_dw_w=2, ssm_conv_w=2, hyb_w_out=1, att_w_qkv=2, att_b_qkv=1, att_w_o=1, att_b_o=1)
BIG = ("ffn1_w_in", "ffn1_w_out", "ffn2_w_in", "ffn2_w_out", "ple_gate_w", "ple_proj_w", "hyb_w_in", "hyb_w_out",
       "att_w_qkv", "att_w_o")
ODD_WIDTH = "hyb_w_in"
BIG_FLAT = tuple(k for k in BIG if k != ODD_WIDTH)
SMALL_SHARDED = ("conv_dw_w", "ssm_conv_w", "att_b_qkv", "att_b_o")
SMALL = tuple(k for k in WEIGHTS if k not in BIG)


def _step(x, p, target, w, m, v):
    mx, my = lax.axis_index("x"), lax.axis_index("y")
    chip = 2 * mx + my

    depth = w["norm_ffn1"].shape[0]
    order = sorted([(k, i) for i in range(depth) for k in BIG if _layer_index(k, i) is not None],
                   key=lambda t: (t[1], _stage(t[0])))
    small_g = all_gather_devices(_pack([w[k] for k in SMALL_SHARDED], F32, 8), "gather_small")
    shards = [w[k][_layer_index(k, i)].astype(MXU_DTYPE) for k, i in order]
    lands = [lax.dynamic_update_slice_in_dim(lax.empty((N_CHIPS,) + s.shape, s.dtype), s[None], chip, axis=0) for s in shards]
    send_sems, recv_sems, shards, lands = gather_start(shards, lands, small_g, "gather_start")

    def fetch(i, k, after):
        p = order.index((k, i))
        g, = gather_wait(send_sems, recv_sems, [shards[p]], [lands[p]], p, after, f"gather_wait_l{i}_{k}")
        if k == ODD_WIDTH:
            return _to_full(g, 1)
        if SHARD_AXIS[k] == 2:
            return ColSharded(g)
        return g.reshape(-1, g.shape[-1])

    small_g = small_g[0::2]
    small_full = {k: _to_full(g, SHARD_AXIS[k])
                  for k, g in zip(SMALL_SHARDED, _unpack(small_g, [w[k].shape for k in SMALL_SHARDED], lead=1))}
    layers = [LayerWeights({k: small_full.get(k, w[k])[_layer_index(k, i)] for k in SMALL if _layer_index(k, i) is not None},
                           functools.partial(fetch, i)) for i in range(depth)]

    def bucket_of(layer, stage):
        return (layer, 0) if layer > 0 else (0, min(stage, 1))

    sink = GradSink({k: w[k].shape for k in BIG}, bucket_of)
    begun = {}

    def stage_done(i, stage, tie):
        b = bucket_of(i, stage)
        if stage > 0 and bucket_of(i, stage - 1) == b:
            return tie
        begun[b] = reduce_begin(list(sink.bufs[b].values()), f"grads_l{b[0]}_{b[1]}")
        return begun[b][1][-1]

    loss, dx, grads = trunk_fwd_bwd(x[0], p[:, 0], target[0], layers, w["final_norm"], sink, stage_done)

    results = {}

    def finish(buckets, after, tag):
        halves = {b: reduce_end(begun[b], after, f"grads_l{b[0]}_{b[1]}") for b in buckets}
        joined = iter(join_halves([h for b in buckets for h in halves[b]], f"grads_join_{tag}"))
        reduced = {b: {c: next(joined) for c in sink.bufs[b]} for b in buckets}
        last = after
        for (k, li), (b, c, _, off, _) in sink.where.items():
            if b in buckets:
                results[k] = adamw_layer(w[k], reduced[b][c], off, m[k], v[k], li, results.get(k), f"adamw_{k}_{li}")
                last = results[k][1]
        return last

    order_b = list(begun)
    started_last = begun[order_b[-1]][1][-1]
    finish(order_b[-1:], finish(order_b[:-1], started_last, "early") if len(order_b) > 1 else dx, "last")
    g_out = {k: results[k][0] for k in BIG}
    vec = _pack([loss[0:1, 0:1]] + [grads[k] for k in SMALL], F32, 8)
    vec = sum_devices(all_gather_devices(vec, "gather_vectors"), "sum_vectors")
    parts = _unpack(vec, [(1, 1)] + [grads[k].shape for k in SMALL])
    loss_out = parts[0].reshape(())
    for k, g in zip(SMALL, parts[1:]):
        if k in SHARD_AXIS:
            ax = SHARD_AXIS[k]
            g = lax.dynamic_slice_in_dim(g, chip * w[k].shape[ax], w[k].shape[ax], axis=ax)
        g_out[k] = g

    delta, new_m, new_v = ({k: results[k][j] for k in BIG} for j in (1, 2, 3))
    shapes = [w[k].shape for k in SMALL]
    packed = [_pack([src[k] for k in SMALL], F32, 8) for src in (w, g_out, m, v)]
    outs = adamw(*packed, "adamw_small")
    for dst, o in zip((delta, new_m, new_v), outs):
        for k, a in zip(SMALL, _unpack(o, shapes)):
            dst[k] = a
    return ((loss_out, dx[None]) + tuple(g_out[k] for k in WEIGHTS) + tuple(delta[k] for k in WEIGHTS)
            + tuple(new_m[k] for k in WEIGHTS) + tuple(new_v[k] for k in WEIGHTS))


def kernel(x, p, norm_ffn1, ffn1_w_in, ffn1_w_out, norm_mix, norm_ffn2, ffn2_w_in, ffn2_w_out, ple_norm, ple_gate_w, ple_proj_w, hyb_w_in, conv_dw_w, conv_dw_b, conv_ln_g, conv_ln_b, ssm_conv_w, ssm_conv_b, ssm_dt_bias, ssm_a_log, ssm_d, ssm_norm, hyb_w_out, att_w_qkv, att_b_qkv, att_sinks, att_w_o, att_b_o, final_norm, loss_target, m_norm_ffn1, m_ffn1_w_in, m_ffn1_w_out, m_norm_mix, m_norm_ffn2, m_ffn2_w_in, m_ffn2_w_out, m_ple_norm, m_ple_gate_w, m_ple_proj_w, m_hyb_w_in, m_conv_dw_w, m_conv_dw_b, m_conv_ln_g, m_conv_ln_b, m_ssm_conv_w, m_ssm_conv_b, m_ssm_dt_bias, m_ssm_a_log, m_ssm_d, m_ssm_norm, m_hyb_w_out, m_att_w_qkv, m_att_b_qkv, m_att_sinks, m_att_w_o, m_att_b_o, m_final_norm, v_norm_ffn1, v_ffn1_w_in, v_ffn1_w_out, v_norm_mix, v_norm_ffn2, v_ffn2_w_in, v_ffn2_w_out, v_ple_norm, v_ple_gate_w, v_ple_proj_w, v_hyb_w_in, v_conv_dw_w, v_conv_dw_b, v_conv_ln_g, v_conv_ln_b, v_ssm_conv_w, v_ssm_conv_b, v_ssm_dt_bias, v_ssm_a_log, v_ssm_d, v_ssm_norm, v_hyb_w_out, v_att_w_qkv, v_att_b_qkv, v_att_sinks, v_att_w_o, v_att_b_o, v_final_norm):
    given = locals()
    w = {k: given[k] for k in WEIGHTS}
    m = {k: given["m_" + k] for k in WEIGHTS}
    v = {k: given["v_" + k] for k in WEIGHTS}
    return _step(x, p, loss_target, w, m, v)
```

```python
import functools
import math

import numpy as np
import jax
import jax.numpy as jnp
from jax import lax
from jax.experimental import pallas as pl
from jax.experimental.pallas import tpu as pltpu

F32 = jnp.float32
BF16 = jnp.bfloat16
MXU_DTYPE = jnp.bfloat16
S = jax.ShapeDtypeStruct
MESH = pl.DeviceIdType.MESH

V7X_VMEM_BYTES = 64 * 2**20
VMEM_LIMIT = 48 * 2**20
LANE = 128

EPS = 1e-6
SSM_HEADS = 16
HEAD_DIM = 64
SSM_GROUPS = 2
SSM_STATE = 128
SSM_CONV = 4
CHUNK = 128
CONV_WIDTH = 31
ATT_HEADS = 16
ATT_KV_HEADS = 4
WINDOW = 128
ROPE_THETA = 10000.0
ADAM_LR = 0.001
ADAM_B1 = 0.9
ADAM_B2 = 0.999
ADAM_EPS = 1e-08
ADAM_WD = 0.01
ADAM_STEP = 10

N_CHIPS = 4
N_DEV = 8

NN = ((1,), (0,))
NT = ((1,), (1,))
TN = ((0,), (0,))


def _mm(a, b, dims=NN):
    return lax.dot_general(a.astype(MXU_DTYPE), b.astype(MXU_DTYPE), (dims, ((), ())), preferred_element_type=F32)


def _split3(a):
    hi = a.astype(BF16)
    r = a - hi.astype(F32)
    mid = r.astype(BF16)
    lo = (r - mid.astype(F32)).astype(BF16)
    return hi, mid, lo


def _mm01(a, onehot, dims=NN):
    o = onehot.astype(BF16)
    out = None
    for part in _split3(a):
        t = lax.dot_general(part, o, (dims, ((), ())), preferred_element_type=F32)
        out = t if out is None else out + t
    return out


def _01mm(onehot, a):
    o = onehot.astype(BF16)
    out = None
    for part in _split3(a):
        t = lax.dot_general(o, part, (NN, ((), ())), preferred_element_type=F32)
        out = t if out is None else out + t
    return out


def _sigmoid(x):
    return 0.5 * jnp.tanh(0.5 * x) + 0.5


def _softplus(x):
    return jnp.maximum(x, 0.0) + jnp.log(1.0 + jnp.exp(-jnp.abs(x)))


def _iota(shape, axis):
    return lax.broadcasted_iota(jnp.int32, shape, axis)


def _head_indicator(width, heads, transposed=False):
    per = width // heads
    if transposed:
        return (_iota((heads, width), 1) // per == _iota((heads, width), 0)).astype(F32)
    return (_iota((width, heads), 0) // per == _iota((width, heads), 1)).astype(F32)


def _acc(ref, i, val):
    @pl.when(i == 0)
    def _():
        ref[...] = val

    @pl.when(i > 0)
    def _():
        ref[...] += val


def _rs(tile, width, col=0, shift=0, n=None):
    if shift == 0:
        return pl.BlockSpec((tile, width), lambda i: (i, col))
    if shift < 0:
        return pl.BlockSpec((tile, width), lambda i: (jnp.maximum(i - 1, 0), col))
    return pl.BlockSpec((tile, width), lambda i: (jnp.minimum(i + 1, n - 1), col))


def _ps(shape):
    return pl.BlockSpec(shape, lambda i: (0,) * len(shape))


def _call(body, name, grid, in_specs, out_specs, out_shape, scratch=(), sem=None):
    return pl.pallas_call(
        body, name=name, grid=grid, in_specs=in_specs, out_specs=out_specs, out_shape=out_shape,
        scratch_shapes=list(scratch),
        compiler_params=pltpu.CompilerParams(dimension_semantics=sem, vmem_limit_bytes=VMEM_LIMIT))


def _row_tile(n, target):
    t = min(n, target)
    assert n % t == 0, (n, t)
    return t


def _pick_tile(dim, target):
    if dim <= target:
        return dim
    t = (int(1.4 * target) // LANE) * LANE
    while t >= LANE:
        if dim % t == 0:
            return t
        t -= LANE
    return dim


ANY = pl.BlockSpec(memory_space=pl.ANY)


def _paired(j):
    return (j % 2) * 2 + j // 2


class ColSharded:
    def __init__(self, arr, paired=False):
        self.arr, self.paired = arr, paired
        self.nch, self.rows, self.per = arr.shape
        self.shape = (self.rows, self.nch * self.per)

    def chip(self, j):
        return _paired(j) if self.paired else j


class Slot:
    def __init__(self, buf, kind, per, off, c0=0, paired=False):
        self.buf, self.kind, self.per, self.off, self.c0, self.paired = buf, kind, per, off, c0, paired

    def chip(self, j):
        return _paired(j) if self.paired else j


def matmul(a, b, mode, name, *, out_dtype=F32, scale=None, res=None, bias=None, into=None, tm=1024, tn=1024, tk=1024):
    bshape = b.shape
    if mode == "nn":
        (m, k), (k2, n) = a.shape, bshape
    elif mode == "nt":
        (m, k), (n, k2) = a.shape, bshape
    else:
        (k, m), (k2, n) = a.shape, bshape
    assert k == k2, (a.shape, bshape, mode)
    tm, tn, tk = _pick_tile(m, tm), _pick_tile(n, tn), _pick_tile(k, tk)
    if isinstance(b, ColSharded):
        if mode == "nn":
            tn = b.per
        else:
            assert mode == "nt"
            tk = b.per
    if into is not None:
        if into.kind == "c":
            tn = into.per
            assert into.off % tm == 0 and n == N_CHIPS * into.per
        else:
            tm = max(1, min(m, int(1.4 * 1024)) // into.per) * into.per
            assert m % tm == 0 and into.off % into.per == 0 and into.c0 % (tm // into.per) == 0
    nk = k // tk
    dims = {"nn": NN, "nt": NT, "tn": TN}[mode]
    a_spec = (pl.BlockSpec((tk, tm), lambda i, j, kk: (kk, i)) if mode == "tn"
              else pl.BlockSpec((tm, tk), lambda i, j, kk: (i, kk)))
    if isinstance(b, ColSharded):
        bchip = b.chip
        b_spec = (pl.BlockSpec((None, tk, tn), lambda i, j, kk: (bchip(j), kk, 0)) if mode == "nn"
                  else pl.BlockSpec((None, tn, tk), lambda i, j, kk: (bchip(kk), j, 0)))
        b = b.arr
    else:
        b_spec = (pl.BlockSpec((tn, tk), lambda i, j, kk: (j, kk)) if mode == "nt"
                  else pl.BlockSpec((tk, tn), lambda i, j, kk: (kk, j)))
    plain_o = pl.BlockSpec((tm, tn), lambda i, j, kk: (i, j))
    ins, in_specs = [a, b], [a_spec, b_spec]
    if bias is not None:
        ins.append(bias)
        in_specs.append(pl.BlockSpec((1, tn), lambda i, j, kk: (0, j)))
    if res is not None:
        ins.append(res)
        in_specs.append(plain_o)
    aliases = {}
    if into is None:
        o_spec, o_shape = plain_o, S((m, n), out_dtype)
    else:
        aliases = {len(ins): 0}
        ins.append(into.buf)
        in_specs.append(ANY)
        o_shape = S(into.buf.shape, into.buf.dtype)
        if into.kind == "c":
            ob, ochip = into.off // tm, into.chip
            o_spec = pl.BlockSpec((None, tm, tn), lambda i, j, kk: (ochip(j), ob + i, 0))
        else:
            q, ob = tm // into.per, into.off // into.per
            cb = into.c0 // q
            o_spec = pl.BlockSpec((q, into.per, tn), lambda i, j, kk: (cb + i, ob, j))

    def body(*refs):
        a_ref, b_ref = refs[0], refs[1]
        o_ref, acc_ref = refs[-2], refs[-1]
        kk = pl.program_id(2)

        @pl.when(kk == 0)
        def _():
            acc_ref[...] = jnp.zeros_like(acc_ref)

        acc_ref[...] += _mm(a_ref[...], b_ref[...], dims)

        @pl.when(kk == nk - 1)
        def _():
            out = acc_ref[...]
            if scale is not None:
                out = out * scale
            pos = 2
            if bias is not None:
                out = out + refs[pos][...]
                pos += 1
            if res is not None:
                out = out + refs[pos][...]
            o_ref[...] = out.astype(o_ref.dtype).reshape(o_ref.shape)

    return pl.pallas_call(
        body, name=name, grid=(m // tm, n // tn, nk), in_specs=in_specs, out_specs=o_spec, out_shape=o_shape,
        scratch_shapes=[pltpu.VMEM((tm, tn), F32)], input_output_aliases=aliases,
        compiler_params=pltpu.CompilerParams(dimension_semantics=("parallel", "parallel", "arbitrary"),
                                             vmem_limit_bytes=VMEM_LIMIT))(*ins)


def rms_fwd(h, g, name):
    n, d = h.shape
    tile = _row_tile(n, 512)

    def body(h_ref, g_ref, o_ref):
        x = h_ref[...]
        r = lax.rsqrt(jnp.mean(x * x, axis=-1, keepdims=True) + EPS)
        o_ref[...] = (x * r * g_ref[...]).astype(o_ref.dtype)

    return _call(body, name, (n // tile,), [_rs(tile, d), _ps((1, d))], _rs(tile, d), S((n, d), MXU_DTYPE),
                 sem=("parallel",))(h, g)


def _rms_bwd_math(x, g, dy):
    r = lax.rsqrt(jnp.mean(x * x, axis=-1, keepdims=True) + EPS)
    xh = x * r
    dg = jnp.sum(dy * xh, axis=0, keepdims=True)
    dxh = dy * g
    dx = r * (dxh - xh * jnp.mean(dxh * xh, axis=-1, keepdims=True))
    return dx, dg


def nt_rms_bwd(a, b, h, g, dh_in, name, extra=None, colsum=False, b_kd=False):
    n, k = a.shape
    d = h.shape[1]
    tm = _row_tile(n, 512)
    sharded = isinstance(b, ColSharded)
    tk = b.per if sharded else _pick_tile(k, 1024)
    nk = k // tk
    dims = NN if b_kd else NT
    if sharded:
        bchip = b.chip
        b_spec = pl.BlockSpec((None, d, tk), lambda i, kk: (bchip(kk), 0, 0))
        b = b.arr
    elif b_kd:
        b_spec = pl.BlockSpec((tk, d), lambda i, kk: (kk, 0))
    else:
        b_spec = pl.BlockSpec((d, tk), lambda i, kk: (0, kk))
    row = pl.BlockSpec((tm, d), lambda i, kk: (i, 0))
    vec = pl.BlockSpec((1, d), lambda i, kk: (0, 0))
    ins, in_specs = [a, b, h, g, dh_in], [pl.BlockSpec((tm, tk), lambda i, kk: (i, kk)), b_spec, row, vec, row]
    if extra is not None:
        k2 = extra[0].shape[1]
        ins += list(extra)
        in_specs += [pl.BlockSpec((tm, k2), lambda i, kk: (i, 0)),
                     pl.BlockSpec((k2, d) if b_kd else (d, k2), lambda i, kk: (0, 0))]
    n_in = len(ins)

    def body(*refs):
        a_ref, b_ref, h_ref, g_ref, dh_ref = refs[:5]
        o_ref, dg_ref = refs[n_in], refs[n_in + 1]
        acc_ref = refs[-1]
        i, kk = pl.program_id(0), pl.program_id(1)

        @pl.when(kk == 0)
        def _():
            acc_ref[...] = _mm(refs[5][...], refs[6][...], dims) if extra is not None else jnp.zeros_like(acc_ref)

        acc_ref[...] += _mm(a_ref[...], b_ref[...], dims)

        @pl.when(kk == nk - 1)
        def _():
            dx, dg = _rms_bwd_math(h_ref[...], g_ref[...], acc_ref[...])
            out = dh_ref[...] + dx
            o_ref[...] = out
            _acc(dg_ref, i, dg)
            if colsum:
                _acc(refs[n_in + 2], i, jnp.sum(out, axis=0, keepdims=True))

    n_vec = 2 if colsum else 1
    return pl.pallas_call(
        body, name=name, grid=(n // tm, nk), in_specs=in_specs, out_specs=[row] + [vec] * n_vec,
        out_shape=[S((n, d), F32)] + [S((1, d), F32)] * n_vec, scratch_shapes=[pltpu.VMEM((tm, d), F32)],
        compiler_params=pltpu.CompilerParams(dimension_semantics=("arbitrary", "arbitrary"), vmem_limit_bytes=VMEM_LIMIT),
    )(*ins)


def rms_bwd(h, g, dxn, dh_in, name, colsum=False):
    n, d = h.shape
    tile = _row_tile(n, 256)

    def body(h_ref, g_ref, dxn_ref, dh_ref, o_ref, dg_ref, *cs_ref):
        i = pl.program_id(0)
        dx, dg = _rms_bwd_math(h_ref[...], g_ref[...], dxn_ref[...].astype(F32))
        out = dh_ref[...] + dx
        o_ref[...] = out
        _acc(dg_ref, i, dg)
        if colsum:
            _acc(cs_ref[0], i, jnp.sum(out, axis=0, keepdims=True))

    outs = [S((n, d), F32), S((1, d), F32)] + ([S((1, d), F32)] if colsum else [])
    ospecs = [_rs(tile, d), _ps((1, d))] + ([_ps((1, d))] if colsum else [])
    return _call(body, name, (n // tile,), [_rs(tile, d), _ps((1, d)), _rs(tile, d), _rs(tile, d)], ospecs, outs,
                 sem=("arbitrary",))(h, g, dxn, dh_in)


def swiglu_in(h, g, w_in, name):
    n, d = h.shape
    per = w_in.per
    nj = w_in.nch // 2
    tile = _row_tile(n, 512)

    def body(h_ref, g_ref, wg_ref, wu_ref, xn_ref, u_ref, hm_ref):
        x = h_ref[...]
        r = lax.rsqrt(jnp.mean(x * x, axis=-1, keepdims=True) + EPS)
        xn = (x * r * g_ref[...]).astype(xn_ref.dtype)

        @pl.when(pl.program_id(1) == 0)
        def _():
            xn_ref[...] = xn

        a = _mm(xn, wg_ref[...])
        b = _mm(xn, wu_ref[...])
        u_ref[:, :per] = a.astype(u_ref.dtype)
        u_ref[:, per:] = b.astype(u_ref.dtype)
        hm_ref[...] = (a * _sigmoid(a) * b).astype(hm_ref.dtype)

    return pl.pallas_call(
        body, name=name, grid=(n // tile, nj),
        in_specs=[pl.BlockSpec((tile, d), lambda i, j: (i, 0)), pl.BlockSpec((1, d), lambda i, j: (0, 0)),
                  pl.BlockSpec((None, d, per), lambda i, j: (j, 0, 0)), pl.BlockSpec((None, d, per), lambda i, j: (nj + j, 0, 0))],
        out_specs=[pl.BlockSpec((tile, d), lambda i, j: (i, 0)), pl.BlockSpec((tile, 2 * per), lambda i, j: (i, j)),
                   pl.BlockSpec((tile, per), lambda i, j: (i, j))],
        out_shape=[S((n, d), MXU_DTYPE), S((n, 2 * nj * per), MXU_DTYPE), S((n, nj * per), MXU_DTYPE)],
        compiler_params=pltpu.CompilerParams(dimension_semantics=("parallel", "arbitrary"), vmem_limit_bytes=VMEM_LIMIT),
    )(h, g, w_in.arr, w_in.arr)


def swiglu_out_bwd(dh, w_out, u, after, name):
    n, d = dh.shape
    f = w_out.shape[0]
    per = u.shape[1] // 4
    nj = f // per
    tile = _row_tile(n, 512)

    def body(dh_ref, w_ref, u_ref, after_ref, du_ref):
        dm = 0.5 * _mm(dh_ref[...], w_ref[...], NT)
        a = u_ref[:, :per].astype(F32)
        b = u_ref[:, per:].astype(F32)
        s = _sigmoid(a)
        du_ref[:, :per] = (dm * b * s * (1.0 + a * (1.0 - s))).astype(du_ref.dtype)
        du_ref[:, per:] = (dm * a * s).astype(du_ref.dtype)

    return pl.pallas_call(
        body, name=name, grid=(n // tile, nj),
        in_specs=[pl.BlockSpec((tile, d), lambda i, j: (i, 0)), pl.BlockSpec((per, d), lambda i, j: (j, 0)),
                  pl.BlockSpec((tile, 2 * per), lambda i, j: (i, j)), ANY],
        out_specs=pl.BlockSpec((tile, 2 * per), lambda i, j: (i, j)),
        out_shape=S(u.shape, MXU_DTYPE),
        compiler_params=pltpu.CompilerParams(dimension_semantics=("parallel", "parallel"), vmem_limit_bytes=VMEM_LIMIT),
    )(dh, w_out, u, after)


def ple_fwd(h, gl, pp, name):
    n, d = h.shape
    tile = _row_tile(n, 512)

    def body(h_ref, gl_ref, pp_ref, o_ref):
        o_ref[...] = h_ref[...] + _sigmoid(gl_ref[...]) * pp_ref[...]

    return _call(body, name, (n // tile,), [_rs(tile, d)] * 3, _rs(tile, d), S((n, d), F32), sem=("parallel",))(h, gl, pp)


def ple_bwd(dh, gl, pp, after, name):
    n, d = dh.shape
    tile = _row_tile(n, 512)

    def body(dh_ref, gl_ref, pp_ref, after_ref, dpp_ref, dgl_ref):
        g = _sigmoid(gl_ref[...])
        dh_ = dh_ref[...]
        dpp_ref[...] = (dh_ * g).astype(dpp_ref.dtype)
        dgl_ref[...] = (dh_ * pp_ref[...] * g * (1.0 - g)).astype(dgl_ref.dtype)

    return _call(body, name, (n // tile,), [_rs(tile, d)] * 3 + [ANY], [_rs(tile, d)] * 2, [S((n, d), MXU_DTYPE)] * 2,
                 sem=("parallel",))(dh, gl, pp, after)


def loss_head(h, g, target, name):
    n, d = h.shape
    tile = _row_tile(n, 256)

    def body(h_ref, g_ref, t_ref, dh_ref, dg_ref, loss_ref):
        i = pl.program_id(0)
        x = h_ref[...]
        gg = g_ref[...]
        r = lax.rsqrt(jnp.mean(x * x, axis=-1, keepdims=True) + EPS)
        err = x * r * gg - t_ref[...]
        part = 0.5 * jnp.sum(jnp.mean(err * err, axis=-1, keepdims=True), axis=0, keepdims=True)
        dx, dg = _rms_bwd_math(x, gg, err * (1.0 / d))
        dh_ref[...] = dx
        _acc(dg_ref, i, dg)
        _acc(loss_ref, i, jnp.broadcast_to(part, (8, LANE)))

    return _call(body, name, (n // tile,), [_rs(tile, d), _ps((1, d)), _rs(tile, d)],
                 [_rs(tile, d), _ps((1, d)), _ps((8, LANE))], [S((n, d), F32), S((1, d), F32), S((8, LANE), F32)],
                 sem=("arbitrary",))(h, g, target)


def _adamw_math(w, g, m, v):
    c1 = np.float32(1.0 - ADAM_B1 ** ADAM_STEP)
    c2 = np.float32(1.0 - ADAM_B2 ** ADAM_STEP)
    mm = ADAM_B1 * m + (1.0 - ADAM_B1) * g
    vv = ADAM_B2 * v + (1.0 - ADAM_B2) * (g * g)
    return -ADAM_LR * ((mm / c1) / (jnp.sqrt(vv / c2) + ADAM_EPS) + ADAM_WD * w), mm, vv


def adamw_layer(w, pack, off, m, v, li, prev, name):
    n, r, c = w.shape

    def body(w_ref, g_ref, m_ref, v_ref, *refs):
        go_ref, d_ref, mo_ref, vo_ref = refs[-4:]
        g = g_ref[...]
        go_ref[...] = g
        d_ref[...], mo_ref[...], vo_ref[...] = _adamw_math(w_ref[...], g, m_ref[...], v_ref[...])

    if r % 8 == 0:
        tile = next(t for t in (512, 256, 128, 64, 32, 16, 8) if r % t == 0 and off % t == 0 and t * c * 4 <= 2**21)
        ob, steps = off // tile, r // tile
        blk = pl.BlockSpec((None, tile, c), lambda i: (li, i, 0))
        g_spec = pl.BlockSpec((tile, c), lambda i: (ob + i, 0))
    else:
        assert off == 0 and pack.shape[0] == r and c % (2 * LANE) == 0
        steps = c // (2 * LANE)
        blk = pl.BlockSpec((None, r, 2 * LANE), lambda i: (li, 0, i))
        g_spec = pl.BlockSpec((r, 2 * LANE), lambda i: (0, i))
    prev = list(prev) if prev is not None else []
    return pl.pallas_call(
        body, name=name, grid=(steps,),
        in_specs=[blk, g_spec, blk, blk] + [ANY] * len(prev),
        out_specs=[blk] * 4, out_shape=[S((n, r, c), F32)] * 4,
        input_output_aliases={4 + j: j for j in range(len(prev))},
        compiler_params=pltpu.CompilerParams(dimension_semantics=("parallel",), vmem_limit_bytes=VMEM_LIMIT),
    )(w, pack, m, v, *prev)


def adamw(w, g, m, v, name):
    r, c = w.shape
    tile = r
    for t in (512, 256, 128, 64, 32, 16, 8):
        if r % t == 0 and t * c * 4 <= 2**21:
            tile = t
            break

    def body(w_ref, g_ref, m_ref, v_ref, d_ref, mo_ref, vo_ref):
        d_ref[...], mo_ref[...], vo_ref[...] = _adamw_math(w_ref[...], g_ref[...], m_ref[...], v_ref[...])

    return _call(body, name, (r // tile,), [_rs(tile, c)] * 4, [_rs(tile, c)] * 3, [S((r, c), F32)] * 3,
                 sem=("parallel",))(w, g, m, v)


def _taps_fwd(sc, w_ref, width, halo, tile, acc):
    for k in range(width):
        o = halo - (width - 1) + k
        acc = acc + w_ref[k:k + 1, :] * sc[o:o + tile, :]
    return acc


def _taps_bwd_x(sc_d, w_ref, width, tile, acc):
    for k in range(width):
        o = (width - 1) - k
        acc = acc + w_ref[k:k + 1, :] * sc_d[o:o + tile, :]
    return acc


def _taps_bwd_w(dy, sc, dw_ref, width, halo, tile, i):
    @pl.when(i == 0)
    def _():
        dw_ref[...] = jnp.zeros_like(dw_ref)

    for k in range(width):
        o = halo - (width - 1) + k
        dw_ref[k:k + 1, :] += jnp.sum(dy * sc[o:o + tile, :], axis=0, keepdims=True)


def _ln_stats(x):
    mu = jnp.mean(x, axis=-1, keepdims=True)
    xc = x - mu
    r = lax.rsqrt(jnp.mean(xc * xc, axis=-1, keepdims=True) + EPS)
    return xc * r, r


def conv_group_fwd(proj, cw, cb, lg, lb, name):
    n = proj.shape[0]
    d = cw.shape[1]
    tile = _row_tile(n, 256)
    halo = 32

    def body(v_ref, g_ref, vp_ref, gp_ref, cw_ref, cb_ref, lg_ref, lb_ref, u_ref, u1_ref, sc):
        i = pl.program_id(0)
        first = (i > 0).astype(F32)
        sc[0:halo, :] = vp_ref[tile - halo:, :] * _sigmoid(gp_ref[tile - halo:, :]) * first
        sc[halo:, :] = v_ref[...] * _sigmoid(g_ref[...])
        u1 = _taps_fwd(sc, cw_ref, CONV_WIDTH, halo, tile, jnp.zeros((tile, d), F32) + cb_ref[...])
        u1_ref[...] = u1
        xh, _ = _ln_stats(u1)
        y = xh * lg_ref[...] + lb_ref[...]
        u_ref[...] = (y * _sigmoid(y)).astype(u_ref.dtype)

    return _call(body, name, (n // tile,),
                 [_rs(tile, d, 0), _rs(tile, d, 1), _rs(tile, d, 0, -1), _rs(tile, d, 1, -1),
                  _ps(cw.shape), _ps((1, d)), _ps((1, d)), _ps((1, d))],
                 [_rs(tile, d), _rs(tile, d)], [S((n, d), MXU_DTYPE), S((n, d), F32)],
                 scratch=[pltpu.VMEM((halo + tile, d), F32)], sem=("arbitrary",))(proj, proj, proj, proj, cw, cb, lg, lb)


def conv_group_bwd(du, u1, proj, cw, lg, lb, name):
    n = proj.shape[0]
    d = cw.shape[1]
    tile = _row_tile(n, 256)
    halo = 32
    nt = n // tile

    def body(du_ref, dun_ref, u1_ref, u1n_ref, v_ref, g_ref, vp_ref, gp_ref, cw_ref, lg_ref, lb_ref,
             dp_ref, dcw_ref, dcb_ref, dlg_ref, dlb_ref, sc, sc_d):
        i = pl.program_id(0)

        def ln_swish_bwd(dy_, u1_):
            xh, r = _ln_stats(u1_)
            y = xh * lg_ref[...] + lb_ref[...]
            s = _sigmoid(y)
            dyy = dy_ * s * (1.0 + y * (1.0 - s))
            dxh = dyy * lg_ref[...]
            dx = r * (dxh - jnp.mean(dxh, axis=-1, keepdims=True) - xh * jnp.mean(dxh * xh, axis=-1, keepdims=True))
            return dx, jnp.sum(dyy * xh, axis=0, keepdims=True), jnp.sum(dyy, axis=0, keepdims=True)

        du1, dlg, dlb = ln_swish_bwd(du_ref[...].astype(F32), u1_ref[...])
        du1n, _, _ = ln_swish_bwd(dun_ref[0:halo, :].astype(F32), u1n_ref[0:halo, :])
        sc_d[0:tile, :] = du1
        sc_d[tile:, :] = du1n * (i < nt - 1).astype(F32)
        sig = _sigmoid(g_ref[...])
        val = v_ref[...]
        sc[0:halo, :] = vp_ref[tile - halo:, :] * _sigmoid(gp_ref[tile - halo:, :]) * (i > 0).astype(F32)
        sc[halo:, :] = val * sig
        du0 = _taps_bwd_x(sc_d, cw_ref, CONV_WIDTH, tile, jnp.zeros((tile, d), F32))
        _taps_bwd_w(du1, sc, dcw_ref, CONV_WIDTH, halo, tile, i)
        _acc(dcb_ref, i, jnp.sum(du1, axis=0, keepdims=True))
        _acc(dlg_ref, i, dlg)
        _acc(dlb_ref, i, dlb)
        dp_ref[:, :d] = (du0 * sig).astype(dp_ref.dtype)
        dp_ref[:, d:] = (du0 * val * sig * (1.0 - sig)).astype(dp_ref.dtype)

    return _call(body, name, (nt,),
                 [_rs(tile, d), _rs(tile, d, 0, 1, nt), _rs(tile, d), _rs(tile, d, 0, 1, nt),
                  _rs(tile, d, 0), _rs(tile, d, 1), _rs(tile, d, 0, -1), _rs(tile, d, 1, -1),
                  _ps(cw.shape), _ps((1, d)), _ps((1, d))],
                 [_rs(tile, 2 * d), _ps(cw.shape), _ps((1, d)), _ps((1, d)), _ps((1, d))],
                 [S((n, proj.shape[1]), MXU_DTYPE), S(cw.shape, F32), S((1, d), F32), S((1, d), F32), S((1, d), F32)],
                 scratch=[pltpu.VMEM((halo + tile, d), F32), pltpu.VMEM((tile + halo, d), F32)],
                 sem=("arbitrary",))(du, du, u1, u1, proj, proj, proj, proj, cw, lg, lb)


def ssm_conv_fwd(proj, dtr, sw, sb, dtb, name):
    n = proj.shape[0]
    w = sw.shape[1]
    inner = SSM_HEADS * HEAD_DIM
    tile = _row_tile(n, 256)
    halo = 8

    def body(x_ref, xp_ref, dtr_ref, sw_ref, sb_ref, dtb_ref, pre_ref, xs_ref, bc_ref, dt_ref, sc):
        i = pl.program_id(0)
        sc[0:halo, :] = xp_ref[tile - halo:, :] * (i > 0).astype(F32)
        sc[halo:, :] = x_ref[...]
        pre = _taps_fwd(sc, sw_ref, SSM_CONV, halo, tile, jnp.zeros((tile, w), F32) + sb_ref[...])
        pre_ref[...] = pre
        act = pre * _sigmoid(pre)
        xs_ref[...] = act[:, :inner]
        bc_ref[...] = act[:, inner:]
        dt = _softplus(dtr_ref[...] + dtb_ref[...])
        dt_ref[...] = jnp.where(_iota(dt.shape, 1) < SSM_HEADS, dt, 0.0)

    return _call(body, name, (n // tile,),
                 [_rs(tile, w, 2), _rs(tile, w, 2, -1), _rs(tile, LANE), _ps(sw.shape), _ps((1, w)), _ps((1, LANE))],
                 [_rs(tile, w), _rs(tile, inner), _rs(tile, w - inner), _rs(tile, LANE)],
                 [S((n, w), F32), S((n, inner), F32), S((n, w - inner), F32), S((n, LANE), F32)],
                 scratch=[pltpu.VMEM((halo + tile, w), F32)], sem=("arbitrary",))(proj, proj, dtr, sw, sb, dtb)


def ssm_conv_bwd(dxs, dbc, pre, proj, sw, dproj, name):
    n = proj.shape[0]
    w = sw.shape[1]
    inner = SSM_HEADS * HEAD_DIM
    tile = _row_tile(n, 256)
    halo = 8
    nt = n // tile

    def body(dxs_ref, dxsn_ref, dbc_ref, dbcn_ref, pre_ref, pren_ref, x_ref, xp_ref, sw_ref, dp_in_ref,
             dx_ref, dsw_ref, dsb_ref, sc, sc_d):
        i = pl.program_id(0)

        def silu_bwd(d_, p_):
            s = _sigmoid(p_)
            return d_ * s * (1.0 + p_ * (1.0 - s))

        sc_d[0:tile, :inner] = silu_bwd(dxs_ref[...], pre_ref[:, :inner])
        sc_d[0:tile, inner:] = silu_bwd(dbc_ref[...], pre_ref[:, inner:])
        last = (i < nt - 1).astype(F32)
        sc_d[tile:, :inner] = silu_bwd(dxsn_ref[0:halo, :], pren_ref[0:halo, :inner]) * last
        sc_d[tile:, inner:] = silu_bwd(dbcn_ref[0:halo, :], pren_ref[0:halo, inner:]) * last
        sc[0:halo, :] = xp_ref[tile - halo:, :] * (i > 0).astype(F32)
        sc[halo:, :] = x_ref[...]
        dpre = sc_d[0:tile, :]
        dx_ref[...] = _taps_bwd_x(sc_d, sw_ref, SSM_CONV, tile, jnp.zeros((tile, w), F32)).astype(dx_ref.dtype)
        _taps_bwd_w(dpre, sc, dsw_ref, SSM_CONV, halo, tile, i)
        _acc(dsb_ref, i, jnp.sum(dpre, axis=0, keepdims=True))

    return pl.pallas_call(
        body, name=name, grid=(nt,),
        in_specs=[_rs(tile, inner), _rs(tile, inner, 0, 1, nt), _rs(tile, w - inner), _rs(tile, w - inner, 0, 1, nt),
                  _rs(tile, w), _rs(tile, w, 0, 1, nt), _rs(tile, w, 2), _rs(tile, w, 2, -1), _ps(sw.shape), ANY],
        out_specs=[_rs(tile, w, 2), _ps(sw.shape), _ps((1, w))],
        out_shape=[S(dproj.shape, dproj.dtype), S(sw.shape, F32), S((1, w), F32)],
        scratch_shapes=[pltpu.VMEM((halo + tile, w), F32), pltpu.VMEM((tile + halo, w), F32)],
        input_output_aliases={9: 0},
        compiler_params=pltpu.CompilerParams(dimension_semantics=("arbitrary",), vmem_limit_bytes=VMEM_LIMIT),
    )(dxs, dxs, dbc, dbc, pre, pre, proj, proj, sw, dproj)


def _ssd_prologue(dt_ref, dtT_ref, al_ref, alc_ref):
    row = _iota((CHUNK, CHUNK), 0)
    col = _iota((CHUNK, CHUNK), 1)
    dt = dt_ref[:, :SSM_HEADS]
    a_row = -jnp.exp(al_ref[:, :SSM_HEADS])
    a_col = -jnp.exp(alc_ref[...])
    cs = _01mm((row >= col).astype(F32), dt * a_row)
    csT = _mm01(dtT_ref[...] * a_col, (row <= col).astype(F32))
    return dt, a_row, cs, csT, row, col


def _decay(cs, csT, h, row, col):
    lm = jnp.exp(jnp.where(row >= col, cs[:, h:h + 1] - csT[h:h + 1, :], -1e30))
    lmT = jnp.exp(jnp.where(col >= row, csT[h:h + 1, :] - cs[:, h:h + 1], -1e30))
    return lm, lmT


def ssd_fwd(xs, bc, dt, dtT, alog_row, alog_col, name):
    n, width = xs.shape
    nc = n // CHUNK
    gw = width // SSM_GROUPS
    hpg = SSM_HEADS // SSM_GROUPS
    ns = SSM_STATE

    def body(xs_ref, bc_ref, dt_ref, dtT_ref, al_ref, alc_ref, y_ref, hs_ref, h_sc):
        i = pl.program_id(0)

        @pl.when(i == 0)
        def _():
            h_sc[...] = jnp.zeros_like(h_sc)

        dt, a_row, cs, csT, row, col = _ssd_prologue(dt_ref, dtT_ref, al_ref, alc_ref)
        indT = _head_indicator(width, SSM_HEADS, transposed=True)
        dt_full = _mm01(dt, indT)
        e_full = jnp.exp(_mm01(cs, indT))
        dte_full = jnp.exp(_mm01(cs[CHUNK - 1:CHUNK, :] - cs, indT))
        xt = xs_ref[...] * dt_full
        hs_ref[0] = h_sc[...]
        lo = _iota((CHUNK, 2 * HEAD_DIM), 1) < HEAD_DIM
        for g in range(SSM_GROUPS):
            bg = bc_ref[:, g * ns:(g + 1) * ns]
            cg = bc_ref[:, (SSM_GROUPS + g) * ns:(SSM_GROUPS + g + 1) * ns]
            gm = _mm(cg, bg, NT)
            hg = h_sc[g * gw:(g + 1) * gw, :]
            yoff = e_full[:, g * gw:(g + 1) * gw] * _mm(cg, hg, NT)
            for pr in range(hpg // 2):
                h0 = g * hpg + 2 * pr
                c0 = h0 * HEAD_DIM
                xp = xt[:, c0:c0 + 2 * HEAD_DIM]
                m0 = gm * _decay(cs, csT, h0, row, col)[0]
                m1 = gm * _decay(cs, csT, h0 + 1, row, col)[0]
                yd = jnp.where(lo, _mm(m0, xp), _mm(m1, xp))
                y_ref[:, c0:c0 + 2 * HEAD_DIM] = yd + yoff[:, 2 * pr * HEAD_DIM:(2 * pr + 2) * HEAD_DIM]
            sg = _mm(xt[:, g * gw:(g + 1) * gw] * dte_full[:, g * gw:(g + 1) * gw], bg, TN)
            for hh in range(hpg):
                h = g * hpg + hh
                r0 = h * HEAD_DIM
                h_sc[r0:r0 + HEAD_DIM, :] = (h_sc[r0:r0 + HEAD_DIM, :] * jnp.exp(csT[h:h + 1, CHUNK - 1:CHUNK])
                                             + sg[hh * HEAD_DIM:(hh + 1) * HEAD_DIM, :])

    bcw = bc.shape[1]
    return _call(body, name, (nc,),
                 [_rs(CHUNK, width), _rs(CHUNK, bcw), _rs(CHUNK, LANE), pl.BlockSpec((SSM_HEADS, CHUNK), lambda i: (0, i)),
                  _ps((1, LANE)), _ps((SSM_HEADS, 1))],
                 [_rs(CHUNK, width), pl.BlockSpec((1, width, ns), lambda i: (i, 0, 0))],
                 [S((n, width), F32), S((nc, width, ns), F32)],
                 scratch=[pltpu.VMEM((width, ns), F32)], sem=("arbitrary",))(xs, bc, dt, dtT, alog_row, alog_col)


def ssd_bwd(xs, bc, dt, dtT, alog_row, alog_col, hs, dy, dxs_skip, name):
    n, width = xs.shape
    nc = n // CHUNK
    gw = width // SSM_GROUPS
    hpg = SSM_HEADS // SSM_GROUPS
    ns = SSM_STATE
    bcw = bc.shape[1]

    def body(xs_ref, bc_ref, dt_ref, dtT_ref, al_ref, alc_ref, hs_ref, dy_ref, skip_ref,
             dxs_ref, dbc_ref, ddtr_ref, dal_ref, ddtb_ref, dh_sc, dxt_sc):
        i = pl.program_id(0)

        @pl.when(i == 0)
        def _():
            dh_sc[...] = jnp.zeros_like(dh_sc)

        dt, a_row, cs, csT, row, col = _ssd_prologue(dt_ref, dtT_ref, al_ref, alc_ref)
        indT = _head_indicator(width, SSM_HEADS, transposed=True)
        ind = _head_indicator(width, SSM_HEADS)
        dt_full = _mm01(dt, indT)
        e_full = jnp.exp(_mm01(cs, indT))
        cs_last = cs[CHUNK - 1:CHUNK, :]
        dte = jnp.exp(cs_last - cs)
        dte_full = _mm01(dte, indT)
        xs_ = xs_ref[...]
        xt = xs_ * dt_full
        dy_ = dy_ref[...]
        hmat = hs_ref[0]
        ds = dh_sc[...]
        lo = _iota((CHUNK, 2 * HEAD_DIM), 1) < HEAD_DIM
        head_lane = _iota((1, SSM_HEADS), 1)
        dcs = jnp.zeros((CHUNK, SSM_HEADS), F32)
        ddte = jnp.zeros((CHUNK, SSM_HEADS), F32)
        for g in range(SSM_GROUPS):
            sl = slice(g * gw, (g + 1) * gw)
            bg = bc_ref[:, g * ns:(g + 1) * ns]
            cg = bc_ref[:, (SSM_GROUPS + g) * ns:(SSM_GROUPS + g + 1) * ns]
            gm = _mm(cg, bg, NT)
            gmT = _mm(bg, cg, NT)
            hg = hmat[sl, :]
            dsg = ds[sl, :]
            dyg = dy_[:, sl]
            xtg = xt[:, sl]
            yoff = e_full[:, sl] * _mm(cg, hg, NT)
            edy = e_full[:, sl] * dyg
            bds = _mm(bg, dsg, NT)
            dxt_g = dte_full[:, sl] * bds
            ddte = ddte + _mm01(xtg * bds, ind[sl, :])
            dcs = dcs + _mm01(dyg * yoff, ind[sl, :])
            db = _mm(xtg * dte_full[:, sl], dsg)
            dc = _mm(edy, hg)
            dhc = _mm(edy, cg, TN)
            dgs = jnp.zeros((CHUNK, CHUNK), F32)
            dgTs = jnp.zeros((CHUNK, CHUNK), F32)
            for pr in range(hpg // 2):
                h0 = g * hpg + 2 * pr
                c0 = 2 * pr * HEAD_DIM
                xp = xtg[:, c0:c0 + 2 * HEAD_DIM]
                dyp = dyg[:, c0:c0 + 2 * HEAD_DIM]
                rr = []
                for h, half in ((h0, lo), (h0 + 1, jnp.logical_not(lo))):
                    lm, lmT = _decay(cs, csT, h, row, col)
                    xm = jnp.where(half, xp, 0.0)
                    dm = _mm(dyp, xm, NT)
                    dmT = _mm(xm, dyp, NT)
                    mT = gmT * lmT
                    z = jnp.sum(dm * (gm * lm), axis=1, keepdims=True) - jnp.sum(dmT * mT, axis=1, keepdims=True)
                    dcs = dcs + z * (head_lane == h).astype(F32)
                    dgs = dgs + dm * lm
                    dgTs = dgTs + dmT * lmT
                    rr.append(_mm(mT, dyp))
                dxt_sc[:, g * gw + c0:g * gw + c0 + 2 * HEAD_DIM] = jnp.where(lo, rr[0], rr[1]) + dxt_g[:, c0:c0 + 2 * HEAD_DIM]
            dbc_ref[:, g * ns:(g + 1) * ns] = db + _mm(dgTs, cg)
            dbc_ref[:, (SSM_GROUPS + g) * ns:(SSM_GROUPS + g + 1) * ns] = dc + _mm(dgs, bg)
            for hh in range(hpg):
                h = g * hpg + hh
                r0 = h * HEAD_DIM
                dh_sc[r0:r0 + HEAD_DIM, :] = (dhc[hh * HEAD_DIM:(hh + 1) * HEAD_DIM, :]
                                              + jnp.exp(csT[h:h + 1, CHUNK - 1:CHUNK]) * ds[r0:r0 + HEAD_DIM, :])
        t = ddte * dte
        per_head = jnp.sum(jnp.sum(ds * hmat, axis=1, keepdims=True) * ind, axis=0, keepdims=True)
        last_add = jnp.sum(t, axis=0, keepdims=True) + jnp.exp(cs_last) * per_head
        dcs = dcs - t + jnp.where(_iota((CHUNK, SSM_HEADS), 0) == CHUNK - 1, last_add, 0.0)
        dadt = _01mm((row <= col).astype(F32), dcs)
        dxt = dxt_sc[...]
        ddt = dadt * a_row + _mm01(dxt * xs_, ind)
        dxs_ref[...] = dxt * dt_full + skip_ref[...]
        ddtr = ddt * (1.0 - jnp.exp(-dt))
        ddtr_ref[...] = jnp.zeros_like(ddtr_ref)
        ddtr_ref[:, :SSM_HEADS] = ddtr.astype(ddtr_ref.dtype)
        _acc(dal_ref, i, jnp.sum(dadt * dt, axis=0, keepdims=True) * a_row)
        _acc(ddtb_ref, i, jnp.sum(ddtr, axis=0, keepdims=True))

    rev = lambda i: (nc - 1 - i, 0)
    return _call(body, name, (nc,),
                 [pl.BlockSpec((CHUNK, width), rev), pl.BlockSpec((CHUNK, bcw), rev), pl.BlockSpec((CHUNK, LANE), rev),
                  pl.BlockSpec((SSM_HEADS, CHUNK), lambda i: (0, nc - 1 - i)), _ps((1, LANE)), _ps((SSM_HEADS, 1)),
                  pl.BlockSpec((1, width, ns), lambda i: (nc - 1 - i, 0, 0)), pl.BlockSpec((CHUNK, width), rev),
                  pl.BlockSpec((CHUNK, width), rev)],
                 [pl.BlockSpec((CHUNK, width), rev), pl.BlockSpec((CHUNK, bcw), rev), pl.BlockSpec((CHUNK, LANE), rev),
                  _ps((1, SSM_HEADS)), _ps((1, SSM_HEADS))],
                 [S((n, width), F32), S((n, bcw), F32), S((n, LANE), MXU_DTYPE), S((1, SSM_HEADS), F32), S((1, SSM_HEADS), F32)],
                 scratch=[pltpu.VMEM((width, ns), F32), pltpu.VMEM((CHUNK, width), F32)],
                 sem=("arbitrary",))(xs, bc, dt, dtT, alog_row, alog_col, hs, dy, dxs_skip)


def ssm_gate_fwd(yssd, xs, proj, dfull, gamma, name):
    n, d = yssd.shape
    tile = _row_tile(n, 256)
    gw = d // SSM_GROUPS

    def body(y_ref, xs_ref, z_ref, df_ref, gm_ref, o_ref):
        z = z_ref[...]
        y2 = (y_ref[...] + df_ref[...] * xs_ref[...]) * (z * _sigmoid(z))
        for g in range(SSM_GROUPS):
            yg = y2[:, g * gw:(g + 1) * gw]
            r = lax.rsqrt(jnp.mean(yg * yg, axis=-1, keepdims=True) + EPS)
            o_ref[:, g * gw:(g + 1) * gw] = (yg * r * gm_ref[:, g * gw:(g + 1) * gw]).astype(o_ref.dtype)

    return _call(body, name, (n // tile,), [_rs(tile, d), _rs(tile, d), _rs(tile, d, 2), _ps((1, d)), _ps((1, d))],
                 _rs(tile, d), S((n, d), MXU_DTYPE), sem=("parallel",))(yssd, xs, proj, dfull, gamma)


def ssm_gate_bwd(dy3, yssd, xs, proj, dfull, gamma, dproj, name):
    n, d = yssd.shape
    tile = _row_tile(n, 256)
    gw = d // SSM_GROUPS

    def body(dy_ref, y_ref, xs_ref, z_ref, df_ref, gm_ref, dp_in_ref, dys_ref, dxs_ref, dz_ref, dgm_ref, dd_ref):
        i = pl.program_id(0)
        z = z_ref[...]
        s = _sigmoid(z)
        xs_ = xs_ref[...]
        y1 = y_ref[...] + df_ref[...] * xs_
        y2 = y1 * (z * s)
        dy_ = dy_ref[...].astype(F32)
        dgm = []
        dy2 = []
        for g in range(SSM_GROUPS):
            sl = slice(g * gw, (g + 1) * gw)
            dxg, dgg = _rms_bwd_math(y2[:, sl], gm_ref[:, sl], dy_[:, sl])
            dy2.append(dxg)
            dgm.append(dgg)
        dy2 = jnp.concatenate(dy2, axis=1)
        dy1 = dy2 * (z * s)
        dys_ref[...] = dy1
        dxs_ref[...] = dy1 * df_ref[...]
        dz_ref[...] = (dy2 * y1 * s * (1.0 + z * (1.0 - s))).astype(dz_ref.dtype)
        _acc(dgm_ref, i, jnp.concatenate(dgm, axis=1))
        colsum = jnp.broadcast_to(jnp.sum(dy1 * xs_, axis=0, keepdims=True), (8, d))
        _acc(dd_ref, i, _mm01(colsum, _head_indicator(d, SSM_HEADS))[0:1, :])

    return pl.pallas_call(
        body, name=name, grid=(n // tile,),
        in_specs=[_rs(tile, d), _rs(tile, d), _rs(tile, d), _rs(tile, d, 2), _ps((1, d)), _ps((1, d)), ANY],
        out_specs=[_rs(tile, d), _rs(tile, d), _rs(tile, d, 2), _ps((1, d)), _ps((1, SSM_HEADS))],
        out_shape=[S((n, d), F32), S((n, d), F32), S(dproj.shape, dproj.dtype), S((1, d), F32), S((1, SSM_HEADS), F32)],
        input_output_aliases={6: 2},
        compiler_params=pltpu.CompilerParams(dimension_semantics=("arbitrary",), vmem_limit_bytes=VMEM_LIMIT),
    )(dy3, yssd, xs, proj, dfull, gamma, dproj)


def _rope128(x, cos, sin_signed):
    half = HEAD_DIM // 2
    lane = _iota(x.shape, 1)
    partner = jnp.where((lane % HEAD_DIM) < half, pltpu.roll(x, LANE - half, 1), pltpu.roll(x, half, 1))
    return x * cos + partner * sin_signed


def rope_fwd(qkv, cos, sin, name):
    n, w = qkv.shape
    qw = ATT_HEADS * HEAD_DIM
    kw = ATT_KV_HEADS * HEAD_DIM
    tile = _row_tile(n, 256)

    def body(x_ref, c_ref, s_ref, q_ref, k_ref, v_ref):
        c, s = c_ref[...], s_ref[...]
        for j in range(qw // LANE):
            q_ref[:, j * LANE:(j + 1) * LANE] = _rope128(x_ref[:, j * LANE:(j + 1) * LANE], c, s).astype(q_ref.dtype)
        for j in range(kw // LANE):
            k_ref[:, j * LANE:(j + 1) * LANE] = _rope128(x_ref[:, qw + j * LANE:qw + (j + 1) * LANE], c, s).astype(k_ref.dtype)
        v_ref[...] = x_ref[:, qw + kw:].astype(v_ref.dtype)

    return _call(body, name, (n // tile,), [_rs(tile, w), _rs(tile, LANE), _rs(tile, LANE)],
                 [_rs(tile, qw), _rs(tile, kw), _rs(tile, kw)],
                 [S((n, qw), MXU_DTYPE), S((n, kw), MXU_DTYPE), S((n, kw), MXU_DTYPE)], sem=("parallel",))(qkv, cos, sin)


ATT_GROUP = ATT_HEADS // ATT_KV_HEADS


def _attn_mask(i):
    row = _iota((ATT_GROUP * WINDOW, 2 * WINDOW), 0) % WINDOW
    s = _iota((ATT_GROUP * WINDOW, 2 * WINDOW), 1)
    return (s > row) & (s <= row + WINDOW) & ((s >= WINDOW) | (i > 0))


def _stack_heads(ref, j, kh, lo):
    parts = []
    for t in range(ATT_GROUP):
        h = ATT_GROUP * j + t
        blk = ref[:, (h // 2) * LANE:(h // 2 + 1) * LANE]
        blk = jnp.where(lo if h % 2 == 0 else jnp.logical_not(lo), blk, jnp.zeros_like(blk))
        parts.append(blk if h % 2 == kh else pltpu.roll(blk, HEAD_DIM, 1))
    return jnp.concatenate(parts, axis=0)


def _unstack_heads(stacked, j, kh, lo, put):
    for t in range(0, ATT_GROUP, 2):
        h = ATT_GROUP * j + t
        even = stacked[t * WINDOW:(t + 1) * WINDOW, :]
        odd = stacked[(t + 1) * WINDOW:(t + 2) * WINDOW, :]
        even = even if kh == 0 else pltpu.roll(even, HEAD_DIM, 1)
        odd = odd if kh == 1 else pltpu.roll(odd, HEAD_DIM, 1)
        put(h // 2, jnp.where(lo, even, odd))


def _per_head_rows(ref, j):
    return jnp.concatenate([ref[:, ATT_GROUP * j + t:ATT_GROUP * j + t + 1] for t in range(ATT_GROUP)], axis=0)


def _per_head_scalar(ref, j):
    rows = _iota((ATT_GROUP * WINDOW, 1), 0) // WINDOW
    out = jnp.zeros((ATT_GROUP * WINDOW, 1), F32)
    for t in range(ATT_GROUP):
        out = out + jnp.where(rows == t, ref[:, ATT_GROUP * j + t:ATT_GROUP * j + t + 1], 0.0)
    return out


def attn_fwd(q, k, v, sinks, name):
    n, qw = q.shape
    kw = k.shape[1]
    nb = n // WINDOW
    scale = HEAD_DIM ** -0.5

    def body(q_ref, kc_ref, kp_ref, vc_ref, vp_ref, sk_ref, o_ref, lse_ref):
        i = pl.program_id(0)
        valid = _attn_mask(i)
        lo = _iota((WINDOW, LANE), 1) < HEAD_DIM
        k2 = jnp.concatenate([kp_ref[...], kc_ref[...]], axis=0)
        v2 = jnp.concatenate([vp_ref[...], vc_ref[...]], axis=0)
        lane1 = _iota((1, LANE), 1)
        lse = jnp.zeros((WINDOW, LANE), F32)

        def put_o(qb, val):
            o_ref[:, qb * LANE:(qb + 1) * LANE] = val.astype(o_ref.dtype)

        for j in range(ATT_KV_HEADS):
            kb, kh = j // 2, j % 2
            q4 = _stack_heads(q_ref, j, kh, lo)
            logits = jnp.where(valid, _mm(q4, k2[:, kb * LANE:(kb + 1) * LANE], NT) * scale, -1e30)
            sk = _per_head_scalar(sk_ref, j)
            m = jnp.maximum(jnp.max(logits, axis=-1, keepdims=True), sk)
            e = jnp.exp(logits - m)
            den = jnp.sum(e, axis=-1, keepdims=True) + jnp.exp(sk - m)
            lse4 = m + jnp.log(den)
            for t in range(ATT_GROUP):
                lse = lse + lse4[t * WINDOW:(t + 1) * WINDOW, :] * (lane1 == ATT_GROUP * j + t).astype(F32)
            _unstack_heads(_mm(e * (1.0 / den), v2[:, kb * LANE:(kb + 1) * LANE]), j, kh, lo, put_o)
        lse_ref[...] = lse

    return _call(body, name, (nb,),
                 [_rs(WINDOW, qw), _rs(WINDOW, kw), _rs(WINDOW, kw, 0, -1), _rs(WINDOW, kw), _rs(WINDOW, kw, 0, -1), _ps((1, LANE))],
                 [_rs(WINDOW, qw), _rs(WINDOW, LANE)], [S((n, qw), MXU_DTYPE), S((n, LANE), F32)],
                 sem=("parallel",))(q, k, k, v, v, sinks)


def attn_bwd(q, k, v, o, do, lse, sinks, name):
    n, qw = q.shape
    kw = k.shape[1]
    nb = n // WINDOW
    scale = HEAD_DIM ** -0.5

    def body(q_ref, kc_ref, kp_ref, vc_ref, vp_ref, o_ref, do_ref, lse_ref, sk_ref,
             dq_ref, dka_ref, dkb_ref, dva_ref, dvb_ref, dsk_ref):
        i = pl.program_id(0)
        valid = _attn_mask(i)
        lo = _iota((WINDOW, LANE), 1) < HEAD_DIM
        k2 = jnp.concatenate([kp_ref[...], kc_ref[...]], axis=0)
        v2 = jnp.concatenate([vp_ref[...], vc_ref[...]], axis=0)
        lane1 = _iota((1, LANE), 1)
        do_ = do_ref[...].astype(F32)
        delta = _mm01(do_ * o_ref[...].astype(F32), _head_indicator(qw, ATT_HEADS))
        dk2 = [jnp.zeros((2 * WINDOW, LANE), F32) for _ in range(kw // LANE)]
        dv2 = [jnp.zeros((2 * WINDOW, LANE), F32) for _ in range(kw // LANE)]
        dsk = jnp.zeros((1, LANE), F32)

        def put_dq(qb, val):
            dq_ref[:, qb * LANE:(qb + 1) * LANE] = val

        for j in range(ATT_KV_HEADS):
            kb, kh = j // 2, j % 2
            q4 = _stack_heads(q_ref, j, kh, lo)
            do4 = _stack_heads(do_ref, j, kh, lo)
            kk = k2[:, kb * LANE:(kb + 1) * LANE]
            vv = v2[:, kb * LANE:(kb + 1) * LANE]
            logits = jnp.where(valid, _mm(q4, kk, NT) * scale, -1e30)
            lse4 = _per_head_rows(lse_ref, j)
            p = jnp.exp(logits - lse4)
            dl = jnp.concatenate([delta[:, ATT_GROUP * j + t:ATT_GROUP * j + t + 1] for t in range(ATT_GROUP)], axis=0)
            ds = p * (_mm(do4, vv, NT) - dl) * scale
            sd = jnp.exp(_per_head_scalar(sk_ref, j) - lse4) * dl
            for t in range(ATT_GROUP):
                dsk = dsk - (jnp.sum(sd[t * WINDOW:(t + 1) * WINDOW, :], axis=0, keepdims=True)
                             * (lane1 == ATT_GROUP * j + t).astype(F32))
            _unstack_heads(_mm(ds, kk), j, kh, lo, put_dq)
            dk2[kb] = dk2[kb] + _mm(ds, q4, TN)
            dv2[kb] = dv2[kb] + _mm(p, do4, TN)
        for kb in range(kw // LANE):
            dkb_ref[:, kb * LANE:(kb + 1) * LANE] = dk2[kb][0:WINDOW, :]
            dka_ref[:, kb * LANE:(kb + 1) * LANE] = dk2[kb][WINDOW:, :]
            dvb_ref[:, kb * LANE:(kb + 1) * LANE] = dv2[kb][0:WINDOW, :]
            dva_ref[:, kb * LANE:(kb + 1) * LANE] = dv2[kb][WINDOW:, :]
        _acc(dsk_ref, i, dsk)

    return _call(body, name, (nb,),
                 [_rs(WINDOW, qw), _rs(WINDOW, kw), _rs(WINDOW, kw, 0, -1), _rs(WINDOW, kw), _rs(WINDOW, kw, 0, -1),
                  _rs(WINDOW, qw), _rs(WINDOW, qw), _rs(WINDOW, LANE), _ps((1, LANE))],
                 [_rs(WINDOW, qw)] + [_rs(WINDOW, kw)] * 4 + [_ps((1, LANE))],
                 [S((n, qw), F32)] + [S((n, kw), F32)] * 4 + [S((1, LANE), F32)],
                 sem=("arbitrary",))(q, k, k, v, v, o, do, lse, sinks)


def attn_grad_merge(dq, dka, dkb, dva, dvb, cos, sin, name):
    n, qw = dq.shape
    kw = dka.shape[1]
    nb = n // WINDOW
    w = qw + 2 * kw

    def body(dq_ref, dka_ref, dkb_ref, dva_ref, dvb_ref, c_ref, s_ref, o_ref, db_ref):
        i = pl.program_id(0)
        c, s = c_ref[...], -s_ref[...]
        nxt = (i < nb - 1).astype(F32)

        @pl.when(i == 0)
        def _():
            db_ref[...] = jnp.zeros_like(db_ref)

        def put(c0, val):
            o_ref[:, c0:c0 + val.shape[1]] = val.astype(o_ref.dtype)
            db_ref[:, c0:c0 + val.shape[1]] += jnp.sum(val, axis=0, keepdims=True)

        for j in range(qw // LANE):
            put(j * LANE, _rope128(dq_ref[:, j * LANE:(j + 1) * LANE], c, s))
        for j in range(kw // LANE):
            sl = slice(j * LANE, (j + 1) * LANE)
            put(qw + j * LANE, _rope128(dka_ref[:, sl] + dkb_ref[:, sl] * nxt, c, s))
        put(qw + kw, dva_ref[...] + dvb_ref[...] * nxt)

    return _call(body, name, (nb,),
                 [_rs(WINDOW, qw), _rs(WINDOW, kw), _rs(WINDOW, kw, 0, 1, nb), _rs(WINDOW, kw), _rs(WINDOW, kw, 0, 1, nb),
                  _rs(WINDOW, LANE), _rs(WINDOW, LANE)],
                 [_rs(WINDOW, w), _ps((1, w))], [S((n, w), MXU_DTYPE), S((1, w), F32)],
                 sem=("arbitrary",))(dq, dka, dkb, dva, dvb, cos, sin)


def _row(v):
    return v.reshape(1, -1)


def _pad_lanes(v, width=LANE):
    return jnp.pad(v.reshape(1, -1), ((0, 0), (0, width - v.size)))


class LayerWeights(dict):
    def __init__(self, small, fetch):
        super().__init__(small)
        self.fetch = fetch

    def need(self, k, after):
        if k not in self:
            self[k] = self.fetch(k, after)
        return self[k]


def ffn_fwd(h, g, w, keys, tag):
    xn, u, hm = swiglu_in(h, _row(g), w.need(keys[0], h), f"{tag}_in")
    return matmul(hm, w.need(keys[1], hm), "nn", f"{tag}_out", scale=0.5, res=h), (h, xn, u, hm)


class GradSink:
    ORDER = ("ffn1_w_out", "ffn2_w_out", "ple_gate_w", "att_w_o", "hyb_w_out", "ffn1_w_in", "ffn2_w_in", "att_w_qkv",
             "ple_proj_w", "hyb_w_in")

    def __init__(self, shard_shapes, bucket_of):
        self.where, rows = {}, {}
        for k in self.ORDER:
            n, r, c = shard_shapes[k]
            for li in range(n):
                layer = li if k in PER_LAYER else 2 * li + (0 if k in EVEN_ONLY else 1)
                rows_b = rows.setdefault(bucket_of(layer, _stage(k)), {})
                key = c if r % 32 == 0 else k
                off = -(-rows_b.get(key, (0, c))[0] // r) * r
                rows_b[key] = (-(-(off + r) // 32) * 32, c)
                self.where[k, li] = (bucket_of(layer, _stage(k)), key, "r" if _shard_axis(k) == 1 else "c", off, r)
        self.bufs = {b: {key: lax.empty((N_CHIPS, r, c), F32) for key, (r, c) in rows_b.items()} for b, rows_b in rows.items()}

    def mm(self, k, li, a, b, name, scale=None, c0=0, paired=False):
        bucket, key, kind, off, r = self.where[k, li]
        buf = self.bufs[bucket][key]
        slot = Slot(buf, kind, r if kind == "r" else buf.shape[2], off, c0, paired)
        self.bufs[bucket][key] = matmul(a, b, "tn", name, scale=scale, into=slot)

    def put(self, k, li, chip_major):
        b, key = self.where[k, li][:2]
        pad = self.bufs[b][key].shape[1] - chip_major.shape[1]
        self.bufs[b][key] = jnp.pad(chip_major, ((0, 0), (0, pad), (0, 0)))


def ffn_bwd(dh, g, w_in, w_out, saved, tag, sink, keys, layer, after, colsum=False):
    h, xn, u, hm = saved
    sink.mm(keys[1], layer, hm, dh, f"{tag}_dwout", scale=0.5)
    du = swiglu_out_bwd(dh, w_out, u, after, f"{tag}_dhm")
    sink.mm(keys[0], layer, xn, du, f"{tag}_dwin", paired=True)
    outs = nt_rms_bwd(du, ColSharded(w_in.arr, paired=True), h, _row(g), dh, f"{tag}_dxn", colsum=colsum)
    return (outs[0], outs[1].reshape(-1)) + ((outs[2],) if colsum else ())


def _hyb_params(w):
    d = w["conv_dw_b"].size
    inner = SSM_HEADS * HEAD_DIM
    main = 3 * d + w["ssm_conv_b"].size
    return dict(
        w_main=w["hyb_w_in"][:main], w_dt=jnp.pad(w["hyb_w_in"][main:], ((0, LANE - SSM_HEADS), (0, 0))),
        cw=jnp.pad(w["conv_dw_w"], ((0, 32 - CONV_WIDTH), (0, 0))), cb=_row(w["conv_dw_b"]),
        lg=_row(w["conv_ln_g"]), lb=_row(w["conv_ln_b"]),
        sw=jnp.pad(w["ssm_conv_w"], ((0, 8 - SSM_CONV), (0, 0))), sb=_row(w["ssm_conv_b"]),
        dtb=_pad_lanes(w["ssm_dt_bias"]), al_row=_pad_lanes(w["ssm_a_log"]), al_col=w["ssm_a_log"].reshape(-1, 1),
        dfull=_row(jnp.repeat(w["ssm_d"], HEAD_DIM)), gamma=_row(w["ssm_norm"]), d=d, inner=inner, main=main)


def hyb_fwd(h, w, tag):
    w.need("hyb_w_in", h)
    q = _hyb_params(w)
    xn = rms_fwd(h, _row(w["norm_mix"]), f"{tag}_rms")
    proj = matmul(xn, q["w_main"], "nt", f"{tag}_in")
    dtr = matmul(xn, q["w_dt"], "nt", f"{tag}_in_dt")
    u, u1 = conv_group_fwd(proj, q["cw"], q["cb"], q["lg"], q["lb"], f"{tag}_conv")
    pre, xs, bc, dt = ssm_conv_fwd(proj, dtr, q["sw"], q["sb"], q["dtb"], f"{tag}_sconv")
    dtT = dt[:, :SSM_HEADS].T
    yssd, hs = ssd_fwd(xs, bc, dt, dtT, q["al_row"], q["al_col"], f"{tag}_ssd")
    y = ssm_gate_fwd(yssd, xs, proj, q["dfull"], q["gamma"], f"{tag}_gate")
    wo = w.need("hyb_w_out", u)
    h2 = matmul(u, wo[:q["d"]], "nn", f"{tag}_out_a", res=h)
    h2 = matmul(y, wo[q["d"]:], "nn", f"{tag}_out_b", res=h2)
    return h2, (h, xn, proj, u, u1, pre, xs, bc, dt, dtT, yssd, hs, y)


def hyb_bwd(dh, w, saved, tag, sink, layer):
    q = _hyb_params(w)
    h, xn, proj, u, u1, pre, xs, bc, dt, dtT, yssd, hs, y = saved
    du = matmul(dh, w["hyb_w_out"][:q["d"]], "nt", f"{tag}_du")
    dy3 = matmul(dh, w["hyb_w_out"][q["d"]:], "nt", f"{tag}_dy")
    sink.mm("hyb_w_out", layer, u, dh, f"{tag}_dwo_a", c0=0)
    sink.mm("hyb_w_out", layer, y, dh, f"{tag}_dwo_b", c0=N_CHIPS // 2)
    dproj, dcw, dcb, dlg, dlb = conv_group_bwd(du, u1, proj, q["cw"], q["lg"], q["lb"], f"{tag}_dconv")
    dyssd, dxs_skip, dproj, dgamma, dd = ssm_gate_bwd(dy3, yssd, xs, proj, q["dfull"], q["gamma"], dproj, f"{tag}_dgate")
    dxs, dbc, ddtr, dalog, ddtb = ssd_bwd(xs, bc, dt, dtT, q["al_row"], q["al_col"], hs, dyssd, dxs_skip, f"{tag}_dssd")
    dproj, dsw, dsb = ssm_conv_bwd(dxs, dbc, pre, proj, q["sw"], dproj, f"{tag}_dsconv")
    dw_in = jnp.concatenate([matmul(dproj, xn, "tn", f"{tag}_dwin"),
                             matmul(ddtr, xn, "tn", f"{tag}_dwin_dt")[:SSM_HEADS]], axis=0)
    sink.put("hyb_w_in", layer, dw_in.reshape((N_CHIPS, -1) + dw_in.shape[1:]))
    dh2, dg = nt_rms_bwd(dproj, q["w_main"], h, _row(w["norm_mix"]), dh, f"{tag}_dxn", extra=(ddtr, q["w_dt"]), b_kd=True)
    grads = dict(norm_mix=dg.reshape(-1), conv_dw_w=dcw[:CONV_WIDTH], conv_dw_b=dcb.reshape(-1),
                 conv_ln_g=dlg.reshape(-1), conv_ln_b=dlb.reshape(-1), ssm_conv_w=dsw[:SSM_CONV], ssm_conv_b=dsb.reshape(-1),
                 ssm_dt_bias=ddtb.reshape(-1), ssm_a_log=dalog.reshape(-1), ssm_d=dd.reshape(-1), ssm_norm=dgamma.reshape(-1))
    return dh2, grads


def rope_tables(n):
    half = HEAD_DIM // 2
    inv = ROPE_THETA ** (-jnp.arange(0, HEAD_DIM, 2, dtype=F32) / HEAD_DIM)
    ang = jnp.arange(n, dtype=F32)[:, None] * inv[None, :]
    cos, sin = jnp.cos(ang), jnp.sin(ang)
    reps = LANE // HEAD_DIM
    return jnp.tile(jnp.concatenate([cos, cos], axis=1), (1, reps)), jnp.tile(jnp.concatenate([-sin, sin], axis=1), (1, reps))


def att_fwd(h, w, tables, tag):
    cos, sin = tables
    xn = rms_fwd(h, _row(w["norm_mix"]), f"{tag}_rms")
    qkv = matmul(xn, w.need("att_w_qkv", h), "nn", f"{tag}_qkv", bias=_row(w["att_b_qkv"]))
    q, k, v = rope_fwd(qkv, cos, sin, f"{tag}_rope")
    sinks = _pad_lanes(w["att_sinks"])
    o, lse = attn_fwd(q, k, v, sinks, f"{tag}_attn")
    h2 = matmul(o, w.need("att_w_o", o), "nn", f"{tag}_o", bias=_row(w["att_b_o"]), res=h)
    return h2, (h, xn, q, k, v, o, lse, sinks)


def att_bwd(dh, dh_colsum, w, saved, tables, tag, sink, layer):
    cos, sin = tables
    h, xn, q, k, v, o, lse, sinks = saved
    do = matmul(dh, w["att_w_o"], "nt", f"{tag}_do")
    sink.mm("att_w_o", layer, o, dh, f"{tag}_dwo")
    dq, dka, dkb, dva, dvb, dsk = attn_bwd(q, k, v, o, do, lse, sinks, f"{tag}_dattn")
    dqkv, dbqkv = attn_grad_merge(dq, dka, dkb, dva, dvb, cos, sin, f"{tag}_drope")
    sink.mm("att_w_qkv", layer, xn, dqkv, f"{tag}_dwqkv")
    dh2, dg = nt_rms_bwd(dqkv, w["att_w_qkv"], h, _row(w["norm_mix"]), dh, f"{tag}_dxn")
    grads = dict(norm_mix=dg.reshape(-1), att_b_qkv=dbqkv.reshape(-1), att_sinks=dsk[0, :ATT_HEADS],
                 att_b_o=dh_colsum.reshape(-1))
    return dh2, grads


def ple_block_fwd(h, pe, w, tag):
    xn = rms_fwd(h, _row(w["ple_norm"]), f"{tag}_rms")
    gl = matmul(xn, w.need("ple_gate_w", h), "nn", f"{tag}_gate")
    pp = matmul(pe, w.need("ple_proj_w", h), "nn", f"{tag}_proj")
    return ple_fwd(h, gl, pp, f"{tag}_mix"), (h, xn, gl, pp, pe)


def ple_block_bwd(dh, w, saved, tag, sink, layer, after):
    h, xn, gl, pp, pe = saved
    dpp, dgl = ple_bwd(dh, gl, pp, after, f"{tag}_dmix")
    sink.mm("ple_proj_w", layer, pe, dpp, f"{tag}_dwp")
    sink.mm("ple_gate_w", layer, xn, dgl, f"{tag}_dwg")
    dh2, dg = nt_rms_bwd(dgl, w["ple_gate_w"], h, _row(w["ple_norm"]), dh, f"{tag}_dxn")
    return dh2, dict(ple_norm=dg.reshape(-1))


PER_LAYER = ("norm_ffn1", "ffn1_w_in", "ffn1_w_out", "norm_mix", "norm_ffn2", "ffn2_w_in", "ffn2_w_out",
             "ple_norm", "ple_gate_w", "ple_proj_w")
EVEN_ONLY = ("hyb_w_in", "conv_dw_w", "conv_dw_b", "conv_ln_g", "conv_ln_b", "ssm_conv_w", "ssm_conv_b",
             "ssm_dt_bias", "ssm_a_log", "ssm_d", "ssm_norm", "hyb_w_out")
ODD_ONLY = ("att_w_qkv", "att_b_qkv", "att_sinks", "att_w_o", "att_b_o")


def _layer_index(k, i):
    if k in PER_LAYER:
        return i
    if k in (EVEN_ONLY if i % 2 == 0 else ODD_ONLY):
        return i // 2
    return None


def _stage(k):
    return 0 if k.startswith("ffn1") else (2 if k.startswith(("ffn2", "ple")) else 1)


def trunk_fwd_bwd(x, pe, target, layers, final_norm, sink, stage_done):
    depth = len(layers)
    tables = rope_tables(x.shape[0])
    h = x
    saved = []
    for i, w in enumerate(layers):
        h, s1 = ffn_fwd(h, w["norm_ffn1"], w, ("ffn1_w_in", "ffn1_w_out"), f"l{i}_ffn1")
        if i % 2 == 0:
            h, s2 = hyb_fwd(h, w, f"l{i}_hyb")
        else:
            h, s2 = att_fwd(h, w, tables, f"l{i}_att")
        h, s3 = ffn_fwd(h, w["norm_ffn2"], w, ("ffn2_w_in", "ffn2_w_out"), f"l{i}_ffn2")
        h, s4 = ple_block_fwd(h, pe[i], w, f"l{i}_ple")
        saved.append((s1, s2, s3, s4))
    dh, dgf, loss = loss_head(h, _row(final_norm), target, "loss_head")
    grads = {}
    tie = dgf
    for i in reversed(range(depth)):
        w = layers[i]
        s1, s2, s3, s4 = saved[i]
        dh, g = ple_block_bwd(dh, w, s4, f"l{i}_ple", sink, i, tie)
        odd = i % 2 == 1
        out = ffn_bwd(dh, w["norm_ffn2"], w["ffn2_w_in"], w["ffn2_w_out"], s3, f"l{i}_ffn2", sink,
                      ("ffn2_w_in", "ffn2_w_out"), i, tie, colsum=odd)
        dh = out[0]
        g.update(norm_ffn2=out[1])
        if odd:
            dh, gm = att_bwd(dh, out[2], w, s2, tables, f"l{i}_att", sink, i // 2)
        else:
            dh, gm = hyb_bwd(dh, w, s2, f"l{i}_hyb", sink, i // 2)
        g.update(gm)
        tie = stage_done(i, 1, tie)
        out = ffn_bwd(dh, w["norm_ffn1"], w["ffn1_w_in"], w["ffn1_w_out"], s1, f"l{i}_ffn1", sink,
                      ("ffn1_w_in", "ffn1_w_out"), i, tie)
        dh = out[0]
        g.update(norm_ffn1=out[1])
        tie = stage_done(i, 0, tie)
        for k, v in g.items():
            grads.setdefault(k, []).insert(0, v)
    grads = {k: jnp.stack(v) for k, v in grads.items()}
    grads["final_norm"] = dgf.reshape(-1)
    return loss, dh, grads


def _me():
    return lax.axis_index("x"), lax.axis_index("y"), lax.axis_index("c")


def _flip(v, f):
    return 1 - v if f else v


def _remote(src, dst, send_sems, recv_sems, k, dev):
    return pltpu.make_async_remote_copy(src_ref=src, dst_ref=dst, send_sem=send_sems.at[k], recv_sem=recv_sems.at[k],
                                        device_id=dev, device_id_type=MESH)


CHIP_FLIPS = ((1, 0), (0, 1), (1, 1))
DEV_FLIPS = tuple((fx, fy, fc) for fx in (0, 1) for fy in (0, 1) for fc in (0, 1))[1:]


def all_gather_chips(xs, name):
    na = len(xs)
    halves = [x.shape[0] // 2 for x in xs]
    assert all(x.shape[0] % 2 == 0 for x in xs)

    def body(*refs):
        x_refs, out_refs = refs[:na], refs[na:2 * na]
        send_sems, recv_sems = refs[2 * na:]
        mx, my, mc = _me()
        chip = 2 * mx + my
        sib = (mx, my, 1 - mc)
        peers = [(_flip(mx, fx), _flip(my, fy)) for fx, fy in CHIP_FLIPS]

        def rows(a, ch, hc):
            return out_refs[a].at[ch, pl.ds(hc * halves[a], halves[a]), :]

        def src(a):
            return x_refs[a].at[pl.ds(mc * halves[a], halves[a]), :]

        first = [_remote(src(a), rows(a, chip, mc), send_sems, recv_sems, 6 * a + j, (px, py, mc))
                 for j, (px, py) in enumerate(peers) for a in range(na)]
        for cp in first:
            cp.start()
        passed = []
        for j, (px, py) in enumerate(peers):
            for a in range(na):
                landed = rows(a, 2 * px + py, mc)
                _remote(src(a), landed, send_sems, recv_sems, 6 * a + j, (px, py, mc)).wait_recv()
                fw = _remote(landed, landed, send_sems, recv_sems, 6 * a + 3 + j, sib)
                fw.start()
                passed.append(fw)
        for j, (px, py) in enumerate(peers):
            for a in range(na):
                _remote(src(a), rows(a, 2 * px + py, 1 - mc), send_sems, recv_sems, 6 * a + 3 + j, sib).wait_recv()
        for cp in first + passed:
            cp.wait_send()

    outs = pl.pallas_call(
        body, name=name, out_shape=[S((N_CHIPS,) + x.shape, x.dtype) for x in xs], in_specs=[ANY] * na, out_specs=[ANY] * na,
        scratch_shapes=[pltpu.SemaphoreType.DMA((6 * na,)), pltpu.SemaphoreType.DMA((6 * na,))])(*xs)
    chip = 2 * lax.axis_index("x") + lax.axis_index("y")
    return [lax.dynamic_update_slice_in_dim(o, x[None], chip, axis=0) for o, x in zip(outs, xs)]


HBM = pl.BlockSpec(memory_space=pltpu.HBM)
SEM = pl.BlockSpec(memory_space=pltpu.SEMAPHORE)
DATAFLOW = pltpu.SideEffectType.DATAFLOW_SIDE_EFFECTING


def gather_start(xs, lands, after, name):
    na = len(xs)

    def body(*refs):
        x_refs, land_refs = refs[:na], refs[na:2 * na]
        send_sems, recv_sems = refs[2 * na + 1], refs[2 * na + 2]
        token = refs[-1]
        mx, my, mc = _me()
        chip = 2 * mx + my
        for a in range(na):
            for j, (fx, fy) in enumerate(CHIP_FLIPS):
                _remote(x_refs[a], land_refs[a].at[chip], send_sems, recv_sems, 3 * a + j,
                        (_flip(mx, fx), _flip(my, fy), mc)).start()
        token[...] = jnp.zeros_like(token)

    outs = pl.pallas_call(
        body, name=name,
        out_shape=(pltpu.SemaphoreType.DMA((3 * na,)), pltpu.SemaphoreType.DMA((3 * na,)))
        + tuple(pltpu.HBM(x.shape, x.dtype) for x in xs) + tuple(pltpu.HBM(l.shape, l.dtype) for l in lands)
        + (S((8, LANE), F32),),
        in_specs=[HBM] * (2 * na) + [pl.BlockSpec(memory_space=pl.ANY)],
        out_specs=(SEM, SEM) + (HBM,) * (2 * na) + (pl.BlockSpec(memory_space=pltpu.VMEM),),
        input_output_aliases={a: 2 + a for a in range(2 * na)},
        compiler_params=pltpu.CompilerParams(has_side_effects=DATAFLOW),
    )(*[pltpu.with_memory_space_constraint(t, pltpu.HBM) for t in list(xs) + list(lands)], after)
    return outs[0], outs[1], list(outs[2:2 + na]), list(outs[2 + na:2 + 2 * na])


def gather_wait(send_sems, recv_sems, xs, lands, first, after, name):
    na = len(xs)

    def body(*refs):
        x_refs, land_refs = refs[:na], refs[na:2 * na]
        send_sems, recv_sems = refs[2 * na], refs[2 * na + 1]
        mx, my, mc = _me()
        for a in range(na):
            for j, (fx, fy) in enumerate(CHIP_FLIPS):
                px, py = _flip(mx, fx), _flip(my, fy)
                cp = _remote(x_refs[a], land_refs[a].at[2 * px + py], send_sems, recv_sems, 3 * (first + a) + j, (px, py, mc))
                cp.wait_send()
                cp.wait_recv()

    outs = pl.pallas_call(
        body, name=name,
        out_shape=tuple(pltpu.HBM(x.shape, x.dtype) for x in xs) + tuple(pltpu.HBM(l.shape, l.dtype) for l in lands),
        in_specs=[HBM] * (2 * na) + [SEM, SEM, pl.BlockSpec(memory_space=pl.ANY)], out_specs=(HBM,) * (2 * na),
        input_output_aliases={a: a for a in range(2 * na)},
        compiler_params=pltpu.CompilerParams(has_side_effects=DATAFLOW),
    )(*xs, *lands, send_sems, recv_sems, after)
    return list(outs[na:])


def all_gather_devices(v, name):
    r, l = v.shape

    def body(v_ref, out_ref, send_sems, recv_sems):
        mx, my, mc = _me()
        me = 4 * mx + 2 * my + mc
        peers = [(_flip(mx, fx), _flip(my, fy), _flip(mc, fc)) for fx, fy, fc in DEV_FLIPS]
        sends = [_remote(v_ref, out_ref.at[me], send_sems, recv_sems, j, p) for j, p in enumerate(peers)]
        for cp in sends:
            cp.start()
        for j, (px, py, pc) in enumerate(peers):
            _remote(v_ref, out_ref.at[4 * px + 2 * py + pc], send_sems, recv_sems, j, (px, py, pc)).wait_recv()
        for cp in sends:
            cp.wait_send()

    out = pl.pallas_call(
        body, name=name, out_shape=S((N_DEV, r, l), v.dtype), in_specs=[ANY], out_specs=ANY,
        scratch_shapes=[pltpu.SemaphoreType.DMA((7,)), pltpu.SemaphoreType.DMA((7,))])(v)
    me = 4 * lax.axis_index("x") + 2 * lax.axis_index("y") + lax.axis_index("c")
    return lax.dynamic_update_slice_in_dim(out, v[None], me, axis=0)


def sum_devices(g8, name):
    nd, r, l = g8.shape
    tile = r
    for t in (512, 256, 128, 64, 32, 16, 8):
        if r % t == 0:
            tile = t
            break

    def body(g_ref, o_ref):
        acc = g_ref[0]
        for d in range(1, nd):
            acc = acc + g_ref[d]
        o_ref[...] = acc

    return _call(body, name, (r // tile,), [pl.BlockSpec((nd, tile, l), lambda i: (0, i, 0))], _rs(tile, l), S((r, l), F32),
                 sem=("parallel",))(g8)


def exchange_halves(gs, name):
    na = len(gs)
    nch = gs[0].shape[0]

    def body(*refs):
        g_refs, out_refs = refs[:na], refs[na:2 * na]
        send_sems, recv_sems = refs[2 * na:]
        mx, my, mc = _me()
        sib = (mx, my, 1 - mc)
        cps = []
        for a in range(na):
            half = gs[a].shape[1] // 2
            for j in range(nch):
                cps.append(_remote(g_refs[a].at[j, pl.ds((1 - mc) * half, half), :], out_refs[a].at[j],
                                   send_sems, recv_sems, nch * a + j, sib))
        for cp in cps:
            cp.start()
        for cp in cps:
            cp.wait_recv()
        for cp in cps:
            cp.wait_send()

    return pl.pallas_call(
        body, name=name, out_shape=[S((nch, g.shape[1] // 2, g.shape[2]), g.dtype) for g in gs],
        in_specs=[ANY] * na, out_specs=[ANY] * na,
        scratch_shapes=[pltpu.SemaphoreType.DMA((nch * na,)), pltpu.SemaphoreType.DMA((nch * na,))])(*gs)


def add_halves(g4, got, name):
    nch, r, l = g4.shape
    half = r // 2
    tile = _pick_rows(half)
    nt = half // tile

    def body(g_ref, r_ref, a_ref, own_ref):
        j = pl.program_id(1)
        chip = 2 * lax.axis_index("x") + lax.axis_index("y")
        val = g_ref[0] + r_ref[0]
        a_ref[0] = val.astype(a_ref.dtype)

        @pl.when(j == chip)
        def _():
            own_ref[...] = val

    return pl.pallas_call(
        body, name=name, grid=(nt, nch),
        in_specs=[pl.BlockSpec((1, tile, l), lambda i, j: (j, lax.axis_index("c") * nt + i, 0)),
                  pl.BlockSpec((1, tile, l), lambda i, j: (j, i, 0))],
        out_specs=[pl.BlockSpec((1, tile, l), lambda i, j: (j, i, 0)), pl.BlockSpec((tile, l), lambda i, j: (i, 0))],
        out_shape=[S((nch, half, l), MXU_DTYPE), S((half, l), F32)],
        compiler_params=pltpu.CompilerParams(dimension_semantics=("parallel", "arbitrary"), vmem_limit_bytes=VMEM_LIMIT))(g4, got)


def _pick_rows(r, cap=512):
    for t in (512, 256, 128, 64, 32, 16):
        if t <= cap and r % t == 0:
            return t
    return r


def exchange_chips(parts, name):
    na = len(parts)

    def body(*refs):
        a_refs, out_refs = refs[:na], refs[na:2 * na]
        send_sems, recv_sems = refs[2 * na:]
        mx, my, mc = _me()
        peers = [(_flip(mx, fx), _flip(my, fy)) for fx, fy in CHIP_FLIPS]
        cps = [_remote(a_refs[a].at[2 * px + py], out_refs[a].at[j], send_sems, recv_sems, 3 * a + j, (px, py, mc))
               for j, (px, py) in enumerate(peers) for a in range(na)]
        for cp in cps:
            cp.start()
        for cp in cps:
            cp.wait_recv()
        for cp in cps:
            cp.wait_send()

    return pl.pallas_call(
        body, name=name, out_shape=[S((3,) + p.shape[1:], p.dtype) for p in parts], in_specs=[ANY] * na, out_specs=[ANY] * na,
        scratch_shapes=[pltpu.SemaphoreType.DMA((3 * na,)), pltpu.SemaphoreType.DMA((3 * na,))])(*parts)


def add_chips(own, got, name):
    h, l = own.shape
    tile = _pick_rows(h)

    def body(o_ref, g_ref, out_ref):
        out_ref[...] = ((o_ref[...] + g_ref[0].astype(F32)) + g_ref[1].astype(F32)) + g_ref[2].astype(F32)

    nt = h // tile
    return _call(body, name, (nt,), [_rs(tile, l), pl.BlockSpec((3, tile, l), lambda i: (0, i, 0))],
                 pl.BlockSpec((tile, l), lambda i: (lax.axis_index("c") * nt + i, 0)),
                 S((2 * h, l), F32), sem=("parallel",))(own, got)


def join_halves(bufs, name):
    na = len(bufs)

    def body(*refs):
        out_refs = refs[na:2 * na]
        send_sems, recv_sems = refs[2 * na:]
        mx, my, mc = _me()
        sib = (mx, my, 1 - mc)

        def half(a, hc):
            h = bufs[a].shape[0] // 2
            return out_refs[a].at[pl.ds(hc * h, h), :]

        cps = [_remote(half(a, mc), half(a, mc), send_sems, recv_sems, a, sib) for a in range(na)]
        for cp in cps:
            cp.start()
        for a in range(na):
            _remote(half(a, mc), half(a, 1 - mc), send_sems, recv_sems, a, sib).wait_recv()
        for cp in cps:
            cp.wait_send()

    return pl.pallas_call(
        body, name=name, out_shape=[S(b.shape, b.dtype) for b in bufs], in_specs=[ANY] * na, out_specs=[ANY] * na,
        input_output_aliases={a: a for a in range(na)},
        scratch_shapes=[pltpu.SemaphoreType.DMA((na,)), pltpu.SemaphoreType.DMA((na,))])(*bufs)


def exchange_chips_start(parts, name):
    na = len(parts)
    lands = [lax.empty((3,) + p.shape[1:], p.dtype) for p in parts]

    def body(*refs):
        a_refs, land_refs = refs[:na], refs[na:2 * na]
        send_sems, recv_sems = refs[2 * na], refs[2 * na + 1]
        mx, my, mc = _me()
        for j, (fx, fy) in enumerate(CHIP_FLIPS):
            px, py = _flip(mx, fx), _flip(my, fy)
            for a in range(na):
                _remote(a_refs[a].at[2 * px + py], land_refs[a].at[j], send_sems, recv_sems, 3 * a + j, (px, py, mc)).start()
        refs[-1][...] = jnp.zeros_like(refs[-1])

    outs = pl.pallas_call(
        body, name=name,
        out_shape=(pltpu.SemaphoreType.DMA((3 * na,)), pltpu.SemaphoreType.DMA((3 * na,)))
        + tuple(pltpu.HBM(t.shape, t.dtype) for t in list(parts) + lands) + (S((8, LANE), F32),),
        in_specs=[HBM] * (2 * na), out_specs=(SEM, SEM) + (HBM,) * (2 * na) + (pl.BlockSpec(memory_space=pltpu.VMEM),),
        input_output_aliases={a: 2 + a for a in range(2 * na)},
        compiler_params=pltpu.CompilerParams(has_side_effects=DATAFLOW),
    )(*[pltpu.with_memory_space_constraint(t, pltpu.HBM) for t in list(parts) + lands])
    return outs[0], outs[1], list(outs[2:2 + na]), list(outs[2 + na:2 + 2 * na]), outs[-1]


def exchange_chips_wait(send_sems, recv_sems, parts, lands, after, name):
    na = len(parts)

    def body(*refs):
        a_refs, land_refs = refs[:na], refs[na:2 * na]
        send_sems, recv_sems = refs[2 * na], refs[2 * na + 1]
        mx, my, mc = _me()
        for j, (fx, fy) in enumerate(CHIP_FLIPS):
            px, py = _flip(mx, fx), _flip(my, fy)
            for a in range(na):
                cp = _remote(a_refs[a].at[2 * px + py], land_refs[a].at[j], send_sems, recv_sems, 3 * a + j, (px, py, mc))
                cp.wait_send()
                cp.wait_recv()

    outs = pl.pallas_call(
        body, name=name, out_shape=tuple(pltpu.HBM(t.shape, t.dtype) for t in list(parts) + list(lands)),
        in_specs=[HBM] * (2 * na) + [SEM, SEM, pl.BlockSpec(memory_space=pl.ANY)], out_specs=(HBM,) * (2 * na),
        input_output_aliases={a: a for a in range(2 * na)},
        compiler_params=pltpu.CompilerParams(has_side_effects=DATAFLOW),
    )(*parts, *lands, send_sems, recv_sems, after)
    return list(outs[na:])


def reduce_begin(gs, tag):
    got = exchange_halves(gs, f"{tag}_d2d")
    sums = [add_halves(g, r, f"{tag}_add1_{i}") for i, (g, r) in enumerate(zip(gs, got))]
    return [own for _, own in sums], exchange_chips_start([a for a, _ in sums], f"{tag}_ici_start")


def reduce_end(state, after, tag):
    owns, (send_sems, recv_sems, parts, lands, _) = state
    got = exchange_chips_wait(send_sems, recv_sems, parts, lands, after, f"{tag}_ici_wait")
    return [add_chips(own, r, f"{tag}_add2_{i}") for i, (own, r) in enumerate(zip(owns, got))]


PACK_L = 1024
BIG_ROW_MULT = 512


def _pack(arrs, dtype, row_mult, lead=None):
    lead_shape = () if lead is None else arrs[0].shape[:lead]
    flat = jnp.concatenate([a.astype(dtype).reshape(lead_shape + (-1,)) for a in arrs], axis=-1)
    n = flat.shape[-1]
    unit = row_mult * PACK_L
    total = -(-n // unit) * unit
    flat = jnp.pad(flat, [(0, 0)] * len(lead_shape) + [(0, total - n)])
    return flat.reshape(lead_shape + (total // PACK_L, PACK_L))


def _unpack(packed, shapes, lead=None):
    lead_shape = () if lead is None else packed.shape[:lead]
    flat = packed.reshape(lead_shape + (-1,))
    out, off = [], 0
    for shp in shapes:
        n = int(np.prod(shp))
        out.append(flat[..., off:off + n].reshape(lead_shape + tuple(shp)))
        off += n
    return out


def _to_full(gathered, axis):
    t = jnp.moveaxis(gathered, 0, axis)
    shp = t.shape
    return t.reshape(shp[:axis] + (shp[axis] * shp[axis + 1],) + shp[axis + 2:])


def _to_chip_major(full, axis):
    shp = full.shape
    t = full.reshape(shp[:axis] + (N_CHIPS, shp[axis] // N_CHIPS) + shp[axis + 1:])
    return jnp.moveaxis(t, axis, 0)


WEIGHTS = ("norm_ffn1", "ffn1_w_in", "ffn1_w_out", "norm_mix", "norm_ffn2", "ffn2_w_in", "ffn2_w_out", "ple_norm",
           "ple_gate_w", "ple_proj_w", "hyb_w_in", "conv_dw_w", "conv_dw_b", "conv_ln_g", "conv_ln_b", "ssm_conv_w",
           "ssm_conv_b", "ssm_dt_bias", "ssm_a_log", "ssm_d", "ssm_norm", "hyb_w_out", "att_w_qkv", "att_b_qkv",
           "att_sinks", "att_w_o", "att_b_o", "final_norm")
SHARD_AXIS = dict(ffn1_w_in=2, ffn1_w_out=1, ffn2_w_in=2, ffn2_w_out=1, ple_gate_w=1, ple_proj_w=2, hyb_w_in=2,
                  conv_dw_w=2, ssm_conv_w=2, hyb_w_out=1, att_w_qkv=2, att_b_qkv=1, att_w_o=1, att_b_o=1)
BIG = ("ffn1_w_in", "ffn1_w_out", "ffn2_w_in", "ffn2_w_out", "ple_gate_w", "ple_proj_w", "hyb_w_in", "hyb_w_out",
       "att_w_qkv", "att_w_o")
TRANSPOSED = ("hyb_w_in",)


def _shard_axis(k):
    return 1 if k in TRANSPOSED else SHARD_AXIS[k]
SMALL_SHARDED = ("conv_dw_w", "ssm_conv_w", "att_b_qkv", "att_b_o")
SMALL = tuple(k for k in WEIGHTS if k not in BIG)


def _step(x, p, target, w, m, v):
    mx, my = lax.axis_index("x"), lax.axis_index("y")
    chip = 2 * mx + my
    w, m, v = ({k: (a.transpose(0, 2, 1) if k in TRANSPOSED else a) for k, a in d.items()} for d in (w, m, v))

    depth = w["norm_ffn1"].shape[0]
    order = sorted([(k, i) for i in range(depth) for k in BIG if _layer_index(k, i) is not None],
                   key=lambda t: (t[1], _stage(t[0])))
    small_g = all_gather_devices(_pack([w[k] for k in SMALL_SHARDED], F32, 8), "gather_small")
    shards = [w[k][_layer_index(k, i)].astype(MXU_DTYPE) for k, i in order]
    lands = [lax.dynamic_update_slice_in_dim(lax.empty((N_CHIPS,) + s.shape, s.dtype), s[None], chip, axis=0) for s in shards]
    send_sems, recv_sems, shards, lands = gather_start(shards, lands, small_g, "gather_start")

    def fetch(i, k, after):
        p = order.index((k, i))
        g, = gather_wait(send_sems, recv_sems, [shards[p]], [lands[p]], p, after, f"gather_wait_l{i}_{k}")
        if _shard_axis(k) == 2:
            return ColSharded(g)
        return g.reshape(-1, g.shape[-1])

    small_g = small_g[0::2]
    small_full = {k: _to_full(g, SHARD_AXIS[k])
                  for k, g in zip(SMALL_SHARDED, _unpack(small_g, [w[k].shape for k in SMALL_SHARDED], lead=1))}
    layers = [LayerWeights({k: small_full.get(k, w[k])[_layer_index(k, i)] for k in SMALL if _layer_index(k, i) is not None},
                           functools.partial(fetch, i)) for i in range(depth)]

    def bucket_of(layer, stage):
        return (layer, 0) if layer > 0 else (0, min(stage, 1))

    sink = GradSink({k: w[k].shape for k in BIG}, bucket_of)
    begun = {}

    def stage_done(i, stage, tie):
        b = bucket_of(i, stage)
        if stage > 0 and bucket_of(i, stage - 1) == b:
            return tie
        begun[b] = reduce_begin(list(sink.bufs[b].values()), f"grads_l{b[0]}_{b[1]}")
        return begun[b][1][-1]

    loss, dx, grads = trunk_fwd_bwd(x[0], p[:, 0], target[0], layers, w["final_norm"], sink, stage_done)

    results = {}

    def finish(buckets, after, tag):
        halves = {b: reduce_end(begun[b], after, f"grads_l{b[0]}_{b[1]}") for b in buckets}
        joined = iter(join_halves([h for b in buckets for h in halves[b]], f"grads_join_{tag}"))
        reduced = {b: {c: next(joined) for c in sink.bufs[b]} for b in buckets}
        last = after
        for (k, li), (b, key, _, off, r) in sink.where.items():
            if b in buckets:
                g = reduced[b][key]
                if r % 8:
                    g, off = g[off:off + r], 0
                results[k] = adamw_layer(w[k], g, off, m[k], v[k], li, results.get(k), f"adamw_{k}_{li}")
                last = results[k][1]
        return last

    order_b = list(begun)
    started_last = begun[order_b[-1]][1][-1]
    finish(order_b[-1:], finish(order_b[:-1], started_last, "early") if len(order_b) > 1 else dx, "last")
    g_out = {k: results[k][0] for k in BIG}
    vec = _pack([loss[0:1, 0:1]] + [grads[k] for k in SMALL], F32, 8)
    vec = sum_devices(all_gather_devices(vec, "gather_vectors"), "sum_vectors")
    parts = _unpack(vec, [(1, 1)] + [grads[k].shape for k in SMALL])
    loss_out = parts[0].reshape(())
    for k, g in zip(SMALL, parts[1:]):
        if k in SHARD_AXIS:
            ax = SHARD_AXIS[k]
            g = lax.dynamic_slice_in_dim(g, chip * w[k].shape[ax], w[k].shape[ax], axis=ax)
        g_out[k] = g

    for k in TRANSPOSED:
        results[k] = [a.transpose(0, 2, 1) for a in results[k]]
    g_out.update({k: results[k][0] for k in TRANSPOSED})
    delta, new_m, new_v = ({k: results[k][j] for k in BIG} for j in (1, 2, 3))
    shapes = [w[k].shape for k in SMALL]
    packed = [_pack([src[k] for k in SMALL], F32, 8) for src in (w, g_out, m, v)]
    outs = adamw(*packed, "adamw_small")
    for dst, o in zip((delta, new_m, new_v), outs):
        for k, a in zip(SMALL, _unpack(o, shapes)):
            dst[k] = a
    return ((loss_out, dx[None]) + tuple(g_out[k] for k in WEIGHTS) + tuple(delta[k] for k in WEIGHTS)
            + tuple(new_m[k] for k in WEIGHTS) + tuple(new_v[k] for k in WEIGHTS))


def kernel(x, p, norm_ffn1, ffn1_w_in, ffn1_w_out, norm_mix, norm_ffn2, ffn2_w_in, ffn2_w_out, ple_norm, ple_gate_w, ple_proj_w, hyb_w_in, conv_dw_w, conv_dw_b, conv_ln_g, conv_ln_b, ssm_conv_w, ssm_conv_b, ssm_dt_bias, ssm_a_log, ssm_d, ssm_norm, hyb_w_out, att_w_qkv, att_b_qkv, att_sinks, att_w_o, att_b_o, final_norm, loss_target, m_norm_ffn1, m_ffn1_w_in, m_ffn1_w_out, m_norm_mix, m_norm_ffn2, m_ffn2_w_in, m_ffn2_w_out, m_ple_norm, m_ple_gate_w, m_ple_proj_w, m_hyb_w_in, m_conv_dw_w, m_conv_dw_b, m_conv_ln_g, m_conv_ln_b, m_ssm_conv_w, m_ssm_conv_b, m_ssm_dt_bias, m_ssm_a_log, m_ssm_d, m_ssm_norm, m_hyb_w_out, m_att_w_qkv, m_att_b_qkv, m_att_sinks, m_att_w_o, m_att_b_o, m_final_norm, v_norm_ffn1, v_ffn1_w_in, v_ffn1_w_out, v_norm_mix, v_norm_ffn2, v_ffn2_w_in, v_ffn2_w_out, v_ple_norm, v_ple_gate_w, v_ple_proj_w, v_hyb_w_in, v_conv_dw_w, v_conv_dw_b, v_conv_ln_g, v_conv_ln_b, v_ssm_conv_w, v_ssm_conv_b, v_ssm_dt_bias, v_ssm_a_log, v_ssm_d, v_ssm_norm, v_hyb_w_out, v_att_w_qkv, v_att_b_qkv, v_att_sinks, v_att_w_o, v_att_b_o, v_final_norm):
    given = locals()
    w = {k: given[k] for k in WEIGHTS}
    m = {k: given["m_" + k] for k in WEIGHTS}
    v = {k: given["v_" + k] for k in WEIGHTS}
    return _step(x, p, loss_target, w, m, v)
```

```python
import functools
import math

import numpy as np
import jax
import jax.numpy as jnp
from jax import lax
from jax.experimental import pallas as pl
from jax.experimental.pallas import tpu as pltpu

F32 = jnp.float32
BF16 = jnp.bfloat16
MXU_DTYPE = jnp.bfloat16
S = jax.ShapeDtypeStruct
MESH = pl.DeviceIdType.MESH

V7X_VMEM_BYTES = 64 * 2**20
VMEM_LIMIT = 48 * 2**20
LANE = 128

EPS = 1e-6
SSM_HEADS = 16
HEAD_DIM = 64
SSM_GROUPS = 2
SSM_STATE = 128
SSM_CONV = 4
CHUNK = 128
CONV_WIDTH = 31
ATT_HEADS = 16
ATT_KV_HEADS = 4
WINDOW = 128
ROPE_THETA = 10000.0
ADAM_LR = 0.001
ADAM_B1 = 0.9
ADAM_B2 = 0.999
ADAM_EPS = 1e-08
ADAM_WD = 0.01
ADAM_STEP = 10

N_CHIPS = 4
N_DEV = 8

NN = ((1,), (0,))
NT = ((1,), (1,))
TN = ((0,), (0,))


def _mm(a, b, dims=NN):
    return lax.dot_general(a.astype(MXU_DTYPE), b.astype(MXU_DTYPE), (dims, ((), ())), preferred_element_type=F32)


def _split3(a):
    hi = a.astype(BF16)
    r = a - hi.astype(F32)
    mid = r.astype(BF16)
    lo = (r - mid.astype(F32)).astype(BF16)
    return hi, mid, lo


def _mm01(a, onehot, dims=NN):
    o = onehot.astype(BF16)
    out = None
    for part in _split3(a):
        t = lax.dot_general(part, o, (dims, ((), ())), preferred_element_type=F32)
        out = t if out is None else out + t
    return out


def _01mm(onehot, a):
    o = onehot.astype(BF16)
    out = None
    for part in _split3(a):
        t = lax.dot_general(o, part, (NN, ((), ())), preferred_element_type=F32)
        out = t if out is None else out + t
    return out


def _sigmoid(x):
    return 0.5 * jnp.tanh(0.5 * x) + 0.5


def _softplus(x):
    return jnp.maximum(x, 0.0) + jnp.log(1.0 + jnp.exp(-jnp.abs(x)))


def _iota(shape, axis):
    return lax.broadcasted_iota(jnp.int32, shape, axis)


def _head_indicator(width, heads, transposed=False):
    per = width // heads
    if transposed:
        return (_iota((heads, width), 1) // per == _iota((heads, width), 0)).astype(F32)
    return (_iota((width, heads), 0) // per == _iota((width, heads), 1)).astype(F32)


def _acc(ref, i, val):
    @pl.when(i == 0)
    def _():
        ref[...] = val

    @pl.when(i > 0)
    def _():
        ref[...] += val


def _rs(tile, width, col=0, shift=0, n=None):
    if shift == 0:
        return pl.BlockSpec((tile, width), lambda i: (i, col))
    if shift < 0:
        return pl.BlockSpec((tile, width), lambda i: (jnp.maximum(i - 1, 0), col))
    return pl.BlockSpec((tile, width), lambda i: (jnp.minimum(i + 1, n - 1), col))


def _ps(shape):
    return pl.BlockSpec(shape, lambda i: (0,) * len(shape))


def _call(body, name, grid, in_specs, out_specs, out_shape, scratch=(), sem=None):
    return pl.pallas_call(
        body, name=name, grid=grid, in_specs=in_specs, out_specs=out_specs, out_shape=out_shape,
        scratch_shapes=list(scratch),
        compiler_params=pltpu.CompilerParams(dimension_semantics=sem, vmem_limit_bytes=VMEM_LIMIT))


def _row_tile(n, target):
    t = min(n, target)
    assert n % t == 0, (n, t)
    return t


def _pick_tile(dim, target):
    if dim <= target:
        return dim
    t = (int(1.4 * target) // LANE) * LANE
    while t >= LANE:
        if dim % t == 0:
            return t
        t -= LANE
    return dim


ANY = pl.BlockSpec(memory_space=pl.ANY)


def _paired(j):
    return (j % 2) * 2 + j // 2


class ColSharded:
    def __init__(self, arr, paired=False):
        self.arr, self.paired = arr, paired
        self.nch, self.rows, self.per = arr.shape
        self.shape = (self.rows, self.nch * self.per)

    def chip(self, j):
        return _paired(j) if self.paired else j


class Slot:
    def __init__(self, buf, kind, per, off, c0=0, paired=False):
        self.buf, self.kind, self.per, self.off, self.c0, self.paired = buf, kind, per, off, c0, paired

    def chip(self, j):
        return _paired(j) if self.paired else j


def matmul(a, b, mode, name, *, out_dtype=F32, scale=None, res=None, bias=None, into=None, tm=1024, tn=1024, tk=1024):
    bshape = b.shape
    if mode == "nn":
        (m, k), (k2, n) = a.shape, bshape
    elif mode == "nt":
        (m, k), (n, k2) = a.shape, bshape
    else:
        (k, m), (k2, n) = a.shape, bshape
    assert k == k2, (a.shape, bshape, mode)
    tm, tn, tk = _pick_tile(m, tm), _pick_tile(n, tn), _pick_tile(k, tk)
    if isinstance(b, ColSharded):
        if mode == "nn":
            tn = b.per
        else:
            assert mode == "nt"
            tk = b.per
    if into is not None:
        if into.kind == "c":
            tn = into.per
            assert into.off % tm == 0 and n == N_CHIPS * into.per
        else:
            tm = max(1, min(m, int(1.4 * 1024)) // into.per) * into.per
            assert m % tm == 0 and into.off % into.per == 0 and into.c0 % (tm // into.per) == 0
    nk = k // tk
    dims = {"nn": NN, "nt": NT, "tn": TN}[mode]
    a_spec = (pl.BlockSpec((tk, tm), lambda i, j, kk: (kk, i)) if mode == "tn"
              else pl.BlockSpec((tm, tk), lambda i, j, kk: (i, kk)))
    if isinstance(b, ColSharded):
        bchip = b.chip
        b_spec = (pl.BlockSpec((None, tk, tn), lambda i, j, kk: (bchip(j), kk, 0)) if mode == "nn"
                  else pl.BlockSpec((None, tn, tk), lambda i, j, kk: (bchip(kk), j, 0)))
        b = b.arr
    else:
        b_spec = (pl.BlockSpec((tn, tk), lambda i, j, kk: (j, kk)) if mode == "nt"
                  else pl.BlockSpec((tk, tn), lambda i, j, kk: (kk, j)))
    plain_o = pl.BlockSpec((tm, tn), lambda i, j, kk: (i, j))
    ins, in_specs = [a, b], [a_spec, b_spec]
    if bias is not None:
        ins.append(bias)
        in_specs.append(pl.BlockSpec((1, tn), lambda i, j, kk: (0, j)))
    if res is not None:
        ins.append(res)
        in_specs.append(plain_o)
    aliases = {}
    if into is None:
        o_spec, o_shape = plain_o, S((m, n), out_dtype)
    else:
        aliases = {len(ins): 0}
        ins.append(into.buf)
        in_specs.append(ANY)
        o_shape = S(into.buf.shape, into.buf.dtype)
        if into.kind == "c":
            ob, ochip = into.off // tm, into.chip
            o_spec = pl.BlockSpec((None, tm, tn), lambda i, j, kk: (ochip(j), ob + i, 0))
        else:
            q, ob = tm // into.per, into.off // into.per
            cb = into.c0 // q
            o_spec = pl.BlockSpec((q, into.per, tn), lambda i, j, kk: (cb + i, ob, j))

    def body(*refs):
        a_ref, b_ref = refs[0], refs[1]
        o_ref, acc_ref = refs[-2], refs[-1]
        kk = pl.program_id(2)

        @pl.when(kk == 0)
        def _():
            acc_ref[...] = jnp.zeros_like(acc_ref)

        acc_ref[...] += _mm(a_ref[...], b_ref[...], dims)

        @pl.when(kk == nk - 1)
        def _():
            out = acc_ref[...]
            if scale is not None:
                out = out * scale
            pos = 2
            if bias is not None:
                out = out + refs[pos][...]
                pos += 1
            if res is not None:
                out = out + refs[pos][...]
            o_ref[...] = out.astype(o_ref.dtype).reshape(o_ref.shape)

    return pl.pallas_call(
        body, name=name, grid=(m // tm, n // tn, nk), in_specs=in_specs, out_specs=o_spec, out_shape=o_shape,
        scratch_shapes=[pltpu.VMEM((tm, tn), F32)], input_output_aliases=aliases,
        compiler_params=pltpu.CompilerParams(dimension_semantics=("parallel", "parallel", "arbitrary"),
                                             vmem_limit_bytes=VMEM_LIMIT))(*ins)


def rms_fwd(h, g, name):
    n, d = h.shape
    tile = _row_tile(n, 512)

    def body(h_ref, g_ref, o_ref):
        x = h_ref[...]
        r = lax.rsqrt(jnp.mean(x * x, axis=-1, keepdims=True) + EPS)
        o_ref[...] = (x * r * g_ref[...]).astype(o_ref.dtype)

    return _call(body, name, (n // tile,), [_rs(tile, d), _ps((1, d))], _rs(tile, d), S((n, d), MXU_DTYPE),
                 sem=("parallel",))(h, g)


def _rms_bwd_math(x, g, dy):
    r = lax.rsqrt(jnp.mean(x * x, axis=-1, keepdims=True) + EPS)
    xh = x * r
    dg = jnp.sum(dy * xh, axis=0, keepdims=True)
    dxh = dy * g
    dx = r * (dxh - xh * jnp.mean(dxh * xh, axis=-1, keepdims=True))
    return dx, dg


def nt_rms_bwd(a, b, h, g, dh_in, name, extra=None, colsum=False, b_kd=False):
    n, k = a.shape
    d = h.shape[1]
    tm = _row_tile(n, 512)
    sharded = isinstance(b, ColSharded)
    tk = b.per if sharded else _pick_tile(k, 1024)
    nk = k // tk
    dims = NN if b_kd else NT
    if sharded:
        bchip = b.chip
        b_spec = pl.BlockSpec((None, d, tk), lambda i, kk: (bchip(kk), 0, 0))
        b = b.arr
    elif b_kd:
        b_spec = pl.BlockSpec((tk, d), lambda i, kk: (kk, 0))
    else:
        b_spec = pl.BlockSpec((d, tk), lambda i, kk: (0, kk))
    row = pl.BlockSpec((tm, d), lambda i, kk: (i, 0))
    vec = pl.BlockSpec((1, d), lambda i, kk: (0, 0))
    ins, in_specs = [a, b, h, g, dh_in], [pl.BlockSpec((tm, tk), lambda i, kk: (i, kk)), b_spec, row, vec, row]
    if extra is not None:
        k2 = extra[0].shape[1]
        ins += list(extra)
        in_specs += [pl.BlockSpec((tm, k2), lambda i, kk: (i, 0)),
                     pl.BlockSpec((k2, d) if b_kd else (d, k2), lambda i, kk: (0, 0))]
    n_in = len(ins)

    def body(*refs):
        a_ref, b_ref, h_ref, g_ref, dh_ref = refs[:5]
        o_ref, dg_ref = refs[n_in], refs[n_in + 1]
        acc_ref = refs[-1]
        i, kk = pl.program_id(0), pl.program_id(1)

        @pl.when(kk == 0)
        def _():
            acc_ref[...] = _mm(refs[5][...], refs[6][...], dims) if extra is not None else jnp.zeros_like(acc_ref)

        acc_ref[...] += _mm(a_ref[...], b_ref[...], dims)

        @pl.when(kk == nk - 1)
        def _():
            dx, dg = _rms_bwd_math(h_ref[...], g_ref[...], acc_ref[...])
            out = dh_ref[...] + dx
            o_ref[...] = out
            _acc(dg_ref, i, dg)
            if colsum:
                _acc(refs[n_in + 2], i, jnp.sum(out, axis=0, keepdims=True))

    n_vec = 2 if colsum else 1
    return pl.pallas_call(
        body, name=name, grid=(n // tm, nk), in_specs=in_specs, out_specs=[row] + [vec] * n_vec,
        out_shape=[S((n, d), F32)] + [S((1, d), F32)] * n_vec, scratch_shapes=[pltpu.VMEM((tm, d), F32)],
        compiler_params=pltpu.CompilerParams(dimension_semantics=("arbitrary", "arbitrary"), vmem_limit_bytes=VMEM_LIMIT),
    )(*ins)


def rms_bwd(h, g, dxn, dh_in, name, colsum=False):
    n, d = h.shape
    tile = _row_tile(n, 256)

    def body(h_ref, g_ref, dxn_ref, dh_ref, o_ref, dg_ref, *cs_ref):
        i = pl.program_id(0)
        dx, dg = _rms_bwd_math(h_ref[...], g_ref[...], dxn_ref[...].astype(F32))
        out = dh_ref[...] + dx
        o_ref[...] = out
        _acc(dg_ref, i, dg)
        if colsum:
            _acc(cs_ref[0], i, jnp.sum(out, axis=0, keepdims=True))

    outs = [S((n, d), F32), S((1, d), F32)] + ([S((1, d), F32)] if colsum else [])
    ospecs = [_rs(tile, d), _ps((1, d))] + ([_ps((1, d))] if colsum else [])
    return _call(body, name, (n // tile,), [_rs(tile, d), _ps((1, d)), _rs(tile, d), _rs(tile, d)], ospecs, outs,
                 sem=("arbitrary",))(h, g, dxn, dh_in)


def swiglu_in(h, g, w_in, name):
    n, d = h.shape
    per = w_in.per
    nj = w_in.nch // 2
    tile = _row_tile(n, 512)

    def body(h_ref, g_ref, wg_ref, wu_ref, xn_ref, u_ref, hm_ref):
        x = h_ref[...]
        r = lax.rsqrt(jnp.mean(x * x, axis=-1, keepdims=True) + EPS)
        xn = (x * r * g_ref[...]).astype(xn_ref.dtype)

        @pl.when(pl.program_id(1) == 0)
        def _():
            xn_ref[...] = xn

        a = _mm(xn, wg_ref[...])
        b = _mm(xn, wu_ref[...])
        u_ref[:, :per] = a.astype(u_ref.dtype)
        u_ref[:, per:] = b.astype(u_ref.dtype)
        hm_ref[...] = (a * _sigmoid(a) * b).astype(hm_ref.dtype)

    return pl.pallas_call(
        body, name=name, grid=(n // tile, nj),
        in_specs=[pl.BlockSpec((tile, d), lambda i, j: (i, 0)), pl.BlockSpec((1, d), lambda i, j: (0, 0)),
                  pl.BlockSpec((None, d, per), lambda i, j: (j, 0, 0)), pl.BlockSpec((None, d, per), lambda i, j: (nj + j, 0, 0))],
        out_specs=[pl.BlockSpec((tile, d), lambda i, j: (i, 0)), pl.BlockSpec((tile, 2 * per), lambda i, j: (i, j)),
                   pl.BlockSpec((tile, per), lambda i, j: (i, j))],
        out_shape=[S((n, d), MXU_DTYPE), S((n, 2 * nj * per), MXU_DTYPE), S((n, nj * per), MXU_DTYPE)],
        compiler_params=pltpu.CompilerParams(dimension_semantics=("parallel", "arbitrary"), vmem_limit_bytes=VMEM_LIMIT),
    )(h, g, w_in.arr, w_in.arr)


def swiglu_out_bwd(dh, w_out, u, after, name):
    n, d = dh.shape
    f = w_out.shape[0]
    per = u.shape[1] // 4
    nj = f // per
    tile = _row_tile(n, 512)

    def body(dh_ref, w_ref, u_ref, after_ref, du_ref):
        dm = 0.5 * _mm(dh_ref[...], w_ref[...], NT)
        a = u_ref[:, :per].astype(F32)
        b = u_ref[:, per:].astype(F32)
        s = _sigmoid(a)
        du_ref[:, :per] = (dm * b * s * (1.0 + a * (1.0 - s))).astype(du_ref.dtype)
        du_ref[:, per:] = (dm * a * s).astype(du_ref.dtype)

    return pl.pallas_call(
        body, name=name, grid=(n // tile, nj),
        in_specs=[pl.BlockSpec((tile, d), lambda i, j: (i, 0)), pl.BlockSpec((per, d), lambda i, j: (j, 0)),
                  pl.BlockSpec((tile, 2 * per), lambda i, j: (i, j)), ANY],
        out_specs=pl.BlockSpec((tile, 2 * per), lambda i, j: (i, j)),
        out_shape=S(u.shape, MXU_DTYPE),
        compiler_params=pltpu.CompilerParams(dimension_semantics=("parallel", "parallel"), vmem_limit_bytes=VMEM_LIMIT),
    )(dh, w_out, u, after)


def ple_fwd(h, gl, pp, name):
    n, d = h.shape
    tile = _row_tile(n, 512)

    def body(h_ref, gl_ref, pp_ref, o_ref):
        o_ref[...] = h_ref[...] + _sigmoid(gl_ref[...]) * pp_ref[...]

    return _call(body, name, (n // tile,), [_rs(tile, d)] * 3, _rs(tile, d), S((n, d), F32), sem=("parallel",))(h, gl, pp)


def ple_bwd(dh, gl, pp, after, name):
    n, d = dh.shape
    tile = _row_tile(n, 512)

    def body(dh_ref, gl_ref, pp_ref, after_ref, dpp_ref, dgl_ref):
        g = _sigmoid(gl_ref[...])
        dh_ = dh_ref[...]
        dpp_ref[...] = (dh_ * g).astype(dpp_ref.dtype)
        dgl_ref[...] = (dh_ * pp_ref[...] * g * (1.0 - g)).astype(dgl_ref.dtype)

    return _call(body, name, (n // tile,), [_rs(tile, d)] * 3 + [ANY], [_rs(tile, d)] * 2, [S((n, d), MXU_DTYPE)] * 2,
                 sem=("parallel",))(dh, gl, pp, after)


def loss_head(h, g, target, name):
    n, d = h.shape
    tile = _row_tile(n, 256)

    def body(h_ref, g_ref, t_ref, dh_ref, dg_ref, loss_ref):
        i = pl.program_id(0)
        x = h_ref[...]
        gg = g_ref[...]
        r = lax.rsqrt(jnp.mean(x * x, axis=-1, keepdims=True) + EPS)
        err = x * r * gg - t_ref[...]
        part = 0.5 * jnp.sum(jnp.mean(err * err, axis=-1, keepdims=True), axis=0, keepdims=True)
        dx, dg = _rms_bwd_math(x, gg, err * (1.0 / d))
        dh_ref[...] = dx
        _acc(dg_ref, i, dg)
        _acc(loss_ref, i, jnp.broadcast_to(part, (8, LANE)))

    return _call(body, name, (n // tile,), [_rs(tile, d), _ps((1, d)), _rs(tile, d)],
                 [_rs(tile, d), _ps((1, d)), _ps((8, LANE))], [S((n, d), F32), S((1, d), F32), S((8, LANE), F32)],
                 sem=("arbitrary",))(h, g, target)


def _adamw_math(w, g, m, v):
    c1 = np.float32(1.0 - ADAM_B1 ** ADAM_STEP)
    c2 = np.float32(1.0 - ADAM_B2 ** ADAM_STEP)
    mm = ADAM_B1 * m + (1.0 - ADAM_B1) * g
    vv = ADAM_B2 * v + (1.0 - ADAM_B2) * (g * g)
    return -ADAM_LR * ((mm / c1) / (jnp.sqrt(vv / c2) + ADAM_EPS) + ADAM_WD * w), mm, vv


def adamw_layer(w, pack, off, m, v, li, prev, name):
    n, r, c = w.shape

    def body(w_ref, g_ref, m_ref, v_ref, *refs):
        go_ref, d_ref, mo_ref, vo_ref = refs[-4:]
        g = g_ref[...]
        go_ref[...] = g
        d_ref[...], mo_ref[...], vo_ref[...] = _adamw_math(w_ref[...], g, m_ref[...], v_ref[...])

    if r % 8 == 0:
        tile = next(t for t in (512, 256, 128, 64, 32, 16, 8) if r % t == 0 and off % t == 0 and t * c * 4 <= 2**21)
        ob, steps = off // tile, r // tile
        blk = pl.BlockSpec((None, tile, c), lambda i: (li, i, 0))
        g_spec = pl.BlockSpec((tile, c), lambda i: (ob + i, 0))
    else:
        assert off == 0 and pack.shape[0] == r and c % (2 * LANE) == 0
        steps = c // (2 * LANE)
        blk = pl.BlockSpec((None, r, 2 * LANE), lambda i: (li, 0, i))
        g_spec = pl.BlockSpec((r, 2 * LANE), lambda i: (0, i))
    prev = list(prev) if prev is not None else []
    return pl.pallas_call(
        body, name=name, grid=(steps,),
        in_specs=[blk, g_spec, blk, blk] + [ANY] * len(prev),
        out_specs=[blk] * 4, out_shape=[S((n, r, c), F32)] * 4,
        input_output_aliases={4 + j: j for j in range(len(prev))},
        compiler_params=pltpu.CompilerParams(dimension_semantics=("parallel",), vmem_limit_bytes=VMEM_LIMIT),
    )(w, pack, m, v, *prev)


def adamw(w, g, m, v, name):
    r, c = w.shape
    tile = r
    for t in (512, 256, 128, 64, 32, 16, 8):
        if r % t == 0 and t * c * 4 <= 2**21:
            tile = t
            break

    def body(w_ref, g_ref, m_ref, v_ref, d_ref, mo_ref, vo_ref):
        d_ref[...], mo_ref[...], vo_ref[...] = _adamw_math(w_ref[...], g_ref[...], m_ref[...], v_ref[...])

    return _call(body, name, (r // tile,), [_rs(tile, c)] * 4, [_rs(tile, c)] * 3, [S((r, c), F32)] * 3,
                 sem=("parallel",))(w, g, m, v)


def _taps_fwd(sc, w_ref, width, halo, tile, acc):
    for k in range(width):
        o = halo - (width - 1) + k
        acc = acc + w_ref[k:k + 1, :] * sc[o:o + tile, :]
    return acc


def _taps_bwd_x(sc_d, w_ref, width, tile, acc):
    for k in range(width):
        o = (width - 1) - k
        acc = acc + w_ref[k:k + 1, :] * sc_d[o:o + tile, :]
    return acc


def _taps_bwd_w(dy, sc, dw_ref, width, halo, tile, i):
    @pl.when(i == 0)
    def _():
        dw_ref[...] = jnp.zeros_like(dw_ref)

    for k in range(width):
        o = halo - (width - 1) + k
        dw_ref[k:k + 1, :] += jnp.sum(dy * sc[o:o + tile, :], axis=0, keepdims=True)


def _ln_stats(x):
    mu = jnp.mean(x, axis=-1, keepdims=True)
    xc = x - mu
    r = lax.rsqrt(jnp.mean(xc * xc, axis=-1, keepdims=True) + EPS)
    return xc * r, r


def conv_group_fwd(proj, cw, cb, lg, lb, name):
    n = proj.shape[0]
    d = cw.shape[1]
    tile = _row_tile(n, 256)
    halo = 32

    def body(v_ref, g_ref, vp_ref, gp_ref, cw_ref, cb_ref, lg_ref, lb_ref, u_ref, u1_ref, sc):
        i = pl.program_id(0)
        first = (i > 0).astype(F32)
        sc[0:halo, :] = vp_ref[tile - halo:, :] * _sigmoid(gp_ref[tile - halo:, :]) * first
        sc[halo:, :] = v_ref[...] * _sigmoid(g_ref[...])
        u1 = _taps_fwd(sc, cw_ref, CONV_WIDTH, halo, tile, jnp.zeros((tile, d), F32) + cb_ref[...])
        u1_ref[...] = u1
        xh, _ = _ln_stats(u1)
        y = xh * lg_ref[...] + lb_ref[...]
        u_ref[...] = (y * _sigmoid(y)).astype(u_ref.dtype)

    return _call(body, name, (n // tile,),
                 [_rs(tile, d, 0), _rs(tile, d, 1), _rs(tile, d, 0, -1), _rs(tile, d, 1, -1),
                  _ps(cw.shape), _ps((1, d)), _ps((1, d)), _ps((1, d))],
                 [_rs(tile, d), _rs(tile, d)], [S((n, d), MXU_DTYPE), S((n, d), F32)],
                 scratch=[pltpu.VMEM((halo + tile, d), F32)], sem=("arbitrary",))(proj, proj, proj, proj, cw, cb, lg, lb)


def conv_group_bwd(du, u1, proj, cw, lg, lb, name):
    n = proj.shape[0]
    d = cw.shape[1]
    tile = _row_tile(n, 256)
    halo = 32
    nt = n // tile

    def body(du_ref, dun_ref, u1_ref, u1n_ref, v_ref, g_ref, vp_ref, gp_ref, cw_ref, lg_ref, lb_ref,
             dp_ref, dcw_ref, dcb_ref, dlg_ref, dlb_ref, sc, sc_d):
        i = pl.program_id(0)

        def ln_swish_bwd(dy_, u1_):
            xh, r = _ln_stats(u1_)
            y = xh * lg_ref[...] + lb_ref[...]
            s = _sigmoid(y)
            dyy = dy_ * s * (1.0 + y * (1.0 - s))
            dxh = dyy * lg_ref[...]
            dx = r * (dxh - jnp.mean(dxh, axis=-1, keepdims=True) - xh * jnp.mean(dxh * xh, axis=-1, keepdims=True))
            return dx, jnp.sum(dyy * xh, axis=0, keepdims=True), jnp.sum(dyy, axis=0, keepdims=True)

        du1, dlg, dlb = ln_swish_bwd(du_ref[...].astype(F32), u1_ref[...])
        du1n, _, _ = ln_swish_bwd(dun_ref[0:halo, :].astype(F32), u1n_ref[0:halo, :])
        sc_d[0:tile, :] = du1
        sc_d[tile:, :] = du1n * (i < nt - 1).astype(F32)
        sig = _sigmoid(g_ref[...])
        val = v_ref[...]
        sc[0:halo, :] = vp_ref[tile - halo:, :] * _sigmoid(gp_ref[tile - halo:, :]) * (i > 0).astype(F32)
        sc[halo:, :] = val * sig
        du0 = _taps_bwd_x(sc_d, cw_ref, CONV_WIDTH, tile, jnp.zeros((tile, d), F32))
        _taps_bwd_w(du1, sc, dcw_ref, CONV_WIDTH, halo, tile, i)
        _acc(dcb_ref, i, jnp.sum(du1, axis=0, keepdims=True))
        _acc(dlg_ref, i, dlg)
        _acc(dlb_ref, i, dlb)
        dp_ref[:, :d] = (du0 * sig).astype(dp_ref.dtype)
        dp_ref[:, d:] = (du0 * val * sig * (1.0 - sig)).astype(dp_ref.dtype)

    return _call(body, name, (nt,),
                 [_rs(tile, d), _rs(tile, d, 0, 1, nt), _rs(tile, d), _rs(tile, d, 0, 1, nt),
                  _rs(tile, d, 0), _rs(tile, d, 1), _rs(tile, d, 0, -1), _rs(tile, d, 1, -1),
                  _ps(cw.shape), _ps((1, d)), _ps((1, d))],
                 [_rs(tile, 2 * d), _ps(cw.shape), _ps((1, d)), _ps((1, d)), _ps((1, d))],
                 [S((n, proj.shape[1]), MXU_DTYPE), S(cw.shape, F32), S((1, d), F32), S((1, d), F32), S((1, d), F32)],
                 scratch=[pltpu.VMEM((halo + tile, d), F32), pltpu.VMEM((tile + halo, d), F32)],
                 sem=("arbitrary",))(du, du, u1, u1, proj, proj, proj, proj, cw, lg, lb)


def ssm_conv_fwd(proj, dtr, sw, sb, dtb, name):
    n = proj.shape[0]
    w = sw.shape[1]
    inner = SSM_HEADS * HEAD_DIM
    tile = _row_tile(n, 256)
    halo = 8

    def body(x_ref, xp_ref, dtr_ref, sw_ref, sb_ref, dtb_ref, pre_ref, xs_ref, bc_ref, dt_ref, sc):
        i = pl.program_id(0)
        sc[0:halo, :] = xp_ref[tile - halo:, :] * (i > 0).astype(F32)
        sc[halo:, :] = x_ref[...]
        pre = _taps_fwd(sc, sw_ref, SSM_CONV, halo, tile, jnp.zeros((tile, w), F32) + sb_ref[...])
        pre_ref[...] = pre
        act = pre * _sigmoid(pre)
        xs_ref[...] = act[:, :inner]
        bc_ref[...] = act[:, inner:]
        dt = _softplus(dtr_ref[...] + dtb_ref[...])
        dt_ref[...] = jnp.where(_iota(dt.shape, 1) < SSM_HEADS, dt, 0.0)

    return _call(body, name, (n // tile,),
                 [_rs(tile, w, 2), _rs(tile, w, 2, -1), _rs(tile, LANE), _ps(sw.shape), _ps((1, w)), _ps((1, LANE))],
                 [_rs(tile, w), _rs(tile, inner), _rs(tile, w - inner), _rs(tile, LANE)],
                 [S((n, w), F32), S((n, inner), F32), S((n, w - inner), F32), S((n, LANE), F32)],
                 scratch=[pltpu.VMEM((halo + tile, w), F32)], sem=("arbitrary",))(proj, proj, dtr, sw, sb, dtb)


def ssm_conv_bwd(dxs, dbc, pre, proj, sw, dproj, name):
    n = proj.shape[0]
    w = sw.shape[1]
    inner = SSM_HEADS * HEAD_DIM
    tile = _row_tile(n, 256)
    halo = 8
    nt = n // tile

    def body(dxs_ref, dxsn_ref, dbc_ref, dbcn_ref, pre_ref, pren_ref, x_ref, xp_ref, sw_ref, dp_in_ref,
             dx_ref, dsw_ref, dsb_ref, sc, sc_d):
        i = pl.program_id(0)

        def silu_bwd(d_, p_):
            s = _sigmoid(p_)
            return d_ * s * (1.0 + p_ * (1.0 - s))

        sc_d[0:tile, :inner] = silu_bwd(dxs_ref[...], pre_ref[:, :inner])
        sc_d[0:tile, inner:] = silu_bwd(dbc_ref[...], pre_ref[:, inner:])
        last = (i < nt - 1).astype(F32)
        sc_d[tile:, :inner] = silu_bwd(dxsn_ref[0:halo, :], pren_ref[0:halo, :inner]) * last
        sc_d[tile:, inner:] = silu_bwd(dbcn_ref[0:halo, :], pren_ref[0:halo, inner:]) * last
        sc[0:halo, :] = xp_ref[tile - halo:, :] * (i > 0).astype(F32)
        sc[halo:, :] = x_ref[...]
        dpre = sc_d[0:tile, :]
        dx_ref[...] = _taps_bwd_x(sc_d, sw_ref, SSM_CONV, tile, jnp.zeros((tile, w), F32)).astype(dx_ref.dtype)
        _taps_bwd_w(dpre, sc, dsw_ref, SSM_CONV, halo, tile, i)
        _acc(dsb_ref, i, jnp.sum(dpre, axis=0, keepdims=True))

    return pl.pallas_call(
        body, name=name, grid=(nt,),
        in_specs=[_rs(tile, inner), _rs(tile, inner, 0, 1, nt), _rs(tile, w - inner), _rs(tile, w - inner, 0, 1, nt),
                  _rs(tile, w), _rs(tile, w, 0, 1, nt), _rs(tile, w, 2), _rs(tile, w, 2, -1), _ps(sw.shape), ANY],
        out_specs=[_rs(tile, w, 2), _ps(sw.shape), _ps((1, w))],
        out_shape=[S(dproj.shape, dproj.dtype), S(sw.shape, F32), S((1, w), F32)],
        scratch_shapes=[pltpu.VMEM((halo + tile, w), F32), pltpu.VMEM((tile + halo, w), F32)],
        input_output_aliases={9: 0},
        compiler_params=pltpu.CompilerParams(dimension_semantics=("arbitrary",), vmem_limit_bytes=VMEM_LIMIT),
    )(dxs, dxs, dbc, dbc, pre, pre, proj, proj, sw, dproj)


def _ssd_prologue(dt_ref, dtT_ref, al_ref, alc_ref):
    row = _iota((CHUNK, CHUNK), 0)
    col = _iota((CHUNK, CHUNK), 1)
    dt = dt_ref[:, :SSM_HEADS]
    a_row = -jnp.exp(al_ref[:, :SSM_HEADS])
    a_col = -jnp.exp(alc_ref[...])
    cs = _01mm((row >= col).astype(F32), dt * a_row)
    csT = _mm01(dtT_ref[...] * a_col, (row <= col).astype(F32))
    return dt, a_row, cs, csT, row, col


def _decay(cs, csT, h, row, col):
    lm = jnp.exp(jnp.where(row >= col, cs[:, h:h + 1] - csT[h:h + 1, :], -1e30))
    lmT = jnp.exp(jnp.where(col >= row, csT[h:h + 1, :] - cs[:, h:h + 1], -1e30))
    return lm, lmT


def ssd_fwd(xs, bc, dt, dtT, alog_row, alog_col, name):
    n, width = xs.shape
    nc = n // CHUNK
    gw = width // SSM_GROUPS
    hpg = SSM_HEADS // SSM_GROUPS
    ns = SSM_STATE

    def body(xs_ref, bc_ref, dt_ref, dtT_ref, al_ref, alc_ref, y_ref, hs_ref, h_sc):
        i = pl.program_id(0)

        @pl.when(i == 0)
        def _():
            h_sc[...] = jnp.zeros_like(h_sc)

        dt, a_row, cs, csT, row, col = _ssd_prologue(dt_ref, dtT_ref, al_ref, alc_ref)
        indT = _head_indicator(width, SSM_HEADS, transposed=True)
        dt_full = _mm01(dt, indT)
        e_full = jnp.exp(_mm01(cs, indT))
        dte_full = jnp.exp(_mm01(cs[CHUNK - 1:CHUNK, :] - cs, indT))
        xt = xs_ref[...] * dt_full
        hs_ref[0] = h_sc[...]
        lo = _iota((CHUNK, 2 * HEAD_DIM), 1) < HEAD_DIM
        for g in range(SSM_GROUPS):
            bg = bc_ref[:, g * ns:(g + 1) * ns]
            cg = bc_ref[:, (SSM_GROUPS + g) * ns:(SSM_GROUPS + g + 1) * ns]
            gm = _mm(cg, bg, NT)
            hg = h_sc[g * gw:(g + 1) * gw, :]
            yoff = e_full[:, g * gw:(g + 1) * gw] * _mm(cg, hg, NT)
            for pr in range(hpg // 2):
                h0 = g * hpg + 2 * pr
                c0 = h0 * HEAD_DIM
                xp = xt[:, c0:c0 + 2 * HEAD_DIM]
                m0 = gm * _decay(cs, csT, h0, row, col)[0]
                m1 = gm * _decay(cs, csT, h0 + 1, row, col)[0]
                yd = jnp.where(lo, _mm(m0, xp), _mm(m1, xp))
                y_ref[:, c0:c0 + 2 * HEAD_DIM] = yd + yoff[:, 2 * pr * HEAD_DIM:(2 * pr + 2) * HEAD_DIM]
            sg = _mm(xt[:, g * gw:(g + 1) * gw] * dte_full[:, g * gw:(g + 1) * gw], bg, TN)
            for hh in range(hpg):
                h = g * hpg + hh
                r0 = h * HEAD_DIM
                h_sc[r0:r0 + HEAD_DIM, :] = (h_sc[r0:r0 + HEAD_DIM, :] * jnp.exp(csT[h:h + 1, CHUNK - 1:CHUNK])
                                             + sg[hh * HEAD_DIM:(hh + 1) * HEAD_DIM, :])

    bcw = bc.shape[1]
    return _call(body, name, (nc,),
                 [_rs(CHUNK, width), _rs(CHUNK, bcw), _rs(CHUNK, LANE), pl.BlockSpec((SSM_HEADS, CHUNK), lambda i: (0, i)),
                  _ps((1, LANE)), _ps((SSM_HEADS, 1))],
                 [_rs(CHUNK, width), pl.BlockSpec((1, width, ns), lambda i: (i, 0, 0))],
                 [S((n, width), F32), S((nc, width, ns), F32)],
                 scratch=[pltpu.VMEM((width, ns), F32)], sem=("arbitrary",))(xs, bc, dt, dtT, alog_row, alog_col)


def ssd_bwd(xs, bc, dt, dtT, alog_row, alog_col, hs, dy, dxs_skip, name):
    n, width = xs.shape
    nc = n // CHUNK
    gw = width // SSM_GROUPS
    hpg = SSM_HEADS // SSM_GROUPS
    ns = SSM_STATE
    bcw = bc.shape[1]

    def body(xs_ref, bc_ref, dt_ref, dtT_ref, al_ref, alc_ref, hs_ref, dy_ref, skip_ref,
             dxs_ref, dbc_ref, ddtr_ref, dal_ref, ddtb_ref, dh_sc, dxt_sc):
        i = pl.program_id(0)

        @pl.when(i == 0)
        def _():
            dh_sc[...] = jnp.zeros_like(dh_sc)

        dt, a_row, cs, csT, row, col = _ssd_prologue(dt_ref, dtT_ref, al_ref, alc_ref)
        indT = _head_indicator(width, SSM_HEADS, transposed=True)
        ind = _head_indicator(width, SSM_HEADS)
        dt_full = _mm01(dt, indT)
        e_full = jnp.exp(_mm01(cs, indT))
        cs_last = cs[CHUNK - 1:CHUNK, :]
        dte = jnp.exp(cs_last - cs)
        dte_full = _mm01(dte, indT)
        xs_ = xs_ref[...]
        xt = xs_ * dt_full
        dy_ = dy_ref[...]
        hmat = hs_ref[0]
        ds = dh_sc[...]
        lo = _iota((CHUNK, 2 * HEAD_DIM), 1) < HEAD_DIM
        head_lane = _iota((1, SSM_HEADS), 1)
        dcs = jnp.zeros((CHUNK, SSM_HEADS), F32)
        ddte = jnp.zeros((CHUNK, SSM_HEADS), F32)
        for g in range(SSM_GROUPS):
            sl = slice(g * gw, (g + 1) * gw)
            bg = bc_ref[:, g * ns:(g + 1) * ns]
            cg = bc_ref[:, (SSM_GROUPS + g) * ns:(SSM_GROUPS + g + 1) * ns]
            gm = _mm(cg, bg, NT)
            gmT = _mm(bg, cg, NT)
            hg = hmat[sl, :]
            dsg = ds[sl, :]
            dyg = dy_[:, sl]
            xtg = xt[:, sl]
            yoff = e_full[:, sl] * _mm(cg, hg, NT)
            edy = e_full[:, sl] * dyg
            bds = _mm(bg, dsg, NT)
            dxt_g = dte_full[:, sl] * bds
            ddte = ddte + _mm01(xtg * bds, ind[sl, :])
            dcs = dcs + _mm01(dyg * yoff, ind[sl, :])
            db = _mm(xtg * dte_full[:, sl], dsg)
            dc = _mm(edy, hg)
            dhc = _mm(edy, cg, TN)
            dgs = jnp.zeros((CHUNK, CHUNK), F32)
            dgTs = jnp.zeros((CHUNK, CHUNK), F32)
            for pr in range(hpg // 2):
                h0 = g * hpg + 2 * pr
                c0 = 2 * pr * HEAD_DIM
                xp = xtg[:, c0:c0 + 2 * HEAD_DIM]
                dyp = dyg[:, c0:c0 + 2 * HEAD_DIM]
                rr = []
                for h, half in ((h0, lo), (h0 + 1, jnp.logical_not(lo))):
                    lm, lmT = _decay(cs, csT, h, row, col)
                    xm = jnp.where(half, xp, 0.0)
                    dm = _mm(dyp, xm, NT)
                    dmT = _mm(xm, dyp, NT)
                    mT = gmT * lmT
                    z = jnp.sum(dm * (gm * lm), axis=1, keepdims=True) - jnp.sum(dmT * mT, axis=1, keepdims=True)
                    dcs = dcs + z * (head_lane == h).astype(F32)
                    dgs = dgs + dm * lm
                    dgTs = dgTs + dmT * lmT
                    rr.append(_mm(mT, dyp))
                dxt_sc[:, g * gw + c0:g * gw + c0 + 2 * HEAD_DIM] = jnp.where(lo, rr[0], rr[1]) + dxt_g[:, c0:c0 + 2 * HEAD_DIM]
            dbc_ref[:, g * ns:(g + 1) * ns] = db + _mm(dgTs, cg)
            dbc_ref[:, (SSM_GROUPS + g) * ns:(SSM_GROUPS + g + 1) * ns] = dc + _mm(dgs, bg)
            for hh in range(hpg):
                h = g * hpg + hh
                r0 = h * HEAD_DIM
                dh_sc[r0:r0 + HEAD_DIM, :] = (dhc[hh * HEAD_DIM:(hh + 1) * HEAD_DIM, :]
                                              + jnp.exp(csT[h:h + 1, CHUNK - 1:CHUNK]) * ds[r0:r0 + HEAD_DIM, :])
        t = ddte * dte
        per_head = jnp.sum(jnp.sum(ds * hmat, axis=1, keepdims=True) * ind, axis=0, keepdims=True)
        last_add = jnp.sum(t, axis=0, keepdims=True) + jnp.exp(cs_last) * per_head
        dcs = dcs - t + jnp.where(_iota((CHUNK, SSM_HEADS), 0) == CHUNK - 1, last_add, 0.0)
        dadt = _01mm((row <= col).astype(F32), dcs)
        dxt = dxt_sc[...]
        ddt = dadt * a_row + _mm01(dxt * xs_, ind)
        dxs_ref[...] = dxt * dt_full + skip_ref[...]
        ddtr = ddt * (1.0 - jnp.exp(-dt))
        ddtr_ref[...] = jnp.zeros_like(ddtr_ref)
        ddtr_ref[:, :SSM_HEADS] = ddtr.astype(ddtr_ref.dtype)
        _acc(dal_ref, i, jnp.sum(dadt * dt, axis=0, keepdims=True) * a_row)
        _acc(ddtb_ref, i, jnp.sum(ddtr, axis=0, keepdims=True))

    rev = lambda i: (nc - 1 - i, 0)
    return _call(body, name, (nc,),
                 [pl.BlockSpec((CHUNK, width), rev), pl.BlockSpec((CHUNK, bcw), rev), pl.BlockSpec((CHUNK, LANE), rev),
                  pl.BlockSpec((SSM_HEADS, CHUNK), lambda i: (0, nc - 1 - i)), _ps((1, LANE)), _ps((SSM_HEADS, 1)),
                  pl.BlockSpec((1, width, ns), lambda i: (nc - 1 - i, 0, 0)), pl.BlockSpec((CHUNK, width), rev),
                  pl.BlockSpec((CHUNK, width), rev)],
                 [pl.BlockSpec((CHUNK, width), rev), pl.BlockSpec((CHUNK, bcw), rev), pl.BlockSpec((CHUNK, LANE), rev),
                  _ps((1, SSM_HEADS)), _ps((1, SSM_HEADS))],
                 [S((n, width), F32), S((n, bcw), F32), S((n, LANE), MXU_DTYPE), S((1, SSM_HEADS), F32), S((1, SSM_HEADS), F32)],
                 scratch=[pltpu.VMEM((width, ns), F32), pltpu.VMEM((CHUNK, width), F32)],
                 sem=("arbitrary",))(xs, bc, dt, dtT, alog_row, alog_col, hs, dy, dxs_skip)


def ssm_gate_fwd(yssd, xs, proj, dfull, gamma, name):
    n, d = yssd.shape
    tile = _row_tile(n, 256)
    gw = d // SSM_GROUPS

    def body(y_ref, xs_ref, z_ref, df_ref, gm_ref, o_ref):
        z = z_ref[...]
        y2 = (y_ref[...] + df_ref[...] * xs_ref[...]) * (z * _sigmoid(z))
        for g in range(SSM_GROUPS):
            yg = y2[:, g * gw:(g + 1) * gw]
            r = lax.rsqrt(jnp.mean(yg * yg, axis=-1, keepdims=True) + EPS)
            o_ref[:, g * gw:(g + 1) * gw] = (yg * r * gm_ref[:, g * gw:(g + 1) * gw]).astype(o_ref.dtype)

    return _call(body, name, (n // tile,), [_rs(tile, d), _rs(tile, d), _rs(tile, d, 2), _ps((1, d)), _ps((1, d))],
                 _rs(tile, d), S((n, d), MXU_DTYPE), sem=("parallel",))(yssd, xs, proj, dfull, gamma)


def ssm_gate_bwd(dy3, yssd, xs, proj, dfull, gamma, dproj, name):
    n, d = yssd.shape
    tile = _row_tile(n, 256)
    gw = d // SSM_GROUPS

    def body(dy_ref, y_ref, xs_ref, z_ref, df_ref, gm_ref, dp_in_ref, dys_ref, dxs_ref, dz_ref, dgm_ref, dd_ref):
        i = pl.program_id(0)
        z = z_ref[...]
        s = _sigmoid(z)
        xs_ = xs_ref[...]
        y1 = y_ref[...] + df_ref[...] * xs_
        y2 = y1 * (z * s)
        dy_ = dy_ref[...].astype(F32)
        dgm = []
        dy2 = []
        for g in range(SSM_GROUPS):
            sl = slice(g * gw, (g + 1) * gw)
            dxg, dgg = _rms_bwd_math(y2[:, sl], gm_ref[:, sl], dy_[:, sl])
            dy2.append(dxg)
            dgm.append(dgg)
        dy2 = jnp.concatenate(dy2, axis=1)
        dy1 = dy2 * (z * s)
        dys_ref[...] = dy1
        dxs_ref[...] = dy1 * df_ref[...]
        dz_ref[...] = (dy2 * y1 * s * (1.0 + z * (1.0 - s))).astype(dz_ref.dtype)
        _acc(dgm_ref, i, jnp.concatenate(dgm, axis=1))
        colsum = jnp.broadcast_to(jnp.sum(dy1 * xs_, axis=0, keepdims=True), (8, d))
        _acc(dd_ref, i, _mm01(colsum, _head_indicator(d, SSM_HEADS))[0:1, :])

    return pl.pallas_call(
        body, name=name, grid=(n // tile,),
        in_specs=[_rs(tile, d), _rs(tile, d), _rs(tile, d), _rs(tile, d, 2), _ps((1, d)), _ps((1, d)), ANY],
        out_specs=[_rs(tile, d), _rs(tile, d), _rs(tile, d, 2), _ps((1, d)), _ps((1, SSM_HEADS))],
        out_shape=[S((n, d), F32), S((n, d), F32), S(dproj.shape, dproj.dtype), S((1, d), F32), S((1, SSM_HEADS), F32)],
        input_output_aliases={6: 2},
        compiler_params=pltpu.CompilerParams(dimension_semantics=("arbitrary",), vmem_limit_bytes=VMEM_LIMIT),
    )(dy3, yssd, xs, proj, dfull, gamma, dproj)


def _rope128(x, cos, sin_signed):
    half = HEAD_DIM // 2
    lane = _iota(x.shape, 1)
    partner = jnp.where((lane % HEAD_DIM) < half, pltpu.roll(x, LANE - half, 1), pltpu.roll(x, half, 1))
    return x * cos + partner * sin_signed


def rope_fwd(qkv, cos, sin, name):
    n, w = qkv.shape
    qw = ATT_HEADS * HEAD_DIM
    kw = ATT_KV_HEADS * HEAD_DIM
    tile = _row_tile(n, 256)

    def body(x_ref, c_ref, s_ref, q_ref, k_ref, v_ref):
        c, s = c_ref[...], s_ref[...]
        for j in range(qw // LANE):
            q_ref[:, j * LANE:(j + 1) * LANE] = _rope128(x_ref[:, j * LANE:(j + 1) * LANE], c, s).astype(q_ref.dtype)
        for j in range(kw // LANE):
            k_ref[:, j * LANE:(j + 1) * LANE] = _rope128(x_ref[:, qw + j * LANE:qw + (j + 1) * LANE], c, s).astype(k_ref.dtype)
        v_ref[...] = x_ref[:, qw + kw:].astype(v_ref.dtype)

    return _call(body, name, (n // tile,), [_rs(tile, w), _rs(tile, LANE), _rs(tile, LANE)],
                 [_rs(tile, qw), _rs(tile, kw), _rs(tile, kw)],
                 [S((n, qw), MXU_DTYPE), S((n, kw), MXU_DTYPE), S((n, kw), MXU_DTYPE)], sem=("parallel",))(qkv, cos, sin)


ATT_GROUP = ATT_HEADS // ATT_KV_HEADS


def _attn_mask(i):
    row = _iota((ATT_GROUP * WINDOW, 2 * WINDOW), 0) % WINDOW
    s = _iota((ATT_GROUP * WINDOW, 2 * WINDOW), 1)
    return (s > row) & (s <= row + WINDOW) & ((s >= WINDOW) | (i > 0))


def _stack_heads(ref, j, kh, lo):
    parts = []
    for t in range(ATT_GROUP):
        h = ATT_GROUP * j + t
        blk = ref[:, (h // 2) * LANE:(h // 2 + 1) * LANE]
        blk = jnp.where(lo if h % 2 == 0 else jnp.logical_not(lo), blk, jnp.zeros_like(blk))
        parts.append(blk if h % 2 == kh else pltpu.roll(blk, HEAD_DIM, 1))
    return jnp.concatenate(parts, axis=0)


def _unstack_heads(stacked, j, kh, lo, put):
    for t in range(0, ATT_GROUP, 2):
        h = ATT_GROUP * j + t
        even = stacked[t * WINDOW:(t + 1) * WINDOW, :]
        odd = stacked[(t + 1) * WINDOW:(t + 2) * WINDOW, :]
        even = even if kh == 0 else pltpu.roll(even, HEAD_DIM, 1)
        odd = odd if kh == 1 else pltpu.roll(odd, HEAD_DIM, 1)
        put(h // 2, jnp.where(lo, even, odd))


def _per_head_rows(ref, j):
    return jnp.concatenate([ref[:, ATT_GROUP * j + t:ATT_GROUP * j + t + 1] for t in range(ATT_GROUP)], axis=0)


def _per_head_scalar(ref, j):
    rows = _iota((ATT_GROUP * WINDOW, 1), 0) // WINDOW
    out = jnp.zeros((ATT_GROUP * WINDOW, 1), F32)
    for t in range(ATT_GROUP):
        out = out + jnp.where(rows == t, ref[:, ATT_GROUP * j + t:ATT_GROUP * j + t + 1], 0.0)
    return out


def attn_fwd(q, k, v, sinks, name):
    n, qw = q.shape
    kw = k.shape[1]
    nb = n // WINDOW
    scale = HEAD_DIM ** -0.5

    def body(q_ref, kc_ref, kp_ref, vc_ref, vp_ref, sk_ref, o_ref, lse_ref):
        i = pl.program_id(0)
        valid = _attn_mask(i)
        lo = _iota((WINDOW, LANE), 1) < HEAD_DIM
        k2 = jnp.concatenate([kp_ref[...], kc_ref[...]], axis=0)
        v2 = jnp.concatenate([vp_ref[...], vc_ref[...]], axis=0)
        lane1 = _iota((1, LANE), 1)
        lse = jnp.zeros((WINDOW, LANE), F32)

        def put_o(qb, val):
            o_ref[:, qb * LANE:(qb + 1) * LANE] = val.astype(o_ref.dtype)

        for j in range(ATT_KV_HEADS):
            kb, kh = j // 2, j % 2
            q4 = _stack_heads(q_ref, j, kh, lo)
            logits = jnp.where(valid, _mm(q4, k2[:, kb * LANE:(kb + 1) * LANE], NT) * scale, -1e30)
            sk = _per_head_scalar(sk_ref, j)
            m = jnp.maximum(jnp.max(logits, axis=-1, keepdims=True), sk)
            e = jnp.exp(logits - m)
            den = jnp.sum(e, axis=-1, keepdims=True) + jnp.exp(sk - m)
            lse4 = m + jnp.log(den)
            for t in range(ATT_GROUP):
                lse = lse + lse4[t * WINDOW:(t + 1) * WINDOW, :] * (lane1 == ATT_GROUP * j + t).astype(F32)
            _unstack_heads(_mm(e * (1.0 / den), v2[:, kb * LANE:(kb + 1) * LANE]), j, kh, lo, put_o)
        lse_ref[...] = lse

    return _call(body, name, (nb,),
                 [_rs(WINDOW, qw), _rs(WINDOW, kw), _rs(WINDOW, kw, 0, -1), _rs(WINDOW, kw), _rs(WINDOW, kw, 0, -1), _ps((1, LANE))],
                 [_rs(WINDOW, qw), _rs(WINDOW, LANE)], [S((n, qw), MXU_DTYPE), S((n, LANE), F32)],
                 sem=("parallel",))(q, k, k, v, v, sinks)


def attn_bwd(q, k, v, o, do, lse, sinks, name):
    n, qw = q.shape
    kw = k.shape[1]
    nb = n // WINDOW
    scale = HEAD_DIM ** -0.5

    def body(q_ref, kc_ref, kp_ref, vc_ref, vp_ref, o_ref, do_ref, lse_ref, sk_ref,
             dq_ref, dka_ref, dkb_ref, dva_ref, dvb_ref, dsk_ref):
        i = pl.program_id(0)
        valid = _attn_mask(i)
        lo = _iota((WINDOW, LANE), 1) < HEAD_DIM
        k2 = jnp.concatenate([kp_ref[...], kc_ref[...]], axis=0)
        v2 = jnp.concatenate([vp_ref[...], vc_ref[...]], axis=0)
        lane1 = _iota((1, LANE), 1)
        do_ = do_ref[...].astype(F32)
        delta = _mm01(do_ * o_ref[...].astype(F32), _head_indicator(qw, ATT_HEADS))
        dk2 = [jnp.zeros((2 * WINDOW, LANE), F32) for _ in range(kw // LANE)]
        dv2 = [jnp.zeros((2 * WINDOW, LANE), F32) for _ in range(kw // LANE)]
        dsk = jnp.zeros((1, LANE), F32)

        def put_dq(qb, val):
            dq_ref[:, qb * LANE:(qb + 1) * LANE] = val

        for j in range(ATT_KV_HEADS):
            kb, kh = j // 2, j % 2
            q4 = _stack_heads(q_ref, j, kh, lo)
            do4 = _stack_heads(do_ref, j, kh, lo)
            kk = k2[:, kb * LANE:(kb + 1) * LANE]
            vv = v2[:, kb * LANE:(kb + 1) * LANE]
            logits = jnp.where(valid, _mm(q4, kk, NT) * scale, -1e30)
            lse4 = _per_head_rows(lse_ref, j)
            p = jnp.exp(logits - lse4)
            dl = jnp.concatenate([delta[:, ATT_GROUP * j + t:ATT_GROUP * j + t + 1] for t in range(ATT_GROUP)], axis=0)
            ds = p * (_mm(do4, vv, NT) - dl) * scale
            sd = jnp.exp(_per_head_scalar(sk_ref, j) - lse4) * dl
            for t in range(ATT_GROUP):
                dsk = dsk - (jnp.sum(sd[t * WINDOW:(t + 1) * WINDOW, :], axis=0, keepdims=True)
                             * (lane1 == ATT_GROUP * j + t).astype(F32))
            _unstack_heads(_mm(ds, kk), j, kh, lo, put_dq)
            dk2[kb] = dk2[kb] + _mm(ds, q4, TN)
            dv2[kb] = dv2[kb] + _mm(p, do4, TN)
        for kb in range(kw // LANE):
            dkb_ref[:, kb * LANE:(kb + 1) * LANE] = dk2[kb][0:WINDOW, :]
            dka_ref[:, kb * LANE:(kb + 1) * LANE] = dk2[kb][WINDOW:, :]
            dvb_ref[:, kb * LANE:(kb + 1) * LANE] = dv2[kb][0:WINDOW, :]
            dva_ref[:, kb * LANE:(kb + 1) * LANE] = dv2[kb][WINDOW:, :]
        _acc(dsk_ref, i, dsk)

    return _call(body, name, (nb,),
                 [_rs(WINDOW, qw), _rs(WINDOW, kw), _rs(WINDOW, kw, 0, -1), _rs(WINDOW, kw), _rs(WINDOW, kw, 0, -1),
                  _rs(WINDOW, qw), _rs(WINDOW, qw), _rs(WINDOW, LANE), _ps((1, LANE))],
                 [_rs(WINDOW, qw)] + [_rs(WINDOW, kw)] * 4 + [_ps((1, LANE))],
                 [S((n, qw), F32)] + [S((n, kw), F32)] * 4 + [S((1, LANE), F32)],
                 sem=("arbitrary",))(q, k, k, v, v, o, do, lse, sinks)


def attn_grad_merge(dq, dka, dkb, dva, dvb, cos, sin, name):
    n, qw = dq.shape
    kw = dka.shape[1]
    nb = n // WINDOW
    w = qw + 2 * kw

    def body(dq_ref, dka_ref, dkb_ref, dva_ref, dvb_ref, c_ref, s_ref, o_ref, db_ref):
        i = pl.program_id(0)
        c, s = c_ref[...], -s_ref[...]
        nxt = (i < nb - 1).astype(F32)

        @pl.when(i == 0)
        def _():
            db_ref[...] = jnp.zeros_like(db_ref)

        def put(c0, val):
            o_ref[:, c0:c0 + val.shape[1]] = val.astype(o_ref.dtype)
            db_ref[:, c0:c0 + val.shape[1]] += jnp.sum(val, axis=0, keepdims=True)

        for j in range(qw // LANE):
            put(j * LANE, _rope128(dq_ref[:, j * LANE:(j + 1) * LANE], c, s))
        for j in range(kw // LANE):
            sl = slice(j * LANE, (j + 1) * LANE)
            put(qw + j * LANE, _rope128(dka_ref[:, sl] + dkb_ref[:, sl] * nxt, c, s))
        put(qw + kw, dva_ref[...] + dvb_ref[...] * nxt)

    return _call(body, name, (nb,),
                 [_rs(WINDOW, qw), _rs(WINDOW, kw), _rs(WINDOW, kw, 0, 1, nb), _rs(WINDOW, kw), _rs(WINDOW, kw, 0, 1, nb),
                  _rs(WINDOW, LANE), _rs(WINDOW, LANE)],
                 [_rs(WINDOW, w), _ps((1, w))], [S((n, w), MXU_DTYPE), S((1, w), F32)],
                 sem=("arbitrary",))(dq, dka, dkb, dva, dvb, cos, sin)


def _row(v):
    return v.reshape(1, -1)


def _pad_lanes(v, width=LANE):
    return jnp.pad(v.reshape(1, -1), ((0, 0), (0, width - v.size)))


class LayerWeights(dict):
    def __init__(self, small, fetch):
        super().__init__(small)
        self.fetch = fetch

    def need(self, k, after):
        if k not in self:
            self[k] = self.fetch(k, after)
        return self[k]


def ffn_fwd(h, g, w, keys, tag):
    xn, u, hm = swiglu_in(h, _row(g), w.need(keys[0], h), f"{tag}_in")
    return matmul(hm, w.need(keys[1], hm), "nn", f"{tag}_out", scale=0.5, res=h), (h, xn, u, hm)


class GradSink:
    ORDER = ("ffn1_w_out", "ffn2_w_out", "ple_gate_w", "att_w_o", "hyb_w_out", "ffn1_w_in", "ffn2_w_in", "att_w_qkv",
             "ple_proj_w", "hyb_w_in")

    def __init__(self, shard_shapes, bucket_of):
        self.where, rows = {}, {}
        for k in self.ORDER:
            n, r, c = shard_shapes[k]
            for li in range(n):
                layer = li if k in PER_LAYER else 2 * li + (0 if k in EVEN_ONLY else 1)
                rows_b = rows.setdefault(bucket_of(layer, _stage(k)), {})
                key = c if r % 32 == 0 else k
                off = -(-rows_b.get(key, (0, c))[0] // r) * r
                rows_b[key] = (-(-(off + r) // 32) * 32, c)
                self.where[k, li] = (bucket_of(layer, _stage(k)), key, "r" if _shard_axis(k) == 1 else "c", off, r)
        self.bufs = {b: {key: lax.empty((N_CHIPS, r, c), MXU_DTYPE) for key, (r, c) in rows_b.items()}
                     for b, rows_b in rows.items()}

    def mm(self, k, li, a, b, name, scale=None, c0=0, paired=False):
        bucket, key, kind, off, r = self.where[k, li]
        buf = self.bufs[bucket][key]
        slot = Slot(buf, kind, r if kind == "r" else buf.shape[2], off, c0, paired)
        self.bufs[bucket][key] = matmul(a, b, "tn", name, scale=scale, into=slot)

    def put(self, k, li, chip_major):
        b, key = self.where[k, li][:2]
        pad = self.bufs[b][key].shape[1] - chip_major.shape[1]
        self.bufs[b][key] = jnp.pad(chip_major.astype(self.bufs[b][key].dtype), ((0, 0), (0, pad), (0, 0)))


def ffn_bwd(dh, g, w_in, w_out, saved, tag, sink, keys, layer, after, colsum=False):
    h, xn, u, hm = saved
    sink.mm(keys[1], layer, hm, dh, f"{tag}_dwout", scale=0.5)
    du = swiglu_out_bwd(dh, w_out, u, after, f"{tag}_dhm")
    sink.mm(keys[0], layer, xn, du, f"{tag}_dwin", paired=True)
    outs = nt_rms_bwd(du, ColSharded(w_in.arr, paired=True), h, _row(g), dh, f"{tag}_dxn", colsum=colsum)
    return (outs[0], outs[1].reshape(-1)) + ((outs[2],) if colsum else ())


def _hyb_params(w):
    d = w["conv_dw_b"].size
    inner = SSM_HEADS * HEAD_DIM
    main = 3 * d + w["ssm_conv_b"].size
    return dict(
        w_main=w["hyb_w_in"][:main], w_dt=jnp.pad(w["hyb_w_in"][main:], ((0, LANE - SSM_HEADS), (0, 0))),
        cw=jnp.pad(w["conv_dw_w"], ((0, 32 - CONV_WIDTH), (0, 0))), cb=_row(w["conv_dw_b"]),
        lg=_row(w["conv_ln_g"]), lb=_row(w["conv_ln_b"]),
        sw=jnp.pad(w["ssm_conv_w"], ((0, 8 - SSM_CONV), (0, 0))), sb=_row(w["ssm_conv_b"]),
        dtb=_pad_lanes(w["ssm_dt_bias"]), al_row=_pad_lanes(w["ssm_a_log"]), al_col=w["ssm_a_log"].reshape(-1, 1),
        dfull=_row(jnp.repeat(w["ssm_d"], HEAD_DIM)), gamma=_row(w["ssm_norm"]), d=d, inner=inner, main=main)


def hyb_fwd(h, w, tag):
    w.need("hyb_w_in", h)
    q = _hyb_params(w)
    xn = rms_fwd(h, _row(w["norm_mix"]), f"{tag}_rms")
    proj = matmul(xn, q["w_main"], "nt", f"{tag}_in")
    dtr = matmul(xn, q["w_dt"], "nt", f"{tag}_in_dt")
    u, u1 = conv_group_fwd(proj, q["cw"], q["cb"], q["lg"], q["lb"], f"{tag}_conv")
    pre, xs, bc, dt = ssm_conv_fwd(proj, dtr, q["sw"], q["sb"], q["dtb"], f"{tag}_sconv")
    dtT = dt[:, :SSM_HEADS].T
    yssd, hs = ssd_fwd(xs, bc, dt, dtT, q["al_row"], q["al_col"], f"{tag}_ssd")
    y = ssm_gate_fwd(yssd, xs, proj, q["dfull"], q["gamma"], f"{tag}_gate")
    wo = w.need("hyb_w_out", u)
    h2 = matmul(u, wo[:q["d"]], "nn", f"{tag}_out_a", res=h)
    h2 = matmul(y, wo[q["d"]:], "nn", f"{tag}_out_b", res=h2)
    return h2, (h, xn, proj, u, u1, pre, xs, bc, dt, dtT, yssd, hs, y)


def hyb_bwd(dh, w, saved, tag, sink, layer):
    q = _hyb_params(w)
    h, xn, proj, u, u1, pre, xs, bc, dt, dtT, yssd, hs, y = saved
    du = matmul(dh, w["hyb_w_out"][:q["d"]], "nt", f"{tag}_du")
    dy3 = matmul(dh, w["hyb_w_out"][q["d"]:], "nt", f"{tag}_dy")
    sink.mm("hyb_w_out", layer, u, dh, f"{tag}_dwo_a", c0=0)
    sink.mm("hyb_w_out", layer, y, dh, f"{tag}_dwo_b", c0=N_CHIPS // 2)
    dproj, dcw, dcb, dlg, dlb = conv_group_bwd(du, u1, proj, q["cw"], q["lg"], q["lb"], f"{tag}_dconv")
    dyssd, dxs_skip, dproj, dgamma, dd = ssm_gate_bwd(dy3, yssd, xs, proj, q["dfull"], q["gamma"], dproj, f"{tag}_dgate")
    dxs, dbc, ddtr, dalog, ddtb = ssd_bwd(xs, bc, dt, dtT, q["al_row"], q["al_col"], hs, dyssd, dxs_skip, f"{tag}_dssd")
    dproj, dsw, dsb = ssm_conv_bwd(dxs, dbc, pre, proj, q["sw"], dproj, f"{tag}_dsconv")
    dw_in = jnp.concatenate([matmul(dproj, xn, "tn", f"{tag}_dwin"),
                             matmul(ddtr, xn, "tn", f"{tag}_dwin_dt")[:SSM_HEADS]], axis=0)
    sink.put("hyb_w_in", layer, dw_in.reshape((N_CHIPS, -1) + dw_in.shape[1:]))
    dh2, dg = nt_rms_bwd(dproj, q["w_main"], h, _row(w["norm_mix"]), dh, f"{tag}_dxn", extra=(ddtr, q["w_dt"]), b_kd=True)
    grads = dict(norm_mix=dg.reshape(-1), conv_dw_w=dcw[:CONV_WIDTH], conv_dw_b=dcb.reshape(-1),
                 conv_ln_g=dlg.reshape(-1), conv_ln_b=dlb.reshape(-1), ssm_conv_w=dsw[:SSM_CONV], ssm_conv_b=dsb.reshape(-1),
                 ssm_dt_bias=ddtb.reshape(-1), ssm_a_log=dalog.reshape(-1), ssm_d=dd.reshape(-1), ssm_norm=dgamma.reshape(-1))
    return dh2, grads


def rope_tables(n):
    half = HEAD_DIM // 2
    inv = ROPE_THETA ** (-jnp.arange(0, HEAD_DIM, 2, dtype=F32) / HEAD_DIM)
    ang = jnp.arange(n, dtype=F32)[:, None] * inv[None, :]
    cos, sin = jnp.cos(ang), jnp.sin(ang)
    reps = LANE // HEAD_DIM
    return jnp.tile(jnp.concatenate([cos, cos], axis=1), (1, reps)), jnp.tile(jnp.concatenate([-sin, sin], axis=1), (1, reps))


def att_fwd(h, w, tables, tag):
    cos, sin = tables
    xn = rms_fwd(h, _row(w["norm_mix"]), f"{tag}_rms")
    qkv = matmul(xn, w.need("att_w_qkv", h), "nn", f"{tag}_qkv", bias=_row(w["att_b_qkv"]))
    q, k, v = rope_fwd(qkv, cos, sin, f"{tag}_rope")
    sinks = _pad_lanes(w["att_sinks"])
    o, lse = attn_fwd(q, k, v, sinks, f"{tag}_attn")
    h2 = matmul(o, w.need("att_w_o", o), "nn", f"{tag}_o", bias=_row(w["att_b_o"]), res=h)
    return h2, (h, xn, q, k, v, o, lse, sinks)


def att_bwd(dh, dh_colsum, w, saved, tables, tag, sink, layer):
    cos, sin = tables
    h, xn, q, k, v, o, lse, sinks = saved
    do = matmul(dh, w["att_w_o"], "nt", f"{tag}_do")
    sink.mm("att_w_o", layer, o, dh, f"{tag}_dwo")
    dq, dka, dkb, dva, dvb, dsk = attn_bwd(q, k, v, o, do, lse, sinks, f"{tag}_dattn")
    dqkv, dbqkv = attn_grad_merge(dq, dka, dkb, dva, dvb, cos, sin, f"{tag}_drope")
    sink.mm("att_w_qkv", layer, xn, dqkv, f"{tag}_dwqkv")
    dh2, dg = nt_rms_bwd(dqkv, w["att_w_qkv"], h, _row(w["norm_mix"]), dh, f"{tag}_dxn")
    grads = dict(norm_mix=dg.reshape(-1), att_b_qkv=dbqkv.reshape(-1), att_sinks=dsk[0, :ATT_HEADS],
                 att_b_o=dh_colsum.reshape(-1))
    return dh2, grads


def ple_block_fwd(h, pe, w, tag):
    xn = rms_fwd(h, _row(w["ple_norm"]), f"{tag}_rms")
    gl = matmul(xn, w.need("ple_gate_w", h), "nn", f"{tag}_gate")
    pp = matmul(pe, w.need("ple_proj_w", h), "nn", f"{tag}_proj")
    return ple_fwd(h, gl, pp, f"{tag}_mix"), (h, xn, gl, pp, pe)


def ple_block_bwd(dh, w, saved, tag, sink, layer, after):
    h, xn, gl, pp, pe = saved
    dpp, dgl = ple_bwd(dh, gl, pp, after, f"{tag}_dmix")
    sink.mm("ple_proj_w", layer, pe, dpp, f"{tag}_dwp")
    sink.mm("ple_gate_w", layer, xn, dgl, f"{tag}_dwg")
    dh2, dg = nt_rms_bwd(dgl, w["ple_gate_w"], h, _row(w["ple_norm"]), dh, f"{tag}_dxn")
    return dh2, dict(ple_norm=dg.reshape(-1))


PER_LAYER = ("norm_ffn1", "ffn1_w_in", "ffn1_w_out", "norm_mix", "norm_ffn2", "ffn2_w_in", "ffn2_w_out",
             "ple_norm", "ple_gate_w", "ple_proj_w")
EVEN_ONLY = ("hyb_w_in", "conv_dw_w", "conv_dw_b", "conv_ln_g", "conv_ln_b", "ssm_conv_w", "ssm_conv_b",
             "ssm_dt_bias", "ssm_a_log", "ssm_d", "ssm_norm", "hyb_w_out")
ODD_ONLY = ("att_w_qkv", "att_b_qkv", "att_sinks", "att_w_o", "att_b_o")


def _layer_index(k, i):
    if k in PER_LAYER:
        return i
    if k in (EVEN_ONLY if i % 2 == 0 else ODD_ONLY):
        return i // 2
    return None


def _stage(k):
    return 0 if k.startswith("ffn1") else (2 if k.startswith(("ffn2", "ple")) else 1)


def trunk_fwd_bwd(x, pe, target, layers, final_norm, sink, stage_done):
    depth = len(layers)
    tables = rope_tables(x.shape[0])
    h = x
    saved = []
    for i, w in enumerate(layers):
        h, s1 = ffn_fwd(h, w["norm_ffn1"], w, ("ffn1_w_in", "ffn1_w_out"), f"l{i}_ffn1")
        if i % 2 == 0:
            h, s2 = hyb_fwd(h, w, f"l{i}_hyb")
        else:
            h, s2 = att_fwd(h, w, tables, f"l{i}_att")
        h, s3 = ffn_fwd(h, w["norm_ffn2"], w, ("ffn2_w_in", "ffn2_w_out"), f"l{i}_ffn2")
        h, s4 = ple_block_fwd(h, pe[i], w, f"l{i}_ple")
        saved.append((s1, s2, s3, s4))
    dh, dgf, loss = loss_head(h, _row(final_norm), target, "loss_head")
    grads = {}
    tie = dgf
    for i in reversed(range(depth)):
        w = layers[i]
        s1, s2, s3, s4 = saved[i]
        dh, g = ple_block_bwd(dh, w, s4, f"l{i}_ple", sink, i, tie)
        odd = i % 2 == 1
        out = ffn_bwd(dh, w["norm_ffn2"], w["ffn2_w_in"], w["ffn2_w_out"], s3, f"l{i}_ffn2", sink,
                      ("ffn2_w_in", "ffn2_w_out"), i, tie, colsum=odd)
        dh = out[0]
        g.update(norm_ffn2=out[1])
        if odd:
            dh, gm = att_bwd(dh, out[2], w, s2, tables, f"l{i}_att", sink, i // 2)
        else:
            dh, gm = hyb_bwd(dh, w, s2, f"l{i}_hyb", sink, i // 2)
        g.update(gm)
        tie = stage_done(i, 1, tie)
        out = ffn_bwd(dh, w["norm_ffn1"], w["ffn1_w_in"], w["ffn1_w_out"], s1, f"l{i}_ffn1", sink,
                      ("ffn1_w_in", "ffn1_w_out"), i, tie)
        dh = out[0]
        g.update(norm_ffn1=out[1])
        tie = stage_done(i, 0, tie)
        for k, v in g.items():
            grads.setdefault(k, []).insert(0, v)
    grads = {k: jnp.stack(v) for k, v in grads.items()}
    grads["final_norm"] = dgf.reshape(-1)
    return loss, dh, grads


def _me():
    return lax.axis_index("x"), lax.axis_index("y"), lax.axis_index("c")


def _flip(v, f):
    return 1 - v if f else v


def _remote(src, dst, send_sems, recv_sems, k, dev):
    return pltpu.make_async_remote_copy(src_ref=src, dst_ref=dst, send_sem=send_sems.at[k], recv_sem=recv_sems.at[k],
                                        device_id=dev, device_id_type=MESH)


CHIP_FLIPS = ((1, 0), (0, 1), (1, 1))
DEV_FLIPS = tuple((fx, fy, fc) for fx in (0, 1) for fy in (0, 1) for fc in (0, 1))[1:]


def all_gather_chips(xs, name):
    na = len(xs)
    halves = [x.shape[0] // 2 for x in xs]
    assert all(x.shape[0] % 2 == 0 for x in xs)

    def body(*refs):
        x_refs, out_refs = refs[:na], refs[na:2 * na]
        send_sems, recv_sems = refs[2 * na:]
        mx, my, mc = _me()
        chip = 2 * mx + my
        sib = (mx, my, 1 - mc)
        peers = [(_flip(mx, fx), _flip(my, fy)) for fx, fy in CHIP_FLIPS]

        def rows(a, ch, hc):
            return out_refs[a].at[ch, pl.ds(hc * halves[a], halves[a]), :]

        def src(a):
            return x_refs[a].at[pl.ds(mc * halves[a], halves[a]), :]

        first = [_remote(src(a), rows(a, chip, mc), send_sems, recv_sems, 6 * a + j, (px, py, mc))
                 for j, (px, py) in enumerate(peers) for a in range(na)]
        for cp in first:
            cp.start()
        passed = []
        for j, (px, py) in enumerate(peers):
            for a in range(na):
                landed = rows(a, 2 * px + py, mc)
                _remote(src(a), landed, send_sems, recv_sems, 6 * a + j, (px, py, mc)).wait_recv()
                fw = _remote(landed, landed, send_sems, recv_sems, 6 * a + 3 + j, sib)
                fw.start()
                passed.append(fw)
        for j, (px, py) in enumerate(peers):
            for a in range(na):
                _remote(src(a), rows(a, 2 * px + py, 1 - mc), send_sems, recv_sems, 6 * a + 3 + j, sib).wait_recv()
        for cp in first + passed:
            cp.wait_send()

    outs = pl.pallas_call(
        body, name=name, out_shape=[S((N_CHIPS,) + x.shape, x.dtype) for x in xs], in_specs=[ANY] * na, out_specs=[ANY] * na,
        scratch_shapes=[pltpu.SemaphoreType.DMA((6 * na,)), pltpu.SemaphoreType.DMA((6 * na,))])(*xs)
    chip = 2 * lax.axis_index("x") + lax.axis_index("y")
    return [lax.dynamic_update_slice_in_dim(o, x[None], chip, axis=0) for o, x in zip(outs, xs)]


HBM = pl.BlockSpec(memory_space=pltpu.HBM)
SEM = pl.BlockSpec(memory_space=pltpu.SEMAPHORE)
DATAFLOW = pltpu.SideEffectType.DATAFLOW_SIDE_EFFECTING


def gather_start(xs, lands, after, name):
    na = len(xs)

    def body(*refs):
        x_refs, land_refs = refs[:na], refs[na:2 * na]
        send_sems, recv_sems = refs[2 * na + 1], refs[2 * na + 2]
        token = refs[-1]
        mx, my, mc = _me()
        chip = 2 * mx + my
        for a in range(na):
            for j, (fx, fy) in enumerate(CHIP_FLIPS):
                _remote(x_refs[a], land_refs[a].at[chip], send_sems, recv_sems, 3 * a + j,
                        (_flip(mx, fx), _flip(my, fy), mc)).start()
        token[...] = jnp.zeros_like(token)

    outs = pl.pallas_call(
        body, name=name,
        out_shape=(pltpu.SemaphoreType.DMA((3 * na,)), pltpu.SemaphoreType.DMA((3 * na,)))
        + tuple(pltpu.HBM(x.shape, x.dtype) for x in xs) + tuple(pltpu.HBM(l.shape, l.dtype) for l in lands)
        + (S((8, LANE), F32),),
        in_specs=[HBM] * (2 * na) + [pl.BlockSpec(memory_space=pl.ANY)],
        out_specs=(SEM, SEM) + (HBM,) * (2 * na) + (pl.BlockSpec(memory_space=pltpu.VMEM),),
        input_output_aliases={a: 2 + a for a in range(2 * na)},
        compiler_params=pltpu.CompilerParams(has_side_effects=DATAFLOW),
    )(*[pltpu.with_memory_space_constraint(t, pltpu.HBM) for t in list(xs) + list(lands)], after)
    return outs[0], outs[1], list(outs[2:2 + na]), list(outs[2 + na:2 + 2 * na])


def gather_wait(send_sems, recv_sems, xs, lands, first, after, name):
    na = len(xs)

    def body(*refs):
        x_refs, land_refs = refs[:na], refs[na:2 * na]
        send_sems, recv_sems = refs[2 * na], refs[2 * na + 1]
        mx, my, mc = _me()
        for a in range(na):
            for j, (fx, fy) in enumerate(CHIP_FLIPS):
                px, py = _flip(mx, fx), _flip(my, fy)
                cp = _remote(x_refs[a], land_refs[a].at[2 * px + py], send_sems, recv_sems, 3 * (first + a) + j, (px, py, mc))
                cp.wait_send()
                cp.wait_recv()

    outs = pl.pallas_call(
        body, name=name,
        out_shape=tuple(pltpu.HBM(x.shape, x.dtype) for x in xs) + tuple(pltpu.HBM(l.shape, l.dtype) for l in lands),
        in_specs=[HBM] * (2 * na) + [SEM, SEM, pl.BlockSpec(memory_space=pl.ANY)], out_specs=(HBM,) * (2 * na),
        input_output_aliases={a: a for a in range(2 * na)},
        compiler_params=pltpu.CompilerParams(has_side_effects=DATAFLOW),
    )(*xs, *lands, send_sems, recv_sems, after)
    return list(outs[na:])


def all_gather_devices(v, name):
    r, l = v.shape

    def body(v_ref, out_ref, send_sems, recv_sems):
        mx, my, mc = _me()
        me = 4 * mx + 2 * my + mc
        peers = [(_flip(mx, fx), _flip(my, fy), _flip(mc, fc)) for fx, fy, fc in DEV_FLIPS]
        sends = [_remote(v_ref, out_ref.at[me], send_sems, recv_sems, j, p) for j, p in enumerate(peers)]
        for cp in sends:
            cp.start()
        for j, (px, py, pc) in enumerate(peers):
            _remote(v_ref, out_ref.at[4 * px + 2 * py + pc], send_sems, recv_sems, j, (px, py, pc)).wait_recv()
        for cp in sends:
            cp.wait_send()

    out = pl.pallas_call(
        body, name=name, out_shape=S((N_DEV, r, l), v.dtype), in_specs=[ANY], out_specs=ANY,
        scratch_shapes=[pltpu.SemaphoreType.DMA((7,)), pltpu.SemaphoreType.DMA((7,))])(v)
    me = 4 * lax.axis_index("x") + 2 * lax.axis_index("y") + lax.axis_index("c")
    return lax.dynamic_update_slice_in_dim(out, v[None], me, axis=0)


def sum_devices(g8, name):
    nd, r, l = g8.shape
    tile = r
    for t in (512, 256, 128, 64, 32, 16, 8):
        if r % t == 0:
            tile = t
            break

    def body(g_ref, o_ref):
        acc = g_ref[0]
        for d in range(1, nd):
            acc = acc + g_ref[d]
        o_ref[...] = acc

    return _call(body, name, (r // tile,), [pl.BlockSpec((nd, tile, l), lambda i: (0, i, 0))], _rs(tile, l), S((r, l), F32),
                 sem=("parallel",))(g8)


def exchange_halves(gs, name):
    na = len(gs)
    nch = gs[0].shape[0]

    def body(*refs):
        g_refs, out_refs = refs[:na], refs[na:2 * na]
        send_sems, recv_sems = refs[2 * na:]
        mx, my, mc = _me()
        sib = (mx, my, 1 - mc)
        cps = []
        for a in range(na):
            half = gs[a].shape[1] // 2
            for j in range(nch):
                cps.append(_remote(g_refs[a].at[j, pl.ds((1 - mc) * half, half), :], out_refs[a].at[j],
                                   send_sems, recv_sems, nch * a + j, sib))
        for cp in cps:
            cp.start()
        for cp in cps:
            cp.wait_recv()
        for cp in cps:
            cp.wait_send()

    return pl.pallas_call(
        body, name=name, out_shape=[S((nch, g.shape[1] // 2, g.shape[2]), g.dtype) for g in gs],
        in_specs=[ANY] * na, out_specs=[ANY] * na,
        scratch_shapes=[pltpu.SemaphoreType.DMA((nch * na,)), pltpu.SemaphoreType.DMA((nch * na,))])(*gs)


def add_halves(g4, got, name):
    nch, r, l = g4.shape
    half = r // 2
    tile = _pick_rows(half)
    nt = half // tile

    def body(g_ref, r_ref, a_ref, own_ref):
        j = pl.program_id(1)
        chip = 2 * lax.axis_index("x") + lax.axis_index("y")
        val = g_ref[0].astype(F32) + r_ref[0].astype(F32)
        a_ref[0] = val.astype(a_ref.dtype)

        @pl.when(j == chip)
        def _():
            own_ref[...] = val

    return pl.pallas_call(
        body, name=name, grid=(nt, nch),
        in_specs=[pl.BlockSpec((1, tile, l), lambda i, j: (j, lax.axis_index("c") * nt + i, 0)),
                  pl.BlockSpec((1, tile, l), lambda i, j: (j, i, 0))],
        out_specs=[pl.BlockSpec((1, tile, l), lambda i, j: (j, i, 0)), pl.BlockSpec((tile, l), lambda i, j: (i, 0))],
        out_shape=[S((nch, half, l), MXU_DTYPE), S((half, l), F32)],
        compiler_params=pltpu.CompilerParams(dimension_semantics=("parallel", "arbitrary"), vmem_limit_bytes=VMEM_LIMIT))(g4, got)


def _pick_rows(r, cap=512):
    for t in (512, 256, 128, 64, 32, 16):
        if t <= cap and r % t == 0:
            return t
    return r


def exchange_chips(parts, name):
    na = len(parts)

    def body(*refs):
        a_refs, out_refs = refs[:na], refs[na:2 * na]
        send_sems, recv_sems = refs[2 * na:]
        mx, my, mc = _me()
        peers = [(_flip(mx, fx), _flip(my, fy)) for fx, fy in CHIP_FLIPS]
        cps = [_remote(a_refs[a].at[2 * px + py], out_refs[a].at[j], send_sems, recv_sems, 3 * a + j, (px, py, mc))
               for j, (px, py) in enumerate(peers) for a in range(na)]
        for cp in cps:
            cp.start()
        for cp in cps:
            cp.wait_recv()
        for cp in cps:
            cp.wait_send()

    return pl.pallas_call(
        body, name=name, out_shape=[S((3,) + p.shape[1:], p.dtype) for p in parts], in_specs=[ANY] * na, out_specs=[ANY] * na,
        scratch_shapes=[pltpu.SemaphoreType.DMA((3 * na,)), pltpu.SemaphoreType.DMA((3 * na,))])(*parts)


def add_chips(own, got, name):
    h, l = own.shape
    tile = _pick_rows(h)

    def body(o_ref, g_ref, out_ref):
        out_ref[...] = ((o_ref[...] + g_ref[0].astype(F32)) + g_ref[1].astype(F32)) + g_ref[2].astype(F32)

    nt = h // tile
    return _call(body, name, (nt,), [_rs(tile, l), pl.BlockSpec((3, tile, l), lambda i: (0, i, 0))],
                 pl.BlockSpec((tile, l), lambda i: (lax.axis_index("c") * nt + i, 0)),
                 S((2 * h, l), F32), sem=("parallel",))(own, got)


def join_halves(bufs, name):
    na = len(bufs)

    def body(*refs):
        out_refs = refs[na:2 * na]
        send_sems, recv_sems = refs[2 * na:]
        mx, my, mc = _me()
        sib = (mx, my, 1 - mc)

        def half(a, hc):
            h = bufs[a].shape[0] // 2
            return out_refs[a].at[pl.ds(hc * h, h), :]

        cps = [_remote(half(a, mc), half(a, mc), send_sems, recv_sems, a, sib) for a in range(na)]
        for cp in cps:
            cp.start()
        for a in range(na):
            _remote(half(a, mc), half(a, 1 - mc), send_sems, recv_sems, a, sib).wait_recv()
        for cp in cps:
            cp.wait_send()

    return pl.pallas_call(
        body, name=name, out_shape=[S(b.shape, b.dtype) for b in bufs], in_specs=[ANY] * na, out_specs=[ANY] * na,
        input_output_aliases={a: a for a in range(na)},
        scratch_shapes=[pltpu.SemaphoreType.DMA((na,)), pltpu.SemaphoreType.DMA((na,))])(*bufs)


def exchange_chips_start(parts, name):
    na = len(parts)
    lands = [lax.empty((3,) + p.shape[1:], p.dtype) for p in parts]

    def body(*refs):
        a_refs, land_refs = refs[:na], refs[na:2 * na]
        send_sems, recv_sems = refs[2 * na], refs[2 * na + 1]
        mx, my, mc = _me()
        for j, (fx, fy) in enumerate(CHIP_FLIPS):
            px, py = _flip(mx, fx), _flip(my, fy)
            for a in range(na):
                _remote(a_refs[a].at[2 * px + py], land_refs[a].at[j], send_sems, recv_sems, 3 * a + j, (px, py, mc)).start()
        refs[-1][...] = jnp.zeros_like(refs[-1])

    outs = pl.pallas_call(
        body, name=name,
        out_shape=(pltpu.SemaphoreType.DMA((3 * na,)), pltpu.SemaphoreType.DMA((3 * na,)))
        + tuple(pltpu.HBM(t.shape, t.dtype) for t in list(parts) + lands) + (S((8, LANE), F32),),
        in_specs=[HBM] * (2 * na), out_specs=(SEM, SEM) + (HBM,) * (2 * na) + (pl.BlockSpec(memory_space=pltpu.VMEM),),
        input_output_aliases={a: 2 + a for a in range(2 * na)},
        compiler_params=pltpu.CompilerParams(has_side_effects=DATAFLOW),
    )(*[pltpu.with_memory_space_constraint(t, pltpu.HBM) for t in list(parts) + lands])
    return outs[0], outs[1], list(outs[2:2 + na]), list(outs[2 + na:2 + 2 * na]), outs[-1]


def exchange_chips_wait(send_sems, recv_sems, parts, lands, after, name):
    na = len(parts)

    def body(*refs):
        a_refs, land_refs = refs[:na], refs[na:2 * na]
        send_sems, recv_sems = refs[2 * na], refs[2 * na + 1]
        mx, my, mc = _me()
        for j, (fx, fy) in enumerate(CHIP_FLIPS):
            px, py = _flip(mx, fx), _flip(my, fy)
            for a in range(na):
                cp = _remote(a_refs[a].at[2 * px + py], land_refs[a].at[j], send_sems, recv_sems, 3 * a + j, (px, py, mc))
                cp.wait_send()
                cp.wait_recv()

    outs = pl.pallas_call(
        body, name=name, out_shape=tuple(pltpu.HBM(t.shape, t.dtype) for t in list(parts) + list(lands)),
        in_specs=[HBM] * (2 * na) + [SEM, SEM, pl.BlockSpec(memory_space=pl.ANY)], out_specs=(HBM,) * (2 * na),
        input_output_aliases={a: a for a in range(2 * na)},
        compiler_params=pltpu.CompilerParams(has_side_effects=DATAFLOW),
    )(*parts, *lands, send_sems, recv_sems, after)
    return list(outs[na:])


def reduce_begin(gs, tag):
    got = exchange_halves(gs, f"{tag}_d2d")
    sums = [add_halves(g, r, f"{tag}_add1_{i}") for i, (g, r) in enumerate(zip(gs, got))]
    return [own for _, own in sums], exchange_chips_start([a for a, _ in sums], f"{tag}_ici_start")


def reduce_end(state, after, tag):
    owns, (send_sems, recv_sems, parts, lands, _) = state
    got = exchange_chips_wait(send_sems, recv_sems, parts, lands, after, f"{tag}_ici_wait")
    return [add_chips(own, r, f"{tag}_add2_{i}") for i, (own, r) in enumerate(zip(owns, got))]


PACK_L = 1024
BIG_ROW_MULT = 512


def _pack(arrs, dtype, row_mult, lead=None):
    lead_shape = () if lead is None else arrs[0].shape[:lead]
    flat = jnp.concatenate([a.astype(dtype).reshape(lead_shape + (-1,)) for a in arrs], axis=-1)
    n = flat.shape[-1]
    unit = row_mult * PACK_L
    total = -(-n // unit) * unit
    flat = jnp.pad(flat, [(0, 0)] * len(lead_shape) + [(0, total - n)])
    return flat.reshape(lead_shape + (total // PACK_L, PACK_L))


def _unpack(packed, shapes, lead=None):
    lead_shape = () if lead is None else packed.shape[:lead]
    flat = packed.reshape(lead_shape + (-1,))
    out, off = [], 0
    for shp in shapes:
        n = int(np.prod(shp))
        out.append(flat[..., off:off + n].reshape(lead_shape + tuple(shp)))
        off += n
    return out


def _to_full(gathered, axis):
    t = jnp.moveaxis(gathered, 0, axis)
    shp = t.shape
    return t.reshape(shp[:axis] + (shp[axis] * shp[axis + 1],) + shp[axis + 2:])


def _to_chip_major(full, axis):
    shp = full.shape
    t = full.reshape(shp[:axis] + (N_CHIPS, shp[axis] // N_CHIPS) + shp[axis + 1:])
    return jnp.moveaxis(t, axis, 0)


WEIGHTS = ("norm_ffn1", "ffn1_w_in", "ffn1_w_out", "norm_mix", "norm_ffn2", "ffn2_w_in", "ffn2_w_out", "ple_norm",
           "ple_gate_w", "ple_proj_w", "hyb_w_in", "conv_dw_w", "conv_dw_b", "conv_ln_g", "conv_ln_b", "ssm_conv_w",
           "ssm_conv_b", "ssm_dt_bias", "ssm_a_log", "ssm_d", "ssm_norm", "hyb_w_out", "att_w_qkv", "att_b_qkv",
           "att_sinks", "att_w_o", "att_b_o", "final_norm")
SHARD_AXIS = dict(ffn1_w_in=2, ffn1_w_out=1, ffn2_w_in=2, ffn2_w_out=1, ple_gate_w=1, ple_proj_w=2, hyb_w_in=2,
                  conv_dw_w=2, ssm_conv_w=2, hyb_w_out=1, att_w_qkv=2, att_b_qkv=1, att_w_o=1, att_b_o=1)
BIG = ("ffn1_w_in", "ffn1_w_out", "ffn2_w_in", "ffn2_w_out", "ple_gate_w", "ple_proj_w", "hyb_w_in", "hyb_w_out",
       "att_w_qkv", "att_w_o")
TRANSPOSED = ("hyb_w_in",)


def _shard_axis(k):
    return 1 if k in TRANSPOSED else SHARD_AXIS[k]
SMALL_SHARDED = ("conv_dw_w", "ssm_conv_w", "att_b_qkv", "att_b_o")
SMALL = tuple(k for k in WEIGHTS if k not in BIG)


def _step(x, p, target, w, m, v):
    mx, my = lax.axis_index("x"), lax.axis_index("y")
    chip = 2 * mx + my
    w, m, v = ({k: (a.transpose(0, 2, 1) if k in TRANSPOSED else a) for k, a in d.items()} for d in (w, m, v))

    depth = w["norm_ffn1"].shape[0]
    order = sorted([(k, i) for i in range(depth) for k in BIG if _layer_index(k, i) is not None],
                   key=lambda t: (t[1], _stage(t[0])))
    small_g = all_gather_devices(_pack([w[k] for k in SMALL_SHARDED], F32, 8), "gather_small")
    shards = [w[k][_layer_index(k, i)].astype(MXU_DTYPE) for k, i in order]
    lands = [lax.dynamic_update_slice_in_dim(lax.empty((N_CHIPS,) + s.shape, s.dtype), s[None], chip, axis=0) for s in shards]
    send_sems, recv_sems, shards, lands = gather_start(shards, lands, small_g, "gather_start")

    def fetch(i, k, after):
        p = order.index((k, i))
        g, = gather_wait(send_sems, recv_sems, [shards[p]], [lands[p]], p, after, f"gather_wait_l{i}_{k}")
        if _shard_axis(k) == 2:
            return ColSharded(g)
        return g.reshape(-1, g.shape[-1])

    small_g = small_g[0::2]
    small_full = {k: _to_full(g, SHARD_AXIS[k])
                  for k, g in zip(SMALL_SHARDED, _unpack(small_g, [w[k].shape for k in SMALL_SHARDED], lead=1))}
    layers = [LayerWeights({k: small_full.get(k, w[k])[_layer_index(k, i)] for k in SMALL if _layer_index(k, i) is not None},
                           functools.partial(fetch, i)) for i in range(depth)]

    def bucket_of(layer, stage):
        return (layer, 0) if layer > 0 else (0, min(stage, 1))

    sink = GradSink({k: w[k].shape for k in BIG}, bucket_of)
    begun = {}

    def stage_done(i, stage, tie):
        b = bucket_of(i, stage)
        if stage > 0 and bucket_of(i, stage - 1) == b:
            return tie
        begun[b] = reduce_begin(list(sink.bufs[b].values()), f"grads_l{b[0]}_{b[1]}")
        return begun[b][1][-1]

    loss, dx, grads = trunk_fwd_bwd(x[0], p[:, 0], target[0], layers, w["final_norm"], sink, stage_done)

    results = {}

    def finish(buckets, after, tag):
        halves = {b: reduce_end(begun[b], after, f"grads_l{b[0]}_{b[1]}") for b in buckets}
        joined = iter(join_halves([h for b in buckets for h in halves[b]], f"grads_join_{tag}"))
        reduced = {b: {c: next(joined) for c in sink.bufs[b]} for b in buckets}
        last = after
        for (k, li), (b, key, _, off, r) in sink.where.items():
            if b in buckets:
                g = reduced[b][key]
                if r % 8:
                    g, off = g[off:off + r], 0
                results[k] = adamw_layer(w[k], g, off, m[k], v[k], li, results.get(k), f"adamw_{k}_{li}")
                last = results[k][1]
        return last

    order_b = list(begun)
    started_last = begun[order_b[-1]][1][-1]
    finish(order_b[-1:], finish(order_b[:-1], started_last, "early") if len(order_b) > 1 else dx, "last")
    g_out = {k: results[k][0] for k in BIG}
    vec = _pack([loss[0:1, 0:1]] + [grads[k] for k in SMALL], F32, 8)
    vec = sum_devices(all_gather_devices(vec, "gather_vectors"), "sum_vectors")
    parts = _unpack(vec, [(1, 1)] + [grads[k].shape for k in SMALL])
    loss_out = parts[0].reshape(())
    for k, g in zip(SMALL, parts[1:]):
        if k in SHARD_AXIS:
            ax = SHARD_AXIS[k]
            g = lax.dynamic_slice_in_dim(g, chip * w[k].shape[ax], w[k].shape[ax], axis=ax)
        g_out[k] = g

    for k in TRANSPOSED:
        results[k] = [a.transpose(0, 2, 1) for a in results[k]]
    g_out.update({k: results[k][0] for k in TRANSPOSED})
    delta, new_m, new_v = ({k: results[k][j] for k in BIG} for j in (1, 2, 3))
    shapes = [w[k].shape for k in SMALL]
    packed = [_pack([src[k] for k in SMALL], F32, 8) for src in (w, g_out, m, v)]
    outs = adamw(*packed, "adamw_small")
    for dst, o in zip((delta, new_m, new_v), outs):
        for k, a in zip(SMALL, _unpack(o, shapes)):
            dst[k] = a
    return ((loss_out, dx[None]) + tuple(g_out[k] for k in WEIGHTS) + tuple(delta[k] for k in WEIGHTS)
            + tuple(new_m[k] for k in WEIGHTS) + tuple(new_v[k] for k in WEIGHTS))


def kernel(x, p, norm_ffn1, ffn1_w_in, ffn1_w_out, norm_mix, norm_ffn2, ffn2_w_in, ffn2_w_out, ple_norm, ple_gate_w, ple_proj_w, hyb_w_in, conv_dw_w, conv_dw_b, conv_ln_g, conv_ln_b, ssm_conv_w, ssm_conv_b, ssm_dt_bias, ssm_a_log, ssm_d, ssm_norm, hyb_w_out, att_w_qkv, att_b_qkv, att_sinks, att_w_o, att_b_o, final_norm, loss_target, m_norm_ffn1, m_ffn1_w_in, m_ffn1_w_out, m_norm_mix, m_norm_ffn2, m_ffn2_w_in, m_ffn2_w_out, m_ple_norm, m_ple_gate_w, m_ple_proj_w, m_hyb_w_in, m_conv_dw_w, m_conv_dw_b, m_conv_ln_g, m_conv_ln_b, m_ssm_conv_w, m_ssm_conv_b, m_ssm_dt_bias, m_ssm_a_log, m_ssm_d, m_ssm_norm, m_hyb_w_out, m_att_w_qkv, m_att_b_qkv, m_att_sinks, m_att_w_o, m_att_b_o, m_final_norm, v_norm_ffn1, v_ffn1_w_in, v_ffn1_w_out, v_norm_mix, v_norm_ffn2, v_ffn2_w_in, v_ffn2_w_out, v_ple_norm, v_ple_gate_w, v_ple_proj_w, v_hyb_w_in, v_conv_dw_w, v_conv_dw_b, v_conv_ln_g, v_conv_ln_b, v_ssm_conv_w, v_ssm_conv_b, v_ssm_dt_bias, v_ssm_a_log, v_ssm_d, v_ssm_norm, v_hyb_w_out, v_att_w_qkv, v_att_b_qkv, v_att_sinks, v_att_w_o, v_att_b_o, v_final_norm):
    given = locals()
    w = {k: given[k] for k in WEIGHTS}
    m = {k: given["m_" + k] for k in WEIGHTS}
    v = {k: given["v_" + k] for k in WEIGHTS}
    return _step(x, p, loss_target, w, m, v)
```

```python
import functools
import math

import numpy as np
import jax
import jax.numpy as jnp
from jax import lax
from jax.experimental import pallas as pl
from jax.experimental.pallas import tpu as pltpu

F32 = jnp.float32
BF16 = jnp.bfloat16
MXU_DTYPE = jnp.bfloat16
S = jax.ShapeDtypeStruct
MESH = pl.DeviceIdType.MESH

V7X_VMEM_BYTES = 64 * 2**20
VMEM_LIMIT = 48 * 2**20
LANE = 128

EPS = 1e-6
SSM_HEADS = 16
HEAD_DIM = 64
SSM_GROUPS = 2
SSM_STATE = 128
SSM_CONV = 4
CHUNK = 128
CONV_WIDTH = 31
ATT_HEADS = 16
ATT_KV_HEADS = 4
WINDOW = 128
ROPE_THETA = 10000.0
ADAM_LR = 0.001
ADAM_B1 = 0.9
ADAM_B2 = 0.999
ADAM_EPS = 1e-08
ADAM_WD = 0.01
ADAM_STEP = 10

N_CHIPS = 4
N_DEV = 8

NN = ((1,), (0,))
NT = ((1,), (1,))
TN = ((0,), (0,))


def _mm(a, b, dims=NN):
    return lax.dot_general(a.astype(MXU_DTYPE), b.astype(MXU_DTYPE), (dims, ((), ())), preferred_element_type=F32)


def _split3(a):
    hi = a.astype(BF16)
    r = a - hi.astype(F32)
    mid = r.astype(BF16)
    lo = (r - mid.astype(F32)).astype(BF16)
    return hi, mid, lo


def _mm01(a, onehot, dims=NN):
    o = onehot.astype(BF16)
    out = None
    for part in _split3(a):
        t = lax.dot_general(part, o, (dims, ((), ())), preferred_element_type=F32)
        out = t if out is None else out + t
    return out


def _01mm(onehot, a):
    o = onehot.astype(BF16)
    out = None
    for part in _split3(a):
        t = lax.dot_general(o, part, (NN, ((), ())), preferred_element_type=F32)
        out = t if out is None else out + t
    return out


def _sigmoid(x):
    return 0.5 * jnp.tanh(0.5 * x) + 0.5


def _softplus(x):
    return jnp.maximum(x, 0.0) + jnp.log(1.0 + jnp.exp(-jnp.abs(x)))


def _iota(shape, axis):
    return lax.broadcasted_iota(jnp.int32, shape, axis)


def _head_indicator(width, heads, transposed=False):
    per = width // heads
    if transposed:
        return (_iota((heads, width), 1) // per == _iota((heads, width), 0)).astype(F32)
    return (_iota((width, heads), 0) // per == _iota((width, heads), 1)).astype(F32)


def _acc(ref, i, val):
    @pl.when(i == 0)
    def _():
        ref[...] = val

    @pl.when(i > 0)
    def _():
        ref[...] += val


def _rs(tile, width, col=0, shift=0, n=None):
    if shift == 0:
        return pl.BlockSpec((tile, width), lambda i: (i, col))
    if shift < 0:
        return pl.BlockSpec((tile, width), lambda i: (jnp.maximum(i - 1, 0), col))
    return pl.BlockSpec((tile, width), lambda i: (jnp.minimum(i + 1, n - 1), col))


def _ps(shape):
    return pl.BlockSpec(shape, lambda i: (0,) * len(shape))


def _call(body, name, grid, in_specs, out_specs, out_shape, scratch=(), sem=None):
    return pl.pallas_call(
        body, name=name, grid=grid, in_specs=in_specs, out_specs=out_specs, out_shape=out_shape,
        scratch_shapes=list(scratch),
        compiler_params=pltpu.CompilerParams(dimension_semantics=sem, vmem_limit_bytes=VMEM_LIMIT))


def _row_tile(n, target):
    t = min(n, target)
    assert n % t == 0, (n, t)
    return t


def _pick_tile(dim, target):
    if dim <= target:
        return dim
    t = (int(1.4 * target) // LANE) * LANE
    while t >= LANE:
        if dim % t == 0:
            return t
        t -= LANE
    return dim


ANY = pl.BlockSpec(memory_space=pl.ANY)


def _paired(j):
    return (j % 2) * 2 + j // 2


class ColSharded:
    def __init__(self, arr, paired=False):
        self.arr, self.paired = arr, paired
        self.nch, self.rows, self.per = arr.shape
        self.shape = (self.rows, self.nch * self.per)

    def chip(self, j):
        return _paired(j) if self.paired else j


class Slot:
    def __init__(self, buf, kind, per, off, c0=0, paired=False):
        self.buf, self.kind, self.per, self.off, self.c0, self.paired = buf, kind, per, off, c0, paired

    def chip(self, j):
        return _paired(j) if self.paired else j


def matmul(a, b, mode, name, *, out_dtype=F32, scale=None, res=None, bias=None, into=None, tm=1024, tn=1024, tk=1024):
    bshape = b.shape
    if mode == "nn":
        (m, k), (k2, n) = a.shape, bshape
    elif mode == "nt":
        (m, k), (n, k2) = a.shape, bshape
    else:
        (k, m), (k2, n) = a.shape, bshape
    assert k == k2, (a.shape, bshape, mode)
    tm, tn, tk = _pick_tile(m, tm), _pick_tile(n, tn), _pick_tile(k, tk)
    if isinstance(b, ColSharded):
        if mode == "nn":
            tn = b.per
        else:
            assert mode == "nt"
            tk = b.per
    if into is not None:
        if into.kind == "c":
            tn = into.per
            assert into.off % tm == 0 and n == N_CHIPS * into.per
        else:
            tm = max(1, min(m, int(1.4 * 1024)) // into.per) * into.per
            assert m % tm == 0 and into.off % into.per == 0 and into.c0 % (tm // into.per) == 0
    nk = k // tk
    dims = {"nn": NN, "nt": NT, "tn": TN}[mode]
    a_spec = (pl.BlockSpec((tk, tm), lambda i, j, kk: (kk, i)) if mode == "tn"
              else pl.BlockSpec((tm, tk), lambda i, j, kk: (i, kk)))
    if isinstance(b, ColSharded):
        bchip = b.chip
        b_spec = (pl.BlockSpec((None, tk, tn), lambda i, j, kk: (bchip(j), kk, 0)) if mode == "nn"
                  else pl.BlockSpec((None, tn, tk), lambda i, j, kk: (bchip(kk), j, 0)))
        b = b.arr
    else:
        b_spec = (pl.BlockSpec((tn, tk), lambda i, j, kk: (j, kk)) if mode == "nt"
                  else pl.BlockSpec((tk, tn), lambda i, j, kk: (kk, j)))
    plain_o = pl.BlockSpec((tm, tn), lambda i, j, kk: (i, j))
    ins, in_specs = [a, b], [a_spec, b_spec]
    if bias is not None:
        ins.append(bias)
        in_specs.append(pl.BlockSpec((1, tn), lambda i, j, kk: (0, j)))
    if res is not None:
        ins.append(res)
        in_specs.append(plain_o)
    aliases = {}
    if into is None:
        o_spec, o_shape = plain_o, S((m, n), out_dtype)
    else:
        aliases = {len(ins): 0}
        ins.append(into.buf)
        in_specs.append(ANY)
        o_shape = S(into.buf.shape, into.buf.dtype)
        if into.kind == "c":
            ob, ochip = into.off // tm, into.chip
            o_spec = pl.BlockSpec((None, tm, tn), lambda i, j, kk: (ochip(j), ob + i, 0))
        else:
            q, ob = tm // into.per, into.off // into.per
            cb = into.c0 // q
            o_spec = pl.BlockSpec((q, into.per, tn), lambda i, j, kk: (cb + i, ob, j))

    def body(*refs):
        a_ref, b_ref = refs[0], refs[1]
        o_ref, acc_ref = refs[-2], refs[-1]
        kk = pl.program_id(2)

        @pl.when(kk == 0)
        def _():
            acc_ref[...] = jnp.zeros_like(acc_ref)

        acc_ref[...] += _mm(a_ref[...], b_ref[...], dims)

        @pl.when(kk == nk - 1)
        def _():
            out = acc_ref[...]
            if scale is not None:
                out = out * scale
            pos = 2
            if bias is not None:
                out = out + refs[pos][...]
                pos += 1
            if res is not None:
                out = out + refs[pos][...]
            o_ref[...] = out.astype(o_ref.dtype).reshape(o_ref.shape)

    return pl.pallas_call(
        body, name=name, grid=(m // tm, n // tn, nk), in_specs=in_specs, out_specs=o_spec, out_shape=o_shape,
        scratch_shapes=[pltpu.VMEM((tm, tn), F32)], input_output_aliases=aliases,
        compiler_params=pltpu.CompilerParams(dimension_semantics=("parallel", "parallel", "arbitrary"),
                                             vmem_limit_bytes=VMEM_LIMIT))(*ins)


def rms_fwd(h, g, name):
    n, d = h.shape
    tile = _row_tile(n, 512)

    def body(h_ref, g_ref, o_ref):
        x = h_ref[...]
        r = lax.rsqrt(jnp.mean(x * x, axis=-1, keepdims=True) + EPS)
        o_ref[...] = (x * r * g_ref[...]).astype(o_ref.dtype)

    return _call(body, name, (n // tile,), [_rs(tile, d), _ps((1, d))], _rs(tile, d), S((n, d), MXU_DTYPE),
                 sem=("parallel",))(h, g)


def _rms_bwd_math(x, g, dy):
    r = lax.rsqrt(jnp.mean(x * x, axis=-1, keepdims=True) + EPS)
    xh = x * r
    dg = jnp.sum(dy * xh, axis=0, keepdims=True)
    dxh = dy * g
    dx = r * (dxh - xh * jnp.mean(dxh * xh, axis=-1, keepdims=True))
    return dx, dg


def nt_rms_bwd(a, b, h, g, dh_in, name, extra=None, colsum=False, b_kd=False):
    n, k = a.shape
    d = h.shape[1]
    tm = _row_tile(n, 512)
    sharded = isinstance(b, ColSharded)
    tk = b.per if sharded else _pick_tile(k, 1024)
    nk = k // tk
    dims = NN if b_kd else NT
    if sharded:
        bchip = b.chip
        b_spec = pl.BlockSpec((None, d, tk), lambda i, kk: (bchip(kk), 0, 0))
        b = b.arr
    elif b_kd:
        b_spec = pl.BlockSpec((tk, d), lambda i, kk: (kk, 0))
    else:
        b_spec = pl.BlockSpec((d, tk), lambda i, kk: (0, kk))
    row = pl.BlockSpec((tm, d), lambda i, kk: (i, 0))
    vec = pl.BlockSpec((1, d), lambda i, kk: (0, 0))
    ins, in_specs = [a, b, h, g, dh_in], [pl.BlockSpec((tm, tk), lambda i, kk: (i, kk)), b_spec, row, vec, row]
    if extra is not None:
        k2 = extra[0].shape[1]
        ins += list(extra)
        in_specs += [pl.BlockSpec((tm, k2), lambda i, kk: (i, 0)),
                     pl.BlockSpec((k2, d) if b_kd else (d, k2), lambda i, kk: (0, 0))]
    n_in = len(ins)

    def body(*refs):
        a_ref, b_ref, h_ref, g_ref, dh_ref = refs[:5]
        o_ref, dg_ref = refs[n_in], refs[n_in + 1]
        acc_ref = refs[-1]
        i, kk = pl.program_id(0), pl.program_id(1)

        @pl.when(kk == 0)
        def _():
            acc_ref[...] = _mm(refs[5][...], refs[6][...], dims) if extra is not None else jnp.zeros_like(acc_ref)

        acc_ref[...] += _mm(a_ref[...], b_ref[...], dims)

        @pl.when(kk == nk - 1)
        def _():
            dx, dg = _rms_bwd_math(h_ref[...], g_ref[...], acc_ref[...])
            out = dh_ref[...] + dx
            o_ref[...] = out
            _acc(dg_ref, i, dg)
            if colsum:
                _acc(refs[n_in + 2], i, jnp.sum(out, axis=0, keepdims=True))

    n_vec = 2 if colsum else 1
    return pl.pallas_call(
        body, name=name, grid=(n // tm, nk), in_specs=in_specs, out_specs=[row] + [vec] * n_vec,
        out_shape=[S((n, d), F32)] + [S((1, d), F32)] * n_vec, scratch_shapes=[pltpu.VMEM((tm, d), F32)],
        compiler_params=pltpu.CompilerParams(dimension_semantics=("arbitrary", "arbitrary"), vmem_limit_bytes=VMEM_LIMIT),
    )(*ins)


def rms_bwd(h, g, dxn, dh_in, name, colsum=False):
    n, d = h.shape
    tile = _row_tile(n, 256)

    def body(h_ref, g_ref, dxn_ref, dh_ref, o_ref, dg_ref, *cs_ref):
        i = pl.program_id(0)
        dx, dg = _rms_bwd_math(h_ref[...], g_ref[...], dxn_ref[...].astype(F32))
        out = dh_ref[...] + dx
        o_ref[...] = out
        _acc(dg_ref, i, dg)
        if colsum:
            _acc(cs_ref[0], i, jnp.sum(out, axis=0, keepdims=True))

    outs = [S((n, d), F32), S((1, d), F32)] + ([S((1, d), F32)] if colsum else [])
    ospecs = [_rs(tile, d), _ps((1, d))] + ([_ps((1, d))] if colsum else [])
    return _call(body, name, (n // tile,), [_rs(tile, d), _ps((1, d)), _rs(tile, d), _rs(tile, d)], ospecs, outs,
                 sem=("arbitrary",))(h, g, dxn, dh_in)


def swiglu_in(h, g, w_in, name):
    n, d = h.shape
    per = w_in.per
    nj = w_in.nch // 2
    tile = _row_tile(n, 512)

    def body(h_ref, g_ref, wg_ref, wu_ref, xn_ref, u_ref, hm_ref):
        x = h_ref[...]
        r = lax.rsqrt(jnp.mean(x * x, axis=-1, keepdims=True) + EPS)
        xn = (x * r * g_ref[...]).astype(xn_ref.dtype)

        @pl.when(pl.program_id(1) == 0)
        def _():
            xn_ref[...] = xn

        a = _mm(xn, wg_ref[...])
        b = _mm(xn, wu_ref[...])
        u_ref[:, :per] = a.astype(u_ref.dtype)
        u_ref[:, per:] = b.astype(u_ref.dtype)
        hm_ref[...] = (a * _sigmoid(a) * b).astype(hm_ref.dtype)

    return pl.pallas_call(
        body, name=name, grid=(n // tile, nj),
        in_specs=[pl.BlockSpec((tile, d), lambda i, j: (i, 0)), pl.BlockSpec((1, d), lambda i, j: (0, 0)),
                  pl.BlockSpec((None, d, per), lambda i, j: (j, 0, 0)), pl.BlockSpec((None, d, per), lambda i, j: (nj + j, 0, 0))],
        out_specs=[pl.BlockSpec((tile, d), lambda i, j: (i, 0)), pl.BlockSpec((tile, 2 * per), lambda i, j: (i, j)),
                   pl.BlockSpec((tile, per), lambda i, j: (i, j))],
        out_shape=[S((n, d), MXU_DTYPE), S((n, 2 * nj * per), MXU_DTYPE), S((n, nj * per), MXU_DTYPE)],
        compiler_params=pltpu.CompilerParams(dimension_semantics=("parallel", "arbitrary"), vmem_limit_bytes=VMEM_LIMIT),
    )(h, g, w_in.arr, w_in.arr)


def swiglu_out_bwd(dh, w_out, u, after, name):
    n, d = dh.shape
    f = w_out.shape[0]
    per = u.shape[1] // 4
    nj = f // per
    tile = _row_tile(n, 512)

    def body(dh_ref, w_ref, u_ref, after_ref, du_ref):
        dm = 0.5 * _mm(dh_ref[...], w_ref[...], NT)
        a = u_ref[:, :per].astype(F32)
        b = u_ref[:, per:].astype(F32)
        s = _sigmoid(a)
        du_ref[:, :per] = (dm * b * s * (1.0 + a * (1.0 - s))).astype(du_ref.dtype)
        du_ref[:, per:] = (dm * a * s).astype(du_ref.dtype)

    return pl.pallas_call(
        body, name=name, grid=(n // tile, nj),
        in_specs=[pl.BlockSpec((tile, d), lambda i, j: (i, 0)), pl.BlockSpec((per, d), lambda i, j: (j, 0)),
                  pl.BlockSpec((tile, 2 * per), lambda i, j: (i, j)), ANY],
        out_specs=pl.BlockSpec((tile, 2 * per), lambda i, j: (i, j)),
        out_shape=S(u.shape, MXU_DTYPE),
        compiler_params=pltpu.CompilerParams(dimension_semantics=("parallel", "parallel"), vmem_limit_bytes=VMEM_LIMIT),
    )(dh, w_out, u, after)


def ple_fwd(h, gl, pp, name):
    n, d = h.shape
    tile = _row_tile(n, 512)

    def body(h_ref, gl_ref, pp_ref, o_ref):
        o_ref[...] = h_ref[...] + _sigmoid(gl_ref[...]) * pp_ref[...]

    return _call(body, name, (n // tile,), [_rs(tile, d)] * 3, _rs(tile, d), S((n, d), F32), sem=("parallel",))(h, gl, pp)


def ple_bwd(dh, gl, pp, after, name):
    n, d = dh.shape
    tile = _row_tile(n, 512)

    def body(dh_ref, gl_ref, pp_ref, after_ref, dpp_ref, dgl_ref):
        g = _sigmoid(gl_ref[...])
        dh_ = dh_ref[...]
        dpp_ref[...] = (dh_ * g).astype(dpp_ref.dtype)
        dgl_ref[...] = (dh_ * pp_ref[...] * g * (1.0 - g)).astype(dgl_ref.dtype)

    return _call(body, name, (n // tile,), [_rs(tile, d)] * 3 + [ANY], [_rs(tile, d)] * 2, [S((n, d), MXU_DTYPE)] * 2,
                 sem=("parallel",))(dh, gl, pp, after)


def loss_head(h, g, target, name):
    n, d = h.shape
    tile = _row_tile(n, 256)

    def body(h_ref, g_ref, t_ref, dh_ref, dg_ref, loss_ref):
        i = pl.program_id(0)
        x = h_ref[...]
        gg = g_ref[...]
        r = lax.rsqrt(jnp.mean(x * x, axis=-1, keepdims=True) + EPS)
        err = x * r * gg - t_ref[...]
        part = 0.5 * jnp.sum(jnp.mean(err * err, axis=-1, keepdims=True), axis=0, keepdims=True)
        dx, dg = _rms_bwd_math(x, gg, err * (1.0 / d))
        dh_ref[...] = dx
        _acc(dg_ref, i, dg)
        _acc(loss_ref, i, jnp.broadcast_to(part, (8, LANE)))

    return _call(body, name, (n // tile,), [_rs(tile, d), _ps((1, d)), _rs(tile, d)],
                 [_rs(tile, d), _ps((1, d)), _ps((8, LANE))], [S((n, d), F32), S((1, d), F32), S((8, LANE), F32)],
                 sem=("arbitrary",))(h, g, target)


def _adamw_math(w, g, m, v):
    c1 = np.float32(1.0 - ADAM_B1 ** ADAM_STEP)
    c2 = np.float32(1.0 - ADAM_B2 ** ADAM_STEP)
    mm = ADAM_B1 * m + (1.0 - ADAM_B1) * g
    vv = ADAM_B2 * v + (1.0 - ADAM_B2) * (g * g)
    return -ADAM_LR * ((mm / c1) / (jnp.sqrt(vv / c2) + ADAM_EPS) + ADAM_WD * w), mm, vv


def adamw_layer(w, pack, off, m, v, li, prev, name):
    n, r, c = w.shape

    def body(w_ref, g_ref, m_ref, v_ref, *refs):
        go_ref, d_ref, mo_ref, vo_ref = refs[-4:]
        g = g_ref[...]
        go_ref[...] = g
        d_ref[...], mo_ref[...], vo_ref[...] = _adamw_math(w_ref[...], g, m_ref[...], v_ref[...])

    if r % 8 == 0:
        tile = next(t for t in (512, 256, 128, 64, 32, 16, 8) if r % t == 0 and off % t == 0 and t * c * 4 <= 2**21)
        ob, steps = off // tile, r // tile
        blk = pl.BlockSpec((None, tile, c), lambda i: (li, i, 0))
        g_spec = pl.BlockSpec((tile, c), lambda i: (ob + i, 0))
    else:
        assert off == 0 and pack.shape[0] == r and c % (2 * LANE) == 0
        steps = c // (2 * LANE)
        blk = pl.BlockSpec((None, r, 2 * LANE), lambda i: (li, 0, i))
        g_spec = pl.BlockSpec((r, 2 * LANE), lambda i: (0, i))
    prev = list(prev) if prev is not None else []
    return pl.pallas_call(
        body, name=name, grid=(steps,),
        in_specs=[blk, g_spec, blk, blk] + [ANY] * len(prev),
        out_specs=[blk] * 4, out_shape=[S((n, r, c), F32)] * 4,
        input_output_aliases={4 + j: j for j in range(len(prev))},
        compiler_params=pltpu.CompilerParams(dimension_semantics=("parallel",), vmem_limit_bytes=VMEM_LIMIT),
    )(w, pack, m, v, *prev)


def adamw(w, g, m, v, name):
    r, c = w.shape
    tile = r
    for t in (512, 256, 128, 64, 32, 16, 8):
        if r % t == 0 and t * c * 4 <= 2**21:
            tile = t
            break

    def body(w_ref, g_ref, m_ref, v_ref, d_ref, mo_ref, vo_ref):
        d_ref[...], mo_ref[...], vo_ref[...] = _adamw_math(w_ref[...], g_ref[...], m_ref[...], v_ref[...])

    return _call(body, name, (r // tile,), [_rs(tile, c)] * 4, [_rs(tile, c)] * 3, [S((r, c), F32)] * 3,
                 sem=("parallel",))(w, g, m, v)


def _taps_fwd(sc, w_ref, width, halo, tile, acc):
    for k in range(width):
        o = halo - (width - 1) + k
        acc = acc + w_ref[k:k + 1, :] * sc[o:o + tile, :]
    return acc


def _taps_bwd_x(sc_d, w_ref, width, tile, acc):
    for k in range(width):
        o = (width - 1) - k
        acc = acc + w_ref[k:k + 1, :] * sc_d[o:o + tile, :]
    return acc


def _taps_bwd_w(dy, sc, dw_ref, width, halo, tile, i):
    @pl.when(i == 0)
    def _():
        dw_ref[...] = jnp.zeros_like(dw_ref)

    for k in range(width):
        o = halo - (width - 1) + k
        dw_ref[k:k + 1, :] += jnp.sum(dy * sc[o:o + tile, :], axis=0, keepdims=True)


def _ln_stats(x):
    mu = jnp.mean(x, axis=-1, keepdims=True)
    xc = x - mu
    r = lax.rsqrt(jnp.mean(xc * xc, axis=-1, keepdims=True) + EPS)
    return xc * r, r


def conv_group_fwd(proj, cw, cb, lg, lb, name):
    n = proj.shape[0]
    d = cw.shape[1]
    tile = _row_tile(n, 256)
    halo = 32

    def body(v_ref, g_ref, vp_ref, gp_ref, cw_ref, cb_ref, lg_ref, lb_ref, u_ref, u1_ref, sc):
        i = pl.program_id(0)
        first = (i > 0).astype(F32)
        sc[0:halo, :] = vp_ref[tile - halo:, :] * _sigmoid(gp_ref[tile - halo:, :]) * first
        sc[halo:, :] = v_ref[...] * _sigmoid(g_ref[...])
        u1 = _taps_fwd(sc, cw_ref, CONV_WIDTH, halo, tile, jnp.zeros((tile, d), F32) + cb_ref[...])
        u1_ref[...] = u1
        xh, _ = _ln_stats(u1)
        y = xh * lg_ref[...] + lb_ref[...]
        u_ref[...] = (y * _sigmoid(y)).astype(u_ref.dtype)

    return _call(body, name, (n // tile,),
                 [_rs(tile, d, 0), _rs(tile, d, 1), _rs(tile, d, 0, -1), _rs(tile, d, 1, -1),
                  _ps(cw.shape), _ps((1, d)), _ps((1, d)), _ps((1, d))],
                 [_rs(tile, d), _rs(tile, d)], [S((n, d), MXU_DTYPE), S((n, d), F32)],
                 scratch=[pltpu.VMEM((halo + tile, d), F32)], sem=("arbitrary",))(proj, proj, proj, proj, cw, cb, lg, lb)


def conv_group_bwd(du, u1, proj, cw, lg, lb, name):
    n = proj.shape[0]
    d = cw.shape[1]
    tile = _row_tile(n, 256)
    halo = 32
    nt = n // tile

    def body(du_ref, dun_ref, u1_ref, u1n_ref, v_ref, g_ref, vp_ref, gp_ref, cw_ref, lg_ref, lb_ref,
             dp_ref, dcw_ref, dcb_ref, dlg_ref, dlb_ref, sc, sc_d):
        i = pl.program_id(0)

        def ln_swish_bwd(dy_, u1_):
            xh, r = _ln_stats(u1_)
            y = xh * lg_ref[...] + lb_ref[...]
            s = _sigmoid(y)
            dyy = dy_ * s * (1.0 + y * (1.0 - s))
            dxh = dyy * lg_ref[...]
            dx = r * (dxh - jnp.mean(dxh, axis=-1, keepdims=True) - xh * jnp.mean(dxh * xh, axis=-1, keepdims=True))
            return dx, jnp.sum(dyy * xh, axis=0, keepdims=True), jnp.sum(dyy, axis=0, keepdims=True)

        du1, dlg, dlb = ln_swish_bwd(du_ref[...].astype(F32), u1_ref[...])
        du1n, _, _ = ln_swish_bwd(dun_ref[0:halo, :].astype(F32), u1n_ref[0:halo, :])
        sc_d[0:tile, :] = du1
        sc_d[tile:, :] = du1n * (i < nt - 1).astype(F32)
        sig = _sigmoid(g_ref[...])
        val = v_ref[...]
        sc[0:halo, :] = vp_ref[tile - halo:, :] * _sigmoid(gp_ref[tile - halo:, :]) * (i > 0).astype(F32)
        sc[halo:, :] = val * sig
        du0 = _taps_bwd_x(sc_d, cw_ref, CONV_WIDTH, tile, jnp.zeros((tile, d), F32))
        _taps_bwd_w(du1, sc, dcw_ref, CONV_WIDTH, halo, tile, i)
        _acc(dcb_ref, i, jnp.sum(du1, axis=0, keepdims=True))
        _acc(dlg_ref, i, dlg)
        _acc(dlb_ref, i, dlb)
        dp_ref[:, :d] = (du0 * sig).astype(dp_ref.dtype)
        dp_ref[:, d:] = (du0 * val * sig * (1.0 - sig)).astype(dp_ref.dtype)

    return _call(body, name, (nt,),
                 [_rs(tile, d), _rs(tile, d, 0, 1, nt), _rs(tile, d), _rs(tile, d, 0, 1, nt),
                  _rs(tile, d, 0), _rs(tile, d, 1), _rs(tile, d, 0, -1), _rs(tile, d, 1, -1),
                  _ps(cw.shape), _ps((1, d)), _ps((1, d))],
                 [_rs(tile, 2 * d), _ps(cw.shape), _ps((1, d)), _ps((1, d)), _ps((1, d))],
                 [S((n, proj.shape[1]), MXU_DTYPE), S(cw.shape, F32), S((1, d), F32), S((1, d), F32), S((1, d), F32)],
                 scratch=[pltpu.VMEM((halo + tile, d), F32), pltpu.VMEM((tile + halo, d), F32)],
                 sem=("arbitrary",))(du, du, u1, u1, proj, proj, proj, proj, cw, lg, lb)


def ssm_conv_fwd(proj, dtr, sw, sb, dtb, name):
    n = proj.shape[0]
    w = sw.shape[1]
    inner = SSM_HEADS * HEAD_DIM
    tile = _row_tile(n, 256)
    halo = 8

    def body(x_ref, xp_ref, dtr_ref, sw_ref, sb_ref, dtb_ref, pre_ref, xs_ref, bc_ref, dt_ref, sc):
        i = pl.program_id(0)
        sc[0:halo, :] = xp_ref[tile - halo:, :] * (i > 0).astype(F32)
        sc[halo:, :] = x_ref[...]
        pre = _taps_fwd(sc, sw_ref, SSM_CONV, halo, tile, jnp.zeros((tile, w), F32) + sb_ref[...])
        pre_ref[...] = pre
        act = pre * _sigmoid(pre)
        xs_ref[...] = act[:, :inner]
        bc_ref[...] = act[:, inner:]
        dt = _softplus(dtr_ref[...] + dtb_ref[...])
        dt_ref[...] = jnp.where(_iota(dt.shape, 1) < SSM_HEADS, dt, 0.0)

    return _call(body, name, (n // tile,),
                 [_rs(tile, w, 2), _rs(tile, w, 2, -1), _rs(tile, LANE), _ps(sw.shape), _ps((1, w)), _ps((1, LANE))],
                 [_rs(tile, w), _rs(tile, inner), _rs(tile, w - inner), _rs(tile, LANE)],
                 [S((n, w), F32), S((n, inner), F32), S((n, w - inner), F32), S((n, LANE), F32)],
                 scratch=[pltpu.VMEM((halo + tile, w), F32)], sem=("arbitrary",))(proj, proj, dtr, sw, sb, dtb)


def ssm_conv_bwd(dxs, dbc, pre, proj, sw, dproj, name):
    n = proj.shape[0]
    w = sw.shape[1]
    inner = SSM_HEADS * HEAD_DIM
    tile = _row_tile(n, 256)
    halo = 8
    nt = n // tile

    def body(dxs_ref, dxsn_ref, dbc_ref, dbcn_ref, pre_ref, pren_ref, x_ref, xp_ref, sw_ref, dp_in_ref,
             dx_ref, dsw_ref, dsb_ref, sc, sc_d):
        i = pl.program_id(0)

        def silu_bwd(d_, p_):
            s = _sigmoid(p_)
            return d_ * s * (1.0 + p_ * (1.0 - s))

        sc_d[0:tile, :inner] = silu_bwd(dxs_ref[...], pre_ref[:, :inner])
        sc_d[0:tile, inner:] = silu_bwd(dbc_ref[...], pre_ref[:, inner:])
        last = (i < nt - 1).astype(F32)
        sc_d[tile:, :inner] = silu_bwd(dxsn_ref[0:halo, :], pren_ref[0:halo, :inner]) * last
        sc_d[tile:, inner:] = silu_bwd(dbcn_ref[0:halo, :], pren_ref[0:halo, inner:]) * last
        sc[0:halo, :] = xp_ref[tile - halo:, :] * (i > 0).astype(F32)
        sc[halo:, :] = x_ref[...]
        dpre = sc_d[0:tile, :]
        dx_ref[...] = _taps_bwd_x(sc_d, sw_ref, SSM_CONV, tile, jnp.zeros((tile, w), F32)).astype(dx_ref.dtype)
        _taps_bwd_w(dpre, sc, dsw_ref, SSM_CONV, halo, tile, i)
        _acc(dsb_ref, i, jnp.sum(dpre, axis=0, keepdims=True))

    return pl.pallas_call(
        body, name=name, grid=(nt,),
        in_specs=[_rs(tile, inner), _rs(tile, inner, 0, 1, nt), _rs(tile, w - inner), _rs(tile, w - inner, 0, 1, nt),
                  _rs(tile, w), _rs(tile, w, 0, 1, nt), _rs(tile, w, 2), _rs(tile, w, 2, -1), _ps(sw.shape), ANY],
        out_specs=[_rs(tile, w, 2), _ps(sw.shape), _ps((1, w))],
        out_shape=[S(dproj.shape, dproj.dtype), S(sw.shape, F32), S((1, w), F32)],
        scratch_shapes=[pltpu.VMEM((halo + tile, w), F32), pltpu.VMEM((tile + halo, w), F32)],
        input_output_aliases={9: 0},
        compiler_params=pltpu.CompilerParams(dimension_semantics=("arbitrary",), vmem_limit_bytes=VMEM_LIMIT),
    )(dxs, dxs, dbc, dbc, pre, pre, proj, proj, sw, dproj)


def _ssd_prologue(dt_ref, dtT_ref, al_ref, alc_ref):
    row = _iota((CHUNK, CHUNK), 0)
    col = _iota((CHUNK, CHUNK), 1)
    dt = dt_ref[:, :SSM_HEADS]
    a_row = -jnp.exp(al_ref[:, :SSM_HEADS])
    a_col = -jnp.exp(alc_ref[...])
    cs = _01mm((row >= col).astype(F32), dt * a_row)
    csT = _mm01(dtT_ref[...] * a_col, (row <= col).astype(F32))
    return dt, a_row, cs, csT, row, col


def _decay(cs, csT, h, row, col):
    lm = jnp.exp(jnp.where(row >= col, cs[:, h:h + 1] - csT[h:h + 1, :], -1e30))
    lmT = jnp.exp(jnp.where(col >= row, csT[h:h + 1, :] - cs[:, h:h + 1], -1e30))
    return lm, lmT


def ssd_fwd(xs, bc, dt, dtT, alog_row, alog_col, name):
    n, width = xs.shape
    nc = n // CHUNK
    gw = width // SSM_GROUPS
    hpg = SSM_HEADS // SSM_GROUPS
    ns = SSM_STATE

    def body(xs_ref, bc_ref, dt_ref, dtT_ref, al_ref, alc_ref, y_ref, hs_ref, h_sc):
        i = pl.program_id(0)

        @pl.when(i == 0)
        def _():
            h_sc[...] = jnp.zeros_like(h_sc)

        dt, a_row, cs, csT, row, col = _ssd_prologue(dt_ref, dtT_ref, al_ref, alc_ref)
        indT = _head_indicator(width, SSM_HEADS, transposed=True)
        dt_full = _mm01(dt, indT)
        e_full = jnp.exp(_mm01(cs, indT))
        dte_full = jnp.exp(_mm01(cs[CHUNK - 1:CHUNK, :] - cs, indT))
        xt = xs_ref[...] * dt_full
        hs_ref[0] = h_sc[...]
        lo = _iota((CHUNK, 2 * HEAD_DIM), 1) < HEAD_DIM
        for g in range(SSM_GROUPS):
            bg = bc_ref[:, g * ns:(g + 1) * ns]
            cg = bc_ref[:, (SSM_GROUPS + g) * ns:(SSM_GROUPS + g + 1) * ns]
            gm = _mm(cg, bg, NT)
            hg = h_sc[g * gw:(g + 1) * gw, :]
            yoff = e_full[:, g * gw:(g + 1) * gw] * _mm(cg, hg, NT)
            for pr in range(hpg // 2):
                h0 = g * hpg + 2 * pr
                c0 = h0 * HEAD_DIM
                xp = xt[:, c0:c0 + 2 * HEAD_DIM]
                m0 = gm * _decay(cs, csT, h0, row, col)[0]
                m1 = gm * _decay(cs, csT, h0 + 1, row, col)[0]
                yd = jnp.where(lo, _mm(m0, xp), _mm(m1, xp))
                y_ref[:, c0:c0 + 2 * HEAD_DIM] = yd + yoff[:, 2 * pr * HEAD_DIM:(2 * pr + 2) * HEAD_DIM]
            sg = _mm(xt[:, g * gw:(g + 1) * gw] * dte_full[:, g * gw:(g + 1) * gw], bg, TN)
            for hh in range(hpg):
                h = g * hpg + hh
                r0 = h * HEAD_DIM
                h_sc[r0:r0 + HEAD_DIM, :] = (h_sc[r0:r0 + HEAD_DIM, :] * jnp.exp(csT[h:h + 1, CHUNK - 1:CHUNK])
                                             + sg[hh * HEAD_DIM:(hh + 1) * HEAD_DIM, :])

    bcw = bc.shape[1]
    return _call(body, name, (nc,),
                 [_rs(CHUNK, width), _rs(CHUNK, bcw), _rs(CHUNK, LANE), pl.BlockSpec((SSM_HEADS, CHUNK), lambda i: (0, i)),
                  _ps((1, LANE)), _ps((SSM_HEADS, 1))],
                 [_rs(CHUNK, width), pl.BlockSpec((1, width, ns), lambda i: (i, 0, 0))],
                 [S((n, width), F32), S((nc, width, ns), F32)],
                 scratch=[pltpu.VMEM((width, ns), F32)], sem=("arbitrary",))(xs, bc, dt, dtT, alog_row, alog_col)


def ssd_bwd(xs, bc, dt, dtT, alog_row, alog_col, hs, dy, dxs_skip, name):
    n, width = xs.shape
    nc = n // CHUNK
    gw = width // SSM_GROUPS
    hpg = SSM_HEADS // SSM_GROUPS
    ns = SSM_STATE
    bcw = bc.shape[1]

    def body(xs_ref, bc_ref, dt_ref, dtT_ref, al_ref, alc_ref, hs_ref, dy_ref, skip_ref,
             dxs_ref, dbc_ref, ddtr_ref, dal_ref, ddtb_ref, dh_sc, dxt_sc):
        i = pl.program_id(0)

        @pl.when(i == 0)
        def _():
            dh_sc[...] = jnp.zeros_like(dh_sc)

        dt, a_row, cs, csT, row, col = _ssd_prologue(dt_ref, dtT_ref, al_ref, alc_ref)
        indT = _head_indicator(width, SSM_HEADS, transposed=True)
        ind = _head_indicator(width, SSM_HEADS)
        dt_full = _mm01(dt, indT)
        e_full = jnp.exp(_mm01(cs, indT))
        cs_last = cs[CHUNK - 1:CHUNK, :]
        dte = jnp.exp(cs_last - cs)
        dte_full = _mm01(dte, indT)
        xs_ = xs_ref[...]
        xt = xs_ * dt_full
        dy_ = dy_ref[...]
        hmat = hs_ref[0]
        ds = dh_sc[...]
        lo = _iota((CHUNK, 2 * HEAD_DIM), 1) < HEAD_DIM
        head_lane = _iota((1, SSM_HEADS), 1)
        dcs = jnp.zeros((CHUNK, SSM_HEADS), F32)
        ddte = jnp.zeros((CHUNK, SSM_HEADS), F32)
        for g in range(SSM_GROUPS):
            sl = slice(g * gw, (g + 1) * gw)
            bg = bc_ref[:, g * ns:(g + 1) * ns]
            cg = bc_ref[:, (SSM_GROUPS + g) * ns:(SSM_GROUPS + g + 1) * ns]
            gm = _mm(cg, bg, NT)
            gmT = _mm(bg, cg, NT)
            hg = hmat[sl, :]
            dsg = ds[sl, :]
            dyg = dy_[:, sl]
            xtg = xt[:, sl]
            yoff = e_full[:, sl] * _mm(cg, hg, NT)
            edy = e_full[:, sl] * dyg
            bds = _mm(bg, dsg, NT)
            dxt_g = dte_full[:, sl] * bds
            ddte = ddte + _mm01(xtg * bds, ind[sl, :])
            dcs = dcs + _mm01(dyg * yoff, ind[sl, :])
            db = _mm(xtg * dte_full[:, sl], dsg)
            dc = _mm(edy, hg)
            dhc = _mm(edy, cg, TN)
            dgs = jnp.zeros((CHUNK, CHUNK), F32)
            dgTs = jnp.zeros((CHUNK, CHUNK), F32)
            for pr in range(hpg // 2):
                h0 = g * hpg + 2 * pr
                c0 = 2 * pr * HEAD_DIM
                xp = xtg[:, c0:c0 + 2 * HEAD_DIM]
                dyp = dyg[:, c0:c0 + 2 * HEAD_DIM]
                rr = []
                for h, half in ((h0, lo), (h0 + 1, jnp.logical_not(lo))):
                    lm, lmT = _decay(cs, csT, h, row, col)
                    xm = jnp.where(half, xp, 0.0)
                    dm = _mm(dyp, xm, NT)
                    dmT = _mm(xm, dyp, NT)
                    mT = gmT * lmT
                    z = jnp.sum(dm * (gm * lm), axis=1, keepdims=True) - jnp.sum(dmT * mT, axis=1, keepdims=True)
                    dcs = dcs + z * (head_lane == h).astype(F32)
                    dgs = dgs + dm * lm
                    dgTs = dgTs + dmT * lmT
                    rr.append(_mm(mT, dyp))
                dxt_sc[:, g * gw + c0:g * gw + c0 + 2 * HEAD_DIM] = jnp.where(lo, rr[0], rr[1]) + dxt_g[:, c0:c0 + 2 * HEAD_DIM]
            dbc_ref[:, g * ns:(g + 1) * ns] = db + _mm(dgTs, cg)
            dbc_ref[:, (SSM_GROUPS + g) * ns:(SSM_GROUPS + g + 1) * ns] = dc + _mm(dgs, bg)
            for hh in range(hpg):
                h = g * hpg + hh
                r0 = h * HEAD_DIM
                dh_sc[r0:r0 + HEAD_DIM, :] = (dhc[hh * HEAD_DIM:(hh + 1) * HEAD_DIM, :]
                                              + jnp.exp(csT[h:h + 1, CHUNK - 1:CHUNK]) * ds[r0:r0 + HEAD_DIM, :])
        t = ddte * dte
        per_head = jnp.sum(jnp.sum(ds * hmat, axis=1, keepdims=True) * ind, axis=0, keepdims=True)
        last_add = jnp.sum(t, axis=0, keepdims=True) + jnp.exp(cs_last) * per_head
        dcs = dcs - t + jnp.where(_iota((CHUNK, SSM_HEADS), 0) == CHUNK - 1, last_add, 0.0)
        dadt = _01mm((row <= col).astype(F32), dcs)
        dxt = dxt_sc[...]
        ddt = dadt * a_row + _mm01(dxt * xs_, ind)
        dxs_ref[...] = dxt * dt_full + skip_ref[...]
        ddtr = ddt * (1.0 - jnp.exp(-dt))
        ddtr_ref[...] = jnp.zeros_like(ddtr_ref)
        ddtr_ref[:, :SSM_HEADS] = ddtr.astype(ddtr_ref.dtype)
        _acc(dal_ref, i, jnp.sum(dadt * dt, axis=0, keepdims=True) * a_row)
        _acc(ddtb_ref, i, jnp.sum(ddtr, axis=0, keepdims=True))

    rev = lambda i: (nc - 1 - i, 0)
    return _call(body, name, (nc,),
                 [pl.BlockSpec((CHUNK, width), rev), pl.BlockSpec((CHUNK, bcw), rev), pl.BlockSpec((CHUNK, LANE), rev),
                  pl.BlockSpec((SSM_HEADS, CHUNK), lambda i: (0, nc - 1 - i)), _ps((1, LANE)), _ps((SSM_HEADS, 1)),
                  pl.BlockSpec((1, width, ns), lambda i: (nc - 1 - i, 0, 0)), pl.BlockSpec((CHUNK, width), rev),
                  pl.BlockSpec((CHUNK, width), rev)],
                 [pl.BlockSpec((CHUNK, width), rev), pl.BlockSpec((CHUNK, bcw), rev), pl.BlockSpec((CHUNK, LANE), rev),
                  _ps((1, SSM_HEADS)), _ps((1, SSM_HEADS))],
                 [S((n, width), F32), S((n, bcw), F32), S((n, LANE), MXU_DTYPE), S((1, SSM_HEADS), F32), S((1, SSM_HEADS), F32)],
                 scratch=[pltpu.VMEM((width, ns), F32), pltpu.VMEM((CHUNK, width), F32)],
                 sem=("arbitrary",))(xs, bc, dt, dtT, alog_row, alog_col, hs, dy, dxs_skip)


def ssm_gate_fwd(yssd, xs, proj, dfull, gamma, name):
    n, d = yssd.shape
    tile = _row_tile(n, 256)
    gw = d // SSM_GROUPS

    def body(y_ref, xs_ref, z_ref, df_ref, gm_ref, o_ref):
        z = z_ref[...]
        y2 = (y_ref[...] + df_ref[...] * xs_ref[...]) * (z * _sigmoid(z))
        for g in range(SSM_GROUPS):
            yg = y2[:, g * gw:(g + 1) * gw]
            r = lax.rsqrt(jnp.mean(yg * yg, axis=-1, keepdims=True) + EPS)
            o_ref[:, g * gw:(g + 1) * gw] = (yg * r * gm_ref[:, g * gw:(g + 1) * gw]).astype(o_ref.dtype)

    return _call(body, name, (n // tile,), [_rs(tile, d), _rs(tile, d), _rs(tile, d, 2), _ps((1, d)), _ps((1, d))],
                 _rs(tile, d), S((n, d), MXU_DTYPE), sem=("parallel",))(yssd, xs, proj, dfull, gamma)


def ssm_gate_bwd(dy3, yssd, xs, proj, dfull, gamma, dproj, name):
    n, d = yssd.shape
    tile = _row_tile(n, 256)
    gw = d // SSM_GROUPS

    def body(dy_ref, y_ref, xs_ref, z_ref, df_ref, gm_ref, dp_in_ref, dys_ref, dxs_ref, dz_ref, dgm_ref, dd_ref):
        i = pl.program_id(0)
        z = z_ref[...]
        s = _sigmoid(z)
        xs_ = xs_ref[...]
        y1 = y_ref[...] + df_ref[...] * xs_
        y2 = y1 * (z * s)
        dy_ = dy_ref[...].astype(F32)
        dgm = []
        dy2 = []
        for g in range(SSM_GROUPS):
            sl = slice(g * gw, (g + 1) * gw)
            dxg, dgg = _rms_bwd_math(y2[:, sl], gm_ref[:, sl], dy_[:, sl])
            dy2.append(dxg)
            dgm.append(dgg)
        dy2 = jnp.concatenate(dy2, axis=1)
        dy1 = dy2 * (z * s)
        dys_ref[...] = dy1
        dxs_ref[...] = dy1 * df_ref[...]
        dz_ref[...] = (dy2 * y1 * s * (1.0 + z * (1.0 - s))).astype(dz_ref.dtype)
        _acc(dgm_ref, i, jnp.concatenate(dgm, axis=1))
        colsum = jnp.broadcast_to(jnp.sum(dy1 * xs_, axis=0, keepdims=True), (8, d))
        _acc(dd_ref, i, _mm01(colsum, _head_indicator(d, SSM_HEADS))[0:1, :])

    return pl.pallas_call(
        body, name=name, grid=(n // tile,),
        in_specs=[_rs(tile, d), _rs(tile, d), _rs(tile, d), _rs(tile, d, 2), _ps((1, d)), _ps((1, d)), ANY],
        out_specs=[_rs(tile, d), _rs(tile, d), _rs(tile, d, 2), _ps((1, d)), _ps((1, SSM_HEADS))],
        out_shape=[S((n, d), F32), S((n, d), F32), S(dproj.shape, dproj.dtype), S((1, d), F32), S((1, SSM_HEADS), F32)],
        input_output_aliases={6: 2},
        compiler_params=pltpu.CompilerParams(dimension_semantics=("arbitrary",), vmem_limit_bytes=VMEM_LIMIT),
    )(dy3, yssd, xs, proj, dfull, gamma, dproj)


def _rope128(x, cos, sin_signed):
    half = HEAD_DIM // 2
    lane = _iota(x.shape, 1)
    partner = jnp.where((lane % HEAD_DIM) < half, pltpu.roll(x, LANE - half, 1), pltpu.roll(x, half, 1))
    return x * cos + partner * sin_signed


def rope_fwd(qkv, cos, sin, name):
    n, w = qkv.shape
    qw = ATT_HEADS * HEAD_DIM
    kw = ATT_KV_HEADS * HEAD_DIM
    tile = _row_tile(n, 256)

    def body(x_ref, c_ref, s_ref, q_ref, k_ref, v_ref):
        c, s = c_ref[...], s_ref[...]
        for j in range(qw // LANE):
            q_ref[:, j * LANE:(j + 1) * LANE] = _rope128(x_ref[:, j * LANE:(j + 1) * LANE], c, s).astype(q_ref.dtype)
        for j in range(kw // LANE):
            k_ref[:, j * LANE:(j + 1) * LANE] = _rope128(x_ref[:, qw + j * LANE:qw + (j + 1) * LANE], c, s).astype(k_ref.dtype)
        v_ref[...] = x_ref[:, qw + kw:].astype(v_ref.dtype)

    return _call(body, name, (n // tile,), [_rs(tile, w), _rs(tile, LANE), _rs(tile, LANE)],
                 [_rs(tile, qw), _rs(tile, kw), _rs(tile, kw)],
                 [S((n, qw), MXU_DTYPE), S((n, kw), MXU_DTYPE), S((n, kw), MXU_DTYPE)], sem=("parallel",))(qkv, cos, sin)


ATT_GROUP = ATT_HEADS // ATT_KV_HEADS


def _attn_mask(i):
    row = _iota((ATT_GROUP * WINDOW, 2 * WINDOW), 0) % WINDOW
    s = _iota((ATT_GROUP * WINDOW, 2 * WINDOW), 1)
    return (s > row) & (s <= row + WINDOW) & ((s >= WINDOW) | (i > 0))


def _stack_heads(ref, j, kh, lo):
    parts = []
    for t in range(ATT_GROUP):
        h = ATT_GROUP * j + t
        blk = ref[:, (h // 2) * LANE:(h // 2 + 1) * LANE]
        blk = jnp.where(lo if h % 2 == 0 else jnp.logical_not(lo), blk, jnp.zeros_like(blk))
        parts.append(blk if h % 2 == kh else pltpu.roll(blk, HEAD_DIM, 1))
    return jnp.concatenate(parts, axis=0)


def _unstack_heads(stacked, j, kh, lo, put):
    for t in range(0, ATT_GROUP, 2):
        h = ATT_GROUP * j + t
        even = stacked[t * WINDOW:(t + 1) * WINDOW, :]
        odd = stacked[(t + 1) * WINDOW:(t + 2) * WINDOW, :]
        even = even if kh == 0 else pltpu.roll(even, HEAD_DIM, 1)
        odd = odd if kh == 1 else pltpu.roll(odd, HEAD_DIM, 1)
        put(h // 2, jnp.where(lo, even, odd))


def _per_head_rows(ref, j):
    return jnp.concatenate([ref[:, ATT_GROUP * j + t:ATT_GROUP * j + t + 1] for t in range(ATT_GROUP)], axis=0)


def _per_head_scalar(ref, j):
    rows = _iota((ATT_GROUP * WINDOW, 1), 0) // WINDOW
    out = jnp.zeros((ATT_GROUP * WINDOW, 1), F32)
    for t in range(ATT_GROUP):
        out = out + jnp.where(rows == t, ref[:, ATT_GROUP * j + t:ATT_GROUP * j + t + 1], 0.0)
    return out


def attn_fwd(q, k, v, sinks, name):
    n, qw = q.shape
    kw = k.shape[1]
    nb = n // WINDOW
    scale = HEAD_DIM ** -0.5

    def body(q_ref, kc_ref, kp_ref, vc_ref, vp_ref, sk_ref, o_ref, lse_ref):
        i = pl.program_id(0)
        valid = _attn_mask(i)
        lo = _iota((WINDOW, LANE), 1) < HEAD_DIM
        k2 = jnp.concatenate([kp_ref[...], kc_ref[...]], axis=0)
        v2 = jnp.concatenate([vp_ref[...], vc_ref[...]], axis=0)
        lane1 = _iota((1, LANE), 1)
        lse = jnp.zeros((WINDOW, LANE), F32)

        def put_o(qb, val):
            o_ref[:, qb * LANE:(qb + 1) * LANE] = val.astype(o_ref.dtype)

        for j in range(ATT_KV_HEADS):
            kb, kh = j // 2, j % 2
            q4 = _stack_heads(q_ref, j, kh, lo)
            logits = jnp.where(valid, _mm(q4, k2[:, kb * LANE:(kb + 1) * LANE], NT) * scale, -1e30)
            sk = _per_head_scalar(sk_ref, j)
            m = jnp.maximum(jnp.max(logits, axis=-1, keepdims=True), sk)
            e = jnp.exp(logits - m)
            den = jnp.sum(e, axis=-1, keepdims=True) + jnp.exp(sk - m)
            lse4 = m + jnp.log(den)
            for t in range(ATT_GROUP):
                lse = lse + lse4[t * WINDOW:(t + 1) * WINDOW, :] * (lane1 == ATT_GROUP * j + t).astype(F32)
            _unstack_heads(_mm(e * (1.0 / den), v2[:, kb * LANE:(kb + 1) * LANE]), j, kh, lo, put_o)
        lse_ref[...] = lse

    return _call(body, name, (nb,),
                 [_rs(WINDOW, qw), _rs(WINDOW, kw), _rs(WINDOW, kw, 0, -1), _rs(WINDOW, kw), _rs(WINDOW, kw, 0, -1), _ps((1, LANE))],
                 [_rs(WINDOW, qw), _rs(WINDOW, LANE)], [S((n, qw), MXU_DTYPE), S((n, LANE), F32)],
                 sem=("parallel",))(q, k, k, v, v, sinks)


def attn_bwd(q, k, v, o, do, lse, sinks, name):
    n, qw = q.shape
    kw = k.shape[1]
    nb = n // WINDOW
    scale = HEAD_DIM ** -0.5

    def body(q_ref, kc_ref, kp_ref, vc_ref, vp_ref, o_ref, do_ref, lse_ref, sk_ref,
             dq_ref, dka_ref, dkb_ref, dva_ref, dvb_ref, dsk_ref):
        i = pl.program_id(0)
        valid = _attn_mask(i)
        lo = _iota((WINDOW, LANE), 1) < HEAD_DIM
        k2 = jnp.concatenate([kp_ref[...], kc_ref[...]], axis=0)
        v2 = jnp.concatenate([vp_ref[...], vc_ref[...]], axis=0)
        lane1 = _iota((1, LANE), 1)
        do_ = do_ref[...].astype(F32)
        delta = _mm01(do_ * o_ref[...].astype(F32), _head_indicator(qw, ATT_HEADS))
        dk2 = [jnp.zeros((2 * WINDOW, LANE), F32) for _ in range(kw // LANE)]
        dv2 = [jnp.zeros((2 * WINDOW, LANE), F32) for _ in range(kw // LANE)]
        dsk = jnp.zeros((1, LANE), F32)

        def put_dq(qb, val):
            dq_ref[:, qb * LANE:(qb + 1) * LANE] = val

        for j in range(ATT_KV_HEADS):
            kb, kh = j // 2, j % 2
            q4 = _stack_heads(q_ref, j, kh, lo)
            do4 = _stack_heads(do_ref, j, kh, lo)
            kk = k2[:, kb * LANE:(kb + 1) * LANE]
            vv = v2[:, kb * LANE:(kb + 1) * LANE]
            logits = jnp.where(valid, _mm(q4, kk, NT) * scale, -1e30)
            lse4 = _per_head_rows(lse_ref, j)
            p = jnp.exp(logits - lse4)
            dl = jnp.concatenate([delta[:, ATT_GROUP * j + t:ATT_GROUP * j + t + 1] for t in range(ATT_GROUP)], axis=0)
            ds = p * (_mm(do4, vv, NT) - dl) * scale
            sd = jnp.exp(_per_head_scalar(sk_ref, j) - lse4) * dl
            for t in range(ATT_GROUP):
                dsk = dsk - (jnp.sum(sd[t * WINDOW:(t + 1) * WINDOW, :], axis=0, keepdims=True)
                             * (lane1 == ATT_GROUP * j + t).astype(F32))
            _unstack_heads(_mm(ds, kk), j, kh, lo, put_dq)
            dk2[kb] = dk2[kb] + _mm(ds, q4, TN)
            dv2[kb] = dv2[kb] + _mm(p, do4, TN)
        for kb in range(kw // LANE):
            dkb_ref[:, kb * LANE:(kb + 1) * LANE] = dk2[kb][0:WINDOW, :]
            dka_ref[:, kb * LANE:(kb + 1) * LANE] = dk2[kb][WINDOW:, :]
            dvb_ref[:, kb * LANE:(kb + 1) * LANE] = dv2[kb][0:WINDOW, :]
            dva_ref[:, kb * LANE:(kb + 1) * LANE] = dv2[kb][WINDOW:, :]
        _acc(dsk_ref, i, dsk)

    return _call(body, name, (nb,),
                 [_rs(WINDOW, qw), _rs(WINDOW, kw), _rs(WINDOW, kw, 0, -1), _rs(WINDOW, kw), _rs(WINDOW, kw, 0, -1),
                  _rs(WINDOW, qw), _rs(WINDOW, qw), _rs(WINDOW, LANE), _ps((1, LANE))],
                 [_rs(WINDOW, qw)] + [_rs(WINDOW, kw)] * 4 + [_ps((1, LANE))],
                 [S((n, qw), F32)] + [S((n, kw), F32)] * 4 + [S((1, LANE), F32)],
                 sem=("arbitrary",))(q, k, k, v, v, o, do, lse, sinks)


def attn_grad_merge(dq, dka, dkb, dva, dvb, cos, sin, name):
    n, qw = dq.shape
    kw = dka.shape[1]
    nb = n // WINDOW
    w = qw + 2 * kw

    def body(dq_ref, dka_ref, dkb_ref, dva_ref, dvb_ref, c_ref, s_ref, o_ref, db_ref):
        i = pl.program_id(0)
        c, s = c_ref[...], -s_ref[...]
        nxt = (i < nb - 1).astype(F32)

        @pl.when(i == 0)
        def _():
            db_ref[...] = jnp.zeros_like(db_ref)

        def put(c0, val):
            o_ref[:, c0:c0 + val.shape[1]] = val.astype(o_ref.dtype)
            db_ref[:, c0:c0 + val.shape[1]] += jnp.sum(val, axis=0, keepdims=True)

        for j in range(qw // LANE):
            put(j * LANE, _rope128(dq_ref[:, j * LANE:(j + 1) * LANE], c, s))
        for j in range(kw // LANE):
            sl = slice(j * LANE, (j + 1) * LANE)
            put(qw + j * LANE, _rope128(dka_ref[:, sl] + dkb_ref[:, sl] * nxt, c, s))
        put(qw + kw, dva_ref[...] + dvb_ref[...] * nxt)

    return _call(body, name, (nb,),
                 [_rs(WINDOW, qw), _rs(WINDOW, kw), _rs(WINDOW, kw, 0, 1, nb), _rs(WINDOW, kw), _rs(WINDOW, kw, 0, 1, nb),
                  _rs(WINDOW, LANE), _rs(WINDOW, LANE)],
                 [_rs(WINDOW, w), _ps((1, w))], [S((n, w), MXU_DTYPE), S((1, w), F32)],
                 sem=("arbitrary",))(dq, dka, dkb, dva, dvb, cos, sin)


def _row(v):
    return v.reshape(1, -1)


def _pad_lanes(v, width=LANE):
    return jnp.pad(v.reshape(1, -1), ((0, 0), (0, width - v.size)))


class LayerWeights(dict):
    def __init__(self, small, fetch):
        super().__init__(small)
        self.fetch = fetch

    def need(self, k, after):
        if k not in self:
            self[k] = self.fetch(k, after)
        return self[k]


def ffn_fwd(h, g, w, keys, tag):
    xn, u, hm = swiglu_in(h, _row(g), w.need(keys[0], h), f"{tag}_in")
    return matmul(hm, w.need(keys[1], hm), "nn", f"{tag}_out", scale=0.5, res=h), (h, xn, u, hm)


class GradSink:
    ORDER = ("ffn1_w_out", "ffn2_w_out", "ple_gate_w", "att_w_o", "hyb_w_out", "ffn1_w_in", "ffn2_w_in", "att_w_qkv",
             "ple_proj_w", "hyb_w_in")

    def __init__(self, shard_shapes, bucket_of):
        self.where, rows = {}, {}
        for k in self.ORDER:
            n, r, c = shard_shapes[k]
            for li in range(n):
                layer = li if k in PER_LAYER else 2 * li + (0 if k in EVEN_ONLY else 1)
                rows_b = rows.setdefault(bucket_of(layer, _stage(k)), {})
                key = c if r % 32 == 0 else k
                off = -(-rows_b.get(key, (0, c))[0] // r) * r
                rows_b[key] = (-(-(off + r) // 32) * 32, c)
                self.where[k, li] = (bucket_of(layer, _stage(k)), key, "r" if _shard_axis(k) == 1 else "c", off, r)
        self.bufs = {b: {key: lax.empty((N_CHIPS, r, c), MXU_DTYPE) for key, (r, c) in rows_b.items()}
                     for b, rows_b in rows.items()}

    def mm(self, k, li, a, b, name, scale=None, c0=0, paired=False):
        bucket, key, kind, off, r = self.where[k, li]
        buf = self.bufs[bucket][key]
        slot = Slot(buf, kind, r if kind == "r" else buf.shape[2], off, c0, paired)
        self.bufs[bucket][key] = matmul(a, b, "tn", name, scale=scale, into=slot)

    def put(self, k, li, chip_major):
        b, key = self.where[k, li][:2]
        pad = self.bufs[b][key].shape[1] - chip_major.shape[1]
        self.bufs[b][key] = jnp.pad(chip_major.astype(self.bufs[b][key].dtype), ((0, 0), (0, pad), (0, 0)))


def ffn_bwd(dh, g, w_in, w_out, saved, tag, sink, keys, layer, after, colsum=False):
    h, xn, u, hm = saved
    sink.mm(keys[1], layer, hm, dh, f"{tag}_dwout", scale=0.5)
    du = swiglu_out_bwd(dh, w_out, u, after, f"{tag}_dhm")
    sink.mm(keys[0], layer, xn, du, f"{tag}_dwin", paired=True)
    outs = nt_rms_bwd(du, ColSharded(w_in.arr, paired=True), h, _row(g), dh, f"{tag}_dxn", colsum=colsum)
    return (outs[0], outs[1].reshape(-1)) + ((outs[2],) if colsum else ())


def _hyb_params(w):
    d = w["conv_dw_b"].size
    inner = SSM_HEADS * HEAD_DIM
    main = 3 * d + w["ssm_conv_b"].size
    return dict(
        w_main=w["hyb_w_in"][:main], w_dt=jnp.pad(w["hyb_w_in"][main:], ((0, LANE - SSM_HEADS), (0, 0))),
        cw=jnp.pad(w["conv_dw_w"], ((0, 32 - CONV_WIDTH), (0, 0))), cb=_row(w["conv_dw_b"]),
        lg=_row(w["conv_ln_g"]), lb=_row(w["conv_ln_b"]),
        sw=jnp.pad(w["ssm_conv_w"], ((0, 8 - SSM_CONV), (0, 0))), sb=_row(w["ssm_conv_b"]),
        dtb=_pad_lanes(w["ssm_dt_bias"]), al_row=_pad_lanes(w["ssm_a_log"]), al_col=w["ssm_a_log"].reshape(-1, 1),
        dfull=_row(jnp.repeat(w["ssm_d"], HEAD_DIM)), gamma=_row(w["ssm_norm"]), d=d, inner=inner, main=main)


def hyb_fwd(h, w, tag):
    w.need("hyb_w_in", h)
    q = _hyb_params(w)
    xn = rms_fwd(h, _row(w["norm_mix"]), f"{tag}_rms")
    proj = matmul(xn, q["w_main"], "nt", f"{tag}_in")
    dtr = matmul(xn, q["w_dt"], "nt", f"{tag}_in_dt")
    u, u1 = conv_group_fwd(proj, q["cw"], q["cb"], q["lg"], q["lb"], f"{tag}_conv")
    pre, xs, bc, dt = ssm_conv_fwd(proj, dtr, q["sw"], q["sb"], q["dtb"], f"{tag}_sconv")
    dtT = dt[:, :SSM_HEADS].T
    yssd, hs = ssd_fwd(xs, bc, dt, dtT, q["al_row"], q["al_col"], f"{tag}_ssd")
    y = ssm_gate_fwd(yssd, xs, proj, q["dfull"], q["gamma"], f"{tag}_gate")
    wo = w.need("hyb_w_out", u)
    h2 = matmul(u, wo[:q["d"]], "nn", f"{tag}_out_a", res=h)
    h2 = matmul(y, wo[q["d"]:], "nn", f"{tag}_out_b", res=h2)
    return h2, (h, xn, proj, u, u1, pre, xs, bc, dt, dtT, yssd, hs, y)


def hyb_bwd(dh, w, saved, tag, sink, layer):
    q = _hyb_params(w)
    h, xn, proj, u, u1, pre, xs, bc, dt, dtT, yssd, hs, y = saved
    du = matmul(dh, w["hyb_w_out"][:q["d"]], "nt", f"{tag}_du")
    dy3 = matmul(dh, w["hyb_w_out"][q["d"]:], "nt", f"{tag}_dy")
    sink.mm("hyb_w_out", layer, u, dh, f"{tag}_dwo_a", c0=0)
    sink.mm("hyb_w_out", layer, y, dh, f"{tag}_dwo_b", c0=N_CHIPS // 2)
    dproj, dcw, dcb, dlg, dlb = conv_group_bwd(du, u1, proj, q["cw"], q["lg"], q["lb"], f"{tag}_dconv")
    dyssd, dxs_skip, dproj, dgamma, dd = ssm_gate_bwd(dy3, yssd, xs, proj, q["dfull"], q["gamma"], dproj, f"{tag}_dgate")
    dxs, dbc, ddtr, dalog, ddtb = ssd_bwd(xs, bc, dt, dtT, q["al_row"], q["al_col"], hs, dyssd, dxs_skip, f"{tag}_dssd")
    dproj, dsw, dsb = ssm_conv_bwd(dxs, dbc, pre, proj, q["sw"], dproj, f"{tag}_dsconv")
    dw_in = jnp.concatenate([matmul(dproj, xn, "tn", f"{tag}_dwin"),
                             matmul(ddtr, xn, "tn", f"{tag}_dwin_dt")[:SSM_HEADS]], axis=0)
    sink.put("hyb_w_in", layer, dw_in.reshape((N_CHIPS, -1) + dw_in.shape[1:]))
    dh2, dg = nt_rms_bwd(dproj, q["w_main"], h, _row(w["norm_mix"]), dh, f"{tag}_dxn", extra=(ddtr, q["w_dt"]), b_kd=True)
    grads = dict(norm_mix=dg.reshape(-1), conv_dw_w=dcw[:CONV_WIDTH], conv_dw_b=dcb.reshape(-1),
                 conv_ln_g=dlg.reshape(-1), conv_ln_b=dlb.reshape(-1), ssm_conv_w=dsw[:SSM_CONV], ssm_conv_b=dsb.reshape(-1),
                 ssm_dt_bias=ddtb.reshape(-1), ssm_a_log=dalog.reshape(-1), ssm_d=dd.reshape(-1), ssm_norm=dgamma.reshape(-1))
    return dh2, grads


def rope_tables(n):
    half = HEAD_DIM // 2
    inv = ROPE_THETA ** (-jnp.arange(0, HEAD_DIM, 2, dtype=F32) / HEAD_DIM)
    ang = jnp.arange(n, dtype=F32)[:, None] * inv[None, :]
    cos, sin = jnp.cos(ang), jnp.sin(ang)
    reps = LANE // HEAD_DIM
    return jnp.tile(jnp.concatenate([cos, cos], axis=1), (1, reps)), jnp.tile(jnp.concatenate([-sin, sin], axis=1), (1, reps))


def att_fwd(h, w, tables, tag):
    cos, sin = tables
    xn = rms_fwd(h, _row(w["norm_mix"]), f"{tag}_rms")
    qkv = matmul(xn, w.need("att_w_qkv", h), "nn", f"{tag}_qkv", bias=_row(w["att_b_qkv"]))
    q, k, v = rope_fwd(qkv, cos, sin, f"{tag}_rope")
    sinks = _pad_lanes(w["att_sinks"])
    o, lse = attn_fwd(q, k, v, sinks, f"{tag}_attn")
    h2 = matmul(o, w.need("att_w_o", o), "nn", f"{tag}_o", bias=_row(w["att_b_o"]), res=h)
    return h2, (h, xn, q, k, v, o, lse, sinks)


def att_bwd(dh, dh_colsum, w, saved, tables, tag, sink, layer):
    cos, sin = tables
    h, xn, q, k, v, o, lse, sinks = saved
    do = matmul(dh, w["att_w_o"], "nt", f"{tag}_do")
    sink.mm("att_w_o", layer, o, dh, f"{tag}_dwo")
    dq, dka, dkb, dva, dvb, dsk = attn_bwd(q, k, v, o, do, lse, sinks, f"{tag}_dattn")
    dqkv, dbqkv = attn_grad_merge(dq, dka, dkb, dva, dvb, cos, sin, f"{tag}_drope")
    sink.mm("att_w_qkv", layer, xn, dqkv, f"{tag}_dwqkv")
    dh2, dg = nt_rms_bwd(dqkv, w["att_w_qkv"], h, _row(w["norm_mix"]), dh, f"{tag}_dxn")
    grads = dict(norm_mix=dg.reshape(-1), att_b_qkv=dbqkv.reshape(-1), att_sinks=dsk[0, :ATT_HEADS],
                 att_b_o=dh_colsum.reshape(-1))
    return dh2, grads


def ple_block_fwd(h, pe, w, tag):
    xn = rms_fwd(h, _row(w["ple_norm"]), f"{tag}_rms")
    gl = matmul(xn, w.need("ple_gate_w", h), "nn", f"{tag}_gate")
    pp = matmul(pe, w.need("ple_proj_w", h), "nn", f"{tag}_proj")
    return ple_fwd(h, gl, pp, f"{tag}_mix"), (h, xn, gl, pp, pe)


def ple_block_bwd(dh, w, saved, tag, sink, layer, after):
    h, xn, gl, pp, pe = saved
    dpp, dgl = ple_bwd(dh, gl, pp, after, f"{tag}_dmix")
    sink.mm("ple_proj_w", layer, pe, dpp, f"{tag}_dwp")
    sink.mm("ple_gate_w", layer, xn, dgl, f"{tag}_dwg")
    dh2, dg = nt_rms_bwd(dgl, w["ple_gate_w"], h, _row(w["ple_norm"]), dh, f"{tag}_dxn")
    return dh2, dict(ple_norm=dg.reshape(-1))


PER_LAYER = ("norm_ffn1", "ffn1_w_in", "ffn1_w_out", "norm_mix", "norm_ffn2", "ffn2_w_in", "ffn2_w_out",
             "ple_norm", "ple_gate_w", "ple_proj_w")
EVEN_ONLY = ("hyb_w_in", "conv_dw_w", "conv_dw_b", "conv_ln_g", "conv_ln_b", "ssm_conv_w", "ssm_conv_b",
             "ssm_dt_bias", "ssm_a_log", "ssm_d", "ssm_norm", "hyb_w_out")
ODD_ONLY = ("att_w_qkv", "att_b_qkv", "att_sinks", "att_w_o", "att_b_o")


def _layer_index(k, i):
    if k in PER_LAYER:
        return i
    if k in (EVEN_ONLY if i % 2 == 0 else ODD_ONLY):
        return i // 2
    return None


def _stage(k):
    return 0 if k.startswith("ffn1") else (2 if k.startswith(("ffn2", "ple")) else 1)


def trunk_fwd_bwd(x, pe, target, layers, final_norm, sink, stage_done):
    depth = len(layers)
    tables = rope_tables(x.shape[0])
    h = x
    saved = []
    for i, w in enumerate(layers):
        h, s1 = ffn_fwd(h, w["norm_ffn1"], w, ("ffn1_w_in", "ffn1_w_out"), f"l{i}_ffn1")
        if i % 2 == 0:
            h, s2 = hyb_fwd(h, w, f"l{i}_hyb")
        else:
            h, s2 = att_fwd(h, w, tables, f"l{i}_att")
        h, s3 = ffn_fwd(h, w["norm_ffn2"], w, ("ffn2_w_in", "ffn2_w_out"), f"l{i}_ffn2")
        h, s4 = ple_block_fwd(h, pe[i], w, f"l{i}_ple")
        saved.append((s1, s2, s3, s4))
    dh, dgf, loss = loss_head(h, _row(final_norm), target, "loss_head")
    grads = {}
    tie = dgf
    for i in reversed(range(depth)):
        w = layers[i]
        s1, s2, s3, s4 = saved[i]
        dh, g = ple_block_bwd(dh, w, s4, f"l{i}_ple", sink, i, tie)
        odd = i % 2 == 1
        out = ffn_bwd(dh, w["norm_ffn2"], w["ffn2_w_in"], w["ffn2_w_out"], s3, f"l{i}_ffn2", sink,
                      ("ffn2_w_in", "ffn2_w_out"), i, tie, colsum=odd)
        dh = out[0]
        g.update(norm_ffn2=out[1])
        if odd:
            dh, gm = att_bwd(dh, out[2], w, s2, tables, f"l{i}_att", sink, i // 2)
        else:
            dh, gm = hyb_bwd(dh, w, s2, f"l{i}_hyb", sink, i // 2)
        g.update(gm)
        tie = stage_done(i, 1, tie)
        out = ffn_bwd(dh, w["norm_ffn1"], w["ffn1_w_in"], w["ffn1_w_out"], s1, f"l{i}_ffn1", sink,
                      ("ffn1_w_in", "ffn1_w_out"), i, tie)
        dh = out[0]
        g.update(norm_ffn1=out[1])
        tie = stage_done(i, 0, tie)
        for k, v in g.items():
            grads.setdefault(k, []).insert(0, v)
    grads = {k: jnp.stack(v) for k, v in grads.items()}
    grads["final_norm"] = dgf.reshape(-1)
    return loss, dh, grads


def _me():
    return lax.axis_index("x"), lax.axis_index("y"), lax.axis_index("c")


def _flip(v, f):
    return 1 - v if f else v


def _remote(src, dst, send_sems, recv_sems, k, dev):
    return pltpu.make_async_remote_copy(src_ref=src, dst_ref=dst, send_sem=send_sems.at[k], recv_sem=recv_sems.at[k],
                                        device_id=dev, device_id_type=MESH)


CHIP_FLIPS = ((1, 0), (0, 1), (1, 1))
DEV_FLIPS = tuple((fx, fy, fc) for fx in (0, 1) for fy in (0, 1) for fc in (0, 1))[1:]


def all_gather_chips(xs, name):
    na = len(xs)
    halves = [x.shape[0] // 2 for x in xs]
    assert all(x.shape[0] % 2 == 0 for x in xs)

    def body(*refs):
        x_refs, out_refs = refs[:na], refs[na:2 * na]
        send_sems, recv_sems = refs[2 * na:]
        mx, my, mc = _me()
        chip = 2 * mx + my
        sib = (mx, my, 1 - mc)
        peers = [(_flip(mx, fx), _flip(my, fy)) for fx, fy in CHIP_FLIPS]

        def rows(a, ch, hc):
            return out_refs[a].at[ch, pl.ds(hc * halves[a], halves[a]), :]

        def src(a):
            return x_refs[a].at[pl.ds(mc * halves[a], halves[a]), :]

        first = [_remote(src(a), rows(a, chip, mc), send_sems, recv_sems, 6 * a + j, (px, py, mc))
                 for j, (px, py) in enumerate(peers) for a in range(na)]
        for cp in first:
            cp.start()
        passed = []
        for j, (px, py) in enumerate(peers):
            for a in range(na):
                landed = rows(a, 2 * px + py, mc)
                _remote(src(a), landed, send_sems, recv_sems, 6 * a + j, (px, py, mc)).wait_recv()
                fw = _remote(landed, landed, send_sems, recv_sems, 6 * a + 3 + j, sib)
                fw.start()
                passed.append(fw)
        for j, (px, py) in enumerate(peers):
            for a in range(na):
                _remote(src(a), rows(a, 2 * px + py, 1 - mc), send_sems, recv_sems, 6 * a + 3 + j, sib).wait_recv()
        for cp in first + passed:
            cp.wait_send()

    outs = pl.pallas_call(
        body, name=name, out_shape=[S((N_CHIPS,) + x.shape, x.dtype) for x in xs], in_specs=[ANY] * na, out_specs=[ANY] * na,
        scratch_shapes=[pltpu.SemaphoreType.DMA((6 * na,)), pltpu.SemaphoreType.DMA((6 * na,))])(*xs)
    chip = 2 * lax.axis_index("x") + lax.axis_index("y")
    return [lax.dynamic_update_slice_in_dim(o, x[None], chip, axis=0) for o, x in zip(outs, xs)]


HBM = pl.BlockSpec(memory_space=pltpu.HBM)
SEM = pl.BlockSpec(memory_space=pltpu.SEMAPHORE)
DATAFLOW = pltpu.SideEffectType.DATAFLOW_SIDE_EFFECTING


def _core_half(ref, axis, c):
    if axis is None:
        return ref
    h = ref.shape[axis] // 2
    return ref.at[pl.ds(c * h, h), :] if axis == 0 else ref.at[:, pl.ds(c * h, h)]


def gather_start(xs, lands, halves, after, name):
    na = len(xs)

    def body(*refs):
        x_refs, land_refs = refs[:na], refs[na:2 * na]
        send_sems, recv_sems = refs[2 * na + 1], refs[2 * na + 2]
        token = refs[-1]
        mx, my, mc = _me()
        chip = 2 * mx + my
        for a in range(na):
            for j, (fx, fy) in enumerate(CHIP_FLIPS):
                _remote(_core_half(x_refs[a], halves[a], mc), _core_half(land_refs[a].at[chip], halves[a], mc),
                        send_sems, recv_sems, 3 * a + j, (_flip(mx, fx), _flip(my, fy), mc)).start()
        token[...] = jnp.zeros_like(token)

    outs = pl.pallas_call(
        body, name=name,
        out_shape=(pltpu.SemaphoreType.DMA((3 * na,)), pltpu.SemaphoreType.DMA((3 * na,)))
        + tuple(pltpu.HBM(x.shape, x.dtype) for x in xs) + tuple(pltpu.HBM(l.shape, l.dtype) for l in lands)
        + (S((8, LANE), F32),),
        in_specs=[HBM] * (2 * na) + [pl.BlockSpec(memory_space=pl.ANY)],
        out_specs=(SEM, SEM) + (HBM,) * (2 * na) + (pl.BlockSpec(memory_space=pltpu.VMEM),),
        input_output_aliases={a: 2 + a for a in range(2 * na)},
        compiler_params=pltpu.CompilerParams(has_side_effects=DATAFLOW),
    )(*[pltpu.with_memory_space_constraint(t, pltpu.HBM) for t in list(xs) + list(lands)], after)
    return outs[0], outs[1], list(outs[2:2 + na]), list(outs[2 + na:2 + 2 * na])


def gather_wait(send_sems, recv_sems, xs, lands, halves, first, after, name):
    na = len(xs)

    def body(*refs):
        x_refs, land_refs = refs[:na], refs[na:2 * na]
        send_sems, recv_sems = refs[2 * na], refs[2 * na + 1]
        mx, my, mc = _me()
        for a in range(na):
            for j, (fx, fy) in enumerate(CHIP_FLIPS):
                px, py = _flip(mx, fx), _flip(my, fy)
                cp = _remote(_core_half(x_refs[a], halves[a], mc), _core_half(land_refs[a].at[2 * px + py], halves[a], mc),
                             send_sems, recv_sems, 3 * (first + a) + j, (px, py, mc))
                cp.wait_send()
                cp.wait_recv()

    outs = pl.pallas_call(
        body, name=name,
        out_shape=tuple(pltpu.HBM(x.shape, x.dtype) for x in xs) + tuple(pltpu.HBM(l.shape, l.dtype) for l in lands),
        in_specs=[HBM] * (2 * na) + [SEM, SEM, pl.BlockSpec(memory_space=pl.ANY)], out_specs=(HBM,) * (2 * na),
        input_output_aliases={a: a for a in range(2 * na)},
        compiler_params=pltpu.CompilerParams(has_side_effects=DATAFLOW),
    )(*xs, *lands, send_sems, recv_sems, after)
    return list(outs[na:])


def forward_halves(land, axis, name):
    def body(in_ref, out_ref, send_sems, recv_sems):
        del in_ref
        mx, my, mc = _me()
        sib = (mx, my, 1 - mc)
        slots = [2 * _flip(mx, fx) + _flip(my, fy) for fx, fy in CHIP_FLIPS]
        cps = [_remote(_core_half(out_ref.at[s], axis, mc), _core_half(out_ref.at[s], axis, mc), send_sems, recv_sems, j, sib)
               for j, s in enumerate(slots)]
        for cp in cps:
            cp.start()
        for j, s in enumerate(slots):
            _remote(_core_half(out_ref.at[s], axis, mc), _core_half(out_ref.at[s], axis, 1 - mc), send_sems, recv_sems, j, sib).wait_recv()
        for cp in cps:
            cp.wait_send()

    return pl.pallas_call(
        body, name=name, out_shape=S(land.shape, land.dtype), in_specs=[ANY], out_specs=ANY, input_output_aliases={0: 0},
        scratch_shapes=[pltpu.SemaphoreType.DMA((3,)), pltpu.SemaphoreType.DMA((3,))])(land)


def all_gather_devices(v, name):
    r, l = v.shape

    def body(v_ref, out_ref, send_sems, recv_sems):
        mx, my, mc = _me()
        me = 4 * mx + 2 * my + mc
        peers = [(_flip(mx, fx), _flip(my, fy), _flip(mc, fc)) for fx, fy, fc in DEV_FLIPS]
        sends = [_remote(v_ref, out_ref.at[me], send_sems, recv_sems, j, p) for j, p in enumerate(peers)]
        for cp in sends:
            cp.start()
        for j, (px, py, pc) in enumerate(peers):
            _remote(v_ref, out_ref.at[4 * px + 2 * py + pc], send_sems, recv_sems, j, (px, py, pc)).wait_recv()
        for cp in sends:
            cp.wait_send()

    out = pl.pallas_call(
        body, name=name, out_shape=S((N_DEV, r, l), v.dtype), in_specs=[ANY], out_specs=ANY,
        scratch_shapes=[pltpu.SemaphoreType.DMA((7,)), pltpu.SemaphoreType.DMA((7,))])(v)
    me = 4 * lax.axis_index("x") + 2 * lax.axis_index("y") + lax.axis_index("c")
    return lax.dynamic_update_slice_in_dim(out, v[None], me, axis=0)


def sum_devices(g8, name):
    nd, r, l = g8.shape
    tile = r
    for t in (512, 256, 128, 64, 32, 16, 8):
        if r % t == 0:
            tile = t
            break

    def body(g_ref, o_ref):
        acc = g_ref[0]
        for d in range(1, nd):
            acc = acc + g_ref[d]
        o_ref[...] = acc

    return _call(body, name, (r // tile,), [pl.BlockSpec((nd, tile, l), lambda i: (0, i, 0))], _rs(tile, l), S((r, l), F32),
                 sem=("parallel",))(g8)


def exchange_halves(gs, name):
    na = len(gs)
    nch = gs[0].shape[0]

    def body(*refs):
        g_refs, out_refs = refs[:na], refs[na:2 * na]
        send_sems, recv_sems = refs[2 * na:]
        mx, my, mc = _me()
        sib = (mx, my, 1 - mc)
        cps = []
        for a in range(na):
            half = gs[a].shape[1] // 2
            for j in range(nch):
                cps.append(_remote(g_refs[a].at[j, pl.ds((1 - mc) * half, half), :], out_refs[a].at[j],
                                   send_sems, recv_sems, nch * a + j, sib))
        for cp in cps:
            cp.start()
        for cp in cps:
            cp.wait_recv()
        for cp in cps:
            cp.wait_send()

    return pl.pallas_call(
        body, name=name, out_shape=[S((nch, g.shape[1] // 2, g.shape[2]), g.dtype) for g in gs],
        in_specs=[ANY] * na, out_specs=[ANY] * na,
        scratch_shapes=[pltpu.SemaphoreType.DMA((nch * na,)), pltpu.SemaphoreType.DMA((nch * na,))])(*gs)


def add_halves(g4, got, name):
    nch, r, l = g4.shape
    half = r // 2
    tile = _pick_rows(half)
    nt = half // tile

    def body(g_ref, r_ref, a_ref, own_ref):
        j = pl.program_id(1)
        chip = 2 * lax.axis_index("x") + lax.axis_index("y")
        val = g_ref[0].astype(F32) + r_ref[0].astype(F32)
        a_ref[0] = val.astype(a_ref.dtype)

        @pl.when(j == chip)
        def _():
            own_ref[...] = val

    return pl.pallas_call(
        body, name=name, grid=(nt, nch),
        in_specs=[pl.BlockSpec((1, tile, l), lambda i, j: (j, lax.axis_index("c") * nt + i, 0)),
                  pl.BlockSpec((1, tile, l), lambda i, j: (j, i, 0))],
        out_specs=[pl.BlockSpec((1, tile, l), lambda i, j: (j, i, 0)), pl.BlockSpec((tile, l), lambda i, j: (i, 0))],
        out_shape=[S((nch, half, l), MXU_DTYPE), S((half, l), F32)],
        compiler_params=pltpu.CompilerParams(dimension_semantics=("parallel", "arbitrary"), vmem_limit_bytes=VMEM_LIMIT))(g4, got)


def _pick_rows(r, cap=640):
    return next((t for t in range(cap - cap % 16, 15, -16) if r % t == 0), r)


def exchange_chips(parts, name):
    na = len(parts)

    def body(*refs):
        a_refs, out_refs = refs[:na], refs[na:2 * na]
        send_sems, recv_sems = refs[2 * na:]
        mx, my, mc = _me()
        peers = [(_flip(mx, fx), _flip(my, fy)) for fx, fy in CHIP_FLIPS]
        cps = [_remote(a_refs[a].at[2 * px + py], out_refs[a].at[j], send_sems, recv_sems, 3 * a + j, (px, py, mc))
               for j, (px, py) in enumerate(peers) for a in range(na)]
        for cp in cps:
            cp.start()
        for cp in cps:
            cp.wait_recv()
        for cp in cps:
            cp.wait_send()

    return pl.pallas_call(
        body, name=name, out_shape=[S((3,) + p.shape[1:], p.dtype) for p in parts], in_specs=[ANY] * na, out_specs=[ANY] * na,
        scratch_shapes=[pltpu.SemaphoreType.DMA((3 * na,)), pltpu.SemaphoreType.DMA((3 * na,))])(*parts)


def add_chips(own, got, name):
    h, l = own.shape
    tile = _pick_rows(h)

    def body(o_ref, g_ref, out_ref):
        out_ref[...] = ((o_ref[...] + g_ref[0].astype(F32)) + g_ref[1].astype(F32)) + g_ref[2].astype(F32)

    nt = h // tile
    return _call(body, name, (nt,), [_rs(tile, l), pl.BlockSpec((3, tile, l), lambda i: (0, i, 0))],
                 pl.BlockSpec((tile, l), lambda i: (lax.axis_index("c") * nt + i, 0)),
                 S((2 * h, l), F32), sem=("parallel",))(own, got)


def join_halves(bufs, name):
    na = len(bufs)

    def body(*refs):
        out_refs = refs[na:2 * na]
        send_sems, recv_sems = refs[2 * na:]
        mx, my, mc = _me()
        sib = (mx, my, 1 - mc)

        def half(a, hc):
            h = bufs[a].shape[0] // 2
            return out_refs[a].at[pl.ds(hc * h, h), :]

        cps = [_remote(half(a, mc), half(a, mc), send_sems, recv_sems, a, sib) for a in range(na)]
        for cp in cps:
            cp.start()
        for a in range(na):
            _remote(half(a, mc), half(a, 1 - mc), send_sems, recv_sems, a, sib).wait_recv()
        for cp in cps:
            cp.wait_send()

    return pl.pallas_call(
        body, name=name, out_shape=[S(b.shape, b.dtype) for b in bufs], in_specs=[ANY] * na, out_specs=[ANY] * na,
        input_output_aliases={a: a for a in range(na)},
        scratch_shapes=[pltpu.SemaphoreType.DMA((na,)), pltpu.SemaphoreType.DMA((na,))])(*bufs)


def exchange_chips_start(parts, name):
    na = len(parts)
    lands = [lax.empty((3,) + p.shape[1:], p.dtype) for p in parts]

    def body(*refs):
        a_refs, land_refs = refs[:na], refs[na:2 * na]
        send_sems, recv_sems = refs[2 * na], refs[2 * na + 1]
        mx, my, mc = _me()
        for j, (fx, fy) in enumerate(CHIP_FLIPS):
            px, py = _flip(mx, fx), _flip(my, fy)
            for a in range(na):
                _remote(a_refs[a].at[2 * px + py], land_refs[a].at[j], send_sems, recv_sems, 3 * a + j, (px, py, mc)).start()
        refs[-1][...] = jnp.zeros_like(refs[-1])

    outs = pl.pallas_call(
        body, name=name,
        out_shape=(pltpu.SemaphoreType.DMA((3 * na,)), pltpu.SemaphoreType.DMA((3 * na,)))
        + tuple(pltpu.HBM(t.shape, t.dtype) for t in list(parts) + lands) + (S((8, LANE), F32),),
        in_specs=[HBM] * (2 * na), out_specs=(SEM, SEM) + (HBM,) * (2 * na) + (pl.BlockSpec(memory_space=pltpu.VMEM),),
        input_output_aliases={a: 2 + a for a in range(2 * na)},
        compiler_params=pltpu.CompilerParams(has_side_effects=DATAFLOW),
    )(*[pltpu.with_memory_space_constraint(t, pltpu.HBM) for t in list(parts) + lands])
    return outs[0], outs[1], list(outs[2:2 + na]), list(outs[2 + na:2 + 2 * na]), outs[-1]


def exchange_chips_wait(send_sems, recv_sems, parts, lands, after, name):
    na = len(parts)

    def body(*refs):
        a_refs, land_refs = refs[:na], refs[na:2 * na]
        send_sems, recv_sems = refs[2 * na], refs[2 * na + 1]
        mx, my, mc = _me()
        for j, (fx, fy) in enumerate(CHIP_FLIPS):
            px, py = _flip(mx, fx), _flip(my, fy)
            for a in range(na):
                cp = _remote(a_refs[a].at[2 * px + py], land_refs[a].at[j], send_sems, recv_sems, 3 * a + j, (px, py, mc))
                cp.wait_send()
                cp.wait_recv()

    outs = pl.pallas_call(
        body, name=name, out_shape=tuple(pltpu.HBM(t.shape, t.dtype) for t in list(parts) + list(lands)),
        in_specs=[HBM] * (2 * na) + [SEM, SEM, pl.BlockSpec(memory_space=pl.ANY)], out_specs=(HBM,) * (2 * na),
        input_output_aliases={a: a for a in range(2 * na)},
        compiler_params=pltpu.CompilerParams(has_side_effects=DATAFLOW),
    )(*parts, *lands, send_sems, recv_sems, after)
    return list(outs[na:])


def reduce_begin(gs, tag):
    got = exchange_halves(gs, f"{tag}_d2d")
    sums = [add_halves(g, r, f"{tag}_add1_{i}") for i, (g, r) in enumerate(zip(gs, got))]
    return [own for _, own in sums], exchange_chips_start([a for a, _ in sums], f"{tag}_ici_start")


def reduce_end(state, after, tag):
    owns, (send_sems, recv_sems, parts, lands, _) = state
    got = exchange_chips_wait(send_sems, recv_sems, parts, lands, after, f"{tag}_ici_wait")
    return [add_chips(own, r, f"{tag}_add2_{i}") for i, (own, r) in enumerate(zip(owns, got))]


PACK_L = 1024
BIG_ROW_MULT = 512


def _pack(arrs, dtype, row_mult, lead=None):
    lead_shape = () if lead is None else arrs[0].shape[:lead]
    flat = jnp.concatenate([a.astype(dtype).reshape(lead_shape + (-1,)) for a in arrs], axis=-1)
    n = flat.shape[-1]
    unit = row_mult * PACK_L
    total = -(-n // unit) * unit
    flat = jnp.pad(flat, [(0, 0)] * len(lead_shape) + [(0, total - n)])
    return flat.reshape(lead_shape + (total // PACK_L, PACK_L))


def _unpack(packed, shapes, lead=None):
    lead_shape = () if lead is None else packed.shape[:lead]
    flat = packed.reshape(lead_shape + (-1,))
    out, off = [], 0
    for shp in shapes:
        n = int(np.prod(shp))
        out.append(flat[..., off:off + n].reshape(lead_shape + tuple(shp)))
        off += n
    return out


def _to_full(gathered, axis):
    t = jnp.moveaxis(gathered, 0, axis)
    shp = t.shape
    return t.reshape(shp[:axis] + (shp[axis] * shp[axis + 1],) + shp[axis + 2:])


def _to_chip_major(full, axis):
    shp = full.shape
    t = full.reshape(shp[:axis] + (N_CHIPS, shp[axis] // N_CHIPS) + shp[axis + 1:])
    return jnp.moveaxis(t, axis, 0)


WEIGHTS = ("norm_ffn1", "ffn1_w_in", "ffn1_w_out", "norm_mix", "norm_ffn2", "ffn2_w_in", "ffn2_w_out", "ple_norm",
           "ple_gate_w", "ple_proj_w", "hyb_w_in", "conv_dw_w", "conv_dw_b", "conv_ln_g", "conv_ln_b", "ssm_conv_w",
           "ssm_conv_b", "ssm_dt_bias", "ssm_a_log", "ssm_d", "ssm_norm", "hyb_w_out", "att_w_qkv", "att_b_qkv",
           "att_sinks", "att_w_o", "att_b_o", "final_norm")
SHARD_AXIS = dict(ffn1_w_in=2, ffn1_w_out=1, ffn2_w_in=2, ffn2_w_out=1, ple_gate_w=1, ple_proj_w=2, hyb_w_in=2,
                  conv_dw_w=2, ssm_conv_w=2, hyb_w_out=1, att_w_qkv=2, att_b_qkv=1, att_w_o=1, att_b_o=1)
BIG = ("ffn1_w_in", "ffn1_w_out", "ffn2_w_in", "ffn2_w_out", "ple_gate_w", "ple_proj_w", "hyb_w_in", "hyb_w_out",
       "att_w_qkv", "att_w_o")
TRANSPOSED = ("hyb_w_in",)
N_FORWARDED = 3


def _shard_axis(k):
    return 1 if k in TRANSPOSED else SHARD_AXIS[k]
SMALL_SHARDED = ("conv_dw_w", "ssm_conv_w", "att_b_qkv", "att_b_o")
SMALL = tuple(k for k in WEIGHTS if k not in BIG)


def _step(x, p, target, w, m, v):
    mx, my = lax.axis_index("x"), lax.axis_index("y")
    chip = 2 * mx + my
    w, m, v = ({k: (a.transpose(0, 2, 1) if k in TRANSPOSED else a) for k, a in d.items()} for d in (w, m, v))

    depth = w["norm_ffn1"].shape[0]
    order = sorted([(k, i) for i in range(depth) for k in BIG if _layer_index(k, i) is not None],
                   key=lambda t: (t[1], _stage(t[0])))
    small_g = all_gather_devices(_pack([w[k] for k in SMALL_SHARDED], F32, 8), "gather_small")
    shards = [w[k][_layer_index(k, i)].astype(MXU_DTYPE) for k, i in order]
    lands = [lax.dynamic_update_slice_in_dim(lax.empty((N_CHIPS,) + s.shape, s.dtype), s[None], chip, axis=0) for s in shards]
    halves = [(0 if s.shape[0] % 32 == 0 else 1) if pos < N_FORWARDED else None for pos, s in enumerate(shards)]
    send_sems, recv_sems, shards, lands = gather_start(shards, lands, halves, small_g, "gather_start")

    def fetch(i, k, after):
        p = order.index((k, i))
        g, = gather_wait(send_sems, recv_sems, [shards[p]], [lands[p]], [halves[p]], p, after, f"gather_wait_l{i}_{k}")
        if halves[p] is not None:
            g = forward_halves(g, halves[p], f"gather_forward_l{i}_{k}")
        if _shard_axis(k) == 2:
            return ColSharded(g)
        return g.reshape(-1, g.shape[-1])

    small_g = small_g[0::2]
    small_full = {k: _to_full(g, SHARD_AXIS[k])
                  for k, g in zip(SMALL_SHARDED, _unpack(small_g, [w[k].shape for k in SMALL_SHARDED], lead=1))}
    layers = [LayerWeights({k: small_full.get(k, w[k])[_layer_index(k, i)] for k in SMALL if _layer_index(k, i) is not None},
                           functools.partial(fetch, i)) for i in range(depth)]

    def bucket_of(layer, stage):
        return (layer, 0) if layer > 0 else (0, min(stage, 1))

    sink = GradSink({k: w[k].shape for k in BIG}, bucket_of)
    begun = {}

    def stage_done(i, stage, tie):
        b = bucket_of(i, stage)
        if stage > 0 and bucket_of(i, stage - 1) == b:
            return tie
        begun[b] = reduce_begin(list(sink.bufs[b].values()), f"grads_l{b[0]}_{b[1]}")
        return begun[b][1][-1]

    loss, dx, grads = trunk_fwd_bwd(x[0], p[:, 0], target[0], layers, w["final_norm"], sink, stage_done)

    results = {}

    def finish(buckets, after, tag):
        halves = {b: reduce_end(begun[b], after, f"grads_l{b[0]}_{b[1]}") for b in buckets}
        joined = iter(join_halves([h for b in buckets for h in halves[b]], f"grads_join_{tag}"))
        reduced = {b: {c: next(joined) for c in sink.bufs[b]} for b in buckets}
        last = after
        for (k, li), (b, key, _, off, r) in sink.where.items():
            if b in buckets:
                g = reduced[b][key]
                if r % 8:
                    g, off = g[off:off + r], 0
                results[k] = adamw_layer(w[k], g, off, m[k], v[k], li, results.get(k), f"adamw_{k}_{li}")
                last = results[k][1]
        return last

    order_b = list(begun)
    started_last = begun[order_b[-1]][1][-1]
    finish(order_b[-1:], finish(order_b[:-1], started_last, "early") if len(order_b) > 1 else dx, "last")
    g_out = {k: results[k][0] for k in BIG}
    vec = _pack([loss[0:1, 0:1]] + [grads[k] for k in SMALL], F32, 8)
    vec = sum_devices(all_gather_devices(vec, "gather_vectors"), "sum_vectors")
    parts = _unpack(vec, [(1, 1)] + [grads[k].shape for k in SMALL])
    loss_out = parts[0].reshape(())
    for k, g in zip(SMALL, parts[1:]):
        if k in SHARD_AXIS:
            ax = SHARD_AXIS[k]
            g = lax.dynamic_slice_in_dim(g, chip * w[k].shape[ax], w[k].shape[ax], axis=ax)
        g_out[k] = g

    for k in TRANSPOSED:
        results[k] = [a.transpose(0, 2, 1) for a in results[k]]
    g_out.update({k: results[k][0] for k in TRANSPOSED})
    delta, new_m, new_v = ({k: results[k][j] for k in BIG} for j in (1, 2, 3))
    shapes = [w[k].shape for k in SMALL]
    packed = [_pack([src[k] for k in SMALL], F32, 8) for src in (w, g_out, m, v)]
    outs = adamw(*packed, "adamw_small")
    for dst, o in zip((delta, new_m, new_v), outs):
        for k, a in zip(SMALL, _unpack(o, shapes)):
            dst[k] = a
    return ((loss_out, dx[None]) + tuple(g_out[k] for k in WEIGHTS) + tuple(delta[k] for k in WEIGHTS)
            + tuple(new_m[k] for k in WEIGHTS) + tuple(new_v[k] for k in WEIGHTS))


def kernel(x, p, norm_ffn1, ffn1_w_in, ffn1_w_out, norm_mix, norm_ffn2, ffn2_w_in, ffn2_w_out, ple_norm, ple_gate_w, ple_proj_w, hyb_w_in, conv_dw_w, conv_dw_b, conv_ln_g, conv_ln_b, ssm_conv_w, ssm_conv_b, ssm_dt_bias, ssm_a_log, ssm_d, ssm_norm, hyb_w_out, att_w_qkv, att_b_qkv, att_sinks, att_w_o, att_b_o, final_norm, loss_target, m_norm_ffn1, m_ffn1_w_in, m_ffn1_w_out, m_norm_mix, m_norm_ffn2, m_ffn2_w_in, m_ffn2_w_out, m_ple_norm, m_ple_gate_w, m_ple_proj_w, m_hyb_w_in, m_conv_dw_w, m_conv_dw_b, m_conv_ln_g, m_conv_ln_b, m_ssm_conv_w, m_ssm_conv_b, m_ssm_dt_bias, m_ssm_a_log, m_ssm_d, m_ssm_norm, m_hyb_w_out, m_att_w_qkv, m_att_b_qkv, m_att_sinks, m_att_w_o, m_att_b_o, m_final_norm, v_norm_ffn1, v_ffn1_w_in, v_ffn1_w_out, v_norm_mix, v_norm_ffn2, v_ffn2_w_in, v_ffn2_w_out, v_ple_norm, v_ple_gate_w, v_ple_proj_w, v_hyb_w_in, v_conv_dw_w, v_conv_dw_b, v_conv_ln_g, v_conv_ln_b, v_ssm_conv_w, v_ssm_conv_b, v_ssm_dt_bias, v_ssm_a_log, v_ssm_d, v_ssm_norm, v_hyb_w_out, v_att_w_qkv, v_att_b_qkv, v_att_sinks, v_att_w_o, v_att_b_o, v_final_norm):
    given = locals()
    w = {k: given[k] for k in WEIGHTS}
    m = {k: given["m_" + k] for k in WEIGHTS}
    v = {k: given["v_" + k] for k in WEIGHTS}
    return _step(x, p, loss_target, w, m, v)
```

```python
import functools
import math

import numpy as np
import jax
import jax.numpy as jnp
from jax import lax
from jax.experimental import pallas as pl
from jax.experimental.pallas import tpu as pltpu

F32 = jnp.float32
BF16 = jnp.bfloat16
MXU_DTYPE = jnp.bfloat16
S = jax.ShapeDtypeStruct
MESH = pl.DeviceIdType.MESH

V7X_VMEM_BYTES = 64 * 2**20
VMEM_LIMIT = 48 * 2**20
LANE = 128

EPS = 1e-6
SSM_HEADS = 16
HEAD_DIM = 64
SSM_GROUPS = 2
SSM_STATE = 128
SSM_CONV = 4
CHUNK = 128
CONV_WIDTH = 31
ATT_HEADS = 16
ATT_KV_HEADS = 4
WINDOW = 128
ROPE_THETA = 10000.0
ADAM_LR = 0.001
ADAM_B1 = 0.9
ADAM_B2 = 0.999
ADAM_EPS = 1e-08
ADAM_WD = 0.01
ADAM_STEP = 10

N_CHIPS = 4
N_DEV = 8

NN = ((1,), (0,))
NT = ((1,), (1,))
TN = ((0,), (0,))


def _mm(a, b, dims=NN):
    return lax.dot_general(a.astype(MXU_DTYPE), b.astype(MXU_DTYPE), (dims, ((), ())), preferred_element_type=F32)


def _split3(a):
    hi = a.astype(BF16)
    r = a - hi.astype(F32)
    mid = r.astype(BF16)
    lo = (r - mid.astype(F32)).astype(BF16)
    return hi, mid, lo


def _mm01(a, onehot, dims=NN):
    o = onehot.astype(BF16)
    out = None
    for part in _split3(a):
        t = lax.dot_general(part, o, (dims, ((), ())), preferred_element_type=F32)
        out = t if out is None else out + t
    return out


def _01mm(onehot, a):
    o = onehot.astype(BF16)
    out = None
    for part in _split3(a):
        t = lax.dot_general(o, part, (NN, ((), ())), preferred_element_type=F32)
        out = t if out is None else out + t
    return out


def _sigmoid(x):
    return 0.5 * jnp.tanh(0.5 * x) + 0.5


def _softplus(x):
    return jnp.maximum(x, 0.0) + jnp.log(1.0 + jnp.exp(-jnp.abs(x)))


def _iota(shape, axis):
    return lax.broadcasted_iota(jnp.int32, shape, axis)


def _head_indicator(width, heads, transposed=False):
    per = width // heads
    if transposed:
        return (_iota((heads, width), 1) // per == _iota((heads, width), 0)).astype(F32)
    return (_iota((width, heads), 0) // per == _iota((width, heads), 1)).astype(F32)


def _acc(ref, i, val):
    @pl.when(i == 0)
    def _():
        ref[...] = val

    @pl.when(i > 0)
    def _():
        ref[...] += val


def _rs(tile, width, col=0, shift=0, n=None):
    if shift == 0:
        return pl.BlockSpec((tile, width), lambda i: (i, col))
    if shift < 0:
        return pl.BlockSpec((tile, width), lambda i: (jnp.maximum(i - 1, 0), col))
    return pl.BlockSpec((tile, width), lambda i: (jnp.minimum(i + 1, n - 1), col))


def _ps(shape):
    return pl.BlockSpec(shape, lambda i: (0,) * len(shape))


def _call(body, name, grid, in_specs, out_specs, out_shape, scratch=(), sem=None):
    return pl.pallas_call(
        body, name=name, grid=grid, in_specs=in_specs, out_specs=out_specs, out_shape=out_shape,
        scratch_shapes=list(scratch),
        compiler_params=pltpu.CompilerParams(dimension_semantics=sem, vmem_limit_bytes=VMEM_LIMIT))


def _row_tile(n, target):
    t = min(n, target)
    assert n % t == 0, (n, t)
    return t


def _pick_tile(dim, target):
    if dim <= target:
        return dim
    t = (int(1.4 * target) // LANE) * LANE
    while t >= LANE:
        if dim % t == 0:
            return t
        t -= LANE
    return dim


ANY = pl.BlockSpec(memory_space=pl.ANY)


def _paired(j):
    return (j % 2) * 2 + j // 2


class ColSharded:
    def __init__(self, arr, paired=False):
        self.arr, self.paired = arr, paired
        self.nch, self.rows, self.per = arr.shape
        self.shape = (self.rows, self.nch * self.per)

    def chip(self, j):
        return _paired(j) if self.paired else j


class Slot:
    def __init__(self, buf, kind, per, off, c0=0, paired=False):
        self.buf, self.kind, self.per, self.off, self.c0, self.paired = buf, kind, per, off, c0, paired

    def chip(self, j):
        return _paired(j) if self.paired else j


def matmul(a, b, mode, name, *, out_dtype=F32, scale=None, res=None, bias=None, into=None, tm=1024, tn=1024, tk=1024):
    bshape = b.shape
    if mode == "nn":
        (m, k), (k2, n) = a.shape, bshape
    elif mode == "nt":
        (m, k), (n, k2) = a.shape, bshape
    else:
        (k, m), (k2, n) = a.shape, bshape
    assert k == k2, (a.shape, bshape, mode)
    tm, tn, tk = _pick_tile(m, tm), _pick_tile(n, tn), _pick_tile(k, tk)
    if isinstance(b, ColSharded):
        if mode == "nn":
            tn = b.per
        else:
            assert mode == "nt"
            tk = b.per
    if into is not None:
        if into.kind == "c":
            tn = into.per
            assert into.off % tm == 0 and n == N_CHIPS * into.per
        else:
            tm = max(1, min(m, int(1.4 * 1024)) // into.per) * into.per
            assert m % tm == 0 and into.off % into.per == 0 and into.c0 % (tm // into.per) == 0
    nk = k // tk
    dims = {"nn": NN, "nt": NT, "tn": TN}[mode]
    a_spec = (pl.BlockSpec((tk, tm), lambda i, j, kk: (kk, i)) if mode == "tn"
              else pl.BlockSpec((tm, tk), lambda i, j, kk: (i, kk)))
    if isinstance(b, ColSharded):
        bchip = b.chip
        b_spec = (pl.BlockSpec((None, tk, tn), lambda i, j, kk: (bchip(j), kk, 0)) if mode == "nn"
                  else pl.BlockSpec((None, tn, tk), lambda i, j, kk: (bchip(kk), j, 0)))
        b = b.arr
    else:
        b_spec = (pl.BlockSpec((tn, tk), lambda i, j, kk: (j, kk)) if mode == "nt"
                  else pl.BlockSpec((tk, tn), lambda i, j, kk: (kk, j)))
    plain_o = pl.BlockSpec((tm, tn), lambda i, j, kk: (i, j))
    ins, in_specs = [a, b], [a_spec, b_spec]
    if bias is not None:
        ins.append(bias)
        in_specs.append(pl.BlockSpec((1, tn), lambda i, j, kk: (0, j)))
    if res is not None:
        ins.append(res)
        in_specs.append(plain_o)
    aliases = {}
    if into is None:
        o_spec, o_shape = plain_o, S((m, n), out_dtype)
    else:
        aliases = {len(ins): 0}
        ins.append(into.buf)
        in_specs.append(ANY)
        o_shape = S(into.buf.shape, into.buf.dtype)
        if into.kind == "c":
            ob, ochip = into.off // tm, into.chip
            o_spec = pl.BlockSpec((None, tm, tn), lambda i, j, kk: (ochip(j), ob + i, 0))
        else:
            q, ob = tm // into.per, into.off // into.per
            cb = into.c0 // q
            o_spec = pl.BlockSpec((q, into.per, tn), lambda i, j, kk: (cb + i, ob, j))

    def body(*refs):
        a_ref, b_ref = refs[0], refs[1]
        o_ref, acc_ref = refs[-2], refs[-1]
        kk = pl.program_id(2)

        @pl.when(kk == 0)
        def _():
            acc_ref[...] = jnp.zeros_like(acc_ref)

        acc_ref[...] += _mm(a_ref[...], b_ref[...], dims)

        @pl.when(kk == nk - 1)
        def _():
            out = acc_ref[...]
            if scale is not None:
                out = out * scale
            pos = 2
            if bias is not None:
                out = out + refs[pos][...]
                pos += 1
            if res is not None:
                out = out + refs[pos][...]
            o_ref[...] = out.astype(o_ref.dtype).reshape(o_ref.shape)

    return pl.pallas_call(
        body, name=name, grid=(m // tm, n // tn, nk), in_specs=in_specs, out_specs=o_spec, out_shape=o_shape,
        scratch_shapes=[pltpu.VMEM((tm, tn), F32)], input_output_aliases=aliases,
        compiler_params=pltpu.CompilerParams(dimension_semantics=("parallel", "parallel", "arbitrary"),
                                             vmem_limit_bytes=VMEM_LIMIT))(*ins)


def rms_fwd(h, g, name):
    n, d = h.shape
    tile = _row_tile(n, 512)

    def body(h_ref, g_ref, o_ref):
        x = h_ref[...]
        r = lax.rsqrt(jnp.mean(x * x, axis=-1, keepdims=True) + EPS)
        o_ref[...] = (x * r * g_ref[...]).astype(o_ref.dtype)

    return _call(body, name, (n // tile,), [_rs(tile, d), _ps((1, d))], _rs(tile, d), S((n, d), MXU_DTYPE),
                 sem=("parallel",))(h, g)


def rms_matmul(h, g, b, mode, name, bias=None):
    n, d = h.shape
    sharded = isinstance(b, ColSharded)
    n_out = b.shape[1] if mode == "nn" else b.shape[0]
    tm = _row_tile(n, 512)
    tn = b.per if sharded else _pick_tile(n_out, 1024)
    if sharded:
        assert mode == "nn"
        bchip = b.chip
        b_spec = pl.BlockSpec((None, d, tn), lambda i, j: (bchip(j), 0, 0))
        b = b.arr
    elif mode == "nn":
        b_spec = pl.BlockSpec((d, tn), lambda i, j: (0, j))
    else:
        b_spec = pl.BlockSpec((tn, d), lambda i, j: (j, 0))
    ins = [h, g, b] + ([bias] if bias is not None else [])
    in_specs = [pl.BlockSpec((tm, d), lambda i, j: (i, 0)), pl.BlockSpec((1, d), lambda i, j: (0, 0)), b_spec]
    if bias is not None:
        in_specs.append(pl.BlockSpec((1, tn), lambda i, j: (0, j)))

    def body(h_ref, g_ref, b_ref, *refs):
        o_ref, xn_ref = refs[-2:]
        x = h_ref[...]
        r = lax.rsqrt(jnp.mean(x * x, axis=-1, keepdims=True) + EPS)
        xn = (x * r * g_ref[...]).astype(xn_ref.dtype)

        @pl.when(pl.program_id(1) == 0)
        def _():
            xn_ref[...] = xn

        out = _mm(xn, b_ref[...], NN if mode == "nn" else NT)
        o_ref[...] = out if bias is None else out + refs[0][...]

    return pl.pallas_call(
        body, name=name, grid=(n // tm, n_out // tn), in_specs=in_specs,
        out_specs=[pl.BlockSpec((tm, tn), lambda i, j: (i, j)), pl.BlockSpec((tm, d), lambda i, j: (i, 0))],
        out_shape=[S((n, n_out), F32), S((n, d), MXU_DTYPE)],
        compiler_params=pltpu.CompilerParams(dimension_semantics=("parallel", "arbitrary"), vmem_limit_bytes=VMEM_LIMIT),
    )(*ins)


def _rms_bwd_math(x, g, dy):
    r = lax.rsqrt(jnp.mean(x * x, axis=-1, keepdims=True) + EPS)
    xh = x * r
    dg = jnp.sum(dy * xh, axis=0, keepdims=True)
    dxh = dy * g
    dx = r * (dxh - xh * jnp.mean(dxh * xh, axis=-1, keepdims=True))
    return dx, dg


def nt_rms_bwd(a, b, h, g, dh_in, name, extra=None, colsum=False, b_kd=False):
    n, k = a.shape
    d = h.shape[1]
    tm = _row_tile(n, 512)
    sharded = isinstance(b, ColSharded)
    tk = b.per if sharded else _pick_tile(k, 1024)
    nk = k // tk
    dims = NN if b_kd else NT
    if sharded:
        bchip = b.chip
        b_spec = pl.BlockSpec((None, d, tk), lambda i, kk: (bchip(kk), 0, 0))
        b = b.arr
    elif b_kd:
        b_spec = pl.BlockSpec((tk, d), lambda i, kk: (kk, 0))
    else:
        b_spec = pl.BlockSpec((d, tk), lambda i, kk: (0, kk))
    row = pl.BlockSpec((tm, d), lambda i, kk: (i, 0))
    vec = pl.BlockSpec((1, d), lambda i, kk: (0, 0))
    ins, in_specs = [a, b, h, g, dh_in], [pl.BlockSpec((tm, tk), lambda i, kk: (i, kk)), b_spec, row, vec, row]
    if extra is not None:
        k2 = extra[0].shape[1]
        ins += list(extra)
        in_specs += [pl.BlockSpec((tm, k2), lambda i, kk: (i, 0)),
                     pl.BlockSpec((k2, d) if b_kd else (d, k2), lambda i, kk: (0, 0))]
    n_in = len(ins)

    def body(*refs):
        a_ref, b_ref, h_ref, g_ref, dh_ref = refs[:5]
        o_ref, dg_ref = refs[n_in], refs[n_in + 1]
        acc_ref = refs[-1]
        i, kk = pl.program_id(0), pl.program_id(1)

        @pl.when(kk == 0)
        def _():
            acc_ref[...] = _mm(refs[5][...], refs[6][...], dims) if extra is not None else jnp.zeros_like(acc_ref)

        acc_ref[...] += _mm(a_ref[...], b_ref[...], dims)

        @pl.when(kk == nk - 1)
        def _():
            dx, dg = _rms_bwd_math(h_ref[...], g_ref[...], acc_ref[...])
            out = dh_ref[...] + dx
            o_ref[...] = out
            _acc(dg_ref, i, dg)
            if colsum:
                _acc(refs[n_in + 2], i, jnp.sum(out, axis=0, keepdims=True))

    n_vec = 2 if colsum else 1
    return pl.pallas_call(
        body, name=name, grid=(n // tm, nk), in_specs=in_specs, out_specs=[row] + [vec] * n_vec,
        out_shape=[S((n, d), F32)] + [S((1, d), F32)] * n_vec, scratch_shapes=[pltpu.VMEM((tm, d), F32)],
        compiler_params=pltpu.CompilerParams(dimension_semantics=("arbitrary", "arbitrary"), vmem_limit_bytes=VMEM_LIMIT),
    )(*ins)


def rms_bwd(h, g, dxn, dh_in, name, colsum=False):
    n, d = h.shape
    tile = _row_tile(n, 256)

    def body(h_ref, g_ref, dxn_ref, dh_ref, o_ref, dg_ref, *cs_ref):
        i = pl.program_id(0)
        dx, dg = _rms_bwd_math(h_ref[...], g_ref[...], dxn_ref[...].astype(F32))
        out = dh_ref[...] + dx
        o_ref[...] = out
        _acc(dg_ref, i, dg)
        if colsum:
            _acc(cs_ref[0], i, jnp.sum(out, axis=0, keepdims=True))

    outs = [S((n, d), F32), S((1, d), F32)] + ([S((1, d), F32)] if colsum else [])
    ospecs = [_rs(tile, d), _ps((1, d))] + ([_ps((1, d))] if colsum else [])
    return _call(body, name, (n // tile,), [_rs(tile, d), _ps((1, d)), _rs(tile, d), _rs(tile, d)], ospecs, outs,
                 sem=("arbitrary",))(h, g, dxn, dh_in)


def swiglu_in(h, g, w_in, name):
    n, d = h.shape
    per = w_in.per
    nj = w_in.nch // 2
    tile = _row_tile(n, 512)

    def body(h_ref, g_ref, wg_ref, wu_ref, xn_ref, u_ref, hm_ref):
        x = h_ref[...]
        r = lax.rsqrt(jnp.mean(x * x, axis=-1, keepdims=True) + EPS)
        xn = (x * r * g_ref[...]).astype(xn_ref.dtype)

        @pl.when(pl.program_id(1) == 0)
        def _():
            xn_ref[...] = xn

        a = _mm(xn, wg_ref[...])
        b = _mm(xn, wu_ref[...])
        u_ref[:, :per] = a.astype(u_ref.dtype)
        u_ref[:, per:] = b.astype(u_ref.dtype)
        hm_ref[...] = (a * _sigmoid(a) * b).astype(hm_ref.dtype)

    return pl.pallas_call(
        body, name=name, grid=(n // tile, nj),
        in_specs=[pl.BlockSpec((tile, d), lambda i, j: (i, 0)), pl.BlockSpec((1, d), lambda i, j: (0, 0)),
                  pl.BlockSpec((None, d, per), lambda i, j: (j, 0, 0)), pl.BlockSpec((None, d, per), lambda i, j: (nj + j, 0, 0))],
        out_specs=[pl.BlockSpec((tile, d), lambda i, j: (i, 0)), pl.BlockSpec((tile, 2 * per), lambda i, j: (i, j)),
                   pl.BlockSpec((tile, per), lambda i, j: (i, j))],
        out_shape=[S((n, d), MXU_DTYPE), S((n, 2 * nj * per), MXU_DTYPE), S((n, nj * per), MXU_DTYPE)],
        compiler_params=pltpu.CompilerParams(dimension_semantics=("parallel", "arbitrary"), vmem_limit_bytes=VMEM_LIMIT),
    )(h, g, w_in.arr, w_in.arr)


def swiglu_out_bwd(dh, w_out, u, after, name):
    n, d = dh.shape
    f = w_out.shape[0]
    per = u.shape[1] // 4
    nj = f // per
    tile = _row_tile(n, 512)

    def body(dh_ref, w_ref, u_ref, after_ref, du_ref):
        dm = 0.5 * _mm(dh_ref[...], w_ref[...], NT)
        a = u_ref[:, :per].astype(F32)
        b = u_ref[:, per:].astype(F32)
        s = _sigmoid(a)
        du_ref[:, :per] = (dm * b * s * (1.0 + a * (1.0 - s))).astype(du_ref.dtype)
        du_ref[:, per:] = (dm * a * s).astype(du_ref.dtype)

    return pl.pallas_call(
        body, name=name, grid=(n // tile, nj),
        in_specs=[pl.BlockSpec((tile, d), lambda i, j: (i, 0)), pl.BlockSpec((per, d), lambda i, j: (j, 0)),
                  pl.BlockSpec((tile, 2 * per), lambda i, j: (i, j)), ANY],
        out_specs=pl.BlockSpec((tile, 2 * per), lambda i, j: (i, j)),
        out_shape=S(u.shape, MXU_DTYPE),
        compiler_params=pltpu.CompilerParams(dimension_semantics=("parallel", "parallel"), vmem_limit_bytes=VMEM_LIMIT),
    )(dh, w_out, u, after)


def ple_fwd(h, gl, pp, name):
    n, d = h.shape
    tile = _row_tile(n, 512)

    def body(h_ref, gl_ref, pp_ref, o_ref):
        o_ref[...] = h_ref[...] + _sigmoid(gl_ref[...]) * pp_ref[...]

    return _call(body, name, (n // tile,), [_rs(tile, d)] * 3, _rs(tile, d), S((n, d), F32), sem=("parallel",))(h, gl, pp)


def ple_bwd(dh, gl, pp, after, name):
    n, d = dh.shape
    tile = _row_tile(n, 512)

    def body(dh_ref, gl_ref, pp_ref, after_ref, dpp_ref, dgl_ref):
        g = _sigmoid(gl_ref[...])
        dh_ = dh_ref[...]
        dpp_ref[...] = (dh_ * g).astype(dpp_ref.dtype)
        dgl_ref[...] = (dh_ * pp_ref[...] * g * (1.0 - g)).astype(dgl_ref.dtype)

    return _call(body, name, (n // tile,), [_rs(tile, d)] * 3 + [ANY], [_rs(tile, d)] * 2, [S((n, d), MXU_DTYPE)] * 2,
                 sem=("parallel",))(dh, gl, pp, after)


def loss_head(h, g, target, name):
    n, d = h.shape
    tile = _row_tile(n, 256)

    def body(h_ref, g_ref, t_ref, dh_ref, dg_ref, loss_ref):
        i = pl.program_id(0)
        x = h_ref[...]
        gg = g_ref[...]
        r = lax.rsqrt(jnp.mean(x * x, axis=-1, keepdims=True) + EPS)
        err = x * r * gg - t_ref[...]
        part = 0.5 * jnp.sum(jnp.mean(err * err, axis=-1, keepdims=True), axis=0, keepdims=True)
        dx, dg = _rms_bwd_math(x, gg, err * (1.0 / d))
        dh_ref[...] = dx
        _acc(dg_ref, i, dg)
        _acc(loss_ref, i, jnp.broadcast_to(part, (8, LANE)))

    return _call(body, name, (n // tile,), [_rs(tile, d), _ps((1, d)), _rs(tile, d)],
                 [_rs(tile, d), _ps((1, d)), _ps((8, LANE))], [S((n, d), F32), S((1, d), F32), S((8, LANE), F32)],
                 sem=("arbitrary",))(h, g, target)


def _adamw_math(w, g, m, v):
    c1 = np.float32(1.0 - ADAM_B1 ** ADAM_STEP)
    c2 = np.float32(1.0 - ADAM_B2 ** ADAM_STEP)
    mm = ADAM_B1 * m + (1.0 - ADAM_B1) * g
    vv = ADAM_B2 * v + (1.0 - ADAM_B2) * (g * g)
    return -ADAM_LR * ((mm / c1) / (jnp.sqrt(vv / c2) + ADAM_EPS) + ADAM_WD * w), mm, vv


def adamw_layer(w, pack, off, m, v, li, prev, name):
    n, r, c = w.shape

    def body(w_ref, g_ref, m_ref, v_ref, *refs):
        go_ref, d_ref, mo_ref, vo_ref = refs[-4:]
        g = g_ref[...]
        go_ref[...] = g
        d_ref[...], mo_ref[...], vo_ref[...] = _adamw_math(w_ref[...], g, m_ref[...], v_ref[...])

    if r % 8 == 0:
        tile = next(t for t in (512, 256, 128, 64, 32, 16, 8) if r % t == 0 and off % t == 0 and t * c * 4 <= 2**21)
        ob, steps = off // tile, r // tile
        blk = pl.BlockSpec((None, tile, c), lambda i: (li, i, 0))
        g_spec = pl.BlockSpec((tile, c), lambda i: (ob + i, 0))
    else:
        assert off == 0 and pack.shape[0] == r and c % (2 * LANE) == 0
        steps = c // (2 * LANE)
        blk = pl.BlockSpec((None, r, 2 * LANE), lambda i: (li, 0, i))
        g_spec = pl.BlockSpec((r, 2 * LANE), lambda i: (0, i))
    prev = list(prev) if prev is not None else []
    return pl.pallas_call(
        body, name=name, grid=(steps,),
        in_specs=[blk, g_spec, blk, blk] + [ANY] * len(prev),
        out_specs=[blk] * 4, out_shape=[S((n, r, c), F32)] * 4,
        input_output_aliases={4 + j: j for j in range(len(prev))},
        compiler_params=pltpu.CompilerParams(dimension_semantics=("parallel",), vmem_limit_bytes=VMEM_LIMIT),
    )(w, pack, m, v, *prev)


def adamw(w, g, m, v, name):
    r, c = w.shape
    tile = r
    for t in (512, 256, 128, 64, 32, 16, 8):
        if r % t == 0 and t * c * 4 <= 2**21:
            tile = t
            break

    def body(w_ref, g_ref, m_ref, v_ref, d_ref, mo_ref, vo_ref):
        d_ref[...], mo_ref[...], vo_ref[...] = _adamw_math(w_ref[...], g_ref[...], m_ref[...], v_ref[...])

    return _call(body, name, (r // tile,), [_rs(tile, c)] * 4, [_rs(tile, c)] * 3, [S((r, c), F32)] * 3,
                 sem=("parallel",))(w, g, m, v)


TAP_VREGS = 32


def _taps(src, w_ref, offsets, tile, put, bias=None):
    c = src.shape[1]
    rp = max(8, TAP_VREGS * 8 * LANE // c // 8 * 8)
    for r0 in range(0, tile, rp):
        acc = jnp.zeros((rp, c), F32) if bias is None else jnp.zeros((rp, c), F32) + bias
        for k, o in enumerate(offsets):
            acc = acc + w_ref[k:k + 1, :] * src[r0 + o:r0 + o + rp, :]
        put(slice(r0, r0 + rp), acc)


def _taps_fwd(sc, w_ref, width, halo, tile, put, bias):
    _taps(sc, w_ref, [halo - (width - 1) + k for k in range(width)], tile, put, bias)


def _taps_bwd_x(sc_d, w_ref, width, tile, put):
    _taps(sc_d, w_ref, [(width - 1) - k for k in range(width)], tile, put)


def _taps_bwd_w(dy, sc, dw_ref, width, halo, tile, i):
    @pl.when(i == 0)
    def _():
        dw_ref[...] = jnp.zeros_like(dw_ref)

    for k in range(width):
        o = halo - (width - 1) + k
        dw_ref[k:k + 1, :] += jnp.sum(dy * sc[o:o + tile, :], axis=0, keepdims=True)


def _ln_stats(x):
    mu = jnp.mean(x, axis=-1, keepdims=True)
    xc = x - mu
    r = lax.rsqrt(jnp.mean(xc * xc, axis=-1, keepdims=True) + EPS)
    return xc * r, r


def conv_group_fwd(proj, cw, cb, lg, lb, name):
    n = proj.shape[0]
    d = cw.shape[1]
    tile = _row_tile(n, 256)
    halo = 32

    def body(v_ref, g_ref, vp_ref, gp_ref, cw_ref, cb_ref, lg_ref, lb_ref, u_ref, u1_ref, sc):
        i = pl.program_id(0)
        first = (i > 0).astype(F32)
        sc[0:halo, :] = vp_ref[tile - halo:, :] * _sigmoid(gp_ref[tile - halo:, :]) * first
        sc[halo:, :] = v_ref[...] * _sigmoid(g_ref[...])
        def put(rows, acc):
            u1_ref[rows, :] = acc

        _taps_fwd(sc, cw_ref, CONV_WIDTH, halo, tile, put, cb_ref[...])
        xh, _ = _ln_stats(u1_ref[...])
        y = xh * lg_ref[...] + lb_ref[...]
        u_ref[...] = (y * _sigmoid(y)).astype(u_ref.dtype)

    return _call(body, name, (n // tile,),
                 [_rs(tile, d, 0), _rs(tile, d, 1), _rs(tile, d, 0, -1), _rs(tile, d, 1, -1),
                  _ps(cw.shape), _ps((1, d)), _ps((1, d)), _ps((1, d))],
                 [_rs(tile, d), _rs(tile, d)], [S((n, d), MXU_DTYPE), S((n, d), F32)],
                 scratch=[pltpu.VMEM((halo + tile, d), F32)], sem=("arbitrary",))(proj, proj, proj, proj, cw, cb, lg, lb)


def conv_group_bwd(du, u1, proj, cw, lg, lb, name):
    n = proj.shape[0]
    d = cw.shape[1]
    tile = _row_tile(n, 256)
    halo = 32
    nt = n // tile

    def body(du_ref, dun_ref, u1_ref, u1n_ref, v_ref, g_ref, vp_ref, gp_ref, cw_ref, lg_ref, lb_ref,
             dp_ref, dcw_ref, dcb_ref, dlg_ref, dlb_ref, sc, sc_d):
        i = pl.program_id(0)

        def ln_swish_bwd(dy_, u1_):
            xh, r = _ln_stats(u1_)
            y = xh * lg_ref[...] + lb_ref[...]
            s = _sigmoid(y)
            dyy = dy_ * s * (1.0 + y * (1.0 - s))
            dxh = dyy * lg_ref[...]
            dx = r * (dxh - jnp.mean(dxh, axis=-1, keepdims=True) - xh * jnp.mean(dxh * xh, axis=-1, keepdims=True))
            return dx, jnp.sum(dyy * xh, axis=0, keepdims=True), jnp.sum(dyy, axis=0, keepdims=True)

        du1, dlg, dlb = ln_swish_bwd(du_ref[...].astype(F32), u1_ref[...])
        du1n, _, _ = ln_swish_bwd(dun_ref[0:halo, :].astype(F32), u1n_ref[0:halo, :])
        sc_d[0:tile, :] = du1
        sc_d[tile:, :] = du1n * (i < nt - 1).astype(F32)
        sc[0:halo, :] = vp_ref[tile - halo:, :] * _sigmoid(gp_ref[tile - halo:, :]) * (i > 0).astype(F32)
        sc[halo:, :] = v_ref[...] * _sigmoid(g_ref[...])

        def put(rows, du0):
            sig = _sigmoid(g_ref[rows, :])
            dp_ref[rows, :d] = (du0 * sig).astype(dp_ref.dtype)
            dp_ref[rows, d:] = (du0 * v_ref[rows, :] * sig * (1.0 - sig)).astype(dp_ref.dtype)

        _taps_bwd_x(sc_d, cw_ref, CONV_WIDTH, tile, put)
        _taps_bwd_w(du1, sc, dcw_ref, CONV_WIDTH, halo, tile, i)
        _acc(dcb_ref, i, jnp.sum(du1, axis=0, keepdims=True))
        _acc(dlg_ref, i, dlg)
        _acc(dlb_ref, i, dlb)

    return _call(body, name, (nt,),
                 [_rs(tile, d), _rs(tile, d, 0, 1, nt), _rs(tile, d), _rs(tile, d, 0, 1, nt),
                  _rs(tile, d, 0), _rs(tile, d, 1), _rs(tile, d, 0, -1), _rs(tile, d, 1, -1),
                  _ps(cw.shape), _ps((1, d)), _ps((1, d))],
                 [_rs(tile, 2 * d), _ps(cw.shape), _ps((1, d)), _ps((1, d)), _ps((1, d))],
                 [S((n, proj.shape[1]), MXU_DTYPE), S(cw.shape, F32), S((1, d), F32), S((1, d), F32), S((1, d), F32)],
                 scratch=[pltpu.VMEM((halo + tile, d), F32), pltpu.VMEM((tile + halo, d), F32)],
                 sem=("arbitrary",))(du, du, u1, u1, proj, proj, proj, proj, cw, lg, lb)


def ssm_conv_fwd(proj, dtr, sw, sb, dtb, name):
    n = proj.shape[0]
    w = sw.shape[1]
    inner = SSM_HEADS * HEAD_DIM
    tile = _row_tile(n, 256)
    halo = 8

    def body(x_ref, xp_ref, dtr_ref, sw_ref, sb_ref, dtb_ref, pre_ref, xs_ref, bc_ref, dt_ref, sc):
        i = pl.program_id(0)
        sc[0:halo, :] = xp_ref[tile - halo:, :] * (i > 0).astype(F32)
        sc[halo:, :] = x_ref[...]
        def put(rows, acc):
            pre_ref[rows, :] = acc

        _taps_fwd(sc, sw_ref, SSM_CONV, halo, tile, put, sb_ref[...])
        pre = pre_ref[...]
        act = pre * _sigmoid(pre)
        xs_ref[...] = act[:, :inner]
        bc_ref[...] = act[:, inner:]
        dt = _softplus(dtr_ref[...] + dtb_ref[...])
        dt_ref[...] = jnp.where(_iota(dt.shape, 1) < SSM_HEADS, dt, 0.0)

    return _call(body, name, (n // tile,),
                 [_rs(tile, w, 2), _rs(tile, w, 2, -1), _rs(tile, LANE), _ps(sw.shape), _ps((1, w)), _ps((1, LANE))],
                 [_rs(tile, w), _rs(tile, inner), _rs(tile, w - inner), _rs(tile, LANE)],
                 [S((n, w), F32), S((n, inner), F32), S((n, w - inner), F32), S((n, LANE), F32)],
                 scratch=[pltpu.VMEM((halo + tile, w), F32)], sem=("arbitrary",))(proj, proj, dtr, sw, sb, dtb)


def ssm_conv_bwd(dxs, dbc, pre, proj, sw, dproj, name):
    n = proj.shape[0]
    w = sw.shape[1]
    inner = SSM_HEADS * HEAD_DIM
    tile = _row_tile(n, 256)
    halo = 8
    nt = n // tile

    def body(dxs_ref, dxsn_ref, dbc_ref, dbcn_ref, pre_ref, pren_ref, x_ref, xp_ref, sw_ref, dp_in_ref,
             dx_ref, dsw_ref, dsb_ref, sc, sc_d):
        i = pl.program_id(0)

        def silu_bwd(d_, p_):
            s = _sigmoid(p_)
            return d_ * s * (1.0 + p_ * (1.0 - s))

        sc_d[0:tile, :inner] = silu_bwd(dxs_ref[...], pre_ref[:, :inner])
        sc_d[0:tile, inner:] = silu_bwd(dbc_ref[...], pre_ref[:, inner:])
        last = (i < nt - 1).astype(F32)
        sc_d[tile:, :inner] = silu_bwd(dxsn_ref[0:halo, :], pren_ref[0:halo, :inner]) * last
        sc_d[tile:, inner:] = silu_bwd(dbcn_ref[0:halo, :], pren_ref[0:halo, inner:]) * last
        sc[0:halo, :] = xp_ref[tile - halo:, :] * (i > 0).astype(F32)
        sc[halo:, :] = x_ref[...]
        dpre = sc_d[0:tile, :]
        def put(rows, acc):
            dx_ref[rows, :] = acc.astype(dx_ref.dtype)

        _taps_bwd_x(sc_d, sw_ref, SSM_CONV, tile, put)
        _taps_bwd_w(dpre, sc, dsw_ref, SSM_CONV, halo, tile, i)
        _acc(dsb_ref, i, jnp.sum(dpre, axis=0, keepdims=True))

    return pl.pallas_call(
        body, name=name, grid=(nt,),
        in_specs=[_rs(tile, inner), _rs(tile, inner, 0, 1, nt), _rs(tile, w - inner), _rs(tile, w - inner, 0, 1, nt),
                  _rs(tile, w), _rs(tile, w, 0, 1, nt), _rs(tile, w, 2), _rs(tile, w, 2, -1), _ps(sw.shape), ANY],
        out_specs=[_rs(tile, w, 2), _ps(sw.shape), _ps((1, w))],
        out_shape=[S(dproj.shape, dproj.dtype), S(sw.shape, F32), S((1, w), F32)],
        scratch_shapes=[pltpu.VMEM((halo + tile, w), F32), pltpu.VMEM((tile + halo, w), F32)],
        input_output_aliases={9: 0},
        compiler_params=pltpu.CompilerParams(dimension_semantics=("arbitrary",), vmem_limit_bytes=VMEM_LIMIT),
    )(dxs, dxs, dbc, dbc, pre, pre, proj, proj, sw, dproj)


def _ssd_prologue(dt_ref, dtT_ref, al_ref, alc_ref):
    row = _iota((CHUNK, CHUNK), 0)
    col = _iota((CHUNK, CHUNK), 1)
    dt = dt_ref[:, :SSM_HEADS]
    a_row = -jnp.exp(al_ref[:, :SSM_HEADS])
    a_col = -jnp.exp(alc_ref[...])
    cs = _01mm((row >= col).astype(F32), dt * a_row)
    csT = _mm01(dtT_ref[...] * a_col, (row <= col).astype(F32))
    return dt, a_row, cs, csT, row, col


def _decay(cs, csT, h, row, col):
    lm = jnp.exp(jnp.where(row >= col, cs[:, h:h + 1] - csT[h:h + 1, :], -1e30))
    lmT = jnp.exp(jnp.where(col >= row, csT[h:h + 1, :] - cs[:, h:h + 1], -1e30))
    return lm, lmT


def ssd_fwd(xs, bc, dt, dtT, alog_row, alog_col, name):
    n, width = xs.shape
    nc = n // CHUNK
    gw = width // SSM_GROUPS
    hpg = SSM_HEADS // SSM_GROUPS
    ns = SSM_STATE

    def body(xs_ref, bc_ref, dt_ref, dtT_ref, al_ref, alc_ref, y_ref, hs_ref, h_sc):
        i = pl.program_id(0)

        @pl.when(i == 0)
        def _():
            h_sc[...] = jnp.zeros_like(h_sc)

        dt, a_row, cs, csT, row, col = _ssd_prologue(dt_ref, dtT_ref, al_ref, alc_ref)
        indT = _head_indicator(width, SSM_HEADS, transposed=True)
        dt_full = _mm01(dt, indT)
        e_full = jnp.exp(_mm01(cs, indT))
        dte_full = jnp.exp(_mm01(cs[CHUNK - 1:CHUNK, :] - cs, indT))
        xt = xs_ref[...] * dt_full
        hs_ref[0] = h_sc[...]
        lo = _iota((CHUNK, 2 * HEAD_DIM), 1) < HEAD_DIM
        for g in range(SSM_GROUPS):
            bg = bc_ref[:, g * ns:(g + 1) * ns]
            cg = bc_ref[:, (SSM_GROUPS + g) * ns:(SSM_GROUPS + g + 1) * ns]
            gm = _mm(cg, bg, NT)
            hg = h_sc[g * gw:(g + 1) * gw, :]
            yoff = e_full[:, g * gw:(g + 1) * gw] * _mm(cg, hg, NT)
            for pr in range(hpg // 2):
                h0 = g * hpg + 2 * pr
                c0 = h0 * HEAD_DIM
                xp = xt[:, c0:c0 + 2 * HEAD_DIM]
                m0 = gm * _decay(cs, csT, h0, row, col)[0]
                m1 = gm * _decay(cs, csT, h0 + 1, row, col)[0]
                yd = jnp.where(lo, _mm(m0, xp), _mm(m1, xp))
                y_ref[:, c0:c0 + 2 * HEAD_DIM] = yd + yoff[:, 2 * pr * HEAD_DIM:(2 * pr + 2) * HEAD_DIM]
            sg = _mm(xt[:, g * gw:(g + 1) * gw] * dte_full[:, g * gw:(g + 1) * gw], bg, TN)
            for hh in range(hpg):
                h = g * hpg + hh
                r0 = h * HEAD_DIM
                h_sc[r0:r0 + HEAD_DIM, :] = (h_sc[r0:r0 + HEAD_DIM, :] * jnp.exp(csT[h:h + 1, CHUNK - 1:CHUNK])
                                             + sg[hh * HEAD_DIM:(hh + 1) * HEAD_DIM, :])

    bcw = bc.shape[1]
    return _call(body, name, (nc,),
                 [_rs(CHUNK, width), _rs(CHUNK, bcw), _rs(CHUNK, LANE), pl.BlockSpec((SSM_HEADS, CHUNK), lambda i: (0, i)),
                  _ps((1, LANE)), _ps((SSM_HEADS, 1))],
                 [_rs(CHUNK, width), pl.BlockSpec((1, width, ns), lambda i: (i, 0, 0))],
                 [S((n, width), F32), S((nc, width, ns), F32)],
                 scratch=[pltpu.VMEM((width, ns), F32)], sem=("arbitrary",))(xs, bc, dt, dtT, alog_row, alog_col)


def ssd_bwd(xs, bc, dt, dtT, alog_row, alog_col, hs, dy, dxs_skip, name):
    n, width = xs.shape
    nc = n // CHUNK
    gw = width // SSM_GROUPS
    hpg = SSM_HEADS // SSM_GROUPS
    ns = SSM_STATE
    bcw = bc.shape[1]

    def body(xs_ref, bc_ref, dt_ref, dtT_ref, al_ref, alc_ref, hs_ref, dy_ref, skip_ref,
             dxs_ref, dbc_ref, ddtr_ref, dal_ref, ddtb_ref, dh_sc, dxt_sc):
        i = pl.program_id(0)

        @pl.when(i == 0)
        def _():
            dh_sc[...] = jnp.zeros_like(dh_sc)

        dt, a_row, cs, csT, row, col = _ssd_prologue(dt_ref, dtT_ref, al_ref, alc_ref)
        indT = _head_indicator(width, SSM_HEADS, transposed=True)
        ind = _head_indicator(width, SSM_HEADS)
        dt_full = _mm01(dt, indT)
        e_full = jnp.exp(_mm01(cs, indT))
        cs_last = cs[CHUNK - 1:CHUNK, :]
        dte = jnp.exp(cs_last - cs)
        dte_full = _mm01(dte, indT)
        xs_ = xs_ref[...]
        xt = xs_ * dt_full
        dy_ = dy_ref[...]
        hmat = hs_ref[0]
        ds = dh_sc[...]
        lo = _iota((CHUNK, 2 * HEAD_DIM), 1) < HEAD_DIM
        head_lane = _iota((1, SSM_HEADS), 1)
        dcs = jnp.zeros((CHUNK, SSM_HEADS), F32)
        ddte = jnp.zeros((CHUNK, SSM_HEADS), F32)
        for g in range(SSM_GROUPS):
            sl = slice(g * gw, (g + 1) * gw)
            bg = bc_ref[:, g * ns:(g + 1) * ns]
            cg = bc_ref[:, (SSM_GROUPS + g) * ns:(SSM_GROUPS + g + 1) * ns]
            gm = _mm(cg, bg, NT)
            gmT = _mm(bg, cg, NT)
            hg = hmat[sl, :]
            dsg = ds[sl, :]
            dyg = dy_[:, sl]
            xtg = xt[:, sl]
            yoff = e_full[:, sl] * _mm(cg, hg, NT)
            edy = e_full[:, sl] * dyg
            bds = _mm(bg, dsg, NT)
            dxt_g = dte_full[:, sl] * bds
            ddte = ddte + _mm01(xtg * bds, ind[sl, :])
            dcs = dcs + _mm01(dyg * yoff, ind[sl, :])
            db = _mm(xtg * dte_full[:, sl], dsg)
            dc = _mm(edy, hg)
            dhc = _mm(edy, cg, TN)
            dgs = jnp.zeros((CHUNK, CHUNK), F32)
            dgTs = jnp.zeros((CHUNK, CHUNK), F32)
            for pr in range(hpg // 2):
                h0 = g * hpg + 2 * pr
                c0 = 2 * pr * HEAD_DIM
                xp = xtg[:, c0:c0 + 2 * HEAD_DIM]
                dyp = dyg[:, c0:c0 + 2 * HEAD_DIM]
                rr = []
                for h, half in ((h0, lo), (h0 + 1, jnp.logical_not(lo))):
                    lm, lmT = _decay(cs, csT, h, row, col)
                    xm = jnp.where(half, xp, 0.0)
                    dm = _mm(dyp, xm, NT)
                    dmT = _mm(xm, dyp, NT)
                    mT = gmT * lmT
                    z = jnp.sum(dm * (gm * lm), axis=1, keepdims=True) - jnp.sum(dmT * mT, axis=1, keepdims=True)
                    dcs = dcs + z * (head_lane == h).astype(F32)
                    dgs = dgs + dm * lm
                    dgTs = dgTs + dmT * lmT
                    rr.append(_mm(mT, dyp))
                dxt_sc[:, g * gw + c0:g * gw + c0 + 2 * HEAD_DIM] = jnp.where(lo, rr[0], rr[1]) + dxt_g[:, c0:c0 + 2 * HEAD_DIM]
            dbc_ref[:, g * ns:(g + 1) * ns] = db + _mm(dgTs, cg)
            dbc_ref[:, (SSM_GROUPS + g) * ns:(SSM_GROUPS + g + 1) * ns] = dc + _mm(dgs, bg)
            for hh in range(hpg):
                h = g * hpg + hh
                r0 = h * HEAD_DIM
                dh_sc[r0:r0 + HEAD_DIM, :] = (dhc[hh * HEAD_DIM:(hh + 1) * HEAD_DIM, :]
                                              + jnp.exp(csT[h:h + 1, CHUNK - 1:CHUNK]) * ds[r0:r0 + HEAD_DIM, :])
        t = ddte * dte
        per_head = jnp.sum(jnp.sum(ds * hmat, axis=1, keepdims=True) * ind, axis=0, keepdims=True)
        last_add = jnp.sum(t, axis=0, keepdims=True) + jnp.exp(cs_last) * per_head
        dcs = dcs - t + jnp.where(_iota((CHUNK, SSM_HEADS), 0) == CHUNK - 1, last_add, 0.0)
        dadt = _01mm((row <= col).astype(F32), dcs)
        dxt = dxt_sc[...]
        ddt = dadt * a_row + _mm01(dxt * xs_, ind)
        dxs_ref[...] = dxt * dt_full + skip_ref[...]
        ddtr = ddt * (1.0 - jnp.exp(-dt))
        ddtr_ref[...] = jnp.zeros_like(ddtr_ref)
        ddtr_ref[:, :SSM_HEADS] = ddtr.astype(ddtr_ref.dtype)
        _acc(dal_ref, i, jnp.sum(dadt * dt, axis=0, keepdims=True) * a_row)
        _acc(ddtb_ref, i, jnp.sum(ddtr, axis=0, keepdims=True))

    rev = lambda i: (nc - 1 - i, 0)
    return _call(body, name, (nc,),
                 [pl.BlockSpec((CHUNK, width), rev), pl.BlockSpec((CHUNK, bcw), rev), pl.BlockSpec((CHUNK, LANE), rev),
                  pl.BlockSpec((SSM_HEADS, CHUNK), lambda i: (0, nc - 1 - i)), _ps((1, LANE)), _ps((SSM_HEADS, 1)),
                  pl.BlockSpec((1, width, ns), lambda i: (nc - 1 - i, 0, 0)), pl.BlockSpec((CHUNK, width), rev),
                  pl.BlockSpec((CHUNK, width), rev)],
                 [pl.BlockSpec((CHUNK, width), rev), pl.BlockSpec((CHUNK, bcw), rev), pl.BlockSpec((CHUNK, LANE), rev),
                  _ps((1, SSM_HEADS)), _ps((1, SSM_HEADS))],
                 [S((n, width), F32), S((n, bcw), F32), S((n, LANE), MXU_DTYPE), S((1, SSM_HEADS), F32), S((1, SSM_HEADS), F32)],
                 scratch=[pltpu.VMEM((width, ns), F32), pltpu.VMEM((CHUNK, width), F32)],
                 sem=("arbitrary",))(xs, bc, dt, dtT, alog_row, alog_col, hs, dy, dxs_skip)


def ssm_gate_fwd(yssd, xs, proj, dfull, gamma, name):
    n, d = yssd.shape
    tile = _row_tile(n, 256)
    gw = d // SSM_GROUPS

    def body(y_ref, xs_ref, z_ref, df_ref, gm_ref, o_ref):
        z = z_ref[...]
        y2 = (y_ref[...] + df_ref[...] * xs_ref[...]) * (z * _sigmoid(z))
        for g in range(SSM_GROUPS):
            yg = y2[:, g * gw:(g + 1) * gw]
            r = lax.rsqrt(jnp.mean(yg * yg, axis=-1, keepdims=True) + EPS)
            o_ref[:, g * gw:(g + 1) * gw] = (yg * r * gm_ref[:, g * gw:(g + 1) * gw]).astype(o_ref.dtype)

    return _call(body, name, (n // tile,), [_rs(tile, d), _rs(tile, d), _rs(tile, d, 2), _ps((1, d)), _ps((1, d))],
                 _rs(tile, d), S((n, d), MXU_DTYPE), sem=("parallel",))(yssd, xs, proj, dfull, gamma)


def ssm_gate_bwd(dy3, yssd, xs, proj, dfull, gamma, dproj, name):
    n, d = yssd.shape
    tile = _row_tile(n, 256)
    gw = d // SSM_GROUPS

    def body(dy_ref, y_ref, xs_ref, z_ref, df_ref, gm_ref, dp_in_ref, dys_ref, dxs_ref, dz_ref, dgm_ref, dd_ref):
        i = pl.program_id(0)
        z = z_ref[...]
        s = _sigmoid(z)
        xs_ = xs_ref[...]
        y1 = y_ref[...] + df_ref[...] * xs_
        y2 = y1 * (z * s)
        dy_ = dy_ref[...].astype(F32)
        dgm = []
        dy2 = []
        for g in range(SSM_GROUPS):
            sl = slice(g * gw, (g + 1) * gw)
            dxg, dgg = _rms_bwd_math(y2[:, sl], gm_ref[:, sl], dy_[:, sl])
            dy2.append(dxg)
            dgm.append(dgg)
        dy2 = jnp.concatenate(dy2, axis=1)
        dy1 = dy2 * (z * s)
        dys_ref[...] = dy1
        dxs_ref[...] = dy1 * df_ref[...]
        dz_ref[...] = (dy2 * y1 * s * (1.0 + z * (1.0 - s))).astype(dz_ref.dtype)
        _acc(dgm_ref, i, jnp.concatenate(dgm, axis=1))
        colsum = jnp.broadcast_to(jnp.sum(dy1 * xs_, axis=0, keepdims=True), (8, d))
        _acc(dd_ref, i, _mm01(colsum, _head_indicator(d, SSM_HEADS))[0:1, :])

    return pl.pallas_call(
        body, name=name, grid=(n // tile,),
        in_specs=[_rs(tile, d), _rs(tile, d), _rs(tile, d), _rs(tile, d, 2), _ps((1, d)), _ps((1, d)), ANY],
        out_specs=[_rs(tile, d), _rs(tile, d), _rs(tile, d, 2), _ps((1, d)), _ps((1, SSM_HEADS))],
        out_shape=[S((n, d), F32), S((n, d), F32), S(dproj.shape, dproj.dtype), S((1, d), F32), S((1, SSM_HEADS), F32)],
        input_output_aliases={6: 2},
        compiler_params=pltpu.CompilerParams(dimension_semantics=("arbitrary",), vmem_limit_bytes=VMEM_LIMIT),
    )(dy3, yssd, xs, proj, dfull, gamma, dproj)


def _rope128(x, cos, sin_signed):
    half = HEAD_DIM // 2
    lane = _iota(x.shape, 1)
    partner = jnp.where((lane % HEAD_DIM) < half, pltpu.roll(x, LANE - half, 1), pltpu.roll(x, half, 1))
    return x * cos + partner * sin_signed


def rope_fwd(qkv, cos, sin, name):
    n, w = qkv.shape
    qw = ATT_HEADS * HEAD_DIM
    kw = ATT_KV_HEADS * HEAD_DIM
    tile = _row_tile(n, 256)

    def body(x_ref, c_ref, s_ref, q_ref, k_ref, v_ref):
        c, s = c_ref[...], s_ref[...]
        for j in range(qw // LANE):
            q_ref[:, j * LANE:(j + 1) * LANE] = _rope128(x_ref[:, j * LANE:(j + 1) * LANE], c, s).astype(q_ref.dtype)
        for j in range(kw // LANE):
            k_ref[:, j * LANE:(j + 1) * LANE] = _rope128(x_ref[:, qw + j * LANE:qw + (j + 1) * LANE], c, s).astype(k_ref.dtype)
        v_ref[...] = x_ref[:, qw + kw:].astype(v_ref.dtype)

    return _call(body, name, (n // tile,), [_rs(tile, w), _rs(tile, LANE), _rs(tile, LANE)],
                 [_rs(tile, qw), _rs(tile, kw), _rs(tile, kw)],
                 [S((n, qw), MXU_DTYPE), S((n, kw), MXU_DTYPE), S((n, kw), MXU_DTYPE)], sem=("parallel",))(qkv, cos, sin)


ATT_GROUP = ATT_HEADS // ATT_KV_HEADS


def _attn_mask(i):
    row = _iota((ATT_GROUP * WINDOW, 2 * WINDOW), 0) % WINDOW
    s = _iota((ATT_GROUP * WINDOW, 2 * WINDOW), 1)
    return (s > row) & (s <= row + WINDOW) & ((s >= WINDOW) | (i > 0))


def _stack_heads(ref, j, kh, lo):
    parts = []
    for t in range(ATT_GROUP):
        h = ATT_GROUP * j + t
        blk = ref[:, (h // 2) * LANE:(h // 2 + 1) * LANE]
        blk = jnp.where(lo if h % 2 == 0 else jnp.logical_not(lo), blk, jnp.zeros_like(blk))
        parts.append(blk if h % 2 == kh else pltpu.roll(blk, HEAD_DIM, 1))
    return jnp.concatenate(parts, axis=0)


def _unstack_heads(stacked, j, kh, lo, put):
    for t in range(0, ATT_GROUP, 2):
        h = ATT_GROUP * j + t
        even = stacked[t * WINDOW:(t + 1) * WINDOW, :]
        odd = stacked[(t + 1) * WINDOW:(t + 2) * WINDOW, :]
        even = even if kh == 0 else pltpu.roll(even, HEAD_DIM, 1)
        odd = odd if kh == 1 else pltpu.roll(odd, HEAD_DIM, 1)
        put(h // 2, jnp.where(lo, even, odd))


def _per_head_rows(ref, j):
    return jnp.concatenate([ref[:, ATT_GROUP * j + t:ATT_GROUP * j + t + 1] for t in range(ATT_GROUP)], axis=0)


def _per_head_scalar(ref, j):
    rows = _iota((ATT_GROUP * WINDOW, 1), 0) // WINDOW
    out = jnp.zeros((ATT_GROUP * WINDOW, 1), F32)
    for t in range(ATT_GROUP):
        out = out + jnp.where(rows == t, ref[:, ATT_GROUP * j + t:ATT_GROUP * j + t + 1], 0.0)
    return out


def attn_fwd(q, k, v, sinks, name):
    n, qw = q.shape
    kw = k.shape[1]
    nb = n // WINDOW
    scale = HEAD_DIM ** -0.5

    def body(q_ref, kc_ref, kp_ref, vc_ref, vp_ref, sk_ref, o_ref, lse_ref):
        i = pl.program_id(0)
        valid = _attn_mask(i)
        lo = _iota((WINDOW, LANE), 1) < HEAD_DIM
        k2 = jnp.concatenate([kp_ref[...], kc_ref[...]], axis=0)
        v2 = jnp.concatenate([vp_ref[...], vc_ref[...]], axis=0)
        lane1 = _iota((1, LANE), 1)
        lse = jnp.zeros((WINDOW, LANE), F32)

        def put_o(qb, val):
            o_ref[:, qb * LANE:(qb + 1) * LANE] = val.astype(o_ref.dtype)

        for j in range(ATT_KV_HEADS):
            kb, kh = j // 2, j % 2
            q4 = _stack_heads(q_ref, j, kh, lo)
            logits = jnp.where(valid, _mm(q4, k2[:, kb * LANE:(kb + 1) * LANE], NT) * scale, -1e30)
            sk = _per_head_scalar(sk_ref, j)
            m = jnp.maximum(jnp.max(logits, axis=-1, keepdims=True), sk)
            e = jnp.exp(logits - m)
            den = jnp.sum(e, axis=-1, keepdims=True) + jnp.exp(sk - m)
            lse4 = m + jnp.log(den)
            for t in range(ATT_GROUP):
                lse = lse + lse4[t * WINDOW:(t + 1) * WINDOW, :] * (lane1 == ATT_GROUP * j + t).astype(F32)
            _unstack_heads(_mm(e * (1.0 / den), v2[:, kb * LANE:(kb + 1) * LANE]), j, kh, lo, put_o)
        lse_ref[...] = lse

    return _call(body, name, (nb,),
                 [_rs(WINDOW, qw), _rs(WINDOW, kw), _rs(WINDOW, kw, 0, -1), _rs(WINDOW, kw), _rs(WINDOW, kw, 0, -1), _ps((1, LANE))],
                 [_rs(WINDOW, qw), _rs(WINDOW, LANE)], [S((n, qw), MXU_DTYPE), S((n, LANE), F32)],
                 sem=("parallel",))(q, k, k, v, v, sinks)


def attn_bwd(q, k, v, o, do, lse, sinks, name):
    n, qw = q.shape
    kw = k.shape[1]
    nb = n // WINDOW
    scale = HEAD_DIM ** -0.5

    def body(q_ref, kc_ref, kp_ref, vc_ref, vp_ref, o_ref, do_ref, lse_ref, sk_ref,
             dq_ref, dka_ref, dkb_ref, dva_ref, dvb_ref, dsk_ref):
        i = pl.program_id(0)
        valid = _attn_mask(i)
        lo = _iota((WINDOW, LANE), 1) < HEAD_DIM
        k2 = jnp.concatenate([kp_ref[...], kc_ref[...]], axis=0)
        v2 = jnp.concatenate([vp_ref[...], vc_ref[...]], axis=0)
        lane1 = _iota((1, LANE), 1)
        do_ = do_ref[...].astype(F32)
        delta = _mm01(do_ * o_ref[...].astype(F32), _head_indicator(qw, ATT_HEADS))
        dk2 = [jnp.zeros((2 * WINDOW, LANE), F32) for _ in range(kw // LANE)]
        dv2 = [jnp.zeros((2 * WINDOW, LANE), F32) for _ in range(kw // LANE)]
        dsk = jnp.zeros((1, LANE), F32)

        def put_dq(qb, val):
            dq_ref[:, qb * LANE:(qb + 1) * LANE] = val

        for j in range(ATT_KV_HEADS):
            kb, kh = j // 2, j % 2
            q4 = _stack_heads(q_ref, j, kh, lo)
            do4 = _stack_heads(do_ref, j, kh, lo)
            kk = k2[:, kb * LANE:(kb + 1) * LANE]
            vv = v2[:, kb * LANE:(kb + 1) * LANE]
            logits = jnp.where(valid, _mm(q4, kk, NT) * scale, -1e30)
            lse4 = _per_head_rows(lse_ref, j)
            p = jnp.exp(logits - lse4)
            dl = jnp.concatenate([delta[:, ATT_GROUP * j + t:ATT_GROUP * j + t + 1] for t in range(ATT_GROUP)], axis=0)
            ds = p * (_mm(do4, vv, NT) - dl) * scale
            sd = jnp.exp(_per_head_scalar(sk_ref, j) - lse4) * dl
            for t in range(ATT_GROUP):
                dsk = dsk - (jnp.sum(sd[t * WINDOW:(t + 1) * WINDOW, :], axis=0, keepdims=True)
                             * (lane1 == ATT_GROUP * j + t).astype(F32))
            _unstack_heads(_mm(ds, kk), j, kh, lo, put_dq)
            dk2[kb] = dk2[kb] + _mm(ds, q4, TN)
            dv2[kb] = dv2[kb] + _mm(p, do4, TN)
        for kb in range(kw // LANE):
            dkb_ref[:, kb * LANE:(kb + 1) * LANE] = dk2[kb][0:WINDOW, :]
            dka_ref[:, kb * LANE:(kb + 1) * LANE] = dk2[kb][WINDOW:, :]
            dvb_ref[:, kb * LANE:(kb + 1) * LANE] = dv2[kb][0:WINDOW, :]
            dva_ref[:, kb * LANE:(kb + 1) * LANE] = dv2[kb][WINDOW:, :]
        _acc(dsk_ref, i, dsk)

    return _call(body, name, (nb,),
                 [_rs(WINDOW, qw), _rs(WINDOW, kw), _rs(WINDOW, kw, 0, -1), _rs(WINDOW, kw), _rs(WINDOW, kw, 0, -1),
                  _rs(WINDOW, qw), _rs(WINDOW, qw), _rs(WINDOW, LANE), _ps((1, LANE))],
                 [_rs(WINDOW, qw)] + [_rs(WINDOW, kw)] * 4 + [_ps((1, LANE))],
                 [S((n, qw), F32)] + [S((n, kw), F32)] * 4 + [S((1, LANE), F32)],
                 sem=("arbitrary",))(q, k, k, v, v, o, do, lse, sinks)


def attn_grad_merge(dq, dka, dkb, dva, dvb, cos, sin, name):
    n, qw = dq.shape
    kw = dka.shape[1]
    nb = n // WINDOW
    w = qw + 2 * kw

    def body(dq_ref, dka_ref, dkb_ref, dva_ref, dvb_ref, c_ref, s_ref, o_ref, db_ref):
        i = pl.program_id(0)
        c, s = c_ref[...], -s_ref[...]
        nxt = (i < nb - 1).astype(F32)

        @pl.when(i == 0)
        def _():
            db_ref[...] = jnp.zeros_like(db_ref)

        def put(c0, val):
            o_ref[:, c0:c0 + val.shape[1]] = val.astype(o_ref.dtype)
            db_ref[:, c0:c0 + val.shape[1]] += jnp.sum(val, axis=0, keepdims=True)

        for j in range(qw // LANE):
            put(j * LANE, _rope128(dq_ref[:, j * LANE:(j + 1) * LANE], c, s))
        for j in range(kw // LANE):
            sl = slice(j * LANE, (j + 1) * LANE)
            put(qw + j * LANE, _rope128(dka_ref[:, sl] + dkb_ref[:, sl] * nxt, c, s))
        put(qw + kw, dva_ref[...] + dvb_ref[...] * nxt)

    return _call(body, name, (nb,),
                 [_rs(WINDOW, qw), _rs(WINDOW, kw), _rs(WINDOW, kw, 0, 1, nb), _rs(WINDOW, kw), _rs(WINDOW, kw, 0, 1, nb),
                  _rs(WINDOW, LANE), _rs(WINDOW, LANE)],
                 [_rs(WINDOW, w), _ps((1, w))], [S((n, w), MXU_DTYPE), S((1, w), F32)],
                 sem=("arbitrary",))(dq, dka, dkb, dva, dvb, cos, sin)


def _row(v):
    return v.reshape(1, -1)


def _pad_lanes(v, width=LANE):
    return jnp.pad(v.reshape(1, -1), ((0, 0), (0, width - v.size)))


class LayerWeights(dict):
    def __init__(self, small, fetch):
        super().__init__(small)
        self.fetch = fetch

    def need(self, k, after):
        if k not in self:
            self[k] = self.fetch(k, after)
        return self[k]


def ffn_fwd(h, g, w, keys, tag):
    xn, u, hm = swiglu_in(h, _row(g), w.need(keys[0], h), f"{tag}_in")
    return matmul(hm, w.need(keys[1], hm), "nn", f"{tag}_out", scale=0.5, res=h), (h, xn, u, hm)


class GradSink:
    ORDER = ("ffn1_w_out", "ffn2_w_out", "ple_gate_w", "att_w_o", "hyb_w_out", "ffn1_w_in", "ffn2_w_in", "att_w_qkv",
             "ple_proj_w", "hyb_w_in")

    def __init__(self, shard_shapes, bucket_of):
        self.where, rows = {}, {}
        for k in self.ORDER:
            n, r, c = shard_shapes[k]
            for li in range(n):
                layer = li if k in PER_LAYER else 2 * li + (0 if k in EVEN_ONLY else 1)
                rows_b = rows.setdefault(bucket_of(layer, _stage(k)), {})
                key = c if r % 32 == 0 else k
                off = -(-rows_b.get(key, (0, c))[0] // r) * r
                rows_b[key] = (-(-(off + r) // 32) * 32, c)
                self.where[k, li] = (bucket_of(layer, _stage(k)), key, "r" if _shard_axis(k) == 1 else "c", off, r)
        self.bufs = {b: {key: lax.empty((N_CHIPS, r, c), MXU_DTYPE) for key, (r, c) in rows_b.items()}
                     for b, rows_b in rows.items()}

    def mm(self, k, li, a, b, name, scale=None, c0=0, paired=False):
        bucket, key, kind, off, r = self.where[k, li]
        buf = self.bufs[bucket][key]
        slot = Slot(buf, kind, r if kind == "r" else buf.shape[2], off, c0, paired)
        self.bufs[bucket][key] = matmul(a, b, "tn", name, scale=scale, into=slot)

    def put(self, k, li, chip_major):
        b, key = self.where[k, li][:2]
        pad = self.bufs[b][key].shape[1] - chip_major.shape[1]
        self.bufs[b][key] = jnp.pad(chip_major.astype(self.bufs[b][key].dtype), ((0, 0), (0, pad), (0, 0)))


def ffn_bwd(dh, g, w_in, w_out, saved, tag, sink, keys, layer, after, colsum=False):
    h, xn, u, hm = saved
    sink.mm(keys[1], layer, hm, dh, f"{tag}_dwout", scale=0.5)
    du = swiglu_out_bwd(dh, w_out, u, after, f"{tag}_dhm")
    sink.mm(keys[0], layer, xn, du, f"{tag}_dwin", paired=True)
    outs = nt_rms_bwd(du, ColSharded(w_in.arr, paired=True), h, _row(g), dh, f"{tag}_dxn", colsum=colsum)
    return (outs[0], outs[1].reshape(-1)) + ((outs[2],) if colsum else ())


def _hyb_params(w):
    d = w["conv_dw_b"].size
    inner = SSM_HEADS * HEAD_DIM
    main = 3 * d + w["ssm_conv_b"].size
    return dict(
        w_main=w["hyb_w_in"][:main], w_dt=jnp.pad(w["hyb_w_in"][main:], ((0, LANE - SSM_HEADS), (0, 0))),
        cw=jnp.pad(w["conv_dw_w"], ((0, 32 - CONV_WIDTH), (0, 0))), cb=_row(w["conv_dw_b"]),
        lg=_row(w["conv_ln_g"]), lb=_row(w["conv_ln_b"]),
        sw=jnp.pad(w["ssm_conv_w"], ((0, 8 - SSM_CONV), (0, 0))), sb=_row(w["ssm_conv_b"]),
        dtb=_pad_lanes(w["ssm_dt_bias"]), al_row=_pad_lanes(w["ssm_a_log"]), al_col=w["ssm_a_log"].reshape(-1, 1),
        dfull=_row(jnp.repeat(w["ssm_d"], HEAD_DIM)), gamma=_row(w["ssm_norm"]), d=d, inner=inner, main=main)


def hyb_fwd(h, w, tag):
    w.need("hyb_w_in", h)
    q = _hyb_params(w)
    proj, xn = rms_matmul(h, _row(w["norm_mix"]), q["w_main"], "nt", f"{tag}_in")
    dtr = matmul(xn, q["w_dt"], "nt", f"{tag}_in_dt")
    u, u1 = conv_group_fwd(proj, q["cw"], q["cb"], q["lg"], q["lb"], f"{tag}_conv")
    pre, xs, bc, dt = ssm_conv_fwd(proj, dtr, q["sw"], q["sb"], q["dtb"], f"{tag}_sconv")
    dtT = dt[:, :SSM_HEADS].T
    yssd, hs = ssd_fwd(xs, bc, dt, dtT, q["al_row"], q["al_col"], f"{tag}_ssd")
    y = ssm_gate_fwd(yssd, xs, proj, q["dfull"], q["gamma"], f"{tag}_gate")
    wo = w.need("hyb_w_out", u)
    h2 = matmul(u, wo[:q["d"]], "nn", f"{tag}_out_a", res=h)
    h2 = matmul(y, wo[q["d"]:], "nn", f"{tag}_out_b", res=h2)
    return h2, (h, xn, proj, u, u1, pre, xs, bc, dt, dtT, yssd, hs, y)


def hyb_bwd(dh, w, saved, tag, sink, layer):
    q = _hyb_params(w)
    h, xn, proj, u, u1, pre, xs, bc, dt, dtT, yssd, hs, y = saved
    du = matmul(dh, w["hyb_w_out"][:q["d"]], "nt", f"{tag}_du")
    dy3 = matmul(dh, w["hyb_w_out"][q["d"]:], "nt", f"{tag}_dy")
    sink.mm("hyb_w_out", layer, u, dh, f"{tag}_dwo_a", c0=0)
    sink.mm("hyb_w_out", layer, y, dh, f"{tag}_dwo_b", c0=N_CHIPS // 2)
    dproj, dcw, dcb, dlg, dlb = conv_group_bwd(du, u1, proj, q["cw"], q["lg"], q["lb"], f"{tag}_dconv")
    dyssd, dxs_skip, dproj, dgamma, dd = ssm_gate_bwd(dy3, yssd, xs, proj, q["dfull"], q["gamma"], dproj, f"{tag}_dgate")
    dxs, dbc, ddtr, dalog, ddtb = ssd_bwd(xs, bc, dt, dtT, q["al_row"], q["al_col"], hs, dyssd, dxs_skip, f"{tag}_dssd")
    dproj, dsw, dsb = ssm_conv_bwd(dxs, dbc, pre, proj, q["sw"], dproj, f"{tag}_dsconv")
    dw_in = jnp.concatenate([matmul(dproj, xn, "tn", f"{tag}_dwin"),
                             matmul(ddtr, xn, "tn", f"{tag}_dwin_dt")[:SSM_HEADS]], axis=0)
    sink.put("hyb_w_in", layer, dw_in.reshape((N_CHIPS, -1) + dw_in.shape[1:]))
    dh2, dg = nt_rms_bwd(dproj, q["w_main"], h, _row(w["norm_mix"]), dh, f"{tag}_dxn", extra=(ddtr, q["w_dt"]), b_kd=True)
    grads = dict(norm_mix=dg.reshape(-1), conv_dw_w=dcw[:CONV_WIDTH], conv_dw_b=dcb.reshape(-1),
                 conv_ln_g=dlg.reshape(-1), conv_ln_b=dlb.reshape(-1), ssm_conv_w=dsw[:SSM_CONV], ssm_conv_b=dsb.reshape(-1),
                 ssm_dt_bias=ddtb.reshape(-1), ssm_a_log=dalog.reshape(-1), ssm_d=dd.reshape(-1), ssm_norm=dgamma.reshape(-1))
    return dh2, grads


def rope_tables(n):
    half = HEAD_DIM // 2
    inv = ROPE_THETA ** (-jnp.arange(0, HEAD_DIM, 2, dtype=F32) / HEAD_DIM)
    ang = jnp.arange(n, dtype=F32)[:, None] * inv[None, :]
    cos, sin = jnp.cos(ang), jnp.sin(ang)
    reps = LANE // HEAD_DIM
    return jnp.tile(jnp.concatenate([cos, cos], axis=1), (1, reps)), jnp.tile(jnp.concatenate([-sin, sin], axis=1), (1, reps))


def att_fwd(h, w, tables, tag):
    cos, sin = tables
    qkv, xn = rms_matmul(h, _row(w["norm_mix"]), w.need("att_w_qkv", h), "nn", f"{tag}_qkv", bias=_row(w["att_b_qkv"]))
    q, k, v = rope_fwd(qkv, cos, sin, f"{tag}_rope")
    sinks = _pad_lanes(w["att_sinks"])
    o, lse = attn_fwd(q, k, v, sinks, f"{tag}_attn")
    h2 = matmul(o, w.need("att_w_o", o), "nn", f"{tag}_o", bias=_row(w["att_b_o"]), res=h)
    return h2, (h, xn, q, k, v, o, lse, sinks)


def att_bwd(dh, dh_colsum, w, saved, tables, tag, sink, layer):
    cos, sin = tables
    h, xn, q, k, v, o, lse, sinks = saved
    do = matmul(dh, w["att_w_o"], "nt", f"{tag}_do")
    sink.mm("att_w_o", layer, o, dh, f"{tag}_dwo")
    dq, dka, dkb, dva, dvb, dsk = attn_bwd(q, k, v, o, do, lse, sinks, f"{tag}_dattn")
    dqkv, dbqkv = attn_grad_merge(dq, dka, dkb, dva, dvb, cos, sin, f"{tag}_drope")
    sink.mm("att_w_qkv", layer, xn, dqkv, f"{tag}_dwqkv")
    dh2, dg = nt_rms_bwd(dqkv, w["att_w_qkv"], h, _row(w["norm_mix"]), dh, f"{tag}_dxn")
    grads = dict(norm_mix=dg.reshape(-1), att_b_qkv=dbqkv.reshape(-1), att_sinks=dsk[0, :ATT_HEADS],
                 att_b_o=dh_colsum.reshape(-1))
    return dh2, grads


def ple_block_fwd(h, pe, w, tag):
    gl, xn = rms_matmul(h, _row(w["ple_norm"]), w.need("ple_gate_w", h), "nn", f"{tag}_gate")
    pp = matmul(pe, w.need("ple_proj_w", h), "nn", f"{tag}_proj")
    return ple_fwd(h, gl, pp, f"{tag}_mix"), (h, xn, gl, pp, pe)


def ple_block_bwd(dh, w, saved, tag, sink, layer, after):
    h, xn, gl, pp, pe = saved
    dpp, dgl = ple_bwd(dh, gl, pp, after, f"{tag}_dmix")
    sink.mm("ple_proj_w", layer, pe, dpp, f"{tag}_dwp")
    sink.mm("ple_gate_w", layer, xn, dgl, f"{tag}_dwg")
    dh2, dg = nt_rms_bwd(dgl, w["ple_gate_w"], h, _row(w["ple_norm"]), dh, f"{tag}_dxn")
    return dh2, dict(ple_norm=dg.reshape(-1))


PER_LAYER = ("norm_ffn1", "ffn1_w_in", "ffn1_w_out", "norm_mix", "norm_ffn2", "ffn2_w_in", "ffn2_w_out",
             "ple_norm", "ple_gate_w", "ple_proj_w")
EVEN_ONLY = ("hyb_w_in", "conv_dw_w", "conv_dw_b", "conv_ln_g", "conv_ln_b", "ssm_conv_w", "ssm_conv_b",
             "ssm_dt_bias", "ssm_a_log", "ssm_d", "ssm_norm", "hyb_w_out")
ODD_ONLY = ("att_w_qkv", "att_b_qkv", "att_sinks", "att_w_o", "att_b_o")


def _layer_index(k, i):
    if k in PER_LAYER:
        return i
    if k in (EVEN_ONLY if i % 2 == 0 else ODD_ONLY):
        return i // 2
    return None


def _stage(k):
    return 0 if k.startswith("ffn1") else (2 if k.startswith(("ffn2", "ple")) else 1)


def trunk_fwd_bwd(x, pe, target, layers, final_norm, sink, stage_done):
    depth = len(layers)
    tables = rope_tables(x.shape[0])
    h = x
    saved = []
    for i, w in enumerate(layers):
        h, s1 = ffn_fwd(h, w["norm_ffn1"], w, ("ffn1_w_in", "ffn1_w_out"), f"l{i}_ffn1")
        if i % 2 == 0:
            h, s2 = hyb_fwd(h, w, f"l{i}_hyb")
        else:
            h, s2 = att_fwd(h, w, tables, f"l{i}_att")
        h, s3 = ffn_fwd(h, w["norm_ffn2"], w, ("ffn2_w_in", "ffn2_w_out"), f"l{i}_ffn2")
        h, s4 = ple_block_fwd(h, pe[i], w, f"l{i}_ple")
        saved.append((s1, s2, s3, s4))
    dh, dgf, loss = loss_head(h, _row(final_norm), target, "loss_head")
    grads = {}
    tie = dgf
    for i in reversed(range(depth)):
        w = layers[i]
        s1, s2, s3, s4 = saved[i]
        dh, g = ple_block_bwd(dh, w, s4, f"l{i}_ple", sink, i, tie)
        odd = i % 2 == 1
        out = ffn_bwd(dh, w["norm_ffn2"], w["ffn2_w_in"], w["ffn2_w_out"], s3, f"l{i}_ffn2", sink,
                      ("ffn2_w_in", "ffn2_w_out"), i, tie, colsum=odd)
        dh = out[0]
        g.update(norm_ffn2=out[1])
        if odd:
            dh, gm = att_bwd(dh, out[2], w, s2, tables, f"l{i}_att", sink, i // 2)
        else:
            dh, gm = hyb_bwd(dh, w, s2, f"l{i}_hyb", sink, i // 2)
        g.update(gm)
        tie = stage_done(i, 1, tie)
        out = ffn_bwd(dh, w["norm_ffn1"], w["ffn1_w_in"], w["ffn1_w_out"], s1, f"l{i}_ffn1", sink,
                      ("ffn1_w_in", "ffn1_w_out"), i, tie)
        dh = out[0]
        g.update(norm_ffn1=out[1])
        tie = stage_done(i, 0, tie)
        for k, v in g.items():
            grads.setdefault(k, []).insert(0, v)
    grads = {k: jnp.stack(v) for k, v in grads.items()}
    grads["final_norm"] = dgf.reshape(-1)
    return loss, dh, grads


def _me():
    return lax.axis_index("x"), lax.axis_index("y"), lax.axis_index("c")


def _flip(v, f):
    return 1 - v if f else v


def _remote(src, dst, send_sems, recv_sems, k, dev):
    return pltpu.make_async_remote_copy(src_ref=src, dst_ref=dst, send_sem=send_sems.at[k], recv_sem=recv_sems.at[k],
                                        device_id=dev, device_id_type=MESH)


CHIP_FLIPS = ((1, 0), (0, 1), (1, 1))
DEV_FLIPS = tuple((fx, fy, fc) for fx in (0, 1) for fy in (0, 1) for fc in (0, 1))[1:]


def all_gather_chips(xs, name):
    na = len(xs)
    halves = [x.shape[0] // 2 for x in xs]
    assert all(x.shape[0] % 2 == 0 for x in xs)

    def body(*refs):
        x_refs, out_refs = refs[:na], refs[na:2 * na]
        send_sems, recv_sems = refs[2 * na:]
        mx, my, mc = _me()
        chip = 2 * mx + my
        sib = (mx, my, 1 - mc)
        peers = [(_flip(mx, fx), _flip(my, fy)) for fx, fy in CHIP_FLIPS]

        def rows(a, ch, hc):
            return out_refs[a].at[ch, pl.ds(hc * halves[a], halves[a]), :]

        def src(a):
            return x_refs[a].at[pl.ds(mc * halves[a], halves[a]), :]

        first = [_remote(src(a), rows(a, chip, mc), send_sems, recv_sems, 6 * a + j, (px, py, mc))
                 for j, (px, py) in enumerate(peers) for a in range(na)]
        for cp in first:
            cp.start()
        passed = []
        for j, (px, py) in enumerate(peers):
            for a in range(na):
                landed = rows(a, 2 * px + py, mc)
                _remote(src(a), landed, send_sems, recv_sems, 6 * a + j, (px, py, mc)).wait_recv()
                fw = _remote(landed, landed, send_sems, recv_sems, 6 * a + 3 + j, sib)
                fw.start()
                passed.append(fw)
        for j, (px, py) in enumerate(peers):
            for a in range(na):
                _remote(src(a), rows(a, 2 * px + py, 1 - mc), send_sems, recv_sems, 6 * a + 3 + j, sib).wait_recv()
        for cp in first + passed:
            cp.wait_send()

    outs = pl.pallas_call(
        body, name=name, out_shape=[S((N_CHIPS,) + x.shape, x.dtype) for x in xs], in_specs=[ANY] * na, out_specs=[ANY] * na,
        scratch_shapes=[pltpu.SemaphoreType.DMA((6 * na,)), pltpu.SemaphoreType.DMA((6 * na,))])(*xs)
    chip = 2 * lax.axis_index("x") + lax.axis_index("y")
    return [lax.dynamic_update_slice_in_dim(o, x[None], chip, axis=0) for o, x in zip(outs, xs)]


HBM = pl.BlockSpec(memory_space=pltpu.HBM)
SEM = pl.BlockSpec(memory_space=pltpu.SEMAPHORE)
DATAFLOW = pltpu.SideEffectType.DATAFLOW_SIDE_EFFECTING


def _core_half(ref, axis, c):
    if axis is None:
        return ref
    h = ref.shape[axis] // 2
    return ref.at[pl.ds(c * h, h), :] if axis == 0 else ref.at[:, pl.ds(c * h, h)]


def gather_start(xs, lands, halves, after, name):
    na = len(xs)

    def body(*refs):
        x_refs, land_refs = refs[:na], refs[na:2 * na]
        send_sems, recv_sems = refs[2 * na + 1], refs[2 * na + 2]
        token = refs[-1]
        mx, my, mc = _me()
        chip = 2 * mx + my
        for a in range(na):
            for j, (fx, fy) in enumerate(CHIP_FLIPS):
                _remote(_core_half(x_refs[a], halves[a], mc), _core_half(land_refs[a].at[chip], halves[a], mc),
                        send_sems, recv_sems, 3 * a + j, (_flip(mx, fx), _flip(my, fy), mc)).start()
        token[...] = jnp.zeros_like(token)

    outs = pl.pallas_call(
        body, name=name,
        out_shape=(pltpu.SemaphoreType.DMA((3 * na,)), pltpu.SemaphoreType.DMA((3 * na,)))
        + tuple(pltpu.HBM(x.shape, x.dtype) for x in xs) + tuple(pltpu.HBM(l.shape, l.dtype) for l in lands)
        + (S((8, LANE), F32),),
        in_specs=[HBM] * (2 * na) + [pl.BlockSpec(memory_space=pl.ANY)],
        out_specs=(SEM, SEM) + (HBM,) * (2 * na) + (pl.BlockSpec(memory_space=pltpu.VMEM),),
        input_output_aliases={a: 2 + a for a in range(2 * na)},
        compiler_params=pltpu.CompilerParams(has_side_effects=DATAFLOW),
    )(*[pltpu.with_memory_space_constraint(t, pltpu.HBM) for t in list(xs) + list(lands)], after)
    return outs[0], outs[1], list(outs[2:2 + na]), list(outs[2 + na:2 + 2 * na])


def gather_wait(send_sems, recv_sems, xs, lands, halves, first, after, name):
    na = len(xs)

    def body(*refs):
        x_refs, land_refs = refs[:na], refs[na:2 * na]
        send_sems, recv_sems = refs[2 * na], refs[2 * na + 1]
        mx, my, mc = _me()
        for a in range(na):
            for j, (fx, fy) in enumerate(CHIP_FLIPS):
                px, py = _flip(mx, fx), _flip(my, fy)
                cp = _remote(_core_half(x_refs[a], halves[a], mc), _core_half(land_refs[a].at[2 * px + py], halves[a], mc),
                             send_sems, recv_sems, 3 * (first + a) + j, (px, py, mc))
                cp.wait_send()
                cp.wait_recv()

    outs = pl.pallas_call(
        body, name=name,
        out_shape=tuple(pltpu.HBM(x.shape, x.dtype) for x in xs) + tuple(pltpu.HBM(l.shape, l.dtype) for l in lands),
        in_specs=[HBM] * (2 * na) + [SEM, SEM, pl.BlockSpec(memory_space=pl.ANY)], out_specs=(HBM,) * (2 * na),
        input_output_aliases={a: a for a in range(2 * na)},
        compiler_params=pltpu.CompilerParams(has_side_effects=DATAFLOW),
    )(*xs, *lands, send_sems, recv_sems, after)
    return list(outs[na:])


def forward_halves(land, axis, name):
    def body(in_ref, out_ref, send_sems, recv_sems):
        del in_ref
        mx, my, mc = _me()
        sib = (mx, my, 1 - mc)
        slots = [2 * _flip(mx, fx) + _flip(my, fy) for fx, fy in CHIP_FLIPS]
        cps = [_remote(_core_half(out_ref.at[s], axis, mc), _core_half(out_ref.at[s], axis, mc), send_sems, recv_sems, j, sib)
               for j, s in enumerate(slots)]
        for cp in cps:
            cp.start()
        for j, s in enumerate(slots):
            _remote(_core_half(out_ref.at[s], axis, mc), _core_half(out_ref.at[s], axis, 1 - mc), send_sems, recv_sems, j, sib).wait_recv()
        for cp in cps:
            cp.wait_send()

    return pl.pallas_call(
        body, name=name, out_shape=S(land.shape, land.dtype), in_specs=[ANY], out_specs=ANY, input_output_aliases={0: 0},
        scratch_shapes=[pltpu.SemaphoreType.DMA((3,)), pltpu.SemaphoreType.DMA((3,))])(land)


def all_gather_devices(v, name):
    r, l = v.shape

    def body(v_ref, out_ref, send_sems, recv_sems):
        mx, my, mc = _me()
        me = 4 * mx + 2 * my + mc
        peers = [(_flip(mx, fx), _flip(my, fy), _flip(mc, fc)) for fx, fy, fc in DEV_FLIPS]
        sends = [_remote(v_ref, out_ref.at[me], send_sems, recv_sems, j, p) for j, p in enumerate(peers)]
        for cp in sends:
            cp.start()
        for j, (px, py, pc) in enumerate(peers):
            _remote(v_ref, out_ref.at[4 * px + 2 * py + pc], send_sems, recv_sems, j, (px, py, pc)).wait_recv()
        for cp in sends:
            cp.wait_send()

    out = pl.pallas_call(
        body, name=name, out_shape=S((N_DEV, r, l), v.dtype), in_specs=[ANY], out_specs=ANY,
        scratch_shapes=[pltpu.SemaphoreType.DMA((7,)), pltpu.SemaphoreType.DMA((7,))])(v)
    me = 4 * lax.axis_index("x") + 2 * lax.axis_index("y") + lax.axis_index("c")
    return lax.dynamic_update_slice_in_dim(out, v[None], me, axis=0)


def sum_devices(g8, name):
    nd, r, l = g8.shape
    tile = r
    for t in (512, 256, 128, 64, 32, 16, 8):
        if r % t == 0:
            tile = t
            break

    def body(g_ref, o_ref):
        acc = g_ref[0]
        for d in range(1, nd):
            acc = acc + g_ref[d]
        o_ref[...] = acc

    return _call(body, name, (r // tile,), [pl.BlockSpec((nd, tile, l), lambda i: (0, i, 0))], _rs(tile, l), S((r, l), F32),
                 sem=("parallel",))(g8)


def exchange_halves(gs, name):
    na = len(gs)
    nch = gs[0].shape[0]

    def body(*refs):
        g_refs, out_refs = refs[:na], refs[na:2 * na]
        send_sems, recv_sems = refs[2 * na:]
        mx, my, mc = _me()
        sib = (mx, my, 1 - mc)
        cps = []
        for a in range(na):
            half = gs[a].shape[1] // 2
            for j in range(nch):
                cps.append(_remote(g_refs[a].at[j, pl.ds((1 - mc) * half, half), :], out_refs[a].at[j],
                                   send_sems, recv_sems, nch * a + j, sib))
        for cp in cps:
            cp.start()
        for cp in cps:
            cp.wait_recv()
        for cp in cps:
            cp.wait_send()

    return pl.pallas_call(
        body, name=name, out_shape=[S((nch, g.shape[1] // 2, g.shape[2]), g.dtype) for g in gs],
        in_specs=[ANY] * na, out_specs=[ANY] * na,
        scratch_shapes=[pltpu.SemaphoreType.DMA((nch * na,)), pltpu.SemaphoreType.DMA((nch * na,))])(*gs)


def add_halves(g4, got, name):
    nch, r, l = g4.shape
    half = r // 2
    tile = _pick_rows(half)
    nt = half // tile

    def body(g_ref, r_ref, a_ref, own_ref):
        j = pl.program_id(1)
        chip = 2 * lax.axis_index("x") + lax.axis_index("y")
        val = g_ref[0].astype(F32) + r_ref[0].astype(F32)
        a_ref[0] = val.astype(a_ref.dtype)

        @pl.when(j == chip)
        def _():
            own_ref[...] = val

    return pl.pallas_call(
        body, name=name, grid=(nt, nch),
        in_specs=[pl.BlockSpec((1, tile, l), lambda i, j: (j, lax.axis_index("c") * nt + i, 0)),
                  pl.BlockSpec((1, tile, l), lambda i, j: (j, i, 0))],
        out_specs=[pl.BlockSpec((1, tile, l), lambda i, j: (j, i, 0)), pl.BlockSpec((tile, l), lambda i, j: (i, 0))],
        out_shape=[S((nch, half, l), MXU_DTYPE), S((half, l), F32)],
        compiler_params=pltpu.CompilerParams(dimension_semantics=("parallel", "arbitrary"), vmem_limit_bytes=VMEM_LIMIT))(g4, got)


def _pick_rows(r, cap=640):
    return next((t for t in range(cap - cap % 16, 15, -16) if r % t == 0), r)


def exchange_chips(parts, name):
    na = len(parts)

    def body(*refs):
        a_refs, out_refs = refs[:na], refs[na:2 * na]
        send_sems, recv_sems = refs[2 * na:]
        mx, my, mc = _me()
        peers = [(_flip(mx, fx), _flip(my, fy)) for fx, fy in CHIP_FLIPS]
        cps = [_remote(a_refs[a].at[2 * px + py], out_refs[a].at[j], send_sems, recv_sems, 3 * a + j, (px, py, mc))
               for j, (px, py) in enumerate(peers) for a in range(na)]
        for cp in cps:
            cp.start()
        for cp in cps:
            cp.wait_recv()
        for cp in cps:
            cp.wait_send()

    return pl.pallas_call(
        body, name=name, out_shape=[S((3,) + p.shape[1:], p.dtype) for p in parts], in_specs=[ANY] * na, out_specs=[ANY] * na,
        scratch_shapes=[pltpu.SemaphoreType.DMA((3 * na,)), pltpu.SemaphoreType.DMA((3 * na,))])(*parts)


def add_chips(own, got, name):
    h, l = own.shape
    tile = _pick_rows(h)

    def body(o_ref, g_ref, out_ref):
        out_ref[...] = ((o_ref[...] + g_ref[0].astype(F32)) + g_ref[1].astype(F32)) + g_ref[2].astype(F32)

    nt = h // tile
    return _call(body, name, (nt,), [_rs(tile, l), pl.BlockSpec((3, tile, l), lambda i: (0, i, 0))],
                 pl.BlockSpec((tile, l), lambda i: (lax.axis_index("c") * nt + i, 0)),
                 S((2 * h, l), F32), sem=("parallel",))(own, got)


def join_halves(bufs, name):
    na = len(bufs)

    def body(*refs):
        out_refs = refs[na:2 * na]
        send_sems, recv_sems = refs[2 * na:]
        mx, my, mc = _me()
        sib = (mx, my, 1 - mc)

        def half(a, hc):
            h = bufs[a].shape[0] // 2
            return out_refs[a].at[pl.ds(hc * h, h), :]

        cps = [_remote(half(a, mc), half(a, mc), send_sems, recv_sems, a, sib) for a in range(na)]
        for cp in cps:
            cp.start()
        for a in range(na):
            _remote(half(a, mc), half(a, 1 - mc), send_sems, recv_sems, a, sib).wait_recv()
        for cp in cps:
            cp.wait_send()

    return pl.pallas_call(
        body, name=name, out_shape=[S(b.shape, b.dtype) for b in bufs], in_specs=[ANY] * na, out_specs=[ANY] * na,
        input_output_aliases={a: a for a in range(na)},
        scratch_shapes=[pltpu.SemaphoreType.DMA((na,)), pltpu.SemaphoreType.DMA((na,))])(*bufs)


def exchange_chips_start(parts, name):
    na = len(parts)
    lands = [lax.empty((3,) + p.shape[1:], p.dtype) for p in parts]

    def body(*refs):
        a_refs, land_refs = refs[:na], refs[na:2 * na]
        send_sems, recv_sems = refs[2 * na], refs[2 * na + 1]
        mx, my, mc = _me()
        for j, (fx, fy) in enumerate(CHIP_FLIPS):
            px, py = _flip(mx, fx), _flip(my, fy)
            for a in range(na):
                _remote(a_refs[a].at[2 * px + py], land_refs[a].at[j], send_sems, recv_sems, 3 * a + j, (px, py, mc)).start()
        refs[-1][...] = jnp.zeros_like(refs[-1])

    outs = pl.pallas_call(
        body, name=name,
        out_shape=(pltpu.SemaphoreType.DMA((3 * na,)), pltpu.SemaphoreType.DMA((3 * na,)))
        + tuple(pltpu.HBM(t.shape, t.dtype) for t in list(parts) + lands) + (S((8, LANE), F32),),
        in_specs=[HBM] * (2 * na), out_specs=(SEM, SEM) + (HBM,) * (2 * na) + (pl.BlockSpec(memory_space=pltpu.VMEM),),
        input_output_aliases={a: 2 + a for a in range(2 * na)},
        compiler_params=pltpu.CompilerParams(has_side_effects=DATAFLOW),
    )(*[pltpu.with_memory_space_constraint(t, pltpu.HBM) for t in list(parts) + lands])
    return outs[0], outs[1], list(outs[2:2 + na]), list(outs[2 + na:2 + 2 * na]), outs[-1]


def exchange_chips_wait(send_sems, recv_sems, parts, lands, after, name):
    na = len(parts)

    def body(*refs):
        a_refs, land_refs = refs[:na], refs[na:2 * na]
        send_sems, recv_sems = refs[2 * na], refs[2 * na + 1]
        mx, my, mc = _me()
        for j, (fx, fy) in enumerate(CHIP_FLIPS):
            px, py = _flip(mx, fx), _flip(my, fy)
            for a in range(na):
                cp = _remote(a_refs[a].at[2 * px + py], land_refs[a].at[j], send_sems, recv_sems, 3 * a + j, (px, py, mc))
                cp.wait_send()
                cp.wait_recv()

    outs = pl.pallas_call(
        body, name=name, out_shape=tuple(pltpu.HBM(t.shape, t.dtype) for t in list(parts) + list(lands)),
        in_specs=[HBM] * (2 * na) + [SEM, SEM, pl.BlockSpec(memory_space=pl.ANY)], out_specs=(HBM,) * (2 * na),
        input_output_aliases={a: a for a in range(2 * na)},
        compiler_params=pltpu.CompilerParams(has_side_effects=DATAFLOW),
    )(*parts, *lands, send_sems, recv_sems, after)
    return list(outs[na:])


def reduce_begin(gs, tag):
    got = exchange_halves(gs, f"{tag}_d2d")
    sums = [add_halves(g, r, f"{tag}_add1_{i}") for i, (g, r) in enumerate(zip(gs, got))]
    return [own for _, own in sums], exchange_chips_start([a for a, _ in sums], f"{tag}_ici_start")


def reduce_end(state, after, tag):
    owns, (send_sems, recv_sems, parts, lands, _) = state
    got = exchange_chips_wait(send_sems, recv_sems, parts, lands, after, f"{tag}_ici_wait")
    return [add_chips(own, r, f"{tag}_add2_{i}") for i, (own, r) in enumerate(zip(owns, got))]


PACK_L = 1024
BIG_ROW_MULT = 512


def _pack(arrs, dtype, row_mult, lead=None):
    lead_shape = () if lead is None else arrs[0].shape[:lead]
    flat = jnp.concatenate([a.astype(dtype).reshape(lead_shape + (-1,)) for a in arrs], axis=-1)
    n = flat.shape[-1]
    unit = row_mult * PACK_L
    total = -(-n // unit) * unit
    flat = jnp.pad(flat, [(0, 0)] * len(lead_shape) + [(0, total - n)])
    return flat.reshape(lead_shape + (total // PACK_L, PACK_L))


def _unpack(packed, shapes, lead=None):
    lead_shape = () if lead is None else packed.shape[:lead]
    flat = packed.reshape(lead_shape + (-1,))
    out, off = [], 0
    for shp in shapes:
        n = int(np.prod(shp))
        out.append(flat[..., off:off + n].reshape(lead_shape + tuple(shp)))
        off += n
    return out


def _to_full(gathered, axis):
    t = jnp.moveaxis(gathered, 0, axis)
    shp = t.shape
    return t.reshape(shp[:axis] + (shp[axis] * shp[axis + 1],) + shp[axis + 2:])


def _to_chip_major(full, axis):
    shp = full.shape
    t = full.reshape(shp[:axis] + (N_CHIPS, shp[axis] // N_CHIPS) + shp[axis + 1:])
    return jnp.moveaxis(t, axis, 0)


WEIGHTS = ("norm_ffn1", "ffn1_w_in", "ffn1_w_out", "norm_mix", "norm_ffn2", "ffn2_w_in", "ffn2_w_out", "ple_norm",
           "ple_gate_w", "ple_proj_w", "hyb_w_in", "conv_dw_w", "conv_dw_b", "conv_ln_g", "conv_ln_b", "ssm_conv_w",
           "ssm_conv_b", "ssm_dt_bias", "ssm_a_log", "ssm_d", "ssm_norm", "hyb_w_out", "att_w_qkv", "att_b_qkv",
           "att_sinks", "att_w_o", "att_b_o", "final_norm")
SHARD_AXIS = dict(ffn1_w_in=2, ffn1_w_out=1, ffn2_w_in=2, ffn2_w_out=1, ple_gate_w=1, ple_proj_w=2, hyb_w_in=2,
                  conv_dw_w=2, ssm_conv_w=2, hyb_w_out=1, att_w_qkv=2, att_b_qkv=1, att_w_o=1, att_b_o=1)
BIG = ("ffn1_w_in", "ffn1_w_out", "ffn2_w_in", "ffn2_w_out", "ple_gate_w", "ple_proj_w", "hyb_w_in", "hyb_w_out",
       "att_w_qkv", "att_w_o")
TRANSPOSED = ("hyb_w_in",)
FORWARDED = (0, 2)


def _shard_axis(k):
    return 1 if k in TRANSPOSED else SHARD_AXIS[k]
SMALL_SHARDED = ("conv_dw_w", "ssm_conv_w", "att_b_qkv", "att_b_o")
SMALL = tuple(k for k in WEIGHTS if k not in BIG)


def _step(x, p, target, w, m, v):
    mx, my = lax.axis_index("x"), lax.axis_index("y")
    chip = 2 * mx + my
    w, m, v = ({k: (a.transpose(0, 2, 1) if k in TRANSPOSED else a) for k, a in d.items()} for d in (w, m, v))

    depth = w["norm_ffn1"].shape[0]
    order = sorted([(k, i) for i in range(depth) for k in BIG if _layer_index(k, i) is not None],
                   key=lambda t: (t[1], _stage(t[0])))
    small_g = all_gather_devices(_pack([w[k] for k in SMALL_SHARDED], F32, 8), "gather_small")
    shards = [w[k][_layer_index(k, i)].astype(MXU_DTYPE) for k, i in order]
    lands = [lax.dynamic_update_slice_in_dim(lax.empty((N_CHIPS,) + s.shape, s.dtype), s[None], chip, axis=0) for s in shards]
    halves = [(0 if s.shape[0] % 32 == 0 else 1) if pos in FORWARDED else None for pos, s in enumerate(shards)]
    send_sems, recv_sems, shards, lands = gather_start(shards, lands, halves, small_g, "gather_start")

    def fetch(i, k, after):
        p = order.index((k, i))
        g, = gather_wait(send_sems, recv_sems, [shards[p]], [lands[p]], [halves[p]], p, after, f"gather_wait_l{i}_{k}")
        if halves[p] is not None:
            g = forward_halves(g, halves[p], f"gather_forward_l{i}_{k}")
        if _shard_axis(k) == 2:
            return ColSharded(g)
        return g.reshape(-1, g.shape[-1])

    small_g = small_g[0::2]
    small_full = {k: _to_full(g, SHARD_AXIS[k])
                  for k, g in zip(SMALL_SHARDED, _unpack(small_g, [w[k].shape for k in SMALL_SHARDED], lead=1))}
    layers = [LayerWeights({k: small_full.get(k, w[k])[_layer_index(k, i)] for k in SMALL if _layer_index(k, i) is not None},
                           functools.partial(fetch, i)) for i in range(depth)]

    def bucket_of(layer, stage):
        return (layer, 0) if layer > 0 else (0, min(stage, 1))

    sink = GradSink({k: w[k].shape for k in BIG}, bucket_of)
    begun = {}

    def stage_done(i, stage, tie):
        b = bucket_of(i, stage)
        if stage > 0 and bucket_of(i, stage - 1) == b:
            return tie
        begun[b] = reduce_begin(list(sink.bufs[b].values()), f"grads_l{b[0]}_{b[1]}")
        return begun[b][1][-1]

    loss, dx, grads = trunk_fwd_bwd(x[0], p[:, 0], target[0], layers, w["final_norm"], sink, stage_done)

    results = {}

    def finish(buckets, after, tag):
        halves = {b: reduce_end(begun[b], after, f"grads_l{b[0]}_{b[1]}") for b in buckets}
        joined = iter(join_halves([h for b in buckets for h in halves[b]], f"grads_join_{tag}"))
        reduced = {b: {c: next(joined) for c in sink.bufs[b]} for b in buckets}
        last = after
        for (k, li), (b, key, _, off, r) in sink.where.items():
            if b in buckets:
                g = reduced[b][key]
                if r % 8:
                    g, off = g[off:off + r], 0
                results[k] = adamw_layer(w[k], g, off, m[k], v[k], li, results.get(k), f"adamw_{k}_{li}")
                last = results[k][1]
        return last

    order_b = list(begun)
    started_last = begun[order_b[-1]][1][-1]
    finish(order_b[-1:], finish(order_b[:-1], started_last, "early") if len(order_b) > 1 else dx, "last")
    g_out = {k: results[k][0] for k in BIG}
    vec = _pack([loss[0:1, 0:1]] + [grads[k] for k in SMALL], F32, 8)
    vec = sum_devices(all_gather_devices(vec, "gather_vectors"), "sum_vectors")
    parts = _unpack(vec, [(1, 1)] + [grads[k].shape for k in SMALL])
    loss_out = parts[0].reshape(())
    for k, g in zip(SMALL, parts[1:]):
        if k in SHARD_AXIS:
            ax = SHARD_AXIS[k]
            g = lax.dynamic_slice_in_dim(g, chip * w[k].shape[ax], w[k].shape[ax], axis=ax)
        g_out[k] = g

    for k in TRANSPOSED:
        results[k] = [a.transpose(0, 2, 1) for a in results[k]]
    g_out.update({k: results[k][0] for k in TRANSPOSED})
    delta, new_m, new_v = ({k: results[k][j] for k in BIG} for j in (1, 2, 3))
    shapes = [w[k].shape for k in SMALL]
    packed = [_pack([src[k] for k in SMALL], F32, 8) for src in (w, g_out, m, v)]
    outs = adamw(*packed, "adamw_small")
    for dst, o in zip((delta, new_m, new_v), outs):
        for k, a in zip(SMALL, _unpack(o, shapes)):
            dst[k] = a
    return ((loss_out, dx[None]) + tuple(g_out[k] for k in WEIGHTS) + tuple(delta[k] for k in WEIGHTS)
            + tuple(new_m[k] for k in WEIGHTS) + tuple(new_v[k] for k in WEIGHTS))


def kernel(x, p, norm_ffn1, ffn1_w_in, ffn1_w_out, norm_mix, norm_ffn2, ffn2_w_in, ffn2_w_out, ple_norm, ple_gate_w, ple_proj_w, hyb_w_in, conv_dw_w, conv_dw_b, conv_ln_g, conv_ln_b, ssm_conv_w, ssm_conv_b, ssm_dt_bias, ssm_a_log, ssm_d, ssm_norm, hyb_w_out, att_w_qkv, att_b_qkv, att_sinks, att_w_o, att_b_o, final_norm, loss_target, m_norm_ffn1, m_ffn1_w_in, m_ffn1_w_out, m_norm_mix, m_norm_ffn2, m_ffn2_w_in, m_ffn2_w_out, m_ple_norm, m_ple_gate_w, m_ple_proj_w, m_hyb_w_in, m_conv_dw_w, m_conv_dw_b, m_conv_ln_g, m_conv_ln_b, m_ssm_conv_w, m_ssm_conv_b, m_ssm_dt_bias, m_ssm_a_log, m_ssm_d, m_ssm_norm, m_hyb_w_out, m_att_w_qkv, m_att_b_qkv, m_att_sinks, m_att_w_o, m_att_b_o, m_final_norm, v_norm_ffn1, v_ffn1_w_in, v_ffn1_w_out, v_norm_mix, v_norm_ffn2, v_ffn2_w_in, v_ffn2_w_out, v_ple_norm, v_ple_gate_w, v_ple_proj_w, v_hyb_w_in, v_conv_dw_w, v_conv_dw_b, v_conv_ln_g, v_conv_ln_b, v_ssm_conv_w, v_ssm_conv_b, v_ssm_dt_bias, v_ssm_a_log, v_ssm_d, v_ssm_norm, v_hyb_w_out, v_att_w_qkv, v_att_b_qkv, v_att_sinks, v_att_w_o, v_att_b_o, v_final_norm):
    given = locals()
    w = {k: given[k] for k in WEIGHTS}
    m = {k: given["m_" + k] for k in WEIGHTS}
    v = {k: given["v_" + k] for k in WEIGHTS}
    return _step(x, p, loss_target, w, m, v)
```

```python
import functools

import numpy as np
import jax
import jax.numpy as jnp
from jax import lax
from jax.experimental import pallas as pl
from jax.experimental.pallas import tpu as pltpu

F32 = jnp.float32
BF16 = jnp.bfloat16
MXU_DTYPE = jnp.bfloat16
S = jax.ShapeDtypeStruct
MESH = pl.DeviceIdType.MESH

VMEM_LIMIT = 48 * 2**20
LANE = 128

EPS = 1e-6
SSM_HEADS = 16
HEAD_DIM = 64
SSM_GROUPS = 2
SSM_STATE = 128
SSM_CONV = 4
CHUNK = 128
CONV_WIDTH = 31
ATT_HEADS = 16
ATT_KV_HEADS = 4
WINDOW = 128
ROPE_THETA = 10000.0
ADAM_LR = 0.001
ADAM_B1 = 0.9
ADAM_B2 = 0.999
ADAM_EPS = 1e-08
ADAM_WD = 0.01
ADAM_STEP = 10

N_CHIPS = 4
N_DEV = 8

NN = ((1,), (0,))
NT = ((1,), (1,))
TN = ((0,), (0,))


def _mm(a, b, dims=NN):
    return lax.dot_general(a.astype(MXU_DTYPE), b.astype(MXU_DTYPE), (dims, ((), ())), preferred_element_type=F32)


def _split3(a):
    hi = a.astype(BF16)
    r = a - hi.astype(F32)
    mid = r.astype(BF16)
    lo = (r - mid.astype(F32)).astype(BF16)
    return hi, mid, lo


def _mm01(a, onehot, dims=NN):
    o = onehot.astype(BF16)
    out = None
    for part in _split3(a):
        t = lax.dot_general(part, o, (dims, ((), ())), preferred_element_type=F32)
        out = t if out is None else out + t
    return out


def _01mm(onehot, a):
    o = onehot.astype(BF16)
    out = None
    for part in _split3(a):
        t = lax.dot_general(o, part, (NN, ((), ())), preferred_element_type=F32)
        out = t if out is None else out + t
    return out


def _sigmoid(x):
    return 0.5 * jnp.tanh(0.5 * x) + 0.5


def _softplus(x):
    return jnp.maximum(x, 0.0) + jnp.log(1.0 + jnp.exp(-jnp.abs(x)))


def _iota(shape, axis):
    return lax.broadcasted_iota(jnp.int32, shape, axis)


def _head_indicator(width, heads, transposed=False):
    per = width // heads
    if transposed:
        return (_iota((heads, width), 1) // per == _iota((heads, width), 0)).astype(F32)
    return (_iota((width, heads), 0) // per == _iota((width, heads), 1)).astype(F32)


def _acc(ref, i, val):
    @pl.when(i == 0)
    def _():
        ref[...] = val

    @pl.when(i > 0)
    def _():
        ref[...] += val


def _rs(tile, width, col=0, shift=0, n=None):
    if shift == 0:
        return pl.BlockSpec((tile, width), lambda i: (i, col))
    if shift < 0:
        return pl.BlockSpec((tile, width), lambda i: (jnp.maximum(i - 1, 0), col))
    return pl.BlockSpec((tile, width), lambda i: (jnp.minimum(i + 1, n - 1), col))


def _ps(shape):
    return pl.BlockSpec(shape, lambda i: (0,) * len(shape))


def _call(body, name, grid, in_specs, out_specs, out_shape, scratch=(), sem=None):
    return pl.pallas_call(
        body, name=name, grid=grid, in_specs=in_specs, out_specs=out_specs, out_shape=out_shape,
        scratch_shapes=list(scratch),
        compiler_params=pltpu.CompilerParams(dimension_semantics=sem, vmem_limit_bytes=VMEM_LIMIT))


def _row_tile(n, target):
    t = min(n, target)
    assert n % t == 0, (n, t)
    return t


def _pick_tile(dim, target):
    if dim <= target:
        return dim
    t = (int(1.4 * target) // LANE) * LANE
    while t >= LANE:
        if dim % t == 0:
            return t
        t -= LANE
    return dim


ANY = pl.BlockSpec(memory_space=pl.ANY)


def _paired(j):
    return (j % 2) * 2 + j // 2


class ColSharded:
    def __init__(self, arr, paired=False):
        self.arr, self.paired = arr, paired
        self.nch, self.rows, self.per = arr.shape
        self.shape = (self.rows, self.nch * self.per)

    def chip(self, j):
        return _paired(j) if self.paired else j


class Slot:
    def __init__(self, buf, kind, per, off, c0=0, paired=False):
        self.buf, self.kind, self.per, self.off, self.c0, self.paired = buf, kind, per, off, c0, paired

    def chip(self, j):
        return _paired(j) if self.paired else j


def matmul(a, b, mode, name, *, out_dtype=F32, scale=None, res=None, bias=None, into=None, tm=1024, tn=1024, tk=1024):
    bshape = b.shape
    if mode == "nn":
        (m, k), (k2, n) = a.shape, bshape
    elif mode == "nt":
        (m, k), (n, k2) = a.shape, bshape
    else:
        (k, m), (k2, n) = a.shape, bshape
    assert k == k2, (a.shape, bshape, mode)
    tm, tn, tk = _pick_tile(m, tm), _pick_tile(n, tn), _pick_tile(k, tk)
    if isinstance(b, ColSharded):
        if mode == "nn":
            tn = b.per
        else:
            assert mode == "nt"
            tk = b.per
    if into is not None:
        if into.kind == "c":
            tn = into.per
            assert into.off % tm == 0 and n == N_CHIPS * into.per
        else:
            tm = max(1, min(m, int(1.4 * 1024)) // into.per) * into.per
            assert m % tm == 0 and into.off % into.per == 0 and into.c0 % (tm // into.per) == 0
    nk = k // tk
    dims = {"nn": NN, "nt": NT, "tn": TN}[mode]
    a_spec = (pl.BlockSpec((tk, tm), lambda i, j, kk: (kk, i)) if mode == "tn"
              else pl.BlockSpec((tm, tk), lambda i, j, kk: (i, kk)))
    if isinstance(b, ColSharded):
        bchip = b.chip
        b_spec = (pl.BlockSpec((None, tk, tn), lambda i, j, kk: (bchip(j), kk, 0)) if mode == "nn"
                  else pl.BlockSpec((None, tn, tk), lambda i, j, kk: (bchip(kk), j, 0)))
        b = b.arr
    else:
        b_spec = (pl.BlockSpec((tn, tk), lambda i, j, kk: (j, kk)) if mode == "nt"
                  else pl.BlockSpec((tk, tn), lambda i, j, kk: (kk, j)))
    plain_o = pl.BlockSpec((tm, tn), lambda i, j, kk: (i, j))
    ins, in_specs = [a, b], [a_spec, b_spec]
    if bias is not None:
        ins.append(bias)
        in_specs.append(pl.BlockSpec((1, tn), lambda i, j, kk: (0, j)))
    if res is not None:
        ins.append(res)
        in_specs.append(plain_o)
    aliases = {}
    if into is None:
        o_spec, o_shape = plain_o, S((m, n), out_dtype)
    else:
        aliases = {len(ins): 0}
        ins.append(into.buf)
        in_specs.append(ANY)
        o_shape = S(into.buf.shape, into.buf.dtype)
        if into.kind == "c":
            ob, ochip = into.off // tm, into.chip
            o_spec = pl.BlockSpec((None, tm, tn), lambda i, j, kk: (ochip(j), ob + i, 0))
        else:
            q, ob = tm // into.per, into.off // into.per
            cb = into.c0 // q
            o_spec = pl.BlockSpec((q, into.per, tn), lambda i, j, kk: (cb + i, ob, j))

    def body(*refs):
        a_ref, b_ref = refs[0], refs[1]
        o_ref, acc_ref = refs[-2], refs[-1]
        kk = pl.program_id(2)

        @pl.when(kk == 0)
        def _():
            acc_ref[...] = jnp.zeros_like(acc_ref)

        acc_ref[...] += _mm(a_ref[...], b_ref[...], dims)

        @pl.when(kk == nk - 1)
        def _():
            out = acc_ref[...]
            if scale is not None:
                out = out * scale
            pos = 2
            if bias is not None:
                out = out + refs[pos][...]
                pos += 1
            if res is not None:
                out = out + refs[pos][...]
            o_ref[...] = out.astype(o_ref.dtype).reshape(o_ref.shape)

    return pl.pallas_call(
        body, name=name, grid=(m // tm, n // tn, nk), in_specs=in_specs, out_specs=o_spec, out_shape=o_shape,
        scratch_shapes=[pltpu.VMEM((tm, tn), F32)], input_output_aliases=aliases,
        compiler_params=pltpu.CompilerParams(dimension_semantics=("parallel", "parallel", "arbitrary"),
                                             vmem_limit_bytes=VMEM_LIMIT))(*ins)


def rms_matmul(h, g, b, mode, name, bias=None):
    n, d = h.shape
    sharded = isinstance(b, ColSharded)
    n_out = b.shape[1] if mode == "nn" else b.shape[0]
    tm = _row_tile(n, 512)
    tn = b.per if sharded else _pick_tile(n_out, 1024)
    if sharded:
        assert mode == "nn"
        bchip = b.chip
        b_spec = pl.BlockSpec((None, d, tn), lambda i, j: (bchip(j), 0, 0))
        b = b.arr
    elif mode == "nn":
        b_spec = pl.BlockSpec((d, tn), lambda i, j: (0, j))
    else:
        b_spec = pl.BlockSpec((tn, d), lambda i, j: (j, 0))
    ins = [h, g, b] + ([bias] if bias is not None else [])
    in_specs = [pl.BlockSpec((tm, d), lambda i, j: (i, 0)), pl.BlockSpec((1, d), lambda i, j: (0, 0)), b_spec]
    if bias is not None:
        in_specs.append(pl.BlockSpec((1, tn), lambda i, j: (0, j)))

    def body(h_ref, g_ref, b_ref, *refs):
        o_ref, xn_ref = refs[-2:]
        x = h_ref[...]
        r = lax.rsqrt(jnp.mean(x * x, axis=-1, keepdims=True) + EPS)
        xn = (x * r * g_ref[...]).astype(xn_ref.dtype)

        @pl.when(pl.program_id(1) == 0)
        def _():
            xn_ref[...] = xn

        out = _mm(xn, b_ref[...], NN if mode == "nn" else NT)
        o_ref[...] = out if bias is None else out + refs[0][...]

    return pl.pallas_call(
        body, name=name, grid=(n // tm, n_out // tn), in_specs=in_specs,
        out_specs=[pl.BlockSpec((tm, tn), lambda i, j: (i, j)), pl.BlockSpec((tm, d), lambda i, j: (i, 0))],
        out_shape=[S((n, n_out), F32), S((n, d), MXU_DTYPE)],
        compiler_params=pltpu.CompilerParams(dimension_semantics=("parallel", "arbitrary"), vmem_limit_bytes=VMEM_LIMIT),
    )(*ins)


def _rms_bwd_math(x, g, dy):
    r = lax.rsqrt(jnp.mean(x * x, axis=-1, keepdims=True) + EPS)
    xh = x * r
    dg = jnp.sum(dy * xh, axis=0, keepdims=True)
    dxh = dy * g
    dx = r * (dxh - xh * jnp.mean(dxh * xh, axis=-1, keepdims=True))
    return dx, dg


def nt_rms_bwd(a, b, h, g, dh_in, name, extra=None, colsum=False, b_kd=False):
    n, k = a.shape
    d = h.shape[1]
    tm = _row_tile(n, 512)
    sharded = isinstance(b, ColSharded)
    tk = b.per if sharded else _pick_tile(k, 1024)
    nk = k // tk
    dims = NN if b_kd else NT
    if sharded:
        bchip = b.chip
        b_spec = pl.BlockSpec((None, d, tk), lambda i, kk: (bchip(kk), 0, 0))
        b = b.arr
    elif b_kd:
        b_spec = pl.BlockSpec((tk, d), lambda i, kk: (kk, 0))
    else:
        b_spec = pl.BlockSpec((d, tk), lambda i, kk: (0, kk))
    row = pl.BlockSpec((tm, d), lambda i, kk: (i, 0))
    vec = pl.BlockSpec((1, d), lambda i, kk: (0, 0))
    ins, in_specs = [a, b, h, g, dh_in], [pl.BlockSpec((tm, tk), lambda i, kk: (i, kk)), b_spec, row, vec, row]
    if extra is not None:
        k2 = extra[0].shape[1]
        ins += list(extra)
        in_specs += [pl.BlockSpec((tm, k2), lambda i, kk: (i, 0)),
                     pl.BlockSpec((k2, d) if b_kd else (d, k2), lambda i, kk: (0, 0))]
    n_in = len(ins)

    def body(*refs):
        a_ref, b_ref, h_ref, g_ref, dh_ref = refs[:5]
        o_ref, dg_ref = refs[n_in], refs[n_in + 1]
        acc_ref = refs[-1]
        i, kk = pl.program_id(0), pl.program_id(1)

        @pl.when(kk == 0)
        def _():
            acc_ref[...] = _mm(refs[5][...], refs[6][...], dims) if extra is not None else jnp.zeros_like(acc_ref)

        acc_ref[...] += _mm(a_ref[...], b_ref[...], dims)

        @pl.when(kk == nk - 1)
        def _():
            dx, dg = _rms_bwd_math(h_ref[...], g_ref[...], acc_ref[...])
            out = dh_ref[...] + dx
            o_ref[...] = out
            _acc(dg_ref, i, dg)
            if colsum:
                _acc(refs[n_in + 2], i, jnp.sum(out, axis=0, keepdims=True))

    n_vec = 2 if colsum else 1
    return pl.pallas_call(
        body, name=name, grid=(n // tm, nk), in_specs=in_specs, out_specs=[row] + [vec] * n_vec,
        out_shape=[S((n, d), F32)] + [S((1, d), F32)] * n_vec, scratch_shapes=[pltpu.VMEM((tm, d), F32)],
        compiler_params=pltpu.CompilerParams(dimension_semantics=("arbitrary", "arbitrary"), vmem_limit_bytes=VMEM_LIMIT),
    )(*ins)


def swiglu_in(h, g, w_in, name):
    n, d = h.shape
    per = w_in.per
    nj = w_in.nch // 2
    tile = _row_tile(n, 512)

    def body(h_ref, g_ref, wg_ref, wu_ref, xn_ref, u_ref, hm_ref):
        x = h_ref[...]
        r = lax.rsqrt(jnp.mean(x * x, axis=-1, keepdims=True) + EPS)
        xn = (x * r * g_ref[...]).astype(xn_ref.dtype)

        @pl.when(pl.program_id(1) == 0)
        def _():
            xn_ref[...] = xn

        a = _mm(xn, wg_ref[...])
        b = _mm(xn, wu_ref[...])
        u_ref[:, :per] = a.astype(u_ref.dtype)
        u_ref[:, per:] = b.astype(u_ref.dtype)
        hm_ref[...] = (a * _sigmoid(a) * b).astype(hm_ref.dtype)

    return pl.pallas_call(
        body, name=name, grid=(n // tile, nj),
        in_specs=[pl.BlockSpec((tile, d), lambda i, j: (i, 0)), pl.BlockSpec((1, d), lambda i, j: (0, 0)),
                  pl.BlockSpec((None, d, per), lambda i, j: (j, 0, 0)), pl.BlockSpec((None, d, per), lambda i, j: (nj + j, 0, 0))],
        out_specs=[pl.BlockSpec((tile, d), lambda i, j: (i, 0)), pl.BlockSpec((tile, 2 * per), lambda i, j: (i, j)),
                   pl.BlockSpec((tile, per), lambda i, j: (i, j))],
        out_shape=[S((n, d), MXU_DTYPE), S((n, 2 * nj * per), MXU_DTYPE), S((n, nj * per), MXU_DTYPE)],
        compiler_params=pltpu.CompilerParams(dimension_semantics=("parallel", "arbitrary"), vmem_limit_bytes=VMEM_LIMIT),
    )(h, g, w_in.arr, w_in.arr)


def swiglu_out_bwd(dh, w_out, u, after, name):
    n, d = dh.shape
    f = w_out.shape[0]
    per = u.shape[1] // 4
    nj = f // per
    tile = _row_tile(n, 512)

    def body(dh_ref, w_ref, u_ref, after_ref, du_ref):
        dm = 0.5 * _mm(dh_ref[...], w_ref[...], NT)
        a = u_ref[:, :per].astype(F32)
        b = u_ref[:, per:].astype(F32)
        s = _sigmoid(a)
        du_ref[:, :per] = (dm * b * s * (1.0 + a * (1.0 - s))).astype(du_ref.dtype)
        du_ref[:, per:] = (dm * a * s).astype(du_ref.dtype)

    return pl.pallas_call(
        body, name=name, grid=(n // tile, nj),
        in_specs=[pl.BlockSpec((tile, d), lambda i, j: (i, 0)), pl.BlockSpec((per, d), lambda i, j: (j, 0)),
                  pl.BlockSpec((tile, 2 * per), lambda i, j: (i, j)), ANY],
        out_specs=pl.BlockSpec((tile, 2 * per), lambda i, j: (i, j)),
        out_shape=S(u.shape, MXU_DTYPE),
        compiler_params=pltpu.CompilerParams(dimension_semantics=("parallel", "parallel"), vmem_limit_bytes=VMEM_LIMIT),
    )(dh, w_out, u, after)


def ple_fwd(h, g, w_gate, pp, name):
    n, d = h.shape
    tile = _row_tile(n, 512)

    def body(h_ref, g_ref, w_ref, pp_ref, o_ref, gl_ref, xn_ref):
        x = h_ref[...]
        r = lax.rsqrt(jnp.mean(x * x, axis=-1, keepdims=True) + EPS)
        xn = (x * r * g_ref[...]).astype(xn_ref.dtype)
        xn_ref[...] = xn
        gl = _mm(xn, w_ref[...])
        gl_ref[...] = gl
        o_ref[...] = x + _sigmoid(gl) * pp_ref[...]

    return _call(body, name, (n // tile,), [_rs(tile, d), _ps((1, d)), _ps(w_gate.shape), _rs(tile, d)],
                 [_rs(tile, d)] * 3, [S((n, d), F32), S((n, d), F32), S((n, d), MXU_DTYPE)], sem=("parallel",))(h, g, w_gate, pp)


def ple_bwd(dh, gl, pp, w_gate, h, g, after, name):
    n, d = dh.shape
    tile = _row_tile(n, 512)

    def body(dh_ref, gl_ref, pp_ref, w_ref, h_ref, g_ref, after_ref, o_ref, dpp_ref, dgl_ref, dg_ref):
        i = pl.program_id(0)
        s = _sigmoid(gl_ref[...])
        dh_ = dh_ref[...]
        dpp_ref[...] = (dh_ * s).astype(dpp_ref.dtype)
        dgl = (dh_ * pp_ref[...] * s * (1.0 - s)).astype(dgl_ref.dtype)
        dgl_ref[...] = dgl
        dx, dg = _rms_bwd_math(h_ref[...], g_ref[...], _mm(dgl, w_ref[...], NT))
        o_ref[...] = dh_ + dx
        _acc(dg_ref, i, dg)

    return _call(body, name, (n // tile,),
                 [_rs(tile, d)] * 3 + [_ps(w_gate.shape), _rs(tile, d), _ps((1, d)), ANY],
                 [_rs(tile, d)] * 3 + [_ps((1, d))],
                 [S((n, d), F32), S((n, d), MXU_DTYPE), S((n, d), MXU_DTYPE), S((1, d), F32)],
                 sem=("arbitrary",))(dh, gl, pp, w_gate, h, g, after)


def loss_head(h, g, target, name):
    n, d = h.shape
    tile = _row_tile(n, 256)

    def body(h_ref, g_ref, t_ref, dh_ref, dg_ref, loss_ref):
        i = pl.program_id(0)
        x = h_ref[...]
        gg = g_ref[...]
        r = lax.rsqrt(jnp.mean(x * x, axis=-1, keepdims=True) + EPS)
        err = x * r * gg - t_ref[...]
        part = 0.5 * jnp.sum(jnp.mean(err * err, axis=-1, keepdims=True), axis=0, keepdims=True)
        dx, dg = _rms_bwd_math(x, gg, err * (1.0 / d))
        dh_ref[...] = dx
        _acc(dg_ref, i, dg)
        _acc(loss_ref, i, jnp.broadcast_to(part, (8, LANE)))

    return _call(body, name, (n // tile,), [_rs(tile, d), _ps((1, d)), _rs(tile, d)],
                 [_rs(tile, d), _ps((1, d)), _ps((8, LANE))], [S((n, d), F32), S((1, d), F32), S((8, LANE), F32)],
                 sem=("arbitrary",))(h, g, target)


def _adamw_math(w, g, m, v):
    c1 = np.float32(1.0 - ADAM_B1 ** ADAM_STEP)
    c2 = np.float32(1.0 - ADAM_B2 ** ADAM_STEP)
    mm = ADAM_B1 * m + (1.0 - ADAM_B1) * g
    vv = ADAM_B2 * v + (1.0 - ADAM_B2) * (g * g)
    return -ADAM_LR * ((mm / c1) / (jnp.sqrt(vv / c2) + ADAM_EPS) + ADAM_WD * w), mm, vv


def adamw_layer(w, pack, off, m, v, li, prev, name):
    n, r, c = w.shape

    def body(w_ref, g_ref, m_ref, v_ref, *refs):
        go_ref, d_ref, mo_ref, vo_ref = refs[-4:]
        g = g_ref[...]
        go_ref[...] = g
        d_ref[...], mo_ref[...], vo_ref[...] = _adamw_math(w_ref[...], g, m_ref[...], v_ref[...])

    if r % 8 == 0:
        cap = 2**21 // (4 * c) // 8 * 8
        tile = next(t for t in range(min(cap, r), 7, -8) if r % t == 0 and off % t == 0)
        ob, steps = off // tile, r // tile
        blk = pl.BlockSpec((None, tile, c), lambda i: (li, i, 0))
        g_spec = pl.BlockSpec((tile, c), lambda i: (ob + i, 0))
    else:
        assert off == 0 and pack.shape[0] == r and c % (2 * LANE) == 0
        steps = c // (2 * LANE)
        blk = pl.BlockSpec((None, r, 2 * LANE), lambda i: (li, 0, i))
        g_spec = pl.BlockSpec((r, 2 * LANE), lambda i: (0, i))
    prev = list(prev) if prev is not None else []
    return pl.pallas_call(
        body, name=name, grid=(steps,),
        in_specs=[blk, g_spec, blk, blk] + [ANY] * len(prev),
        out_specs=[blk] * 4, out_shape=[S((n, r, c), F32)] * 4,
        input_output_aliases={4 + j: j for j in range(len(prev))},
        compiler_params=pltpu.CompilerParams(dimension_semantics=("parallel",), vmem_limit_bytes=VMEM_LIMIT),
    )(w, pack, m, v, *prev)


def adamw(w, g, m, v, name):
    r, c = w.shape
    tile = r
    for t in (512, 256, 128, 64, 32, 16, 8):
        if r % t == 0 and t * c * 4 <= 2**21:
            tile = t
            break

    def body(w_ref, g_ref, m_ref, v_ref, d_ref, mo_ref, vo_ref):
        d_ref[...], mo_ref[...], vo_ref[...] = _adamw_math(w_ref[...], g_ref[...], m_ref[...], v_ref[...])

    return _call(body, name, (r // tile,), [_rs(tile, c)] * 4, [_rs(tile, c)] * 3, [S((r, c), F32)] * 3,
                 sem=("parallel",))(w, g, m, v)


TAP_VREGS = 32


def _taps(src, w_ref, offsets, tile, put, bias=None):
    c = src.shape[1]
    rp = max(8, TAP_VREGS * 8 * LANE // c // 8 * 8)
    for r0 in range(0, tile, rp):
        acc = jnp.zeros((rp, c), F32) if bias is None else jnp.zeros((rp, c), F32) + bias
        for k, o in enumerate(offsets):
            acc = acc + w_ref[k:k + 1, :] * src[r0 + o:r0 + o + rp, :]
        put(slice(r0, r0 + rp), acc)


def _taps_fwd(sc, w_ref, width, halo, tile, put, bias):
    _taps(sc, w_ref, [halo - (width - 1) + k for k in range(width)], tile, put, bias)


def _taps_bwd_x(sc_d, w_ref, width, tile, put):
    _taps(sc_d, w_ref, [(width - 1) - k for k in range(width)], tile, put)


def _taps_bwd_w(dy, sc, dw_ref, width, halo, tile, i):
    @pl.when(i == 0)
    def _():
        dw_ref[...] = jnp.zeros_like(dw_ref)

    for k in range(width):
        o = halo - (width - 1) + k
        dw_ref[k:k + 1, :] += jnp.sum(dy * sc[o:o + tile, :], axis=0, keepdims=True)


def _ln_stats(x):
    mu = jnp.mean(x, axis=-1, keepdims=True)
    xc = x - mu
    r = lax.rsqrt(jnp.mean(xc * xc, axis=-1, keepdims=True) + EPS)
    return xc * r, r


def conv_group_fwd(proj, cw, cb, lg, lb, name):
    n = proj.shape[0]
    d = cw.shape[1]
    tile = _row_tile(n, 256)
    halo = 32

    def body(v_ref, g_ref, vp_ref, gp_ref, cw_ref, cb_ref, lg_ref, lb_ref, u_ref, u1_ref, sc):
        i = pl.program_id(0)
        first = (i > 0).astype(F32)
        sc[0:halo, :] = vp_ref[tile - halo:, :] * _sigmoid(gp_ref[tile - halo:, :]) * first
        sc[halo:, :] = v_ref[...] * _sigmoid(g_ref[...])
        def put(rows, acc):
            u1_ref[rows, :] = acc

        _taps_fwd(sc, cw_ref, CONV_WIDTH, halo, tile, put, cb_ref[...])
        xh, _ = _ln_stats(u1_ref[...])
        y = xh * lg_ref[...] + lb_ref[...]
        u_ref[...] = (y * _sigmoid(y)).astype(u_ref.dtype)

    return _call(body, name, (n // tile,),
                 [_rs(tile, d, 0), _rs(tile, d, 1), _rs(tile, d, 0, -1), _rs(tile, d, 1, -1),
                  _ps(cw.shape), _ps((1, d)), _ps((1, d)), _ps((1, d))],
                 [_rs(tile, d), _rs(tile, d)], [S((n, d), MXU_DTYPE), S((n, d), F32)],
                 scratch=[pltpu.VMEM((halo + tile, d), F32)], sem=("arbitrary",))(proj, proj, proj, proj, cw, cb, lg, lb)


def conv_group_bwd(du, u1, proj, cw, lg, lb, name):
    n = proj.shape[0]
    d = cw.shape[1]
    tile = _row_tile(n, 256)
    halo = 32
    nt = n // tile

    def body(du_ref, dun_ref, u1_ref, u1n_ref, v_ref, g_ref, vp_ref, gp_ref, cw_ref, lg_ref, lb_ref,
             dp_ref, dcw_ref, dcb_ref, dlg_ref, dlb_ref, sc, sc_d):
        i = pl.program_id(0)

        def ln_swish_bwd(dy_, u1_):
            xh, r = _ln_stats(u1_)
            y = xh * lg_ref[...] + lb_ref[...]
            s = _sigmoid(y)
            dyy = dy_ * s * (1.0 + y * (1.0 - s))
            dxh = dyy * lg_ref[...]
            dx = r * (dxh - jnp.mean(dxh, axis=-1, keepdims=True) - xh * jnp.mean(dxh * xh, axis=-1, keepdims=True))
            return dx, jnp.sum(dyy * xh, axis=0, keepdims=True), jnp.sum(dyy, axis=0, keepdims=True)

        du1, dlg, dlb = ln_swish_bwd(du_ref[...].astype(F32), u1_ref[...])
        du1n, _, _ = ln_swish_bwd(dun_ref[0:halo, :].astype(F32), u1n_ref[0:halo, :])
        sc_d[0:tile, :] = du1
        sc_d[tile:, :] = du1n * (i < nt - 1).astype(F32)
        sc[0:halo, :] = vp_ref[tile - halo:, :] * _sigmoid(gp_ref[tile - halo:, :]) * (i > 0).astype(F32)
        sc[halo:, :] = v_ref[...] * _sigmoid(g_ref[...])

        def put(rows, du0):
            sig = _sigmoid(g_ref[rows, :])
            dp_ref[rows, :d] = (du0 * sig).astype(dp_ref.dtype)
            dp_ref[rows, d:] = (du0 * v_ref[rows, :] * sig * (1.0 - sig)).astype(dp_ref.dtype)

        _taps_bwd_x(sc_d, cw_ref, CONV_WIDTH, tile, put)
        _taps_bwd_w(du1, sc, dcw_ref, CONV_WIDTH, halo, tile, i)
        _acc(dcb_ref, i, jnp.sum(du1, axis=0, keepdims=True))
        _acc(dlg_ref, i, dlg)
        _acc(dlb_ref, i, dlb)

    return _call(body, name, (nt,),
                 [_rs(tile, d), _rs(tile, d, 0, 1, nt), _rs(tile, d), _rs(tile, d, 0, 1, nt),
                  _rs(tile, d, 0), _rs(tile, d, 1), _rs(tile, d, 0, -1), _rs(tile, d, 1, -1),
                  _ps(cw.shape), _ps((1, d)), _ps((1, d))],
                 [_rs(tile, 2 * d), _ps(cw.shape), _ps((1, d)), _ps((1, d)), _ps((1, d))],
                 [S((n, proj.shape[1]), MXU_DTYPE), S(cw.shape, F32), S((1, d), F32), S((1, d), F32), S((1, d), F32)],
                 scratch=[pltpu.VMEM((halo + tile, d), F32), pltpu.VMEM((tile + halo, d), F32)],
                 sem=("arbitrary",))(du, du, u1, u1, proj, proj, proj, proj, cw, lg, lb)


def ssm_conv_fwd(proj, dtr, sw, sb, dtb, name):
    n = proj.shape[0]
    w = sw.shape[1]
    inner = SSM_HEADS * HEAD_DIM
    tile = _row_tile(n, 256)
    halo = 8

    def body(x_ref, xp_ref, dtr_ref, sw_ref, sb_ref, dtb_ref, pre_ref, xs_ref, bc_ref, dt_ref, sc):
        i = pl.program_id(0)
        sc[0:halo, :] = xp_ref[tile - halo:, :] * (i > 0).astype(F32)
        sc[halo:, :] = x_ref[...]
        def put(rows, acc):
            pre_ref[rows, :] = acc

        _taps_fwd(sc, sw_ref, SSM_CONV, halo, tile, put, sb_ref[...])
        pre = pre_ref[...]
        act = pre * _sigmoid(pre)
        xs_ref[...] = act[:, :inner]
        bc_ref[...] = act[:, inner:]
        dt = _softplus(dtr_ref[...] + dtb_ref[...])
        dt_ref[...] = jnp.where(_iota(dt.shape, 1) < SSM_HEADS, dt, 0.0)

    return _call(body, name, (n // tile,),
                 [_rs(tile, w, 2), _rs(tile, w, 2, -1), _rs(tile, LANE), _ps(sw.shape), _ps((1, w)), _ps((1, LANE))],
                 [_rs(tile, w), _rs(tile, inner), _rs(tile, w - inner), _rs(tile, LANE)],
                 [S((n, w), F32), S((n, inner), F32), S((n, w - inner), F32), S((n, LANE), F32)],
                 scratch=[pltpu.VMEM((halo + tile, w), F32)], sem=("arbitrary",))(proj, proj, dtr, sw, sb, dtb)


def ssm_conv_bwd(dxs, dbc, pre, proj, sw, dproj, name):
    n = proj.shape[0]
    w = sw.shape[1]
    inner = SSM_HEADS * HEAD_DIM
    tile = _row_tile(n, 256)
    halo = 8
    nt = n // tile

    def body(dxs_ref, dxsn_ref, dbc_ref, dbcn_ref, pre_ref, pren_ref, x_ref, xp_ref, sw_ref, dp_in_ref,
             dx_ref, dsw_ref, dsb_ref, sc, sc_d):
        i = pl.program_id(0)

        def silu_bwd(d_, p_):
            s = _sigmoid(p_)
            return d_ * s * (1.0 + p_ * (1.0 - s))

        sc_d[0:tile, :inner] = silu_bwd(dxs_ref[...], pre_ref[:, :inner])
        sc_d[0:tile, inner:] = silu_bwd(dbc_ref[...], pre_ref[:, inner:])
        last = (i < nt - 1).astype(F32)
        sc_d[tile:, :inner] = silu_bwd(dxsn_ref[0:halo, :], pren_ref[0:halo, :inner]) * last
        sc_d[tile:, inner:] = silu_bwd(dbcn_ref[0:halo, :], pren_ref[0:halo, inner:]) * last
        sc[0:halo, :] = xp_ref[tile - halo:, :] * (i > 0).astype(F32)
        sc[halo:, :] = x_ref[...]
        dpre = sc_d[0:tile, :]
        def put(rows, acc):
            dx_ref[rows, :] = acc.astype(dx_ref.dtype)

        _taps_bwd_x(sc_d, sw_ref, SSM_CONV, tile, put)
        _taps_bwd_w(dpre, sc, dsw_ref, SSM_CONV, halo, tile, i)
        _acc(dsb_ref, i, jnp.sum(dpre, axis=0, keepdims=True))

    return pl.pallas_call(
        body, name=name, grid=(nt,),
        in_specs=[_rs(tile, inner), _rs(tile, inner, 0, 1, nt), _rs(tile, w - inner), _rs(tile, w - inner, 0, 1, nt),
                  _rs(tile, w), _rs(tile, w, 0, 1, nt), _rs(tile, w, 2), _rs(tile, w, 2, -1), _ps(sw.shape), ANY],
        out_specs=[_rs(tile, w, 2), _ps(sw.shape), _ps((1, w))],
        out_shape=[S(dproj.shape, dproj.dtype), S(sw.shape, F32), S((1, w), F32)],
        scratch_shapes=[pltpu.VMEM((halo + tile, w), F32), pltpu.VMEM((tile + halo, w), F32)],
        input_output_aliases={9: 0},
        compiler_params=pltpu.CompilerParams(dimension_semantics=("arbitrary",), vmem_limit_bytes=VMEM_LIMIT),
    )(dxs, dxs, dbc, dbc, pre, pre, proj, proj, sw, dproj)


def _ssd_prologue(dt_ref, dtT_ref, al_ref, alc_ref):
    row = _iota((CHUNK, CHUNK), 0)
    col = _iota((CHUNK, CHUNK), 1)
    dt = dt_ref[:, :SSM_HEADS]
    a_row = -jnp.exp(al_ref[:, :SSM_HEADS])
    a_col = -jnp.exp(alc_ref[...])
    cs = _01mm((row >= col).astype(F32), dt * a_row)
    csT = _mm01(dtT_ref[...] * a_col, (row <= col).astype(F32))
    return dt, a_row, cs, csT, row, col


def _decay(cs, csT, h, row, col):
    lm = jnp.exp(jnp.where(row >= col, cs[:, h:h + 1] - csT[h:h + 1, :], -1e30))
    lmT = jnp.exp(jnp.where(col >= row, csT[h:h + 1, :] - cs[:, h:h + 1], -1e30))
    return lm, lmT


def ssd_fwd(xs, bc, dt, dtT, alog_row, alog_col, name):
    n, width = xs.shape
    nc = n // CHUNK
    gw = width // SSM_GROUPS
    hpg = SSM_HEADS // SSM_GROUPS
    ns = SSM_STATE

    def body(xs_ref, bc_ref, dt_ref, dtT_ref, al_ref, alc_ref, y_ref, hs_ref, h_sc):
        i = pl.program_id(0)

        @pl.when(i == 0)
        def _():
            h_sc[...] = jnp.zeros_like(h_sc)

        dt, a_row, cs, csT, row, col = _ssd_prologue(dt_ref, dtT_ref, al_ref, alc_ref)
        indT = _head_indicator(width, SSM_HEADS, transposed=True)
        dt_full = _mm01(dt, indT)
        e_full = jnp.exp(_mm01(cs, indT))
        dte_full = jnp.exp(_mm01(cs[CHUNK - 1:CHUNK, :] - cs, indT))
        xt = xs_ref[...] * dt_full
        hs_ref[0] = h_sc[...]
        lo = _iota((CHUNK, 2 * HEAD_DIM), 1) < HEAD_DIM
        for g in range(SSM_GROUPS):
            bg = bc_ref[:, g * ns:(g + 1) * ns]
            cg = bc_ref[:, (SSM_GROUPS + g) * ns:(SSM_GROUPS + g + 1) * ns]
            gm = _mm(cg, bg, NT)
            hg = h_sc[g * gw:(g + 1) * gw, :]
            yoff = e_full[:, g * gw:(g + 1) * gw] * _mm(cg, hg, NT)
            for pr in range(hpg // 2):
                h0 = g * hpg + 2 * pr
                c0 = h0 * HEAD_DIM
                xp = xt[:, c0:c0 + 2 * HEAD_DIM]
                m0 = gm * _decay(cs, csT, h0, row, col)[0]
                m1 = gm * _decay(cs, csT, h0 + 1, row, col)[0]
                yd = jnp.where(lo, _mm(m0, xp), _mm(m1, xp))
                y_ref[:, c0:c0 + 2 * HEAD_DIM] = yd + yoff[:, 2 * pr * HEAD_DIM:(2 * pr + 2) * HEAD_DIM]
            sg = _mm(xt[:, g * gw:(g + 1) * gw] * dte_full[:, g * gw:(g + 1) * gw], bg, TN)
            for hh in range(hpg):
                h = g * hpg + hh
                r0 = h * HEAD_DIM
                h_sc[r0:r0 + HEAD_DIM, :] = (h_sc[r0:r0 + HEAD_DIM, :] * jnp.exp(csT[h:h + 1, CHUNK - 1:CHUNK])
                                             + sg[hh * HEAD_DIM:(hh + 1) * HEAD_DIM, :])

    bcw = bc.shape[1]
    return _call(body, name, (nc,),
                 [_rs(CHUNK, width), _rs(CHUNK, bcw), _rs(CHUNK, LANE), pl.BlockSpec((SSM_HEADS, CHUNK), lambda i: (0, i)),
                  _ps((1, LANE)), _ps((SSM_HEADS, 1))],
                 [_rs(CHUNK, width), pl.BlockSpec((1, width, ns), lambda i: (i, 0, 0))],
                 [S((n, width), F32), S((nc, width, ns), F32)],
                 scratch=[pltpu.VMEM((width, ns), F32)], sem=("arbitrary",))(xs, bc, dt, dtT, alog_row, alog_col)


def ssd_bwd(xs, bc, dt, dtT, alog_row, alog_col, hs, dy, dxs_skip, name):
    n, width = xs.shape
    nc = n // CHUNK
    gw = width // SSM_GROUPS
    hpg = SSM_HEADS // SSM_GROUPS
    ns = SSM_STATE
    bcw = bc.shape[1]

    def body(xs_ref, bc_ref, dt_ref, dtT_ref, al_ref, alc_ref, hs_ref, dy_ref, skip_ref,
             dxs_ref, dbc_ref, ddtr_ref, dal_ref, ddtb_ref, dh_sc, dxt_sc):
        i = pl.program_id(0)

        @pl.when(i == 0)
        def _():
            dh_sc[...] = jnp.zeros_like(dh_sc)

        dt, a_row, cs, csT, row, col = _ssd_prologue(dt_ref, dtT_ref, al_ref, alc_ref)
        indT = _head_indicator(width, SSM_HEADS, transposed=True)
        ind = _head_indicator(width, SSM_HEADS)
        dt_full = _mm01(dt, indT)
        e_full = jnp.exp(_mm01(cs, indT))
        cs_last = cs[CHUNK - 1:CHUNK, :]
        dte = jnp.exp(cs_last - cs)
        dte_full = _mm01(dte, indT)
        xs_ = xs_ref[...]
        xt = xs_ * dt_full
        dy_ = dy_ref[...]
        hmat = hs_ref[0]
        ds = dh_sc[...]
        lo = _iota((CHUNK, 2 * HEAD_DIM), 1) < HEAD_DIM
        head_lane = _iota((1, SSM_HEADS), 1)
        dcs = jnp.zeros((CHUNK, SSM_HEADS), F32)
        ddte = jnp.zeros((CHUNK, SSM_HEADS), F32)
        for g in range(SSM_GROUPS):
            sl = slice(g * gw, (g + 1) * gw)
            bg = bc_ref[:, g * ns:(g + 1) * ns]
            cg = bc_ref[:, (SSM_GROUPS + g) * ns:(SSM_GROUPS + g + 1) * ns]
            gm = _mm(cg, bg, NT)
            gmT = _mm(bg, cg, NT)
            hg = hmat[sl, :]
            dsg = ds[sl, :]
            dyg = dy_[:, sl]
            xtg = xt[:, sl]
            yoff = e_full[:, sl] * _mm(cg, hg, NT)
            edy = e_full[:, sl] * dyg
            bds = _mm(bg, dsg, NT)
            dxt_g = dte_full[:, sl] * bds
            ddte = ddte + _mm01(xtg * bds, ind[sl, :])
            dcs = dcs + _mm01(dyg * yoff, ind[sl, :])
            db = _mm(xtg * dte_full[:, sl], dsg)
            dc = _mm(edy, hg)
            dhc = _mm(edy, cg, TN)
            dgs = jnp.zeros((CHUNK, CHUNK), F32)
            dgTs = jnp.zeros((CHUNK, CHUNK), F32)
            for pr in range(hpg // 2):
                h0 = g * hpg + 2 * pr
                c0 = 2 * pr * HEAD_DIM
                xp = xtg[:, c0:c0 + 2 * HEAD_DIM]
                dyp = dyg[:, c0:c0 + 2 * HEAD_DIM]
                rr = []
                for h, half in ((h0, lo), (h0 + 1, jnp.logical_not(lo))):
                    lm, lmT = _decay(cs, csT, h, row, col)
                    xm = jnp.where(half, xp, 0.0)
                    dm = _mm(dyp, xm, NT)
                    dmT = _mm(xm, dyp, NT)
                    mT = gmT * lmT
                    z = jnp.sum(dm * (gm * lm), axis=1, keepdims=True) - jnp.sum(dmT * mT, axis=1, keepdims=True)
                    dcs = dcs + z * (head_lane == h).astype(F32)
                    dgs = dgs + dm * lm
                    dgTs = dgTs + dmT * lmT
                    rr.append(_mm(mT, dyp))
                dxt_sc[:, g * gw + c0:g * gw + c0 + 2 * HEAD_DIM] = jnp.where(lo, rr[0], rr[1]) + dxt_g[:, c0:c0 + 2 * HEAD_DIM]
            dbc_ref[:, g * ns:(g + 1) * ns] = db + _mm(dgTs, cg)
            dbc_ref[:, (SSM_GROUPS + g) * ns:(SSM_GROUPS + g + 1) * ns] = dc + _mm(dgs, bg)
            for hh in range(hpg):
                h = g * hpg + hh
                r0 = h * HEAD_DIM
                dh_sc[r0:r0 + HEAD_DIM, :] = (dhc[hh * HEAD_DIM:(hh + 1) * HEAD_DIM, :]
                                              + jnp.exp(csT[h:h + 1, CHUNK - 1:CHUNK]) * ds[r0:r0 + HEAD_DIM, :])
        t = ddte * dte
        per_head = jnp.sum(jnp.sum(ds * hmat, axis=1, keepdims=True) * ind, axis=0, keepdims=True)
        last_add = jnp.sum(t, axis=0, keepdims=True) + jnp.exp(cs_last) * per_head
        dcs = dcs - t + jnp.where(_iota((CHUNK, SSM_HEADS), 0) == CHUNK - 1, last_add, 0.0)
        dadt = _01mm((row <= col).astype(F32), dcs)
        dxt = dxt_sc[...]
        ddt = dadt * a_row + _mm01(dxt * xs_, ind)
        dxs_ref[...] = dxt * dt_full + skip_ref[...]
        ddtr = ddt * (1.0 - jnp.exp(-dt))
        ddtr_ref[...] = jnp.zeros_like(ddtr_ref)
        ddtr_ref[:, :SSM_HEADS] = ddtr.astype(ddtr_ref.dtype)
        _acc(dal_ref, i, jnp.sum(dadt * dt, axis=0, keepdims=True) * a_row)
        _acc(ddtb_ref, i, jnp.sum(ddtr, axis=0, keepdims=True))

    rev = lambda i: (nc - 1 - i, 0)
    return _call(body, name, (nc,),
                 [pl.BlockSpec((CHUNK, width), rev), pl.BlockSpec((CHUNK, bcw), rev), pl.BlockSpec((CHUNK, LANE), rev),
                  pl.BlockSpec((SSM_HEADS, CHUNK), lambda i: (0, nc - 1 - i)), _ps((1, LANE)), _ps((SSM_HEADS, 1)),
                  pl.BlockSpec((1, width, ns), lambda i: (nc - 1 - i, 0, 0)), pl.BlockSpec((CHUNK, width), rev),
                  pl.BlockSpec((CHUNK, width), rev)],
                 [pl.BlockSpec((CHUNK, width), rev), pl.BlockSpec((CHUNK, bcw), rev), pl.BlockSpec((CHUNK, LANE), rev),
                  _ps((1, SSM_HEADS)), _ps((1, SSM_HEADS))],
                 [S((n, width), F32), S((n, bcw), F32), S((n, LANE), MXU_DTYPE), S((1, SSM_HEADS), F32), S((1, SSM_HEADS), F32)],
                 scratch=[pltpu.VMEM((width, ns), F32), pltpu.VMEM((CHUNK, width), F32)],
                 sem=("arbitrary",))(xs, bc, dt, dtT, alog_row, alog_col, hs, dy, dxs_skip)


def ssm_gate_fwd(yssd, xs, proj, dfull, gamma, name):
    n, d = yssd.shape
    tile = _row_tile(n, 256)
    gw = d // SSM_GROUPS

    def body(y_ref, xs_ref, z_ref, df_ref, gm_ref, o_ref):
        z = z_ref[...]
        y2 = (y_ref[...] + df_ref[...] * xs_ref[...]) * (z * _sigmoid(z))
        for g in range(SSM_GROUPS):
            yg = y2[:, g * gw:(g + 1) * gw]
            r = lax.rsqrt(jnp.mean(yg * yg, axis=-1, keepdims=True) + EPS)
            o_ref[:, g * gw:(g + 1) * gw] = (yg * r * gm_ref[:, g * gw:(g + 1) * gw]).astype(o_ref.dtype)

    return _call(body, name, (n // tile,), [_rs(tile, d), _rs(tile, d), _rs(tile, d, 2), _ps((1, d)), _ps((1, d))],
                 _rs(tile, d), S((n, d), MXU_DTYPE), sem=("parallel",))(yssd, xs, proj, dfull, gamma)


def ssm_gate_bwd(dy3, yssd, xs, proj, dfull, gamma, dproj, name):
    n, d = yssd.shape
    tile = _row_tile(n, 256)
    gw = d // SSM_GROUPS

    def body(dy_ref, y_ref, xs_ref, z_ref, df_ref, gm_ref, dp_in_ref, dys_ref, dxs_ref, dz_ref, dgm_ref, dd_ref):
        i = pl.program_id(0)
        z = z_ref[...]
        s = _sigmoid(z)
        xs_ = xs_ref[...]
        y1 = y_ref[...] + df_ref[...] * xs_
        y2 = y1 * (z * s)
        dy_ = dy_ref[...].astype(F32)
        dgm = []
        dy2 = []
        for g in range(SSM_GROUPS):
            sl = slice(g * gw, (g + 1) * gw)
            dxg, dgg = _rms_bwd_math(y2[:, sl], gm_ref[:, sl], dy_[:, sl])
            dy2.append(dxg)
            dgm.append(dgg)
        dy2 = jnp.concatenate(dy2, axis=1)
        dy1 = dy2 * (z * s)
        dys_ref[...] = dy1
        dxs_ref[...] = dy1 * df_ref[...]
        dz_ref[...] = (dy2 * y1 * s * (1.0 + z * (1.0 - s))).astype(dz_ref.dtype)
        _acc(dgm_ref, i, jnp.concatenate(dgm, axis=1))
        colsum = jnp.broadcast_to(jnp.sum(dy1 * xs_, axis=0, keepdims=True), (8, d))
        _acc(dd_ref, i, _mm01(colsum, _head_indicator(d, SSM_HEADS))[0:1, :])

    return pl.pallas_call(
        body, name=name, grid=(n // tile,),
        in_specs=[_rs(tile, d), _rs(tile, d), _rs(tile, d), _rs(tile, d, 2), _ps((1, d)), _ps((1, d)), ANY],
        out_specs=[_rs(tile, d), _rs(tile, d), _rs(tile, d, 2), _ps((1, d)), _ps((1, SSM_HEADS))],
        out_shape=[S((n, d), F32), S((n, d), F32), S(dproj.shape, dproj.dtype), S((1, d), F32), S((1, SSM_HEADS), F32)],
        input_output_aliases={6: 2},
        compiler_params=pltpu.CompilerParams(dimension_semantics=("arbitrary",), vmem_limit_bytes=VMEM_LIMIT),
    )(dy3, yssd, xs, proj, dfull, gamma, dproj)


def _rope128(x, cos, sin_signed):
    half = HEAD_DIM // 2
    lane = _iota(x.shape, 1)
    partner = jnp.where((lane % HEAD_DIM) < half, pltpu.roll(x, LANE - half, 1), pltpu.roll(x, half, 1))
    return x * cos + partner * sin_signed


def rope_fwd(qkv, cos, sin, name):
    n, w = qkv.shape
    qw = ATT_HEADS * HEAD_DIM
    kw = ATT_KV_HEADS * HEAD_DIM
    tile = _row_tile(n, 256)

    def body(x_ref, c_ref, s_ref, q_ref, k_ref, v_ref):
        c, s = c_ref[...], s_ref[...]
        for j in range(qw // LANE):
            q_ref[:, j * LANE:(j + 1) * LANE] = _rope128(x_ref[:, j * LANE:(j + 1) * LANE], c, s).astype(q_ref.dtype)
        for j in range(kw // LANE):
            k_ref[:, j * LANE:(j + 1) * LANE] = _rope128(x_ref[:, qw + j * LANE:qw + (j + 1) * LANE], c, s).astype(k_ref.dtype)
        v_ref[...] = x_ref[:, qw + kw:].astype(v_ref.dtype)

    return _call(body, name, (n // tile,), [_rs(tile, w), _rs(tile, LANE), _rs(tile, LANE)],
                 [_rs(tile, qw), _rs(tile, kw), _rs(tile, kw)],
                 [S((n, qw), MXU_DTYPE), S((n, kw), MXU_DTYPE), S((n, kw), MXU_DTYPE)], sem=("parallel",))(qkv, cos, sin)


ATT_GROUP = ATT_HEADS // ATT_KV_HEADS


def _attn_mask(i):
    row = _iota((ATT_GROUP * WINDOW, 2 * WINDOW), 0) % WINDOW
    s = _iota((ATT_GROUP * WINDOW, 2 * WINDOW), 1)
    return (s > row) & (s <= row + WINDOW) & ((s >= WINDOW) | (i > 0))


def _stack_heads(ref, j, kh, lo):
    parts = []
    for t in range(ATT_GROUP):
        h = ATT_GROUP * j + t
        blk = ref[:, (h // 2) * LANE:(h // 2 + 1) * LANE]
        blk = jnp.where(lo if h % 2 == 0 else jnp.logical_not(lo), blk, jnp.zeros_like(blk))
        parts.append(blk if h % 2 == kh else pltpu.roll(blk, HEAD_DIM, 1))
    return jnp.concatenate(parts, axis=0)


def _unstack_heads(stacked, j, kh, lo, put):
    for t in range(0, ATT_GROUP, 2):
        h = ATT_GROUP * j + t
        even = stacked[t * WINDOW:(t + 1) * WINDOW, :]
        odd = stacked[(t + 1) * WINDOW:(t + 2) * WINDOW, :]
        even = even if kh == 0 else pltpu.roll(even, HEAD_DIM, 1)
        odd = odd if kh == 1 else pltpu.roll(odd, HEAD_DIM, 1)
        put(h // 2, jnp.where(lo, even, odd))


def _per_head_rows(ref, j):
    return jnp.concatenate([ref[:, ATT_GROUP * j + t:ATT_GROUP * j + t + 1] for t in range(ATT_GROUP)], axis=0)


def _per_head_scalar(ref, j):
    rows = _iota((ATT_GROUP * WINDOW, 1), 0) // WINDOW
    out = jnp.zeros((ATT_GROUP * WINDOW, 1), F32)
    for t in range(ATT_GROUP):
        out = out + jnp.where(rows == t, ref[:, ATT_GROUP * j + t:ATT_GROUP * j + t + 1], 0.0)
    return out


def attn_fwd(q, k, v, sinks, name):
    n, qw = q.shape
    kw = k.shape[1]
    nb = n // WINDOW
    scale = HEAD_DIM ** -0.5

    def body(q_ref, kc_ref, kp_ref, vc_ref, vp_ref, sk_ref, o_ref, lse_ref):
        i = pl.program_id(0)
        valid = _attn_mask(i)
        lo = _iota((WINDOW, LANE), 1) < HEAD_DIM
        k2 = jnp.concatenate([kp_ref[...], kc_ref[...]], axis=0)
        v2 = jnp.concatenate([vp_ref[...], vc_ref[...]], axis=0)
        lane1 = _iota((1, LANE), 1)
        lse = jnp.zeros((WINDOW, LANE), F32)

        def put_o(qb, val):
            o_ref[:, qb * LANE:(qb + 1) * LANE] = val.astype(o_ref.dtype)

        for j in range(ATT_KV_HEADS):
            kb, kh = j // 2, j % 2
            q4 = _stack_heads(q_ref, j, kh, lo)
            logits = jnp.where(valid, _mm(q4, k2[:, kb * LANE:(kb + 1) * LANE], NT) * scale, -1e30)
            sk = _per_head_scalar(sk_ref, j)
            m = jnp.maximum(jnp.max(logits, axis=-1, keepdims=True), sk)
            e = jnp.exp(logits - m)
            den = jnp.sum(e, axis=-1, keepdims=True) + jnp.exp(sk - m)
            lse4 = m + jnp.log(den)
            for t in range(ATT_GROUP):
                lse = lse + lse4[t * WINDOW:(t + 1) * WINDOW, :] * (lane1 == ATT_GROUP * j + t).astype(F32)
            _unstack_heads(_mm(e * (1.0 / den), v2[:, kb * LANE:(kb + 1) * LANE]), j, kh, lo, put_o)
        lse_ref[...] = lse

    return _call(body, name, (nb,),
                 [_rs(WINDOW, qw), _rs(WINDOW, kw), _rs(WINDOW, kw, 0, -1), _rs(WINDOW, kw), _rs(WINDOW, kw, 0, -1), _ps((1, LANE))],
                 [_rs(WINDOW, qw), _rs(WINDOW, LANE)], [S((n, qw), MXU_DTYPE), S((n, LANE), F32)],
                 sem=("parallel",))(q, k, k, v, v, sinks)


def attn_bwd(q, k, v, o, do, lse, sinks, name):
    n, qw = q.shape
    kw = k.shape[1]
    nb = n // WINDOW
    scale = HEAD_DIM ** -0.5

    def body(q_ref, kc_ref, kp_ref, vc_ref, vp_ref, o_ref, do_ref, lse_ref, sk_ref,
             dq_ref, dka_ref, dkb_ref, dva_ref, dvb_ref, dsk_ref):
        i = pl.program_id(0)
        valid = _attn_mask(i)
        lo = _iota((WINDOW, LANE), 1) < HEAD_DIM
        k2 = jnp.concatenate([kp_ref[...], kc_ref[...]], axis=0)
        v2 = jnp.concatenate([vp_ref[...], vc_ref[...]], axis=0)
        lane1 = _iota((1, LANE), 1)
        do_ = do_ref[...].astype(F32)
        delta = _mm01(do_ * o_ref[...].astype(F32), _head_indicator(qw, ATT_HEADS))
        dk2 = [jnp.zeros((2 * WINDOW, LANE), F32) for _ in range(kw // LANE)]
        dv2 = [jnp.zeros((2 * WINDOW, LANE), F32) for _ in range(kw // LANE)]
        dsk = jnp.zeros((1, LANE), F32)

        def put_dq(qb, val):
            dq_ref[:, qb * LANE:(qb + 1) * LANE] = val

        for j in range(ATT_KV_HEADS):
            kb, kh = j // 2, j % 2
            q4 = _stack_heads(q_ref, j, kh, lo)
            do4 = _stack_heads(do_ref, j, kh, lo)
            kk = k2[:, kb * LANE:(kb + 1) * LANE]
            vv = v2[:, kb * LANE:(kb + 1) * LANE]
            logits = jnp.where(valid, _mm(q4, kk, NT) * scale, -1e30)
            lse4 = _per_head_rows(lse_ref, j)
            p = jnp.exp(logits - lse4)
            dl = jnp.concatenate([delta[:, ATT_GROUP * j + t:ATT_GROUP * j + t + 1] for t in range(ATT_GROUP)], axis=0)
            ds = p * (_mm(do4, vv, NT) - dl) * scale
            sd = jnp.exp(_per_head_scalar(sk_ref, j) - lse4) * dl
            for t in range(ATT_GROUP):
                dsk = dsk - (jnp.sum(sd[t * WINDOW:(t + 1) * WINDOW, :], axis=0, keepdims=True)
                             * (lane1 == ATT_GROUP * j + t).astype(F32))
            _unstack_heads(_mm(ds, kk), j, kh, lo, put_dq)
            dk2[kb] = dk2[kb] + _mm(ds, q4, TN)
            dv2[kb] = dv2[kb] + _mm(p, do4, TN)
        for kb in range(kw // LANE):
            dkb_ref[:, kb * LANE:(kb + 1) * LANE] = dk2[kb][0:WINDOW, :]
            dka_ref[:, kb * LANE:(kb + 1) * LANE] = dk2[kb][WINDOW:, :]
            dvb_ref[:, kb * LANE:(kb + 1) * LANE] = dv2[kb][0:WINDOW, :]
            dva_ref[:, kb * LANE:(kb + 1) * LANE] = dv2[kb][WINDOW:, :]
        _acc(dsk_ref, i, dsk)

    return _call(body, name, (nb,),
                 [_rs(WINDOW, qw), _rs(WINDOW, kw), _rs(WINDOW, kw, 0, -1), _rs(WINDOW, kw), _rs(WINDOW, kw, 0, -1),
                  _rs(WINDOW, qw), _rs(WINDOW, qw), _rs(WINDOW, LANE), _ps((1, LANE))],
                 [_rs(WINDOW, qw)] + [_rs(WINDOW, kw)] * 4 + [_ps((1, LANE))],
                 [S((n, qw), F32)] + [S((n, kw), F32)] * 4 + [S((1, LANE), F32)],
                 sem=("arbitrary",))(q, k, k, v, v, o, do, lse, sinks)


def attn_grad_merge(dq, dka, dkb, dva, dvb, cos, sin, name):
    n, qw = dq.shape
    kw = dka.shape[1]
    nb = n // WINDOW
    w = qw + 2 * kw

    def body(dq_ref, dka_ref, dkb_ref, dva_ref, dvb_ref, c_ref, s_ref, o_ref, db_ref):
        i = pl.program_id(0)
        c, s = c_ref[...], -s_ref[...]
        nxt = (i < nb - 1).astype(F32)

        @pl.when(i == 0)
        def _():
            db_ref[...] = jnp.zeros_like(db_ref)

        def put(c0, val):
            o_ref[:, c0:c0 + val.shape[1]] = val.astype(o_ref.dtype)
            db_ref[:, c0:c0 + val.shape[1]] += jnp.sum(val, axis=0, keepdims=True)

        for j in range(qw // LANE):
            put(j * LANE, _rope128(dq_ref[:, j * LANE:(j + 1) * LANE], c, s))
        for j in range(kw // LANE):
            sl = slice(j * LANE, (j + 1) * LANE)
            put(qw + j * LANE, _rope128(dka_ref[:, sl] + dkb_ref[:, sl] * nxt, c, s))
        put(qw + kw, dva_ref[...] + dvb_ref[...] * nxt)

    return _call(body, name, (nb,),
                 [_rs(WINDOW, qw), _rs(WINDOW, kw), _rs(WINDOW, kw, 0, 1, nb), _rs(WINDOW, kw), _rs(WINDOW, kw, 0, 1, nb),
                  _rs(WINDOW, LANE), _rs(WINDOW, LANE)],
                 [_rs(WINDOW, w), _ps((1, w))], [S((n, w), MXU_DTYPE), S((1, w), F32)],
                 sem=("arbitrary",))(dq, dka, dkb, dva, dvb, cos, sin)


def _row(v):
    return v.reshape(1, -1)


def _pad_lanes(v, width=LANE):
    return jnp.pad(v.reshape(1, -1), ((0, 0), (0, width - v.size)))


class LayerWeights(dict):
    def __init__(self, small, fetch):
        super().__init__(small)
        self.fetch = fetch

    def need(self, k, after):
        if k not in self:
            self[k] = self.fetch(k, after)
        return self[k]


def ffn_fwd(h, g, w, keys, tag):
    xn, u, hm = swiglu_in(h, _row(g), w.need(keys[0], h), f"{tag}_in")
    return matmul(hm, w.need(keys[1], hm), "nn", f"{tag}_out", scale=0.5, res=h), (h, xn, u, hm)


class GradSink:
    ORDER = ("ffn1_w_out", "ffn2_w_out", "ple_gate_w", "att_w_o", "hyb_w_out", "ffn1_w_in", "ffn2_w_in", "att_w_qkv",
             "ple_proj_w", "hyb_w_in")

    def __init__(self, shard_shapes, bucket_of):
        self.where, rows = {}, {}
        for k in self.ORDER:
            n, r, c = shard_shapes[k]
            for li in range(n):
                layer = li if k in PER_LAYER else 2 * li + (0 if k in EVEN_ONLY else 1)
                rows_b = rows.setdefault(bucket_of(layer, _stage(k)), {})
                key = c if r % 32 == 0 else k
                off = -(-rows_b.get(key, (0, c))[0] // r) * r
                rows_b[key] = (-(-(off + r) // 32) * 32, c)
                self.where[k, li] = (bucket_of(layer, _stage(k)), key, "r" if _shard_axis(k) == 1 else "c", off, r)
        self.bufs = {b: {key: lax.empty((N_CHIPS, r, c), MXU_DTYPE) for key, (r, c) in rows_b.items()}
                     for b, rows_b in rows.items()}

    def mm(self, k, li, a, b, name, scale=None, c0=0, paired=False):
        bucket, key, kind, off, r = self.where[k, li]
        buf = self.bufs[bucket][key]
        slot = Slot(buf, kind, r if kind == "r" else buf.shape[2], off, c0, paired)
        self.bufs[bucket][key] = matmul(a, b, "tn", name, scale=scale, into=slot)

    def put(self, k, li, chip_major):
        b, key = self.where[k, li][:2]
        pad = self.bufs[b][key].shape[1] - chip_major.shape[1]
        self.bufs[b][key] = jnp.pad(chip_major.astype(self.bufs[b][key].dtype), ((0, 0), (0, pad), (0, 0)))


def ffn_bwd(dh, g, w_in, w_out, saved, tag, sink, keys, layer, after, colsum=False):
    h, xn, u, hm = saved
    sink.mm(keys[1], layer, hm, dh, f"{tag}_dwout", scale=0.5)
    du = swiglu_out_bwd(dh, w_out, u, after, f"{tag}_dhm")
    sink.mm(keys[0], layer, xn, du, f"{tag}_dwin", paired=True)
    outs = nt_rms_bwd(du, ColSharded(w_in.arr, paired=True), h, _row(g), dh, f"{tag}_dxn", colsum=colsum)
    return (outs[0], outs[1].reshape(-1)) + ((outs[2],) if colsum else ())


def _hyb_params(w):
    d = w["conv_dw_b"].size
    inner = SSM_HEADS * HEAD_DIM
    main = 3 * d + w["ssm_conv_b"].size
    return dict(
        w_main=w["hyb_w_in"][:main], w_dt=jnp.pad(w["hyb_w_in"][main:], ((0, LANE - SSM_HEADS), (0, 0))),
        cw=jnp.pad(w["conv_dw_w"], ((0, 32 - CONV_WIDTH), (0, 0))), cb=_row(w["conv_dw_b"]),
        lg=_row(w["conv_ln_g"]), lb=_row(w["conv_ln_b"]),
        sw=jnp.pad(w["ssm_conv_w"], ((0, 8 - SSM_CONV), (0, 0))), sb=_row(w["ssm_conv_b"]),
        dtb=_pad_lanes(w["ssm_dt_bias"]), al_row=_pad_lanes(w["ssm_a_log"]), al_col=w["ssm_a_log"].reshape(-1, 1),
        dfull=_row(jnp.repeat(w["ssm_d"], HEAD_DIM)), gamma=_row(w["ssm_norm"]), d=d, inner=inner, main=main)


def hyb_fwd(h, w, tag):
    w.need("hyb_w_in", h)
    q = _hyb_params(w)
    proj, xn = rms_matmul(h, _row(w["norm_mix"]), q["w_main"], "nt", f"{tag}_in")
    dtr = matmul(xn, q["w_dt"], "nt", f"{tag}_in_dt")
    u, u1 = conv_group_fwd(proj, q["cw"], q["cb"], q["lg"], q["lb"], f"{tag}_conv")
    pre, xs, bc, dt = ssm_conv_fwd(proj, dtr, q["sw"], q["sb"], q["dtb"], f"{tag}_sconv")
    dtT = dt[:, :SSM_HEADS].T
    yssd, hs = ssd_fwd(xs, bc, dt, dtT, q["al_row"], q["al_col"], f"{tag}_ssd")
    y = ssm_gate_fwd(yssd, xs, proj, q["dfull"], q["gamma"], f"{tag}_gate")
    wo = w.need("hyb_w_out", u)
    h2 = matmul(u, wo[:q["d"]], "nn", f"{tag}_out_a", res=h)
    h2 = matmul(y, wo[q["d"]:], "nn", f"{tag}_out_b", res=h2)
    return h2, (h, xn, proj, u, u1, pre, xs, bc, dt, dtT, yssd, hs, y)


def hyb_bwd(dh, w, saved, tag, sink, layer):
    q = _hyb_params(w)
    h, xn, proj, u, u1, pre, xs, bc, dt, dtT, yssd, hs, y = saved
    du = matmul(dh, w["hyb_w_out"][:q["d"]], "nt", f"{tag}_du")
    dy3 = matmul(dh, w["hyb_w_out"][q["d"]:], "nt", f"{tag}_dy")
    sink.mm("hyb_w_out", layer, u, dh, f"{tag}_dwo_a", c0=0)
    sink.mm("hyb_w_out", layer, y, dh, f"{tag}_dwo_b", c0=N_CHIPS // 2)
    dproj, dcw, dcb, dlg, dlb = conv_group_bwd(du, u1, proj, q["cw"], q["lg"], q["lb"], f"{tag}_dconv")
    dyssd, dxs_skip, dproj, dgamma, dd = ssm_gate_bwd(dy3, yssd, xs, proj, q["dfull"], q["gamma"], dproj, f"{tag}_dgate")
    dxs, dbc, ddtr, dalog, ddtb = ssd_bwd(xs, bc, dt, dtT, q["al_row"], q["al_col"], hs, dyssd, dxs_skip, f"{tag}_dssd")
    dproj, dsw, dsb = ssm_conv_bwd(dxs, dbc, pre, proj, q["sw"], dproj, f"{tag}_dsconv")
    dw_in = jnp.concatenate([matmul(dproj, xn, "tn", f"{tag}_dwin"),
                             matmul(ddtr, xn, "tn", f"{tag}_dwin_dt")[:SSM_HEADS]], axis=0)
    sink.put("hyb_w_in", layer, dw_in.reshape((N_CHIPS, -1) + dw_in.shape[1:]))
    dh2, dg = nt_rms_bwd(dproj, q["w_main"], h, _row(w["norm_mix"]), dh, f"{tag}_dxn", extra=(ddtr, q["w_dt"]), b_kd=True)
    grads = dict(norm_mix=dg.reshape(-1), conv_dw_w=dcw[:CONV_WIDTH], conv_dw_b=dcb.reshape(-1),
                 conv_ln_g=dlg.reshape(-1), conv_ln_b=dlb.reshape(-1), ssm_conv_w=dsw[:SSM_CONV], ssm_conv_b=dsb.reshape(-1),
                 ssm_dt_bias=ddtb.reshape(-1), ssm_a_log=dalog.reshape(-1), ssm_d=dd.reshape(-1), ssm_norm=dgamma.reshape(-1))
    return dh2, grads


def rope_tables(n):
    half = HEAD_DIM // 2
    inv = ROPE_THETA ** (-jnp.arange(0, HEAD_DIM, 2, dtype=F32) / HEAD_DIM)
    ang = jnp.arange(n, dtype=F32)[:, None] * inv[None, :]
    cos, sin = jnp.cos(ang), jnp.sin(ang)
    reps = LANE // HEAD_DIM
    return jnp.tile(jnp.concatenate([cos, cos], axis=1), (1, reps)), jnp.tile(jnp.concatenate([-sin, sin], axis=1), (1, reps))


def att_fwd(h, w, tables, tag):
    cos, sin = tables
    qkv, xn = rms_matmul(h, _row(w["norm_mix"]), w.need("att_w_qkv", h), "nn", f"{tag}_qkv", bias=_row(w["att_b_qkv"]))
    q, k, v = rope_fwd(qkv, cos, sin, f"{tag}_rope")
    sinks = _pad_lanes(w["att_sinks"])
    o, lse = attn_fwd(q, k, v, sinks, f"{tag}_attn")
    h2 = matmul(o, w.need("att_w_o", o), "nn", f"{tag}_o", bias=_row(w["att_b_o"]), res=h)
    return h2, (h, xn, q, k, v, o, lse, sinks)


def att_bwd(dh, dh_colsum, w, saved, tables, tag, sink, layer):
    cos, sin = tables
    h, xn, q, k, v, o, lse, sinks = saved
    do = matmul(dh, w["att_w_o"], "nt", f"{tag}_do")
    sink.mm("att_w_o", layer, o, dh, f"{tag}_dwo")
    dq, dka, dkb, dva, dvb, dsk = attn_bwd(q, k, v, o, do, lse, sinks, f"{tag}_dattn")
    dqkv, dbqkv = attn_grad_merge(dq, dka, dkb, dva, dvb, cos, sin, f"{tag}_drope")
    sink.mm("att_w_qkv", layer, xn, dqkv, f"{tag}_dwqkv")
    dh2, dg = nt_rms_bwd(dqkv, w["att_w_qkv"], h, _row(w["norm_mix"]), dh, f"{tag}_dxn")
    grads = dict(norm_mix=dg.reshape(-1), att_b_qkv=dbqkv.reshape(-1), att_sinks=dsk[0, :ATT_HEADS],
                 att_b_o=dh_colsum.reshape(-1))
    return dh2, grads


def ple_block_fwd(h, pe, w, tag):
    pp = matmul(pe, w.need("ple_proj_w", h), "nn", f"{tag}_proj")
    out, gl, xn = ple_fwd(h, _row(w["ple_norm"]), w.need("ple_gate_w", h), pp, f"{tag}_gate")
    return out, (h, xn, gl, pp, pe)


def ple_block_bwd(dh, w, saved, tag, sink, layer, after):
    h, xn, gl, pp, pe = saved
    dh2, dpp, dgl, dg = ple_bwd(dh, gl, pp, w["ple_gate_w"], h, _row(w["ple_norm"]), after, f"{tag}_dgate")
    sink.mm("ple_proj_w", layer, pe, dpp, f"{tag}_dwp")
    sink.mm("ple_gate_w", layer, xn, dgl, f"{tag}_dwg")
    return dh2, dict(ple_norm=dg.reshape(-1))


PER_LAYER = ("norm_ffn1", "ffn1_w_in", "ffn1_w_out", "norm_mix", "norm_ffn2", "ffn2_w_in", "ffn2_w_out",
             "ple_norm", "ple_gate_w", "ple_proj_w")
EVEN_ONLY = ("hyb_w_in", "conv_dw_w", "conv_dw_b", "conv_ln_g", "conv_ln_b", "ssm_conv_w", "ssm_conv_b",
             "ssm_dt_bias", "ssm_a_log", "ssm_d", "ssm_norm", "hyb_w_out")
ODD_ONLY = ("att_w_qkv", "att_b_qkv", "att_sinks", "att_w_o", "att_b_o")


def _layer_index(k, i):
    if k in PER_LAYER:
        return i
    if k in (EVEN_ONLY if i % 2 == 0 else ODD_ONLY):
        return i // 2
    return None


def _stage(k):
    return 0 if k.startswith("ffn1") else (2 if k.startswith(("ffn2", "ple")) else 1)


def trunk_fwd_bwd(x, pe, target, layers, final_norm, sink, stage_done):
    depth = len(layers)
    tables = rope_tables(x.shape[0])
    h = x
    saved = []
    for i, w in enumerate(layers):
        h, s1 = ffn_fwd(h, w["norm_ffn1"], w, ("ffn1_w_in", "ffn1_w_out"), f"l{i}_ffn1")
        if i % 2 == 0:
            h, s2 = hyb_fwd(h, w, f"l{i}_hyb")
        else:
            h, s2 = att_fwd(h, w, tables, f"l{i}_att")
        h, s3 = ffn_fwd(h, w["norm_ffn2"], w, ("ffn2_w_in", "ffn2_w_out"), f"l{i}_ffn2")
        h, s4 = ple_block_fwd(h, pe[i], w, f"l{i}_ple")
        saved.append((s1, s2, s3, s4))
    dh, dgf, loss = loss_head(h, _row(final_norm), target, "loss_head")
    grads = {}
    tie = dgf
    for i in reversed(range(depth)):
        w = layers[i]
        s1, s2, s3, s4 = saved[i]
        dh, g = ple_block_bwd(dh, w, s4, f"l{i}_ple", sink, i, tie)
        odd = i % 2 == 1
        out = ffn_bwd(dh, w["norm_ffn2"], w["ffn2_w_in"], w["ffn2_w_out"], s3, f"l{i}_ffn2", sink,
                      ("ffn2_w_in", "ffn2_w_out"), i, tie, colsum=odd)
        dh = out[0]
        g.update(norm_ffn2=out[1])
        if odd:
            dh, gm = att_bwd(dh, out[2], w, s2, tables, f"l{i}_att", sink, i // 2)
        else:
            dh, gm = hyb_bwd(dh, w, s2, f"l{i}_hyb", sink, i // 2)
        g.update(gm)
        tie = stage_done(i, 1, tie)
        out = ffn_bwd(dh, w["norm_ffn1"], w["ffn1_w_in"], w["ffn1_w_out"], s1, f"l{i}_ffn1", sink,
                      ("ffn1_w_in", "ffn1_w_out"), i, tie)
        dh = out[0]
        g.update(norm_ffn1=out[1])
        tie = stage_done(i, 0, tie)
        for k, v in g.items():
            grads.setdefault(k, []).insert(0, v)
    grads = {k: jnp.stack(v) for k, v in grads.items()}
    grads["final_norm"] = dgf.reshape(-1)
    return loss, dh, grads


def _me():
    return lax.axis_index("x"), lax.axis_index("y"), lax.axis_index("c")


def _flip(v, f):
    return 1 - v if f else v


def _remote(src, dst, send_sems, recv_sems, k, dev):
    return pltpu.make_async_remote_copy(src_ref=src, dst_ref=dst, send_sem=send_sems.at[k], recv_sem=recv_sems.at[k],
                                        device_id=dev, device_id_type=MESH)


CHIP_FLIPS = ((1, 0), (0, 1), (1, 1))
DEV_FLIPS = tuple((fx, fy, fc) for fx in (0, 1) for fy in (0, 1) for fc in (0, 1))[1:]


HBM = pl.BlockSpec(memory_space=pltpu.HBM)
SEM = pl.BlockSpec(memory_space=pltpu.SEMAPHORE)
DATAFLOW = pltpu.SideEffectType.DATAFLOW_SIDE_EFFECTING


def _core_half(ref, axis, c):
    if axis is None:
        return ref
    h = ref.shape[axis] // 2
    return ref.at[pl.ds(c * h, h), :] if axis == 0 else ref.at[:, pl.ds(c * h, h)]


def gather_start(xs, lands, halves, after, name):
    na = len(xs)

    def body(*refs):
        x_refs, land_refs = refs[:na], refs[na:2 * na]
        send_sems, recv_sems = refs[2 * na + 1], refs[2 * na + 2]
        token = refs[-1]
        mx, my, mc = _me()
        chip = 2 * mx + my
        for a in range(na):
            for j, (fx, fy) in enumerate(CHIP_FLIPS):
                _remote(_core_half(x_refs[a], halves[a], mc), _core_half(land_refs[a].at[chip], halves[a], mc),
                        send_sems, recv_sems, 3 * a + j, (_flip(mx, fx), _flip(my, fy), mc)).start()
        token[...] = jnp.zeros_like(token)

    outs = pl.pallas_call(
        body, name=name,
        out_shape=(pltpu.SemaphoreType.DMA((3 * na,)), pltpu.SemaphoreType.DMA((3 * na,)))
        + tuple(pltpu.HBM(x.shape, x.dtype) for x in xs) + tuple(pltpu.HBM(l.shape, l.dtype) for l in lands)
        + (S((8, LANE), F32),),
        in_specs=[HBM] * (2 * na) + [pl.BlockSpec(memory_space=pl.ANY)],
        out_specs=(SEM, SEM) + (HBM,) * (2 * na) + (pl.BlockSpec(memory_space=pltpu.VMEM),),
        input_output_aliases={a: 2 + a for a in range(2 * na)},
        compiler_params=pltpu.CompilerParams(has_side_effects=DATAFLOW),
    )(*[pltpu.with_memory_space_constraint(t, pltpu.HBM) for t in list(xs) + list(lands)], after)
    return outs[0], outs[1], list(outs[2:2 + na]), list(outs[2 + na:2 + 2 * na])


def gather_wait(send_sems, recv_sems, xs, lands, halves, first, after, name):
    na = len(xs)

    def body(*refs):
        x_refs, land_refs = refs[:na], refs[na:2 * na]
        send_sems, recv_sems = refs[2 * na], refs[2 * na + 1]
        mx, my, mc = _me()
        for a in range(na):
            for j, (fx, fy) in enumerate(CHIP_FLIPS):
                px, py = _flip(mx, fx), _flip(my, fy)
                cp = _remote(_core_half(x_refs[a], halves[a], mc), _core_half(land_refs[a].at[2 * px + py], halves[a], mc),
                             send_sems, recv_sems, 3 * (first + a) + j, (px, py, mc))
                cp.wait_send()
                cp.wait_recv()

    outs = pl.pallas_call(
        body, name=name,
        out_shape=tuple(pltpu.HBM(x.shape, x.dtype) for x in xs) + tuple(pltpu.HBM(l.shape, l.dtype) for l in lands),
        in_specs=[HBM] * (2 * na) + [SEM, SEM, pl.BlockSpec(memory_space=pl.ANY)], out_specs=(HBM,) * (2 * na),
        input_output_aliases={a: a for a in range(2 * na)},
        compiler_params=pltpu.CompilerParams(has_side_effects=DATAFLOW),
    )(*xs, *lands, send_sems, recv_sems, after)
    return list(outs[na:])


def forward_halves(land, axis, name):
    def body(in_ref, out_ref, send_sems, recv_sems):
        del in_ref
        mx, my, mc = _me()
        sib = (mx, my, 1 - mc)
        slots = [2 * _flip(mx, fx) + _flip(my, fy) for fx, fy in CHIP_FLIPS]
        cps = [_remote(_core_half(out_ref.at[s], axis, mc), _core_half(out_ref.at[s], axis, mc), send_sems, recv_sems, j, sib)
               for j, s in enumerate(slots)]
        for cp in cps:
            cp.start()
        for j, s in enumerate(slots):
            _remote(_core_half(out_ref.at[s], axis, mc), _core_half(out_ref.at[s], axis, 1 - mc), send_sems, recv_sems, j, sib).wait_recv()
        for cp in cps:
            cp.wait_send()

    return pl.pallas_call(
        body, name=name, out_shape=S(land.shape, land.dtype), in_specs=[ANY], out_specs=ANY, input_output_aliases={0: 0},
        scratch_shapes=[pltpu.SemaphoreType.DMA((3,)), pltpu.SemaphoreType.DMA((3,))])(land)


def all_gather_devices(v, name):
    r, l = v.shape

    def body(v_ref, out_ref, send_sems, recv_sems):
        mx, my, mc = _me()
        me = 4 * mx + 2 * my + mc
        peers = [(_flip(mx, fx), _flip(my, fy), _flip(mc, fc)) for fx, fy, fc in DEV_FLIPS]
        sends = [_remote(v_ref, out_ref.at[me], send_sems, recv_sems, j, p) for j, p in enumerate(peers)]
        for cp in sends:
            cp.start()
        for j, (px, py, pc) in enumerate(peers):
            _remote(v_ref, out_ref.at[4 * px + 2 * py + pc], send_sems, recv_sems, j, (px, py, pc)).wait_recv()
        for cp in sends:
            cp.wait_send()

    out = pl.pallas_call(
        body, name=name, out_shape=S((N_DEV, r, l), v.dtype), in_specs=[ANY], out_specs=ANY,
        scratch_shapes=[pltpu.SemaphoreType.DMA((7,)), pltpu.SemaphoreType.DMA((7,))])(v)
    me = 4 * lax.axis_index("x") + 2 * lax.axis_index("y") + lax.axis_index("c")
    return lax.dynamic_update_slice_in_dim(out, v[None], me, axis=0)


def sum_devices(g8, name):
    nd, r, l = g8.shape
    tile = r
    for t in (512, 256, 128, 64, 32, 16, 8):
        if r % t == 0:
            tile = t
            break

    def body(g_ref, o_ref):
        acc = g_ref[0]
        for d in range(1, nd):
            acc = acc + g_ref[d]
        o_ref[...] = acc

    return _call(body, name, (r // tile,), [pl.BlockSpec((nd, tile, l), lambda i: (0, i, 0))], _rs(tile, l), S((r, l), F32),
                 sem=("parallel",))(g8)


def exchange_halves(gs, name):
    na = len(gs)
    nch = gs[0].shape[0]

    def body(*refs):
        g_refs, out_refs = refs[:na], refs[na:2 * na]
        send_sems, recv_sems = refs[2 * na:]
        mx, my, mc = _me()
        sib = (mx, my, 1 - mc)
        cps = []
        for a in range(na):
            half = gs[a].shape[1] // 2
            for j in range(nch):
                cps.append(_remote(g_refs[a].at[j, pl.ds((1 - mc) * half, half), :], out_refs[a].at[j],
                                   send_sems, recv_sems, nch * a + j, sib))
        for cp in cps:
            cp.start()
        for cp in cps:
            cp.wait_recv()
        for cp in cps:
            cp.wait_send()

    return pl.pallas_call(
        body, name=name, out_shape=[S((nch, g.shape[1] // 2, g.shape[2]), g.dtype) for g in gs],
        in_specs=[ANY] * na, out_specs=[ANY] * na,
        scratch_shapes=[pltpu.SemaphoreType.DMA((nch * na,)), pltpu.SemaphoreType.DMA((nch * na,))])(*gs)


def add_halves(g4, got, name):
    nch, r, l = g4.shape
    half = r // 2
    tile = _pick_rows(half)
    nt = half // tile

    def body(g_ref, r_ref, a_ref, own_ref):
        j = pl.program_id(1)
        chip = 2 * lax.axis_index("x") + lax.axis_index("y")
        val = g_ref[0].astype(F32) + r_ref[0].astype(F32)
        a_ref[0] = val.astype(a_ref.dtype)

        @pl.when(j == chip)
        def _():
            own_ref[...] = val

    return pl.pallas_call(
        body, name=name, grid=(nt, nch),
        in_specs=[pl.BlockSpec((1, tile, l), lambda i, j: (j, lax.axis_index("c") * nt + i, 0)),
                  pl.BlockSpec((1, tile, l), lambda i, j: (j, i, 0))],
        out_specs=[pl.BlockSpec((1, tile, l), lambda i, j: (j, i, 0)), pl.BlockSpec((tile, l), lambda i, j: (i, 0))],
        out_shape=[S((nch, half, l), MXU_DTYPE), S((half, l), F32)],
        compiler_params=pltpu.CompilerParams(dimension_semantics=("parallel", "arbitrary"), vmem_limit_bytes=VMEM_LIMIT))(g4, got)


def _pick_rows(r, cap=640):
    return next((t for t in range(cap - cap % 16, 15, -16) if r % t == 0), r)


def add_chips(own, got, name):
    h, l = own.shape
    tile = _pick_rows(h)

    def body(o_ref, g_ref, out_ref):
        out_ref[...] = ((o_ref[...] + g_ref[0].astype(F32)) + g_ref[1].astype(F32)) + g_ref[2].astype(F32)

    nt = h // tile
    return _call(body, name, (nt,), [_rs(tile, l), pl.BlockSpec((3, tile, l), lambda i: (0, i, 0))],
                 pl.BlockSpec((tile, l), lambda i: (lax.axis_index("c") * nt + i, 0)),
                 S((2 * h, l), F32), sem=("parallel",))(own, got)


def join_halves(bufs, name):
    na = len(bufs)

    def body(*refs):
        out_refs = refs[na:2 * na]
        send_sems, recv_sems = refs[2 * na:]
        mx, my, mc = _me()
        sib = (mx, my, 1 - mc)

        def half(a, hc):
            h = bufs[a].shape[0] // 2
            return out_refs[a].at[pl.ds(hc * h, h), :]

        cps = [_remote(half(a, mc), half(a, mc), send_sems, recv_sems, a, sib) for a in range(na)]
        for cp in cps:
            cp.start()
        for a in range(na):
            _remote(half(a, mc), half(a, 1 - mc), send_sems, recv_sems, a, sib).wait_recv()
        for cp in cps:
            cp.wait_send()

    return pl.pallas_call(
        body, name=name, out_shape=[S(b.shape, b.dtype) for b in bufs], in_specs=[ANY] * na, out_specs=[ANY] * na,
        input_output_aliases={a: a for a in range(na)},
        scratch_shapes=[pltpu.SemaphoreType.DMA((na,)), pltpu.SemaphoreType.DMA((na,))])(*bufs)


def exchange_chips_start(parts, name):
    na = len(parts)
    lands = [lax.empty((3,) + p.shape[1:], p.dtype) for p in parts]

    def body(*refs):
        a_refs, land_refs = refs[:na], refs[na:2 * na]
        send_sems, recv_sems = refs[2 * na], refs[2 * na + 1]
        mx, my, mc = _me()
        for j, (fx, fy) in enumerate(CHIP_FLIPS):
            px, py = _flip(mx, fx), _flip(my, fy)
            for a in range(na):
                _remote(a_refs[a].at[2 * px + py], land_refs[a].at[j], send_sems, recv_sems, 3 * a + j, (px, py, mc)).start()
        refs[-1][...] = jnp.zeros_like(refs[-1])

    outs = pl.pallas_call(
        body, name=name,
        out_shape=(pltpu.SemaphoreType.DMA((3 * na,)), pltpu.SemaphoreType.DMA((3 * na,)))
        + tuple(pltpu.HBM(t.shape, t.dtype) for t in list(parts) + lands) + (S((8, LANE), F32),),
        in_specs=[HBM] * (2 * na), out_specs=(SEM, SEM) + (HBM,) * (2 * na) + (pl.BlockSpec(memory_space=pltpu.VMEM),),
        input_output_aliases={a: 2 + a for a in range(2 * na)},
        compiler_params=pltpu.CompilerParams(has_side_effects=DATAFLOW),
    )(*[pltpu.with_memory_space_constraint(t, pltpu.HBM) for t in list(parts) + lands])
    return outs[0], outs[1], list(outs[2:2 + na]), list(outs[2 + na:2 + 2 * na]), outs[-1]


def exchange_chips_wait(send_sems, recv_sems, parts, lands, after, name):
    na = len(parts)

    def body(*refs):
        a_refs, land_refs = refs[:na], refs[na:2 * na]
        send_sems, recv_sems = refs[2 * na], refs[2 * na + 1]
        mx, my, mc = _me()
        for j, (fx, fy) in enumerate(CHIP_FLIPS):
            px, py = _flip(mx, fx), _flip(my, fy)
            for a in range(na):
                cp = _remote(a_refs[a].at[2 * px + py], land_refs[a].at[j], send_sems, recv_sems, 3 * a + j, (px, py, mc))
                cp.wait_send()
                cp.wait_recv()

    outs = pl.pallas_call(
        body, name=name, out_shape=tuple(pltpu.HBM(t.shape, t.dtype) for t in list(parts) + list(lands)),
        in_specs=[HBM] * (2 * na) + [SEM, SEM, pl.BlockSpec(memory_space=pl.ANY)], out_specs=(HBM,) * (2 * na),
        input_output_aliases={a: a for a in range(2 * na)},
        compiler_params=pltpu.CompilerParams(has_side_effects=DATAFLOW),
    )(*parts, *lands, send_sems, recv_sems, after)
    return list(outs[na:])


def reduce_begin(gs, tag):
    got = exchange_halves(gs, f"{tag}_d2d")
    sums = [add_halves(g, r, f"{tag}_add1_{i}") for i, (g, r) in enumerate(zip(gs, got))]
    return [own for _, own in sums], exchange_chips_start([a for a, _ in sums], f"{tag}_ici_start")


def reduce_end(state, after, tag):
    owns, (send_sems, recv_sems, parts, lands, _) = state
    got = exchange_chips_wait(send_sems, recv_sems, parts, lands, after, f"{tag}_ici_wait")
    return [add_chips(own, r, f"{tag}_add2_{i}") for i, (own, r) in enumerate(zip(owns, got))]


PACK_L = 1024


def _pack(arrs, dtype, row_mult, lead=None):
    lead_shape = () if lead is None else arrs[0].shape[:lead]
    flat = jnp.concatenate([a.astype(dtype).reshape(lead_shape + (-1,)) for a in arrs], axis=-1)
    n = flat.shape[-1]
    unit = row_mult * PACK_L
    total = -(-n // unit) * unit
    flat = jnp.pad(flat, [(0, 0)] * len(lead_shape) + [(0, total - n)])
    return flat.reshape(lead_shape + (total // PACK_L, PACK_L))


def _unpack(packed, shapes, lead=None):
    lead_shape = () if lead is None else packed.shape[:lead]
    flat = packed.reshape(lead_shape + (-1,))
    out, off = [], 0
    for shp in shapes:
        n = int(np.prod(shp))
        out.append(flat[..., off:off + n].reshape(lead_shape + tuple(shp)))
        off += n
    return out


def _to_full(gathered, axis):
    t = jnp.moveaxis(gathered, 0, axis)
    shp = t.shape
    return t.reshape(shp[:axis] + (shp[axis] * shp[axis + 1],) + shp[axis + 2:])


WEIGHTS = ("norm_ffn1", "ffn1_w_in", "ffn1_w_out", "norm_mix", "norm_ffn2", "ffn2_w_in", "ffn2_w_out", "ple_norm",
           "ple_gate_w", "ple_proj_w", "hyb_w_in", "conv_dw_w", "conv_dw_b", "conv_ln_g", "conv_ln_b", "ssm_conv_w",
           "ssm_conv_b", "ssm_dt_bias", "ssm_a_log", "ssm_d", "ssm_norm", "hyb_w_out", "att_w_qkv", "att_b_qkv",
           "att_sinks", "att_w_o", "att_b_o", "final_norm")
SHARD_AXIS = dict(ffn1_w_in=2, ffn1_w_out=1, ffn2_w_in=2, ffn2_w_out=1, ple_gate_w=1, ple_proj_w=2, hyb_w_in=2,
                  conv_dw_w=2, ssm_conv_w=2, hyb_w_out=1, att_w_qkv=2, att_b_qkv=1, att_w_o=1, att_b_o=1)
BIG = ("ffn1_w_in", "ffn1_w_out", "ffn2_w_in", "ffn2_w_out", "ple_gate_w", "ple_proj_w", "hyb_w_in", "hyb_w_out",
       "att_w_qkv", "att_w_o")
TRANSPOSED = ("hyb_w_in",)
FORWARDED = (0, 2)


def _shard_axis(k):
    return 1 if k in TRANSPOSED else SHARD_AXIS[k]
SMALL_SHARDED = ("conv_dw_w", "ssm_conv_w", "att_b_qkv", "att_b_o")
SMALL = tuple(k for k in WEIGHTS if k not in BIG)


def _step(x, p, target, w, m, v):
    mx, my = lax.axis_index("x"), lax.axis_index("y")
    chip = 2 * mx + my
    w, m, v = ({k: (a.transpose(0, 2, 1) if k in TRANSPOSED else a) for k, a in d.items()} for d in (w, m, v))

    depth = w["norm_ffn1"].shape[0]
    order = sorted([(k, i) for i in range(depth) for k in BIG if _layer_index(k, i) is not None],
                   key=lambda t: (t[1], _stage(t[0])))
    small_g = all_gather_devices(_pack([w[k] for k in SMALL_SHARDED], F32, 8), "gather_small")
    shards = [w[k][_layer_index(k, i)].astype(MXU_DTYPE) for k, i in order]
    lands = [lax.dynamic_update_slice_in_dim(lax.empty((N_CHIPS,) + s.shape, s.dtype), s[None], chip, axis=0) for s in shards]
    halves = [(0 if s.shape[0] % 32 == 0 else 1) if pos in FORWARDED else None for pos, s in enumerate(shards)]
    send_sems, recv_sems, shards, lands = gather_start(shards, lands, halves, small_g, "gather_start")

    def fetch(i, k, after):
        p = order.index((k, i))
        g, = gather_wait(send_sems, recv_sems, [shards[p]], [lands[p]], [halves[p]], p, after, f"gather_wait_l{i}_{k}")
        if halves[p] is not None:
            g = forward_halves(g, halves[p], f"gather_forward_l{i}_{k}")
        if _shard_axis(k) == 2:
            return ColSharded(g)
        return g.reshape(-1, g.shape[-1])

    small_g = small_g[0::2]
    small_full = {k: _to_full(g, SHARD_AXIS[k])
                  for k, g in zip(SMALL_SHARDED, _unpack(small_g, [w[k].shape for k in SMALL_SHARDED], lead=1))}
    layers = [LayerWeights({k: small_full.get(k, w[k])[_layer_index(k, i)] for k in SMALL if _layer_index(k, i) is not None},
                           functools.partial(fetch, i)) for i in range(depth)]

    def bucket_of(layer, stage):
        return (layer, 0) if layer > 0 else (0, min(stage, 1))

    sink = GradSink({k: w[k].shape for k in BIG}, bucket_of)
    begun = {}

    def stage_done(i, stage, tie):
        b = bucket_of(i, stage)
        if stage > 0 and bucket_of(i, stage - 1) == b:
            return tie
        begun[b] = reduce_begin(list(sink.bufs[b].values()), f"grads_l{b[0]}_{b[1]}")
        return begun[b][1][-1]

    loss, dx, grads = trunk_fwd_bwd(x[0], p[:, 0], target[0], layers, w["final_norm"], sink, stage_done)

    results = {}

    def finish(buckets, after, tag):
        halves = {b: reduce_end(begun[b], after, f"grads_l{b[0]}_{b[1]}") for b in buckets}
        joined = iter(join_halves([h for b in buckets for h in halves[b]], f"grads_join_{tag}"))
        reduced = {b: {c: next(joined) for c in sink.bufs[b]} for b in buckets}
        last = after
        for (k, li), (b, key, _, off, r) in sink.where.items():
            if b in buckets:
                g = reduced[b][key]
                if r % 8:
                    g, off = g[off:off + r], 0
                results[k] = adamw_layer(w[k], g, off, m[k], v[k], li, results.get(k), f"adamw_{k}_{li}")
                last = results[k][1]
        return last

    order_b = list(begun)
    started_last = begun[order_b[-1]][1][-1]
    finish(order_b[-1:], finish(order_b[:-1], started_last, "early") if len(order_b) > 1 else dx, "last")
    g_out = {k: results[k][0] for k in BIG}
    vec = _pack([loss[0:1, 0:1]] + [grads[k] for k in SMALL], F32, 8)
    vec = sum_devices(all_gather_devices(vec, "gather_vectors"), "sum_vectors")
    parts = _unpack(vec, [(1, 1)] + [grads[k].shape for k in SMALL])
    loss_out = parts[0].reshape(())
    for k, g in zip(SMALL, parts[1:]):
        if k in SHARD_AXIS:
            ax = SHARD_AXIS[k]
            g = lax.dynamic_slice_in_dim(g, chip * w[k].shape[ax], w[k].shape[ax], axis=ax)
        g_out[k] = g

    for k in TRANSPOSED:
        results[k] = [a.transpose(0, 2, 1) for a in results[k]]
    g_out.update({k: results[k][0] for k in TRANSPOSED})
    delta, new_m, new_v = ({k: results[k][j] for k in BIG} for j in (1, 2, 3))
    shapes = [w[k].shape for k in SMALL]
    packed = [_pack([src[k] for k in SMALL], F32, 8) for src in (w, g_out, m, v)]
    outs = adamw(*packed, "adamw_small")
    for dst, o in zip((delta, new_m, new_v), outs):
        for k, a in zip(SMALL, _unpack(o, shapes)):
            dst[k] = a
    return ((loss_out, dx[None]) + tuple(g_out[k] for k in WEIGHTS) + tuple(delta[k] for k in WEIGHTS)
            + tuple(new_m[k] for k in WEIGHTS) + tuple(new_v[k] for k in WEIGHTS))


def kernel(x, p, norm_ffn1, ffn1_w_in, ffn1_w_out, norm_mix, norm_ffn2, ffn2_w_in, ffn2_w_out, ple_norm, ple_gate_w, ple_proj_w, hyb_w_in, conv_dw_w, conv_dw_b, conv_ln_g, conv_ln_b, ssm_conv_w, ssm_conv_b, ssm_dt_bias, ssm_a_log, ssm_d, ssm_norm, hyb_w_out, att_w_qkv, att_b_qkv, att_sinks, att_w_o, att_b_o, final_norm, loss_target, m_norm_ffn1, m_ffn1_w_in, m_ffn1_w_out, m_norm_mix, m_norm_ffn2, m_ffn2_w_in, m_ffn2_w_out, m_ple_norm, m_ple_gate_w, m_ple_proj_w, m_hyb_w_in, m_conv_dw_w, m_conv_dw_b, m_conv_ln_g, m_conv_ln_b, m_ssm_conv_w, m_ssm_conv_b, m_ssm_dt_bias, m_ssm_a_log, m_ssm_d, m_ssm_norm, m_hyb_w_out, m_att_w_qkv, m_att_b_qkv, m_att_sinks, m_att_w_o, m_att_b_o, m_final_norm, v_norm_ffn1, v_ffn1_w_in, v_ffn1_w_out, v_norm_mix, v_norm_ffn2, v_ffn2_w_in, v_ffn2_w_out, v_ple_norm, v_ple_gate_w, v_ple_proj_w, v_hyb_w_in, v_conv_dw_w, v_conv_dw_b, v_conv_ln_g, v_conv_ln_b, v_ssm_conv_w, v_ssm_conv_b, v_ssm_dt_bias, v_ssm_a_log, v_ssm_d, v_ssm_norm, v_hyb_w_out, v_att_w_qkv, v_att_b_qkv, v_att_sinks, v_att_w_o, v_att_b_o, v_final_norm):
    given = locals()
    w = {k: given[k] for k in WEIGHTS}
    m = {k: given["m_" + k] for k in WEIGHTS}
    v = {k: given["v_" + k] for k in WEIGHTS}
    return _step(x, p, loss_target, w, m, v)
```

```python
import functools

import numpy as np
import jax
import jax.numpy as jnp
from jax import lax
from jax.experimental import pallas as pl
from jax.experimental.pallas import tpu as pltpu

F32 = jnp.float32
BF16 = jnp.bfloat16
MXU_DTYPE = jnp.bfloat16
S = jax.ShapeDtypeStruct
MESH = pl.DeviceIdType.MESH

VMEM_LIMIT = 48 * 2**20
LANE = 128

EPS = 1e-6
SSM_HEADS = 16
HEAD_DIM = 64
SSM_GROUPS = 2
SSM_STATE = 128
SSM_CONV = 4
CHUNK = 128
CONV_WIDTH = 31
ATT_HEADS = 16
ATT_KV_HEADS = 4
WINDOW = 128
ROPE_THETA = 10000.0
ADAM_LR = 0.001
ADAM_B1 = 0.9
ADAM_B2 = 0.999
ADAM_EPS = 1e-08
ADAM_WD = 0.01
ADAM_STEP = 10

N_CHIPS = 4
N_DEV = 8

NN = ((1,), (0,))
NT = ((1,), (1,))
TN = ((0,), (0,))


def _mm(a, b, dims=NN):
    return lax.dot_general(a.astype(MXU_DTYPE), b.astype(MXU_DTYPE), (dims, ((), ())), preferred_element_type=F32)


def _split3(a):
    hi = a.astype(BF16)
    r = a - hi.astype(F32)
    mid = r.astype(BF16)
    lo = (r - mid.astype(F32)).astype(BF16)
    return hi, mid, lo


def _mm01(a, onehot, dims=NN):
    o = onehot.astype(BF16)
    out = None
    for part in _split3(a):
        t = lax.dot_general(part, o, (dims, ((), ())), preferred_element_type=F32)
        out = t if out is None else out + t
    return out


def _01mm(onehot, a):
    o = onehot.astype(BF16)
    out = None
    for part in _split3(a):
        t = lax.dot_general(o, part, (NN, ((), ())), preferred_element_type=F32)
        out = t if out is None else out + t
    return out


def _sigmoid(x):
    return 0.5 * jnp.tanh(0.5 * x) + 0.5


def _softplus(x):
    return jnp.maximum(x, 0.0) + jnp.log(1.0 + jnp.exp(-jnp.abs(x)))


def _iota(shape, axis):
    return lax.broadcasted_iota(jnp.int32, shape, axis)


def _head_indicator(width, heads, transposed=False):
    per = width // heads
    if transposed:
        return (_iota((heads, width), 1) // per == _iota((heads, width), 0)).astype(F32)
    return (_iota((width, heads), 0) // per == _iota((width, heads), 1)).astype(F32)


def _acc(ref, i, val):
    @pl.when(i == 0)
    def _():
        ref[...] = val

    @pl.when(i > 0)
    def _():
        ref[...] += val


def _rs(tile, width, col=0, shift=0, n=None):
    if shift == 0:
        return pl.BlockSpec((tile, width), lambda i: (i, col))
    if shift < 0:
        return pl.BlockSpec((tile, width), lambda i: (jnp.maximum(i - 1, 0), col))
    return pl.BlockSpec((tile, width), lambda i: (jnp.minimum(i + 1, n - 1), col))


def _ps(shape):
    return pl.BlockSpec(shape, lambda i: (0,) * len(shape))


def _call(body, name, grid, in_specs, out_specs, out_shape, scratch=(), sem=None):
    return pl.pallas_call(
        body, name=name, grid=grid, in_specs=in_specs, out_specs=out_specs, out_shape=out_shape,
        scratch_shapes=list(scratch),
        compiler_params=pltpu.CompilerParams(dimension_semantics=sem, vmem_limit_bytes=VMEM_LIMIT))


def _row_tile(n, target):
    t = min(n, target)
    assert n % t == 0, (n, t)
    return t


def _pick_tile(dim, target):
    if dim <= target:
        return dim
    t = (int(1.4 * target) // LANE) * LANE
    while t >= LANE:
        if dim % t == 0:
            return t
        t -= LANE
    return dim


ANY = pl.BlockSpec(memory_space=pl.ANY)


def _paired(j):
    return (j % 2) * 2 + j // 2


class ColSharded:
    def __init__(self, arr, paired=False):
        self.arr, self.paired = arr, paired
        self.nch, self.rows, self.per = arr.shape
        self.shape = (self.rows, self.nch * self.per)

    def chip(self, j):
        return _paired(j) if self.paired else j


class Slot:
    def __init__(self, buf, kind, per, off, c0=0, paired=False):
        self.buf, self.kind, self.per, self.off, self.c0, self.paired = buf, kind, per, off, c0, paired

    def chip(self, j):
        return _paired(j) if self.paired else j


def matmul(a, b, mode, name, *, out_dtype=F32, scale=None, res=None, bias=None, into=None, tm=1024, tn=1024, tk=1024):
    bshape = b.shape
    if mode == "nn":
        (m, k), (k2, n) = a.shape, bshape
    elif mode == "nt":
        (m, k), (n, k2) = a.shape, bshape
    else:
        (k, m), (k2, n) = a.shape, bshape
    assert k == k2, (a.shape, bshape, mode)
    tm, tn, tk = _pick_tile(m, tm), _pick_tile(n, tn), _pick_tile(k, tk)
    if isinstance(b, ColSharded):
        if mode == "nn":
            tn = b.per
        else:
            assert mode == "nt"
            tk = b.per
    if into is not None:
        if into.kind == "c":
            tn = into.per
            assert into.off % tm == 0 and n == N_CHIPS * into.per
        else:
            tm = max(1, min(m, int(1.4 * 1024)) // into.per) * into.per
            assert m % tm == 0 and into.off % into.per == 0 and into.c0 % (tm // into.per) == 0
    nk = k // tk
    dims = {"nn": NN, "nt": NT, "tn": TN}[mode]
    a_spec = (pl.BlockSpec((tk, tm), lambda i, j, kk: (kk, i)) if mode == "tn"
              else pl.BlockSpec((tm, tk), lambda i, j, kk: (i, kk)))
    if isinstance(b, ColSharded):
        bchip = b.chip
        b_spec = (pl.BlockSpec((None, tk, tn), lambda i, j, kk: (bchip(j), kk, 0)) if mode == "nn"
                  else pl.BlockSpec((None, tn, tk), lambda i, j, kk: (bchip(kk), j, 0)))
        b = b.arr
    else:
        b_spec = (pl.BlockSpec((tn, tk), lambda i, j, kk: (j, kk)) if mode == "nt"
                  else pl.BlockSpec((tk, tn), lambda i, j, kk: (kk, j)))
    plain_o = pl.BlockSpec((tm, tn), lambda i, j, kk: (i, j))
    ins, in_specs = [a, b], [a_spec, b_spec]
    if bias is not None:
        ins.append(bias)
        in_specs.append(pl.BlockSpec((1, tn), lambda i, j, kk: (0, j)))
    if res is not None:
        ins.append(res)
        in_specs.append(plain_o)
    aliases = {}
    if into is None:
        o_spec, o_shape = plain_o, S((m, n), out_dtype)
    else:
        aliases = {len(ins): 0}
        ins.append(into.buf)
        in_specs.append(ANY)
        o_shape = S(into.buf.shape, into.buf.dtype)
        if into.kind == "c":
            ob, ochip = into.off // tm, into.chip
            o_spec = pl.BlockSpec((None, tm, tn), lambda i, j, kk: (ochip(j), ob + i, 0))
        else:
            q, ob = tm // into.per, into.off // into.per
            cb = into.c0 // q
            o_spec = pl.BlockSpec((q, into.per, tn), lambda i, j, kk: (cb + i, ob, j))

    def body(*refs):
        a_ref, b_ref = refs[0], refs[1]
        o_ref, acc_ref = refs[-2], refs[-1]
        kk = pl.program_id(2)

        @pl.when(kk == 0)
        def _():
            acc_ref[...] = jnp.zeros_like(acc_ref)

        acc_ref[...] += _mm(a_ref[...], b_ref[...], dims)

        @pl.when(kk == nk - 1)
        def _():
            out = acc_ref[...]
            if scale is not None:
                out = out * scale
            pos = 2
            if bias is not None:
                out = out + refs[pos][...]
                pos += 1
            if res is not None:
                out = out + refs[pos][...]
            o_ref[...] = out.astype(o_ref.dtype).reshape(o_ref.shape)

    return pl.pallas_call(
        body, name=name, grid=(m // tm, n // tn, nk), in_specs=in_specs, out_specs=o_spec, out_shape=o_shape,
        scratch_shapes=[pltpu.VMEM((tm, tn), F32)], input_output_aliases=aliases,
        compiler_params=pltpu.CompilerParams(dimension_semantics=("parallel", "parallel", "arbitrary"),
                                             vmem_limit_bytes=VMEM_LIMIT))(*ins)


def rms_matmul(h, g, b, mode, name, bias=None):
    n, d = h.shape
    sharded = isinstance(b, ColSharded)
    n_out = b.shape[1] if mode == "nn" else b.shape[0]
    tm = _row_tile(n, 512)
    tn = b.per if sharded else _pick_tile(n_out, 1024)
    if sharded:
        assert mode == "nn"
        bchip = b.chip
        b_spec = pl.BlockSpec((None, d, tn), lambda i, j: (bchip(j), 0, 0))
        b = b.arr
    elif mode == "nn":
        b_spec = pl.BlockSpec((d, tn), lambda i, j: (0, j))
    else:
        b_spec = pl.BlockSpec((tn, d), lambda i, j: (j, 0))
    ins = [h, g, b] + ([bias] if bias is not None else [])
    in_specs = [pl.BlockSpec((tm, d), lambda i, j: (i, 0)), pl.BlockSpec((1, d), lambda i, j: (0, 0)), b_spec]
    if bias is not None:
        in_specs.append(pl.BlockSpec((1, tn), lambda i, j: (0, j)))

    def body(h_ref, g_ref, b_ref, *refs):
        o_ref, xn_ref = refs[-2:]
        x = h_ref[...]
        r = lax.rsqrt(jnp.mean(x * x, axis=-1, keepdims=True) + EPS)
        xn = (x * r * g_ref[...]).astype(xn_ref.dtype)

        @pl.when(pl.program_id(1) == 0)
        def _():
            xn_ref[...] = xn

        out = _mm(xn, b_ref[...], NN if mode == "nn" else NT)
        o_ref[...] = out if bias is None else out + refs[0][...]

    return pl.pallas_call(
        body, name=name, grid=(n // tm, n_out // tn), in_specs=in_specs,
        out_specs=[pl.BlockSpec((tm, tn), lambda i, j: (i, j)), pl.BlockSpec((tm, d), lambda i, j: (i, 0))],
        out_shape=[S((n, n_out), F32), S((n, d), MXU_DTYPE)],
        compiler_params=pltpu.CompilerParams(dimension_semantics=("parallel", "arbitrary"), vmem_limit_bytes=VMEM_LIMIT),
    )(*ins)


def _rms_bwd_math(x, g, dy):
    r = lax.rsqrt(jnp.mean(x * x, axis=-1, keepdims=True) + EPS)
    xh = x * r
    dg = jnp.sum(dy * xh, axis=0, keepdims=True)
    dxh = dy * g
    dx = r * (dxh - xh * jnp.mean(dxh * xh, axis=-1, keepdims=True))
    return dx, dg


def nt_rms_bwd(a, b, h, g, dh_in, name, extra=None, colsum=False, b_kd=False):
    n, k = a.shape
    d = h.shape[1]
    tm = _row_tile(n, 512)
    sharded = isinstance(b, ColSharded)
    tk = b.per if sharded else _pick_tile(k, 1024)
    nk = k // tk
    dims = NN if b_kd else NT
    if sharded:
        bchip = b.chip
        b_spec = pl.BlockSpec((None, d, tk), lambda i, kk: (bchip(kk), 0, 0))
        b = b.arr
    elif b_kd:
        b_spec = pl.BlockSpec((tk, d), lambda i, kk: (kk, 0))
    else:
        b_spec = pl.BlockSpec((d, tk), lambda i, kk: (0, kk))
    row = pl.BlockSpec((tm, d), lambda i, kk: (i, 0))
    vec = pl.BlockSpec((1, d), lambda i, kk: (0, 0))
    ins, in_specs = [a, b, h, g, dh_in], [pl.BlockSpec((tm, tk), lambda i, kk: (i, kk)), b_spec, row, vec, row]
    if extra is not None:
        k2 = extra[0].shape[1]
        ins += list(extra)
        in_specs += [pl.BlockSpec((tm, k2), lambda i, kk: (i, 0)),
                     pl.BlockSpec((k2, d) if b_kd else (d, k2), lambda i, kk: (0, 0))]
    n_in = len(ins)

    def body(*refs):
        a_ref, b_ref, h_ref, g_ref, dh_ref = refs[:5]
        o_ref, dg_ref = refs[n_in], refs[n_in + 1]
        acc_ref = refs[-1]
        i, kk = pl.program_id(0), pl.program_id(1)

        @pl.when(kk == 0)
        def _():
            acc_ref[...] = _mm(refs[5][...], refs[6][...], dims) if extra is not None else jnp.zeros_like(acc_ref)

        acc_ref[...] += _mm(a_ref[...], b_ref[...], dims)

        @pl.when(kk == nk - 1)
        def _():
            dx, dg = _rms_bwd_math(h_ref[...], g_ref[...], acc_ref[...])
            out = dh_ref[...] + dx
            o_ref[...] = out
            _acc(dg_ref, i, dg)
            if colsum:
                _acc(refs[n_in + 2], i, jnp.sum(out, axis=0, keepdims=True))

    n_vec = 2 if colsum else 1
    return pl.pallas_call(
        body, name=name, grid=(n // tm, nk), in_specs=in_specs, out_specs=[row] + [vec] * n_vec,
        out_shape=[S((n, d), F32)] + [S((1, d), F32)] * n_vec, scratch_shapes=[pltpu.VMEM((tm, d), F32)],
        compiler_params=pltpu.CompilerParams(dimension_semantics=("arbitrary", "arbitrary"), vmem_limit_bytes=VMEM_LIMIT),
    )(*ins)


def swiglu_in(h, g, w_in, name):
    n, d = h.shape
    per = w_in.per
    nj = w_in.nch // 2
    tile = _row_tile(n, 512)

    def body(h_ref, g_ref, wg_ref, wu_ref, xn_ref, u_ref, hm_ref):
        x = h_ref[...]
        r = lax.rsqrt(jnp.mean(x * x, axis=-1, keepdims=True) + EPS)
        xn = (x * r * g_ref[...]).astype(xn_ref.dtype)

        @pl.when(pl.program_id(1) == 0)
        def _():
            xn_ref[...] = xn

        a = _mm(xn, wg_ref[...])
        b = _mm(xn, wu_ref[...])
        u_ref[:, :per] = a.astype(u_ref.dtype)
        u_ref[:, per:] = b.astype(u_ref.dtype)
        hm_ref[...] = (a * _sigmoid(a) * b).astype(hm_ref.dtype)

    return pl.pallas_call(
        body, name=name, grid=(n // tile, nj),
        in_specs=[pl.BlockSpec((tile, d), lambda i, j: (i, 0)), pl.BlockSpec((1, d), lambda i, j: (0, 0)),
                  pl.BlockSpec((None, d, per), lambda i, j: (j, 0, 0)), pl.BlockSpec((None, d, per), lambda i, j: (nj + j, 0, 0))],
        out_specs=[pl.BlockSpec((tile, d), lambda i, j: (i, 0)), pl.BlockSpec((tile, 2 * per), lambda i, j: (i, j)),
                   pl.BlockSpec((tile, per), lambda i, j: (i, j))],
        out_shape=[S((n, d), MXU_DTYPE), S((n, 2 * nj * per), MXU_DTYPE), S((n, nj * per), MXU_DTYPE)],
        compiler_params=pltpu.CompilerParams(dimension_semantics=("parallel", "arbitrary"), vmem_limit_bytes=VMEM_LIMIT),
    )(h, g, w_in.arr, w_in.arr)


def swiglu_out_bwd(dh, w_out, u, after, name):
    n, d = dh.shape
    f = w_out.shape[0]
    per = u.shape[1] // 4
    nj = f // per
    tile = _row_tile(n, 512)

    def body(dh_ref, w_ref, u_ref, after_ref, du_ref):
        dm = 0.5 * _mm(dh_ref[...], w_ref[...], NT)
        a = u_ref[:, :per].astype(F32)
        b = u_ref[:, per:].astype(F32)
        s = _sigmoid(a)
        du_ref[:, :per] = (dm * b * s * (1.0 + a * (1.0 - s))).astype(du_ref.dtype)
        du_ref[:, per:] = (dm * a * s).astype(du_ref.dtype)

    return pl.pallas_call(
        body, name=name, grid=(n // tile, nj),
        in_specs=[pl.BlockSpec((tile, d), lambda i, j: (i, 0)), pl.BlockSpec((per, d), lambda i, j: (j, 0)),
                  pl.BlockSpec((tile, 2 * per), lambda i, j: (i, j)), ANY],
        out_specs=pl.BlockSpec((tile, 2 * per), lambda i, j: (i, j)),
        out_shape=S(u.shape, MXU_DTYPE),
        compiler_params=pltpu.CompilerParams(dimension_semantics=("parallel", "parallel"), vmem_limit_bytes=VMEM_LIMIT),
    )(dh, w_out, u, after)


def ple_fwd(h, g, w_gate, pp, name):
    n, d = h.shape
    tile = _row_tile(n, 512)

    def body(h_ref, g_ref, w_ref, pp_ref, o_ref, gl_ref, xn_ref):
        x = h_ref[...]
        r = lax.rsqrt(jnp.mean(x * x, axis=-1, keepdims=True) + EPS)
        xn = (x * r * g_ref[...]).astype(xn_ref.dtype)
        xn_ref[...] = xn
        gl = _mm(xn, w_ref[...])
        gl_ref[...] = gl
        o_ref[...] = x + _sigmoid(gl) * pp_ref[...]

    return _call(body, name, (n // tile,), [_rs(tile, d), _ps((1, d)), _ps(w_gate.shape), _rs(tile, d)],
                 [_rs(tile, d)] * 3, [S((n, d), F32), S((n, d), F32), S((n, d), MXU_DTYPE)], sem=("parallel",))(h, g, w_gate, pp)


def ple_bwd(dh, gl, pp, w_gate, h, g, after, name):
    n, d = dh.shape
    tile = _row_tile(n, 512)

    def body(dh_ref, gl_ref, pp_ref, w_ref, h_ref, g_ref, after_ref, o_ref, dpp_ref, dgl_ref, dg_ref):
        i = pl.program_id(0)
        s = _sigmoid(gl_ref[...])
        dh_ = dh_ref[...]
        dpp_ref[...] = (dh_ * s).astype(dpp_ref.dtype)
        dgl = (dh_ * pp_ref[...] * s * (1.0 - s)).astype(dgl_ref.dtype)
        dgl_ref[...] = dgl
        dx, dg = _rms_bwd_math(h_ref[...], g_ref[...], _mm(dgl, w_ref[...], NT))
        o_ref[...] = dh_ + dx
        _acc(dg_ref, i, dg)

    return _call(body, name, (n // tile,),
                 [_rs(tile, d)] * 3 + [_ps(w_gate.shape), _rs(tile, d), _ps((1, d)), ANY],
                 [_rs(tile, d)] * 3 + [_ps((1, d))],
                 [S((n, d), F32), S((n, d), MXU_DTYPE), S((n, d), MXU_DTYPE), S((1, d), F32)],
                 sem=("arbitrary",))(dh, gl, pp, w_gate, h, g, after)


def loss_head(h, g, target, name):
    n, d = h.shape
    tile = _row_tile(n, 256)

    def body(h_ref, g_ref, t_ref, dh_ref, dg_ref, loss_ref):
        i = pl.program_id(0)
        x = h_ref[...]
        gg = g_ref[...]
        r = lax.rsqrt(jnp.mean(x * x, axis=-1, keepdims=True) + EPS)
        err = x * r * gg - t_ref[...]
        part = 0.5 * jnp.sum(jnp.mean(err * err, axis=-1, keepdims=True), axis=0, keepdims=True)
        dx, dg = _rms_bwd_math(x, gg, err * (1.0 / d))
        dh_ref[...] = dx
        _acc(dg_ref, i, dg)
        _acc(loss_ref, i, jnp.broadcast_to(part, (8, LANE)))

    return _call(body, name, (n // tile,), [_rs(tile, d), _ps((1, d)), _rs(tile, d)],
                 [_rs(tile, d), _ps((1, d)), _ps((8, LANE))], [S((n, d), F32), S((1, d), F32), S((8, LANE), F32)],
                 sem=("arbitrary",))(h, g, target)


def _adamw_math(w, g, m, v):
    c1 = np.float32(1.0 - ADAM_B1 ** ADAM_STEP)
    c2 = np.float32(1.0 - ADAM_B2 ** ADAM_STEP)
    mm = ADAM_B1 * m + (1.0 - ADAM_B1) * g
    vv = ADAM_B2 * v + (1.0 - ADAM_B2) * (g * g)
    return -ADAM_LR * ((mm / c1) / (jnp.sqrt(vv / c2) + ADAM_EPS) + ADAM_WD * w), mm, vv


def adamw_layer(w, pack, off, m, v, li, prev, name):
    n, r, c = w.shape

    def body(w_ref, g_ref, m_ref, v_ref, *refs):
        go_ref, d_ref, mo_ref, vo_ref = refs[-4:]
        g = g_ref[...]
        go_ref[...] = g
        d_ref[...], mo_ref[...], vo_ref[...] = _adamw_math(w_ref[...], g, m_ref[...], v_ref[...])

    if r % 8 == 0:
        cap = 2**21 // (4 * c) // 8 * 8
        tile = next(t for t in range(min(cap, r), 7, -8) if r % t == 0 and off % t == 0)
        ob, steps = off // tile, r // tile
        blk = pl.BlockSpec((None, tile, c), lambda i: (li, i, 0))
        g_spec = pl.BlockSpec((tile, c), lambda i: (ob + i, 0))
    else:
        assert off == 0 and pack.shape[0] == r and c % (2 * LANE) == 0
        steps = c // (2 * LANE)
        blk = pl.BlockSpec((None, r, 2 * LANE), lambda i: (li, 0, i))
        g_spec = pl.BlockSpec((r, 2 * LANE), lambda i: (0, i))
    prev = list(prev) if prev is not None else []
    return pl.pallas_call(
        body, name=name, grid=(steps,),
        in_specs=[blk, g_spec, blk, blk] + [ANY] * len(prev),
        out_specs=[blk] * 4, out_shape=[S((n, r, c), F32)] * 4,
        input_output_aliases={4 + j: j for j in range(len(prev))},
        compiler_params=pltpu.CompilerParams(dimension_semantics=("parallel",), vmem_limit_bytes=VMEM_LIMIT),
    )(w, pack, m, v, *prev)


def adamw(w, g, m, v, name):
    r, c = w.shape
    tile = r
    for t in (512, 256, 128, 64, 32, 16, 8):
        if r % t == 0 and t * c * 4 <= 2**21:
            tile = t
            break

    def body(w_ref, g_ref, m_ref, v_ref, d_ref, mo_ref, vo_ref):
        d_ref[...], mo_ref[...], vo_ref[...] = _adamw_math(w_ref[...], g_ref[...], m_ref[...], v_ref[...])

    return _call(body, name, (r // tile,), [_rs(tile, c)] * 4, [_rs(tile, c)] * 3, [S((r, c), F32)] * 3,
                 sem=("parallel",))(w, g, m, v)


TAP_VREGS = 32


def _taps(src, w_ref, offsets, tile, put, bias=None):
    c = src.shape[1]
    rp = max(8, TAP_VREGS * 8 * LANE // c // 8 * 8)
    for r0 in range(0, tile, rp):
        acc = jnp.zeros((rp, c), F32) if bias is None else jnp.zeros((rp, c), F32) + bias
        for k, o in enumerate(offsets):
            acc = acc + w_ref[k:k + 1, :] * src[r0 + o:r0 + o + rp, :]
        put(slice(r0, r0 + rp), acc)


def _taps_fwd(sc, w_ref, width, halo, tile, put, bias):
    _taps(sc, w_ref, [halo - (width - 1) + k for k in range(width)], tile, put, bias)


def _taps_bwd_x(sc_d, w_ref, width, tile, put):
    _taps(sc_d, w_ref, [(width - 1) - k for k in range(width)], tile, put)


def _taps_bwd_w(dy, sc, dw_ref, width, halo, tile, i):
    @pl.when(i == 0)
    def _():
        dw_ref[...] = jnp.zeros_like(dw_ref)

    for k in range(width):
        o = halo - (width - 1) + k
        dw_ref[k:k + 1, :] += jnp.sum(dy * sc[o:o + tile, :], axis=0, keepdims=True)


def _ln_stats(x):
    mu = jnp.mean(x, axis=-1, keepdims=True)
    xc = x - mu
    r = lax.rsqrt(jnp.mean(xc * xc, axis=-1, keepdims=True) + EPS)
    return xc * r, r


def conv_group_fwd(proj, cw, cb, lg, lb, name):
    n = proj.shape[0]
    d = cw.shape[1]
    tile = _row_tile(n, 256)
    halo = 32

    def body(v_ref, g_ref, vp_ref, gp_ref, cw_ref, cb_ref, lg_ref, lb_ref, u_ref, u1_ref, sc):
        i = pl.program_id(0)
        first = (i > 0).astype(F32)
        sc[0:halo, :] = vp_ref[tile - halo:, :] * _sigmoid(gp_ref[tile - halo:, :]) * first
        sc[halo:, :] = v_ref[...] * _sigmoid(g_ref[...])
        def put(rows, acc):
            u1_ref[rows, :] = acc

        _taps_fwd(sc, cw_ref, CONV_WIDTH, halo, tile, put, cb_ref[...])
        xh, _ = _ln_stats(u1_ref[...])
        y = xh * lg_ref[...] + lb_ref[...]
        u_ref[...] = (y * _sigmoid(y)).astype(u_ref.dtype)

    return _call(body, name, (n // tile,),
                 [_rs(tile, d, 0), _rs(tile, d, 1), _rs(tile, d, 0, -1), _rs(tile, d, 1, -1),
                  _ps(cw.shape), _ps((1, d)), _ps((1, d)), _ps((1, d))],
                 [_rs(tile, d), _rs(tile, d)], [S((n, d), MXU_DTYPE), S((n, d), F32)],
                 scratch=[pltpu.VMEM((halo + tile, d), F32)], sem=("arbitrary",))(proj, proj, proj, proj, cw, cb, lg, lb)


def conv_group_bwd(du, u1, proj, cw, lg, lb, name):
    n = proj.shape[0]
    d = cw.shape[1]
    tile = _row_tile(n, 256)
    halo = 32
    nt = n // tile

    def body(du_ref, dun_ref, u1_ref, u1n_ref, v_ref, g_ref, vp_ref, gp_ref, cw_ref, lg_ref, lb_ref,
             dp_ref, dcw_ref, dcb_ref, dlg_ref, dlb_ref, sc, sc_d):
        i = pl.program_id(0)

        def ln_swish_bwd(dy_, u1_):
            xh, r = _ln_stats(u1_)
            y = xh * lg_ref[...] + lb_ref[...]
            s = _sigmoid(y)
            dyy = dy_ * s * (1.0 + y * (1.0 - s))
            dxh = dyy * lg_ref[...]
            dx = r * (dxh - jnp.mean(dxh, axis=-1, keepdims=True) - xh * jnp.mean(dxh * xh, axis=-1, keepdims=True))
            return dx, jnp.sum(dyy * xh, axis=0, keepdims=True), jnp.sum(dyy, axis=0, keepdims=True)

        du1, dlg, dlb = ln_swish_bwd(du_ref[...].astype(F32), u1_ref[...])
        du1n, _, _ = ln_swish_bwd(dun_ref[0:halo, :].astype(F32), u1n_ref[0:halo, :])
        sc_d[0:tile, :] = du1
        sc_d[tile:, :] = du1n * (i < nt - 1).astype(F32)
        sc[0:halo, :] = vp_ref[tile - halo:, :] * _sigmoid(gp_ref[tile - halo:, :]) * (i > 0).astype(F32)
        sc[halo:, :] = v_ref[...] * _sigmoid(g_ref[...])

        def put(rows, du0):
            sig = _sigmoid(g_ref[rows, :])
            dp_ref[rows, :d] = (du0 * sig).astype(dp_ref.dtype)
            dp_ref[rows, d:] = (du0 * v_ref[rows, :] * sig * (1.0 - sig)).astype(dp_ref.dtype)

        _taps_bwd_x(sc_d, cw_ref, CONV_WIDTH, tile, put)
        _taps_bwd_w(du1, sc, dcw_ref, CONV_WIDTH, halo, tile, i)
        _acc(dcb_ref, i, jnp.sum(du1, axis=0, keepdims=True))
        _acc(dlg_ref, i, dlg)
        _acc(dlb_ref, i, dlb)

    return _call(body, name, (nt,),
                 [_rs(tile, d), _rs(tile, d, 0, 1, nt), _rs(tile, d), _rs(tile, d, 0, 1, nt),
                  _rs(tile, d, 0), _rs(tile, d, 1), _rs(tile, d, 0, -1), _rs(tile, d, 1, -1),
                  _ps(cw.shape), _ps((1, d)), _ps((1, d))],
                 [_rs(tile, 2 * d), _ps(cw.shape), _ps((1, d)), _ps((1, d)), _ps((1, d))],
                 [S((n, proj.shape[1]), MXU_DTYPE), S(cw.shape, F32), S((1, d), F32), S((1, d), F32), S((1, d), F32)],
                 scratch=[pltpu.VMEM((halo + tile, d), F32), pltpu.VMEM((tile + halo, d), F32)],
                 sem=("arbitrary",))(du, du, u1, u1, proj, proj, proj, proj, cw, lg, lb)


def ssm_conv_fwd(proj, dtr, sw, sb, dtb, name):
    n = proj.shape[0]
    w = sw.shape[1]
    inner = SSM_HEADS * HEAD_DIM
    tile = _row_tile(n, 256)
    halo = 8

    def body(x_ref, xp_ref, dtr_ref, sw_ref, sb_ref, dtb_ref, pre_ref, xs_ref, bc_ref, dt_ref, sc):
        i = pl.program_id(0)
        sc[0:halo, :] = xp_ref[tile - halo:, :] * (i > 0).astype(F32)
        sc[halo:, :] = x_ref[...]
        def put(rows, acc):
            pre_ref[rows, :] = acc

        _taps_fwd(sc, sw_ref, SSM_CONV, halo, tile, put, sb_ref[...])
        pre = pre_ref[...]
        act = pre * _sigmoid(pre)
        xs_ref[...] = act[:, :inner]
        bc_ref[...] = act[:, inner:]
        dt = _softplus(dtr_ref[...] + dtb_ref[...])
        dt_ref[...] = jnp.where(_iota(dt.shape, 1) < SSM_HEADS, dt, 0.0)

    return _call(body, name, (n // tile,),
                 [_rs(tile, w, 2), _rs(tile, w, 2, -1), _rs(tile, LANE), _ps(sw.shape), _ps((1, w)), _ps((1, LANE))],
                 [_rs(tile, w), _rs(tile, inner), _rs(tile, w - inner), _rs(tile, LANE)],
                 [S((n, w), F32), S((n, inner), F32), S((n, w - inner), F32), S((n, LANE), F32)],
                 scratch=[pltpu.VMEM((halo + tile, w), F32)], sem=("arbitrary",))(proj, proj, dtr, sw, sb, dtb)


def ssm_conv_bwd(dxs, dbc, pre, proj, sw, dproj, name):
    n = proj.shape[0]
    w = sw.shape[1]
    inner = SSM_HEADS * HEAD_DIM
    tile = _row_tile(n, 256)
    halo = 8
    nt = n // tile

    def body(dxs_ref, dxsn_ref, dbc_ref, dbcn_ref, pre_ref, pren_ref, x_ref, xp_ref, sw_ref, dp_in_ref,
             dx_ref, dsw_ref, dsb_ref, sc, sc_d):
        i = pl.program_id(0)

        def silu_bwd(d_, p_):
            s = _sigmoid(p_)
            return d_ * s * (1.0 + p_ * (1.0 - s))

        sc_d[0:tile, :inner] = silu_bwd(dxs_ref[...], pre_ref[:, :inner])
        sc_d[0:tile, inner:] = silu_bwd(dbc_ref[...], pre_ref[:, inner:])
        last = (i < nt - 1).astype(F32)
        sc_d[tile:, :inner] = silu_bwd(dxsn_ref[0:halo, :], pren_ref[0:halo, :inner]) * last
        sc_d[tile:, inner:] = silu_bwd(dbcn_ref[0:halo, :], pren_ref[0:halo, inner:]) * last
        sc[0:halo, :] = xp_ref[tile - halo:, :] * (i > 0).astype(F32)
        sc[halo:, :] = x_ref[...]
        dpre = sc_d[0:tile, :]
        def put(rows, acc):
            dx_ref[rows, :] = acc.astype(dx_ref.dtype)

        _taps_bwd_x(sc_d, sw_ref, SSM_CONV, tile, put)
        _taps_bwd_w(dpre, sc, dsw_ref, SSM_CONV, halo, tile, i)
        _acc(dsb_ref, i, jnp.sum(dpre, axis=0, keepdims=True))

    return pl.pallas_call(
        body, name=name, grid=(nt,),
        in_specs=[_rs(tile, inner), _rs(tile, inner, 0, 1, nt), _rs(tile, w - inner), _rs(tile, w - inner, 0, 1, nt),
                  _rs(tile, w), _rs(tile, w, 0, 1, nt), _rs(tile, w, 2), _rs(tile, w, 2, -1), _ps(sw.shape), ANY],
        out_specs=[_rs(tile, w, 2), _ps(sw.shape), _ps((1, w))],
        out_shape=[S(dproj.shape, dproj.dtype), S(sw.shape, F32), S((1, w), F32)],
        scratch_shapes=[pltpu.VMEM((halo + tile, w), F32), pltpu.VMEM((tile + halo, w), F32)],
        input_output_aliases={9: 0},
        compiler_params=pltpu.CompilerParams(dimension_semantics=("arbitrary",), vmem_limit_bytes=VMEM_LIMIT),
    )(dxs, dxs, dbc, dbc, pre, pre, proj, proj, sw, dproj)


def _ssd_prologue(dt_ref, dtT_ref, al_ref, alc_ref):
    row = _iota((CHUNK, CHUNK), 0)
    col = _iota((CHUNK, CHUNK), 1)
    dt = dt_ref[:, :SSM_HEADS]
    a_row = -jnp.exp(al_ref[:, :SSM_HEADS])
    a_col = -jnp.exp(alc_ref[...])
    cs = _01mm((row >= col).astype(F32), dt * a_row)
    csT = _mm01(dtT_ref[...] * a_col, (row <= col).astype(F32))
    return dt, a_row, cs, csT, row, col


def _decay(cs, csT, h, row, col):
    lm = jnp.exp(jnp.where(row >= col, cs[:, h:h + 1] - csT[h:h + 1, :], -1e30))
    lmT = jnp.exp(jnp.where(col >= row, csT[h:h + 1, :] - cs[:, h:h + 1], -1e30))
    return lm, lmT


def ssd_fwd(xs, bc, dt, dtT, alog_row, alog_col, name):
    n, width = xs.shape
    nc = n // CHUNK
    gw = width // SSM_GROUPS
    hpg = SSM_HEADS // SSM_GROUPS
    ns = SSM_STATE

    def body(xs_ref, bc_ref, dt_ref, dtT_ref, al_ref, alc_ref, y_ref, hs_ref, h_sc):
        i = pl.program_id(0)

        @pl.when(i == 0)
        def _():
            h_sc[...] = jnp.zeros_like(h_sc)

        dt, a_row, cs, csT, row, col = _ssd_prologue(dt_ref, dtT_ref, al_ref, alc_ref)
        indT = _head_indicator(width, SSM_HEADS, transposed=True)
        dt_full = _mm01(dt, indT)
        e_full = jnp.exp(_mm01(cs, indT))
        dte_full = jnp.exp(_mm01(cs[CHUNK - 1:CHUNK, :] - cs, indT))
        xt = xs_ref[...] * dt_full
        hs_ref[0] = h_sc[...]
        lo = _iota((CHUNK, 2 * HEAD_DIM), 1) < HEAD_DIM
        for g in range(SSM_GROUPS):
            bg = bc_ref[:, g * ns:(g + 1) * ns]
            cg = bc_ref[:, (SSM_GROUPS + g) * ns:(SSM_GROUPS + g + 1) * ns]
            gm = _mm(cg, bg, NT)
            hg = h_sc[g * gw:(g + 1) * gw, :]
            yoff = e_full[:, g * gw:(g + 1) * gw] * _mm(cg, hg, NT)
            for pr in range(hpg // 2):
                h0 = g * hpg + 2 * pr
                c0 = h0 * HEAD_DIM
                xp = xt[:, c0:c0 + 2 * HEAD_DIM]
                m0 = gm * _decay(cs, csT, h0, row, col)[0]
                m1 = gm * _decay(cs, csT, h0 + 1, row, col)[0]
                yd = jnp.where(lo, _mm(m0, xp), _mm(m1, xp))
                y_ref[:, c0:c0 + 2 * HEAD_DIM] = yd + yoff[:, 2 * pr * HEAD_DIM:(2 * pr + 2) * HEAD_DIM]
            sg = _mm(xt[:, g * gw:(g + 1) * gw] * dte_full[:, g * gw:(g + 1) * gw], bg, TN)
            for hh in range(hpg):
                h = g * hpg + hh
                r0 = h * HEAD_DIM
                h_sc[r0:r0 + HEAD_DIM, :] = (h_sc[r0:r0 + HEAD_DIM, :] * jnp.exp(csT[h:h + 1, CHUNK - 1:CHUNK])
                                             + sg[hh * HEAD_DIM:(hh + 1) * HEAD_DIM, :])

    bcw = bc.shape[1]
    return _call(body, name, (nc,),
                 [_rs(CHUNK, width), _rs(CHUNK, bcw), _rs(CHUNK, LANE), pl.BlockSpec((SSM_HEADS, CHUNK), lambda i: (0, i)),
                  _ps((1, LANE)), _ps((SSM_HEADS, 1))],
                 [_rs(CHUNK, width), pl.BlockSpec((1, width, ns), lambda i: (i, 0, 0))],
                 [S((n, width), F32), S((nc, width, ns), F32)],
                 scratch=[pltpu.VMEM((width, ns), F32)], sem=("arbitrary",))(xs, bc, dt, dtT, alog_row, alog_col)


def ssd_bwd(xs, bc, dt, dtT, alog_row, alog_col, hs, dy, dxs_skip, name):
    n, width = xs.shape
    nc = n // CHUNK
    gw = width // SSM_GROUPS
    hpg = SSM_HEADS // SSM_GROUPS
    ns = SSM_STATE
    bcw = bc.shape[1]

    def body(xs_ref, bc_ref, dt_ref, dtT_ref, al_ref, alc_ref, hs_ref, dy_ref, skip_ref,
             dxs_ref, dbc_ref, ddtr_ref, dal_ref, ddtb_ref, dh_sc, dxt_sc):
        i = pl.program_id(0)

        @pl.when(i == 0)
        def _():
            dh_sc[...] = jnp.zeros_like(dh_sc)

        dt, a_row, cs, csT, row, col = _ssd_prologue(dt_ref, dtT_ref, al_ref, alc_ref)
        indT = _head_indicator(width, SSM_HEADS, transposed=True)
        ind = _head_indicator(width, SSM_HEADS)
        dt_full = _mm01(dt, indT)
        e_full = jnp.exp(_mm01(cs, indT))
        cs_last = cs[CHUNK - 1:CHUNK, :]
        dte = jnp.exp(cs_last - cs)
        dte_full = _mm01(dte, indT)
        xs_ = xs_ref[...]
        xt = xs_ * dt_full
        dy_ = dy_ref[...]
        hmat = hs_ref[0]
        ds = dh_sc[...]
        lo = _iota((CHUNK, 2 * HEAD_DIM), 1) < HEAD_DIM
        head_lane = _iota((1, SSM_HEADS), 1)
        dcs = jnp.zeros((CHUNK, SSM_HEADS), F32)
        ddte = jnp.zeros((CHUNK, SSM_HEADS), F32)
        for g in range(SSM_GROUPS):
            sl = slice(g * gw, (g + 1) * gw)
            bg = bc_ref[:, g * ns:(g + 1) * ns]
            cg = bc_ref[:, (SSM_GROUPS + g) * ns:(SSM_GROUPS + g + 1) * ns]
            gm = _mm(cg, bg, NT)
            gmT = _mm(bg, cg, NT)
            hg = hmat[sl, :]
            dsg = ds[sl, :]
            dyg = dy_[:, sl]
            xtg = xt[:, sl]
            yoff = e_full[:, sl] * _mm(cg, hg, NT)
            edy = e_full[:, sl] * dyg
            bds = _mm(bg, dsg, NT)
            dxt_g = dte_full[:, sl] * bds
            ddte = ddte + _mm01(xtg * bds, ind[sl, :])
            dcs = dcs + _mm01(dyg * yoff, ind[sl, :])
            db = _mm(xtg * dte_full[:, sl], dsg)
            dc = _mm(edy, hg)
            dhc = _mm(edy, cg, TN)
            dgs = jnp.zeros((CHUNK, CHUNK), F32)
            dgTs = jnp.zeros((CHUNK, CHUNK), F32)
            for pr in range(hpg // 2):
                h0 = g * hpg + 2 * pr
                c0 = 2 * pr * HEAD_DIM
                xp = xtg[:, c0:c0 + 2 * HEAD_DIM]
                dyp = dyg[:, c0:c0 + 2 * HEAD_DIM]
                rr = []
                for h, half in ((h0, lo), (h0 + 1, jnp.logical_not(lo))):
                    lm, lmT = _decay(cs, csT, h, row, col)
                    xm = jnp.where(half, xp, 0.0)
                    dm = _mm(dyp, xm, NT)
                    dmT = _mm(xm, dyp, NT)
                    mT = gmT * lmT
                    z = jnp.sum(dm * (gm * lm), axis=1, keepdims=True) - jnp.sum(dmT * mT, axis=1, keepdims=True)
                    dcs = dcs + z * (head_lane == h).astype(F32)
                    dgs = dgs + dm * lm
                    dgTs = dgTs + dmT * lmT
                    rr.append(_mm(mT, dyp))
                dxt_sc[:, g * gw + c0:g * gw + c0 + 2 * HEAD_DIM] = jnp.where(lo, rr[0], rr[1]) + dxt_g[:, c0:c0 + 2 * HEAD_DIM]
            dbc_ref[:, g * ns:(g + 1) * ns] = db + _mm(dgTs, cg)
            dbc_ref[:, (SSM_GROUPS + g) * ns:(SSM_GROUPS + g + 1) * ns] = dc + _mm(dgs, bg)
            for hh in range(hpg):
                h = g * hpg + hh
                r0 = h * HEAD_DIM
                dh_sc[r0:r0 + HEAD_DIM, :] = (dhc[hh * HEAD_DIM:(hh + 1) * HEAD_DIM, :]
                                              + jnp.exp(csT[h:h + 1, CHUNK - 1:CHUNK]) * ds[r0:r0 + HEAD_DIM, :])
        t = ddte * dte
        per_head = jnp.sum(jnp.sum(ds * hmat, axis=1, keepdims=True) * ind, axis=0, keepdims=True)
        last_add = jnp.sum(t, axis=0, keepdims=True) + jnp.exp(cs_last) * per_head
        dcs = dcs - t + jnp.where(_iota((CHUNK, SSM_HEADS), 0) == CHUNK - 1, last_add, 0.0)
        dadt = _01mm((row <= col).astype(F32), dcs)
        dxt = dxt_sc[...]
        ddt = dadt * a_row + _mm01(dxt * xs_, ind)
        dxs_ref[...] = dxt * dt_full + skip_ref[...]
        ddtr = ddt * (1.0 - jnp.exp(-dt))
        ddtr_ref[...] = jnp.zeros_like(ddtr_ref)
        ddtr_ref[:, :SSM_HEADS] = ddtr.astype(ddtr_ref.dtype)
        _acc(dal_ref, i, jnp.sum(dadt * dt, axis=0, keepdims=True) * a_row)
        _acc(ddtb_ref, i, jnp.sum(ddtr, axis=0, keepdims=True))

    rev = lambda i: (nc - 1 - i, 0)
    return _call(body, name, (nc,),
                 [pl.BlockSpec((CHUNK, width), rev), pl.BlockSpec((CHUNK, bcw), rev), pl.BlockSpec((CHUNK, LANE), rev),
                  pl.BlockSpec((SSM_HEADS, CHUNK), lambda i: (0, nc - 1 - i)), _ps((1, LANE)), _ps((SSM_HEADS, 1)),
                  pl.BlockSpec((1, width, ns), lambda i: (nc - 1 - i, 0, 0)), pl.BlockSpec((CHUNK, width), rev),
                  pl.BlockSpec((CHUNK, width), rev)],
                 [pl.BlockSpec((CHUNK, width), rev), pl.BlockSpec((CHUNK, bcw), rev), pl.BlockSpec((CHUNK, LANE), rev),
                  _ps((1, SSM_HEADS)), _ps((1, SSM_HEADS))],
                 [S((n, width), F32), S((n, bcw), F32), S((n, LANE), MXU_DTYPE), S((1, SSM_HEADS), F32), S((1, SSM_HEADS), F32)],
                 scratch=[pltpu.VMEM((width, ns), F32), pltpu.VMEM((CHUNK, width), F32)],
                 sem=("arbitrary",))(xs, bc, dt, dtT, alog_row, alog_col, hs, dy, dxs_skip)


def ssm_gate_fwd(yssd, xs, proj, dfull, gamma, name):
    n, d = yssd.shape
    tile = _row_tile(n, 256)
    gw = d // SSM_GROUPS

    def body(y_ref, xs_ref, z_ref, df_ref, gm_ref, o_ref):
        z = z_ref[...]
        y2 = (y_ref[...] + df_ref[...] * xs_ref[...]) * (z * _sigmoid(z))
        for g in range(SSM_GROUPS):
            yg = y2[:, g * gw:(g + 1) * gw]
            r = lax.rsqrt(jnp.mean(yg * yg, axis=-1, keepdims=True) + EPS)
            o_ref[:, g * gw:(g + 1) * gw] = (yg * r * gm_ref[:, g * gw:(g + 1) * gw]).astype(o_ref.dtype)

    return _call(body, name, (n // tile,), [_rs(tile, d), _rs(tile, d), _rs(tile, d, 2), _ps((1, d)), _ps((1, d))],
                 _rs(tile, d), S((n, d), MXU_DTYPE), sem=("parallel",))(yssd, xs, proj, dfull, gamma)


def ssm_gate_bwd(dy3, yssd, xs, proj, dfull, gamma, dproj, name):
    n, d = yssd.shape
    tile = _row_tile(n, 256)
    gw = d // SSM_GROUPS

    def body(dy_ref, y_ref, xs_ref, z_ref, df_ref, gm_ref, dp_in_ref, dys_ref, dxs_ref, dz_ref, dgm_ref, dd_ref):
        i = pl.program_id(0)
        z = z_ref[...]
        s = _sigmoid(z)
        xs_ = xs_ref[...]
        y1 = y_ref[...] + df_ref[...] * xs_
        y2 = y1 * (z * s)
        dy_ = dy_ref[...].astype(F32)
        dgm = []
        dy2 = []
        for g in range(SSM_GROUPS):
            sl = slice(g * gw, (g + 1) * gw)
            dxg, dgg = _rms_bwd_math(y2[:, sl], gm_ref[:, sl], dy_[:, sl])
            dy2.append(dxg)
            dgm.append(dgg)
        dy2 = jnp.concatenate(dy2, axis=1)
        dy1 = dy2 * (z * s)
        dys_ref[...] = dy1
        dxs_ref[...] = dy1 * df_ref[...]
        dz_ref[...] = (dy2 * y1 * s * (1.0 + z * (1.0 - s))).astype(dz_ref.dtype)
        _acc(dgm_ref, i, jnp.concatenate(dgm, axis=1))
        colsum = jnp.broadcast_to(jnp.sum(dy1 * xs_, axis=0, keepdims=True), (8, d))
        _acc(dd_ref, i, _mm01(colsum, _head_indicator(d, SSM_HEADS))[0:1, :])

    return pl.pallas_call(
        body, name=name, grid=(n // tile,),
        in_specs=[_rs(tile, d), _rs(tile, d), _rs(tile, d), _rs(tile, d, 2), _ps((1, d)), _ps((1, d)), ANY],
        out_specs=[_rs(tile, d), _rs(tile, d), _rs(tile, d, 2), _ps((1, d)), _ps((1, SSM_HEADS))],
        out_shape=[S((n, d), F32), S((n, d), F32), S(dproj.shape, dproj.dtype), S((1, d), F32), S((1, SSM_HEADS), F32)],
        input_output_aliases={6: 2},
        compiler_params=pltpu.CompilerParams(dimension_semantics=("arbitrary",), vmem_limit_bytes=VMEM_LIMIT),
    )(dy3, yssd, xs, proj, dfull, gamma, dproj)


def _rope128(x, cos, sin_signed):
    half = HEAD_DIM // 2
    lane = _iota(x.shape, 1)
    partner = jnp.where((lane % HEAD_DIM) < half, pltpu.roll(x, LANE - half, 1), pltpu.roll(x, half, 1))
    return x * cos + partner * sin_signed


def rope_fwd(qkv, cos, sin, name):
    n, w = qkv.shape
    qw = ATT_HEADS * HEAD_DIM
    kw = ATT_KV_HEADS * HEAD_DIM
    tile = _row_tile(n, 256)

    def body(x_ref, c_ref, s_ref, q_ref, k_ref, v_ref):
        c, s = c_ref[...], s_ref[...]
        for j in range(qw // LANE):
            q_ref[:, j * LANE:(j + 1) * LANE] = _rope128(x_ref[:, j * LANE:(j + 1) * LANE], c, s).astype(q_ref.dtype)
        for j in range(kw // LANE):
            k_ref[:, j * LANE:(j + 1) * LANE] = _rope128(x_ref[:, qw + j * LANE:qw + (j + 1) * LANE], c, s).astype(k_ref.dtype)
        v_ref[...] = x_ref[:, qw + kw:].astype(v_ref.dtype)

    return _call(body, name, (n // tile,), [_rs(tile, w), _rs(tile, LANE), _rs(tile, LANE)],
                 [_rs(tile, qw), _rs(tile, kw), _rs(tile, kw)],
                 [S((n, qw), MXU_DTYPE), S((n, kw), MXU_DTYPE), S((n, kw), MXU_DTYPE)], sem=("parallel",))(qkv, cos, sin)


ATT_GROUP = ATT_HEADS // ATT_KV_HEADS


def _attn_mask(i):
    row = _iota((ATT_GROUP * WINDOW, 2 * WINDOW), 0) % WINDOW
    s = _iota((ATT_GROUP * WINDOW, 2 * WINDOW), 1)
    return (s > row) & (s <= row + WINDOW) & ((s >= WINDOW) | (i > 0))


def _stack_heads(ref, j, kh, lo):
    parts = []
    for t in range(ATT_GROUP):
        h = ATT_GROUP * j + t
        blk = ref[:, (h // 2) * LANE:(h // 2 + 1) * LANE]
        blk = jnp.where(lo if h % 2 == 0 else jnp.logical_not(lo), blk, jnp.zeros_like(blk))
        parts.append(blk if h % 2 == kh else pltpu.roll(blk, HEAD_DIM, 1))
    return jnp.concatenate(parts, axis=0)


def _unstack_heads(stacked, j, kh, lo, put):
    for t in range(0, ATT_GROUP, 2):
        h = ATT_GROUP * j + t
        even = stacked[t * WINDOW:(t + 1) * WINDOW, :]
        odd = stacked[(t + 1) * WINDOW:(t + 2) * WINDOW, :]
        even = even if kh == 0 else pltpu.roll(even, HEAD_DIM, 1)
        odd = odd if kh == 1 else pltpu.roll(odd, HEAD_DIM, 1)
        put(h // 2, jnp.where(lo, even, odd))


def _per_head_rows(ref, j):
    return jnp.concatenate([ref[:, ATT_GROUP * j + t:ATT_GROUP * j + t + 1] for t in range(ATT_GROUP)], axis=0)


def _per_head_scalar(ref, j):
    rows = _iota((ATT_GROUP * WINDOW, 1), 0) // WINDOW
    out = jnp.zeros((ATT_GROUP * WINDOW, 1), F32)
    for t in range(ATT_GROUP):
        out = out + jnp.where(rows == t, ref[:, ATT_GROUP * j + t:ATT_GROUP * j + t + 1], 0.0)
    return out


def attn_fwd(q, k, v, sinks, name):
    n, qw = q.shape
    kw = k.shape[1]
    nb = n // WINDOW
    scale = HEAD_DIM ** -0.5

    def body(q_ref, kc_ref, kp_ref, vc_ref, vp_ref, sk_ref, o_ref, lse_ref):
        i = pl.program_id(0)
        valid = _attn_mask(i)
        lo = _iota((WINDOW, LANE), 1) < HEAD_DIM
        k2 = jnp.concatenate([kp_ref[...], kc_ref[...]], axis=0)
        v2 = jnp.concatenate([vp_ref[...], vc_ref[...]], axis=0)
        lane1 = _iota((1, LANE), 1)
        lse = jnp.zeros((WINDOW, LANE), F32)

        def put_o(qb, val):
            o_ref[:, qb * LANE:(qb + 1) * LANE] = val.astype(o_ref.dtype)

        for j in range(ATT_KV_HEADS):
            kb, kh = j // 2, j % 2
            q4 = _stack_heads(q_ref, j, kh, lo)
            logits = jnp.where(valid, _mm(q4, k2[:, kb * LANE:(kb + 1) * LANE], NT) * scale, -1e30)
            sk = _per_head_scalar(sk_ref, j)
            m = jnp.maximum(jnp.max(logits, axis=-1, keepdims=True), sk)
            e = jnp.exp(logits - m)
            den = jnp.sum(e, axis=-1, keepdims=True) + jnp.exp(sk - m)
            lse4 = m + jnp.log(den)
            for t in range(ATT_GROUP):
                lse = lse + lse4[t * WINDOW:(t + 1) * WINDOW, :] * (lane1 == ATT_GROUP * j + t).astype(F32)
            _unstack_heads(_mm(e * (1.0 / den), v2[:, kb * LANE:(kb + 1) * LANE]), j, kh, lo, put_o)
        lse_ref[...] = lse

    return _call(body, name, (nb,),
                 [_rs(WINDOW, qw), _rs(WINDOW, kw), _rs(WINDOW, kw, 0, -1), _rs(WINDOW, kw), _rs(WINDOW, kw, 0, -1), _ps((1, LANE))],
                 [_rs(WINDOW, qw), _rs(WINDOW, LANE)], [S((n, qw), MXU_DTYPE), S((n, LANE), F32)],
                 sem=("parallel",))(q, k, k, v, v, sinks)


def attn_bwd(q, k, v, o, do, lse, sinks, name):
    n, qw = q.shape
    kw = k.shape[1]
    nb = n // WINDOW
    scale = HEAD_DIM ** -0.5

    def body(q_ref, kc_ref, kp_ref, vc_ref, vp_ref, o_ref, do_ref, lse_ref, sk_ref,
             dq_ref, dka_ref, dkb_ref, dva_ref, dvb_ref, dsk_ref):
        i = pl.program_id(0)
        valid = _attn_mask(i)
        lo = _iota((WINDOW, LANE), 1) < HEAD_DIM
        k2 = jnp.concatenate([kp_ref[...], kc_ref[...]], axis=0)
        v2 = jnp.concatenate([vp_ref[...], vc_ref[...]], axis=0)
        lane1 = _iota((1, LANE), 1)
        do_ = do_ref[...].astype(F32)
        delta = _mm01(do_ * o_ref[...].astype(F32), _head_indicator(qw, ATT_HEADS))
        dk2 = [jnp.zeros((2 * WINDOW, LANE), F32) for _ in range(kw // LANE)]
        dv2 = [jnp.zeros((2 * WINDOW, LANE), F32) for _ in range(kw // LANE)]
        dsk = jnp.zeros((1, LANE), F32)

        def put_dq(qb, val):
            dq_ref[:, qb * LANE:(qb + 1) * LANE] = val

        for j in range(ATT_KV_HEADS):
            kb, kh = j // 2, j % 2
            q4 = _stack_heads(q_ref, j, kh, lo)
            do4 = _stack_heads(do_ref, j, kh, lo)
            kk = k2[:, kb * LANE:(kb + 1) * LANE]
            vv = v2[:, kb * LANE:(kb + 1) * LANE]
            logits = jnp.where(valid, _mm(q4, kk, NT) * scale, -1e30)
            lse4 = _per_head_rows(lse_ref, j)
            p = jnp.exp(logits - lse4)
            dl = jnp.concatenate([delta[:, ATT_GROUP * j + t:ATT_GROUP * j + t + 1] for t in range(ATT_GROUP)], axis=0)
            ds = p * (_mm(do4, vv, NT) - dl) * scale
            sd = jnp.exp(_per_head_scalar(sk_ref, j) - lse4) * dl
            for t in range(ATT_GROUP):
                dsk = dsk - (jnp.sum(sd[t * WINDOW:(t + 1) * WINDOW, :], axis=0, keepdims=True)
                             * (lane1 == ATT_GROUP * j + t).astype(F32))
            _unstack_heads(_mm(ds, kk), j, kh, lo, put_dq)
            dk2[kb] = dk2[kb] + _mm(ds, q4, TN)
            dv2[kb] = dv2[kb] + _mm(p, do4, TN)
        for kb in range(kw // LANE):
            dkb_ref[:, kb * LANE:(kb + 1) * LANE] = dk2[kb][0:WINDOW, :]
            dka_ref[:, kb * LANE:(kb + 1) * LANE] = dk2[kb][WINDOW:, :]
            dvb_ref[:, kb * LANE:(kb + 1) * LANE] = dv2[kb][0:WINDOW, :]
            dva_ref[:, kb * LANE:(kb + 1) * LANE] = dv2[kb][WINDOW:, :]
        _acc(dsk_ref, i, dsk)

    return _call(body, name, (nb,),
                 [_rs(WINDOW, qw), _rs(WINDOW, kw), _rs(WINDOW, kw, 0, -1), _rs(WINDOW, kw), _rs(WINDOW, kw, 0, -1),
                  _rs(WINDOW, qw), _rs(WINDOW, qw), _rs(WINDOW, LANE), _ps((1, LANE))],
                 [_rs(WINDOW, qw)] + [_rs(WINDOW, kw)] * 4 + [_ps((1, LANE))],
                 [S((n, qw), F32)] + [S((n, kw), F32)] * 4 + [S((1, LANE), F32)],
                 sem=("arbitrary",))(q, k, k, v, v, o, do, lse, sinks)


def attn_grad_merge(dq, dka, dkb, dva, dvb, cos, sin, name):
    n, qw = dq.shape
    kw = dka.shape[1]
    nb = n // WINDOW
    w = qw + 2 * kw

    def body(dq_ref, dka_ref, dkb_ref, dva_ref, dvb_ref, c_ref, s_ref, o_ref, db_ref):
        i = pl.program_id(0)
        c, s = c_ref[...], -s_ref[...]
        nxt = (i < nb - 1).astype(F32)

        @pl.when(i == 0)
        def _():
            db_ref[...] = jnp.zeros_like(db_ref)

        def put(c0, val):
            o_ref[:, c0:c0 + val.shape[1]] = val.astype(o_ref.dtype)
            db_ref[:, c0:c0 + val.shape[1]] += jnp.sum(val, axis=0, keepdims=True)

        for j in range(qw // LANE):
            put(j * LANE, _rope128(dq_ref[:, j * LANE:(j + 1) * LANE], c, s))
        for j in range(kw // LANE):
            sl = slice(j * LANE, (j + 1) * LANE)
            put(qw + j * LANE, _rope128(dka_ref[:, sl] + dkb_ref[:, sl] * nxt, c, s))
        put(qw + kw, dva_ref[...] + dvb_ref[...] * nxt)

    return _call(body, name, (nb,),
                 [_rs(WINDOW, qw), _rs(WINDOW, kw), _rs(WINDOW, kw, 0, 1, nb), _rs(WINDOW, kw), _rs(WINDOW, kw, 0, 1, nb),
                  _rs(WINDOW, LANE), _rs(WINDOW, LANE)],
                 [_rs(WINDOW, w), _ps((1, w))], [S((n, w), MXU_DTYPE), S((1, w), F32)],
                 sem=("arbitrary",))(dq, dka, dkb, dva, dvb, cos, sin)


def _row(v):
    return v.reshape(1, -1)


def _pad_lanes(v, width=LANE):
    return jnp.pad(v.reshape(1, -1), ((0, 0), (0, width - v.size)))


class LayerWeights(dict):
    def __init__(self, small, fetch):
        super().__init__(small)
        self.fetch = fetch

    def need(self, k, after):
        if k not in self:
            self[k] = self.fetch(k, after)
        return self[k]


def ffn_fwd(h, g, w, keys, tag):
    xn, u, hm = swiglu_in(h, _row(g), w.need(keys[0], h), f"{tag}_in")
    return matmul(hm, w.need(keys[1], hm), "nn", f"{tag}_out", scale=0.5, res=h), (h, xn, u, hm)


class GradSink:
    ORDER = ("ffn1_w_out", "ffn2_w_out", "ple_gate_w", "att_w_o", "hyb_w_out", "ffn1_w_in", "ffn2_w_in", "att_w_qkv",
             "ple_proj_w", "hyb_w_in")

    def __init__(self, shard_shapes, bucket_of):
        self.where, rows = {}, {}
        for k in self.ORDER:
            n, r, c = shard_shapes[k]
            for li in range(n):
                layer = li if k in PER_LAYER else 2 * li + (0 if k in EVEN_ONLY else 1)
                rows_b = rows.setdefault(bucket_of(layer, _stage(k)), {})
                key = c if r % 32 == 0 else k
                off = -(-rows_b.get(key, (0, c))[0] // r) * r
                rows_b[key] = (-(-(off + r) // 32) * 32, c)
                self.where[k, li] = (bucket_of(layer, _stage(k)), key, "r" if _shard_axis(k) == 1 else "c", off, r)
        self.bufs = {b: {key: lax.empty((N_CHIPS, r, c), MXU_DTYPE) for key, (r, c) in rows_b.items()}
                     for b, rows_b in rows.items()}

    def mm(self, k, li, a, b, name, scale=None, c0=0, paired=False):
        bucket, key, kind, off, r = self.where[k, li]
        buf = self.bufs[bucket][key]
        slot = Slot(buf, kind, r if kind == "r" else buf.shape[2], off, c0, paired)
        self.bufs[bucket][key] = matmul(a, b, "tn", name, scale=scale, into=slot)

    def put(self, k, li, chip_major):
        b, key = self.where[k, li][:2]
        pad = self.bufs[b][key].shape[1] - chip_major.shape[1]
        self.bufs[b][key] = jnp.pad(chip_major.astype(self.bufs[b][key].dtype), ((0, 0), (0, pad), (0, 0)))


def ffn_bwd(dh, g, w_in, w_out, saved, tag, sink, keys, layer, after, colsum=False):
    h, xn, u, hm = saved
    sink.mm(keys[1], layer, hm, dh, f"{tag}_dwout", scale=0.5)
    du = swiglu_out_bwd(dh, w_out, u, after, f"{tag}_dhm")
    sink.mm(keys[0], layer, xn, du, f"{tag}_dwin", paired=True)
    outs = nt_rms_bwd(du, ColSharded(w_in.arr, paired=True), h, _row(g), dh, f"{tag}_dxn", colsum=colsum)
    return (outs[0], outs[1].reshape(-1)) + ((outs[2],) if colsum else ())


def _hyb_params(w):
    d = w["conv_dw_b"].size
    inner = SSM_HEADS * HEAD_DIM
    main = 3 * d + w["ssm_conv_b"].size
    return dict(
        w_main=w["hyb_w_in"][:main], w_dt=jnp.pad(w["hyb_w_in"][main:], ((0, LANE - SSM_HEADS), (0, 0))),
        cw=jnp.pad(w["conv_dw_w"], ((0, 32 - CONV_WIDTH), (0, 0))), cb=_row(w["conv_dw_b"]),
        lg=_row(w["conv_ln_g"]), lb=_row(w["conv_ln_b"]),
        sw=jnp.pad(w["ssm_conv_w"], ((0, 8 - SSM_CONV), (0, 0))), sb=_row(w["ssm_conv_b"]),
        dtb=_pad_lanes(w["ssm_dt_bias"]), al_row=_pad_lanes(w["ssm_a_log"]), al_col=w["ssm_a_log"].reshape(-1, 1),
        dfull=_row(jnp.repeat(w["ssm_d"], HEAD_DIM)), gamma=_row(w["ssm_norm"]), d=d, inner=inner, main=main)


def hyb_fwd(h, w, tag):
    w.need("hyb_w_in", h)
    q = _hyb_params(w)
    proj, xn = rms_matmul(h, _row(w["norm_mix"]), q["w_main"], "nt", f"{tag}_in")
    dtr = matmul(xn, q["w_dt"], "nt", f"{tag}_in_dt")
    u, u1 = conv_group_fwd(proj, q["cw"], q["cb"], q["lg"], q["lb"], f"{tag}_conv")
    pre, xs, bc, dt = ssm_conv_fwd(proj, dtr, q["sw"], q["sb"], q["dtb"], f"{tag}_sconv")
    dtT = dt[:, :SSM_HEADS].T
    yssd, hs = ssd_fwd(xs, bc, dt, dtT, q["al_row"], q["al_col"], f"{tag}_ssd")
    y = ssm_gate_fwd(yssd, xs, proj, q["dfull"], q["gamma"], f"{tag}_gate")
    wo = w.need("hyb_w_out", u)
    h2 = matmul(u, wo[:q["d"]], "nn", f"{tag}_out_a", res=h)
    h2 = matmul(y, wo[q["d"]:], "nn", f"{tag}_out_b", res=h2)
    return h2, (h, xn, proj, u, u1, pre, xs, bc, dt, dtT, yssd, hs, y)


def hyb_bwd(dh, w, saved, tag, sink, layer):
    q = _hyb_params(w)
    h, xn, proj, u, u1, pre, xs, bc, dt, dtT, yssd, hs, y = saved
    du = matmul(dh, w["hyb_w_out"][:q["d"]], "nt", f"{tag}_du")
    dy3 = matmul(dh, w["hyb_w_out"][q["d"]:], "nt", f"{tag}_dy")
    sink.mm("hyb_w_out", layer, u, dh, f"{tag}_dwo_a", c0=0)
    sink.mm("hyb_w_out", layer, y, dh, f"{tag}_dwo_b", c0=N_CHIPS // 2)
    dproj, dcw, dcb, dlg, dlb = conv_group_bwd(du, u1, proj, q["cw"], q["lg"], q["lb"], f"{tag}_dconv")
    dyssd, dxs_skip, dproj, dgamma, dd = ssm_gate_bwd(dy3, yssd, xs, proj, q["dfull"], q["gamma"], dproj, f"{tag}_dgate")
    dxs, dbc, ddtr, dalog, ddtb = ssd_bwd(xs, bc, dt, dtT, q["al_row"], q["al_col"], hs, dyssd, dxs_skip, f"{tag}_dssd")
    dproj, dsw, dsb = ssm_conv_bwd(dxs, dbc, pre, proj, q["sw"], dproj, f"{tag}_dsconv")
    dw_in = jnp.concatenate([matmul(dproj, xn, "tn", f"{tag}_dwin"),
                             matmul(ddtr, xn, "tn", f"{tag}_dwin_dt")[:SSM_HEADS]], axis=0)
    sink.put("hyb_w_in", layer, dw_in.reshape((N_CHIPS, -1) + dw_in.shape[1:]))
    dh2, dg = nt_rms_bwd(dproj, q["w_main"], h, _row(w["norm_mix"]), dh, f"{tag}_dxn", extra=(ddtr, q["w_dt"]), b_kd=True)
    grads = dict(norm_mix=dg.reshape(-1), conv_dw_w=dcw[:CONV_WIDTH], conv_dw_b=dcb.reshape(-1),
                 conv_ln_g=dlg.reshape(-1), conv_ln_b=dlb.reshape(-1), ssm_conv_w=dsw[:SSM_CONV], ssm_conv_b=dsb.reshape(-1),
                 ssm_dt_bias=ddtb.reshape(-1), ssm_a_log=dalog.reshape(-1), ssm_d=dd.reshape(-1), ssm_norm=dgamma.reshape(-1))
    return dh2, grads


def rope_tables(n):
    half = HEAD_DIM // 2
    inv = ROPE_THETA ** (-jnp.arange(0, HEAD_DIM, 2, dtype=F32) / HEAD_DIM)
    ang = jnp.arange(n, dtype=F32)[:, None] * inv[None, :]
    cos, sin = jnp.cos(ang), jnp.sin(ang)
    reps = LANE // HEAD_DIM
    return jnp.tile(jnp.concatenate([cos, cos], axis=1), (1, reps)), jnp.tile(jnp.concatenate([-sin, sin], axis=1), (1, reps))


def att_fwd(h, w, tables, tag):
    cos, sin = tables
    qkv, xn = rms_matmul(h, _row(w["norm_mix"]), w.need("att_w_qkv", h), "nn", f"{tag}_qkv", bias=_row(w["att_b_qkv"]))
    q, k, v = rope_fwd(qkv, cos, sin, f"{tag}_rope")
    sinks = _pad_lanes(w["att_sinks"])
    o, lse = attn_fwd(q, k, v, sinks, f"{tag}_attn")
    h2 = matmul(o, w.need("att_w_o", o), "nn", f"{tag}_o", bias=_row(w["att_b_o"]), res=h)
    return h2, (h, xn, q, k, v, o, lse, sinks)


def att_bwd(dh, dh_colsum, w, saved, tables, tag, sink, layer):
    cos, sin = tables
    h, xn, q, k, v, o, lse, sinks = saved
    do = matmul(dh, w["att_w_o"], "nt", f"{tag}_do")
    sink.mm("att_w_o", layer, o, dh, f"{tag}_dwo")
    dq, dka, dkb, dva, dvb, dsk = attn_bwd(q, k, v, o, do, lse, sinks, f"{tag}_dattn")
    dqkv, dbqkv = attn_grad_merge(dq, dka, dkb, dva, dvb, cos, sin, f"{tag}_drope")
    sink.mm("att_w_qkv", layer, xn, dqkv, f"{tag}_dwqkv")
    dh2, dg = nt_rms_bwd(dqkv, w["att_w_qkv"], h, _row(w["norm_mix"]), dh, f"{tag}_dxn")
    grads = dict(norm_mix=dg.reshape(-1), att_b_qkv=dbqkv.reshape(-1), att_sinks=dsk[0, :ATT_HEADS],
                 att_b_o=dh_colsum.reshape(-1))
    return dh2, grads


def ple_block_fwd(h, pe, w, tag):
    pp = matmul(pe, w.need("ple_proj_w", h), "nn", f"{tag}_proj")
    out, gl, xn = ple_fwd(h, _row(w["ple_norm"]), w.need("ple_gate_w", h), pp, f"{tag}_gate")
    return out, (h, xn, gl, pp, pe)


def ple_block_bwd(dh, w, saved, tag, sink, layer, after):
    h, xn, gl, pp, pe = saved
    dh2, dpp, dgl, dg = ple_bwd(dh, gl, pp, w["ple_gate_w"], h, _row(w["ple_norm"]), after, f"{tag}_dgate")
    sink.mm("ple_proj_w", layer, pe, dpp, f"{tag}_dwp")
    sink.mm("ple_gate_w", layer, xn, dgl, f"{tag}_dwg")
    return dh2, dict(ple_norm=dg.reshape(-1))


PER_LAYER = ("norm_ffn1", "ffn1_w_in", "ffn1_w_out", "norm_mix", "norm_ffn2", "ffn2_w_in", "ffn2_w_out",
             "ple_norm", "ple_gate_w", "ple_proj_w")
EVEN_ONLY = ("hyb_w_in", "conv_dw_w", "conv_dw_b", "conv_ln_g", "conv_ln_b", "ssm_conv_w", "ssm_conv_b",
             "ssm_dt_bias", "ssm_a_log", "ssm_d", "ssm_norm", "hyb_w_out")
ODD_ONLY = ("att_w_qkv", "att_b_qkv", "att_sinks", "att_w_o", "att_b_o")


def _layer_index(k, i):
    if k in PER_LAYER:
        return i
    if k in (EVEN_ONLY if i % 2 == 0 else ODD_ONLY):
        return i // 2
    return None


def _stage(k):
    return 0 if k.startswith("ffn1") else (2 if k.startswith(("ffn2", "ple")) else 1)


def trunk_fwd_bwd(x, pe, target, layers, final_norm, sink, stage_done):
    depth = len(layers)
    tables = rope_tables(x.shape[0])
    h = x
    saved = []
    for i, w in enumerate(layers):
        h, s1 = ffn_fwd(h, w["norm_ffn1"], w, ("ffn1_w_in", "ffn1_w_out"), f"l{i}_ffn1")
        if i % 2 == 0:
            h, s2 = hyb_fwd(h, w, f"l{i}_hyb")
        else:
            h, s2 = att_fwd(h, w, tables, f"l{i}_att")
        h, s3 = ffn_fwd(h, w["norm_ffn2"], w, ("ffn2_w_in", "ffn2_w_out"), f"l{i}_ffn2")
        h, s4 = ple_block_fwd(h, pe[i], w, f"l{i}_ple")
        saved.append((s1, s2, s3, s4))
    dh, dgf, loss = loss_head(h, _row(final_norm), target, "loss_head")
    grads = {}
    tie = dgf
    for i in reversed(range(depth)):
        w = layers[i]
        s1, s2, s3, s4 = saved[i]
        dh, g = ple_block_bwd(dh, w, s4, f"l{i}_ple", sink, i, tie)
        odd = i % 2 == 1
        out = ffn_bwd(dh, w["norm_ffn2"], w["ffn2_w_in"], w["ffn2_w_out"], s3, f"l{i}_ffn2", sink,
                      ("ffn2_w_in", "ffn2_w_out"), i, tie, colsum=odd)
        dh = out[0]
        g.update(norm_ffn2=out[1])
        if odd:
            dh, gm = att_bwd(dh, out[2], w, s2, tables, f"l{i}_att", sink, i // 2)
        else:
            dh, gm = hyb_bwd(dh, w, s2, f"l{i}_hyb", sink, i // 2)
        g.update(gm)
        tie = stage_done(i, 1, tie)
        out = ffn_bwd(dh, w["norm_ffn1"], w["ffn1_w_in"], w["ffn1_w_out"], s1, f"l{i}_ffn1", sink,
                      ("ffn1_w_in", "ffn1_w_out"), i, tie)
        dh = out[0]
        g.update(norm_ffn1=out[1])
        tie = stage_done(i, 0, tie)
        for k, v in g.items():
            grads.setdefault(k, []).insert(0, v)
    grads = {k: jnp.stack(v) for k, v in grads.items()}
    grads["final_norm"] = dgf.reshape(-1)
    return loss, dh, grads


def _me():
    return lax.axis_index("x"), lax.axis_index("y"), lax.axis_index("c")


def _flip(v, f):
    return 1 - v if f else v


def _remote(src, dst, send_sems, recv_sems, k, dev):
    return pltpu.make_async_remote_copy(src_ref=src, dst_ref=dst, send_sem=send_sems.at[k], recv_sem=recv_sems.at[k],
                                        device_id=dev, device_id_type=MESH)


CHIP_FLIPS = ((1, 0), (0, 1), (1, 1))
DEV_FLIPS = tuple((fx, fy, fc) for fx in (0, 1) for fy in (0, 1) for fc in (0, 1))[1:]


HBM = pl.BlockSpec(memory_space=pltpu.HBM)
SEM = pl.BlockSpec(memory_space=pltpu.SEMAPHORE)
DATAFLOW = pltpu.SideEffectType.DATAFLOW_SIDE_EFFECTING


def _core_half(ref, axis, c):
    if axis is None:
        return ref
    h = ref.shape[axis] // 2
    return ref.at[pl.ds(c * h, h), :] if axis == 0 else ref.at[:, pl.ds(c * h, h)]


def gather_start(xs, lands, halves, after, name):
    na = len(xs)

    def body(*refs):
        x_refs, land_refs = refs[:na], refs[na:2 * na]
        send_sems, recv_sems = refs[2 * na + 1], refs[2 * na + 2]
        token = refs[-1]
        mx, my, mc = _me()
        chip = 2 * mx + my
        for a in range(na):
            for j, (fx, fy) in enumerate(CHIP_FLIPS):
                _remote(_core_half(x_refs[a], halves[a], mc), _core_half(land_refs[a].at[chip], halves[a], mc),
                        send_sems, recv_sems, 3 * a + j, (_flip(mx, fx), _flip(my, fy), mc)).start()
        token[...] = jnp.zeros_like(token)

    outs = pl.pallas_call(
        body, name=name,
        out_shape=(pltpu.SemaphoreType.DMA((3 * na,)), pltpu.SemaphoreType.DMA((3 * na,)))
        + tuple(pltpu.HBM(x.shape, x.dtype) for x in xs) + tuple(pltpu.HBM(l.shape, l.dtype) for l in lands)
        + (S((8, LANE), F32),),
        in_specs=[HBM] * (2 * na) + [pl.BlockSpec(memory_space=pl.ANY)],
        out_specs=(SEM, SEM) + (HBM,) * (2 * na) + (pl.BlockSpec(memory_space=pltpu.VMEM),),
        input_output_aliases={a: 2 + a for a in range(2 * na)},
        compiler_params=pltpu.CompilerParams(has_side_effects=DATAFLOW),
    )(*[pltpu.with_memory_space_constraint(t, pltpu.HBM) for t in list(xs) + list(lands)], after)
    return outs[0], outs[1], list(outs[2:2 + na]), list(outs[2 + na:2 + 2 * na])


def gather_wait(send_sems, recv_sems, xs, lands, halves, first, after, name):
    na = len(xs)

    def body(*refs):
        x_refs, land_refs = refs[:na], refs[na:2 * na]
        send_sems, recv_sems = refs[2 * na], refs[2 * na + 1]
        mx, my, mc = _me()
        for a in range(na):
            for j, (fx, fy) in enumerate(CHIP_FLIPS):
                px, py = _flip(mx, fx), _flip(my, fy)
                cp = _remote(_core_half(x_refs[a], halves[a], mc), _core_half(land_refs[a].at[2 * px + py], halves[a], mc),
                             send_sems, recv_sems, 3 * (first + a) + j, (px, py, mc))
                cp.wait_send()
                cp.wait_recv()

    outs = pl.pallas_call(
        body, name=name,
        out_shape=tuple(pltpu.HBM(x.shape, x.dtype) for x in xs) + tuple(pltpu.HBM(l.shape, l.dtype) for l in lands),
        in_specs=[HBM] * (2 * na) + [SEM, SEM, pl.BlockSpec(memory_space=pl.ANY)], out_specs=(HBM,) * (2 * na),
        input_output_aliases={a: a for a in range(2 * na)},
        compiler_params=pltpu.CompilerParams(has_side_effects=DATAFLOW),
    )(*xs, *lands, send_sems, recv_sems, after)
    return list(outs[na:])


def forward_halves(land, axis, name):
    def body(in_ref, out_ref, send_sems, recv_sems):
        del in_ref
        mx, my, mc = _me()
        sib = (mx, my, 1 - mc)
        slots = [2 * _flip(mx, fx) + _flip(my, fy) for fx, fy in CHIP_FLIPS]
        cps = [_remote(_core_half(out_ref.at[s], axis, mc), _core_half(out_ref.at[s], axis, mc), send_sems, recv_sems, j, sib)
               for j, s in enumerate(slots)]
        for cp in cps:
            cp.start()
        for j, s in enumerate(slots):
            _remote(_core_half(out_ref.at[s], axis, mc), _core_half(out_ref.at[s], axis, 1 - mc), send_sems, recv_sems, j, sib).wait_recv()
        for cp in cps:
            cp.wait_send()

    return pl.pallas_call(
        body, name=name, out_shape=S(land.shape, land.dtype), in_specs=[ANY], out_specs=ANY, input_output_aliases={0: 0},
        scratch_shapes=[pltpu.SemaphoreType.DMA((3,)), pltpu.SemaphoreType.DMA((3,))])(land)


def all_gather_devices(v, name):
    r, l = v.shape

    def body(v_ref, out_ref, send_sems, recv_sems):
        mx, my, mc = _me()
        me = 4 * mx + 2 * my + mc
        peers = [(_flip(mx, fx), _flip(my, fy), _flip(mc, fc)) for fx, fy, fc in DEV_FLIPS]
        sends = [_remote(v_ref, out_ref.at[me], send_sems, recv_sems, j, p) for j, p in enumerate(peers)]
        for cp in sends:
            cp.start()
        for j, (px, py, pc) in enumerate(peers):
            _remote(v_ref, out_ref.at[4 * px + 2 * py + pc], send_sems, recv_sems, j, (px, py, pc)).wait_recv()
        for cp in sends:
            cp.wait_send()

    out = pl.pallas_call(
        body, name=name, out_shape=S((N_DEV, r, l), v.dtype), in_specs=[ANY], out_specs=ANY,
        scratch_shapes=[pltpu.SemaphoreType.DMA((7,)), pltpu.SemaphoreType.DMA((7,))])(v)
    me = 4 * lax.axis_index("x") + 2 * lax.axis_index("y") + lax.axis_index("c")
    return lax.dynamic_update_slice_in_dim(out, v[None], me, axis=0)


def gather_devices_start(v, name):
    me = 4 * lax.axis_index("x") + 2 * lax.axis_index("y") + lax.axis_index("c")
    land = lax.dynamic_update_slice_in_dim(lax.empty((N_DEV,) + v.shape, v.dtype), v[None], me, axis=0)

    def body(v_ref, land_ref, send_sems, recv_sems, v_thru, land_thru, token):
        mx, my, mc = _me()
        for j, (fx, fy, fc) in enumerate(DEV_FLIPS):
            _remote(v_ref, land_ref.at[4 * mx + 2 * my + mc], send_sems, recv_sems, j,
                    (_flip(mx, fx), _flip(my, fy), _flip(mc, fc))).start()
        token[...] = jnp.zeros_like(token)

    outs = pl.pallas_call(
        body, name=name,
        out_shape=(pltpu.SemaphoreType.DMA((7,)), pltpu.SemaphoreType.DMA((7,)), pltpu.HBM(v.shape, v.dtype),
                   pltpu.HBM(land.shape, land.dtype), S((8, LANE), F32)),
        in_specs=[HBM, HBM], out_specs=(SEM, SEM, HBM, HBM, pl.BlockSpec(memory_space=pltpu.VMEM)),
        input_output_aliases={0: 2, 1: 3}, compiler_params=pltpu.CompilerParams(has_side_effects=DATAFLOW),
    )(pltpu.with_memory_space_constraint(v, pltpu.HBM), pltpu.with_memory_space_constraint(land, pltpu.HBM))
    return outs[:4]


def gather_devices_wait(send_sems, recv_sems, v, land, after, name):
    def body(v_ref, land_ref, send_sems, recv_sems, after_ref, v_dead, got_ref):
        mx, my, mc = _me()
        for j, (fx, fy, fc) in enumerate(DEV_FLIPS):
            px, py, pc = _flip(mx, fx), _flip(my, fy), _flip(mc, fc)
            cp = _remote(v_ref, land_ref.at[4 * px + 2 * py + pc], send_sems, recv_sems, j, (px, py, pc))
            cp.wait_send()
            cp.wait_recv()

    return pl.pallas_call(
        body, name=name, out_shape=(pltpu.HBM(v.shape, v.dtype), pltpu.HBM(land.shape, land.dtype)),
        in_specs=[HBM, HBM, SEM, SEM, pl.BlockSpec(memory_space=pl.ANY)], out_specs=(HBM, HBM),
        input_output_aliases={0: 0, 1: 1}, compiler_params=pltpu.CompilerParams(has_side_effects=DATAFLOW),
    )(v, land, send_sems, recv_sems, after)[1]


def sum_devices(g8, name):
    nd, r, l = g8.shape
    tile = r
    for t in (512, 256, 128, 64, 32, 16, 8):
        if r % t == 0:
            tile = t
            break

    def body(g_ref, o_ref):
        acc = g_ref[0]
        for d in range(1, nd):
            acc = acc + g_ref[d]
        o_ref[...] = acc

    return _call(body, name, (r // tile,), [pl.BlockSpec((nd, tile, l), lambda i: (0, i, 0))], _rs(tile, l), S((r, l), F32),
                 sem=("parallel",))(g8)


def exchange_halves(gs, name):
    na = len(gs)
    nch = gs[0].shape[0]

    def body(*refs):
        g_refs, out_refs = refs[:na], refs[na:2 * na]
        send_sems, recv_sems = refs[2 * na:]
        mx, my, mc = _me()
        sib = (mx, my, 1 - mc)
        cps = []
        for a in range(na):
            half = gs[a].shape[1] // 2
            for j in range(nch):
                cps.append(_remote(g_refs[a].at[j, pl.ds((1 - mc) * half, half), :], out_refs[a].at[j],
                                   send_sems, recv_sems, nch * a + j, sib))
        for cp in cps:
            cp.start()
        for cp in cps:
            cp.wait_recv()
        for cp in cps:
            cp.wait_send()

    return pl.pallas_call(
        body, name=name, out_shape=[S((nch, g.shape[1] // 2, g.shape[2]), g.dtype) for g in gs],
        in_specs=[ANY] * na, out_specs=[ANY] * na,
        scratch_shapes=[pltpu.SemaphoreType.DMA((nch * na,)), pltpu.SemaphoreType.DMA((nch * na,))])(*gs)


def add_halves(g4, got, name):
    nch, r, l = g4.shape
    half = r // 2
    tile = _pick_rows(half)
    nt = half // tile

    def body(g_ref, r_ref, a_ref, own_ref):
        j = pl.program_id(1)
        chip = 2 * lax.axis_index("x") + lax.axis_index("y")
        val = g_ref[0].astype(F32) + r_ref[0].astype(F32)
        a_ref[0] = val.astype(a_ref.dtype)

        @pl.when(j == chip)
        def _():
            own_ref[...] = val

    return pl.pallas_call(
        body, name=name, grid=(nt, nch),
        in_specs=[pl.BlockSpec((1, tile, l), lambda i, j: (j, lax.axis_index("c") * nt + i, 0)),
                  pl.BlockSpec((1, tile, l), lambda i, j: (j, i, 0))],
        out_specs=[pl.BlockSpec((1, tile, l), lambda i, j: (j, i, 0)), pl.BlockSpec((tile, l), lambda i, j: (i, 0))],
        out_shape=[S((nch, half, l), MXU_DTYPE), S((half, l), F32)],
        compiler_params=pltpu.CompilerParams(dimension_semantics=("parallel", "arbitrary"), vmem_limit_bytes=VMEM_LIMIT))(g4, got)


def _pick_rows(r, cap=640):
    return next((t for t in range(cap - cap % 16, 15, -16) if r % t == 0), r)


def add_chips(own, got, name):
    h, l = own.shape
    tile = _pick_rows(h)

    def body(o_ref, g_ref, out_ref):
        out_ref[...] = ((o_ref[...] + g_ref[0].astype(F32)) + g_ref[1].astype(F32)) + g_ref[2].astype(F32)

    nt = h // tile
    return _call(body, name, (nt,), [_rs(tile, l), pl.BlockSpec((3, tile, l), lambda i: (0, i, 0))],
                 pl.BlockSpec((tile, l), lambda i: (lax.axis_index("c") * nt + i, 0)),
                 S((2 * h, l), F32), sem=("parallel",))(own, got)


def join_halves(bufs, name):
    na = len(bufs)

    def body(*refs):
        out_refs = refs[na:2 * na]
        send_sems, recv_sems = refs[2 * na:]
        mx, my, mc = _me()
        sib = (mx, my, 1 - mc)

        def half(a, hc):
            h = bufs[a].shape[0] // 2
            return out_refs[a].at[pl.ds(hc * h, h), :]

        cps = [_remote(half(a, mc), half(a, mc), send_sems, recv_sems, a, sib) for a in range(na)]
        for cp in cps:
            cp.start()
        for a in range(na):
            _remote(half(a, mc), half(a, 1 - mc), send_sems, recv_sems, a, sib).wait_recv()
        for cp in cps:
            cp.wait_send()

    return pl.pallas_call(
        body, name=name, out_shape=[S(b.shape, b.dtype) for b in bufs], in_specs=[ANY] * na, out_specs=[ANY] * na,
        input_output_aliases={a: a for a in range(na)},
        scratch_shapes=[pltpu.SemaphoreType.DMA((na,)), pltpu.SemaphoreType.DMA((na,))])(*bufs)


def exchange_chips_start(parts, name):
    na = len(parts)
    lands = [lax.empty((3,) + p.shape[1:], p.dtype) for p in parts]

    def body(*refs):
        a_refs, land_refs = refs[:na], refs[na:2 * na]
        send_sems, recv_sems = refs[2 * na], refs[2 * na + 1]
        mx, my, mc = _me()
        for j, (fx, fy) in enumerate(CHIP_FLIPS):
            px, py = _flip(mx, fx), _flip(my, fy)
            for a in range(na):
                _remote(a_refs[a].at[2 * px + py], land_refs[a].at[j], send_sems, recv_sems, 3 * a + j, (px, py, mc)).start()
        refs[-1][...] = jnp.zeros_like(refs[-1])

    outs = pl.pallas_call(
        body, name=name,
        out_shape=(pltpu.SemaphoreType.DMA((3 * na,)), pltpu.SemaphoreType.DMA((3 * na,)))
        + tuple(pltpu.HBM(t.shape, t.dtype) for t in list(parts) + lands) + (S((8, LANE), F32),),
        in_specs=[HBM] * (2 * na), out_specs=(SEM, SEM) + (HBM,) * (2 * na) + (pl.BlockSpec(memory_space=pltpu.VMEM),),
        input_output_aliases={a: 2 + a for a in range(2 * na)},
        compiler_params=pltpu.CompilerParams(has_side_effects=DATAFLOW),
    )(*[pltpu.with_memory_space_constraint(t, pltpu.HBM) for t in list(parts) + lands])
    return outs[0], outs[1], list(outs[2:2 + na]), list(outs[2 + na:2 + 2 * na]), outs[-1]


def exchange_chips_wait(send_sems, recv_sems, parts, lands, after, name):
    na = len(parts)

    def body(*refs):
        a_refs, land_refs = refs[:na], refs[na:2 * na]
        send_sems, recv_sems = refs[2 * na], refs[2 * na + 1]
        mx, my, mc = _me()
        for j, (fx, fy) in enumerate(CHIP_FLIPS):
            px, py = _flip(mx, fx), _flip(my, fy)
            for a in range(na):
                cp = _remote(a_refs[a].at[2 * px + py], land_refs[a].at[j], send_sems, recv_sems, 3 * a + j, (px, py, mc))
                cp.wait_send()
                cp.wait_recv()

    outs = pl.pallas_call(
        body, name=name, out_shape=tuple(pltpu.HBM(t.shape, t.dtype) for t in list(parts) + list(lands)),
        in_specs=[HBM] * (2 * na) + [SEM, SEM, pl.BlockSpec(memory_space=pl.ANY)], out_specs=(HBM,) * (2 * na),
        input_output_aliases={a: a for a in range(2 * na)},
        compiler_params=pltpu.CompilerParams(has_side_effects=DATAFLOW),
    )(*parts, *lands, send_sems, recv_sems, after)
    return list(outs[na:])


def reduce_begin(gs, tag):
    got = exchange_halves(gs, f"{tag}_d2d")
    sums = [add_halves(g, r, f"{tag}_add1_{i}") for i, (g, r) in enumerate(zip(gs, got))]
    return [own for _, own in sums], exchange_chips_start([a for a, _ in sums], f"{tag}_ici_start")


def reduce_end(state, after, tag):
    owns, (send_sems, recv_sems, parts, lands, _) = state
    got = exchange_chips_wait(send_sems, recv_sems, parts, lands, after, f"{tag}_ici_wait")
    return [add_chips(own, r, f"{tag}_add2_{i}") for i, (own, r) in enumerate(zip(owns, got))]


PACK_L = 1024


def _pack(arrs, dtype, row_mult, lead=None):
    lead_shape = () if lead is None else arrs[0].shape[:lead]
    flat = jnp.concatenate([a.astype(dtype).reshape(lead_shape + (-1,)) for a in arrs], axis=-1)
    n = flat.shape[-1]
    unit = row_mult * PACK_L
    total = -(-n // unit) * unit
    flat = jnp.pad(flat, [(0, 0)] * len(lead_shape) + [(0, total - n)])
    return flat.reshape(lead_shape + (total // PACK_L, PACK_L))


def _unpack(packed, shapes, lead=None):
    lead_shape = () if lead is None else packed.shape[:lead]
    flat = packed.reshape(lead_shape + (-1,))
    out, off = [], 0
    for shp in shapes:
        n = int(np.prod(shp))
        out.append(flat[..., off:off + n].reshape(lead_shape + tuple(shp)))
        off += n
    return out


def _to_full(gathered, axis):
    t = jnp.moveaxis(gathered, 0, axis)
    shp = t.shape
    return t.reshape(shp[:axis] + (shp[axis] * shp[axis + 1],) + shp[axis + 2:])


WEIGHTS = ("norm_ffn1", "ffn1_w_in", "ffn1_w_out", "norm_mix", "norm_ffn2", "ffn2_w_in", "ffn2_w_out", "ple_norm",
           "ple_gate_w", "ple_proj_w", "hyb_w_in", "conv_dw_w", "conv_dw_b", "conv_ln_g", "conv_ln_b", "ssm_conv_w",
           "ssm_conv_b", "ssm_dt_bias", "ssm_a_log", "ssm_d", "ssm_norm", "hyb_w_out", "att_w_qkv", "att_b_qkv",
           "att_sinks", "att_w_o", "att_b_o", "final_norm")
SHARD_AXIS = dict(ffn1_w_in=2, ffn1_w_out=1, ffn2_w_in=2, ffn2_w_out=1, ple_gate_w=1, ple_proj_w=2, hyb_w_in=2,
                  conv_dw_w=2, ssm_conv_w=2, hyb_w_out=1, att_w_qkv=2, att_b_qkv=1, att_w_o=1, att_b_o=1)
BIG = ("ffn1_w_in", "ffn1_w_out", "ffn2_w_in", "ffn2_w_out", "ple_gate_w", "ple_proj_w", "hyb_w_in", "hyb_w_out",
       "att_w_qkv", "att_w_o")
TRANSPOSED = ("hyb_w_in",)
FORWARDED = (0, 2)


def _shard_axis(k):
    return 1 if k in TRANSPOSED else SHARD_AXIS[k]
SMALL_SHARDED = ("conv_dw_w", "ssm_conv_w", "att_b_qkv", "att_b_o")
SMALL = tuple(k for k in WEIGHTS if k not in BIG)


def _step(x, p, target, w, m, v):
    mx, my = lax.axis_index("x"), lax.axis_index("y")
    chip = 2 * mx + my
    w, m, v = ({k: (a.transpose(0, 2, 1) if k in TRANSPOSED else a) for k, a in d.items()} for d in (w, m, v))

    depth = w["norm_ffn1"].shape[0]
    order = sorted([(k, i) for i in range(depth) for k in BIG if _layer_index(k, i) is not None],
                   key=lambda t: (t[1], _stage(t[0])))
    small_g = all_gather_devices(_pack([w[k] for k in SMALL_SHARDED], F32, 8), "gather_small")
    shards = [w[k][_layer_index(k, i)].astype(MXU_DTYPE) for k, i in order]
    lands = [lax.dynamic_update_slice_in_dim(lax.empty((N_CHIPS,) + s.shape, s.dtype), s[None], chip, axis=0) for s in shards]
    halves = [(0 if s.shape[0] % 32 == 0 else 1) if pos in FORWARDED else None for pos, s in enumerate(shards)]
    send_sems, recv_sems, shards, lands = gather_start(shards, lands, halves, small_g, "gather_start")

    def fetch(i, k, after):
        p = order.index((k, i))
        g, = gather_wait(send_sems, recv_sems, [shards[p]], [lands[p]], [halves[p]], p, after, f"gather_wait_l{i}_{k}")
        if halves[p] is not None:
            g = forward_halves(g, halves[p], f"gather_forward_l{i}_{k}")
        if _shard_axis(k) == 2:
            return ColSharded(g)
        return g.reshape(-1, g.shape[-1])

    small_g = small_g[0::2]
    small_full = {k: _to_full(g, SHARD_AXIS[k])
                  for k, g in zip(SMALL_SHARDED, _unpack(small_g, [w[k].shape for k in SMALL_SHARDED], lead=1))}
    layers = [LayerWeights({k: small_full.get(k, w[k])[_layer_index(k, i)] for k in SMALL if _layer_index(k, i) is not None},
                           functools.partial(fetch, i)) for i in range(depth)]

    def bucket_of(layer, stage):
        return (layer, 0) if layer > 0 else (0, min(stage, 1))

    sink = GradSink({k: w[k].shape for k in BIG}, bucket_of)
    begun = {}

    def stage_done(i, stage, tie):
        b = bucket_of(i, stage)
        if stage > 0 and bucket_of(i, stage - 1) == b:
            return tie
        begun[b] = reduce_begin(list(sink.bufs[b].values()), f"grads_l{b[0]}_{b[1]}")
        return begun[b][1][-1]

    loss, dx, grads = trunk_fwd_bwd(x[0], p[:, 0], target[0], layers, w["final_norm"], sink, stage_done)

    results = {}

    def finish(buckets, after, tag):
        halves = {b: reduce_end(begun[b], after, f"grads_l{b[0]}_{b[1]}") for b in buckets}
        joined = iter(join_halves([h for b in buckets for h in halves[b]], f"grads_join_{tag}"))
        reduced = {b: {c: next(joined) for c in sink.bufs[b]} for b in buckets}
        last = after
        for (k, li), (b, key, _, off, r) in sink.where.items():
            if b in buckets:
                g = reduced[b][key]
                if r % 8:
                    g, off = g[off:off + r], 0
                results[k] = adamw_layer(w[k], g, off, m[k], v[k], li, results.get(k), f"adamw_{k}_{li}")
                last = results[k][1]
        return last

    vec = gather_devices_start(_pack([loss[0:1, 0:1]] + [grads[k] for k in SMALL], F32, 8), "gather_vectors_start")
    order_b = list(begun)
    started_last = begun[order_b[-1]][1][-1]
    done = finish(order_b[-1:], finish(order_b[:-1], started_last, "early") if len(order_b) > 1 else dx, "last")
    g_out = {k: results[k][0] for k in BIG}
    vec = sum_devices(gather_devices_wait(*vec, done, "gather_vectors_wait"), "sum_vectors")
    parts = _unpack(vec, [(1, 1)] + [grads[k].shape for k in SMALL])
    loss_out = parts[0].reshape(())
    for k, g in zip(SMALL, parts[1:]):
        if k in SHARD_AXIS:
            ax = SHARD_AXIS[k]
            g = lax.dynamic_slice_in_dim(g, chip * w[k].shape[ax], w[k].shape[ax], axis=ax)
        g_out[k] = g

    for k in TRANSPOSED:
        results[k] = [a.transpose(0, 2, 1) for a in results[k]]
    g_out.update({k: results[k][0] for k in TRANSPOSED})
    delta, new_m, new_v = ({k: results[k][j] for k in BIG} for j in (1, 2, 3))
    shapes = [w[k].shape for k in SMALL]
    packed = [_pack([src[k] for k in SMALL], F32, 8) for src in (w, g_out, m, v)]
    outs = adamw(*packed, "adamw_small")
    for dst, o in zip((delta, new_m, new_v), outs):
        for k, a in zip(SMALL, _unpack(o, shapes)):
            dst[k] = a
    return ((loss_out, dx[None]) + tuple(g_out[k] for k in WEIGHTS) + tuple(delta[k] for k in WEIGHTS)
            + tuple(new_m[k] for k in WEIGHTS) + tuple(new_v[k] for k in WEIGHTS))


def kernel(x, p, norm_ffn1, ffn1_w_in, ffn1_w_out, norm_mix, norm_ffn2, ffn2_w_in, ffn2_w_out, ple_norm, ple_gate_w, ple_proj_w, hyb_w_in, conv_dw_w, conv_dw_b, conv_ln_g, conv_ln_b, ssm_conv_w, ssm_conv_b, ssm_dt_bias, ssm_a_log, ssm_d, ssm_norm, hyb_w_out, att_w_qkv, att_b_qkv, att_sinks, att_w_o, att_b_o, final_norm, loss_target, m_norm_ffn1, m_ffn1_w_in, m_ffn1_w_out, m_norm_mix, m_norm_ffn2, m_ffn2_w_in, m_ffn2_w_out, m_ple_norm, m_ple_gate_w, m_ple_proj_w, m_hyb_w_in, m_conv_dw_w, m_conv_dw_b, m_conv_ln_g, m_conv_ln_b, m_ssm_conv_w, m_ssm_conv_b, m_ssm_dt_bias, m_ssm_a_log, m_ssm_d, m_ssm_norm, m_hyb_w_out, m_att_w_qkv, m_att_b_qkv, m_att_sinks, m_att_w_o, m_att_b_o, m_final_norm, v_norm_ffn1, v_ffn1_w_in, v_ffn1_w_out, v_norm_mix, v_norm_ffn2, v_ffn2_w_in, v_ffn2_w_out, v_ple_norm, v_ple_gate_w, v_ple_proj_w, v_hyb_w_in, v_conv_dw_w, v_conv_dw_b, v_conv_ln_g, v_conv_ln_b, v_ssm_conv_w, v_ssm_conv_b, v_ssm_dt_bias, v_ssm_a_log, v_ssm_d, v_ssm_norm, v_hyb_w_out, v_att_w_qkv, v_att_b_qkv, v_att_sinks, v_att_w_o, v_att_b_o, v_final_norm):
    given = locals()
    w = {k: given[k] for k in WEIGHTS}
    m = {k: given["m_" + k] for k in WEIGHTS}
    v = {k: given["v_" + k] for k in WEIGHTS}
    return _step(x, p, loss_target, w, m, v)
```

```python
import functools

import numpy as np
import jax
import jax.numpy as jnp
from jax import lax
from jax.experimental import pallas as pl
from jax.experimental.pallas import tpu as pltpu

F32 = jnp.float32
BF16 = jnp.bfloat16
MXU_DTYPE = jnp.bfloat16
S = jax.ShapeDtypeStruct
MESH = pl.DeviceIdType.MESH

VMEM_LIMIT = 48 * 2**20
LANE = 128

EPS = 1e-6
SSM_HEADS = 16
HEAD_DIM = 64
SSM_GROUPS = 2
SSM_STATE = 128
SSM_CONV = 4
CHUNK = 128
CONV_WIDTH = 31
ATT_HEADS = 16
ATT_KV_HEADS = 4
WINDOW = 128
ROPE_THETA = 10000.0
ADAM_LR = 0.001
ADAM_B1 = 0.9
ADAM_B2 = 0.999
ADAM_EPS = 1e-08
ADAM_WD = 0.01
ADAM_STEP = 10

N_CHIPS = 4
N_DEV = 8

NN = ((1,), (0,))
NT = ((1,), (1,))
TN = ((0,), (0,))


def _mm(a, b, dims=NN):
    return lax.dot_general(a.astype(MXU_DTYPE), b.astype(MXU_DTYPE), (dims, ((), ())), preferred_element_type=F32)


def _split3(a):
    hi = a.astype(BF16)
    r = a - hi.astype(F32)
    mid = r.astype(BF16)
    lo = (r - mid.astype(F32)).astype(BF16)
    return hi, mid, lo


def _mm01(a, onehot, dims=NN):
    o = onehot.astype(BF16)
    out = None
    for part in _split3(a):
        t = lax.dot_general(part, o, (dims, ((), ())), preferred_element_type=F32)
        out = t if out is None else out + t
    return out


def _01mm(onehot, a):
    o = onehot.astype(BF16)
    out = None
    for part in _split3(a):
        t = lax.dot_general(o, part, (NN, ((), ())), preferred_element_type=F32)
        out = t if out is None else out + t
    return out


def _sigmoid(x):
    return 0.5 * jnp.tanh(0.5 * x) + 0.5


def _softplus(x):
    return jnp.maximum(x, 0.0) + jnp.log(1.0 + jnp.exp(-jnp.abs(x)))


def _iota(shape, axis):
    return lax.broadcasted_iota(jnp.int32, shape, axis)


def _head_indicator(width, heads, transposed=False):
    per = width // heads
    if transposed:
        return (_iota((heads, width), 1) // per == _iota((heads, width), 0)).astype(F32)
    return (_iota((width, heads), 0) // per == _iota((width, heads), 1)).astype(F32)


def _acc(ref, i, val):
    @pl.when(i == 0)
    def _():
        ref[...] = val

    @pl.when(i > 0)
    def _():
        ref[...] += val


def _rs(tile, width, col=0, shift=0, n=None):
    if shift == 0:
        return pl.BlockSpec((tile, width), lambda i: (i, col))
    if shift < 0:
        return pl.BlockSpec((tile, width), lambda i: (jnp.maximum(i - 1, 0), col))
    return pl.BlockSpec((tile, width), lambda i: (jnp.minimum(i + 1, n - 1), col))


def _ps(shape):
    return pl.BlockSpec(shape, lambda i: (0,) * len(shape))


def _call(body, name, grid, in_specs, out_specs, out_shape, scratch=(), sem=None):
    return pl.pallas_call(
        body, name=name, grid=grid, in_specs=in_specs, out_specs=out_specs, out_shape=out_shape,
        scratch_shapes=list(scratch),
        compiler_params=pltpu.CompilerParams(dimension_semantics=sem, vmem_limit_bytes=VMEM_LIMIT))


def _row_tile(n, target):
    t = min(n, target)
    assert n % t == 0, (n, t)
    return t


def _pick_tile(dim, target):
    if dim <= target:
        return dim
    t = (int(1.4 * target) // LANE) * LANE
    while t >= LANE:
        if dim % t == 0:
            return t
        t -= LANE
    return dim


ANY = pl.BlockSpec(memory_space=pl.ANY)


def _paired(j):
    return (j % 2) * 2 + j // 2


class ColSharded:
    def __init__(self, arr, paired=False):
        self.arr, self.paired = arr, paired
        self.nch, self.rows, self.per = arr.shape
        self.shape = (self.rows, self.nch * self.per)

    def chip(self, j):
        return _paired(j) if self.paired else j


class Slot:
    def __init__(self, buf, kind, per, off, c0=0, paired=False):
        self.buf, self.kind, self.per, self.off, self.c0, self.paired = buf, kind, per, off, c0, paired

    def chip(self, j):
        return _paired(j) if self.paired else j


def matmul(a, b, mode, name, *, out_dtype=F32, scale=None, res=None, bias=None, into=None, tm=1024, tn=1024, tk=1024):
    bshape = b.shape
    if mode == "nn":
        (m, k), (k2, n) = a.shape, bshape
    elif mode == "nt":
        (m, k), (n, k2) = a.shape, bshape
    else:
        (k, m), (k2, n) = a.shape, bshape
    assert k == k2, (a.shape, bshape, mode)
    tm, tn, tk = _pick_tile(m, tm), _pick_tile(n, tn), _pick_tile(k, tk)
    if isinstance(b, ColSharded):
        if mode == "nn":
            tn = b.per
        else:
            assert mode == "nt"
            tk = b.per
    if into is not None:
        if into.kind == "c":
            tn = into.per
            assert into.off % tm == 0 and n == N_CHIPS * into.per
        else:
            tm = max(1, min(m, int(1.4 * 1024)) // into.per) * into.per
            assert m % tm == 0 and into.off % into.per == 0 and into.c0 % (tm // into.per) == 0
    nk = k // tk
    dims = {"nn": NN, "nt": NT, "tn": TN}[mode]
    a_spec = (pl.BlockSpec((tk, tm), lambda i, j, kk: (kk, i)) if mode == "tn"
              else pl.BlockSpec((tm, tk), lambda i, j, kk: (i, kk)))
    if isinstance(b, ColSharded):
        bchip = b.chip
        b_spec = (pl.BlockSpec((None, tk, tn), lambda i, j, kk: (bchip(j), kk, 0)) if mode == "nn"
                  else pl.BlockSpec((None, tn, tk), lambda i, j, kk: (bchip(kk), j, 0)))
        b = b.arr
    else:
        b_spec = (pl.BlockSpec((tn, tk), lambda i, j, kk: (j, kk)) if mode == "nt"
                  else pl.BlockSpec((tk, tn), lambda i, j, kk: (kk, j)))
    plain_o = pl.BlockSpec((tm, tn), lambda i, j, kk: (i, j))
    ins, in_specs = [a, b], [a_spec, b_spec]
    if bias is not None:
        ins.append(bias)
        in_specs.append(pl.BlockSpec((1, tn), lambda i, j, kk: (0, j)))
    if res is not None:
        ins.append(res)
        in_specs.append(plain_o)
    aliases = {}
    if into is None:
        o_spec, o_shape = plain_o, S((m, n), out_dtype)
    else:
        aliases = {len(ins): 0}
        ins.append(into.buf)
        in_specs.append(ANY)
        o_shape = S(into.buf.shape, into.buf.dtype)
        if into.kind == "c":
            ob, ochip = into.off // tm, into.chip
            o_spec = pl.BlockSpec((None, tm, tn), lambda i, j, kk: (ochip(j), ob + i, 0))
        else:
            q, ob = tm // into.per, into.off // into.per
            cb = into.c0 // q
            o_spec = pl.BlockSpec((q, into.per, tn), lambda i, j, kk: (cb + i, ob, j))

    def body(*refs):
        a_ref, b_ref = refs[0], refs[1]
        o_ref, acc_ref = refs[-2], refs[-1]
        kk = pl.program_id(2)

        @pl.when(kk == 0)
        def _():
            acc_ref[...] = jnp.zeros_like(acc_ref)

        acc_ref[...] += _mm(a_ref[...], b_ref[...], dims)

        @pl.when(kk == nk - 1)
        def _():
            out = acc_ref[...]
            if scale is not None:
                out = out * scale
            pos = 2
            if bias is not None:
                out = out + refs[pos][...]
                pos += 1
            if res is not None:
                out = out + refs[pos][...]
            o_ref[...] = out.astype(o_ref.dtype).reshape(o_ref.shape)

    return pl.pallas_call(
        body, name=name, grid=(m // tm, n // tn, nk), in_specs=in_specs, out_specs=o_spec, out_shape=o_shape,
        scratch_shapes=[pltpu.VMEM((tm, tn), F32)], input_output_aliases=aliases,
        compiler_params=pltpu.CompilerParams(dimension_semantics=("parallel", "parallel", "arbitrary"),
                                             vmem_limit_bytes=VMEM_LIMIT))(*ins)


def rms_matmul(h, g, b, mode, name, bias=None):
    n, d = h.shape
    sharded = isinstance(b, ColSharded)
    n_out = b.shape[1] if mode == "nn" else b.shape[0]
    tm = _row_tile(n, 512)
    tn = b.per if sharded else _pick_tile(n_out, 1024)
    if sharded:
        assert mode == "nn"
        bchip = b.chip
        b_spec = pl.BlockSpec((None, d, tn), lambda i, j: (bchip(j), 0, 0))
        b = b.arr
    elif mode == "nn":
        b_spec = pl.BlockSpec((d, tn), lambda i, j: (0, j))
    else:
        b_spec = pl.BlockSpec((tn, d), lambda i, j: (j, 0))
    ins = [h, g, b] + ([bias] if bias is not None else [])
    in_specs = [pl.BlockSpec((tm, d), lambda i, j: (i, 0)), pl.BlockSpec((1, d), lambda i, j: (0, 0)), b_spec]
    if bias is not None:
        in_specs.append(pl.BlockSpec((1, tn), lambda i, j: (0, j)))

    def body(h_ref, g_ref, b_ref, *refs):
        o_ref, xn_ref = refs[-2:]

        @pl.when(pl.program_id(1) == 0)
        def _():
            x = h_ref[...]
            r = lax.rsqrt(jnp.mean(x * x, axis=-1, keepdims=True) + EPS)
            xn_ref[...] = (x * r * g_ref[...]).astype(xn_ref.dtype)

        out = _mm(xn_ref[...], b_ref[...], NN if mode == "nn" else NT)
        o_ref[...] = out if bias is None else out + refs[0][...]

    return pl.pallas_call(
        body, name=name, grid=(n // tm, n_out // tn), in_specs=in_specs,
        out_specs=[pl.BlockSpec((tm, tn), lambda i, j: (i, j)), pl.BlockSpec((tm, d), lambda i, j: (i, 0))],
        out_shape=[S((n, n_out), F32), S((n, d), MXU_DTYPE)],
        compiler_params=pltpu.CompilerParams(dimension_semantics=("parallel", "arbitrary"), vmem_limit_bytes=VMEM_LIMIT),
    )(*ins)


def _rms_bwd_math(x, g, dy):
    r = lax.rsqrt(jnp.mean(x * x, axis=-1, keepdims=True) + EPS)
    xh = x * r
    dg = jnp.sum(dy * xh, axis=0, keepdims=True)
    dxh = dy * g
    dx = r * (dxh - xh * jnp.mean(dxh * xh, axis=-1, keepdims=True))
    return dx, dg


def nt_rms_bwd(a, b, h, g, dh_in, name, extra=None, colsum=False, b_kd=False):
    n, k = a.shape
    d = h.shape[1]
    tm = _row_tile(n, 512)
    sharded = isinstance(b, ColSharded)
    tk = b.per if sharded else _pick_tile(k, 1024)
    nk = k // tk
    dims = NN if b_kd else NT
    if sharded:
        bchip = b.chip
        b_spec = pl.BlockSpec((None, d, tk), lambda i, kk: (bchip(kk), 0, 0))
        b = b.arr
    elif b_kd:
        b_spec = pl.BlockSpec((tk, d), lambda i, kk: (kk, 0))
    else:
        b_spec = pl.BlockSpec((d, tk), lambda i, kk: (0, kk))
    row = pl.BlockSpec((tm, d), lambda i, kk: (i, 0))
    vec = pl.BlockSpec((1, d), lambda i, kk: (0, 0))
    ins, in_specs = [a, b, h, g, dh_in], [pl.BlockSpec((tm, tk), lambda i, kk: (i, kk)), b_spec, row, vec, row]
    if extra is not None:
        k2 = extra[0].shape[1]
        ins += list(extra)
        in_specs += [pl.BlockSpec((tm, k2), lambda i, kk: (i, 0)),
                     pl.BlockSpec((k2, d) if b_kd else (d, k2), lambda i, kk: (0, 0))]
    n_in = len(ins)

    def body(*refs):
        a_ref, b_ref, h_ref, g_ref, dh_ref = refs[:5]
        o_ref, dg_ref = refs[n_in], refs[n_in + 1]
        acc_ref = refs[-1]
        i, kk = pl.program_id(0), pl.program_id(1)

        @pl.when(kk == 0)
        def _():
            acc_ref[...] = _mm(refs[5][...], refs[6][...], dims) if extra is not None else jnp.zeros_like(acc_ref)

        acc_ref[...] += _mm(a_ref[...], b_ref[...], dims)

        @pl.when(kk == nk - 1)
        def _():
            dx, dg = _rms_bwd_math(h_ref[...], g_ref[...], acc_ref[...])
            out = dh_ref[...] + dx
            o_ref[...] = out
            _acc(dg_ref, i, dg)
            if colsum:
                _acc(refs[n_in + 2], i, jnp.sum(out, axis=0, keepdims=True))

    n_vec = 2 if colsum else 1
    return pl.pallas_call(
        body, name=name, grid=(n // tm, nk), in_specs=in_specs, out_specs=[row] + [vec] * n_vec,
        out_shape=[S((n, d), F32)] + [S((1, d), F32)] * n_vec, scratch_shapes=[pltpu.VMEM((tm, d), F32)],
        compiler_params=pltpu.CompilerParams(dimension_semantics=("arbitrary", "arbitrary"), vmem_limit_bytes=VMEM_LIMIT),
    )(*ins)


def swiglu_in(h, g, w_in, name):
    n, d = h.shape
    per = w_in.per
    nj = w_in.nch // 2
    tile = _row_tile(n, 512)

    def body(h_ref, g_ref, wg_ref, wu_ref, xn_ref, u_ref, hm_ref):
        @pl.when(pl.program_id(1) == 0)
        def _():
            x = h_ref[...]
            r = lax.rsqrt(jnp.mean(x * x, axis=-1, keepdims=True) + EPS)
            xn_ref[...] = (x * r * g_ref[...]).astype(xn_ref.dtype)

        xn = xn_ref[...]
        a = _mm(xn, wg_ref[...])
        b = _mm(xn, wu_ref[...])
        u_ref[:, :per] = a.astype(u_ref.dtype)
        u_ref[:, per:] = b.astype(u_ref.dtype)
        hm_ref[...] = (a * _sigmoid(a) * b).astype(hm_ref.dtype)

    return pl.pallas_call(
        body, name=name, grid=(n // tile, nj),
        in_specs=[pl.BlockSpec((tile, d), lambda i, j: (i, 0)), pl.BlockSpec((1, d), lambda i, j: (0, 0)),
                  pl.BlockSpec((None, d, per), lambda i, j: (j, 0, 0)), pl.BlockSpec((None, d, per), lambda i, j: (nj + j, 0, 0))],
        out_specs=[pl.BlockSpec((tile, d), lambda i, j: (i, 0)), pl.BlockSpec((tile, 2 * per), lambda i, j: (i, j)),
                   pl.BlockSpec((tile, per), lambda i, j: (i, j))],
        out_shape=[S((n, d), MXU_DTYPE), S((n, 2 * nj * per), MXU_DTYPE), S((n, nj * per), MXU_DTYPE)],
        compiler_params=pltpu.CompilerParams(dimension_semantics=("parallel", "arbitrary"), vmem_limit_bytes=VMEM_LIMIT),
    )(h, g, w_in.arr, w_in.arr)


def swiglu_out_bwd(dh, w_out, u, after, name):
    n, d = dh.shape
    f = w_out.shape[0]
    per = u.shape[1] // 4
    nj = f // per
    tile = _row_tile(n, 512)

    def body(dh_ref, w_ref, u_ref, after_ref, du_ref):
        dm = 0.5 * _mm(dh_ref[...], w_ref[...], NT)
        a = u_ref[:, :per].astype(F32)
        b = u_ref[:, per:].astype(F32)
        s = _sigmoid(a)
        du_ref[:, :per] = (dm * b * s * (1.0 + a * (1.0 - s))).astype(du_ref.dtype)
        du_ref[:, per:] = (dm * a * s).astype(du_ref.dtype)

    return pl.pallas_call(
        body, name=name, grid=(n // tile, nj),
        in_specs=[pl.BlockSpec((tile, d), lambda i, j: (i, 0)), pl.BlockSpec((per, d), lambda i, j: (j, 0)),
                  pl.BlockSpec((tile, 2 * per), lambda i, j: (i, j)), ANY],
        out_specs=pl.BlockSpec((tile, 2 * per), lambda i, j: (i, j)),
        out_shape=S(u.shape, MXU_DTYPE),
        compiler_params=pltpu.CompilerParams(dimension_semantics=("parallel", "parallel"), vmem_limit_bytes=VMEM_LIMIT),
    )(dh, w_out, u, after)


def ple_fwd(h, g, w_gate, pp, name):
    n, d = h.shape
    tile = _row_tile(n, 512)

    def body(h_ref, g_ref, w_ref, pp_ref, o_ref, gl_ref, xn_ref):
        x = h_ref[...]
        r = lax.rsqrt(jnp.mean(x * x, axis=-1, keepdims=True) + EPS)
        xn = (x * r * g_ref[...]).astype(xn_ref.dtype)
        xn_ref[...] = xn
        gl = _mm(xn, w_ref[...])
        gl_ref[...] = gl
        o_ref[...] = x + _sigmoid(gl) * pp_ref[...]

    return _call(body, name, (n // tile,), [_rs(tile, d), _ps((1, d)), _ps(w_gate.shape), _rs(tile, d)],
                 [_rs(tile, d)] * 3, [S((n, d), F32), S((n, d), F32), S((n, d), MXU_DTYPE)], sem=("parallel",))(h, g, w_gate, pp)


def ple_bwd(dh, gl, pp, w_gate, h, g, after, name):
    n, d = dh.shape
    tile = _row_tile(n, 512)

    def body(dh_ref, gl_ref, pp_ref, w_ref, h_ref, g_ref, after_ref, o_ref, dpp_ref, dgl_ref, dg_ref):
        i = pl.program_id(0)
        s = _sigmoid(gl_ref[...])
        dh_ = dh_ref[...]
        dpp_ref[...] = (dh_ * s).astype(dpp_ref.dtype)
        dgl = (dh_ * pp_ref[...] * s * (1.0 - s)).astype(dgl_ref.dtype)
        dgl_ref[...] = dgl
        dx, dg = _rms_bwd_math(h_ref[...], g_ref[...], _mm(dgl, w_ref[...], NT))
        o_ref[...] = dh_ + dx
        _acc(dg_ref, i, dg)

    return _call(body, name, (n // tile,),
                 [_rs(tile, d)] * 3 + [_ps(w_gate.shape), _rs(tile, d), _ps((1, d)), ANY],
                 [_rs(tile, d)] * 3 + [_ps((1, d))],
                 [S((n, d), F32), S((n, d), MXU_DTYPE), S((n, d), MXU_DTYPE), S((1, d), F32)],
                 sem=("arbitrary",))(dh, gl, pp, w_gate, h, g, after)


def loss_head(h, g, target, name):
    n, d = h.shape
    tile = _row_tile(n, 256)

    def body(h_ref, g_ref, t_ref, dh_ref, dg_ref, loss_ref):
        i = pl.program_id(0)
        x = h_ref[...]
        gg = g_ref[...]
        r = lax.rsqrt(jnp.mean(x * x, axis=-1, keepdims=True) + EPS)
        err = x * r * gg - t_ref[...]
        part = 0.5 * jnp.sum(jnp.mean(err * err, axis=-1, keepdims=True), axis=0, keepdims=True)
        dx, dg = _rms_bwd_math(x, gg, err * (1.0 / d))
        dh_ref[...] = dx
        _acc(dg_ref, i, dg)
        _acc(loss_ref, i, jnp.broadcast_to(part, (8, LANE)))

    return _call(body, name, (n // tile,), [_rs(tile, d), _ps((1, d)), _rs(tile, d)],
                 [_rs(tile, d), _ps((1, d)), _ps((8, LANE))], [S((n, d), F32), S((1, d), F32), S((8, LANE), F32)],
                 sem=("arbitrary",))(h, g, target)


def _adamw_math(w, g, m, v):
    c1 = np.float32(1.0 - ADAM_B1 ** ADAM_STEP)
    c2 = np.float32(1.0 - ADAM_B2 ** ADAM_STEP)
    mm = ADAM_B1 * m + (1.0 - ADAM_B1) * g
    vv = ADAM_B2 * v + (1.0 - ADAM_B2) * (g * g)
    return -ADAM_LR * ((mm / c1) / (jnp.sqrt(vv / c2) + ADAM_EPS) + ADAM_WD * w), mm, vv


def adamw_layer(w, pack, off, m, v, li, prev, name):
    n, r, c = w.shape

    def body(w_ref, g_ref, m_ref, v_ref, *refs):
        go_ref, d_ref, mo_ref, vo_ref = refs[-4:]
        g = g_ref[...]
        go_ref[...] = g
        d_ref[...], mo_ref[...], vo_ref[...] = _adamw_math(w_ref[...], g, m_ref[...], v_ref[...])

    if r % 8 == 0:
        cap = 2**21 // (4 * c) // 8 * 8
        tile = next(t for t in range(min(cap, r), 7, -8) if r % t == 0 and off % t == 0)
        ob, steps = off // tile, r // tile
        blk = pl.BlockSpec((None, tile, c), lambda i: (li, i, 0))
        g_spec = pl.BlockSpec((tile, c), lambda i: (ob + i, 0))
    else:
        assert off == 0 and pack.shape[0] == r and c % (2 * LANE) == 0
        steps = c // (2 * LANE)
        blk = pl.BlockSpec((None, r, 2 * LANE), lambda i: (li, 0, i))
        g_spec = pl.BlockSpec((r, 2 * LANE), lambda i: (0, i))
    prev = list(prev) if prev is not None else []
    return pl.pallas_call(
        body, name=name, grid=(steps,),
        in_specs=[blk, g_spec, blk, blk] + [ANY] * len(prev),
        out_specs=[blk] * 4, out_shape=[S((n, r, c), F32)] * 4,
        input_output_aliases={4 + j: j for j in range(len(prev))},
        compiler_params=pltpu.CompilerParams(dimension_semantics=("parallel",), vmem_limit_bytes=VMEM_LIMIT),
    )(w, pack, m, v, *prev)


def adamw(w, g, m, v, name):
    r, c = w.shape
    tile = r
    for t in (512, 256, 128, 64, 32, 16, 8):
        if r % t == 0 and t * c * 4 <= 2**21:
            tile = t
            break

    def body(w_ref, g_ref, m_ref, v_ref, d_ref, mo_ref, vo_ref):
        d_ref[...], mo_ref[...], vo_ref[...] = _adamw_math(w_ref[...], g_ref[...], m_ref[...], v_ref[...])

    return _call(body, name, (r // tile,), [_rs(tile, c)] * 4, [_rs(tile, c)] * 3, [S((r, c), F32)] * 3,
                 sem=("parallel",))(w, g, m, v)


TAP_VREGS = 32


def _taps(src, w_ref, offsets, tile, put, bias=None):
    c = src.shape[1]
    rp = max(8, TAP_VREGS * 8 * LANE // c // 8 * 8)
    for r0 in range(0, tile, rp):
        acc = jnp.zeros((rp, c), F32) if bias is None else jnp.zeros((rp, c), F32) + bias
        for k, o in enumerate(offsets):
            acc = acc + w_ref[k:k + 1, :] * src[r0 + o:r0 + o + rp, :]
        put(slice(r0, r0 + rp), acc)


def _taps_fwd(sc, w_ref, width, halo, tile, put, bias):
    _taps(sc, w_ref, [halo - (width - 1) + k for k in range(width)], tile, put, bias)


def _taps_bwd_x(sc_d, w_ref, width, tile, put):
    _taps(sc_d, w_ref, [(width - 1) - k for k in range(width)], tile, put)


def _taps_bwd_w(dy, sc, dw_ref, width, halo, tile, i):
    @pl.when(i == 0)
    def _():
        dw_ref[...] = jnp.zeros_like(dw_ref)

    for k in range(width):
        o = halo - (width - 1) + k
        dw_ref[k:k + 1, :] += jnp.sum(dy * sc[o:o + tile, :], axis=0, keepdims=True)


def _ln_stats(x):
    mu = jnp.mean(x, axis=-1, keepdims=True)
    xc = x - mu
    r = lax.rsqrt(jnp.mean(xc * xc, axis=-1, keepdims=True) + EPS)
    return xc * r, r


def conv_group_fwd(proj, cw, cb, lg, lb, name):
    n = proj.shape[0]
    d = cw.shape[1]
    tile = _row_tile(n, 256)
    halo = 32

    def body(v_ref, g_ref, vp_ref, gp_ref, cw_ref, cb_ref, lg_ref, lb_ref, u_ref, u1_ref, sc):
        i = pl.program_id(0)
        first = (i > 0).astype(F32)
        sc[0:halo, :] = vp_ref[tile - halo:, :] * _sigmoid(gp_ref[tile - halo:, :]) * first
        sc[halo:, :] = v_ref[...] * _sigmoid(g_ref[...])
        def put(rows, acc):
            u1_ref[rows, :] = acc

        _taps_fwd(sc, cw_ref, CONV_WIDTH, halo, tile, put, cb_ref[...])
        xh, _ = _ln_stats(u1_ref[...])
        y = xh * lg_ref[...] + lb_ref[...]
        u_ref[...] = (y * _sigmoid(y)).astype(u_ref.dtype)

    return _call(body, name, (n // tile,),
                 [_rs(tile, d, 0), _rs(tile, d, 1), _rs(tile, d, 0, -1), _rs(tile, d, 1, -1),
                  _ps(cw.shape), _ps((1, d)), _ps((1, d)), _ps((1, d))],
                 [_rs(tile, d), _rs(tile, d)], [S((n, d), MXU_DTYPE), S((n, d), F32)],
                 scratch=[pltpu.VMEM((halo + tile, d), F32)], sem=("arbitrary",))(proj, proj, proj, proj, cw, cb, lg, lb)


def conv_group_bwd(du, u1, proj, cw, lg, lb, name):
    n = proj.shape[0]
    d = cw.shape[1]
    tile = _row_tile(n, 256)
    halo = 32
    nt = n // tile

    def body(du_ref, dun_ref, u1_ref, u1n_ref, v_ref, g_ref, vp_ref, gp_ref, cw_ref, lg_ref, lb_ref,
             dp_ref, dcw_ref, dcb_ref, dlg_ref, dlb_ref, sc, sc_d):
        i = pl.program_id(0)

        def ln_swish_bwd(dy_, u1_):
            xh, r = _ln_stats(u1_)
            y = xh * lg_ref[...] + lb_ref[...]
            s = _sigmoid(y)
            dyy = dy_ * s * (1.0 + y * (1.0 - s))
            dxh = dyy * lg_ref[...]
            dx = r * (dxh - jnp.mean(dxh, axis=-1, keepdims=True) - xh * jnp.mean(dxh * xh, axis=-1, keepdims=True))
            return dx, jnp.sum(dyy * xh, axis=0, keepdims=True), jnp.sum(dyy, axis=0, keepdims=True)

        du1, dlg, dlb = ln_swish_bwd(du_ref[...].astype(F32), u1_ref[...])
        du1n, _, _ = ln_swish_bwd(dun_ref[0:halo, :].astype(F32), u1n_ref[0:halo, :])
        sc_d[0:tile, :] = du1
        sc_d[tile:, :] = du1n * (i < nt - 1).astype(F32)
        sc[0:halo, :] = vp_ref[tile - halo:, :] * _sigmoid(gp_ref[tile - halo:, :]) * (i > 0).astype(F32)
        sc[halo:, :] = v_ref[...] * _sigmoid(g_ref[...])

        def put(rows, du0):
            sig = _sigmoid(g_ref[rows, :])
            dp_ref[rows, :d] = (du0 * sig).astype(dp_ref.dtype)
            dp_ref[rows, d:] = (du0 * v_ref[rows, :] * sig * (1.0 - sig)).astype(dp_ref.dtype)

        _taps_bwd_x(sc_d, cw_ref, CONV_WIDTH, tile, put)
        _taps_bwd_w(du1, sc, dcw_ref, CONV_WIDTH, halo, tile, i)
        _acc(dcb_ref, i, jnp.sum(du1, axis=0, keepdims=True))
        _acc(dlg_ref, i, dlg)
        _acc(dlb_ref, i, dlb)

    return _call(body, name, (nt,),
                 [_rs(tile, d), _rs(tile, d, 0, 1, nt), _rs(tile, d), _rs(tile, d, 0, 1, nt),
                  _rs(tile, d, 0), _rs(tile, d, 1), _rs(tile, d, 0, -1), _rs(tile, d, 1, -1),
                  _ps(cw.shape), _ps((1, d)), _ps((1, d))],
                 [_rs(tile, 2 * d), _ps(cw.shape), _ps((1, d)), _ps((1, d)), _ps((1, d))],
                 [S((n, proj.shape[1]), MXU_DTYPE), S(cw.shape, F32), S((1, d), F32), S((1, d), F32), S((1, d), F32)],
                 scratch=[pltpu.VMEM((halo + tile, d), F32), pltpu.VMEM((tile + halo, d), F32)],
                 sem=("arbitrary",))(du, du, u1, u1, proj, proj, proj, proj, cw, lg, lb)


def ssm_conv_fwd(proj, dtr, sw, sb, dtb, name):
    n = proj.shape[0]
    w = sw.shape[1]
    inner = SSM_HEADS * HEAD_DIM
    tile = _row_tile(n, 256)
    halo = 8

    def body(x_ref, xp_ref, dtr_ref, sw_ref, sb_ref, dtb_ref, pre_ref, xs_ref, bc_ref, dt_ref, sc):
        i = pl.program_id(0)
        sc[0:halo, :] = xp_ref[tile - halo:, :] * (i > 0).astype(F32)
        sc[halo:, :] = x_ref[...]
        def put(rows, acc):
            pre_ref[rows, :] = acc

        _taps_fwd(sc, sw_ref, SSM_CONV, halo, tile, put, sb_ref[...])
        pre = pre_ref[...]
        act = pre * _sigmoid(pre)
        xs_ref[...] = act[:, :inner]
        bc_ref[...] = act[:, inner:]
        dt = _softplus(dtr_ref[...] + dtb_ref[...])
        dt_ref[...] = jnp.where(_iota(dt.shape, 1) < SSM_HEADS, dt, 0.0)

    return _call(body, name, (n // tile,),
                 [_rs(tile, w, 2), _rs(tile, w, 2, -1), _rs(tile, LANE), _ps(sw.shape), _ps((1, w)), _ps((1, LANE))],
                 [_rs(tile, w), _rs(tile, inner), _rs(tile, w - inner), _rs(tile, LANE)],
                 [S((n, w), F32), S((n, inner), F32), S((n, w - inner), F32), S((n, LANE), F32)],
                 scratch=[pltpu.VMEM((halo + tile, w), F32)], sem=("arbitrary",))(proj, proj, dtr, sw, sb, dtb)


def ssm_conv_bwd(dxs, dbc, pre, proj, sw, dproj, name):
    n = proj.shape[0]
    w = sw.shape[1]
    inner = SSM_HEADS * HEAD_DIM
    tile = _row_tile(n, 256)
    halo = 8
    nt = n // tile

    def body(dxs_ref, dxsn_ref, dbc_ref, dbcn_ref, pre_ref, pren_ref, x_ref, xp_ref, sw_ref, dp_in_ref,
             dx_ref, dsw_ref, dsb_ref, sc, sc_d):
        i = pl.program_id(0)

        def silu_bwd(d_, p_):
            s = _sigmoid(p_)
            return d_ * s * (1.0 + p_ * (1.0 - s))

        sc_d[0:tile, :inner] = silu_bwd(dxs_ref[...], pre_ref[:, :inner])
        sc_d[0:tile, inner:] = silu_bwd(dbc_ref[...], pre_ref[:, inner:])
        last = (i < nt - 1).astype(F32)
        sc_d[tile:, :inner] = silu_bwd(dxsn_ref[0:halo, :], pren_ref[0:halo, :inner]) * last
        sc_d[tile:, inner:] = silu_bwd(dbcn_ref[0:halo, :], pren_ref[0:halo, inner:]) * last
        sc[0:halo, :] = xp_ref[tile - halo:, :] * (i > 0).astype(F32)
        sc[halo:, :] = x_ref[...]
        dpre = sc_d[0:tile, :]
        def put(rows, acc):
            dx_ref[rows, :] = acc.astype(dx_ref.dtype)

        _taps_bwd_x(sc_d, sw_ref, SSM_CONV, tile, put)
        _taps_bwd_w(dpre, sc, dsw_ref, SSM_CONV, halo, tile, i)
        _acc(dsb_ref, i, jnp.sum(dpre, axis=0, keepdims=True))

    return pl.pallas_call(
        body, name=name, grid=(nt,),
        in_specs=[_rs(tile, inner), _rs(tile, inner, 0, 1, nt), _rs(tile, w - inner), _rs(tile, w - inner, 0, 1, nt),
                  _rs(tile, w), _rs(tile, w, 0, 1, nt), _rs(tile, w, 2), _rs(tile, w, 2, -1), _ps(sw.shape), ANY],
        out_specs=[_rs(tile, w, 2), _ps(sw.shape), _ps((1, w))],
        out_shape=[S(dproj.shape, dproj.dtype), S(sw.shape, F32), S((1, w), F32)],
        scratch_shapes=[pltpu.VMEM((halo + tile, w), F32), pltpu.VMEM((tile + halo, w), F32)],
        input_output_aliases={9: 0},
        compiler_params=pltpu.CompilerParams(dimension_semantics=("arbitrary",), vmem_limit_bytes=VMEM_LIMIT),
    )(dxs, dxs, dbc, dbc, pre, pre, proj, proj, sw, dproj)


def _ssd_prologue(dt_ref, dtT_ref, al_ref, alc_ref):
    row = _iota((CHUNK, CHUNK), 0)
    col = _iota((CHUNK, CHUNK), 1)
    dt = dt_ref[:, :SSM_HEADS]
    a_row = -jnp.exp(al_ref[:, :SSM_HEADS])
    a_col = -jnp.exp(alc_ref[...])
    cs = _01mm((row >= col).astype(F32), dt * a_row)
    csT = _mm01(dtT_ref[...] * a_col, (row <= col).astype(F32))
    return dt, a_row, cs, csT, row, col


def _decay(cs, csT, h, row, col):
    lm = jnp.exp(jnp.where(row >= col, cs[:, h:h + 1] - csT[h:h + 1, :], -1e30))
    lmT = jnp.exp(jnp.where(col >= row, csT[h:h + 1, :] - cs[:, h:h + 1], -1e30))
    return lm, lmT


def ssd_fwd(xs, bc, dt, dtT, alog_row, alog_col, name):
    n, width = xs.shape
    nc = n // CHUNK
    gw = width // SSM_GROUPS
    hpg = SSM_HEADS // SSM_GROUPS
    ns = SSM_STATE

    def body(xs_ref, bc_ref, dt_ref, dtT_ref, al_ref, alc_ref, y_ref, hs_ref, h_sc):
        i = pl.program_id(0)

        @pl.when(i == 0)
        def _():
            h_sc[...] = jnp.zeros_like(h_sc)

        dt, a_row, cs, csT, row, col = _ssd_prologue(dt_ref, dtT_ref, al_ref, alc_ref)
        indT = _head_indicator(width, SSM_HEADS, transposed=True)
        dt_full = _mm01(dt, indT)
        e_full = jnp.exp(_mm01(cs, indT))
        dte_full = jnp.exp(_mm01(cs[CHUNK - 1:CHUNK, :] - cs, indT))
        xt = xs_ref[...] * dt_full
        hs_ref[0] = h_sc[...]
        lo = _iota((CHUNK, 2 * HEAD_DIM), 1) < HEAD_DIM
        for g in range(SSM_GROUPS):
            bg = bc_ref[:, g * ns:(g + 1) * ns]
            cg = bc_ref[:, (SSM_GROUPS + g) * ns:(SSM_GROUPS + g + 1) * ns]
            gm = _mm(cg, bg, NT)
            hg = h_sc[g * gw:(g + 1) * gw, :]
            yoff = e_full[:, g * gw:(g + 1) * gw] * _mm(cg, hg, NT)
            for pr in range(hpg // 2):
                h0 = g * hpg + 2 * pr
                c0 = h0 * HEAD_DIM
                xp = xt[:, c0:c0 + 2 * HEAD_DIM]
                m0 = gm * _decay(cs, csT, h0, row, col)[0]
                m1 = gm * _decay(cs, csT, h0 + 1, row, col)[0]
                yd = jnp.where(lo, _mm(m0, xp), _mm(m1, xp))
                y_ref[:, c0:c0 + 2 * HEAD_DIM] = yd + yoff[:, 2 * pr * HEAD_DIM:(2 * pr + 2) * HEAD_DIM]
            sg = _mm(xt[:, g * gw:(g + 1) * gw] * dte_full[:, g * gw:(g + 1) * gw], bg, TN)
            for hh in range(hpg):
                h = g * hpg + hh
                r0 = h * HEAD_DIM
                h_sc[r0:r0 + HEAD_DIM, :] = (h_sc[r0:r0 + HEAD_DIM, :] * jnp.exp(csT[h:h + 1, CHUNK - 1:CHUNK])
                                             + sg[hh * HEAD_DIM:(hh + 1) * HEAD_DIM, :])

    bcw = bc.shape[1]
    return _call(body, name, (nc,),
                 [_rs(CHUNK, width), _rs(CHUNK, bcw), _rs(CHUNK, LANE), pl.BlockSpec((SSM_HEADS, CHUNK), lambda i: (0, i)),
                  _ps((1, LANE)), _ps((SSM_HEADS, 1))],
                 [_rs(CHUNK, width), pl.BlockSpec((1, width, ns), lambda i: (i, 0, 0))],
                 [S((n, width), F32), S((nc, width, ns), F32)],
                 scratch=[pltpu.VMEM((width, ns), F32)], sem=("arbitrary",))(xs, bc, dt, dtT, alog_row, alog_col)


def ssd_bwd(xs, bc, dt, dtT, alog_row, alog_col, hs, dy, dxs_skip, name):
    n, width = xs.shape
    nc = n // CHUNK
    gw = width // SSM_GROUPS
    hpg = SSM_HEADS // SSM_GROUPS
    ns = SSM_STATE
    bcw = bc.shape[1]

    def body(xs_ref, bc_ref, dt_ref, dtT_ref, al_ref, alc_ref, hs_ref, dy_ref, skip_ref,
             dxs_ref, dbc_ref, ddtr_ref, dal_ref, ddtb_ref, dh_sc, dxt_sc):
        i = pl.program_id(0)

        @pl.when(i == 0)
        def _():
            dh_sc[...] = jnp.zeros_like(dh_sc)

        dt, a_row, cs, csT, row, col = _ssd_prologue(dt_ref, dtT_ref, al_ref, alc_ref)
        indT = _head_indicator(width, SSM_HEADS, transposed=True)
        ind = _head_indicator(width, SSM_HEADS)
        dt_full = _mm01(dt, indT)
        e_full = jnp.exp(_mm01(cs, indT))
        cs_last = cs[CHUNK - 1:CHUNK, :]
        dte = jnp.exp(cs_last - cs)
        dte_full = _mm01(dte, indT)
        xs_ = xs_ref[...]
        xt = xs_ * dt_full
        dy_ = dy_ref[...]
        hmat = hs_ref[0]
        ds = dh_sc[...]
        lo = _iota((CHUNK, 2 * HEAD_DIM), 1) < HEAD_DIM
        head_lane = _iota((1, SSM_HEADS), 1)
        dcs = jnp.zeros((CHUNK, SSM_HEADS), F32)
        ddte = jnp.zeros((CHUNK, SSM_HEADS), F32)
        for g in range(SSM_GROUPS):
            sl = slice(g * gw, (g + 1) * gw)
            bg = bc_ref[:, g * ns:(g + 1) * ns]
            cg = bc_ref[:, (SSM_GROUPS + g) * ns:(SSM_GROUPS + g + 1) * ns]
            gm = _mm(cg, bg, NT)
            gmT = _mm(bg, cg, NT)
            hg = hmat[sl, :]
            dsg = ds[sl, :]
            dyg = dy_[:, sl]
            xtg = xt[:, sl]
            yoff = e_full[:, sl] * _mm(cg, hg, NT)
            edy = e_full[:, sl] * dyg
            bds = _mm(bg, dsg, NT)
            dxt_g = dte_full[:, sl] * bds
            ddte = ddte + _mm01(xtg * bds, ind[sl, :])
            dcs = dcs + _mm01(dyg * yoff, ind[sl, :])
            db = _mm(xtg * dte_full[:, sl], dsg)
            dc = _mm(edy, hg)
            dhc = _mm(edy, cg, TN)
            dgs = jnp.zeros((CHUNK, CHUNK), F32)
            dgTs = jnp.zeros((CHUNK, CHUNK), F32)
            for pr in range(hpg // 2):
                h0 = g * hpg + 2 * pr
                c0 = 2 * pr * HEAD_DIM
                xp = xtg[:, c0:c0 + 2 * HEAD_DIM]
                dyp = dyg[:, c0:c0 + 2 * HEAD_DIM]
                rr = []
                for h, half in ((h0, lo), (h0 + 1, jnp.logical_not(lo))):
                    lm, lmT = _decay(cs, csT, h, row, col)
                    xm = jnp.where(half, xp, 0.0)
                    dm = _mm(dyp, xm, NT)
                    dmT = _mm(xm, dyp, NT)
                    mT = gmT * lmT
                    z = jnp.sum(dm * (gm * lm), axis=1, keepdims=True) - jnp.sum(dmT * mT, axis=1, keepdims=True)
                    dcs = dcs + z * (head_lane == h).astype(F32)
                    dgs = dgs + dm * lm
                    dgTs = dgTs + dmT * lmT
                    rr.append(_mm(mT, dyp))
                dxt_sc[:, g * gw + c0:g * gw + c0 + 2 * HEAD_DIM] = jnp.where(lo, rr[0], rr[1]) + dxt_g[:, c0:c0 + 2 * HEAD_DIM]
            dbc_ref[:, g * ns:(g + 1) * ns] = db + _mm(dgTs, cg)
            dbc_ref[:, (SSM_GROUPS + g) * ns:(SSM_GROUPS + g + 1) * ns] = dc + _mm(dgs, bg)
            for hh in range(hpg):
                h = g * hpg + hh
                r0 = h * HEAD_DIM
                dh_sc[r0:r0 + HEAD_DIM, :] = (dhc[hh * HEAD_DIM:(hh + 1) * HEAD_DIM, :]
                                              + jnp.exp(csT[h:h + 1, CHUNK - 1:CHUNK]) * ds[r0:r0 + HEAD_DIM, :])
        t = ddte * dte
        per_head = jnp.sum(jnp.sum(ds * hmat, axis=1, keepdims=True) * ind, axis=0, keepdims=True)
        last_add = jnp.sum(t, axis=0, keepdims=True) + jnp.exp(cs_last) * per_head
        dcs = dcs - t + jnp.where(_iota((CHUNK, SSM_HEADS), 0) == CHUNK - 1, last_add, 0.0)
        dadt = _01mm((row <= col).astype(F32), dcs)
        dxt = dxt_sc[...]
        ddt = dadt * a_row + _mm01(dxt * xs_, ind)
        dxs_ref[...] = dxt * dt_full + skip_ref[...]
        ddtr = ddt * (1.0 - jnp.exp(-dt))
        ddtr_ref[...] = jnp.zeros_like(ddtr_ref)
        ddtr_ref[:, :SSM_HEADS] = ddtr.astype(ddtr_ref.dtype)
        _acc(dal_ref, i, jnp.sum(dadt * dt, axis=0, keepdims=True) * a_row)
        _acc(ddtb_ref, i, jnp.sum(ddtr, axis=0, keepdims=True))

    rev = lambda i: (nc - 1 - i, 0)
    return _call(body, name, (nc,),
                 [pl.BlockSpec((CHUNK, width), rev), pl.BlockSpec((CHUNK, bcw), rev), pl.BlockSpec((CHUNK, LANE), rev),
                  pl.BlockSpec((SSM_HEADS, CHUNK), lambda i: (0, nc - 1 - i)), _ps((1, LANE)), _ps((SSM_HEADS, 1)),
                  pl.BlockSpec((1, width, ns), lambda i: (nc - 1 - i, 0, 0)), pl.BlockSpec((CHUNK, width), rev),
                  pl.BlockSpec((CHUNK, width), rev)],
                 [pl.BlockSpec((CHUNK, width), rev), pl.BlockSpec((CHUNK, bcw), rev), pl.BlockSpec((CHUNK, LANE), rev),
                  _ps((1, SSM_HEADS)), _ps((1, SSM_HEADS))],
                 [S((n, width), F32), S((n, bcw), F32), S((n, LANE), MXU_DTYPE), S((1, SSM_HEADS), F32), S((1, SSM_HEADS), F32)],
                 scratch=[pltpu.VMEM((width, ns), F32), pltpu.VMEM((CHUNK, width), F32)],
                 sem=("arbitrary",))(xs, bc, dt, dtT, alog_row, alog_col, hs, dy, dxs_skip)


def ssm_gate_fwd(yssd, xs, proj, dfull, gamma, name):
    n, d = yssd.shape
    tile = _row_tile(n, 256)
    gw = d // SSM_GROUPS

    def body(y_ref, xs_ref, z_ref, df_ref, gm_ref, o_ref):
        z = z_ref[...]
        y2 = (y_ref[...] + df_ref[...] * xs_ref[...]) * (z * _sigmoid(z))
        for g in range(SSM_GROUPS):
            yg = y2[:, g * gw:(g + 1) * gw]
            r = lax.rsqrt(jnp.mean(yg * yg, axis=-1, keepdims=True) + EPS)
            o_ref[:, g * gw:(g + 1) * gw] = (yg * r * gm_ref[:, g * gw:(g + 1) * gw]).astype(o_ref.dtype)

    return _call(body, name, (n // tile,), [_rs(tile, d), _rs(tile, d), _rs(tile, d, 2), _ps((1, d)), _ps((1, d))],
                 _rs(tile, d), S((n, d), MXU_DTYPE), sem=("parallel",))(yssd, xs, proj, dfull, gamma)


def ssm_gate_bwd(dy3, yssd, xs, proj, dfull, gamma, dproj, name):
    n, d = yssd.shape
    tile = _row_tile(n, 256)
    gw = d // SSM_GROUPS

    def body(dy_ref, y_ref, xs_ref, z_ref, df_ref, gm_ref, dp_in_ref, dys_ref, dxs_ref, dz_ref, dgm_ref, dd_ref):
        i = pl.program_id(0)
        z = z_ref[...]
        s = _sigmoid(z)
        xs_ = xs_ref[...]
        y1 = y_ref[...] + df_ref[...] * xs_
        y2 = y1 * (z * s)
        dy_ = dy_ref[...].astype(F32)
        dgm = []
        dy2 = []
        for g in range(SSM_GROUPS):
            sl = slice(g * gw, (g + 1) * gw)
            dxg, dgg = _rms_bwd_math(y2[:, sl], gm_ref[:, sl], dy_[:, sl])
            dy2.append(dxg)
            dgm.append(dgg)
        dy2 = jnp.concatenate(dy2, axis=1)
        dy1 = dy2 * (z * s)
        dys_ref[...] = dy1
        dxs_ref[...] = dy1 * df_ref[...]
        dz_ref[...] = (dy2 * y1 * s * (1.0 + z * (1.0 - s))).astype(dz_ref.dtype)
        _acc(dgm_ref, i, jnp.concatenate(dgm, axis=1))
        colsum = jnp.broadcast_to(jnp.sum(dy1 * xs_, axis=0, keepdims=True), (8, d))
        _acc(dd_ref, i, _mm01(colsum, _head_indicator(d, SSM_HEADS))[0:1, :])

    return pl.pallas_call(
        body, name=name, grid=(n // tile,),
        in_specs=[_rs(tile, d), _rs(tile, d), _rs(tile, d), _rs(tile, d, 2), _ps((1, d)), _ps((1, d)), ANY],
        out_specs=[_rs(tile, d), _rs(tile, d), _rs(tile, d, 2), _ps((1, d)), _ps((1, SSM_HEADS))],
        out_shape=[S((n, d), F32), S((n, d), F32), S(dproj.shape, dproj.dtype), S((1, d), F32), S((1, SSM_HEADS), F32)],
        input_output_aliases={6: 2},
        compiler_params=pltpu.CompilerParams(dimension_semantics=("arbitrary",), vmem_limit_bytes=VMEM_LIMIT),
    )(dy3, yssd, xs, proj, dfull, gamma, dproj)


def _rope128(x, cos, sin_signed):
    half = HEAD_DIM // 2
    lane = _iota(x.shape, 1)
    partner = jnp.where((lane % HEAD_DIM) < half, pltpu.roll(x, LANE - half, 1), pltpu.roll(x, half, 1))
    return x * cos + partner * sin_signed


def rope_fwd(qkv, cos, sin, name):
    n, w = qkv.shape
    qw = ATT_HEADS * HEAD_DIM
    kw = ATT_KV_HEADS * HEAD_DIM
    tile = _row_tile(n, 256)

    def body(x_ref, c_ref, s_ref, q_ref, k_ref, v_ref):
        c, s = c_ref[...], s_ref[...]
        for j in range(qw // LANE):
            q_ref[:, j * LANE:(j + 1) * LANE] = _rope128(x_ref[:, j * LANE:(j + 1) * LANE], c, s).astype(q_ref.dtype)
        for j in range(kw // LANE):
            k_ref[:, j * LANE:(j + 1) * LANE] = _rope128(x_ref[:, qw + j * LANE:qw + (j + 1) * LANE], c, s).astype(k_ref.dtype)
        v_ref[...] = x_ref[:, qw + kw:].astype(v_ref.dtype)

    return _call(body, name, (n // tile,), [_rs(tile, w), _rs(tile, LANE), _rs(tile, LANE)],
                 [_rs(tile, qw), _rs(tile, kw), _rs(tile, kw)],
                 [S((n, qw), MXU_DTYPE), S((n, kw), MXU_DTYPE), S((n, kw), MXU_DTYPE)], sem=("parallel",))(qkv, cos, sin)


ATT_GROUP = ATT_HEADS // ATT_KV_HEADS


def _attn_mask(i):
    row = _iota((ATT_GROUP * WINDOW, 2 * WINDOW), 0) % WINDOW
    s = _iota((ATT_GROUP * WINDOW, 2 * WINDOW), 1)
    return (s > row) & (s <= row + WINDOW) & ((s >= WINDOW) | (i > 0))


def _stack_heads(ref, j, kh, lo):
    parts = []
    for t in range(ATT_GROUP):
        h = ATT_GROUP * j + t
        blk = ref[:, (h // 2) * LANE:(h // 2 + 1) * LANE]
        blk = jnp.where(lo if h % 2 == 0 else jnp.logical_not(lo), blk, jnp.zeros_like(blk))
        parts.append(blk if h % 2 == kh else pltpu.roll(blk, HEAD_DIM, 1))
    return jnp.concatenate(parts, axis=0)


def _unstack_heads(stacked, j, kh, lo, put):
    for t in range(0, ATT_GROUP, 2):
        h = ATT_GROUP * j + t
        even = stacked[t * WINDOW:(t + 1) * WINDOW, :]
        odd = stacked[(t + 1) * WINDOW:(t + 2) * WINDOW, :]
        even = even if kh == 0 else pltpu.roll(even, HEAD_DIM, 1)
        odd = odd if kh == 1 else pltpu.roll(odd, HEAD_DIM, 1)
        put(h // 2, jnp.where(lo, even, odd))


def _per_head_rows(ref, j):
    return jnp.concatenate([ref[:, ATT_GROUP * j + t:ATT_GROUP * j + t + 1] for t in range(ATT_GROUP)], axis=0)


def _per_head_scalar(ref, j):
    rows = _iota((ATT_GROUP * WINDOW, 1), 0) // WINDOW
    out = jnp.zeros((ATT_GROUP * WINDOW, 1), F32)
    for t in range(ATT_GROUP):
        out = out + jnp.where(rows == t, ref[:, ATT_GROUP * j + t:ATT_GROUP * j + t + 1], 0.0)
    return out


def attn_fwd(q, k, v, sinks, name):
    n, qw = q.shape
    kw = k.shape[1]
    nb = n // WINDOW
    scale = HEAD_DIM ** -0.5

    def body(q_ref, kc_ref, kp_ref, vc_ref, vp_ref, sk_ref, o_ref, lse_ref):
        i = pl.program_id(0)
        valid = _attn_mask(i)
        lo = _iota((WINDOW, LANE), 1) < HEAD_DIM
        k2 = jnp.concatenate([kp_ref[...], kc_ref[...]], axis=0)
        v2 = jnp.concatenate([vp_ref[...], vc_ref[...]], axis=0)
        lane1 = _iota((1, LANE), 1)
        lse = jnp.zeros((WINDOW, LANE), F32)

        def put_o(qb, val):
            o_ref[:, qb * LANE:(qb + 1) * LANE] = val.astype(o_ref.dtype)

        for j in range(ATT_KV_HEADS):
            kb, kh = j // 2, j % 2
            q4 = _stack_heads(q_ref, j, kh, lo)
            logits = jnp.where(valid, _mm(q4, k2[:, kb * LANE:(kb + 1) * LANE], NT) * scale, -1e30)
            sk = _per_head_scalar(sk_ref, j)
            m = jnp.maximum(jnp.max(logits, axis=-1, keepdims=True), sk)
            e = jnp.exp(logits - m)
            den = jnp.sum(e, axis=-1, keepdims=True) + jnp.exp(sk - m)
            lse4 = m + jnp.log(den)
            for t in range(ATT_GROUP):
                lse = lse + lse4[t * WINDOW:(t + 1) * WINDOW, :] * (lane1 == ATT_GROUP * j + t).astype(F32)
            _unstack_heads(_mm(e * (1.0 / den), v2[:, kb * LANE:(kb + 1) * LANE]), j, kh, lo, put_o)
        lse_ref[...] = lse

    return _call(body, name, (nb,),
                 [_rs(WINDOW, qw), _rs(WINDOW, kw), _rs(WINDOW, kw, 0, -1), _rs(WINDOW, kw), _rs(WINDOW, kw, 0, -1), _ps((1, LANE))],
                 [_rs(WINDOW, qw), _rs(WINDOW, LANE)], [S((n, qw), MXU_DTYPE), S((n, LANE), F32)],
                 sem=("parallel",))(q, k, k, v, v, sinks)


def attn_bwd(q, k, v, o, do, lse, sinks, name):
    n, qw = q.shape
    kw = k.shape[1]
    nb = n // WINDOW
    scale = HEAD_DIM ** -0.5

    def body(q_ref, kc_ref, kp_ref, vc_ref, vp_ref, o_ref, do_ref, lse_ref, sk_ref,
             dq_ref, dka_ref, dkb_ref, dva_ref, dvb_ref, dsk_ref):
        i = pl.program_id(0)
        valid = _attn_mask(i)
        lo = _iota((WINDOW, LANE), 1) < HEAD_DIM
        k2 = jnp.concatenate([kp_ref[...], kc_ref[...]], axis=0)
        v2 = jnp.concatenate([vp_ref[...], vc_ref[...]], axis=0)
        lane1 = _iota((1, LANE), 1)
        do_ = do_ref[...].astype(F32)
        delta = _mm01(do_ * o_ref[...].astype(F32), _head_indicator(qw, ATT_HEADS))
        dk2 = [jnp.zeros((2 * WINDOW, LANE), F32) for _ in range(kw // LANE)]
        dv2 = [jnp.zeros((2 * WINDOW, LANE), F32) for _ in range(kw // LANE)]
        dsk = jnp.zeros((1, LANE), F32)

        def put_dq(qb, val):
            dq_ref[:, qb * LANE:(qb + 1) * LANE] = val

        for j in range(ATT_KV_HEADS):
            kb, kh = j // 2, j % 2
            q4 = _stack_heads(q_ref, j, kh, lo)
            do4 = _stack_heads(do_ref, j, kh, lo)
            kk = k2[:, kb * LANE:(kb + 1) * LANE]
            vv = v2[:, kb * LANE:(kb + 1) * LANE]
            logits = jnp.where(valid, _mm(q4, kk, NT) * scale, -1e30)
            lse4 = _per_head_rows(lse_ref, j)
            p = jnp.exp(logits - lse4)
            dl = jnp.concatenate([delta[:, ATT_GROUP * j + t:ATT_GROUP * j + t + 1] for t in range(ATT_GROUP)], axis=0)
            ds = p * (_mm(do4, vv, NT) - dl) * scale
            sd = jnp.exp(_per_head_scalar(sk_ref, j) - lse4) * dl
            for t in range(ATT_GROUP):
                dsk = dsk - (jnp.sum(sd[t * WINDOW:(t + 1) * WINDOW, :], axis=0, keepdims=True)
                             * (lane1 == ATT_GROUP * j + t).astype(F32))
            _unstack_heads(_mm(ds, kk), j, kh, lo, put_dq)
            dk2[kb] = dk2[kb] + _mm(ds, q4, TN)
            dv2[kb] = dv2[kb] + _mm(p, do4, TN)
        for kb in range(kw // LANE):
            dkb_ref[:, kb * LANE:(kb + 1) * LANE] = dk2[kb][0:WINDOW, :]
            dka_ref[:, kb * LANE:(kb + 1) * LANE] = dk2[kb][WINDOW:, :]
            dvb_ref[:, kb * LANE:(kb + 1) * LANE] = dv2[kb][0:WINDOW, :]
            dva_ref[:, kb * LANE:(kb + 1) * LANE] = dv2[kb][WINDOW:, :]
        _acc(dsk_ref, i, dsk)

    return _call(body, name, (nb,),
                 [_rs(WINDOW, qw), _rs(WINDOW, kw), _rs(WINDOW, kw, 0, -1), _rs(WINDOW, kw), _rs(WINDOW, kw, 0, -1),
                  _rs(WINDOW, qw), _rs(WINDOW, qw), _rs(WINDOW, LANE), _ps((1, LANE))],
                 [_rs(WINDOW, qw)] + [_rs(WINDOW, kw)] * 4 + [_ps((1, LANE))],
                 [S((n, qw), F32)] + [S((n, kw), F32)] * 4 + [S((1, LANE), F32)],
                 sem=("arbitrary",))(q, k, k, v, v, o, do, lse, sinks)


def attn_grad_merge(dq, dka, dkb, dva, dvb, cos, sin, name):
    n, qw = dq.shape
    kw = dka.shape[1]
    nb = n // WINDOW
    w = qw + 2 * kw

    def body(dq_ref, dka_ref, dkb_ref, dva_ref, dvb_ref, c_ref, s_ref, o_ref, db_ref):
        i = pl.program_id(0)
        c, s = c_ref[...], -s_ref[...]
        nxt = (i < nb - 1).astype(F32)

        @pl.when(i == 0)
        def _():
            db_ref[...] = jnp.zeros_like(db_ref)

        def put(c0, val):
            o_ref[:, c0:c0 + val.shape[1]] = val.astype(o_ref.dtype)
            db_ref[:, c0:c0 + val.shape[1]] += jnp.sum(val, axis=0, keepdims=True)

        for j in range(qw // LANE):
            put(j * LANE, _rope128(dq_ref[:, j * LANE:(j + 1) * LANE], c, s))
        for j in range(kw // LANE):
            sl = slice(j * LANE, (j + 1) * LANE)
            put(qw + j * LANE, _rope128(dka_ref[:, sl] + dkb_ref[:, sl] * nxt, c, s))
        put(qw + kw, dva_ref[...] + dvb_ref[...] * nxt)

    return _call(body, name, (nb,),
                 [_rs(WINDOW, qw), _rs(WINDOW, kw), _rs(WINDOW, kw, 0, 1, nb), _rs(WINDOW, kw), _rs(WINDOW, kw, 0, 1, nb),
                  _rs(WINDOW, LANE), _rs(WINDOW, LANE)],
                 [_rs(WINDOW, w), _ps((1, w))], [S((n, w), MXU_DTYPE), S((1, w), F32)],
                 sem=("arbitrary",))(dq, dka, dkb, dva, dvb, cos, sin)


def _row(v):
    return v.reshape(1, -1)


def _pad_lanes(v, width=LANE):
    return jnp.pad(v.reshape(1, -1), ((0, 0), (0, width - v.size)))


class LayerWeights(dict):
    def __init__(self, small, fetch):
        super().__init__(small)
        self.fetch = fetch

    def need(self, k, after):
        if k not in self:
            self[k] = self.fetch(k, after)
        return self[k]


def ffn_fwd(h, g, w, keys, tag):
    xn, u, hm = swiglu_in(h, _row(g), w.need(keys[0], h), f"{tag}_in")
    return matmul(hm, w.need(keys[1], hm), "nn", f"{tag}_out", scale=0.5, res=h), (h, xn, u, hm)


class GradSink:
    ORDER = ("ffn1_w_out", "ffn2_w_out", "ple_gate_w", "att_w_o", "hyb_w_out", "ffn1_w_in", "ffn2_w_in", "att_w_qkv",
             "ple_proj_w", "hyb_w_in")

    def __init__(self, shard_shapes, bucket_of):
        self.where, rows = {}, {}
        for k in self.ORDER:
            n, r, c = shard_shapes[k]
            for li in range(n):
                layer = li if k in PER_LAYER else 2 * li + (0 if k in EVEN_ONLY else 1)
                rows_b = rows.setdefault(bucket_of(layer, _stage(k)), {})
                key = c if r % 32 == 0 else k
                off = -(-rows_b.get(key, (0, c))[0] // r) * r
                rows_b[key] = (-(-(off + r) // 32) * 32, c)
                self.where[k, li] = (bucket_of(layer, _stage(k)), key, "r" if _shard_axis(k) == 1 else "c", off, r)
        self.bufs = {b: {key: lax.empty((N_CHIPS, r, c), MXU_DTYPE) for key, (r, c) in rows_b.items()}
                     for b, rows_b in rows.items()}

    def mm(self, k, li, a, b, name, scale=None, c0=0, paired=False):
        bucket, key, kind, off, r = self.where[k, li]
        buf = self.bufs[bucket][key]
        slot = Slot(buf, kind, r if kind == "r" else buf.shape[2], off, c0, paired)
        self.bufs[bucket][key] = matmul(a, b, "tn", name, scale=scale, into=slot)

    def put(self, k, li, chip_major):
        b, key = self.where[k, li][:2]
        pad = self.bufs[b][key].shape[1] - chip_major.shape[1]
        self.bufs[b][key] = jnp.pad(chip_major.astype(self.bufs[b][key].dtype), ((0, 0), (0, pad), (0, 0)))


def ffn_bwd(dh, g, w_in, w_out, saved, tag, sink, keys, layer, after, colsum=False):
    h, xn, u, hm = saved
    sink.mm(keys[1], layer, hm, dh, f"{tag}_dwout", scale=0.5)
    du = swiglu_out_bwd(dh, w_out, u, after, f"{tag}_dhm")
    sink.mm(keys[0], layer, xn, du, f"{tag}_dwin", paired=True)
    outs = nt_rms_bwd(du, ColSharded(w_in.arr, paired=True), h, _row(g), dh, f"{tag}_dxn", colsum=colsum)
    return (outs[0], outs[1].reshape(-1)) + ((outs[2],) if colsum else ())


def _hyb_params(w):
    d = w["conv_dw_b"].size
    inner = SSM_HEADS * HEAD_DIM
    main = 3 * d + w["ssm_conv_b"].size
    return dict(
        w_main=w["hyb_w_in"][:main], w_dt=jnp.pad(w["hyb_w_in"][main:], ((0, LANE - SSM_HEADS), (0, 0))),
        cw=jnp.pad(w["conv_dw_w"], ((0, 32 - CONV_WIDTH), (0, 0))), cb=_row(w["conv_dw_b"]),
        lg=_row(w["conv_ln_g"]), lb=_row(w["conv_ln_b"]),
        sw=jnp.pad(w["ssm_conv_w"], ((0, 8 - SSM_CONV), (0, 0))), sb=_row(w["ssm_conv_b"]),
        dtb=_pad_lanes(w["ssm_dt_bias"]), al_row=_pad_lanes(w["ssm_a_log"]), al_col=w["ssm_a_log"].reshape(-1, 1),
        dfull=_row(jnp.repeat(w["ssm_d"], HEAD_DIM)), gamma=_row(w["ssm_norm"]), d=d, inner=inner, main=main)


def hyb_fwd(h, w, tag):
    w.need("hyb_w_in", h)
    q = _hyb_params(w)
    proj, xn = rms_matmul(h, _row(w["norm_mix"]), q["w_main"], "nt", f"{tag}_in")
    dtr = matmul(xn, q["w_dt"], "nt", f"{tag}_in_dt")
    u, u1 = conv_group_fwd(proj, q["cw"], q["cb"], q["lg"], q["lb"], f"{tag}_conv")
    pre, xs, bc, dt = ssm_conv_fwd(proj, dtr, q["sw"], q["sb"], q["dtb"], f"{tag}_sconv")
    dtT = dt[:, :SSM_HEADS].T
    yssd, hs = ssd_fwd(xs, bc, dt, dtT, q["al_row"], q["al_col"], f"{tag}_ssd")
    y = ssm_gate_fwd(yssd, xs, proj, q["dfull"], q["gamma"], f"{tag}_gate")
    wo = w.need("hyb_w_out", u)
    h2 = matmul(u, wo[:q["d"]], "nn", f"{tag}_out_a", res=h)
    h2 = matmul(y, wo[q["d"]:], "nn", f"{tag}_out_b", res=h2)
    return h2, (h, xn, proj, u, u1, pre, xs, bc, dt, dtT, yssd, hs, y)


def hyb_bwd(dh, w, saved, tag, sink, layer):
    q = _hyb_params(w)
    h, xn, proj, u, u1, pre, xs, bc, dt, dtT, yssd, hs, y = saved
    du = matmul(dh, w["hyb_w_out"][:q["d"]], "nt", f"{tag}_du")
    dy3 = matmul(dh, w["hyb_w_out"][q["d"]:], "nt", f"{tag}_dy")
    sink.mm("hyb_w_out", layer, u, dh, f"{tag}_dwo_a", c0=0)
    sink.mm("hyb_w_out", layer, y, dh, f"{tag}_dwo_b", c0=N_CHIPS // 2)
    dproj, dcw, dcb, dlg, dlb = conv_group_bwd(du, u1, proj, q["cw"], q["lg"], q["lb"], f"{tag}_dconv")
    dyssd, dxs_skip, dproj, dgamma, dd = ssm_gate_bwd(dy3, yssd, xs, proj, q["dfull"], q["gamma"], dproj, f"{tag}_dgate")
    dxs, dbc, ddtr, dalog, ddtb = ssd_bwd(xs, bc, dt, dtT, q["al_row"], q["al_col"], hs, dyssd, dxs_skip, f"{tag}_dssd")
    dproj, dsw, dsb = ssm_conv_bwd(dxs, dbc, pre, proj, q["sw"], dproj, f"{tag}_dsconv")
    dw_in = jnp.concatenate([matmul(dproj, xn, "tn", f"{tag}_dwin"),
                             matmul(ddtr, xn, "tn", f"{tag}_dwin_dt")[:SSM_HEADS]], axis=0)
    sink.put("hyb_w_in", layer, dw_in.reshape((N_CHIPS, -1) + dw_in.shape[1:]))
    dh2, dg = nt_rms_bwd(dproj, q["w_main"], h, _row(w["norm_mix"]), dh, f"{tag}_dxn", extra=(ddtr, q["w_dt"]), b_kd=True)
    grads = dict(norm_mix=dg.reshape(-1), conv_dw_w=dcw[:CONV_WIDTH], conv_dw_b=dcb.reshape(-1),
                 conv_ln_g=dlg.reshape(-1), conv_ln_b=dlb.reshape(-1), ssm_conv_w=dsw[:SSM_CONV], ssm_conv_b=dsb.reshape(-1),
                 ssm_dt_bias=ddtb.reshape(-1), ssm_a_log=dalog.reshape(-1), ssm_d=dd.reshape(-1), ssm_norm=dgamma.reshape(-1))
    return dh2, grads


def rope_tables(n):
    half = HEAD_DIM // 2
    inv = ROPE_THETA ** (-jnp.arange(0, HEAD_DIM, 2, dtype=F32) / HEAD_DIM)
    ang = jnp.arange(n, dtype=F32)[:, None] * inv[None, :]
    cos, sin = jnp.cos(ang), jnp.sin(ang)
    reps = LANE // HEAD_DIM
    return jnp.tile(jnp.concatenate([cos, cos], axis=1), (1, reps)), jnp.tile(jnp.concatenate([-sin, sin], axis=1), (1, reps))


def att_fwd(h, w, tables, tag):
    cos, sin = tables
    qkv, xn = rms_matmul(h, _row(w["norm_mix"]), w.need("att_w_qkv", h), "nn", f"{tag}_qkv", bias=_row(w["att_b_qkv"]))
    q, k, v = rope_fwd(qkv, cos, sin, f"{tag}_rope")
    sinks = _pad_lanes(w["att_sinks"])
    o, lse = attn_fwd(q, k, v, sinks, f"{tag}_attn")
    h2 = matmul(o, w.need("att_w_o", o), "nn", f"{tag}_o", bias=_row(w["att_b_o"]), res=h)
    return h2, (h, xn, q, k, v, o, lse, sinks)


def att_bwd(dh, dh_colsum, w, saved, tables, tag, sink, layer):
    cos, sin = tables
    h, xn, q, k, v, o, lse, sinks = saved
    do = matmul(dh, w["att_w_o"], "nt", f"{tag}_do")
    sink.mm("att_w_o", layer, o, dh, f"{tag}_dwo")
    dq, dka, dkb, dva, dvb, dsk = attn_bwd(q, k, v, o, do, lse, sinks, f"{tag}_dattn")
    dqkv, dbqkv = attn_grad_merge(dq, dka, dkb, dva, dvb, cos, sin, f"{tag}_drope")
    sink.mm("att_w_qkv", layer, xn, dqkv, f"{tag}_dwqkv")
    dh2, dg = nt_rms_bwd(dqkv, w["att_w_qkv"], h, _row(w["norm_mix"]), dh, f"{tag}_dxn")
    grads = dict(norm_mix=dg.reshape(-1), att_b_qkv=dbqkv.reshape(-1), att_sinks=dsk[0, :ATT_HEADS],
                 att_b_o=dh_colsum.reshape(-1))
    return dh2, grads


def ple_block_fwd(h, pe, w, tag):
    pp = matmul(pe, w.need("ple_proj_w", h), "nn", f"{tag}_proj")
    out, gl, xn = ple_fwd(h, _row(w["ple_norm"]), w.need("ple_gate_w", h), pp, f"{tag}_gate")
    return out, (h, xn, gl, pp, pe)


def ple_block_bwd(dh, w, saved, tag, sink, layer, after):
    h, xn, gl, pp, pe = saved
    dh2, dpp, dgl, dg = ple_bwd(dh, gl, pp, w["ple_gate_w"], h, _row(w["ple_norm"]), after, f"{tag}_dgate")
    sink.mm("ple_proj_w", layer, pe, dpp, f"{tag}_dwp")
    sink.mm("ple_gate_w", layer, xn, dgl, f"{tag}_dwg")
    return dh2, dict(ple_norm=dg.reshape(-1))


PER_LAYER = ("norm_ffn1", "ffn1_w_in", "ffn1_w_out", "norm_mix", "norm_ffn2", "ffn2_w_in", "ffn2_w_out",
             "ple_norm", "ple_gate_w", "ple_proj_w")
EVEN_ONLY = ("hyb_w_in", "conv_dw_w", "conv_dw_b", "conv_ln_g", "conv_ln_b", "ssm_conv_w", "ssm_conv_b",
             "ssm_dt_bias", "ssm_a_log", "ssm_d", "ssm_norm", "hyb_w_out")
ODD_ONLY = ("att_w_qkv", "att_b_qkv", "att_sinks", "att_w_o", "att_b_o")


def _layer_index(k, i):
    if k in PER_LAYER:
        return i
    if k in (EVEN_ONLY if i % 2 == 0 else ODD_ONLY):
        return i // 2
    return None


def _stage(k):
    return 0 if k.startswith("ffn1") else (2 if k.startswith(("ffn2", "ple")) else 1)


def trunk_fwd_bwd(x, pe, target, layers, final_norm, sink, stage_done):
    depth = len(layers)
    tables = rope_tables(x.shape[0])
    h = x
    saved = []
    for i, w in enumerate(layers):
        h, s1 = ffn_fwd(h, w["norm_ffn1"], w, ("ffn1_w_in", "ffn1_w_out"), f"l{i}_ffn1")
        if i % 2 == 0:
            h, s2 = hyb_fwd(h, w, f"l{i}_hyb")
        else:
            h, s2 = att_fwd(h, w, tables, f"l{i}_att")
        h, s3 = ffn_fwd(h, w["norm_ffn2"], w, ("ffn2_w_in", "ffn2_w_out"), f"l{i}_ffn2")
        h, s4 = ple_block_fwd(h, pe[i], w, f"l{i}_ple")
        saved.append((s1, s2, s3, s4))
    dh, dgf, loss = loss_head(h, _row(final_norm), target, "loss_head")
    grads = {}
    tie = dgf
    for i in reversed(range(depth)):
        w = layers[i]
        s1, s2, s3, s4 = saved[i]
        dh, g = ple_block_bwd(dh, w, s4, f"l{i}_ple", sink, i, tie)
        odd = i % 2 == 1
        out = ffn_bwd(dh, w["norm_ffn2"], w["ffn2_w_in"], w["ffn2_w_out"], s3, f"l{i}_ffn2", sink,
                      ("ffn2_w_in", "ffn2_w_out"), i, tie, colsum=odd)
        dh = out[0]
        g.update(norm_ffn2=out[1])
        if odd:
            dh, gm = att_bwd(dh, out[2], w, s2, tables, f"l{i}_att", sink, i // 2)
        else:
            dh, gm = hyb_bwd(dh, w, s2, f"l{i}_hyb", sink, i // 2)
        g.update(gm)
        tie = stage_done(i, 1, tie)
        out = ffn_bwd(dh, w["norm_ffn1"], w["ffn1_w_in"], w["ffn1_w_out"], s1, f"l{i}_ffn1", sink,
                      ("ffn1_w_in", "ffn1_w_out"), i, tie)
        dh = out[0]
        g.update(norm_ffn1=out[1])
        tie = stage_done(i, 0, tie)
        for k, v in g.items():
            grads.setdefault(k, []).insert(0, v)
    grads = {k: jnp.stack(v) for k, v in grads.items()}
    grads["final_norm"] = dgf.reshape(-1)
    return loss, dh, grads


def _me():
    return lax.axis_index("x"), lax.axis_index("y"), lax.axis_index("c")


def _flip(v, f):
    return 1 - v if f else v


def _remote(src, dst, send_sems, recv_sems, k, dev):
    return pltpu.make_async_remote_copy(src_ref=src, dst_ref=dst, send_sem=send_sems.at[k], recv_sem=recv_sems.at[k],
                                        device_id=dev, device_id_type=MESH)


CHIP_FLIPS = ((1, 0), (0, 1), (1, 1))
DEV_FLIPS = tuple((fx, fy, fc) for fx in (0, 1) for fy in (0, 1) for fc in (0, 1))[1:]


HBM = pl.BlockSpec(memory_space=pltpu.HBM)
SEM = pl.BlockSpec(memory_space=pltpu.SEMAPHORE)
DATAFLOW = pltpu.SideEffectType.DATAFLOW_SIDE_EFFECTING


def _core_half(ref, axis, c):
    if axis is None:
        return ref
    h = ref.shape[axis] // 2
    return ref.at[pl.ds(c * h, h), :] if axis == 0 else ref.at[:, pl.ds(c * h, h)]


def gather_start(xs, lands, halves, after, name):
    na = len(xs)

    def body(*refs):
        x_refs, land_refs = refs[:na], refs[na:2 * na]
        send_sems, recv_sems = refs[2 * na + 1], refs[2 * na + 2]
        token = refs[-1]
        mx, my, mc = _me()
        chip = 2 * mx + my
        for a in range(na):
            for j, (fx, fy) in enumerate(CHIP_FLIPS):
                _remote(_core_half(x_refs[a], halves[a], mc), _core_half(land_refs[a].at[chip], halves[a], mc),
                        send_sems, recv_sems, 3 * a + j, (_flip(mx, fx), _flip(my, fy), mc)).start()
        token[...] = jnp.zeros_like(token)

    outs = pl.pallas_call(
        body, name=name,
        out_shape=(pltpu.SemaphoreType.DMA((3 * na,)), pltpu.SemaphoreType.DMA((3 * na,)))
        + tuple(pltpu.HBM(x.shape, x.dtype) for x in xs) + tuple(pltpu.HBM(l.shape, l.dtype) for l in lands)
        + (S((8, LANE), F32),),
        in_specs=[HBM] * (2 * na) + [pl.BlockSpec(memory_space=pl.ANY)],
        out_specs=(SEM, SEM) + (HBM,) * (2 * na) + (pl.BlockSpec(memory_space=pltpu.VMEM),),
        input_output_aliases={a: 2 + a for a in range(2 * na)},
        compiler_params=pltpu.CompilerParams(has_side_effects=DATAFLOW),
    )(*[pltpu.with_memory_space_constraint(t, pltpu.HBM) for t in list(xs) + list(lands)], after)
    return outs[0], outs[1], list(outs[2:2 + na]), list(outs[2 + na:2 + 2 * na])


def gather_wait(send_sems, recv_sems, xs, lands, halves, first, after, name):
    na = len(xs)

    def body(*refs):
        x_refs, land_refs = refs[:na], refs[na:2 * na]
        send_sems, recv_sems = refs[2 * na], refs[2 * na + 1]
        mx, my, mc = _me()
        for a in range(na):
            for j, (fx, fy) in enumerate(CHIP_FLIPS):
                px, py = _flip(mx, fx), _flip(my, fy)
                cp = _remote(_core_half(x_refs[a], halves[a], mc), _core_half(land_refs[a].at[2 * px + py], halves[a], mc),
                             send_sems, recv_sems, 3 * (first + a) + j, (px, py, mc))
                cp.wait_send()
                cp.wait_recv()

    outs = pl.pallas_call(
        body, name=name,
        out_shape=tuple(pltpu.HBM(x.shape, x.dtype) for x in xs) + tuple(pltpu.HBM(l.shape, l.dtype) for l in lands),
        in_specs=[HBM] * (2 * na) + [SEM, SEM, pl.BlockSpec(memory_space=pl.ANY)], out_specs=(HBM,) * (2 * na),
        input_output_aliases={a: a for a in range(2 * na)},
        compiler_params=pltpu.CompilerParams(has_side_effects=DATAFLOW),
    )(*xs, *lands, send_sems, recv_sems, after)
    return list(outs[na:])


def forward_halves(land, axis, name):
    def body(in_ref, out_ref, send_sems, recv_sems):
        del in_ref
        mx, my, mc = _me()
        sib = (mx, my, 1 - mc)
        slots = [2 * _flip(mx, fx) + _flip(my, fy) for fx, fy in CHIP_FLIPS]
        cps = [_remote(_core_half(out_ref.at[s], axis, mc), _core_half(out_ref.at[s], axis, mc), send_sems, recv_sems, j, sib)
               for j, s in enumerate(slots)]
        for cp in cps:
            cp.start()
        for j, s in enumerate(slots):
            _remote(_core_half(out_ref.at[s], axis, mc), _core_half(out_ref.at[s], axis, 1 - mc), send_sems, recv_sems, j, sib).wait_recv()
        for cp in cps:
            cp.wait_send()

    return pl.pallas_call(
        body, name=name, out_shape=S(land.shape, land.dtype), in_specs=[ANY], out_specs=ANY, input_output_aliases={0: 0},
        scratch_shapes=[pltpu.SemaphoreType.DMA((3,)), pltpu.SemaphoreType.DMA((3,))])(land)


def all_gather_devices(v, name):
    r, l = v.shape

    def body(v_ref, out_ref, send_sems, recv_sems):
        mx, my, mc = _me()
        me = 4 * mx + 2 * my + mc
        peers = [(_flip(mx, fx), _flip(my, fy), _flip(mc, fc)) for fx, fy, fc in DEV_FLIPS]
        sends = [_remote(v_ref, out_ref.at[me], send_sems, recv_sems, j, p) for j, p in enumerate(peers)]
        for cp in sends:
            cp.start()
        for j, (px, py, pc) in enumerate(peers):
            _remote(v_ref, out_ref.at[4 * px + 2 * py + pc], send_sems, recv_sems, j, (px, py, pc)).wait_recv()
        for cp in sends:
            cp.wait_send()

    out = pl.pallas_call(
        body, name=name, out_shape=S((N_DEV, r, l), v.dtype), in_specs=[ANY], out_specs=ANY,
        scratch_shapes=[pltpu.SemaphoreType.DMA((7,)), pltpu.SemaphoreType.DMA((7,))])(v)
    me = 4 * lax.axis_index("x") + 2 * lax.axis_index("y") + lax.axis_index("c")
    return lax.dynamic_update_slice_in_dim(out, v[None], me, axis=0)


def gather_devices_start(v, name):
    me = 4 * lax.axis_index("x") + 2 * lax.axis_index("y") + lax.axis_index("c")
    land = lax.dynamic_update_slice_in_dim(lax.empty((N_DEV,) + v.shape, v.dtype), v[None], me, axis=0)

    def body(v_ref, land_ref, send_sems, recv_sems, v_thru, land_thru, token):
        mx, my, mc = _me()
        for j, (fx, fy, fc) in enumerate(DEV_FLIPS):
            _remote(v_ref, land_ref.at[4 * mx + 2 * my + mc], send_sems, recv_sems, j,
                    (_flip(mx, fx), _flip(my, fy), _flip(mc, fc))).start()
        token[...] = jnp.zeros_like(token)

    outs = pl.pallas_call(
        body, name=name,
        out_shape=(pltpu.SemaphoreType.DMA((7,)), pltpu.SemaphoreType.DMA((7,)), pltpu.HBM(v.shape, v.dtype),
                   pltpu.HBM(land.shape, land.dtype), S((8, LANE), F32)),
        in_specs=[HBM, HBM], out_specs=(SEM, SEM, HBM, HBM, pl.BlockSpec(memory_space=pltpu.VMEM)),
        input_output_aliases={0: 2, 1: 3}, compiler_params=pltpu.CompilerParams(has_side_effects=DATAFLOW),
    )(pltpu.with_memory_space_constraint(v, pltpu.HBM), pltpu.with_memory_space_constraint(land, pltpu.HBM))
    return outs[:4]


def gather_devices_wait(send_sems, recv_sems, v, land, after, name):
    def body(v_ref, land_ref, send_sems, recv_sems, after_ref, v_dead, got_ref):
        mx, my, mc = _me()
        for j, (fx, fy, fc) in enumerate(DEV_FLIPS):
            px, py, pc = _flip(mx, fx), _flip(my, fy), _flip(mc, fc)
            cp = _remote(v_ref, land_ref.at[4 * px + 2 * py + pc], send_sems, recv_sems, j, (px, py, pc))
            cp.wait_send()
            cp.wait_recv()

    return pl.pallas_call(
        body, name=name, out_shape=(pltpu.HBM(v.shape, v.dtype), pltpu.HBM(land.shape, land.dtype)),
        in_specs=[HBM, HBM, SEM, SEM, pl.BlockSpec(memory_space=pl.ANY)], out_specs=(HBM, HBM),
        input_output_aliases={0: 0, 1: 1}, compiler_params=pltpu.CompilerParams(has_side_effects=DATAFLOW),
    )(v, land, send_sems, recv_sems, after)[1]


def sum_devices(g8, name):
    nd, r, l = g8.shape
    tile = r
    for t in (512, 256, 128, 64, 32, 16, 8):
        if r % t == 0:
            tile = t
            break

    def body(g_ref, o_ref):
        acc = g_ref[0]
        for d in range(1, nd):
            acc = acc + g_ref[d]
        o_ref[...] = acc

    return _call(body, name, (r // tile,), [pl.BlockSpec((nd, tile, l), lambda i: (0, i, 0))], _rs(tile, l), S((r, l), F32),
                 sem=("parallel",))(g8)


def exchange_halves(gs, name):
    na = len(gs)
    nch = gs[0].shape[0]

    def body(*refs):
        g_refs, out_refs = refs[:na], refs[na:2 * na]
        send_sems, recv_sems = refs[2 * na:]
        mx, my, mc = _me()
        sib = (mx, my, 1 - mc)
        cps = []
        for a in range(na):
            half = gs[a].shape[1] // 2
            for j in range(nch):
                cps.append(_remote(g_refs[a].at[j, pl.ds((1 - mc) * half, half), :], out_refs[a].at[j],
                                   send_sems, recv_sems, nch * a + j, sib))
        for cp in cps:
            cp.start()
        for cp in cps:
            cp.wait_recv()
        for cp in cps:
            cp.wait_send()

    return pl.pallas_call(
        body, name=name, out_shape=[S((nch, g.shape[1] // 2, g.shape[2]), g.dtype) for g in gs],
        in_specs=[ANY] * na, out_specs=[ANY] * na,
        scratch_shapes=[pltpu.SemaphoreType.DMA((nch * na,)), pltpu.SemaphoreType.DMA((nch * na,))])(*gs)


def add_halves(g4, got, name):
    nch, r, l = g4.shape
    half = r // 2
    tile = _pick_rows(half)
    nt = half // tile

    def body(g_ref, r_ref, a_ref, own_ref):
        j = pl.program_id(1)
        chip = 2 * lax.axis_index("x") + lax.axis_index("y")
        val = g_ref[0].astype(F32) + r_ref[0].astype(F32)
        a_ref[0] = val.astype(a_ref.dtype)

        @pl.when(j == chip)
        def _():
            own_ref[...] = val

    return pl.pallas_call(
        body, name=name, grid=(nt, nch),
        in_specs=[pl.BlockSpec((1, tile, l), lambda i, j: (j, lax.axis_index("c") * nt + i, 0)),
                  pl.BlockSpec((1, tile, l), lambda i, j: (j, i, 0))],
        out_specs=[pl.BlockSpec((1, tile, l), lambda i, j: (j, i, 0)), pl.BlockSpec((tile, l), lambda i, j: (i, 0))],
        out_shape=[S((nch, half, l), MXU_DTYPE), S((half, l), F32)],
        compiler_params=pltpu.CompilerParams(dimension_semantics=("parallel", "arbitrary"), vmem_limit_bytes=VMEM_LIMIT))(g4, got)


def _pick_rows(r, cap=640):
    return next((t for t in range(cap - cap % 16, 15, -16) if r % t == 0), r)


def add_chips(own, got, name):
    h, l = own.shape
    tile = _pick_rows(h)

    def body(o_ref, g_ref, out_ref):
        out_ref[...] = ((o_ref[...] + g_ref[0].astype(F32)) + g_ref[1].astype(F32)) + g_ref[2].astype(F32)

    nt = h // tile
    return _call(body, name, (nt,), [_rs(tile, l), pl.BlockSpec((3, tile, l), lambda i: (0, i, 0))],
                 pl.BlockSpec((tile, l), lambda i: (lax.axis_index("c") * nt + i, 0)),
                 S((2 * h, l), F32), sem=("parallel",))(own, got)


def join_halves(bufs, name):
    na = len(bufs)

    def body(*refs):
        out_refs = refs[na:2 * na]
        send_sems, recv_sems = refs[2 * na:]
        mx, my, mc = _me()
        sib = (mx, my, 1 - mc)

        def half(a, hc):
            h = bufs[a].shape[0] // 2
            return out_refs[a].at[pl.ds(hc * h, h), :]

        cps = [_remote(half(a, mc), half(a, mc), send_sems, recv_sems, a, sib) for a in range(na)]
        for cp in cps:
            cp.start()
        for a in range(na):
            _remote(half(a, mc), half(a, 1 - mc), send_sems, recv_sems, a, sib).wait_recv()
        for cp in cps:
            cp.wait_send()

    return pl.pallas_call(
        body, name=name, out_shape=[S(b.shape, b.dtype) for b in bufs], in_specs=[ANY] * na, out_specs=[ANY] * na,
        input_output_aliases={a: a for a in range(na)},
        scratch_shapes=[pltpu.SemaphoreType.DMA((na,)), pltpu.SemaphoreType.DMA((na,))])(*bufs)


def exchange_chips_start(parts, name):
    na = len(parts)
    lands = [lax.empty((3,) + p.shape[1:], p.dtype) for p in parts]

    def body(*refs):
        a_refs, land_refs = refs[:na], refs[na:2 * na]
        send_sems, recv_sems = refs[2 * na], refs[2 * na + 1]
        mx, my, mc = _me()
        for j, (fx, fy) in enumerate(CHIP_FLIPS):
            px, py = _flip(mx, fx), _flip(my, fy)
            for a in range(na):
                _remote(a_refs[a].at[2 * px + py], land_refs[a].at[j], send_sems, recv_sems, 3 * a + j, (px, py, mc)).start()
        refs[-1][...] = jnp.zeros_like(refs[-1])

    outs = pl.pallas_call(
        body, name=name,
        out_shape=(pltpu.SemaphoreType.DMA((3 * na,)), pltpu.SemaphoreType.DMA((3 * na,)))
        + tuple(pltpu.HBM(t.shape, t.dtype) for t in list(parts) + lands) + (S((8, LANE), F32),),
        in_specs=[HBM] * (2 * na), out_specs=(SEM, SEM) + (HBM,) * (2 * na) + (pl.BlockSpec(memory_space=pltpu.VMEM),),
        input_output_aliases={a: 2 + a for a in range(2 * na)},
        compiler_params=pltpu.CompilerParams(has_side_effects=DATAFLOW),
    )(*[pltpu.with_memory_space_constraint(t, pltpu.HBM) for t in list(parts) + lands])
    return outs[0], outs[1], list(outs[2:2 + na]), list(outs[2 + na:2 + 2 * na]), outs[-1]


def exchange_chips_wait(send_sems, recv_sems, parts, lands, after, name):
    na = len(parts)

    def body(*refs):
        a_refs, land_refs = refs[:na], refs[na:2 * na]
        send_sems, recv_sems = refs[2 * na], refs[2 * na + 1]
        mx, my, mc = _me()
        for j, (fx, fy) in enumerate(CHIP_FLIPS):
            px, py = _flip(mx, fx), _flip(my, fy)
            for a in range(na):
                cp = _remote(a_refs[a].at[2 * px + py], land_refs[a].at[j], send_sems, recv_sems, 3 * a + j, (px, py, mc))
                cp.wait_send()
                cp.wait_recv()

    outs = pl.pallas_call(
        body, name=name, out_shape=tuple(pltpu.HBM(t.shape, t.dtype) for t in list(parts) + list(lands)),
        in_specs=[HBM] * (2 * na) + [SEM, SEM, pl.BlockSpec(memory_space=pl.ANY)], out_specs=(HBM,) * (2 * na),
        input_output_aliases={a: a for a in range(2 * na)},
        compiler_params=pltpu.CompilerParams(has_side_effects=DATAFLOW),
    )(*parts, *lands, send_sems, recv_sems, after)
    return list(outs[na:])


def reduce_begin(gs, tag):
    got = exchange_halves(gs, f"{tag}_d2d")
    sums = [add_halves(g, r, f"{tag}_add1_{i}") for i, (g, r) in enumerate(zip(gs, got))]
    return [own for _, own in sums], exchange_chips_start([a for a, _ in sums], f"{tag}_ici_start")


def reduce_end(state, after, tag):
    owns, (send_sems, recv_sems, parts, lands, _) = state
    got = exchange_chips_wait(send_sems, recv_sems, parts, lands, after, f"{tag}_ici_wait")
    return [add_chips(own, r, f"{tag}_add2_{i}") for i, (own, r) in enumerate(zip(owns, got))]


PACK_L = 1024


def _pack(arrs, dtype, row_mult, lead=None):
    lead_shape = () if lead is None else arrs[0].shape[:lead]
    flat = jnp.concatenate([a.astype(dtype).reshape(lead_shape + (-1,)) for a in arrs], axis=-1)
    n = flat.shape[-1]
    unit = row_mult * PACK_L
    total = -(-n // unit) * unit
    flat = jnp.pad(flat, [(0, 0)] * len(lead_shape) + [(0, total - n)])
    return flat.reshape(lead_shape + (total // PACK_L, PACK_L))


def _unpack(packed, shapes, lead=None):
    lead_shape = () if lead is None else packed.shape[:lead]
    flat = packed.reshape(lead_shape + (-1,))
    out, off = [], 0
    for shp in shapes:
        n = int(np.prod(shp))
        out.append(flat[..., off:off + n].reshape(lead_shape + tuple(shp)))
        off += n
    return out


def _to_full(gathered, axis):
    t = jnp.moveaxis(gathered, 0, axis)
    shp = t.shape
    return t.reshape(shp[:axis] + (shp[axis] * shp[axis + 1],) + shp[axis + 2:])


WEIGHTS = ("norm_ffn1", "ffn1_w_in", "ffn1_w_out", "norm_mix", "norm_ffn2", "ffn2_w_in", "ffn2_w_out", "ple_norm",
           "ple_gate_w", "ple_proj_w", "hyb_w_in", "conv_dw_w", "conv_dw_b", "conv_ln_g", "conv_ln_b", "ssm_conv_w",
           "ssm_conv_b", "ssm_dt_bias", "ssm_a_log", "ssm_d", "ssm_norm", "hyb_w_out", "att_w_qkv", "att_b_qkv",
           "att_sinks", "att_w_o", "att_b_o", "final_norm")
SHARD_AXIS = dict(ffn1_w_in=2, ffn1_w_out=1, ffn2_w_in=2, ffn2_w_out=1, ple_gate_w=1, ple_proj_w=2, hyb_w_in=2,
                  conv_dw_w=2, ssm_conv_w=2, hyb_w_out=1, att_w_qkv=2, att_b_qkv=1, att_w_o=1, att_b_o=1)
BIG = ("ffn1_w_in", "ffn1_w_out", "ffn2_w_in", "ffn2_w_out", "ple_gate_w", "ple_proj_w", "hyb_w_in", "hyb_w_out",
       "att_w_qkv", "att_w_o")
TRANSPOSED = ("hyb_w_in",)
FORWARDED = (0, 2)


def _shard_axis(k):
    return 1 if k in TRANSPOSED else SHARD_AXIS[k]
SMALL_SHARDED = ("conv_dw_w", "ssm_conv_w", "att_b_qkv", "att_b_o")
SMALL = tuple(k for k in WEIGHTS if k not in BIG)


def _step(x, p, target, w, m, v):
    mx, my = lax.axis_index("x"), lax.axis_index("y")
    chip = 2 * mx + my
    w, m, v = ({k: (a.transpose(0, 2, 1) if k in TRANSPOSED else a) for k, a in d.items()} for d in (w, m, v))

    depth = w["norm_ffn1"].shape[0]
    order = sorted([(k, i) for i in range(depth) for k in BIG if _layer_index(k, i) is not None],
                   key=lambda t: (t[1], _stage(t[0])))
    small_g = all_gather_devices(_pack([w[k] for k in SMALL_SHARDED], F32, 8), "gather_small")
    shards = [w[k][_layer_index(k, i)].astype(MXU_DTYPE) for k, i in order]
    lands = [lax.dynamic_update_slice_in_dim(lax.empty((N_CHIPS,) + s.shape, s.dtype), s[None], chip, axis=0) for s in shards]
    halves = [(0 if s.shape[0] % 32 == 0 else 1) if pos in FORWARDED else None for pos, s in enumerate(shards)]
    send_sems, recv_sems, shards, lands = gather_start(shards, lands, halves, small_g, "gather_start")

    def fetch(i, k, after):
        p = order.index((k, i))
        g, = gather_wait(send_sems, recv_sems, [shards[p]], [lands[p]], [halves[p]], p, after, f"gather_wait_l{i}_{k}")
        if halves[p] is not None:
            g = forward_halves(g, halves[p], f"gather_forward_l{i}_{k}")
        if _shard_axis(k) == 2:
            return ColSharded(g)
        return g.reshape(-1, g.shape[-1])

    small_g = small_g[0::2]
    small_full = {k: _to_full(g, SHARD_AXIS[k])
                  for k, g in zip(SMALL_SHARDED, _unpack(small_g, [w[k].shape for k in SMALL_SHARDED], lead=1))}
    layers = [LayerWeights({k: small_full.get(k, w[k])[_layer_index(k, i)] for k in SMALL if _layer_index(k, i) is not None},
                           functools.partial(fetch, i)) for i in range(depth)]

    def bucket_of(layer, stage):
        return (layer, 0) if layer > 0 else (0, min(stage, 1))

    sink = GradSink({k: w[k].shape for k in BIG}, bucket_of)
    begun = {}

    def stage_done(i, stage, tie):
        b = bucket_of(i, stage)
        if stage > 0 and bucket_of(i, stage - 1) == b:
            return tie
        begun[b] = reduce_begin(list(sink.bufs[b].values()), f"grads_l{b[0]}_{b[1]}")
        return begun[b][1][-1]

    loss, dx, grads = trunk_fwd_bwd(x[0], p[:, 0], target[0], layers, w["final_norm"], sink, stage_done)

    results = {}

    def finish(buckets, after, tag):
        halves = {b: reduce_end(begun[b], after, f"grads_l{b[0]}_{b[1]}") for b in buckets}
        joined = iter(join_halves([h for b in buckets for h in halves[b]], f"grads_join_{tag}"))
        reduced = {b: {c: next(joined) for c in sink.bufs[b]} for b in buckets}
        last = after
        for (k, li), (b, key, _, off, r) in sink.where.items():
            if b in buckets:
                g = reduced[b][key]
                if r % 8:
                    g, off = g[off:off + r], 0
                results[k] = adamw_layer(w[k], g, off, m[k], v[k], li, results.get(k), f"adamw_{k}_{li}")
                last = results[k][1]
        return last

    vec = gather_devices_start(_pack([loss[0:1, 0:1]] + [grads[k] for k in SMALL], F32, 8), "gather_vectors_start")
    order_b = list(begun)
    started_last = begun[order_b[-1]][1][-1]
    done = finish(order_b[-1:], finish(order_b[:-1], started_last, "early") if len(order_b) > 1 else dx, "last")
    g_out = {k: results[k][0] for k in BIG}
    vec = sum_devices(gather_devices_wait(*vec, done, "gather_vectors_wait"), "sum_vectors")
    parts = _unpack(vec, [(1, 1)] + [grads[k].shape for k in SMALL])
    loss_out = parts[0].reshape(())
    for k, g in zip(SMALL, parts[1:]):
        if k in SHARD_AXIS:
            ax = SHARD_AXIS[k]
            g = lax.dynamic_slice_in_dim(g, chip * w[k].shape[ax], w[k].shape[ax], axis=ax)
        g_out[k] = g

    for k in TRANSPOSED:
        results[k] = [a.transpose(0, 2, 1) for a in results[k]]
    g_out.update({k: results[k][0] for k in TRANSPOSED})
    delta, new_m, new_v = ({k: results[k][j] for k in BIG} for j in (1, 2, 3))
    shapes = [w[k].shape for k in SMALL]
    packed = [_pack([src[k] for k in SMALL], F32, 8) for src in (w, g_out, m, v)]
    outs = adamw(*packed, "adamw_small")
    for dst, o in zip((delta, new_m, new_v), outs):
        for k, a in zip(SMALL, _unpack(o, shapes)):
            dst[k] = a
    return ((loss_out, dx[None]) + tuple(g_out[k] for k in WEIGHTS) + tuple(delta[k] for k in WEIGHTS)
            + tuple(new_m[k] for k in WEIGHTS) + tuple(new_v[k] for k in WEIGHTS))


def kernel(x, p, norm_ffn1, ffn1_w_in, ffn1_w_out, norm_mix, norm_ffn2, ffn2_w_in, ffn2_w_out, ple_norm, ple_gate_w, ple_proj_w, hyb_w_in, conv_dw_w, conv_dw_b, conv_ln_g, conv_ln_b, ssm_conv_w, ssm_conv_b, ssm_dt_bias, ssm_a_log, ssm_d, ssm_norm, hyb_w_out, att_w_qkv, att_b_qkv, att_sinks, att_w_o, att_b_o, final_norm, loss_target, m_norm_ffn1, m_ffn1_w_in, m_ffn1_w_out, m_norm_mix, m_norm_ffn2, m_ffn2_w_in, m_ffn2_w_out, m_ple_norm, m_ple_gate_w, m_ple_proj_w, m_hyb_w_in, m_conv_dw_w, m_conv_dw_b, m_conv_ln_g, m_conv_ln_b, m_ssm_conv_w, m_ssm_conv_b, m_ssm_dt_bias, m_ssm_a_log, m_ssm_d, m_ssm_norm, m_hyb_w_out, m_att_w_qkv, m_att_b_qkv, m_att_sinks, m_att_w_o, m_att_b_o, m_final_norm, v_norm_ffn1, v_ffn1_w_in, v_ffn1_w_out, v_norm_mix, v_norm_ffn2, v_ffn2_w_in, v_ffn2_w_out, v_ple_norm, v_ple_gate_w, v_ple_proj_w, v_hyb_w_in, v_conv_dw_w, v_conv_dw_b, v_conv_ln_g, v_conv_ln_b, v_ssm_conv_w, v_ssm_conv_b, v_ssm_dt_bias, v_ssm_a_log, v_ssm_d, v_ssm_norm, v_hyb_w_out, v_att_w_qkv, v_att_b_qkv, v_att_sinks, v_att_w_o, v_att_b_o, v_final_norm):
    given = locals()
    w = {k: given[k] for k in WEIGHTS}
    m = {k: given["m_" + k] for k in WEIGHTS}
    v = {k: given["v_" + k] for k in WEIGHTS}
    return _step(x, p, loss_target, w, m, v)
```

```python
import functools

import numpy as np
import jax
import jax.numpy as jnp
from jax import lax
from jax.experimental import pallas as pl
from jax.experimental.pallas import tpu as pltpu

F32 = jnp.float32
BF16 = jnp.bfloat16
MXU_DTYPE = jnp.bfloat16
S = jax.ShapeDtypeStruct
MESH = pl.DeviceIdType.MESH

VMEM_LIMIT = 48 * 2**20
LANE = 128

EPS = 1e-6
SSM_HEADS = 16
HEAD_DIM = 64
SSM_GROUPS = 2
SSM_STATE = 128
SSM_CONV = 4
CHUNK = 128
CONV_WIDTH = 31
ATT_HEADS = 16
ATT_KV_HEADS = 4
WINDOW = 128
ROPE_THETA = 10000.0
ADAM_LR = 0.001
ADAM_B1 = 0.9
ADAM_B2 = 0.999
ADAM_EPS = 1e-08
ADAM_WD = 0.01
ADAM_STEP = 10

N_CHIPS = 4
N_DEV = 8

NN = ((1,), (0,))
NT = ((1,), (1,))
TN = ((0,), (0,))


def _mm(a, b, dims=NN):
    return lax.dot_general(a.astype(MXU_DTYPE), b.astype(MXU_DTYPE), (dims, ((), ())), preferred_element_type=F32)


def _split3(a):
    hi = a.astype(BF16)
    r = a - hi.astype(F32)
    mid = r.astype(BF16)
    lo = (r - mid.astype(F32)).astype(BF16)
    return hi, mid, lo


def _mm01(a, onehot, dims=NN):
    o = onehot.astype(BF16)
    out = None
    for part in _split3(a):
        t = lax.dot_general(part, o, (dims, ((), ())), preferred_element_type=F32)
        out = t if out is None else out + t
    return out


def _01mm(onehot, a):
    o = onehot.astype(BF16)
    out = None
    for part in _split3(a):
        t = lax.dot_general(o, part, (NN, ((), ())), preferred_element_type=F32)
        out = t if out is None else out + t
    return out


def _sigmoid(x):
    return 0.5 * jnp.tanh(0.5 * x) + 0.5


def _softplus(x):
    return jnp.maximum(x, 0.0) + jnp.log(1.0 + jnp.exp(-jnp.abs(x)))


def _iota(shape, axis):
    return lax.broadcasted_iota(jnp.int32, shape, axis)


def _head_indicator(width, heads, transposed=False):
    per = width // heads
    if transposed:
        return (_iota((heads, width), 1) // per == _iota((heads, width), 0)).astype(F32)
    return (_iota((width, heads), 0) // per == _iota((width, heads), 1)).astype(F32)


def _acc(ref, i, val):
    @pl.when(i == 0)
    def _():
        ref[...] = val

    @pl.when(i > 0)
    def _():
        ref[...] += val


def _rs(tile, width, col=0, shift=0, n=None):
    if shift == 0:
        return pl.BlockSpec((tile, width), lambda i: (i, col))
    if shift < 0:
        return pl.BlockSpec((tile, width), lambda i: (jnp.maximum(i - 1, 0), col))
    return pl.BlockSpec((tile, width), lambda i: (jnp.minimum(i + 1, n - 1), col))


def _ps(shape):
    return pl.BlockSpec(shape, lambda i: (0,) * len(shape))


def _call(body, name, grid, in_specs, out_specs, out_shape, scratch=(), sem=None):
    return pl.pallas_call(
        body, name=name, grid=grid, in_specs=in_specs, out_specs=out_specs, out_shape=out_shape,
        scratch_shapes=list(scratch),
        compiler_params=pltpu.CompilerParams(dimension_semantics=sem, vmem_limit_bytes=VMEM_LIMIT))


def _row_tile(n, target):
    t = min(n, target)
    assert n % t == 0, (n, t)
    return t


def _pick_tile(dim, target):
    if dim <= target:
        return dim
    t = (int(1.4 * target) // LANE) * LANE
    while t >= LANE:
        if dim % t == 0:
            return t
        t -= LANE
    return dim


ANY = pl.BlockSpec(memory_space=pl.ANY)


def _paired(j):
    return (j % 2) * 2 + j // 2


class ColSharded:
    def __init__(self, arr, paired=False):
        self.arr, self.paired = arr, paired
        self.nch, self.rows, self.per = arr.shape
        self.shape = (self.rows, self.nch * self.per)

    def chip(self, j):
        return _paired(j) if self.paired else j


class Slot:
    def __init__(self, buf, kind, per, off, c0=0, paired=False):
        self.buf, self.kind, self.per, self.off, self.c0, self.paired = buf, kind, per, off, c0, paired

    def chip(self, j):
        return _paired(j) if self.paired else j


def matmul(a, b, mode, name, *, out_dtype=F32, scale=None, res=None, bias=None, into=None, tm=1024, tn=1024, tk=1024):
    bshape = b.shape
    if mode == "nn":
        (m, k), (k2, n) = a.shape, bshape
    elif mode == "nt":
        (m, k), (n, k2) = a.shape, bshape
    else:
        (k, m), (k2, n) = a.shape, bshape
    assert k == k2, (a.shape, bshape, mode)
    if mode == "tn":
        tk = 2 * tk
    tm, tn, tk = _pick_tile(m, tm), _pick_tile(n, tn), _pick_tile(k, tk)
    if isinstance(b, ColSharded):
        if mode == "nn":
            tn = b.per
        else:
            assert mode == "nt"
            tk = b.per
    if into is not None:
        if into.kind == "c":
            tn = into.per
            assert into.off % tm == 0 and n == N_CHIPS * into.per
        else:
            tm = max(1, min(m, int(1.4 * 1024)) // into.per) * into.per
            assert m % tm == 0 and into.off % into.per == 0 and into.c0 % (tm // into.per) == 0
    nk = k // tk
    dims = {"nn": NN, "nt": NT, "tn": TN}[mode]
    a_spec = (pl.BlockSpec((tk, tm), lambda i, j, kk: (kk, i)) if mode == "tn"
              else pl.BlockSpec((tm, tk), lambda i, j, kk: (i, kk)))
    if isinstance(b, ColSharded):
        bchip = b.chip
        b_spec = (pl.BlockSpec((None, tk, tn), lambda i, j, kk: (bchip(j), kk, 0)) if mode == "nn"
                  else pl.BlockSpec((None, tn, tk), lambda i, j, kk: (bchip(kk), j, 0)))
        b = b.arr
    else:
        b_spec = (pl.BlockSpec((tn, tk), lambda i, j, kk: (j, kk)) if mode == "nt"
                  else pl.BlockSpec((tk, tn), lambda i, j, kk: (kk, j)))
    plain_o = pl.BlockSpec((tm, tn), lambda i, j, kk: (i, j))
    ins, in_specs = [a, b], [a_spec, b_spec]
    if bias is not None:
        ins.append(bias)
        in_specs.append(pl.BlockSpec((1, tn), lambda i, j, kk: (0, j)))
    if res is not None:
        ins.append(res)
        in_specs.append(plain_o)
    aliases = {}
    if into is None:
        o_spec, o_shape = plain_o, S((m, n), out_dtype)
    else:
        aliases = {len(ins): 0}
        ins.append(into.buf)
        in_specs.append(ANY)
        o_shape = S(into.buf.shape, into.buf.dtype)
        if into.kind == "c":
            ob, ochip = into.off // tm, into.chip
            o_spec = pl.BlockSpec((None, tm, tn), lambda i, j, kk: (ochip(j), ob + i, 0))
        else:
            q, ob = tm // into.per, into.off // into.per
            cb = into.c0 // q
            o_spec = pl.BlockSpec((q, into.per, tn), lambda i, j, kk: (cb + i, ob, j))

    def body(*refs):
        a_ref, b_ref = refs[0], refs[1]
        o_ref, acc_ref = refs[-2], refs[-1]
        kk = pl.program_id(2)

        @pl.when(kk == 0)
        def _():
            acc_ref[...] = jnp.zeros_like(acc_ref)

        acc_ref[...] += _mm(a_ref[...], b_ref[...], dims)

        @pl.when(kk == nk - 1)
        def _():
            out = acc_ref[...]
            if scale is not None:
                out = out * scale
            pos = 2
            if bias is not None:
                out = out + refs[pos][...]
                pos += 1
            if res is not None:
                out = out + refs[pos][...]
            o_ref[...] = out.astype(o_ref.dtype).reshape(o_ref.shape)

    return pl.pallas_call(
        body, name=name, grid=(m // tm, n // tn, nk), in_specs=in_specs, out_specs=o_spec, out_shape=o_shape,
        scratch_shapes=[pltpu.VMEM((tm, tn), F32)], input_output_aliases=aliases,
        compiler_params=pltpu.CompilerParams(dimension_semantics=("parallel", "parallel", "arbitrary"),
                                             vmem_limit_bytes=VMEM_LIMIT))(*ins)


def rms_matmul(h, g, b, mode, name, bias=None):
    n, d = h.shape
    sharded = isinstance(b, ColSharded)
    n_out = b.shape[1] if mode == "nn" else b.shape[0]
    tm = _row_tile(n, 512)
    tn = b.per if sharded else _pick_tile(n_out, 1024)
    if sharded:
        assert mode == "nn"
        bchip = b.chip
        b_spec = pl.BlockSpec((None, d, tn), lambda i, j: (bchip(j), 0, 0))
        b = b.arr
    elif mode == "nn":
        b_spec = pl.BlockSpec((d, tn), lambda i, j: (0, j))
    else:
        b_spec = pl.BlockSpec((tn, d), lambda i, j: (j, 0))
    ins = [h, g, b] + ([bias] if bias is not None else [])
    in_specs = [pl.BlockSpec((tm, d), lambda i, j: (i, 0)), pl.BlockSpec((1, d), lambda i, j: (0, 0)), b_spec]
    if bias is not None:
        in_specs.append(pl.BlockSpec((1, tn), lambda i, j: (0, j)))

    def body(h_ref, g_ref, b_ref, *refs):
        o_ref, xn_ref = refs[-2:]
        x = h_ref[...]
        r = lax.rsqrt(jnp.mean(x * x, axis=-1, keepdims=True) + EPS)
        xn = (x * r * g_ref[...]).astype(xn_ref.dtype)

        @pl.when(pl.program_id(1) == 0)
        def _():
            xn_ref[...] = xn

        out = _mm(xn, b_ref[...], NN if mode == "nn" else NT)
        o_ref[...] = out if bias is None else out + refs[0][...]

    return pl.pallas_call(
        body, name=name, grid=(n // tm, n_out // tn), in_specs=in_specs,
        out_specs=[pl.BlockSpec((tm, tn), lambda i, j: (i, j)), pl.BlockSpec((tm, d), lambda i, j: (i, 0))],
        out_shape=[S((n, n_out), F32), S((n, d), MXU_DTYPE)],
        compiler_params=pltpu.CompilerParams(dimension_semantics=("parallel", "arbitrary"), vmem_limit_bytes=VMEM_LIMIT),
    )(*ins)


def _rms_bwd_math(x, g, dy):
    r = lax.rsqrt(jnp.mean(x * x, axis=-1, keepdims=True) + EPS)
    xh = x * r
    dg = jnp.sum(dy * xh, axis=0, keepdims=True)
    dxh = dy * g
    dx = r * (dxh - xh * jnp.mean(dxh * xh, axis=-1, keepdims=True))
    return dx, dg


def nt_rms_bwd(a, b, h, g, dh_in, name, extra=None, colsum=False, b_kd=False):
    n, k = a.shape
    d = h.shape[1]
    tm = _row_tile(n, 1024)
    sharded = isinstance(b, ColSharded)
    tk = b.per if sharded else _pick_tile(k, 1024)
    nk = k // tk
    dims = NN if b_kd else NT
    if sharded:
        bchip = b.chip
        b_spec = pl.BlockSpec((None, d, tk), lambda i, kk: (bchip(kk), 0, 0))
        b = b.arr
    elif b_kd:
        b_spec = pl.BlockSpec((tk, d), lambda i, kk: (kk, 0))
    else:
        b_spec = pl.BlockSpec((d, tk), lambda i, kk: (0, kk))
    row = pl.BlockSpec((tm, d), lambda i, kk: (i, 0))
    vec = pl.BlockSpec((1, d), lambda i, kk: (0, 0))
    ins, in_specs = [a, b, h, g, dh_in], [pl.BlockSpec((tm, tk), lambda i, kk: (i, kk)), b_spec, row, vec, row]
    if extra is not None:
        k2 = extra[0].shape[1]
        ins += list(extra)
        in_specs += [pl.BlockSpec((tm, k2), lambda i, kk: (i, 0)),
                     pl.BlockSpec((k2, d) if b_kd else (d, k2), lambda i, kk: (0, 0))]
    n_in = len(ins)

    def body(*refs):
        a_ref, b_ref, h_ref, g_ref, dh_ref = refs[:5]
        o_ref, dg_ref = refs[n_in], refs[n_in + 1]
        acc_ref = refs[-1]
        i, kk = pl.program_id(0), pl.program_id(1)

        @pl.when(kk == 0)
        def _():
            acc_ref[...] = _mm(refs[5][...], refs[6][...], dims) if extra is not None else jnp.zeros_like(acc_ref)

        acc_ref[...] += _mm(a_ref[...], b_ref[...], dims)

        @pl.when(kk == nk - 1)
        def _():
            dx, dg = _rms_bwd_math(h_ref[...], g_ref[...], acc_ref[...])
            out = dh_ref[...] + dx
            o_ref[...] = out
            _acc(dg_ref, i, dg)
            if colsum:
                _acc(refs[n_in + 2], i, jnp.sum(out, axis=0, keepdims=True))

    n_vec = 2 if colsum else 1
    return pl.pallas_call(
        body, name=name, grid=(n // tm, nk), in_specs=in_specs, out_specs=[row] + [vec] * n_vec,
        out_shape=[S((n, d), F32)] + [S((1, d), F32)] * n_vec, scratch_shapes=[pltpu.VMEM((tm, d), F32)],
        compiler_params=pltpu.CompilerParams(dimension_semantics=("arbitrary", "arbitrary"), vmem_limit_bytes=VMEM_LIMIT),
    )(*ins)


def swiglu_in(h, g, w_in, name):
    n, d = h.shape
    per = w_in.per
    nj = w_in.nch // 2
    tile = _row_tile(n, 1024)

    def body(h_ref, g_ref, wg_ref, wu_ref, xn_ref, u_ref, hm_ref):
        x = h_ref[...]
        r = lax.rsqrt(jnp.mean(x * x, axis=-1, keepdims=True) + EPS)
        xn = (x * r * g_ref[...]).astype(xn_ref.dtype)

        @pl.when(pl.program_id(1) == 0)
        def _():
            xn_ref[...] = xn

        a = _mm(xn, wg_ref[...])
        b = _mm(xn, wu_ref[...])
        u_ref[:, :per] = a.astype(u_ref.dtype)
        u_ref[:, per:] = b.astype(u_ref.dtype)
        hm_ref[...] = (a * _sigmoid(a) * b).astype(hm_ref.dtype)

    return pl.pallas_call(
        body, name=name, grid=(n // tile, nj),
        in_specs=[pl.BlockSpec((tile, d), lambda i, j: (i, 0)), pl.BlockSpec((1, d), lambda i, j: (0, 0)),
                  pl.BlockSpec((None, d, per), lambda i, j: (j, 0, 0)), pl.BlockSpec((None, d, per), lambda i, j: (nj + j, 0, 0))],
        out_specs=[pl.BlockSpec((tile, d), lambda i, j: (i, 0)), pl.BlockSpec((tile, 2 * per), lambda i, j: (i, j)),
                   pl.BlockSpec((tile, per), lambda i, j: (i, j))],
        out_shape=[S((n, d), MXU_DTYPE), S((n, 2 * nj * per), MXU_DTYPE), S((n, nj * per), MXU_DTYPE)],
        compiler_params=pltpu.CompilerParams(dimension_semantics=("parallel", "arbitrary"), vmem_limit_bytes=VMEM_LIMIT),
    )(h, g, w_in.arr, w_in.arr)


def swiglu_out_bwd(dh, w_out, u, after, name):
    n, d = dh.shape
    f = w_out.shape[0]
    per = u.shape[1] // 4
    nj = f // per
    tile = _row_tile(n, 1024)

    def body(dh_ref, w_ref, u_ref, after_ref, du_ref):
        dm = 0.5 * _mm(dh_ref[...], w_ref[...], NT)
        a = u_ref[:, :per].astype(F32)
        b = u_ref[:, per:].astype(F32)
        s = _sigmoid(a)
        du_ref[:, :per] = (dm * b * s * (1.0 + a * (1.0 - s))).astype(du_ref.dtype)
        du_ref[:, per:] = (dm * a * s).astype(du_ref.dtype)

    return pl.pallas_call(
        body, name=name, grid=(n // tile, nj),
        in_specs=[pl.BlockSpec((tile, d), lambda i, j: (i, 0)), pl.BlockSpec((per, d), lambda i, j: (j, 0)),
                  pl.BlockSpec((tile, 2 * per), lambda i, j: (i, j)), ANY],
        out_specs=pl.BlockSpec((tile, 2 * per), lambda i, j: (i, j)),
        out_shape=S(u.shape, MXU_DTYPE),
        compiler_params=pltpu.CompilerParams(dimension_semantics=("parallel", "parallel"), vmem_limit_bytes=VMEM_LIMIT),
    )(dh, w_out, u, after)


def ple_fwd(h, g, w_gate, pp, name):
    n, d = h.shape
    tile = _row_tile(n, 512)

    def body(h_ref, g_ref, w_ref, pp_ref, o_ref, gl_ref, xn_ref):
        x = h_ref[...]
        r = lax.rsqrt(jnp.mean(x * x, axis=-1, keepdims=True) + EPS)
        xn = (x * r * g_ref[...]).astype(xn_ref.dtype)
        xn_ref[...] = xn
        gl = _mm(xn, w_ref[...])
        gl_ref[...] = gl
        o_ref[...] = x + _sigmoid(gl) * pp_ref[...]

    return _call(body, name, (n // tile,), [_rs(tile, d), _ps((1, d)), _ps(w_gate.shape), _rs(tile, d)],
                 [_rs(tile, d)] * 3, [S((n, d), F32), S((n, d), F32), S((n, d), MXU_DTYPE)], sem=("parallel",))(h, g, w_gate, pp)


def ple_bwd(dh, gl, pp, w_gate, h, g, after, name):
    n, d = dh.shape
    tile = _row_tile(n, 512)

    def body(dh_ref, gl_ref, pp_ref, w_ref, h_ref, g_ref, after_ref, o_ref, dpp_ref, dgl_ref, dg_ref):
        i = pl.program_id(0)
        s = _sigmoid(gl_ref[...])
        dh_ = dh_ref[...]
        dpp_ref[...] = (dh_ * s).astype(dpp_ref.dtype)
        dgl = (dh_ * pp_ref[...] * s * (1.0 - s)).astype(dgl_ref.dtype)
        dgl_ref[...] = dgl
        dx, dg = _rms_bwd_math(h_ref[...], g_ref[...], _mm(dgl, w_ref[...], NT))
        o_ref[...] = dh_ + dx
        _acc(dg_ref, i, dg)

    return _call(body, name, (n // tile,),
                 [_rs(tile, d)] * 3 + [_ps(w_gate.shape), _rs(tile, d), _ps((1, d)), ANY],
                 [_rs(tile, d)] * 3 + [_ps((1, d))],
                 [S((n, d), F32), S((n, d), MXU_DTYPE), S((n, d), MXU_DTYPE), S((1, d), F32)],
                 sem=("arbitrary",))(dh, gl, pp, w_gate, h, g, after)


def loss_head(h, g, target, name):
    n, d = h.shape
    tile = _row_tile(n, 256)

    def body(h_ref, g_ref, t_ref, dh_ref, dg_ref, loss_ref):
        i = pl.program_id(0)
        x = h_ref[...]
        gg = g_ref[...]
        r = lax.rsqrt(jnp.mean(x * x, axis=-1, keepdims=True) + EPS)
        err = x * r * gg - t_ref[...]
        part = 0.5 * jnp.sum(jnp.mean(err * err, axis=-1, keepdims=True), axis=0, keepdims=True)
        dx, dg = _rms_bwd_math(x, gg, err * (1.0 / d))
        dh_ref[...] = dx
        _acc(dg_ref, i, dg)
        _acc(loss_ref, i, jnp.broadcast_to(part, (8, LANE)))

    return _call(body, name, (n // tile,), [_rs(tile, d), _ps((1, d)), _rs(tile, d)],
                 [_rs(tile, d), _ps((1, d)), _ps((8, LANE))], [S((n, d), F32), S((1, d), F32), S((8, LANE), F32)],
                 sem=("arbitrary",))(h, g, target)


def _adamw_math(w, g, m, v):
    c1 = np.float32(1.0 - ADAM_B1 ** ADAM_STEP)
    c2 = np.float32(1.0 - ADAM_B2 ** ADAM_STEP)
    mm = ADAM_B1 * m + (1.0 - ADAM_B1) * g
    vv = ADAM_B2 * v + (1.0 - ADAM_B2) * (g * g)
    return -ADAM_LR * ((mm / c1) / (jnp.sqrt(vv / c2) + ADAM_EPS) + ADAM_WD * w), mm, vv


def adamw_layer(w, pack, off, m, v, li, prev, name):
    n, r, c = w.shape

    def body(w_ref, g_ref, m_ref, v_ref, *refs):
        go_ref, d_ref, mo_ref, vo_ref = refs[-4:]
        g = g_ref[...]
        go_ref[...] = g
        d_ref[...], mo_ref[...], vo_ref[...] = _adamw_math(w_ref[...], g, m_ref[...], v_ref[...])

    if r % 8 == 0:
        cap = 2**21 // (4 * c) // 8 * 8
        tile = next(t for t in range(min(cap, r), 7, -8) if r % t == 0 and off % t == 0)
        ob, steps = off // tile, r // tile
        blk = pl.BlockSpec((None, tile, c), lambda i: (li, i, 0))
        g_spec = pl.BlockSpec((tile, c), lambda i: (ob + i, 0))
    else:
        assert off == 0 and pack.shape[0] == r and c % (2 * LANE) == 0
        steps = c // (2 * LANE)
        blk = pl.BlockSpec((None, r, 2 * LANE), lambda i: (li, 0, i))
        g_spec = pl.BlockSpec((r, 2 * LANE), lambda i: (0, i))
    prev = list(prev) if prev is not None else []
    return pl.pallas_call(
        body, name=name, grid=(steps,),
        in_specs=[blk, g_spec, blk, blk] + [ANY] * len(prev),
        out_specs=[blk] * 4, out_shape=[S((n, r, c), F32)] * 4,
        input_output_aliases={4 + j: j for j in range(len(prev))},
        compiler_params=pltpu.CompilerParams(dimension_semantics=("parallel",), vmem_limit_bytes=VMEM_LIMIT),
    )(w, pack, m, v, *prev)


def adamw(w, g, m, v, name):
    r, c = w.shape
    tile = r
    for t in (512, 256, 128, 64, 32, 16, 8):
        if r % t == 0 and t * c * 4 <= 2**21:
            tile = t
            break

    def body(w_ref, g_ref, m_ref, v_ref, d_ref, mo_ref, vo_ref):
        d_ref[...], mo_ref[...], vo_ref[...] = _adamw_math(w_ref[...], g_ref[...], m_ref[...], v_ref[...])

    return _call(body, name, (r // tile,), [_rs(tile, c)] * 4, [_rs(tile, c)] * 3, [S((r, c), F32)] * 3,
                 sem=("parallel",))(w, g, m, v)


TAP_VREGS = 32


def _taps(src, w_ref, offsets, tile, put, bias=None):
    c = src.shape[1]
    rp = max(8, TAP_VREGS * 8 * LANE // c // 8 * 8)
    for r0 in range(0, tile, rp):
        acc = jnp.zeros((rp, c), F32) if bias is None else jnp.zeros((rp, c), F32) + bias
        for k, o in enumerate(offsets):
            acc = acc + w_ref[k:k + 1, :] * src[r0 + o:r0 + o + rp, :]
        put(slice(r0, r0 + rp), acc)


def _taps_fwd(sc, w_ref, width, halo, tile, put, bias):
    _taps(sc, w_ref, [halo - (width - 1) + k for k in range(width)], tile, put, bias)


def _taps_bwd_x(sc_d, w_ref, width, tile, put):
    _taps(sc_d, w_ref, [(width - 1) - k for k in range(width)], tile, put)


def _taps_bwd_w(dy, sc, dw_ref, width, halo, tile, i):
    @pl.when(i == 0)
    def _():
        dw_ref[...] = jnp.zeros_like(dw_ref)

    for k in range(width):
        o = halo - (width - 1) + k
        dw_ref[k:k + 1, :] += jnp.sum(dy * sc[o:o + tile, :], axis=0, keepdims=True)


def _ln_stats(x):
    mu = jnp.mean(x, axis=-1, keepdims=True)
    xc = x - mu
    r = lax.rsqrt(jnp.mean(xc * xc, axis=-1, keepdims=True) + EPS)
    return xc * r, r


def conv_group_fwd(proj, cw, cb, lg, lb, name):
    n = proj.shape[0]
    d = cw.shape[1]
    tile = _row_tile(n, 256)
    halo = 32

    def body(v_ref, g_ref, vp_ref, gp_ref, cw_ref, cb_ref, lg_ref, lb_ref, u_ref, u1_ref, sc):
        i = pl.program_id(0)
        first = (i > 0).astype(F32)
        sc[0:halo, :] = vp_ref[tile - halo:, :] * _sigmoid(gp_ref[tile - halo:, :]) * first
        sc[halo:, :] = v_ref[...] * _sigmoid(g_ref[...])
        def put(rows, acc):
            u1_ref[rows, :] = acc

        _taps_fwd(sc, cw_ref, CONV_WIDTH, halo, tile, put, cb_ref[...])
        xh, _ = _ln_stats(u1_ref[...])
        y = xh * lg_ref[...] + lb_ref[...]
        u_ref[...] = (y * _sigmoid(y)).astype(u_ref.dtype)

    return _call(body, name, (n // tile,),
                 [_rs(tile, d, 0), _rs(tile, d, 1), _rs(tile, d, 0, -1), _rs(tile, d, 1, -1),
                  _ps(cw.shape), _ps((1, d)), _ps((1, d)), _ps((1, d))],
                 [_rs(tile, d), _rs(tile, d)], [S((n, d), MXU_DTYPE), S((n, d), F32)],
                 scratch=[pltpu.VMEM((halo + tile, d), F32)], sem=("arbitrary",))(proj, proj, proj, proj, cw, cb, lg, lb)


def conv_group_bwd(du, u1, proj, cw, lg, lb, name):
    n = proj.shape[0]
    d = cw.shape[1]
    tile = _row_tile(n, 256)
    halo = 32
    nt = n // tile

    def body(du_ref, dun_ref, u1_ref, u1n_ref, v_ref, g_ref, vp_ref, gp_ref, cw_ref, lg_ref, lb_ref,
             dp_ref, dcw_ref, dcb_ref, dlg_ref, dlb_ref, sc, sc_d):
        i = pl.program_id(0)

        def ln_swish_bwd(dy_, u1_):
            xh, r = _ln_stats(u1_)
            y = xh * lg_ref[...] + lb_ref[...]
            s = _sigmoid(y)
            dyy = dy_ * s * (1.0 + y * (1.0 - s))
            dxh = dyy * lg_ref[...]
            dx = r * (dxh - jnp.mean(dxh, axis=-1, keepdims=True) - xh * jnp.mean(dxh * xh, axis=-1, keepdims=True))
            return dx, jnp.sum(dyy * xh, axis=0, keepdims=True), jnp.sum(dyy, axis=0, keepdims=True)

        du1, dlg, dlb = ln_swish_bwd(du_ref[...].astype(F32), u1_ref[...])
        du1n, _, _ = ln_swish_bwd(dun_ref[0:halo, :].astype(F32), u1n_ref[0:halo, :])
        sc_d[0:tile, :] = du1
        sc_d[tile:, :] = du1n * (i < nt - 1).astype(F32)
        sc[0:halo, :] = vp_ref[tile - halo:, :] * _sigmoid(gp_ref[tile - halo:, :]) * (i > 0).astype(F32)
        sc[halo:, :] = v_ref[...] * _sigmoid(g_ref[...])

        def put(rows, du0):
            sig = _sigmoid(g_ref[rows, :])
            dp_ref[rows, :d] = (du0 * sig).astype(dp_ref.dtype)
            dp_ref[rows, d:] = (du0 * v_ref[rows, :] * sig * (1.0 - sig)).astype(dp_ref.dtype)

        _taps_bwd_x(sc_d, cw_ref, CONV_WIDTH, tile, put)
        _taps_bwd_w(du1, sc, dcw_ref, CONV_WIDTH, halo, tile, i)
        _acc(dcb_ref, i, jnp.sum(du1, axis=0, keepdims=True))
        _acc(dlg_ref, i, dlg)
        _acc(dlb_ref, i, dlb)

    return _call(body, name, (nt,),
                 [_rs(tile, d), _rs(tile, d, 0, 1, nt), _rs(tile, d), _rs(tile, d, 0, 1, nt),
                  _rs(tile, d, 0), _rs(tile, d, 1), _rs(tile, d, 0, -1), _rs(tile, d, 1, -1),
                  _ps(cw.shape), _ps((1, d)), _ps((1, d))],
                 [_rs(tile, 2 * d), _ps(cw.shape), _ps((1, d)), _ps((1, d)), _ps((1, d))],
                 [S((n, proj.shape[1]), MXU_DTYPE), S(cw.shape, F32), S((1, d), F32), S((1, d), F32), S((1, d), F32)],
                 scratch=[pltpu.VMEM((halo + tile, d), F32), pltpu.VMEM((tile + halo, d), F32)],
                 sem=("arbitrary",))(du, du, u1, u1, proj, proj, proj, proj, cw, lg, lb)


def ssm_conv_fwd(proj, dtr, sw, sb, dtb, name):
    n = proj.shape[0]
    w = sw.shape[1]
    inner = SSM_HEADS * HEAD_DIM
    tile = _row_tile(n, 256)
    halo = 8

    def body(x_ref, xp_ref, dtr_ref, sw_ref, sb_ref, dtb_ref, pre_ref, xs_ref, bc_ref, dt_ref, sc):
        i = pl.program_id(0)
        sc[0:halo, :] = xp_ref[tile - halo:, :] * (i > 0).astype(F32)
        sc[halo:, :] = x_ref[...]
        def put(rows, acc):
            pre_ref[rows, :] = acc

        _taps_fwd(sc, sw_ref, SSM_CONV, halo, tile, put, sb_ref[...])
        pre = pre_ref[...]
        act = pre * _sigmoid(pre)
        xs_ref[...] = act[:, :inner]
        bc_ref[...] = act[:, inner:]
        dt = _softplus(dtr_ref[...] + dtb_ref[...])
        dt_ref[...] = jnp.where(_iota(dt.shape, 1) < SSM_HEADS, dt, 0.0)

    return _call(body, name, (n // tile,),
                 [_rs(tile, w, 2), _rs(tile, w, 2, -1), _rs(tile, LANE), _ps(sw.shape), _ps((1, w)), _ps((1, LANE))],
                 [_rs(tile, w), _rs(tile, inner), _rs(tile, w - inner), _rs(tile, LANE)],
                 [S((n, w), F32), S((n, inner), F32), S((n, w - inner), F32), S((n, LANE), F32)],
                 scratch=[pltpu.VMEM((halo + tile, w), F32)], sem=("arbitrary",))(proj, proj, dtr, sw, sb, dtb)


def ssm_conv_bwd(dxs, dbc, pre, proj, sw, dproj, name):
    n = proj.shape[0]
    w = sw.shape[1]
    inner = SSM_HEADS * HEAD_DIM
    tile = _row_tile(n, 256)
    halo = 8
    nt = n // tile

    def body(dxs_ref, dxsn_ref, dbc_ref, dbcn_ref, pre_ref, pren_ref, x_ref, xp_ref, sw_ref, dp_in_ref,
             dx_ref, dsw_ref, dsb_ref, sc, sc_d):
        i = pl.program_id(0)

        def silu_bwd(d_, p_):
            s = _sigmoid(p_)
            return d_ * s * (1.0 + p_ * (1.0 - s))

        sc_d[0:tile, :inner] = silu_bwd(dxs_ref[...], pre_ref[:, :inner])
        sc_d[0:tile, inner:] = silu_bwd(dbc_ref[...], pre_ref[:, inner:])
        last = (i < nt - 1).astype(F32)
        sc_d[tile:, :inner] = silu_bwd(dxsn_ref[0:halo, :], pren_ref[0:halo, :inner]) * last
        sc_d[tile:, inner:] = silu_bwd(dbcn_ref[0:halo, :], pren_ref[0:halo, inner:]) * last
        sc[0:halo, :] = xp_ref[tile - halo:, :] * (i > 0).astype(F32)
        sc[halo:, :] = x_ref[...]
        dpre = sc_d[0:tile, :]
        def put(rows, acc):
            dx_ref[rows, :] = acc.astype(dx_ref.dtype)

        _taps_bwd_x(sc_d, sw_ref, SSM_CONV, tile, put)
        _taps_bwd_w(dpre, sc, dsw_ref, SSM_CONV, halo, tile, i)
        _acc(dsb_ref, i, jnp.sum(dpre, axis=0, keepdims=True))

    return pl.pallas_call(
        body, name=name, grid=(nt,),
        in_specs=[_rs(tile, inner), _rs(tile, inner, 0, 1, nt), _rs(tile, w - inner), _rs(tile, w - inner, 0, 1, nt),
                  _rs(tile, w), _rs(tile, w, 0, 1, nt), _rs(tile, w, 2), _rs(tile, w, 2, -1), _ps(sw.shape), ANY],
        out_specs=[_rs(tile, w, 2), _ps(sw.shape), _ps((1, w))],
        out_shape=[S(dproj.shape, dproj.dtype), S(sw.shape, F32), S((1, w), F32)],
        scratch_shapes=[pltpu.VMEM((halo + tile, w), F32), pltpu.VMEM((tile + halo, w), F32)],
        input_output_aliases={9: 0},
        compiler_params=pltpu.CompilerParams(dimension_semantics=("arbitrary",), vmem_limit_bytes=VMEM_LIMIT),
    )(dxs, dxs, dbc, dbc, pre, pre, proj, proj, sw, dproj)


def _ssd_prologue(dt_ref, dtT_ref, al_ref, alc_ref):
    row = _iota((CHUNK, CHUNK), 0)
    col = _iota((CHUNK, CHUNK), 1)
    dt = dt_ref[:, :SSM_HEADS]
    a_row = -jnp.exp(al_ref[:, :SSM_HEADS])
    a_col = -jnp.exp(alc_ref[...])
    cs = _01mm((row >= col).astype(F32), dt * a_row)
    csT = _mm01(dtT_ref[...] * a_col, (row <= col).astype(F32))
    return dt, a_row, cs, csT, row, col


def _decay(cs, csT, h, row, col):
    lm = jnp.exp(jnp.where(row >= col, cs[:, h:h + 1] - csT[h:h + 1, :], -1e30))
    lmT = jnp.exp(jnp.where(col >= row, csT[h:h + 1, :] - cs[:, h:h + 1], -1e30))
    return lm, lmT


def ssd_fwd(xs, bc, dt, dtT, alog_row, alog_col, name):
    n, width = xs.shape
    nc = n // CHUNK
    gw = width // SSM_GROUPS
    hpg = SSM_HEADS // SSM_GROUPS
    ns = SSM_STATE

    def body(xs_ref, bc_ref, dt_ref, dtT_ref, al_ref, alc_ref, y_ref, hs_ref, h_sc):
        i = pl.program_id(0)

        @pl.when(i == 0)
        def _():
            h_sc[...] = jnp.zeros_like(h_sc)

        dt, a_row, cs, csT, row, col = _ssd_prologue(dt_ref, dtT_ref, al_ref, alc_ref)
        indT = _head_indicator(width, SSM_HEADS, transposed=True)
        dt_full = _mm01(dt, indT)
        e_full = jnp.exp(_mm01(cs, indT))
        dte_full = jnp.exp(_mm01(cs[CHUNK - 1:CHUNK, :] - cs, indT))
        xt = xs_ref[...] * dt_full
        hs_ref[0] = h_sc[...]
        lo = _iota((CHUNK, 2 * HEAD_DIM), 1) < HEAD_DIM
        for g in range(SSM_GROUPS):
            bg = bc_ref[:, g * ns:(g + 1) * ns]
            cg = bc_ref[:, (SSM_GROUPS + g) * ns:(SSM_GROUPS + g + 1) * ns]
            gm = _mm(cg, bg, NT)
            hg = h_sc[g * gw:(g + 1) * gw, :]
            yoff = e_full[:, g * gw:(g + 1) * gw] * _mm(cg, hg, NT)
            for pr in range(hpg // 2):
                h0 = g * hpg + 2 * pr
                c0 = h0 * HEAD_DIM
                xp = xt[:, c0:c0 + 2 * HEAD_DIM]
                m0 = gm * _decay(cs, csT, h0, row, col)[0]
                m1 = gm * _decay(cs, csT, h0 + 1, row, col)[0]
                yd = jnp.where(lo, _mm(m0, xp), _mm(m1, xp))
                y_ref[:, c0:c0 + 2 * HEAD_DIM] = yd + yoff[:, 2 * pr * HEAD_DIM:(2 * pr + 2) * HEAD_DIM]
            sg = _mm(xt[:, g * gw:(g + 1) * gw] * dte_full[:, g * gw:(g + 1) * gw], bg, TN)
            for hh in range(hpg):
                h = g * hpg + hh
                r0 = h * HEAD_DIM
                h_sc[r0:r0 + HEAD_DIM, :] = (h_sc[r0:r0 + HEAD_DIM, :] * jnp.exp(csT[h:h + 1, CHUNK - 1:CHUNK])
                                             + sg[hh * HEAD_DIM:(hh + 1) * HEAD_DIM, :])

    bcw = bc.shape[1]
    return _call(body, name, (nc,),
                 [_rs(CHUNK, width), _rs(CHUNK, bcw), _rs(CHUNK, LANE), pl.BlockSpec((SSM_HEADS, CHUNK), lambda i: (0, i)),
                  _ps((1, LANE)), _ps((SSM_HEADS, 1))],
                 [_rs(CHUNK, width), pl.BlockSpec((1, width, ns), lambda i: (i, 0, 0))],
                 [S((n, width), F32), S((nc, width, ns), F32)],
                 scratch=[pltpu.VMEM((width, ns), F32)], sem=("arbitrary",))(xs, bc, dt, dtT, alog_row, alog_col)


def ssd_bwd(xs, bc, dt, dtT, alog_row, alog_col, hs, dy, dxs_skip, name):
    n, width = xs.shape
    nc = n // CHUNK
    gw = width // SSM_GROUPS
    hpg = SSM_HEADS // SSM_GROUPS
    ns = SSM_STATE
    bcw = bc.shape[1]

    def body(xs_ref, bc_ref, dt_ref, dtT_ref, al_ref, alc_ref, hs_ref, dy_ref, skip_ref,
             dxs_ref, dbc_ref, ddtr_ref, dal_ref, ddtb_ref, dh_sc, dxt_sc):
        i = pl.program_id(0)

        @pl.when(i == 0)
        def _():
            dh_sc[...] = jnp.zeros_like(dh_sc)

        dt, a_row, cs, csT, row, col = _ssd_prologue(dt_ref, dtT_ref, al_ref, alc_ref)
        indT = _head_indicator(width, SSM_HEADS, transposed=True)
        ind = _head_indicator(width, SSM_HEADS)
        dt_full = _mm01(dt, indT)
        e_full = jnp.exp(_mm01(cs, indT))
        cs_last = cs[CHUNK - 1:CHUNK, :]
        dte = jnp.exp(cs_last - cs)
        dte_full = _mm01(dte, indT)
        xs_ = xs_ref[...]
        xt = xs_ * dt_full
        dy_ = dy_ref[...]
        hmat = hs_ref[0]
        ds = dh_sc[...]
        lo = _iota((CHUNK, 2 * HEAD_DIM), 1) < HEAD_DIM
        head_lane = _iota((1, SSM_HEADS), 1)
        dcs = jnp.zeros((CHUNK, SSM_HEADS), F32)
        ddte = jnp.zeros((CHUNK, SSM_HEADS), F32)
        for g in range(SSM_GROUPS):
            sl = slice(g * gw, (g + 1) * gw)
            bg = bc_ref[:, g * ns:(g + 1) * ns]
            cg = bc_ref[:, (SSM_GROUPS + g) * ns:(SSM_GROUPS + g + 1) * ns]
            gm = _mm(cg, bg, NT)
            gmT = _mm(bg, cg, NT)
            hg = hmat[sl, :]
            dsg = ds[sl, :]
            dyg = dy_[:, sl]
            xtg = xt[:, sl]
            yoff = e_full[:, sl] * _mm(cg, hg, NT)
            edy = e_full[:, sl] * dyg
            bds = _mm(bg, dsg, NT)
            dxt_g = dte_full[:, sl] * bds
            ddte = ddte + _mm01(xtg * bds, ind[sl, :])
            dcs = dcs + _mm01(dyg * yoff, ind[sl, :])
            db = _mm(xtg * dte_full[:, sl], dsg)
            dc = _mm(edy, hg)
            dhc = _mm(edy, cg, TN)
            dgs = jnp.zeros((CHUNK, CHUNK), F32)
            dgTs = jnp.zeros((CHUNK, CHUNK), F32)
            for pr in range(hpg // 2):
                h0 = g * hpg + 2 * pr
                c0 = 2 * pr * HEAD_DIM
                xp = xtg[:, c0:c0 + 2 * HEAD_DIM]
                dyp = dyg[:, c0:c0 + 2 * HEAD_DIM]
                rr = []
                for h, half in ((h0, lo), (h0 + 1, jnp.logical_not(lo))):
                    lm, lmT = _decay(cs, csT, h, row, col)
                    xm = jnp.where(half, xp, 0.0)
                    dm = _mm(dyp, xm, NT)
                    dmT = _mm(xm, dyp, NT)
                    mT = gmT * lmT
                    z = jnp.sum(dm * (gm * lm), axis=1, keepdims=True) - jnp.sum(dmT * mT, axis=1, keepdims=True)
                    dcs = dcs + z * (head_lane == h).astype(F32)
                    dgs = dgs + dm * lm
                    dgTs = dgTs + dmT * lmT
                    rr.append(_mm(mT, dyp))
                dxt_sc[:, g * gw + c0:g * gw + c0 + 2 * HEAD_DIM] = jnp.where(lo, rr[0], rr[1]) + dxt_g[:, c0:c0 + 2 * HEAD_DIM]
            dbc_ref[:, g * ns:(g + 1) * ns] = db + _mm(dgTs, cg)
            dbc_ref[:, (SSM_GROUPS + g) * ns:(SSM_GROUPS + g + 1) * ns] = dc + _mm(dgs, bg)
            for hh in range(hpg):
                h = g * hpg + hh
                r0 = h * HEAD_DIM
                dh_sc[r0:r0 + HEAD_DIM, :] = (dhc[hh * HEAD_DIM:(hh + 1) * HEAD_DIM, :]
                                              + jnp.exp(csT[h:h + 1, CHUNK - 1:CHUNK]) * ds[r0:r0 + HEAD_DIM, :])
        t = ddte * dte
        per_head = jnp.sum(jnp.sum(ds * hmat, axis=1, keepdims=True) * ind, axis=0, keepdims=True)
        last_add = jnp.sum(t, axis=0, keepdims=True) + jnp.exp(cs_last) * per_head
        dcs = dcs - t + jnp.where(_iota((CHUNK, SSM_HEADS), 0) == CHUNK - 1, last_add, 0.0)
        dadt = _01mm((row <= col).astype(F32), dcs)
        dxt = dxt_sc[...]
        ddt = dadt * a_row + _mm01(dxt * xs_, ind)
        dxs_ref[...] = dxt * dt_full + skip_ref[...]
        ddtr = ddt * (1.0 - jnp.exp(-dt))
        ddtr_ref[...] = jnp.zeros_like(ddtr_ref)
        ddtr_ref[:, :SSM_HEADS] = ddtr.astype(ddtr_ref.dtype)
        _acc(dal_ref, i, jnp.sum(dadt * dt, axis=0, keepdims=True) * a_row)
        _acc(ddtb_ref, i, jnp.sum(ddtr, axis=0, keepdims=True))

    rev = lambda i: (nc - 1 - i, 0)
    return _call(body, name, (nc,),
                 [pl.BlockSpec((CHUNK, width), rev), pl.BlockSpec((CHUNK, bcw), rev), pl.BlockSpec((CHUNK, LANE), rev),
                  pl.BlockSpec((SSM_HEADS, CHUNK), lambda i: (0, nc - 1 - i)), _ps((1, LANE)), _ps((SSM_HEADS, 1)),
                  pl.BlockSpec((1, width, ns), lambda i: (nc - 1 - i, 0, 0)), pl.BlockSpec((CHUNK, width), rev),
                  pl.BlockSpec((CHUNK, width), rev)],
                 [pl.BlockSpec((CHUNK, width), rev), pl.BlockSpec((CHUNK, bcw), rev), pl.BlockSpec((CHUNK, LANE), rev),
                  _ps((1, SSM_HEADS)), _ps((1, SSM_HEADS))],
                 [S((n, width), F32), S((n, bcw), F32), S((n, LANE), MXU_DTYPE), S((1, SSM_HEADS), F32), S((1, SSM_HEADS), F32)],
                 scratch=[pltpu.VMEM((width, ns), F32), pltpu.VMEM((CHUNK, width), F32)],
                 sem=("arbitrary",))(xs, bc, dt, dtT, alog_row, alog_col, hs, dy, dxs_skip)


def ssm_gate_fwd(yssd, xs, proj, dfull, gamma, name):
    n, d = yssd.shape
    tile = _row_tile(n, 256)
    gw = d // SSM_GROUPS

    def body(y_ref, xs_ref, z_ref, df_ref, gm_ref, o_ref):
        z = z_ref[...]
        y2 = (y_ref[...] + df_ref[...] * xs_ref[...]) * (z * _sigmoid(z))
        for g in range(SSM_GROUPS):
            yg = y2[:, g * gw:(g + 1) * gw]
            r = lax.rsqrt(jnp.mean(yg * yg, axis=-1, keepdims=True) + EPS)
            o_ref[:, g * gw:(g + 1) * gw] = (yg * r * gm_ref[:, g * gw:(g + 1) * gw]).astype(o_ref.dtype)

    return _call(body, name, (n // tile,), [_rs(tile, d), _rs(tile, d), _rs(tile, d, 2), _ps((1, d)), _ps((1, d))],
                 _rs(tile, d), S((n, d), MXU_DTYPE), sem=("parallel",))(yssd, xs, proj, dfull, gamma)


def ssm_gate_bwd(dy3, yssd, xs, proj, dfull, gamma, dproj, name):
    n, d = yssd.shape
    tile = _row_tile(n, 256)
    gw = d // SSM_GROUPS

    def body(dy_ref, y_ref, xs_ref, z_ref, df_ref, gm_ref, dp_in_ref, dys_ref, dxs_ref, dz_ref, dgm_ref, dd_ref):
        i = pl.program_id(0)
        z = z_ref[...]
        s = _sigmoid(z)
        xs_ = xs_ref[...]
        y1 = y_ref[...] + df_ref[...] * xs_
        y2 = y1 * (z * s)
        dy_ = dy_ref[...].astype(F32)
        dgm = []
        dy2 = []
        for g in range(SSM_GROUPS):
            sl = slice(g * gw, (g + 1) * gw)
            dxg, dgg = _rms_bwd_math(y2[:, sl], gm_ref[:, sl], dy_[:, sl])
            dy2.append(dxg)
            dgm.append(dgg)
        dy2 = jnp.concatenate(dy2, axis=1)
        dy1 = dy2 * (z * s)
        dys_ref[...] = dy1
        dxs_ref[...] = dy1 * df_ref[...]
        dz_ref[...] = (dy2 * y1 * s * (1.0 + z * (1.0 - s))).astype(dz_ref.dtype)
        _acc(dgm_ref, i, jnp.concatenate(dgm, axis=1))
        colsum = jnp.broadcast_to(jnp.sum(dy1 * xs_, axis=0, keepdims=True), (8, d))
        _acc(dd_ref, i, _mm01(colsum, _head_indicator(d, SSM_HEADS))[0:1, :])

    return pl.pallas_call(
        body, name=name, grid=(n // tile,),
        in_specs=[_rs(tile, d), _rs(tile, d), _rs(tile, d), _rs(tile, d, 2), _ps((1, d)), _ps((1, d)), ANY],
        out_specs=[_rs(tile, d), _rs(tile, d), _rs(tile, d, 2), _ps((1, d)), _ps((1, SSM_HEADS))],
        out_shape=[S((n, d), F32), S((n, d), F32), S(dproj.shape, dproj.dtype), S((1, d), F32), S((1, SSM_HEADS), F32)],
        input_output_aliases={6: 2},
        compiler_params=pltpu.CompilerParams(dimension_semantics=("arbitrary",), vmem_limit_bytes=VMEM_LIMIT),
    )(dy3, yssd, xs, proj, dfull, gamma, dproj)


def _rope128(x, cos, sin_signed):
    half = HEAD_DIM // 2
    lane = _iota(x.shape, 1)
    partner = jnp.where((lane % HEAD_DIM) < half, pltpu.roll(x, LANE - half, 1), pltpu.roll(x, half, 1))
    return x * cos + partner * sin_signed


def rope_fwd(qkv, cos, sin, name):
    n, w = qkv.shape
    qw = ATT_HEADS * HEAD_DIM
    kw = ATT_KV_HEADS * HEAD_DIM
    tile = _row_tile(n, 256)

    def body(x_ref, c_ref, s_ref, q_ref, k_ref, v_ref):
        c, s = c_ref[...], s_ref[...]
        for j in range(qw // LANE):
            q_ref[:, j * LANE:(j + 1) * LANE] = _rope128(x_ref[:, j * LANE:(j + 1) * LANE], c, s).astype(q_ref.dtype)
        for j in range(kw // LANE):
            k_ref[:, j * LANE:(j + 1) * LANE] = _rope128(x_ref[:, qw + j * LANE:qw + (j + 1) * LANE], c, s).astype(k_ref.dtype)
        v_ref[...] = x_ref[:, qw + kw:].astype(v_ref.dtype)

    return _call(body, name, (n // tile,), [_rs(tile, w), _rs(tile, LANE), _rs(tile, LANE)],
                 [_rs(tile, qw), _rs(tile, kw), _rs(tile, kw)],
                 [S((n, qw), MXU_DTYPE), S((n, kw), MXU_DTYPE), S((n, kw), MXU_DTYPE)], sem=("parallel",))(qkv, cos, sin)


ATT_GROUP = ATT_HEADS // ATT_KV_HEADS


def _attn_mask(i):
    row = _iota((ATT_GROUP * WINDOW, 2 * WINDOW), 0) % WINDOW
    s = _iota((ATT_GROUP * WINDOW, 2 * WINDOW), 1)
    return (s > row) & (s <= row + WINDOW) & ((s >= WINDOW) | (i > 0))


def _stack_heads(ref, j, kh, lo):
    parts = []
    for t in range(ATT_GROUP):
        h = ATT_GROUP * j + t
        blk = ref[:, (h // 2) * LANE:(h // 2 + 1) * LANE]
        blk = jnp.where(lo if h % 2 == 0 else jnp.logical_not(lo), blk, jnp.zeros_like(blk))
        parts.append(blk if h % 2 == kh else pltpu.roll(blk, HEAD_DIM, 1))
    return jnp.concatenate(parts, axis=0)


def _unstack_heads(stacked, j, kh, lo, put):
    for t in range(0, ATT_GROUP, 2):
        h = ATT_GROUP * j + t
        even = stacked[t * WINDOW:(t + 1) * WINDOW, :]
        odd = stacked[(t + 1) * WINDOW:(t + 2) * WINDOW, :]
        even = even if kh == 0 else pltpu.roll(even, HEAD_DIM, 1)
        odd = odd if kh == 1 else pltpu.roll(odd, HEAD_DIM, 1)
        put(h // 2, jnp.where(lo, even, odd))


def _per_head_rows(ref, j):
    return jnp.concatenate([ref[:, ATT_GROUP * j + t:ATT_GROUP * j + t + 1] for t in range(ATT_GROUP)], axis=0)


def _per_head_scalar(ref, j):
    rows = _iota((ATT_GROUP * WINDOW, 1), 0) // WINDOW
    out = jnp.zeros((ATT_GROUP * WINDOW, 1), F32)
    for t in range(ATT_GROUP):
        out = out + jnp.where(rows == t, ref[:, ATT_GROUP * j + t:ATT_GROUP * j + t + 1], 0.0)
    return out


def attn_fwd(q, k, v, sinks, name):
    n, qw = q.shape
    kw = k.shape[1]
    nb = n // WINDOW
    scale = HEAD_DIM ** -0.5

    def body(q_ref, kc_ref, kp_ref, vc_ref, vp_ref, sk_ref, o_ref, lse_ref):
        i = pl.program_id(0)
        valid = _attn_mask(i)
        lo = _iota((WINDOW, LANE), 1) < HEAD_DIM
        k2 = jnp.concatenate([kp_ref[...], kc_ref[...]], axis=0)
        v2 = jnp.concatenate([vp_ref[...], vc_ref[...]], axis=0)
        lane1 = _iota((1, LANE), 1)
        lse = jnp.zeros((WINDOW, LANE), F32)

        def put_o(qb, val):
            o_ref[:, qb * LANE:(qb + 1) * LANE] = val.astype(o_ref.dtype)

        for j in range(ATT_KV_HEADS):
            kb, kh = j // 2, j % 2
            q4 = _stack_heads(q_ref, j, kh, lo)
            logits = jnp.where(valid, _mm(q4, k2[:, kb * LANE:(kb + 1) * LANE], NT) * scale, -1e30)
            sk = _per_head_scalar(sk_ref, j)
            m = jnp.maximum(jnp.max(logits, axis=-1, keepdims=True), sk)
            e = jnp.exp(logits - m)
            den = jnp.sum(e, axis=-1, keepdims=True) + jnp.exp(sk - m)
            lse4 = m + jnp.log(den)
            for t in range(ATT_GROUP):
                lse = lse + lse4[t * WINDOW:(t + 1) * WINDOW, :] * (lane1 == ATT_GROUP * j + t).astype(F32)
            _unstack_heads(_mm(e * (1.0 / den), v2[:, kb * LANE:(kb + 1) * LANE]), j, kh, lo, put_o)
        lse_ref[...] = lse

    return _call(body, name, (nb,),
                 [_rs(WINDOW, qw), _rs(WINDOW, kw), _rs(WINDOW, kw, 0, -1), _rs(WINDOW, kw), _rs(WINDOW, kw, 0, -1), _ps((1, LANE))],
                 [_rs(WINDOW, qw), _rs(WINDOW, LANE)], [S((n, qw), MXU_DTYPE), S((n, LANE), F32)],
                 sem=("parallel",))(q, k, k, v, v, sinks)


def attn_bwd(q, k, v, o, do, lse, sinks, name):
    n, qw = q.shape
    kw = k.shape[1]
    nb = n // WINDOW
    scale = HEAD_DIM ** -0.5

    def body(q_ref, kc_ref, kp_ref, vc_ref, vp_ref, o_ref, do_ref, lse_ref, sk_ref,
             dq_ref, dka_ref, dkb_ref, dva_ref, dvb_ref, dsk_ref):
        i = pl.program_id(0)
        valid = _attn_mask(i)
        lo = _iota((WINDOW, LANE), 1) < HEAD_DIM
        k2 = jnp.concatenate([kp_ref[...], kc_ref[...]], axis=0)
        v2 = jnp.concatenate([vp_ref[...], vc_ref[...]], axis=0)
        lane1 = _iota((1, LANE), 1)
        do_ = do_ref[...].astype(F32)
        delta = _mm01(do_ * o_ref[...].astype(F32), _head_indicator(qw, ATT_HEADS))
        dk2 = [jnp.zeros((2 * WINDOW, LANE), F32) for _ in range(kw // LANE)]
        dv2 = [jnp.zeros((2 * WINDOW, LANE), F32) for _ in range(kw // LANE)]
        dsk = jnp.zeros((1, LANE), F32)

        def put_dq(qb, val):
            dq_ref[:, qb * LANE:(qb + 1) * LANE] = val

        for j in range(ATT_KV_HEADS):
            kb, kh = j // 2, j % 2
            q4 = _stack_heads(q_ref, j, kh, lo)
            do4 = _stack_heads(do_ref, j, kh, lo)
            kk = k2[:, kb * LANE:(kb + 1) * LANE]
            vv = v2[:, kb * LANE:(kb + 1) * LANE]
            logits = jnp.where(valid, _mm(q4, kk, NT) * scale, -1e30)
            lse4 = _per_head_rows(lse_ref, j)
            p = jnp.exp(logits - lse4)
            dl = jnp.concatenate([delta[:, ATT_GROUP * j + t:ATT_GROUP * j + t + 1] for t in range(ATT_GROUP)], axis=0)
            ds = p * (_mm(do4, vv, NT) - dl) * scale
            sd = jnp.exp(_per_head_scalar(sk_ref, j) - lse4) * dl
            for t in range(ATT_GROUP):
                dsk = dsk - (jnp.sum(sd[t * WINDOW:(t + 1) * WINDOW, :], axis=0, keepdims=True)
                             * (lane1 == ATT_GROUP * j + t).astype(F32))
            _unstack_heads(_mm(ds, kk), j, kh, lo, put_dq)
            dk2[kb] = dk2[kb] + _mm(ds, q4, TN)
            dv2[kb] = dv2[kb] + _mm(p, do4, TN)
        for kb in range(kw // LANE):
            dkb_ref[:, kb * LANE:(kb + 1) * LANE] = dk2[kb][0:WINDOW, :]
            dka_ref[:, kb * LANE:(kb + 1) * LANE] = dk2[kb][WINDOW:, :]
            dvb_ref[:, kb * LANE:(kb + 1) * LANE] = dv2[kb][0:WINDOW, :]
            dva_ref[:, kb * LANE:(kb + 1) * LANE] = dv2[kb][WINDOW:, :]
        _acc(dsk_ref, i, dsk)

    return _call(body, name, (nb,),
                 [_rs(WINDOW, qw), _rs(WINDOW, kw), _rs(WINDOW, kw, 0, -1), _rs(WINDOW, kw), _rs(WINDOW, kw, 0, -1),
                  _rs(WINDOW, qw), _rs(WINDOW, qw), _rs(WINDOW, LANE), _ps((1, LANE))],
                 [_rs(WINDOW, qw)] + [_rs(WINDOW, kw)] * 4 + [_ps((1, LANE))],
                 [S((n, qw), F32)] + [S((n, kw), F32)] * 4 + [S((1, LANE), F32)],
                 sem=("arbitrary",))(q, k, k, v, v, o, do, lse, sinks)


def attn_grad_merge(dq, dka, dkb, dva, dvb, cos, sin, name):
    n, qw = dq.shape
    kw = dka.shape[1]
    nb = n // WINDOW
    w = qw + 2 * kw

    def body(dq_ref, dka_ref, dkb_ref, dva_ref, dvb_ref, c_ref, s_ref, o_ref, db_ref):
        i = pl.program_id(0)
        c, s = c_ref[...], -s_ref[...]
        nxt = (i < nb - 1).astype(F32)

        @pl.when(i == 0)
        def _():
            db_ref[...] = jnp.zeros_like(db_ref)

        def put(c0, val):
            o_ref[:, c0:c0 + val.shape[1]] = val.astype(o_ref.dtype)
            db_ref[:, c0:c0 + val.shape[1]] += jnp.sum(val, axis=0, keepdims=True)

        for j in range(qw // LANE):
            put(j * LANE, _rope128(dq_ref[:, j * LANE:(j + 1) * LANE], c, s))
        for j in range(kw // LANE):
            sl = slice(j * LANE, (j + 1) * LANE)
            put(qw + j * LANE, _rope128(dka_ref[:, sl] + dkb_ref[:, sl] * nxt, c, s))
        put(qw + kw, dva_ref[...] + dvb_ref[...] * nxt)

    return _call(body, name, (nb,),
                 [_rs(WINDOW, qw), _rs(WINDOW, kw), _rs(WINDOW, kw, 0, 1, nb), _rs(WINDOW, kw), _rs(WINDOW, kw, 0, 1, nb),
                  _rs(WINDOW, LANE), _rs(WINDOW, LANE)],
                 [_rs(WINDOW, w), _ps((1, w))], [S((n, w), MXU_DTYPE), S((1, w), F32)],
                 sem=("arbitrary",))(dq, dka, dkb, dva, dvb, cos, sin)


def _row(v):
    return v.reshape(1, -1)


def _pad_lanes(v, width=LANE):
    return jnp.pad(v.reshape(1, -1), ((0, 0), (0, width - v.size)))


class LayerWeights(dict):
    def __init__(self, small, fetch):
        super().__init__(small)
        self.fetch = fetch

    def need(self, k, after):
        if k not in self:
            self[k] = self.fetch(k, after)
        return self[k]


def ffn_fwd(h, g, w, keys, tag):
    xn, u, hm = swiglu_in(h, _row(g), w.need(keys[0], h), f"{tag}_in")
    return matmul(hm, w.need(keys[1], hm), "nn", f"{tag}_out", scale=0.5, res=h), (h, xn, u, hm)


class GradSink:
    ORDER = ("ffn1_w_out", "ffn2_w_out", "ple_gate_w", "att_w_o", "hyb_w_out", "ffn1_w_in", "ffn2_w_in", "att_w_qkv",
             "ple_proj_w", "hyb_w_in")

    def __init__(self, shard_shapes, bucket_of):
        self.where, rows = {}, {}
        for k in self.ORDER:
            n, r, c = shard_shapes[k]
            for li in range(n):
                layer = li if k in PER_LAYER else 2 * li + (0 if k in EVEN_ONLY else 1)
                rows_b = rows.setdefault(bucket_of(layer, _stage(k)), {})
                key = c if r % 32 == 0 else k
                off = -(-rows_b.get(key, (0, c))[0] // r) * r
                rows_b[key] = (-(-(off + r) // 32) * 32, c)
                self.where[k, li] = (bucket_of(layer, _stage(k)), key, "r" if _shard_axis(k) == 1 else "c", off, r)
        self.bufs = {b: {key: lax.empty((N_CHIPS, r, c), MXU_DTYPE) for key, (r, c) in rows_b.items()}
                     for b, rows_b in rows.items()}

    def mm(self, k, li, a, b, name, scale=None, c0=0, paired=False):
        bucket, key, kind, off, r = self.where[k, li]
        buf = self.bufs[bucket][key]
        slot = Slot(buf, kind, r if kind == "r" else buf.shape[2], off, c0, paired)
        self.bufs[bucket][key] = matmul(a, b, "tn", name, scale=scale, into=slot)

    def put(self, k, li, chip_major):
        b, key = self.where[k, li][:2]
        pad = self.bufs[b][key].shape[1] - chip_major.shape[1]
        self.bufs[b][key] = jnp.pad(chip_major.astype(self.bufs[b][key].dtype), ((0, 0), (0, pad), (0, 0)))


def ffn_bwd(dh, g, w_in, w_out, saved, tag, sink, keys, layer, after, colsum=False):
    h, xn, u, hm = saved
    sink.mm(keys[1], layer, hm, dh, f"{tag}_dwout", scale=0.5)
    du = swiglu_out_bwd(dh, w_out, u, after, f"{tag}_dhm")
    sink.mm(keys[0], layer, xn, du, f"{tag}_dwin", paired=True)
    outs = nt_rms_bwd(du, ColSharded(w_in.arr, paired=True), h, _row(g), dh, f"{tag}_dxn", colsum=colsum)
    return (outs[0], outs[1].reshape(-1)) + ((outs[2],) if colsum else ())


def _hyb_params(w):
    d = w["conv_dw_b"].size
    inner = SSM_HEADS * HEAD_DIM
    main = 3 * d + w["ssm_conv_b"].size
    return dict(
        w_main=w["hyb_w_in"][:main], w_dt=jnp.pad(w["hyb_w_in"][main:], ((0, LANE - SSM_HEADS), (0, 0))),
        cw=jnp.pad(w["conv_dw_w"], ((0, 32 - CONV_WIDTH), (0, 0))), cb=_row(w["conv_dw_b"]),
        lg=_row(w["conv_ln_g"]), lb=_row(w["conv_ln_b"]),
        sw=jnp.pad(w["ssm_conv_w"], ((0, 8 - SSM_CONV), (0, 0))), sb=_row(w["ssm_conv_b"]),
        dtb=_pad_lanes(w["ssm_dt_bias"]), al_row=_pad_lanes(w["ssm_a_log"]), al_col=w["ssm_a_log"].reshape(-1, 1),
        dfull=_row(jnp.repeat(w["ssm_d"], HEAD_DIM)), gamma=_row(w["ssm_norm"]), d=d, inner=inner, main=main)


def hyb_fwd(h, w, tag):
    w.need("hyb_w_in", h)
    q = _hyb_params(w)
    proj, xn = rms_matmul(h, _row(w["norm_mix"]), q["w_main"], "nt", f"{tag}_in")
    dtr = matmul(xn, q["w_dt"], "nt", f"{tag}_in_dt")
    u, u1 = conv_group_fwd(proj, q["cw"], q["cb"], q["lg"], q["lb"], f"{tag}_conv")
    pre, xs, bc, dt = ssm_conv_fwd(proj, dtr, q["sw"], q["sb"], q["dtb"], f"{tag}_sconv")
    dtT = dt[:, :SSM_HEADS].T
    yssd, hs = ssd_fwd(xs, bc, dt, dtT, q["al_row"], q["al_col"], f"{tag}_ssd")
    y = ssm_gate_fwd(yssd, xs, proj, q["dfull"], q["gamma"], f"{tag}_gate")
    wo = w.need("hyb_w_out", u)
    h2 = matmul(u, wo[:q["d"]], "nn", f"{tag}_out_a", res=h)
    h2 = matmul(y, wo[q["d"]:], "nn", f"{tag}_out_b", res=h2)
    return h2, (h, xn, proj, u, u1, pre, xs, bc, dt, dtT, yssd, hs, y)


def hyb_bwd(dh, w, saved, tag, sink, layer):
    q = _hyb_params(w)
    h, xn, proj, u, u1, pre, xs, bc, dt, dtT, yssd, hs, y = saved
    du = matmul(dh, w["hyb_w_out"][:q["d"]], "nt", f"{tag}_du")
    dy3 = matmul(dh, w["hyb_w_out"][q["d"]:], "nt", f"{tag}_dy")
    sink.mm("hyb_w_out", layer, u, dh, f"{tag}_dwo_a", c0=0)
    sink.mm("hyb_w_out", layer, y, dh, f"{tag}_dwo_b", c0=N_CHIPS // 2)
    dproj, dcw, dcb, dlg, dlb = conv_group_bwd(du, u1, proj, q["cw"], q["lg"], q["lb"], f"{tag}_dconv")
    dyssd, dxs_skip, dproj, dgamma, dd = ssm_gate_bwd(dy3, yssd, xs, proj, q["dfull"], q["gamma"], dproj, f"{tag}_dgate")
    dxs, dbc, ddtr, dalog, ddtb = ssd_bwd(xs, bc, dt, dtT, q["al_row"], q["al_col"], hs, dyssd, dxs_skip, f"{tag}_dssd")
    dproj, dsw, dsb = ssm_conv_bwd(dxs, dbc, pre, proj, q["sw"], dproj, f"{tag}_dsconv")
    dw_in = jnp.concatenate([matmul(dproj, xn, "tn", f"{tag}_dwin"),
                             matmul(ddtr, xn, "tn", f"{tag}_dwin_dt")[:SSM_HEADS]], axis=0)
    sink.put("hyb_w_in", layer, dw_in.reshape((N_CHIPS, -1) + dw_in.shape[1:]))
    dh2, dg = nt_rms_bwd(dproj, q["w_main"], h, _row(w["norm_mix"]), dh, f"{tag}_dxn", extra=(ddtr, q["w_dt"]), b_kd=True)
    grads = dict(norm_mix=dg.reshape(-1), conv_dw_w=dcw[:CONV_WIDTH], conv_dw_b=dcb.reshape(-1),
                 conv_ln_g=dlg.reshape(-1), conv_ln_b=dlb.reshape(-1), ssm_conv_w=dsw[:SSM_CONV], ssm_conv_b=dsb.reshape(-1),
                 ssm_dt_bias=ddtb.reshape(-1), ssm_a_log=dalog.reshape(-1), ssm_d=dd.reshape(-1), ssm_norm=dgamma.reshape(-1))
    return dh2, grads


def rope_tables(n):
    half = HEAD_DIM // 2
    inv = ROPE_THETA ** (-jnp.arange(0, HEAD_DIM, 2, dtype=F32) / HEAD_DIM)
    ang = jnp.arange(n, dtype=F32)[:, None] * inv[None, :]
    cos, sin = jnp.cos(ang), jnp.sin(ang)
    reps = LANE // HEAD_DIM
    return jnp.tile(jnp.concatenate([cos, cos], axis=1), (1, reps)), jnp.tile(jnp.concatenate([-sin, sin], axis=1), (1, reps))


def att_fwd(h, w, tables, tag):
    cos, sin = tables
    qkv, xn = rms_matmul(h, _row(w["norm_mix"]), w.need("att_w_qkv", h), "nn", f"{tag}_qkv", bias=_row(w["att_b_qkv"]))
    q, k, v = rope_fwd(qkv, cos, sin, f"{tag}_rope")
    sinks = _pad_lanes(w["att_sinks"])
    o, lse = attn_fwd(q, k, v, sinks, f"{tag}_attn")
    h2 = matmul(o, w.need("att_w_o", o), "nn", f"{tag}_o", bias=_row(w["att_b_o"]), res=h)
    return h2, (h, xn, q, k, v, o, lse, sinks)


def att_bwd(dh, dh_colsum, w, saved, tables, tag, sink, layer):
    cos, sin = tables
    h, xn, q, k, v, o, lse, sinks = saved
    do = matmul(dh, w["att_w_o"], "nt", f"{tag}_do")
    sink.mm("att_w_o", layer, o, dh, f"{tag}_dwo")
    dq, dka, dkb, dva, dvb, dsk = attn_bwd(q, k, v, o, do, lse, sinks, f"{tag}_dattn")
    dqkv, dbqkv = attn_grad_merge(dq, dka, dkb, dva, dvb, cos, sin, f"{tag}_drope")
    sink.mm("att_w_qkv", layer, xn, dqkv, f"{tag}_dwqkv")
    dh2, dg = nt_rms_bwd(dqkv, w["att_w_qkv"], h, _row(w["norm_mix"]), dh, f"{tag}_dxn")
    grads = dict(norm_mix=dg.reshape(-1), att_b_qkv=dbqkv.reshape(-1), att_sinks=dsk[0, :ATT_HEADS],
                 att_b_o=dh_colsum.reshape(-1))
    return dh2, grads


def ple_block_fwd(h, pe, w, tag):
    pp = matmul(pe, w.need("ple_proj_w", h), "nn", f"{tag}_proj")
    out, gl, xn = ple_fwd(h, _row(w["ple_norm"]), w.need("ple_gate_w", h), pp, f"{tag}_gate")
    return out, (h, xn, gl, pp, pe)


def ple_block_bwd(dh, w, saved, tag, sink, layer, after):
    h, xn, gl, pp, pe = saved
    dh2, dpp, dgl, dg = ple_bwd(dh, gl, pp, w["ple_gate_w"], h, _row(w["ple_norm"]), after, f"{tag}_dgate")
    sink.mm("ple_proj_w", layer, pe, dpp, f"{tag}_dwp")
    sink.mm("ple_gate_w", layer, xn, dgl, f"{tag}_dwg")
    return dh2, dict(ple_norm=dg.reshape(-1))


PER_LAYER = ("norm_ffn1", "ffn1_w_in", "ffn1_w_out", "norm_mix", "norm_ffn2", "ffn2_w_in", "ffn2_w_out",
             "ple_norm", "ple_gate_w", "ple_proj_w")
EVEN_ONLY = ("hyb_w_in", "conv_dw_w", "conv_dw_b", "conv_ln_g", "conv_ln_b", "ssm_conv_w", "ssm_conv_b",
             "ssm_dt_bias", "ssm_a_log", "ssm_d", "ssm_norm", "hyb_w_out")
ODD_ONLY = ("att_w_qkv", "att_b_qkv", "att_sinks", "att_w_o", "att_b_o")


def _layer_index(k, i):
    if k in PER_LAYER:
        return i
    if k in (EVEN_ONLY if i % 2 == 0 else ODD_ONLY):
        return i // 2
    return None


def _stage(k):
    return 0 if k.startswith("ffn1") else (2 if k.startswith(("ffn2", "ple")) else 1)


def trunk_fwd_bwd(x, pe, target, layers, final_norm, sink, stage_done):
    depth = len(layers)
    tables = rope_tables(x.shape[0])
    h = x
    saved = []
    for i, w in enumerate(layers):
        h, s1 = ffn_fwd(h, w["norm_ffn1"], w, ("ffn1_w_in", "ffn1_w_out"), f"l{i}_ffn1")
        if i % 2 == 0:
            h, s2 = hyb_fwd(h, w, f"l{i}_hyb")
        else:
            h, s2 = att_fwd(h, w, tables, f"l{i}_att")
        h, s3 = ffn_fwd(h, w["norm_ffn2"], w, ("ffn2_w_in", "ffn2_w_out"), f"l{i}_ffn2")
        h, s4 = ple_block_fwd(h, pe[i], w, f"l{i}_ple")
        saved.append((s1, s2, s3, s4))
    dh, dgf, loss = loss_head(h, _row(final_norm), target, "loss_head")
    grads = {}
    tie = dgf
    for i in reversed(range(depth)):
        w = layers[i]
        s1, s2, s3, s4 = saved[i]
        dh, g = ple_block_bwd(dh, w, s4, f"l{i}_ple", sink, i, tie)
        odd = i % 2 == 1
        out = ffn_bwd(dh, w["norm_ffn2"], w["ffn2_w_in"], w["ffn2_w_out"], s3, f"l{i}_ffn2", sink,
                      ("ffn2_w_in", "ffn2_w_out"), i, tie, colsum=odd)
        dh = out[0]
        g.update(norm_ffn2=out[1])
        if odd:
            dh, gm = att_bwd(dh, out[2], w, s2, tables, f"l{i}_att", sink, i // 2)
        else:
            dh, gm = hyb_bwd(dh, w, s2, f"l{i}_hyb", sink, i // 2)
        g.update(gm)
        tie = stage_done(i, 1, tie)
        out = ffn_bwd(dh, w["norm_ffn1"], w["ffn1_w_in"], w["ffn1_w_out"], s1, f"l{i}_ffn1", sink,
                      ("ffn1_w_in", "ffn1_w_out"), i, tie)
        dh = out[0]
        g.update(norm_ffn1=out[1])
        tie = stage_done(i, 0, tie)
        for k, v in g.items():
            grads.setdefault(k, []).insert(0, v)
    grads = {k: jnp.stack(v) for k, v in grads.items()}
    grads["final_norm"] = dgf.reshape(-1)
    return loss, dh, grads


def _me():
    return lax.axis_index("x"), lax.axis_index("y"), lax.axis_index("c")


def _flip(v, f):
    return 1 - v if f else v


def _remote(src, dst, send_sems, recv_sems, k, dev):
    return pltpu.make_async_remote_copy(src_ref=src, dst_ref=dst, send_sem=send_sems.at[k], recv_sem=recv_sems.at[k],
                                        device_id=dev, device_id_type=MESH)


CHIP_FLIPS = ((1, 0), (0, 1), (1, 1))
DEV_FLIPS = tuple((fx, fy, fc) for fx in (0, 1) for fy in (0, 1) for fc in (0, 1))[1:]


HBM = pl.BlockSpec(memory_space=pltpu.HBM)
SEM = pl.BlockSpec(memory_space=pltpu.SEMAPHORE)
DATAFLOW = pltpu.SideEffectType.DATAFLOW_SIDE_EFFECTING


def _core_half(ref, axis, c):
    if axis is None:
        return ref
    h = ref.shape[axis] // 2
    return ref.at[pl.ds(c * h, h), :] if axis == 0 else ref.at[:, pl.ds(c * h, h)]


def gather_start(xs, lands, halves, after, name):
    na = len(xs)

    def body(*refs):
        x_refs, land_refs = refs[:na], refs[na:2 * na]
        send_sems, recv_sems = refs[2 * na + 1], refs[2 * na + 2]
        token = refs[-1]
        mx, my, mc = _me()
        chip = 2 * mx + my
        for a in range(na):
            for j, (fx, fy) in enumerate(CHIP_FLIPS):
                _remote(_core_half(x_refs[a], halves[a], mc), _core_half(land_refs[a].at[chip], halves[a], mc),
                        send_sems, recv_sems, 3 * a + j, (_flip(mx, fx), _flip(my, fy), mc)).start()
        token[...] = jnp.zeros_like(token)

    outs = pl.pallas_call(
        body, name=name,
        out_shape=(pltpu.SemaphoreType.DMA((3 * na,)), pltpu.SemaphoreType.DMA((3 * na,)))
        + tuple(pltpu.HBM(x.shape, x.dtype) for x in xs) + tuple(pltpu.HBM(l.shape, l.dtype) for l in lands)
        + (S((8, LANE), F32),),
        in_specs=[HBM] * (2 * na) + [pl.BlockSpec(memory_space=pl.ANY)],
        out_specs=(SEM, SEM) + (HBM,) * (2 * na) + (pl.BlockSpec(memory_space=pltpu.VMEM),),
        input_output_aliases={a: 2 + a for a in range(2 * na)},
        compiler_params=pltpu.CompilerParams(has_side_effects=DATAFLOW),
    )(*[pltpu.with_memory_space_constraint(t, pltpu.HBM) for t in list(xs) + list(lands)], after)
    return outs[0], outs[1], list(outs[2:2 + na]), list(outs[2 + na:2 + 2 * na])


def gather_wait(send_sems, recv_sems, xs, lands, halves, first, after, name):
    na = len(xs)

    def body(*refs):
        x_refs, land_refs = refs[:na], refs[na:2 * na]
        send_sems, recv_sems = refs[2 * na], refs[2 * na + 1]
        mx, my, mc = _me()
        for a in range(na):
            for j, (fx, fy) in enumerate(CHIP_FLIPS):
                px, py = _flip(mx, fx), _flip(my, fy)
                cp = _remote(_core_half(x_refs[a], halves[a], mc), _core_half(land_refs[a].at[2 * px + py], halves[a], mc),
                             send_sems, recv_sems, 3 * (first + a) + j, (px, py, mc))
                cp.wait_send()
                cp.wait_recv()

    outs = pl.pallas_call(
        body, name=name,
        out_shape=tuple(pltpu.HBM(x.shape, x.dtype) for x in xs) + tuple(pltpu.HBM(l.shape, l.dtype) for l in lands),
        in_specs=[HBM] * (2 * na) + [SEM, SEM, pl.BlockSpec(memory_space=pl.ANY)], out_specs=(HBM,) * (2 * na),
        input_output_aliases={a: a for a in range(2 * na)},
        compiler_params=pltpu.CompilerParams(has_side_effects=DATAFLOW),
    )(*xs, *lands, send_sems, recv_sems, after)
    return list(outs[na:])


def forward_halves(land, axis, name):
    def body(in_ref, out_ref, send_sems, recv_sems):
        del in_ref
        mx, my, mc = _me()
        sib = (mx, my, 1 - mc)
        slots = [2 * _flip(mx, fx) + _flip(my, fy) for fx, fy in CHIP_FLIPS]
        cps = [_remote(_core_half(out_ref.at[s], axis, mc), _core_half(out_ref.at[s], axis, mc), send_sems, recv_sems, j, sib)
               for j, s in enumerate(slots)]
        for cp in cps:
            cp.start()
        for j, s in enumerate(slots):
            _remote(_core_half(out_ref.at[s], axis, mc), _core_half(out_ref.at[s], axis, 1 - mc), send_sems, recv_sems, j, sib).wait_recv()
        for cp in cps:
            cp.wait_send()

    return pl.pallas_call(
        body, name=name, out_shape=S(land.shape, land.dtype), in_specs=[ANY], out_specs=ANY, input_output_aliases={0: 0},
        scratch_shapes=[pltpu.SemaphoreType.DMA((3,)), pltpu.SemaphoreType.DMA((3,))])(land)


def all_gather_devices(v, name):
    r, l = v.shape

    def body(v_ref, out_ref, send_sems, recv_sems):
        mx, my, mc = _me()
        me = 4 * mx + 2 * my + mc
        peers = [(_flip(mx, fx), _flip(my, fy), _flip(mc, fc)) for fx, fy, fc in DEV_FLIPS]
        sends = [_remote(v_ref, out_ref.at[me], send_sems, recv_sems, j, p) for j, p in enumerate(peers)]
        for cp in sends:
            cp.start()
        for j, (px, py, pc) in enumerate(peers):
            _remote(v_ref, out_ref.at[4 * px + 2 * py + pc], send_sems, recv_sems, j, (px, py, pc)).wait_recv()
        for cp in sends:
            cp.wait_send()

    out = pl.pallas_call(
        body, name=name, out_shape=S((N_DEV, r, l), v.dtype), in_specs=[ANY], out_specs=ANY,
        scratch_shapes=[pltpu.SemaphoreType.DMA((7,)), pltpu.SemaphoreType.DMA((7,))])(v)
    me = 4 * lax.axis_index("x") + 2 * lax.axis_index("y") + lax.axis_index("c")
    return lax.dynamic_update_slice_in_dim(out, v[None], me, axis=0)


def gather_devices_start(v, name):
    me = 4 * lax.axis_index("x") + 2 * lax.axis_index("y") + lax.axis_index("c")
    land = lax.dynamic_update_slice_in_dim(lax.empty((N_DEV,) + v.shape, v.dtype), v[None], me, axis=0)

    def body(v_ref, land_ref, send_sems, recv_sems, v_thru, land_thru, token):
        mx, my, mc = _me()
        for j, (fx, fy, fc) in enumerate(DEV_FLIPS):
            _remote(v_ref, land_ref.at[4 * mx + 2 * my + mc], send_sems, recv_sems, j,
                    (_flip(mx, fx), _flip(my, fy), _flip(mc, fc))).start()
        token[...] = jnp.zeros_like(token)

    outs = pl.pallas_call(
        body, name=name,
        out_shape=(pltpu.SemaphoreType.DMA((7,)), pltpu.SemaphoreType.DMA((7,)), pltpu.HBM(v.shape, v.dtype),
                   pltpu.HBM(land.shape, land.dtype), S((8, LANE), F32)),
        in_specs=[HBM, HBM], out_specs=(SEM, SEM, HBM, HBM, pl.BlockSpec(memory_space=pltpu.VMEM)),
        input_output_aliases={0: 2, 1: 3}, compiler_params=pltpu.CompilerParams(has_side_effects=DATAFLOW),
    )(pltpu.with_memory_space_constraint(v, pltpu.HBM), pltpu.with_memory_space_constraint(land, pltpu.HBM))
    return outs[:4]


def gather_devices_wait(send_sems, recv_sems, v, land, after, name):
    def body(v_ref, land_ref, send_sems, recv_sems, after_ref, v_dead, got_ref):
        mx, my, mc = _me()
        for j, (fx, fy, fc) in enumerate(DEV_FLIPS):
            px, py, pc = _flip(mx, fx), _flip(my, fy), _flip(mc, fc)
            cp = _remote(v_ref, land_ref.at[4 * px + 2 * py + pc], send_sems, recv_sems, j, (px, py, pc))
            cp.wait_send()
            cp.wait_recv()

    return pl.pallas_call(
        body, name=name, out_shape=(pltpu.HBM(v.shape, v.dtype), pltpu.HBM(land.shape, land.dtype)),
        in_specs=[HBM, HBM, SEM, SEM, pl.BlockSpec(memory_space=pl.ANY)], out_specs=(HBM, HBM),
        input_output_aliases={0: 0, 1: 1}, compiler_params=pltpu.CompilerParams(has_side_effects=DATAFLOW),
    )(v, land, send_sems, recv_sems, after)[1]


def sum_devices(g8, name):
    nd, r, l = g8.shape
    tile = r
    for t in (512, 256, 128, 64, 32, 16, 8):
        if r % t == 0:
            tile = t
            break

    def body(g_ref, o_ref):
        acc = g_ref[0]
        for d in range(1, nd):
            acc = acc + g_ref[d]
        o_ref[...] = acc

    return _call(body, name, (r // tile,), [pl.BlockSpec((nd, tile, l), lambda i: (0, i, 0))], _rs(tile, l), S((r, l), F32),
                 sem=("parallel",))(g8)


def exchange_halves(gs, name):
    na = len(gs)
    nch = gs[0].shape[0]

    def body(*refs):
        g_refs, out_refs = refs[:na], refs[na:2 * na]
        send_sems, recv_sems = refs[2 * na:]
        mx, my, mc = _me()
        sib = (mx, my, 1 - mc)
        cps = []
        for a in range(na):
            half = gs[a].shape[1] // 2
            for j in range(nch):
                cps.append(_remote(g_refs[a].at[j, pl.ds((1 - mc) * half, half), :], out_refs[a].at[j],
                                   send_sems, recv_sems, nch * a + j, sib))
        for cp in cps:
            cp.start()
        for cp in cps:
            cp.wait_recv()
        for cp in cps:
            cp.wait_send()

    return pl.pallas_call(
        body, name=name, out_shape=[S((nch, g.shape[1] // 2, g.shape[2]), g.dtype) for g in gs],
        in_specs=[ANY] * na, out_specs=[ANY] * na,
        scratch_shapes=[pltpu.SemaphoreType.DMA((nch * na,)), pltpu.SemaphoreType.DMA((nch * na,))])(*gs)


def add_halves(g4, got, name):
    nch, r, l = g4.shape
    half = r // 2
    tile = _pick_rows(half)
    nt = half // tile

    def body(g_ref, r_ref, a_ref, own_ref):
        j = pl.program_id(1)
        chip = 2 * lax.axis_index("x") + lax.axis_index("y")
        val = g_ref[0].astype(F32) + r_ref[0].astype(F32)
        a_ref[0] = val.astype(a_ref.dtype)

        @pl.when(j == chip)
        def _():
            own_ref[...] = val

    return pl.pallas_call(
        body, name=name, grid=(nt, nch),
        in_specs=[pl.BlockSpec((1, tile, l), lambda i, j: (j, lax.axis_index("c") * nt + i, 0)),
                  pl.BlockSpec((1, tile, l), lambda i, j: (j, i, 0))],
        out_specs=[pl.BlockSpec((1, tile, l), lambda i, j: (j, i, 0)), pl.BlockSpec((tile, l), lambda i, j: (i, 0))],
        out_shape=[S((nch, half, l), MXU_DTYPE), S((half, l), F32)],
        compiler_params=pltpu.CompilerParams(dimension_semantics=("parallel", "arbitrary"), vmem_limit_bytes=VMEM_LIMIT))(g4, got)


def _pick_rows(r, cap=640):
    return next((t for t in range(cap - cap % 16, 15, -16) if r % t == 0), r)


def add_chips(own, got, name):
    h, l = own.shape
    tile = _pick_rows(h)

    def body(o_ref, g_ref, out_ref):
        out_ref[...] = ((o_ref[...] + g_ref[0].astype(F32)) + g_ref[1].astype(F32)) + g_ref[2].astype(F32)

    nt = h // tile
    return _call(body, name, (nt,), [_rs(tile, l), pl.BlockSpec((3, tile, l), lambda i: (0, i, 0))],
                 pl.BlockSpec((tile, l), lambda i: (lax.axis_index("c") * nt + i, 0)),
                 S((2 * h, l), F32), sem=("parallel",))(own, got)


def join_halves(bufs, name):
    na = len(bufs)

    def body(*refs):
        out_refs = refs[na:2 * na]
        send_sems, recv_sems = refs[2 * na:]
        mx, my, mc = _me()
        sib = (mx, my, 1 - mc)

        def half(a, hc):
            h = bufs[a].shape[0] // 2
            return out_refs[a].at[pl.ds(hc * h, h), :]

        cps = [_remote(half(a, mc), half(a, mc), send_sems, recv_sems, a, sib) for a in range(na)]
        for cp in cps:
            cp.start()
        for a in range(na):
            _remote(half(a, mc), half(a, 1 - mc), send_sems, recv_sems, a, sib).wait_recv()
        for cp in cps:
            cp.wait_send()

    return pl.pallas_call(
        body, name=name, out_shape=[S(b.shape, b.dtype) for b in bufs], in_specs=[ANY] * na, out_specs=[ANY] * na,
        input_output_aliases={a: a for a in range(na)},
        scratch_shapes=[pltpu.SemaphoreType.DMA((na,)), pltpu.SemaphoreType.DMA((na,))])(*bufs)


def exchange_chips_start(parts, name):
    na = len(parts)
    lands = [lax.empty((3,) + p.shape[1:], p.dtype) for p in parts]

    def body(*refs):
        a_refs, land_refs = refs[:na], refs[na:2 * na]
        send_sems, recv_sems = refs[2 * na], refs[2 * na + 1]
        mx, my, mc = _me()
        for j, (fx, fy) in enumerate(CHIP_FLIPS):
            px, py = _flip(mx, fx), _flip(my, fy)
            for a in range(na):
                _remote(a_refs[a].at[2 * px + py], land_refs[a].at[j], send_sems, recv_sems, 3 * a + j, (px, py, mc)).start()
        refs[-1][...] = jnp.zeros_like(refs[-1])

    outs = pl.pallas_call(
        body, name=name,
        out_shape=(pltpu.SemaphoreType.DMA((3 * na,)), pltpu.SemaphoreType.DMA((3 * na,)))
        + tuple(pltpu.HBM(t.shape, t.dtype) for t in list(parts) + lands) + (S((8, LANE), F32),),
        in_specs=[HBM] * (2 * na), out_specs=(SEM, SEM) + (HBM,) * (2 * na) + (pl.BlockSpec(memory_space=pltpu.VMEM),),
        input_output_aliases={a: 2 + a for a in range(2 * na)},
        compiler_params=pltpu.CompilerParams(has_side_effects=DATAFLOW),
    )(*[pltpu.with_memory_space_constraint(t, pltpu.HBM) for t in list(parts) + lands])
    return outs[0], outs[1], list(outs[2:2 + na]), list(outs[2 + na:2 + 2 * na]), outs[-1]


def exchange_chips_wait(send_sems, recv_sems, parts, lands, after, name):
    na = len(parts)

    def body(*refs):
        a_refs, land_refs = refs[:na], refs[na:2 * na]
        send_sems, recv_sems = refs[2 * na], refs[2 * na + 1]
        mx, my, mc = _me()
        for j, (fx, fy) in enumerate(CHIP_FLIPS):
            px, py = _flip(mx, fx), _flip(my, fy)
            for a in range(na):
                cp = _remote(a_refs[a].at[2 * px + py], land_refs[a].at[j], send_sems, recv_sems, 3 * a + j, (px, py, mc))
                cp.wait_send()
                cp.wait_recv()

    outs = pl.pallas_call(
        body, name=name, out_shape=tuple(pltpu.HBM(t.shape, t.dtype) for t in list(parts) + list(lands)),
        in_specs=[HBM] * (2 * na) + [SEM, SEM, pl.BlockSpec(memory_space=pl.ANY)], out_specs=(HBM,) * (2 * na),
        input_output_aliases={a: a for a in range(2 * na)},
        compiler_params=pltpu.CompilerParams(has_side_effects=DATAFLOW),
    )(*parts, *lands, send_sems, recv_sems, after)
    return list(outs[na:])


def reduce_begin(gs, tag):
    got = exchange_halves(gs, f"{tag}_d2d")
    sums = [add_halves(g, r, f"{tag}_add1_{i}") for i, (g, r) in enumerate(zip(gs, got))]
    return [own for _, own in sums], exchange_chips_start([a for a, _ in sums], f"{tag}_ici_start")


def reduce_end(state, after, tag):
    owns, (send_sems, recv_sems, parts, lands, _) = state
    got = exchange_chips_wait(send_sems, recv_sems, parts, lands, after, f"{tag}_ici_wait")
    return [add_chips(own, r, f"{tag}_add2_{i}") for i, (own, r) in enumerate(zip(owns, got))]


PACK_L = 1024


def _pack(arrs, dtype, row_mult, lead=None):
    lead_shape = () if lead is None else arrs[0].shape[:lead]
    flat = jnp.concatenate([a.astype(dtype).reshape(lead_shape + (-1,)) for a in arrs], axis=-1)
    n = flat.shape[-1]
    unit = row_mult * PACK_L
    total = -(-n // unit) * unit
    flat = jnp.pad(flat, [(0, 0)] * len(lead_shape) + [(0, total - n)])
    return flat.reshape(lead_shape + (total // PACK_L, PACK_L))


def _unpack(packed, shapes, lead=None):
    lead_shape = () if lead is None else packed.shape[:lead]
    flat = packed.reshape(lead_shape + (-1,))
    out, off = [], 0
    for shp in shapes:
        n = int(np.prod(shp))
        out.append(flat[..., off:off + n].reshape(lead_shape + tuple(shp)))
        off += n
    return out


def _to_full(gathered, axis):
    t = jnp.moveaxis(gathered, 0, axis)
    shp = t.shape
    return t.reshape(shp[:axis] + (shp[axis] * shp[axis + 1],) + shp[axis + 2:])


WEIGHTS = ("norm_ffn1", "ffn1_w_in", "ffn1_w_out", "norm_mix", "norm_ffn2", "ffn2_w_in", "ffn2_w_out", "ple_norm",
           "ple_gate_w", "ple_proj_w", "hyb_w_in", "conv_dw_w", "conv_dw_b", "conv_ln_g", "conv_ln_b", "ssm_conv_w",
           "ssm_conv_b", "ssm_dt_bias", "ssm_a_log", "ssm_d", "ssm_norm", "hyb_w_out", "att_w_qkv", "att_b_qkv",
           "att_sinks", "att_w_o", "att_b_o", "final_norm")
SHARD_AXIS = dict(ffn1_w_in=2, ffn1_w_out=1, ffn2_w_in=2, ffn2_w_out=1, ple_gate_w=1, ple_proj_w=2, hyb_w_in=2,
                  conv_dw_w=2, ssm_conv_w=2, hyb_w_out=1, att_w_qkv=2, att_b_qkv=1, att_w_o=1, att_b_o=1)
BIG = ("ffn1_w_in", "ffn1_w_out", "ffn2_w_in", "ffn2_w_out", "ple_gate_w", "ple_proj_w", "hyb_w_in", "hyb_w_out",
       "att_w_qkv", "att_w_o")
TRANSPOSED = ("hyb_w_in",)
FORWARDED = (0, 2)


def _shard_axis(k):
    return 1 if k in TRANSPOSED else SHARD_AXIS[k]
SMALL_SHARDED = ("conv_dw_w", "ssm_conv_w", "att_b_qkv", "att_b_o")
SMALL = tuple(k for k in WEIGHTS if k not in BIG)


def _step(x, p, target, w, m, v):
    mx, my = lax.axis_index("x"), lax.axis_index("y")
    chip = 2 * mx + my
    w, m, v = ({k: (a.transpose(0, 2, 1) if k in TRANSPOSED else a) for k, a in d.items()} for d in (w, m, v))

    depth = w["norm_ffn1"].shape[0]
    order = sorted([(k, i) for i in range(depth) for k in BIG if _layer_index(k, i) is not None],
                   key=lambda t: (t[1], _stage(t[0])))
    small_g = all_gather_devices(_pack([w[k] for k in SMALL_SHARDED], F32, 8), "gather_small")
    shards = [w[k][_layer_index(k, i)].astype(MXU_DTYPE) for k, i in order]
    lands = [lax.dynamic_update_slice_in_dim(lax.empty((N_CHIPS,) + s.shape, s.dtype), s[None], chip, axis=0) for s in shards]
    halves = [(0 if s.shape[0] % 32 == 0 else 1) if pos in FORWARDED else None for pos, s in enumerate(shards)]
    send_sems, recv_sems, shards, lands = gather_start(shards, lands, halves, small_g, "gather_start")

    def fetch(i, k, after):
        p = order.index((k, i))
        g, = gather_wait(send_sems, recv_sems, [shards[p]], [lands[p]], [halves[p]], p, after, f"gather_wait_l{i}_{k}")
        if halves[p] is not None:
            g = forward_halves(g, halves[p], f"gather_forward_l{i}_{k}")
        if _shard_axis(k) == 2:
            return ColSharded(g)
        return g.reshape(-1, g.shape[-1])

    small_g = small_g[0::2]
    small_full = {k: _to_full(g, SHARD_AXIS[k])
                  for k, g in zip(SMALL_SHARDED, _unpack(small_g, [w[k].shape for k in SMALL_SHARDED], lead=1))}
    layers = [LayerWeights({k: small_full.get(k, w[k])[_layer_index(k, i)] for k in SMALL if _layer_index(k, i) is not None},
                           functools.partial(fetch, i)) for i in range(depth)]

    def bucket_of(layer, stage):
        return (layer, 0) if layer > 0 else (0, min(stage, 1))

    sink = GradSink({k: w[k].shape for k in BIG}, bucket_of)
    begun = {}

    def stage_done(i, stage, tie):
        b = bucket_of(i, stage)
        if stage > 0 and bucket_of(i, stage - 1) == b:
            return tie
        begun[b] = reduce_begin(list(sink.bufs[b].values()), f"grads_l{b[0]}_{b[1]}")
        return begun[b][1][-1]

    loss, dx, grads = trunk_fwd_bwd(x[0], p[:, 0], target[0], layers, w["final_norm"], sink, stage_done)

    results = {}

    def finish(buckets, after, tag):
        halves = {b: reduce_end(begun[b], after, f"grads_l{b[0]}_{b[1]}") for b in buckets}
        joined = iter(join_halves([h for b in buckets for h in halves[b]], f"grads_join_{tag}"))
        reduced = {b: {c: next(joined) for c in sink.bufs[b]} for b in buckets}
        last = after
        for (k, li), (b, key, _, off, r) in sink.where.items():
            if b in buckets:
                g = reduced[b][key]
                if r % 8:
                    g, off = g[off:off + r], 0
                results[k] = adamw_layer(w[k], g, off, m[k], v[k], li, results.get(k), f"adamw_{k}_{li}")
                last = results[k][1]
        return last

    vec = gather_devices_start(_pack([loss[0:1, 0:1]] + [grads[k] for k in SMALL], F32, 8), "gather_vectors_start")
    order_b = list(begun)
    started_last = begun[order_b[-1]][1][-1]
    done = finish(order_b[-1:], finish(order_b[:-1], started_last, "early") if len(order_b) > 1 else dx, "last")
    g_out = {k: results[k][0] for k in BIG}
    vec = sum_devices(gather_devices_wait(*vec, done, "gather_vectors_wait"), "sum_vectors")
    parts = _unpack(vec, [(1, 1)] + [grads[k].shape for k in SMALL])
    loss_out = parts[0].reshape(())
    for k, g in zip(SMALL, parts[1:]):
        if k in SHARD_AXIS:
            ax = SHARD_AXIS[k]
            g = lax.dynamic_slice_in_dim(g, chip * w[k].shape[ax], w[k].shape[ax], axis=ax)
        g_out[k] = g

    for k in TRANSPOSED:
        results[k] = [a.transpose(0, 2, 1) for a in results[k]]
    g_out.update({k: results[k][0] for k in TRANSPOSED})
    delta, new_m, new_v = ({k: results[k][j] for k in BIG} for j in (1, 2, 3))
    shapes = [w[k].shape for k in SMALL]
    packed = [_pack([src[k] for k in SMALL], F32, 8) for src in (w, g_out, m, v)]
    outs = adamw(*packed, "adamw_small")
    for dst, o in zip((delta, new_m, new_v), outs):
        for k, a in zip(SMALL, _unpack(o, shapes)):
            dst[k] = a
    return ((loss_out, dx[None]) + tuple(g_out[k] for k in WEIGHTS) + tuple(delta[k] for k in WEIGHTS)
            + tuple(new_m[k] for k in WEIGHTS) + tuple(new_v[k] for k in WEIGHTS))


def kernel(x, p, norm_ffn1, ffn1_w_in, ffn1_w_out, norm_mix, norm_ffn2, ffn2_w_in, ffn2_w_out, ple_norm, ple_gate_w, ple_proj_w, hyb_w_in, conv_dw_w, conv_dw_b, conv_ln_g, conv_ln_b, ssm_conv_w, ssm_conv_b, ssm_dt_bias, ssm_a_log, ssm_d, ssm_norm, hyb_w_out, att_w_qkv, att_b_qkv, att_sinks, att_w_o, att_b_o, final_norm, loss_target, m_norm_ffn1, m_ffn1_w_in, m_ffn1_w_out, m_norm_mix, m_norm_ffn2, m_ffn2_w_in, m_ffn2_w_out, m_ple_norm, m_ple_gate_w, m_ple_proj_w, m_hyb_w_in, m_conv_dw_w, m_conv_dw_b, m_conv_ln_g, m_conv_ln_b, m_ssm_conv_w, m_ssm_conv_b, m_ssm_dt_bias, m_ssm_a_log, m_ssm_d, m_ssm_norm, m_hyb_w_out, m_att_w_qkv, m_att_b_qkv, m_att_sinks, m_att_w_o, m_att_b_o, m_final_norm, v_norm_ffn1, v_ffn1_w_in, v_ffn1_w_out, v_norm_mix, v_norm_ffn2, v_ffn2_w_in, v_ffn2_w_out, v_ple_norm, v_ple_gate_w, v_ple_proj_w, v_hyb_w_in, v_conv_dw_w, v_conv_dw_b, v_conv_ln_g, v_conv_ln_b, v_ssm_conv_w, v_ssm_conv_b, v_ssm_dt_bias, v_ssm_a_log, v_ssm_d, v_ssm_norm, v_hyb_w_out, v_att_w_qkv, v_att_b_qkv, v_att_sinks, v_att_w_o, v_att_b_o, v_final_norm):
    given = locals()
    w = {k: given[k] for k in WEIGHTS}
    m = {k: given["m_" + k] for k in WEIGHTS}
    v = {k: given["v_" + k] for k in WEIGHTS}
    return _step(x, p, loss_target, w, m, v)
```

```python
import functools

import numpy as np
import jax
import jax.numpy as jnp
from jax import lax
from jax.experimental import pallas as pl
from jax.experimental.pallas import tpu as pltpu

F32 = jnp.float32
BF16 = jnp.bfloat16
MXU_DTYPE = jnp.bfloat16
S = jax.ShapeDtypeStruct
MESH = pl.DeviceIdType.MESH

VMEM_LIMIT = 48 * 2**20
LANE = 128

EPS = 1e-6
SSM_HEADS = 16
HEAD_DIM = 64
SSM_GROUPS = 2
SSM_STATE = 128
SSM_CONV = 4
CHUNK = 128
CONV_WIDTH = 31
ATT_HEADS = 16
ATT_KV_HEADS = 4
WINDOW = 128
ROPE_THETA = 10000.0
ADAM_LR = 0.001
ADAM_B1 = 0.9
ADAM_B2 = 0.999
ADAM_EPS = 1e-08
ADAM_WD = 0.01
ADAM_STEP = 10

N_CHIPS = 4
N_DEV = 8

NN = ((1,), (0,))
NT = ((1,), (1,))
TN = ((0,), (0,))


def _mm(a, b, dims=NN):
    return lax.dot_general(a.astype(MXU_DTYPE), b.astype(MXU_DTYPE), (dims, ((), ())), preferred_element_type=F32)


def _split3(a):
    hi = a.astype(BF16)
    r = a - hi.astype(F32)
    mid = r.astype(BF16)
    lo = (r - mid.astype(F32)).astype(BF16)
    return hi, mid, lo


def _mm01(a, onehot, dims=NN):
    o = onehot.astype(BF16)
    out = None
    for part in _split3(a):
        t = lax.dot_general(part, o, (dims, ((), ())), preferred_element_type=F32)
        out = t if out is None else out + t
    return out


def _01mm(onehot, a):
    o = onehot.astype(BF16)
    out = None
    for part in _split3(a):
        t = lax.dot_general(o, part, (NN, ((), ())), preferred_element_type=F32)
        out = t if out is None else out + t
    return out


def _sigmoid(x):
    return 0.5 * jnp.tanh(0.5 * x) + 0.5


def _softplus(x):
    return jnp.maximum(x, 0.0) + jnp.log(1.0 + jnp.exp(-jnp.abs(x)))


def _iota(shape, axis):
    return lax.broadcasted_iota(jnp.int32, shape, axis)


def _head_indicator(width, heads, transposed=False):
    per = width // heads
    if transposed:
        return (_iota((heads, width), 1) // per == _iota((heads, width), 0)).astype(F32)
    return (_iota((width, heads), 0) // per == _iota((width, heads), 1)).astype(F32)


def _acc(ref, i, val):
    @pl.when(i == 0)
    def _():
        ref[...] = val

    @pl.when(i > 0)
    def _():
        ref[...] += val


def _rs(tile, width, col=0, shift=0, n=None):
    if shift == 0:
        return pl.BlockSpec((tile, width), lambda i: (i, col))
    if shift < 0:
        return pl.BlockSpec((tile, width), lambda i: (jnp.maximum(i - 1, 0), col))
    return pl.BlockSpec((tile, width), lambda i: (jnp.minimum(i + 1, n - 1), col))


def _ps(shape):
    return pl.BlockSpec(shape, lambda i: (0,) * len(shape))


def _call(body, name, grid, in_specs, out_specs, out_shape, scratch=(), sem=None):
    return pl.pallas_call(
        body, name=name, grid=grid, in_specs=in_specs, out_specs=out_specs, out_shape=out_shape,
        scratch_shapes=list(scratch),
        compiler_params=pltpu.CompilerParams(dimension_semantics=sem, vmem_limit_bytes=VMEM_LIMIT))


def _row_tile(n, target):
    t = min(n, target)
    assert n % t == 0, (n, t)
    return t


def _pick_tile(dim, target):
    if dim <= target:
        return dim
    t = (int(1.4 * target) // LANE) * LANE
    while t >= LANE:
        if dim % t == 0:
            return t
        t -= LANE
    return dim


ANY = pl.BlockSpec(memory_space=pl.ANY)


def _paired(j):
    return (j % 2) * 2 + j // 2


class ColSharded:
    def __init__(self, arr, paired=False):
        self.arr, self.paired = arr, paired
        self.nch, self.rows, self.per = arr.shape
        self.shape = (self.rows, self.nch * self.per)

    def chip(self, j):
        return _paired(j) if self.paired else j


class Slot:
    def __init__(self, buf, kind, per, off, c0=0, paired=False):
        self.buf, self.kind, self.per, self.off, self.c0, self.paired = buf, kind, per, off, c0, paired

    def chip(self, j):
        return _paired(j) if self.paired else j


def matmul(a, b, mode, name, *, out_dtype=F32, scale=None, res=None, bias=None, into=None, tm=1024, tn=1024, tk=1024):
    bshape = b.shape
    if mode == "nn":
        (m, k), (k2, n) = a.shape, bshape
    elif mode == "nt":
        (m, k), (n, k2) = a.shape, bshape
    else:
        (k, m), (k2, n) = a.shape, bshape
    assert k == k2, (a.shape, bshape, mode)
    if mode == "tn":
        tk = 2 * tk
    tm, tn, tk = _pick_tile(m, tm), _pick_tile(n, tn), _pick_tile(k, tk)
    if isinstance(b, ColSharded):
        if mode == "nn":
            tn = b.per
        else:
            assert mode == "nt"
            tk = b.per
    if into is not None:
        if into.kind == "c":
            tn = into.per
            assert into.off % tm == 0 and n == N_CHIPS * into.per
        else:
            tm = max(1, min(m, int(1.4 * 1024)) // into.per) * into.per
            assert m % tm == 0 and into.off % into.per == 0 and into.c0 % (tm // into.per) == 0
    nk = k // tk
    dims = {"nn": NN, "nt": NT, "tn": TN}[mode]
    a_spec = (pl.BlockSpec((tk, tm), lambda i, j, kk: (kk, i)) if mode == "tn"
              else pl.BlockSpec((tm, tk), lambda i, j, kk: (i, kk)))
    if isinstance(b, ColSharded):
        bchip = b.chip
        b_spec = (pl.BlockSpec((None, tk, tn), lambda i, j, kk: (bchip(j), kk, 0)) if mode == "nn"
                  else pl.BlockSpec((None, tn, tk), lambda i, j, kk: (bchip(kk), j, 0)))
        b = b.arr
    else:
        b_spec = (pl.BlockSpec((tn, tk), lambda i, j, kk: (j, kk)) if mode == "nt"
                  else pl.BlockSpec((tk, tn), lambda i, j, kk: (kk, j)))
    plain_o = pl.BlockSpec((tm, tn), lambda i, j, kk: (i, j))
    ins, in_specs = [a, b], [a_spec, b_spec]
    if bias is not None:
        ins.append(bias)
        in_specs.append(pl.BlockSpec((1, tn), lambda i, j, kk: (0, j)))
    if res is not None:
        ins.append(res)
        in_specs.append(plain_o)
    aliases = {}
    if into is None:
        o_spec, o_shape = plain_o, S((m, n), out_dtype)
    else:
        aliases = {len(ins): 0}
        ins.append(into.buf)
        in_specs.append(ANY)
        o_shape = S(into.buf.shape, into.buf.dtype)
        if into.kind == "c":
            ob, ochip = into.off // tm, into.chip
            o_spec = pl.BlockSpec((None, tm, tn), lambda i, j, kk: (ochip(j), ob + i, 0))
        else:
            q, ob = tm // into.per, into.off // into.per
            cb = into.c0 // q
            o_spec = pl.BlockSpec((q, into.per, tn), lambda i, j, kk: (cb + i, ob, j))

    def body(*refs):
        a_ref, b_ref = refs[0], refs[1]
        o_ref, acc_ref = refs[-2], refs[-1]
        kk = pl.program_id(2)

        @pl.when(kk == 0)
        def _():
            acc_ref[...] = jnp.zeros_like(acc_ref)

        acc_ref[...] += _mm(a_ref[...], b_ref[...], dims)

        @pl.when(kk == nk - 1)
        def _():
            out = acc_ref[...]
            if scale is not None:
                out = out * scale
            pos = 2
            if bias is not None:
                out = out + refs[pos][...]
                pos += 1
            if res is not None:
                out = out + refs[pos][...]
            o_ref[...] = out.astype(o_ref.dtype).reshape(o_ref.shape)

    return pl.pallas_call(
        body, name=name, grid=(m // tm, n // tn, nk), in_specs=in_specs, out_specs=o_spec, out_shape=o_shape,
        scratch_shapes=[pltpu.VMEM((tm, tn), F32)], input_output_aliases=aliases,
        compiler_params=pltpu.CompilerParams(dimension_semantics=("parallel", "parallel", "arbitrary"),
                                             vmem_limit_bytes=VMEM_LIMIT))(*ins)


def rms_matmul(h, g, b, mode, name, bias=None):
    n, d = h.shape
    sharded = isinstance(b, ColSharded)
    n_out = b.shape[1] if mode == "nn" else b.shape[0]
    tm = _row_tile(n, 1024)
    tn = b.per if sharded else _pick_tile(n_out, 1024)
    if sharded:
        assert mode == "nn"
        bchip = b.chip
        b_spec = pl.BlockSpec((None, d, tn), lambda i, j: (bchip(j), 0, 0))
        b = b.arr
    elif mode == "nn":
        b_spec = pl.BlockSpec((d, tn), lambda i, j: (0, j))
    else:
        b_spec = pl.BlockSpec((tn, d), lambda i, j: (j, 0))
    ins = [h, g, b] + ([bias] if bias is not None else [])
    in_specs = [pl.BlockSpec((tm, d), lambda i, j: (i, 0)), pl.BlockSpec((1, d), lambda i, j: (0, 0)), b_spec]
    if bias is not None:
        in_specs.append(pl.BlockSpec((1, tn), lambda i, j: (0, j)))

    def body(h_ref, g_ref, b_ref, *refs):
        o_ref, xn_ref = refs[-2:]
        x = h_ref[...]
        r = lax.rsqrt(jnp.mean(x * x, axis=-1, keepdims=True) + EPS)
        xn = (x * r * g_ref[...]).astype(xn_ref.dtype)

        @pl.when(pl.program_id(1) == 0)
        def _():
            xn_ref[...] = xn

        out = _mm(xn, b_ref[...], NN if mode == "nn" else NT)
        o_ref[...] = out if bias is None else out + refs[0][...]

    return pl.pallas_call(
        body, name=name, grid=(n // tm, n_out // tn), in_specs=in_specs,
        out_specs=[pl.BlockSpec((tm, tn), lambda i, j: (i, j)), pl.BlockSpec((tm, d), lambda i, j: (i, 0))],
        out_shape=[S((n, n_out), F32), S((n, d), MXU_DTYPE)],
        compiler_params=pltpu.CompilerParams(dimension_semantics=("parallel", "arbitrary"), vmem_limit_bytes=VMEM_LIMIT),
    )(*ins)


def _rms_bwd_math(x, g, dy):
    r = lax.rsqrt(jnp.mean(x * x, axis=-1, keepdims=True) + EPS)
    xh = x * r
    dg = jnp.sum(dy * xh, axis=0, keepdims=True)
    dxh = dy * g
    dx = r * (dxh - xh * jnp.mean(dxh * xh, axis=-1, keepdims=True))
    return dx, dg


def nt_rms_bwd(a, b, h, g, dh_in, name, extra=None, colsum=False, b_kd=False):
    n, k = a.shape
    d = h.shape[1]
    tm = _row_tile(n, 1024)
    sharded = isinstance(b, ColSharded)
    tk = b.per if sharded else _pick_tile(k, 1024)
    nk = k // tk
    dims = NN if b_kd else NT
    if sharded:
        bchip = b.chip
        b_spec = pl.BlockSpec((None, d, tk), lambda i, kk: (bchip(kk), 0, 0))
        b = b.arr
    elif b_kd:
        b_spec = pl.BlockSpec((tk, d), lambda i, kk: (kk, 0))
    else:
        b_spec = pl.BlockSpec((d, tk), lambda i, kk: (0, kk))
    row = pl.BlockSpec((tm, d), lambda i, kk: (i, 0))
    vec = pl.BlockSpec((1, d), lambda i, kk: (0, 0))
    ins, in_specs = [a, b, h, g, dh_in], [pl.BlockSpec((tm, tk), lambda i, kk: (i, kk)), b_spec, row, vec, row]
    if extra is not None:
        k2 = extra[0].shape[1]
        ins += list(extra)
        in_specs += [pl.BlockSpec((tm, k2), lambda i, kk: (i, 0)),
                     pl.BlockSpec((k2, d) if b_kd else (d, k2), lambda i, kk: (0, 0))]
    n_in = len(ins)

    def body(*refs):
        a_ref, b_ref, h_ref, g_ref, dh_ref = refs[:5]
        o_ref, dg_ref = refs[n_in], refs[n_in + 1]
        acc_ref = refs[-1]
        i, kk = pl.program_id(0), pl.program_id(1)

        @pl.when(kk == 0)
        def _():
            acc_ref[...] = _mm(refs[5][...], refs[6][...], dims) if extra is not None else jnp.zeros_like(acc_ref)

        acc_ref[...] += _mm(a_ref[...], b_ref[...], dims)

        @pl.when(kk == nk - 1)
        def _():
            dx, dg = _rms_bwd_math(h_ref[...], g_ref[...], acc_ref[...])
            out = dh_ref[...] + dx
            o_ref[...] = out
            _acc(dg_ref, i, dg)
            if colsum:
                _acc(refs[n_in + 2], i, jnp.sum(out, axis=0, keepdims=True))

    n_vec = 2 if colsum else 1
    return pl.pallas_call(
        body, name=name, grid=(n // tm, nk), in_specs=in_specs, out_specs=[row] + [vec] * n_vec,
        out_shape=[S((n, d), F32)] + [S((1, d), F32)] * n_vec, scratch_shapes=[pltpu.VMEM((tm, d), F32)],
        compiler_params=pltpu.CompilerParams(dimension_semantics=("arbitrary", "arbitrary"), vmem_limit_bytes=VMEM_LIMIT),
    )(*ins)


def swiglu_in(h, g, w_in, name):
    n, d = h.shape
    per = w_in.per
    nj = w_in.nch // 2
    tile = _row_tile(n, 1024)

    def body(h_ref, g_ref, wg_ref, wu_ref, xn_ref, u_ref, hm_ref):
        x = h_ref[...]
        r = lax.rsqrt(jnp.mean(x * x, axis=-1, keepdims=True) + EPS)
        xn = (x * r * g_ref[...]).astype(xn_ref.dtype)

        @pl.when(pl.program_id(1) == 0)
        def _():
            xn_ref[...] = xn

        a = _mm(xn, wg_ref[...])
        b = _mm(xn, wu_ref[...])
        u_ref[:, :per] = a.astype(u_ref.dtype)
        u_ref[:, per:] = b.astype(u_ref.dtype)
        hm_ref[...] = (a * _sigmoid(a) * b).astype(hm_ref.dtype)

    return pl.pallas_call(
        body, name=name, grid=(n // tile, nj),
        in_specs=[pl.BlockSpec((tile, d), lambda i, j: (i, 0)), pl.BlockSpec((1, d), lambda i, j: (0, 0)),
                  pl.BlockSpec((None, d, per), lambda i, j: (j, 0, 0)), pl.BlockSpec((None, d, per), lambda i, j: (nj + j, 0, 0))],
        out_specs=[pl.BlockSpec((tile, d), lambda i, j: (i, 0)), pl.BlockSpec((tile, 2 * per), lambda i, j: (i, j)),
                   pl.BlockSpec((tile, per), lambda i, j: (i, j))],
        out_shape=[S((n, d), MXU_DTYPE), S((n, 2 * nj * per), MXU_DTYPE), S((n, nj * per), MXU_DTYPE)],
        compiler_params=pltpu.CompilerParams(dimension_semantics=("parallel", "arbitrary"), vmem_limit_bytes=VMEM_LIMIT),
    )(h, g, w_in.arr, w_in.arr)


def swiglu_out_bwd(dh, w_out, u, after, name):
    n, d = dh.shape
    f = w_out.shape[0]
    per = u.shape[1] // 4
    nj = f // per
    tile = _row_tile(n, 1024)

    def body(dh_ref, w_ref, u_ref, after_ref, du_ref):
        dm = 0.5 * _mm(dh_ref[...], w_ref[...], NT)
        a = u_ref[:, :per].astype(F32)
        b = u_ref[:, per:].astype(F32)
        s = _sigmoid(a)
        du_ref[:, :per] = (dm * b * s * (1.0 + a * (1.0 - s))).astype(du_ref.dtype)
        du_ref[:, per:] = (dm * a * s).astype(du_ref.dtype)

    return pl.pallas_call(
        body, name=name, grid=(n // tile, nj),
        in_specs=[pl.BlockSpec((tile, d), lambda i, j: (i, 0)), pl.BlockSpec((per, d), lambda i, j: (j, 0)),
                  pl.BlockSpec((tile, 2 * per), lambda i, j: (i, j)), ANY],
        out_specs=pl.BlockSpec((tile, 2 * per), lambda i, j: (i, j)),
        out_shape=S(u.shape, MXU_DTYPE),
        compiler_params=pltpu.CompilerParams(dimension_semantics=("parallel", "parallel"), vmem_limit_bytes=VMEM_LIMIT),
    )(dh, w_out, u, after)


def ple_fwd(h, g, w_gate, pp, name):
    n, d = h.shape
    tile = _row_tile(n, 512)

    def body(h_ref, g_ref, w_ref, pp_ref, o_ref, gl_ref, xn_ref):
        x = h_ref[...]
        r = lax.rsqrt(jnp.mean(x * x, axis=-1, keepdims=True) + EPS)
        xn = (x * r * g_ref[...]).astype(xn_ref.dtype)
        xn_ref[...] = xn
        gl = _mm(xn, w_ref[...])
        gl_ref[...] = gl
        o_ref[...] = x + _sigmoid(gl) * pp_ref[...]

    return _call(body, name, (n // tile,), [_rs(tile, d), _ps((1, d)), _ps(w_gate.shape), _rs(tile, d)],
                 [_rs(tile, d)] * 3, [S((n, d), F32), S((n, d), F32), S((n, d), MXU_DTYPE)], sem=("parallel",))(h, g, w_gate, pp)


def ple_bwd(dh, gl, pp, w_gate, h, g, after, name):
    n, d = dh.shape
    tile = _row_tile(n, 512)

    def body(dh_ref, gl_ref, pp_ref, w_ref, h_ref, g_ref, after_ref, o_ref, dpp_ref, dgl_ref, dg_ref):
        i = pl.program_id(0)
        s = _sigmoid(gl_ref[...])
        dh_ = dh_ref[...]
        dpp_ref[...] = (dh_ * s).astype(dpp_ref.dtype)
        dgl = (dh_ * pp_ref[...] * s * (1.0 - s)).astype(dgl_ref.dtype)
        dgl_ref[...] = dgl
        dx, dg = _rms_bwd_math(h_ref[...], g_ref[...], _mm(dgl, w_ref[...], NT))
        o_ref[...] = dh_ + dx
        _acc(dg_ref, i, dg)

    return _call(body, name, (n // tile,),
                 [_rs(tile, d)] * 3 + [_ps(w_gate.shape), _rs(tile, d), _ps((1, d)), ANY],
                 [_rs(tile, d)] * 3 + [_ps((1, d))],
                 [S((n, d), F32), S((n, d), MXU_DTYPE), S((n, d), MXU_DTYPE), S((1, d), F32)],
                 sem=("arbitrary",))(dh, gl, pp, w_gate, h, g, after)


def loss_head(h, g, target, name):
    n, d = h.shape
    tile = _row_tile(n, 512)

    def body(h_ref, g_ref, t_ref, dh_ref, dg_ref, loss_ref):
        i = pl.program_id(0)
        x = h_ref[...]
        gg = g_ref[...]
        r = lax.rsqrt(jnp.mean(x * x, axis=-1, keepdims=True) + EPS)
        err = x * r * gg - t_ref[...]
        part = 0.5 * jnp.sum(jnp.mean(err * err, axis=-1, keepdims=True), axis=0, keepdims=True)
        dx, dg = _rms_bwd_math(x, gg, err * (1.0 / d))
        dh_ref[...] = dx
        _acc(dg_ref, i, dg)
        _acc(loss_ref, i, jnp.broadcast_to(part, (8, LANE)))

    return _call(body, name, (n // tile,), [_rs(tile, d), _ps((1, d)), _rs(tile, d)],
                 [_rs(tile, d), _ps((1, d)), _ps((8, LANE))], [S((n, d), F32), S((1, d), F32), S((8, LANE), F32)],
                 sem=("arbitrary",))(h, g, target)


def _adamw_math(w, g, m, v):
    c1 = np.float32(1.0 - ADAM_B1 ** ADAM_STEP)
    c2 = np.float32(1.0 - ADAM_B2 ** ADAM_STEP)
    mm = ADAM_B1 * m + (1.0 - ADAM_B1) * g
    vv = ADAM_B2 * v + (1.0 - ADAM_B2) * (g * g)
    return -ADAM_LR * ((mm / c1) / (jnp.sqrt(vv / c2) + ADAM_EPS) + ADAM_WD * w), mm, vv


def adamw_layer(w, pack, off, m, v, li, prev, name):
    n, r, c = w.shape

    def body(w_ref, g_ref, m_ref, v_ref, *refs):
        go_ref, d_ref, mo_ref, vo_ref = refs[-4:]
        g = g_ref[...]
        go_ref[...] = g
        d_ref[...], mo_ref[...], vo_ref[...] = _adamw_math(w_ref[...], g, m_ref[...], v_ref[...])

    if r % 8 == 0:
        cap = 2**21 // (4 * c) // 8 * 8
        tile = next(t for t in range(min(cap, r), 7, -8) if r % t == 0 and off % t == 0)
        ob, steps = off // tile, r // tile
        blk = pl.BlockSpec((None, tile, c), lambda i: (li, i, 0))
        g_spec = pl.BlockSpec((tile, c), lambda i: (ob + i, 0))
    else:
        assert off == 0 and pack.shape[0] == r and c % (2 * LANE) == 0
        steps = c // (2 * LANE)
        blk = pl.BlockSpec((None, r, 2 * LANE), lambda i: (li, 0, i))
        g_spec = pl.BlockSpec((r, 2 * LANE), lambda i: (0, i))
    prev = list(prev) if prev is not None else []
    return pl.pallas_call(
        body, name=name, grid=(steps,),
        in_specs=[blk, g_spec, blk, blk] + [ANY] * len(prev),
        out_specs=[blk] * 4, out_shape=[S((n, r, c), F32)] * 4,
        input_output_aliases={4 + j: j for j in range(len(prev))},
        compiler_params=pltpu.CompilerParams(dimension_semantics=("parallel",), vmem_limit_bytes=VMEM_LIMIT),
    )(w, pack, m, v, *prev)


def adamw(w, g, m, v, name):
    r, c = w.shape
    tile = r
    for t in (512, 256, 128, 64, 32, 16, 8):
        if r % t == 0 and t * c * 4 <= 2**21:
            tile = t
            break

    def body(w_ref, g_ref, m_ref, v_ref, d_ref, mo_ref, vo_ref):
        d_ref[...], mo_ref[...], vo_ref[...] = _adamw_math(w_ref[...], g_ref[...], m_ref[...], v_ref[...])

    return _call(body, name, (r // tile,), [_rs(tile, c)] * 4, [_rs(tile, c)] * 3, [S((r, c), F32)] * 3,
                 sem=("parallel",))(w, g, m, v)


TAP_VREGS = 32


def _taps(src, w_ref, offsets, tile, put, bias=None):
    c = src.shape[1]
    rp = max(8, TAP_VREGS * 8 * LANE // c // 8 * 8)
    for r0 in range(0, tile, rp):
        acc = jnp.zeros((rp, c), F32) if bias is None else jnp.zeros((rp, c), F32) + bias
        for k, o in enumerate(offsets):
            acc = acc + w_ref[k:k + 1, :] * src[r0 + o:r0 + o + rp, :]
        put(slice(r0, r0 + rp), acc)


def _taps_fwd(sc, w_ref, width, halo, tile, put, bias):
    _taps(sc, w_ref, [halo - (width - 1) + k for k in range(width)], tile, put, bias)


def _taps_bwd_x(sc_d, w_ref, width, tile, put):
    _taps(sc_d, w_ref, [(width - 1) - k for k in range(width)], tile, put)


def _taps_bwd_w(dy, sc, dw_ref, width, halo, tile, i):
    @pl.when(i == 0)
    def _():
        dw_ref[...] = jnp.zeros_like(dw_ref)

    for k in range(width):
        o = halo - (width - 1) + k
        dw_ref[k:k + 1, :] += jnp.sum(dy * sc[o:o + tile, :], axis=0, keepdims=True)


def _ln_stats(x):
    mu = jnp.mean(x, axis=-1, keepdims=True)
    xc = x - mu
    r = lax.rsqrt(jnp.mean(xc * xc, axis=-1, keepdims=True) + EPS)
    return xc * r, r


def conv_group_fwd(proj, cw, cb, lg, lb, name):
    n = proj.shape[0]
    d = cw.shape[1]
    tile = _row_tile(n, 256)
    halo = 32

    def body(v_ref, g_ref, vp_ref, gp_ref, cw_ref, cb_ref, lg_ref, lb_ref, u_ref, u1_ref, sc):
        i = pl.program_id(0)
        first = (i > 0).astype(F32)
        sc[0:halo, :] = vp_ref[tile - halo:, :] * _sigmoid(gp_ref[tile - halo:, :]) * first
        sc[halo:, :] = v_ref[...] * _sigmoid(g_ref[...])
        def put(rows, acc):
            u1_ref[rows, :] = acc

        _taps_fwd(sc, cw_ref, CONV_WIDTH, halo, tile, put, cb_ref[...])
        xh, _ = _ln_stats(u1_ref[...])
        y = xh * lg_ref[...] + lb_ref[...]
        u_ref[...] = (y * _sigmoid(y)).astype(u_ref.dtype)

    return _call(body, name, (n // tile,),
                 [_rs(tile, d, 0), _rs(tile, d, 1), _rs(tile, d, 0, -1), _rs(tile, d, 1, -1),
                  _ps(cw.shape), _ps((1, d)), _ps((1, d)), _ps((1, d))],
                 [_rs(tile, d), _rs(tile, d)], [S((n, d), MXU_DTYPE), S((n, d), F32)],
                 scratch=[pltpu.VMEM((halo + tile, d), F32)], sem=("arbitrary",))(proj, proj, proj, proj, cw, cb, lg, lb)


def conv_group_bwd(du, u1, proj, cw, lg, lb, name):
    n = proj.shape[0]
    d = cw.shape[1]
    tile = _row_tile(n, 256)
    halo = 32
    nt = n // tile

    def body(du_ref, dun_ref, u1_ref, u1n_ref, v_ref, g_ref, vp_ref, gp_ref, cw_ref, lg_ref, lb_ref,
             dp_ref, dcw_ref, dcb_ref, dlg_ref, dlb_ref, sc, sc_d):
        i = pl.program_id(0)

        def ln_swish_bwd(dy_, u1_):
            xh, r = _ln_stats(u1_)
            y = xh * lg_ref[...] + lb_ref[...]
            s = _sigmoid(y)
            dyy = dy_ * s * (1.0 + y * (1.0 - s))
            dxh = dyy * lg_ref[...]
            dx = r * (dxh - jnp.mean(dxh, axis=-1, keepdims=True) - xh * jnp.mean(dxh * xh, axis=-1, keepdims=True))
            return dx, jnp.sum(dyy * xh, axis=0, keepdims=True), jnp.sum(dyy, axis=0, keepdims=True)

        du1, dlg, dlb = ln_swish_bwd(du_ref[...].astype(F32), u1_ref[...])
        du1n, _, _ = ln_swish_bwd(dun_ref[0:halo, :].astype(F32), u1n_ref[0:halo, :])
        sc_d[0:tile, :] = du1
        sc_d[tile:, :] = du1n * (i < nt - 1).astype(F32)
        sc[0:halo, :] = vp_ref[tile - halo:, :] * _sigmoid(gp_ref[tile - halo:, :]) * (i > 0).astype(F32)
        sc[halo:, :] = v_ref[...] * _sigmoid(g_ref[...])

        def put(rows, du0):
            sig = _sigmoid(g_ref[rows, :])
            dp_ref[rows, :d] = (du0 * sig).astype(dp_ref.dtype)
            dp_ref[rows, d:] = (du0 * v_ref[rows, :] * sig * (1.0 - sig)).astype(dp_ref.dtype)

        _taps_bwd_x(sc_d, cw_ref, CONV_WIDTH, tile, put)
        _taps_bwd_w(du1, sc, dcw_ref, CONV_WIDTH, halo, tile, i)
        _acc(dcb_ref, i, jnp.sum(du1, axis=0, keepdims=True))
        _acc(dlg_ref, i, dlg)
        _acc(dlb_ref, i, dlb)

    return _call(body, name, (nt,),
                 [_rs(tile, d), _rs(tile, d, 0, 1, nt), _rs(tile, d), _rs(tile, d, 0, 1, nt),
                  _rs(tile, d, 0), _rs(tile, d, 1), _rs(tile, d, 0, -1), _rs(tile, d, 1, -1),
                  _ps(cw.shape), _ps((1, d)), _ps((1, d))],
                 [_rs(tile, 2 * d), _ps(cw.shape), _ps((1, d)), _ps((1, d)), _ps((1, d))],
                 [S((n, proj.shape[1]), MXU_DTYPE), S(cw.shape, F32), S((1, d), F32), S((1, d), F32), S((1, d), F32)],
                 scratch=[pltpu.VMEM((halo + tile, d), F32), pltpu.VMEM((tile + halo, d), F32)],
                 sem=("arbitrary",))(du, du, u1, u1, proj, proj, proj, proj, cw, lg, lb)


def ssm_conv_fwd(proj, dtr, sw, sb, dtb, name):
    n = proj.shape[0]
    w = sw.shape[1]
    inner = SSM_HEADS * HEAD_DIM
    tile = _row_tile(n, 256)
    halo = 8

    def body(x_ref, xp_ref, dtr_ref, sw_ref, sb_ref, dtb_ref, pre_ref, xs_ref, bc_ref, dt_ref, sc):
        i = pl.program_id(0)
        sc[0:halo, :] = xp_ref[tile - halo:, :] * (i > 0).astype(F32)
        sc[halo:, :] = x_ref[...]
        def put(rows, acc):
            pre_ref[rows, :] = acc

        _taps_fwd(sc, sw_ref, SSM_CONV, halo, tile, put, sb_ref[...])
        pre = pre_ref[...]
        act = pre * _sigmoid(pre)
        xs_ref[...] = act[:, :inner]
        bc_ref[...] = act[:, inner:]
        dt = _softplus(dtr_ref[...] + dtb_ref[...])
        dt_ref[...] = jnp.where(_iota(dt.shape, 1) < SSM_HEADS, dt, 0.0)

    return _call(body, name, (n // tile,),
                 [_rs(tile, w, 2), _rs(tile, w, 2, -1), _rs(tile, LANE), _ps(sw.shape), _ps((1, w)), _ps((1, LANE))],
                 [_rs(tile, w), _rs(tile, inner), _rs(tile, w - inner), _rs(tile, LANE)],
                 [S((n, w), F32), S((n, inner), F32), S((n, w - inner), F32), S((n, LANE), F32)],
                 scratch=[pltpu.VMEM((halo + tile, w), F32)], sem=("arbitrary",))(proj, proj, dtr, sw, sb, dtb)


def ssm_conv_bwd(dxs, dbc, pre, proj, sw, dproj, name):
    n = proj.shape[0]
    w = sw.shape[1]
    inner = SSM_HEADS * HEAD_DIM
    tile = _row_tile(n, 256)
    halo = 8
    nt = n // tile

    def body(dxs_ref, dxsn_ref, dbc_ref, dbcn_ref, pre_ref, pren_ref, x_ref, xp_ref, sw_ref, dp_in_ref,
             dx_ref, dsw_ref, dsb_ref, sc, sc_d):
        i = pl.program_id(0)

        def silu_bwd(d_, p_):
            s = _sigmoid(p_)
            return d_ * s * (1.0 + p_ * (1.0 - s))

        sc_d[0:tile, :inner] = silu_bwd(dxs_ref[...], pre_ref[:, :inner])
        sc_d[0:tile, inner:] = silu_bwd(dbc_ref[...], pre_ref[:, inner:])
        last = (i < nt - 1).astype(F32)
        sc_d[tile:, :inner] = silu_bwd(dxsn_ref[0:halo, :], pren_ref[0:halo, :inner]) * last
        sc_d[tile:, inner:] = silu_bwd(dbcn_ref[0:halo, :], pren_ref[0:halo, inner:]) * last
        sc[0:halo, :] = xp_ref[tile - halo:, :] * (i > 0).astype(F32)
        sc[halo:, :] = x_ref[...]
        dpre = sc_d[0:tile, :]
        def put(rows, acc):
            dx_ref[rows, :] = acc.astype(dx_ref.dtype)

        _taps_bwd_x(sc_d, sw_ref, SSM_CONV, tile, put)
        _taps_bwd_w(dpre, sc, dsw_ref, SSM_CONV, halo, tile, i)
        _acc(dsb_ref, i, jnp.sum(dpre, axis=0, keepdims=True))

    return pl.pallas_call(
        body, name=name, grid=(nt,),
        in_specs=[_rs(tile, inner), _rs(tile, inner, 0, 1, nt), _rs(tile, w - inner), _rs(tile, w - inner, 0, 1, nt),
                  _rs(tile, w), _rs(tile, w, 0, 1, nt), _rs(tile, w, 2), _rs(tile, w, 2, -1), _ps(sw.shape), ANY],
        out_specs=[_rs(tile, w, 2), _ps(sw.shape), _ps((1, w))],
        out_shape=[S(dproj.shape, dproj.dtype), S(sw.shape, F32), S((1, w), F32)],
        scratch_shapes=[pltpu.VMEM((halo + tile, w), F32), pltpu.VMEM((tile + halo, w), F32)],
        input_output_aliases={9: 0},
        compiler_params=pltpu.CompilerParams(dimension_semantics=("arbitrary",), vmem_limit_bytes=VMEM_LIMIT),
    )(dxs, dxs, dbc, dbc, pre, pre, proj, proj, sw, dproj)


def _ssd_prologue(dt_ref, dtT_ref, al_ref, alc_ref):
    row = _iota((CHUNK, CHUNK), 0)
    col = _iota((CHUNK, CHUNK), 1)
    dt = dt_ref[:, :SSM_HEADS]
    a_row = -jnp.exp(al_ref[:, :SSM_HEADS])
    a_col = -jnp.exp(alc_ref[...])
    cs = _01mm((row >= col).astype(F32), dt * a_row)
    csT = _mm01(dtT_ref[...] * a_col, (row <= col).astype(F32))
    return dt, a_row, cs, csT, row, col


def _decay(cs, csT, h, row, col):
    lm = jnp.exp(jnp.where(row >= col, cs[:, h:h + 1] - csT[h:h + 1, :], -1e30))
    lmT = jnp.exp(jnp.where(col >= row, csT[h:h + 1, :] - cs[:, h:h + 1], -1e30))
    return lm, lmT


def ssd_fwd(xs, bc, dt, dtT, alog_row, alog_col, name):
    n, width = xs.shape
    nc = n // CHUNK
    gw = width // SSM_GROUPS
    hpg = SSM_HEADS // SSM_GROUPS
    ns = SSM_STATE

    def body(xs_ref, bc_ref, dt_ref, dtT_ref, al_ref, alc_ref, y_ref, hs_ref, h_sc):
        i = pl.program_id(0)

        @pl.when(i == 0)
        def _():
            h_sc[...] = jnp.zeros_like(h_sc)

        dt, a_row, cs, csT, row, col = _ssd_prologue(dt_ref, dtT_ref, al_ref, alc_ref)
        indT = _head_indicator(width, SSM_HEADS, transposed=True)
        dt_full = _mm01(dt, indT)
        e_full = jnp.exp(_mm01(cs, indT))
        dte_full = jnp.exp(_mm01(cs[CHUNK - 1:CHUNK, :] - cs, indT))
        xt = xs_ref[...] * dt_full
        hs_ref[0] = h_sc[...]
        lo = _iota((CHUNK, 2 * HEAD_DIM), 1) < HEAD_DIM
        for g in range(SSM_GROUPS):
            bg = bc_ref[:, g * ns:(g + 1) * ns]
            cg = bc_ref[:, (SSM_GROUPS + g) * ns:(SSM_GROUPS + g + 1) * ns]
            gm = _mm(cg, bg, NT)
            hg = h_sc[g * gw:(g + 1) * gw, :]
            yoff = e_full[:, g * gw:(g + 1) * gw] * _mm(cg, hg, NT)
            for pr in range(hpg // 2):
                h0 = g * hpg + 2 * pr
                c0 = h0 * HEAD_DIM
                xp = xt[:, c0:c0 + 2 * HEAD_DIM]
                m0 = gm * _decay(cs, csT, h0, row, col)[0]
                m1 = gm * _decay(cs, csT, h0 + 1, row, col)[0]
                yd = jnp.where(lo, _mm(m0, xp), _mm(m1, xp))
                y_ref[:, c0:c0 + 2 * HEAD_DIM] = yd + yoff[:, 2 * pr * HEAD_DIM:(2 * pr + 2) * HEAD_DIM]
            sg = _mm(xt[:, g * gw:(g + 1) * gw] * dte_full[:, g * gw:(g + 1) * gw], bg, TN)
            for hh in range(hpg):
                h = g * hpg + hh
                r0 = h * HEAD_DIM
                h_sc[r0:r0 + HEAD_DIM, :] = (h_sc[r0:r0 + HEAD_DIM, :] * jnp.exp(csT[h:h + 1, CHUNK - 1:CHUNK])
                                             + sg[hh * HEAD_DIM:(hh + 1) * HEAD_DIM, :])

    bcw = bc.shape[1]
    return _call(body, name, (nc,),
                 [_rs(CHUNK, width), _rs(CHUNK, bcw), _rs(CHUNK, LANE), pl.BlockSpec((SSM_HEADS, CHUNK), lambda i: (0, i)),
                  _ps((1, LANE)), _ps((SSM_HEADS, 1))],
                 [_rs(CHUNK, width), pl.BlockSpec((1, width, ns), lambda i: (i, 0, 0))],
                 [S((n, width), F32), S((nc, width, ns), F32)],
                 scratch=[pltpu.VMEM((width, ns), F32)], sem=("arbitrary",))(xs, bc, dt, dtT, alog_row, alog_col)


def ssd_bwd(xs, bc, dt, dtT, alog_row, alog_col, hs, dy, dxs_skip, name):
    n, width = xs.shape
    nc = n // CHUNK
    gw = width // SSM_GROUPS
    hpg = SSM_HEADS // SSM_GROUPS
    ns = SSM_STATE
    bcw = bc.shape[1]

    def body(xs_ref, bc_ref, dt_ref, dtT_ref, al_ref, alc_ref, hs_ref, dy_ref, skip_ref,
             dxs_ref, dbc_ref, ddtr_ref, dal_ref, ddtb_ref, dh_sc, dxt_sc):
        i = pl.program_id(0)

        @pl.when(i == 0)
        def _():
            dh_sc[...] = jnp.zeros_like(dh_sc)

        dt, a_row, cs, csT, row, col = _ssd_prologue(dt_ref, dtT_ref, al_ref, alc_ref)
        indT = _head_indicator(width, SSM_HEADS, transposed=True)
        ind = _head_indicator(width, SSM_HEADS)
        dt_full = _mm01(dt, indT)
        e_full = jnp.exp(_mm01(cs, indT))
        cs_last = cs[CHUNK - 1:CHUNK, :]
        dte = jnp.exp(cs_last - cs)
        dte_full = _mm01(dte, indT)
        xs_ = xs_ref[...]
        xt = xs_ * dt_full
        dy_ = dy_ref[...]
        hmat = hs_ref[0]
        ds = dh_sc[...]
        lo = _iota((CHUNK, 2 * HEAD_DIM), 1) < HEAD_DIM
        head_lane = _iota((1, SSM_HEADS), 1)
        dcs = jnp.zeros((CHUNK, SSM_HEADS), F32)
        ddte = jnp.zeros((CHUNK, SSM_HEADS), F32)
        for g in range(SSM_GROUPS):
            sl = slice(g * gw, (g + 1) * gw)
            bg = bc_ref[:, g * ns:(g + 1) * ns]
            cg = bc_ref[:, (SSM_GROUPS + g) * ns:(SSM_GROUPS + g + 1) * ns]
            gm = _mm(cg, bg, NT)
            gmT = _mm(bg, cg, NT)
            hg = hmat[sl, :]
            dsg = ds[sl, :]
            dyg = dy_[:, sl]
            xtg = xt[:, sl]
            yoff = e_full[:, sl] * _mm(cg, hg, NT)
            edy = e_full[:, sl] * dyg
            bds = _mm(bg, dsg, NT)
            dxt_g = dte_full[:, sl] * bds
            ddte = ddte + _mm01(xtg * bds, ind[sl, :])
            dcs = dcs + _mm01(dyg * yoff, ind[sl, :])
            db = _mm(xtg * dte_full[:, sl], dsg)
            dc = _mm(edy, hg)
            dhc = _mm(edy, cg, TN)
            dgs = jnp.zeros((CHUNK, CHUNK), F32)
            dgTs = jnp.zeros((CHUNK, CHUNK), F32)
            for pr in range(hpg // 2):
                h0 = g * hpg + 2 * pr
                c0 = 2 * pr * HEAD_DIM
                xp = xtg[:, c0:c0 + 2 * HEAD_DIM]
                dyp = dyg[:, c0:c0 + 2 * HEAD_DIM]
                rr = []
                for h, half in ((h0, lo), (h0 + 1, jnp.logical_not(lo))):
                    lm, lmT = _decay(cs, csT, h, row, col)
                    xm = jnp.where(half, xp, 0.0)
                    dm = _mm(dyp, xm, NT)
                    dmT = _mm(xm, dyp, NT)
                    mT = gmT * lmT
                    z = jnp.sum(dm * (gm * lm), axis=1, keepdims=True) - jnp.sum(dmT * mT, axis=1, keepdims=True)
                    dcs = dcs + z * (head_lane == h).astype(F32)
                    dgs = dgs + dm * lm
                    dgTs = dgTs + dmT * lmT
                    rr.append(_mm(mT, dyp))
                dxt_sc[:, g * gw + c0:g * gw + c0 + 2 * HEAD_DIM] = jnp.where(lo, rr[0], rr[1]) + dxt_g[:, c0:c0 + 2 * HEAD_DIM]
            dbc_ref[:, g * ns:(g + 1) * ns] = db + _mm(dgTs, cg)
            dbc_ref[:, (SSM_GROUPS + g) * ns:(SSM_GROUPS + g + 1) * ns] = dc + _mm(dgs, bg)
            for hh in range(hpg):
                h = g * hpg + hh
                r0 = h * HEAD_DIM
                dh_sc[r0:r0 + HEAD_DIM, :] = (dhc[hh * HEAD_DIM:(hh + 1) * HEAD_DIM, :]
                                              + jnp.exp(csT[h:h + 1, CHUNK - 1:CHUNK]) * ds[r0:r0 + HEAD_DIM, :])
        t = ddte * dte
        per_head = jnp.sum(jnp.sum(ds * hmat, axis=1, keepdims=True) * ind, axis=0, keepdims=True)
        last_add = jnp.sum(t, axis=0, keepdims=True) + jnp.exp(cs_last) * per_head
        dcs = dcs - t + jnp.where(_iota((CHUNK, SSM_HEADS), 0) == CHUNK - 1, last_add, 0.0)
        dadt = _01mm((row <= col).astype(F32), dcs)
        dxt = dxt_sc[...]
        ddt = dadt * a_row + _mm01(dxt * xs_, ind)
        dxs_ref[...] = dxt * dt_full + skip_ref[...]
        ddtr = ddt * (1.0 - jnp.exp(-dt))
        ddtr_ref[...] = jnp.zeros_like(ddtr_ref)
        ddtr_ref[:, :SSM_HEADS] = ddtr.astype(ddtr_ref.dtype)
        _acc(dal_ref, i, jnp.sum(dadt * dt, axis=0, keepdims=True) * a_row)
        _acc(ddtb_ref, i, jnp.sum(ddtr, axis=0, keepdims=True))

    rev = lambda i: (nc - 1 - i, 0)
    return _call(body, name, (nc,),
                 [pl.BlockSpec((CHUNK, width), rev), pl.BlockSpec((CHUNK, bcw), rev), pl.BlockSpec((CHUNK, LANE), rev),
                  pl.BlockSpec((SSM_HEADS, CHUNK), lambda i: (0, nc - 1 - i)), _ps((1, LANE)), _ps((SSM_HEADS, 1)),
                  pl.BlockSpec((1, width, ns), lambda i: (nc - 1 - i, 0, 0)), pl.BlockSpec((CHUNK, width), rev),
                  pl.BlockSpec((CHUNK, width), rev)],
                 [pl.BlockSpec((CHUNK, width), rev), pl.BlockSpec((CHUNK, bcw), rev), pl.BlockSpec((CHUNK, LANE), rev),
                  _ps((1, SSM_HEADS)), _ps((1, SSM_HEADS))],
                 [S((n, width), F32), S((n, bcw), F32), S((n, LANE), MXU_DTYPE), S((1, SSM_HEADS), F32), S((1, SSM_HEADS), F32)],
                 scratch=[pltpu.VMEM((width, ns), F32), pltpu.VMEM((CHUNK, width), F32)],
                 sem=("arbitrary",))(xs, bc, dt, dtT, alog_row, alog_col, hs, dy, dxs_skip)


def ssm_gate_fwd(yssd, xs, proj, dfull, gamma, name):
    n, d = yssd.shape
    tile = _row_tile(n, 512)
    gw = d // SSM_GROUPS

    def body(y_ref, xs_ref, z_ref, df_ref, gm_ref, o_ref):
        z = z_ref[...]
        y2 = (y_ref[...] + df_ref[...] * xs_ref[...]) * (z * _sigmoid(z))
        for g in range(SSM_GROUPS):
            yg = y2[:, g * gw:(g + 1) * gw]
            r = lax.rsqrt(jnp.mean(yg * yg, axis=-1, keepdims=True) + EPS)
            o_ref[:, g * gw:(g + 1) * gw] = (yg * r * gm_ref[:, g * gw:(g + 1) * gw]).astype(o_ref.dtype)

    return _call(body, name, (n // tile,), [_rs(tile, d), _rs(tile, d), _rs(tile, d, 2), _ps((1, d)), _ps((1, d))],
                 _rs(tile, d), S((n, d), MXU_DTYPE), sem=("parallel",))(yssd, xs, proj, dfull, gamma)


def ssm_gate_bwd(dy3, yssd, xs, proj, dfull, gamma, dproj, name):
    n, d = yssd.shape
    tile = _row_tile(n, 512)
    gw = d // SSM_GROUPS

    def body(dy_ref, y_ref, xs_ref, z_ref, df_ref, gm_ref, dp_in_ref, dys_ref, dxs_ref, dz_ref, dgm_ref, dd_ref):
        i = pl.program_id(0)
        z = z_ref[...]
        s = _sigmoid(z)
        xs_ = xs_ref[...]
        y1 = y_ref[...] + df_ref[...] * xs_
        y2 = y1 * (z * s)
        dy_ = dy_ref[...].astype(F32)
        dgm = []
        dy2 = []
        for g in range(SSM_GROUPS):
            sl = slice(g * gw, (g + 1) * gw)
            dxg, dgg = _rms_bwd_math(y2[:, sl], gm_ref[:, sl], dy_[:, sl])
            dy2.append(dxg)
            dgm.append(dgg)
        dy2 = jnp.concatenate(dy2, axis=1)
        dy1 = dy2 * (z * s)
        dys_ref[...] = dy1
        dxs_ref[...] = dy1 * df_ref[...]
        dz_ref[...] = (dy2 * y1 * s * (1.0 + z * (1.0 - s))).astype(dz_ref.dtype)
        _acc(dgm_ref, i, jnp.concatenate(dgm, axis=1))
        colsum = jnp.broadcast_to(jnp.sum(dy1 * xs_, axis=0, keepdims=True), (8, d))
        _acc(dd_ref, i, _mm01(colsum, _head_indicator(d, SSM_HEADS))[0:1, :])

    return pl.pallas_call(
        body, name=name, grid=(n // tile,),
        in_specs=[_rs(tile, d), _rs(tile, d), _rs(tile, d), _rs(tile, d, 2), _ps((1, d)), _ps((1, d)), ANY],
        out_specs=[_rs(tile, d), _rs(tile, d), _rs(tile, d, 2), _ps((1, d)), _ps((1, SSM_HEADS))],
        out_shape=[S((n, d), F32), S((n, d), F32), S(dproj.shape, dproj.dtype), S((1, d), F32), S((1, SSM_HEADS), F32)],
        input_output_aliases={6: 2},
        compiler_params=pltpu.CompilerParams(dimension_semantics=("arbitrary",), vmem_limit_bytes=VMEM_LIMIT),
    )(dy3, yssd, xs, proj, dfull, gamma, dproj)


def _rope128(x, cos, sin_signed):
    half = HEAD_DIM // 2
    lane = _iota(x.shape, 1)
    partner = jnp.where((lane % HEAD_DIM) < half, pltpu.roll(x, LANE - half, 1), pltpu.roll(x, half, 1))
    return x * cos + partner * sin_signed


def rope_fwd(qkv, cos, sin, name):
    n, w = qkv.shape
    qw = ATT_HEADS * HEAD_DIM
    kw = ATT_KV_HEADS * HEAD_DIM
    tile = _row_tile(n, 512)

    def body(x_ref, c_ref, s_ref, q_ref, k_ref, v_ref):
        c, s = c_ref[...], s_ref[...]
        for j in range(qw // LANE):
            q_ref[:, j * LANE:(j + 1) * LANE] = _rope128(x_ref[:, j * LANE:(j + 1) * LANE], c, s).astype(q_ref.dtype)
        for j in range(kw // LANE):
            k_ref[:, j * LANE:(j + 1) * LANE] = _rope128(x_ref[:, qw + j * LANE:qw + (j + 1) * LANE], c, s).astype(k_ref.dtype)
        v_ref[...] = x_ref[:, qw + kw:].astype(v_ref.dtype)

    return _call(body, name, (n // tile,), [_rs(tile, w), _rs(tile, LANE), _rs(tile, LANE)],
                 [_rs(tile, qw), _rs(tile, kw), _rs(tile, kw)],
                 [S((n, qw), MXU_DTYPE), S((n, kw), MXU_DTYPE), S((n, kw), MXU_DTYPE)], sem=("parallel",))(qkv, cos, sin)


ATT_GROUP = ATT_HEADS // ATT_KV_HEADS


def _attn_mask(i):
    row = _iota((ATT_GROUP * WINDOW, 2 * WINDOW), 0) % WINDOW
    s = _iota((ATT_GROUP * WINDOW, 2 * WINDOW), 1)
    return (s > row) & (s <= row + WINDOW) & ((s >= WINDOW) | (i > 0))


def _stack_heads(ref, j, kh, lo):
    parts = []
    for t in range(ATT_GROUP):
        h = ATT_GROUP * j + t
        blk = ref[:, (h // 2) * LANE:(h // 2 + 1) * LANE]
        blk = jnp.where(lo if h % 2 == 0 else jnp.logical_not(lo), blk, jnp.zeros_like(blk))
        parts.append(blk if h % 2 == kh else pltpu.roll(blk, HEAD_DIM, 1))
    return jnp.concatenate(parts, axis=0)


def _unstack_heads(stacked, j, kh, lo, put):
    for t in range(0, ATT_GROUP, 2):
        h = ATT_GROUP * j + t
        even = stacked[t * WINDOW:(t + 1) * WINDOW, :]
        odd = stacked[(t + 1) * WINDOW:(t + 2) * WINDOW, :]
        even = even if kh == 0 else pltpu.roll(even, HEAD_DIM, 1)
        odd = odd if kh == 1 else pltpu.roll(odd, HEAD_DIM, 1)
        put(h // 2, jnp.where(lo, even, odd))


def _per_head_rows(ref, j):
    return jnp.concatenate([ref[:, ATT_GROUP * j + t:ATT_GROUP * j + t + 1] for t in range(ATT_GROUP)], axis=0)


def _per_head_scalar(ref, j):
    rows = _iota((ATT_GROUP * WINDOW, 1), 0) // WINDOW
    out = jnp.zeros((ATT_GROUP * WINDOW, 1), F32)
    for t in range(ATT_GROUP):
        out = out + jnp.where(rows == t, ref[:, ATT_GROUP * j + t:ATT_GROUP * j + t + 1], 0.0)
    return out


def attn_fwd(q, k, v, sinks, name):
    n, qw = q.shape
    kw = k.shape[1]
    nb = n // WINDOW
    scale = HEAD_DIM ** -0.5

    def body(q_ref, kc_ref, kp_ref, vc_ref, vp_ref, sk_ref, o_ref, lse_ref):
        i = pl.program_id(0)
        valid = _attn_mask(i)
        lo = _iota((WINDOW, LANE), 1) < HEAD_DIM
        k2 = jnp.concatenate([kp_ref[...], kc_ref[...]], axis=0)
        v2 = jnp.concatenate([vp_ref[...], vc_ref[...]], axis=0)
        lane1 = _iota((1, LANE), 1)
        lse = jnp.zeros((WINDOW, LANE), F32)

        def put_o(qb, val):
            o_ref[:, qb * LANE:(qb + 1) * LANE] = val.astype(o_ref.dtype)

        for j in range(ATT_KV_HEADS):
            kb, kh = j // 2, j % 2
            q4 = _stack_heads(q_ref, j, kh, lo)
            logits = jnp.where(valid, _mm(q4, k2[:, kb * LANE:(kb + 1) * LANE], NT) * scale, -1e30)
            sk = _per_head_scalar(sk_ref, j)
            m = jnp.maximum(jnp.max(logits, axis=-1, keepdims=True), sk)
            e = jnp.exp(logits - m)
            den = jnp.sum(e, axis=-1, keepdims=True) + jnp.exp(sk - m)
            lse4 = m + jnp.log(den)
            for t in range(ATT_GROUP):
                lse = lse + lse4[t * WINDOW:(t + 1) * WINDOW, :] * (lane1 == ATT_GROUP * j + t).astype(F32)
            _unstack_heads(_mm(e * (1.0 / den), v2[:, kb * LANE:(kb + 1) * LANE]), j, kh, lo, put_o)
        lse_ref[...] = lse

    return _call(body, name, (nb,),
                 [_rs(WINDOW, qw), _rs(WINDOW, kw), _rs(WINDOW, kw, 0, -1), _rs(WINDOW, kw), _rs(WINDOW, kw, 0, -1), _ps((1, LANE))],
                 [_rs(WINDOW, qw), _rs(WINDOW, LANE)], [S((n, qw), MXU_DTYPE), S((n, LANE), F32)],
                 sem=("parallel",))(q, k, k, v, v, sinks)


def attn_bwd(q, k, v, o, do, lse, sinks, name):
    n, qw = q.shape
    kw = k.shape[1]
    nb = n // WINDOW
    scale = HEAD_DIM ** -0.5

    def body(q_ref, kc_ref, kp_ref, vc_ref, vp_ref, o_ref, do_ref, lse_ref, sk_ref,
             dq_ref, dka_ref, dkb_ref, dva_ref, dvb_ref, dsk_ref):
        i = pl.program_id(0)
        valid = _attn_mask(i)
        lo = _iota((WINDOW, LANE), 1) < HEAD_DIM
        k2 = jnp.concatenate([kp_ref[...], kc_ref[...]], axis=0)
        v2 = jnp.concatenate([vp_ref[...], vc_ref[...]], axis=0)
        lane1 = _iota((1, LANE), 1)
        do_ = do_ref[...].astype(F32)
        delta = _mm01(do_ * o_ref[...].astype(F32), _head_indicator(qw, ATT_HEADS))
        dk2 = [jnp.zeros((2 * WINDOW, LANE), F32) for _ in range(kw // LANE)]
        dv2 = [jnp.zeros((2 * WINDOW, LANE), F32) for _ in range(kw // LANE)]
        dsk = jnp.zeros((1, LANE), F32)

        def put_dq(qb, val):
            dq_ref[:, qb * LANE:(qb + 1) * LANE] = val

        for j in range(ATT_KV_HEADS):
            kb, kh = j // 2, j % 2
            q4 = _stack_heads(q_ref, j, kh, lo)
            do4 = _stack_heads(do_ref, j, kh, lo)
            kk = k2[:, kb * LANE:(kb + 1) * LANE]
            vv = v2[:, kb * LANE:(kb + 1) * LANE]
            logits = jnp.where(valid, _mm(q4, kk, NT) * scale, -1e30)
            lse4 = _per_head_rows(lse_ref, j)
            p = jnp.exp(logits - lse4)
            dl = jnp.concatenate([delta[:, ATT_GROUP * j + t:ATT_GROUP * j + t + 1] for t in range(ATT_GROUP)], axis=0)
            ds = p * (_mm(do4, vv, NT) - dl) * scale
            sd = jnp.exp(_per_head_scalar(sk_ref, j) - lse4) * dl
            for t in range(ATT_GROUP):
                dsk = dsk - (jnp.sum(sd[t * WINDOW:(t + 1) * WINDOW, :], axis=0, keepdims=True)
                             * (lane1 == ATT_GROUP * j + t).astype(F32))
            _unstack_heads(_mm(ds, kk), j, kh, lo, put_dq)
            dk2[kb] = dk2[kb] + _mm(ds, q4, TN)
            dv2[kb] = dv2[kb] + _mm(p, do4, TN)
        for kb in range(kw // LANE):
            dkb_ref[:, kb * LANE:(kb + 1) * LANE] = dk2[kb][0:WINDOW, :]
            dka_ref[:, kb * LANE:(kb + 1) * LANE] = dk2[kb][WINDOW:, :]
            dvb_ref[:, kb * LANE:(kb + 1) * LANE] = dv2[kb][0:WINDOW, :]
            dva_ref[:, kb * LANE:(kb + 1) * LANE] = dv2[kb][WINDOW:, :]
        _acc(dsk_ref, i, dsk)

    return _call(body, name, (nb,),
                 [_rs(WINDOW, qw), _rs(WINDOW, kw), _rs(WINDOW, kw, 0, -1), _rs(WINDOW, kw), _rs(WINDOW, kw, 0, -1),
                  _rs(WINDOW, qw), _rs(WINDOW, qw), _rs(WINDOW, LANE), _ps((1, LANE))],
                 [_rs(WINDOW, qw)] + [_rs(WINDOW, kw)] * 4 + [_ps((1, LANE))],
                 [S((n, qw), F32)] + [S((n, kw), F32)] * 4 + [S((1, LANE), F32)],
                 sem=("arbitrary",))(q, k, k, v, v, o, do, lse, sinks)


def attn_grad_merge(dq, dka, dkb, dva, dvb, cos, sin, name):
    n, qw = dq.shape
    kw = dka.shape[1]
    nb = n // WINDOW
    w = qw + 2 * kw

    def body(dq_ref, dka_ref, dkb_ref, dva_ref, dvb_ref, c_ref, s_ref, o_ref, db_ref):
        i = pl.program_id(0)
        c, s = c_ref[...], -s_ref[...]
        nxt = (i < nb - 1).astype(F32)

        @pl.when(i == 0)
        def _():
            db_ref[...] = jnp.zeros_like(db_ref)

        def put(c0, val):
            o_ref[:, c0:c0 + val.shape[1]] = val.astype(o_ref.dtype)
            db_ref[:, c0:c0 + val.shape[1]] += jnp.sum(val, axis=0, keepdims=True)

        for j in range(qw // LANE):
            put(j * LANE, _rope128(dq_ref[:, j * LANE:(j + 1) * LANE], c, s))
        for j in range(kw // LANE):
            sl = slice(j * LANE, (j + 1) * LANE)
            put(qw + j * LANE, _rope128(dka_ref[:, sl] + dkb_ref[:, sl] * nxt, c, s))
        put(qw + kw, dva_ref[...] + dvb_ref[...] * nxt)

    return _call(body, name, (nb,),
                 [_rs(WINDOW, qw), _rs(WINDOW, kw), _rs(WINDOW, kw, 0, 1, nb), _rs(WINDOW, kw), _rs(WINDOW, kw, 0, 1, nb),
                  _rs(WINDOW, LANE), _rs(WINDOW, LANE)],
                 [_rs(WINDOW, w), _ps((1, w))], [S((n, w), MXU_DTYPE), S((1, w), F32)],
                 sem=("arbitrary",))(dq, dka, dkb, dva, dvb, cos, sin)


def _row(v):
    return v.reshape(1, -1)


def _pad_lanes(v, width=LANE):
    return jnp.pad(v.reshape(1, -1), ((0, 0), (0, width - v.size)))


class LayerWeights(dict):
    def __init__(self, small, fetch):
        super().__init__(small)
        self.fetch = fetch

    def need(self, k, after):
        if k not in self:
            self[k] = self.fetch(k, after)
        return self[k]


def ffn_fwd(h, g, w, keys, tag):
    xn, u, hm = swiglu_in(h, _row(g), w.need(keys[0], h), f"{tag}_in")
    return matmul(hm, w.need(keys[1], hm), "nn", f"{tag}_out", scale=0.5, res=h), (h, xn, u, hm)


class GradSink:
    ORDER = ("ffn1_w_out", "ffn2_w_out", "ple_gate_w", "att_w_o", "hyb_w_out", "ffn1_w_in", "ffn2_w_in", "att_w_qkv",
             "ple_proj_w", "hyb_w_in")

    def __init__(self, shard_shapes, bucket_of):
        self.where, rows = {}, {}
        for k in self.ORDER:
            n, r, c = shard_shapes[k]
            for li in range(n):
                layer = li if k in PER_LAYER else 2 * li + (0 if k in EVEN_ONLY else 1)
                rows_b = rows.setdefault(bucket_of(layer, _stage(k)), {})
                key = c if r % 32 == 0 else k
                off = -(-rows_b.get(key, (0, c))[0] // r) * r
                rows_b[key] = (-(-(off + r) // 32) * 32, c)
                self.where[k, li] = (bucket_of(layer, _stage(k)), key, "r" if _shard_axis(k) == 1 else "c", off, r)
        self.bufs = {b: {key: lax.empty((N_CHIPS, r, c), MXU_DTYPE) for key, (r, c) in rows_b.items()}
                     for b, rows_b in rows.items()}

    def mm(self, k, li, a, b, name, scale=None, c0=0, paired=False):
        bucket, key, kind, off, r = self.where[k, li]
        buf = self.bufs[bucket][key]
        slot = Slot(buf, kind, r if kind == "r" else buf.shape[2], off, c0, paired)
        self.bufs[bucket][key] = matmul(a, b, "tn", name, scale=scale, into=slot)

    def put(self, k, li, chip_major):
        b, key = self.where[k, li][:2]
        pad = self.bufs[b][key].shape[1] - chip_major.shape[1]
        self.bufs[b][key] = jnp.pad(chip_major.astype(self.bufs[b][key].dtype), ((0, 0), (0, pad), (0, 0)))


def ffn_bwd(dh, g, w_in, w_out, saved, tag, sink, keys, layer, after, colsum=False):
    h, xn, u, hm = saved
    sink.mm(keys[1], layer, hm, dh, f"{tag}_dwout", scale=0.5)
    du = swiglu_out_bwd(dh, w_out, u, after, f"{tag}_dhm")
    sink.mm(keys[0], layer, xn, du, f"{tag}_dwin", paired=True)
    outs = nt_rms_bwd(du, ColSharded(w_in.arr, paired=True), h, _row(g), dh, f"{tag}_dxn", colsum=colsum)
    return (outs[0], outs[1].reshape(-1)) + ((outs[2],) if colsum else ())


def _hyb_params(w):
    d = w["conv_dw_b"].size
    inner = SSM_HEADS * HEAD_DIM
    main = 3 * d + w["ssm_conv_b"].size
    return dict(
        w_main=w["hyb_w_in"][:main], w_dt=jnp.pad(w["hyb_w_in"][main:], ((0, LANE - SSM_HEADS), (0, 0))),
        cw=jnp.pad(w["conv_dw_w"], ((0, 32 - CONV_WIDTH), (0, 0))), cb=_row(w["conv_dw_b"]),
        lg=_row(w["conv_ln_g"]), lb=_row(w["conv_ln_b"]),
        sw=jnp.pad(w["ssm_conv_w"], ((0, 8 - SSM_CONV), (0, 0))), sb=_row(w["ssm_conv_b"]),
        dtb=_pad_lanes(w["ssm_dt_bias"]), al_row=_pad_lanes(w["ssm_a_log"]), al_col=w["ssm_a_log"].reshape(-1, 1),
        dfull=_row(jnp.repeat(w["ssm_d"], HEAD_DIM)), gamma=_row(w["ssm_norm"]), d=d, inner=inner, main=main)


def hyb_fwd(h, w, tag):
    w.need("hyb_w_in", h)
    q = _hyb_params(w)
    proj, xn = rms_matmul(h, _row(w["norm_mix"]), q["w_main"], "nt", f"{tag}_in")
    dtr = matmul(xn, q["w_dt"], "nt", f"{tag}_in_dt")
    u, u1 = conv_group_fwd(proj, q["cw"], q["cb"], q["lg"], q["lb"], f"{tag}_conv")
    pre, xs, bc, dt = ssm_conv_fwd(proj, dtr, q["sw"], q["sb"], q["dtb"], f"{tag}_sconv")
    dtT = dt[:, :SSM_HEADS].T
    yssd, hs = ssd_fwd(xs, bc, dt, dtT, q["al_row"], q["al_col"], f"{tag}_ssd")
    y = ssm_gate_fwd(yssd, xs, proj, q["dfull"], q["gamma"], f"{tag}_gate")
    wo = w.need("hyb_w_out", u)
    h2 = matmul(u, wo[:q["d"]], "nn", f"{tag}_out_a", res=h)
    h2 = matmul(y, wo[q["d"]:], "nn", f"{tag}_out_b", res=h2)
    return h2, (h, xn, proj, u, u1, pre, xs, bc, dt, dtT, yssd, hs, y)


def hyb_bwd(dh, w, saved, tag, sink, layer):
    q = _hyb_params(w)
    h, xn, proj, u, u1, pre, xs, bc, dt, dtT, yssd, hs, y = saved
    du = matmul(dh, w["hyb_w_out"][:q["d"]], "nt", f"{tag}_du")
    dy3 = matmul(dh, w["hyb_w_out"][q["d"]:], "nt", f"{tag}_dy")
    sink.mm("hyb_w_out", layer, u, dh, f"{tag}_dwo_a", c0=0)
    sink.mm("hyb_w_out", layer, y, dh, f"{tag}_dwo_b", c0=N_CHIPS // 2)
    dproj, dcw, dcb, dlg, dlb = conv_group_bwd(du, u1, proj, q["cw"], q["lg"], q["lb"], f"{tag}_dconv")
    dyssd, dxs_skip, dproj, dgamma, dd = ssm_gate_bwd(dy3, yssd, xs, proj, q["dfull"], q["gamma"], dproj, f"{tag}_dgate")
    dxs, dbc, ddtr, dalog, ddtb = ssd_bwd(xs, bc, dt, dtT, q["al_row"], q["al_col"], hs, dyssd, dxs_skip, f"{tag}_dssd")
    dproj, dsw, dsb = ssm_conv_bwd(dxs, dbc, pre, proj, q["sw"], dproj, f"{tag}_dsconv")
    dw_in = jnp.concatenate([matmul(dproj, xn, "tn", f"{tag}_dwin"),
                             matmul(ddtr, xn, "tn", f"{tag}_dwin_dt")[:SSM_HEADS]], axis=0)
    sink.put("hyb_w_in", layer, dw_in.reshape((N_CHIPS, -1) + dw_in.shape[1:]))
    dh2, dg = nt_rms_bwd(dproj, q["w_main"], h, _row(w["norm_mix"]), dh, f"{tag}_dxn", extra=(ddtr, q["w_dt"]), b_kd=True)
    grads = dict(norm_mix=dg.reshape(-1), conv_dw_w=dcw[:CONV_WIDTH], conv_dw_b=dcb.reshape(-1),
                 conv_ln_g=dlg.reshape(-1), conv_ln_b=dlb.reshape(-1), ssm_conv_w=dsw[:SSM_CONV], ssm_conv_b=dsb.reshape(-1),
                 ssm_dt_bias=ddtb.reshape(-1), ssm_a_log=dalog.reshape(-1), ssm_d=dd.reshape(-1), ssm_norm=dgamma.reshape(-1))
    return dh2, grads


def rope_tables(n):
    half = HEAD_DIM // 2
    inv = ROPE_THETA ** (-jnp.arange(0, HEAD_DIM, 2, dtype=F32) / HEAD_DIM)
    ang = jnp.arange(n, dtype=F32)[:, None] * inv[None, :]
    cos, sin = jnp.cos(ang), jnp.sin(ang)
    reps = LANE // HEAD_DIM
    return jnp.tile(jnp.concatenate([cos, cos], axis=1), (1, reps)), jnp.tile(jnp.concatenate([-sin, sin], axis=1), (1, reps))


def att_fwd(h, w, tables, tag):
    cos, sin = tables
    qkv, xn = rms_matmul(h, _row(w["norm_mix"]), w.need("att_w_qkv", h), "nn", f"{tag}_qkv", bias=_row(w["att_b_qkv"]))
    q, k, v = rope_fwd(qkv, cos, sin, f"{tag}_rope")
    sinks = _pad_lanes(w["att_sinks"])
    o, lse = attn_fwd(q, k, v, sinks, f"{tag}_attn")
    h2 = matmul(o, w.need("att_w_o", o), "nn", f"{tag}_o", bias=_row(w["att_b_o"]), res=h)
    return h2, (h, xn, q, k, v, o, lse, sinks)


def att_bwd(dh, dh_colsum, w, saved, tables, tag, sink, layer):
    cos, sin = tables
    h, xn, q, k, v, o, lse, sinks = saved
    do = matmul(dh, w["att_w_o"], "nt", f"{tag}_do")
    sink.mm("att_w_o", layer, o, dh, f"{tag}_dwo")
    dq, dka, dkb, dva, dvb, dsk = attn_bwd(q, k, v, o, do, lse, sinks, f"{tag}_dattn")
    dqkv, dbqkv = attn_grad_merge(dq, dka, dkb, dva, dvb, cos, sin, f"{tag}_drope")
    sink.mm("att_w_qkv", layer, xn, dqkv, f"{tag}_dwqkv")
    dh2, dg = nt_rms_bwd(dqkv, w["att_w_qkv"], h, _row(w["norm_mix"]), dh, f"{tag}_dxn")
    grads = dict(norm_mix=dg.reshape(-1), att_b_qkv=dbqkv.reshape(-1), att_sinks=dsk[0, :ATT_HEADS],
                 att_b_o=dh_colsum.reshape(-1))
    return dh2, grads


def ple_block_fwd(h, pe, w, tag):
    pp = matmul(pe, w.need("ple_proj_w", h), "nn", f"{tag}_proj")
    out, gl, xn = ple_fwd(h, _row(w["ple_norm"]), w.need("ple_gate_w", h), pp, f"{tag}_gate")
    return out, (h, xn, gl, pp, pe)


def ple_block_bwd(dh, w, saved, tag, sink, layer, after):
    h, xn, gl, pp, pe = saved
    dh2, dpp, dgl, dg = ple_bwd(dh, gl, pp, w["ple_gate_w"], h, _row(w["ple_norm"]), after, f"{tag}_dgate")
    sink.mm("ple_proj_w", layer, pe, dpp, f"{tag}_dwp")
    sink.mm("ple_gate_w", layer, xn, dgl, f"{tag}_dwg")
    return dh2, dict(ple_norm=dg.reshape(-1))


PER_LAYER = ("norm_ffn1", "ffn1_w_in", "ffn1_w_out", "norm_mix", "norm_ffn2", "ffn2_w_in", "ffn2_w_out",
             "ple_norm", "ple_gate_w", "ple_proj_w")
EVEN_ONLY = ("hyb_w_in", "conv_dw_w", "conv_dw_b", "conv_ln_g", "conv_ln_b", "ssm_conv_w", "ssm_conv_b",
             "ssm_dt_bias", "ssm_a_log", "ssm_d", "ssm_norm", "hyb_w_out")
ODD_ONLY = ("att_w_qkv", "att_b_qkv", "att_sinks", "att_w_o", "att_b_o")


def _layer_index(k, i):
    if k in PER_LAYER:
        return i
    if k in (EVEN_ONLY if i % 2 == 0 else ODD_ONLY):
        return i // 2
    return None


def _stage(k):
    return 0 if k.startswith("ffn1") else (2 if k.startswith(("ffn2", "ple")) else 1)


def trunk_fwd_bwd(x, pe, target, layers, final_norm, sink, stage_done):
    depth = len(layers)
    tables = rope_tables(x.shape[0])
    h = x
    saved = []
    for i, w in enumerate(layers):
        h, s1 = ffn_fwd(h, w["norm_ffn1"], w, ("ffn1_w_in", "ffn1_w_out"), f"l{i}_ffn1")
        if i % 2 == 0:
            h, s2 = hyb_fwd(h, w, f"l{i}_hyb")
        else:
            h, s2 = att_fwd(h, w, tables, f"l{i}_att")
        h, s3 = ffn_fwd(h, w["norm_ffn2"], w, ("ffn2_w_in", "ffn2_w_out"), f"l{i}_ffn2")
        h, s4 = ple_block_fwd(h, pe[i], w, f"l{i}_ple")
        saved.append((s1, s2, s3, s4))
    dh, dgf, loss = loss_head(h, _row(final_norm), target, "loss_head")
    grads = {}
    tie = dgf
    for i in reversed(range(depth)):
        w = layers[i]
        s1, s2, s3, s4 = saved[i]
        dh, g = ple_block_bwd(dh, w, s4, f"l{i}_ple", sink, i, tie)
        odd = i % 2 == 1
        out = ffn_bwd(dh, w["norm_ffn2"], w["ffn2_w_in"], w["ffn2_w_out"], s3, f"l{i}_ffn2", sink,
                      ("ffn2_w_in", "ffn2_w_out"), i, tie, colsum=odd)
        dh = out[0]
        g.update(norm_ffn2=out[1])
        if odd:
            dh, gm = att_bwd(dh, out[2], w, s2, tables, f"l{i}_att", sink, i // 2)
        else:
            dh, gm = hyb_bwd(dh, w, s2, f"l{i}_hyb", sink, i // 2)
        g.update(gm)
        tie = stage_done(i, 1, tie)
        out = ffn_bwd(dh, w["norm_ffn1"], w["ffn1_w_in"], w["ffn1_w_out"], s1, f"l{i}_ffn1", sink,
                      ("ffn1_w_in", "ffn1_w_out"), i, tie)
        dh = out[0]
        g.update(norm_ffn1=out[1])
        tie = stage_done(i, 0, tie)
        for k, v in g.items():
            grads.setdefault(k, []).insert(0, v)
    grads = {k: jnp.stack(v) for k, v in grads.items()}
    grads["final_norm"] = dgf.reshape(-1)
    return loss, dh, grads


def _me():
    return lax.axis_index("x"), lax.axis_index("y"), lax.axis_index("c")


def _flip(v, f):
    return 1 - v if f else v


def _remote(src, dst, send_sems, recv_sems, k, dev):
    return pltpu.make_async_remote_copy(src_ref=src, dst_ref=dst, send_sem=send_sems.at[k], recv_sem=recv_sems.at[k],
                                        device_id=dev, device_id_type=MESH)


CHIP_FLIPS = ((1, 0), (0, 1), (1, 1))
DEV_FLIPS = tuple((fx, fy, fc) for fx in (0, 1) for fy in (0, 1) for fc in (0, 1))[1:]


HBM = pl.BlockSpec(memory_space=pltpu.HBM)
SEM = pl.BlockSpec(memory_space=pltpu.SEMAPHORE)
DATAFLOW = pltpu.SideEffectType.DATAFLOW_SIDE_EFFECTING


def _core_half(ref, axis, c):
    if axis is None:
        return ref
    h = ref.shape[axis] // 2
    return ref.at[pl.ds(c * h, h), :] if axis == 0 else ref.at[:, pl.ds(c * h, h)]


def gather_start(xs, lands, halves, after, name):
    na = len(xs)

    def body(*refs):
        x_refs, land_refs = refs[:na], refs[na:2 * na]
        send_sems, recv_sems = refs[2 * na + 1], refs[2 * na + 2]
        token = refs[-1]
        mx, my, mc = _me()
        chip = 2 * mx + my
        for a in range(na):
            for j, (fx, fy) in enumerate(CHIP_FLIPS):
                _remote(_core_half(x_refs[a], halves[a], mc), _core_half(land_refs[a].at[chip], halves[a], mc),
                        send_sems, recv_sems, 3 * a + j, (_flip(mx, fx), _flip(my, fy), mc)).start()
        token[...] = jnp.zeros_like(token)

    outs = pl.pallas_call(
        body, name=name,
        out_shape=(pltpu.SemaphoreType.DMA((3 * na,)), pltpu.SemaphoreType.DMA((3 * na,)))
        + tuple(pltpu.HBM(x.shape, x.dtype) for x in xs) + tuple(pltpu.HBM(l.shape, l.dtype) for l in lands)
        + (S((8, LANE), F32),),
        in_specs=[HBM] * (2 * na) + [pl.BlockSpec(memory_space=pl.ANY)],
        out_specs=(SEM, SEM) + (HBM,) * (2 * na) + (pl.BlockSpec(memory_space=pltpu.VMEM),),
        input_output_aliases={a: 2 + a for a in range(2 * na)},
        compiler_params=pltpu.CompilerParams(has_side_effects=DATAFLOW),
    )(*[pltpu.with_memory_space_constraint(t, pltpu.HBM) for t in list(xs) + list(lands)], after)
    return outs[0], outs[1], list(outs[2:2 + na]), list(outs[2 + na:2 + 2 * na])


def gather_wait(send_sems, recv_sems, xs, lands, halves, first, after, name):
    na = len(xs)

    def body(*refs):
        x_refs, land_refs = refs[:na], refs[na:2 * na]
        send_sems, recv_sems = refs[2 * na], refs[2 * na + 1]
        mx, my, mc = _me()
        for a in range(na):
            for j, (fx, fy) in enumerate(CHIP_FLIPS):
                px, py = _flip(mx, fx), _flip(my, fy)
                cp = _remote(_core_half(x_refs[a], halves[a], mc), _core_half(land_refs[a].at[2 * px + py], halves[a], mc),
                             send_sems, recv_sems, 3 * (first + a) + j, (px, py, mc))
                cp.wait_send()
                cp.wait_recv()

    outs = pl.pallas_call(
        body, name=name,
        out_shape=tuple(pltpu.HBM(x.shape, x.dtype) for x in xs) + tuple(pltpu.HBM(l.shape, l.dtype) for l in lands),
        in_specs=[HBM] * (2 * na) + [SEM, SEM, pl.BlockSpec(memory_space=pl.ANY)], out_specs=(HBM,) * (2 * na),
        input_output_aliases={a: a for a in range(2 * na)},
        compiler_params=pltpu.CompilerParams(has_side_effects=DATAFLOW),
    )(*xs, *lands, send_sems, recv_sems, after)
    return list(outs[na:])


def forward_halves(land, axis, name):
    def body(in_ref, out_ref, send_sems, recv_sems):
        del in_ref
        mx, my, mc = _me()
        sib = (mx, my, 1 - mc)
        slots = [2 * _flip(mx, fx) + _flip(my, fy) for fx, fy in CHIP_FLIPS]
        cps = [_remote(_core_half(out_ref.at[s], axis, mc), _core_half(out_ref.at[s], axis, mc), send_sems, recv_sems, j, sib)
               for j, s in enumerate(slots)]
        for cp in cps:
            cp.start()
        for j, s in enumerate(slots):
            _remote(_core_half(out_ref.at[s], axis, mc), _core_half(out_ref.at[s], axis, 1 - mc), send_sems, recv_sems, j, sib).wait_recv()
        for cp in cps:
            cp.wait_send()

    return pl.pallas_call(
        body, name=name, out_shape=S(land.shape, land.dtype), in_specs=[ANY], out_specs=ANY, input_output_aliases={0: 0},
        scratch_shapes=[pltpu.SemaphoreType.DMA((3,)), pltpu.SemaphoreType.DMA((3,))])(land)


def all_gather_devices(v, name):
    r, l = v.shape

    def body(v_ref, out_ref, send_sems, recv_sems):
        mx, my, mc = _me()
        me = 4 * mx + 2 * my + mc
        peers = [(_flip(mx, fx), _flip(my, fy), _flip(mc, fc)) for fx, fy, fc in DEV_FLIPS]
        sends = [_remote(v_ref, out_ref.at[me], send_sems, recv_sems, j, p) for j, p in enumerate(peers)]
        for cp in sends:
            cp.start()
        for j, (px, py, pc) in enumerate(peers):
            _remote(v_ref, out_ref.at[4 * px + 2 * py + pc], send_sems, recv_sems, j, (px, py, pc)).wait_recv()
        for cp in sends:
            cp.wait_send()

    out = pl.pallas_call(
        body, name=name, out_shape=S((N_DEV, r, l), v.dtype), in_specs=[ANY], out_specs=ANY,
        scratch_shapes=[pltpu.SemaphoreType.DMA((7,)), pltpu.SemaphoreType.DMA((7,))])(v)
    me = 4 * lax.axis_index("x") + 2 * lax.axis_index("y") + lax.axis_index("c")
    return lax.dynamic_update_slice_in_dim(out, v[None], me, axis=0)


def gather_devices_start(v, name):
    me = 4 * lax.axis_index("x") + 2 * lax.axis_index("y") + lax.axis_index("c")
    land = lax.dynamic_update_slice_in_dim(lax.empty((N_DEV,) + v.shape, v.dtype), v[None], me, axis=0)

    def body(v_ref, land_ref, send_sems, recv_sems, v_thru, land_thru, token):
        mx, my, mc = _me()
        for j, (fx, fy, fc) in enumerate(DEV_FLIPS):
            _remote(v_ref, land_ref.at[4 * mx + 2 * my + mc], send_sems, recv_sems, j,
                    (_flip(mx, fx), _flip(my, fy), _flip(mc, fc))).start()
        token[...] = jnp.zeros_like(token)

    outs = pl.pallas_call(
        body, name=name,
        out_shape=(pltpu.SemaphoreType.DMA((7,)), pltpu.SemaphoreType.DMA((7,)), pltpu.HBM(v.shape, v.dtype),
                   pltpu.HBM(land.shape, land.dtype), S((8, LANE), F32)),
        in_specs=[HBM, HBM], out_specs=(SEM, SEM, HBM, HBM, pl.BlockSpec(memory_space=pltpu.VMEM)),
        input_output_aliases={0: 2, 1: 3}, compiler_params=pltpu.CompilerParams(has_side_effects=DATAFLOW),
    )(pltpu.with_memory_space_constraint(v, pltpu.HBM), pltpu.with_memory_space_constraint(land, pltpu.HBM))
    return outs[:4]


def gather_devices_wait(send_sems, recv_sems, v, land, after, name):
    def body(v_ref, land_ref, send_sems, recv_sems, after_ref, v_dead, got_ref):
        mx, my, mc = _me()
        for j, (fx, fy, fc) in enumerate(DEV_FLIPS):
            px, py, pc = _flip(mx, fx), _flip(my, fy), _flip(mc, fc)
            cp = _remote(v_ref, land_ref.at[4 * px + 2 * py + pc], send_sems, recv_sems, j, (px, py, pc))
            cp.wait_send()
            cp.wait_recv()

    return pl.pallas_call(
        body, name=name, out_shape=(pltpu.HBM(v.shape, v.dtype), pltpu.HBM(land.shape, land.dtype)),
        in_specs=[HBM, HBM, SEM, SEM, pl.BlockSpec(memory_space=pl.ANY)], out_specs=(HBM, HBM),
        input_output_aliases={0: 0, 1: 1}, compiler_params=pltpu.CompilerParams(has_side_effects=DATAFLOW),
    )(v, land, send_sems, recv_sems, after)[1]


def sum_devices(g8, name):
    nd, r, l = g8.shape
    tile = r
    for t in (512, 256, 128, 64, 32, 16, 8):
        if r % t == 0:
            tile = t
            break

    def body(g_ref, o_ref):
        acc = g_ref[0]
        for d in range(1, nd):
            acc = acc + g_ref[d]
        o_ref[...] = acc

    return _call(body, name, (r // tile,), [pl.BlockSpec((nd, tile, l), lambda i: (0, i, 0))], _rs(tile, l), S((r, l), F32),
                 sem=("parallel",))(g8)


def exchange_halves(gs, name):
    na = len(gs)
    nch = gs[0].shape[0]

    def body(*refs):
        g_refs, out_refs = refs[:na], refs[na:2 * na]
        send_sems, recv_sems = refs[2 * na:]
        mx, my, mc = _me()
        sib = (mx, my, 1 - mc)
        cps = []
        for a in range(na):
            half = gs[a].shape[1] // 2
            for j in range(nch):
                cps.append(_remote(g_refs[a].at[j, pl.ds((1 - mc) * half, half), :], out_refs[a].at[j],
                                   send_sems, recv_sems, nch * a + j, sib))
        for cp in cps:
            cp.start()
        for cp in cps:
            cp.wait_recv()
        for cp in cps:
            cp.wait_send()

    return pl.pallas_call(
        body, name=name, out_shape=[S((nch, g.shape[1] // 2, g.shape[2]), g.dtype) for g in gs],
        in_specs=[ANY] * na, out_specs=[ANY] * na,
        scratch_shapes=[pltpu.SemaphoreType.DMA((nch * na,)), pltpu.SemaphoreType.DMA((nch * na,))])(*gs)


def add_halves(g4, got, name):
    nch, r, l = g4.shape
    half = r // 2
    tile = _pick_rows(half)
    nt = half // tile

    def body(g_ref, r_ref, a_ref, own_ref):
        j = pl.program_id(1)
        chip = 2 * lax.axis_index("x") + lax.axis_index("y")
        val = g_ref[0].astype(F32) + r_ref[0].astype(F32)
        a_ref[0] = val.astype(a_ref.dtype)

        @pl.when(j == chip)
        def _():
            own_ref[...] = val

    return pl.pallas_call(
        body, name=name, grid=(nt, nch),
        in_specs=[pl.BlockSpec((1, tile, l), lambda i, j: (j, lax.axis_index("c") * nt + i, 0)),
                  pl.BlockSpec((1, tile, l), lambda i, j: (j, i, 0))],
        out_specs=[pl.BlockSpec((1, tile, l), lambda i, j: (j, i, 0)), pl.BlockSpec((tile, l), lambda i, j: (i, 0))],
        out_shape=[S((nch, half, l), MXU_DTYPE), S((half, l), F32)],
        compiler_params=pltpu.CompilerParams(dimension_semantics=("parallel", "arbitrary"), vmem_limit_bytes=VMEM_LIMIT))(g4, got)


def _pick_rows(r, cap=640):
    return next((t for t in range(cap - cap % 16, 15, -16) if r % t == 0), r)


def add_chips(own, got, name):
    h, l = own.shape
    tile = _pick_rows(h)

    def body(o_ref, g_ref, out_ref):
        out_ref[...] = ((o_ref[...] + g_ref[0].astype(F32)) + g_ref[1].astype(F32)) + g_ref[2].astype(F32)

    nt = h // tile
    return _call(body, name, (nt,), [_rs(tile, l), pl.BlockSpec((3, tile, l), lambda i: (0, i, 0))],
                 pl.BlockSpec((tile, l), lambda i: (lax.axis_index("c") * nt + i, 0)),
                 S((2 * h, l), F32), sem=("parallel",))(own, got)


def join_halves(bufs, name):
    na = len(bufs)

    def body(*refs):
        out_refs = refs[na:2 * na]
        send_sems, recv_sems = refs[2 * na:]
        mx, my, mc = _me()
        sib = (mx, my, 1 - mc)

        def half(a, hc):
            h = bufs[a].shape[0] // 2
            return out_refs[a].at[pl.ds(hc * h, h), :]

        cps = [_remote(half(a, mc), half(a, mc), send_sems, recv_sems, a, sib) for a in range(na)]
        for cp in cps:
            cp.start()
        for a in range(na):
            _remote(half(a, mc), half(a, 1 - mc), send_sems, recv_sems, a, sib).wait_recv()
        for cp in cps:
            cp.wait_send()

    return pl.pallas_call(
        body, name=name, out_shape=[S(b.shape, b.dtype) for b in bufs], in_specs=[ANY] * na, out_specs=[ANY] * na,
        input_output_aliases={a: a for a in range(na)},
        scratch_shapes=[pltpu.SemaphoreType.DMA((na,)), pltpu.SemaphoreType.DMA((na,))])(*bufs)


def exchange_chips_start(parts, name):
    na = len(parts)
    lands = [lax.empty((3,) + p.shape[1:], p.dtype) for p in parts]

    def body(*refs):
        a_refs, land_refs = refs[:na], refs[na:2 * na]
        send_sems, recv_sems = refs[2 * na], refs[2 * na + 1]
        mx, my, mc = _me()
        for j, (fx, fy) in enumerate(CHIP_FLIPS):
            px, py = _flip(mx, fx), _flip(my, fy)
            for a in range(na):
                _remote(a_refs[a].at[2 * px + py], land_refs[a].at[j], send_sems, recv_sems, 3 * a + j, (px, py, mc)).start()
        refs[-1][...] = jnp.zeros_like(refs[-1])

    outs = pl.pallas_call(
        body, name=name,
        out_shape=(pltpu.SemaphoreType.DMA((3 * na,)), pltpu.SemaphoreType.DMA((3 * na,)))
        + tuple(pltpu.HBM(t.shape, t.dtype) for t in list(parts) + lands) + (S((8, LANE), F32),),
        in_specs=[HBM] * (2 * na), out_specs=(SEM, SEM) + (HBM,) * (2 * na) + (pl.BlockSpec(memory_space=pltpu.VMEM),),
        input_output_aliases={a: 2 + a for a in range(2 * na)},
        compiler_params=pltpu.CompilerParams(has_side_effects=DATAFLOW),
    )(*[pltpu.with_memory_space_constraint(t, pltpu.HBM) for t in list(parts) + lands])
    return outs[0], outs[1], list(outs[2:2 + na]), list(outs[2 + na:2 + 2 * na]), outs[-1]


def exchange_chips_wait(send_sems, recv_sems, parts, lands, after, name):
    na = len(parts)

    def body(*refs):
        a_refs, land_refs = refs[:na], refs[na:2 * na]
        send_sems, recv_sems = refs[2 * na], refs[2 * na + 1]
        mx, my, mc = _me()
        for j, (fx, fy) in enumerate(CHIP_FLIPS):
            px, py = _flip(mx, fx), _flip(my, fy)
            for a in range(na):
                cp = _remote(a_refs[a].at[2 * px + py], land_refs[a].at[j], send_sems, recv_sems, 3 * a + j, (px, py, mc))
                cp.wait_send()
                cp.wait_recv()

    outs = pl.pallas_call(
        body, name=name, out_shape=tuple(pltpu.HBM(t.shape, t.dtype) for t in list(parts) + list(lands)),
        in_specs=[HBM] * (2 * na) + [SEM, SEM, pl.BlockSpec(memory_space=pl.ANY)], out_specs=(HBM,) * (2 * na),
        input_output_aliases={a: a for a in range(2 * na)},
        compiler_params=pltpu.CompilerParams(has_side_effects=DATAFLOW),
    )(*parts, *lands, send_sems, recv_sems, after)
    return list(outs[na:])


def reduce_begin(gs, tag):
    got = exchange_halves(gs, f"{tag}_d2d")
    sums = [add_halves(g, r, f"{tag}_add1_{i}") for i, (g, r) in enumerate(zip(gs, got))]
    return [own for _, own in sums], exchange_chips_start([a for a, _ in sums], f"{tag}_ici_start")


def reduce_end(state, after, tag):
    owns, (send_sems, recv_sems, parts, lands, _) = state
    got = exchange_chips_wait(send_sems, recv_sems, parts, lands, after, f"{tag}_ici_wait")
    return [add_chips(own, r, f"{tag}_add2_{i}") for i, (own, r) in enumerate(zip(owns, got))]


PACK_L = 1024


def _pack(arrs, dtype, row_mult, lead=None):
    lead_shape = () if lead is None else arrs[0].shape[:lead]
    flat = jnp.concatenate([a.astype(dtype).reshape(lead_shape + (-1,)) for a in arrs], axis=-1)
    n = flat.shape[-1]
    unit = row_mult * PACK_L
    total = -(-n // unit) * unit
    flat = jnp.pad(flat, [(0, 0)] * len(lead_shape) + [(0, total - n)])
    return flat.reshape(lead_shape + (total // PACK_L, PACK_L))


def _unpack(packed, shapes, lead=None):
    lead_shape = () if lead is None else packed.shape[:lead]
    flat = packed.reshape(lead_shape + (-1,))
    out, off = [], 0
    for shp in shapes:
        n = int(np.prod(shp))
        out.append(flat[..., off:off + n].reshape(lead_shape + tuple(shp)))
        off += n
    return out


def _to_full(gathered, axis):
    t = jnp.moveaxis(gathered, 0, axis)
    shp = t.shape
    return t.reshape(shp[:axis] + (shp[axis] * shp[axis + 1],) + shp[axis + 2:])


WEIGHTS = ("norm_ffn1", "ffn1_w_in", "ffn1_w_out", "norm_mix", "norm_ffn2", "ffn2_w_in", "ffn2_w_out", "ple_norm",
           "ple_gate_w", "ple_proj_w", "hyb_w_in", "conv_dw_w", "conv_dw_b", "conv_ln_g", "conv_ln_b", "ssm_conv_w",
           "ssm_conv_b", "ssm_dt_bias", "ssm_a_log", "ssm_d", "ssm_norm", "hyb_w_out", "att_w_qkv", "att_b_qkv",
           "att_sinks", "att_w_o", "att_b_o", "final_norm")
SHARD_AXIS = dict(ffn1_w_in=2, ffn1_w_out=1, ffn2_w_in=2, ffn2_w_out=1, ple_gate_w=1, ple_proj_w=2, hyb_w_in=2,
                  conv_dw_w=2, ssm_conv_w=2, hyb_w_out=1, att_w_qkv=2, att_b_qkv=1, att_w_o=1, att_b_o=1)
BIG = ("ffn1_w_in", "ffn1_w_out", "ffn2_w_in", "ffn2_w_out", "ple_gate_w", "ple_proj_w", "hyb_w_in", "hyb_w_out",
       "att_w_qkv", "att_w_o")
TRANSPOSED = ("hyb_w_in",)
FORWARDED = (0, 2)


def _shard_axis(k):
    return 1 if k in TRANSPOSED else SHARD_AXIS[k]
SMALL_SHARDED = ("conv_dw_w", "ssm_conv_w", "att_b_qkv", "att_b_o")
SMALL = tuple(k for k in WEIGHTS if k not in BIG)


def _step(x, p, target, w, m, v):
    mx, my = lax.axis_index("x"), lax.axis_index("y")
    chip = 2 * mx + my
    w, m, v = ({k: (a.transpose(0, 2, 1) if k in TRANSPOSED else a) for k, a in d.items()} for d in (w, m, v))

    depth = w["norm_ffn1"].shape[0]
    order = sorted([(k, i) for i in range(depth) for k in BIG if _layer_index(k, i) is not None],
                   key=lambda t: (t[1], _stage(t[0])))
    small_g = all_gather_devices(_pack([w[k] for k in SMALL_SHARDED], F32, 8), "gather_small")
    shards = [w[k][_layer_index(k, i)].astype(MXU_DTYPE) for k, i in order]
    lands = [lax.dynamic_update_slice_in_dim(lax.empty((N_CHIPS,) + s.shape, s.dtype), s[None], chip, axis=0) for s in shards]
    halves = [(0 if s.shape[0] % 32 == 0 else 1) if pos in FORWARDED else None for pos, s in enumerate(shards)]
    send_sems, recv_sems, shards, lands = gather_start(shards, lands, halves, small_g, "gather_start")

    def fetch(i, k, after):
        p = order.index((k, i))
        g, = gather_wait(send_sems, recv_sems, [shards[p]], [lands[p]], [halves[p]], p, after, f"gather_wait_l{i}_{k}")
        if halves[p] is not None:
            g = forward_halves(g, halves[p], f"gather_forward_l{i}_{k}")
        if _shard_axis(k) == 2:
            return ColSharded(g)
        return g.reshape(-1, g.shape[-1])

    small_g = small_g[0::2]
    small_full = {k: _to_full(g, SHARD_AXIS[k])
                  for k, g in zip(SMALL_SHARDED, _unpack(small_g, [w[k].shape for k in SMALL_SHARDED], lead=1))}
    layers = [LayerWeights({k: small_full.get(k, w[k])[_layer_index(k, i)] for k in SMALL if _layer_index(k, i) is not None},
                           functools.partial(fetch, i)) for i in range(depth)]

    def bucket_of(layer, stage):
        return (layer, 0) if layer > 0 else (0, min(stage, 1))

    sink = GradSink({k: w[k].shape for k in BIG}, bucket_of)
    begun = {}

    def stage_done(i, stage, tie):
        b = bucket_of(i, stage)
        if stage > 0 and bucket_of(i, stage - 1) == b:
            return tie
        begun[b] = reduce_begin(list(sink.bufs[b].values()), f"grads_l{b[0]}_{b[1]}")
        return begun[b][1][-1]

    loss, dx, grads = trunk_fwd_bwd(x[0], p[:, 0], target[0], layers, w["final_norm"], sink, stage_done)

    results = {}

    def finish(buckets, after, tag):
        halves = {b: reduce_end(begun[b], after, f"grads_l{b[0]}_{b[1]}") for b in buckets}
        joined = iter(join_halves([h for b in buckets for h in halves[b]], f"grads_join_{tag}"))
        reduced = {b: {c: next(joined) for c in sink.bufs[b]} for b in buckets}
        last = after
        for (k, li), (b, key, _, off, r) in sink.where.items():
            if b in buckets:
                g = reduced[b][key]
                if r % 8:
                    g, off = g[off:off + r], 0
                results[k] = adamw_layer(w[k], g, off, m[k], v[k], li, results.get(k), f"adamw_{k}_{li}")
                last = results[k][1]
        return last

    vec = gather_devices_start(_pack([loss[0:1, 0:1]] + [grads[k] for k in SMALL], F32, 8), "gather_vectors_start")
    order_b = list(begun)
    started_last = begun[order_b[-1]][1][-1]
    done = finish(order_b[-1:], finish(order_b[:-1], started_last, "early") if len(order_b) > 1 else dx, "last")
    g_out = {k: results[k][0] for k in BIG}
    vec = sum_devices(gather_devices_wait(*vec, done, "gather_vectors_wait"), "sum_vectors")
    parts = _unpack(vec, [(1, 1)] + [grads[k].shape for k in SMALL])
    loss_out = parts[0].reshape(())
    for k, g in zip(SMALL, parts[1:]):
        if k in SHARD_AXIS:
            ax = SHARD_AXIS[k]
            g = lax.dynamic_slice_in_dim(g, chip * w[k].shape[ax], w[k].shape[ax], axis=ax)
        g_out[k] = g

    for k in TRANSPOSED:
        results[k] = [a.transpose(0, 2, 1) for a in results[k]]
    g_out.update({k: results[k][0] for k in TRANSPOSED})
    delta, new_m, new_v = ({k: results[k][j] for k in BIG} for j in (1, 2, 3))
    shapes = [w[k].shape for k in SMALL]
    packed = [_pack([src[k] for k in SMALL], F32, 8) for src in (w, g_out, m, v)]
    outs = adamw(*packed, "adamw_small")
    for dst, o in zip((delta, new_m, new_v), outs):
        for k, a in zip(SMALL, _unpack(o, shapes)):
            dst[k] = a
    return ((loss_out, dx[None]) + tuple(g_out[k] for k in WEIGHTS) + tuple(delta[k] for k in WEIGHTS)
            + tuple(new_m[k] for k in WEIGHTS) + tuple(new_v[k] for k in WEIGHTS))


def kernel(x, p, norm_ffn1, ffn1_w_in, ffn1_w_out, norm_mix, norm_ffn2, ffn2_w_in, ffn2_w_out, ple_norm, ple_gate_w, ple_proj_w, hyb_w_in, conv_dw_w, conv_dw_b, conv_ln_g, conv_ln_b, ssm_conv_w, ssm_conv_b, ssm_dt_bias, ssm_a_log, ssm_d, ssm_norm, hyb_w_out, att_w_qkv, att_b_qkv, att_sinks, att_w_o, att_b_o, final_norm, loss_target, m_norm_ffn1, m_ffn1_w_in, m_ffn1_w_out, m_norm_mix, m_norm_ffn2, m_ffn2_w_in, m_ffn2_w_out, m_ple_norm, m_ple_gate_w, m_ple_proj_w, m_hyb_w_in, m_conv_dw_w, m_conv_dw_b, m_conv_ln_g, m_conv_ln_b, m_ssm_conv_w, m_ssm_conv_b, m_ssm_dt_bias, m_ssm_a_log, m_ssm_d, m_ssm_norm, m_hyb_w_out, m_att_w_qkv, m_att_b_qkv, m_att_sinks, m_att_w_o, m_att_b_o, m_final_norm, v_norm_ffn1, v_ffn1_w_in, v_ffn1_w_out, v_norm_mix, v_norm_ffn2, v_ffn2_w_in, v_ffn2_w_out, v_ple_norm, v_ple_gate_w, v_ple_proj_w, v_hyb_w_in, v_conv_dw_w, v_conv_dw_b, v_conv_ln_g, v_conv_ln_b, v_ssm_conv_w, v_ssm_conv_b, v_ssm_dt_bias, v_ssm_a_log, v_ssm_d, v_ssm_norm, v_hyb_w_out, v_att_w_qkv, v_att_b_qkv, v_att_sinks, v_att_w_o, v_att_b_o, v_final_norm):
    given = locals()
    w = {k: given[k] for k in WEIGHTS}
    m = {k: given["m_" + k] for k in WEIGHTS}
    v = {k: given["v_" + k] for k in WEIGHTS}
    return _step(x, p, loss_target, w, m, v)
```

```python
import functools

import numpy as np
import jax
import jax.numpy as jnp
from jax import lax
from jax.experimental import pallas as pl
from jax.experimental.pallas import tpu as pltpu

F32 = jnp.float32
BF16 = jnp.bfloat16
MXU_DTYPE = jnp.bfloat16
S = jax.ShapeDtypeStruct
MESH = pl.DeviceIdType.MESH

VMEM_LIMIT = 48 * 2**20
LANE = 128

EPS = 1e-6
SSM_HEADS = 16
HEAD_DIM = 64
SSM_GROUPS = 2
SSM_STATE = 128
SSM_CONV = 4
CHUNK = 128
CONV_WIDTH = 31
ATT_HEADS = 16
ATT_KV_HEADS = 4
WINDOW = 128
ROPE_THETA = 10000.0
ADAM_LR = 0.001
ADAM_B1 = 0.9
ADAM_B2 = 0.999
ADAM_EPS = 1e-08
ADAM_WD = 0.01
ADAM_STEP = 10

N_CHIPS = 4
N_DEV = 8

NN = ((1,), (0,))
NT = ((1,), (1,))
TN = ((0,), (0,))


def _mm(a, b, dims=NN):
    return lax.dot_general(a.astype(MXU_DTYPE), b.astype(MXU_DTYPE), (dims, ((), ())), preferred_element_type=F32)


def _split3(a):
    hi = a.astype(BF16)
    r = a - hi.astype(F32)
    mid = r.astype(BF16)
    lo = (r - mid.astype(F32)).astype(BF16)
    return hi, mid, lo


def _mm01(a, onehot, dims=NN):
    o = onehot.astype(BF16)
    out = None
    for part in _split3(a):
        t = lax.dot_general(part, o, (dims, ((), ())), preferred_element_type=F32)
        out = t if out is None else out + t
    return out


def _01mm(onehot, a):
    o = onehot.astype(BF16)
    out = None
    for part in _split3(a):
        t = lax.dot_general(o, part, (NN, ((), ())), preferred_element_type=F32)
        out = t if out is None else out + t
    return out


def _sigmoid(x):
    return 0.5 * jnp.tanh(0.5 * x) + 0.5


def _softplus(x):
    return jnp.maximum(x, 0.0) + jnp.log(1.0 + jnp.exp(-jnp.abs(x)))


def _iota(shape, axis):
    return lax.broadcasted_iota(jnp.int32, shape, axis)


def _head_indicator(width, heads, transposed=False):
    per = width // heads
    if transposed:
        return (_iota((heads, width), 1) // per == _iota((heads, width), 0)).astype(F32)
    return (_iota((width, heads), 0) // per == _iota((width, heads), 1)).astype(F32)


def _acc(ref, i, val):
    @pl.when(i == 0)
    def _():
        ref[...] = val

    @pl.when(i > 0)
    def _():
        ref[...] += val


def _rs(tile, width, col=0, shift=0, n=None):
    if shift == 0:
        return pl.BlockSpec((tile, width), lambda i: (i, col))
    if shift < 0:
        return pl.BlockSpec((tile, width), lambda i: (jnp.maximum(i - 1, 0), col))
    return pl.BlockSpec((tile, width), lambda i: (jnp.minimum(i + 1, n - 1), col))


def _ps(shape):
    return pl.BlockSpec(shape, lambda i: (0,) * len(shape))


def _call(body, name, grid, in_specs, out_specs, out_shape, scratch=(), sem=None):
    return pl.pallas_call(
        body, name=name, grid=grid, in_specs=in_specs, out_specs=out_specs, out_shape=out_shape,
        scratch_shapes=list(scratch),
        compiler_params=pltpu.CompilerParams(dimension_semantics=sem, vmem_limit_bytes=VMEM_LIMIT))


def _row_tile(n, target):
    t = min(n, target)
    assert n % t == 0, (n, t)
    return t


def _pick_tile(dim, target):
    if dim <= target:
        return dim
    t = (int(1.4 * target) // LANE) * LANE
    while t >= LANE:
        if dim % t == 0:
            return t
        t -= LANE
    return dim


ANY = pl.BlockSpec(memory_space=pl.ANY)


def _paired(j):
    return (j % 2) * 2 + j // 2


class ColSharded:
    def __init__(self, arr, paired=False):
        self.arr, self.paired = arr, paired
        self.nch, self.rows, self.per = arr.shape
        self.shape = (self.rows, self.nch * self.per)

    def chip(self, j):
        return _paired(j) if self.paired else j


class Slot:
    def __init__(self, buf, kind, per, off, c0=0, paired=False):
        self.buf, self.kind, self.per, self.off, self.c0, self.paired = buf, kind, per, off, c0, paired

    def chip(self, j):
        return _paired(j) if self.paired else j


def matmul(a, b, mode, name, *, out_dtype=F32, scale=None, res=None, bias=None, into=None, tm=1024, tn=1024, tk=1024):
    bshape = b.shape
    if mode == "nn":
        (m, k), (k2, n) = a.shape, bshape
    elif mode == "nt":
        (m, k), (n, k2) = a.shape, bshape
    else:
        (k, m), (k2, n) = a.shape, bshape
    assert k == k2, (a.shape, bshape, mode)
    if mode == "tn":
        tk = 2 * tk
    tm, tn, tk = _pick_tile(m, tm), _pick_tile(n, tn), _pick_tile(k, tk)
    if isinstance(b, ColSharded):
        if mode == "nn":
            tn = b.per
        else:
            assert mode == "nt"
            tk = b.per
    if into is not None:
        if into.kind == "c":
            tn = into.per
            assert into.off % tm == 0 and n == N_CHIPS * into.per
        else:
            tm = max(1, min(m, int(1.4 * 1024)) // into.per) * into.per
            assert m % tm == 0 and into.off % into.per == 0 and into.c0 % (tm // into.per) == 0
    nk = k // tk
    dims = {"nn": NN, "nt": NT, "tn": TN}[mode]
    a_spec = (pl.BlockSpec((tk, tm), lambda i, j, kk: (kk, i)) if mode == "tn"
              else pl.BlockSpec((tm, tk), lambda i, j, kk: (i, kk)))
    if isinstance(b, ColSharded):
        bchip = b.chip
        b_spec = (pl.BlockSpec((None, tk, tn), lambda i, j, kk: (bchip(j), kk, 0)) if mode == "nn"
                  else pl.BlockSpec((None, tn, tk), lambda i, j, kk: (bchip(kk), j, 0)))
        b = b.arr
    else:
        b_spec = (pl.BlockSpec((tn, tk), lambda i, j, kk: (j, kk)) if mode == "nt"
                  else pl.BlockSpec((tk, tn), lambda i, j, kk: (kk, j)))
    plain_o = pl.BlockSpec((tm, tn), lambda i, j, kk: (i, j))
    ins, in_specs = [a, b], [a_spec, b_spec]
    if bias is not None:
        ins.append(bias)
        in_specs.append(pl.BlockSpec((1, tn), lambda i, j, kk: (0, j)))
    if res is not None:
        ins.append(res)
        in_specs.append(plain_o)
    aliases = {}
    if into is None:
        o_spec, o_shape = plain_o, S((m, n), out_dtype)
    else:
        aliases = {len(ins): 0}
        ins.append(into.buf)
        in_specs.append(ANY)
        o_shape = S(into.buf.shape, into.buf.dtype)
        if into.kind == "c":
            ob, ochip = into.off // tm, into.chip
            o_spec = pl.BlockSpec((None, tm, tn), lambda i, j, kk: (ochip(j), ob + i, 0))
        else:
            q, ob = tm // into.per, into.off // into.per
            cb = into.c0 // q
            o_spec = pl.BlockSpec((q, into.per, tn), lambda i, j, kk: (cb + i, ob, j))

    def body(*refs):
        a_ref, b_ref = refs[0], refs[1]
        o_ref, acc_ref = refs[-2], refs[-1]
        kk = pl.program_id(2)

        @pl.when(kk == 0)
        def _():
            acc_ref[...] = jnp.zeros_like(acc_ref)

        acc_ref[...] += _mm(a_ref[...], b_ref[...], dims)

        @pl.when(kk == nk - 1)
        def _():
            out = acc_ref[...]
            if scale is not None:
                out = out * scale
            pos = 2
            if bias is not None:
                out = out + refs[pos][...]
                pos += 1
            if res is not None:
                out = out + refs[pos][...]
            o_ref[...] = out.astype(o_ref.dtype).reshape(o_ref.shape)

    return pl.pallas_call(
        body, name=name, grid=(m // tm, n // tn, nk), in_specs=in_specs, out_specs=o_spec, out_shape=o_shape,
        scratch_shapes=[pltpu.VMEM((tm, tn), F32)], input_output_aliases=aliases,
        compiler_params=pltpu.CompilerParams(dimension_semantics=("parallel", "parallel", "arbitrary"),
                                             vmem_limit_bytes=VMEM_LIMIT))(*ins)


def rms_matmul(h, g, b, mode, name, bias=None):
    n, d = h.shape
    sharded = isinstance(b, ColSharded)
    n_out = b.shape[1] if mode == "nn" else b.shape[0]
    tm = _row_tile(n, 1024)
    tn = b.per if sharded else _pick_tile(n_out, 1024)
    if sharded:
        assert mode == "nn"
        bchip = b.chip
        b_spec = pl.BlockSpec((None, d, tn), lambda i, j: (bchip(j), 0, 0))
        b = b.arr
    elif mode == "nn":
        b_spec = pl.BlockSpec((d, tn), lambda i, j: (0, j))
    else:
        b_spec = pl.BlockSpec((tn, d), lambda i, j: (j, 0))
    ins = [h, g, b] + ([bias] if bias is not None else [])
    in_specs = [pl.BlockSpec((tm, d), lambda i, j: (i, 0)), pl.BlockSpec((1, d), lambda i, j: (0, 0)), b_spec]
    if bias is not None:
        in_specs.append(pl.BlockSpec((1, tn), lambda i, j: (0, j)))

    def body(h_ref, g_ref, b_ref, *refs):
        o_ref, xn_ref = refs[-2:]
        x = h_ref[...]
        r = lax.rsqrt(jnp.mean(x * x, axis=-1, keepdims=True) + EPS)
        xn = (x * r * g_ref[...]).astype(xn_ref.dtype)

        @pl.when(pl.program_id(1) == 0)
        def _():
            xn_ref[...] = xn

        out = _mm(xn, b_ref[...], NN if mode == "nn" else NT)
        o_ref[...] = out if bias is None else out + refs[0][...]

    return pl.pallas_call(
        body, name=name, grid=(n // tm, n_out // tn), in_specs=in_specs,
        out_specs=[pl.BlockSpec((tm, tn), lambda i, j: (i, j)), pl.BlockSpec((tm, d), lambda i, j: (i, 0))],
        out_shape=[S((n, n_out), F32), S((n, d), MXU_DTYPE)],
        compiler_params=pltpu.CompilerParams(dimension_semantics=("parallel", "arbitrary"), vmem_limit_bytes=VMEM_LIMIT),
    )(*ins)


def _rms_bwd_math(x, g, dy):
    r = lax.rsqrt(jnp.mean(x * x, axis=-1, keepdims=True) + EPS)
    xh = x * r
    dg = jnp.sum(dy * xh, axis=0, keepdims=True)
    dxh = dy * g
    dx = r * (dxh - xh * jnp.mean(dxh * xh, axis=-1, keepdims=True))
    return dx, dg


def nt_rms_bwd(a, b, h, g, dh_in, name, extra=None, colsum=False, b_kd=False):
    n, k = a.shape
    d = h.shape[1]
    tm = _row_tile(n, 1024)
    sharded = isinstance(b, ColSharded)
    tk = b.per if sharded else _pick_tile(k, 1024)
    nk = k // tk
    dims = NN if b_kd else NT
    if sharded:
        bchip = b.chip
        b_spec = pl.BlockSpec((None, d, tk), lambda i, kk: (bchip(kk), 0, 0))
        b = b.arr
    elif b_kd:
        b_spec = pl.BlockSpec((tk, d), lambda i, kk: (kk, 0))
    else:
        b_spec = pl.BlockSpec((d, tk), lambda i, kk: (0, kk))
    row = pl.BlockSpec((tm, d), lambda i, kk: (i, 0))
    vec = pl.BlockSpec((1, d), lambda i, kk: (0, 0))
    ins, in_specs = [a, b, h, g, dh_in], [pl.BlockSpec((tm, tk), lambda i, kk: (i, kk)), b_spec, row, vec, row]
    if extra is not None:
        k2 = extra[0].shape[1]
        ins += list(extra)
        in_specs += [pl.BlockSpec((tm, k2), lambda i, kk: (i, 0)),
                     pl.BlockSpec((k2, d) if b_kd else (d, k2), lambda i, kk: (0, 0))]
    n_in = len(ins)

    def body(*refs):
        a_ref, b_ref, h_ref, g_ref, dh_ref = refs[:5]
        o_ref, dg_ref = refs[n_in], refs[n_in + 1]
        acc_ref = refs[-1]
        i, kk = pl.program_id(0), pl.program_id(1)

        @pl.when(kk == 0)
        def _():
            acc_ref[...] = _mm(refs[5][...], refs[6][...], dims) if extra is not None else jnp.zeros_like(acc_ref)

        acc_ref[...] += _mm(a_ref[...], b_ref[...], dims)

        @pl.when(kk == nk - 1)
        def _():
            dx, dg = _rms_bwd_math(h_ref[...], g_ref[...], acc_ref[...])
            out = dh_ref[...] + dx
            o_ref[...] = out
            _acc(dg_ref, i, dg)
            if colsum:
                _acc(refs[n_in + 2], i, jnp.sum(out, axis=0, keepdims=True))

    n_vec = 2 if colsum else 1
    return pl.pallas_call(
        body, name=name, grid=(n // tm, nk), in_specs=in_specs, out_specs=[row] + [vec] * n_vec,
        out_shape=[S((n, d), F32)] + [S((1, d), F32)] * n_vec, scratch_shapes=[pltpu.VMEM((tm, d), F32)],
        compiler_params=pltpu.CompilerParams(dimension_semantics=("arbitrary", "arbitrary"), vmem_limit_bytes=VMEM_LIMIT),
    )(*ins)


def swiglu_in(h, g, w_in, name):
    n, d = h.shape
    per = w_in.per
    nj = w_in.nch // 2
    tile = _row_tile(n, 1024)

    def body(h_ref, g_ref, wg_ref, wu_ref, xn_ref, u_ref, hm_ref):
        x = h_ref[...]
        r = lax.rsqrt(jnp.mean(x * x, axis=-1, keepdims=True) + EPS)
        xn = (x * r * g_ref[...]).astype(xn_ref.dtype)

        @pl.when(pl.program_id(1) == 0)
        def _():
            xn_ref[...] = xn

        a = _mm(xn, wg_ref[...])
        b = _mm(xn, wu_ref[...])
        u_ref[:, :per] = a.astype(u_ref.dtype)
        u_ref[:, per:] = b.astype(u_ref.dtype)
        hm_ref[...] = (a * _sigmoid(a) * b).astype(hm_ref.dtype)

    return pl.pallas_call(
        body, name=name, grid=(n // tile, nj),
        in_specs=[pl.BlockSpec((tile, d), lambda i, j: (i, 0)), pl.BlockSpec((1, d), lambda i, j: (0, 0)),
                  pl.BlockSpec((None, d, per), lambda i, j: (j, 0, 0)), pl.BlockSpec((None, d, per), lambda i, j: (nj + j, 0, 0))],
        out_specs=[pl.BlockSpec((tile, d), lambda i, j: (i, 0)), pl.BlockSpec((tile, 2 * per), lambda i, j: (i, j)),
                   pl.BlockSpec((tile, per), lambda i, j: (i, j))],
        out_shape=[S((n, d), MXU_DTYPE), S((n, 2 * nj * per), MXU_DTYPE), S((n, nj * per), MXU_DTYPE)],
        compiler_params=pltpu.CompilerParams(dimension_semantics=("parallel", "arbitrary"), vmem_limit_bytes=VMEM_LIMIT),
    )(h, g, w_in.arr, w_in.arr)


def swiglu_out_bwd(dh, w_out, u, after, name):
    n, d = dh.shape
    f = w_out.shape[0]
    per = u.shape[1] // 4
    nj = f // per
    tile = _row_tile(n, 1024)

    def body(dh_ref, w_ref, u_ref, after_ref, du_ref):
        dm = 0.5 * _mm(dh_ref[...], w_ref[...], NT)
        a = u_ref[:, :per].astype(F32)
        b = u_ref[:, per:].astype(F32)
        s = _sigmoid(a)
        du_ref[:, :per] = (dm * b * s * (1.0 + a * (1.0 - s))).astype(du_ref.dtype)
        du_ref[:, per:] = (dm * a * s).astype(du_ref.dtype)

    return pl.pallas_call(
        body, name=name, grid=(n // tile, nj),
        in_specs=[pl.BlockSpec((tile, d), lambda i, j: (i, 0)), pl.BlockSpec((per, d), lambda i, j: (j, 0)),
                  pl.BlockSpec((tile, 2 * per), lambda i, j: (i, j)), ANY],
        out_specs=pl.BlockSpec((tile, 2 * per), lambda i, j: (i, j)),
        out_shape=S(u.shape, MXU_DTYPE),
        compiler_params=pltpu.CompilerParams(dimension_semantics=("parallel", "parallel"), vmem_limit_bytes=VMEM_LIMIT),
    )(dh, w_out, u, after)


def ple_fwd(h, g, w_gate, pp, name):
    n, d = h.shape
    tile = _row_tile(n, 512)

    def body(h_ref, g_ref, w_ref, pp_ref, o_ref, gl_ref, xn_ref):
        x = h_ref[...]
        r = lax.rsqrt(jnp.mean(x * x, axis=-1, keepdims=True) + EPS)
        xn = (x * r * g_ref[...]).astype(xn_ref.dtype)
        xn_ref[...] = xn
        gl = _mm(xn, w_ref[...])
        gl_ref[...] = gl
        o_ref[...] = x + _sigmoid(gl) * pp_ref[...]

    return _call(body, name, (n // tile,), [_rs(tile, d), _ps((1, d)), _ps(w_gate.shape), _rs(tile, d)],
                 [_rs(tile, d)] * 3, [S((n, d), F32), S((n, d), F32), S((n, d), MXU_DTYPE)], sem=("parallel",))(h, g, w_gate, pp)


def ple_bwd(dh, gl, pp, w_gate, h, g, after, name):
    n, d = dh.shape
    tile = _row_tile(n, 512)

    def body(dh_ref, gl_ref, pp_ref, w_ref, h_ref, g_ref, after_ref, o_ref, dpp_ref, dgl_ref, dg_ref):
        i = pl.program_id(0)
        s = _sigmoid(gl_ref[...])
        dh_ = dh_ref[...]
        dpp_ref[...] = (dh_ * s).astype(dpp_ref.dtype)
        dgl = (dh_ * pp_ref[...] * s * (1.0 - s)).astype(dgl_ref.dtype)
        dgl_ref[...] = dgl
        dx, dg = _rms_bwd_math(h_ref[...], g_ref[...], _mm(dgl, w_ref[...], NT))
        o_ref[...] = dh_ + dx
        _acc(dg_ref, i, dg)

    return _call(body, name, (n // tile,),
                 [_rs(tile, d)] * 3 + [_ps(w_gate.shape), _rs(tile, d), _ps((1, d)), ANY],
                 [_rs(tile, d)] * 3 + [_ps((1, d))],
                 [S((n, d), F32), S((n, d), MXU_DTYPE), S((n, d), MXU_DTYPE), S((1, d), F32)],
                 sem=("arbitrary",))(dh, gl, pp, w_gate, h, g, after)


def loss_head(h, g, target, name):
    n, d = h.shape
    tile = _row_tile(n, 512)

    def body(h_ref, g_ref, t_ref, dh_ref, dg_ref, loss_ref):
        i = pl.program_id(0)
        x = h_ref[...]
        gg = g_ref[...]
        r = lax.rsqrt(jnp.mean(x * x, axis=-1, keepdims=True) + EPS)
        err = x * r * gg - t_ref[...]
        part = 0.5 * jnp.sum(jnp.mean(err * err, axis=-1, keepdims=True), axis=0, keepdims=True)
        dx, dg = _rms_bwd_math(x, gg, err * (1.0 / d))
        dh_ref[...] = dx
        _acc(dg_ref, i, dg)
        _acc(loss_ref, i, jnp.broadcast_to(part, (8, LANE)))

    return _call(body, name, (n // tile,), [_rs(tile, d), _ps((1, d)), _rs(tile, d)],
                 [_rs(tile, d), _ps((1, d)), _ps((8, LANE))], [S((n, d), F32), S((1, d), F32), S((8, LANE), F32)],
                 sem=("arbitrary",))(h, g, target)


def _adamw_math(w, g, m, v):
    c1 = np.float32(1.0 - ADAM_B1 ** ADAM_STEP)
    c2 = np.float32(1.0 - ADAM_B2 ** ADAM_STEP)
    mm = ADAM_B1 * m + (1.0 - ADAM_B1) * g
    vv = ADAM_B2 * v + (1.0 - ADAM_B2) * (g * g)
    return -ADAM_LR * ((mm / c1) / (jnp.sqrt(vv / c2) + ADAM_EPS) + ADAM_WD * w), mm, vv


def adamw_layer(w, pack, off, m, v, li, prev, name):
    n, r, c = w.shape

    def body(w_ref, g_ref, m_ref, v_ref, *refs):
        go_ref, d_ref, mo_ref, vo_ref = refs[-4:]
        g = g_ref[...]
        go_ref[...] = g
        d_ref[...], mo_ref[...], vo_ref[...] = _adamw_math(w_ref[...], g, m_ref[...], v_ref[...])

    if r % 8 == 0:
        cap = 2**21 // (4 * c) // 8 * 8
        tile = next(t for t in range(min(cap, r), 7, -8) if r % t == 0 and off % t == 0)
        ob, steps = off // tile, r // tile
        blk = pl.BlockSpec((None, tile, c), lambda i: (li, i, 0))
        g_spec = pl.BlockSpec((tile, c), lambda i: (ob + i, 0))
    else:
        assert off == 0 and pack.shape[0] == r and c % (2 * LANE) == 0
        steps = c // (2 * LANE)
        blk = pl.BlockSpec((None, r, 2 * LANE), lambda i: (li, 0, i))
        g_spec = pl.BlockSpec((r, 2 * LANE), lambda i: (0, i))
    prev = list(prev) if prev is not None else []
    return pl.pallas_call(
        body, name=name, grid=(steps,),
        in_specs=[blk, g_spec, blk, blk] + [ANY] * len(prev),
        out_specs=[blk] * 4, out_shape=[S((n, r, c), F32)] * 4,
        input_output_aliases={4 + j: j for j in range(len(prev))},
        compiler_params=pltpu.CompilerParams(dimension_semantics=("parallel",), vmem_limit_bytes=VMEM_LIMIT),
    )(w, pack, m, v, *prev)


def adamw(w, g, m, v, name):
    r, c = w.shape
    tile = r
    for t in (512, 256, 128, 64, 32, 16, 8):
        if r % t == 0 and t * c * 4 <= 2**21:
            tile = t
            break

    def body(w_ref, g_ref, m_ref, v_ref, d_ref, mo_ref, vo_ref):
        d_ref[...], mo_ref[...], vo_ref[...] = _adamw_math(w_ref[...], g_ref[...], m_ref[...], v_ref[...])

    return _call(body, name, (r // tile,), [_rs(tile, c)] * 4, [_rs(tile, c)] * 3, [S((r, c), F32)] * 3,
                 sem=("parallel",))(w, g, m, v)


TAP_VREGS = 32


def _taps(src, w_ref, offsets, tile, put, bias=None):
    c = src.shape[1]
    rp = max(8, TAP_VREGS * 8 * LANE // c // 8 * 8)
    for r0 in range(0, tile, rp):
        acc = jnp.zeros((rp, c), F32) if bias is None else jnp.zeros((rp, c), F32) + bias
        for k, o in enumerate(offsets):
            acc = acc + w_ref[k:k + 1, :] * src[r0 + o:r0 + o + rp, :]
        put(slice(r0, r0 + rp), acc)


def _taps_fwd(sc, w_ref, width, halo, tile, put, bias):
    _taps(sc, w_ref, [halo - (width - 1) + k for k in range(width)], tile, put, bias)


def _taps_bwd_x(sc_d, w_ref, width, tile, put):
    _taps(sc_d, w_ref, [(width - 1) - k for k in range(width)], tile, put)


def _taps_bwd_w(dy, sc, dw_ref, width, halo, tile, i):
    @pl.when(i == 0)
    def _():
        dw_ref[...] = jnp.zeros_like(dw_ref)

    for k in range(width):
        o = halo - (width - 1) + k
        dw_ref[k:k + 1, :] += jnp.sum(dy * sc[o:o + tile, :], axis=0, keepdims=True)


def _ln_stats(x):
    mu = jnp.mean(x, axis=-1, keepdims=True)
    xc = x - mu
    r = lax.rsqrt(jnp.mean(xc * xc, axis=-1, keepdims=True) + EPS)
    return xc * r, r


def conv_group_fwd(proj, cw, cb, lg, lb, name):
    n = proj.shape[0]
    d = cw.shape[1]
    tile = _row_tile(n, 256)
    halo = 32

    def body(v_ref, g_ref, vp_ref, gp_ref, cw_ref, cb_ref, lg_ref, lb_ref, u_ref, u1_ref, sc):
        i = pl.program_id(0)
        first = (i > 0).astype(F32)
        sc[0:halo, :] = vp_ref[tile - halo:, :] * _sigmoid(gp_ref[tile - halo:, :]) * first
        sc[halo:, :] = v_ref[...] * _sigmoid(g_ref[...])
        def put(rows, acc):
            u1_ref[rows, :] = acc

        _taps_fwd(sc, cw_ref, CONV_WIDTH, halo, tile, put, cb_ref[...])
        xh, _ = _ln_stats(u1_ref[...])
        y = xh * lg_ref[...] + lb_ref[...]
        u_ref[...] = (y * _sigmoid(y)).astype(u_ref.dtype)

    return _call(body, name, (n // tile,),
                 [_rs(tile, d, 0), _rs(tile, d, 1), _rs(tile, d, 0, -1), _rs(tile, d, 1, -1),
                  _ps(cw.shape), _ps((1, d)), _ps((1, d)), _ps((1, d))],
                 [_rs(tile, d), _rs(tile, d)], [S((n, d), MXU_DTYPE), S((n, d), F32)],
                 scratch=[pltpu.VMEM((halo + tile, d), F32)], sem=("arbitrary",))(proj, proj, proj, proj, cw, cb, lg, lb)


def conv_group_bwd(du, u1, proj, cw, lg, lb, name):
    n = proj.shape[0]
    d = cw.shape[1]
    tile = _row_tile(n, 256)
    halo = 32
    nt = n // tile

    def body(du_ref, dun_ref, u1_ref, u1n_ref, v_ref, g_ref, vp_ref, gp_ref, cw_ref, lg_ref, lb_ref,
             dp_ref, dcw_ref, dcb_ref, dlg_ref, dlb_ref, sc, sc_d):
        i = pl.program_id(0)

        def ln_swish_bwd(dy_, u1_):
            xh, r = _ln_stats(u1_)
            y = xh * lg_ref[...] + lb_ref[...]
            s = _sigmoid(y)
            dyy = dy_ * s * (1.0 + y * (1.0 - s))
            dxh = dyy * lg_ref[...]
            dx = r * (dxh - jnp.mean(dxh, axis=-1, keepdims=True) - xh * jnp.mean(dxh * xh, axis=-1, keepdims=True))
            return dx, jnp.sum(dyy * xh, axis=0, keepdims=True), jnp.sum(dyy, axis=0, keepdims=True)

        du1, dlg, dlb = ln_swish_bwd(du_ref[...].astype(F32), u1_ref[...])
        du1n, _, _ = ln_swish_bwd(dun_ref[0:halo, :].astype(F32), u1n_ref[0:halo, :])
        sc_d[0:tile, :] = du1
        sc_d[tile:, :] = du1n * (i < nt - 1).astype(F32)
        sc[0:halo, :] = vp_ref[tile - halo:, :] * _sigmoid(gp_ref[tile - halo:, :]) * (i > 0).astype(F32)
        sc[halo:, :] = v_ref[...] * _sigmoid(g_ref[...])

        def put(rows, du0):
            sig = _sigmoid(g_ref[rows, :])
            dp_ref[rows, :d] = (du0 * sig).astype(dp_ref.dtype)
            dp_ref[rows, d:] = (du0 * v_ref[rows, :] * sig * (1.0 - sig)).astype(dp_ref.dtype)

        _taps_bwd_x(sc_d, cw_ref, CONV_WIDTH, tile, put)
        _taps_bwd_w(du1, sc, dcw_ref, CONV_WIDTH, halo, tile, i)
        _acc(dcb_ref, i, jnp.sum(du1, axis=0, keepdims=True))
        _acc(dlg_ref, i, dlg)
        _acc(dlb_ref, i, dlb)

    return _call(body, name, (nt,),
                 [_rs(tile, d), _rs(tile, d, 0, 1, nt), _rs(tile, d), _rs(tile, d, 0, 1, nt),
                  _rs(tile, d, 0), _rs(tile, d, 1), _rs(tile, d, 0, -1), _rs(tile, d, 1, -1),
                  _ps(cw.shape), _ps((1, d)), _ps((1, d))],
                 [_rs(tile, 2 * d), _ps(cw.shape), _ps((1, d)), _ps((1, d)), _ps((1, d))],
                 [S((n, proj.shape[1]), MXU_DTYPE), S(cw.shape, F32), S((1, d), F32), S((1, d), F32), S((1, d), F32)],
                 scratch=[pltpu.VMEM((halo + tile, d), F32), pltpu.VMEM((tile + halo, d), F32)],
                 sem=("arbitrary",))(du, du, u1, u1, proj, proj, proj, proj, cw, lg, lb)


def ssm_conv_fwd(proj, dtr, sw, sb, dtb, name):
    n = proj.shape[0]
    w = sw.shape[1]
    inner = SSM_HEADS * HEAD_DIM
    tile = _row_tile(n, 256)
    halo = 8

    def body(x_ref, xp_ref, dtr_ref, sw_ref, sb_ref, dtb_ref, pre_ref, xs_ref, bc_ref, dt_ref, sc):
        i = pl.program_id(0)
        sc[0:halo, :] = xp_ref[tile - halo:, :] * (i > 0).astype(F32)
        sc[halo:, :] = x_ref[...]
        def put(rows, acc):
            pre_ref[rows, :] = acc

        _taps_fwd(sc, sw_ref, SSM_CONV, halo, tile, put, sb_ref[...])
        pre = pre_ref[...]
        act = pre * _sigmoid(pre)
        xs_ref[...] = act[:, :inner]
        bc_ref[...] = act[:, inner:]
        dt = _softplus(dtr_ref[...] + dtb_ref[...])
        dt_ref[...] = jnp.where(_iota(dt.shape, 1) < SSM_HEADS, dt, 0.0)

    return _call(body, name, (n // tile,),
                 [_rs(tile, w, 2), _rs(tile, w, 2, -1), _rs(tile, LANE), _ps(sw.shape), _ps((1, w)), _ps((1, LANE))],
                 [_rs(tile, w), _rs(tile, inner), _rs(tile, w - inner), _rs(tile, LANE)],
                 [S((n, w), F32), S((n, inner), F32), S((n, w - inner), F32), S((n, LANE), F32)],
                 scratch=[pltpu.VMEM((halo + tile, w), F32)], sem=("arbitrary",))(proj, proj, dtr, sw, sb, dtb)


def ssm_conv_bwd(dxs, dbc, pre, proj, sw, dproj, name):
    n = proj.shape[0]
    w = sw.shape[1]
    inner = SSM_HEADS * HEAD_DIM
    tile = _row_tile(n, 256)
    halo = 8
    nt = n // tile

    def body(dxs_ref, dxsn_ref, dbc_ref, dbcn_ref, pre_ref, pren_ref, x_ref, xp_ref, sw_ref, dp_in_ref,
             dx_ref, dsw_ref, dsb_ref, sc, sc_d):
        i = pl.program_id(0)

        def silu_bwd(d_, p_):
            s = _sigmoid(p_)
            return d_ * s * (1.0 + p_ * (1.0 - s))

        sc_d[0:tile, :inner] = silu_bwd(dxs_ref[...], pre_ref[:, :inner])
        sc_d[0:tile, inner:] = silu_bwd(dbc_ref[...], pre_ref[:, inner:])
        last = (i < nt - 1).astype(F32)
        sc_d[tile:, :inner] = silu_bwd(dxsn_ref[0:halo, :], pren_ref[0:halo, :inner]) * last
        sc_d[tile:, inner:] = silu_bwd(dbcn_ref[0:halo, :], pren_ref[0:halo, inner:]) * last
        sc[0:halo, :] = xp_ref[tile - halo:, :] * (i > 0).astype(F32)
        sc[halo:, :] = x_ref[...]
        dpre = sc_d[0:tile, :]
        def put(rows, acc):
            dx_ref[rows, :] = acc.astype(dx_ref.dtype)

        _taps_bwd_x(sc_d, sw_ref, SSM_CONV, tile, put)
        _taps_bwd_w(dpre, sc, dsw_ref, SSM_CONV, halo, tile, i)
        _acc(dsb_ref, i, jnp.sum(dpre, axis=0, keepdims=True))

    return pl.pallas_call(
        body, name=name, grid=(nt,),
        in_specs=[_rs(tile, inner), _rs(tile, inner, 0, 1, nt), _rs(tile, w - inner), _rs(tile, w - inner, 0, 1, nt),
                  _rs(tile, w), _rs(tile, w, 0, 1, nt), _rs(tile, w, 2), _rs(tile, w, 2, -1), _ps(sw.shape), ANY],
        out_specs=[_rs(tile, w, 2), _ps(sw.shape), _ps((1, w))],
        out_shape=[S(dproj.shape, dproj.dtype), S(sw.shape, F32), S((1, w), F32)],
        scratch_shapes=[pltpu.VMEM((halo + tile, w), F32), pltpu.VMEM((tile + halo, w), F32)],
        input_output_aliases={9: 0},
        compiler_params=pltpu.CompilerParams(dimension_semantics=("arbitrary",), vmem_limit_bytes=VMEM_LIMIT),
    )(dxs, dxs, dbc, dbc, pre, pre, proj, proj, sw, dproj)


def _ssd_prologue(dt_ref, dtT_ref, al_ref, alc_ref):
    row = _iota((CHUNK, CHUNK), 0)
    col = _iota((CHUNK, CHUNK), 1)
    dt = dt_ref[:, :SSM_HEADS]
    a_row = -jnp.exp(al_ref[:, :SSM_HEADS])
    a_col = -jnp.exp(alc_ref[...])
    cs = _01mm((row >= col).astype(F32), dt * a_row)
    csT = _mm01(dtT_ref[...] * a_col, (row <= col).astype(F32))
    return dt, a_row, cs, csT, row, col


def _decay(cs, csT, h, row, col):
    lm = jnp.exp(jnp.where(row >= col, cs[:, h:h + 1] - csT[h:h + 1, :], -1e30))
    lmT = jnp.exp(jnp.where(col >= row, csT[h:h + 1, :] - cs[:, h:h + 1], -1e30))
    return lm, lmT


def ssd_fwd(xs, bc, dt, dtT, alog_row, alog_col, name):
    n, width = xs.shape
    nc = n // CHUNK
    gw = width // SSM_GROUPS
    hpg = SSM_HEADS // SSM_GROUPS
    ns = SSM_STATE

    def body(xs_ref, bc_ref, dt_ref, dtT_ref, al_ref, alc_ref, y_ref, hs_ref, h_sc):
        i = pl.program_id(0)

        @pl.when(i == 0)
        def _():
            h_sc[...] = jnp.zeros_like(h_sc)

        dt, a_row, cs, csT, row, col = _ssd_prologue(dt_ref, dtT_ref, al_ref, alc_ref)
        indT = _head_indicator(width, SSM_HEADS, transposed=True)
        dt_full = _mm01(dt, indT)
        e_full = jnp.exp(_mm01(cs, indT))
        dte_full = jnp.exp(_mm01(cs[CHUNK - 1:CHUNK, :] - cs, indT))
        xt = xs_ref[...] * dt_full
        hs_ref[0] = h_sc[...]
        lo = _iota((CHUNK, 2 * HEAD_DIM), 1) < HEAD_DIM
        groups = range(SSM_GROUPS)
        bgs = [bc_ref[:, g * ns:(g + 1) * ns] for g in groups]
        cgs = [bc_ref[:, (SSM_GROUPS + g) * ns:(SSM_GROUPS + g + 1) * ns] for g in groups]
        gms = [_mm(cg, bg, NT) for cg, bg in zip(cgs, bgs)]
        yoffs = [e_full[:, g * gw:(g + 1) * gw] * _mm(cgs[g], h_sc[g * gw:(g + 1) * gw, :], NT) for g in groups]
        sgs = [_mm(xt[:, g * gw:(g + 1) * gw] * dte_full[:, g * gw:(g + 1) * gw], bgs[g], TN) for g in groups]
        ms = [gms[h // hpg] * _decay(cs, csT, h, row, col)[0] for h in range(SSM_HEADS)]
        for pr in range(SSM_HEADS // 2):
            c0 = 2 * pr * HEAD_DIM
            xp = xt[:, c0:c0 + 2 * HEAD_DIM]
            yd = jnp.where(lo, _mm(ms[2 * pr], xp), _mm(ms[2 * pr + 1], xp))
            y_ref[:, c0:c0 + 2 * HEAD_DIM] = yd + yoffs[2 * pr // hpg][:, c0 % gw:c0 % gw + 2 * HEAD_DIM]
        for h in range(SSM_HEADS):
            r0 = h * HEAD_DIM
            h_sc[r0:r0 + HEAD_DIM, :] = (h_sc[r0:r0 + HEAD_DIM, :] * jnp.exp(csT[h:h + 1, CHUNK - 1:CHUNK])
                                         + sgs[h // hpg][r0 % gw:r0 % gw + HEAD_DIM, :])

    bcw = bc.shape[1]
    return _call(body, name, (nc,),
                 [_rs(CHUNK, width), _rs(CHUNK, bcw), _rs(CHUNK, LANE), pl.BlockSpec((SSM_HEADS, CHUNK), lambda i: (0, i)),
                  _ps((1, LANE)), _ps((SSM_HEADS, 1))],
                 [_rs(CHUNK, width), pl.BlockSpec((1, width, ns), lambda i: (i, 0, 0))],
                 [S((n, width), F32), S((nc, width, ns), F32)],
                 scratch=[pltpu.VMEM((width, ns), F32)], sem=("arbitrary",))(xs, bc, dt, dtT, alog_row, alog_col)


def ssd_bwd(xs, bc, dt, dtT, alog_row, alog_col, hs, dy, dxs_skip, name):
    n, width = xs.shape
    nc = n // CHUNK
    gw = width // SSM_GROUPS
    hpg = SSM_HEADS // SSM_GROUPS
    ns = SSM_STATE
    bcw = bc.shape[1]

    def body(xs_ref, bc_ref, dt_ref, dtT_ref, al_ref, alc_ref, hs_ref, dy_ref, skip_ref,
             dxs_ref, dbc_ref, ddtr_ref, dal_ref, ddtb_ref, dh_sc, dxt_sc):
        i = pl.program_id(0)

        @pl.when(i == 0)
        def _():
            dh_sc[...] = jnp.zeros_like(dh_sc)

        dt, a_row, cs, csT, row, col = _ssd_prologue(dt_ref, dtT_ref, al_ref, alc_ref)
        indT = _head_indicator(width, SSM_HEADS, transposed=True)
        ind = _head_indicator(width, SSM_HEADS)
        dt_full = _mm01(dt, indT)
        e_full = jnp.exp(_mm01(cs, indT))
        cs_last = cs[CHUNK - 1:CHUNK, :]
        dte = jnp.exp(cs_last - cs)
        dte_full = _mm01(dte, indT)
        xs_ = xs_ref[...]
        xt = xs_ * dt_full
        dy_ = dy_ref[...]
        hmat = hs_ref[0]
        ds = dh_sc[...]
        lo = _iota((CHUNK, 2 * HEAD_DIM), 1) < HEAD_DIM
        head_lane = _iota((1, SSM_HEADS), 1)
        dcs = jnp.zeros((CHUNK, SSM_HEADS), F32)
        ddte = jnp.zeros((CHUNK, SSM_HEADS), F32)
        for g in range(SSM_GROUPS):
            sl = slice(g * gw, (g + 1) * gw)
            bg = bc_ref[:, g * ns:(g + 1) * ns]
            cg = bc_ref[:, (SSM_GROUPS + g) * ns:(SSM_GROUPS + g + 1) * ns]
            gm = _mm(cg, bg, NT)
            gmT = _mm(bg, cg, NT)
            hg = hmat[sl, :]
            dsg = ds[sl, :]
            dyg = dy_[:, sl]
            xtg = xt[:, sl]
            yoff = e_full[:, sl] * _mm(cg, hg, NT)
            edy = e_full[:, sl] * dyg
            bds = _mm(bg, dsg, NT)
            dxt_g = dte_full[:, sl] * bds
            ddte = ddte + _mm01(xtg * bds, ind[sl, :])
            dcs = dcs + _mm01(dyg * yoff, ind[sl, :])
            db = _mm(xtg * dte_full[:, sl], dsg)
            dc = _mm(edy, hg)
            dhc = _mm(edy, cg, TN)
            hs_g = range(g * hpg, (g + 1) * hpg)
            pair = lambda a, h: a[:, (h - g * hpg) // 2 * 2 * HEAD_DIM:((h - g * hpg) // 2 + 1) * 2 * HEAD_DIM]
            decs = [_decay(cs, csT, h, row, col) for h in hs_g]
            xms = [jnp.where(lo if h % 2 == 0 else jnp.logical_not(lo), pair(xtg, h), 0.0) for h in hs_g]
            dms = [_mm(pair(dyg, h), xm, NT) for h, xm in zip(hs_g, xms)]
            dmTs = [_mm(xm, pair(dyg, h), NT) for h, xm in zip(hs_g, xms)]
            mTs = [gmT * lmT for _, lmT in decs]
            for h, (lm, _), dm, dmT, mT in zip(hs_g, decs, dms, dmTs, mTs):
                z = jnp.sum(dm * (gm * lm), axis=1, keepdims=True) - jnp.sum(dmT * mT, axis=1, keepdims=True)
                dcs = dcs + z * (head_lane == h).astype(F32)
            dgs = sum(dm * lm for dm, (lm, _) in zip(dms, decs))
            dgTs = sum(dmT * lmT for dmT, (_, lmT) in zip(dmTs, decs))
            rrs = [_mm(mT, pair(dyg, h)) for h, mT in zip(hs_g, mTs)]
            for pr in range(hpg // 2):
                c0 = 2 * pr * HEAD_DIM
                dxt_sc[:, g * gw + c0:g * gw + c0 + 2 * HEAD_DIM] = (jnp.where(lo, rrs[2 * pr], rrs[2 * pr + 1])
                                                                     + dxt_g[:, c0:c0 + 2 * HEAD_DIM])
            dbc_ref[:, g * ns:(g + 1) * ns] = db + _mm(dgTs, cg)
            dbc_ref[:, (SSM_GROUPS + g) * ns:(SSM_GROUPS + g + 1) * ns] = dc + _mm(dgs, bg)
            for hh in range(hpg):
                h = g * hpg + hh
                r0 = h * HEAD_DIM
                dh_sc[r0:r0 + HEAD_DIM, :] = (dhc[hh * HEAD_DIM:(hh + 1) * HEAD_DIM, :]
                                              + jnp.exp(csT[h:h + 1, CHUNK - 1:CHUNK]) * ds[r0:r0 + HEAD_DIM, :])
        t = ddte * dte
        per_head = jnp.sum(jnp.sum(ds * hmat, axis=1, keepdims=True) * ind, axis=0, keepdims=True)
        last_add = jnp.sum(t, axis=0, keepdims=True) + jnp.exp(cs_last) * per_head
        dcs = dcs - t + jnp.where(_iota((CHUNK, SSM_HEADS), 0) == CHUNK - 1, last_add, 0.0)
        dadt = _01mm((row <= col).astype(F32), dcs)
        dxt = dxt_sc[...]
        ddt = dadt * a_row + _mm01(dxt * xs_, ind)
        dxs_ref[...] = dxt * dt_full + skip_ref[...]
        ddtr = ddt * (1.0 - jnp.exp(-dt))
        ddtr_ref[...] = jnp.zeros_like(ddtr_ref)
        ddtr_ref[:, :SSM_HEADS] = ddtr.astype(ddtr_ref.dtype)
        _acc(dal_ref, i, jnp.sum(dadt * dt, axis=0, keepdims=True) * a_row)
        _acc(ddtb_ref, i, jnp.sum(ddtr, axis=0, keepdims=True))

    rev = lambda i: (nc - 1 - i, 0)
    return _call(body, name, (nc,),
                 [pl.BlockSpec((CHUNK, width), rev), pl.BlockSpec((CHUNK, bcw), rev), pl.BlockSpec((CHUNK, LANE), rev),
                  pl.BlockSpec((SSM_HEADS, CHUNK), lambda i: (0, nc - 1 - i)), _ps((1, LANE)), _ps((SSM_HEADS, 1)),
                  pl.BlockSpec((1, width, ns), lambda i: (nc - 1 - i, 0, 0)), pl.BlockSpec((CHUNK, width), rev),
                  pl.BlockSpec((CHUNK, width), rev)],
                 [pl.BlockSpec((CHUNK, width), rev), pl.BlockSpec((CHUNK, bcw), rev), pl.BlockSpec((CHUNK, LANE), rev),
                  _ps((1, SSM_HEADS)), _ps((1, SSM_HEADS))],
                 [S((n, width), F32), S((n, bcw), F32), S((n, LANE), MXU_DTYPE), S((1, SSM_HEADS), F32), S((1, SSM_HEADS), F32)],
                 scratch=[pltpu.VMEM((width, ns), F32), pltpu.VMEM((CHUNK, width), F32)],
                 sem=("arbitrary",))(xs, bc, dt, dtT, alog_row, alog_col, hs, dy, dxs_skip)


def ssm_gate_fwd(yssd, xs, proj, dfull, gamma, name):
    n, d = yssd.shape
    tile = _row_tile(n, 512)
    gw = d // SSM_GROUPS

    def body(y_ref, xs_ref, z_ref, df_ref, gm_ref, o_ref):
        z = z_ref[...]
        y2 = (y_ref[...] + df_ref[...] * xs_ref[...]) * (z * _sigmoid(z))
        for g in range(SSM_GROUPS):
            yg = y2[:, g * gw:(g + 1) * gw]
            r = lax.rsqrt(jnp.mean(yg * yg, axis=-1, keepdims=True) + EPS)
            o_ref[:, g * gw:(g + 1) * gw] = (yg * r * gm_ref[:, g * gw:(g + 1) * gw]).astype(o_ref.dtype)

    return _call(body, name, (n // tile,), [_rs(tile, d), _rs(tile, d), _rs(tile, d, 2), _ps((1, d)), _ps((1, d))],
                 _rs(tile, d), S((n, d), MXU_DTYPE), sem=("parallel",))(yssd, xs, proj, dfull, gamma)


def ssm_gate_bwd(dy3, yssd, xs, proj, dfull, gamma, dproj, name):
    n, d = yssd.shape
    tile = _row_tile(n, 512)
    gw = d // SSM_GROUPS

    def body(dy_ref, y_ref, xs_ref, z_ref, df_ref, gm_ref, dp_in_ref, dys_ref, dxs_ref, dz_ref, dgm_ref, dd_ref):
        i = pl.program_id(0)
        z = z_ref[...]
        s = _sigmoid(z)
        xs_ = xs_ref[...]
        y1 = y_ref[...] + df_ref[...] * xs_
        y2 = y1 * (z * s)
        dy_ = dy_ref[...].astype(F32)
        dgm = []
        dy2 = []
        for g in range(SSM_GROUPS):
            sl = slice(g * gw, (g + 1) * gw)
            dxg, dgg = _rms_bwd_math(y2[:, sl], gm_ref[:, sl], dy_[:, sl])
            dy2.append(dxg)
            dgm.append(dgg)
        dy2 = jnp.concatenate(dy2, axis=1)
        dy1 = dy2 * (z * s)
        dys_ref[...] = dy1
        dxs_ref[...] = dy1 * df_ref[...]
        dz_ref[...] = (dy2 * y1 * s * (1.0 + z * (1.0 - s))).astype(dz_ref.dtype)
        _acc(dgm_ref, i, jnp.concatenate(dgm, axis=1))
        colsum = jnp.broadcast_to(jnp.sum(dy1 * xs_, axis=0, keepdims=True), (8, d))
        _acc(dd_ref, i, _mm01(colsum, _head_indicator(d, SSM_HEADS))[0:1, :])

    return pl.pallas_call(
        body, name=name, grid=(n // tile,),
        in_specs=[_rs(tile, d), _rs(tile, d), _rs(tile, d), _rs(tile, d, 2), _ps((1, d)), _ps((1, d)), ANY],
        out_specs=[_rs(tile, d), _rs(tile, d), _rs(tile, d, 2), _ps((1, d)), _ps((1, SSM_HEADS))],
        out_shape=[S((n, d), F32), S((n, d), F32), S(dproj.shape, dproj.dtype), S((1, d), F32), S((1, SSM_HEADS), F32)],
        input_output_aliases={6: 2},
        compiler_params=pltpu.CompilerParams(dimension_semantics=("arbitrary",), vmem_limit_bytes=VMEM_LIMIT),
    )(dy3, yssd, xs, proj, dfull, gamma, dproj)


def _rope128(x, cos, sin_signed):
    half = HEAD_DIM // 2
    lane = _iota(x.shape, 1)
    partner = jnp.where((lane % HEAD_DIM) < half, pltpu.roll(x, LANE - half, 1), pltpu.roll(x, half, 1))
    return x * cos + partner * sin_signed


def rope_fwd(qkv, cos, sin, name):
    n, w = qkv.shape
    qw = ATT_HEADS * HEAD_DIM
    kw = ATT_KV_HEADS * HEAD_DIM
    tile = _row_tile(n, 512)

    def body(x_ref, c_ref, s_ref, q_ref, k_ref, v_ref):
        c, s = c_ref[...], s_ref[...]
        for j in range(qw // LANE):
            q_ref[:, j * LANE:(j + 1) * LANE] = _rope128(x_ref[:, j * LANE:(j + 1) * LANE], c, s).astype(q_ref.dtype)
        for j in range(kw // LANE):
            k_ref[:, j * LANE:(j + 1) * LANE] = _rope128(x_ref[:, qw + j * LANE:qw + (j + 1) * LANE], c, s).astype(k_ref.dtype)
        v_ref[...] = x_ref[:, qw + kw:].astype(v_ref.dtype)

    return _call(body, name, (n // tile,), [_rs(tile, w), _rs(tile, LANE), _rs(tile, LANE)],
                 [_rs(tile, qw), _rs(tile, kw), _rs(tile, kw)],
                 [S((n, qw), MXU_DTYPE), S((n, kw), MXU_DTYPE), S((n, kw), MXU_DTYPE)], sem=("parallel",))(qkv, cos, sin)


ATT_GROUP = ATT_HEADS // ATT_KV_HEADS


def _attn_mask(i):
    row = _iota((ATT_GROUP * WINDOW, 2 * WINDOW), 0) % WINDOW
    s = _iota((ATT_GROUP * WINDOW, 2 * WINDOW), 1)
    return (s > row) & (s <= row + WINDOW) & ((s >= WINDOW) | (i > 0))


def _stack_heads(ref, j, kh, lo):
    parts = []
    for t in range(ATT_GROUP):
        h = ATT_GROUP * j + t
        blk = ref[:, (h // 2) * LANE:(h // 2 + 1) * LANE]
        blk = jnp.where(lo if h % 2 == 0 else jnp.logical_not(lo), blk, jnp.zeros_like(blk))
        parts.append(blk if h % 2 == kh else pltpu.roll(blk, HEAD_DIM, 1))
    return jnp.concatenate(parts, axis=0)


def _unstack_heads(stacked, j, kh, lo, put):
    for t in range(0, ATT_GROUP, 2):
        h = ATT_GROUP * j + t
        even = stacked[t * WINDOW:(t + 1) * WINDOW, :]
        odd = stacked[(t + 1) * WINDOW:(t + 2) * WINDOW, :]
        even = even if kh == 0 else pltpu.roll(even, HEAD_DIM, 1)
        odd = odd if kh == 1 else pltpu.roll(odd, HEAD_DIM, 1)
        put(h // 2, jnp.where(lo, even, odd))


def _per_head_rows(ref, j):
    return jnp.concatenate([ref[:, ATT_GROUP * j + t:ATT_GROUP * j + t + 1] for t in range(ATT_GROUP)], axis=0)


def _per_head_scalar(ref, j):
    rows = _iota((ATT_GROUP * WINDOW, 1), 0) // WINDOW
    out = jnp.zeros((ATT_GROUP * WINDOW, 1), F32)
    for t in range(ATT_GROUP):
        out = out + jnp.where(rows == t, ref[:, ATT_GROUP * j + t:ATT_GROUP * j + t + 1], 0.0)
    return out


def attn_fwd(q, k, v, sinks, name):
    n, qw = q.shape
    kw = k.shape[1]
    nb = n // WINDOW
    scale = HEAD_DIM ** -0.5

    def body(q_ref, kc_ref, kp_ref, vc_ref, vp_ref, sk_ref, o_ref, lse_ref):
        i = pl.program_id(0)
        valid = _attn_mask(i)
        lo = _iota((WINDOW, LANE), 1) < HEAD_DIM
        k2 = jnp.concatenate([kp_ref[...], kc_ref[...]], axis=0)
        v2 = jnp.concatenate([vp_ref[...], vc_ref[...]], axis=0)
        lane1 = _iota((1, LANE), 1)
        lse = jnp.zeros((WINDOW, LANE), F32)

        def put_o(qb, val):
            o_ref[:, qb * LANE:(qb + 1) * LANE] = val.astype(o_ref.dtype)

        kv = [(j, j // 2, j % 2) for j in range(ATT_KV_HEADS)]
        logits = [jnp.where(valid, _mm(_stack_heads(q_ref, j, kh, lo), k2[:, kb * LANE:(kb + 1) * LANE], NT) * scale, -1e30)
                  for j, kb, kh in kv]
        sks = [_per_head_scalar(sk_ref, j) for j, _, _ in kv]
        ms = [jnp.maximum(jnp.max(l, axis=-1, keepdims=True), sk) for l, sk in zip(logits, sks)]
        es = [jnp.exp(l - m) for l, m in zip(logits, ms)]
        dens = [jnp.sum(e, axis=-1, keepdims=True) + jnp.exp(sk - m) for e, sk, m in zip(es, sks, ms)]
        for (j, kb, kh), e, m, den in zip(kv, es, ms, dens):
            lse4 = m + jnp.log(den)
            for t in range(ATT_GROUP):
                lse = lse + lse4[t * WINDOW:(t + 1) * WINDOW, :] * (lane1 == ATT_GROUP * j + t).astype(F32)
            _unstack_heads(_mm(e * (1.0 / den), v2[:, kb * LANE:(kb + 1) * LANE]), j, kh, lo, put_o)
        lse_ref[...] = lse

    return _call(body, name, (nb,),
                 [_rs(WINDOW, qw), _rs(WINDOW, kw), _rs(WINDOW, kw, 0, -1), _rs(WINDOW, kw), _rs(WINDOW, kw, 0, -1), _ps((1, LANE))],
                 [_rs(WINDOW, qw), _rs(WINDOW, LANE)], [S((n, qw), MXU_DTYPE), S((n, LANE), F32)],
                 sem=("parallel",))(q, k, k, v, v, sinks)


def attn_bwd(q, k, v, o, do, lse, sinks, name):
    n, qw = q.shape
    kw = k.shape[1]
    nb = n // WINDOW
    scale = HEAD_DIM ** -0.5

    def body(q_ref, kc_ref, kp_ref, vc_ref, vp_ref, o_ref, do_ref, lse_ref, sk_ref,
             dq_ref, dka_ref, dkb_ref, dva_ref, dvb_ref, dsk_ref):
        i = pl.program_id(0)
        valid = _attn_mask(i)
        lo = _iota((WINDOW, LANE), 1) < HEAD_DIM
        k2 = jnp.concatenate([kp_ref[...], kc_ref[...]], axis=0)
        v2 = jnp.concatenate([vp_ref[...], vc_ref[...]], axis=0)
        lane1 = _iota((1, LANE), 1)
        do_ = do_ref[...].astype(F32)
        delta = _mm01(do_ * o_ref[...].astype(F32), _head_indicator(qw, ATT_HEADS))
        dk2 = [jnp.zeros((2 * WINDOW, LANE), F32) for _ in range(kw // LANE)]
        dv2 = [jnp.zeros((2 * WINDOW, LANE), F32) for _ in range(kw // LANE)]
        dsk = jnp.zeros((1, LANE), F32)

        def put_dq(qb, val):
            dq_ref[:, qb * LANE:(qb + 1) * LANE] = val

        kv = [(j, j // 2, j % 2) for j in range(ATT_KV_HEADS)]
        q4s = [_stack_heads(q_ref, j, kh, lo) for j, _, kh in kv]
        do4s = [_stack_heads(do_ref, j, kh, lo) for j, _, kh in kv]
        kks = [k2[:, kb * LANE:(kb + 1) * LANE] for _, kb, _ in kv]
        vvs = [v2[:, kb * LANE:(kb + 1) * LANE] for _, kb, _ in kv]
        lses = [_per_head_rows(lse_ref, j) for j, _, _ in kv]
        dls = [jnp.concatenate([delta[:, ATT_GROUP * j + t:ATT_GROUP * j + t + 1] for t in range(ATT_GROUP)], axis=0)
               for j, _, _ in kv]
        ps = [jnp.exp(jnp.where(valid, _mm(q4, kk, NT) * scale, -1e30) - lse4) for q4, kk, lse4 in zip(q4s, kks, lses)]
        dss = [p * (_mm(do4, vv, NT) - dl) * scale for p, do4, vv, dl in zip(ps, do4s, vvs, dls)]
        for (j, kb, kh), q4, do4, kk, lse4, dl, p, ds in zip(kv, q4s, do4s, kks, lses, dls, ps, dss):
            sd = jnp.exp(_per_head_scalar(sk_ref, j) - lse4) * dl
            for t in range(ATT_GROUP):
                dsk = dsk - (jnp.sum(sd[t * WINDOW:(t + 1) * WINDOW, :], axis=0, keepdims=True)
                             * (lane1 == ATT_GROUP * j + t).astype(F32))
            _unstack_heads(_mm(ds, kk), j, kh, lo, put_dq)
            dk2[kb] = dk2[kb] + _mm(ds, q4, TN)
            dv2[kb] = dv2[kb] + _mm(p, do4, TN)
        for kb in range(kw // LANE):
            dkb_ref[:, kb * LANE:(kb + 1) * LANE] = dk2[kb][0:WINDOW, :]
            dka_ref[:, kb * LANE:(kb + 1) * LANE] = dk2[kb][WINDOW:, :]
            dvb_ref[:, kb * LANE:(kb + 1) * LANE] = dv2[kb][0:WINDOW, :]
            dva_ref[:, kb * LANE:(kb + 1) * LANE] = dv2[kb][WINDOW:, :]
        _acc(dsk_ref, i, dsk)

    return _call(body, name, (nb,),
                 [_rs(WINDOW, qw), _rs(WINDOW, kw), _rs(WINDOW, kw, 0, -1), _rs(WINDOW, kw), _rs(WINDOW, kw, 0, -1),
                  _rs(WINDOW, qw), _rs(WINDOW, qw), _rs(WINDOW, LANE), _ps((1, LANE))],
                 [_rs(WINDOW, qw)] + [_rs(WINDOW, kw)] * 4 + [_ps((1, LANE))],
                 [S((n, qw), F32)] + [S((n, kw), F32)] * 4 + [S((1, LANE), F32)],
                 sem=("arbitrary",))(q, k, k, v, v, o, do, lse, sinks)


def attn_grad_merge(dq, dka, dkb, dva, dvb, cos, sin, name):
    n, qw = dq.shape
    kw = dka.shape[1]
    nb = n // WINDOW
    w = qw + 2 * kw

    def body(dq_ref, dka_ref, dkb_ref, dva_ref, dvb_ref, c_ref, s_ref, o_ref, db_ref):
        i = pl.program_id(0)
        c, s = c_ref[...], -s_ref[...]
        nxt = (i < nb - 1).astype(F32)

        @pl.when(i == 0)
        def _():
            db_ref[...] = jnp.zeros_like(db_ref)

        def put(c0, val):
            o_ref[:, c0:c0 + val.shape[1]] = val.astype(o_ref.dtype)
            db_ref[:, c0:c0 + val.shape[1]] += jnp.sum(val, axis=0, keepdims=True)

        for j in range(qw // LANE):
            put(j * LANE, _rope128(dq_ref[:, j * LANE:(j + 1) * LANE], c, s))
        for j in range(kw // LANE):
            sl = slice(j * LANE, (j + 1) * LANE)
            put(qw + j * LANE, _rope128(dka_ref[:, sl] + dkb_ref[:, sl] * nxt, c, s))
        put(qw + kw, dva_ref[...] + dvb_ref[...] * nxt)

    return _call(body, name, (nb,),
                 [_rs(WINDOW, qw), _rs(WINDOW, kw), _rs(WINDOW, kw, 0, 1, nb), _rs(WINDOW, kw), _rs(WINDOW, kw, 0, 1, nb),
                  _rs(WINDOW, LANE), _rs(WINDOW, LANE)],
                 [_rs(WINDOW, w), _ps((1, w))], [S((n, w), MXU_DTYPE), S((1, w), F32)],
                 sem=("arbitrary",))(dq, dka, dkb, dva, dvb, cos, sin)


def _row(v):
    return v.reshape(1, -1)


def _pad_lanes(v, width=LANE):
    return jnp.pad(v.reshape(1, -1), ((0, 0), (0, width - v.size)))


class LayerWeights(dict):
    def __init__(self, small, fetch):
        super().__init__(small)
        self.fetch = fetch

    def need(self, k, after):
        if k not in self:
            self[k] = self.fetch(k, after)
        return self[k]


def ffn_fwd(h, g, w, keys, tag):
    xn, u, hm = swiglu_in(h, _row(g), w.need(keys[0], h), f"{tag}_in")
    return matmul(hm, w.need(keys[1], hm), "nn", f"{tag}_out", scale=0.5, res=h), (h, xn, u, hm)


class GradSink:
    ORDER = ("ffn1_w_out", "ffn2_w_out", "ple_gate_w", "att_w_o", "hyb_w_out", "ffn1_w_in", "ffn2_w_in", "att_w_qkv",
             "ple_proj_w", "hyb_w_in")

    def __init__(self, shard_shapes, bucket_of):
        self.where, rows = {}, {}
        for k in self.ORDER:
            n, r, c = shard_shapes[k]
            for li in range(n):
                layer = li if k in PER_LAYER else 2 * li + (0 if k in EVEN_ONLY else 1)
                rows_b = rows.setdefault(bucket_of(layer, _stage(k)), {})
                key = c if r % 32 == 0 else k
                off = -(-rows_b.get(key, (0, c))[0] // r) * r
                rows_b[key] = (-(-(off + r) // 32) * 32, c)
                self.where[k, li] = (bucket_of(layer, _stage(k)), key, "r" if _shard_axis(k) == 1 else "c", off, r)
        self.bufs = {b: {key: lax.empty((N_CHIPS, r, c), MXU_DTYPE) for key, (r, c) in rows_b.items()}
                     for b, rows_b in rows.items()}

    def mm(self, k, li, a, b, name, scale=None, c0=0, paired=False):
        bucket, key, kind, off, r = self.where[k, li]
        buf = self.bufs[bucket][key]
        slot = Slot(buf, kind, r if kind == "r" else buf.shape[2], off, c0, paired)
        self.bufs[bucket][key] = matmul(a, b, "tn", name, scale=scale, into=slot)

    def put(self, k, li, chip_major):
        b, key = self.where[k, li][:2]
        pad = self.bufs[b][key].shape[1] - chip_major.shape[1]
        self.bufs[b][key] = jnp.pad(chip_major.astype(self.bufs[b][key].dtype), ((0, 0), (0, pad), (0, 0)))


def ffn_bwd(dh, g, w_in, w_out, saved, tag, sink, keys, layer, after, colsum=False):
    h, xn, u, hm = saved
    sink.mm(keys[1], layer, hm, dh, f"{tag}_dwout", scale=0.5)
    du = swiglu_out_bwd(dh, w_out, u, after, f"{tag}_dhm")
    sink.mm(keys[0], layer, xn, du, f"{tag}_dwin", paired=True)
    outs = nt_rms_bwd(du, ColSharded(w_in.arr, paired=True), h, _row(g), dh, f"{tag}_dxn", colsum=colsum)
    return (outs[0], outs[1].reshape(-1)) + ((outs[2],) if colsum else ())


def _hyb_params(w):
    d = w["conv_dw_b"].size
    inner = SSM_HEADS * HEAD_DIM
    main = 3 * d + w["ssm_conv_b"].size
    return dict(
        w_main=w["hyb_w_in"][:main], w_dt=jnp.pad(w["hyb_w_in"][main:], ((0, LANE - SSM_HEADS), (0, 0))),
        cw=jnp.pad(w["conv_dw_w"], ((0, 32 - CONV_WIDTH), (0, 0))), cb=_row(w["conv_dw_b"]),
        lg=_row(w["conv_ln_g"]), lb=_row(w["conv_ln_b"]),
        sw=jnp.pad(w["ssm_conv_w"], ((0, 8 - SSM_CONV), (0, 0))), sb=_row(w["ssm_conv_b"]),
        dtb=_pad_lanes(w["ssm_dt_bias"]), al_row=_pad_lanes(w["ssm_a_log"]), al_col=w["ssm_a_log"].reshape(-1, 1),
        dfull=_row(jnp.repeat(w["ssm_d"], HEAD_DIM)), gamma=_row(w["ssm_norm"]), d=d, inner=inner, main=main)


def hyb_fwd(h, w, tag):
    w.need("hyb_w_in", h)
    q = _hyb_params(w)
    proj, xn = rms_matmul(h, _row(w["norm_mix"]), q["w_main"], "nt", f"{tag}_in")
    dtr = matmul(xn, q["w_dt"], "nt", f"{tag}_in_dt")
    u, u1 = conv_group_fwd(proj, q["cw"], q["cb"], q["lg"], q["lb"], f"{tag}_conv")
    pre, xs, bc, dt = ssm_conv_fwd(proj, dtr, q["sw"], q["sb"], q["dtb"], f"{tag}_sconv")
    dtT = dt[:, :SSM_HEADS].T
    yssd, hs = ssd_fwd(xs, bc, dt, dtT, q["al_row"], q["al_col"], f"{tag}_ssd")
    y = ssm_gate_fwd(yssd, xs, proj, q["dfull"], q["gamma"], f"{tag}_gate")
    wo = w.need("hyb_w_out", u)
    h2 = matmul(u, wo[:q["d"]], "nn", f"{tag}_out_a", res=h)
    h2 = matmul(y, wo[q["d"]:], "nn", f"{tag}_out_b", res=h2)
    return h2, (h, xn, proj, u, u1, pre, xs, bc, dt, dtT, yssd, hs, y)


def hyb_bwd(dh, w, saved, tag, sink, layer):
    q = _hyb_params(w)
    h, xn, proj, u, u1, pre, xs, bc, dt, dtT, yssd, hs, y = saved
    du = matmul(dh, w["hyb_w_out"][:q["d"]], "nt", f"{tag}_du")
    dy3 = matmul(dh, w["hyb_w_out"][q["d"]:], "nt", f"{tag}_dy")
    sink.mm("hyb_w_out", layer, u, dh, f"{tag}_dwo_a", c0=0)
    sink.mm("hyb_w_out", layer, y, dh, f"{tag}_dwo_b", c0=N_CHIPS // 2)
    dproj, dcw, dcb, dlg, dlb = conv_group_bwd(du, u1, proj, q["cw"], q["lg"], q["lb"], f"{tag}_dconv")
    dyssd, dxs_skip, dproj, dgamma, dd = ssm_gate_bwd(dy3, yssd, xs, proj, q["dfull"], q["gamma"], dproj, f"{tag}_dgate")
    dxs, dbc, ddtr, dalog, ddtb = ssd_bwd(xs, bc, dt, dtT, q["al_row"], q["al_col"], hs, dyssd, dxs_skip, f"{tag}_dssd")
    dproj, dsw, dsb = ssm_conv_bwd(dxs, dbc, pre, proj, q["sw"], dproj, f"{tag}_dsconv")
    dw_in = jnp.concatenate([matmul(dproj, xn, "tn", f"{tag}_dwin"),
                             matmul(ddtr, xn, "tn", f"{tag}_dwin_dt")[:SSM_HEADS]], axis=0)
    sink.put("hyb_w_in", layer, dw_in.reshape((N_CHIPS, -1) + dw_in.shape[1:]))
    dh2, dg = nt_rms_bwd(dproj, q["w_main"], h, _row(w["norm_mix"]), dh, f"{tag}_dxn", extra=(ddtr, q["w_dt"]), b_kd=True)
    grads = dict(norm_mix=dg.reshape(-1), conv_dw_w=dcw[:CONV_WIDTH], conv_dw_b=dcb.reshape(-1),
                 conv_ln_g=dlg.reshape(-1), conv_ln_b=dlb.reshape(-1), ssm_conv_w=dsw[:SSM_CONV], ssm_conv_b=dsb.reshape(-1),
                 ssm_dt_bias=ddtb.reshape(-1), ssm_a_log=dalog.reshape(-1), ssm_d=dd.reshape(-1), ssm_norm=dgamma.reshape(-1))
    return dh2, grads


def rope_tables(n):
    half = HEAD_DIM // 2
    inv = ROPE_THETA ** (-jnp.arange(0, HEAD_DIM, 2, dtype=F32) / HEAD_DIM)
    ang = jnp.arange(n, dtype=F32)[:, None] * inv[None, :]
    cos, sin = jnp.cos(ang), jnp.sin(ang)
    reps = LANE // HEAD_DIM
    return jnp.tile(jnp.concatenate([cos, cos], axis=1), (1, reps)), jnp.tile(jnp.concatenate([-sin, sin], axis=1), (1, reps))


def att_fwd(h, w, tables, tag):
    cos, sin = tables
    qkv, xn = rms_matmul(h, _row(w["norm_mix"]), w.need("att_w_qkv", h), "nn", f"{tag}_qkv", bias=_row(w["att_b_qkv"]))
    q, k, v = rope_fwd(qkv, cos, sin, f"{tag}_rope")
    sinks = _pad_lanes(w["att_sinks"])
    o, lse = attn_fwd(q, k, v, sinks, f"{tag}_attn")
    h2 = matmul(o, w.need("att_w_o", o), "nn", f"{tag}_o", bias=_row(w["att_b_o"]), res=h)
    return h2, (h, xn, q, k, v, o, lse, sinks)


def att_bwd(dh, dh_colsum, w, saved, tables, tag, sink, layer):
    cos, sin = tables
    h, xn, q, k, v, o, lse, sinks = saved
    do = matmul(dh, w["att_w_o"], "nt", f"{tag}_do")
    sink.mm("att_w_o", layer, o, dh, f"{tag}_dwo")
    dq, dka, dkb, dva, dvb, dsk = attn_bwd(q, k, v, o, do, lse, sinks, f"{tag}_dattn")
    dqkv, dbqkv = attn_grad_merge(dq, dka, dkb, dva, dvb, cos, sin, f"{tag}_drope")
    sink.mm("att_w_qkv", layer, xn, dqkv, f"{tag}_dwqkv")
    dh2, dg = nt_rms_bwd(dqkv, w["att_w_qkv"], h, _row(w["norm_mix"]), dh, f"{tag}_dxn")
    grads = dict(norm_mix=dg.reshape(-1), att_b_qkv=dbqkv.reshape(-1), att_sinks=dsk[0, :ATT_HEADS],
                 att_b_o=dh_colsum.reshape(-1))
    return dh2, grads


def ple_block_fwd(h, pe, w, tag):
    pp = matmul(pe, w.need("ple_proj_w", h), "nn", f"{tag}_proj")
    out, gl, xn = ple_fwd(h, _row(w["ple_norm"]), w.need("ple_gate_w", h), pp, f"{tag}_gate")
    return out, (h, xn, gl, pp, pe)


def ple_block_bwd(dh, w, saved, tag, sink, layer, after):
    h, xn, gl, pp, pe = saved
    dh2, dpp, dgl, dg = ple_bwd(dh, gl, pp, w["ple_gate_w"], h, _row(w["ple_norm"]), after, f"{tag}_dgate")
    sink.mm("ple_proj_w", layer, pe, dpp, f"{tag}_dwp")
    sink.mm("ple_gate_w", layer, xn, dgl, f"{tag}_dwg")
    return dh2, dict(ple_norm=dg.reshape(-1))


PER_LAYER = ("norm_ffn1", "ffn1_w_in", "ffn1_w_out", "norm_mix", "norm_ffn2", "ffn2_w_in", "ffn2_w_out",
             "ple_norm", "ple_gate_w", "ple_proj_w")
EVEN_ONLY = ("hyb_w_in", "conv_dw_w", "conv_dw_b", "conv_ln_g", "conv_ln_b", "ssm_conv_w", "ssm_conv_b",
             "ssm_dt_bias", "ssm_a_log", "ssm_d", "ssm_norm", "hyb_w_out")
ODD_ONLY = ("att_w_qkv", "att_b_qkv", "att_sinks", "att_w_o", "att_b_o")


def _layer_index(k, i):
    if k in PER_LAYER:
        return i
    if k in (EVEN_ONLY if i % 2 == 0 else ODD_ONLY):
        return i // 2
    return None


def _stage(k):
    return 0 if k.startswith("ffn1") else (2 if k.startswith(("ffn2", "ple")) else 1)


def trunk_fwd_bwd(x, pe, target, layers, final_norm, sink, stage_done):
    depth = len(layers)
    tables = rope_tables(x.shape[0])
    h = x
    saved = []
    for i, w in enumerate(layers):
        h, s1 = ffn_fwd(h, w["norm_ffn1"], w, ("ffn1_w_in", "ffn1_w_out"), f"l{i}_ffn1")
        if i % 2 == 0:
            h, s2 = hyb_fwd(h, w, f"l{i}_hyb")
        else:
            h, s2 = att_fwd(h, w, tables, f"l{i}_att")
        h, s3 = ffn_fwd(h, w["norm_ffn2"], w, ("ffn2_w_in", "ffn2_w_out"), f"l{i}_ffn2")
        h, s4 = ple_block_fwd(h, pe[i], w, f"l{i}_ple")
        saved.append((s1, s2, s3, s4))
    dh, dgf, loss = loss_head(h, _row(final_norm), target, "loss_head")
    grads = {}
    tie = dgf
    for i in reversed(range(depth)):
        w = layers[i]
        s1, s2, s3, s4 = saved[i]
        dh, g = ple_block_bwd(dh, w, s4, f"l{i}_ple", sink, i, tie)
        odd = i % 2 == 1
        out = ffn_bwd(dh, w["norm_ffn2"], w["ffn2_w_in"], w["ffn2_w_out"], s3, f"l{i}_ffn2", sink,
                      ("ffn2_w_in", "ffn2_w_out"), i, tie, colsum=odd)
        dh = out[0]
        g.update(norm_ffn2=out[1])
        if odd:
            dh, gm = att_bwd(dh, out[2], w, s2, tables, f"l{i}_att", sink, i // 2)
        else:
            dh, gm = hyb_bwd(dh, w, s2, f"l{i}_hyb", sink, i // 2)
        g.update(gm)
        tie = stage_done(i, 1, tie)
        out = ffn_bwd(dh, w["norm_ffn1"], w["ffn1_w_in"], w["ffn1_w_out"], s1, f"l{i}_ffn1", sink,
                      ("ffn1_w_in", "ffn1_w_out"), i, tie)
        dh = out[0]
        g.update(norm_ffn1=out[1])
        tie = stage_done(i, 0, tie)
        for k, v in g.items():
            grads.setdefault(k, []).insert(0, v)
    grads = {k: jnp.stack(v) for k, v in grads.items()}
    grads["final_norm"] = dgf.reshape(-1)
    return loss, dh, grads


def _me():
    return lax.axis_index("x"), lax.axis_index("y"), lax.axis_index("c")


def _flip(v, f):
    return 1 - v if f else v


def _remote(src, dst, send_sems, recv_sems, k, dev):
    return pltpu.make_async_remote_copy(src_ref=src, dst_ref=dst, send_sem=send_sems.at[k], recv_sem=recv_sems.at[k],
                                        device_id=dev, device_id_type=MESH)


CHIP_FLIPS = ((1, 0), (0, 1), (1, 1))
DEV_FLIPS = tuple((fx, fy, fc) for fx in (0, 1) for fy in (0, 1) for fc in (0, 1))[1:]


HBM = pl.BlockSpec(memory_space=pltpu.HBM)
SEM = pl.BlockSpec(memory_space=pltpu.SEMAPHORE)
DATAFLOW = pltpu.SideEffectType.DATAFLOW_SIDE_EFFECTING


def _core_half(ref, axis, c):
    if axis is None:
        return ref
    h = ref.shape[axis] // 2
    return ref.at[pl.ds(c * h, h), :] if axis == 0 else ref.at[:, pl.ds(c * h, h)]


def gather_start(xs, lands, halves, after, name):
    na = len(xs)

    def body(*refs):
        x_refs, land_refs = refs[:na], refs[na:2 * na]
        send_sems, recv_sems = refs[2 * na + 1], refs[2 * na + 2]
        token = refs[-1]
        mx, my, mc = _me()
        chip = 2 * mx + my
        for a in range(na):
            for j, (fx, fy) in enumerate(CHIP_FLIPS):
                _remote(_core_half(x_refs[a], halves[a], mc), _core_half(land_refs[a].at[chip], halves[a], mc),
                        send_sems, recv_sems, 3 * a + j, (_flip(mx, fx), _flip(my, fy), mc)).start()
        token[...] = jnp.zeros_like(token)

    outs = pl.pallas_call(
        body, name=name,
        out_shape=(pltpu.SemaphoreType.DMA((3 * na,)), pltpu.SemaphoreType.DMA((3 * na,)))
        + tuple(pltpu.HBM(x.shape, x.dtype) for x in xs) + tuple(pltpu.HBM(l.shape, l.dtype) for l in lands)
        + (S((8, LANE), F32),),
        in_specs=[HBM] * (2 * na) + [pl.BlockSpec(memory_space=pl.ANY)],
        out_specs=(SEM, SEM) + (HBM,) * (2 * na) + (pl.BlockSpec(memory_space=pltpu.VMEM),),
        input_output_aliases={a: 2 + a for a in range(2 * na)},
        compiler_params=pltpu.CompilerParams(has_side_effects=DATAFLOW),
    )(*[pltpu.with_memory_space_constraint(t, pltpu.HBM) for t in list(xs) + list(lands)], after)
    return outs[0], outs[1], list(outs[2:2 + na]), list(outs[2 + na:2 + 2 * na])


def gather_wait(send_sems, recv_sems, xs, lands, halves, first, after, name):
    na = len(xs)

    def body(*refs):
        x_refs, land_refs = refs[:na], refs[na:2 * na]
        send_sems, recv_sems = refs[2 * na], refs[2 * na + 1]
        mx, my, mc = _me()
        for a in range(na):
            for j, (fx, fy) in enumerate(CHIP_FLIPS):
                px, py = _flip(mx, fx), _flip(my, fy)
                cp = _remote(_core_half(x_refs[a], halves[a], mc), _core_half(land_refs[a].at[2 * px + py], halves[a], mc),
                             send_sems, recv_sems, 3 * (first + a) + j, (px, py, mc))
                cp.wait_send()
                cp.wait_recv()

    outs = pl.pallas_call(
        body, name=name,
        out_shape=tuple(pltpu.HBM(x.shape, x.dtype) for x in xs) + tuple(pltpu.HBM(l.shape, l.dtype) for l in lands),
        in_specs=[HBM] * (2 * na) + [SEM, SEM, pl.BlockSpec(memory_space=pl.ANY)], out_specs=(HBM,) * (2 * na),
        input_output_aliases={a: a for a in range(2 * na)},
        compiler_params=pltpu.CompilerParams(has_side_effects=DATAFLOW),
    )(*xs, *lands, send_sems, recv_sems, after)
    return list(outs[na:])


def forward_halves(land, axis, name):
    def body(in_ref, out_ref, send_sems, recv_sems):
        del in_ref
        mx, my, mc = _me()
        sib = (mx, my, 1 - mc)
        slots = [2 * _flip(mx, fx) + _flip(my, fy) for fx, fy in CHIP_FLIPS]
        cps = [_remote(_core_half(out_ref.at[s], axis, mc), _core_half(out_ref.at[s], axis, mc), send_sems, recv_sems, j, sib)
               for j, s in enumerate(slots)]
        for cp in cps:
            cp.start()
        for j, s in enumerate(slots):
            _remote(_core_half(out_ref.at[s], axis, mc), _core_half(out_ref.at[s], axis, 1 - mc), send_sems, recv_sems, j, sib).wait_recv()
        for cp in cps:
            cp.wait_send()

    return pl.pallas_call(
        body, name=name, out_shape=S(land.shape, land.dtype), in_specs=[ANY], out_specs=ANY, input_output_aliases={0: 0},
        scratch_shapes=[pltpu.SemaphoreType.DMA((3,)), pltpu.SemaphoreType.DMA((3,))])(land)


def all_gather_devices(v, name):
    r, l = v.shape

    def body(v_ref, out_ref, send_sems, recv_sems):
        mx, my, mc = _me()
        me = 4 * mx + 2 * my + mc
        peers = [(_flip(mx, fx), _flip(my, fy), _flip(mc, fc)) for fx, fy, fc in DEV_FLIPS]
        sends = [_remote(v_ref, out_ref.at[me], send_sems, recv_sems, j, p) for j, p in enumerate(peers)]
        for cp in sends:
            cp.start()
        for j, (px, py, pc) in enumerate(peers):
            _remote(v_ref, out_ref.at[4 * px + 2 * py + pc], send_sems, recv_sems, j, (px, py, pc)).wait_recv()
        for cp in sends:
            cp.wait_send()

    out = pl.pallas_call(
        body, name=name, out_shape=S((N_DEV, r, l), v.dtype), in_specs=[ANY], out_specs=ANY,
        scratch_shapes=[pltpu.SemaphoreType.DMA((7,)), pltpu.SemaphoreType.DMA((7,))])(v)
    me = 4 * lax.axis_index("x") + 2 * lax.axis_index("y") + lax.axis_index("c")
    return lax.dynamic_update_slice_in_dim(out, v[None], me, axis=0)


def gather_devices_start(v, name):
    me = 4 * lax.axis_index("x") + 2 * lax.axis_index("y") + lax.axis_index("c")
    land = lax.dynamic_update_slice_in_dim(lax.empty((N_DEV,) + v.shape, v.dtype), v[None], me, axis=0)

    def body(v_ref, land_ref, send_sems, recv_sems, v_thru, land_thru, token):
        mx, my, mc = _me()
        for j, (fx, fy, fc) in enumerate(DEV_FLIPS):
            _remote(v_ref, land_ref.at[4 * mx + 2 * my + mc], send_sems, recv_sems, j,
                    (_flip(mx, fx), _flip(my, fy), _flip(mc, fc))).start()
        token[...] = jnp.zeros_like(token)

    outs = pl.pallas_call(
        body, name=name,
        out_shape=(pltpu.SemaphoreType.DMA((7,)), pltpu.SemaphoreType.DMA((7,)), pltpu.HBM(v.shape, v.dtype),
                   pltpu.HBM(land.shape, land.dtype), S((8, LANE), F32)),
        in_specs=[HBM, HBM], out_specs=(SEM, SEM, HBM, HBM, pl.BlockSpec(memory_space=pltpu.VMEM)),
        input_output_aliases={0: 2, 1: 3}, compiler_params=pltpu.CompilerParams(has_side_effects=DATAFLOW),
    )(pltpu.with_memory_space_constraint(v, pltpu.HBM), pltpu.with_memory_space_constraint(land, pltpu.HBM))
    return outs[:4]


def gather_devices_wait(send_sems, recv_sems, v, land, after, name):
    def body(v_ref, land_ref, send_sems, recv_sems, after_ref, v_dead, got_ref):
        mx, my, mc = _me()
        for j, (fx, fy, fc) in enumerate(DEV_FLIPS):
            px, py, pc = _flip(mx, fx), _flip(my, fy), _flip(mc, fc)
            cp = _remote(v_ref, land_ref.at[4 * px + 2 * py + pc], send_sems, recv_sems, j, (px, py, pc))
            cp.wait_send()
            cp.wait_recv()

    return pl.pallas_call(
        body, name=name, out_shape=(pltpu.HBM(v.shape, v.dtype), pltpu.HBM(land.shape, land.dtype)),
        in_specs=[HBM, HBM, SEM, SEM, pl.BlockSpec(memory_space=pl.ANY)], out_specs=(HBM, HBM),
        input_output_aliases={0: 0, 1: 1}, compiler_params=pltpu.CompilerParams(has_side_effects=DATAFLOW),
    )(v, land, send_sems, recv_sems, after)[1]


def sum_devices(g8, name):
    nd, r, l = g8.shape
    tile = r
    for t in (512, 256, 128, 64, 32, 16, 8):
        if r % t == 0:
            tile = t
            break

    def body(g_ref, o_ref):
        acc = g_ref[0]
        for d in range(1, nd):
            acc = acc + g_ref[d]
        o_ref[...] = acc

    return _call(body, name, (r // tile,), [pl.BlockSpec((nd, tile, l), lambda i: (0, i, 0))], _rs(tile, l), S((r, l), F32),
                 sem=("parallel",))(g8)


def exchange_halves(gs, name):
    na = len(gs)
    nch = gs[0].shape[0]

    def body(*refs):
        g_refs, out_refs = refs[:na], refs[na:2 * na]
        send_sems, recv_sems = refs[2 * na:]
        mx, my, mc = _me()
        sib = (mx, my, 1 - mc)
        cps = []
        for a in range(na):
            half = gs[a].shape[1] // 2
            for j in range(nch):
                cps.append(_remote(g_refs[a].at[j, pl.ds((1 - mc) * half, half), :], out_refs[a].at[j],
                                   send_sems, recv_sems, nch * a + j, sib))
        for cp in cps:
            cp.start()
        for cp in cps:
            cp.wait_recv()
        for cp in cps:
            cp.wait_send()

    return pl.pallas_call(
        body, name=name, out_shape=[S((nch, g.shape[1] // 2, g.shape[2]), g.dtype) for g in gs],
        in_specs=[ANY] * na, out_specs=[ANY] * na,
        scratch_shapes=[pltpu.SemaphoreType.DMA((nch * na,)), pltpu.SemaphoreType.DMA((nch * na,))])(*gs)


def add_halves(g4, got, name):
    nch, r, l = g4.shape
    half = r // 2
    tile = _pick_rows(half)
    nt = half // tile

    def body(g_ref, r_ref, a_ref, own_ref):
        j = pl.program_id(1)
        chip = 2 * lax.axis_index("x") + lax.axis_index("y")
        val = g_ref[0].astype(F32) + r_ref[0].astype(F32)
        a_ref[0] = val.astype(a_ref.dtype)

        @pl.when(j == chip)
        def _():
            own_ref[...] = val

    return pl.pallas_call(
        body, name=name, grid=(nt, nch),
        in_specs=[pl.BlockSpec((1, tile, l), lambda i, j: (j, lax.axis_index("c") * nt + i, 0)),
                  pl.BlockSpec((1, tile, l), lambda i, j: (j, i, 0))],
        out_specs=[pl.BlockSpec((1, tile, l), lambda i, j: (j, i, 0)), pl.BlockSpec((tile, l), lambda i, j: (i, 0))],
        out_shape=[S((nch, half, l), MXU_DTYPE), S((half, l), F32)],
        compiler_params=pltpu.CompilerParams(dimension_semantics=("parallel", "arbitrary"), vmem_limit_bytes=VMEM_LIMIT))(g4, got)


def _pick_rows(r, cap=640):
    return next((t for t in range(cap - cap % 16, 15, -16) if r % t == 0), r)


def add_chips(own, got, name):
    h, l = own.shape
    tile = _pick_rows(h)

    def body(o_ref, g_ref, out_ref):
        out_ref[...] = ((o_ref[...] + g_ref[0].astype(F32)) + g_ref[1].astype(F32)) + g_ref[2].astype(F32)

    nt = h // tile
    return _call(body, name, (nt,), [_rs(tile, l), pl.BlockSpec((3, tile, l), lambda i: (0, i, 0))],
                 pl.BlockSpec((tile, l), lambda i: (lax.axis_index("c") * nt + i, 0)),
                 S((2 * h, l), F32), sem=("parallel",))(own, got)


def join_halves(bufs, name):
    na = len(bufs)

    def body(*refs):
        out_refs = refs[na:2 * na]
        send_sems, recv_sems = refs[2 * na:]
        mx, my, mc = _me()
        sib = (mx, my, 1 - mc)

        def half(a, hc):
            h = bufs[a].shape[0] // 2
            return out_refs[a].at[pl.ds(hc * h, h), :]

        cps = [_remote(half(a, mc), half(a, mc), send_sems, recv_sems, a, sib) for a in range(na)]
        for cp in cps:
            cp.start()
        for a in range(na):
            _remote(half(a, mc), half(a, 1 - mc), send_sems, recv_sems, a, sib).wait_recv()
        for cp in cps:
            cp.wait_send()

    return pl.pallas_call(
        body, name=name, out_shape=[S(b.shape, b.dtype) for b in bufs], in_specs=[ANY] * na, out_specs=[ANY] * na,
        input_output_aliases={a: a for a in range(na)},
        scratch_shapes=[pltpu.SemaphoreType.DMA((na,)), pltpu.SemaphoreType.DMA((na,))])(*bufs)


def exchange_chips_start(parts, name):
    na = len(parts)
    lands = [lax.empty((3,) + p.shape[1:], p.dtype) for p in parts]

    def body(*refs):
        a_refs, land_refs = refs[:na], refs[na:2 * na]
        send_sems, recv_sems = refs[2 * na], refs[2 * na + 1]
        mx, my, mc = _me()
        for j, (fx, fy) in enumerate(CHIP_FLIPS):
            px, py = _flip(mx, fx), _flip(my, fy)
            for a in range(na):
                _remote(a_refs[a].at[2 * px + py], land_refs[a].at[j], send_sems, recv_sems, 3 * a + j, (px, py, mc)).start()
        refs[-1][...] = jnp.zeros_like(refs[-1])

    outs = pl.pallas_call(
        body, name=name,
        out_shape=(pltpu.SemaphoreType.DMA((3 * na,)), pltpu.SemaphoreType.DMA((3 * na,)))
        + tuple(pltpu.HBM(t.shape, t.dtype) for t in list(parts) + lands) + (S((8, LANE), F32),),
        in_specs=[HBM] * (2 * na), out_specs=(SEM, SEM) + (HBM,) * (2 * na) + (pl.BlockSpec(memory_space=pltpu.VMEM),),
        input_output_aliases={a: 2 + a for a in range(2 * na)},
        compiler_params=pltpu.CompilerParams(has_side_effects=DATAFLOW),
    )(*[pltpu.with_memory_space_constraint(t, pltpu.HBM) for t in list(parts) + lands])
    return outs[0], outs[1], list(outs[2:2 + na]), list(outs[2 + na:2 + 2 * na]), outs[-1]


def exchange_chips_wait(send_sems, recv_sems, parts, lands, after, name):
    na = len(parts)

    def body(*refs):
        a_refs, land_refs = refs[:na], refs[na:2 * na]
        send_sems, recv_sems = refs[2 * na], refs[2 * na + 1]
        mx, my, mc = _me()
        for j, (fx, fy) in enumerate(CHIP_FLIPS):
            px, py = _flip(mx, fx), _flip(my, fy)
            for a in range(na):
                cp = _remote(a_refs[a].at[2 * px + py], land_refs[a].at[j], send_sems, recv_sems, 3 * a + j, (px, py, mc))
                cp.wait_send()
                cp.wait_recv()

    outs = pl.pallas_call(
        body, name=name, out_shape=tuple(pltpu.HBM(t.shape, t.dtype) for t in list(parts) + list(lands)),
        in_specs=[HBM] * (2 * na) + [SEM, SEM, pl.BlockSpec(memory_space=pl.ANY)], out_specs=(HBM,) * (2 * na),
        input_output_aliases={a: a for a in range(2 * na)},
        compiler_params=pltpu.CompilerParams(has_side_effects=DATAFLOW),
    )(*parts, *lands, send_sems, recv_sems, after)
    return list(outs[na:])


def reduce_begin(gs, tag):
    got = exchange_halves(gs, f"{tag}_d2d")
    sums = [add_halves(g, r, f"{tag}_add1_{i}") for i, (g, r) in enumerate(zip(gs, got))]
    return [own for _, own in sums], exchange_chips_start([a for a, _ in sums], f"{tag}_ici_start")


def reduce_end(state, after, tag):
    owns, (send_sems, recv_sems, parts, lands, _) = state
    got = exchange_chips_wait(send_sems, recv_sems, parts, lands, after, f"{tag}_ici_wait")
    return [add_chips(own, r, f"{tag}_add2_{i}") for i, (own, r) in enumerate(zip(owns, got))]


PACK_L = 1024


def _pack(arrs, dtype, row_mult, lead=None):
    lead_shape = () if lead is None else arrs[0].shape[:lead]
    flat = jnp.concatenate([a.astype(dtype).reshape(lead_shape + (-1,)) for a in arrs], axis=-1)
    n = flat.shape[-1]
    unit = row_mult * PACK_L
    total = -(-n // unit) * unit
    flat = jnp.pad(flat, [(0, 0)] * len(lead_shape) + [(0, total - n)])
    return flat.reshape(lead_shape + (total // PACK_L, PACK_L))


def _unpack(packed, shapes, lead=None):
    lead_shape = () if lead is None else packed.shape[:lead]
    flat = packed.reshape(lead_shape + (-1,))
    out, off = [], 0
    for shp in shapes:
        n = int(np.prod(shp))
        out.append(flat[..., off:off + n].reshape(lead_shape + tuple(shp)))
        off += n
    return out


def _to_full(gathered, axis):
    t = jnp.moveaxis(gathered, 0, axis)
    shp = t.shape
    return t.reshape(shp[:axis] + (shp[axis] * shp[axis + 1],) + shp[axis + 2:])


WEIGHTS = ("norm_ffn1", "ffn1_w_in", "ffn1_w_out", "norm_mix", "norm_ffn2", "ffn2_w_in", "ffn2_w_out", "ple_norm",
           "ple_gate_w", "ple_proj_w", "hyb_w_in", "conv_dw_w", "conv_dw_b", "conv_ln_g", "conv_ln_b", "ssm_conv_w",
           "ssm_conv_b", "ssm_dt_bias", "ssm_a_log", "ssm_d", "ssm_norm", "hyb_w_out", "att_w_qkv", "att_b_qkv",
           "att_sinks", "att_w_o", "att_b_o", "final_norm")
SHARD_AXIS = dict(ffn1_w_in=2, ffn1_w_out=1, ffn2_w_in=2, ffn2_w_out=1, ple_gate_w=1, ple_proj_w=2, hyb_w_in=2,
                  conv_dw_w=2, ssm_conv_w=2, hyb_w_out=1, att_w_qkv=2, att_b_qkv=1, att_w_o=1, att_b_o=1)
BIG = ("ffn1_w_in", "ffn1_w_out", "ffn2_w_in", "ffn2_w_out", "ple_gate_w", "ple_proj_w", "hyb_w_in", "hyb_w_out",
       "att_w_qkv", "att_w_o")
TRANSPOSED = ("hyb_w_in",)
FORWARDED = (0, 2)


def _shard_axis(k):
    return 1 if k in TRANSPOSED else SHARD_AXIS[k]
SMALL_SHARDED = ("conv_dw_w", "ssm_conv_w", "att_b_qkv", "att_b_o")
SMALL = tuple(k for k in WEIGHTS if k not in BIG)


def _step(x, p, target, w, m, v):
    mx, my = lax.axis_index("x"), lax.axis_index("y")
    chip = 2 * mx + my
    w, m, v = ({k: (a.transpose(0, 2, 1) if k in TRANSPOSED else a) for k, a in d.items()} for d in (w, m, v))

    depth = w["norm_ffn1"].shape[0]
    order = sorted([(k, i) for i in range(depth) for k in BIG if _layer_index(k, i) is not None],
                   key=lambda t: (t[1], _stage(t[0])))
    small_g = all_gather_devices(_pack([w[k] for k in SMALL_SHARDED], F32, 8), "gather_small")
    shards = [w[k][_layer_index(k, i)].astype(MXU_DTYPE) for k, i in order]
    lands = [lax.dynamic_update_slice_in_dim(lax.empty((N_CHIPS,) + s.shape, s.dtype), s[None], chip, axis=0) for s in shards]
    halves = [(0 if s.shape[0] % 32 == 0 else 1) if pos in FORWARDED else None for pos, s in enumerate(shards)]
    send_sems, recv_sems, shards, lands = gather_start(shards, lands, halves, small_g, "gather_start")

    def fetch(i, k, after):
        p = order.index((k, i))
        g, = gather_wait(send_sems, recv_sems, [shards[p]], [lands[p]], [halves[p]], p, after, f"gather_wait_l{i}_{k}")
        if halves[p] is not None:
            g = forward_halves(g, halves[p], f"gather_forward_l{i}_{k}")
        if _shard_axis(k) == 2:
            return ColSharded(g)
        return g.reshape(-1, g.shape[-1])

    small_g = small_g[0::2]
    small_full = {k: _to_full(g, SHARD_AXIS[k])
                  for k, g in zip(SMALL_SHARDED, _unpack(small_g, [w[k].shape for k in SMALL_SHARDED], lead=1))}
    layers = [LayerWeights({k: small_full.get(k, w[k])[_layer_index(k, i)] for k in SMALL if _layer_index(k, i) is not None},
                           functools.partial(fetch, i)) for i in range(depth)]

    def bucket_of(layer, stage):
        return (layer, 0) if layer > 0 else (0, min(stage, 1))

    sink = GradSink({k: w[k].shape for k in BIG}, bucket_of)
    begun = {}

    def stage_done(i, stage, tie):
        b = bucket_of(i, stage)
        if stage > 0 and bucket_of(i, stage - 1) == b:
            return tie
        begun[b] = reduce_begin(list(sink.bufs[b].values()), f"grads_l{b[0]}_{b[1]}")
        return begun[b][1][-1]

    loss, dx, grads = trunk_fwd_bwd(x[0], p[:, 0], target[0], layers, w["final_norm"], sink, stage_done)

    results = {}

    def finish(buckets, after, tag):
        halves = {b: reduce_end(begun[b], after, f"grads_l{b[0]}_{b[1]}") for b in buckets}
        joined = iter(join_halves([h for b in buckets for h in halves[b]], f"grads_join_{tag}"))
        reduced = {b: {c: next(joined) for c in sink.bufs[b]} for b in buckets}
        last = after
        for (k, li), (b, key, _, off, r) in sink.where.items():
            if b in buckets:
                g = reduced[b][key]
                if r % 8:
                    g, off = g[off:off + r], 0
                results[k] = adamw_layer(w[k], g, off, m[k], v[k], li, results.get(k), f"adamw_{k}_{li}")
                last = results[k][1]
        return last

    vec = gather_devices_start(_pack([loss[0:1, 0:1]] + [grads[k] for k in SMALL], F32, 8), "gather_vectors_start")
    order_b = list(begun)
    started_last = begun[order_b[-1]][1][-1]
    done = finish(order_b[-1:], finish(order_b[:-1], started_last, "early") if len(order_b) > 1 else dx, "last")
    g_out = {k: results[k][0] for k in BIG}
    vec = sum_devices(gather_devices_wait(*vec, done, "gather_vectors_wait"), "sum_vectors")
    parts = _unpack(vec, [(1, 1)] + [grads[k].shape for k in SMALL])
    loss_out = parts[0].reshape(())
    for k, g in zip(SMALL, parts[1:]):
        if k in SHARD_AXIS:
            ax = SHARD_AXIS[k]
            g = lax.dynamic_slice_in_dim(g, chip * w[k].shape[ax], w[k].shape[ax], axis=ax)
        g_out[k] = g

    for k in TRANSPOSED:
        results[k] = [a.transpose(0, 2, 1) for a in results[k]]
    g_out.update({k: results[k][0] for k in TRANSPOSED})
    delta, new_m, new_v = ({k: results[k][j] for k in BIG} for j in (1, 2, 3))
    shapes = [w[k].shape for k in SMALL]
    packed = [_pack([src[k] for k in SMALL], F32, 8) for src in (w, g_out, m, v)]
    outs = adamw(*packed, "adamw_small")
    for dst, o in zip((delta, new_m, new_v), outs):
        for k, a in zip(SMALL, _unpack(o, shapes)):
            dst[k] = a
    return ((loss_out, dx[None]) + tuple(g_out[k] for k in WEIGHTS) + tuple(delta[k] for k in WEIGHTS)
            + tuple(new_m[k] for k in WEIGHTS) + tuple(new_v[k] for k in WEIGHTS))


def kernel(x, p, norm_ffn1, ffn1_w_in, ffn1_w_out, norm_mix, norm_ffn2, ffn2_w_in, ffn2_w_out, ple_norm, ple_gate_w, ple_proj_w, hyb_w_in, conv_dw_w, conv_dw_b, conv_ln_g, conv_ln_b, ssm_conv_w, ssm_conv_b, ssm_dt_bias, ssm_a_log, ssm_d, ssm_norm, hyb_w_out, att_w_qkv, att_b_qkv, att_sinks, att_w_o, att_b_o, final_norm, loss_target, m_norm_ffn1, m_ffn1_w_in, m_ffn1_w_out, m_norm_mix, m_norm_ffn2, m_ffn2_w_in, m_ffn2_w_out, m_ple_norm, m_ple_gate_w, m_ple_proj_w, m_hyb_w_in, m_conv_dw_w, m_conv_dw_b, m_conv_ln_g, m_conv_ln_b, m_ssm_conv_w, m_ssm_conv_b, m_ssm_dt_bias, m_ssm_a_log, m_ssm_d, m_ssm_norm, m_hyb_w_out, m_att_w_qkv, m_att_b_qkv, m_att_sinks, m_att_w_o, m_att_b_o, m_final_norm, v_norm_ffn1, v_ffn1_w_in, v_ffn1_w_out, v_norm_mix, v_norm_ffn2, v_ffn2_w_in, v_ffn2_w_out, v_ple_norm, v_ple_gate_w, v_ple_proj_w, v_hyb_w_in, v_conv_dw_w, v_conv_dw_b, v_conv_ln_g, v_conv_ln_b, v_ssm_conv_w, v_ssm_conv_b, v_ssm_dt_bias, v_ssm_a_log, v_ssm_d, v_ssm_norm, v_hyb_w_out, v_att_w_qkv, v_att_b_qkv, v_att_sinks, v_att_w_o, v_att_b_o, v_final_norm):
    given = locals()
    w = {k: given[k] for k in WEIGHTS}
    m = {k: given["m_" + k] for k in WEIGHTS}
    v = {k: given["v_" + k] for k in WEIGHTS}
    return _step(x, p, loss_target, w, m, v)
```

```python
import functools

import numpy as np
import jax
import jax.numpy as jnp
from jax import lax
from jax.experimental import pallas as pl
from jax.experimental.pallas import tpu as pltpu

F32 = jnp.float32
BF16 = jnp.bfloat16
MXU_DTYPE = jnp.bfloat16
S = jax.ShapeDtypeStruct
MESH = pl.DeviceIdType.MESH

VMEM_LIMIT = 48 * 2**20
LANE = 128

EPS = 1e-6
SSM_HEADS = 16
HEAD_DIM = 64
SSM_GROUPS = 2
SSM_STATE = 128
SSM_CONV = 4
CHUNK = 128
CONV_WIDTH = 31
ATT_HEADS = 16
ATT_KV_HEADS = 4
WINDOW = 128
ROPE_THETA = 10000.0
ADAM_LR = 0.001
ADAM_B1 = 0.9
ADAM_B2 = 0.999
ADAM_EPS = 1e-08
ADAM_WD = 0.01
ADAM_STEP = 10

N_CHIPS = 4
N_DEV = 8

NN = ((1,), (0,))
NT = ((1,), (1,))
TN = ((0,), (0,))


def _mm(a, b, dims=NN):
    return lax.dot_general(a.astype(MXU_DTYPE), b.astype(MXU_DTYPE), (dims, ((), ())), preferred_element_type=F32)


def _split3(a):
    hi = a.astype(BF16)
    r = a - hi.astype(F32)
    mid = r.astype(BF16)
    lo = (r - mid.astype(F32)).astype(BF16)
    return hi, mid, lo


def _mm01(a, onehot, dims=NN):
    o = onehot.astype(BF16)
    out = None
    for part in _split3(a):
        t = lax.dot_general(part, o, (dims, ((), ())), preferred_element_type=F32)
        out = t if out is None else out + t
    return out


def _01mm(onehot, a):
    o = onehot.astype(BF16)
    out = None
    for part in _split3(a):
        t = lax.dot_general(o, part, (NN, ((), ())), preferred_element_type=F32)
        out = t if out is None else out + t
    return out


def _sigmoid(x):
    return 0.5 * jnp.tanh(0.5 * x) + 0.5


def _softplus(x):
    return jnp.maximum(x, 0.0) + jnp.log(1.0 + jnp.exp(-jnp.abs(x)))


def _iota(shape, axis):
    return lax.broadcasted_iota(jnp.int32, shape, axis)


def _head_indicator(width, heads, transposed=False):
    per = width // heads
    if transposed:
        return (_iota((heads, width), 1) // per == _iota((heads, width), 0)).astype(F32)
    return (_iota((width, heads), 0) // per == _iota((width, heads), 1)).astype(F32)


def _acc(ref, i, val):
    @pl.when(i == 0)
    def _():
        ref[...] = val

    @pl.when(i > 0)
    def _():
        ref[...] += val


def _rs(tile, width, col=0, shift=0, n=None):
    if shift == 0:
        return pl.BlockSpec((tile, width), lambda i: (i, col))
    if shift < 0:
        return pl.BlockSpec((tile, width), lambda i: (jnp.maximum(i - 1, 0), col))
    return pl.BlockSpec((tile, width), lambda i: (jnp.minimum(i + 1, n - 1), col))


def _ps(shape):
    return pl.BlockSpec(shape, lambda i: (0,) * len(shape))


def _call(body, name, grid, in_specs, out_specs, out_shape, scratch=(), sem=None):
    return pl.pallas_call(
        body, name=name, grid=grid, in_specs=in_specs, out_specs=out_specs, out_shape=out_shape,
        scratch_shapes=list(scratch),
        compiler_params=pltpu.CompilerParams(dimension_semantics=sem, vmem_limit_bytes=VMEM_LIMIT))


def _row_tile(n, target):
    t = min(n, target)
    assert n % t == 0, (n, t)
    return t


def _pick_tile(dim, target):
    if dim <= target:
        return dim
    t = (int(1.4 * target) // LANE) * LANE
    while t >= LANE:
        if dim % t == 0:
            return t
        t -= LANE
    return dim


ANY = pl.BlockSpec(memory_space=pl.ANY)


def _paired(j):
    return (j % 2) * 2 + j // 2


class ColSharded:
    def __init__(self, arr, paired=False):
        self.arr, self.paired = arr, paired
        self.nch, self.rows, self.per = arr.shape
        self.shape = (self.rows, self.nch * self.per)

    def chip(self, j):
        return _paired(j) if self.paired else j


class Slot:
    def __init__(self, buf, kind, per, off, c0=0, paired=False):
        self.buf, self.kind, self.per, self.off, self.c0, self.paired = buf, kind, per, off, c0, paired

    def chip(self, j):
        return _paired(j) if self.paired else j


def matmul(a, b, mode, name, *, out_dtype=F32, scale=None, res=None, bias=None, into=None, tm=1024, tn=1024, tk=1024):
    bshape = b.shape
    if mode == "nn":
        (m, k), (k2, n) = a.shape, bshape
    elif mode == "nt":
        (m, k), (n, k2) = a.shape, bshape
    else:
        (k, m), (k2, n) = a.shape, bshape
    assert k == k2, (a.shape, bshape, mode)
    if mode == "tn":
        tk = 2 * tk
    elif k <= 3 * tk:
        tk = k
    tm, tn, tk = _pick_tile(m, tm), _pick_tile(n, tn), _pick_tile(k, tk)
    if isinstance(b, ColSharded):
        if mode == "nn":
            tn = b.per
        else:
            assert mode == "nt"
            tk = b.per
    if into is not None:
        if into.kind == "c":
            tn = into.per
            assert into.off % tm == 0 and n == N_CHIPS * into.per
        else:
            tm = max(1, min(m, int(1.4 * 1024)) // into.per) * into.per
            assert m % tm == 0 and into.off % into.per == 0 and into.c0 % (tm // into.per) == 0
    nk = k // tk
    dims = {"nn": NN, "nt": NT, "tn": TN}[mode]
    a_spec = (pl.BlockSpec((tk, tm), lambda i, j, kk: (kk, i)) if mode == "tn"
              else pl.BlockSpec((tm, tk), lambda i, j, kk: (i, kk)))
    if isinstance(b, ColSharded):
        bchip = b.chip
        b_spec = (pl.BlockSpec((None, tk, tn), lambda i, j, kk: (bchip(j), kk, 0)) if mode == "nn"
                  else pl.BlockSpec((None, tn, tk), lambda i, j, kk: (bchip(kk), j, 0)))
        b = b.arr
    else:
        b_spec = (pl.BlockSpec((tn, tk), lambda i, j, kk: (j, kk)) if mode == "nt"
                  else pl.BlockSpec((tk, tn), lambda i, j, kk: (kk, j)))
    plain_o = pl.BlockSpec((tm, tn), lambda i, j, kk: (i, j))
    ins, in_specs = [a, b], [a_spec, b_spec]
    if bias is not None:
        ins.append(bias)
        in_specs.append(pl.BlockSpec((1, tn), lambda i, j, kk: (0, j)))
    if res is not None:
        ins.append(res)
        in_specs.append(plain_o)
    aliases = {}
    if into is None:
        o_spec, o_shape = plain_o, S((m, n), out_dtype)
    else:
        aliases = {len(ins): 0}
        ins.append(into.buf)
        in_specs.append(ANY)
        o_shape = S(into.buf.shape, into.buf.dtype)
        if into.kind == "c":
            ob, ochip = into.off // tm, into.chip
            o_spec = pl.BlockSpec((None, tm, tn), lambda i, j, kk: (ochip(j), ob + i, 0))
        else:
            q, ob = tm // into.per, into.off // into.per
            cb = into.c0 // q
            o_spec = pl.BlockSpec((q, into.per, tn), lambda i, j, kk: (cb + i, ob, j))

    def body(*refs):
        a_ref, b_ref = refs[0], refs[1]
        o_ref, acc_ref = refs[-2], refs[-1]
        kk = pl.program_id(2)

        def finish(out):
            if scale is not None:
                out = out * scale
            pos = 2
            if bias is not None:
                out = out + refs[pos][...]
                pos += 1
            if res is not None:
                out = out + refs[pos][...]
            o_ref[...] = out.astype(o_ref.dtype).reshape(o_ref.shape)

        if nk == 1:
            finish(_mm(a_ref[...], b_ref[...], dims))
            return

        @pl.when(kk == 0)
        def _():
            acc_ref[...] = jnp.zeros_like(acc_ref)

        acc_ref[...] += _mm(a_ref[...], b_ref[...], dims)

        @pl.when(kk == nk - 1)
        def _():
            finish(acc_ref[...])

    return pl.pallas_call(
        body, name=name, grid=(m // tm, n // tn, nk), in_specs=in_specs, out_specs=o_spec, out_shape=o_shape,
        scratch_shapes=[pltpu.VMEM((tm, tn), F32)], input_output_aliases=aliases,
        compiler_params=pltpu.CompilerParams(dimension_semantics=("parallel", "parallel", "arbitrary"),
                                             vmem_limit_bytes=VMEM_LIMIT))(*ins)


def rms_matmul(h, g, b, mode, name, bias=None):
    n, d = h.shape
    sharded = isinstance(b, ColSharded)
    n_out = b.shape[1] if mode == "nn" else b.shape[0]
    tm = _row_tile(n, 1024)
    tn = b.per if sharded else _pick_tile(n_out, 1024)
    if sharded:
        assert mode == "nn"
        bchip = b.chip
        b_spec = pl.BlockSpec((None, d, tn), lambda i, j: (bchip(j), 0, 0))
        b = b.arr
    elif mode == "nn":
        b_spec = pl.BlockSpec((d, tn), lambda i, j: (0, j))
    else:
        b_spec = pl.BlockSpec((tn, d), lambda i, j: (j, 0))
    ins = [h, g, b] + ([bias] if bias is not None else [])
    in_specs = [pl.BlockSpec((tm, d), lambda i, j: (i, 0)), pl.BlockSpec((1, d), lambda i, j: (0, 0)), b_spec]
    if bias is not None:
        in_specs.append(pl.BlockSpec((1, tn), lambda i, j: (0, j)))

    def body(h_ref, g_ref, b_ref, *refs):
        o_ref, xn_ref = refs[-2:]
        x = h_ref[...]
        r = lax.rsqrt(jnp.mean(x * x, axis=-1, keepdims=True) + EPS)
        xn = (x * r * g_ref[...]).astype(xn_ref.dtype)

        @pl.when(pl.program_id(1) == 0)
        def _():
            xn_ref[...] = xn

        out = _mm(xn, b_ref[...], NN if mode == "nn" else NT)
        o_ref[...] = out if bias is None else out + refs[0][...]

    return pl.pallas_call(
        body, name=name, grid=(n // tm, n_out // tn), in_specs=in_specs,
        out_specs=[pl.BlockSpec((tm, tn), lambda i, j: (i, j)), pl.BlockSpec((tm, d), lambda i, j: (i, 0))],
        out_shape=[S((n, n_out), F32), S((n, d), MXU_DTYPE)],
        compiler_params=pltpu.CompilerParams(dimension_semantics=("parallel", "arbitrary"), vmem_limit_bytes=VMEM_LIMIT),
    )(*ins)


def _rms_bwd_math(x, g, dy):
    r = lax.rsqrt(jnp.mean(x * x, axis=-1, keepdims=True) + EPS)
    xh = x * r
    dg = jnp.sum(dy * xh, axis=0, keepdims=True)
    dxh = dy * g
    dx = r * (dxh - xh * jnp.mean(dxh * xh, axis=-1, keepdims=True))
    return dx, dg


def nt_rms_bwd(a, b, h, g, dh_in, name, extra=None, colsum=False, b_kd=False):
    n, k = a.shape
    d = h.shape[1]
    tm = _row_tile(n, 1024)
    sharded = isinstance(b, ColSharded)
    tk = b.per if sharded else _pick_tile(k, 1024)
    nk = k // tk
    dims = NN if b_kd else NT
    if sharded:
        bchip = b.chip
        b_spec = pl.BlockSpec((None, d, tk), lambda i, kk: (bchip(kk), 0, 0))
        b = b.arr
    elif b_kd:
        b_spec = pl.BlockSpec((tk, d), lambda i, kk: (kk, 0))
    else:
        b_spec = pl.BlockSpec((d, tk), lambda i, kk: (0, kk))
    row = pl.BlockSpec((tm, d), lambda i, kk: (i, 0))
    vec = pl.BlockSpec((1, d), lambda i, kk: (0, 0))
    ins, in_specs = [a, b, h, g, dh_in], [pl.BlockSpec((tm, tk), lambda i, kk: (i, kk)), b_spec, row, vec, row]
    if extra is not None:
        k2 = extra[0].shape[1]
        ins += list(extra)
        in_specs += [pl.BlockSpec((tm, k2), lambda i, kk: (i, 0)),
                     pl.BlockSpec((k2, d) if b_kd else (d, k2), lambda i, kk: (0, 0))]
    n_in = len(ins)

    def body(*refs):
        a_ref, b_ref, h_ref, g_ref, dh_ref = refs[:5]
        o_ref, dg_ref = refs[n_in], refs[n_in + 1]
        acc_ref = refs[-1]
        i, kk = pl.program_id(0), pl.program_id(1)

        @pl.when(kk == 0)
        def _():
            acc_ref[...] = _mm(refs[5][...], refs[6][...], dims) if extra is not None else jnp.zeros_like(acc_ref)

        acc_ref[...] += _mm(a_ref[...], b_ref[...], dims)

        @pl.when(kk == nk - 1)
        def _():
            dx, dg = _rms_bwd_math(h_ref[...], g_ref[...], acc_ref[...])
            out = dh_ref[...] + dx
            o_ref[...] = out
            _acc(dg_ref, i, dg)
            if colsum:
                _acc(refs[n_in + 2], i, jnp.sum(out, axis=0, keepdims=True))

    n_vec = 2 if colsum else 1
    return pl.pallas_call(
        body, name=name, grid=(n // tm, nk), in_specs=in_specs, out_specs=[row] + [vec] * n_vec,
        out_shape=[S((n, d), F32)] + [S((1, d), F32)] * n_vec, scratch_shapes=[pltpu.VMEM((tm, d), F32)],
        compiler_params=pltpu.CompilerParams(dimension_semantics=("arbitrary", "arbitrary"), vmem_limit_bytes=VMEM_LIMIT),
    )(*ins)


def swiglu_in(h, g, w_in, name):
    n, d = h.shape
    per = w_in.per
    nj = w_in.nch // 2
    tile = _row_tile(n, 1024)

    def body(h_ref, g_ref, wg_ref, wu_ref, xn_ref, u_ref, hm_ref):
        x = h_ref[...]
        r = lax.rsqrt(jnp.mean(x * x, axis=-1, keepdims=True) + EPS)
        xn = (x * r * g_ref[...]).astype(xn_ref.dtype)

        @pl.when(pl.program_id(1) == 0)
        def _():
            xn_ref[...] = xn

        a = _mm(xn, wg_ref[...])
        b = _mm(xn, wu_ref[...])
        u_ref[:, :per] = a.astype(u_ref.dtype)
        u_ref[:, per:] = b.astype(u_ref.dtype)
        hm_ref[...] = (a * _sigmoid(a) * b).astype(hm_ref.dtype)

    return pl.pallas_call(
        body, name=name, grid=(n // tile, nj),
        in_specs=[pl.BlockSpec((tile, d), lambda i, j: (i, 0)), pl.BlockSpec((1, d), lambda i, j: (0, 0)),
                  pl.BlockSpec((None, d, per), lambda i, j: (j, 0, 0)), pl.BlockSpec((None, d, per), lambda i, j: (nj + j, 0, 0))],
        out_specs=[pl.BlockSpec((tile, d), lambda i, j: (i, 0)), pl.BlockSpec((tile, 2 * per), lambda i, j: (i, j)),
                   pl.BlockSpec((tile, per), lambda i, j: (i, j))],
        out_shape=[S((n, d), MXU_DTYPE), S((n, 2 * nj * per), MXU_DTYPE), S((n, nj * per), MXU_DTYPE)],
        compiler_params=pltpu.CompilerParams(dimension_semantics=("parallel", "arbitrary"), vmem_limit_bytes=VMEM_LIMIT),
    )(h, g, w_in.arr, w_in.arr)


def swiglu_out_bwd(dh, w_out, u, after, name):
    n, d = dh.shape
    f = w_out.shape[0]
    per = u.shape[1] // 4
    nj = f // per
    tile = _row_tile(n, 1024)

    def body(dh_ref, w_ref, u_ref, after_ref, du_ref):
        dm = 0.5 * _mm(dh_ref[...], w_ref[...], NT)
        a = u_ref[:, :per].astype(F32)
        b = u_ref[:, per:].astype(F32)
        s = _sigmoid(a)
        du_ref[:, :per] = (dm * b * s * (1.0 + a * (1.0 - s))).astype(du_ref.dtype)
        du_ref[:, per:] = (dm * a * s).astype(du_ref.dtype)

    return pl.pallas_call(
        body, name=name, grid=(n // tile, nj),
        in_specs=[pl.BlockSpec((tile, d), lambda i, j: (i, 0)), pl.BlockSpec((per, d), lambda i, j: (j, 0)),
                  pl.BlockSpec((tile, 2 * per), lambda i, j: (i, j)), ANY],
        out_specs=pl.BlockSpec((tile, 2 * per), lambda i, j: (i, j)),
        out_shape=S(u.shape, MXU_DTYPE),
        compiler_params=pltpu.CompilerParams(dimension_semantics=("parallel", "parallel"), vmem_limit_bytes=VMEM_LIMIT),
    )(dh, w_out, u, after)


def ple_fwd(h, g, w_gate, pp, name):
    n, d = h.shape
    tile = _row_tile(n, 512)

    def body(h_ref, g_ref, w_ref, pp_ref, o_ref, gl_ref, xn_ref):
        x = h_ref[...]
        r = lax.rsqrt(jnp.mean(x * x, axis=-1, keepdims=True) + EPS)
        xn = (x * r * g_ref[...]).astype(xn_ref.dtype)
        xn_ref[...] = xn
        gl = _mm(xn, w_ref[...])
        gl_ref[...] = gl
        o_ref[...] = x + _sigmoid(gl) * pp_ref[...]

    return _call(body, name, (n // tile,), [_rs(tile, d), _ps((1, d)), _ps(w_gate.shape), _rs(tile, d)],
                 [_rs(tile, d)] * 3, [S((n, d), F32), S((n, d), F32), S((n, d), MXU_DTYPE)], sem=("parallel",))(h, g, w_gate, pp)


def ple_bwd(dh, gl, pp, w_gate, h, g, after, name):
    n, d = dh.shape
    tile = _row_tile(n, 512)

    def body(dh_ref, gl_ref, pp_ref, w_ref, h_ref, g_ref, after_ref, o_ref, dpp_ref, dgl_ref, dg_ref):
        i = pl.program_id(0)
        s = _sigmoid(gl_ref[...])
        dh_ = dh_ref[...]
        dpp_ref[...] = (dh_ * s).astype(dpp_ref.dtype)
        dgl = (dh_ * pp_ref[...] * s * (1.0 - s)).astype(dgl_ref.dtype)
        dgl_ref[...] = dgl
        dx, dg = _rms_bwd_math(h_ref[...], g_ref[...], _mm(dgl, w_ref[...], NT))
        o_ref[...] = dh_ + dx
        _acc(dg_ref, i, dg)

    return _call(body, name, (n // tile,),
                 [_rs(tile, d)] * 3 + [_ps(w_gate.shape), _rs(tile, d), _ps((1, d)), ANY],
                 [_rs(tile, d)] * 3 + [_ps((1, d))],
                 [S((n, d), F32), S((n, d), MXU_DTYPE), S((n, d), MXU_DTYPE), S((1, d), F32)],
                 sem=("arbitrary",))(dh, gl, pp, w_gate, h, g, after)


def loss_head(h, g, target, name):
    n, d = h.shape
    tile = _row_tile(n, 512)

    def body(h_ref, g_ref, t_ref, dh_ref, dg_ref, loss_ref):
        i = pl.program_id(0)
        x = h_ref[...]
        gg = g_ref[...]
        r = lax.rsqrt(jnp.mean(x * x, axis=-1, keepdims=True) + EPS)
        err = x * r * gg - t_ref[...]
        part = 0.5 * jnp.sum(jnp.mean(err * err, axis=-1, keepdims=True), axis=0, keepdims=True)
        dx, dg = _rms_bwd_math(x, gg, err * (1.0 / d))
        dh_ref[...] = dx
        _acc(dg_ref, i, dg)
        _acc(loss_ref, i, jnp.broadcast_to(part, (8, LANE)))

    return _call(body, name, (n // tile,), [_rs(tile, d), _ps((1, d)), _rs(tile, d)],
                 [_rs(tile, d), _ps((1, d)), _ps((8, LANE))], [S((n, d), F32), S((1, d), F32), S((8, LANE), F32)],
                 sem=("arbitrary",))(h, g, target)


def _adamw_math(w, g, m, v):
    c1 = np.float32(1.0 - ADAM_B1 ** ADAM_STEP)
    c2 = np.float32(1.0 - ADAM_B2 ** ADAM_STEP)
    mm = ADAM_B1 * m + (1.0 - ADAM_B1) * g
    vv = ADAM_B2 * v + (1.0 - ADAM_B2) * (g * g)
    return -ADAM_LR * ((mm / c1) / (jnp.sqrt(vv / c2) + ADAM_EPS) + ADAM_WD * w), mm, vv


def adamw_layer(w, pack, off, m, v, li, prev, name):
    n, r, c = w.shape

    def body(w_ref, g_ref, m_ref, v_ref, *refs):
        go_ref, d_ref, mo_ref, vo_ref = refs[-4:]
        g = g_ref[...]
        go_ref[...] = g
        d_ref[...], mo_ref[...], vo_ref[...] = _adamw_math(w_ref[...], g, m_ref[...], v_ref[...])

    if r % 8 == 0:
        cap = 2**21 // (4 * c) // 8 * 8
        tile = next(t for t in range(min(cap, r), 7, -8) if r % t == 0 and off % t == 0)
        ob, steps = off // tile, r // tile
        blk = pl.BlockSpec((None, tile, c), lambda i: (li, i, 0))
        g_spec = pl.BlockSpec((tile, c), lambda i: (ob + i, 0))
    else:
        assert off == 0 and pack.shape[0] == r and c % (2 * LANE) == 0
        steps = c // (2 * LANE)
        blk = pl.BlockSpec((None, r, 2 * LANE), lambda i: (li, 0, i))
        g_spec = pl.BlockSpec((r, 2 * LANE), lambda i: (0, i))
    prev = list(prev) if prev is not None else []
    return pl.pallas_call(
        body, name=name, grid=(steps,),
        in_specs=[blk, g_spec, blk, blk] + [ANY] * len(prev),
        out_specs=[blk] * 4, out_shape=[S((n, r, c), F32)] * 4,
        input_output_aliases={4 + j: j for j in range(len(prev))},
        compiler_params=pltpu.CompilerParams(dimension_semantics=("parallel",), vmem_limit_bytes=VMEM_LIMIT),
    )(w, pack, m, v, *prev)


def adamw(w, g, m, v, name):
    r, c = w.shape
    tile = r
    for t in (512, 256, 128, 64, 32, 16, 8):
        if r % t == 0 and t * c * 4 <= 2**21:
            tile = t
            break

    def body(w_ref, g_ref, m_ref, v_ref, d_ref, mo_ref, vo_ref):
        d_ref[...], mo_ref[...], vo_ref[...] = _adamw_math(w_ref[...], g_ref[...], m_ref[...], v_ref[...])

    return _call(body, name, (r // tile,), [_rs(tile, c)] * 4, [_rs(tile, c)] * 3, [S((r, c), F32)] * 3,
                 sem=("parallel",))(w, g, m, v)


TAP_VREGS = 32


def _taps(src, w_ref, offsets, tile, put, bias=None):
    c = src.shape[1]
    rp = max(8, TAP_VREGS * 8 * LANE // c // 8 * 8)
    for r0 in range(0, tile, rp):
        acc = jnp.zeros((rp, c), F32) if bias is None else jnp.zeros((rp, c), F32) + bias
        for k, o in enumerate(offsets):
            acc = acc + w_ref[k:k + 1, :] * src[r0 + o:r0 + o + rp, :]
        put(slice(r0, r0 + rp), acc)


def _taps_fwd(sc, w_ref, width, halo, tile, put, bias):
    _taps(sc, w_ref, [halo - (width - 1) + k for k in range(width)], tile, put, bias)


def _taps_bwd_x(sc_d, w_ref, width, tile, put):
    _taps(sc_d, w_ref, [(width - 1) - k for k in range(width)], tile, put)


def _taps_bwd_w(dy, sc, dw_ref, width, halo, tile, i):
    @pl.when(i == 0)
    def _():
        dw_ref[...] = jnp.zeros_like(dw_ref)

    for k in range(width):
        o = halo - (width - 1) + k
        dw_ref[k:k + 1, :] += jnp.sum(dy * sc[o:o + tile, :], axis=0, keepdims=True)


def _ln_stats(x):
    mu = jnp.mean(x, axis=-1, keepdims=True)
    xc = x - mu
    r = lax.rsqrt(jnp.mean(xc * xc, axis=-1, keepdims=True) + EPS)
    return xc * r, r


def conv_group_fwd(proj, cw, cb, lg, lb, name):
    n = proj.shape[0]
    d = cw.shape[1]
    tile = _row_tile(n, 256)
    halo = 32

    def body(v_ref, g_ref, vp_ref, gp_ref, cw_ref, cb_ref, lg_ref, lb_ref, u_ref, u1_ref, sc):
        i = pl.program_id(0)
        first = (i > 0).astype(F32)
        sc[0:halo, :] = vp_ref[tile - halo:, :] * _sigmoid(gp_ref[tile - halo:, :]) * first
        sc[halo:, :] = v_ref[...] * _sigmoid(g_ref[...])
        def put(rows, acc):
            u1_ref[rows, :] = acc

        _taps_fwd(sc, cw_ref, CONV_WIDTH, halo, tile, put, cb_ref[...])
        xh, _ = _ln_stats(u1_ref[...])
        y = xh * lg_ref[...] + lb_ref[...]
        u_ref[...] = (y * _sigmoid(y)).astype(u_ref.dtype)

    return _call(body, name, (n // tile,),
                 [_rs(tile, d, 0), _rs(tile, d, 1), _rs(tile, d, 0, -1), _rs(tile, d, 1, -1),
                  _ps(cw.shape), _ps((1, d)), _ps((1, d)), _ps((1, d))],
                 [_rs(tile, d), _rs(tile, d)], [S((n, d), MXU_DTYPE), S((n, d), F32)],
                 scratch=[pltpu.VMEM((halo + tile, d), F32)], sem=("arbitrary",))(proj, proj, proj, proj, cw, cb, lg, lb)


def conv_group_bwd(du, u1, proj, cw, lg, lb, name):
    n = proj.shape[0]
    d = cw.shape[1]
    tile = _row_tile(n, 256)
    halo = 32
    nt = n // tile

    def body(du_ref, dun_ref, u1_ref, u1n_ref, v_ref, g_ref, vp_ref, gp_ref, cw_ref, lg_ref, lb_ref,
             dp_ref, dcw_ref, dcb_ref, dlg_ref, dlb_ref, sc, sc_d):
        i = pl.program_id(0)

        def ln_swish_bwd(dy_, u1_):
            xh, r = _ln_stats(u1_)
            y = xh * lg_ref[...] + lb_ref[...]
            s = _sigmoid(y)
            dyy = dy_ * s * (1.0 + y * (1.0 - s))
            dxh = dyy * lg_ref[...]
            dx = r * (dxh - jnp.mean(dxh, axis=-1, keepdims=True) - xh * jnp.mean(dxh * xh, axis=-1, keepdims=True))
            return dx, jnp.sum(dyy * xh, axis=0, keepdims=True), jnp.sum(dyy, axis=0, keepdims=True)

        du1, dlg, dlb = ln_swish_bwd(du_ref[...].astype(F32), u1_ref[...])
        du1n, _, _ = ln_swish_bwd(dun_ref[0:halo, :].astype(F32), u1n_ref[0:halo, :])
        sc_d[0:tile, :] = du1
        sc_d[tile:, :] = du1n * (i < nt - 1).astype(F32)
        sc[0:halo, :] = vp_ref[tile - halo:, :] * _sigmoid(gp_ref[tile - halo:, :]) * (i > 0).astype(F32)
        sc[halo:, :] = v_ref[...] * _sigmoid(g_ref[...])

        def put(rows, du0):
            sig = _sigmoid(g_ref[rows, :])
            dp_ref[rows, :d] = (du0 * sig).astype(dp_ref.dtype)
            dp_ref[rows, d:] = (du0 * v_ref[rows, :] * sig * (1.0 - sig)).astype(dp_ref.dtype)

        _taps_bwd_x(sc_d, cw_ref, CONV_WIDTH, tile, put)
        _taps_bwd_w(du1, sc, dcw_ref, CONV_WIDTH, halo, tile, i)
        _acc(dcb_ref, i, jnp.sum(du1, axis=0, keepdims=True))
        _acc(dlg_ref, i, dlg)
        _acc(dlb_ref, i, dlb)

    return _call(body, name, (nt,),
                 [_rs(tile, d), _rs(tile, d, 0, 1, nt), _rs(tile, d), _rs(tile, d, 0, 1, nt),
                  _rs(tile, d, 0), _rs(tile, d, 1), _rs(tile, d, 0, -1), _rs(tile, d, 1, -1),
                  _ps(cw.shape), _ps((1, d)), _ps((1, d))],
                 [_rs(tile, 2 * d), _ps(cw.shape), _ps((1, d)), _ps((1, d)), _ps((1, d))],
                 [S((n, proj.shape[1]), MXU_DTYPE), S(cw.shape, F32), S((1, d), F32), S((1, d), F32), S((1, d), F32)],
                 scratch=[pltpu.VMEM((halo + tile, d), F32), pltpu.VMEM((tile + halo, d), F32)],
                 sem=("arbitrary",))(du, du, u1, u1, proj, proj, proj, proj, cw, lg, lb)


def ssm_conv_fwd(proj, dtr, sw, sb, dtb, name):
    n = proj.shape[0]
    w = sw.shape[1]
    inner = SSM_HEADS * HEAD_DIM
    tile = _row_tile(n, 256)
    halo = 8

    def body(x_ref, xp_ref, dtr_ref, sw_ref, sb_ref, dtb_ref, pre_ref, xs_ref, bc_ref, dt_ref, sc):
        i = pl.program_id(0)
        sc[0:halo, :] = xp_ref[tile - halo:, :] * (i > 0).astype(F32)
        sc[halo:, :] = x_ref[...]
        def put(rows, acc):
            pre_ref[rows, :] = acc

        _taps_fwd(sc, sw_ref, SSM_CONV, halo, tile, put, sb_ref[...])
        pre = pre_ref[...]
        act = pre * _sigmoid(pre)
        xs_ref[...] = act[:, :inner]
        bc_ref[...] = act[:, inner:]
        dt = _softplus(dtr_ref[...] + dtb_ref[...])
        dt_ref[...] = jnp.where(_iota(dt.shape, 1) < SSM_HEADS, dt, 0.0)

    return _call(body, name, (n // tile,),
                 [_rs(tile, w, 2), _rs(tile, w, 2, -1), _rs(tile, LANE), _ps(sw.shape), _ps((1, w)), _ps((1, LANE))],
                 [_rs(tile, w), _rs(tile, inner), _rs(tile, w - inner), _rs(tile, LANE)],
                 [S((n, w), F32), S((n, inner), F32), S((n, w - inner), F32), S((n, LANE), F32)],
                 scratch=[pltpu.VMEM((halo + tile, w), F32)], sem=("arbitrary",))(proj, proj, dtr, sw, sb, dtb)


def ssm_conv_bwd(dxs, dbc, pre, proj, sw, dproj, name):
    n = proj.shape[0]
    w = sw.shape[1]
    inner = SSM_HEADS * HEAD_DIM
    tile = _row_tile(n, 256)
    halo = 8
    nt = n // tile

    def body(dxs_ref, dxsn_ref, dbc_ref, dbcn_ref, pre_ref, pren_ref, x_ref, xp_ref, sw_ref, dp_in_ref,
             dx_ref, dsw_ref, dsb_ref, sc, sc_d):
        i = pl.program_id(0)

        def silu_bwd(d_, p_):
            s = _sigmoid(p_)
            return d_ * s * (1.0 + p_ * (1.0 - s))

        sc_d[0:tile, :inner] = silu_bwd(dxs_ref[...], pre_ref[:, :inner])
        sc_d[0:tile, inner:] = silu_bwd(dbc_ref[...], pre_ref[:, inner:])
        last = (i < nt - 1).astype(F32)
        sc_d[tile:, :inner] = silu_bwd(dxsn_ref[0:halo, :], pren_ref[0:halo, :inner]) * last
        sc_d[tile:, inner:] = silu_bwd(dbcn_ref[0:halo, :], pren_ref[0:halo, inner:]) * last
        sc[0:halo, :] = xp_ref[tile - halo:, :] * (i > 0).astype(F32)
        sc[halo:, :] = x_ref[...]
        dpre = sc_d[0:tile, :]
        def put(rows, acc):
            dx_ref[rows, :] = acc.astype(dx_ref.dtype)

        _taps_bwd_x(sc_d, sw_ref, SSM_CONV, tile, put)
        _taps_bwd_w(dpre, sc, dsw_ref, SSM_CONV, halo, tile, i)
        _acc(dsb_ref, i, jnp.sum(dpre, axis=0, keepdims=True))

    return pl.pallas_call(
        body, name=name, grid=(nt,),
        in_specs=[_rs(tile, inner), _rs(tile, inner, 0, 1, nt), _rs(tile, w - inner), _rs(tile, w - inner, 0, 1, nt),
                  _rs(tile, w), _rs(tile, w, 0, 1, nt), _rs(tile, w, 2), _rs(tile, w, 2, -1), _ps(sw.shape), ANY],
        out_specs=[_rs(tile, w, 2), _ps(sw.shape), _ps((1, w))],
        out_shape=[S(dproj.shape, dproj.dtype), S(sw.shape, F32), S((1, w), F32)],
        scratch_shapes=[pltpu.VMEM((halo + tile, w), F32), pltpu.VMEM((tile + halo, w), F32)],
        input_output_aliases={9: 0},
        compiler_params=pltpu.CompilerParams(dimension_semantics=("arbitrary",), vmem_limit_bytes=VMEM_LIMIT),
    )(dxs, dxs, dbc, dbc, pre, pre, proj, proj, sw, dproj)


def _ssd_prologue(dt_ref, dtT_ref, al_ref, alc_ref):
    row = _iota((CHUNK, CHUNK), 0)
    col = _iota((CHUNK, CHUNK), 1)
    dt = dt_ref[:, :SSM_HEADS]
    a_row = -jnp.exp(al_ref[:, :SSM_HEADS])
    a_col = -jnp.exp(alc_ref[...])
    cs = _01mm((row >= col).astype(F32), dt * a_row)
    csT = _mm01(dtT_ref[...] * a_col, (row <= col).astype(F32))
    return dt, a_row, cs, csT, row, col


def _decay(cs, csT, h, row, col):
    lm = jnp.exp(jnp.where(row >= col, cs[:, h:h + 1] - csT[h:h + 1, :], -1e30))
    lmT = jnp.exp(jnp.where(col >= row, csT[h:h + 1, :] - cs[:, h:h + 1], -1e30))
    return lm, lmT


def ssd_fwd(xs, bc, dt, dtT, alog_row, alog_col, name):
    n, width = xs.shape
    nc = n // CHUNK
    gw = width // SSM_GROUPS
    hpg = SSM_HEADS // SSM_GROUPS
    ns = SSM_STATE

    def body(xs_ref, bc_ref, dt_ref, dtT_ref, al_ref, alc_ref, y_ref, hs_ref, h_sc):
        i = pl.program_id(0)

        @pl.when(i == 0)
        def _():
            h_sc[...] = jnp.zeros_like(h_sc)

        dt, a_row, cs, csT, row, col = _ssd_prologue(dt_ref, dtT_ref, al_ref, alc_ref)
        indT = _head_indicator(width, SSM_HEADS, transposed=True)
        dt_full = _mm01(dt, indT)
        e_full = jnp.exp(_mm01(cs, indT))
        dte_full = jnp.exp(_mm01(cs[CHUNK - 1:CHUNK, :] - cs, indT))
        xt = xs_ref[...] * dt_full
        hs_ref[0] = h_sc[...]
        lo = _iota((CHUNK, 2 * HEAD_DIM), 1) < HEAD_DIM
        groups = range(SSM_GROUPS)
        bgs = [bc_ref[:, g * ns:(g + 1) * ns] for g in groups]
        cgs = [bc_ref[:, (SSM_GROUPS + g) * ns:(SSM_GROUPS + g + 1) * ns] for g in groups]
        gms = [_mm(cg, bg, NT) for cg, bg in zip(cgs, bgs)]
        yoffs = [e_full[:, g * gw:(g + 1) * gw] * _mm(cgs[g], h_sc[g * gw:(g + 1) * gw, :], NT) for g in groups]
        sgs = [_mm(xt[:, g * gw:(g + 1) * gw] * dte_full[:, g * gw:(g + 1) * gw], bgs[g], TN) for g in groups]
        ms = [gms[h // hpg] * _decay(cs, csT, h, row, col)[0] for h in range(SSM_HEADS)]
        for pr in range(SSM_HEADS // 2):
            c0 = 2 * pr * HEAD_DIM
            xp = xt[:, c0:c0 + 2 * HEAD_DIM]
            yd = jnp.where(lo, _mm(ms[2 * pr], xp), _mm(ms[2 * pr + 1], xp))
            y_ref[:, c0:c0 + 2 * HEAD_DIM] = yd + yoffs[2 * pr // hpg][:, c0 % gw:c0 % gw + 2 * HEAD_DIM]
        for h in range(SSM_HEADS):
            r0 = h * HEAD_DIM
            h_sc[r0:r0 + HEAD_DIM, :] = (h_sc[r0:r0 + HEAD_DIM, :] * jnp.exp(csT[h:h + 1, CHUNK - 1:CHUNK])
                                         + sgs[h // hpg][r0 % gw:r0 % gw + HEAD_DIM, :])

    bcw = bc.shape[1]
    return _call(body, name, (nc,),
                 [_rs(CHUNK, width), _rs(CHUNK, bcw), _rs(CHUNK, LANE), pl.BlockSpec((SSM_HEADS, CHUNK), lambda i: (0, i)),
                  _ps((1, LANE)), _ps((SSM_HEADS, 1))],
                 [_rs(CHUNK, width), pl.BlockSpec((1, width, ns), lambda i: (i, 0, 0))],
                 [S((n, width), F32), S((nc, width, ns), F32)],
                 scratch=[pltpu.VMEM((width, ns), F32)], sem=("arbitrary",))(xs, bc, dt, dtT, alog_row, alog_col)


def ssd_bwd(xs, bc, dt, dtT, alog_row, alog_col, hs, dy, dxs_skip, name):
    n, width = xs.shape
    nc = n // CHUNK
    gw = width // SSM_GROUPS
    hpg = SSM_HEADS // SSM_GROUPS
    ns = SSM_STATE
    bcw = bc.shape[1]

    def body(xs_ref, bc_ref, dt_ref, dtT_ref, al_ref, alc_ref, hs_ref, dy_ref, skip_ref,
             dxs_ref, dbc_ref, ddtr_ref, dal_ref, ddtb_ref, dh_sc, dxt_sc):
        i = pl.program_id(0)

        @pl.when(i == 0)
        def _():
            dh_sc[...] = jnp.zeros_like(dh_sc)

        dt, a_row, cs, csT, row, col = _ssd_prologue(dt_ref, dtT_ref, al_ref, alc_ref)
        indT = _head_indicator(width, SSM_HEADS, transposed=True)
        ind = _head_indicator(width, SSM_HEADS)
        dt_full = _mm01(dt, indT)
        e_full = jnp.exp(_mm01(cs, indT))
        cs_last = cs[CHUNK - 1:CHUNK, :]
        dte = jnp.exp(cs_last - cs)
        dte_full = _mm01(dte, indT)
        xs_ = xs_ref[...]
        xt = xs_ * dt_full
        dy_ = dy_ref[...]
        hmat = hs_ref[0]
        ds = dh_sc[...]
        lo = _iota((CHUNK, 2 * HEAD_DIM), 1) < HEAD_DIM
        head_lane = _iota((1, SSM_HEADS), 1)
        dcs = jnp.zeros((CHUNK, SSM_HEADS), F32)
        ddte = jnp.zeros((CHUNK, SSM_HEADS), F32)
        for g in range(SSM_GROUPS):
            sl = slice(g * gw, (g + 1) * gw)
            bg = bc_ref[:, g * ns:(g + 1) * ns]
            cg = bc_ref[:, (SSM_GROUPS + g) * ns:(SSM_GROUPS + g + 1) * ns]
            gm = _mm(cg, bg, NT)
            gmT = _mm(bg, cg, NT)
            hg = hmat[sl, :]
            dsg = ds[sl, :]
            dyg = dy_[:, sl]
            xtg = xt[:, sl]
            yoff = e_full[:, sl] * _mm(cg, hg, NT)
            edy = e_full[:, sl] * dyg
            bds = _mm(bg, dsg, NT)
            dxt_g = dte_full[:, sl] * bds
            ddte = ddte + _mm01(xtg * bds, ind[sl, :])
            dcs = dcs + _mm01(dyg * yoff, ind[sl, :])
            db = _mm(xtg * dte_full[:, sl], dsg)
            dc = _mm(edy, hg)
            dhc = _mm(edy, cg, TN)
            hs_g = range(g * hpg, (g + 1) * hpg)
            pair = lambda a, h: a[:, (h - g * hpg) // 2 * 2 * HEAD_DIM:((h - g * hpg) // 2 + 1) * 2 * HEAD_DIM]
            decs = [_decay(cs, csT, h, row, col) for h in hs_g]
            xms = [jnp.where(lo if h % 2 == 0 else jnp.logical_not(lo), pair(xtg, h), 0.0) for h in hs_g]
            dms = [_mm(pair(dyg, h), xm, NT) for h, xm in zip(hs_g, xms)]
            dmTs = [_mm(xm, pair(dyg, h), NT) for h, xm in zip(hs_g, xms)]
            mTs = [gmT * lmT for _, lmT in decs]
            for h, (lm, _), dm, dmT, mT in zip(hs_g, decs, dms, dmTs, mTs):
                z = jnp.sum(dm * (gm * lm), axis=1, keepdims=True) - jnp.sum(dmT * mT, axis=1, keepdims=True)
                dcs = dcs + z * (head_lane == h).astype(F32)
            dgs = sum(dm * lm for dm, (lm, _) in zip(dms, decs))
            dgTs = sum(dmT * lmT for dmT, (_, lmT) in zip(dmTs, decs))
            rrs = [_mm(mT, pair(dyg, h)) for h, mT in zip(hs_g, mTs)]
            for pr in range(hpg // 2):
                c0 = 2 * pr * HEAD_DIM
                dxt_sc[:, g * gw + c0:g * gw + c0 + 2 * HEAD_DIM] = (jnp.where(lo, rrs[2 * pr], rrs[2 * pr + 1])
                                                                     + dxt_g[:, c0:c0 + 2 * HEAD_DIM])
            dbc_ref[:, g * ns:(g + 1) * ns] = db + _mm(dgTs, cg)
            dbc_ref[:, (SSM_GROUPS + g) * ns:(SSM_GROUPS + g + 1) * ns] = dc + _mm(dgs, bg)
            for hh in range(hpg):
                h = g * hpg + hh
                r0 = h * HEAD_DIM
                dh_sc[r0:r0 + HEAD_DIM, :] = (dhc[hh * HEAD_DIM:(hh + 1) * HEAD_DIM, :]
                                              + jnp.exp(csT[h:h + 1, CHUNK - 1:CHUNK]) * ds[r0:r0 + HEAD_DIM, :])
        t = ddte * dte
        per_head = jnp.sum(jnp.sum(ds * hmat, axis=1, keepdims=True) * ind, axis=0, keepdims=True)
        last_add = jnp.sum(t, axis=0, keepdims=True) + jnp.exp(cs_last) * per_head
        dcs = dcs - t + jnp.where(_iota((CHUNK, SSM_HEADS), 0) == CHUNK - 1, last_add, 0.0)
        dadt = _01mm((row <= col).astype(F32), dcs)
        dxt = dxt_sc[...]
        ddt = dadt * a_row + _mm01(dxt * xs_, ind)
        dxs_ref[...] = dxt * dt_full + skip_ref[...]
        ddtr = ddt * (1.0 - jnp.exp(-dt))
        ddtr_ref[...] = jnp.zeros_like(ddtr_ref)
        ddtr_ref[:, :SSM_HEADS] = ddtr.astype(ddtr_ref.dtype)
        _acc(dal_ref, i, jnp.sum(dadt * dt, axis=0, keepdims=True) * a_row)
        _acc(ddtb_ref, i, jnp.sum(ddtr, axis=0, keepdims=True))

    rev = lambda i: (nc - 1 - i, 0)
    return _call(body, name, (nc,),
                 [pl.BlockSpec((CHUNK, width), rev), pl.BlockSpec((CHUNK, bcw), rev), pl.BlockSpec((CHUNK, LANE), rev),
                  pl.BlockSpec((SSM_HEADS, CHUNK), lambda i: (0, nc - 1 - i)), _ps((1, LANE)), _ps((SSM_HEADS, 1)),
                  pl.BlockSpec((1, width, ns), lambda i: (nc - 1 - i, 0, 0)), pl.BlockSpec((CHUNK, width), rev),
                  pl.BlockSpec((CHUNK, width), rev)],
                 [pl.BlockSpec((CHUNK, width), rev), pl.BlockSpec((CHUNK, bcw), rev), pl.BlockSpec((CHUNK, LANE), rev),
                  _ps((1, SSM_HEADS)), _ps((1, SSM_HEADS))],
                 [S((n, width), F32), S((n, bcw), F32), S((n, LANE), MXU_DTYPE), S((1, SSM_HEADS), F32), S((1, SSM_HEADS), F32)],
                 scratch=[pltpu.VMEM((width, ns), F32), pltpu.VMEM((CHUNK, width), F32)],
                 sem=("arbitrary",))(xs, bc, dt, dtT, alog_row, alog_col, hs, dy, dxs_skip)


def ssm_gate_fwd(yssd, xs, proj, dfull, gamma, name):
    n, d = yssd.shape
    tile = _row_tile(n, 512)
    gw = d // SSM_GROUPS

    def body(y_ref, xs_ref, z_ref, df_ref, gm_ref, o_ref):
        z = z_ref[...]
        y2 = (y_ref[...] + df_ref[...] * xs_ref[...]) * (z * _sigmoid(z))
        for g in range(SSM_GROUPS):
            yg = y2[:, g * gw:(g + 1) * gw]
            r = lax.rsqrt(jnp.mean(yg * yg, axis=-1, keepdims=True) + EPS)
            o_ref[:, g * gw:(g + 1) * gw] = (yg * r * gm_ref[:, g * gw:(g + 1) * gw]).astype(o_ref.dtype)

    return _call(body, name, (n // tile,), [_rs(tile, d), _rs(tile, d), _rs(tile, d, 2), _ps((1, d)), _ps((1, d))],
                 _rs(tile, d), S((n, d), MXU_DTYPE), sem=("parallel",))(yssd, xs, proj, dfull, gamma)


def ssm_gate_bwd(dy3, yssd, xs, proj, dfull, gamma, dproj, name):
    n, d = yssd.shape
    tile = _row_tile(n, 512)
    gw = d // SSM_GROUPS

    def body(dy_ref, y_ref, xs_ref, z_ref, df_ref, gm_ref, dp_in_ref, dys_ref, dxs_ref, dz_ref, dgm_ref, dd_ref):
        i = pl.program_id(0)
        z = z_ref[...]
        s = _sigmoid(z)
        xs_ = xs_ref[...]
        y1 = y_ref[...] + df_ref[...] * xs_
        y2 = y1 * (z * s)
        dy_ = dy_ref[...].astype(F32)
        dgm = []
        dy2 = []
        for g in range(SSM_GROUPS):
            sl = slice(g * gw, (g + 1) * gw)
            dxg, dgg = _rms_bwd_math(y2[:, sl], gm_ref[:, sl], dy_[:, sl])
            dy2.append(dxg)
            dgm.append(dgg)
        dy2 = jnp.concatenate(dy2, axis=1)
        dy1 = dy2 * (z * s)
        dys_ref[...] = dy1
        dxs_ref[...] = dy1 * df_ref[...]
        dz_ref[...] = (dy2 * y1 * s * (1.0 + z * (1.0 - s))).astype(dz_ref.dtype)
        _acc(dgm_ref, i, jnp.concatenate(dgm, axis=1))
        colsum = jnp.broadcast_to(jnp.sum(dy1 * xs_, axis=0, keepdims=True), (8, d))
        _acc(dd_ref, i, _mm01(colsum, _head_indicator(d, SSM_HEADS))[0:1, :])

    return pl.pallas_call(
        body, name=name, grid=(n // tile,),
        in_specs=[_rs(tile, d), _rs(tile, d), _rs(tile, d), _rs(tile, d, 2), _ps((1, d)), _ps((1, d)), ANY],
        out_specs=[_rs(tile, d), _rs(tile, d), _rs(tile, d, 2), _ps((1, d)), _ps((1, SSM_HEADS))],
        out_shape=[S((n, d), F32), S((n, d), F32), S(dproj.shape, dproj.dtype), S((1, d), F32), S((1, SSM_HEADS), F32)],
        input_output_aliases={6: 2},
        compiler_params=pltpu.CompilerParams(dimension_semantics=("arbitrary",), vmem_limit_bytes=VMEM_LIMIT),
    )(dy3, yssd, xs, proj, dfull, gamma, dproj)


def _rope128(x, cos, sin_signed):
    half = HEAD_DIM // 2
    lane = _iota(x.shape, 1)
    partner = jnp.where((lane % HEAD_DIM) < half, pltpu.roll(x, LANE - half, 1), pltpu.roll(x, half, 1))
    return x * cos + partner * sin_signed


def rope_fwd(qkv, cos, sin, name):
    n, w = qkv.shape
    qw = ATT_HEADS * HEAD_DIM
    kw = ATT_KV_HEADS * HEAD_DIM
    tile = _row_tile(n, 512)

    def body(x_ref, c_ref, s_ref, q_ref, k_ref, v_ref):
        c, s = c_ref[...], s_ref[...]
        for j in range(qw // LANE):
            q_ref[:, j * LANE:(j + 1) * LANE] = _rope128(x_ref[:, j * LANE:(j + 1) * LANE], c, s).astype(q_ref.dtype)
        for j in range(kw // LANE):
            k_ref[:, j * LANE:(j + 1) * LANE] = _rope128(x_ref[:, qw + j * LANE:qw + (j + 1) * LANE], c, s).astype(k_ref.dtype)
        v_ref[...] = x_ref[:, qw + kw:].astype(v_ref.dtype)

    return _call(body, name, (n // tile,), [_rs(tile, w), _rs(tile, LANE), _rs(tile, LANE)],
                 [_rs(tile, qw), _rs(tile, kw), _rs(tile, kw)],
                 [S((n, qw), MXU_DTYPE), S((n, kw), MXU_DTYPE), S((n, kw), MXU_DTYPE)], sem=("parallel",))(qkv, cos, sin)


ATT_GROUP = ATT_HEADS // ATT_KV_HEADS


def _attn_mask(i):
    row = _iota((ATT_GROUP * WINDOW, 2 * WINDOW), 0) % WINDOW
    s = _iota((ATT_GROUP * WINDOW, 2 * WINDOW), 1)
    return (s > row) & (s <= row + WINDOW) & ((s >= WINDOW) | (i > 0))


def _stack_heads(ref, j, kh, lo):
    parts = []
    for t in range(ATT_GROUP):
        h = ATT_GROUP * j + t
        blk = ref[:, (h // 2) * LANE:(h // 2 + 1) * LANE]
        blk = jnp.where(lo if h % 2 == 0 else jnp.logical_not(lo), blk, jnp.zeros_like(blk))
        parts.append(blk if h % 2 == kh else pltpu.roll(blk, HEAD_DIM, 1))
    return jnp.concatenate(parts, axis=0)


def _unstack_heads(stacked, j, kh, lo, put):
    for t in range(0, ATT_GROUP, 2):
        h = ATT_GROUP * j + t
        even = stacked[t * WINDOW:(t + 1) * WINDOW, :]
        odd = stacked[(t + 1) * WINDOW:(t + 2) * WINDOW, :]
        even = even if kh == 0 else pltpu.roll(even, HEAD_DIM, 1)
        odd = odd if kh == 1 else pltpu.roll(odd, HEAD_DIM, 1)
        put(h // 2, jnp.where(lo, even, odd))


def _per_head_rows(ref, j):
    return jnp.concatenate([ref[:, ATT_GROUP * j + t:ATT_GROUP * j + t + 1] for t in range(ATT_GROUP)], axis=0)


def _per_head_scalar(ref, j):
    rows = _iota((ATT_GROUP * WINDOW, 1), 0) // WINDOW
    out = jnp.zeros((ATT_GROUP * WINDOW, 1), F32)
    for t in range(ATT_GROUP):
        out = out + jnp.where(rows == t, ref[:, ATT_GROUP * j + t:ATT_GROUP * j + t + 1], 0.0)
    return out


def attn_fwd(q, k, v, sinks, name):
    n, qw = q.shape
    kw = k.shape[1]
    nb = n // WINDOW
    scale = HEAD_DIM ** -0.5

    def body(q_ref, kc_ref, kp_ref, vc_ref, vp_ref, sk_ref, o_ref, lse_ref):
        i = pl.program_id(0)
        valid = _attn_mask(i)
        lo = _iota((WINDOW, LANE), 1) < HEAD_DIM
        k2 = jnp.concatenate([kp_ref[...], kc_ref[...]], axis=0)
        v2 = jnp.concatenate([vp_ref[...], vc_ref[...]], axis=0)
        lane1 = _iota((1, LANE), 1)
        lse = jnp.zeros((WINDOW, LANE), F32)

        def put_o(qb, val):
            o_ref[:, qb * LANE:(qb + 1) * LANE] = val.astype(o_ref.dtype)

        kv = [(j, j // 2, j % 2) for j in range(ATT_KV_HEADS)]
        logits = [jnp.where(valid, _mm(_stack_heads(q_ref, j, kh, lo), k2[:, kb * LANE:(kb + 1) * LANE], NT) * scale, -1e30)
                  for j, kb, kh in kv]
        sks = [_per_head_scalar(sk_ref, j) for j, _, _ in kv]
        ms = [jnp.maximum(jnp.max(l, axis=-1, keepdims=True), sk) for l, sk in zip(logits, sks)]
        es = [jnp.exp(l - m) for l, m in zip(logits, ms)]
        dens = [jnp.sum(e, axis=-1, keepdims=True) + jnp.exp(sk - m) for e, sk, m in zip(es, sks, ms)]
        for (j, kb, kh), e, m, den in zip(kv, es, ms, dens):
            lse4 = m + jnp.log(den)
            for t in range(ATT_GROUP):
                lse = lse + lse4[t * WINDOW:(t + 1) * WINDOW, :] * (lane1 == ATT_GROUP * j + t).astype(F32)
            _unstack_heads(_mm(e * (1.0 / den), v2[:, kb * LANE:(kb + 1) * LANE]), j, kh, lo, put_o)
        lse_ref[...] = lse

    return _call(body, name, (nb,),
                 [_rs(WINDOW, qw), _rs(WINDOW, kw), _rs(WINDOW, kw, 0, -1), _rs(WINDOW, kw), _rs(WINDOW, kw, 0, -1), _ps((1, LANE))],
                 [_rs(WINDOW, qw), _rs(WINDOW, LANE)], [S((n, qw), MXU_DTYPE), S((n, LANE), F32)],
                 sem=("parallel",))(q, k, k, v, v, sinks)


def attn_bwd(q, k, v, o, do, lse, sinks, name):
    n, qw = q.shape
    kw = k.shape[1]
    nb = n // WINDOW
    scale = HEAD_DIM ** -0.5

    def body(q_ref, kc_ref, kp_ref, vc_ref, vp_ref, o_ref, do_ref, lse_ref, sk_ref,
             dq_ref, dka_ref, dkb_ref, dva_ref, dvb_ref, dsk_ref):
        i = pl.program_id(0)
        valid = _attn_mask(i)
        lo = _iota((WINDOW, LANE), 1) < HEAD_DIM
        k2 = jnp.concatenate([kp_ref[...], kc_ref[...]], axis=0)
        v2 = jnp.concatenate([vp_ref[...], vc_ref[...]], axis=0)
        lane1 = _iota((1, LANE), 1)
        do_ = do_ref[...].astype(F32)
        delta = _mm01(do_ * o_ref[...].astype(F32), _head_indicator(qw, ATT_HEADS))
        dk2 = [jnp.zeros((2 * WINDOW, LANE), F32) for _ in range(kw // LANE)]
        dv2 = [jnp.zeros((2 * WINDOW, LANE), F32) for _ in range(kw // LANE)]
        dsk = jnp.zeros((1, LANE), F32)

        def put_dq(qb, val):
            dq_ref[:, qb * LANE:(qb + 1) * LANE] = val

        kv = [(j, j // 2, j % 2) for j in range(ATT_KV_HEADS)]
        q4s = [_stack_heads(q_ref, j, kh, lo) for j, _, kh in kv]
        do4s = [_stack_heads(do_ref, j, kh, lo) for j, _, kh in kv]
        kks = [k2[:, kb * LANE:(kb + 1) * LANE] for _, kb, _ in kv]
        vvs = [v2[:, kb * LANE:(kb + 1) * LANE] for _, kb, _ in kv]
        lses = [_per_head_rows(lse_ref, j) for j, _, _ in kv]
        dls = [jnp.concatenate([delta[:, ATT_GROUP * j + t:ATT_GROUP * j + t + 1] for t in range(ATT_GROUP)], axis=0)
               for j, _, _ in kv]
        ps = [jnp.exp(jnp.where(valid, _mm(q4, kk, NT) * scale, -1e30) - lse4) for q4, kk, lse4 in zip(q4s, kks, lses)]
        dss = [p * (_mm(do4, vv, NT) - dl) * scale for p, do4, vv, dl in zip(ps, do4s, vvs, dls)]
        for (j, kb, kh), q4, do4, kk, lse4, dl, p, ds in zip(kv, q4s, do4s, kks, lses, dls, ps, dss):
            sd = jnp.exp(_per_head_scalar(sk_ref, j) - lse4) * dl
            for t in range(ATT_GROUP):
                dsk = dsk - (jnp.sum(sd[t * WINDOW:(t + 1) * WINDOW, :], axis=0, keepdims=True)
                             * (lane1 == ATT_GROUP * j + t).astype(F32))
            _unstack_heads(_mm(ds, kk), j, kh, lo, put_dq)
            dk2[kb] = dk2[kb] + _mm(ds, q4, TN)
            dv2[kb] = dv2[kb] + _mm(p, do4, TN)
        for kb in range(kw // LANE):
            dkb_ref[:, kb * LANE:(kb + 1) * LANE] = dk2[kb][0:WINDOW, :]
            dka_ref[:, kb * LANE:(kb + 1) * LANE] = dk2[kb][WINDOW:, :]
            dvb_ref[:, kb * LANE:(kb + 1) * LANE] = dv2[kb][0:WINDOW, :]
            dva_ref[:, kb * LANE:(kb + 1) * LANE] = dv2[kb][WINDOW:, :]
        _acc(dsk_ref, i, dsk)

    return _call(body, name, (nb,),
                 [_rs(WINDOW, qw), _rs(WINDOW, kw), _rs(WINDOW, kw, 0, -1), _rs(WINDOW, kw), _rs(WINDOW, kw, 0, -1),
                  _rs(WINDOW, qw), _rs(WINDOW, qw), _rs(WINDOW, LANE), _ps((1, LANE))],
                 [_rs(WINDOW, qw)] + [_rs(WINDOW, kw)] * 4 + [_ps((1, LANE))],
                 [S((n, qw), F32)] + [S((n, kw), F32)] * 4 + [S((1, LANE), F32)],
                 sem=("arbitrary",))(q, k, k, v, v, o, do, lse, sinks)


def attn_grad_merge(dq, dka, dkb, dva, dvb, cos, sin, name):
    n, qw = dq.shape
    kw = dka.shape[1]
    nb = n // WINDOW
    w = qw + 2 * kw

    def body(dq_ref, dka_ref, dkb_ref, dva_ref, dvb_ref, c_ref, s_ref, o_ref, db_ref):
        i = pl.program_id(0)
        c, s = c_ref[...], -s_ref[...]
        nxt = (i < nb - 1).astype(F32)

        @pl.when(i == 0)
        def _():
            db_ref[...] = jnp.zeros_like(db_ref)

        def put(c0, val):
            o_ref[:, c0:c0 + val.shape[1]] = val.astype(o_ref.dtype)
            db_ref[:, c0:c0 + val.shape[1]] += jnp.sum(val, axis=0, keepdims=True)

        for j in range(qw // LANE):
            put(j * LANE, _rope128(dq_ref[:, j * LANE:(j + 1) * LANE], c, s))
        for j in range(kw // LANE):
            sl = slice(j * LANE, (j + 1) * LANE)
            put(qw + j * LANE, _rope128(dka_ref[:, sl] + dkb_ref[:, sl] * nxt, c, s))
        put(qw + kw, dva_ref[...] + dvb_ref[...] * nxt)

    return _call(body, name, (nb,),
                 [_rs(WINDOW, qw), _rs(WINDOW, kw), _rs(WINDOW, kw, 0, 1, nb), _rs(WINDOW, kw), _rs(WINDOW, kw, 0, 1, nb),
                  _rs(WINDOW, LANE), _rs(WINDOW, LANE)],
                 [_rs(WINDOW, w), _ps((1, w))], [S((n, w), MXU_DTYPE), S((1, w), F32)],
                 sem=("arbitrary",))(dq, dka, dkb, dva, dvb, cos, sin)


def _row(v):
    return v.reshape(1, -1)


def _pad_lanes(v, width=LANE):
    return jnp.pad(v.reshape(1, -1), ((0, 0), (0, width - v.size)))


class LayerWeights(dict):
    def __init__(self, small, fetch):
        super().__init__(small)
        self.fetch = fetch

    def need(self, k, after):
        if k not in self:
            self[k] = self.fetch(k, after)
        return self[k]


def ffn_fwd(h, g, w, keys, tag):
    xn, u, hm = swiglu_in(h, _row(g), w.need(keys[0], h), f"{tag}_in")
    return matmul(hm, w.need(keys[1], hm), "nn", f"{tag}_out", scale=0.5, res=h), (h, xn, u, hm)


class GradSink:
    ORDER = ("ffn1_w_out", "ffn2_w_out", "ple_gate_w", "att_w_o", "hyb_w_out", "ffn1_w_in", "ffn2_w_in", "att_w_qkv",
             "ple_proj_w", "hyb_w_in")

    def __init__(self, shard_shapes, bucket_of):
        self.where, rows = {}, {}
        for k in self.ORDER:
            n, r, c = shard_shapes[k]
            for li in range(n):
                layer = li if k in PER_LAYER else 2 * li + (0 if k in EVEN_ONLY else 1)
                rows_b = rows.setdefault(bucket_of(layer, _stage(k)), {})
                key = c if r % 32 == 0 else k
                off = -(-rows_b.get(key, (0, c))[0] // r) * r
                rows_b[key] = (-(-(off + r) // 32) * 32, c)
                self.where[k, li] = (bucket_of(layer, _stage(k)), key, "r" if _shard_axis(k) == 1 else "c", off, r)
        self.bufs = {b: {key: lax.empty((N_CHIPS, r, c), MXU_DTYPE) for key, (r, c) in rows_b.items()}
                     for b, rows_b in rows.items()}

    def mm(self, k, li, a, b, name, scale=None, c0=0, paired=False):
        bucket, key, kind, off, r = self.where[k, li]
        buf = self.bufs[bucket][key]
        slot = Slot(buf, kind, r if kind == "r" else buf.shape[2], off, c0, paired)
        self.bufs[bucket][key] = matmul(a, b, "tn", name, scale=scale, into=slot)

    def put(self, k, li, chip_major):
        b, key = self.where[k, li][:2]
        pad = self.bufs[b][key].shape[1] - chip_major.shape[1]
        self.bufs[b][key] = jnp.pad(chip_major.astype(self.bufs[b][key].dtype), ((0, 0), (0, pad), (0, 0)))


def ffn_bwd(dh, g, w_in, w_out, saved, tag, sink, keys, layer, after, colsum=False):
    h, xn, u, hm = saved
    sink.mm(keys[1], layer, hm, dh, f"{tag}_dwout", scale=0.5)
    du = swiglu_out_bwd(dh, w_out, u, after, f"{tag}_dhm")
    sink.mm(keys[0], layer, xn, du, f"{tag}_dwin", paired=True)
    outs = nt_rms_bwd(du, ColSharded(w_in.arr, paired=True), h, _row(g), dh, f"{tag}_dxn", colsum=colsum)
    return (outs[0], outs[1].reshape(-1)) + ((outs[2],) if colsum else ())


def _hyb_params(w):
    d = w["conv_dw_b"].size
    inner = SSM_HEADS * HEAD_DIM
    main = 3 * d + w["ssm_conv_b"].size
    return dict(
        w_main=w["hyb_w_in"][:main], w_dt=jnp.pad(w["hyb_w_in"][main:], ((0, LANE - SSM_HEADS), (0, 0))),
        cw=jnp.pad(w["conv_dw_w"], ((0, 32 - CONV_WIDTH), (0, 0))), cb=_row(w["conv_dw_b"]),
        lg=_row(w["conv_ln_g"]), lb=_row(w["conv_ln_b"]),
        sw=jnp.pad(w["ssm_conv_w"], ((0, 8 - SSM_CONV), (0, 0))), sb=_row(w["ssm_conv_b"]),
        dtb=_pad_lanes(w["ssm_dt_bias"]), al_row=_pad_lanes(w["ssm_a_log"]), al_col=w["ssm_a_log"].reshape(-1, 1),
        dfull=_row(jnp.repeat(w["ssm_d"], HEAD_DIM)), gamma=_row(w["ssm_norm"]), d=d, inner=inner, main=main)


def hyb_fwd(h, w, tag):
    w.need("hyb_w_in", h)
    q = _hyb_params(w)
    proj, xn = rms_matmul(h, _row(w["norm_mix"]), q["w_main"], "nt", f"{tag}_in")
    dtr = matmul(xn, q["w_dt"], "nt", f"{tag}_in_dt")
    u, u1 = conv_group_fwd(proj, q["cw"], q["cb"], q["lg"], q["lb"], f"{tag}_conv")
    pre, xs, bc, dt = ssm_conv_fwd(proj, dtr, q["sw"], q["sb"], q["dtb"], f"{tag}_sconv")
    dtT = dt[:, :SSM_HEADS].T
    yssd, hs = ssd_fwd(xs, bc, dt, dtT, q["al_row"], q["al_col"], f"{tag}_ssd")
    y = ssm_gate_fwd(yssd, xs, proj, q["dfull"], q["gamma"], f"{tag}_gate")
    wo = w.need("hyb_w_out", u)
    h2 = matmul(u, wo[:q["d"]], "nn", f"{tag}_out_a", res=h)
    h2 = matmul(y, wo[q["d"]:], "nn", f"{tag}_out_b", res=h2)
    return h2, (h, xn, proj, u, u1, pre, xs, bc, dt, dtT, yssd, hs, y)


def hyb_bwd(dh, w, saved, tag, sink, layer):
    q = _hyb_params(w)
    h, xn, proj, u, u1, pre, xs, bc, dt, dtT, yssd, hs, y = saved
    du = matmul(dh, w["hyb_w_out"][:q["d"]], "nt", f"{tag}_du")
    dy3 = matmul(dh, w["hyb_w_out"][q["d"]:], "nt", f"{tag}_dy")
    sink.mm("hyb_w_out", layer, u, dh, f"{tag}_dwo_a", c0=0)
    sink.mm("hyb_w_out", layer, y, dh, f"{tag}_dwo_b", c0=N_CHIPS // 2)
    dproj, dcw, dcb, dlg, dlb = conv_group_bwd(du, u1, proj, q["cw"], q["lg"], q["lb"], f"{tag}_dconv")
    dyssd, dxs_skip, dproj, dgamma, dd = ssm_gate_bwd(dy3, yssd, xs, proj, q["dfull"], q["gamma"], dproj, f"{tag}_dgate")
    dxs, dbc, ddtr, dalog, ddtb = ssd_bwd(xs, bc, dt, dtT, q["al_row"], q["al_col"], hs, dyssd, dxs_skip, f"{tag}_dssd")
    dproj, dsw, dsb = ssm_conv_bwd(dxs, dbc, pre, proj, q["sw"], dproj, f"{tag}_dsconv")
    dw_in = jnp.concatenate([matmul(dproj, xn, "tn", f"{tag}_dwin"),
                             matmul(ddtr, xn, "tn", f"{tag}_dwin_dt")[:SSM_HEADS]], axis=0)
    sink.put("hyb_w_in", layer, dw_in.reshape((N_CHIPS, -1) + dw_in.shape[1:]))
    dh2, dg = nt_rms_bwd(dproj, q["w_main"], h, _row(w["norm_mix"]), dh, f"{tag}_dxn", extra=(ddtr, q["w_dt"]), b_kd=True)
    grads = dict(norm_mix=dg.reshape(-1), conv_dw_w=dcw[:CONV_WIDTH], conv_dw_b=dcb.reshape(-1),
                 conv_ln_g=dlg.reshape(-1), conv_ln_b=dlb.reshape(-1), ssm_conv_w=dsw[:SSM_CONV], ssm_conv_b=dsb.reshape(-1),
                 ssm_dt_bias=ddtb.reshape(-1), ssm_a_log=dalog.reshape(-1), ssm_d=dd.reshape(-1), ssm_norm=dgamma.reshape(-1))
    return dh2, grads


def rope_tables(n):
    half = HEAD_DIM // 2
    inv = ROPE_THETA ** (-jnp.arange(0, HEAD_DIM, 2, dtype=F32) / HEAD_DIM)
    ang = jnp.arange(n, dtype=F32)[:, None] * inv[None, :]
    cos, sin = jnp.cos(ang), jnp.sin(ang)
    reps = LANE // HEAD_DIM
    return jnp.tile(jnp.concatenate([cos, cos], axis=1), (1, reps)), jnp.tile(jnp.concatenate([-sin, sin], axis=1), (1, reps))


def att_fwd(h, w, tables, tag):
    cos, sin = tables
    qkv, xn = rms_matmul(h, _row(w["norm_mix"]), w.need("att_w_qkv", h), "nn", f"{tag}_qkv", bias=_row(w["att_b_qkv"]))
    q, k, v = rope_fwd(qkv, cos, sin, f"{tag}_rope")
    sinks = _pad_lanes(w["att_sinks"])
    o, lse = attn_fwd(q, k, v, sinks, f"{tag}_attn")
    h2 = matmul(o, w.need("att_w_o", o), "nn", f"{tag}_o", bias=_row(w["att_b_o"]), res=h)
    return h2, (h, xn, q, k, v, o, lse, sinks)


def att_bwd(dh, dh_colsum, w, saved, tables, tag, sink, layer):
    cos, sin = tables
    h, xn, q, k, v, o, lse, sinks = saved
    do = matmul(dh, w["att_w_o"], "nt", f"{tag}_do")
    sink.mm("att_w_o", layer, o, dh, f"{tag}_dwo")
    dq, dka, dkb, dva, dvb, dsk = attn_bwd(q, k, v, o, do, lse, sinks, f"{tag}_dattn")
    dqkv, dbqkv = attn_grad_merge(dq, dka, dkb, dva, dvb, cos, sin, f"{tag}_drope")
    sink.mm("att_w_qkv", layer, xn, dqkv, f"{tag}_dwqkv")
    dh2, dg = nt_rms_bwd(dqkv, w["att_w_qkv"], h, _row(w["norm_mix"]), dh, f"{tag}_dxn")
    grads = dict(norm_mix=dg.reshape(-1), att_b_qkv=dbqkv.reshape(-1), att_sinks=dsk[0, :ATT_HEADS],
                 att_b_o=dh_colsum.reshape(-1))
    return dh2, grads


def ple_block_fwd(h, pe, w, tag):
    pp = matmul(pe, w.need("ple_proj_w", h), "nn", f"{tag}_proj")
    out, gl, xn = ple_fwd(h, _row(w["ple_norm"]), w.need("ple_gate_w", h), pp, f"{tag}_gate")
    return out, (h, xn, gl, pp, pe)


def ple_block_bwd(dh, w, saved, tag, sink, layer, after):
    h, xn, gl, pp, pe = saved
    dh2, dpp, dgl, dg = ple_bwd(dh, gl, pp, w["ple_gate_w"], h, _row(w["ple_norm"]), after, f"{tag}_dgate")
    sink.mm("ple_proj_w", layer, pe, dpp, f"{tag}_dwp")
    sink.mm("ple_gate_w", layer, xn, dgl, f"{tag}_dwg")
    return dh2, dict(ple_norm=dg.reshape(-1))


PER_LAYER = ("norm_ffn1", "ffn1_w_in", "ffn1_w_out", "norm_mix", "norm_ffn2", "ffn2_w_in", "ffn2_w_out",
             "ple_norm", "ple_gate_w", "ple_proj_w")
EVEN_ONLY = ("hyb_w_in", "conv_dw_w", "conv_dw_b", "conv_ln_g", "conv_ln_b", "ssm_conv_w", "ssm_conv_b",
             "ssm_dt_bias", "ssm_a_log", "ssm_d", "ssm_norm", "hyb_w_out")
ODD_ONLY = ("att_w_qkv", "att_b_qkv", "att_sinks", "att_w_o", "att_b_o")


def _layer_index(k, i):
    if k in PER_LAYER:
        return i
    if k in (EVEN_ONLY if i % 2 == 0 else ODD_ONLY):
        return i // 2
    return None


def _stage(k):
    return 0 if k.startswith("ffn1") else (2 if k.startswith(("ffn2", "ple")) else 1)


def trunk_fwd_bwd(x, pe, target, layers, final_norm, sink, stage_done):
    depth = len(layers)
    tables = rope_tables(x.shape[0])
    h = x
    saved = []
    for i, w in enumerate(layers):
        h, s1 = ffn_fwd(h, w["norm_ffn1"], w, ("ffn1_w_in", "ffn1_w_out"), f"l{i}_ffn1")
        if i % 2 == 0:
            h, s2 = hyb_fwd(h, w, f"l{i}_hyb")
        else:
            h, s2 = att_fwd(h, w, tables, f"l{i}_att")
        h, s3 = ffn_fwd(h, w["norm_ffn2"], w, ("ffn2_w_in", "ffn2_w_out"), f"l{i}_ffn2")
        h, s4 = ple_block_fwd(h, pe[i], w, f"l{i}_ple")
        saved.append((s1, s2, s3, s4))
    dh, dgf, loss = loss_head(h, _row(final_norm), target, "loss_head")
    grads = {}
    tie = dgf
    for i in reversed(range(depth)):
        w = layers[i]
        s1, s2, s3, s4 = saved[i]
        dh, g = ple_block_bwd(dh, w, s4, f"l{i}_ple", sink, i, tie)
        odd = i % 2 == 1
        out = ffn_bwd(dh, w["norm_ffn2"], w["ffn2_w_in"], w["ffn2_w_out"], s3, f"l{i}_ffn2", sink,
                      ("ffn2_w_in", "ffn2_w_out"), i, tie, colsum=odd)
        dh = out[0]
        g.update(norm_ffn2=out[1])
        if odd:
            dh, gm = att_bwd(dh, out[2], w, s2, tables, f"l{i}_att", sink, i // 2)
        else:
            dh, gm = hyb_bwd(dh, w, s2, f"l{i}_hyb", sink, i // 2)
        g.update(gm)
        tie = stage_done(i, 1, tie)
        out = ffn_bwd(dh, w["norm_ffn1"], w["ffn1_w_in"], w["ffn1_w_out"], s1, f"l{i}_ffn1", sink,
                      ("ffn1_w_in", "ffn1_w_out"), i, tie)
        dh = out[0]
        g.update(norm_ffn1=out[1])
        tie = stage_done(i, 0, tie)
        for k, v in g.items():
            grads.setdefault(k, []).insert(0, v)
    grads = {k: jnp.stack(v) for k, v in grads.items()}
    grads["final_norm"] = dgf.reshape(-1)
    return loss, dh, grads


def _me():
    return lax.axis_index("x"), lax.axis_index("y"), lax.axis_index("c")


def _flip(v, f):
    return 1 - v if f else v


def _remote(src, dst, send_sems, recv_sems, k, dev):
    return pltpu.make_async_remote_copy(src_ref=src, dst_ref=dst, send_sem=send_sems.at[k], recv_sem=recv_sems.at[k],
                                        device_id=dev, device_id_type=MESH)


CHIP_FLIPS = ((1, 0), (0, 1), (1, 1))
DEV_FLIPS = tuple((fx, fy, fc) for fx in (0, 1) for fy in (0, 1) for fc in (0, 1))[1:]


HBM = pl.BlockSpec(memory_space=pltpu.HBM)
SEM = pl.BlockSpec(memory_space=pltpu.SEMAPHORE)
DATAFLOW = pltpu.SideEffectType.DATAFLOW_SIDE_EFFECTING


def _core_half(ref, axis, c):
    if axis is None:
        return ref
    h = ref.shape[axis] // 2
    return ref.at[pl.ds(c * h, h), :] if axis == 0 else ref.at[:, pl.ds(c * h, h)]


def gather_start(xs, lands, halves, after, name):
    na = len(xs)

    def body(*refs):
        x_refs, land_refs = refs[:na], refs[na:2 * na]
        send_sems, recv_sems = refs[2 * na + 1], refs[2 * na + 2]
        token = refs[-1]
        mx, my, mc = _me()
        chip = 2 * mx + my
        for a in range(na):
            for j, (fx, fy) in enumerate(CHIP_FLIPS):
                _remote(_core_half(x_refs[a], halves[a], mc), _core_half(land_refs[a].at[chip], halves[a], mc),
                        send_sems, recv_sems, 3 * a + j, (_flip(mx, fx), _flip(my, fy), mc)).start()
        token[...] = jnp.zeros_like(token)

    outs = pl.pallas_call(
        body, name=name,
        out_shape=(pltpu.SemaphoreType.DMA((3 * na,)), pltpu.SemaphoreType.DMA((3 * na,)))
        + tuple(pltpu.HBM(x.shape, x.dtype) for x in xs) + tuple(pltpu.HBM(l.shape, l.dtype) for l in lands)
        + (S((8, LANE), F32),),
        in_specs=[HBM] * (2 * na) + [pl.BlockSpec(memory_space=pl.ANY)],
        out_specs=(SEM, SEM) + (HBM,) * (2 * na) + (pl.BlockSpec(memory_space=pltpu.VMEM),),
        input_output_aliases={a: 2 + a for a in range(2 * na)},
        compiler_params=pltpu.CompilerParams(has_side_effects=DATAFLOW),
    )(*[pltpu.with_memory_space_constraint(t, pltpu.HBM) for t in list(xs) + list(lands)], after)
    return outs[0], outs[1], list(outs[2:2 + na]), list(outs[2 + na:2 + 2 * na])


def gather_wait(send_sems, recv_sems, xs, lands, halves, first, after, name):
    na = len(xs)

    def body(*refs):
        x_refs, land_refs = refs[:na], refs[na:2 * na]
        send_sems, recv_sems = refs[2 * na], refs[2 * na + 1]
        mx, my, mc = _me()
        for a in range(na):
            for j, (fx, fy) in enumerate(CHIP_FLIPS):
                px, py = _flip(mx, fx), _flip(my, fy)
                cp = _remote(_core_half(x_refs[a], halves[a], mc), _core_half(land_refs[a].at[2 * px + py], halves[a], mc),
                             send_sems, recv_sems, 3 * (first + a) + j, (px, py, mc))
                cp.wait_send()
                cp.wait_recv()

    outs = pl.pallas_call(
        body, name=name,
        out_shape=tuple(pltpu.HBM(x.shape, x.dtype) for x in xs) + tuple(pltpu.HBM(l.shape, l.dtype) for l in lands),
        in_specs=[HBM] * (2 * na) + [SEM, SEM, pl.BlockSpec(memory_space=pl.ANY)], out_specs=(HBM,) * (2 * na),
        input_output_aliases={a: a for a in range(2 * na)},
        compiler_params=pltpu.CompilerParams(has_side_effects=DATAFLOW),
    )(*xs, *lands, send_sems, recv_sems, after)
    return list(outs[na:])


def forward_halves(land, axis, name):
    def body(in_ref, out_ref, send_sems, recv_sems):
        del in_ref
        mx, my, mc = _me()
        sib = (mx, my, 1 - mc)
        slots = [2 * _flip(mx, fx) + _flip(my, fy) for fx, fy in CHIP_FLIPS]
        cps = [_remote(_core_half(out_ref.at[s], axis, mc), _core_half(out_ref.at[s], axis, mc), send_sems, recv_sems, j, sib)
               for j, s in enumerate(slots)]
        for cp in cps:
            cp.start()
        for j, s in enumerate(slots):
            _remote(_core_half(out_ref.at[s], axis, mc), _core_half(out_ref.at[s], axis, 1 - mc), send_sems, recv_sems, j, sib).wait_recv()
        for cp in cps:
            cp.wait_send()

    return pl.pallas_call(
        body, name=name, out_shape=S(land.shape, land.dtype), in_specs=[ANY], out_specs=ANY, input_output_aliases={0: 0},
        scratch_shapes=[pltpu.SemaphoreType.DMA((3,)), pltpu.SemaphoreType.DMA((3,))])(land)


def all_gather_devices(v, name):
    r, l = v.shape

    def body(v_ref, out_ref, send_sems, recv_sems):
        mx, my, mc = _me()
        me = 4 * mx + 2 * my + mc
        peers = [(_flip(mx, fx), _flip(my, fy), _flip(mc, fc)) for fx, fy, fc in DEV_FLIPS]
        sends = [_remote(v_ref, out_ref.at[me], send_sems, recv_sems, j, p) for j, p in enumerate(peers)]
        for cp in sends:
            cp.start()
        for j, (px, py, pc) in enumerate(peers):
            _remote(v_ref, out_ref.at[4 * px + 2 * py + pc], send_sems, recv_sems, j, (px, py, pc)).wait_recv()
        for cp in sends:
            cp.wait_send()

    out = pl.pallas_call(
        body, name=name, out_shape=S((N_DEV, r, l), v.dtype), in_specs=[ANY], out_specs=ANY,
        scratch_shapes=[pltpu.SemaphoreType.DMA((7,)), pltpu.SemaphoreType.DMA((7,))])(v)
    me = 4 * lax.axis_index("x") + 2 * lax.axis_index("y") + lax.axis_index("c")
    return lax.dynamic_update_slice_in_dim(out, v[None], me, axis=0)


def gather_devices_start(v, name):
    me = 4 * lax.axis_index("x") + 2 * lax.axis_index("y") + lax.axis_index("c")
    land = lax.dynamic_update_slice_in_dim(lax.empty((N_DEV,) + v.shape, v.dtype), v[None], me, axis=0)

    def body(v_ref, land_ref, send_sems, recv_sems, v_thru, land_thru, token):
        mx, my, mc = _me()
        for j, (fx, fy, fc) in enumerate(DEV_FLIPS):
            _remote(v_ref, land_ref.at[4 * mx + 2 * my + mc], send_sems, recv_sems, j,
                    (_flip(mx, fx), _flip(my, fy), _flip(mc, fc))).start()
        token[...] = jnp.zeros_like(token)

    outs = pl.pallas_call(
        body, name=name,
        out_shape=(pltpu.SemaphoreType.DMA((7,)), pltpu.SemaphoreType.DMA((7,)), pltpu.HBM(v.shape, v.dtype),
                   pltpu.HBM(land.shape, land.dtype), S((8, LANE), F32)),
        in_specs=[HBM, HBM], out_specs=(SEM, SEM, HBM, HBM, pl.BlockSpec(memory_space=pltpu.VMEM)),
        input_output_aliases={0: 2, 1: 3}, compiler_params=pltpu.CompilerParams(has_side_effects=DATAFLOW),
    )(pltpu.with_memory_space_constraint(v, pltpu.HBM), pltpu.with_memory_space_constraint(land, pltpu.HBM))
    return outs[:4]


def gather_devices_wait(send_sems, recv_sems, v, land, after, name):
    def body(v_ref, land_ref, send_sems, recv_sems, after_ref, v_dead, got_ref):
        mx, my, mc = _me()
        for j, (fx, fy, fc) in enumerate(DEV_FLIPS):
            px, py, pc = _flip(mx, fx), _flip(my, fy), _flip(mc, fc)
            cp = _remote(v_ref, land_ref.at[4 * px + 2 * py + pc], send_sems, recv_sems, j, (px, py, pc))
            cp.wait_send()
            cp.wait_recv()

    return pl.pallas_call(
        body, name=name, out_shape=(pltpu.HBM(v.shape, v.dtype), pltpu.HBM(land.shape, land.dtype)),
        in_specs=[HBM, HBM, SEM, SEM, pl.BlockSpec(memory_space=pl.ANY)], out_specs=(HBM, HBM),
        input_output_aliases={0: 0, 1: 1}, compiler_params=pltpu.CompilerParams(has_side_effects=DATAFLOW),
    )(v, land, send_sems, recv_sems, after)[1]


def sum_devices(g8, name):
    nd, r, l = g8.shape
    tile = r
    for t in (512, 256, 128, 64, 32, 16, 8):
        if r % t == 0:
            tile = t
            break

    def body(g_ref, o_ref):
        acc = g_ref[0]
        for d in range(1, nd):
            acc = acc + g_ref[d]
        o_ref[...] = acc

    return _call(body, name, (r // tile,), [pl.BlockSpec((nd, tile, l), lambda i: (0, i, 0))], _rs(tile, l), S((r, l), F32),
                 sem=("parallel",))(g8)


def exchange_halves(gs, name):
    na = len(gs)
    nch = gs[0].shape[0]

    def body(*refs):
        g_refs, out_refs = refs[:na], refs[na:2 * na]
        send_sems, recv_sems = refs[2 * na:]
        mx, my, mc = _me()
        sib = (mx, my, 1 - mc)
        cps = []
        for a in range(na):
            half = gs[a].shape[1] // 2
            for j in range(nch):
                cps.append(_remote(g_refs[a].at[j, pl.ds((1 - mc) * half, half), :], out_refs[a].at[j],
                                   send_sems, recv_sems, nch * a + j, sib))
        for cp in cps:
            cp.start()
        for cp in cps:
            cp.wait_recv()
        for cp in cps:
            cp.wait_send()

    return pl.pallas_call(
        body, name=name, out_shape=[S((nch, g.shape[1] // 2, g.shape[2]), g.dtype) for g in gs],
        in_specs=[ANY] * na, out_specs=[ANY] * na,
        scratch_shapes=[pltpu.SemaphoreType.DMA((nch * na,)), pltpu.SemaphoreType.DMA((nch * na,))])(*gs)


def add_halves(g4, got, name):
    nch, r, l = g4.shape
    half = r // 2
    tile = _pick_rows(half)
    nt = half // tile

    def body(g_ref, r_ref, a_ref, own_ref):
        j = pl.program_id(1)
        chip = 2 * lax.axis_index("x") + lax.axis_index("y")
        val = g_ref[0].astype(F32) + r_ref[0].astype(F32)
        a_ref[0] = val.astype(a_ref.dtype)

        @pl.when(j == chip)
        def _():
            own_ref[...] = val

    return pl.pallas_call(
        body, name=name, grid=(nt, nch),
        in_specs=[pl.BlockSpec((1, tile, l), lambda i, j: (j, lax.axis_index("c") * nt + i, 0)),
                  pl.BlockSpec((1, tile, l), lambda i, j: (j, i, 0))],
        out_specs=[pl.BlockSpec((1, tile, l), lambda i, j: (j, i, 0)), pl.BlockSpec((tile, l), lambda i, j: (i, 0))],
        out_shape=[S((nch, half, l), MXU_DTYPE), S((half, l), F32)],
        compiler_params=pltpu.CompilerParams(dimension_semantics=("parallel", "arbitrary"), vmem_limit_bytes=VMEM_LIMIT))(g4, got)


def _pick_rows(r, cap=640):
    return next((t for t in range(cap - cap % 16, 15, -16) if r % t == 0), r)


def add_chips(own, got, name):
    h, l = own.shape
    tile = _pick_rows(h)

    def body(o_ref, g_ref, out_ref):
        out_ref[...] = ((o_ref[...] + g_ref[0].astype(F32)) + g_ref[1].astype(F32)) + g_ref[2].astype(F32)

    nt = h // tile
    return _call(body, name, (nt,), [_rs(tile, l), pl.BlockSpec((3, tile, l), lambda i: (0, i, 0))],
                 pl.BlockSpec((tile, l), lambda i: (lax.axis_index("c") * nt + i, 0)),
                 S((2 * h, l), F32), sem=("parallel",))(own, got)


def join_halves(bufs, name):
    na = len(bufs)

    def body(*refs):
        out_refs = refs[na:2 * na]
        send_sems, recv_sems = refs[2 * na:]
        mx, my, mc = _me()
        sib = (mx, my, 1 - mc)

        def half(a, hc):
            h = bufs[a].shape[0] // 2
            return out_refs[a].at[pl.ds(hc * h, h), :]

        cps = [_remote(half(a, mc), half(a, mc), send_sems, recv_sems, a, sib) for a in range(na)]
        for cp in cps:
            cp.start()
        for a in range(na):
            _remote(half(a, mc), half(a, 1 - mc), send_sems, recv_sems, a, sib).wait_recv()
        for cp in cps:
            cp.wait_send()

    return pl.pallas_call(
        body, name=name, out_shape=[S(b.shape, b.dtype) for b in bufs], in_specs=[ANY] * na, out_specs=[ANY] * na,
        input_output_aliases={a: a for a in range(na)},
        scratch_shapes=[pltpu.SemaphoreType.DMA((na,)), pltpu.SemaphoreType.DMA((na,))])(*bufs)


def exchange_chips_start(parts, name):
    na = len(parts)
    lands = [lax.empty((3,) + p.shape[1:], p.dtype) for p in parts]

    def body(*refs):
        a_refs, land_refs = refs[:na], refs[na:2 * na]
        send_sems, recv_sems = refs[2 * na], refs[2 * na + 1]
        mx, my, mc = _me()
        for j, (fx, fy) in enumerate(CHIP_FLIPS):
            px, py = _flip(mx, fx), _flip(my, fy)
            for a in range(na):
                _remote(a_refs[a].at[2 * px + py], land_refs[a].at[j], send_sems, recv_sems, 3 * a + j, (px, py, mc)).start()
        refs[-1][...] = jnp.zeros_like(refs[-1])

    outs = pl.pallas_call(
        body, name=name,
        out_shape=(pltpu.SemaphoreType.DMA((3 * na,)), pltpu.SemaphoreType.DMA((3 * na,)))
        + tuple(pltpu.HBM(t.shape, t.dtype) for t in list(parts) + lands) + (S((8, LANE), F32),),
        in_specs=[HBM] * (2 * na), out_specs=(SEM, SEM) + (HBM,) * (2 * na) + (pl.BlockSpec(memory_space=pltpu.VMEM),),
        input_output_aliases={a: 2 + a for a in range(2 * na)},
        compiler_params=pltpu.CompilerParams(has_side_effects=DATAFLOW),
    )(*[pltpu.with_memory_space_constraint(t, pltpu.HBM) for t in list(parts) + lands])
    return outs[0], outs[1], list(outs[2:2 + na]), list(outs[2 + na:2 + 2 * na]), outs[-1]


def exchange_chips_wait(send_sems, recv_sems, parts, lands, after, name):
    na = len(parts)

    def body(*refs):
        a_refs, land_refs = refs[:na], refs[na:2 * na]
        send_sems, recv_sems = refs[2 * na], refs[2 * na + 1]
        mx, my, mc = _me()
        for j, (fx, fy) in enumerate(CHIP_FLIPS):
            px, py = _flip(mx, fx), _flip(my, fy)
            for a in range(na):
                cp = _remote(a_refs[a].at[2 * px + py], land_refs[a].at[j], send_sems, recv_sems, 3 * a + j, (px, py, mc))
                cp.wait_send()
                cp.wait_recv()

    outs = pl.pallas_call(
        body, name=name, out_shape=tuple(pltpu.HBM(t.shape, t.dtype) for t in list(parts) + list(lands)),
        in_specs=[HBM] * (2 * na) + [SEM, SEM, pl.BlockSpec(memory_space=pl.ANY)], out_specs=(HBM,) * (2 * na),
        input_output_aliases={a: a for a in range(2 * na)},
        compiler_params=pltpu.CompilerParams(has_side_effects=DATAFLOW),
    )(*parts, *lands, send_sems, recv_sems, after)
    return list(outs[na:])


def reduce_begin(gs, tag):
    got = exchange_halves(gs, f"{tag}_d2d")
    sums = [add_halves(g, r, f"{tag}_add1_{i}") for i, (g, r) in enumerate(zip(gs, got))]
    return [own for _, own in sums], exchange_chips_start([a for a, _ in sums], f"{tag}_ici_start")


def reduce_end(state, after, tag):
    owns, (send_sems, recv_sems, parts, lands, _) = state
    got = exchange_chips_wait(send_sems, recv_sems, parts, lands, after, f"{tag}_ici_wait")
    return [add_chips(own, r, f"{tag}_add2_{i}") for i, (own, r) in enumerate(zip(owns, got))]


PACK_L = 1024


def _pack(arrs, dtype, row_mult, lead=None):
    lead_shape = () if lead is None else arrs[0].shape[:lead]
    flat = jnp.concatenate([a.astype(dtype).reshape(lead_shape + (-1,)) for a in arrs], axis=-1)
    n = flat.shape[-1]
    unit = row_mult * PACK_L
    total = -(-n // unit) * unit
    flat = jnp.pad(flat, [(0, 0)] * len(lead_shape) + [(0, total - n)])
    return flat.reshape(lead_shape + (total // PACK_L, PACK_L))


def _unpack(packed, shapes, lead=None):
    lead_shape = () if lead is None else packed.shape[:lead]
    flat = packed.reshape(lead_shape + (-1,))
    out, off = [], 0
    for shp in shapes:
        n = int(np.prod(shp))
        out.append(flat[..., off:off + n].reshape(lead_shape + tuple(shp)))
        off += n
    return out


def _to_full(gathered, axis):
    t = jnp.moveaxis(gathered, 0, axis)
    shp = t.shape
    return t.reshape(shp[:axis] + (shp[axis] * shp[axis + 1],) + shp[axis + 2:])


WEIGHTS = ("norm_ffn1", "ffn1_w_in", "ffn1_w_out", "norm_mix", "norm_ffn2", "ffn2_w_in", "ffn2_w_out", "ple_norm",
           "ple_gate_w", "ple_proj_w", "hyb_w_in", "conv_dw_w", "conv_dw_b", "conv_ln_g", "conv_ln_b", "ssm_conv_w",
           "ssm_conv_b", "ssm_dt_bias", "ssm_a_log", "ssm_d", "ssm_norm", "hyb_w_out", "att_w_qkv", "att_b_qkv",
           "att_sinks", "att_w_o", "att_b_o", "final_norm")
SHARD_AXIS = dict(ffn1_w_in=2, ffn1_w_out=1, ffn2_w_in=2, ffn2_w_out=1, ple_gate_w=1, ple_proj_w=2, hyb_w_in=2,
                  conv_dw_w=2, ssm_conv_w=2, hyb_w_out=1, att_w_qkv=2, att_b_qkv=1, att_w_o=1, att_b_o=1)
BIG = ("ffn1_w_in", "ffn1_w_out", "ffn2_w_in", "ffn2_w_out", "ple_gate_w", "ple_proj_w", "hyb_w_in", "hyb_w_out",
       "att_w_qkv", "att_w_o")
TRANSPOSED = ("hyb_w_in",)
FORWARDED = (0, 2)


def _shard_axis(k):
    return 1 if k in TRANSPOSED else SHARD_AXIS[k]
SMALL_SHARDED = ("conv_dw_w", "ssm_conv_w", "att_b_qkv", "att_b_o")
SMALL = tuple(k for k in WEIGHTS if k not in BIG)


def _step(x, p, target, w, m, v):
    mx, my = lax.axis_index("x"), lax.axis_index("y")
    chip = 2 * mx + my
    w, m, v = ({k: (a.transpose(0, 2, 1) if k in TRANSPOSED else a) for k, a in d.items()} for d in (w, m, v))

    depth = w["norm_ffn1"].shape[0]
    order = sorted([(k, i) for i in range(depth) for k in BIG if _layer_index(k, i) is not None],
                   key=lambda t: (t[1], _stage(t[0])))
    small_g = all_gather_devices(_pack([w[k] for k in SMALL_SHARDED], F32, 8), "gather_small")
    shards = [w[k][_layer_index(k, i)].astype(MXU_DTYPE) for k, i in order]
    lands = [lax.dynamic_update_slice_in_dim(lax.empty((N_CHIPS,) + s.shape, s.dtype), s[None], chip, axis=0) for s in shards]
    halves = [(0 if s.shape[0] % 32 == 0 else 1) if pos in FORWARDED else None for pos, s in enumerate(shards)]
    send_sems, recv_sems, shards, lands = gather_start(shards, lands, halves, small_g, "gather_start")

    def fetch(i, k, after):
        p = order.index((k, i))
        g, = gather_wait(send_sems, recv_sems, [shards[p]], [lands[p]], [halves[p]], p, after, f"gather_wait_l{i}_{k}")
        if halves[p] is not None:
            g = forward_halves(g, halves[p], f"gather_forward_l{i}_{k}")
        if _shard_axis(k) == 2:
            return ColSharded(g)
        return g.reshape(-1, g.shape[-1])

    small_g = small_g[0::2]
    small_full = {k: _to_full(g, SHARD_AXIS[k])
                  for k, g in zip(SMALL_SHARDED, _unpack(small_g, [w[k].shape for k in SMALL_SHARDED], lead=1))}
    layers = [LayerWeights({k: small_full.get(k, w[k])[_layer_index(k, i)] for k in SMALL if _layer_index(k, i) is not None},
                           functools.partial(fetch, i)) for i in range(depth)]

    def bucket_of(layer, stage):
        return (layer, 0) if layer > 0 else (0, min(stage, 1))

    sink = GradSink({k: w[k].shape for k in BIG}, bucket_of)
    begun = {}

    def stage_done(i, stage, tie):
        b = bucket_of(i, stage)
        if stage > 0 and bucket_of(i, stage - 1) == b:
            return tie
        begun[b] = reduce_begin(list(sink.bufs[b].values()), f"grads_l{b[0]}_{b[1]}")
        return begun[b][1][-1]

    loss, dx, grads = trunk_fwd_bwd(x[0], p[:, 0], target[0], layers, w["final_norm"], sink, stage_done)

    results = {}

    def finish(buckets, after, tag):
        halves = {b: reduce_end(begun[b], after, f"grads_l{b[0]}_{b[1]}") for b in buckets}
        joined = iter(join_halves([h for b in buckets for h in halves[b]], f"grads_join_{tag}"))
        reduced = {b: {c: next(joined) for c in sink.bufs[b]} for b in buckets}
        last = after
        for (k, li), (b, key, _, off, r) in sink.where.items():
            if b in buckets:
                g = reduced[b][key]
                if r % 8:
                    g, off = g[off:off + r], 0
                results[k] = adamw_layer(w[k], g, off, m[k], v[k], li, results.get(k), f"adamw_{k}_{li}")
                last = results[k][1]
        return last

    vec = gather_devices_start(_pack([loss[0:1, 0:1]] + [grads[k] for k in SMALL], F32, 8), "gather_vectors_start")
    order_b = list(begun)
    started_last = begun[order_b[-1]][1][-1]
    done = finish(order_b[-1:], finish(order_b[:-1], started_last, "early") if len(order_b) > 1 else dx, "last")
    g_out = {k: results[k][0] for k in BIG}
    vec = sum_devices(gather_devices_wait(*vec, done, "gather_vectors_wait"), "sum_vectors")
    parts = _unpack(vec, [(1, 1)] + [grads[k].shape for k in SMALL])
    loss_out = parts[0].reshape(())
    for k, g in zip(SMALL, parts[1:]):
        if k in SHARD_AXIS:
            ax = SHARD_AXIS[k]
            g = lax.dynamic_slice_in_dim(g, chip * w[k].shape[ax], w[k].shape[ax], axis=ax)
        g_out[k] = g

    for k in TRANSPOSED:
        results[k] = [a.transpose(0, 2, 1) for a in results[k]]
    g_out.update({k: results[k][0] for k in TRANSPOSED})
    delta, new_m, new_v = ({k: results[k][j] for k in BIG} for j in (1, 2, 3))
    shapes = [w[k].shape for k in SMALL]
    packed = [_pack([src[k] for k in SMALL], F32, 8) for src in (w, g_out, m, v)]
    outs = adamw(*packed, "adamw_small")
    for dst, o in zip((delta, new_m, new_v), outs):
        for k, a in zip(SMALL, _unpack(o, shapes)):
            dst[k] = a
    return ((loss_out, dx[None]) + tuple(g_out[k] for k in WEIGHTS) + tuple(delta[k] for k in WEIGHTS)
            + tuple(new_m[k] for k in WEIGHTS) + tuple(new_v[k] for k in WEIGHTS))


def kernel(x, p, norm_ffn1, ffn1_w_in, ffn1_w_out, norm_mix, norm_ffn2, ffn2_w_in, ffn2_w_out, ple_norm, ple_gate_w, ple_proj_w, hyb_w_in, conv_dw_w, conv_dw_b, conv_ln_g, conv_ln_b, ssm_conv_w, ssm_conv_b, ssm_dt_bias, ssm_a_log, ssm_d, ssm_norm, hyb_w_out, att_w_qkv, att_b_qkv, att_sinks, att_w_o, att_b_o, final_norm, loss_target, m_norm_ffn1, m_ffn1_w_in, m_ffn1_w_out, m_norm_mix, m_norm_ffn2, m_ffn2_w_in, m_ffn2_w_out, m_ple_norm, m_ple_gate_w, m_ple_proj_w, m_hyb_w_in, m_conv_dw_w, m_conv_dw_b, m_conv_ln_g, m_conv_ln_b, m_ssm_conv_w, m_ssm_conv_b, m_ssm_dt_bias, m_ssm_a_log, m_ssm_d, m_ssm_norm, m_hyb_w_out, m_att_w_qkv, m_att_b_qkv, m_att_sinks, m_att_w_o, m_att_b_o, m_final_norm, v_norm_ffn1, v_ffn1_w_in, v_ffn1_w_out, v_norm_mix, v_norm_ffn2, v_ffn2_w_in, v_ffn2_w_out, v_ple_norm, v_ple_gate_w, v_ple_proj_w, v_hyb_w_in, v_conv_dw_w, v_conv_dw_b, v_conv_ln_g, v_conv_ln_b, v_ssm_conv_w, v_ssm_conv_b, v_ssm_dt_bias, v_ssm_a_log, v_ssm_d, v_ssm_norm, v_hyb_w_out, v_att_w_qkv, v_att_b_qkv, v_att_sinks, v_att_w_o, v_att_b_o, v_final_norm):
    given = locals()
    w = {k: given[k] for k in WEIGHTS}
    m = {k: given["m_" + k] for k in WEIGHTS}
    v = {k: given["v_" + k] for k in WEIGHTS}
    return _step(x, p, loss_target, w, m, v)
```

```python
import functools

import numpy as np
import jax
import jax.numpy as jnp
from jax import lax
from jax.experimental import pallas as pl
from jax.experimental.pallas import tpu as pltpu

F32 = jnp.float32
BF16 = jnp.bfloat16
MXU_DTYPE = jnp.bfloat16
S = jax.ShapeDtypeStruct
MESH = pl.DeviceIdType.MESH

VMEM_LIMIT = 48 * 2**20
LANE = 128

EPS = 1e-6
SSM_HEADS = 16
HEAD_DIM = 64
SSM_GROUPS = 2
SSM_STATE = 128
SSM_CONV = 4
CHUNK = 128
CONV_WIDTH = 31
ATT_HEADS = 16
ATT_KV_HEADS = 4
WINDOW = 128
ROPE_THETA = 10000.0
ADAM_LR = 0.001
ADAM_B1 = 0.9
ADAM_B2 = 0.999
ADAM_EPS = 1e-08
ADAM_WD = 0.01
ADAM_STEP = 10

N_CHIPS = 4
N_DEV = 8

NN = ((1,), (0,))
NT = ((1,), (1,))
TN = ((0,), (0,))


def _mm(a, b, dims=NN):
    return lax.dot_general(a.astype(MXU_DTYPE), b.astype(MXU_DTYPE), (dims, ((), ())), preferred_element_type=F32)


def _split3(a):
    hi = a.astype(BF16)
    r = a - hi.astype(F32)
    mid = r.astype(BF16)
    lo = (r - mid.astype(F32)).astype(BF16)
    return hi, mid, lo


def _mm01(a, onehot, dims=NN):
    o = onehot.astype(BF16)
    out = None
    for part in _split3(a):
        t = lax.dot_general(part, o, (dims, ((), ())), preferred_element_type=F32)
        out = t if out is None else out + t
    return out


def _01mm(onehot, a):
    o = onehot.astype(BF16)
    out = None
    for part in _split3(a):
        t = lax.dot_general(o, part, (NN, ((), ())), preferred_element_type=F32)
        out = t if out is None else out + t
    return out


def _sigmoid(x):
    return 0.5 * jnp.tanh(0.5 * x) + 0.5


def _softplus(x):
    return jnp.maximum(x, 0.0) + jnp.log(1.0 + jnp.exp(-jnp.abs(x)))


def _iota(shape, axis):
    return lax.broadcasted_iota(jnp.int32, shape, axis)


def _head_indicator(width, heads, transposed=False):
    per = width // heads
    if transposed:
        return (_iota((heads, width), 1) // per == _iota((heads, width), 0)).astype(F32)
    return (_iota((width, heads), 0) // per == _iota((width, heads), 1)).astype(F32)


def _acc(ref, i, val):
    @pl.when(i == 0)
    def _():
        ref[...] = val

    @pl.when(i > 0)
    def _():
        ref[...] += val


def _rs(tile, width, col=0, shift=0, n=None):
    if shift == 0:
        return pl.BlockSpec((tile, width), lambda i: (i, col))
    if shift < 0:
        return pl.BlockSpec((tile, width), lambda i: (jnp.maximum(i - 1, 0), col))
    return pl.BlockSpec((tile, width), lambda i: (jnp.minimum(i + 1, n - 1), col))


def _ps(shape):
    return pl.BlockSpec(shape, lambda i: (0,) * len(shape))


def _call(body, name, grid, in_specs, out_specs, out_shape, scratch=(), sem=None):
    return pl.pallas_call(
        body, name=name, grid=grid, in_specs=in_specs, out_specs=out_specs, out_shape=out_shape,
        scratch_shapes=list(scratch),
        compiler_params=pltpu.CompilerParams(dimension_semantics=sem, vmem_limit_bytes=VMEM_LIMIT))


def _row_tile(n, target):
    t = min(n, target)
    assert n % t == 0, (n, t)
    return t


def _pick_tile(dim, target):
    if dim <= target:
        return dim
    t = (int(1.4 * target) // LANE) * LANE
    while t >= LANE:
        if dim % t == 0:
            return t
        t -= LANE
    return dim


ANY = pl.BlockSpec(memory_space=pl.ANY)


def _paired(j):
    return (j % 2) * 2 + j // 2


class ColSharded:
    def __init__(self, arr, paired=False):
        self.arr, self.paired = arr, paired
        self.nch, self.rows, self.per = arr.shape
        self.shape = (self.rows, self.nch * self.per)

    def chip(self, j):
        return _paired(j) if self.paired else j


class Slot:
    def __init__(self, buf, kind, per, off, c0=0, paired=False):
        self.buf, self.kind, self.per, self.off, self.c0, self.paired = buf, kind, per, off, c0, paired

    def chip(self, j):
        return _paired(j) if self.paired else j


def matmul(a, b, mode, name, *, out_dtype=F32, scale=None, res=None, bias=None, into=None, tm=1024, tn=1024, tk=1024):
    bshape = b.shape
    if mode == "nn":
        (m, k), (k2, n) = a.shape, bshape
    elif mode == "nt":
        (m, k), (n, k2) = a.shape, bshape
    else:
        (k, m), (k2, n) = a.shape, bshape
    assert k == k2, (a.shape, bshape, mode)
    if mode == "tn":
        tk = 2 * tk
    elif k <= 3 * tk:
        tk = k
    tm, tn, tk = _pick_tile(m, tm), _pick_tile(n, tn), _pick_tile(k, tk)
    if isinstance(b, ColSharded):
        if mode == "nn":
            tn = b.per
        else:
            assert mode == "nt"
            tk = b.per
    if into is not None:
        if into.kind == "c":
            tn = into.per
            assert into.off % tm == 0 and n == N_CHIPS * into.per
        else:
            tm = max(1, min(m, int(1.4 * 1024)) // into.per) * into.per
            assert m % tm == 0 and into.off % into.per == 0 and into.c0 % (tm // into.per) == 0
    nk = k // tk
    dims = {"nn": NN, "nt": NT, "tn": TN}[mode]
    a_spec = (pl.BlockSpec((tk, tm), lambda i, j, kk: (kk, i)) if mode == "tn"
              else pl.BlockSpec((tm, tk), lambda i, j, kk: (i, kk)))
    if isinstance(b, ColSharded):
        bchip = b.chip
        b_spec = (pl.BlockSpec((None, tk, tn), lambda i, j, kk: (bchip(j), kk, 0)) if mode == "nn"
                  else pl.BlockSpec((None, tn, tk), lambda i, j, kk: (bchip(kk), j, 0)))
        b = b.arr
    else:
        b_spec = (pl.BlockSpec((tn, tk), lambda i, j, kk: (j, kk)) if mode == "nt"
                  else pl.BlockSpec((tk, tn), lambda i, j, kk: (kk, j)))
    plain_o = pl.BlockSpec((tm, tn), lambda i, j, kk: (i, j))
    ins, in_specs = [a, b], [a_spec, b_spec]
    if bias is not None:
        ins.append(bias)
        in_specs.append(pl.BlockSpec((1, tn), lambda i, j, kk: (0, j)))
    if res is not None:
        ins.append(res)
        in_specs.append(plain_o)
    aliases = {}
    if into is None:
        o_spec, o_shape = plain_o, S((m, n), out_dtype)
    else:
        aliases = {len(ins): 0}
        ins.append(into.buf)
        in_specs.append(ANY)
        o_shape = S(into.buf.shape, into.buf.dtype)
        if into.kind == "c":
            ob, ochip = into.off // tm, into.chip
            o_spec = pl.BlockSpec((None, tm, tn), lambda i, j, kk: (ochip(j), ob + i, 0))
        else:
            q, ob = tm // into.per, into.off // into.per
            cb = into.c0 // q
            o_spec = pl.BlockSpec((q, into.per, tn), lambda i, j, kk: (cb + i, ob, j))

    def body(*refs):
        a_ref, b_ref = refs[0], refs[1]
        o_ref, acc_ref = refs[-2], refs[-1]
        kk = pl.program_id(2)

        def finish(out):
            if scale is not None:
                out = out * scale
            pos = 2
            if bias is not None:
                out = out + refs[pos][...]
                pos += 1
            if res is not None:
                out = out + refs[pos][...]
            o_ref[...] = out.astype(o_ref.dtype).reshape(o_ref.shape)

        if nk == 1:
            finish(_mm(a_ref[...], b_ref[...], dims))
            return

        @pl.when(kk == 0)
        def _():
            acc_ref[...] = jnp.zeros_like(acc_ref)

        acc_ref[...] += _mm(a_ref[...], b_ref[...], dims)

        @pl.when(kk == nk - 1)
        def _():
            finish(acc_ref[...])

    return pl.pallas_call(
        body, name=name, grid=(m // tm, n // tn, nk), in_specs=in_specs, out_specs=o_spec, out_shape=o_shape,
        scratch_shapes=[pltpu.VMEM((tm, tn), F32)], input_output_aliases=aliases,
        compiler_params=pltpu.CompilerParams(dimension_semantics=("parallel", "parallel", "arbitrary"),
                                             vmem_limit_bytes=VMEM_LIMIT))(*ins)


def rms_matmul(h, g, b, mode, name, bias=None):
    n, d = h.shape
    sharded = isinstance(b, ColSharded)
    n_out = b.shape[1] if mode == "nn" else b.shape[0]
    tm = _row_tile(n, 1024)
    tn = b.per if sharded else _pick_tile(n_out, 1024)
    if sharded:
        assert mode == "nn"
        bchip = b.chip
        b_spec = pl.BlockSpec((None, d, tn), lambda i, j: (bchip(j), 0, 0))
        b = b.arr
    elif mode == "nn":
        b_spec = pl.BlockSpec((d, tn), lambda i, j: (0, j))
    else:
        b_spec = pl.BlockSpec((tn, d), lambda i, j: (j, 0))
    ins = [h, g, b] + ([bias] if bias is not None else [])
    in_specs = [pl.BlockSpec((tm, d), lambda i, j: (i, 0)), pl.BlockSpec((1, d), lambda i, j: (0, 0)), b_spec]
    if bias is not None:
        in_specs.append(pl.BlockSpec((1, tn), lambda i, j: (0, j)))

    def body(h_ref, g_ref, b_ref, *refs):
        o_ref, xn_ref = refs[-2:]
        x = h_ref[...]
        r = lax.rsqrt(jnp.mean(x * x, axis=-1, keepdims=True) + EPS)
        xn = (x * r * g_ref[...]).astype(xn_ref.dtype)

        @pl.when(pl.program_id(1) == 0)
        def _():
            xn_ref[...] = xn

        out = _mm(xn, b_ref[...], NN if mode == "nn" else NT)
        o_ref[...] = out if bias is None else out + refs[0][...]

    return pl.pallas_call(
        body, name=name, grid=(n // tm, n_out // tn), in_specs=in_specs,
        out_specs=[pl.BlockSpec((tm, tn), lambda i, j: (i, j)), pl.BlockSpec((tm, d), lambda i, j: (i, 0))],
        out_shape=[S((n, n_out), F32), S((n, d), MXU_DTYPE)],
        compiler_params=pltpu.CompilerParams(dimension_semantics=("parallel", "arbitrary"), vmem_limit_bytes=VMEM_LIMIT),
    )(*ins)


def _rms_bwd_math(x, g, dy):
    r = lax.rsqrt(jnp.mean(x * x, axis=-1, keepdims=True) + EPS)
    xh = x * r
    dg = jnp.sum(dy * xh, axis=0, keepdims=True)
    dxh = dy * g
    dx = r * (dxh - xh * jnp.mean(dxh * xh, axis=-1, keepdims=True))
    return dx, dg


def nt_rms_bwd(a, b, h, g, dh_in, name, extra=None, colsum=False, b_kd=False):
    n, k = a.shape
    d = h.shape[1]
    tm = _row_tile(n, 1024)
    sharded = isinstance(b, ColSharded)
    tk = b.per if sharded else _pick_tile(k, 1024)
    nk = k // tk
    dims = NN if b_kd else NT
    if sharded:
        bchip = b.chip
        b_spec = pl.BlockSpec((None, d, tk), lambda i, kk: (bchip(kk), 0, 0))
        b = b.arr
    elif b_kd:
        b_spec = pl.BlockSpec((tk, d), lambda i, kk: (kk, 0))
    else:
        b_spec = pl.BlockSpec((d, tk), lambda i, kk: (0, kk))
    row = pl.BlockSpec((tm, d), lambda i, kk: (i, 0))
    vec = pl.BlockSpec((1, d), lambda i, kk: (0, 0))
    ins, in_specs = [a, b, h, g, dh_in], [pl.BlockSpec((tm, tk), lambda i, kk: (i, kk)), b_spec, row, vec, row]
    if extra is not None:
        k2 = extra[0].shape[1]
        ins += list(extra)
        in_specs += [pl.BlockSpec((tm, k2), lambda i, kk: (i, 0)),
                     pl.BlockSpec((k2, d) if b_kd else (d, k2), lambda i, kk: (0, 0))]
    n_in = len(ins)

    def body(*refs):
        a_ref, b_ref, h_ref, g_ref, dh_ref = refs[:5]
        o_ref, dg_ref = refs[n_in], refs[n_in + 1]
        acc_ref = refs[-1]
        i, kk = pl.program_id(0), pl.program_id(1)

        @pl.when(kk == 0)
        def _():
            acc_ref[...] = _mm(refs[5][...], refs[6][...], dims) if extra is not None else jnp.zeros_like(acc_ref)

        acc_ref[...] += _mm(a_ref[...], b_ref[...], dims)

        @pl.when(kk == nk - 1)
        def _():
            dx, dg = _rms_bwd_math(h_ref[...], g_ref[...], acc_ref[...])
            out = dh_ref[...] + dx
            o_ref[...] = out
            _acc(dg_ref, i, dg)
            if colsum:
                _acc(refs[n_in + 2], i, jnp.sum(out, axis=0, keepdims=True))

    n_vec = 2 if colsum else 1
    return pl.pallas_call(
        body, name=name, grid=(n // tm, nk), in_specs=in_specs, out_specs=[row] + [vec] * n_vec,
        out_shape=[S((n, d), F32)] + [S((1, d), F32)] * n_vec, scratch_shapes=[pltpu.VMEM((tm, d), F32)],
        compiler_params=pltpu.CompilerParams(dimension_semantics=("arbitrary", "arbitrary"), vmem_limit_bytes=VMEM_LIMIT),
    )(*ins)


def swiglu_in(h, g, w_in, name):
    n, d = h.shape
    per = w_in.per
    nj = w_in.nch // 2
    tile = _row_tile(n, 1024)

    def body(h_ref, g_ref, wg_ref, wu_ref, xn_ref, u_ref, hm_ref):
        x = h_ref[...]
        r = lax.rsqrt(jnp.mean(x * x, axis=-1, keepdims=True) + EPS)
        xn = (x * r * g_ref[...]).astype(xn_ref.dtype)

        @pl.when(pl.program_id(1) == 0)
        def _():
            xn_ref[...] = xn

        a = _mm(xn, wg_ref[...])
        b = _mm(xn, wu_ref[...])
        u_ref[:, :per] = a.astype(u_ref.dtype)
        u_ref[:, per:] = b.astype(u_ref.dtype)
        hm_ref[...] = (a * _sigmoid(a) * b).astype(hm_ref.dtype)

    return pl.pallas_call(
        body, name=name, grid=(n // tile, nj),
        in_specs=[pl.BlockSpec((tile, d), lambda i, j: (i, 0)), pl.BlockSpec((1, d), lambda i, j: (0, 0)),
                  pl.BlockSpec((None, d, per), lambda i, j: (j, 0, 0)), pl.BlockSpec((None, d, per), lambda i, j: (nj + j, 0, 0))],
        out_specs=[pl.BlockSpec((tile, d), lambda i, j: (i, 0)), pl.BlockSpec((tile, 2 * per), lambda i, j: (i, j)),
                   pl.BlockSpec((tile, per), lambda i, j: (i, j))],
        out_shape=[S((n, d), MXU_DTYPE), S((n, 2 * nj * per), MXU_DTYPE), S((n, nj * per), MXU_DTYPE)],
        compiler_params=pltpu.CompilerParams(dimension_semantics=("parallel", "arbitrary"), vmem_limit_bytes=VMEM_LIMIT),
    )(h, g, w_in.arr, w_in.arr)


def swiglu_out_bwd(dh, w_out, u, after, name):
    n, d = dh.shape
    f = w_out.shape[0]
    per = u.shape[1] // 4
    nj = f // per
    tile = _row_tile(n, 1024)

    def body(dh_ref, w_ref, u_ref, after_ref, du_ref):
        dm = 0.5 * _mm(dh_ref[...], w_ref[...], NT)
        a = u_ref[:, :per].astype(F32)
        b = u_ref[:, per:].astype(F32)
        s = _sigmoid(a)
        du_ref[:, :per] = (dm * b * s * (1.0 + a * (1.0 - s))).astype(du_ref.dtype)
        du_ref[:, per:] = (dm * a * s).astype(du_ref.dtype)

    return pl.pallas_call(
        body, name=name, grid=(n // tile, nj),
        in_specs=[pl.BlockSpec((tile, d), lambda i, j: (i, 0)), pl.BlockSpec((per, d), lambda i, j: (j, 0)),
                  pl.BlockSpec((tile, 2 * per), lambda i, j: (i, j)), ANY],
        out_specs=pl.BlockSpec((tile, 2 * per), lambda i, j: (i, j)),
        out_shape=S(u.shape, MXU_DTYPE),
        compiler_params=pltpu.CompilerParams(dimension_semantics=("parallel", "parallel"), vmem_limit_bytes=VMEM_LIMIT),
    )(dh, w_out, u, after)


def ple_fwd(h, g, w_gate, pp, name):
    n, d = h.shape
    tile = _row_tile(n, 512)

    def body(h_ref, g_ref, w_ref, pp_ref, o_ref, gl_ref, xn_ref):
        x = h_ref[...]
        r = lax.rsqrt(jnp.mean(x * x, axis=-1, keepdims=True) + EPS)
        xn = (x * r * g_ref[...]).astype(xn_ref.dtype)
        xn_ref[...] = xn
        gl = _mm(xn, w_ref[...])
        gl_ref[...] = gl
        o_ref[...] = x + _sigmoid(gl) * pp_ref[...]

    return _call(body, name, (n // tile,), [_rs(tile, d), _ps((1, d)), _ps(w_gate.shape), _rs(tile, d)],
                 [_rs(tile, d)] * 3, [S((n, d), F32), S((n, d), F32), S((n, d), MXU_DTYPE)], sem=("parallel",))(h, g, w_gate, pp)


def ple_bwd(dh, gl, pp, w_gate, h, g, after, name):
    n, d = dh.shape
    tile = _row_tile(n, 512)

    def body(dh_ref, gl_ref, pp_ref, w_ref, h_ref, g_ref, after_ref, o_ref, dpp_ref, dgl_ref, dg_ref):
        i = pl.program_id(0)
        s = _sigmoid(gl_ref[...])
        dh_ = dh_ref[...]
        dpp_ref[...] = (dh_ * s).astype(dpp_ref.dtype)
        dgl = (dh_ * pp_ref[...] * s * (1.0 - s)).astype(dgl_ref.dtype)
        dgl_ref[...] = dgl
        dx, dg = _rms_bwd_math(h_ref[...], g_ref[...], _mm(dgl, w_ref[...], NT))
        o_ref[...] = dh_ + dx
        _acc(dg_ref, i, dg)

    return _call(body, name, (n // tile,),
                 [_rs(tile, d)] * 3 + [_ps(w_gate.shape), _rs(tile, d), _ps((1, d)), ANY],
                 [_rs(tile, d)] * 3 + [_ps((1, d))],
                 [S((n, d), F32), S((n, d), MXU_DTYPE), S((n, d), MXU_DTYPE), S((1, d), F32)],
                 sem=("arbitrary",))(dh, gl, pp, w_gate, h, g, after)


def loss_head(h, g, target, name):
    n, d = h.shape
    tile = _row_tile(n, 512)

    def body(h_ref, g_ref, t_ref, dh_ref, dg_ref, loss_ref):
        i = pl.program_id(0)
        x = h_ref[...]
        gg = g_ref[...]
        r = lax.rsqrt(jnp.mean(x * x, axis=-1, keepdims=True) + EPS)
        err = x * r * gg - t_ref[...]
        part = 0.5 * jnp.sum(jnp.mean(err * err, axis=-1, keepdims=True), axis=0, keepdims=True)
        dx, dg = _rms_bwd_math(x, gg, err * (1.0 / d))
        dh_ref[...] = dx
        _acc(dg_ref, i, dg)
        _acc(loss_ref, i, jnp.broadcast_to(part, (8, LANE)))

    return _call(body, name, (n // tile,), [_rs(tile, d), _ps((1, d)), _rs(tile, d)],
                 [_rs(tile, d), _ps((1, d)), _ps((8, LANE))], [S((n, d), F32), S((1, d), F32), S((8, LANE), F32)],
                 sem=("arbitrary",))(h, g, target)


def _adamw_math(w, g, m, v):
    c1 = np.float32(1.0 - ADAM_B1 ** ADAM_STEP)
    c2 = np.float32(1.0 - ADAM_B2 ** ADAM_STEP)
    mm = ADAM_B1 * m + (1.0 - ADAM_B1) * g
    vv = ADAM_B2 * v + (1.0 - ADAM_B2) * (g * g)
    return -ADAM_LR * ((mm / c1) / (jnp.sqrt(vv / c2) + ADAM_EPS) + ADAM_WD * w), mm, vv


def adamw_layer(w, pack, off, m, v, li, prev, name):
    n, r, c = w.shape

    def body(w_ref, g_ref, m_ref, v_ref, *refs):
        go_ref, d_ref, mo_ref, vo_ref = refs[-4:]
        g = g_ref[...]
        go_ref[...] = g
        d_ref[...], mo_ref[...], vo_ref[...] = _adamw_math(w_ref[...], g, m_ref[...], v_ref[...])

    if r % 8 == 0:
        cap = 2**21 // (4 * c) // 8 * 8
        tile = next(t for t in range(min(cap, r), 7, -8) if r % t == 0 and off % t == 0)
        ob, steps = off // tile, r // tile
        blk = pl.BlockSpec((None, tile, c), lambda i: (li, i, 0))
        g_spec = pl.BlockSpec((tile, c), lambda i: (ob + i, 0))
    else:
        assert off == 0 and pack.shape[0] == r and c % (2 * LANE) == 0
        steps = c // (2 * LANE)
        blk = pl.BlockSpec((None, r, 2 * LANE), lambda i: (li, 0, i))
        g_spec = pl.BlockSpec((r, 2 * LANE), lambda i: (0, i))
    prev = list(prev) if prev is not None else []
    return pl.pallas_call(
        body, name=name, grid=(steps,),
        in_specs=[blk, g_spec, blk, blk] + [ANY] * len(prev),
        out_specs=[blk] * 4, out_shape=[S((n, r, c), F32)] * 4,
        input_output_aliases={4 + j: j for j in range(len(prev))},
        compiler_params=pltpu.CompilerParams(dimension_semantics=("parallel",), vmem_limit_bytes=VMEM_LIMIT),
    )(w, pack, m, v, *prev)


def adamw(w, g, m, v, name):
    r, c = w.shape
    tile = r
    for t in (512, 256, 128, 64, 32, 16, 8):
        if r % t == 0 and t * c * 4 <= 2**21:
            tile = t
            break

    def body(w_ref, g_ref, m_ref, v_ref, d_ref, mo_ref, vo_ref):
        d_ref[...], mo_ref[...], vo_ref[...] = _adamw_math(w_ref[...], g_ref[...], m_ref[...], v_ref[...])

    return _call(body, name, (r // tile,), [_rs(tile, c)] * 4, [_rs(tile, c)] * 3, [S((r, c), F32)] * 3,
                 sem=("parallel",))(w, g, m, v)


TAP_VREGS = 32


def _taps(src, w_ref, offsets, tile, put, bias=None):
    c = src.shape[1]
    rp = max(8, TAP_VREGS * 8 * LANE // c // 8 * 8)
    for r0 in range(0, tile, rp):
        acc = jnp.zeros((rp, c), F32) if bias is None else jnp.zeros((rp, c), F32) + bias
        for k, o in enumerate(offsets):
            acc = acc + w_ref[k:k + 1, :] * src[r0 + o:r0 + o + rp, :]
        put(slice(r0, r0 + rp), acc)


def _taps_fwd(sc, w_ref, width, halo, tile, put, bias):
    _taps(sc, w_ref, [halo - (width - 1) + k for k in range(width)], tile, put, bias)


def _taps_bwd_x(sc_d, w_ref, width, tile, put):
    _taps(sc_d, w_ref, [(width - 1) - k for k in range(width)], tile, put)


def _taps_bwd_w(dy, sc, dw_ref, width, halo, tile, i):
    @pl.when(i == 0)
    def _():
        dw_ref[...] = jnp.zeros_like(dw_ref)

    for k in range(width):
        o = halo - (width - 1) + k
        dw_ref[k:k + 1, :] += jnp.sum(dy * sc[o:o + tile, :], axis=0, keepdims=True)


def _ln_stats(x):
    mu = jnp.mean(x, axis=-1, keepdims=True)
    xc = x - mu
    r = lax.rsqrt(jnp.mean(xc * xc, axis=-1, keepdims=True) + EPS)
    return xc * r, r


def conv_group_fwd(proj, cw, cb, lg, lb, name):
    n = proj.shape[0]
    d = cw.shape[1]
    tile = _row_tile(n, 256)
    halo = 32

    def body(v_ref, g_ref, vp_ref, gp_ref, cw_ref, cb_ref, lg_ref, lb_ref, u_ref, u1_ref, sc):
        i = pl.program_id(0)
        first = (i > 0).astype(F32)
        sc[0:halo, :] = vp_ref[tile - halo:, :] * _sigmoid(gp_ref[tile - halo:, :]) * first
        sc[halo:, :] = v_ref[...] * _sigmoid(g_ref[...])
        def put(rows, acc):
            u1_ref[rows, :] = acc

        _taps_fwd(sc, cw_ref, CONV_WIDTH, halo, tile, put, cb_ref[...])
        xh, _ = _ln_stats(u1_ref[...])
        y = xh * lg_ref[...] + lb_ref[...]
        u_ref[...] = (y * _sigmoid(y)).astype(u_ref.dtype)

    return _call(body, name, (n // tile,),
                 [_rs(tile, d, 0), _rs(tile, d, 1), _rs(tile, d, 0, -1), _rs(tile, d, 1, -1),
                  _ps(cw.shape), _ps((1, d)), _ps((1, d)), _ps((1, d))],
                 [_rs(tile, d), _rs(tile, d)], [S((n, d), MXU_DTYPE), S((n, d), F32)],
                 scratch=[pltpu.VMEM((halo + tile, d), F32)], sem=("arbitrary",))(proj, proj, proj, proj, cw, cb, lg, lb)


def conv_group_bwd(du, u1, proj, cw, lg, lb, after, name):
    n = proj.shape[0]
    d = cw.shape[1]
    tile = _row_tile(n, 256)
    halo = 32
    nt = n // tile

    def body(du_ref, dun_ref, u1_ref, u1n_ref, v_ref, g_ref, vp_ref, gp_ref, cw_ref, lg_ref, lb_ref, after_ref,
             dp_ref, dcw_ref, dcb_ref, dlg_ref, dlb_ref, sc, sc_d):
        i = pl.program_id(0)

        def ln_swish_bwd(dy_, u1_):
            xh, r = _ln_stats(u1_)
            y = xh * lg_ref[...] + lb_ref[...]
            s = _sigmoid(y)
            dyy = dy_ * s * (1.0 + y * (1.0 - s))
            dxh = dyy * lg_ref[...]
            dx = r * (dxh - jnp.mean(dxh, axis=-1, keepdims=True) - xh * jnp.mean(dxh * xh, axis=-1, keepdims=True))
            return dx, jnp.sum(dyy * xh, axis=0, keepdims=True), jnp.sum(dyy, axis=0, keepdims=True)

        du1, dlg, dlb = ln_swish_bwd(du_ref[...].astype(F32), u1_ref[...])
        du1n, _, _ = ln_swish_bwd(dun_ref[0:halo, :].astype(F32), u1n_ref[0:halo, :])
        sc_d[0:tile, :] = du1
        sc_d[tile:, :] = du1n * (i < nt - 1).astype(F32)
        sc[0:halo, :] = vp_ref[tile - halo:, :] * _sigmoid(gp_ref[tile - halo:, :]) * (i > 0).astype(F32)
        sc[halo:, :] = v_ref[...] * _sigmoid(g_ref[...])

        def put(rows, du0):
            sig = _sigmoid(g_ref[rows, :])
            dp_ref[rows, :d] = (du0 * sig).astype(dp_ref.dtype)
            dp_ref[rows, d:] = (du0 * v_ref[rows, :] * sig * (1.0 - sig)).astype(dp_ref.dtype)

        _taps_bwd_x(sc_d, cw_ref, CONV_WIDTH, tile, put)
        _taps_bwd_w(du1, sc, dcw_ref, CONV_WIDTH, halo, tile, i)
        _acc(dcb_ref, i, jnp.sum(du1, axis=0, keepdims=True))
        _acc(dlg_ref, i, dlg)
        _acc(dlb_ref, i, dlb)

    return _call(body, name, (nt,),
                 [_rs(tile, d), _rs(tile, d, 0, 1, nt), _rs(tile, d), _rs(tile, d, 0, 1, nt),
                  _rs(tile, d, 0), _rs(tile, d, 1), _rs(tile, d, 0, -1), _rs(tile, d, 1, -1),
                  _ps(cw.shape), _ps((1, d)), _ps((1, d)), ANY],
                 [_rs(tile, 2 * d), _ps(cw.shape), _ps((1, d)), _ps((1, d)), _ps((1, d))],
                 [S((n, proj.shape[1]), MXU_DTYPE), S(cw.shape, F32), S((1, d), F32), S((1, d), F32), S((1, d), F32)],
                 scratch=[pltpu.VMEM((halo + tile, d), F32), pltpu.VMEM((tile + halo, d), F32)],
                 sem=("arbitrary",))(du, du, u1, u1, proj, proj, proj, proj, cw, lg, lb, after)


def ssm_conv_fwd(proj, dtr, sw, sb, dtb, name):
    n = proj.shape[0]
    w = sw.shape[1]
    inner = SSM_HEADS * HEAD_DIM
    tile = _row_tile(n, 256)
    halo = 8

    def body(x_ref, xp_ref, dtr_ref, sw_ref, sb_ref, dtb_ref, pre_ref, xs_ref, bc_ref, dt_ref, sc):
        i = pl.program_id(0)
        sc[0:halo, :] = xp_ref[tile - halo:, :] * (i > 0).astype(F32)
        sc[halo:, :] = x_ref[...]
        def put(rows, acc):
            pre_ref[rows, :] = acc

        _taps_fwd(sc, sw_ref, SSM_CONV, halo, tile, put, sb_ref[...])
        pre = pre_ref[...]
        act = pre * _sigmoid(pre)
        xs_ref[...] = act[:, :inner]
        bc_ref[...] = act[:, inner:]
        dt = _softplus(dtr_ref[...] + dtb_ref[...])
        dt_ref[...] = jnp.where(_iota(dt.shape, 1) < SSM_HEADS, dt, 0.0)

    return _call(body, name, (n // tile,),
                 [_rs(tile, w, 2), _rs(tile, w, 2, -1), _rs(tile, LANE), _ps(sw.shape), _ps((1, w)), _ps((1, LANE))],
                 [_rs(tile, w), _rs(tile, inner), _rs(tile, w - inner), _rs(tile, LANE)],
                 [S((n, w), F32), S((n, inner), F32), S((n, w - inner), F32), S((n, LANE), F32)],
                 scratch=[pltpu.VMEM((halo + tile, w), F32)], sem=("arbitrary",))(proj, proj, dtr, sw, sb, dtb)


def ssm_conv_bwd(dxs, dbc, pre, proj, sw, dproj, name):
    n = proj.shape[0]
    w = sw.shape[1]
    inner = SSM_HEADS * HEAD_DIM
    tile = _row_tile(n, 256)
    halo = 8
    nt = n // tile

    def body(dxs_ref, dxsn_ref, dbc_ref, dbcn_ref, pre_ref, pren_ref, x_ref, xp_ref, sw_ref, dp_in_ref,
             dx_ref, dsw_ref, dsb_ref, sc, sc_d):
        i = pl.program_id(0)

        def silu_bwd(d_, p_):
            s = _sigmoid(p_)
            return d_ * s * (1.0 + p_ * (1.0 - s))

        sc_d[0:tile, :inner] = silu_bwd(dxs_ref[...], pre_ref[:, :inner])
        sc_d[0:tile, inner:] = silu_bwd(dbc_ref[...], pre_ref[:, inner:])
        last = (i < nt - 1).astype(F32)
        sc_d[tile:, :inner] = silu_bwd(dxsn_ref[0:halo, :], pren_ref[0:halo, :inner]) * last
        sc_d[tile:, inner:] = silu_bwd(dbcn_ref[0:halo, :], pren_ref[0:halo, inner:]) * last
        sc[0:halo, :] = xp_ref[tile - halo:, :] * (i > 0).astype(F32)
        sc[halo:, :] = x_ref[...]
        dpre = sc_d[0:tile, :]
        def put(rows, acc):
            dx_ref[rows, :] = acc.astype(dx_ref.dtype)

        _taps_bwd_x(sc_d, sw_ref, SSM_CONV, tile, put)
        _taps_bwd_w(dpre, sc, dsw_ref, SSM_CONV, halo, tile, i)
        _acc(dsb_ref, i, jnp.sum(dpre, axis=0, keepdims=True))

    return pl.pallas_call(
        body, name=name, grid=(nt,),
        in_specs=[_rs(tile, inner), _rs(tile, inner, 0, 1, nt), _rs(tile, w - inner), _rs(tile, w - inner, 0, 1, nt),
                  _rs(tile, w), _rs(tile, w, 0, 1, nt), _rs(tile, w, 2), _rs(tile, w, 2, -1), _ps(sw.shape), ANY],
        out_specs=[_rs(tile, w, 2), _ps(sw.shape), _ps((1, w))],
        out_shape=[S(dproj.shape, dproj.dtype), S(sw.shape, F32), S((1, w), F32)],
        scratch_shapes=[pltpu.VMEM((halo + tile, w), F32), pltpu.VMEM((tile + halo, w), F32)],
        input_output_aliases={9: 0},
        compiler_params=pltpu.CompilerParams(dimension_semantics=("arbitrary",), vmem_limit_bytes=VMEM_LIMIT),
    )(dxs, dxs, dbc, dbc, pre, pre, proj, proj, sw, dproj)


def _ssd_prologue(dt_ref, dtT_ref, al_ref, alc_ref):
    row = _iota((CHUNK, CHUNK), 0)
    col = _iota((CHUNK, CHUNK), 1)
    dt = dt_ref[:, :SSM_HEADS]
    a_row = -jnp.exp(al_ref[:, :SSM_HEADS])
    a_col = -jnp.exp(alc_ref[...])
    cs = _01mm((row >= col).astype(F32), dt * a_row)
    csT = _mm01(dtT_ref[...] * a_col, (row <= col).astype(F32))
    return dt, a_row, cs, csT, row, col


def _decay(cs, csT, h, row, col):
    lm = jnp.exp(jnp.where(row >= col, cs[:, h:h + 1] - csT[h:h + 1, :], -1e30))
    lmT = jnp.exp(jnp.where(col >= row, csT[h:h + 1, :] - cs[:, h:h + 1], -1e30))
    return lm, lmT


def ssd_fwd(xs, bc, dt, dtT, alog_row, alog_col, name):
    n, width = xs.shape
    nc = n // CHUNK
    gw = width // SSM_GROUPS
    hpg = SSM_HEADS // SSM_GROUPS
    ns = SSM_STATE

    def body(xs_ref, bc_ref, dt_ref, dtT_ref, al_ref, alc_ref, y_ref, hs_ref, h_sc):
        i = pl.program_id(0)

        @pl.when(i == 0)
        def _():
            h_sc[...] = jnp.zeros_like(h_sc)

        dt, a_row, cs, csT, row, col = _ssd_prologue(dt_ref, dtT_ref, al_ref, alc_ref)
        indT = _head_indicator(width, SSM_HEADS, transposed=True)
        dt_full = _mm01(dt, indT)
        e_full = jnp.exp(_mm01(cs, indT))
        dte_full = jnp.exp(_mm01(cs[CHUNK - 1:CHUNK, :] - cs, indT))
        xt = xs_ref[...] * dt_full
        hs_ref[0] = h_sc[...]
        lo = _iota((CHUNK, 2 * HEAD_DIM), 1) < HEAD_DIM
        groups = range(SSM_GROUPS)
        bgs = [bc_ref[:, g * ns:(g + 1) * ns] for g in groups]
        cgs = [bc_ref[:, (SSM_GROUPS + g) * ns:(SSM_GROUPS + g + 1) * ns] for g in groups]
        gms = [_mm(cg, bg, NT) for cg, bg in zip(cgs, bgs)]
        yoffs = [e_full[:, g * gw:(g + 1) * gw] * _mm(cgs[g], h_sc[g * gw:(g + 1) * gw, :], NT) for g in groups]
        sgs = [_mm(xt[:, g * gw:(g + 1) * gw] * dte_full[:, g * gw:(g + 1) * gw], bgs[g], TN) for g in groups]
        ms = [gms[h // hpg] * _decay(cs, csT, h, row, col)[0] for h in range(SSM_HEADS)]
        for pr in range(SSM_HEADS // 2):
            c0 = 2 * pr * HEAD_DIM
            xp = xt[:, c0:c0 + 2 * HEAD_DIM]
            yd = jnp.where(lo, _mm(ms[2 * pr], xp), _mm(ms[2 * pr + 1], xp))
            y_ref[:, c0:c0 + 2 * HEAD_DIM] = yd + yoffs[2 * pr // hpg][:, c0 % gw:c0 % gw + 2 * HEAD_DIM]
        for h in range(SSM_HEADS):
            r0 = h * HEAD_DIM
            h_sc[r0:r0 + HEAD_DIM, :] = (h_sc[r0:r0 + HEAD_DIM, :] * jnp.exp(csT[h:h + 1, CHUNK - 1:CHUNK])
                                         + sgs[h // hpg][r0 % gw:r0 % gw + HEAD_DIM, :])

    bcw = bc.shape[1]
    return _call(body, name, (nc,),
                 [_rs(CHUNK, width), _rs(CHUNK, bcw), _rs(CHUNK, LANE), pl.BlockSpec((SSM_HEADS, CHUNK), lambda i: (0, i)),
                  _ps((1, LANE)), _ps((SSM_HEADS, 1))],
                 [_rs(CHUNK, width), pl.BlockSpec((1, width, ns), lambda i: (i, 0, 0))],
                 [S((n, width), F32), S((nc, width, ns), F32)],
                 scratch=[pltpu.VMEM((width, ns), F32)], sem=("arbitrary",))(xs, bc, dt, dtT, alog_row, alog_col)


def ssd_bwd(xs, bc, dt, dtT, alog_row, alog_col, hs, dy, dxs_skip, name):
    n, width = xs.shape
    nc = n // CHUNK
    gw = width // SSM_GROUPS
    hpg = SSM_HEADS // SSM_GROUPS
    ns = SSM_STATE
    bcw = bc.shape[1]

    def body(xs_ref, bc_ref, dt_ref, dtT_ref, al_ref, alc_ref, hs_ref, dy_ref, skip_ref,
             dxs_ref, dbc_ref, ddtr_ref, dal_ref, ddtb_ref, dh_sc, dxt_sc):
        i = pl.program_id(0)

        @pl.when(i == 0)
        def _():
            dh_sc[...] = jnp.zeros_like(dh_sc)

        dt, a_row, cs, csT, row, col = _ssd_prologue(dt_ref, dtT_ref, al_ref, alc_ref)
        indT = _head_indicator(width, SSM_HEADS, transposed=True)
        ind = _head_indicator(width, SSM_HEADS)
        dt_full = _mm01(dt, indT)
        e_full = jnp.exp(_mm01(cs, indT))
        cs_last = cs[CHUNK - 1:CHUNK, :]
        dte = jnp.exp(cs_last - cs)
        dte_full = _mm01(dte, indT)
        xs_ = xs_ref[...]
        xt = xs_ * dt_full
        dy_ = dy_ref[...]
        hmat = hs_ref[0]
        ds = dh_sc[...]
        lo = _iota((CHUNK, 2 * HEAD_DIM), 1) < HEAD_DIM
        head_lane = _iota((1, SSM_HEADS), 1)
        dcs = jnp.zeros((CHUNK, SSM_HEADS), F32)
        ddte = jnp.zeros((CHUNK, SSM_HEADS), F32)
        for g in range(SSM_GROUPS):
            sl = slice(g * gw, (g + 1) * gw)
            bg = bc_ref[:, g * ns:(g + 1) * ns]
            cg = bc_ref[:, (SSM_GROUPS + g) * ns:(SSM_GROUPS + g + 1) * ns]
            gm = _mm(cg, bg, NT)
            gmT = _mm(bg, cg, NT)
            hg = hmat[sl, :]
            dsg = ds[sl, :]
            dyg = dy_[:, sl]
            xtg = xt[:, sl]
            yoff = e_full[:, sl] * _mm(cg, hg, NT)
            edy = e_full[:, sl] * dyg
            bds = _mm(bg, dsg, NT)
            dxt_g = dte_full[:, sl] * bds
            ddte = ddte + _mm01(xtg * bds, ind[sl, :])
            dcs = dcs + _mm01(dyg * yoff, ind[sl, :])
            db = _mm(xtg * dte_full[:, sl], dsg)
            dc = _mm(edy, hg)
            dhc = _mm(edy, cg, TN)
            hs_g = range(g * hpg, (g + 1) * hpg)
            pair = lambda a, h: a[:, (h - g * hpg) // 2 * 2 * HEAD_DIM:((h - g * hpg) // 2 + 1) * 2 * HEAD_DIM]
            decs = [_decay(cs, csT, h, row, col) for h in hs_g]
            xms = [jnp.where(lo if h % 2 == 0 else jnp.logical_not(lo), pair(xtg, h), 0.0) for h in hs_g]
            dms = [_mm(pair(dyg, h), xm, NT) for h, xm in zip(hs_g, xms)]
            dmTs = [_mm(xm, pair(dyg, h), NT) for h, xm in zip(hs_g, xms)]
            mTs = [gmT * lmT for _, lmT in decs]
            for h, (lm, _), dm, dmT, mT in zip(hs_g, decs, dms, dmTs, mTs):
                z = jnp.sum(dm * (gm * lm), axis=1, keepdims=True) - jnp.sum(dmT * mT, axis=1, keepdims=True)
                dcs = dcs + z * (head_lane == h).astype(F32)
            dgs = sum(dm * lm for dm, (lm, _) in zip(dms, decs))
            dgTs = sum(dmT * lmT for dmT, (_, lmT) in zip(dmTs, decs))
            rrs = [_mm(mT, pair(dyg, h)) for h, mT in zip(hs_g, mTs)]
            for pr in range(hpg // 2):
                c0 = 2 * pr * HEAD_DIM
                dxt_sc[:, g * gw + c0:g * gw + c0 + 2 * HEAD_DIM] = (jnp.where(lo, rrs[2 * pr], rrs[2 * pr + 1])
                                                                     + dxt_g[:, c0:c0 + 2 * HEAD_DIM])
            dbc_ref[:, g * ns:(g + 1) * ns] = db + _mm(dgTs, cg)
            dbc_ref[:, (SSM_GROUPS + g) * ns:(SSM_GROUPS + g + 1) * ns] = dc + _mm(dgs, bg)
            for hh in range(hpg):
                h = g * hpg + hh
                r0 = h * HEAD_DIM
                dh_sc[r0:r0 + HEAD_DIM, :] = (dhc[hh * HEAD_DIM:(hh + 1) * HEAD_DIM, :]
                                              + jnp.exp(csT[h:h + 1, CHUNK - 1:CHUNK]) * ds[r0:r0 + HEAD_DIM, :])
        t = ddte * dte
        per_head = jnp.sum(jnp.sum(ds * hmat, axis=1, keepdims=True) * ind, axis=0, keepdims=True)
        last_add = jnp.sum(t, axis=0, keepdims=True) + jnp.exp(cs_last) * per_head
        dcs = dcs - t + jnp.where(_iota((CHUNK, SSM_HEADS), 0) == CHUNK - 1, last_add, 0.0)
        dadt = _01mm((row <= col).astype(F32), dcs)
        dxt = dxt_sc[...]
        ddt = dadt * a_row + _mm01(dxt * xs_, ind)
        dxs_ref[...] = dxt * dt_full + skip_ref[...]
        ddtr = ddt * (1.0 - jnp.exp(-dt))
        ddtr_ref[...] = jnp.zeros_like(ddtr_ref)
        ddtr_ref[:, :SSM_HEADS] = ddtr.astype(ddtr_ref.dtype)
        _acc(dal_ref, i, jnp.sum(dadt * dt, axis=0, keepdims=True) * a_row)
        _acc(ddtb_ref, i, jnp.sum(ddtr, axis=0, keepdims=True))

    rev = lambda i: (nc - 1 - i, 0)
    return _call(body, name, (nc,),
                 [pl.BlockSpec((CHUNK, width), rev), pl.BlockSpec((CHUNK, bcw), rev), pl.BlockSpec((CHUNK, LANE), rev),
                  pl.BlockSpec((SSM_HEADS, CHUNK), lambda i: (0, nc - 1 - i)), _ps((1, LANE)), _ps((SSM_HEADS, 1)),
                  pl.BlockSpec((1, width, ns), lambda i: (nc - 1 - i, 0, 0)), pl.BlockSpec((CHUNK, width), rev),
                  pl.BlockSpec((CHUNK, width), rev)],
                 [pl.BlockSpec((CHUNK, width), rev), pl.BlockSpec((CHUNK, bcw), rev), pl.BlockSpec((CHUNK, LANE), rev),
                  _ps((1, SSM_HEADS)), _ps((1, SSM_HEADS))],
                 [S((n, width), F32), S((n, bcw), F32), S((n, LANE), MXU_DTYPE), S((1, SSM_HEADS), F32), S((1, SSM_HEADS), F32)],
                 scratch=[pltpu.VMEM((width, ns), F32), pltpu.VMEM((CHUNK, width), F32)],
                 sem=("arbitrary",))(xs, bc, dt, dtT, alog_row, alog_col, hs, dy, dxs_skip)


def ssm_gate_fwd(yssd, xs, proj, dfull, gamma, name):
    n, d = yssd.shape
    tile = _row_tile(n, 512)
    gw = d // SSM_GROUPS

    def body(y_ref, xs_ref, z_ref, df_ref, gm_ref, o_ref):
        z = z_ref[...]
        y2 = (y_ref[...] + df_ref[...] * xs_ref[...]) * (z * _sigmoid(z))
        for g in range(SSM_GROUPS):
            yg = y2[:, g * gw:(g + 1) * gw]
            r = lax.rsqrt(jnp.mean(yg * yg, axis=-1, keepdims=True) + EPS)
            o_ref[:, g * gw:(g + 1) * gw] = (yg * r * gm_ref[:, g * gw:(g + 1) * gw]).astype(o_ref.dtype)

    return _call(body, name, (n // tile,), [_rs(tile, d), _rs(tile, d), _rs(tile, d, 2), _ps((1, d)), _ps((1, d))],
                 _rs(tile, d), S((n, d), MXU_DTYPE), sem=("parallel",))(yssd, xs, proj, dfull, gamma)


def ssm_gate_bwd(dy3, yssd, xs, proj, dfull, gamma, dproj, name):
    n, d = yssd.shape
    tile = _row_tile(n, 512)
    gw = d // SSM_GROUPS

    def body(dy_ref, y_ref, xs_ref, z_ref, df_ref, gm_ref, dp_in_ref, dys_ref, dxs_ref, dz_ref, dgm_ref, dd_ref):
        i = pl.program_id(0)
        z = z_ref[...]
        s = _sigmoid(z)
        xs_ = xs_ref[...]
        y1 = y_ref[...] + df_ref[...] * xs_
        y2 = y1 * (z * s)
        dy_ = dy_ref[...].astype(F32)
        dgm = []
        dy2 = []
        for g in range(SSM_GROUPS):
            sl = slice(g * gw, (g + 1) * gw)
            dxg, dgg = _rms_bwd_math(y2[:, sl], gm_ref[:, sl], dy_[:, sl])
            dy2.append(dxg)
            dgm.append(dgg)
        dy2 = jnp.concatenate(dy2, axis=1)
        dy1 = dy2 * (z * s)
        dys_ref[...] = dy1
        dxs_ref[...] = dy1 * df_ref[...]
        dz_ref[...] = (dy2 * y1 * s * (1.0 + z * (1.0 - s))).astype(dz_ref.dtype)
        _acc(dgm_ref, i, jnp.concatenate(dgm, axis=1))
        colsum = jnp.broadcast_to(jnp.sum(dy1 * xs_, axis=0, keepdims=True), (8, d))
        _acc(dd_ref, i, _mm01(colsum, _head_indicator(d, SSM_HEADS))[0:1, :])

    return pl.pallas_call(
        body, name=name, grid=(n // tile,),
        in_specs=[_rs(tile, d), _rs(tile, d), _rs(tile, d), _rs(tile, d, 2), _ps((1, d)), _ps((1, d)), ANY],
        out_specs=[_rs(tile, d), _rs(tile, d), _rs(tile, d, 2), _ps((1, d)), _ps((1, SSM_HEADS))],
        out_shape=[S((n, d), F32), S((n, d), F32), S(dproj.shape, dproj.dtype), S((1, d), F32), S((1, SSM_HEADS), F32)],
        input_output_aliases={6: 2},
        compiler_params=pltpu.CompilerParams(dimension_semantics=("arbitrary",), vmem_limit_bytes=VMEM_LIMIT),
    )(dy3, yssd, xs, proj, dfull, gamma, dproj)


def _rope128(x, cos, sin_signed):
    half = HEAD_DIM // 2
    lane = _iota(x.shape, 1)
    partner = jnp.where((lane % HEAD_DIM) < half, pltpu.roll(x, LANE - half, 1), pltpu.roll(x, half, 1))
    return x * cos + partner * sin_signed


def rope_fwd(qkv, cos, sin, name):
    n, w = qkv.shape
    qw = ATT_HEADS * HEAD_DIM
    kw = ATT_KV_HEADS * HEAD_DIM
    tile = _row_tile(n, 512)

    def body(x_ref, c_ref, s_ref, q_ref, k_ref, v_ref):
        c, s = c_ref[...], s_ref[...]
        for j in range(qw // LANE):
            q_ref[:, j * LANE:(j + 1) * LANE] = _rope128(x_ref[:, j * LANE:(j + 1) * LANE], c, s).astype(q_ref.dtype)
        for j in range(kw // LANE):
            k_ref[:, j * LANE:(j + 1) * LANE] = _rope128(x_ref[:, qw + j * LANE:qw + (j + 1) * LANE], c, s).astype(k_ref.dtype)
        v_ref[...] = x_ref[:, qw + kw:].astype(v_ref.dtype)

    return _call(body, name, (n // tile,), [_rs(tile, w), _rs(tile, LANE), _rs(tile, LANE)],
                 [_rs(tile, qw), _rs(tile, kw), _rs(tile, kw)],
                 [S((n, qw), MXU_DTYPE), S((n, kw), MXU_DTYPE), S((n, kw), MXU_DTYPE)], sem=("parallel",))(qkv, cos, sin)


ATT_GROUP = ATT_HEADS // ATT_KV_HEADS


def _attn_mask(i):
    row = _iota((ATT_GROUP * WINDOW, 2 * WINDOW), 0) % WINDOW
    s = _iota((ATT_GROUP * WINDOW, 2 * WINDOW), 1)
    return (s > row) & (s <= row + WINDOW) & ((s >= WINDOW) | (i > 0))


def _stack_heads(ref, j, kh, lo):
    parts = []
    for t in range(ATT_GROUP):
        h = ATT_GROUP * j + t
        blk = ref[:, (h // 2) * LANE:(h // 2 + 1) * LANE]
        blk = jnp.where(lo if h % 2 == 0 else jnp.logical_not(lo), blk, jnp.zeros_like(blk))
        parts.append(blk if h % 2 == kh else pltpu.roll(blk, HEAD_DIM, 1))
    return jnp.concatenate(parts, axis=0)


def _unstack_heads(stacked, j, kh, lo, put):
    for t in range(0, ATT_GROUP, 2):
        h = ATT_GROUP * j + t
        even = stacked[t * WINDOW:(t + 1) * WINDOW, :]
        odd = stacked[(t + 1) * WINDOW:(t + 2) * WINDOW, :]
        even = even if kh == 0 else pltpu.roll(even, HEAD_DIM, 1)
        odd = odd if kh == 1 else pltpu.roll(odd, HEAD_DIM, 1)
        put(h // 2, jnp.where(lo, even, odd))


def _per_head_rows(ref, j):
    return jnp.concatenate([ref[:, ATT_GROUP * j + t:ATT_GROUP * j + t + 1] for t in range(ATT_GROUP)], axis=0)


def _per_head_scalar(ref, j):
    rows = _iota((ATT_GROUP * WINDOW, 1), 0) // WINDOW
    out = jnp.zeros((ATT_GROUP * WINDOW, 1), F32)
    for t in range(ATT_GROUP):
        out = out + jnp.where(rows == t, ref[:, ATT_GROUP * j + t:ATT_GROUP * j + t + 1], 0.0)
    return out


def attn_fwd(q, k, v, sinks, name):
    n, qw = q.shape
    kw = k.shape[1]
    nb = n // WINDOW
    scale = HEAD_DIM ** -0.5

    def body(q_ref, kc_ref, kp_ref, vc_ref, vp_ref, sk_ref, o_ref, lse_ref):
        i = pl.program_id(0)
        valid = _attn_mask(i)
        lo = _iota((WINDOW, LANE), 1) < HEAD_DIM
        k2 = jnp.concatenate([kp_ref[...], kc_ref[...]], axis=0)
        v2 = jnp.concatenate([vp_ref[...], vc_ref[...]], axis=0)
        lane1 = _iota((1, LANE), 1)
        lse = jnp.zeros((WINDOW, LANE), F32)

        def put_o(qb, val):
            o_ref[:, qb * LANE:(qb + 1) * LANE] = val.astype(o_ref.dtype)

        kv = [(j, j // 2, j % 2) for j in range(ATT_KV_HEADS)]
        logits = [jnp.where(valid, _mm(_stack_heads(q_ref, j, kh, lo), k2[:, kb * LANE:(kb + 1) * LANE], NT) * scale, -1e30)
                  for j, kb, kh in kv]
        sks = [_per_head_scalar(sk_ref, j) for j, _, _ in kv]
        ms = [jnp.maximum(jnp.max(l, axis=-1, keepdims=True), sk) for l, sk in zip(logits, sks)]
        es = [jnp.exp(l - m) for l, m in zip(logits, ms)]
        dens = [jnp.sum(e, axis=-1, keepdims=True) + jnp.exp(sk - m) for e, sk, m in zip(es, sks, ms)]
        for (j, kb, kh), e, m, den in zip(kv, es, ms, dens):
            lse4 = m + jnp.log(den)
            for t in range(ATT_GROUP):
                lse = lse + lse4[t * WINDOW:(t + 1) * WINDOW, :] * (lane1 == ATT_GROUP * j + t).astype(F32)
            _unstack_heads(_mm(e * (1.0 / den), v2[:, kb * LANE:(kb + 1) * LANE]), j, kh, lo, put_o)
        lse_ref[...] = lse

    return _call(body, name, (nb,),
                 [_rs(WINDOW, qw), _rs(WINDOW, kw), _rs(WINDOW, kw, 0, -1), _rs(WINDOW, kw), _rs(WINDOW, kw, 0, -1), _ps((1, LANE))],
                 [_rs(WINDOW, qw), _rs(WINDOW, LANE)], [S((n, qw), MXU_DTYPE), S((n, LANE), F32)],
                 sem=("parallel",))(q, k, k, v, v, sinks)


def attn_bwd(q, k, v, o, do, lse, sinks, name):
    n, qw = q.shape
    kw = k.shape[1]
    nb = n // WINDOW
    scale = HEAD_DIM ** -0.5

    def body(q_ref, kc_ref, kp_ref, vc_ref, vp_ref, o_ref, do_ref, lse_ref, sk_ref,
             dq_ref, dka_ref, dkb_ref, dva_ref, dvb_ref, dsk_ref):
        i = pl.program_id(0)
        valid = _attn_mask(i)
        lo = _iota((WINDOW, LANE), 1) < HEAD_DIM
        k2 = jnp.concatenate([kp_ref[...], kc_ref[...]], axis=0)
        v2 = jnp.concatenate([vp_ref[...], vc_ref[...]], axis=0)
        lane1 = _iota((1, LANE), 1)
        do_ = do_ref[...].astype(F32)
        delta = _mm01(do_ * o_ref[...].astype(F32), _head_indicator(qw, ATT_HEADS))
        dk2 = [jnp.zeros((2 * WINDOW, LANE), F32) for _ in range(kw // LANE)]
        dv2 = [jnp.zeros((2 * WINDOW, LANE), F32) for _ in range(kw // LANE)]
        dsk = jnp.zeros((1, LANE), F32)

        def put_dq(qb, val):
            dq_ref[:, qb * LANE:(qb + 1) * LANE] = val

        kv = [(j, j // 2, j % 2) for j in range(ATT_KV_HEADS)]
        q4s = [_stack_heads(q_ref, j, kh, lo) for j, _, kh in kv]
        do4s = [_stack_heads(do_ref, j, kh, lo) for j, _, kh in kv]
        kks = [k2[:, kb * LANE:(kb + 1) * LANE] for _, kb, _ in kv]
        vvs = [v2[:, kb * LANE:(kb + 1) * LANE] for _, kb, _ in kv]
        lses = [_per_head_rows(lse_ref, j) for j, _, _ in kv]
        dls = [jnp.concatenate([delta[:, ATT_GROUP * j + t:ATT_GROUP * j + t + 1] for t in range(ATT_GROUP)], axis=0)
               for j, _, _ in kv]
        ps = [jnp.exp(jnp.where(valid, _mm(q4, kk, NT) * scale, -1e30) - lse4) for q4, kk, lse4 in zip(q4s, kks, lses)]
        dss = [p * (_mm(do4, vv, NT) - dl) * scale for p, do4, vv, dl in zip(ps, do4s, vvs, dls)]
        for (j, kb, kh), q4, do4, kk, lse4, dl, p, ds in zip(kv, q4s, do4s, kks, lses, dls, ps, dss):
            sd = jnp.exp(_per_head_scalar(sk_ref, j) - lse4) * dl
            for t in range(ATT_GROUP):
                dsk = dsk - (jnp.sum(sd[t * WINDOW:(t + 1) * WINDOW, :], axis=0, keepdims=True)
                             * (lane1 == ATT_GROUP * j + t).astype(F32))
            _unstack_heads(_mm(ds, kk), j, kh, lo, put_dq)
            dk2[kb] = dk2[kb] + _mm(ds, q4, TN)
            dv2[kb] = dv2[kb] + _mm(p, do4, TN)
        for kb in range(kw // LANE):
            dkb_ref[:, kb * LANE:(kb + 1) * LANE] = dk2[kb][0:WINDOW, :]
            dka_ref[:, kb * LANE:(kb + 1) * LANE] = dk2[kb][WINDOW:, :]
            dvb_ref[:, kb * LANE:(kb + 1) * LANE] = dv2[kb][0:WINDOW, :]
            dva_ref[:, kb * LANE:(kb + 1) * LANE] = dv2[kb][WINDOW:, :]
        _acc(dsk_ref, i, dsk)

    return _call(body, name, (nb,),
                 [_rs(WINDOW, qw), _rs(WINDOW, kw), _rs(WINDOW, kw, 0, -1), _rs(WINDOW, kw), _rs(WINDOW, kw, 0, -1),
                  _rs(WINDOW, qw), _rs(WINDOW, qw), _rs(WINDOW, LANE), _ps((1, LANE))],
                 [_rs(WINDOW, qw)] + [_rs(WINDOW, kw)] * 4 + [_ps((1, LANE))],
                 [S((n, qw), F32)] + [S((n, kw), F32)] * 4 + [S((1, LANE), F32)],
                 sem=("arbitrary",))(q, k, k, v, v, o, do, lse, sinks)


def attn_grad_merge(dq, dka, dkb, dva, dvb, cos, sin, name):
    n, qw = dq.shape
    kw = dka.shape[1]
    nb = n // WINDOW
    w = qw + 2 * kw

    def body(dq_ref, dka_ref, dkb_ref, dva_ref, dvb_ref, c_ref, s_ref, o_ref, db_ref):
        i = pl.program_id(0)
        c, s = c_ref[...], -s_ref[...]
        nxt = (i < nb - 1).astype(F32)

        @pl.when(i == 0)
        def _():
            db_ref[...] = jnp.zeros_like(db_ref)

        def put(c0, val):
            o_ref[:, c0:c0 + val.shape[1]] = val.astype(o_ref.dtype)
            db_ref[:, c0:c0 + val.shape[1]] += jnp.sum(val, axis=0, keepdims=True)

        for j in range(qw // LANE):
            put(j * LANE, _rope128(dq_ref[:, j * LANE:(j + 1) * LANE], c, s))
        for j in range(kw // LANE):
            sl = slice(j * LANE, (j + 1) * LANE)
            put(qw + j * LANE, _rope128(dka_ref[:, sl] + dkb_ref[:, sl] * nxt, c, s))
        put(qw + kw, dva_ref[...] + dvb_ref[...] * nxt)

    return _call(body, name, (nb,),
                 [_rs(WINDOW, qw), _rs(WINDOW, kw), _rs(WINDOW, kw, 0, 1, nb), _rs(WINDOW, kw), _rs(WINDOW, kw, 0, 1, nb),
                  _rs(WINDOW, LANE), _rs(WINDOW, LANE)],
                 [_rs(WINDOW, w), _ps((1, w))], [S((n, w), MXU_DTYPE), S((1, w), F32)],
                 sem=("arbitrary",))(dq, dka, dkb, dva, dvb, cos, sin)


def _row(v):
    return v.reshape(1, -1)


def _pad_lanes(v, width=LANE):
    return jnp.pad(v.reshape(1, -1), ((0, 0), (0, width - v.size)))


class LayerWeights(dict):
    def __init__(self, small, fetch):
        super().__init__(small)
        self.fetch = fetch

    def need(self, k, after):
        if k not in self:
            self[k] = self.fetch(k, after)
        return self[k]


def ffn_fwd(h, g, w, keys, tag):
    xn, u, hm = swiglu_in(h, _row(g), w.need(keys[0], h), f"{tag}_in")
    return matmul(hm, w.need(keys[1], hm), "nn", f"{tag}_out", scale=0.5, res=h), (h, xn, u, hm)


class GradSink:
    ORDER = ("ffn1_w_out", "ffn2_w_out", "ple_gate_w", "att_w_o", "hyb_w_out", "ffn1_w_in", "ffn2_w_in", "att_w_qkv",
             "ple_proj_w", "hyb_w_in")

    def __init__(self, shard_shapes, bucket_of):
        self.where, rows = {}, {}
        for k in self.ORDER:
            n, r, c = shard_shapes[k]
            for li in range(n):
                layer = li if k in PER_LAYER else 2 * li + (0 if k in EVEN_ONLY else 1)
                rows_b = rows.setdefault(bucket_of(layer, _stage(k)), {})
                key = c if r % 32 == 0 else k
                off = -(-rows_b.get(key, (0, c))[0] // r) * r
                rows_b[key] = (-(-(off + r) // 32) * 32, c)
                self.where[k, li] = (bucket_of(layer, _stage(k)), key, "r" if _shard_axis(k) == 1 else "c", off, r)
        self.bufs = {b: {key: lax.empty((N_CHIPS, r, c), MXU_DTYPE) for key, (r, c) in rows_b.items()}
                     for b, rows_b in rows.items()}

    def mm(self, k, li, a, b, name, scale=None, c0=0, paired=False):
        bucket, key, kind, off, r = self.where[k, li]
        buf = self.bufs[bucket][key]
        slot = Slot(buf, kind, r if kind == "r" else buf.shape[2], off, c0, paired)
        self.bufs[bucket][key] = matmul(a, b, "tn", name, scale=scale, into=slot)

    def put(self, k, li, chip_major):
        b, key = self.where[k, li][:2]
        pad = self.bufs[b][key].shape[1] - chip_major.shape[1]
        self.bufs[b][key] = jnp.pad(chip_major.astype(self.bufs[b][key].dtype), ((0, 0), (0, pad), (0, 0)))


def ffn_bwd(dh, g, w_in, w_out, saved, tag, sink, keys, layer, after, colsum=False):
    h, xn, u, hm = saved
    sink.mm(keys[1], layer, hm, dh, f"{tag}_dwout", scale=0.5)
    du = swiglu_out_bwd(dh, w_out, u, after, f"{tag}_dhm")
    sink.mm(keys[0], layer, xn, du, f"{tag}_dwin", paired=True)
    outs = nt_rms_bwd(du, ColSharded(w_in.arr, paired=True), h, _row(g), dh, f"{tag}_dxn", colsum=colsum)
    return (outs[0], outs[1].reshape(-1)) + ((outs[2],) if colsum else ())


def _hyb_params(w):
    d = w["conv_dw_b"].size
    inner = SSM_HEADS * HEAD_DIM
    main = 3 * d + w["ssm_conv_b"].size
    return dict(
        w_main=w["hyb_w_in"][:main], w_dt=jnp.pad(w["hyb_w_in"][main:], ((0, LANE - SSM_HEADS), (0, 0))),
        cw=jnp.pad(w["conv_dw_w"], ((0, 32 - CONV_WIDTH), (0, 0))), cb=_row(w["conv_dw_b"]),
        lg=_row(w["conv_ln_g"]), lb=_row(w["conv_ln_b"]),
        sw=jnp.pad(w["ssm_conv_w"], ((0, 8 - SSM_CONV), (0, 0))), sb=_row(w["ssm_conv_b"]),
        dtb=_pad_lanes(w["ssm_dt_bias"]), al_row=_pad_lanes(w["ssm_a_log"]), al_col=w["ssm_a_log"].reshape(-1, 1),
        dfull=_row(jnp.repeat(w["ssm_d"], HEAD_DIM)), gamma=_row(w["ssm_norm"]), d=d, inner=inner, main=main)


def hyb_fwd(h, w, tag):
    w.need("hyb_w_in", h)
    q = _hyb_params(w)
    proj, xn = rms_matmul(h, _row(w["norm_mix"]), q["w_main"], "nt", f"{tag}_in")
    dtr = matmul(xn, q["w_dt"], "nt", f"{tag}_in_dt")
    u, u1 = conv_group_fwd(proj, q["cw"], q["cb"], q["lg"], q["lb"], f"{tag}_conv")
    pre, xs, bc, dt = ssm_conv_fwd(proj, dtr, q["sw"], q["sb"], q["dtb"], f"{tag}_sconv")
    dtT = dt[:, :SSM_HEADS].T
    yssd, hs = ssd_fwd(xs, bc, dt, dtT, q["al_row"], q["al_col"], f"{tag}_ssd")
    y = ssm_gate_fwd(yssd, xs, proj, q["dfull"], q["gamma"], f"{tag}_gate")
    wo = w.need("hyb_w_out", u)
    h2 = matmul(u, wo[:q["d"]], "nn", f"{tag}_out_a", res=h)
    h2 = matmul(y, wo[q["d"]:], "nn", f"{tag}_out_b", res=h2)
    return h2, (h, xn, proj, u, u1, pre, xs, bc, dt, dtT, yssd, hs, y)


def hyb_bwd(dh, w, saved, tag, sink, layer, after):
    q = _hyb_params(w)
    h, xn, proj, u, u1, pre, xs, bc, dt, dtT, yssd, hs, y = saved
    du = matmul(dh, w["hyb_w_out"][:q["d"]], "nt", f"{tag}_du")
    dy3 = matmul(dh, w["hyb_w_out"][q["d"]:], "nt", f"{tag}_dy")
    sink.mm("hyb_w_out", layer, u, dh, f"{tag}_dwo_a", c0=0)
    sink.mm("hyb_w_out", layer, y, dh, f"{tag}_dwo_b", c0=N_CHIPS // 2)
    dproj, dcw, dcb, dlg, dlb = conv_group_bwd(du, u1, proj, q["cw"], q["lg"], q["lb"], after, f"{tag}_dconv")
    dyssd, dxs_skip, dproj, dgamma, dd = ssm_gate_bwd(dy3, yssd, xs, proj, q["dfull"], q["gamma"], dproj, f"{tag}_dgate")
    dxs, dbc, ddtr, dalog, ddtb = ssd_bwd(xs, bc, dt, dtT, q["al_row"], q["al_col"], hs, dyssd, dxs_skip, f"{tag}_dssd")
    dproj, dsw, dsb = ssm_conv_bwd(dxs, dbc, pre, proj, q["sw"], dproj, f"{tag}_dsconv")
    dw_in = jnp.concatenate([matmul(dproj, xn, "tn", f"{tag}_dwin"),
                             matmul(ddtr, xn, "tn", f"{tag}_dwin_dt")[:SSM_HEADS]], axis=0)
    sink.put("hyb_w_in", layer, dw_in.reshape((N_CHIPS, -1) + dw_in.shape[1:]))
    dh2, dg = nt_rms_bwd(dproj, q["w_main"], h, _row(w["norm_mix"]), dh, f"{tag}_dxn", extra=(ddtr, q["w_dt"]), b_kd=True)
    grads = dict(norm_mix=dg.reshape(-1), conv_dw_w=dcw[:CONV_WIDTH], conv_dw_b=dcb.reshape(-1),
                 conv_ln_g=dlg.reshape(-1), conv_ln_b=dlb.reshape(-1), ssm_conv_w=dsw[:SSM_CONV], ssm_conv_b=dsb.reshape(-1),
                 ssm_dt_bias=ddtb.reshape(-1), ssm_a_log=dalog.reshape(-1), ssm_d=dd.reshape(-1), ssm_norm=dgamma.reshape(-1))
    return dh2, grads


def rope_tables(n):
    half = HEAD_DIM // 2
    inv = ROPE_THETA ** (-jnp.arange(0, HEAD_DIM, 2, dtype=F32) / HEAD_DIM)
    ang = jnp.arange(n, dtype=F32)[:, None] * inv[None, :]
    cos, sin = jnp.cos(ang), jnp.sin(ang)
    reps = LANE // HEAD_DIM
    return jnp.tile(jnp.concatenate([cos, cos], axis=1), (1, reps)), jnp.tile(jnp.concatenate([-sin, sin], axis=1), (1, reps))


def att_fwd(h, w, tables, tag):
    cos, sin = tables
    qkv, xn = rms_matmul(h, _row(w["norm_mix"]), w.need("att_w_qkv", h), "nn", f"{tag}_qkv", bias=_row(w["att_b_qkv"]))
    q, k, v = rope_fwd(qkv, cos, sin, f"{tag}_rope")
    sinks = _pad_lanes(w["att_sinks"])
    o, lse = attn_fwd(q, k, v, sinks, f"{tag}_attn")
    h2 = matmul(o, w.need("att_w_o", o), "nn", f"{tag}_o", bias=_row(w["att_b_o"]), res=h)
    return h2, (h, xn, q, k, v, o, lse, sinks)


def att_bwd(dh, dh_colsum, w, saved, tables, tag, sink, layer):
    cos, sin = tables
    h, xn, q, k, v, o, lse, sinks = saved
    do = matmul(dh, w["att_w_o"], "nt", f"{tag}_do")
    sink.mm("att_w_o", layer, o, dh, f"{tag}_dwo")
    dq, dka, dkb, dva, dvb, dsk = attn_bwd(q, k, v, o, do, lse, sinks, f"{tag}_dattn")
    dqkv, dbqkv = attn_grad_merge(dq, dka, dkb, dva, dvb, cos, sin, f"{tag}_drope")
    sink.mm("att_w_qkv", layer, xn, dqkv, f"{tag}_dwqkv")
    dh2, dg = nt_rms_bwd(dqkv, w["att_w_qkv"], h, _row(w["norm_mix"]), dh, f"{tag}_dxn")
    grads = dict(norm_mix=dg.reshape(-1), att_b_qkv=dbqkv.reshape(-1), att_sinks=dsk[0, :ATT_HEADS],
                 att_b_o=dh_colsum.reshape(-1))
    return dh2, grads


def ple_block_fwd(h, pe, w, tag):
    pp = matmul(pe, w.need("ple_proj_w", h), "nn", f"{tag}_proj")
    out, gl, xn = ple_fwd(h, _row(w["ple_norm"]), w.need("ple_gate_w", h), pp, f"{tag}_gate")
    return out, (h, xn, gl, pp, pe)


def ple_block_bwd(dh, w, saved, tag, sink, layer, after):
    h, xn, gl, pp, pe = saved
    dh2, dpp, dgl, dg = ple_bwd(dh, gl, pp, w["ple_gate_w"], h, _row(w["ple_norm"]), after, f"{tag}_dgate")
    sink.mm("ple_proj_w", layer, pe, dpp, f"{tag}_dwp")
    sink.mm("ple_gate_w", layer, xn, dgl, f"{tag}_dwg")
    return dh2, dict(ple_norm=dg.reshape(-1))


PER_LAYER = ("norm_ffn1", "ffn1_w_in", "ffn1_w_out", "norm_mix", "norm_ffn2", "ffn2_w_in", "ffn2_w_out",
             "ple_norm", "ple_gate_w", "ple_proj_w")
EVEN_ONLY = ("hyb_w_in", "conv_dw_w", "conv_dw_b", "conv_ln_g", "conv_ln_b", "ssm_conv_w", "ssm_conv_b",
             "ssm_dt_bias", "ssm_a_log", "ssm_d", "ssm_norm", "hyb_w_out")
ODD_ONLY = ("att_w_qkv", "att_b_qkv", "att_sinks", "att_w_o", "att_b_o")


def _layer_index(k, i):
    if k in PER_LAYER:
        return i
    if k in (EVEN_ONLY if i % 2 == 0 else ODD_ONLY):
        return i // 2
    return None


def _stage(k):
    return 0 if k.startswith("ffn1") else (2 if k.startswith(("ffn2", "ple")) else 1)


def trunk_fwd_bwd(x, pe, target, layers, final_norm, sink, stage_done):
    depth = len(layers)
    tables = rope_tables(x.shape[0])
    h = x
    saved = []
    for i, w in enumerate(layers):
        h, s1 = ffn_fwd(h, w["norm_ffn1"], w, ("ffn1_w_in", "ffn1_w_out"), f"l{i}_ffn1")
        if i % 2 == 0:
            h, s2 = hyb_fwd(h, w, f"l{i}_hyb")
        else:
            h, s2 = att_fwd(h, w, tables, f"l{i}_att")
        h, s3 = ffn_fwd(h, w["norm_ffn2"], w, ("ffn2_w_in", "ffn2_w_out"), f"l{i}_ffn2")
        h, s4 = ple_block_fwd(h, pe[i], w, f"l{i}_ple")
        saved.append((s1, s2, s3, s4))
    dh, dgf, loss = loss_head(h, _row(final_norm), target, "loss_head")
    grads = {}
    tie = dgf
    for i in reversed(range(depth)):
        w = layers[i]
        s1, s2, s3, s4 = saved[i]
        dh, g = ple_block_bwd(dh, w, s4, f"l{i}_ple", sink, i, tie)
        odd = i % 2 == 1
        out = ffn_bwd(dh, w["norm_ffn2"], w["ffn2_w_in"], w["ffn2_w_out"], s3, f"l{i}_ffn2", sink,
                      ("ffn2_w_in", "ffn2_w_out"), i, tie, colsum=odd)
        dh = out[0]
        g.update(norm_ffn2=out[1])
        tie = stage_done(i, 2, tie)
        if odd:
            dh, gm = att_bwd(dh, out[2], w, s2, tables, f"l{i}_att", sink, i // 2)
        else:
            dh, gm = hyb_bwd(dh, w, s2, f"l{i}_hyb", sink, i // 2, tie)
        g.update(gm)
        tie = stage_done(i, 1, tie)
        out = ffn_bwd(dh, w["norm_ffn1"], w["ffn1_w_in"], w["ffn1_w_out"], s1, f"l{i}_ffn1", sink,
                      ("ffn1_w_in", "ffn1_w_out"), i, tie)
        dh = out[0]
        g.update(norm_ffn1=out[1])
        tie = stage_done(i, 0, tie)
        for k, v in g.items():
            grads.setdefault(k, []).insert(0, v)
    grads = {k: jnp.stack(v) for k, v in grads.items()}
    grads["final_norm"] = dgf.reshape(-1)
    return loss, dh, grads


def _me():
    return lax.axis_index("x"), lax.axis_index("y"), lax.axis_index("c")


def _flip(v, f):
    return 1 - v if f else v


def _remote(src, dst, send_sems, recv_sems, k, dev):
    return pltpu.make_async_remote_copy(src_ref=src, dst_ref=dst, send_sem=send_sems.at[k], recv_sem=recv_sems.at[k],
                                        device_id=dev, device_id_type=MESH)


CHIP_FLIPS = ((1, 0), (0, 1), (1, 1))
DEV_FLIPS = tuple((fx, fy, fc) for fx in (0, 1) for fy in (0, 1) for fc in (0, 1))[1:]


HBM = pl.BlockSpec(memory_space=pltpu.HBM)
SEM = pl.BlockSpec(memory_space=pltpu.SEMAPHORE)
DATAFLOW = pltpu.SideEffectType.DATAFLOW_SIDE_EFFECTING


def _core_half(ref, axis, c):
    if axis is None:
        return ref
    h = ref.shape[axis] // 2
    return ref.at[pl.ds(c * h, h), :] if axis == 0 else ref.at[:, pl.ds(c * h, h)]


def gather_start(xs, lands, halves, after, name):
    na = len(xs)

    def body(*refs):
        x_refs, land_refs = refs[:na], refs[na:2 * na]
        send_sems, recv_sems = refs[2 * na + 1], refs[2 * na + 2]
        token = refs[-1]
        mx, my, mc = _me()
        chip = 2 * mx + my
        for a in range(na):
            for j, (fx, fy) in enumerate(CHIP_FLIPS):
                _remote(_core_half(x_refs[a], halves[a], mc), _core_half(land_refs[a].at[chip], halves[a], mc),
                        send_sems, recv_sems, 3 * a + j, (_flip(mx, fx), _flip(my, fy), mc)).start()
        token[...] = jnp.zeros_like(token)

    outs = pl.pallas_call(
        body, name=name,
        out_shape=(pltpu.SemaphoreType.DMA((3 * na,)), pltpu.SemaphoreType.DMA((3 * na,)))
        + tuple(pltpu.HBM(x.shape, x.dtype) for x in xs) + tuple(pltpu.HBM(l.shape, l.dtype) for l in lands)
        + (S((8, LANE), F32),),
        in_specs=[HBM] * (2 * na) + [pl.BlockSpec(memory_space=pl.ANY)],
        out_specs=(SEM, SEM) + (HBM,) * (2 * na) + (pl.BlockSpec(memory_space=pltpu.VMEM),),
        input_output_aliases={a: 2 + a for a in range(2 * na)},
        compiler_params=pltpu.CompilerParams(has_side_effects=DATAFLOW),
    )(*[pltpu.with_memory_space_constraint(t, pltpu.HBM) for t in list(xs) + list(lands)], after)
    return outs[0], outs[1], list(outs[2:2 + na]), list(outs[2 + na:2 + 2 * na])


def gather_wait(send_sems, recv_sems, xs, lands, halves, first, after, name):
    na = len(xs)

    def body(*refs):
        x_refs, land_refs = refs[:na], refs[na:2 * na]
        send_sems, recv_sems = refs[2 * na], refs[2 * na + 1]
        mx, my, mc = _me()
        for a in range(na):
            for j, (fx, fy) in enumerate(CHIP_FLIPS):
                px, py = _flip(mx, fx), _flip(my, fy)
                cp = _remote(_core_half(x_refs[a], halves[a], mc), _core_half(land_refs[a].at[2 * px + py], halves[a], mc),
                             send_sems, recv_sems, 3 * (first + a) + j, (px, py, mc))
                cp.wait_send()
                cp.wait_recv()

    outs = pl.pallas_call(
        body, name=name,
        out_shape=tuple(pltpu.HBM(x.shape, x.dtype) for x in xs) + tuple(pltpu.HBM(l.shape, l.dtype) for l in lands),
        in_specs=[HBM] * (2 * na) + [SEM, SEM, pl.BlockSpec(memory_space=pl.ANY)], out_specs=(HBM,) * (2 * na),
        input_output_aliases={a: a for a in range(2 * na)},
        compiler_params=pltpu.CompilerParams(has_side_effects=DATAFLOW),
    )(*xs, *lands, send_sems, recv_sems, after)
    return list(outs[na:])


def forward_halves(land, axis, name):
    def body(in_ref, out_ref, send_sems, recv_sems):
        del in_ref
        mx, my, mc = _me()
        sib = (mx, my, 1 - mc)
        slots = [2 * _flip(mx, fx) + _flip(my, fy) for fx, fy in CHIP_FLIPS]
        cps = [_remote(_core_half(out_ref.at[s], axis, mc), _core_half(out_ref.at[s], axis, mc), send_sems, recv_sems, j, sib)
               for j, s in enumerate(slots)]
        for cp in cps:
            cp.start()
        for j, s in enumerate(slots):
            _remote(_core_half(out_ref.at[s], axis, mc), _core_half(out_ref.at[s], axis, 1 - mc), send_sems, recv_sems, j, sib).wait_recv()
        for cp in cps:
            cp.wait_send()

    return pl.pallas_call(
        body, name=name, out_shape=S(land.shape, land.dtype), in_specs=[ANY], out_specs=ANY, input_output_aliases={0: 0},
        scratch_shapes=[pltpu.SemaphoreType.DMA((3,)), pltpu.SemaphoreType.DMA((3,))])(land)


def all_gather_devices(v, name):
    r, l = v.shape

    def body(v_ref, out_ref, send_sems, recv_sems):
        mx, my, mc = _me()
        me = 4 * mx + 2 * my + mc
        peers = [(_flip(mx, fx), _flip(my, fy), _flip(mc, fc)) for fx, fy, fc in DEV_FLIPS]
        sends = [_remote(v_ref, out_ref.at[me], send_sems, recv_sems, j, p) for j, p in enumerate(peers)]
        for cp in sends:
            cp.start()
        for j, (px, py, pc) in enumerate(peers):
            _remote(v_ref, out_ref.at[4 * px + 2 * py + pc], send_sems, recv_sems, j, (px, py, pc)).wait_recv()
        for cp in sends:
            cp.wait_send()

    out = pl.pallas_call(
        body, name=name, out_shape=S((N_DEV, r, l), v.dtype), in_specs=[ANY], out_specs=ANY,
        scratch_shapes=[pltpu.SemaphoreType.DMA((7,)), pltpu.SemaphoreType.DMA((7,))])(v)
    me = 4 * lax.axis_index("x") + 2 * lax.axis_index("y") + lax.axis_index("c")
    return lax.dynamic_update_slice_in_dim(out, v[None], me, axis=0)


def gather_devices_start(v, name):
    me = 4 * lax.axis_index("x") + 2 * lax.axis_index("y") + lax.axis_index("c")
    land = lax.dynamic_update_slice_in_dim(lax.empty((N_DEV,) + v.shape, v.dtype), v[None], me, axis=0)

    def body(v_ref, land_ref, send_sems, recv_sems, v_thru, land_thru, token):
        mx, my, mc = _me()
        for j, (fx, fy, fc) in enumerate(DEV_FLIPS):
            _remote(v_ref, land_ref.at[4 * mx + 2 * my + mc], send_sems, recv_sems, j,
                    (_flip(mx, fx), _flip(my, fy), _flip(mc, fc))).start()
        token[...] = jnp.zeros_like(token)

    outs = pl.pallas_call(
        body, name=name,
        out_shape=(pltpu.SemaphoreType.DMA((7,)), pltpu.SemaphoreType.DMA((7,)), pltpu.HBM(v.shape, v.dtype),
                   pltpu.HBM(land.shape, land.dtype), S((8, LANE), F32)),
        in_specs=[HBM, HBM], out_specs=(SEM, SEM, HBM, HBM, pl.BlockSpec(memory_space=pltpu.VMEM)),
        input_output_aliases={0: 2, 1: 3}, compiler_params=pltpu.CompilerParams(has_side_effects=DATAFLOW),
    )(pltpu.with_memory_space_constraint(v, pltpu.HBM), pltpu.with_memory_space_constraint(land, pltpu.HBM))
    return outs[:4]


def gather_devices_wait(send_sems, recv_sems, v, land, after, name):
    def body(v_ref, land_ref, send_sems, recv_sems, after_ref, v_dead, got_ref):
        mx, my, mc = _me()
        for j, (fx, fy, fc) in enumerate(DEV_FLIPS):
            px, py, pc = _flip(mx, fx), _flip(my, fy), _flip(mc, fc)
            cp = _remote(v_ref, land_ref.at[4 * px + 2 * py + pc], send_sems, recv_sems, j, (px, py, pc))
            cp.wait_send()
            cp.wait_recv()

    return pl.pallas_call(
        body, name=name, out_shape=(pltpu.HBM(v.shape, v.dtype), pltpu.HBM(land.shape, land.dtype)),
        in_specs=[HBM, HBM, SEM, SEM, pl.BlockSpec(memory_space=pl.ANY)], out_specs=(HBM, HBM),
        input_output_aliases={0: 0, 1: 1}, compiler_params=pltpu.CompilerParams(has_side_effects=DATAFLOW),
    )(v, land, send_sems, recv_sems, after)[1]


def sum_devices(g8, name):
    nd, r, l = g8.shape
    tile = r
    for t in (512, 256, 128, 64, 32, 16, 8):
        if r % t == 0:
            tile = t
            break

    def body(g_ref, o_ref):
        acc = g_ref[0]
        for d in range(1, nd):
            acc = acc + g_ref[d]
        o_ref[...] = acc

    return _call(body, name, (r // tile,), [pl.BlockSpec((nd, tile, l), lambda i: (0, i, 0))], _rs(tile, l), S((r, l), F32),
                 sem=("parallel",))(g8)


def exchange_halves(gs, name):
    na = len(gs)
    nch = gs[0].shape[0]

    def body(*refs):
        g_refs, out_refs = refs[:na], refs[na:2 * na]
        send_sems, recv_sems = refs[2 * na:]
        mx, my, mc = _me()
        sib = (mx, my, 1 - mc)
        cps = []
        for a in range(na):
            half = gs[a].shape[1] // 2
            for j in range(nch):
                cps.append(_remote(g_refs[a].at[j, pl.ds((1 - mc) * half, half), :], out_refs[a].at[j],
                                   send_sems, recv_sems, nch * a + j, sib))
        for cp in cps:
            cp.start()
        for cp in cps:
            cp.wait_recv()
        for cp in cps:
            cp.wait_send()

    return pl.pallas_call(
        body, name=name, out_shape=[S((nch, g.shape[1] // 2, g.shape[2]), g.dtype) for g in gs],
        in_specs=[ANY] * na, out_specs=[ANY] * na,
        scratch_shapes=[pltpu.SemaphoreType.DMA((nch * na,)), pltpu.SemaphoreType.DMA((nch * na,))])(*gs)


def exchange_halves_start(gs, name):
    na = len(gs)
    nch = gs[0].shape[0]
    lands = [lax.empty((nch, g.shape[1] // 2, g.shape[2]), g.dtype) for g in gs]

    def body(*refs):
        g_refs, land_refs = refs[:na], refs[na:2 * na]
        send_sems, recv_sems = refs[2 * na], refs[2 * na + 1]
        mx, my, mc = _me()
        for a in range(na):
            half = gs[a].shape[1] // 2
            for j in range(nch):
                _remote(g_refs[a].at[j, pl.ds((1 - mc) * half, half), :], land_refs[a].at[j], send_sems, recv_sems,
                        nch * a + j, (mx, my, 1 - mc)).start()
        refs[-1][...] = jnp.zeros_like(refs[-1])

    outs = pl.pallas_call(
        body, name=name,
        out_shape=(pltpu.SemaphoreType.DMA((nch * na,)), pltpu.SemaphoreType.DMA((nch * na,)))
        + tuple(pltpu.HBM(t.shape, t.dtype) for t in list(gs) + lands) + (S((8, LANE), F32),),
        in_specs=[HBM] * (2 * na), out_specs=(SEM, SEM) + (HBM,) * (2 * na) + (pl.BlockSpec(memory_space=pltpu.VMEM),),
        input_output_aliases={a: 2 + a for a in range(2 * na)},
        compiler_params=pltpu.CompilerParams(has_side_effects=DATAFLOW),
    )(*[pltpu.with_memory_space_constraint(t, pltpu.HBM) for t in list(gs) + lands])
    return outs[0], outs[1], list(outs[2:2 + na]), list(outs[2 + na:2 + 2 * na]), outs[-1]


def exchange_halves_wait(send_sems, recv_sems, gs, lands, after, name):
    na = len(gs)
    nch = gs[0].shape[0]

    def body(*refs):
        g_refs, land_refs = refs[:na], refs[na:2 * na]
        send_sems, recv_sems = refs[2 * na], refs[2 * na + 1]
        mx, my, mc = _me()
        for a in range(na):
            half = gs[a].shape[1] // 2
            for j in range(nch):
                cp = _remote(g_refs[a].at[j, pl.ds((1 - mc) * half, half), :], land_refs[a].at[j], send_sems, recv_sems,
                             nch * a + j, (mx, my, 1 - mc))
                cp.wait_send()
                cp.wait_recv()

    outs = pl.pallas_call(
        body, name=name, out_shape=tuple(pltpu.HBM(t.shape, t.dtype) for t in list(gs) + list(lands)),
        in_specs=[HBM] * (2 * na) + [SEM, SEM, pl.BlockSpec(memory_space=pl.ANY)], out_specs=(HBM,) * (2 * na),
        input_output_aliases={a: a for a in range(2 * na)},
        compiler_params=pltpu.CompilerParams(has_side_effects=DATAFLOW),
    )(*gs, *lands, send_sems, recv_sems, after)
    return list(outs[:na]), list(outs[na:])


def add_halves(g4, got, name):
    nch, r, l = g4.shape
    half = r // 2
    tile = _pick_rows(half)
    nt = half // tile

    def body(g_ref, r_ref, a_ref, own_ref):
        j = pl.program_id(1)
        chip = 2 * lax.axis_index("x") + lax.axis_index("y")
        val = g_ref[0].astype(F32) + r_ref[0].astype(F32)
        a_ref[0] = val.astype(a_ref.dtype)

        @pl.when(j == chip)
        def _():
            own_ref[...] = val

    return pl.pallas_call(
        body, name=name, grid=(nt, nch),
        in_specs=[pl.BlockSpec((1, tile, l), lambda i, j: (j, lax.axis_index("c") * nt + i, 0)),
                  pl.BlockSpec((1, tile, l), lambda i, j: (j, i, 0))],
        out_specs=[pl.BlockSpec((1, tile, l), lambda i, j: (j, i, 0)), pl.BlockSpec((tile, l), lambda i, j: (i, 0))],
        out_shape=[S((nch, half, l), MXU_DTYPE), S((half, l), F32)],
        compiler_params=pltpu.CompilerParams(dimension_semantics=("parallel", "arbitrary"), vmem_limit_bytes=VMEM_LIMIT))(g4, got)


def _pick_rows(r, cap=640):
    return next((t for t in range(cap - cap % 16, 15, -16) if r % t == 0), r)


def add_chips(own, got, name):
    h, l = own.shape
    tile = _pick_rows(h)

    def body(o_ref, g_ref, out_ref):
        out_ref[...] = ((o_ref[...] + g_ref[0].astype(F32)) + g_ref[1].astype(F32)) + g_ref[2].astype(F32)

    nt = h // tile
    return _call(body, name, (nt,), [_rs(tile, l), pl.BlockSpec((3, tile, l), lambda i: (0, i, 0))],
                 pl.BlockSpec((tile, l), lambda i: (lax.axis_index("c") * nt + i, 0)),
                 S((2 * h, l), F32), sem=("parallel",))(own, got)


def join_halves(bufs, name):
    na = len(bufs)

    def body(*refs):
        out_refs = refs[na:2 * na]
        send_sems, recv_sems = refs[2 * na:]
        mx, my, mc = _me()
        sib = (mx, my, 1 - mc)

        def half(a, hc):
            h = bufs[a].shape[0] // 2
            return out_refs[a].at[pl.ds(hc * h, h), :]

        cps = [_remote(half(a, mc), half(a, mc), send_sems, recv_sems, a, sib) for a in range(na)]
        for cp in cps:
            cp.start()
        for a in range(na):
            _remote(half(a, mc), half(a, 1 - mc), send_sems, recv_sems, a, sib).wait_recv()
        for cp in cps:
            cp.wait_send()

    return pl.pallas_call(
        body, name=name, out_shape=[S(b.shape, b.dtype) for b in bufs], in_specs=[ANY] * na, out_specs=[ANY] * na,
        input_output_aliases={a: a for a in range(na)},
        scratch_shapes=[pltpu.SemaphoreType.DMA((na,)), pltpu.SemaphoreType.DMA((na,))])(*bufs)


def exchange_chips_start(parts, name):
    na = len(parts)
    lands = [lax.empty((3,) + p.shape[1:], p.dtype) for p in parts]

    def body(*refs):
        a_refs, land_refs = refs[:na], refs[na:2 * na]
        send_sems, recv_sems = refs[2 * na], refs[2 * na + 1]
        mx, my, mc = _me()
        for j, (fx, fy) in enumerate(CHIP_FLIPS):
            px, py = _flip(mx, fx), _flip(my, fy)
            for a in range(na):
                _remote(a_refs[a].at[2 * px + py], land_refs[a].at[j], send_sems, recv_sems, 3 * a + j, (px, py, mc)).start()
        refs[-1][...] = jnp.zeros_like(refs[-1])

    outs = pl.pallas_call(
        body, name=name,
        out_shape=(pltpu.SemaphoreType.DMA((3 * na,)), pltpu.SemaphoreType.DMA((3 * na,)))
        + tuple(pltpu.HBM(t.shape, t.dtype) for t in list(parts) + lands) + (S((8, LANE), F32),),
        in_specs=[HBM] * (2 * na), out_specs=(SEM, SEM) + (HBM,) * (2 * na) + (pl.BlockSpec(memory_space=pltpu.VMEM),),
        input_output_aliases={a: 2 + a for a in range(2 * na)},
        compiler_params=pltpu.CompilerParams(has_side_effects=DATAFLOW),
    )(*[pltpu.with_memory_space_constraint(t, pltpu.HBM) for t in list(parts) + lands])
    return outs[0], outs[1], list(outs[2:2 + na]), list(outs[2 + na:2 + 2 * na]), outs[-1]


def exchange_chips_wait(send_sems, recv_sems, parts, lands, after, name):
    na = len(parts)

    def body(*refs):
        a_refs, land_refs = refs[:na], refs[na:2 * na]
        send_sems, recv_sems = refs[2 * na], refs[2 * na + 1]
        mx, my, mc = _me()
        for j, (fx, fy) in enumerate(CHIP_FLIPS):
            px, py = _flip(mx, fx), _flip(my, fy)
            for a in range(na):
                cp = _remote(a_refs[a].at[2 * px + py], land_refs[a].at[j], send_sems, recv_sems, 3 * a + j, (px, py, mc))
                cp.wait_send()
                cp.wait_recv()

    outs = pl.pallas_call(
        body, name=name, out_shape=tuple(pltpu.HBM(t.shape, t.dtype) for t in list(parts) + list(lands)),
        in_specs=[HBM] * (2 * na) + [SEM, SEM, pl.BlockSpec(memory_space=pl.ANY)], out_specs=(HBM,) * (2 * na),
        input_output_aliases={a: a for a in range(2 * na)},
        compiler_params=pltpu.CompilerParams(has_side_effects=DATAFLOW),
    )(*parts, *lands, send_sems, recv_sems, after)
    return list(outs[na:])


def reduce_begin(gs, tag):
    got = exchange_halves(gs, f"{tag}_d2d")
    return reduce_continue(gs, got, tag)


def reduce_continue(gs, got, tag):
    sums = [add_halves(g, r, f"{tag}_add1_{i}") for i, (g, r) in enumerate(zip(gs, got))]
    return [own for _, own in sums], exchange_chips_start([a for a, _ in sums], f"{tag}_ici_start")


def reduce_end(state, after, tag):
    owns, (send_sems, recv_sems, parts, lands, _) = state
    got = exchange_chips_wait(send_sems, recv_sems, parts, lands, after, f"{tag}_ici_wait")
    return [add_chips(own, r, f"{tag}_add2_{i}") for i, (own, r) in enumerate(zip(owns, got))]


PACK_L = 1024


def _pack(arrs, dtype, row_mult, lead=None):
    lead_shape = () if lead is None else arrs[0].shape[:lead]
    flat = jnp.concatenate([a.astype(dtype).reshape(lead_shape + (-1,)) for a in arrs], axis=-1)
    n = flat.shape[-1]
    unit = row_mult * PACK_L
    total = -(-n // unit) * unit
    flat = jnp.pad(flat, [(0, 0)] * len(lead_shape) + [(0, total - n)])
    return flat.reshape(lead_shape + (total // PACK_L, PACK_L))


def _unpack(packed, shapes, lead=None):
    lead_shape = () if lead is None else packed.shape[:lead]
    flat = packed.reshape(lead_shape + (-1,))
    out, off = [], 0
    for shp in shapes:
        n = int(np.prod(shp))
        out.append(flat[..., off:off + n].reshape(lead_shape + tuple(shp)))
        off += n
    return out


def _to_full(gathered, axis):
    t = jnp.moveaxis(gathered, 0, axis)
    shp = t.shape
    return t.reshape(shp[:axis] + (shp[axis] * shp[axis + 1],) + shp[axis + 2:])


WEIGHTS = ("norm_ffn1", "ffn1_w_in", "ffn1_w_out", "norm_mix", "norm_ffn2", "ffn2_w_in", "ffn2_w_out", "ple_norm",
           "ple_gate_w", "ple_proj_w", "hyb_w_in", "conv_dw_w", "conv_dw_b", "conv_ln_g", "conv_ln_b", "ssm_conv_w",
           "ssm_conv_b", "ssm_dt_bias", "ssm_a_log", "ssm_d", "ssm_norm", "hyb_w_out", "att_w_qkv", "att_b_qkv",
           "att_sinks", "att_w_o", "att_b_o", "final_norm")
SHARD_AXIS = dict(ffn1_w_in=2, ffn1_w_out=1, ffn2_w_in=2, ffn2_w_out=1, ple_gate_w=1, ple_proj_w=2, hyb_w_in=2,
                  conv_dw_w=2, ssm_conv_w=2, hyb_w_out=1, att_w_qkv=2, att_b_qkv=1, att_w_o=1, att_b_o=1)
BIG = ("ffn1_w_in", "ffn1_w_out", "ffn2_w_in", "ffn2_w_out", "ple_gate_w", "ple_proj_w", "hyb_w_in", "hyb_w_out",
       "att_w_qkv", "att_w_o")
TRANSPOSED = ("hyb_w_in",)
FORWARDED = (0, 2)


def _shard_axis(k):
    return 1 if k in TRANSPOSED else SHARD_AXIS[k]
SMALL_SHARDED = ("conv_dw_w", "ssm_conv_w", "att_b_qkv", "att_b_o")
SMALL = tuple(k for k in WEIGHTS if k not in BIG)


def _step(x, p, target, w, m, v):
    mx, my = lax.axis_index("x"), lax.axis_index("y")
    chip = 2 * mx + my
    w, m, v = ({k: (a.transpose(0, 2, 1) if k in TRANSPOSED else a) for k, a in d.items()} for d in (w, m, v))

    depth = w["norm_ffn1"].shape[0]
    order = sorted([(k, i) for i in range(depth) for k in BIG if _layer_index(k, i) is not None],
                   key=lambda t: (t[1], _stage(t[0])))
    small_g = all_gather_devices(_pack([w[k] for k in SMALL_SHARDED], F32, 8), "gather_small")
    shards = [w[k][_layer_index(k, i)].astype(MXU_DTYPE) for k, i in order]
    lands = [lax.dynamic_update_slice_in_dim(lax.empty((N_CHIPS,) + s.shape, s.dtype), s[None], chip, axis=0) for s in shards]
    halves = [(0 if s.shape[0] % 32 == 0 else 1) if pos in FORWARDED else None for pos, s in enumerate(shards)]
    send_sems, recv_sems, shards, lands = gather_start(shards, lands, halves, small_g, "gather_start")

    def fetch(i, k, after):
        p = order.index((k, i))
        g, = gather_wait(send_sems, recv_sems, [shards[p]], [lands[p]], [halves[p]], p, after, f"gather_wait_l{i}_{k}")
        if halves[p] is not None:
            g = forward_halves(g, halves[p], f"gather_forward_l{i}_{k}")
        if _shard_axis(k) == 2:
            return ColSharded(g)
        return g.reshape(-1, g.shape[-1])

    small_g = small_g[0::2]
    small_full = {k: _to_full(g, SHARD_AXIS[k])
                  for k, g in zip(SMALL_SHARDED, _unpack(small_g, [w[k].shape for k in SMALL_SHARDED], lead=1))}
    layers = [LayerWeights({k: small_full.get(k, w[k])[_layer_index(k, i)] for k in SMALL if _layer_index(k, i) is not None},
                           functools.partial(fetch, i)) for i in range(depth)]

    def bucket_of(layer, stage):
        return (layer, 0) if layer > 0 else (0, min(stage, 1))

    sink = GradSink({k: w[k].shape for k in BIG}, bucket_of)
    begun = {}
    pending = []

    def stage_done(i, stage, tie):
        if stage == 2 and pending:
            b, (send_sems, recv_sems, gs, lands, _) = pending.pop()
            gs, got = exchange_halves_wait(send_sems, recv_sems, gs, lands, tie, f"grads_l{b[0]}_{b[1]}_d2d_wait")
            begun[b] = reduce_continue(gs, got, f"grads_l{b[0]}_{b[1]}")
            return begun[b][1][-1]
        b = bucket_of(i, stage)
        if stage > 0 and bucket_of(i, stage - 1) == b:
            return tie
        gs = list(sink.bufs[b].values())
        if i > 0:
            pending.append((b, exchange_halves_start(gs, f"grads_l{b[0]}_{b[1]}_d2d_start")))
            return pending[-1][1][-1]
        begun[b] = reduce_begin(gs, f"grads_l{b[0]}_{b[1]}")
        return begun[b][1][-1]

    loss, dx, grads = trunk_fwd_bwd(x[0], p[:, 0], target[0], layers, w["final_norm"], sink, stage_done)

    results = {}

    def finish(buckets, after, tag):
        halves = {b: reduce_end(begun[b], after, f"grads_l{b[0]}_{b[1]}") for b in buckets}
        joined = iter(join_halves([h for b in buckets for h in halves[b]], f"grads_join_{tag}"))
        reduced = {b: {c: next(joined) for c in sink.bufs[b]} for b in buckets}
        last = after
        for (k, li), (b, key, _, off, r) in sink.where.items():
            if b in buckets:
                g = reduced[b][key]
                if r % 8:
                    g, off = g[off:off + r], 0
                results[k] = adamw_layer(w[k], g, off, m[k], v[k], li, results.get(k), f"adamw_{k}_{li}")
                last = results[k][1]
        return last

    vec = gather_devices_start(_pack([loss[0:1, 0:1]] + [grads[k] for k in SMALL], F32, 8), "gather_vectors_start")
    order_b = list(begun)
    started_last = begun[order_b[-1]][1][-1]
    done = finish(order_b[-1:], finish(order_b[:-1], started_last, "early") if len(order_b) > 1 else dx, "last")
    g_out = {k: results[k][0] for k in BIG}
    vec = sum_devices(gather_devices_wait(*vec, done, "gather_vectors_wait"), "sum_vectors")
    parts = _unpack(vec, [(1, 1)] + [grads[k].shape for k in SMALL])
    loss_out = parts[0].reshape(())
    for k, g in zip(SMALL, parts[1:]):
        if k in SHARD_AXIS:
            ax = SHARD_AXIS[k]
            g = lax.dynamic_slice_in_dim(g, chip * w[k].shape[ax], w[k].shape[ax], axis=ax)
        g_out[k] = g

    for k in TRANSPOSED:
        results[k] = [a.transpose(0, 2, 1) for a in results[k]]
    g_out.update({k: results[k][0] for k in TRANSPOSED})
    delta, new_m, new_v = ({k: results[k][j] for k in BIG} for j in (1, 2, 3))
    shapes = [w[k].shape for k in SMALL]
    packed = [_pack([src[k] for k in SMALL], F32, 8) for src in (w, g_out, m, v)]
    outs = adamw(*packed, "adamw_small")
    for dst, o in zip((delta, new_m, new_v), outs):
        for k, a in zip(SMALL, _unpack(o, shapes)):
            dst[k] = a
    return ((loss_out, dx[None]) + tuple(g_out[k] for k in WEIGHTS) + tuple(delta[k] for k in WEIGHTS)
            + tuple(new_m[k] for k in WEIGHTS) + tuple(new_v[k] for k in WEIGHTS))


def kernel(x, p, norm_ffn1, ffn1_w_in, ffn1_w_out, norm_mix, norm_ffn2, ffn2_w_in, ffn2_w_out, ple_norm, ple_gate_w, ple_proj_w, hyb_w_in, conv_dw_w, conv_dw_b, conv_ln_g, conv_ln_b, ssm_conv_w, ssm_conv_b, ssm_dt_bias, ssm_a_log, ssm_d, ssm_norm, hyb_w_out, att_w_qkv, att_b_qkv, att_sinks, att_w_o, att_b_o, final_norm, loss_target, m_norm_ffn1, m_ffn1_w_in, m_ffn1_w_out, m_norm_mix, m_norm_ffn2, m_ffn2_w_in, m_ffn2_w_out, m_ple_norm, m_ple_gate_w, m_ple_proj_w, m_hyb_w_in, m_conv_dw_w, m_conv_dw_b, m_conv_ln_g, m_conv_ln_b, m_ssm_conv_w, m_ssm_conv_b, m_ssm_dt_bias, m_ssm_a_log, m_ssm_d, m_ssm_norm, m_hyb_w_out, m_att_w_qkv, m_att_b_qkv, m_att_sinks, m_att_w_o, m_att_b_o, m_final_norm, v_norm_ffn1, v_ffn1_w_in, v_ffn1_w_out, v_norm_mix, v_norm_ffn2, v_ffn2_w_in, v_ffn2_w_out, v_ple_norm, v_ple_gate_w, v_ple_proj_w, v_hyb_w_in, v_conv_dw_w, v_conv_dw_b, v_conv_ln_g, v_conv_ln_b, v_ssm_conv_w, v_ssm_conv_b, v_ssm_dt_bias, v_ssm_a_log, v_ssm_d, v_ssm_norm, v_hyb_w_out, v_att_w_qkv, v_att_b_qkv, v_att_sinks, v_att_w_o, v_att_b_o, v_final_norm):
    given = locals()
    w = {k: given[k] for k in WEIGHTS}
    m = {k: given["m_" + k] for k in WEIGHTS}
    v = {k: given["v_" + k] for k in WEIGHTS}
    return _step(x, p, loss_target, w, m, v)
```

```python
import functools

import numpy as np
import jax
import jax.numpy as jnp
from jax import lax
from jax.experimental import pallas as pl
from jax.experimental.pallas import tpu as pltpu

F32 = jnp.float32
BF16 = jnp.bfloat16
MXU_DTYPE = jnp.bfloat16
S = jax.ShapeDtypeStruct
MESH = pl.DeviceIdType.MESH

VMEM_LIMIT = 48 * 2**20
LANE = 128

EPS = 1e-6
SSM_HEADS = 16
HEAD_DIM = 64
SSM_GROUPS = 2
SSM_STATE = 128
SSM_CONV = 4
CHUNK = 128
CONV_WIDTH = 31
ATT_HEADS = 16
ATT_KV_HEADS = 4
WINDOW = 128
ROPE_THETA = 10000.0
ADAM_LR = 0.001
ADAM_B1 = 0.9
ADAM_B2 = 0.999
ADAM_EPS = 1e-08
ADAM_WD = 0.01
ADAM_STEP = 10

N_CHIPS = 4
N_DEV = 8

NN = ((1,), (0,))
NT = ((1,), (1,))
TN = ((0,), (0,))


def _mm(a, b, dims=NN):
    return lax.dot_general(a.astype(MXU_DTYPE), b.astype(MXU_DTYPE), (dims, ((), ())), preferred_element_type=F32)


def _split3(a):
    hi = a.astype(BF16)
    r = a - hi.astype(F32)
    mid = r.astype(BF16)
    lo = (r - mid.astype(F32)).astype(BF16)
    return hi, mid, lo


def _mm01(a, onehot, dims=NN):
    o = onehot.astype(BF16)
    out = None
    for part in _split3(a):
        t = lax.dot_general(part, o, (dims, ((), ())), preferred_element_type=F32)
        out = t if out is None else out + t
    return out


def _01mm(onehot, a):
    o = onehot.astype(BF16)
    out = None
    for part in _split3(a):
        t = lax.dot_general(o, part, (NN, ((), ())), preferred_element_type=F32)
        out = t if out is None else out + t
    return out


def _sigmoid(x):
    return 0.5 * jnp.tanh(0.5 * x) + 0.5


def _softplus(x):
    return jnp.maximum(x, 0.0) + jnp.log(1.0 + jnp.exp(-jnp.abs(x)))


def _iota(shape, axis):
    return lax.broadcasted_iota(jnp.int32, shape, axis)


def _head_indicator(width, heads, transposed=False):
    per = width // heads
    if transposed:
        return (_iota((heads, width), 1) // per == _iota((heads, width), 0)).astype(F32)
    return (_iota((width, heads), 0) // per == _iota((width, heads), 1)).astype(F32)


def _acc(ref, i, val):
    @pl.when(i == 0)
    def _():
        ref[...] = val

    @pl.when(i > 0)
    def _():
        ref[...] += val


def _rs(tile, width, col=0, shift=0, n=None):
    if shift == 0:
        return pl.BlockSpec((tile, width), lambda i: (i, col))
    if shift < 0:
        return pl.BlockSpec((tile, width), lambda i: (jnp.maximum(i - 1, 0), col))
    return pl.BlockSpec((tile, width), lambda i: (jnp.minimum(i + 1, n - 1), col))


def _ps(shape):
    return pl.BlockSpec(shape, lambda i: (0,) * len(shape))


def _call(body, name, grid, in_specs, out_specs, out_shape, scratch=(), sem=None):
    return pl.pallas_call(
        body, name=name, grid=grid, in_specs=in_specs, out_specs=out_specs, out_shape=out_shape,
        scratch_shapes=list(scratch),
        compiler_params=pltpu.CompilerParams(dimension_semantics=sem, vmem_limit_bytes=VMEM_LIMIT))


def _row_tile(n, target):
    t = min(n, target)
    assert n % t == 0, (n, t)
    return t


def _pick_tile(dim, target):
    if dim <= target:
        return dim
    t = (int(1.4 * target) // LANE) * LANE
    while t >= LANE:
        if dim % t == 0:
            return t
        t -= LANE
    return dim


ANY = pl.BlockSpec(memory_space=pl.ANY)


def _paired(j):
    return (j % 2) * 2 + j // 2


class ColSharded:
    def __init__(self, arr, paired=False):
        self.arr, self.paired = arr, paired
        self.nch, self.rows, self.per = arr.shape
        self.shape = (self.rows, self.nch * self.per)

    def chip(self, j):
        return _paired(j) if self.paired else j


class Slot:
    def __init__(self, buf, kind, per, off, c0=0, paired=False):
        self.buf, self.kind, self.per, self.off, self.c0, self.paired = buf, kind, per, off, c0, paired

    def chip(self, j):
        return _paired(j) if self.paired else j


def matmul(a, b, mode, name, *, out_dtype=F32, scale=None, res=None, bias=None, into=None, tm=1024, tn=1024, tk=1024):
    bshape = b.shape
    if mode == "nn":
        (m, k), (k2, n) = a.shape, bshape
    elif mode == "nt":
        (m, k), (n, k2) = a.shape, bshape
    else:
        (k, m), (k2, n) = a.shape, bshape
    assert k == k2, (a.shape, bshape, mode)
    if mode == "tn":
        tk = 2 * tk
    elif k <= 3 * tk:
        tk = k
    tm, tn, tk = _pick_tile(m, tm), _pick_tile(n, tn), _pick_tile(k, tk)
    if isinstance(b, ColSharded):
        if mode == "nn":
            tn = b.per
        else:
            assert mode == "nt"
            tk = b.per
    if into is not None:
        if into.kind == "c":
            tn = into.per
            assert into.off % tm == 0 and n == N_CHIPS * into.per
        else:
            tm = max(1, min(m, int(1.4 * 1024)) // into.per) * into.per
            assert m % tm == 0 and into.off % into.per == 0 and into.c0 % (tm // into.per) == 0
    nk = k // tk
    dims = {"nn": NN, "nt": NT, "tn": TN}[mode]
    a_spec = (pl.BlockSpec((tk, tm), lambda i, j, kk: (kk, i)) if mode == "tn"
              else pl.BlockSpec((tm, tk), lambda i, j, kk: (i, kk)))
    if isinstance(b, ColSharded):
        bchip = b.chip
        b_spec = (pl.BlockSpec((None, tk, tn), lambda i, j, kk: (bchip(j), kk, 0)) if mode == "nn"
                  else pl.BlockSpec((None, tn, tk), lambda i, j, kk: (bchip(kk), j, 0)))
        b = b.arr
    else:
        b_spec = (pl.BlockSpec((tn, tk), lambda i, j, kk: (j, kk)) if mode == "nt"
                  else pl.BlockSpec((tk, tn), lambda i, j, kk: (kk, j)))
    plain_o = pl.BlockSpec((tm, tn), lambda i, j, kk: (i, j))
    ins, in_specs = [a, b], [a_spec, b_spec]
    if bias is not None:
        ins.append(bias)
        in_specs.append(pl.BlockSpec((1, tn), lambda i, j, kk: (0, j)))
    if res is not None:
        ins.append(res)
        in_specs.append(plain_o)
    aliases = {}
    if into is None:
        o_spec, o_shape = plain_o, S((m, n), out_dtype)
    else:
        aliases = {len(ins): 0}
        ins.append(into.buf)
        in_specs.append(ANY)
        o_shape = S(into.buf.shape, into.buf.dtype)
        if into.kind == "c":
            ob, ochip = into.off // tm, into.chip
            o_spec = pl.BlockSpec((None, tm, tn), lambda i, j, kk: (ochip(j), ob + i, 0))
        else:
            q, ob = tm // into.per, into.off // into.per
            cb = into.c0 // q
            o_spec = pl.BlockSpec((q, into.per, tn), lambda i, j, kk: (cb + i, ob, j))

    def body(*refs):
        a_ref, b_ref = refs[0], refs[1]
        o_ref, acc_ref = refs[-2], refs[-1]
        kk = pl.program_id(2)

        def finish(out):
            if scale is not None:
                out = out * scale
            pos = 2
            if bias is not None:
                out = out + refs[pos][...]
                pos += 1
            if res is not None:
                out = out + refs[pos][...]
            o_ref[...] = out.astype(o_ref.dtype).reshape(o_ref.shape)

        if nk == 1:
            finish(_mm(a_ref[...], b_ref[...], dims))
            return

        @pl.when(kk == 0)
        def _():
            acc_ref[...] = jnp.zeros_like(acc_ref)

        acc_ref[...] += _mm(a_ref[...], b_ref[...], dims)

        @pl.when(kk == nk - 1)
        def _():
            finish(acc_ref[...])

    return pl.pallas_call(
        body, name=name, grid=(m // tm, n // tn, nk), in_specs=in_specs, out_specs=o_spec, out_shape=o_shape,
        scratch_shapes=[pltpu.VMEM((tm, tn), F32)], input_output_aliases=aliases,
        compiler_params=pltpu.CompilerParams(dimension_semantics=("parallel", "parallel", "arbitrary"),
                                             vmem_limit_bytes=VMEM_LIMIT))(*ins)


def rms_matmul(h, g, b, mode, name, bias=None):
    n, d = h.shape
    sharded = isinstance(b, ColSharded)
    n_out = b.shape[1] if mode == "nn" else b.shape[0]
    tm = _row_tile(n, 1024)
    tn = b.per if sharded else _pick_tile(n_out, 1024)
    if sharded:
        assert mode == "nn"
        bchip = b.chip
        b_spec = pl.BlockSpec((None, d, tn), lambda i, j: (bchip(j), 0, 0))
        b = b.arr
    elif mode == "nn":
        b_spec = pl.BlockSpec((d, tn), lambda i, j: (0, j))
    else:
        b_spec = pl.BlockSpec((tn, d), lambda i, j: (j, 0))
    ins = [h, g, b] + ([bias] if bias is not None else [])
    in_specs = [pl.BlockSpec((tm, d), lambda i, j: (i, 0)), pl.BlockSpec((1, d), lambda i, j: (0, 0)), b_spec]
    if bias is not None:
        in_specs.append(pl.BlockSpec((1, tn), lambda i, j: (0, j)))

    def body(h_ref, g_ref, b_ref, *refs):
        o_ref, xn_ref = refs[-2:]
        x = h_ref[...]
        r = lax.rsqrt(jnp.mean(x * x, axis=-1, keepdims=True) + EPS)
        xn = (x * r * g_ref[...]).astype(xn_ref.dtype)

        @pl.when(pl.program_id(1) == 0)
        def _():
            xn_ref[...] = xn

        out = _mm(xn, b_ref[...], NN if mode == "nn" else NT)
        o_ref[...] = out if bias is None else out + refs[0][...]

    return pl.pallas_call(
        body, name=name, grid=(n // tm, n_out // tn), in_specs=in_specs,
        out_specs=[pl.BlockSpec((tm, tn), lambda i, j: (i, j)), pl.BlockSpec((tm, d), lambda i, j: (i, 0))],
        out_shape=[S((n, n_out), F32), S((n, d), MXU_DTYPE)],
        compiler_params=pltpu.CompilerParams(dimension_semantics=("parallel", "arbitrary"), vmem_limit_bytes=VMEM_LIMIT),
    )(*ins)


def _rms_bwd_math(x, g, dy):
    r = lax.rsqrt(jnp.mean(x * x, axis=-1, keepdims=True) + EPS)
    xh = x * r
    dg = jnp.sum(dy * xh, axis=0, keepdims=True)
    dxh = dy * g
    dx = r * (dxh - xh * jnp.mean(dxh * xh, axis=-1, keepdims=True))
    return dx, dg


def nt_rms_bwd(a, b, h, g, dh_in, name, extra=None, colsum=False, b_kd=False):
    n, k = a.shape
    d = h.shape[1]
    tm = _row_tile(n, 1024)
    sharded = isinstance(b, ColSharded)
    tk = b.per if sharded else _pick_tile(k, 1024)
    nk = k // tk
    dims = NN if b_kd else NT
    if sharded:
        bchip = b.chip
        b_spec = pl.BlockSpec((None, d, tk), lambda i, kk: (bchip(kk), 0, 0))
        b = b.arr
    elif b_kd:
        b_spec = pl.BlockSpec((tk, d), lambda i, kk: (kk, 0))
    else:
        b_spec = pl.BlockSpec((d, tk), lambda i, kk: (0, kk))
    row = pl.BlockSpec((tm, d), lambda i, kk: (i, 0))
    vec = pl.BlockSpec((1, d), lambda i, kk: (0, 0))
    ins, in_specs = [a, b, h, g, dh_in], [pl.BlockSpec((tm, tk), lambda i, kk: (i, kk)), b_spec, row, vec, row]
    if extra is not None:
        k2 = extra[0].shape[1]
        ins += list(extra)
        in_specs += [pl.BlockSpec((tm, k2), lambda i, kk: (i, 0)),
                     pl.BlockSpec((k2, d) if b_kd else (d, k2), lambda i, kk: (0, 0))]
    n_in = len(ins)

    def body(*refs):
        a_ref, b_ref, h_ref, g_ref, dh_ref = refs[:5]
        o_ref, dg_ref = refs[n_in], refs[n_in + 1]
        acc_ref = refs[-1]
        i, kk = pl.program_id(0), pl.program_id(1)

        @pl.when(kk == 0)
        def _():
            acc_ref[...] = _mm(refs[5][...], refs[6][...], dims) if extra is not None else jnp.zeros_like(acc_ref)

        acc_ref[...] += _mm(a_ref[...], b_ref[...], dims)

        @pl.when(kk == nk - 1)
        def _():
            dx, dg = _rms_bwd_math(h_ref[...], g_ref[...], acc_ref[...])
            out = dh_ref[...] + dx
            o_ref[...] = out
            _acc(dg_ref, i, dg)
            if colsum:
                _acc(refs[n_in + 2], i, jnp.sum(out, axis=0, keepdims=True))

    n_vec = 2 if colsum else 1
    return pl.pallas_call(
        body, name=name, grid=(n // tm, nk), in_specs=in_specs, out_specs=[row] + [vec] * n_vec,
        out_shape=[S((n, d), F32)] + [S((1, d), F32)] * n_vec, scratch_shapes=[pltpu.VMEM((tm, d), F32)],
        compiler_params=pltpu.CompilerParams(dimension_semantics=("arbitrary", "arbitrary"), vmem_limit_bytes=VMEM_LIMIT),
    )(*ins)


def swiglu_in(h, g, w_in, name):
    n, d = h.shape
    per = w_in.per
    nj = w_in.nch // 2
    tile = _row_tile(n, 1024)

    def body(h_ref, g_ref, wg_ref, wu_ref, xn_ref, u_ref, hm_ref):
        x = h_ref[...]
        r = lax.rsqrt(jnp.mean(x * x, axis=-1, keepdims=True) + EPS)
        xn = (x * r * g_ref[...]).astype(xn_ref.dtype)

        @pl.when(pl.program_id(1) == 0)
        def _():
            xn_ref[...] = xn

        a = _mm(xn, wg_ref[...])
        b = _mm(xn, wu_ref[...])
        u_ref[:, :per] = a.astype(u_ref.dtype)
        u_ref[:, per:] = b.astype(u_ref.dtype)
        hm_ref[...] = (a * _sigmoid(a) * b).astype(hm_ref.dtype)

    return pl.pallas_call(
        body, name=name, grid=(n // tile, nj),
        in_specs=[pl.BlockSpec((tile, d), lambda i, j: (i, 0)), pl.BlockSpec((1, d), lambda i, j: (0, 0)),
                  pl.BlockSpec((None, d, per), lambda i, j: (j, 0, 0)), pl.BlockSpec((None, d, per), lambda i, j: (nj + j, 0, 0))],
        out_specs=[pl.BlockSpec((tile, d), lambda i, j: (i, 0)), pl.BlockSpec((tile, 2 * per), lambda i, j: (i, j)),
                   pl.BlockSpec((tile, per), lambda i, j: (i, j))],
        out_shape=[S((n, d), MXU_DTYPE), S((n, 2 * nj * per), MXU_DTYPE), S((n, nj * per), MXU_DTYPE)],
        compiler_params=pltpu.CompilerParams(dimension_semantics=("parallel", "arbitrary"), vmem_limit_bytes=VMEM_LIMIT),
    )(h, g, w_in.arr, w_in.arr)


def swiglu_out_bwd(dh, w_out, u, after, name):
    n, d = dh.shape
    f = w_out.shape[0]
    per = u.shape[1] // 4
    nj = f // per
    tile = _row_tile(n, 1024)

    def body(dh_ref, w_ref, u_ref, after_ref, du_ref):
        dm = 0.5 * _mm(dh_ref[...], w_ref[...], NT)
        a = u_ref[:, :per].astype(F32)
        b = u_ref[:, per:].astype(F32)
        s = _sigmoid(a)
        du_ref[:, :per] = (dm * b * s * (1.0 + a * (1.0 - s))).astype(du_ref.dtype)
        du_ref[:, per:] = (dm * a * s).astype(du_ref.dtype)

    return pl.pallas_call(
        body, name=name, grid=(n // tile, nj),
        in_specs=[pl.BlockSpec((tile, d), lambda i, j: (i, 0)), pl.BlockSpec((per, d), lambda i, j: (j, 0)),
                  pl.BlockSpec((tile, 2 * per), lambda i, j: (i, j)), ANY],
        out_specs=pl.BlockSpec((tile, 2 * per), lambda i, j: (i, j)),
        out_shape=S(u.shape, MXU_DTYPE),
        compiler_params=pltpu.CompilerParams(dimension_semantics=("parallel", "parallel"), vmem_limit_bytes=VMEM_LIMIT),
    )(dh, w_out, u, after)


def ple_fwd(h, g, w_gate, pp, name):
    n, d = h.shape
    tile = _row_tile(n, 512)

    def body(h_ref, g_ref, w_ref, pp_ref, o_ref, gl_ref, xn_ref):
        x = h_ref[...]
        r = lax.rsqrt(jnp.mean(x * x, axis=-1, keepdims=True) + EPS)
        xn = (x * r * g_ref[...]).astype(xn_ref.dtype)
        xn_ref[...] = xn
        gl = _mm(xn, w_ref[...])
        gl_ref[...] = gl
        o_ref[...] = x + _sigmoid(gl) * pp_ref[...]

    return _call(body, name, (n // tile,), [_rs(tile, d), _ps((1, d)), _ps(w_gate.shape), _rs(tile, d)],
                 [_rs(tile, d)] * 3, [S((n, d), F32), S((n, d), F32), S((n, d), MXU_DTYPE)], sem=("parallel",))(h, g, w_gate, pp)


def ple_bwd(dh, gl, pp, w_gate, h, g, after, name):
    n, d = dh.shape
    tile = _row_tile(n, 512)

    def body(dh_ref, gl_ref, pp_ref, w_ref, h_ref, g_ref, after_ref, o_ref, dpp_ref, dgl_ref, dg_ref):
        i = pl.program_id(0)
        s = _sigmoid(gl_ref[...])
        dh_ = dh_ref[...]
        dpp_ref[...] = (dh_ * s).astype(dpp_ref.dtype)
        dgl = (dh_ * pp_ref[...] * s * (1.0 - s)).astype(dgl_ref.dtype)
        dgl_ref[...] = dgl
        dx, dg = _rms_bwd_math(h_ref[...], g_ref[...], _mm(dgl, w_ref[...], NT))
        o_ref[...] = dh_ + dx
        _acc(dg_ref, i, dg)

    return _call(body, name, (n // tile,),
                 [_rs(tile, d)] * 3 + [_ps(w_gate.shape), _rs(tile, d), _ps((1, d)), ANY],
                 [_rs(tile, d)] * 3 + [_ps((1, d))],
                 [S((n, d), F32), S((n, d), MXU_DTYPE), S((n, d), MXU_DTYPE), S((1, d), F32)],
                 sem=("arbitrary",))(dh, gl, pp, w_gate, h, g, after)


def loss_head(h, g, target, name):
    n, d = h.shape
    tile = _row_tile(n, 512)

    def body(h_ref, g_ref, t_ref, dh_ref, dg_ref, loss_ref):
        i = pl.program_id(0)
        x = h_ref[...]
        gg = g_ref[...]
        r = lax.rsqrt(jnp.mean(x * x, axis=-1, keepdims=True) + EPS)
        err = x * r * gg - t_ref[...]
        part = 0.5 * jnp.sum(jnp.mean(err * err, axis=-1, keepdims=True), axis=0, keepdims=True)
        dx, dg = _rms_bwd_math(x, gg, err * (1.0 / d))
        dh_ref[...] = dx
        _acc(dg_ref, i, dg)
        _acc(loss_ref, i, jnp.broadcast_to(part, (8, LANE)))

    return _call(body, name, (n // tile,), [_rs(tile, d), _ps((1, d)), _rs(tile, d)],
                 [_rs(tile, d), _ps((1, d)), _ps((8, LANE))], [S((n, d), F32), S((1, d), F32), S((8, LANE), F32)],
                 sem=("arbitrary",))(h, g, target)


def _adamw_math(w, g, m, v):
    c1 = np.float32(1.0 - ADAM_B1 ** ADAM_STEP)
    c2 = np.float32(1.0 - ADAM_B2 ** ADAM_STEP)
    mm = ADAM_B1 * m + (1.0 - ADAM_B1) * g
    vv = ADAM_B2 * v + (1.0 - ADAM_B2) * (g * g)
    return -ADAM_LR * ((mm / c1) / (jnp.sqrt(vv / c2) + ADAM_EPS) + ADAM_WD * w), mm, vv


def adamw_layer(w, pack, off, m, v, li, prev, name):
    n, r, c = w.shape

    def body(w_ref, g_ref, m_ref, v_ref, *refs):
        go_ref, d_ref, mo_ref, vo_ref = refs[-4:]
        g = g_ref[...]
        go_ref[...] = g
        d_ref[...], mo_ref[...], vo_ref[...] = _adamw_math(w_ref[...], g, m_ref[...], v_ref[...])

    if r % 8 == 0:
        cap = 2**21 // (4 * c) // 8 * 8
        tile = next(t for t in range(min(cap, r), 7, -8) if r % t == 0 and off % t == 0)
        ob, steps = off // tile, r // tile
        blk = pl.BlockSpec((None, tile, c), lambda i: (li, i, 0))
        g_spec = pl.BlockSpec((tile, c), lambda i: (ob + i, 0))
    else:
        assert off == 0 and pack.shape[0] == r and c % (2 * LANE) == 0
        steps = c // (2 * LANE)
        blk = pl.BlockSpec((None, r, 2 * LANE), lambda i: (li, 0, i))
        g_spec = pl.BlockSpec((r, 2 * LANE), lambda i: (0, i))
    prev = list(prev) if prev is not None else []
    return pl.pallas_call(
        body, name=name, grid=(steps,),
        in_specs=[blk, g_spec, blk, blk] + [ANY] * len(prev),
        out_specs=[blk] * 4, out_shape=[S((n, r, c), F32)] * 4,
        input_output_aliases={4 + j: j for j in range(len(prev))},
        compiler_params=pltpu.CompilerParams(dimension_semantics=("parallel",), vmem_limit_bytes=VMEM_LIMIT),
    )(w, pack, m, v, *prev)


def adamw(w, g, m, v, name):
    r, c = w.shape
    tile = r
    for t in (512, 256, 128, 64, 32, 16, 8):
        if r % t == 0 and t * c * 4 <= 2**21:
            tile = t
            break

    def body(w_ref, g_ref, m_ref, v_ref, d_ref, mo_ref, vo_ref):
        d_ref[...], mo_ref[...], vo_ref[...] = _adamw_math(w_ref[...], g_ref[...], m_ref[...], v_ref[...])

    return _call(body, name, (r // tile,), [_rs(tile, c)] * 4, [_rs(tile, c)] * 3, [S((r, c), F32)] * 3,
                 sem=("parallel",))(w, g, m, v)


TAP_VREGS = 32


def _taps(src, w_ref, offsets, tile, put, bias=None):
    c = src.shape[1]
    rp = max(8, TAP_VREGS * 8 * LANE // c // 8 * 8)
    for r0 in range(0, tile, rp):
        acc = jnp.zeros((rp, c), F32) if bias is None else jnp.zeros((rp, c), F32) + bias
        for k, o in enumerate(offsets):
            acc = acc + w_ref[k:k + 1, :] * src[r0 + o:r0 + o + rp, :]
        put(slice(r0, r0 + rp), acc)


def _taps_fwd(sc, w_ref, width, halo, tile, put, bias):
    _taps(sc, w_ref, [halo - (width - 1) + k for k in range(width)], tile, put, bias)


def _taps_bwd_x(sc_d, w_ref, width, tile, put):
    _taps(sc_d, w_ref, [(width - 1) - k for k in range(width)], tile, put)


def _taps_bwd_w(dy, sc, dw_ref, width, halo, tile, i):
    @pl.when(i == 0)
    def _():
        dw_ref[...] = jnp.zeros_like(dw_ref)

    for k in range(width):
        o = halo - (width - 1) + k
        dw_ref[k:k + 1, :] += jnp.sum(dy * sc[o:o + tile, :], axis=0, keepdims=True)


def _ln_stats(x):
    mu = jnp.mean(x, axis=-1, keepdims=True)
    xc = x - mu
    r = lax.rsqrt(jnp.mean(xc * xc, axis=-1, keepdims=True) + EPS)
    return xc * r, r


def conv_group_fwd(proj, cw, cb, lg, lb, name):
    n = proj.shape[0]
    d = cw.shape[1]
    tile = _row_tile(n, 256)
    halo = 32

    def body(v_ref, g_ref, vp_ref, gp_ref, cw_ref, cb_ref, lg_ref, lb_ref, u_ref, u1_ref, sc):
        i = pl.program_id(0)
        first = (i > 0).astype(F32)
        sc[0:halo, :] = vp_ref[tile - halo:, :] * _sigmoid(gp_ref[tile - halo:, :]) * first
        sc[halo:, :] = v_ref[...] * _sigmoid(g_ref[...])
        def put(rows, acc):
            u1_ref[rows, :] = acc

        _taps_fwd(sc, cw_ref, CONV_WIDTH, halo, tile, put, cb_ref[...])
        xh, _ = _ln_stats(u1_ref[...])
        y = xh * lg_ref[...] + lb_ref[...]
        u_ref[...] = (y * _sigmoid(y)).astype(u_ref.dtype)

    return _call(body, name, (n // tile,),
                 [_rs(tile, d, 0), _rs(tile, d, 1), _rs(tile, d, 0, -1), _rs(tile, d, 1, -1),
                  _ps(cw.shape), _ps((1, d)), _ps((1, d)), _ps((1, d))],
                 [_rs(tile, d), _rs(tile, d)], [S((n, d), MXU_DTYPE), S((n, d), F32)],
                 scratch=[pltpu.VMEM((halo + tile, d), F32)], sem=("arbitrary",))(proj, proj, proj, proj, cw, cb, lg, lb)


def conv_group_bwd(du, u1, proj, cw, lg, lb, after, name):
    n = proj.shape[0]
    d = cw.shape[1]
    tile = _row_tile(n, 256)
    halo = 32
    nt = n // tile

    def body(du_ref, dun_ref, u1_ref, u1n_ref, v_ref, g_ref, vp_ref, gp_ref, cw_ref, lg_ref, lb_ref, after_ref,
             dp_ref, dcw_ref, dcb_ref, dlg_ref, dlb_ref, sc, sc_d):
        i = pl.program_id(0)

        def ln_swish_bwd(dy_, u1_):
            xh, r = _ln_stats(u1_)
            y = xh * lg_ref[...] + lb_ref[...]
            s = _sigmoid(y)
            dyy = dy_ * s * (1.0 + y * (1.0 - s))
            dxh = dyy * lg_ref[...]
            dx = r * (dxh - jnp.mean(dxh, axis=-1, keepdims=True) - xh * jnp.mean(dxh * xh, axis=-1, keepdims=True))
            return dx, jnp.sum(dyy * xh, axis=0, keepdims=True), jnp.sum(dyy, axis=0, keepdims=True)

        du1, dlg, dlb = ln_swish_bwd(du_ref[...].astype(F32), u1_ref[...])
        du1n, _, _ = ln_swish_bwd(dun_ref[0:halo, :].astype(F32), u1n_ref[0:halo, :])
        sc_d[0:tile, :] = du1
        sc_d[tile:, :] = du1n * (i < nt - 1).astype(F32)
        sc[0:halo, :] = vp_ref[tile - halo:, :] * _sigmoid(gp_ref[tile - halo:, :]) * (i > 0).astype(F32)
        sc[halo:, :] = v_ref[...] * _sigmoid(g_ref[...])

        def put(rows, du0):
            sig = _sigmoid(g_ref[rows, :])
            dp_ref[rows, :d] = (du0 * sig).astype(dp_ref.dtype)
            dp_ref[rows, d:] = (du0 * v_ref[rows, :] * sig * (1.0 - sig)).astype(dp_ref.dtype)

        _taps_bwd_x(sc_d, cw_ref, CONV_WIDTH, tile, put)
        _taps_bwd_w(du1, sc, dcw_ref, CONV_WIDTH, halo, tile, i)
        _acc(dcb_ref, i, jnp.sum(du1, axis=0, keepdims=True))
        _acc(dlg_ref, i, dlg)
        _acc(dlb_ref, i, dlb)

    return _call(body, name, (nt,),
                 [_rs(tile, d), _rs(tile, d, 0, 1, nt), _rs(tile, d), _rs(tile, d, 0, 1, nt),
                  _rs(tile, d, 0), _rs(tile, d, 1), _rs(tile, d, 0, -1), _rs(tile, d, 1, -1),
                  _ps(cw.shape), _ps((1, d)), _ps((1, d)), ANY],
                 [_rs(tile, 2 * d), _ps(cw.shape), _ps((1, d)), _ps((1, d)), _ps((1, d))],
                 [S((n, proj.shape[1]), MXU_DTYPE), S(cw.shape, F32), S((1, d), F32), S((1, d), F32), S((1, d), F32)],
                 scratch=[pltpu.VMEM((halo + tile, d), F32), pltpu.VMEM((tile + halo, d), F32)],
                 sem=("arbitrary",))(du, du, u1, u1, proj, proj, proj, proj, cw, lg, lb, after)


def ssm_conv_fwd(proj, dtr, sw, sb, dtb, name):
    n = proj.shape[0]
    w = sw.shape[1]
    inner = SSM_HEADS * HEAD_DIM
    tile = _row_tile(n, 256)
    halo = 8

    def body(x_ref, xp_ref, dtr_ref, sw_ref, sb_ref, dtb_ref, pre_ref, xs_ref, bc_ref, dt_ref, sc):
        i = pl.program_id(0)
        sc[0:halo, :] = xp_ref[tile - halo:, :] * (i > 0).astype(F32)
        sc[halo:, :] = x_ref[...]
        def put(rows, acc):
            pre_ref[rows, :] = acc

        _taps_fwd(sc, sw_ref, SSM_CONV, halo, tile, put, sb_ref[...])
        pre = pre_ref[...]
        act = pre * _sigmoid(pre)
        xs_ref[...] = act[:, :inner]
        bc_ref[...] = act[:, inner:]
        dt = _softplus(dtr_ref[...] + dtb_ref[...])
        dt_ref[...] = jnp.where(_iota(dt.shape, 1) < SSM_HEADS, dt, 0.0)

    return _call(body, name, (n // tile,),
                 [_rs(tile, w, 2), _rs(tile, w, 2, -1), _rs(tile, LANE), _ps(sw.shape), _ps((1, w)), _ps((1, LANE))],
                 [_rs(tile, w), _rs(tile, inner), _rs(tile, w - inner), _rs(tile, LANE)],
                 [S((n, w), F32), S((n, inner), F32), S((n, w - inner), F32), S((n, LANE), F32)],
                 scratch=[pltpu.VMEM((halo + tile, w), F32)], sem=("arbitrary",))(proj, proj, dtr, sw, sb, dtb)


def ssm_conv_bwd(dxs, dbc, pre, proj, sw, dproj, name):
    n = proj.shape[0]
    w = sw.shape[1]
    inner = SSM_HEADS * HEAD_DIM
    tile = _row_tile(n, 256)
    halo = 8
    nt = n // tile

    def body(dxs_ref, dxsn_ref, dbc_ref, dbcn_ref, pre_ref, pren_ref, x_ref, xp_ref, sw_ref, dp_in_ref,
             dx_ref, dsw_ref, dsb_ref, sc, sc_d):
        i = pl.program_id(0)

        def silu_bwd(d_, p_):
            s = _sigmoid(p_)
            return d_ * s * (1.0 + p_ * (1.0 - s))

        sc_d[0:tile, :inner] = silu_bwd(dxs_ref[...], pre_ref[:, :inner])
        sc_d[0:tile, inner:] = silu_bwd(dbc_ref[...], pre_ref[:, inner:])
        last = (i < nt - 1).astype(F32)
        sc_d[tile:, :inner] = silu_bwd(dxsn_ref[0:halo, :], pren_ref[0:halo, :inner]) * last
        sc_d[tile:, inner:] = silu_bwd(dbcn_ref[0:halo, :], pren_ref[0:halo, inner:]) * last
        sc[0:halo, :] = xp_ref[tile - halo:, :] * (i > 0).astype(F32)
        sc[halo:, :] = x_ref[...]
        dpre = sc_d[0:tile, :]
        def put(rows, acc):
            dx_ref[rows, :] = acc.astype(dx_ref.dtype)

        _taps_bwd_x(sc_d, sw_ref, SSM_CONV, tile, put)
        _taps_bwd_w(dpre, sc, dsw_ref, SSM_CONV, halo, tile, i)
        _acc(dsb_ref, i, jnp.sum(dpre, axis=0, keepdims=True))

    return pl.pallas_call(
        body, name=name, grid=(nt,),
        in_specs=[_rs(tile, inner), _rs(tile, inner, 0, 1, nt), _rs(tile, w - inner), _rs(tile, w - inner, 0, 1, nt),
                  _rs(tile, w), _rs(tile, w, 0, 1, nt), _rs(tile, w, 2), _rs(tile, w, 2, -1), _ps(sw.shape), ANY],
        out_specs=[_rs(tile, w, 2), _ps(sw.shape), _ps((1, w))],
        out_shape=[S(dproj.shape, dproj.dtype), S(sw.shape, F32), S((1, w), F32)],
        scratch_shapes=[pltpu.VMEM((halo + tile, w), F32), pltpu.VMEM((tile + halo, w), F32)],
        input_output_aliases={9: 0},
        compiler_params=pltpu.CompilerParams(dimension_semantics=("arbitrary",), vmem_limit_bytes=VMEM_LIMIT),
    )(dxs, dxs, dbc, dbc, pre, pre, proj, proj, sw, dproj)


def _ssd_prologue(dt_ref, dtT_ref, al_ref, alc_ref):
    row = _iota((CHUNK, CHUNK), 0)
    col = _iota((CHUNK, CHUNK), 1)
    dt = dt_ref[:, :SSM_HEADS]
    a_row = -jnp.exp(al_ref[:, :SSM_HEADS])
    a_col = -jnp.exp(alc_ref[...])
    cs = _01mm((row >= col).astype(F32), dt * a_row)
    csT = _mm01(dtT_ref[...] * a_col, (row <= col).astype(F32))
    return dt, a_row, cs, csT, row, col


def _decay(cs, csT, h, row, col):
    lm = jnp.exp(jnp.where(row >= col, cs[:, h:h + 1] - csT[h:h + 1, :], -1e30))
    lmT = jnp.exp(jnp.where(col >= row, csT[h:h + 1, :] - cs[:, h:h + 1], -1e30))
    return lm, lmT


def ssd_fwd(xs, bc, dt, dtT, alog_row, alog_col, name):
    n, width = xs.shape
    nc = n // CHUNK
    gw = width // SSM_GROUPS
    hpg = SSM_HEADS // SSM_GROUPS
    ns = SSM_STATE

    def body(xs_ref, bc_ref, dt_ref, dtT_ref, al_ref, alc_ref, y_ref, hs_ref, h_sc):
        i = pl.program_id(0)

        @pl.when(i == 0)
        def _():
            h_sc[...] = jnp.zeros_like(h_sc)

        dt, a_row, cs, csT, row, col = _ssd_prologue(dt_ref, dtT_ref, al_ref, alc_ref)
        indT = _head_indicator(width, SSM_HEADS, transposed=True)
        dt_full = _mm01(dt, indT)
        e_full = jnp.exp(_mm01(cs, indT))
        dte_full = jnp.exp(_mm01(cs[CHUNK - 1:CHUNK, :] - cs, indT))
        xt = xs_ref[...] * dt_full
        hs_ref[0] = h_sc[...]
        lo = _iota((CHUNK, 2 * HEAD_DIM), 1) < HEAD_DIM
        groups = range(SSM_GROUPS)
        bgs = [bc_ref[:, g * ns:(g + 1) * ns] for g in groups]
        cgs = [bc_ref[:, (SSM_GROUPS + g) * ns:(SSM_GROUPS + g + 1) * ns] for g in groups]
        gms = [_mm(cg, bg, NT) for cg, bg in zip(cgs, bgs)]
        yoffs = [e_full[:, g * gw:(g + 1) * gw] * _mm(cgs[g], h_sc[g * gw:(g + 1) * gw, :], NT) for g in groups]
        sgs = [_mm(xt[:, g * gw:(g + 1) * gw] * dte_full[:, g * gw:(g + 1) * gw], bgs[g], TN) for g in groups]
        ms = [gms[h // hpg] * _decay(cs, csT, h, row, col)[0] for h in range(SSM_HEADS)]
        for pr in range(SSM_HEADS // 2):
            c0 = 2 * pr * HEAD_DIM
            xp = xt[:, c0:c0 + 2 * HEAD_DIM]
            yd = jnp.where(lo, _mm(ms[2 * pr], xp), _mm(ms[2 * pr + 1], xp))
            y_ref[:, c0:c0 + 2 * HEAD_DIM] = yd + yoffs[2 * pr // hpg][:, c0 % gw:c0 % gw + 2 * HEAD_DIM]
        for h in range(SSM_HEADS):
            r0 = h * HEAD_DIM
            h_sc[r0:r0 + HEAD_DIM, :] = (h_sc[r0:r0 + HEAD_DIM, :] * jnp.exp(csT[h:h + 1, CHUNK - 1:CHUNK])
                                         + sgs[h // hpg][r0 % gw:r0 % gw + HEAD_DIM, :])

    bcw = bc.shape[1]
    return _call(body, name, (nc,),
                 [_rs(CHUNK, width), _rs(CHUNK, bcw), _rs(CHUNK, LANE), pl.BlockSpec((SSM_HEADS, CHUNK), lambda i: (0, i)),
                  _ps((1, LANE)), _ps((SSM_HEADS, 1))],
                 [_rs(CHUNK, width), pl.BlockSpec((1, width, ns), lambda i: (i, 0, 0))],
                 [S((n, width), F32), S((nc, width, ns), F32)],
                 scratch=[pltpu.VMEM((width, ns), F32)], sem=("arbitrary",))(xs, bc, dt, dtT, alog_row, alog_col)


def ssd_bwd(xs, bc, dt, dtT, alog_row, alog_col, hs, dy, dxs_skip, name):
    n, width = xs.shape
    nc = n // CHUNK
    gw = width // SSM_GROUPS
    hpg = SSM_HEADS // SSM_GROUPS
    ns = SSM_STATE
    bcw = bc.shape[1]

    def body(xs_ref, bc_ref, dt_ref, dtT_ref, al_ref, alc_ref, hs_ref, dy_ref, skip_ref,
             dxs_ref, dbc_ref, ddtr_ref, dal_ref, ddtb_ref, dh_sc, dxt_sc):
        i = pl.program_id(0)

        @pl.when(i == 0)
        def _():
            dh_sc[...] = jnp.zeros_like(dh_sc)

        dt, a_row, cs, csT, row, col = _ssd_prologue(dt_ref, dtT_ref, al_ref, alc_ref)
        indT = _head_indicator(width, SSM_HEADS, transposed=True)
        ind = _head_indicator(width, SSM_HEADS)
        dt_full = _mm01(dt, indT)
        e_full = jnp.exp(_mm01(cs, indT))
        cs_last = cs[CHUNK - 1:CHUNK, :]
        dte = jnp.exp(cs_last - cs)
        dte_full = _mm01(dte, indT)
        xs_ = xs_ref[...]
        xt = xs_ * dt_full
        dy_ = dy_ref[...]
        hmat = hs_ref[0]
        ds = dh_sc[...]
        lo = _iota((CHUNK, 2 * HEAD_DIM), 1) < HEAD_DIM
        head_lane = _iota((1, SSM_HEADS), 1)
        dcs = jnp.zeros((CHUNK, SSM_HEADS), F32)
        ddte = jnp.zeros((CHUNK, SSM_HEADS), F32)
        for g in range(SSM_GROUPS):
            sl = slice(g * gw, (g + 1) * gw)
            bg = bc_ref[:, g * ns:(g + 1) * ns]
            cg = bc_ref[:, (SSM_GROUPS + g) * ns:(SSM_GROUPS + g + 1) * ns]
            gm = _mm(cg, bg, NT)
            gmT = _mm(bg, cg, NT)
            hg = hmat[sl, :]
            dsg = ds[sl, :]
            dyg = dy_[:, sl]
            xtg = xt[:, sl]
            yoff = e_full[:, sl] * _mm(cg, hg, NT)
            edy = e_full[:, sl] * dyg
            bds = _mm(bg, dsg, NT)
            dxt_g = dte_full[:, sl] * bds
            ddte = ddte + _mm01(xtg * bds, ind[sl, :])
            dcs = dcs + _mm01(dyg * yoff, ind[sl, :])
            db = _mm(xtg * dte_full[:, sl], dsg)
            dc = _mm(edy, hg)
            dhc = _mm(edy, cg, TN)
            hs_g = range(g * hpg, (g + 1) * hpg)
            pair = lambda a, h: a[:, (h - g * hpg) // 2 * 2 * HEAD_DIM:((h - g * hpg) // 2 + 1) * 2 * HEAD_DIM]
            decs = [_decay(cs, csT, h, row, col) for h in hs_g]
            xms = [jnp.where(lo if h % 2 == 0 else jnp.logical_not(lo), pair(xtg, h), 0.0) for h in hs_g]
            dms = [_mm(pair(dyg, h), xm, NT) for h, xm in zip(hs_g, xms)]
            dmTs = [_mm(xm, pair(dyg, h), NT) for h, xm in zip(hs_g, xms)]
            mTs = [gmT * lmT for _, lmT in decs]
            for h, (lm, _), dm, dmT, mT in zip(hs_g, decs, dms, dmTs, mTs):
                z = jnp.sum(dm * (gm * lm), axis=1, keepdims=True) - jnp.sum(dmT * mT, axis=1, keepdims=True)
                dcs = dcs + z * (head_lane == h).astype(F32)
            dgs = sum(dm * lm for dm, (lm, _) in zip(dms, decs))
            dgTs = sum(dmT * lmT for dmT, (_, lmT) in zip(dmTs, decs))
            rrs = [_mm(mT, pair(dyg, h)) for h, mT in zip(hs_g, mTs)]
            for pr in range(hpg // 2):
                c0 = 2 * pr * HEAD_DIM
                dxt_sc[:, g * gw + c0:g * gw + c0 + 2 * HEAD_DIM] = (jnp.where(lo, rrs[2 * pr], rrs[2 * pr + 1])
                                                                     + dxt_g[:, c0:c0 + 2 * HEAD_DIM])
            dbc_ref[:, g * ns:(g + 1) * ns] = db + _mm(dgTs, cg)
            dbc_ref[:, (SSM_GROUPS + g) * ns:(SSM_GROUPS + g + 1) * ns] = dc + _mm(dgs, bg)
            for hh in range(hpg):
                h = g * hpg + hh
                r0 = h * HEAD_DIM
                dh_sc[r0:r0 + HEAD_DIM, :] = (dhc[hh * HEAD_DIM:(hh + 1) * HEAD_DIM, :]
                                              + jnp.exp(csT[h:h + 1, CHUNK - 1:CHUNK]) * ds[r0:r0 + HEAD_DIM, :])
        t = ddte * dte
        per_head = jnp.sum(jnp.sum(ds * hmat, axis=1, keepdims=True) * ind, axis=0, keepdims=True)
        last_add = jnp.sum(t, axis=0, keepdims=True) + jnp.exp(cs_last) * per_head
        dcs = dcs - t + jnp.where(_iota((CHUNK, SSM_HEADS), 0) == CHUNK - 1, last_add, 0.0)
        dadt = _01mm((row <= col).astype(F32), dcs)
        dxt = dxt_sc[...]
        ddt = dadt * a_row + _mm01(dxt * xs_, ind)
        dxs_ref[...] = dxt * dt_full + skip_ref[...]
        ddtr = ddt * (1.0 - jnp.exp(-dt))
        ddtr_ref[...] = jnp.zeros_like(ddtr_ref)
        ddtr_ref[:, :SSM_HEADS] = ddtr.astype(ddtr_ref.dtype)
        _acc(dal_ref, i, jnp.sum(dadt * dt, axis=0, keepdims=True) * a_row)
        _acc(ddtb_ref, i, jnp.sum(ddtr, axis=0, keepdims=True))

    rev = lambda i: (nc - 1 - i, 0)
    return _call(body, name, (nc,),
                 [pl.BlockSpec((CHUNK, width), rev), pl.BlockSpec((CHUNK, bcw), rev), pl.BlockSpec((CHUNK, LANE), rev),
                  pl.BlockSpec((SSM_HEADS, CHUNK), lambda i: (0, nc - 1 - i)), _ps((1, LANE)), _ps((SSM_HEADS, 1)),
                  pl.BlockSpec((1, width, ns), lambda i: (nc - 1 - i, 0, 0)), pl.BlockSpec((CHUNK, width), rev),
                  pl.BlockSpec((CHUNK, width), rev)],
                 [pl.BlockSpec((CHUNK, width), rev), pl.BlockSpec((CHUNK, bcw), rev), pl.BlockSpec((CHUNK, LANE), rev),
                  _ps((1, SSM_HEADS)), _ps((1, SSM_HEADS))],
                 [S((n, width), F32), S((n, bcw), F32), S((n, LANE), MXU_DTYPE), S((1, SSM_HEADS), F32), S((1, SSM_HEADS), F32)],
                 scratch=[pltpu.VMEM((width, ns), F32), pltpu.VMEM((CHUNK, width), F32)],
                 sem=("arbitrary",))(xs, bc, dt, dtT, alog_row, alog_col, hs, dy, dxs_skip)


def ssm_gate_fwd(yssd, xs, proj, dfull, gamma, name):
    n, d = yssd.shape
    tile = _row_tile(n, 512)
    gw = d // SSM_GROUPS

    def body(y_ref, xs_ref, z_ref, df_ref, gm_ref, o_ref):
        z = z_ref[...]
        y2 = (y_ref[...] + df_ref[...] * xs_ref[...]) * (z * _sigmoid(z))
        for g in range(SSM_GROUPS):
            yg = y2[:, g * gw:(g + 1) * gw]
            r = lax.rsqrt(jnp.mean(yg * yg, axis=-1, keepdims=True) + EPS)
            o_ref[:, g * gw:(g + 1) * gw] = (yg * r * gm_ref[:, g * gw:(g + 1) * gw]).astype(o_ref.dtype)

    return _call(body, name, (n // tile,), [_rs(tile, d), _rs(tile, d), _rs(tile, d, 2), _ps((1, d)), _ps((1, d))],
                 _rs(tile, d), S((n, d), MXU_DTYPE), sem=("parallel",))(yssd, xs, proj, dfull, gamma)


def ssm_gate_bwd(dy3, yssd, xs, proj, dfull, gamma, dproj, name):
    n, d = yssd.shape
    tile = _row_tile(n, 512)
    gw = d // SSM_GROUPS

    def body(dy_ref, y_ref, xs_ref, z_ref, df_ref, gm_ref, dp_in_ref, dys_ref, dxs_ref, dz_ref, dgm_ref, dd_ref):
        i = pl.program_id(0)
        z = z_ref[...]
        s = _sigmoid(z)
        xs_ = xs_ref[...]
        y1 = y_ref[...] + df_ref[...] * xs_
        y2 = y1 * (z * s)
        dy_ = dy_ref[...].astype(F32)
        dgm = []
        dy2 = []
        for g in range(SSM_GROUPS):
            sl = slice(g * gw, (g + 1) * gw)
            dxg, dgg = _rms_bwd_math(y2[:, sl], gm_ref[:, sl], dy_[:, sl])
            dy2.append(dxg)
            dgm.append(dgg)
        dy2 = jnp.concatenate(dy2, axis=1)
        dy1 = dy2 * (z * s)
        dys_ref[...] = dy1
        dxs_ref[...] = dy1 * df_ref[...]
        dz_ref[...] = (dy2 * y1 * s * (1.0 + z * (1.0 - s))).astype(dz_ref.dtype)
        _acc(dgm_ref, i, jnp.concatenate(dgm, axis=1))
        colsum = jnp.broadcast_to(jnp.sum(dy1 * xs_, axis=0, keepdims=True), (8, d))
        _acc(dd_ref, i, _mm01(colsum, _head_indicator(d, SSM_HEADS))[0:1, :])

    return pl.pallas_call(
        body, name=name, grid=(n // tile,),
        in_specs=[_rs(tile, d), _rs(tile, d), _rs(tile, d), _rs(tile, d, 2), _ps((1, d)), _ps((1, d)), ANY],
        out_specs=[_rs(tile, d), _rs(tile, d), _rs(tile, d, 2), _ps((1, d)), _ps((1, SSM_HEADS))],
        out_shape=[S((n, d), F32), S((n, d), F32), S(dproj.shape, dproj.dtype), S((1, d), F32), S((1, SSM_HEADS), F32)],
        input_output_aliases={6: 2},
        compiler_params=pltpu.CompilerParams(dimension_semantics=("arbitrary",), vmem_limit_bytes=VMEM_LIMIT),
    )(dy3, yssd, xs, proj, dfull, gamma, dproj)


def _rope128(x, cos, sin_signed):
    half = HEAD_DIM // 2
    lane = _iota(x.shape, 1)
    partner = jnp.where((lane % HEAD_DIM) < half, pltpu.roll(x, LANE - half, 1), pltpu.roll(x, half, 1))
    return x * cos + partner * sin_signed


def rope_fwd(qkv, cos, sin, name):
    n, w = qkv.shape
    qw = ATT_HEADS * HEAD_DIM
    kw = ATT_KV_HEADS * HEAD_DIM
    tile = _row_tile(n, 512)

    def body(x_ref, c_ref, s_ref, q_ref, k_ref, v_ref):
        c, s = c_ref[...], s_ref[...]
        for j in range(qw // LANE):
            q_ref[:, j * LANE:(j + 1) * LANE] = _rope128(x_ref[:, j * LANE:(j + 1) * LANE], c, s).astype(q_ref.dtype)
        for j in range(kw // LANE):
            k_ref[:, j * LANE:(j + 1) * LANE] = _rope128(x_ref[:, qw + j * LANE:qw + (j + 1) * LANE], c, s).astype(k_ref.dtype)
        v_ref[...] = x_ref[:, qw + kw:].astype(v_ref.dtype)

    return _call(body, name, (n // tile,), [_rs(tile, w), _rs(tile, LANE), _rs(tile, LANE)],
                 [_rs(tile, qw), _rs(tile, kw), _rs(tile, kw)],
                 [S((n, qw), MXU_DTYPE), S((n, kw), MXU_DTYPE), S((n, kw), MXU_DTYPE)], sem=("parallel",))(qkv, cos, sin)


ATT_GROUP = ATT_HEADS // ATT_KV_HEADS


def _attn_mask(i):
    row = _iota((ATT_GROUP * WINDOW, 2 * WINDOW), 0) % WINDOW
    s = _iota((ATT_GROUP * WINDOW, 2 * WINDOW), 1)
    return (s > row) & (s <= row + WINDOW) & ((s >= WINDOW) | (i > 0))


def _stack_heads(ref, j, kh, lo):
    parts = []
    for t in range(ATT_GROUP):
        h = ATT_GROUP * j + t
        blk = ref[:, (h // 2) * LANE:(h // 2 + 1) * LANE]
        blk = jnp.where(lo if h % 2 == 0 else jnp.logical_not(lo), blk, jnp.zeros_like(blk))
        parts.append(blk if h % 2 == kh else pltpu.roll(blk, HEAD_DIM, 1))
    return jnp.concatenate(parts, axis=0)


def _unstack_heads(stacked, j, kh, lo, put):
    for t in range(0, ATT_GROUP, 2):
        h = ATT_GROUP * j + t
        even = stacked[t * WINDOW:(t + 1) * WINDOW, :]
        odd = stacked[(t + 1) * WINDOW:(t + 2) * WINDOW, :]
        even = even if kh == 0 else pltpu.roll(even, HEAD_DIM, 1)
        odd = odd if kh == 1 else pltpu.roll(odd, HEAD_DIM, 1)
        put(h // 2, jnp.where(lo, even, odd))


def _per_head_rows(ref, j):
    return jnp.concatenate([ref[:, ATT_GROUP * j + t:ATT_GROUP * j + t + 1] for t in range(ATT_GROUP)], axis=0)


def _per_head_scalar(ref, j):
    rows = _iota((ATT_GROUP * WINDOW, 1), 0) // WINDOW
    out = jnp.zeros((ATT_GROUP * WINDOW, 1), F32)
    for t in range(ATT_GROUP):
        out = out + jnp.where(rows == t, ref[:, ATT_GROUP * j + t:ATT_GROUP * j + t + 1], 0.0)
    return out


def attn_fwd(q, k, v, sinks, name):
    n, qw = q.shape
    kw = k.shape[1]
    nb = n // WINDOW
    scale = HEAD_DIM ** -0.5

    def body(q_ref, kc_ref, kp_ref, vc_ref, vp_ref, sk_ref, o_ref, lse_ref):
        i = pl.program_id(0)
        valid = _attn_mask(i)
        lo = _iota((WINDOW, LANE), 1) < HEAD_DIM
        k2 = jnp.concatenate([kp_ref[...], kc_ref[...]], axis=0)
        v2 = jnp.concatenate([vp_ref[...], vc_ref[...]], axis=0)
        lane1 = _iota((1, LANE), 1)
        lse = jnp.zeros((WINDOW, LANE), F32)

        def put_o(qb, val):
            o_ref[:, qb * LANE:(qb + 1) * LANE] = val.astype(o_ref.dtype)

        kv = [(j, j // 2, j % 2) for j in range(ATT_KV_HEADS)]
        logits = [jnp.where(valid, _mm(_stack_heads(q_ref, j, kh, lo), k2[:, kb * LANE:(kb + 1) * LANE], NT) * scale, -1e30)
                  for j, kb, kh in kv]
        sks = [_per_head_scalar(sk_ref, j) for j, _, _ in kv]
        ms = [jnp.maximum(jnp.max(l, axis=-1, keepdims=True), sk) for l, sk in zip(logits, sks)]
        es = [jnp.exp(l - m) for l, m in zip(logits, ms)]
        dens = [jnp.sum(e, axis=-1, keepdims=True) + jnp.exp(sk - m) for e, sk, m in zip(es, sks, ms)]
        for (j, kb, kh), e, m, den in zip(kv, es, ms, dens):
            lse4 = m + jnp.log(den)
            for t in range(ATT_GROUP):
                lse = lse + lse4[t * WINDOW:(t + 1) * WINDOW, :] * (lane1 == ATT_GROUP * j + t).astype(F32)
            _unstack_heads(_mm(e * (1.0 / den), v2[:, kb * LANE:(kb + 1) * LANE]), j, kh, lo, put_o)
        lse_ref[...] = lse

    return _call(body, name, (nb,),
                 [_rs(WINDOW, qw), _rs(WINDOW, kw), _rs(WINDOW, kw, 0, -1), _rs(WINDOW, kw), _rs(WINDOW, kw, 0, -1), _ps((1, LANE))],
                 [_rs(WINDOW, qw), _rs(WINDOW, LANE)], [S((n, qw), MXU_DTYPE), S((n, LANE), F32)],
                 sem=("parallel",))(q, k, k, v, v, sinks)


def attn_bwd(q, k, v, o, do, lse, sinks, name):
    n, qw = q.shape
    kw = k.shape[1]
    nb = n // WINDOW
    scale = HEAD_DIM ** -0.5

    def body(q_ref, kc_ref, kp_ref, vc_ref, vp_ref, o_ref, do_ref, lse_ref, sk_ref,
             dq_ref, dka_ref, dkb_ref, dva_ref, dvb_ref, dsk_ref):
        i = pl.program_id(0)
        valid = _attn_mask(i)
        lo = _iota((WINDOW, LANE), 1) < HEAD_DIM
        k2 = jnp.concatenate([kp_ref[...], kc_ref[...]], axis=0)
        v2 = jnp.concatenate([vp_ref[...], vc_ref[...]], axis=0)
        lane1 = _iota((1, LANE), 1)
        do_ = do_ref[...].astype(F32)
        delta = _mm01(do_ * o_ref[...].astype(F32), _head_indicator(qw, ATT_HEADS))
        dk2 = [jnp.zeros((2 * WINDOW, LANE), F32) for _ in range(kw // LANE)]
        dv2 = [jnp.zeros((2 * WINDOW, LANE), F32) for _ in range(kw // LANE)]
        dsk = jnp.zeros((1, LANE), F32)

        def put_dq(qb, val):
            dq_ref[:, qb * LANE:(qb + 1) * LANE] = val

        kv = [(j, j // 2, j % 2) for j in range(ATT_KV_HEADS)]
        q4s = [_stack_heads(q_ref, j, kh, lo) for j, _, kh in kv]
        do4s = [_stack_heads(do_ref, j, kh, lo) for j, _, kh in kv]
        kks = [k2[:, kb * LANE:(kb + 1) * LANE] for _, kb, _ in kv]
        vvs = [v2[:, kb * LANE:(kb + 1) * LANE] for _, kb, _ in kv]
        lses = [_per_head_rows(lse_ref, j) for j, _, _ in kv]
        dls = [jnp.concatenate([delta[:, ATT_GROUP * j + t:ATT_GROUP * j + t + 1] for t in range(ATT_GROUP)], axis=0)
               for j, _, _ in kv]
        ps = [jnp.exp(jnp.where(valid, _mm(q4, kk, NT) * scale, -1e30) - lse4) for q4, kk, lse4 in zip(q4s, kks, lses)]
        dss = [p * (_mm(do4, vv, NT) - dl) * scale for p, do4, vv, dl in zip(ps, do4s, vvs, dls)]
        for (j, kb, kh), q4, do4, kk, lse4, dl, p, ds in zip(kv, q4s, do4s, kks, lses, dls, ps, dss):
            sd = jnp.exp(_per_head_scalar(sk_ref, j) - lse4) * dl
            for t in range(ATT_GROUP):
                dsk = dsk - (jnp.sum(sd[t * WINDOW:(t + 1) * WINDOW, :], axis=0, keepdims=True)
                             * (lane1 == ATT_GROUP * j + t).astype(F32))
            _unstack_heads(_mm(ds, kk), j, kh, lo, put_dq)
            dk2[kb] = dk2[kb] + _mm(ds, q4, TN)
            dv2[kb] = dv2[kb] + _mm(p, do4, TN)
        for kb in range(kw // LANE):
            dkb_ref[:, kb * LANE:(kb + 1) * LANE] = dk2[kb][0:WINDOW, :]
            dka_ref[:, kb * LANE:(kb + 1) * LANE] = dk2[kb][WINDOW:, :]
            dvb_ref[:, kb * LANE:(kb + 1) * LANE] = dv2[kb][0:WINDOW, :]
            dva_ref[:, kb * LANE:(kb + 1) * LANE] = dv2[kb][WINDOW:, :]
        _acc(dsk_ref, i, dsk)

    return _call(body, name, (nb,),
                 [_rs(WINDOW, qw), _rs(WINDOW, kw), _rs(WINDOW, kw, 0, -1), _rs(WINDOW, kw), _rs(WINDOW, kw, 0, -1),
                  _rs(WINDOW, qw), _rs(WINDOW, qw), _rs(WINDOW, LANE), _ps((1, LANE))],
                 [_rs(WINDOW, qw)] + [_rs(WINDOW, kw)] * 4 + [_ps((1, LANE))],
                 [S((n, qw), F32)] + [S((n, kw), F32)] * 4 + [S((1, LANE), F32)],
                 sem=("arbitrary",))(q, k, k, v, v, o, do, lse, sinks)


def attn_grad_merge(dq, dka, dkb, dva, dvb, cos, sin, name):
    n, qw = dq.shape
    kw = dka.shape[1]
    nb = n // WINDOW
    w = qw + 2 * kw

    def body(dq_ref, dka_ref, dkb_ref, dva_ref, dvb_ref, c_ref, s_ref, o_ref, db_ref):
        i = pl.program_id(0)
        c, s = c_ref[...], -s_ref[...]
        nxt = (i < nb - 1).astype(F32)

        @pl.when(i == 0)
        def _():
            db_ref[...] = jnp.zeros_like(db_ref)

        def put(c0, val):
            o_ref[:, c0:c0 + val.shape[1]] = val.astype(o_ref.dtype)
            db_ref[:, c0:c0 + val.shape[1]] += jnp.sum(val, axis=0, keepdims=True)

        for j in range(qw // LANE):
            put(j * LANE, _rope128(dq_ref[:, j * LANE:(j + 1) * LANE], c, s))
        for j in range(kw // LANE):
            sl = slice(j * LANE, (j + 1) * LANE)
            put(qw + j * LANE, _rope128(dka_ref[:, sl] + dkb_ref[:, sl] * nxt, c, s))
        put(qw + kw, dva_ref[...] + dvb_ref[...] * nxt)

    return _call(body, name, (nb,),
                 [_rs(WINDOW, qw), _rs(WINDOW, kw), _rs(WINDOW, kw, 0, 1, nb), _rs(WINDOW, kw), _rs(WINDOW, kw, 0, 1, nb),
                  _rs(WINDOW, LANE), _rs(WINDOW, LANE)],
                 [_rs(WINDOW, w), _ps((1, w))], [S((n, w), MXU_DTYPE), S((1, w), F32)],
                 sem=("arbitrary",))(dq, dka, dkb, dva, dvb, cos, sin)


def _row(v):
    return v.reshape(1, -1)


def _pad_lanes(v, width=LANE):
    return jnp.pad(v.reshape(1, -1), ((0, 0), (0, width - v.size)))


def chip_major_tn(a, a2, x, rows, name):
    t, m = a.shape
    d = x.shape[1]
    padded = -(-rows // 32) * 32
    assert m % 8 == 0 and (N_CHIPS - 1) * rows + padded <= m + a2.shape[1] and N_CHIPS * rows >= m
    tk, tn = _row_tile(t, 512), _pick_tile(d, 512)
    nk = t // tk

    def body(a_ref, a2_ref, x_ref, o_ref, acc_ref):
        k = pl.program_id(1)
        x = x_ref[...]
        top, bottom = _mm(a_ref[...], x, TN), _mm(a2_ref[...], x, TN)

        @pl.when(k == 0)
        def _():
            acc_ref[:m] = top
            acc_ref[m:] = bottom

        @pl.when(k > 0)
        def _():
            acc_ref[:m] += top
            acc_ref[m:] += bottom

        @pl.when(k == nk - 1)
        def _():
            keep = _iota((padded, tn), 0) < rows
            for j in range(N_CHIPS):
                o_ref[j] = jnp.where(keep, acc_ref[j * rows:j * rows + padded, :], 0.0).astype(o_ref.dtype)

    return _call(body, name, (d // tn, nk),
                 [pl.BlockSpec((tk, m), lambda n, k: (k, 0)), pl.BlockSpec((tk, a2.shape[1]), lambda n, k: (k, 0)),
                  pl.BlockSpec((tk, tn), lambda n, k: (k, n))],
                 pl.BlockSpec((N_CHIPS, padded, tn), lambda n, k: (0, 0, n)), S((N_CHIPS, padded, d), MXU_DTYPE),
                 scratch=[pltpu.VMEM((m + a2.shape[1], tn), F32)], sem=("parallel", "arbitrary"))(a, a2, x)


class LayerWeights(dict):
    def __init__(self, small, fetch):
        super().__init__(small)
        self.fetch = fetch

    def need(self, k, after):
        if k not in self:
            self[k] = self.fetch(k, after)
        return self[k]


def ffn_fwd(h, g, w, keys, tag):
    xn, u, hm = swiglu_in(h, _row(g), w.need(keys[0], h), f"{tag}_in")
    return matmul(hm, w.need(keys[1], hm), "nn", f"{tag}_out", scale=0.5, res=h), (h, xn, u, hm)


class GradSink:
    ORDER = ("ffn1_w_out", "ffn2_w_out", "ple_gate_w", "att_w_o", "hyb_w_out", "ffn1_w_in", "ffn2_w_in", "att_w_qkv",
             "ple_proj_w", "hyb_w_in")

    def __init__(self, shard_shapes, bucket_of):
        self.where, rows = {}, {}
        for k in self.ORDER:
            n, r, c = shard_shapes[k]
            for li in range(n):
                layer = li if k in PER_LAYER else 2 * li + (0 if k in EVEN_ONLY else 1)
                rows_b = rows.setdefault(bucket_of(layer, _stage(k)), {})
                key = c if r % 32 == 0 else k
                off = -(-rows_b.get(key, (0, c))[0] // r) * r
                rows_b[key] = (-(-(off + r) // 32) * 32, c)
                self.where[k, li] = (bucket_of(layer, _stage(k)), key, "r" if _shard_axis(k) == 1 else "c", off, r)
        self.bufs = {b: {key: lax.empty((N_CHIPS, r, c), MXU_DTYPE) for key, (r, c) in rows_b.items()}
                     for b, rows_b in rows.items()}

    def mm(self, k, li, a, b, name, scale=None, c0=0, paired=False):
        bucket, key, kind, off, r = self.where[k, li]
        buf = self.bufs[bucket][key]
        slot = Slot(buf, kind, r if kind == "r" else buf.shape[2], off, c0, paired)
        self.bufs[bucket][key] = matmul(a, b, "tn", name, scale=scale, into=slot)

    def put(self, k, li, chip_major):
        b, key = self.where[k, li][:2]
        assert (chip_major.shape, chip_major.dtype) == (self.bufs[b][key].shape, self.bufs[b][key].dtype)
        self.bufs[b][key] = chip_major


def ffn_bwd(dh, g, w_in, w_out, saved, tag, sink, keys, layer, after, colsum=False):
    h, xn, u, hm = saved
    sink.mm(keys[1], layer, hm, dh, f"{tag}_dwout", scale=0.5)
    du = swiglu_out_bwd(dh, w_out, u, after, f"{tag}_dhm")
    sink.mm(keys[0], layer, xn, du, f"{tag}_dwin", paired=True)
    outs = nt_rms_bwd(du, ColSharded(w_in.arr, paired=True), h, _row(g), dh, f"{tag}_dxn", colsum=colsum)
    return (outs[0], outs[1].reshape(-1)) + ((outs[2],) if colsum else ())


def _hyb_params(w):
    d = w["conv_dw_b"].size
    inner = SSM_HEADS * HEAD_DIM
    main = 3 * d + w["ssm_conv_b"].size
    return dict(
        w_main=w["hyb_w_in"][:main], w_dt=jnp.pad(w["hyb_w_in"][main:], ((0, LANE - SSM_HEADS), (0, 0))),
        cw=jnp.pad(w["conv_dw_w"], ((0, 32 - CONV_WIDTH), (0, 0))), cb=_row(w["conv_dw_b"]),
        lg=_row(w["conv_ln_g"]), lb=_row(w["conv_ln_b"]),
        sw=jnp.pad(w["ssm_conv_w"], ((0, 8 - SSM_CONV), (0, 0))), sb=_row(w["ssm_conv_b"]),
        dtb=_pad_lanes(w["ssm_dt_bias"]), al_row=_pad_lanes(w["ssm_a_log"]), al_col=w["ssm_a_log"].reshape(-1, 1),
        dfull=_row(jnp.repeat(w["ssm_d"], HEAD_DIM)), gamma=_row(w["ssm_norm"]), d=d, inner=inner, main=main)


def hyb_fwd(h, w, tag):
    w.need("hyb_w_in", h)
    q = _hyb_params(w)
    proj, xn = rms_matmul(h, _row(w["norm_mix"]), q["w_main"], "nt", f"{tag}_in")
    dtr = matmul(xn, q["w_dt"], "nt", f"{tag}_in_dt")
    u, u1 = conv_group_fwd(proj, q["cw"], q["cb"], q["lg"], q["lb"], f"{tag}_conv")
    pre, xs, bc, dt = ssm_conv_fwd(proj, dtr, q["sw"], q["sb"], q["dtb"], f"{tag}_sconv")
    dtT = dt[:, :SSM_HEADS].T
    yssd, hs = ssd_fwd(xs, bc, dt, dtT, q["al_row"], q["al_col"], f"{tag}_ssd")
    y = ssm_gate_fwd(yssd, xs, proj, q["dfull"], q["gamma"], f"{tag}_gate")
    wo = w.need("hyb_w_out", u)
    h2 = matmul(u, wo[:q["d"]], "nn", f"{tag}_out_a", res=h)
    h2 = matmul(y, wo[q["d"]:], "nn", f"{tag}_out_b", res=h2)
    return h2, (h, xn, proj, u, u1, pre, xs, bc, dt, dtT, yssd, hs, y)


def hyb_bwd(dh, w, saved, tag, sink, layer, after):
    q = _hyb_params(w)
    h, xn, proj, u, u1, pre, xs, bc, dt, dtT, yssd, hs, y = saved
    du = matmul(dh, w["hyb_w_out"][:q["d"]], "nt", f"{tag}_du")
    dy3 = matmul(dh, w["hyb_w_out"][q["d"]:], "nt", f"{tag}_dy")
    sink.mm("hyb_w_out", layer, u, dh, f"{tag}_dwo_a", c0=0)
    sink.mm("hyb_w_out", layer, y, dh, f"{tag}_dwo_b", c0=N_CHIPS // 2)
    dproj, dcw, dcb, dlg, dlb = conv_group_bwd(du, u1, proj, q["cw"], q["lg"], q["lb"], after, f"{tag}_dconv")
    dyssd, dxs_skip, dproj, dgamma, dd = ssm_gate_bwd(dy3, yssd, xs, proj, q["dfull"], q["gamma"], dproj, f"{tag}_dgate")
    dxs, dbc, ddtr, dalog, ddtb = ssd_bwd(xs, bc, dt, dtT, q["al_row"], q["al_col"], hs, dyssd, dxs_skip, f"{tag}_dssd")
    dproj, dsw, dsb = ssm_conv_bwd(dxs, dbc, pre, proj, q["sw"], dproj, f"{tag}_dsconv")
    sink.put("hyb_w_in", layer, chip_major_tn(dproj, ddtr, xn, (q["main"] + SSM_HEADS) // N_CHIPS, f"{tag}_dwin"))
    dh2, dg = nt_rms_bwd(dproj, q["w_main"], h, _row(w["norm_mix"]), dh, f"{tag}_dxn", extra=(ddtr, q["w_dt"]), b_kd=True)
    grads = dict(norm_mix=dg.reshape(-1), conv_dw_w=dcw[:CONV_WIDTH], conv_dw_b=dcb.reshape(-1),
                 conv_ln_g=dlg.reshape(-1), conv_ln_b=dlb.reshape(-1), ssm_conv_w=dsw[:SSM_CONV], ssm_conv_b=dsb.reshape(-1),
                 ssm_dt_bias=ddtb.reshape(-1), ssm_a_log=dalog.reshape(-1), ssm_d=dd.reshape(-1), ssm_norm=dgamma.reshape(-1))
    return dh2, grads


def rope_tables(n):
    half = HEAD_DIM // 2
    inv = ROPE_THETA ** (-jnp.arange(0, HEAD_DIM, 2, dtype=F32) / HEAD_DIM)
    ang = jnp.arange(n, dtype=F32)[:, None] * inv[None, :]
    cos, sin = jnp.cos(ang), jnp.sin(ang)
    reps = LANE // HEAD_DIM
    return jnp.tile(jnp.concatenate([cos, cos], axis=1), (1, reps)), jnp.tile(jnp.concatenate([-sin, sin], axis=1), (1, reps))


def att_fwd(h, w, tables, tag):
    cos, sin = tables
    qkv, xn = rms_matmul(h, _row(w["norm_mix"]), w.need("att_w_qkv", h), "nn", f"{tag}_qkv", bias=_row(w["att_b_qkv"]))
    q, k, v = rope_fwd(qkv, cos, sin, f"{tag}_rope")
    sinks = _pad_lanes(w["att_sinks"])
    o, lse = attn_fwd(q, k, v, sinks, f"{tag}_attn")
    h2 = matmul(o, w.need("att_w_o", o), "nn", f"{tag}_o", bias=_row(w["att_b_o"]), res=h)
    return h2, (h, xn, q, k, v, o, lse, sinks)


def att_bwd(dh, dh_colsum, w, saved, tables, tag, sink, layer):
    cos, sin = tables
    h, xn, q, k, v, o, lse, sinks = saved
    do = matmul(dh, w["att_w_o"], "nt", f"{tag}_do")
    sink.mm("att_w_o", layer, o, dh, f"{tag}_dwo")
    dq, dka, dkb, dva, dvb, dsk = attn_bwd(q, k, v, o, do, lse, sinks, f"{tag}_dattn")
    dqkv, dbqkv = attn_grad_merge(dq, dka, dkb, dva, dvb, cos, sin, f"{tag}_drope")
    sink.mm("att_w_qkv", layer, xn, dqkv, f"{tag}_dwqkv")
    dh2, dg = nt_rms_bwd(dqkv, w["att_w_qkv"], h, _row(w["norm_mix"]), dh, f"{tag}_dxn")
    grads = dict(norm_mix=dg.reshape(-1), att_b_qkv=dbqkv.reshape(-1), att_sinks=dsk[0, :ATT_HEADS],
                 att_b_o=dh_colsum.reshape(-1))
    return dh2, grads


def ple_block_fwd(h, pe, w, tag):
    pp = matmul(pe, w.need("ple_proj_w", h), "nn", f"{tag}_proj")
    out, gl, xn = ple_fwd(h, _row(w["ple_norm"]), w.need("ple_gate_w", h), pp, f"{tag}_gate")
    return out, (h, xn, gl, pp, pe)


def ple_block_bwd(dh, w, saved, tag, sink, layer, after):
    h, xn, gl, pp, pe = saved
    dh2, dpp, dgl, dg = ple_bwd(dh, gl, pp, w["ple_gate_w"], h, _row(w["ple_norm"]), after, f"{tag}_dgate")
    sink.mm("ple_proj_w", layer, pe, dpp, f"{tag}_dwp")
    sink.mm("ple_gate_w", layer, xn, dgl, f"{tag}_dwg")
    return dh2, dict(ple_norm=dg.reshape(-1))


PER_LAYER = ("norm_ffn1", "ffn1_w_in", "ffn1_w_out", "norm_mix", "norm_ffn2", "ffn2_w_in", "ffn2_w_out",
             "ple_norm", "ple_gate_w", "ple_proj_w")
EVEN_ONLY = ("hyb_w_in", "conv_dw_w", "conv_dw_b", "conv_ln_g", "conv_ln_b", "ssm_conv_w", "ssm_conv_b",
             "ssm_dt_bias", "ssm_a_log", "ssm_d", "ssm_norm", "hyb_w_out")
ODD_ONLY = ("att_w_qkv", "att_b_qkv", "att_sinks", "att_w_o", "att_b_o")


def _layer_index(k, i):
    if k in PER_LAYER:
        return i
    if k in (EVEN_ONLY if i % 2 == 0 else ODD_ONLY):
        return i // 2
    return None


def _stage(k):
    return 0 if k.startswith("ffn1") else (2 if k.startswith(("ffn2", "ple")) else 1)


def trunk_fwd_bwd(x, pe, target, layers, final_norm, sink, stage_done):
    depth = len(layers)
    tables = rope_tables(x.shape[0])
    h = x
    saved = []
    for i, w in enumerate(layers):
        h, s1 = ffn_fwd(h, w["norm_ffn1"], w, ("ffn1_w_in", "ffn1_w_out"), f"l{i}_ffn1")
        if i % 2 == 0:
            h, s2 = hyb_fwd(h, w, f"l{i}_hyb")
        else:
            h, s2 = att_fwd(h, w, tables, f"l{i}_att")
        h, s3 = ffn_fwd(h, w["norm_ffn2"], w, ("ffn2_w_in", "ffn2_w_out"), f"l{i}_ffn2")
        h, s4 = ple_block_fwd(h, pe[i], w, f"l{i}_ple")
        saved.append((s1, s2, s3, s4))
    dh, dgf, loss = loss_head(h, _row(final_norm), target, "loss_head")
    grads = {}
    tie = dgf
    for i in reversed(range(depth)):
        w = layers[i]
        s1, s2, s3, s4 = saved[i]
        dh, g = ple_block_bwd(dh, w, s4, f"l{i}_ple", sink, i, tie)
        odd = i % 2 == 1
        out = ffn_bwd(dh, w["norm_ffn2"], w["ffn2_w_in"], w["ffn2_w_out"], s3, f"l{i}_ffn2", sink,
                      ("ffn2_w_in", "ffn2_w_out"), i, tie, colsum=odd)
        dh = out[0]
        g.update(norm_ffn2=out[1])
        tie = stage_done(i, 2, tie)
        if odd:
            dh, gm = att_bwd(dh, out[2], w, s2, tables, f"l{i}_att", sink, i // 2)
        else:
            dh, gm = hyb_bwd(dh, w, s2, f"l{i}_hyb", sink, i // 2, tie)
        g.update(gm)
        tie = stage_done(i, 1, tie)
        out = ffn_bwd(dh, w["norm_ffn1"], w["ffn1_w_in"], w["ffn1_w_out"], s1, f"l{i}_ffn1", sink,
                      ("ffn1_w_in", "ffn1_w_out"), i, tie)
        dh = out[0]
        g.update(norm_ffn1=out[1])
        tie = stage_done(i, 0, tie)
        for k, v in g.items():
            grads.setdefault(k, []).insert(0, v)
    grads = {k: jnp.stack(v) for k, v in grads.items()}
    grads["final_norm"] = dgf.reshape(-1)
    return loss, dh, grads


def _me():
    return lax.axis_index("x"), lax.axis_index("y"), lax.axis_index("c")


def _flip(v, f):
    return 1 - v if f else v


def _remote(src, dst, send_sems, recv_sems, k, dev):
    return pltpu.make_async_remote_copy(src_ref=src, dst_ref=dst, send_sem=send_sems.at[k], recv_sem=recv_sems.at[k],
                                        device_id=dev, device_id_type=MESH)


CHIP_FLIPS = ((1, 0), (0, 1), (1, 1))
DEV_FLIPS = tuple((fx, fy, fc) for fx in (0, 1) for fy in (0, 1) for fc in (0, 1))[1:]


HBM = pl.BlockSpec(memory_space=pltpu.HBM)
SEM = pl.BlockSpec(memory_space=pltpu.SEMAPHORE)
DATAFLOW = pltpu.SideEffectType.DATAFLOW_SIDE_EFFECTING


def _core_half(ref, axis, c):
    if axis is None:
        return ref
    h = ref.shape[axis] // 2
    return ref.at[pl.ds(c * h, h), :] if axis == 0 else ref.at[:, pl.ds(c * h, h)]


def gather_start(xs, lands, halves, after, name):
    na = len(xs)

    def body(*refs):
        x_refs, land_refs = refs[:na], refs[na:2 * na]
        send_sems, recv_sems = refs[2 * na + 1], refs[2 * na + 2]
        token = refs[-1]
        mx, my, mc = _me()
        chip = 2 * mx + my
        for a in range(na):
            for j, (fx, fy) in enumerate(CHIP_FLIPS):
                _remote(_core_half(x_refs[a], halves[a], mc), _core_half(land_refs[a].at[chip], halves[a], mc),
                        send_sems, recv_sems, 3 * a + j, (_flip(mx, fx), _flip(my, fy), mc)).start()
        token[...] = jnp.zeros_like(token)

    outs = pl.pallas_call(
        body, name=name,
        out_shape=(pltpu.SemaphoreType.DMA((3 * na,)), pltpu.SemaphoreType.DMA((3 * na,)))
        + tuple(pltpu.HBM(x.shape, x.dtype) for x in xs) + tuple(pltpu.HBM(l.shape, l.dtype) for l in lands)
        + (S((8, LANE), F32),),
        in_specs=[HBM] * (2 * na) + [pl.BlockSpec(memory_space=pl.ANY)],
        out_specs=(SEM, SEM) + (HBM,) * (2 * na) + (pl.BlockSpec(memory_space=pltpu.VMEM),),
        input_output_aliases={a: 2 + a for a in range(2 * na)},
        compiler_params=pltpu.CompilerParams(has_side_effects=DATAFLOW),
    )(*[pltpu.with_memory_space_constraint(t, pltpu.HBM) for t in list(xs) + list(lands)], after)
    return outs[0], outs[1], list(outs[2:2 + na]), list(outs[2 + na:2 + 2 * na])


def gather_wait(send_sems, recv_sems, xs, lands, halves, first, after, name):
    na = len(xs)

    def body(*refs):
        x_refs, land_refs = refs[:na], refs[na:2 * na]
        send_sems, recv_sems = refs[2 * na], refs[2 * na + 1]
        mx, my, mc = _me()
        for a in range(na):
            for j, (fx, fy) in enumerate(CHIP_FLIPS):
                px, py = _flip(mx, fx), _flip(my, fy)
                cp = _remote(_core_half(x_refs[a], halves[a], mc), _core_half(land_refs[a].at[2 * px + py], halves[a], mc),
                             send_sems, recv_sems, 3 * (first + a) + j, (px, py, mc))
                cp.wait_send()
                cp.wait_recv()

    outs = pl.pallas_call(
        body, name=name,
        out_shape=tuple(pltpu.HBM(x.shape, x.dtype) for x in xs) + tuple(pltpu.HBM(l.shape, l.dtype) for l in lands),
        in_specs=[HBM] * (2 * na) + [SEM, SEM, pl.BlockSpec(memory_space=pl.ANY)], out_specs=(HBM,) * (2 * na),
        input_output_aliases={a: a for a in range(2 * na)},
        compiler_params=pltpu.CompilerParams(has_side_effects=DATAFLOW),
    )(*xs, *lands, send_sems, recv_sems, after)
    return list(outs[na:])


def forward_halves(land, axis, name):
    def body(in_ref, out_ref, send_sems, recv_sems):
        del in_ref
        mx, my, mc = _me()
        sib = (mx, my, 1 - mc)
        slots = [2 * _flip(mx, fx) + _flip(my, fy) for fx, fy in CHIP_FLIPS]
        cps = [_remote(_core_half(out_ref.at[s], axis, mc), _core_half(out_ref.at[s], axis, mc), send_sems, recv_sems, j, sib)
               for j, s in enumerate(slots)]
        for cp in cps:
            cp.start()
        for j, s in enumerate(slots):
            _remote(_core_half(out_ref.at[s], axis, mc), _core_half(out_ref.at[s], axis, 1 - mc), send_sems, recv_sems, j, sib).wait_recv()
        for cp in cps:
            cp.wait_send()

    return pl.pallas_call(
        body, name=name, out_shape=S(land.shape, land.dtype), in_specs=[ANY], out_specs=ANY, input_output_aliases={0: 0},
        scratch_shapes=[pltpu.SemaphoreType.DMA((3,)), pltpu.SemaphoreType.DMA((3,))])(land)


def all_gather_devices(v, name):
    r, l = v.shape

    def body(v_ref, out_ref, send_sems, recv_sems):
        mx, my, mc = _me()
        me = 4 * mx + 2 * my + mc
        peers = [(_flip(mx, fx), _flip(my, fy), _flip(mc, fc)) for fx, fy, fc in DEV_FLIPS]
        sends = [_remote(v_ref, out_ref.at[me], send_sems, recv_sems, j, p) for j, p in enumerate(peers)]
        for cp in sends:
            cp.start()
        for j, (px, py, pc) in enumerate(peers):
            _remote(v_ref, out_ref.at[4 * px + 2 * py + pc], send_sems, recv_sems, j, (px, py, pc)).wait_recv()
        for cp in sends:
            cp.wait_send()

    out = pl.pallas_call(
        body, name=name, out_shape=S((N_DEV, r, l), v.dtype), in_specs=[ANY], out_specs=ANY,
        scratch_shapes=[pltpu.SemaphoreType.DMA((7,)), pltpu.SemaphoreType.DMA((7,))])(v)
    me = 4 * lax.axis_index("x") + 2 * lax.axis_index("y") + lax.axis_index("c")
    return lax.dynamic_update_slice_in_dim(out, v[None], me, axis=0)


def gather_devices_start(v, name):
    me = 4 * lax.axis_index("x") + 2 * lax.axis_index("y") + lax.axis_index("c")
    land = lax.dynamic_update_slice_in_dim(lax.empty((N_DEV,) + v.shape, v.dtype), v[None], me, axis=0)

    def body(v_ref, land_ref, send_sems, recv_sems, v_thru, land_thru, token):
        mx, my, mc = _me()
        for j, (fx, fy, fc) in enumerate(DEV_FLIPS):
            _remote(v_ref, land_ref.at[4 * mx + 2 * my + mc], send_sems, recv_sems, j,
                    (_flip(mx, fx), _flip(my, fy), _flip(mc, fc))).start()
        token[...] = jnp.zeros_like(token)

    outs = pl.pallas_call(
        body, name=name,
        out_shape=(pltpu.SemaphoreType.DMA((7,)), pltpu.SemaphoreType.DMA((7,)), pltpu.HBM(v.shape, v.dtype),
                   pltpu.HBM(land.shape, land.dtype), S((8, LANE), F32)),
        in_specs=[HBM, HBM], out_specs=(SEM, SEM, HBM, HBM, pl.BlockSpec(memory_space=pltpu.VMEM)),
        input_output_aliases={0: 2, 1: 3}, compiler_params=pltpu.CompilerParams(has_side_effects=DATAFLOW),
    )(pltpu.with_memory_space_constraint(v, pltpu.HBM), pltpu.with_memory_space_constraint(land, pltpu.HBM))
    return outs[:4]


def gather_devices_wait(send_sems, recv_sems, v, land, after, name):
    def body(v_ref, land_ref, send_sems, recv_sems, after_ref, v_dead, got_ref):
        mx, my, mc = _me()
        for j, (fx, fy, fc) in enumerate(DEV_FLIPS):
            px, py, pc = _flip(mx, fx), _flip(my, fy), _flip(mc, fc)
            cp = _remote(v_ref, land_ref.at[4 * px + 2 * py + pc], send_sems, recv_sems, j, (px, py, pc))
            cp.wait_send()
            cp.wait_recv()

    return pl.pallas_call(
        body, name=name, out_shape=(pltpu.HBM(v.shape, v.dtype), pltpu.HBM(land.shape, land.dtype)),
        in_specs=[HBM, HBM, SEM, SEM, pl.BlockSpec(memory_space=pl.ANY)], out_specs=(HBM, HBM),
        input_output_aliases={0: 0, 1: 1}, compiler_params=pltpu.CompilerParams(has_side_effects=DATAFLOW),
    )(v, land, send_sems, recv_sems, after)[1]


def sum_devices(g8, name):
    nd, r, l = g8.shape
    tile = r
    for t in (512, 256, 128, 64, 32, 16, 8):
        if r % t == 0:
            tile = t
            break

    def body(g_ref, o_ref):
        acc = g_ref[0]
        for d in range(1, nd):
            acc = acc + g_ref[d]
        o_ref[...] = acc

    return _call(body, name, (r // tile,), [pl.BlockSpec((nd, tile, l), lambda i: (0, i, 0))], _rs(tile, l), S((r, l), F32),
                 sem=("parallel",))(g8)


def exchange_halves(gs, name):
    na = len(gs)
    nch = gs[0].shape[0]

    def body(*refs):
        g_refs, out_refs = refs[:na], refs[na:2 * na]
        send_sems, recv_sems = refs[2 * na:]
        mx, my, mc = _me()
        sib = (mx, my, 1 - mc)
        cps = []
        for a in range(na):
            half = gs[a].shape[1] // 2
            for j in range(nch):
                cps.append(_remote(g_refs[a].at[j, pl.ds((1 - mc) * half, half), :], out_refs[a].at[j],
                                   send_sems, recv_sems, nch * a + j, sib))
        for cp in cps:
            cp.start()
        for cp in cps:
            cp.wait_recv()
        for cp in cps:
            cp.wait_send()

    return pl.pallas_call(
        body, name=name, out_shape=[S((nch, g.shape[1] // 2, g.shape[2]), g.dtype) for g in gs],
        in_specs=[ANY] * na, out_specs=[ANY] * na,
        scratch_shapes=[pltpu.SemaphoreType.DMA((nch * na,)), pltpu.SemaphoreType.DMA((nch * na,))])(*gs)


def exchange_halves_start(gs, name):
    na = len(gs)
    nch = gs[0].shape[0]
    lands = [lax.empty((nch, g.shape[1] // 2, g.shape[2]), g.dtype) for g in gs]

    def body(*refs):
        g_refs, land_refs = refs[:na], refs[na:2 * na]
        send_sems, recv_sems = refs[2 * na], refs[2 * na + 1]
        mx, my, mc = _me()
        for a in range(na):
            half = gs[a].shape[1] // 2
            for j in range(nch):
                _remote(g_refs[a].at[j, pl.ds((1 - mc) * half, half), :], land_refs[a].at[j], send_sems, recv_sems,
                        nch * a + j, (mx, my, 1 - mc)).start()
        refs[-1][...] = jnp.zeros_like(refs[-1])

    outs = pl.pallas_call(
        body, name=name,
        out_shape=(pltpu.SemaphoreType.DMA((nch * na,)), pltpu.SemaphoreType.DMA((nch * na,)))
        + tuple(pltpu.HBM(t.shape, t.dtype) for t in list(gs) + lands) + (S((8, LANE), F32),),
        in_specs=[HBM] * (2 * na), out_specs=(SEM, SEM) + (HBM,) * (2 * na) + (pl.BlockSpec(memory_space=pltpu.VMEM),),
        input_output_aliases={a: 2 + a for a in range(2 * na)},
        compiler_params=pltpu.CompilerParams(has_side_effects=DATAFLOW),
    )(*[pltpu.with_memory_space_constraint(t, pltpu.HBM) for t in list(gs) + lands])
    return outs[0], outs[1], list(outs[2:2 + na]), list(outs[2 + na:2 + 2 * na]), outs[-1]


def exchange_halves_wait(send_sems, recv_sems, gs, lands, after, name):
    na = len(gs)
    nch = gs[0].shape[0]

    def body(*refs):
        g_refs, land_refs = refs[:na], refs[na:2 * na]
        send_sems, recv_sems = refs[2 * na], refs[2 * na + 1]
        mx, my, mc = _me()
        for a in range(na):
            half = gs[a].shape[1] // 2
            for j in range(nch):
                cp = _remote(g_refs[a].at[j, pl.ds((1 - mc) * half, half), :], land_refs[a].at[j], send_sems, recv_sems,
                             nch * a + j, (mx, my, 1 - mc))
                cp.wait_send()
                cp.wait_recv()

    outs = pl.pallas_call(
        body, name=name, out_shape=tuple(pltpu.HBM(t.shape, t.dtype) for t in list(gs) + list(lands)),
        in_specs=[HBM] * (2 * na) + [SEM, SEM, pl.BlockSpec(memory_space=pl.ANY)], out_specs=(HBM,) * (2 * na),
        input_output_aliases={a: a for a in range(2 * na)},
        compiler_params=pltpu.CompilerParams(has_side_effects=DATAFLOW),
    )(*gs, *lands, send_sems, recv_sems, after)
    return list(outs[:na]), list(outs[na:])


def add_halves(g4, got, name):
    nch, r, l = g4.shape
    half = r // 2
    tile = _pick_rows(half)
    nt = half // tile

    def body(g_ref, r_ref, a_ref, own_ref):
        j = pl.program_id(1)
        chip = 2 * lax.axis_index("x") + lax.axis_index("y")
        val = g_ref[0].astype(F32) + r_ref[0].astype(F32)
        a_ref[0] = val.astype(a_ref.dtype)

        @pl.when(j == chip)
        def _():
            own_ref[...] = val

    return pl.pallas_call(
        body, name=name, grid=(nt, nch),
        in_specs=[pl.BlockSpec((1, tile, l), lambda i, j: (j, lax.axis_index("c") * nt + i, 0)),
                  pl.BlockSpec((1, tile, l), lambda i, j: (j, i, 0))],
        out_specs=[pl.BlockSpec((1, tile, l), lambda i, j: (j, i, 0)), pl.BlockSpec((tile, l), lambda i, j: (i, 0))],
        out_shape=[S((nch, half, l), MXU_DTYPE), S((half, l), F32)],
        compiler_params=pltpu.CompilerParams(dimension_semantics=("parallel", "arbitrary"), vmem_limit_bytes=VMEM_LIMIT))(g4, got)


def _pick_rows(r, cap=640):
    return next((t for t in range(cap - cap % 16, 15, -16) if r % t == 0), r)


def add_chips(own, got, name):
    h, l = own.shape
    tile = _pick_rows(h)

    def body(o_ref, g_ref, out_ref):
        out_ref[...] = ((o_ref[...] + g_ref[0].astype(F32)) + g_ref[1].astype(F32)) + g_ref[2].astype(F32)

    nt = h // tile
    return _call(body, name, (nt,), [_rs(tile, l), pl.BlockSpec((3, tile, l), lambda i: (0, i, 0))],
                 pl.BlockSpec((tile, l), lambda i: (lax.axis_index("c") * nt + i, 0)),
                 S((2 * h, l), F32), sem=("parallel",))(own, got)


def join_halves(bufs, name):
    na = len(bufs)

    def body(*refs):
        out_refs = refs[na:2 * na]
        send_sems, recv_sems = refs[2 * na:]
        mx, my, mc = _me()
        sib = (mx, my, 1 - mc)

        def half(a, hc):
            h = bufs[a].shape[0] // 2
            return out_refs[a].at[pl.ds(hc * h, h), :]

        cps = [_remote(half(a, mc), half(a, mc), send_sems, recv_sems, a, sib) for a in range(na)]
        for cp in cps:
            cp.start()
        for a in range(na):
            _remote(half(a, mc), half(a, 1 - mc), send_sems, recv_sems, a, sib).wait_recv()
        for cp in cps:
            cp.wait_send()

    return pl.pallas_call(
        body, name=name, out_shape=[S(b.shape, b.dtype) for b in bufs], in_specs=[ANY] * na, out_specs=[ANY] * na,
        input_output_aliases={a: a for a in range(na)},
        scratch_shapes=[pltpu.SemaphoreType.DMA((na,)), pltpu.SemaphoreType.DMA((na,))])(*bufs)


def exchange_chips_start(parts, name):
    na = len(parts)
    lands = [lax.empty((3,) + p.shape[1:], p.dtype) for p in parts]

    def body(*refs):
        a_refs, land_refs = refs[:na], refs[na:2 * na]
        send_sems, recv_sems = refs[2 * na], refs[2 * na + 1]
        mx, my, mc = _me()
        for j, (fx, fy) in enumerate(CHIP_FLIPS):
            px, py = _flip(mx, fx), _flip(my, fy)
            for a in range(na):
                _remote(a_refs[a].at[2 * px + py], land_refs[a].at[j], send_sems, recv_sems, 3 * a + j, (px, py, mc)).start()
        refs[-1][...] = jnp.zeros_like(refs[-1])

    outs = pl.pallas_call(
        body, name=name,
        out_shape=(pltpu.SemaphoreType.DMA((3 * na,)), pltpu.SemaphoreType.DMA((3 * na,)))
        + tuple(pltpu.HBM(t.shape, t.dtype) for t in list(parts) + lands) + (S((8, LANE), F32),),
        in_specs=[HBM] * (2 * na), out_specs=(SEM, SEM) + (HBM,) * (2 * na) + (pl.BlockSpec(memory_space=pltpu.VMEM),),
        input_output_aliases={a: 2 + a for a in range(2 * na)},
        compiler_params=pltpu.CompilerParams(has_side_effects=DATAFLOW),
    )(*[pltpu.with_memory_space_constraint(t, pltpu.HBM) for t in list(parts) + lands])
    return outs[0], outs[1], list(outs[2:2 + na]), list(outs[2 + na:2 + 2 * na]), outs[-1]


def exchange_chips_wait(send_sems, recv_sems, parts, lands, after, name):
    na = len(parts)

    def body(*refs):
        a_refs, land_refs = refs[:na], refs[na:2 * na]
        send_sems, recv_sems = refs[2 * na], refs[2 * na + 1]
        mx, my, mc = _me()
        for j, (fx, fy) in enumerate(CHIP_FLIPS):
            px, py = _flip(mx, fx), _flip(my, fy)
            for a in range(na):
                cp = _remote(a_refs[a].at[2 * px + py], land_refs[a].at[j], send_sems, recv_sems, 3 * a + j, (px, py, mc))
                cp.wait_send()
                cp.wait_recv()

    outs = pl.pallas_call(
        body, name=name, out_shape=tuple(pltpu.HBM(t.shape, t.dtype) for t in list(parts) + list(lands)),
        in_specs=[HBM] * (2 * na) + [SEM, SEM, pl.BlockSpec(memory_space=pl.ANY)], out_specs=(HBM,) * (2 * na),
        input_output_aliases={a: a for a in range(2 * na)},
        compiler_params=pltpu.CompilerParams(has_side_effects=DATAFLOW),
    )(*parts, *lands, send_sems, recv_sems, after)
    return list(outs[na:])


def reduce_begin(gs, tag):
    got = exchange_halves(gs, f"{tag}_d2d")
    return reduce_continue(gs, got, tag)


def reduce_continue(gs, got, tag):
    sums = [add_halves(g, r, f"{tag}_add1_{i}") for i, (g, r) in enumerate(zip(gs, got))]
    return [own for _, own in sums], exchange_chips_start([a for a, _ in sums], f"{tag}_ici_start")


def reduce_end(state, after, tag):
    owns, (send_sems, recv_sems, parts, lands, _) = state
    got = exchange_chips_wait(send_sems, recv_sems, parts, lands, after, f"{tag}_ici_wait")
    return [add_chips(own, r, f"{tag}_add2_{i}") for i, (own, r) in enumerate(zip(owns, got))]


PACK_L = 1024


def _pack(arrs, dtype, row_mult, lead=None):
    lead_shape = () if lead is None else arrs[0].shape[:lead]
    flat = jnp.concatenate([a.astype(dtype).reshape(lead_shape + (-1,)) for a in arrs], axis=-1)
    n = flat.shape[-1]
    unit = row_mult * PACK_L
    total = -(-n // unit) * unit
    flat = jnp.pad(flat, [(0, 0)] * len(lead_shape) + [(0, total - n)])
    return flat.reshape(lead_shape + (total // PACK_L, PACK_L))


def _unpack(packed, shapes, lead=None):
    lead_shape = () if lead is None else packed.shape[:lead]
    flat = packed.reshape(lead_shape + (-1,))
    out, off = [], 0
    for shp in shapes:
        n = int(np.prod(shp))
        out.append(flat[..., off:off + n].reshape(lead_shape + tuple(shp)))
        off += n
    return out


def _to_full(gathered, axis):
    t = jnp.moveaxis(gathered, 0, axis)
    shp = t.shape
    return t.reshape(shp[:axis] + (shp[axis] * shp[axis + 1],) + shp[axis + 2:])


WEIGHTS = ("norm_ffn1", "ffn1_w_in", "ffn1_w_out", "norm_mix", "norm_ffn2", "ffn2_w_in", "ffn2_w_out", "ple_norm",
           "ple_gate_w", "ple_proj_w", "hyb_w_in", "conv_dw_w", "conv_dw_b", "conv_ln_g", "conv_ln_b", "ssm_conv_w",
           "ssm_conv_b", "ssm_dt_bias", "ssm_a_log", "ssm_d", "ssm_norm", "hyb_w_out", "att_w_qkv", "att_b_qkv",
           "att_sinks", "att_w_o", "att_b_o", "final_norm")
SHARD_AXIS = dict(ffn1_w_in=2, ffn1_w_out=1, ffn2_w_in=2, ffn2_w_out=1, ple_gate_w=1, ple_proj_w=2, hyb_w_in=2,
                  conv_dw_w=2, ssm_conv_w=2, hyb_w_out=1, att_w_qkv=2, att_b_qkv=1, att_w_o=1, att_b_o=1)
BIG = ("ffn1_w_in", "ffn1_w_out", "ffn2_w_in", "ffn2_w_out", "ple_gate_w", "ple_proj_w", "hyb_w_in", "hyb_w_out",
       "att_w_qkv", "att_w_o")
TRANSPOSED = ("hyb_w_in",)
FORWARDED = (0, 2)


def _shard_axis(k):
    return 1 if k in TRANSPOSED else SHARD_AXIS[k]
SMALL_SHARDED = ("conv_dw_w", "ssm_conv_w", "att_b_qkv", "att_b_o")
SMALL = tuple(k for k in WEIGHTS if k not in BIG)


def _step(x, p, target, w, m, v):
    mx, my = lax.axis_index("x"), lax.axis_index("y")
    chip = 2 * mx + my
    w, m, v = ({k: (a.transpose(0, 2, 1) if k in TRANSPOSED else a) for k, a in d.items()} for d in (w, m, v))

    depth = w["norm_ffn1"].shape[0]
    order = sorted([(k, i) for i in range(depth) for k in BIG if _layer_index(k, i) is not None],
                   key=lambda t: (t[1], _stage(t[0])))
    small_g = all_gather_devices(_pack([w[k] for k in SMALL_SHARDED], F32, 8), "gather_small")
    shards = [w[k][_layer_index(k, i)].astype(MXU_DTYPE) for k, i in order]
    lands = [lax.dynamic_update_slice_in_dim(lax.empty((N_CHIPS,) + s.shape, s.dtype), s[None], chip, axis=0) for s in shards]
    halves = [(0 if s.shape[0] % 32 == 0 else 1) if pos in FORWARDED else None for pos, s in enumerate(shards)]
    send_sems, recv_sems, shards, lands = gather_start(shards, lands, halves, small_g, "gather_start")

    def fetch(i, k, after):
        p = order.index((k, i))
        g, = gather_wait(send_sems, recv_sems, [shards[p]], [lands[p]], [halves[p]], p, after, f"gather_wait_l{i}_{k}")
        if halves[p] is not None:
            g = forward_halves(g, halves[p], f"gather_forward_l{i}_{k}")
        if _shard_axis(k) == 2:
            return ColSharded(g)
        return g.reshape(-1, g.shape[-1])

    small_g = small_g[0::2]
    small_full = {k: _to_full(g, SHARD_AXIS[k])
                  for k, g in zip(SMALL_SHARDED, _unpack(small_g, [w[k].shape for k in SMALL_SHARDED], lead=1))}
    layers = [LayerWeights({k: small_full.get(k, w[k])[_layer_index(k, i)] for k in SMALL if _layer_index(k, i) is not None},
                           functools.partial(fetch, i)) for i in range(depth)]

    def bucket_of(layer, stage):
        return (layer, 0) if layer > 0 else (0, min(stage, 1))

    sink = GradSink({k: w[k].shape for k in BIG}, bucket_of)
    begun = {}
    pending = []

    def stage_done(i, stage, tie):
        if stage == 2 and pending:
            b, (send_sems, recv_sems, gs, lands, _) = pending.pop()
            gs, got = exchange_halves_wait(send_sems, recv_sems, gs, lands, tie, f"grads_l{b[0]}_{b[1]}_d2d_wait")
            begun[b] = reduce_continue(gs, got, f"grads_l{b[0]}_{b[1]}")
            return begun[b][1][-1]
        b = bucket_of(i, stage)
        if stage > 0 and bucket_of(i, stage - 1) == b:
            return tie
        gs = list(sink.bufs[b].values())
        if i > 0:
            pending.append((b, exchange_halves_start(gs, f"grads_l{b[0]}_{b[1]}_d2d_start")))
            return pending[-1][1][-1]
        begun[b] = reduce_begin(gs, f"grads_l{b[0]}_{b[1]}")
        return begun[b][1][-1]

    loss, dx, grads = trunk_fwd_bwd(x[0], p[:, 0], target[0], layers, w["final_norm"], sink, stage_done)

    results = {}

    def finish(buckets, after, tag):
        halves = {b: reduce_end(begun[b], after, f"grads_l{b[0]}_{b[1]}") for b in buckets}
        joined = iter(join_halves([h for b in buckets for h in halves[b]], f"grads_join_{tag}"))
        reduced = {b: {c: next(joined) for c in sink.bufs[b]} for b in buckets}
        last = after
        for (k, li), (b, key, _, off, r) in sink.where.items():
            if b in buckets:
                g = reduced[b][key]
                if r % 8:
                    g, off = g[off:off + r], 0
                results[k] = adamw_layer(w[k], g, off, m[k], v[k], li, results.get(k), f"adamw_{k}_{li}")
                last = results[k][1]
        return last

    vec = gather_devices_start(_pack([loss[0:1, 0:1]] + [grads[k] for k in SMALL], F32, 8), "gather_vectors_start")
    order_b = list(begun)
    started_last = begun[order_b[-1]][1][-1]
    done = finish(order_b[-1:], finish(order_b[:-1], started_last, "early") if len(order_b) > 1 else dx, "last")
    g_out = {k: results[k][0] for k in BIG}
    vec = sum_devices(gather_devices_wait(*vec, done, "gather_vectors_wait"), "sum_vectors")
    parts = _unpack(vec, [(1, 1)] + [grads[k].shape for k in SMALL])
    loss_out = parts[0].reshape(())
    for k, g in zip(SMALL, parts[1:]):
        if k in SHARD_AXIS:
            ax = SHARD_AXIS[k]
            g = lax.dynamic_slice_in_dim(g, chip * w[k].shape[ax], w[k].shape[ax], axis=ax)
        g_out[k] = g

    for k in TRANSPOSED:
        results[k] = [a.transpose(0, 2, 1) for a in results[k]]
    g_out.update({k: results[k][0] for k in TRANSPOSED})
    delta, new_m, new_v = ({k: results[k][j] for k in BIG} for j in (1, 2, 3))
    shapes = [w[k].shape for k in SMALL]
    packed = [_pack([src[k] for k in SMALL], F32, 8) for src in (w, g_out, m, v)]
    outs = adamw(*packed, "adamw_small")
    for dst, o in zip((delta, new_m, new_v), outs):
        for k, a in zip(SMALL, _unpack(o, shapes)):
            dst[k] = a
    return ((loss_out, dx[None]) + tuple(g_out[k] for k in WEIGHTS) + tuple(delta[k] for k in WEIGHTS)
            + tuple(new_m[k] for k in WEIGHTS) + tuple(new_v[k] for k in WEIGHTS))


def kernel(x, p, norm_ffn1, ffn1_w_in, ffn1_w_out, norm_mix, norm_ffn2, ffn2_w_in, ffn2_w_out, ple_norm, ple_gate_w, ple_proj_w, hyb_w_in, conv_dw_w, conv_dw_b, conv_ln_g, conv_ln_b, ssm_conv_w, ssm_conv_b, ssm_dt_bias, ssm_a_log, ssm_d, ssm_norm, hyb_w_out, att_w_qkv, att_b_qkv, att_sinks, att_w_o, att_b_o, final_norm, loss_target, m_norm_ffn1, m_ffn1_w_in, m_ffn1_w_out, m_norm_mix, m_norm_ffn2, m_ffn2_w_in, m_ffn2_w_out, m_ple_norm, m_ple_gate_w, m_ple_proj_w, m_hyb_w_in, m_conv_dw_w, m_conv_dw_b, m_conv_ln_g, m_conv_ln_b, m_ssm_conv_w, m_ssm_conv_b, m_ssm_dt_bias, m_ssm_a_log, m_ssm_d, m_ssm_norm, m_hyb_w_out, m_att_w_qkv, m_att_b_qkv, m_att_sinks, m_att_w_o, m_att_b_o, m_final_norm, v_norm_ffn1, v_ffn1_w_in, v_ffn1_w_out, v_norm_mix, v_norm_ffn2, v_ffn2_w_in, v_ffn2_w_out, v_ple_norm, v_ple_gate_w, v_ple_proj_w, v_hyb_w_in, v_conv_dw_w, v_conv_dw_b, v_conv_ln_g, v_conv_ln_b, v_ssm_conv_w, v_ssm_conv_b, v_ssm_dt_bias, v_ssm_a_log, v_ssm_d, v_ssm_norm, v_hyb_w_out, v_att_w_qkv, v_att_b_qkv, v_att_sinks, v_att_w_o, v_att_b_o, v_final_norm):
    given = locals()
    w = {k: given[k] for k in WEIGHTS}
    m = {k: given["m_" + k] for k in WEIGHTS}
    v = {k: given["v_" + k] for k in WEIGHTS}
    return _step(x, p, loss_target, w, m, v)
```

```python
import functools

import numpy as np
import jax
import jax.numpy as jnp
from jax import lax
from jax.experimental import pallas as pl
from jax.experimental.pallas import tpu as pltpu

F32 = jnp.float32
BF16 = jnp.bfloat16
MXU_DTYPE = jnp.bfloat16
S = jax.ShapeDtypeStruct
MESH = pl.DeviceIdType.MESH

VMEM_LIMIT = 48 * 2**20
LANE = 128

EPS = 1e-6
SSM_HEADS = 16
HEAD_DIM = 64
SSM_GROUPS = 2
SSM_STATE = 128
SSM_CONV = 4
CHUNK = 128
CONV_WIDTH = 31
ATT_HEADS = 16
ATT_KV_HEADS = 4
WINDOW = 128
ROPE_THETA = 10000.0
ADAM_LR = 0.001
ADAM_B1 = 0.9
ADAM_B2 = 0.999
ADAM_EPS = 1e-08
ADAM_WD = 0.01
ADAM_STEP = 10

N_CHIPS = 4
N_DEV = 8

NN = ((1,), (0,))
NT = ((1,), (1,))
TN = ((0,), (0,))


def _mm(a, b, dims=NN):
    return lax.dot_general(a.astype(MXU_DTYPE), b.astype(MXU_DTYPE), (dims, ((), ())), preferred_element_type=F32)


def _split3(a):
    hi = a.astype(BF16)
    r = a - hi.astype(F32)
    mid = r.astype(BF16)
    lo = (r - mid.astype(F32)).astype(BF16)
    return hi, mid, lo


def _mm01(a, onehot, dims=NN):
    o = onehot.astype(BF16)
    out = None
    for part in _split3(a):
        t = lax.dot_general(part, o, (dims, ((), ())), preferred_element_type=F32)
        out = t if out is None else out + t
    return out


def _01mm(onehot, a):
    o = onehot.astype(BF16)
    out = None
    for part in _split3(a):
        t = lax.dot_general(o, part, (NN, ((), ())), preferred_element_type=F32)
        out = t if out is None else out + t
    return out


def _sigmoid(x):
    return 0.5 * jnp.tanh(0.5 * x) + 0.5


def _softplus(x):
    return jnp.maximum(x, 0.0) + jnp.log(1.0 + jnp.exp(-jnp.abs(x)))


def _iota(shape, axis):
    return lax.broadcasted_iota(jnp.int32, shape, axis)


def _head_indicator(width, heads, transposed=False):
    per = width // heads
    if transposed:
        return (_iota((heads, width), 1) // per == _iota((heads, width), 0)).astype(F32)
    return (_iota((width, heads), 0) // per == _iota((width, heads), 1)).astype(F32)


def _acc(ref, i, val):
    @pl.when(i == 0)
    def _():
        ref[...] = val

    @pl.when(i > 0)
    def _():
        ref[...] += val


def _rs(tile, width, col=0, shift=0, n=None):
    if shift == 0:
        return pl.BlockSpec((tile, width), lambda i: (i, col))
    if shift < 0:
        return pl.BlockSpec((tile, width), lambda i: (jnp.maximum(i - 1, 0), col))
    return pl.BlockSpec((tile, width), lambda i: (jnp.minimum(i + 1, n - 1), col))


def _ps(shape):
    return pl.BlockSpec(shape, lambda i: (0,) * len(shape))


def _call(body, name, grid, in_specs, out_specs, out_shape, scratch=(), sem=None):
    return pl.pallas_call(
        body, name=name, grid=grid, in_specs=in_specs, out_specs=out_specs, out_shape=out_shape,
        scratch_shapes=list(scratch),
        compiler_params=pltpu.CompilerParams(dimension_semantics=sem, vmem_limit_bytes=VMEM_LIMIT))


def _row_tile(n, target):
    t = min(n, target)
    assert n % t == 0, (n, t)
    return t


def _pick_tile(dim, target):
    if dim <= target:
        return dim
    t = (int(1.4 * target) // LANE) * LANE
    while t >= LANE:
        if dim % t == 0:
            return t
        t -= LANE
    return dim


ANY = pl.BlockSpec(memory_space=pl.ANY)


def _paired(j):
    return (j % 2) * 2 + j // 2


class ColSharded:
    def __init__(self, arr, paired=False):
        self.arr, self.paired = arr, paired
        self.nch, self.rows, self.per = arr.shape
        self.shape = (self.rows, self.nch * self.per)

    def chip(self, j):
        return _paired(j) if self.paired else j


class Slot:
    def __init__(self, buf, kind, per, off, c0=0, paired=False):
        self.buf, self.kind, self.per, self.off, self.c0, self.paired = buf, kind, per, off, c0, paired

    def chip(self, j):
        return _paired(j) if self.paired else j


def matmul(a, b, mode, name, *, out_dtype=F32, scale=None, res=None, bias=None, into=None, tm=1024, tn=1024, tk=1024):
    bshape = b.shape
    if mode == "nn":
        (m, k), (k2, n) = a.shape, bshape
    elif mode == "nt":
        (m, k), (n, k2) = a.shape, bshape
    else:
        (k, m), (k2, n) = a.shape, bshape
    assert k == k2, (a.shape, bshape, mode)
    if mode == "tn":
        tk = 2 * tk
    elif k <= 3 * tk:
        tk = k
    tm, tn, tk = _pick_tile(m, tm), _pick_tile(n, tn), _pick_tile(k, tk)
    if isinstance(b, ColSharded):
        if mode == "nn":
            tn = b.per
        else:
            assert mode == "nt"
            tk = b.per
    if into is not None:
        if into.kind == "c":
            tn = into.per
            assert into.off % tm == 0 and n == N_CHIPS * into.per
        else:
            tm = max(1, min(m, int(1.4 * 1024)) // into.per) * into.per
            assert m % tm == 0 and into.off % into.per == 0 and into.c0 % (tm // into.per) == 0
    nk = k // tk
    dims = {"nn": NN, "nt": NT, "tn": TN}[mode]
    a_spec = (pl.BlockSpec((tk, tm), lambda i, j, kk: (kk, i)) if mode == "tn"
              else pl.BlockSpec((tm, tk), lambda i, j, kk: (i, kk)))
    if isinstance(b, ColSharded):
        bchip = b.chip
        b_spec = (pl.BlockSpec((None, tk, tn), lambda i, j, kk: (bchip(j), kk, 0)) if mode == "nn"
                  else pl.BlockSpec((None, tn, tk), lambda i, j, kk: (bchip(kk), j, 0)))
        b = b.arr
    else:
        b_spec = (pl.BlockSpec((tn, tk), lambda i, j, kk: (j, kk)) if mode == "nt"
                  else pl.BlockSpec((tk, tn), lambda i, j, kk: (kk, j)))
    plain_o = pl.BlockSpec((tm, tn), lambda i, j, kk: (i, j))
    ins, in_specs = [a, b], [a_spec, b_spec]
    if bias is not None:
        ins.append(bias)
        in_specs.append(pl.BlockSpec((1, tn), lambda i, j, kk: (0, j)))
    if res is not None:
        ins.append(res)
        in_specs.append(plain_o)
    aliases = {}
    if into is None:
        o_spec, o_shape = plain_o, S((m, n), out_dtype)
    else:
        aliases = {len(ins): 0}
        ins.append(into.buf)
        in_specs.append(ANY)
        o_shape = S(into.buf.shape, into.buf.dtype)
        if into.kind == "c":
            ob, ochip = into.off // tm, into.chip
            o_spec = pl.BlockSpec((None, tm, tn), lambda i, j, kk: (ochip(j), ob + i, 0))
        else:
            q, ob = tm // into.per, into.off // into.per
            cb = into.c0 // q
            o_spec = pl.BlockSpec((q, into.per, tn), lambda i, j, kk: (cb + i, ob, j))

    def body(*refs):
        a_ref, b_ref = refs[0], refs[1]
        o_ref, acc_ref = refs[-2], refs[-1]
        kk = pl.program_id(2)

        def finish(out):
            if scale is not None:
                out = out * scale
            pos = 2
            if bias is not None:
                out = out + refs[pos][...]
                pos += 1
            if res is not None:
                out = out + refs[pos][...]
            o_ref[...] = out.astype(o_ref.dtype).reshape(o_ref.shape)

        if nk == 1:
            finish(_mm(a_ref[...], b_ref[...], dims))
            return

        @pl.when(kk == 0)
        def _():
            acc_ref[...] = jnp.zeros_like(acc_ref)

        acc_ref[...] += _mm(a_ref[...], b_ref[...], dims)

        @pl.when(kk == nk - 1)
        def _():
            finish(acc_ref[...])

    return pl.pallas_call(
        body, name=name, grid=(m // tm, n // tn, nk), in_specs=in_specs, out_specs=o_spec, out_shape=o_shape,
        scratch_shapes=[pltpu.VMEM((tm, tn), F32)], input_output_aliases=aliases,
        compiler_params=pltpu.CompilerParams(dimension_semantics=("parallel", "parallel", "arbitrary"),
                                             vmem_limit_bytes=VMEM_LIMIT))(*ins)


def rms_matmul(h, g, b, mode, name, bias=None):
    n, d = h.shape
    sharded = isinstance(b, ColSharded)
    n_out = b.shape[1] if mode == "nn" else b.shape[0]
    tm = _row_tile(n, 1024)
    tn = b.per if sharded else _pick_tile(n_out, 1024)
    if sharded:
        assert mode == "nn"
        bchip = b.chip
        b_spec = pl.BlockSpec((None, d, tn), lambda i, j: (bchip(j), 0, 0))
        b = b.arr
    elif mode == "nn":
        b_spec = pl.BlockSpec((d, tn), lambda i, j: (0, j))
    else:
        b_spec = pl.BlockSpec((tn, d), lambda i, j: (j, 0))
    ins = [h, g, b] + ([bias] if bias is not None else [])
    in_specs = [pl.BlockSpec((tm, d), lambda i, j: (i, 0)), pl.BlockSpec((1, d), lambda i, j: (0, 0)), b_spec]
    if bias is not None:
        in_specs.append(pl.BlockSpec((1, tn), lambda i, j: (0, j)))

    def body(h_ref, g_ref, b_ref, *refs):
        o_ref, xn_ref = refs[-2:]
        x = h_ref[...]
        r = lax.rsqrt(jnp.mean(x * x, axis=-1, keepdims=True) + EPS)
        xn = (x * r * g_ref[...]).astype(xn_ref.dtype)

        @pl.when(pl.program_id(1) == 0)
        def _():
            xn_ref[...] = xn

        out = _mm(xn, b_ref[...], NN if mode == "nn" else NT)
        o_ref[...] = out if bias is None else out + refs[0][...]

    return pl.pallas_call(
        body, name=name, grid=(n // tm, n_out // tn), in_specs=in_specs,
        out_specs=[pl.BlockSpec((tm, tn), lambda i, j: (i, j)), pl.BlockSpec((tm, d), lambda i, j: (i, 0))],
        out_shape=[S((n, n_out), F32), S((n, d), MXU_DTYPE)],
        compiler_params=pltpu.CompilerParams(dimension_semantics=("parallel", "arbitrary"), vmem_limit_bytes=VMEM_LIMIT),
    )(*ins)


def _rms_bwd_math(x, g, dy):
    r = lax.rsqrt(jnp.mean(x * x, axis=-1, keepdims=True) + EPS)
    xh = x * r
    dg = jnp.sum(dy * xh, axis=0, keepdims=True)
    dxh = dy * g
    dx = r * (dxh - xh * jnp.mean(dxh * xh, axis=-1, keepdims=True))
    return dx, dg


def nt_rms_bwd(a, b, h, g, dh_in, name, extra=None, colsum=False, b_kd=False):
    n, k = a.shape
    d = h.shape[1]
    tm = _row_tile(n, 1024)
    sharded = isinstance(b, ColSharded)
    tk = b.per if sharded else _pick_tile(k, 1024)
    nk = k // tk
    dims = NN if b_kd else NT
    if sharded:
        bchip = b.chip
        b_spec = pl.BlockSpec((None, d, tk), lambda i, kk: (bchip(kk), 0, 0))
        b = b.arr
    elif b_kd:
        b_spec = pl.BlockSpec((tk, d), lambda i, kk: (kk, 0))
    else:
        b_spec = pl.BlockSpec((d, tk), lambda i, kk: (0, kk))
    row = pl.BlockSpec((tm, d), lambda i, kk: (i, 0))
    vec = pl.BlockSpec((1, d), lambda i, kk: (0, 0))
    ins, in_specs = [a, b, h, g, dh_in], [pl.BlockSpec((tm, tk), lambda i, kk: (i, kk)), b_spec, row, vec, row]
    if extra is not None:
        k2 = extra[0].shape[1]
        ins += list(extra)
        in_specs += [pl.BlockSpec((tm, k2), lambda i, kk: (i, 0)),
                     pl.BlockSpec((k2, d) if b_kd else (d, k2), lambda i, kk: (0, 0))]
    n_in = len(ins)

    def body(*refs):
        a_ref, b_ref, h_ref, g_ref, dh_ref = refs[:5]
        o_ref, dg_ref = refs[n_in], refs[n_in + 1]
        acc_ref = refs[-1]
        i, kk = pl.program_id(0), pl.program_id(1)

        @pl.when(kk == 0)
        def _():
            acc_ref[...] = _mm(refs[5][...], refs[6][...], dims) if extra is not None else jnp.zeros_like(acc_ref)

        acc_ref[...] += _mm(a_ref[...], b_ref[...], dims)

        @pl.when(kk == nk - 1)
        def _():
            dx, dg = _rms_bwd_math(h_ref[...], g_ref[...], acc_ref[...])
            out = dh_ref[...] + dx
            o_ref[...] = out
            _acc(dg_ref, i, dg)
            if colsum:
                _acc(refs[n_in + 2], i, jnp.sum(out, axis=0, keepdims=True))

    n_vec = 2 if colsum else 1
    return pl.pallas_call(
        body, name=name, grid=(n // tm, nk), in_specs=in_specs, out_specs=[row] + [vec] * n_vec,
        out_shape=[S((n, d), F32)] + [S((1, d), F32)] * n_vec, scratch_shapes=[pltpu.VMEM((tm, d), F32)],
        compiler_params=pltpu.CompilerParams(dimension_semantics=("arbitrary", "arbitrary"), vmem_limit_bytes=VMEM_LIMIT),
    )(*ins)


def swiglu_in(h, g, w_in, name):
    n, d = h.shape
    per = w_in.per
    nj = w_in.nch // 2
    tile = _row_tile(n, 1024)

    def body(h_ref, g_ref, wg_ref, wu_ref, xn_ref, u_ref, hm_ref):
        x = h_ref[...]
        r = lax.rsqrt(jnp.mean(x * x, axis=-1, keepdims=True) + EPS)
        xn = (x * r * g_ref[...]).astype(xn_ref.dtype)

        @pl.when(pl.program_id(1) == 0)
        def _():
            xn_ref[...] = xn

        a = _mm(xn, wg_ref[...])
        b = _mm(xn, wu_ref[...])
        u_ref[:, :per] = a.astype(u_ref.dtype)
        u_ref[:, per:] = b.astype(u_ref.dtype)
        hm_ref[...] = (a * _sigmoid(a) * b).astype(hm_ref.dtype)

    return pl.pallas_call(
        body, name=name, grid=(n // tile, nj),
        in_specs=[pl.BlockSpec((tile, d), lambda i, j: (i, 0)), pl.BlockSpec((1, d), lambda i, j: (0, 0)),
                  pl.BlockSpec((None, d, per), lambda i, j: (j, 0, 0)), pl.BlockSpec((None, d, per), lambda i, j: (nj + j, 0, 0))],
        out_specs=[pl.BlockSpec((tile, d), lambda i, j: (i, 0)), pl.BlockSpec((tile, 2 * per), lambda i, j: (i, j)),
                   pl.BlockSpec((tile, per), lambda i, j: (i, j))],
        out_shape=[S((n, d), MXU_DTYPE), S((n, 2 * nj * per), MXU_DTYPE), S((n, nj * per), MXU_DTYPE)],
        compiler_params=pltpu.CompilerParams(dimension_semantics=("parallel", "arbitrary"), vmem_limit_bytes=VMEM_LIMIT),
    )(h, g, w_in.arr, w_in.arr)


def swiglu_out_bwd(dh, w_out, u, after, name):
    n, d = dh.shape
    f = w_out.shape[0]
    per = u.shape[1] // 4
    nj = f // per
    tile = _row_tile(n, 1024)

    def body(dh_ref, w_ref, u_ref, after_ref, du_ref):
        dm = 0.5 * _mm(dh_ref[...], w_ref[...], NT)
        a = u_ref[:, :per].astype(F32)
        b = u_ref[:, per:].astype(F32)
        s = _sigmoid(a)
        du_ref[:, :per] = (dm * b * s * (1.0 + a * (1.0 - s))).astype(du_ref.dtype)
        du_ref[:, per:] = (dm * a * s).astype(du_ref.dtype)

    return pl.pallas_call(
        body, name=name, grid=(n // tile, nj),
        in_specs=[pl.BlockSpec((tile, d), lambda i, j: (i, 0)), pl.BlockSpec((per, d), lambda i, j: (j, 0)),
                  pl.BlockSpec((tile, 2 * per), lambda i, j: (i, j)), ANY],
        out_specs=pl.BlockSpec((tile, 2 * per), lambda i, j: (i, j)),
        out_shape=S(u.shape, MXU_DTYPE),
        compiler_params=pltpu.CompilerParams(dimension_semantics=("parallel", "parallel"), vmem_limit_bytes=VMEM_LIMIT),
    )(dh, w_out, u, after)


def ple_fwd(h, g, w_gate, pp, name):
    n, d = h.shape
    tile = _row_tile(n, 512)

    def body(h_ref, g_ref, w_ref, pp_ref, o_ref, gl_ref, xn_ref):
        x = h_ref[...]
        r = lax.rsqrt(jnp.mean(x * x, axis=-1, keepdims=True) + EPS)
        xn = (x * r * g_ref[...]).astype(xn_ref.dtype)
        xn_ref[...] = xn
        gl = _mm(xn, w_ref[...])
        gl_ref[...] = gl
        o_ref[...] = x + _sigmoid(gl) * pp_ref[...]

    return _call(body, name, (n // tile,), [_rs(tile, d), _ps((1, d)), _ps(w_gate.shape), _rs(tile, d)],
                 [_rs(tile, d)] * 3, [S((n, d), F32), S((n, d), F32), S((n, d), MXU_DTYPE)], sem=("parallel",))(h, g, w_gate, pp)


def ple_bwd(dh, gl, pp, w_gate, h, g, after, name):
    n, d = dh.shape
    tile = _row_tile(n, 512)

    def body(dh_ref, gl_ref, pp_ref, w_ref, h_ref, g_ref, after_ref, o_ref, dpp_ref, dgl_ref, dg_ref):
        i = pl.program_id(0)
        s = _sigmoid(gl_ref[...])
        dh_ = dh_ref[...]
        dpp_ref[...] = (dh_ * s).astype(dpp_ref.dtype)
        dgl = (dh_ * pp_ref[...] * s * (1.0 - s)).astype(dgl_ref.dtype)
        dgl_ref[...] = dgl
        dx, dg = _rms_bwd_math(h_ref[...], g_ref[...], _mm(dgl, w_ref[...], NT))
        o_ref[...] = dh_ + dx
        _acc(dg_ref, i, dg)

    return _call(body, name, (n // tile,),
                 [_rs(tile, d)] * 3 + [_ps(w_gate.shape), _rs(tile, d), _ps((1, d)), ANY],
                 [_rs(tile, d)] * 3 + [_ps((1, d))],
                 [S((n, d), F32), S((n, d), MXU_DTYPE), S((n, d), MXU_DTYPE), S((1, d), F32)],
                 sem=("arbitrary",))(dh, gl, pp, w_gate, h, g, after)


def loss_head(h, g, target, name):
    n, d = h.shape
    tile = _row_tile(n, 512)

    def body(h_ref, g_ref, t_ref, dh_ref, dg_ref, loss_ref):
        i = pl.program_id(0)
        x = h_ref[...]
        gg = g_ref[...]
        r = lax.rsqrt(jnp.mean(x * x, axis=-1, keepdims=True) + EPS)
        err = x * r * gg - t_ref[...]
        part = 0.5 * jnp.sum(jnp.mean(err * err, axis=-1, keepdims=True), axis=0, keepdims=True)
        dx, dg = _rms_bwd_math(x, gg, err * (1.0 / d))
        dh_ref[...] = dx
        _acc(dg_ref, i, dg)
        _acc(loss_ref, i, jnp.broadcast_to(part, (8, LANE)))

    return _call(body, name, (n // tile,), [_rs(tile, d), _ps((1, d)), _rs(tile, d)],
                 [_rs(tile, d), _ps((1, d)), _ps((8, LANE))], [S((n, d), F32), S((1, d), F32), S((8, LANE), F32)],
                 sem=("arbitrary",))(h, g, target)


def _adamw_math(w, g, m, v):
    c1 = np.float32(1.0 - ADAM_B1 ** ADAM_STEP)
    c2 = np.float32(1.0 - ADAM_B2 ** ADAM_STEP)
    mm = ADAM_B1 * m + (1.0 - ADAM_B1) * g
    vv = ADAM_B2 * v + (1.0 - ADAM_B2) * (g * g)
    return -ADAM_LR * ((mm / c1) / (jnp.sqrt(vv / c2) + ADAM_EPS) + ADAM_WD * w), mm, vv


def adamw_layer(w, pack, off, m, v, li, prev, name):
    n, r, c = w.shape

    def body(w_ref, g_ref, m_ref, v_ref, *refs):
        go_ref, d_ref, mo_ref, vo_ref = refs[-4:]
        g = g_ref[...]
        go_ref[...] = g
        d_ref[...], mo_ref[...], vo_ref[...] = _adamw_math(w_ref[...], g, m_ref[...], v_ref[...])

    if r % 8 == 0:
        cap = 2**21 // (4 * c) // 8 * 8
        tile = next(t for t in range(min(cap, r), 7, -8) if r % t == 0 and off % t == 0)
        ob, steps = off // tile, r // tile
        blk = pl.BlockSpec((None, tile, c), lambda i: (li, i, 0))
        g_spec = pl.BlockSpec((tile, c), lambda i: (ob + i, 0))
    else:
        assert off == 0 and pack.shape[0] == r and c % (2 * LANE) == 0
        steps = c // (2 * LANE)
        blk = pl.BlockSpec((None, r, 2 * LANE), lambda i: (li, 0, i))
        g_spec = pl.BlockSpec((r, 2 * LANE), lambda i: (0, i))
    prev = list(prev) if prev is not None else []
    return pl.pallas_call(
        body, name=name, grid=(steps,),
        in_specs=[blk, g_spec, blk, blk] + [ANY] * len(prev),
        out_specs=[blk] * 4, out_shape=[S((n, r, c), F32)] * 4,
        input_output_aliases={4 + j: j for j in range(len(prev))},
        compiler_params=pltpu.CompilerParams(dimension_semantics=("parallel",), vmem_limit_bytes=VMEM_LIMIT),
    )(w, pack, m, v, *prev)


def adamw(w, g, m, v, name):
    r, c = w.shape
    tile = r
    for t in (512, 256, 128, 64, 32, 16, 8):
        if r % t == 0 and t * c * 4 <= 2**21:
            tile = t
            break

    def body(w_ref, g_ref, m_ref, v_ref, d_ref, mo_ref, vo_ref):
        d_ref[...], mo_ref[...], vo_ref[...] = _adamw_math(w_ref[...], g_ref[...], m_ref[...], v_ref[...])

    return _call(body, name, (r // tile,), [_rs(tile, c)] * 4, [_rs(tile, c)] * 3, [S((r, c), F32)] * 3,
                 sem=("parallel",))(w, g, m, v)


TAP_VREGS = 32


def _taps(src, w_ref, offsets, tile, put, bias=None):
    c = src.shape[1]
    rp = max(8, TAP_VREGS * 8 * LANE // c // 8 * 8)
    for r0 in range(0, tile, rp):
        acc = jnp.zeros((rp, c), F32) if bias is None else jnp.zeros((rp, c), F32) + bias
        for k, o in enumerate(offsets):
            acc = acc + w_ref[k:k + 1, :] * src[r0 + o:r0 + o + rp, :]
        put(slice(r0, r0 + rp), acc)


def _taps_fwd(sc, w_ref, width, halo, tile, put, bias):
    _taps(sc, w_ref, [halo - (width - 1) + k for k in range(width)], tile, put, bias)


def _taps_bwd_x(sc_d, w_ref, width, tile, put):
    _taps(sc_d, w_ref, [(width - 1) - k for k in range(width)], tile, put)


def _taps_bwd_w(dy, sc, dw_ref, width, halo, tile, i):
    @pl.when(i == 0)
    def _():
        dw_ref[...] = jnp.zeros_like(dw_ref)

    for k in range(width):
        o = halo - (width - 1) + k
        dw_ref[k:k + 1, :] += jnp.sum(dy * sc[o:o + tile, :], axis=0, keepdims=True)


def _ln_stats(x):
    mu = jnp.mean(x, axis=-1, keepdims=True)
    xc = x - mu
    r = lax.rsqrt(jnp.mean(xc * xc, axis=-1, keepdims=True) + EPS)
    return xc * r, r


def conv_group_fwd(proj, cw, cb, lg, lb, name):
    n = proj.shape[0]
    d = cw.shape[1]
    tile = _row_tile(n, 256)
    halo = 32

    def body(v_ref, g_ref, vp_ref, gp_ref, cw_ref, cb_ref, lg_ref, lb_ref, u_ref, u1_ref, sc):
        i = pl.program_id(0)
        first = (i > 0).astype(F32)
        sc[0:halo, :] = vp_ref[tile - halo:, :] * _sigmoid(gp_ref[tile - halo:, :]) * first
        sc[halo:, :] = v_ref[...] * _sigmoid(g_ref[...])
        def put(rows, acc):
            u1_ref[rows, :] = acc

        _taps_fwd(sc, cw_ref, CONV_WIDTH, halo, tile, put, cb_ref[...])
        xh, _ = _ln_stats(u1_ref[...])
        y = xh * lg_ref[...] + lb_ref[...]
        u_ref[...] = (y * _sigmoid(y)).astype(u_ref.dtype)

    return _call(body, name, (n // tile,),
                 [_rs(tile, d, 0), _rs(tile, d, 1), _rs(tile, d, 0, -1), _rs(tile, d, 1, -1),
                  _ps(cw.shape), _ps((1, d)), _ps((1, d)), _ps((1, d))],
                 [_rs(tile, d), _rs(tile, d)], [S((n, d), MXU_DTYPE), S((n, d), F32)],
                 scratch=[pltpu.VMEM((halo + tile, d), F32)], sem=("arbitrary",))(proj, proj, proj, proj, cw, cb, lg, lb)


def conv_group_bwd(du, u1, proj, cw, lg, lb, after, name):
    n = proj.shape[0]
    d = cw.shape[1]
    tile = _row_tile(n, 256)
    halo = 32
    nt = n // tile

    def body(du_ref, dun_ref, u1_ref, u1n_ref, v_ref, g_ref, vp_ref, gp_ref, cw_ref, lg_ref, lb_ref, after_ref,
             dp_ref, dcw_ref, dcb_ref, dlg_ref, dlb_ref, sc, sc_d):
        i = pl.program_id(0)

        def ln_swish_bwd(dy_, u1_):
            xh, r = _ln_stats(u1_)
            y = xh * lg_ref[...] + lb_ref[...]
            s = _sigmoid(y)
            dyy = dy_ * s * (1.0 + y * (1.0 - s))
            dxh = dyy * lg_ref[...]
            dx = r * (dxh - jnp.mean(dxh, axis=-1, keepdims=True) - xh * jnp.mean(dxh * xh, axis=-1, keepdims=True))
            return dx, jnp.sum(dyy * xh, axis=0, keepdims=True), jnp.sum(dyy, axis=0, keepdims=True)

        du1, dlg, dlb = ln_swish_bwd(du_ref[...].astype(F32), u1_ref[...])
        du1n, _, _ = ln_swish_bwd(dun_ref[0:halo, :].astype(F32), u1n_ref[0:halo, :])
        sc_d[0:tile, :] = du1
        sc_d[tile:, :] = du1n * (i < nt - 1).astype(F32)
        sc[0:halo, :] = vp_ref[tile - halo:, :] * _sigmoid(gp_ref[tile - halo:, :]) * (i > 0).astype(F32)
        sc[halo:, :] = v_ref[...] * _sigmoid(g_ref[...])

        def put(rows, du0):
            sig = _sigmoid(g_ref[rows, :])
            dp_ref[rows, :d] = (du0 * sig).astype(dp_ref.dtype)
            dp_ref[rows, d:] = (du0 * v_ref[rows, :] * sig * (1.0 - sig)).astype(dp_ref.dtype)

        _taps_bwd_x(sc_d, cw_ref, CONV_WIDTH, tile, put)
        _taps_bwd_w(du1, sc, dcw_ref, CONV_WIDTH, halo, tile, i)
        _acc(dcb_ref, i, jnp.sum(du1, axis=0, keepdims=True))
        _acc(dlg_ref, i, dlg)
        _acc(dlb_ref, i, dlb)

    return _call(body, name, (nt,),
                 [_rs(tile, d), _rs(tile, d, 0, 1, nt), _rs(tile, d), _rs(tile, d, 0, 1, nt),
                  _rs(tile, d, 0), _rs(tile, d, 1), _rs(tile, d, 0, -1), _rs(tile, d, 1, -1),
                  _ps(cw.shape), _ps((1, d)), _ps((1, d)), ANY],
                 [_rs(tile, 2 * d), _ps(cw.shape), _ps((1, d)), _ps((1, d)), _ps((1, d))],
                 [S((n, proj.shape[1]), MXU_DTYPE), S(cw.shape, F32), S((1, d), F32), S((1, d), F32), S((1, d), F32)],
                 scratch=[pltpu.VMEM((halo + tile, d), F32), pltpu.VMEM((tile + halo, d), F32)],
                 sem=("arbitrary",))(du, du, u1, u1, proj, proj, proj, proj, cw, lg, lb, after)


def ssm_conv_fwd(proj, dtr, sw, sb, dtb, name):
    n = proj.shape[0]
    w = sw.shape[1]
    inner = SSM_HEADS * HEAD_DIM
    tile = _row_tile(n, 256)
    halo = 8

    def body(x_ref, xp_ref, dtr_ref, sw_ref, sb_ref, dtb_ref, pre_ref, xs_ref, bc_ref, dt_ref, sc):
        i = pl.program_id(0)
        sc[0:halo, :] = xp_ref[tile - halo:, :] * (i > 0).astype(F32)
        sc[halo:, :] = x_ref[...]
        def put(rows, acc):
            pre_ref[rows, :] = acc

        _taps_fwd(sc, sw_ref, SSM_CONV, halo, tile, put, sb_ref[...])
        pre = pre_ref[...]
        act = pre * _sigmoid(pre)
        xs_ref[...] = act[:, :inner]
        bc_ref[...] = act[:, inner:]
        dt = _softplus(dtr_ref[...] + dtb_ref[...])
        dt_ref[...] = jnp.where(_iota(dt.shape, 1) < SSM_HEADS, dt, 0.0)

    return _call(body, name, (n // tile,),
                 [_rs(tile, w, 2), _rs(tile, w, 2, -1), _rs(tile, LANE), _ps(sw.shape), _ps((1, w)), _ps((1, LANE))],
                 [_rs(tile, w), _rs(tile, inner), _rs(tile, w - inner), _rs(tile, LANE)],
                 [S((n, w), F32), S((n, inner), F32), S((n, w - inner), F32), S((n, LANE), F32)],
                 scratch=[pltpu.VMEM((halo + tile, w), F32)], sem=("arbitrary",))(proj, proj, dtr, sw, sb, dtb)


def ssm_conv_bwd(dxs, dbc, pre, proj, sw, dproj, name):
    n = proj.shape[0]
    w = sw.shape[1]
    inner = SSM_HEADS * HEAD_DIM
    tile = _row_tile(n, 256)
    halo = 8
    nt = n // tile

    def body(dxs_ref, dxsn_ref, dbc_ref, dbcn_ref, pre_ref, pren_ref, x_ref, xp_ref, sw_ref, dp_in_ref,
             dx_ref, dsw_ref, dsb_ref, sc, sc_d):
        i = pl.program_id(0)

        def silu_bwd(d_, p_):
            s = _sigmoid(p_)
            return d_ * s * (1.0 + p_ * (1.0 - s))

        sc_d[0:tile, :inner] = silu_bwd(dxs_ref[...], pre_ref[:, :inner])
        sc_d[0:tile, inner:] = silu_bwd(dbc_ref[...], pre_ref[:, inner:])
        last = (i < nt - 1).astype(F32)
        sc_d[tile:, :inner] = silu_bwd(dxsn_ref[0:halo, :], pren_ref[0:halo, :inner]) * last
        sc_d[tile:, inner:] = silu_bwd(dbcn_ref[0:halo, :], pren_ref[0:halo, inner:]) * last
        sc[0:halo, :] = xp_ref[tile - halo:, :] * (i > 0).astype(F32)
        sc[halo:, :] = x_ref[...]
        dpre = sc_d[0:tile, :]
        def put(rows, acc):
            dx_ref[rows, :] = acc.astype(dx_ref.dtype)

        _taps_bwd_x(sc_d, sw_ref, SSM_CONV, tile, put)
        _taps_bwd_w(dpre, sc, dsw_ref, SSM_CONV, halo, tile, i)
        _acc(dsb_ref, i, jnp.sum(dpre, axis=0, keepdims=True))

    return pl.pallas_call(
        body, name=name, grid=(nt,),
        in_specs=[_rs(tile, inner), _rs(tile, inner, 0, 1, nt), _rs(tile, w - inner), _rs(tile, w - inner, 0, 1, nt),
                  _rs(tile, w), _rs(tile, w, 0, 1, nt), _rs(tile, w, 2), _rs(tile, w, 2, -1), _ps(sw.shape), ANY],
        out_specs=[_rs(tile, w, 2), _ps(sw.shape), _ps((1, w))],
        out_shape=[S(dproj.shape, dproj.dtype), S(sw.shape, F32), S((1, w), F32)],
        scratch_shapes=[pltpu.VMEM((halo + tile, w), F32), pltpu.VMEM((tile + halo, w), F32)],
        input_output_aliases={9: 0},
        compiler_params=pltpu.CompilerParams(dimension_semantics=("arbitrary",), vmem_limit_bytes=VMEM_LIMIT),
    )(dxs, dxs, dbc, dbc, pre, pre, proj, proj, sw, dproj)


def _ssd_prologue(dt_ref, dtT_ref, al_ref, alc_ref):
    row = _iota((CHUNK, CHUNK), 0)
    col = _iota((CHUNK, CHUNK), 1)
    dt = dt_ref[:, :SSM_HEADS]
    a_row = -jnp.exp(al_ref[:, :SSM_HEADS])
    a_col = -jnp.exp(alc_ref[...])
    cs = _01mm((row >= col).astype(F32), dt * a_row)
    csT = _mm01(dtT_ref[...] * a_col, (row <= col).astype(F32))
    return dt, a_row, cs, csT, row, col


def _decay(cs, csT, h, row, col):
    lm = jnp.exp(jnp.where(row >= col, cs[:, h:h + 1] - csT[h:h + 1, :], -1e30))
    lmT = jnp.exp(jnp.where(col >= row, csT[h:h + 1, :] - cs[:, h:h + 1], -1e30))
    return lm, lmT


def ssd_fwd(xs, bc, dt, dtT, alog_row, alog_col, name):
    n, width = xs.shape
    nc = n // CHUNK
    gw = width // SSM_GROUPS
    hpg = SSM_HEADS // SSM_GROUPS
    ns = SSM_STATE

    def body(xs_ref, bc_ref, dt_ref, dtT_ref, al_ref, alc_ref, y_ref, hs_ref, h_sc):
        i = pl.program_id(0)

        @pl.when(i == 0)
        def _():
            h_sc[...] = jnp.zeros_like(h_sc)

        dt, a_row, cs, csT, row, col = _ssd_prologue(dt_ref, dtT_ref, al_ref, alc_ref)
        indT = _head_indicator(width, SSM_HEADS, transposed=True)
        dt_full = _mm01(dt, indT)
        e_full = jnp.exp(_mm01(cs, indT))
        dte_full = jnp.exp(_mm01(cs[CHUNK - 1:CHUNK, :] - cs, indT))
        xt = xs_ref[...] * dt_full
        hs_ref[0] = h_sc[...]
        lo = _iota((CHUNK, 2 * HEAD_DIM), 1) < HEAD_DIM
        groups = range(SSM_GROUPS)
        bgs = [bc_ref[:, g * ns:(g + 1) * ns] for g in groups]
        cgs = [bc_ref[:, (SSM_GROUPS + g) * ns:(SSM_GROUPS + g + 1) * ns] for g in groups]
        gms = [_mm(cg, bg, NT) for cg, bg in zip(cgs, bgs)]
        yoffs = [e_full[:, g * gw:(g + 1) * gw] * _mm(cgs[g], h_sc[g * gw:(g + 1) * gw, :], NT) for g in groups]
        sgs = [_mm(xt[:, g * gw:(g + 1) * gw] * dte_full[:, g * gw:(g + 1) * gw], bgs[g], TN) for g in groups]
        ms = [gms[h // hpg] * _decay(cs, csT, h, row, col)[0] for h in range(SSM_HEADS)]
        for pr in range(SSM_HEADS // 2):
            c0 = 2 * pr * HEAD_DIM
            xp = xt[:, c0:c0 + 2 * HEAD_DIM]
            yd = jnp.where(lo, _mm(ms[2 * pr], xp), _mm(ms[2 * pr + 1], xp))
            y_ref[:, c0:c0 + 2 * HEAD_DIM] = yd + yoffs[2 * pr // hpg][:, c0 % gw:c0 % gw + 2 * HEAD_DIM]
        for h in range(SSM_HEADS):
            r0 = h * HEAD_DIM
            h_sc[r0:r0 + HEAD_DIM, :] = (h_sc[r0:r0 + HEAD_DIM, :] * jnp.exp(csT[h:h + 1, CHUNK - 1:CHUNK])
                                         + sgs[h // hpg][r0 % gw:r0 % gw + HEAD_DIM, :])

    bcw = bc.shape[1]
    return _call(body, name, (nc,),
                 [_rs(CHUNK, width), _rs(CHUNK, bcw), _rs(CHUNK, LANE), pl.BlockSpec((SSM_HEADS, CHUNK), lambda i: (0, i)),
                  _ps((1, LANE)), _ps((SSM_HEADS, 1))],
                 [_rs(CHUNK, width), pl.BlockSpec((1, width, ns), lambda i: (i, 0, 0))],
                 [S((n, width), F32), S((nc, width, ns), F32)],
                 scratch=[pltpu.VMEM((width, ns), F32)], sem=("arbitrary",))(xs, bc, dt, dtT, alog_row, alog_col)


def ssd_bwd(xs, bc, dt, dtT, alog_row, alog_col, hs, dy, dxs_skip, name):
    n, width = xs.shape
    nc = n // CHUNK
    gw = width // SSM_GROUPS
    hpg = SSM_HEADS // SSM_GROUPS
    ns = SSM_STATE
    bcw = bc.shape[1]

    def body(xs_ref, bc_ref, dt_ref, dtT_ref, al_ref, alc_ref, hs_ref, dy_ref, skip_ref,
             dxs_ref, dbc_ref, ddtr_ref, dal_ref, ddtb_ref, dh_sc, dxt_sc):
        i = pl.program_id(0)

        @pl.when(i == 0)
        def _():
            dh_sc[...] = jnp.zeros_like(dh_sc)

        dt, a_row, cs, csT, row, col = _ssd_prologue(dt_ref, dtT_ref, al_ref, alc_ref)
        indT = _head_indicator(width, SSM_HEADS, transposed=True)
        ind = _head_indicator(width, SSM_HEADS)
        dt_full = _mm01(dt, indT)
        e_full = jnp.exp(_mm01(cs, indT))
        cs_last = cs[CHUNK - 1:CHUNK, :]
        dte = jnp.exp(cs_last - cs)
        dte_full = _mm01(dte, indT)
        xs_ = xs_ref[...]
        xt = xs_ * dt_full
        dy_ = dy_ref[...]
        hmat = hs_ref[0]
        ds = dh_sc[...]
        lo = _iota((CHUNK, 2 * HEAD_DIM), 1) < HEAD_DIM
        head_lane = _iota((1, SSM_HEADS), 1)
        dcs = jnp.zeros((CHUNK, SSM_HEADS), F32)
        ddte = jnp.zeros((CHUNK, SSM_HEADS), F32)
        for g in range(SSM_GROUPS):
            sl = slice(g * gw, (g + 1) * gw)
            bg = bc_ref[:, g * ns:(g + 1) * ns]
            cg = bc_ref[:, (SSM_GROUPS + g) * ns:(SSM_GROUPS + g + 1) * ns]
            gm = _mm(cg, bg, NT)
            gmT = _mm(bg, cg, NT)
            hg = hmat[sl, :]
            dsg = ds[sl, :]
            dyg = dy_[:, sl]
            xtg = xt[:, sl]
            yoff = e_full[:, sl] * _mm(cg, hg, NT)
            edy = e_full[:, sl] * dyg
            bds = _mm(bg, dsg, NT)
            dxt_g = dte_full[:, sl] * bds
            ddte = ddte + _mm01(xtg * bds, ind[sl, :])
            dcs = dcs + _mm01(dyg * yoff, ind[sl, :])
            db = _mm(xtg * dte_full[:, sl], dsg)
            dc = _mm(edy, hg)
            dhc = _mm(edy, cg, TN)
            hs_g = range(g * hpg, (g + 1) * hpg)
            pair = lambda a, h: a[:, (h - g * hpg) // 2 * 2 * HEAD_DIM:((h - g * hpg) // 2 + 1) * 2 * HEAD_DIM]
            decs = [_decay(cs, csT, h, row, col) for h in hs_g]
            xms = [jnp.where(lo if h % 2 == 0 else jnp.logical_not(lo), pair(xtg, h), 0.0) for h in hs_g]
            dms = [_mm(pair(dyg, h), xm, NT) for h, xm in zip(hs_g, xms)]
            dmTs = [_mm(xm, pair(dyg, h), NT) for h, xm in zip(hs_g, xms)]
            mTs = [gmT * lmT for _, lmT in decs]
            for h, (lm, _), dm, dmT, mT in zip(hs_g, decs, dms, dmTs, mTs):
                z = jnp.sum(dm * (gm * lm), axis=1, keepdims=True) - jnp.sum(dmT * mT, axis=1, keepdims=True)
                dcs = dcs + z * (head_lane == h).astype(F32)
            dgs = sum(dm * lm for dm, (lm, _) in zip(dms, decs))
            dgTs = sum(dmT * lmT for dmT, (_, lmT) in zip(dmTs, decs))
            rrs = [_mm(mT, pair(dyg, h)) for h, mT in zip(hs_g, mTs)]
            for pr in range(hpg // 2):
                c0 = 2 * pr * HEAD_DIM
                dxt_sc[:, g * gw + c0:g * gw + c0 + 2 * HEAD_DIM] = (jnp.where(lo, rrs[2 * pr], rrs[2 * pr + 1])
                                                                     + dxt_g[:, c0:c0 + 2 * HEAD_DIM])
            dbc_ref[:, g * ns:(g + 1) * ns] = db + _mm(dgTs, cg)
            dbc_ref[:, (SSM_GROUPS + g) * ns:(SSM_GROUPS + g + 1) * ns] = dc + _mm(dgs, bg)
            for hh in range(hpg):
                h = g * hpg + hh
                r0 = h * HEAD_DIM
                dh_sc[r0:r0 + HEAD_DIM, :] = (dhc[hh * HEAD_DIM:(hh + 1) * HEAD_DIM, :]
                                              + jnp.exp(csT[h:h + 1, CHUNK - 1:CHUNK]) * ds[r0:r0 + HEAD_DIM, :])
        t = ddte * dte
        per_head = jnp.sum(jnp.sum(ds * hmat, axis=1, keepdims=True) * ind, axis=0, keepdims=True)
        last_add = jnp.sum(t, axis=0, keepdims=True) + jnp.exp(cs_last) * per_head
        dcs = dcs - t + jnp.where(_iota((CHUNK, SSM_HEADS), 0) == CHUNK - 1, last_add, 0.0)
        dadt = _01mm((row <= col).astype(F32), dcs)
        dxt = dxt_sc[...]
        ddt = dadt * a_row + _mm01(dxt * xs_, ind)
        dxs_ref[...] = dxt * dt_full + skip_ref[...]
        ddtr = ddt * (1.0 - jnp.exp(-dt))
        ddtr_ref[...] = jnp.zeros_like(ddtr_ref)
        ddtr_ref[:, :SSM_HEADS] = ddtr.astype(ddtr_ref.dtype)
        _acc(dal_ref, i, jnp.sum(dadt * dt, axis=0, keepdims=True) * a_row)
        _acc(ddtb_ref, i, jnp.sum(ddtr, axis=0, keepdims=True))

    rev = lambda i: (nc - 1 - i, 0)
    return _call(body, name, (nc,),
                 [pl.BlockSpec((CHUNK, width), rev), pl.BlockSpec((CHUNK, bcw), rev), pl.BlockSpec((CHUNK, LANE), rev),
                  pl.BlockSpec((SSM_HEADS, CHUNK), lambda i: (0, nc - 1 - i)), _ps((1, LANE)), _ps((SSM_HEADS, 1)),
                  pl.BlockSpec((1, width, ns), lambda i: (nc - 1 - i, 0, 0)), pl.BlockSpec((CHUNK, width), rev),
                  pl.BlockSpec((CHUNK, width), rev)],
                 [pl.BlockSpec((CHUNK, width), rev), pl.BlockSpec((CHUNK, bcw), rev), pl.BlockSpec((CHUNK, LANE), rev),
                  _ps((1, SSM_HEADS)), _ps((1, SSM_HEADS))],
                 [S((n, width), F32), S((n, bcw), F32), S((n, LANE), MXU_DTYPE), S((1, SSM_HEADS), F32), S((1, SSM_HEADS), F32)],
                 scratch=[pltpu.VMEM((width, ns), F32), pltpu.VMEM((CHUNK, width), F32)],
                 sem=("arbitrary",))(xs, bc, dt, dtT, alog_row, alog_col, hs, dy, dxs_skip)


def ssm_gate_fwd(yssd, xs, proj, dfull, gamma, name):
    n, d = yssd.shape
    tile = _row_tile(n, 512)
    gw = d // SSM_GROUPS

    def body(y_ref, xs_ref, z_ref, df_ref, gm_ref, o_ref):
        z = z_ref[...]
        y2 = (y_ref[...] + df_ref[...] * xs_ref[...]) * (z * _sigmoid(z))
        for g in range(SSM_GROUPS):
            yg = y2[:, g * gw:(g + 1) * gw]
            r = lax.rsqrt(jnp.mean(yg * yg, axis=-1, keepdims=True) + EPS)
            o_ref[:, g * gw:(g + 1) * gw] = (yg * r * gm_ref[:, g * gw:(g + 1) * gw]).astype(o_ref.dtype)

    return _call(body, name, (n // tile,), [_rs(tile, d), _rs(tile, d), _rs(tile, d, 2), _ps((1, d)), _ps((1, d))],
                 _rs(tile, d), S((n, d), MXU_DTYPE), sem=("parallel",))(yssd, xs, proj, dfull, gamma)


def ssm_gate_bwd(dy3, yssd, xs, proj, dfull, gamma, dproj, name):
    n, d = yssd.shape
    tile = _row_tile(n, 512)
    gw = d // SSM_GROUPS

    def body(dy_ref, y_ref, xs_ref, z_ref, df_ref, gm_ref, dp_in_ref, dys_ref, dxs_ref, dz_ref, dgm_ref, dd_ref):
        i = pl.program_id(0)
        z = z_ref[...]
        s = _sigmoid(z)
        xs_ = xs_ref[...]
        y1 = y_ref[...] + df_ref[...] * xs_
        y2 = y1 * (z * s)
        dy_ = dy_ref[...].astype(F32)
        dgm = []
        dy2 = []
        for g in range(SSM_GROUPS):
            sl = slice(g * gw, (g + 1) * gw)
            dxg, dgg = _rms_bwd_math(y2[:, sl], gm_ref[:, sl], dy_[:, sl])
            dy2.append(dxg)
            dgm.append(dgg)
        dy2 = jnp.concatenate(dy2, axis=1)
        dy1 = dy2 * (z * s)
        dys_ref[...] = dy1
        dxs_ref[...] = dy1 * df_ref[...]
        dz_ref[...] = (dy2 * y1 * s * (1.0 + z * (1.0 - s))).astype(dz_ref.dtype)
        _acc(dgm_ref, i, jnp.concatenate(dgm, axis=1))
        colsum = jnp.broadcast_to(jnp.sum(dy1 * xs_, axis=0, keepdims=True), (8, d))
        _acc(dd_ref, i, _mm01(colsum, _head_indicator(d, SSM_HEADS))[0:1, :])

    return pl.pallas_call(
        body, name=name, grid=(n // tile,),
        in_specs=[_rs(tile, d), _rs(tile, d), _rs(tile, d), _rs(tile, d, 2), _ps((1, d)), _ps((1, d)), ANY],
        out_specs=[_rs(tile, d), _rs(tile, d), _rs(tile, d, 2), _ps((1, d)), _ps((1, SSM_HEADS))],
        out_shape=[S((n, d), F32), S((n, d), F32), S(dproj.shape, dproj.dtype), S((1, d), F32), S((1, SSM_HEADS), F32)],
        input_output_aliases={6: 2},
        compiler_params=pltpu.CompilerParams(dimension_semantics=("arbitrary",), vmem_limit_bytes=VMEM_LIMIT),
    )(dy3, yssd, xs, proj, dfull, gamma, dproj)


def _rope128(x, cos, sin_signed):
    half = HEAD_DIM // 2
    lane = _iota(x.shape, 1)
    partner = jnp.where((lane % HEAD_DIM) < half, pltpu.roll(x, LANE - half, 1), pltpu.roll(x, half, 1))
    return x * cos + partner * sin_signed


def rope_fwd(qkv, cos, sin, name):
    n, w = qkv.shape
    qw = ATT_HEADS * HEAD_DIM
    kw = ATT_KV_HEADS * HEAD_DIM
    tile = _row_tile(n, 512)

    def body(x_ref, c_ref, s_ref, q_ref, k_ref, v_ref):
        c, s = c_ref[...], s_ref[...]
        for j in range(qw // LANE):
            q_ref[:, j * LANE:(j + 1) * LANE] = _rope128(x_ref[:, j * LANE:(j + 1) * LANE], c, s).astype(q_ref.dtype)
        for j in range(kw // LANE):
            k_ref[:, j * LANE:(j + 1) * LANE] = _rope128(x_ref[:, qw + j * LANE:qw + (j + 1) * LANE], c, s).astype(k_ref.dtype)
        v_ref[...] = x_ref[:, qw + kw:].astype(v_ref.dtype)

    return _call(body, name, (n // tile,), [_rs(tile, w), _rs(tile, LANE), _rs(tile, LANE)],
                 [_rs(tile, qw), _rs(tile, kw), _rs(tile, kw)],
                 [S((n, qw), MXU_DTYPE), S((n, kw), MXU_DTYPE), S((n, kw), MXU_DTYPE)], sem=("parallel",))(qkv, cos, sin)


ATT_GROUP = ATT_HEADS // ATT_KV_HEADS


def _attn_mask(i):
    row = _iota((ATT_GROUP * WINDOW, 2 * WINDOW), 0) % WINDOW
    s = _iota((ATT_GROUP * WINDOW, 2 * WINDOW), 1)
    return (s > row) & (s <= row + WINDOW) & ((s >= WINDOW) | (i > 0))


def _stack_heads(ref, j, kh, lo):
    parts = []
    for t in range(ATT_GROUP):
        h = ATT_GROUP * j + t
        blk = ref[:, (h // 2) * LANE:(h // 2 + 1) * LANE]
        blk = jnp.where(lo if h % 2 == 0 else jnp.logical_not(lo), blk, jnp.zeros_like(blk))
        parts.append(blk if h % 2 == kh else pltpu.roll(blk, HEAD_DIM, 1))
    return jnp.concatenate(parts, axis=0)


def _unstack_heads(stacked, j, kh, lo, put):
    for t in range(0, ATT_GROUP, 2):
        h = ATT_GROUP * j + t
        even = stacked[t * WINDOW:(t + 1) * WINDOW, :]
        odd = stacked[(t + 1) * WINDOW:(t + 2) * WINDOW, :]
        even = even if kh == 0 else pltpu.roll(even, HEAD_DIM, 1)
        odd = odd if kh == 1 else pltpu.roll(odd, HEAD_DIM, 1)
        put(h // 2, jnp.where(lo, even, odd))


def _per_head_rows(ref, j):
    return jnp.concatenate([ref[:, ATT_GROUP * j + t:ATT_GROUP * j + t + 1] for t in range(ATT_GROUP)], axis=0)


def _per_head_scalar(ref, j):
    rows = _iota((ATT_GROUP * WINDOW, 1), 0) // WINDOW
    out = jnp.zeros((ATT_GROUP * WINDOW, 1), F32)
    for t in range(ATT_GROUP):
        out = out + jnp.where(rows == t, ref[:, ATT_GROUP * j + t:ATT_GROUP * j + t + 1], 0.0)
    return out


def attn_fwd(q, k, v, sinks, name):
    n, qw = q.shape
    kw = k.shape[1]
    nb = n // WINDOW
    scale = HEAD_DIM ** -0.5

    def body(q_ref, kc_ref, kp_ref, vc_ref, vp_ref, sk_ref, o_ref, lse_ref):
        i = pl.program_id(0)
        valid = _attn_mask(i)
        lo = _iota((WINDOW, LANE), 1) < HEAD_DIM
        k2 = jnp.concatenate([kp_ref[...], kc_ref[...]], axis=0)
        v2 = jnp.concatenate([vp_ref[...], vc_ref[...]], axis=0)
        lane1 = _iota((1, LANE), 1)
        lse = jnp.zeros((WINDOW, LANE), F32)

        def put_o(qb, val):
            o_ref[:, qb * LANE:(qb + 1) * LANE] = val.astype(o_ref.dtype)

        kv = [(j, j // 2, j % 2) for j in range(ATT_KV_HEADS)]
        logits = [jnp.where(valid, _mm(_stack_heads(q_ref, j, kh, lo), k2[:, kb * LANE:(kb + 1) * LANE], NT) * scale, -1e30)
                  for j, kb, kh in kv]
        sks = [_per_head_scalar(sk_ref, j) for j, _, _ in kv]
        ms = [jnp.maximum(jnp.max(l, axis=-1, keepdims=True), sk) for l, sk in zip(logits, sks)]
        es = [jnp.exp(l - m) for l, m in zip(logits, ms)]
        dens = [jnp.sum(e, axis=-1, keepdims=True) + jnp.exp(sk - m) for e, sk, m in zip(es, sks, ms)]
        for (j, kb, kh), e, m, den in zip(kv, es, ms, dens):
            lse4 = m + jnp.log(den)
            for t in range(ATT_GROUP):
                lse = lse + lse4[t * WINDOW:(t + 1) * WINDOW, :] * (lane1 == ATT_GROUP * j + t).astype(F32)
            _unstack_heads(_mm(e * (1.0 / den), v2[:, kb * LANE:(kb + 1) * LANE]), j, kh, lo, put_o)
        lse_ref[...] = lse

    return _call(body, name, (nb,),
                 [_rs(WINDOW, qw), _rs(WINDOW, kw), _rs(WINDOW, kw, 0, -1), _rs(WINDOW, kw), _rs(WINDOW, kw, 0, -1), _ps((1, LANE))],
                 [_rs(WINDOW, qw), _rs(WINDOW, LANE)], [S((n, qw), MXU_DTYPE), S((n, LANE), F32)],
                 sem=("parallel",))(q, k, k, v, v, sinks)


def attn_bwd(q, k, v, o, do, lse, sinks, name):
    n, qw = q.shape
    kw = k.shape[1]
    nb = n // WINDOW
    scale = HEAD_DIM ** -0.5

    def body(q_ref, kc_ref, kp_ref, vc_ref, vp_ref, o_ref, do_ref, lse_ref, sk_ref,
             dq_ref, dka_ref, dkb_ref, dva_ref, dvb_ref, dsk_ref):
        i = pl.program_id(0)
        valid = _attn_mask(i)
        lo = _iota((WINDOW, LANE), 1) < HEAD_DIM
        k2 = jnp.concatenate([kp_ref[...], kc_ref[...]], axis=0)
        v2 = jnp.concatenate([vp_ref[...], vc_ref[...]], axis=0)
        lane1 = _iota((1, LANE), 1)
        do_ = do_ref[...].astype(F32)
        delta = _mm01(do_ * o_ref[...].astype(F32), _head_indicator(qw, ATT_HEADS))
        dk2 = [jnp.zeros((2 * WINDOW, LANE), F32) for _ in range(kw // LANE)]
        dv2 = [jnp.zeros((2 * WINDOW, LANE), F32) for _ in range(kw // LANE)]
        dsk = jnp.zeros((1, LANE), F32)

        def put_dq(qb, val):
            dq_ref[:, qb * LANE:(qb + 1) * LANE] = val

        kv = [(j, j // 2, j % 2) for j in range(ATT_KV_HEADS)]
        q4s = [_stack_heads(q_ref, j, kh, lo) for j, _, kh in kv]
        do4s = [_stack_heads(do_ref, j, kh, lo) for j, _, kh in kv]
        kks = [k2[:, kb * LANE:(kb + 1) * LANE] for _, kb, _ in kv]
        vvs = [v2[:, kb * LANE:(kb + 1) * LANE] for _, kb, _ in kv]
        lses = [_per_head_rows(lse_ref, j) for j, _, _ in kv]
        dls = [jnp.concatenate([delta[:, ATT_GROUP * j + t:ATT_GROUP * j + t + 1] for t in range(ATT_GROUP)], axis=0)
               for j, _, _ in kv]
        ps = [jnp.exp(jnp.where(valid, _mm(q4, kk, NT) * scale, -1e30) - lse4) for q4, kk, lse4 in zip(q4s, kks, lses)]
        dss = [p * (_mm(do4, vv, NT) - dl) * scale for p, do4, vv, dl in zip(ps, do4s, vvs, dls)]
        for (j, kb, kh), q4, do4, kk, lse4, dl, p, ds in zip(kv, q4s, do4s, kks, lses, dls, ps, dss):
            sd = jnp.exp(_per_head_scalar(sk_ref, j) - lse4) * dl
            for t in range(ATT_GROUP):
                dsk = dsk - (jnp.sum(sd[t * WINDOW:(t + 1) * WINDOW, :], axis=0, keepdims=True)
                             * (lane1 == ATT_GROUP * j + t).astype(F32))
            _unstack_heads(_mm(ds, kk), j, kh, lo, put_dq)
            dk2[kb] = dk2[kb] + _mm(ds, q4, TN)
            dv2[kb] = dv2[kb] + _mm(p, do4, TN)
        for kb in range(kw // LANE):
            dkb_ref[:, kb * LANE:(kb + 1) * LANE] = dk2[kb][0:WINDOW, :]
            dka_ref[:, kb * LANE:(kb + 1) * LANE] = dk2[kb][WINDOW:, :]
            dvb_ref[:, kb * LANE:(kb + 1) * LANE] = dv2[kb][0:WINDOW, :]
            dva_ref[:, kb * LANE:(kb + 1) * LANE] = dv2[kb][WINDOW:, :]
        _acc(dsk_ref, i, dsk)

    return _call(body, name, (nb,),
                 [_rs(WINDOW, qw), _rs(WINDOW, kw), _rs(WINDOW, kw, 0, -1), _rs(WINDOW, kw), _rs(WINDOW, kw, 0, -1),
                  _rs(WINDOW, qw), _rs(WINDOW, qw), _rs(WINDOW, LANE), _ps((1, LANE))],
                 [_rs(WINDOW, qw)] + [_rs(WINDOW, kw)] * 4 + [_ps((1, LANE))],
                 [S((n, qw), F32)] + [S((n, kw), F32)] * 4 + [S((1, LANE), F32)],
                 sem=("arbitrary",))(q, k, k, v, v, o, do, lse, sinks)


def attn_grad_merge(dq, dka, dkb, dva, dvb, cos, sin, name):
    n, qw = dq.shape
    kw = dka.shape[1]
    nb = n // WINDOW
    w = qw + 2 * kw

    def body(dq_ref, dka_ref, dkb_ref, dva_ref, dvb_ref, c_ref, s_ref, o_ref, db_ref):
        i = pl.program_id(0)
        c, s = c_ref[...], -s_ref[...]
        nxt = (i < nb - 1).astype(F32)

        @pl.when(i == 0)
        def _():
            db_ref[...] = jnp.zeros_like(db_ref)

        def put(c0, val):
            o_ref[:, c0:c0 + val.shape[1]] = val.astype(o_ref.dtype)
            db_ref[:, c0:c0 + val.shape[1]] += jnp.sum(val, axis=0, keepdims=True)

        for j in range(qw // LANE):
            put(j * LANE, _rope128(dq_ref[:, j * LANE:(j + 1) * LANE], c, s))
        for j in range(kw // LANE):
            sl = slice(j * LANE, (j + 1) * LANE)
            put(qw + j * LANE, _rope128(dka_ref[:, sl] + dkb_ref[:, sl] * nxt, c, s))
        put(qw + kw, dva_ref[...] + dvb_ref[...] * nxt)

    return _call(body, name, (nb,),
                 [_rs(WINDOW, qw), _rs(WINDOW, kw), _rs(WINDOW, kw, 0, 1, nb), _rs(WINDOW, kw), _rs(WINDOW, kw, 0, 1, nb),
                  _rs(WINDOW, LANE), _rs(WINDOW, LANE)],
                 [_rs(WINDOW, w), _ps((1, w))], [S((n, w), MXU_DTYPE), S((1, w), F32)],
                 sem=("arbitrary",))(dq, dka, dkb, dva, dvb, cos, sin)


def _row(v):
    return v.reshape(1, -1)


def _pad_lanes(v, width=LANE):
    return jnp.pad(v.reshape(1, -1), ((0, 0), (0, width - v.size)))


def chip_major_tn(a, a2, x, rows, name):
    t, m = a.shape
    d = x.shape[1]
    padded = -(-rows // 32) * 32
    assert m % 8 == 0 and (N_CHIPS - 1) * rows + padded <= m + a2.shape[1] and N_CHIPS * rows >= m
    tk, tn = _row_tile(t, 512), _pick_tile(d, 512)
    nk = t // tk

    def body(a_ref, a2_ref, x_ref, o_ref, acc_ref):
        k = pl.program_id(1)
        x = x_ref[...]
        top, bottom = _mm(a_ref[...], x, TN), _mm(a2_ref[...], x, TN)

        @pl.when(k == 0)
        def _():
            acc_ref[:m] = top
            acc_ref[m:] = bottom

        @pl.when(k > 0)
        def _():
            acc_ref[:m] += top
            acc_ref[m:] += bottom

        @pl.when(k == nk - 1)
        def _():
            keep = _iota((padded, tn), 0) < rows
            for j in range(N_CHIPS):
                o_ref[j] = jnp.where(keep, acc_ref[j * rows:j * rows + padded, :], 0.0).astype(o_ref.dtype)

    return _call(body, name, (d // tn, nk),
                 [pl.BlockSpec((tk, m), lambda n, k: (k, 0)), pl.BlockSpec((tk, a2.shape[1]), lambda n, k: (k, 0)),
                  pl.BlockSpec((tk, tn), lambda n, k: (k, n))],
                 pl.BlockSpec((N_CHIPS, padded, tn), lambda n, k: (0, 0, n)), S((N_CHIPS, padded, d), MXU_DTYPE),
                 scratch=[pltpu.VMEM((m + a2.shape[1], tn), F32)], sem=("parallel", "arbitrary"))(a, a2, x)


class LayerWeights(dict):
    def __init__(self, small, fetch):
        super().__init__(small)
        self.fetch = fetch

    def need(self, k, after):
        if k not in self:
            self[k] = self.fetch(k, after)
        return self[k]


def ffn_fwd(h, g, w, keys, tag):
    xn, u, hm = swiglu_in(h, _row(g), w.need(keys[0], h), f"{tag}_in")
    return matmul(hm, w.need(keys[1], hm), "nn", f"{tag}_out", scale=0.5, res=h), (h, xn, u, hm)


class GradSink:
    ORDER = ("ffn1_w_out", "ffn2_w_out", "ple_gate_w", "att_w_o", "hyb_w_out", "ffn1_w_in", "ffn2_w_in", "att_w_qkv",
             "ple_proj_w", "hyb_w_in")

    def __init__(self, shard_shapes, bucket_of):
        self.where, rows = {}, {}
        for k in self.ORDER:
            n, r, c = shard_shapes[k]
            for li in range(n):
                layer = li if k in PER_LAYER else 2 * li + (0 if k in EVEN_ONLY else 1)
                rows_b = rows.setdefault(bucket_of(layer, _stage(k)), {})
                key = c if r % 32 == 0 else k
                off = -(-rows_b.get(key, (0, c))[0] // r) * r
                rows_b[key] = (-(-(off + r) // 32) * 32, c)
                self.where[k, li] = (bucket_of(layer, _stage(k)), key, "r" if _shard_axis(k) == 1 else "c", off, r)
        self.bufs = {b: {key: lax.empty((N_CHIPS, r, c), MXU_DTYPE) for key, (r, c) in rows_b.items()}
                     for b, rows_b in rows.items()}

    def mm(self, k, li, a, b, name, scale=None, c0=0, paired=False):
        bucket, key, kind, off, r = self.where[k, li]
        buf = self.bufs[bucket][key]
        slot = Slot(buf, kind, r if kind == "r" else buf.shape[2], off, c0, paired)
        self.bufs[bucket][key] = matmul(a, b, "tn", name, scale=scale, into=slot)

    def put(self, k, li, chip_major):
        b, key = self.where[k, li][:2]
        assert (chip_major.shape, chip_major.dtype) == (self.bufs[b][key].shape, self.bufs[b][key].dtype)
        self.bufs[b][key] = chip_major


def ffn_bwd(dh, g, w_in, w_out, saved, tag, sink, keys, layer, after, colsum=False):
    h, xn, u, hm = saved
    sink.mm(keys[1], layer, hm, dh, f"{tag}_dwout", scale=0.5)
    du = swiglu_out_bwd(dh, w_out, u, after, f"{tag}_dhm")
    sink.mm(keys[0], layer, xn, du, f"{tag}_dwin", paired=True)
    outs = nt_rms_bwd(du, ColSharded(w_in.arr, paired=True), h, _row(g), dh, f"{tag}_dxn", colsum=colsum)
    return (outs[0], outs[1].reshape(-1)) + ((outs[2],) if colsum else ())


def _hyb_params(w):
    d = w["conv_dw_b"].size
    inner = SSM_HEADS * HEAD_DIM
    main = 3 * d + w["ssm_conv_b"].size
    return dict(
        w_main=w["hyb_w_in"][:main], w_dt=jnp.pad(w["hyb_w_in"][main:], ((0, LANE - SSM_HEADS), (0, 0))),
        cw=jnp.pad(w["conv_dw_w"], ((0, 32 - CONV_WIDTH), (0, 0))), cb=_row(w["conv_dw_b"]),
        lg=_row(w["conv_ln_g"]), lb=_row(w["conv_ln_b"]),
        sw=jnp.pad(w["ssm_conv_w"], ((0, 8 - SSM_CONV), (0, 0))), sb=_row(w["ssm_conv_b"]),
        dtb=_pad_lanes(w["ssm_dt_bias"]), al_row=_pad_lanes(w["ssm_a_log"]), al_col=w["ssm_a_log"].reshape(-1, 1),
        dfull=_row(jnp.repeat(w["ssm_d"], HEAD_DIM)), gamma=_row(w["ssm_norm"]), d=d, inner=inner, main=main)


def hyb_fwd(h, w, tag):
    w.need("hyb_w_in", h)
    q = _hyb_params(w)
    proj, xn = rms_matmul(h, _row(w["norm_mix"]), q["w_main"], "nt", f"{tag}_in")
    dtr = matmul(xn, q["w_dt"], "nt", f"{tag}_in_dt")
    u, u1 = conv_group_fwd(proj, q["cw"], q["cb"], q["lg"], q["lb"], f"{tag}_conv")
    pre, xs, bc, dt = ssm_conv_fwd(proj, dtr, q["sw"], q["sb"], q["dtb"], f"{tag}_sconv")
    dtT = dt[:, :SSM_HEADS].T
    yssd, hs = ssd_fwd(xs, bc, dt, dtT, q["al_row"], q["al_col"], f"{tag}_ssd")
    y = ssm_gate_fwd(yssd, xs, proj, q["dfull"], q["gamma"], f"{tag}_gate")
    wo = w.need("hyb_w_out", u)
    h2 = matmul(u, wo[:q["d"]], "nn", f"{tag}_out_a", res=h)
    h2 = matmul(y, wo[q["d"]:], "nn", f"{tag}_out_b", res=h2)
    return h2, (h, xn, proj, u, u1, pre, xs, bc, dt, dtT, yssd, hs, y)


def hyb_bwd(dh, w, saved, tag, sink, layer, after):
    q = _hyb_params(w)
    h, xn, proj, u, u1, pre, xs, bc, dt, dtT, yssd, hs, y = saved
    du = matmul(dh, w["hyb_w_out"][:q["d"]], "nt", f"{tag}_du")
    dy3 = matmul(dh, w["hyb_w_out"][q["d"]:], "nt", f"{tag}_dy")
    sink.mm("hyb_w_out", layer, u, dh, f"{tag}_dwo_a", c0=0)
    sink.mm("hyb_w_out", layer, y, dh, f"{tag}_dwo_b", c0=N_CHIPS // 2)
    dproj, dcw, dcb, dlg, dlb = conv_group_bwd(du, u1, proj, q["cw"], q["lg"], q["lb"], after, f"{tag}_dconv")
    dyssd, dxs_skip, dproj, dgamma, dd = ssm_gate_bwd(dy3, yssd, xs, proj, q["dfull"], q["gamma"], dproj, f"{tag}_dgate")
    dxs, dbc, ddtr, dalog, ddtb = ssd_bwd(xs, bc, dt, dtT, q["al_row"], q["al_col"], hs, dyssd, dxs_skip, f"{tag}_dssd")
    dproj, dsw, dsb = ssm_conv_bwd(dxs, dbc, pre, proj, q["sw"], dproj, f"{tag}_dsconv")
    sink.put("hyb_w_in", layer, chip_major_tn(dproj, ddtr, xn, (q["main"] + SSM_HEADS) // N_CHIPS, f"{tag}_dwin"))
    dh2, dg = nt_rms_bwd(dproj, q["w_main"], h, _row(w["norm_mix"]), dh, f"{tag}_dxn", extra=(ddtr, q["w_dt"]), b_kd=True)
    grads = dict(norm_mix=dg.reshape(-1), conv_dw_w=dcw[:CONV_WIDTH], conv_dw_b=dcb.reshape(-1),
                 conv_ln_g=dlg.reshape(-1), conv_ln_b=dlb.reshape(-1), ssm_conv_w=dsw[:SSM_CONV], ssm_conv_b=dsb.reshape(-1),
                 ssm_dt_bias=ddtb.reshape(-1), ssm_a_log=dalog.reshape(-1), ssm_d=dd.reshape(-1), ssm_norm=dgamma.reshape(-1))
    return dh2, grads


def rope_tables(n):
    half = HEAD_DIM // 2
    inv = ROPE_THETA ** (-jnp.arange(0, HEAD_DIM, 2, dtype=F32) / HEAD_DIM)
    ang = jnp.arange(n, dtype=F32)[:, None] * inv[None, :]
    cos, sin = jnp.cos(ang), jnp.sin(ang)
    reps = LANE // HEAD_DIM
    return jnp.tile(jnp.concatenate([cos, cos], axis=1), (1, reps)), jnp.tile(jnp.concatenate([-sin, sin], axis=1), (1, reps))


def att_fwd(h, w, tables, tag):
    cos, sin = tables
    qkv, xn = rms_matmul(h, _row(w["norm_mix"]), w.need("att_w_qkv", h), "nn", f"{tag}_qkv", bias=_row(w["att_b_qkv"]))
    q, k, v = rope_fwd(qkv, cos, sin, f"{tag}_rope")
    sinks = _pad_lanes(w["att_sinks"])
    o, lse = attn_fwd(q, k, v, sinks, f"{tag}_attn")
    h2 = matmul(o, w.need("att_w_o", o), "nn", f"{tag}_o", bias=_row(w["att_b_o"]), res=h)
    return h2, (h, xn, q, k, v, o, lse, sinks)


def att_bwd(dh, dh_colsum, w, saved, tables, tag, sink, layer):
    cos, sin = tables
    h, xn, q, k, v, o, lse, sinks = saved
    do = matmul(dh, w["att_w_o"], "nt", f"{tag}_do")
    sink.mm("att_w_o", layer, o, dh, f"{tag}_dwo")
    dq, dka, dkb, dva, dvb, dsk = attn_bwd(q, k, v, o, do, lse, sinks, f"{tag}_dattn")
    dqkv, dbqkv = attn_grad_merge(dq, dka, dkb, dva, dvb, cos, sin, f"{tag}_drope")
    sink.mm("att_w_qkv", layer, xn, dqkv, f"{tag}_dwqkv")
    dh2, dg = nt_rms_bwd(dqkv, w["att_w_qkv"], h, _row(w["norm_mix"]), dh, f"{tag}_dxn")
    grads = dict(norm_mix=dg.reshape(-1), att_b_qkv=dbqkv.reshape(-1), att_sinks=dsk[0, :ATT_HEADS],
                 att_b_o=dh_colsum.reshape(-1))
    return dh2, grads


def ple_block_fwd(h, pe, w, tag):
    pp = matmul(pe, w.need("ple_proj_w", h), "nn", f"{tag}_proj")
    out, gl, xn = ple_fwd(h, _row(w["ple_norm"]), w.need("ple_gate_w", h), pp, f"{tag}_gate")
    return out, (h, xn, gl, pp, pe)


def ple_block_bwd(dh, w, saved, tag, sink, layer, after):
    h, xn, gl, pp, pe = saved
    dh2, dpp, dgl, dg = ple_bwd(dh, gl, pp, w["ple_gate_w"], h, _row(w["ple_norm"]), after, f"{tag}_dgate")
    sink.mm("ple_proj_w", layer, pe, dpp, f"{tag}_dwp")
    sink.mm("ple_gate_w", layer, xn, dgl, f"{tag}_dwg")
    return dh2, dict(ple_norm=dg.reshape(-1))


PER_LAYER = ("norm_ffn1", "ffn1_w_in", "ffn1_w_out", "norm_mix", "norm_ffn2", "ffn2_w_in", "ffn2_w_out",
             "ple_norm", "ple_gate_w", "ple_proj_w")
EVEN_ONLY = ("hyb_w_in", "conv_dw_w", "conv_dw_b", "conv_ln_g", "conv_ln_b", "ssm_conv_w", "ssm_conv_b",
             "ssm_dt_bias", "ssm_a_log", "ssm_d", "ssm_norm", "hyb_w_out")
ODD_ONLY = ("att_w_qkv", "att_b_qkv", "att_sinks", "att_w_o", "att_b_o")


def _layer_index(k, i):
    if k in PER_LAYER:
        return i
    if k in (EVEN_ONLY if i % 2 == 0 else ODD_ONLY):
        return i // 2
    return None


def _stage(k):
    return 0 if k.startswith("ffn1") else (2 if k.startswith(("ffn2", "ple")) else 1)


def trunk_fwd_bwd(x, pe, target, layers, final_norm, sink, stage_done):
    depth = len(layers)
    tables = rope_tables(x.shape[0])
    h = x
    saved = []
    for i, w in enumerate(layers):
        h, s1 = ffn_fwd(h, w["norm_ffn1"], w, ("ffn1_w_in", "ffn1_w_out"), f"l{i}_ffn1")
        if i % 2 == 0:
            h, s2 = hyb_fwd(h, w, f"l{i}_hyb")
        else:
            h, s2 = att_fwd(h, w, tables, f"l{i}_att")
        h, s3 = ffn_fwd(h, w["norm_ffn2"], w, ("ffn2_w_in", "ffn2_w_out"), f"l{i}_ffn2")
        h, s4 = ple_block_fwd(h, pe[i], w, f"l{i}_ple")
        saved.append((s1, s2, s3, s4))
    dh, dgf, loss = loss_head(h, _row(final_norm), target, "loss_head")
    grads = {}
    tie = dgf
    for i in reversed(range(depth)):
        w = layers[i]
        s1, s2, s3, s4 = saved[i]
        dh, g = ple_block_bwd(dh, w, s4, f"l{i}_ple", sink, i, tie)
        odd = i % 2 == 1
        out = ffn_bwd(dh, w["norm_ffn2"], w["ffn2_w_in"], w["ffn2_w_out"], s3, f"l{i}_ffn2", sink,
                      ("ffn2_w_in", "ffn2_w_out"), i, tie, colsum=odd)
        dh = out[0]
        g.update(norm_ffn2=out[1])
        tie = stage_done(i, 2, tie)
        if odd:
            dh, gm = att_bwd(dh, out[2], w, s2, tables, f"l{i}_att", sink, i // 2)
        else:
            dh, gm = hyb_bwd(dh, w, s2, f"l{i}_hyb", sink, i // 2, tie)
        g.update(gm)
        tie = stage_done(i, 1, tie)
        out = ffn_bwd(dh, w["norm_ffn1"], w["ffn1_w_in"], w["ffn1_w_out"], s1, f"l{i}_ffn1", sink,
                      ("ffn1_w_in", "ffn1_w_out"), i, tie)
        dh = out[0]
        g.update(norm_ffn1=out[1])
        tie = stage_done(i, 0, tie)
        for k, v in g.items():
            grads.setdefault(k, []).insert(0, v)
    grads = {k: jnp.stack(v) for k, v in grads.items()}
    grads["final_norm"] = dgf.reshape(-1)
    return loss, dh, grads


def _me():
    return lax.axis_index("x"), lax.axis_index("y"), lax.axis_index("c")


def _flip(v, f):
    return 1 - v if f else v


def _remote(src, dst, send_sems, recv_sems, k, dev):
    return pltpu.make_async_remote_copy(src_ref=src, dst_ref=dst, send_sem=send_sems.at[k], recv_sem=recv_sems.at[k],
                                        device_id=dev, device_id_type=MESH)


CHIP_FLIPS = ((1, 0), (0, 1), (1, 1))
DEV_FLIPS = tuple((fx, fy, fc) for fx in (0, 1) for fy in (0, 1) for fc in (0, 1))[1:]


HBM = pl.BlockSpec(memory_space=pltpu.HBM)
SEM = pl.BlockSpec(memory_space=pltpu.SEMAPHORE)
DATAFLOW = pltpu.SideEffectType.DATAFLOW_SIDE_EFFECTING


def _core_half(ref, axis, c):
    if axis is None:
        return ref
    h = ref.shape[axis] // 2
    return ref.at[pl.ds(c * h, h), :] if axis == 0 else ref.at[:, pl.ds(c * h, h)]


def gather_start(xs, lands, halves, after, name):
    na = len(xs)

    def body(*refs):
        x_refs, land_refs = refs[:na], refs[na:2 * na]
        send_sems, recv_sems = refs[2 * na + 1], refs[2 * na + 2]
        token = refs[-1]
        mx, my, mc = _me()
        chip = 2 * mx + my
        for a in range(na):
            for j, (fx, fy) in enumerate(CHIP_FLIPS):
                _remote(_core_half(x_refs[a], halves[a], mc), _core_half(land_refs[a].at[chip], halves[a], mc),
                        send_sems, recv_sems, 3 * a + j, (_flip(mx, fx), _flip(my, fy), mc)).start()
        token[...] = jnp.zeros_like(token)

    outs = pl.pallas_call(
        body, name=name,
        out_shape=(pltpu.SemaphoreType.DMA((3 * na,)), pltpu.SemaphoreType.DMA((3 * na,)))
        + tuple(pltpu.HBM(x.shape, x.dtype) for x in xs) + tuple(pltpu.HBM(l.shape, l.dtype) for l in lands)
        + (S((8, LANE), F32),),
        in_specs=[HBM] * (2 * na) + [pl.BlockSpec(memory_space=pl.ANY)],
        out_specs=(SEM, SEM) + (HBM,) * (2 * na) + (pl.BlockSpec(memory_space=pltpu.VMEM),),
        input_output_aliases={a: 2 + a for a in range(2 * na)},
        compiler_params=pltpu.CompilerParams(has_side_effects=DATAFLOW),
    )(*[pltpu.with_memory_space_constraint(t, pltpu.HBM) for t in list(xs) + list(lands)], after)
    return outs[0], outs[1], list(outs[2:2 + na]), list(outs[2 + na:2 + 2 * na])


def gather_wait(send_sems, recv_sems, xs, lands, halves, first, after, name):
    na = len(xs)

    def body(*refs):
        x_refs, land_refs = refs[:na], refs[na:2 * na]
        send_sems, recv_sems = refs[2 * na], refs[2 * na + 1]
        mx, my, mc = _me()
        for a in range(na):
            for j, (fx, fy) in enumerate(CHIP_FLIPS):
                px, py = _flip(mx, fx), _flip(my, fy)
                cp = _remote(_core_half(x_refs[a], halves[a], mc), _core_half(land_refs[a].at[2 * px + py], halves[a], mc),
                             send_sems, recv_sems, 3 * (first + a) + j, (px, py, mc))
                cp.wait_send()
                cp.wait_recv()

    outs = pl.pallas_call(
        body, name=name,
        out_shape=tuple(pltpu.HBM(x.shape, x.dtype) for x in xs) + tuple(pltpu.HBM(l.shape, l.dtype) for l in lands),
        in_specs=[HBM] * (2 * na) + [SEM, SEM, pl.BlockSpec(memory_space=pl.ANY)], out_specs=(HBM,) * (2 * na),
        input_output_aliases={a: a for a in range(2 * na)},
        compiler_params=pltpu.CompilerParams(has_side_effects=DATAFLOW),
    )(*xs, *lands, send_sems, recv_sems, after)
    return list(outs[na:])


def forward_halves(land, axis, name):
    def body(in_ref, out_ref, send_sems, recv_sems):
        del in_ref
        mx, my, mc = _me()
        sib = (mx, my, 1 - mc)
        slots = [2 * _flip(mx, fx) + _flip(my, fy) for fx, fy in CHIP_FLIPS]
        cps = [_remote(_core_half(out_ref.at[s], axis, mc), _core_half(out_ref.at[s], axis, mc), send_sems, recv_sems, j, sib)
               for j, s in enumerate(slots)]
        for cp in cps:
            cp.start()
        for j, s in enumerate(slots):
            _remote(_core_half(out_ref.at[s], axis, mc), _core_half(out_ref.at[s], axis, 1 - mc), send_sems, recv_sems, j, sib).wait_recv()
        for cp in cps:
            cp.wait_send()

    return pl.pallas_call(
        body, name=name, out_shape=S(land.shape, land.dtype), in_specs=[ANY], out_specs=ANY, input_output_aliases={0: 0},
        scratch_shapes=[pltpu.SemaphoreType.DMA((3,)), pltpu.SemaphoreType.DMA((3,))])(land)


def all_gather_devices(v, name):
    r, l = v.shape

    def body(v_ref, out_ref, send_sems, recv_sems):
        mx, my, mc = _me()
        me = 4 * mx + 2 * my + mc
        peers = [(_flip(mx, fx), _flip(my, fy), _flip(mc, fc)) for fx, fy, fc in DEV_FLIPS]
        sends = [_remote(v_ref, out_ref.at[me], send_sems, recv_sems, j, p) for j, p in enumerate(peers)]
        for cp in sends:
            cp.start()
        for j, (px, py, pc) in enumerate(peers):
            _remote(v_ref, out_ref.at[4 * px + 2 * py + pc], send_sems, recv_sems, j, (px, py, pc)).wait_recv()
        for cp in sends:
            cp.wait_send()

    out = pl.pallas_call(
        body, name=name, out_shape=S((N_DEV, r, l), v.dtype), in_specs=[ANY], out_specs=ANY,
        scratch_shapes=[pltpu.SemaphoreType.DMA((7,)), pltpu.SemaphoreType.DMA((7,))])(v)
    me = 4 * lax.axis_index("x") + 2 * lax.axis_index("y") + lax.axis_index("c")
    return lax.dynamic_update_slice_in_dim(out, v[None], me, axis=0)


def gather_devices_start(v, name):
    me = 4 * lax.axis_index("x") + 2 * lax.axis_index("y") + lax.axis_index("c")
    land = lax.dynamic_update_slice_in_dim(lax.empty((N_DEV,) + v.shape, v.dtype), v[None], me, axis=0)

    def body(v_ref, land_ref, send_sems, recv_sems, v_thru, land_thru, token):
        mx, my, mc = _me()
        for j, (fx, fy, fc) in enumerate(DEV_FLIPS):
            _remote(v_ref, land_ref.at[4 * mx + 2 * my + mc], send_sems, recv_sems, j,
                    (_flip(mx, fx), _flip(my, fy), _flip(mc, fc))).start()
        token[...] = jnp.zeros_like(token)

    outs = pl.pallas_call(
        body, name=name,
        out_shape=(pltpu.SemaphoreType.DMA((7,)), pltpu.SemaphoreType.DMA((7,)), pltpu.HBM(v.shape, v.dtype),
                   pltpu.HBM(land.shape, land.dtype), S((8, LANE), F32)),
        in_specs=[HBM, HBM], out_specs=(SEM, SEM, HBM, HBM, pl.BlockSpec(memory_space=pltpu.VMEM)),
        input_output_aliases={0: 2, 1: 3}, compiler_params=pltpu.CompilerParams(has_side_effects=DATAFLOW),
    )(pltpu.with_memory_space_constraint(v, pltpu.HBM), pltpu.with_memory_space_constraint(land, pltpu.HBM))
    return outs[:4]


def gather_devices_wait(send_sems, recv_sems, v, land, after, name):
    def body(v_ref, land_ref, send_sems, recv_sems, after_ref, v_dead, got_ref):
        mx, my, mc = _me()
        for j, (fx, fy, fc) in enumerate(DEV_FLIPS):
            px, py, pc = _flip(mx, fx), _flip(my, fy), _flip(mc, fc)
            cp = _remote(v_ref, land_ref.at[4 * px + 2 * py + pc], send_sems, recv_sems, j, (px, py, pc))
            cp.wait_send()
            cp.wait_recv()

    return pl.pallas_call(
        body, name=name, out_shape=(pltpu.HBM(v.shape, v.dtype), pltpu.HBM(land.shape, land.dtype)),
        in_specs=[HBM, HBM, SEM, SEM, pl.BlockSpec(memory_space=pl.ANY)], out_specs=(HBM, HBM),
        input_output_aliases={0: 0, 1: 1}, compiler_params=pltpu.CompilerParams(has_side_effects=DATAFLOW),
    )(v, land, send_sems, recv_sems, after)[1]


def sum_devices(g8, name):
    nd, r, l = g8.shape
    tile = r
    for t in (512, 256, 128, 64, 32, 16, 8):
        if r % t == 0:
            tile = t
            break

    def body(g_ref, o_ref):
        acc = g_ref[0]
        for d in range(1, nd):
            acc = acc + g_ref[d]
        o_ref[...] = acc

    return _call(body, name, (r // tile,), [pl.BlockSpec((nd, tile, l), lambda i: (0, i, 0))], _rs(tile, l), S((r, l), F32),
                 sem=("parallel",))(g8)


def exchange_halves(gs, name):
    na = len(gs)
    nch = gs[0].shape[0]

    def body(*refs):
        g_refs, out_refs = refs[:na], refs[na:2 * na]
        send_sems, recv_sems = refs[2 * na:]
        mx, my, mc = _me()
        sib = (mx, my, 1 - mc)
        cps = []
        for a in range(na):
            half = gs[a].shape[1] // 2
            for j in range(nch):
                cps.append(_remote(g_refs[a].at[j, pl.ds((1 - mc) * half, half), :], out_refs[a].at[j],
                                   send_sems, recv_sems, nch * a + j, sib))
        for cp in cps:
            cp.start()
        for cp in cps:
            cp.wait_recv()
        for cp in cps:
            cp.wait_send()

    return pl.pallas_call(
        body, name=name, out_shape=[S((nch, g.shape[1] // 2, g.shape[2]), g.dtype) for g in gs],
        in_specs=[ANY] * na, out_specs=[ANY] * na,
        scratch_shapes=[pltpu.SemaphoreType.DMA((nch * na,)), pltpu.SemaphoreType.DMA((nch * na,))])(*gs)


def exchange_halves_start(gs, name):
    na = len(gs)
    nch = gs[0].shape[0]
    lands = [lax.empty((nch, g.shape[1] // 2, g.shape[2]), g.dtype) for g in gs]

    def body(*refs):
        g_refs, land_refs = refs[:na], refs[na:2 * na]
        send_sems, recv_sems = refs[2 * na], refs[2 * na + 1]
        mx, my, mc = _me()
        for a in range(na):
            half = gs[a].shape[1] // 2
            for j in range(nch):
                _remote(g_refs[a].at[j, pl.ds((1 - mc) * half, half), :], land_refs[a].at[j], send_sems, recv_sems,
                        nch * a + j, (mx, my, 1 - mc)).start()
        refs[-1][...] = jnp.zeros_like(refs[-1])

    outs = pl.pallas_call(
        body, name=name,
        out_shape=(pltpu.SemaphoreType.DMA((nch * na,)), pltpu.SemaphoreType.DMA((nch * na,)))
        + tuple(pltpu.HBM(t.shape, t.dtype) for t in list(gs) + lands) + (S((8, LANE), F32),),
        in_specs=[HBM] * (2 * na), out_specs=(SEM, SEM) + (HBM,) * (2 * na) + (pl.BlockSpec(memory_space=pltpu.VMEM),),
        input_output_aliases={a: 2 + a for a in range(2 * na)},
        compiler_params=pltpu.CompilerParams(has_side_effects=DATAFLOW),
    )(*[pltpu.with_memory_space_constraint(t, pltpu.HBM) for t in list(gs) + lands])
    return outs[0], outs[1], list(outs[2:2 + na]), list(outs[2 + na:2 + 2 * na]), outs[-1]


def exchange_halves_wait(send_sems, recv_sems, gs, lands, after, name):
    na = len(gs)
    nch = gs[0].shape[0]

    def body(*refs):
        g_refs, land_refs = refs[:na], refs[na:2 * na]
        send_sems, recv_sems = refs[2 * na], refs[2 * na + 1]
        mx, my, mc = _me()
        for a in range(na):
            half = gs[a].shape[1] // 2
            for j in range(nch):
                cp = _remote(g_refs[a].at[j, pl.ds((1 - mc) * half, half), :], land_refs[a].at[j], send_sems, recv_sems,
                             nch * a + j, (mx, my, 1 - mc))
                cp.wait_send()
                cp.wait_recv()

    outs = pl.pallas_call(
        body, name=name, out_shape=tuple(pltpu.HBM(t.shape, t.dtype) for t in list(gs) + list(lands)),
        in_specs=[HBM] * (2 * na) + [SEM, SEM, pl.BlockSpec(memory_space=pl.ANY)], out_specs=(HBM,) * (2 * na),
        input_output_aliases={a: a for a in range(2 * na)},
        compiler_params=pltpu.CompilerParams(has_side_effects=DATAFLOW),
    )(*gs, *lands, send_sems, recv_sems, after)
    return list(outs[:na]), list(outs[na:])


def add_halves(g4, got, name):
    nch, r, l = g4.shape
    half = r // 2
    tile = _pick_rows(half)
    nt = half // tile

    def body(g_ref, r_ref, a_ref, own_ref):
        j = pl.program_id(1)
        chip = 2 * lax.axis_index("x") + lax.axis_index("y")
        val = g_ref[0].astype(F32) + r_ref[0].astype(F32)
        a_ref[0] = val.astype(a_ref.dtype)

        @pl.when(j == chip)
        def _():
            own_ref[...] = val

    return pl.pallas_call(
        body, name=name, grid=(nt, nch),
        in_specs=[pl.BlockSpec((1, tile, l), lambda i, j: (j, lax.axis_index("c") * nt + i, 0)),
                  pl.BlockSpec((1, tile, l), lambda i, j: (j, i, 0))],
        out_specs=[pl.BlockSpec((1, tile, l), lambda i, j: (j, i, 0)), pl.BlockSpec((tile, l), lambda i, j: (i, 0))],
        out_shape=[S((nch, half, l), MXU_DTYPE), S((half, l), F32)],
        compiler_params=pltpu.CompilerParams(dimension_semantics=("parallel", "arbitrary"), vmem_limit_bytes=VMEM_LIMIT))(g4, got)


def _pick_rows(r, cap=640):
    return next((t for t in range(cap - cap % 16, 15, -16) if r % t == 0), r)


def add_chips(own, got, name):
    h, l = own.shape
    tile = _pick_rows(h)

    def body(o_ref, g_ref, out_ref):
        out_ref[...] = ((o_ref[...] + g_ref[0].astype(F32)) + g_ref[1].astype(F32)) + g_ref[2].astype(F32)

    nt = h // tile
    return _call(body, name, (nt,), [_rs(tile, l), pl.BlockSpec((3, tile, l), lambda i: (0, i, 0))],
                 pl.BlockSpec((tile, l), lambda i: (lax.axis_index("c") * nt + i, 0)),
                 S((2 * h, l), F32), sem=("parallel",))(own, got)


def join_halves(bufs, name):
    na = len(bufs)

    def body(*refs):
        out_refs = refs[na:2 * na]
        send_sems, recv_sems = refs[2 * na:]
        mx, my, mc = _me()
        sib = (mx, my, 1 - mc)

        def half(a, hc):
            h = bufs[a].shape[0] // 2
            return out_refs[a].at[pl.ds(hc * h, h), :]

        cps = [_remote(half(a, mc), half(a, mc), send_sems, recv_sems, a, sib) for a in range(na)]
        for cp in cps:
            cp.start()
        for a in range(na):
            _remote(half(a, mc), half(a, 1 - mc), send_sems, recv_sems, a, sib).wait_recv()
        for cp in cps:
            cp.wait_send()

    return pl.pallas_call(
        body, name=name, out_shape=[S(b.shape, b.dtype) for b in bufs], in_specs=[ANY] * na, out_specs=[ANY] * na,
        input_output_aliases={a: a for a in range(na)},
        scratch_shapes=[pltpu.SemaphoreType.DMA((na,)), pltpu.SemaphoreType.DMA((na,))])(*bufs)


def exchange_chips_start(parts, name):
    na = len(parts)
    lands = [lax.empty((3,) + p.shape[1:], p.dtype) for p in parts]

    def body(*refs):
        a_refs, land_refs = refs[:na], refs[na:2 * na]
        send_sems, recv_sems = refs[2 * na], refs[2 * na + 1]
        mx, my, mc = _me()
        for j, (fx, fy) in enumerate(CHIP_FLIPS):
            px, py = _flip(mx, fx), _flip(my, fy)
            for a in range(na):
                _remote(a_refs[a].at[2 * px + py], land_refs[a].at[j], send_sems, recv_sems, 3 * a + j, (px, py, mc)).start()
        refs[-1][...] = jnp.zeros_like(refs[-1])

    outs = pl.pallas_call(
        body, name=name,
        out_shape=(pltpu.SemaphoreType.DMA((3 * na,)), pltpu.SemaphoreType.DMA((3 * na,)))
        + tuple(pltpu.HBM(t.shape, t.dtype) for t in list(parts) + lands) + (S((8, LANE), F32),),
        in_specs=[HBM] * (2 * na), out_specs=(SEM, SEM) + (HBM,) * (2 * na) + (pl.BlockSpec(memory_space=pltpu.VMEM),),
        input_output_aliases={a: 2 + a for a in range(2 * na)},
        compiler_params=pltpu.CompilerParams(has_side_effects=DATAFLOW),
    )(*[pltpu.with_memory_space_constraint(t, pltpu.HBM) for t in list(parts) + lands])
    return outs[0], outs[1], list(outs[2:2 + na]), list(outs[2 + na:2 + 2 * na]), outs[-1]


def exchange_chips_wait(send_sems, recv_sems, parts, lands, after, name):
    na = len(parts)

    def body(*refs):
        a_refs, land_refs = refs[:na], refs[na:2 * na]
        send_sems, recv_sems = refs[2 * na], refs[2 * na + 1]
        mx, my, mc = _me()
        for j, (fx, fy) in enumerate(CHIP_FLIPS):
            px, py = _flip(mx, fx), _flip(my, fy)
            for a in range(na):
                cp = _remote(a_refs[a].at[2 * px + py], land_refs[a].at[j], send_sems, recv_sems, 3 * a + j, (px, py, mc))
                cp.wait_send()
                cp.wait_recv()

    outs = pl.pallas_call(
        body, name=name, out_shape=tuple(pltpu.HBM(t.shape, t.dtype) for t in list(parts) + list(lands)),
        in_specs=[HBM] * (2 * na) + [SEM, SEM, pl.BlockSpec(memory_space=pl.ANY)], out_specs=(HBM,) * (2 * na),
        input_output_aliases={a: a for a in range(2 * na)},
        compiler_params=pltpu.CompilerParams(has_side_effects=DATAFLOW),
    )(*parts, *lands, send_sems, recv_sems, after)
    return list(outs[na:])


def reduce_begin(gs, tag):
    got = exchange_halves(gs, f"{tag}_d2d")
    return reduce_continue(gs, got, tag)


def reduce_continue(gs, got, tag):
    sums = [add_halves(g, r, f"{tag}_add1_{i}") for i, (g, r) in enumerate(zip(gs, got))]
    return [own for _, own in sums], exchange_chips_start([a for a, _ in sums], f"{tag}_ici_start")


def reduce_end(state, after, tag):
    owns, (send_sems, recv_sems, parts, lands, _) = state
    got = exchange_chips_wait(send_sems, recv_sems, parts, lands, after, f"{tag}_ici_wait")
    return [add_chips(own, r, f"{tag}_add2_{i}") for i, (own, r) in enumerate(zip(owns, got))]


PACK_L = 1024


def _pack(arrs, dtype, row_mult, lead=None):
    lead_shape = () if lead is None else arrs[0].shape[:lead]
    flat = jnp.concatenate([a.astype(dtype).reshape(lead_shape + (-1,)) for a in arrs], axis=-1)
    n = flat.shape[-1]
    unit = row_mult * PACK_L
    total = -(-n // unit) * unit
    flat = jnp.pad(flat, [(0, 0)] * len(lead_shape) + [(0, total - n)])
    return flat.reshape(lead_shape + (total // PACK_L, PACK_L))


def _unpack(packed, shapes, lead=None):
    lead_shape = () if lead is None else packed.shape[:lead]
    flat = packed.reshape(lead_shape + (-1,))
    out, off = [], 0
    for shp in shapes:
        n = int(np.prod(shp))
        out.append(flat[..., off:off + n].reshape(lead_shape + tuple(shp)))
        off += n
    return out


def _to_full(gathered, axis):
    t = jnp.moveaxis(gathered, 0, axis)
    shp = t.shape
    return t.reshape(shp[:axis] + (shp[axis] * shp[axis + 1],) + shp[axis + 2:])


WEIGHTS = ("norm_ffn1", "ffn1_w_in", "ffn1_w_out", "norm_mix", "norm_ffn2", "ffn2_w_in", "ffn2_w_out", "ple_norm",
           "ple_gate_w", "ple_proj_w", "hyb_w_in", "conv_dw_w", "conv_dw_b", "conv_ln_g", "conv_ln_b", "ssm_conv_w",
           "ssm_conv_b", "ssm_dt_bias", "ssm_a_log", "ssm_d", "ssm_norm", "hyb_w_out", "att_w_qkv", "att_b_qkv",
           "att_sinks", "att_w_o", "att_b_o", "final_norm")
SHARD_AXIS = dict(ffn1_w_in=2, ffn1_w_out=1, ffn2_w_in=2, ffn2_w_out=1, ple_gate_w=1, ple_proj_w=2, hyb_w_in=2,
                  conv_dw_w=2, ssm_conv_w=2, hyb_w_out=1, att_w_qkv=2, att_b_qkv=1, att_w_o=1, att_b_o=1)
BIG = ("ffn1_w_in", "ffn1_w_out", "ffn2_w_in", "ffn2_w_out", "ple_gate_w", "ple_proj_w", "hyb_w_in", "hyb_w_out",
       "att_w_qkv", "att_w_o")
TRANSPOSED = ("hyb_w_in",)
FORWARDED = (0, 2)


def _shard_axis(k):
    return 1 if k in TRANSPOSED else SHARD_AXIS[k]
SMALL_SHARDED = ("conv_dw_w", "ssm_conv_w", "att_b_qkv", "att_b_o")
SMALL = tuple(k for k in WEIGHTS if k not in BIG)


def _step(x, p, target, w, m, v):
    mx, my = lax.axis_index("x"), lax.axis_index("y")
    chip = 2 * mx + my
    w, m, v = ({k: (a.transpose(0, 2, 1) if k in TRANSPOSED else a) for k, a in d.items()} for d in (w, m, v))

    depth = w["norm_ffn1"].shape[0]
    order = sorted([(k, i) for i in range(depth) for k in BIG if _layer_index(k, i) is not None],
                   key=lambda t: (t[1], _stage(t[0])))
    small = gather_devices_start(_pack([w[k] for k in SMALL_SHARDED], F32, 8), "gather_small_start")
    shards =[w[k][_layer_index(k, i)].astype(MXU_DTYPE) for k, i in order]
    lands = [lax.dynamic_update_slice_in_dim(lax.empty((N_CHIPS,) + s.shape, s.dtype), s[None], chip, axis=0) for s in shards]
    halves = [(0 if s.shape[0] % 32 == 0 else 1) if pos in FORWARDED else None for pos, s in enumerate(shards)]
    send_sems, recv_sems, shards, lands = gather_start(shards, lands, halves, small[3], "gather_start")
    small_g = gather_devices_wait(*small, shards[0], "gather_small_wait")

    def fetch(i, k, after):
        p = order.index((k, i))
        g, = gather_wait(send_sems, recv_sems, [shards[p]], [lands[p]], [halves[p]], p, after, f"gather_wait_l{i}_{k}")
        if halves[p] is not None:
            g = forward_halves(g, halves[p], f"gather_forward_l{i}_{k}")
        if _shard_axis(k) == 2:
            return ColSharded(g)
        return g.reshape(-1, g.shape[-1])

    small_g = small_g[0::2]
    small_full = {k: _to_full(g, SHARD_AXIS[k])
                  for k, g in zip(SMALL_SHARDED, _unpack(small_g, [w[k].shape for k in SMALL_SHARDED], lead=1))}
    layers = [LayerWeights({k: small_full.get(k, w[k])[_layer_index(k, i)] for k in SMALL if _layer_index(k, i) is not None},
                           functools.partial(fetch, i)) for i in range(depth)]

    def bucket_of(layer, stage):
        return (layer, 0) if layer > 0 else (0, min(stage, 1))

    sink = GradSink({k: w[k].shape for k in BIG}, bucket_of)
    begun = {}
    pending = []

    def stage_done(i, stage, tie):
        if stage == 2 and pending:
            b, (send_sems, recv_sems, gs, lands, _) = pending.pop()
            gs, got = exchange_halves_wait(send_sems, recv_sems, gs, lands, tie, f"grads_l{b[0]}_{b[1]}_d2d_wait")
            begun[b] = reduce_continue(gs, got, f"grads_l{b[0]}_{b[1]}")
            return begun[b][1][-1]
        b = bucket_of(i, stage)
        if stage > 0 and bucket_of(i, stage - 1) == b:
            return tie
        gs = list(sink.bufs[b].values())
        if i > 0:
            pending.append((b, exchange_halves_start(gs, f"grads_l{b[0]}_{b[1]}_d2d_start")))
            return pending[-1][1][-1]
        begun[b] = reduce_begin(gs, f"grads_l{b[0]}_{b[1]}")
        return begun[b][1][-1]

    loss, dx, grads = trunk_fwd_bwd(x[0], p[:, 0], target[0], layers, w["final_norm"], sink, stage_done)

    results = {}

    def finish(buckets, after, tag):
        halves = {b: reduce_end(begun[b], after, f"grads_l{b[0]}_{b[1]}") for b in buckets}
        joined = iter(join_halves([h for b in buckets for h in halves[b]], f"grads_join_{tag}"))
        reduced = {b: {c: next(joined) for c in sink.bufs[b]} for b in buckets}
        last = after
        for (k, li), (b, key, _, off, r) in sink.where.items():
            if b in buckets:
                g = reduced[b][key]
                if r % 8:
                    g, off = g[off:off + r], 0
                results[k] = adamw_layer(w[k], g, off, m[k], v[k], li, results.get(k), f"adamw_{k}_{li}")
                last = results[k][1]
        return last

    vec = gather_devices_start(_pack([loss[0:1, 0:1]] + [grads[k] for k in SMALL], F32, 8), "gather_vectors_start")
    order_b = list(begun)
    started_last = begun[order_b[-1]][1][-1]
    done = finish(order_b[-1:], finish(order_b[:-1], started_last, "early") if len(order_b) > 1 else dx, "last")
    g_out = {k: results[k][0] for k in BIG}
    vec = sum_devices(gather_devices_wait(*vec, done, "gather_vectors_wait"), "sum_vectors")
    parts = _unpack(vec, [(1, 1)] + [grads[k].shape for k in SMALL])
    loss_out = parts[0].reshape(())
    for k, g in zip(SMALL, parts[1:]):
        if k in SHARD_AXIS:
            ax = SHARD_AXIS[k]
            g = lax.dynamic_slice_in_dim(g, chip * w[k].shape[ax], w[k].shape[ax], axis=ax)
        g_out[k] = g

    for k in TRANSPOSED:
        results[k] = [a.transpose(0, 2, 1) for a in results[k]]
    g_out.update({k: results[k][0] for k in TRANSPOSED})
    delta, new_m, new_v = ({k: results[k][j] for k in BIG} for j in (1, 2, 3))
    shapes = [w[k].shape for k in SMALL]
    packed = [_pack([src[k] for k in SMALL], F32, 8) for src in (w, g_out, m, v)]
    outs = adamw(*packed, "adamw_small")
    for dst, o in zip((delta, new_m, new_v), outs):
        for k, a in zip(SMALL, _unpack(o, shapes)):
            dst[k] = a
    return ((loss_out, dx[None]) + tuple(g_out[k] for k in WEIGHTS) + tuple(delta[k] for k in WEIGHTS)
            + tuple(new_m[k] for k in WEIGHTS) + tuple(new_v[k] for k in WEIGHTS))


def kernel(x, p, norm_ffn1, ffn1_w_in, ffn1_w_out, norm_mix, norm_ffn2, ffn2_w_in, ffn2_w_out, ple_norm, ple_gate_w, ple_proj_w, hyb_w_in, conv_dw_w, conv_dw_b, conv_ln_g, conv_ln_b, ssm_conv_w, ssm_conv_b, ssm_dt_bias, ssm_a_log, ssm_d, ssm_norm, hyb_w_out, att_w_qkv, att_b_qkv, att_sinks, att_w_o, att_b_o, final_norm, loss_target, m_norm_ffn1, m_ffn1_w_in, m_ffn1_w_out, m_norm_mix, m_norm_ffn2, m_ffn2_w_in, m_ffn2_w_out, m_ple_norm, m_ple_gate_w, m_ple_proj_w, m_hyb_w_in, m_conv_dw_w, m_conv_dw_b, m_conv_ln_g, m_conv_ln_b, m_ssm_conv_w, m_ssm_conv_b, m_ssm_dt_bias, m_ssm_a_log, m_ssm_d, m_ssm_norm, m_hyb_w_out, m_att_w_qkv, m_att_b_qkv, m_att_sinks, m_att_w_o, m_att_b_o, m_final_norm, v_norm_ffn1, v_ffn1_w_in, v_ffn1_w_out, v_norm_mix, v_norm_ffn2, v_ffn2_w_in, v_ffn2_w_out, v_ple_norm, v_ple_gate_w, v_ple_proj_w, v_hyb_w_in, v_conv_dw_w, v_conv_dw_b, v_conv_ln_g, v_conv_ln_b, v_ssm_conv_w, v_ssm_conv_b, v_ssm_dt_bias, v_ssm_a_log, v_ssm_d, v_ssm_norm, v_hyb_w_out, v_att_w_qkv, v_att_b_qkv, v_att_sinks, v_att_w_o, v_att_b_o, v_final_norm):
    given = locals()
    w = {k: given[k] for k in WEIGHTS}
    m = {k: given["m_" + k] for k in WEIGHTS}
    v = {k: given["v_" + k] for k in WEIGHTS}
    return _step(x, p, loss_target, w, m, v)
```

```python
import functools

import numpy as np
import jax
import jax.numpy as jnp
from jax import lax
from jax.experimental import pallas as pl
from jax.experimental.pallas import tpu as pltpu

F32 = jnp.float32
BF16 = jnp.bfloat16
MXU_DTYPE = jnp.bfloat16
S = jax.ShapeDtypeStruct
MESH = pl.DeviceIdType.MESH

VMEM_LIMIT = 48 * 2**20
LANE = 128

EPS = 1e-6
SSM_HEADS = 16
HEAD_DIM = 64
SSM_GROUPS = 2
SSM_STATE = 128
SSM_CONV = 4
CHUNK = 128
CONV_WIDTH = 31
ATT_HEADS = 16
ATT_KV_HEADS = 4
WINDOW = 128
ROPE_THETA = 10000.0
ADAM_LR = 0.001
ADAM_B1 = 0.9
ADAM_B2 = 0.999
ADAM_EPS = 1e-08
ADAM_WD = 0.01
ADAM_STEP = 10

N_CHIPS = 4
N_DEV = 8

NN = ((1,), (0,))
NT = ((1,), (1,))
TN = ((0,), (0,))


def _mm(a, b, dims=NN):
    return lax.dot_general(a.astype(MXU_DTYPE), b.astype(MXU_DTYPE), (dims, ((), ())), preferred_element_type=F32)


def _split3(a):
    hi = a.astype(BF16)
    r = a - hi.astype(F32)
    mid = r.astype(BF16)
    lo = (r - mid.astype(F32)).astype(BF16)
    return hi, mid, lo


def _mm01(a, onehot, dims=NN):
    o = onehot.astype(BF16)
    out = None
    for part in _split3(a):
        t = lax.dot_general(part, o, (dims, ((), ())), preferred_element_type=F32)
        out = t if out is None else out + t
    return out


def _01mm(onehot, a):
    o = onehot.astype(BF16)
    out = None
    for part in _split3(a):
        t = lax.dot_general(o, part, (NN, ((), ())), preferred_element_type=F32)
        out = t if out is None else out + t
    return out


def _sigmoid(x):
    return 0.5 * jnp.tanh(0.5 * x) + 0.5


def _softplus(x):
    return jnp.maximum(x, 0.0) + jnp.log(1.0 + jnp.exp(-jnp.abs(x)))


def _iota(shape, axis):
    return lax.broadcasted_iota(jnp.int32, shape, axis)


def _head_indicator(width, heads, transposed=False):
    per = width // heads
    if transposed:
        return (_iota((heads, width), 1) // per == _iota((heads, width), 0)).astype(F32)
    return (_iota((width, heads), 0) // per == _iota((width, heads), 1)).astype(F32)


def _acc(ref, i, val):
    @pl.when(i == 0)
    def _():
        ref[...] = val

    @pl.when(i > 0)
    def _():
        ref[...] += val


def _rs(tile, width, col=0, shift=0, n=None):
    if shift == 0:
        return pl.BlockSpec((tile, width), lambda i: (i, col))
    if shift < 0:
        return pl.BlockSpec((tile, width), lambda i: (jnp.maximum(i - 1, 0), col))
    return pl.BlockSpec((tile, width), lambda i: (jnp.minimum(i + 1, n - 1), col))


def _ps(shape):
    return pl.BlockSpec(shape, lambda i: (0,) * len(shape))


def _call(body, name, grid, in_specs, out_specs, out_shape, scratch=(), sem=None):
    return pl.pallas_call(
        body, name=name, grid=grid, in_specs=in_specs, out_specs=out_specs, out_shape=out_shape,
        scratch_shapes=list(scratch),
        compiler_params=pltpu.CompilerParams(dimension_semantics=sem, vmem_limit_bytes=VMEM_LIMIT))


def _row_tile(n, target):
    t = min(n, target)
    assert n % t == 0, (n, t)
    return t


def _pick_tile(dim, target):
    if dim <= target:
        return dim
    t = (int(1.4 * target) // LANE) * LANE
    while t >= LANE:
        if dim % t == 0:
            return t
        t -= LANE
    return dim


ANY = pl.BlockSpec(memory_space=pl.ANY)


def _paired(j):
    return (j % 2) * 2 + j // 2


class ColSharded:
    def __init__(self, arr, paired=False):
        self.arr, self.paired = arr, paired
        self.nch, self.rows, self.per = arr.shape
        self.shape = (self.rows, self.nch * self.per)

    def chip(self, j):
        return _paired(j) if self.paired else j


class Slot:
    def __init__(self, buf, kind, per, off, c0=0, paired=False):
        self.buf, self.kind, self.per, self.off, self.c0, self.paired = buf, kind, per, off, c0, paired

    def chip(self, j):
        return _paired(j) if self.paired else j


def matmul(a, b, mode, name, *, out_dtype=F32, scale=None, res=None, bias=None, into=None, tm=1024, tn=1024, tk=1024):
    bshape = b.shape
    if mode == "nn":
        (m, k), (k2, n) = a.shape, bshape
    elif mode == "nt":
        (m, k), (n, k2) = a.shape, bshape
    else:
        (k, m), (k2, n) = a.shape, bshape
    assert k == k2, (a.shape, bshape, mode)
    if mode == "tn":
        tk = 2 * tk
    elif k <= 3 * tk:
        tk = k
    tm, tn, tk = _pick_tile(m, tm), _pick_tile(n, tn), _pick_tile(k, tk)
    if isinstance(b, ColSharded):
        if mode == "nn":
            tn = b.per
        else:
            assert mode == "nt"
            tk = b.per
    if into is not None:
        if into.kind == "c":
            tn = into.per
            assert into.off % tm == 0 and n == N_CHIPS * into.per
        else:
            tm = max(1, min(m, int(1.4 * 1024)) // into.per) * into.per
            assert m % tm == 0 and into.off % into.per == 0 and into.c0 % (tm // into.per) == 0
    nk = k // tk
    dims = {"nn": NN, "nt": NT, "tn": TN}[mode]
    a_spec = (pl.BlockSpec((tk, tm), lambda i, j, kk: (kk, i)) if mode == "tn"
              else pl.BlockSpec((tm, tk), lambda i, j, kk: (i, kk)))
    if isinstance(b, ColSharded):
        bchip = b.chip
        b_spec = (pl.BlockSpec((None, tk, tn), lambda i, j, kk: (bchip(j), kk, 0)) if mode == "nn"
                  else pl.BlockSpec((None, tn, tk), lambda i, j, kk: (bchip(kk), j, 0)))
        b = b.arr
    else:
        b_spec = (pl.BlockSpec((tn, tk), lambda i, j, kk: (j, kk)) if mode == "nt"
                  else pl.BlockSpec((tk, tn), lambda i, j, kk: (kk, j)))
    plain_o = pl.BlockSpec((tm, tn), lambda i, j, kk: (i, j))
    ins, in_specs = [a, b], [a_spec, b_spec]
    if bias is not None:
        ins.append(bias)
        in_specs.append(pl.BlockSpec((1, tn), lambda i, j, kk: (0, j)))
    if res is not None:
        ins.append(res)
        in_specs.append(plain_o)
    aliases = {}
    if into is None:
        o_spec, o_shape = plain_o, S((m, n), out_dtype)
    else:
        aliases = {len(ins): 0}
        ins.append(into.buf)
        in_specs.append(ANY)
        o_shape = S(into.buf.shape, into.buf.dtype)
        if into.kind == "c":
            ob, ochip = into.off // tm, into.chip
            o_spec = pl.BlockSpec((None, tm, tn), lambda i, j, kk: (ochip(j), ob + i, 0))
        else:
            q, ob = tm // into.per, into.off // into.per
            cb = into.c0 // q
            o_spec = pl.BlockSpec((q, into.per, tn), lambda i, j, kk: (cb + i, ob, j))

    def body(*refs):
        a_ref, b_ref = refs[0], refs[1]
        o_ref, acc_ref = refs[-2], refs[-1]
        kk = pl.program_id(2)

        def finish(out):
            if scale is not None:
                out = out * scale
            pos = 2
            if bias is not None:
                out = out + refs[pos][...]
                pos += 1
            if res is not None:
                out = out + refs[pos][...]
            o_ref[...] = out.astype(o_ref.dtype).reshape(o_ref.shape)

        if nk == 1:
            finish(_mm(a_ref[...], b_ref[...], dims))
            return

        @pl.when(kk == 0)
        def _():
            acc_ref[...] = jnp.zeros_like(acc_ref)

        acc_ref[...] += _mm(a_ref[...], b_ref[...], dims)

        @pl.when(kk == nk - 1)
        def _():
            finish(acc_ref[...])

    return pl.pallas_call(
        body, name=name, grid=(m // tm, n // tn, nk), in_specs=in_specs, out_specs=o_spec, out_shape=o_shape,
        scratch_shapes=[pltpu.VMEM((tm, tn), F32)], input_output_aliases=aliases,
        compiler_params=pltpu.CompilerParams(dimension_semantics=("parallel", "parallel", "arbitrary"),
                                             vmem_limit_bytes=VMEM_LIMIT))(*ins)


def rms_matmul(h, g, b, mode, name, bias=None, n_out=None):
    n, d = h.shape
    sharded = isinstance(b, ColSharded)
    n_out = n_out or (b.shape[1] if mode == "nn" else b.shape[0])
    tm = _row_tile(n, 1024)
    tn = b.per if sharded else _pick_tile(n_out, 1024)
    if sharded:
        assert mode == "nn"
        bchip = b.chip
        b_spec = pl.BlockSpec((None, d, tn), lambda i, j: (bchip(j), 0, 0))
        b = b.arr
    elif mode == "nn":
        b_spec = pl.BlockSpec((d, tn), lambda i, j: (0, j))
    else:
        b_spec = pl.BlockSpec((tn, d), lambda i, j: (j, 0))
    ins = [h, g, b] + ([bias] if bias is not None else [])
    in_specs = [pl.BlockSpec((tm, d), lambda i, j: (i, 0)), pl.BlockSpec((1, d), lambda i, j: (0, 0)), b_spec]
    if bias is not None:
        in_specs.append(pl.BlockSpec((1, tn), lambda i, j: (0, j)))

    def body(h_ref, g_ref, b_ref, *refs):
        o_ref, xn_ref = refs[-2:]
        x = h_ref[...]
        r = lax.rsqrt(jnp.mean(x * x, axis=-1, keepdims=True) + EPS)
        xn = (x * r * g_ref[...]).astype(xn_ref.dtype)

        @pl.when(pl.program_id(1) == 0)
        def _():
            xn_ref[...] = xn

        out = _mm(xn, b_ref[...], NN if mode == "nn" else NT)
        o_ref[...] = out if bias is None else out + refs[0][...]

    return pl.pallas_call(
        body, name=name, grid=(n // tm, n_out // tn), in_specs=in_specs,
        out_specs=[pl.BlockSpec((tm, tn), lambda i, j: (i, j)), pl.BlockSpec((tm, d), lambda i, j: (i, 0))],
        out_shape=[S((n, n_out), F32), S((n, d), MXU_DTYPE)],
        compiler_params=pltpu.CompilerParams(dimension_semantics=("parallel", "arbitrary"), vmem_limit_bytes=VMEM_LIMIT),
    )(*ins)


def _rms_bwd_math(x, g, dy):
    r = lax.rsqrt(jnp.mean(x * x, axis=-1, keepdims=True) + EPS)
    xh = x * r
    dg = jnp.sum(dy * xh, axis=0, keepdims=True)
    dxh = dy * g
    dx = r * (dxh - xh * jnp.mean(dxh * xh, axis=-1, keepdims=True))
    return dx, dg


def nt_rms_bwd(a, b, h, g, dh_in, name, extra=None, colsum=False, b_kd=False):
    n, k = a.shape
    d = h.shape[1]
    tm = _row_tile(n, 1024)
    sharded = isinstance(b, ColSharded)
    tk = b.per if sharded else _pick_tile(k, 1024)
    nk = k // tk
    dims = NN if b_kd else NT
    if sharded:
        bchip = b.chip
        b_spec = pl.BlockSpec((None, d, tk), lambda i, kk: (bchip(kk), 0, 0))
        b = b.arr
    elif b_kd:
        b_spec = pl.BlockSpec((tk, d), lambda i, kk: (kk, 0))
    else:
        b_spec = pl.BlockSpec((d, tk), lambda i, kk: (0, kk))
    row = pl.BlockSpec((tm, d), lambda i, kk: (i, 0))
    vec = pl.BlockSpec((1, d), lambda i, kk: (0, 0))
    ins, in_specs = [a, b, h, g, dh_in], [pl.BlockSpec((tm, tk), lambda i, kk: (i, kk)), b_spec, row, vec, row]
    if extra is not None:
        k2 = extra[0].shape[1]
        ins += list(extra)
        in_specs += [pl.BlockSpec((tm, k2), lambda i, kk: (i, 0)),
                     pl.BlockSpec((k2, d) if b_kd else (d, k2), lambda i, kk: (0, 0))]
    n_in = len(ins)

    def body(*refs):
        a_ref, b_ref, h_ref, g_ref, dh_ref = refs[:5]
        o_ref, dg_ref = refs[n_in], refs[n_in + 1]
        acc_ref = refs[-1]
        i, kk = pl.program_id(0), pl.program_id(1)

        @pl.when(kk == 0)
        def _():
            acc_ref[...] = _mm(refs[5][...], refs[6][...], dims) if extra is not None else jnp.zeros_like(acc_ref)

        acc_ref[...] += _mm(a_ref[...], b_ref[...], dims)

        @pl.when(kk == nk - 1)
        def _():
            dx, dg = _rms_bwd_math(h_ref[...], g_ref[...], acc_ref[...])
            out = dh_ref[...] + dx
            o_ref[...] = out
            _acc(dg_ref, i, dg)
            if colsum:
                _acc(refs[n_in + 2], i, jnp.sum(out, axis=0, keepdims=True))

    n_vec = 2 if colsum else 1
    return pl.pallas_call(
        body, name=name, grid=(n // tm, nk), in_specs=in_specs, out_specs=[row] + [vec] * n_vec,
        out_shape=[S((n, d), F32)] + [S((1, d), F32)] * n_vec, scratch_shapes=[pltpu.VMEM((tm, d), F32)],
        compiler_params=pltpu.CompilerParams(dimension_semantics=("arbitrary", "arbitrary"), vmem_limit_bytes=VMEM_LIMIT),
    )(*ins)


def swiglu_in(h, g, w_in, name):
    n, d = h.shape
    per = w_in.per
    nj = w_in.nch // 2
    tile = _row_tile(n, 1024)

    def body(h_ref, g_ref, wg_ref, wu_ref, xn_ref, u_ref, hm_ref):
        x = h_ref[...]
        r = lax.rsqrt(jnp.mean(x * x, axis=-1, keepdims=True) + EPS)
        xn = (x * r * g_ref[...]).astype(xn_ref.dtype)

        @pl.when(pl.program_id(1) == 0)
        def _():
            xn_ref[...] = xn

        a = _mm(xn, wg_ref[...])
        b = _mm(xn, wu_ref[...])
        u_ref[:, :per] = a.astype(u_ref.dtype)
        u_ref[:, per:] = b.astype(u_ref.dtype)
        hm_ref[...] = (a * _sigmoid(a) * b).astype(hm_ref.dtype)

    return pl.pallas_call(
        body, name=name, grid=(n // tile, nj),
        in_specs=[pl.BlockSpec((tile, d), lambda i, j: (i, 0)), pl.BlockSpec((1, d), lambda i, j: (0, 0)),
                  pl.BlockSpec((None, d, per), lambda i, j: (j, 0, 0)), pl.BlockSpec((None, d, per), lambda i, j: (nj + j, 0, 0))],
        out_specs=[pl.BlockSpec((tile, d), lambda i, j: (i, 0)), pl.BlockSpec((tile, 2 * per), lambda i, j: (i, j)),
                   pl.BlockSpec((tile, per), lambda i, j: (i, j))],
        out_shape=[S((n, d), MXU_DTYPE), S((n, 2 * nj * per), MXU_DTYPE), S((n, nj * per), MXU_DTYPE)],
        compiler_params=pltpu.CompilerParams(dimension_semantics=("parallel", "arbitrary"), vmem_limit_bytes=VMEM_LIMIT),
    )(h, g, w_in.arr, w_in.arr)


def swiglu_out_bwd(dh, w_out, u, after, name):
    n, d = dh.shape
    f = w_out.shape[0]
    per = u.shape[1] // 4
    nj = f // per
    tile = _row_tile(n, 1024)

    def body(dh_ref, w_ref, u_ref, after_ref, du_ref):
        dm = 0.5 * _mm(dh_ref[...], w_ref[...], NT)
        a = u_ref[:, :per].astype(F32)
        b = u_ref[:, per:].astype(F32)
        s = _sigmoid(a)
        du_ref[:, :per] = (dm * b * s * (1.0 + a * (1.0 - s))).astype(du_ref.dtype)
        du_ref[:, per:] = (dm * a * s).astype(du_ref.dtype)

    return pl.pallas_call(
        body, name=name, grid=(n // tile, nj),
        in_specs=[pl.BlockSpec((tile, d), lambda i, j: (i, 0)), pl.BlockSpec((per, d), lambda i, j: (j, 0)),
                  pl.BlockSpec((tile, 2 * per), lambda i, j: (i, j)), ANY],
        out_specs=pl.BlockSpec((tile, 2 * per), lambda i, j: (i, j)),
        out_shape=S(u.shape, MXU_DTYPE),
        compiler_params=pltpu.CompilerParams(dimension_semantics=("parallel", "parallel"), vmem_limit_bytes=VMEM_LIMIT),
    )(dh, w_out, u, after)


def ple_fwd(h, g, w_gate, pp, name):
    n, d = h.shape
    tile = _row_tile(n, 512)

    def body(h_ref, g_ref, w_ref, pp_ref, o_ref, gl_ref, xn_ref):
        x = h_ref[...]
        r = lax.rsqrt(jnp.mean(x * x, axis=-1, keepdims=True) + EPS)
        xn = (x * r * g_ref[...]).astype(xn_ref.dtype)
        xn_ref[...] = xn
        gl = _mm(xn, w_ref[...])
        gl_ref[...] = gl
        o_ref[...] = x + _sigmoid(gl) * pp_ref[...]

    return _call(body, name, (n // tile,), [_rs(tile, d), _ps((1, d)), _ps(w_gate.shape), _rs(tile, d)],
                 [_rs(tile, d)] * 3, [S((n, d), F32), S((n, d), F32), S((n, d), MXU_DTYPE)], sem=("parallel",))(h, g, w_gate, pp)


def ple_bwd(dh, gl, pp, w_gate, h, g, after, name):
    n, d = dh.shape
    tile = _row_tile(n, 512)

    def body(dh_ref, gl_ref, pp_ref, w_ref, h_ref, g_ref, after_ref, o_ref, dpp_ref, dgl_ref, dg_ref):
        i = pl.program_id(0)
        s = _sigmoid(gl_ref[...])
        dh_ = dh_ref[...]
        dpp_ref[...] = (dh_ * s).astype(dpp_ref.dtype)
        dgl = (dh_ * pp_ref[...] * s * (1.0 - s)).astype(dgl_ref.dtype)
        dgl_ref[...] = dgl
        dx, dg = _rms_bwd_math(h_ref[...], g_ref[...], _mm(dgl, w_ref[...], NT))
        o_ref[...] = dh_ + dx
        _acc(dg_ref, i, dg)

    return _call(body, name, (n // tile,),
                 [_rs(tile, d)] * 3 + [_ps(w_gate.shape), _rs(tile, d), _ps((1, d)), ANY],
                 [_rs(tile, d)] * 3 + [_ps((1, d))],
                 [S((n, d), F32), S((n, d), MXU_DTYPE), S((n, d), MXU_DTYPE), S((1, d), F32)],
                 sem=("arbitrary",))(dh, gl, pp, w_gate, h, g, after)


def loss_head(h, g, target, name):
    n, d = h.shape
    tile = _row_tile(n, 512)

    def body(h_ref, g_ref, t_ref, dh_ref, dg_ref, loss_ref):
        i = pl.program_id(0)
        x = h_ref[...]
        gg = g_ref[...]
        r = lax.rsqrt(jnp.mean(x * x, axis=-1, keepdims=True) + EPS)
        err = x * r * gg - t_ref[...]
        part = 0.5 * jnp.sum(jnp.mean(err * err, axis=-1, keepdims=True), axis=0, keepdims=True)
        dx, dg = _rms_bwd_math(x, gg, err * (1.0 / d))
        dh_ref[...] = dx
        _acc(dg_ref, i, dg)
        _acc(loss_ref, i, jnp.broadcast_to(part, (8, LANE)))

    return _call(body, name, (n // tile,), [_rs(tile, d), _ps((1, d)), _rs(tile, d)],
                 [_rs(tile, d), _ps((1, d)), _ps((8, LANE))], [S((n, d), F32), S((1, d), F32), S((8, LANE), F32)],
                 sem=("arbitrary",))(h, g, target)


def _adamw_math(w, g, m, v):
    c1 = np.float32(1.0 - ADAM_B1 ** ADAM_STEP)
    c2 = np.float32(1.0 - ADAM_B2 ** ADAM_STEP)
    mm = ADAM_B1 * m + (1.0 - ADAM_B1) * g
    vv = ADAM_B2 * v + (1.0 - ADAM_B2) * (g * g)
    return -ADAM_LR * ((mm / c1) / (jnp.sqrt(vv / c2) + ADAM_EPS) + ADAM_WD * w), mm, vv


def adamw_layer(w, pack, off, m, v, li, prev, name):
    n, r, c = w.shape

    def body(w_ref, g_ref, m_ref, v_ref, *refs):
        go_ref, d_ref, mo_ref, vo_ref = refs[-4:]
        g = g_ref[...]
        go_ref[...] = g
        d_ref[...], mo_ref[...], vo_ref[...] = _adamw_math(w_ref[...], g, m_ref[...], v_ref[...])

    if r % 8 == 0:
        cap = 2**21 // (4 * c) // 8 * 8
        tile = next(t for t in range(min(cap, r), 7, -8) if r % t == 0 and off % t == 0)
        ob, steps = off // tile, r // tile
        blk = pl.BlockSpec((None, tile, c), lambda i: (li, i, 0))
        g_spec = pl.BlockSpec((tile, c), lambda i: (ob + i, 0))
    else:
        assert off == 0 and pack.shape[0] == r and c % (2 * LANE) == 0
        steps = c // (2 * LANE)
        blk = pl.BlockSpec((None, r, 2 * LANE), lambda i: (li, 0, i))
        g_spec = pl.BlockSpec((r, 2 * LANE), lambda i: (0, i))
    prev = list(prev) if prev is not None else []
    return pl.pallas_call(
        body, name=name, grid=(steps,),
        in_specs=[blk, g_spec, blk, blk] + [ANY] * len(prev),
        out_specs=[blk] * 4, out_shape=[S((n, r, c), F32)] * 4,
        input_output_aliases={4 + j: j for j in range(len(prev))},
        compiler_params=pltpu.CompilerParams(dimension_semantics=("parallel",), vmem_limit_bytes=VMEM_LIMIT),
    )(w, pack, m, v, *prev)


def adamw(w, g, m, v, name):
    r, c = w.shape
    tile = r
    for t in (512, 256, 128, 64, 32, 16, 8):
        if r % t == 0 and t * c * 4 <= 2**21:
            tile = t
            break

    def body(w_ref, g_ref, m_ref, v_ref, d_ref, mo_ref, vo_ref):
        d_ref[...], mo_ref[...], vo_ref[...] = _adamw_math(w_ref[...], g_ref[...], m_ref[...], v_ref[...])

    return _call(body, name, (r // tile,), [_rs(tile, c)] * 4, [_rs(tile, c)] * 3, [S((r, c), F32)] * 3,
                 sem=("parallel",))(w, g, m, v)


TAP_VREGS = 32


def _taps(src, w_ref, offsets, tile, put, bias=None):
    c = src.shape[1]
    rp = max(8, TAP_VREGS * 8 * LANE // c // 8 * 8)
    for r0 in range(0, tile, rp):
        acc = jnp.zeros((rp, c), F32) if bias is None else jnp.zeros((rp, c), F32) + bias
        for k, o in enumerate(offsets):
            acc = acc + w_ref[k:k + 1, :] * src[r0 + o:r0 + o + rp, :]
        put(slice(r0, r0 + rp), acc)


def _taps_fwd(sc, w_ref, width, halo, tile, put, bias):
    _taps(sc, w_ref, [halo - (width - 1) + k for k in range(width)], tile, put, bias)


def _taps_bwd_x(sc_d, w_ref, width, tile, put):
    _taps(sc_d, w_ref, [(width - 1) - k for k in range(width)], tile, put)


def _taps_bwd_w(dy, sc, dw_ref, width, halo, tile, i):
    @pl.when(i == 0)
    def _():
        dw_ref[...] = jnp.zeros_like(dw_ref)

    for k in range(width):
        o = halo - (width - 1) + k
        dw_ref[k:k + 1, :] += jnp.sum(dy * sc[o:o + tile, :], axis=0, keepdims=True)


def _ln_stats(x):
    mu = jnp.mean(x, axis=-1, keepdims=True)
    xc = x - mu
    r = lax.rsqrt(jnp.mean(xc * xc, axis=-1, keepdims=True) + EPS)
    return xc * r, r


def conv_group_fwd(proj, cw, cb, lg, lb, name):
    n = proj.shape[0]
    d = cw.shape[1]
    tile = _row_tile(n, 256)
    halo = 32

    def body(v_ref, g_ref, vp_ref, gp_ref, cw_ref, cb_ref, lg_ref, lb_ref, u_ref, u1_ref, sc):
        i = pl.program_id(0)
        first = (i > 0).astype(F32)
        sc[0:halo, :] = vp_ref[tile - halo:, :] * _sigmoid(gp_ref[tile - halo:, :]) * first
        sc[halo:, :] = v_ref[...] * _sigmoid(g_ref[...])
        def put(rows, acc):
            u1_ref[rows, :] = acc

        _taps_fwd(sc, cw_ref, CONV_WIDTH, halo, tile, put, cb_ref[...])
        xh, _ = _ln_stats(u1_ref[...])
        y = xh * lg_ref[...] + lb_ref[...]
        u_ref[...] = (y * _sigmoid(y)).astype(u_ref.dtype)

    return _call(body, name, (n // tile,),
                 [_rs(tile, d, 0), _rs(tile, d, 1), _rs(tile, d, 0, -1), _rs(tile, d, 1, -1),
                  _ps(cw.shape), _ps((1, d)), _ps((1, d)), _ps((1, d))],
                 [_rs(tile, d), _rs(tile, d)], [S((n, d), MXU_DTYPE), S((n, d), F32)],
                 scratch=[pltpu.VMEM((halo + tile, d), F32)], sem=("arbitrary",))(proj, proj, proj, proj, cw, cb, lg, lb)


def conv_group_bwd(du, u1, proj, cw, lg, lb, after, name):
    n = proj.shape[0]
    d = cw.shape[1]
    tile = _row_tile(n, 256)
    halo = 32
    nt = n // tile

    def body(du_ref, dun_ref, u1_ref, u1n_ref, v_ref, g_ref, vp_ref, gp_ref, cw_ref, lg_ref, lb_ref, after_ref,
             dp_ref, dcw_ref, dcb_ref, dlg_ref, dlb_ref, sc, sc_d):
        i = pl.program_id(0)

        def ln_swish_bwd(dy_, u1_):
            xh, r = _ln_stats(u1_)
            y = xh * lg_ref[...] + lb_ref[...]
            s = _sigmoid(y)
            dyy = dy_ * s * (1.0 + y * (1.0 - s))
            dxh = dyy * lg_ref[...]
            dx = r * (dxh - jnp.mean(dxh, axis=-1, keepdims=True) - xh * jnp.mean(dxh * xh, axis=-1, keepdims=True))
            return dx, jnp.sum(dyy * xh, axis=0, keepdims=True), jnp.sum(dyy, axis=0, keepdims=True)

        du1, dlg, dlb = ln_swish_bwd(du_ref[...].astype(F32), u1_ref[...])
        du1n, _, _ = ln_swish_bwd(dun_ref[0:halo, :].astype(F32), u1n_ref[0:halo, :])
        sc_d[0:tile, :] = du1
        sc_d[tile:, :] = du1n * (i < nt - 1).astype(F32)
        sc[0:halo, :] = vp_ref[tile - halo:, :] * _sigmoid(gp_ref[tile - halo:, :]) * (i > 0).astype(F32)
        sc[halo:, :] = v_ref[...] * _sigmoid(g_ref[...])

        def put(rows, du0):
            sig = _sigmoid(g_ref[rows, :])
            dp_ref[rows, :d] = (du0 * sig).astype(dp_ref.dtype)
            dp_ref[rows, d:] = (du0 * v_ref[rows, :] * sig * (1.0 - sig)).astype(dp_ref.dtype)

        _taps_bwd_x(sc_d, cw_ref, CONV_WIDTH, tile, put)
        _taps_bwd_w(du1, sc, dcw_ref, CONV_WIDTH, halo, tile, i)
        _acc(dcb_ref, i, jnp.sum(du1, axis=0, keepdims=True))
        _acc(dlg_ref, i, dlg)
        _acc(dlb_ref, i, dlb)

    return _call(body, name, (nt,),
                 [_rs(tile, d), _rs(tile, d, 0, 1, nt), _rs(tile, d), _rs(tile, d, 0, 1, nt),
                  _rs(tile, d, 0), _rs(tile, d, 1), _rs(tile, d, 0, -1), _rs(tile, d, 1, -1),
                  _ps(cw.shape), _ps((1, d)), _ps((1, d)), ANY],
                 [_rs(tile, 2 * d), _ps(cw.shape), _ps((1, d)), _ps((1, d)), _ps((1, d))],
                 [S((n, proj.shape[1]), MXU_DTYPE), S(cw.shape, F32), S((1, d), F32), S((1, d), F32), S((1, d), F32)],
                 scratch=[pltpu.VMEM((halo + tile, d), F32), pltpu.VMEM((tile + halo, d), F32)],
                 sem=("arbitrary",))(du, du, u1, u1, proj, proj, proj, proj, cw, lg, lb, after)


def ssm_conv_fwd(proj, dtr, sw, sb, dtb, name):
    n = proj.shape[0]
    w = sw.shape[1]
    inner = SSM_HEADS * HEAD_DIM
    tile = _row_tile(n, 256)
    halo = 8

    def body(x_ref, xp_ref, dtr_ref, sw_ref, sb_ref, dtb_ref, pre_ref, xs_ref, bc_ref, dt_ref, sc):
        i = pl.program_id(0)
        sc[0:halo, :] = xp_ref[tile - halo:, :] * (i > 0).astype(F32)
        sc[halo:, :] = x_ref[...]
        def put(rows, acc):
            pre_ref[rows, :] = acc

        _taps_fwd(sc, sw_ref, SSM_CONV, halo, tile, put, sb_ref[...])
        pre = pre_ref[...]
        act = pre * _sigmoid(pre)
        xs_ref[...] = act[:, :inner]
        bc_ref[...] = act[:, inner:]
        dt = _softplus(dtr_ref[...] + dtb_ref[...])
        dt_ref[...] = jnp.where(_iota(dt.shape, 1) < SSM_HEADS, dt, 0.0)

    return _call(body, name, (n // tile,),
                 [_rs(tile, w, 2), _rs(tile, w, 2, -1), _rs(tile, LANE), _ps(sw.shape), _ps((1, w)), _ps((1, LANE))],
                 [_rs(tile, w), _rs(tile, inner), _rs(tile, w - inner), _rs(tile, LANE)],
                 [S((n, w), F32), S((n, inner), F32), S((n, w - inner), F32), S((n, LANE), F32)],
                 scratch=[pltpu.VMEM((halo + tile, w), F32)], sem=("arbitrary",))(proj, proj, dtr, sw, sb, dtb)


def ssm_conv_bwd(dxs, dbc, pre, proj, sw, dproj, name):
    n = proj.shape[0]
    w = sw.shape[1]
    inner = SSM_HEADS * HEAD_DIM
    tile = _row_tile(n, 256)
    halo = 8
    nt = n // tile

    def body(dxs_ref, dxsn_ref, dbc_ref, dbcn_ref, pre_ref, pren_ref, x_ref, xp_ref, sw_ref, dp_in_ref,
             dx_ref, dsw_ref, dsb_ref, sc, sc_d):
        i = pl.program_id(0)

        def silu_bwd(d_, p_):
            s = _sigmoid(p_)
            return d_ * s * (1.0 + p_ * (1.0 - s))

        sc_d[0:tile, :inner] = silu_bwd(dxs_ref[...], pre_ref[:, :inner])
        sc_d[0:tile, inner:] = silu_bwd(dbc_ref[...], pre_ref[:, inner:])
        last = (i < nt - 1).astype(F32)
        sc_d[tile:, :inner] = silu_bwd(dxsn_ref[0:halo, :], pren_ref[0:halo, :inner]) * last
        sc_d[tile:, inner:] = silu_bwd(dbcn_ref[0:halo, :], pren_ref[0:halo, inner:]) * last
        sc[0:halo, :] = xp_ref[tile - halo:, :] * (i > 0).astype(F32)
        sc[halo:, :] = x_ref[...]
        dpre = sc_d[0:tile, :]
        def put(rows, acc):
            dx_ref[rows, :] = acc.astype(dx_ref.dtype)

        _taps_bwd_x(sc_d, sw_ref, SSM_CONV, tile, put)
        _taps_bwd_w(dpre, sc, dsw_ref, SSM_CONV, halo, tile, i)
        _acc(dsb_ref, i, jnp.sum(dpre, axis=0, keepdims=True))

    return pl.pallas_call(
        body, name=name, grid=(nt,),
        in_specs=[_rs(tile, inner), _rs(tile, inner, 0, 1, nt), _rs(tile, w - inner), _rs(tile, w - inner, 0, 1, nt),
                  _rs(tile, w), _rs(tile, w, 0, 1, nt), _rs(tile, w, 2), _rs(tile, w, 2, -1), _ps(sw.shape), ANY],
        out_specs=[_rs(tile, w, 2), _ps(sw.shape), _ps((1, w))],
        out_shape=[S(dproj.shape, dproj.dtype), S(sw.shape, F32), S((1, w), F32)],
        scratch_shapes=[pltpu.VMEM((halo + tile, w), F32), pltpu.VMEM((tile + halo, w), F32)],
        input_output_aliases={9: 0},
        compiler_params=pltpu.CompilerParams(dimension_semantics=("arbitrary",), vmem_limit_bytes=VMEM_LIMIT),
    )(dxs, dxs, dbc, dbc, pre, pre, proj, proj, sw, dproj)


def _ssd_prologue(dt_ref, dtT_ref, al_ref, alc_ref):
    row = _iota((CHUNK, CHUNK), 0)
    col = _iota((CHUNK, CHUNK), 1)
    dt = dt_ref[:, :SSM_HEADS]
    a_row = -jnp.exp(al_ref[:, :SSM_HEADS])
    a_col = -jnp.exp(alc_ref[...])
    cs = _01mm((row >= col).astype(F32), dt * a_row)
    csT = _mm01(dtT_ref[...] * a_col, (row <= col).astype(F32))
    return dt, a_row, cs, csT, row, col


def _decay(cs, csT, h, row, col):
    lm = jnp.exp(jnp.where(row >= col, cs[:, h:h + 1] - csT[h:h + 1, :], -1e30))
    lmT = jnp.exp(jnp.where(col >= row, csT[h:h + 1, :] - cs[:, h:h + 1], -1e30))
    return lm, lmT


def ssd_fwd(xs, bc, dt, dtT, alog_row, alog_col, name):
    n, width = xs.shape
    nc = n // CHUNK
    gw = width // SSM_GROUPS
    hpg = SSM_HEADS // SSM_GROUPS
    ns = SSM_STATE

    def body(xs_ref, bc_ref, dt_ref, dtT_ref, al_ref, alc_ref, y_ref, hs_ref, h_sc):
        i = pl.program_id(0)

        @pl.when(i == 0)
        def _():
            h_sc[...] = jnp.zeros_like(h_sc)

        dt, a_row, cs, csT, row, col = _ssd_prologue(dt_ref, dtT_ref, al_ref, alc_ref)
        indT = _head_indicator(width, SSM_HEADS, transposed=True)
        dt_full = _mm01(dt, indT)
        e_full = jnp.exp(_mm01(cs, indT))
        dte_full = jnp.exp(_mm01(cs[CHUNK - 1:CHUNK, :] - cs, indT))
        xt = xs_ref[...] * dt_full
        hs_ref[0] = h_sc[...]
        lo = _iota((CHUNK, 2 * HEAD_DIM), 1) < HEAD_DIM
        groups = range(SSM_GROUPS)
        bgs = [bc_ref[:, g * ns:(g + 1) * ns] for g in groups]
        cgs = [bc_ref[:, (SSM_GROUPS + g) * ns:(SSM_GROUPS + g + 1) * ns] for g in groups]
        gms = [_mm(cg, bg, NT) for cg, bg in zip(cgs, bgs)]
        yoffs = [e_full[:, g * gw:(g + 1) * gw] * _mm(cgs[g], h_sc[g * gw:(g + 1) * gw, :], NT) for g in groups]
        sgs = [_mm(xt[:, g * gw:(g + 1) * gw] * dte_full[:, g * gw:(g + 1) * gw], bgs[g], TN) for g in groups]
        ms = [gms[h // hpg] * _decay(cs, csT, h, row, col)[0] for h in range(SSM_HEADS)]
        for pr in range(SSM_HEADS // 2):
            c0 = 2 * pr * HEAD_DIM
            xp = xt[:, c0:c0 + 2 * HEAD_DIM]
            yd = jnp.where(lo, _mm(ms[2 * pr], xp), _mm(ms[2 * pr + 1], xp))
            y_ref[:, c0:c0 + 2 * HEAD_DIM] = yd + yoffs[2 * pr // hpg][:, c0 % gw:c0 % gw + 2 * HEAD_DIM]
        for h in range(SSM_HEADS):
            r0 = h * HEAD_DIM
            h_sc[r0:r0 + HEAD_DIM, :] = (h_sc[r0:r0 + HEAD_DIM, :] * jnp.exp(csT[h:h + 1, CHUNK - 1:CHUNK])
                                         + sgs[h // hpg][r0 % gw:r0 % gw + HEAD_DIM, :])

    bcw = bc.shape[1]
    return _call(body, name, (nc,),
                 [_rs(CHUNK, width), _rs(CHUNK, bcw), _rs(CHUNK, LANE), pl.BlockSpec((SSM_HEADS, CHUNK), lambda i: (0, i)),
                  _ps((1, LANE)), _ps((SSM_HEADS, 1))],
                 [_rs(CHUNK, width), pl.BlockSpec((1, width, ns), lambda i: (i, 0, 0))],
                 [S((n, width), F32), S((nc, width, ns), F32)],
                 scratch=[pltpu.VMEM((width, ns), F32)], sem=("arbitrary",))(xs, bc, dt, dtT, alog_row, alog_col)


def ssd_bwd(xs, bc, dt, dtT, alog_row, alog_col, hs, dy, dxs_skip, name):
    n, width = xs.shape
    nc = n // CHUNK
    gw = width // SSM_GROUPS
    hpg = SSM_HEADS // SSM_GROUPS
    ns = SSM_STATE
    bcw = bc.shape[1]

    def body(xs_ref, bc_ref, dt_ref, dtT_ref, al_ref, alc_ref, hs_ref, dy_ref, skip_ref,
             dxs_ref, dbc_ref, ddtr_ref, dal_ref, ddtb_ref, dh_sc, dxt_sc):
        i = pl.program_id(0)

        @pl.when(i == 0)
        def _():
            dh_sc[...] = jnp.zeros_like(dh_sc)

        dt, a_row, cs, csT, row, col = _ssd_prologue(dt_ref, dtT_ref, al_ref, alc_ref)
        indT = _head_indicator(width, SSM_HEADS, transposed=True)
        ind = _head_indicator(width, SSM_HEADS)
        dt_full = _mm01(dt, indT)
        e_full = jnp.exp(_mm01(cs, indT))
        cs_last = cs[CHUNK - 1:CHUNK, :]
        dte = jnp.exp(cs_last - cs)
        dte_full = _mm01(dte, indT)
        xs_ = xs_ref[...]
        xt = xs_ * dt_full
        dy_ = dy_ref[...]
        hmat = hs_ref[0]
        ds = dh_sc[...]
        lo = _iota((CHUNK, 2 * HEAD_DIM), 1) < HEAD_DIM
        head_lane = _iota((1, SSM_HEADS), 1)
        dcs = jnp.zeros((CHUNK, SSM_HEADS), F32)
        ddte = jnp.zeros((CHUNK, SSM_HEADS), F32)
        for g in range(SSM_GROUPS):
            sl = slice(g * gw, (g + 1) * gw)
            bg = bc_ref[:, g * ns:(g + 1) * ns]
            cg = bc_ref[:, (SSM_GROUPS + g) * ns:(SSM_GROUPS + g + 1) * ns]
            gm = _mm(cg, bg, NT)
            gmT = _mm(bg, cg, NT)
            hg = hmat[sl, :]
            dsg = ds[sl, :]
            dyg = dy_[:, sl]
            xtg = xt[:, sl]
            yoff = e_full[:, sl] * _mm(cg, hg, NT)
            edy = e_full[:, sl] * dyg
            bds = _mm(bg, dsg, NT)
            dxt_g = dte_full[:, sl] * bds
            ddte = ddte + _mm01(xtg * bds, ind[sl, :])
            dcs = dcs + _mm01(dyg * yoff, ind[sl, :])
            db = _mm(xtg * dte_full[:, sl], dsg)
            dc = _mm(edy, hg)
            dhc = _mm(edy, cg, TN)
            hs_g = range(g * hpg, (g + 1) * hpg)
            pair = lambda a, h: a[:, (h - g * hpg) // 2 * 2 * HEAD_DIM:((h - g * hpg) // 2 + 1) * 2 * HEAD_DIM]
            decs = [_decay(cs, csT, h, row, col) for h in hs_g]
            xms = [jnp.where(lo if h % 2 == 0 else jnp.logical_not(lo), pair(xtg, h), 0.0) for h in hs_g]
            dms = [_mm(pair(dyg, h), xm, NT) for h, xm in zip(hs_g, xms)]
            dmTs = [_mm(xm, pair(dyg, h), NT) for h, xm in zip(hs_g, xms)]
            mTs = [gmT * lmT for _, lmT in decs]
            for h, (lm, _), dm, dmT, mT in zip(hs_g, decs, dms, dmTs, mTs):
                z = jnp.sum(dm * (gm * lm), axis=1, keepdims=True) - jnp.sum(dmT * mT, axis=1, keepdims=True)
                dcs = dcs + z * (head_lane == h).astype(F32)
            dgs = sum(dm * lm for dm, (lm, _) in zip(dms, decs))
            dgTs = sum(dmT * lmT for dmT, (_, lmT) in zip(dmTs, decs))
            rrs = [_mm(mT, pair(dyg, h)) for h, mT in zip(hs_g, mTs)]
            for pr in range(hpg // 2):
                c0 = 2 * pr * HEAD_DIM
                dxt_sc[:, g * gw + c0:g * gw + c0 + 2 * HEAD_DIM] = (jnp.where(lo, rrs[2 * pr], rrs[2 * pr + 1])
                                                                     + dxt_g[:, c0:c0 + 2 * HEAD_DIM])
            dbc_ref[:, g * ns:(g + 1) * ns] = db + _mm(dgTs, cg)
            dbc_ref[:, (SSM_GROUPS + g) * ns:(SSM_GROUPS + g + 1) * ns] = dc + _mm(dgs, bg)
            for hh in range(hpg):
                h = g * hpg + hh
                r0 = h * HEAD_DIM
                dh_sc[r0:r0 + HEAD_DIM, :] = (dhc[hh * HEAD_DIM:(hh + 1) * HEAD_DIM, :]
                                              + jnp.exp(csT[h:h + 1, CHUNK - 1:CHUNK]) * ds[r0:r0 + HEAD_DIM, :])
        t = ddte * dte
        per_head = jnp.sum(jnp.sum(ds * hmat, axis=1, keepdims=True) * ind, axis=0, keepdims=True)
        last_add = jnp.sum(t, axis=0, keepdims=True) + jnp.exp(cs_last) * per_head
        dcs = dcs - t + jnp.where(_iota((CHUNK, SSM_HEADS), 0) == CHUNK - 1, last_add, 0.0)
        dadt = _01mm((row <= col).astype(F32), dcs)
        dxt = dxt_sc[...]
        ddt = dadt * a_row + _mm01(dxt * xs_, ind)
        dxs_ref[...] = dxt * dt_full + skip_ref[...]
        ddtr = ddt * (1.0 - jnp.exp(-dt))
        ddtr_ref[...] = jnp.zeros_like(ddtr_ref)
        ddtr_ref[:, :SSM_HEADS] = ddtr.astype(ddtr_ref.dtype)
        _acc(dal_ref, i, jnp.sum(dadt * dt, axis=0, keepdims=True) * a_row)
        _acc(ddtb_ref, i, jnp.sum(ddtr, axis=0, keepdims=True))

    rev = lambda i: (nc - 1 - i, 0)
    return _call(body, name, (nc,),
                 [pl.BlockSpec((CHUNK, width), rev), pl.BlockSpec((CHUNK, bcw), rev), pl.BlockSpec((CHUNK, LANE), rev),
                  pl.BlockSpec((SSM_HEADS, CHUNK), lambda i: (0, nc - 1 - i)), _ps((1, LANE)), _ps((SSM_HEADS, 1)),
                  pl.BlockSpec((1, width, ns), lambda i: (nc - 1 - i, 0, 0)), pl.BlockSpec((CHUNK, width), rev),
                  pl.BlockSpec((CHUNK, width), rev)],
                 [pl.BlockSpec((CHUNK, width), rev), pl.BlockSpec((CHUNK, bcw), rev), pl.BlockSpec((CHUNK, LANE), rev),
                  _ps((1, SSM_HEADS)), _ps((1, SSM_HEADS))],
                 [S((n, width), F32), S((n, bcw), F32), S((n, LANE), MXU_DTYPE), S((1, SSM_HEADS), F32), S((1, SSM_HEADS), F32)],
                 scratch=[pltpu.VMEM((width, ns), F32), pltpu.VMEM((CHUNK, width), F32)],
                 sem=("arbitrary",))(xs, bc, dt, dtT, alog_row, alog_col, hs, dy, dxs_skip)


def ssm_gate_fwd(yssd, xs, proj, dfull, gamma, name):
    n, d = yssd.shape
    tile = _row_tile(n, 512)
    gw = d // SSM_GROUPS

    def body(y_ref, xs_ref, z_ref, df_ref, gm_ref, o_ref):
        z = z_ref[...]
        y2 = (y_ref[...] + df_ref[...] * xs_ref[...]) * (z * _sigmoid(z))
        for g in range(SSM_GROUPS):
            yg = y2[:, g * gw:(g + 1) * gw]
            r = lax.rsqrt(jnp.mean(yg * yg, axis=-1, keepdims=True) + EPS)
            o_ref[:, g * gw:(g + 1) * gw] = (yg * r * gm_ref[:, g * gw:(g + 1) * gw]).astype(o_ref.dtype)

    return _call(body, name, (n // tile,), [_rs(tile, d), _rs(tile, d), _rs(tile, d, 2), _ps((1, d)), _ps((1, d))],
                 _rs(tile, d), S((n, d), MXU_DTYPE), sem=("parallel",))(yssd, xs, proj, dfull, gamma)


def ssm_gate_bwd(dy3, yssd, xs, proj, dfull, gamma, dproj, name):
    n, d = yssd.shape
    tile = _row_tile(n, 512)
    gw = d // SSM_GROUPS

    def body(dy_ref, y_ref, xs_ref, z_ref, df_ref, gm_ref, dp_in_ref, dys_ref, dxs_ref, dz_ref, dgm_ref, dd_ref):
        i = pl.program_id(0)
        z = z_ref[...]
        s = _sigmoid(z)
        xs_ = xs_ref[...]
        y1 = y_ref[...] + df_ref[...] * xs_
        y2 = y1 * (z * s)
        dy_ = dy_ref[...].astype(F32)
        dgm = []
        dy2 = []
        for g in range(SSM_GROUPS):
            sl = slice(g * gw, (g + 1) * gw)
            dxg, dgg = _rms_bwd_math(y2[:, sl], gm_ref[:, sl], dy_[:, sl])
            dy2.append(dxg)
            dgm.append(dgg)
        dy2 = jnp.concatenate(dy2, axis=1)
        dy1 = dy2 * (z * s)
        dys_ref[...] = dy1
        dxs_ref[...] = dy1 * df_ref[...]
        dz_ref[...] = (dy2 * y1 * s * (1.0 + z * (1.0 - s))).astype(dz_ref.dtype)
        _acc(dgm_ref, i, jnp.concatenate(dgm, axis=1))
        colsum = jnp.broadcast_to(jnp.sum(dy1 * xs_, axis=0, keepdims=True), (8, d))
        _acc(dd_ref, i, _mm01(colsum, _head_indicator(d, SSM_HEADS))[0:1, :])

    return pl.pallas_call(
        body, name=name, grid=(n // tile,),
        in_specs=[_rs(tile, d), _rs(tile, d), _rs(tile, d), _rs(tile, d, 2), _ps((1, d)), _ps((1, d)), ANY],
        out_specs=[_rs(tile, d), _rs(tile, d), _rs(tile, d, 2), _ps((1, d)), _ps((1, SSM_HEADS))],
        out_shape=[S((n, d), F32), S((n, d), F32), S(dproj.shape, dproj.dtype), S((1, d), F32), S((1, SSM_HEADS), F32)],
        input_output_aliases={6: 2},
        compiler_params=pltpu.CompilerParams(dimension_semantics=("arbitrary",), vmem_limit_bytes=VMEM_LIMIT),
    )(dy3, yssd, xs, proj, dfull, gamma, dproj)


def _rope128(x, cos, sin_signed):
    half = HEAD_DIM // 2
    lane = _iota(x.shape, 1)
    partner = jnp.where((lane % HEAD_DIM) < half, pltpu.roll(x, LANE - half, 1), pltpu.roll(x, half, 1))
    return x * cos + partner * sin_signed


def rope_fwd(qkv, cos, sin, name):
    n, w = qkv.shape
    qw = ATT_HEADS * HEAD_DIM
    kw = ATT_KV_HEADS * HEAD_DIM
    tile = _row_tile(n, 512)

    def body(x_ref, c_ref, s_ref, q_ref, k_ref, v_ref):
        c, s = c_ref[...], s_ref[...]
        for j in range(qw // LANE):
            q_ref[:, j * LANE:(j + 1) * LANE] = _rope128(x_ref[:, j * LANE:(j + 1) * LANE], c, s).astype(q_ref.dtype)
        for j in range(kw // LANE):
            k_ref[:, j * LANE:(j + 1) * LANE] = _rope128(x_ref[:, qw + j * LANE:qw + (j + 1) * LANE], c, s).astype(k_ref.dtype)
        v_ref[...] = x_ref[:, qw + kw:].astype(v_ref.dtype)

    return _call(body, name, (n // tile,), [_rs(tile, w), _rs(tile, LANE), _rs(tile, LANE)],
                 [_rs(tile, qw), _rs(tile, kw), _rs(tile, kw)],
                 [S((n, qw), MXU_DTYPE), S((n, kw), MXU_DTYPE), S((n, kw), MXU_DTYPE)], sem=("parallel",))(qkv, cos, sin)


ATT_GROUP = ATT_HEADS // ATT_KV_HEADS


def _attn_mask(i):
    row = _iota((ATT_GROUP * WINDOW, 2 * WINDOW), 0) % WINDOW
    s = _iota((ATT_GROUP * WINDOW, 2 * WINDOW), 1)
    return (s > row) & (s <= row + WINDOW) & ((s >= WINDOW) | (i > 0))


def _stack_heads(ref, j, kh, lo):
    parts = []
    for t in range(ATT_GROUP):
        h = ATT_GROUP * j + t
        blk = ref[:, (h // 2) * LANE:(h // 2 + 1) * LANE]
        blk = jnp.where(lo if h % 2 == 0 else jnp.logical_not(lo), blk, jnp.zeros_like(blk))
        parts.append(blk if h % 2 == kh else pltpu.roll(blk, HEAD_DIM, 1))
    return jnp.concatenate(parts, axis=0)


def _unstack_heads(stacked, j, kh, lo, put):
    for t in range(0, ATT_GROUP, 2):
        h = ATT_GROUP * j + t
        even = stacked[t * WINDOW:(t + 1) * WINDOW, :]
        odd = stacked[(t + 1) * WINDOW:(t + 2) * WINDOW, :]
        even = even if kh == 0 else pltpu.roll(even, HEAD_DIM, 1)
        odd = odd if kh == 1 else pltpu.roll(odd, HEAD_DIM, 1)
        put(h // 2, jnp.where(lo, even, odd))


def _per_head_rows(ref, j):
    return jnp.concatenate([ref[:, ATT_GROUP * j + t:ATT_GROUP * j + t + 1] for t in range(ATT_GROUP)], axis=0)


def _per_head_scalar(ref, j):
    rows = _iota((ATT_GROUP * WINDOW, 1), 0) // WINDOW
    out = jnp.zeros((ATT_GROUP * WINDOW, 1), F32)
    for t in range(ATT_GROUP):
        out = out + jnp.where(rows == t, ref[:, ATT_GROUP * j + t:ATT_GROUP * j + t + 1], 0.0)
    return out


def attn_fwd(q, k, v, sinks, name):
    n, qw = q.shape
    kw = k.shape[1]
    nb = n // WINDOW
    scale = HEAD_DIM ** -0.5

    def body(q_ref, kc_ref, kp_ref, vc_ref, vp_ref, sk_ref, o_ref, lse_ref):
        i = pl.program_id(0)
        valid = _attn_mask(i)
        lo = _iota((WINDOW, LANE), 1) < HEAD_DIM
        k2 = jnp.concatenate([kp_ref[...], kc_ref[...]], axis=0)
        v2 = jnp.concatenate([vp_ref[...], vc_ref[...]], axis=0)
        lane1 = _iota((1, LANE), 1)
        lse = jnp.zeros((WINDOW, LANE), F32)

        def put_o(qb, val):
            o_ref[:, qb * LANE:(qb + 1) * LANE] = val.astype(o_ref.dtype)

        kv = [(j, j // 2, j % 2) for j in range(ATT_KV_HEADS)]
        logits = [jnp.where(valid, _mm(_stack_heads(q_ref, j, kh, lo), k2[:, kb * LANE:(kb + 1) * LANE], NT) * scale, -1e30)
                  for j, kb, kh in kv]
        sks = [_per_head_scalar(sk_ref, j) for j, _, _ in kv]
        ms = [jnp.maximum(jnp.max(l, axis=-1, keepdims=True), sk) for l, sk in zip(logits, sks)]
        es = [jnp.exp(l - m) for l, m in zip(logits, ms)]
        dens = [jnp.sum(e, axis=-1, keepdims=True) + jnp.exp(sk - m) for e, sk, m in zip(es, sks, ms)]
        for (j, kb, kh), e, m, den in zip(kv, es, ms, dens):
            lse4 = m + jnp.log(den)
            for t in range(ATT_GROUP):
                lse = lse + lse4[t * WINDOW:(t + 1) * WINDOW, :] * (lane1 == ATT_GROUP * j + t).astype(F32)
            _unstack_heads(_mm(e * (1.0 / den), v2[:, kb * LANE:(kb + 1) * LANE]), j, kh, lo, put_o)
        lse_ref[...] = lse

    return _call(body, name, (nb,),
                 [_rs(WINDOW, qw), _rs(WINDOW, kw), _rs(WINDOW, kw, 0, -1), _rs(WINDOW, kw), _rs(WINDOW, kw, 0, -1), _ps((1, LANE))],
                 [_rs(WINDOW, qw), _rs(WINDOW, LANE)], [S((n, qw), MXU_DTYPE), S((n, LANE), F32)],
                 sem=("parallel",))(q, k, k, v, v, sinks)


def attn_bwd(q, k, v, o, do, lse, sinks, name):
    n, qw = q.shape
    kw = k.shape[1]
    nb = n // WINDOW
    scale = HEAD_DIM ** -0.5

    def body(q_ref, kc_ref, kp_ref, vc_ref, vp_ref, o_ref, do_ref, lse_ref, sk_ref,
             dq_ref, dka_ref, dkb_ref, dva_ref, dvb_ref, dsk_ref):
        i = pl.program_id(0)
        valid = _attn_mask(i)
        lo = _iota((WINDOW, LANE), 1) < HEAD_DIM
        k2 = jnp.concatenate([kp_ref[...], kc_ref[...]], axis=0)
        v2 = jnp.concatenate([vp_ref[...], vc_ref[...]], axis=0)
        lane1 = _iota((1, LANE), 1)
        do_ = do_ref[...].astype(F32)
        delta = _mm01(do_ * o_ref[...].astype(F32), _head_indicator(qw, ATT_HEADS))
        dk2 = [jnp.zeros((2 * WINDOW, LANE), F32) for _ in range(kw // LANE)]
        dv2 = [jnp.zeros((2 * WINDOW, LANE), F32) for _ in range(kw // LANE)]
        dsk = jnp.zeros((1, LANE), F32)

        def put_dq(qb, val):
            dq_ref[:, qb * LANE:(qb + 1) * LANE] = val

        kv = [(j, j // 2, j % 2) for j in range(ATT_KV_HEADS)]
        q4s = [_stack_heads(q_ref, j, kh, lo) for j, _, kh in kv]
        do4s = [_stack_heads(do_ref, j, kh, lo) for j, _, kh in kv]
        kks = [k2[:, kb * LANE:(kb + 1) * LANE] for _, kb, _ in kv]
        vvs = [v2[:, kb * LANE:(kb + 1) * LANE] for _, kb, _ in kv]
        lses = [_per_head_rows(lse_ref, j) for j, _, _ in kv]
        dls = [jnp.concatenate([delta[:, ATT_GROUP * j + t:ATT_GROUP * j + t + 1] for t in range(ATT_GROUP)], axis=0)
               for j, _, _ in kv]
        ps = [jnp.exp(jnp.where(valid, _mm(q4, kk, NT) * scale, -1e30) - lse4) for q4, kk, lse4 in zip(q4s, kks, lses)]
        dss = [p * (_mm(do4, vv, NT) - dl) * scale for p, do4, vv, dl in zip(ps, do4s, vvs, dls)]
        for (j, kb, kh), q4, do4, kk, lse4, dl, p, ds in zip(kv, q4s, do4s, kks, lses, dls, ps, dss):
            sd = jnp.exp(_per_head_scalar(sk_ref, j) - lse4) * dl
            for t in range(ATT_GROUP):
                dsk = dsk - (jnp.sum(sd[t * WINDOW:(t + 1) * WINDOW, :], axis=0, keepdims=True)
                             * (lane1 == ATT_GROUP * j + t).astype(F32))
            _unstack_heads(_mm(ds, kk), j, kh, lo, put_dq)
            dk2[kb] = dk2[kb] + _mm(ds, q4, TN)
            dv2[kb] = dv2[kb] + _mm(p, do4, TN)
        for kb in range(kw // LANE):
            dkb_ref[:, kb * LANE:(kb + 1) * LANE] = dk2[kb][0:WINDOW, :]
            dka_ref[:, kb * LANE:(kb + 1) * LANE] = dk2[kb][WINDOW:, :]
            dvb_ref[:, kb * LANE:(kb + 1) * LANE] = dv2[kb][0:WINDOW, :]
            dva_ref[:, kb * LANE:(kb + 1) * LANE] = dv2[kb][WINDOW:, :]
        _acc(dsk_ref, i, dsk)

    return _call(body, name, (nb,),
                 [_rs(WINDOW, qw), _rs(WINDOW, kw), _rs(WINDOW, kw, 0, -1), _rs(WINDOW, kw), _rs(WINDOW, kw, 0, -1),
                  _rs(WINDOW, qw), _rs(WINDOW, qw), _rs(WINDOW, LANE), _ps((1, LANE))],
                 [_rs(WINDOW, qw)] + [_rs(WINDOW, kw)] * 4 + [_ps((1, LANE))],
                 [S((n, qw), F32)] + [S((n, kw), F32)] * 4 + [S((1, LANE), F32)],
                 sem=("arbitrary",))(q, k, k, v, v, o, do, lse, sinks)


def attn_grad_merge(dq, dka, dkb, dva, dvb, cos, sin, name):
    n, qw = dq.shape
    kw = dka.shape[1]
    nb = n // WINDOW
    w = qw + 2 * kw

    def body(dq_ref, dka_ref, dkb_ref, dva_ref, dvb_ref, c_ref, s_ref, o_ref, db_ref):
        i = pl.program_id(0)
        c, s = c_ref[...], -s_ref[...]
        nxt = (i < nb - 1).astype(F32)

        @pl.when(i == 0)
        def _():
            db_ref[...] = jnp.zeros_like(db_ref)

        def put(c0, val):
            o_ref[:, c0:c0 + val.shape[1]] = val.astype(o_ref.dtype)
            db_ref[:, c0:c0 + val.shape[1]] += jnp.sum(val, axis=0, keepdims=True)

        for j in range(qw // LANE):
            put(j * LANE, _rope128(dq_ref[:, j * LANE:(j + 1) * LANE], c, s))
        for j in range(kw // LANE):
            sl = slice(j * LANE, (j + 1) * LANE)
            put(qw + j * LANE, _rope128(dka_ref[:, sl] + dkb_ref[:, sl] * nxt, c, s))
        put(qw + kw, dva_ref[...] + dvb_ref[...] * nxt)

    return _call(body, name, (nb,),
                 [_rs(WINDOW, qw), _rs(WINDOW, kw), _rs(WINDOW, kw, 0, 1, nb), _rs(WINDOW, kw), _rs(WINDOW, kw, 0, 1, nb),
                  _rs(WINDOW, LANE), _rs(WINDOW, LANE)],
                 [_rs(WINDOW, w), _ps((1, w))], [S((n, w), MXU_DTYPE), S((1, w), F32)],
                 sem=("arbitrary",))(dq, dka, dkb, dva, dvb, cos, sin)


def _row(v):
    return v.reshape(1, -1)


def _pad_lanes(v, width=LANE):
    return jnp.pad(v.reshape(1, -1), ((0, 0), (0, width - v.size)))


def chip_major_tn(a, a2, x, rows, name):
    t, m = a.shape
    d = x.shape[1]
    padded = -(-rows // 32) * 32
    assert m % 8 == 0 and (N_CHIPS - 1) * rows + padded <= m + a2.shape[1] and N_CHIPS * rows >= m
    tk, tn = _row_tile(t, 512), _pick_tile(d, 512)
    nk = t // tk

    def body(a_ref, a2_ref, x_ref, o_ref, acc_ref):
        k = pl.program_id(1)
        x = x_ref[...]
        top, bottom = _mm(a_ref[...], x, TN), _mm(a2_ref[...], x, TN)

        @pl.when(k == 0)
        def _():
            acc_ref[:m] = top
            acc_ref[m:] = bottom

        @pl.when(k > 0)
        def _():
            acc_ref[:m] += top
            acc_ref[m:] += bottom

        @pl.when(k == nk - 1)
        def _():
            keep = _iota((padded, tn), 0) < rows
            for j in range(N_CHIPS):
                o_ref[j] = jnp.where(keep, acc_ref[j * rows:j * rows + padded, :], 0.0).astype(o_ref.dtype)

    return _call(body, name, (d // tn, nk),
                 [pl.BlockSpec((tk, m), lambda n, k: (k, 0)), pl.BlockSpec((tk, a2.shape[1]), lambda n, k: (k, 0)),
                  pl.BlockSpec((tk, tn), lambda n, k: (k, n))],
                 pl.BlockSpec((N_CHIPS, padded, tn), lambda n, k: (0, 0, n)), S((N_CHIPS, padded, d), MXU_DTYPE),
                 scratch=[pltpu.VMEM((m + a2.shape[1], tn), F32)], sem=("parallel", "arbitrary"))(a, a2, x)


class LayerWeights(dict):
    def __init__(self, small, fetch):
        super().__init__(small)
        self.fetch = fetch

    def need(self, k, after):
        if k not in self:
            self[k] = self.fetch(k, after)
        return self[k]


def ffn_fwd(h, g, w, keys, tag):
    xn, u, hm = swiglu_in(h, _row(g), w.need(keys[0], h), f"{tag}_in")
    return matmul(hm, w.need(keys[1], hm), "nn", f"{tag}_out", scale=0.5, res=h), (h, xn, u, hm)


class GradSink:
    ORDER = ("ffn1_w_out", "ffn2_w_out", "ple_gate_w", "att_w_o", "hyb_w_out", "ffn1_w_in", "ffn2_w_in", "att_w_qkv",
             "ple_proj_w", "hyb_w_in")

    def __init__(self, shard_shapes, bucket_of):
        self.where, rows = {}, {}
        for k in self.ORDER:
            n, r, c = shard_shapes[k]
            for li in range(n):
                layer = li if k in PER_LAYER else 2 * li + (0 if k in EVEN_ONLY else 1)
                rows_b = rows.setdefault(bucket_of(layer, _stage(k)), {})
                key = c if r % 32 == 0 else k
                off = -(-rows_b.get(key, (0, c))[0] // r) * r
                rows_b[key] = (-(-(off + r) // 32) * 32, c)
                self.where[k, li] = (bucket_of(layer, _stage(k)), key, "r" if _shard_axis(k) == 1 else "c", off, r)
        self.bufs = {b: {key: lax.empty((N_CHIPS, r, c), MXU_DTYPE) for key, (r, c) in rows_b.items()}
                     for b, rows_b in rows.items()}

    def mm(self, k, li, a, b, name, scale=None, c0=0, paired=False):
        bucket, key, kind, off, r = self.where[k, li]
        buf = self.bufs[bucket][key]
        slot = Slot(buf, kind, r if kind == "r" else buf.shape[2], off, c0, paired)
        self.bufs[bucket][key] = matmul(a, b, "tn", name, scale=scale, into=slot)

    def put(self, k, li, chip_major):
        b, key = self.where[k, li][:2]
        assert (chip_major.shape, chip_major.dtype) == (self.bufs[b][key].shape, self.bufs[b][key].dtype)
        self.bufs[b][key] = chip_major


def ffn_bwd(dh, g, w_in, w_out, saved, tag, sink, keys, layer, after, colsum=False):
    h, xn, u, hm = saved
    sink.mm(keys[1], layer, hm, dh, f"{tag}_dwout", scale=0.5)
    du = swiglu_out_bwd(dh, w_out, u, after, f"{tag}_dhm")
    sink.mm(keys[0], layer, xn, du, f"{tag}_dwin", paired=True)
    outs = nt_rms_bwd(du, ColSharded(w_in.arr, paired=True), h, _row(g), dh, f"{tag}_dxn", colsum=colsum)
    return (outs[0], outs[1].reshape(-1)) + ((outs[2],) if colsum else ())


def _hyb_params(w):
    d = w["conv_dw_b"].size
    inner = SSM_HEADS * HEAD_DIM
    main = 3 * d + w["ssm_conv_b"].size
    return dict(
        w_main=w["hyb_w_in"], w_dt=jnp.pad(w["hyb_w_in"][main:], ((0, LANE - SSM_HEADS), (0, 0))),
        cw=jnp.pad(w["conv_dw_w"], ((0, 32 - CONV_WIDTH), (0, 0))), cb=_row(w["conv_dw_b"]),
        lg=_row(w["conv_ln_g"]), lb=_row(w["conv_ln_b"]),
        sw=jnp.pad(w["ssm_conv_w"], ((0, 8 - SSM_CONV), (0, 0))), sb=_row(w["ssm_conv_b"]),
        dtb=_pad_lanes(w["ssm_dt_bias"]), al_row=_pad_lanes(w["ssm_a_log"]), al_col=w["ssm_a_log"].reshape(-1, 1),
        dfull=_row(jnp.repeat(w["ssm_d"], HEAD_DIM)), gamma=_row(w["ssm_norm"]), d=d, inner=inner, main=main)


def hyb_fwd(h, w, tag):
    w.need("hyb_w_in", h)
    q = _hyb_params(w)
    proj, xn = rms_matmul(h, _row(w["norm_mix"]), q["w_main"], "nt", f"{tag}_in", n_out=q["main"])
    dtr = matmul(xn, q["w_dt"], "nt", f"{tag}_in_dt")
    u, u1 = conv_group_fwd(proj, q["cw"], q["cb"], q["lg"], q["lb"], f"{tag}_conv")
    pre, xs, bc, dt = ssm_conv_fwd(proj, dtr, q["sw"], q["sb"], q["dtb"], f"{tag}_sconv")
    dtT = dt[:, :SSM_HEADS].T
    yssd, hs = ssd_fwd(xs, bc, dt, dtT, q["al_row"], q["al_col"], f"{tag}_ssd")
    y = ssm_gate_fwd(yssd, xs, proj, q["dfull"], q["gamma"], f"{tag}_gate")
    wo = w.need("hyb_w_out", u)
    h2 = matmul(u, wo[:q["d"]], "nn", f"{tag}_out_a", res=h)
    h2 = matmul(y, wo[q["d"]:], "nn", f"{tag}_out_b", res=h2)
    return h2, (h, xn, proj, u, u1, pre, xs, bc, dt, dtT, yssd, hs, y)


def hyb_bwd(dh, w, saved, tag, sink, layer, after):
    q = _hyb_params(w)
    h, xn, proj, u, u1, pre, xs, bc, dt, dtT, yssd, hs, y = saved
    du = matmul(dh, w["hyb_w_out"][:q["d"]], "nt", f"{tag}_du")
    dy3 = matmul(dh, w["hyb_w_out"][q["d"]:], "nt", f"{tag}_dy")
    sink.mm("hyb_w_out", layer, u, dh, f"{tag}_dwo_a", c0=0)
    sink.mm("hyb_w_out", layer, y, dh, f"{tag}_dwo_b", c0=N_CHIPS // 2)
    dproj, dcw, dcb, dlg, dlb = conv_group_bwd(du, u1, proj, q["cw"], q["lg"], q["lb"], after, f"{tag}_dconv")
    dyssd, dxs_skip, dproj, dgamma, dd = ssm_gate_bwd(dy3, yssd, xs, proj, q["dfull"], q["gamma"], dproj, f"{tag}_dgate")
    dxs, dbc, ddtr, dalog, ddtb = ssd_bwd(xs, bc, dt, dtT, q["al_row"], q["al_col"], hs, dyssd, dxs_skip, f"{tag}_dssd")
    dproj, dsw, dsb = ssm_conv_bwd(dxs, dbc, pre, proj, q["sw"], dproj, f"{tag}_dsconv")
    sink.put("hyb_w_in", layer, chip_major_tn(dproj, ddtr, xn, (q["main"] + SSM_HEADS) // N_CHIPS, f"{tag}_dwin"))
    dh2, dg = nt_rms_bwd(dproj, q["w_main"], h, _row(w["norm_mix"]), dh, f"{tag}_dxn", extra=(ddtr, q["w_dt"]), b_kd=True)
    grads = dict(norm_mix=dg.reshape(-1), conv_dw_w=dcw[:CONV_WIDTH], conv_dw_b=dcb.reshape(-1),
                 conv_ln_g=dlg.reshape(-1), conv_ln_b=dlb.reshape(-1), ssm_conv_w=dsw[:SSM_CONV], ssm_conv_b=dsb.reshape(-1),
                 ssm_dt_bias=ddtb.reshape(-1), ssm_a_log=dalog.reshape(-1), ssm_d=dd.reshape(-1), ssm_norm=dgamma.reshape(-1))
    return dh2, grads


def rope_tables(n):
    half = HEAD_DIM // 2
    inv = ROPE_THETA ** (-jnp.arange(0, HEAD_DIM, 2, dtype=F32) / HEAD_DIM)
    ang = jnp.arange(n, dtype=F32)[:, None] * inv[None, :]
    cos, sin = jnp.cos(ang), jnp.sin(ang)
    reps = LANE // HEAD_DIM
    return jnp.tile(jnp.concatenate([cos, cos], axis=1), (1, reps)), jnp.tile(jnp.concatenate([-sin, sin], axis=1), (1, reps))


def att_fwd(h, w, tables, tag):
    cos, sin = tables
    qkv, xn = rms_matmul(h, _row(w["norm_mix"]), w.need("att_w_qkv", h), "nn", f"{tag}_qkv", bias=_row(w["att_b_qkv"]))
    q, k, v = rope_fwd(qkv, cos, sin, f"{tag}_rope")
    sinks = _pad_lanes(w["att_sinks"])
    o, lse = attn_fwd(q, k, v, sinks, f"{tag}_attn")
    h2 = matmul(o, w.need("att_w_o", o), "nn", f"{tag}_o", bias=_row(w["att_b_o"]), res=h)
    return h2, (h, xn, q, k, v, o, lse, sinks)


def att_bwd(dh, dh_colsum, w, saved, tables, tag, sink, layer):
    cos, sin = tables
    h, xn, q, k, v, o, lse, sinks = saved
    do = matmul(dh, w["att_w_o"], "nt", f"{tag}_do")
    sink.mm("att_w_o", layer, o, dh, f"{tag}_dwo")
    dq, dka, dkb, dva, dvb, dsk = attn_bwd(q, k, v, o, do, lse, sinks, f"{tag}_dattn")
    dqkv, dbqkv = attn_grad_merge(dq, dka, dkb, dva, dvb, cos, sin, f"{tag}_drope")
    sink.mm("att_w_qkv", layer, xn, dqkv, f"{tag}_dwqkv")
    dh2, dg = nt_rms_bwd(dqkv, w["att_w_qkv"], h, _row(w["norm_mix"]), dh, f"{tag}_dxn")
    grads = dict(norm_mix=dg.reshape(-1), att_b_qkv=dbqkv.reshape(-1), att_sinks=dsk[0, :ATT_HEADS],
                 att_b_o=dh_colsum.reshape(-1))
    return dh2, grads


def ple_block_fwd(h, pe, w, tag):
    pp = matmul(pe, w.need("ple_proj_w", h), "nn", f"{tag}_proj")
    out, gl, xn = ple_fwd(h, _row(w["ple_norm"]), w.need("ple_gate_w", h), pp, f"{tag}_gate")
    return out, (h, xn, gl, pp, pe)


def ple_block_bwd(dh, w, saved, tag, sink, layer, after):
    h, xn, gl, pp, pe = saved
    dh2, dpp, dgl, dg = ple_bwd(dh, gl, pp, w["ple_gate_w"], h, _row(w["ple_norm"]), after, f"{tag}_dgate")
    sink.mm("ple_proj_w", layer, pe, dpp, f"{tag}_dwp")
    sink.mm("ple_gate_w", layer, xn, dgl, f"{tag}_dwg")
    return dh2, dict(ple_norm=dg.reshape(-1))


PER_LAYER = ("norm_ffn1", "ffn1_w_in", "ffn1_w_out", "norm_mix", "norm_ffn2", "ffn2_w_in", "ffn2_w_out",
             "ple_norm", "ple_gate_w", "ple_proj_w")
EVEN_ONLY = ("hyb_w_in", "conv_dw_w", "conv_dw_b", "conv_ln_g", "conv_ln_b", "ssm_conv_w", "ssm_conv_b",
             "ssm_dt_bias", "ssm_a_log", "ssm_d", "ssm_norm", "hyb_w_out")
ODD_ONLY = ("att_w_qkv", "att_b_qkv", "att_sinks", "att_w_o", "att_b_o")


def _layer_index(k, i):
    if k in PER_LAYER:
        return i
    if k in (EVEN_ONLY if i % 2 == 0 else ODD_ONLY):
        return i // 2
    return None


def _stage(k):
    return 0 if k.startswith("ffn1") else (2 if k.startswith(("ffn2", "ple")) else 1)


def trunk_fwd_bwd(x, pe, target, layers, final_norm, sink, stage_done):
    depth = len(layers)
    tables = rope_tables(x.shape[0])
    h = x
    saved = []
    for i, w in enumerate(layers):
        h, s1 = ffn_fwd(h, w["norm_ffn1"], w, ("ffn1_w_in", "ffn1_w_out"), f"l{i}_ffn1")
        if i % 2 == 0:
            h, s2 = hyb_fwd(h, w, f"l{i}_hyb")
        else:
            h, s2 = att_fwd(h, w, tables, f"l{i}_att")
        h, s3 = ffn_fwd(h, w["norm_ffn2"], w, ("ffn2_w_in", "ffn2_w_out"), f"l{i}_ffn2")
        h, s4 = ple_block_fwd(h, pe[i], w, f"l{i}_ple")
        saved.append((s1, s2, s3, s4))
    dh, dgf, loss = loss_head(h, _row(final_norm), target, "loss_head")
    grads = {}
    tie = dgf
    for i in reversed(range(depth)):
        w = layers[i]
        s1, s2, s3, s4 = saved[i]
        dh, g = ple_block_bwd(dh, w, s4, f"l{i}_ple", sink, i, tie)
        odd = i % 2 == 1
        out = ffn_bwd(dh, w["norm_ffn2"], w["ffn2_w_in"], w["ffn2_w_out"], s3, f"l{i}_ffn2", sink,
                      ("ffn2_w_in", "ffn2_w_out"), i, tie, colsum=odd)
        dh = out[0]
        g.update(norm_ffn2=out[1])
        tie = stage_done(i, 2, tie)
        if odd:
            dh, gm = att_bwd(dh, out[2], w, s2, tables, f"l{i}_att", sink, i // 2)
        else:
            dh, gm = hyb_bwd(dh, w, s2, f"l{i}_hyb", sink, i // 2, tie)
        g.update(gm)
        tie = stage_done(i, 1, tie)
        out = ffn_bwd(dh, w["norm_ffn1"], w["ffn1_w_in"], w["ffn1_w_out"], s1, f"l{i}_ffn1", sink,
                      ("ffn1_w_in", "ffn1_w_out"), i, tie)
        dh = out[0]
        g.update(norm_ffn1=out[1])
        tie = stage_done(i, 0, tie)
        for k, v in g.items():
            grads.setdefault(k, []).insert(0, v)
    grads = {k: jnp.stack(v) for k, v in grads.items()}
    grads["final_norm"] = dgf.reshape(-1)
    return loss, dh, grads


def _me():
    return lax.axis_index("x"), lax.axis_index("y"), lax.axis_index("c")


def _flip(v, f):
    return 1 - v if f else v


def _remote(src, dst, send_sems, recv_sems, k, dev):
    return pltpu.make_async_remote_copy(src_ref=src, dst_ref=dst, send_sem=send_sems.at[k], recv_sem=recv_sems.at[k],
                                        device_id=dev, device_id_type=MESH)


CHIP_FLIPS = ((1, 0), (0, 1), (1, 1))
DEV_FLIPS = tuple((fx, fy, fc) for fx in (0, 1) for fy in (0, 1) for fc in (0, 1))[1:]


HBM = pl.BlockSpec(memory_space=pltpu.HBM)
SEM = pl.BlockSpec(memory_space=pltpu.SEMAPHORE)
DATAFLOW = pltpu.SideEffectType.DATAFLOW_SIDE_EFFECTING


def _core_half(ref, axis, c):
    if axis is None:
        return ref
    h = ref.shape[axis] // 2
    return ref.at[pl.ds(c * h, h), :] if axis == 0 else ref.at[:, pl.ds(c * h, h)]


def gather_start(xs, lands, halves, after, name):
    na = len(xs)

    def body(*refs):
        x_refs, land_refs = refs[:na], refs[na:2 * na]
        send_sems, recv_sems = refs[2 * na + 1], refs[2 * na + 2]
        token = refs[-1]
        mx, my, mc = _me()
        chip = 2 * mx + my
        for a in range(na):
            for j, (fx, fy) in enumerate(CHIP_FLIPS):
                _remote(_core_half(x_refs[a], halves[a], mc), _core_half(land_refs[a].at[chip], halves[a], mc),
                        send_sems, recv_sems, 3 * a + j, (_flip(mx, fx), _flip(my, fy), mc)).start()
        token[...] = jnp.zeros_like(token)

    outs = pl.pallas_call(
        body, name=name,
        out_shape=(pltpu.SemaphoreType.DMA((3 * na,)), pltpu.SemaphoreType.DMA((3 * na,)))
        + tuple(pltpu.HBM(x.shape, x.dtype) for x in xs) + tuple(pltpu.HBM(l.shape, l.dtype) for l in lands)
        + (S((8, LANE), F32),),
        in_specs=[HBM] * (2 * na) + [pl.BlockSpec(memory_space=pl.ANY)],
        out_specs=(SEM, SEM) + (HBM,) * (2 * na) + (pl.BlockSpec(memory_space=pltpu.VMEM),),
        input_output_aliases={a: 2 + a for a in range(2 * na)},
        compiler_params=pltpu.CompilerParams(has_side_effects=DATAFLOW),
    )(*[pltpu.with_memory_space_constraint(t, pltpu.HBM) for t in list(xs) + list(lands)], after)
    return outs[0], outs[1], list(outs[2:2 + na]), list(outs[2 + na:2 + 2 * na])


def gather_wait(send_sems, recv_sems, xs, lands, halves, first, after, name):
    na = len(xs)

    def body(*refs):
        x_refs, land_refs = refs[:na], refs[na:2 * na]
        send_sems, recv_sems = refs[2 * na], refs[2 * na + 1]
        mx, my, mc = _me()
        for a in range(na):
            for j, (fx, fy) in enumerate(CHIP_FLIPS):
                px, py = _flip(mx, fx), _flip(my, fy)
                cp = _remote(_core_half(x_refs[a], halves[a], mc), _core_half(land_refs[a].at[2 * px + py], halves[a], mc),
                             send_sems, recv_sems, 3 * (first + a) + j, (px, py, mc))
                cp.wait_send()
                cp.wait_recv()

    outs = pl.pallas_call(
        body, name=name,
        out_shape=tuple(pltpu.HBM(x.shape, x.dtype) for x in xs) + tuple(pltpu.HBM(l.shape, l.dtype) for l in lands),
        in_specs=[HBM] * (2 * na) + [SEM, SEM, pl.BlockSpec(memory_space=pl.ANY)], out_specs=(HBM,) * (2 * na),
        input_output_aliases={a: a for a in range(2 * na)},
        compiler_params=pltpu.CompilerParams(has_side_effects=DATAFLOW),
    )(*xs, *lands, send_sems, recv_sems, after)
    return list(outs[na:])


def forward_halves(land, axis, name):
    def body(in_ref, out_ref, send_sems, recv_sems):
        del in_ref
        mx, my, mc = _me()
        sib = (mx, my, 1 - mc)
        slots = [2 * _flip(mx, fx) + _flip(my, fy) for fx, fy in CHIP_FLIPS]
        cps = [_remote(_core_half(out_ref.at[s], axis, mc), _core_half(out_ref.at[s], axis, mc), send_sems, recv_sems, j, sib)
               for j, s in enumerate(slots)]
        for cp in cps:
            cp.start()
        for j, s in enumerate(slots):
            _remote(_core_half(out_ref.at[s], axis, mc), _core_half(out_ref.at[s], axis, 1 - mc), send_sems, recv_sems, j, sib).wait_recv()
        for cp in cps:
            cp.wait_send()

    return pl.pallas_call(
        body, name=name, out_shape=S(land.shape, land.dtype), in_specs=[ANY], out_specs=ANY, input_output_aliases={0: 0},
        scratch_shapes=[pltpu.SemaphoreType.DMA((3,)), pltpu.SemaphoreType.DMA((3,))])(land)


def all_gather_devices(v, name):
    r, l = v.shape

    def body(v_ref, out_ref, send_sems, recv_sems):
        mx, my, mc = _me()
        me = 4 * mx + 2 * my + mc
        peers = [(_flip(mx, fx), _flip(my, fy), _flip(mc, fc)) for fx, fy, fc in DEV_FLIPS]
        sends = [_remote(v_ref, out_ref.at[me], send_sems, recv_sems, j, p) for j, p in enumerate(peers)]
        for cp in sends:
            cp.start()
        for j, (px, py, pc) in enumerate(peers):
            _remote(v_ref, out_ref.at[4 * px + 2 * py + pc], send_sems, recv_sems, j, (px, py, pc)).wait_recv()
        for cp in sends:
            cp.wait_send()

    out = pl.pallas_call(
        body, name=name, out_shape=S((N_DEV, r, l), v.dtype), in_specs=[ANY], out_specs=ANY,
        scratch_shapes=[pltpu.SemaphoreType.DMA((7,)), pltpu.SemaphoreType.DMA((7,))])(v)
    me = 4 * lax.axis_index("x") + 2 * lax.axis_index("y") + lax.axis_index("c")
    return lax.dynamic_update_slice_in_dim(out, v[None], me, axis=0)


def gather_devices_start(v, name):
    me = 4 * lax.axis_index("x") + 2 * lax.axis_index("y") + lax.axis_index("c")
    land = lax.dynamic_update_slice_in_dim(lax.empty((N_DEV,) + v.shape, v.dtype), v[None], me, axis=0)

    def body(v_ref, land_ref, send_sems, recv_sems, v_thru, land_thru, token):
        mx, my, mc = _me()
        for j, (fx, fy, fc) in enumerate(DEV_FLIPS):
            _remote(v_ref, land_ref.at[4 * mx + 2 * my + mc], send_sems, recv_sems, j,
                    (_flip(mx, fx), _flip(my, fy), _flip(mc, fc))).start()
        token[...] = jnp.zeros_like(token)

    outs = pl.pallas_call(
        body, name=name,
        out_shape=(pltpu.SemaphoreType.DMA((7,)), pltpu.SemaphoreType.DMA((7,)), pltpu.HBM(v.shape, v.dtype),
                   pltpu.HBM(land.shape, land.dtype), S((8, LANE), F32)),
        in_specs=[HBM, HBM], out_specs=(SEM, SEM, HBM, HBM, pl.BlockSpec(memory_space=pltpu.VMEM)),
        input_output_aliases={0: 2, 1: 3}, compiler_params=pltpu.CompilerParams(has_side_effects=DATAFLOW),
    )(pltpu.with_memory_space_constraint(v, pltpu.HBM), pltpu.with_memory_space_constraint(land, pltpu.HBM))
    return outs[:4]


def gather_devices_wait(send_sems, recv_sems, v, land, after, name):
    def body(v_ref, land_ref, send_sems, recv_sems, after_ref, v_dead, got_ref):
        mx, my, mc = _me()
        for j, (fx, fy, fc) in enumerate(DEV_FLIPS):
            px, py, pc = _flip(mx, fx), _flip(my, fy), _flip(mc, fc)
            cp = _remote(v_ref, land_ref.at[4 * px + 2 * py + pc], send_sems, recv_sems, j, (px, py, pc))
            cp.wait_send()
            cp.wait_recv()

    return pl.pallas_call(
        body, name=name, out_shape=(pltpu.HBM(v.shape, v.dtype), pltpu.HBM(land.shape, land.dtype)),
        in_specs=[HBM, HBM, SEM, SEM, pl.BlockSpec(memory_space=pl.ANY)], out_specs=(HBM, HBM),
        input_output_aliases={0: 0, 1: 1}, compiler_params=pltpu.CompilerParams(has_side_effects=DATAFLOW),
    )(v, land, send_sems, recv_sems, after)[1]


def sum_devices(g8, name):
    nd, r, l = g8.shape
    tile = r
    for t in (512, 256, 128, 64, 32, 16, 8):
        if r % t == 0:
            tile = t
            break

    def body(g_ref, o_ref):
        acc = g_ref[0]
        for d in range(1, nd):
            acc = acc + g_ref[d]
        o_ref[...] = acc

    return _call(body, name, (r // tile,), [pl.BlockSpec((nd, tile, l), lambda i: (0, i, 0))], _rs(tile, l), S((r, l), F32),
                 sem=("parallel",))(g8)


def exchange_halves(gs, name):
    na = len(gs)
    nch = gs[0].shape[0]

    def body(*refs):
        g_refs, out_refs = refs[:na], refs[na:2 * na]
        send_sems, recv_sems = refs[2 * na:]
        mx, my, mc = _me()
        sib = (mx, my, 1 - mc)
        cps = []
        for a in range(na):
            half = gs[a].shape[1] // 2
            for j in range(nch):
                cps.append(_remote(g_refs[a].at[j, pl.ds((1 - mc) * half, half), :], out_refs[a].at[j],
                                   send_sems, recv_sems, nch * a + j, sib))
        for cp in cps:
            cp.start()
        for cp in cps:
            cp.wait_recv()
        for cp in cps:
            cp.wait_send()

    return pl.pallas_call(
        body, name=name, out_shape=[S((nch, g.shape[1] // 2, g.shape[2]), g.dtype) for g in gs],
        in_specs=[ANY] * na, out_specs=[ANY] * na,
        scratch_shapes=[pltpu.SemaphoreType.DMA((nch * na,)), pltpu.SemaphoreType.DMA((nch * na,))])(*gs)


def exchange_halves_start(gs, name):
    na = len(gs)
    nch = gs[0].shape[0]
    lands = [lax.empty((nch, g.shape[1] // 2, g.shape[2]), g.dtype) for g in gs]

    def body(*refs):
        g_refs, land_refs = refs[:na], refs[na:2 * na]
        send_sems, recv_sems = refs[2 * na], refs[2 * na + 1]
        mx, my, mc = _me()
        for a in range(na):
            half = gs[a].shape[1] // 2
            for j in range(nch):
                _remote(g_refs[a].at[j, pl.ds((1 - mc) * half, half), :], land_refs[a].at[j], send_sems, recv_sems,
                        nch * a + j, (mx, my, 1 - mc)).start()
        refs[-1][...] = jnp.zeros_like(refs[-1])

    outs = pl.pallas_call(
        body, name=name,
        out_shape=(pltpu.SemaphoreType.DMA((nch * na,)), pltpu.SemaphoreType.DMA((nch * na,)))
        + tuple(pltpu.HBM(t.shape, t.dtype) for t in list(gs) + lands) + (S((8, LANE), F32),),
        in_specs=[HBM] * (2 * na), out_specs=(SEM, SEM) + (HBM,) * (2 * na) + (pl.BlockSpec(memory_space=pltpu.VMEM),),
        input_output_aliases={a: 2 + a for a in range(2 * na)},
        compiler_params=pltpu.CompilerParams(has_side_effects=DATAFLOW),
    )(*[pltpu.with_memory_space_constraint(t, pltpu.HBM) for t in list(gs) + lands])
    return outs[0], outs[1], list(outs[2:2 + na]), list(outs[2 + na:2 + 2 * na]), outs[-1]


def exchange_halves_wait(send_sems, recv_sems, gs, lands, after, name):
    na = len(gs)
    nch = gs[0].shape[0]

    def body(*refs):
        g_refs, land_refs = refs[:na], refs[na:2 * na]
        send_sems, recv_sems = refs[2 * na], refs[2 * na + 1]
        mx, my, mc = _me()
        for a in range(na):
            half = gs[a].shape[1] // 2
            for j in range(nch):
                cp = _remote(g_refs[a].at[j, pl.ds((1 - mc) * half, half), :], land_refs[a].at[j], send_sems, recv_sems,
                             nch * a + j, (mx, my, 1 - mc))
                cp.wait_send()
                cp.wait_recv()

    outs = pl.pallas_call(
        body, name=name, out_shape=tuple(pltpu.HBM(t.shape, t.dtype) for t in list(gs) + list(lands)),
        in_specs=[HBM] * (2 * na) + [SEM, SEM, pl.BlockSpec(memory_space=pl.ANY)], out_specs=(HBM,) * (2 * na),
        input_output_aliases={a: a for a in range(2 * na)},
        compiler_params=pltpu.CompilerParams(has_side_effects=DATAFLOW),
    )(*gs, *lands, send_sems, recv_sems, after)
    return list(outs[:na]), list(outs[na:])


def add_halves(g4, got, name):
    nch, r, l = g4.shape
    half = r // 2
    tile = _pick_rows(half)
    nt = half // tile

    def body(g_ref, r_ref, a_ref, own_ref):
        j = pl.program_id(1)
        chip = 2 * lax.axis_index("x") + lax.axis_index("y")
        val = g_ref[0].astype(F32) + r_ref[0].astype(F32)
        a_ref[0] = val.astype(a_ref.dtype)

        @pl.when(j == chip)
        def _():
            own_ref[...] = val

    return pl.pallas_call(
        body, name=name, grid=(nt, nch),
        in_specs=[pl.BlockSpec((1, tile, l), lambda i, j: (j, lax.axis_index("c") * nt + i, 0)),
                  pl.BlockSpec((1, tile, l), lambda i, j: (j, i, 0))],
        out_specs=[pl.BlockSpec((1, tile, l), lambda i, j: (j, i, 0)), pl.BlockSpec((tile, l), lambda i, j: (i, 0))],
        out_shape=[S((nch, half, l), MXU_DTYPE), S((half, l), F32)],
        compiler_params=pltpu.CompilerParams(dimension_semantics=("parallel", "arbitrary"), vmem_limit_bytes=VMEM_LIMIT))(g4, got)


def _pick_rows(r, cap=640):
    return next((t for t in range(cap - cap % 16, 15, -16) if r % t == 0), r)


def add_chips(own, got, name):
    h, l = own.shape
    tile = _pick_rows(h)

    def body(o_ref, g_ref, out_ref):
        out_ref[...] = ((o_ref[...] + g_ref[0].astype(F32)) + g_ref[1].astype(F32)) + g_ref[2].astype(F32)

    nt = h // tile
    return _call(body, name, (nt,), [_rs(tile, l), pl.BlockSpec((3, tile, l), lambda i: (0, i, 0))],
                 pl.BlockSpec((tile, l), lambda i: (lax.axis_index("c") * nt + i, 0)),
                 S((2 * h, l), F32), sem=("parallel",))(own, got)


def join_halves(bufs, name):
    na = len(bufs)

    def body(*refs):
        out_refs = refs[na:2 * na]
        send_sems, recv_sems = refs[2 * na:]
        mx, my, mc = _me()
        sib = (mx, my, 1 - mc)

        def half(a, hc):
            h = bufs[a].shape[0] // 2
            return out_refs[a].at[pl.ds(hc * h, h), :]

        cps = [_remote(half(a, mc), half(a, mc), send_sems, recv_sems, a, sib) for a in range(na)]
        for cp in cps:
            cp.start()
        for a in range(na):
            _remote(half(a, mc), half(a, 1 - mc), send_sems, recv_sems, a, sib).wait_recv()
        for cp in cps:
            cp.wait_send()

    return pl.pallas_call(
        body, name=name, out_shape=[S(b.shape, b.dtype) for b in bufs], in_specs=[ANY] * na, out_specs=[ANY] * na,
        input_output_aliases={a: a for a in range(na)},
        scratch_shapes=[pltpu.SemaphoreType.DMA((na,)), pltpu.SemaphoreType.DMA((na,))])(*bufs)


def exchange_chips_start(parts, name):
    na = len(parts)
    lands = [lax.empty((3,) + p.shape[1:], p.dtype) for p in parts]

    def body(*refs):
        a_refs, land_refs = refs[:na], refs[na:2 * na]
        send_sems, recv_sems = refs[2 * na], refs[2 * na + 1]
        mx, my, mc = _me()
        for j, (fx, fy) in enumerate(CHIP_FLIPS):
            px, py = _flip(mx, fx), _flip(my, fy)
            for a in range(na):
                _remote(a_refs[a].at[2 * px + py], land_refs[a].at[j], send_sems, recv_sems, 3 * a + j, (px, py, mc)).start()
        refs[-1][...] = jnp.zeros_like(refs[-1])

    outs = pl.pallas_call(
        body, name=name,
        out_shape=(pltpu.SemaphoreType.DMA((3 * na,)), pltpu.SemaphoreType.DMA((3 * na,)))
        + tuple(pltpu.HBM(t.shape, t.dtype) for t in list(parts) + lands) + (S((8, LANE), F32),),
        in_specs=[HBM] * (2 * na), out_specs=(SEM, SEM) + (HBM,) * (2 * na) + (pl.BlockSpec(memory_space=pltpu.VMEM),),
        input_output_aliases={a: 2 + a for a in range(2 * na)},
        compiler_params=pltpu.CompilerParams(has_side_effects=DATAFLOW),
    )(*[pltpu.with_memory_space_constraint(t, pltpu.HBM) for t in list(parts) + lands])
    return outs[0], outs[1], list(outs[2:2 + na]), list(outs[2 + na:2 + 2 * na]), outs[-1]


def exchange_chips_wait(send_sems, recv_sems, parts, lands, after, name):
    na = len(parts)

    def body(*refs):
        a_refs, land_refs = refs[:na], refs[na:2 * na]
        send_sems, recv_sems = refs[2 * na], refs[2 * na + 1]
        mx, my, mc = _me()
        for j, (fx, fy) in enumerate(CHIP_FLIPS):
            px, py = _flip(mx, fx), _flip(my, fy)
            for a in range(na):
                cp = _remote(a_refs[a].at[2 * px + py], land_refs[a].at[j], send_sems, recv_sems, 3 * a + j, (px, py, mc))
                cp.wait_send()
                cp.wait_recv()

    outs = pl.pallas_call(
        body, name=name, out_shape=tuple(pltpu.HBM(t.shape, t.dtype) for t in list(parts) + list(lands)),
        in_specs=[HBM] * (2 * na) + [SEM, SEM, pl.BlockSpec(memory_space=pl.ANY)], out_specs=(HBM,) * (2 * na),
        input_output_aliases={a: a for a in range(2 * na)},
        compiler_params=pltpu.CompilerParams(has_side_effects=DATAFLOW),
    )(*parts, *lands, send_sems, recv_sems, after)
    return list(outs[na:])


def reduce_begin(gs, tag):
    got = exchange_halves(gs, f"{tag}_d2d")
    return reduce_continue(gs, got, tag)


def reduce_continue(gs, got, tag):
    sums = [add_halves(g, r, f"{tag}_add1_{i}") for i, (g, r) in enumerate(zip(gs, got))]
    return [own for _, own in sums], exchange_chips_start([a for a, _ in sums], f"{tag}_ici_start")


def reduce_end(state, after, tag):
    owns, (send_sems, recv_sems, parts, lands, _) = state
    got = exchange_chips_wait(send_sems, recv_sems, parts, lands, after, f"{tag}_ici_wait")
    return [add_chips(own, r, f"{tag}_add2_{i}") for i, (own, r) in enumerate(zip(owns, got))]


PACK_L = 1024


def _pack(arrs, dtype, row_mult, lead=None):
    lead_shape = () if lead is None else arrs[0].shape[:lead]
    flat = jnp.concatenate([a.astype(dtype).reshape(lead_shape + (-1,)) for a in arrs], axis=-1)
    n = flat.shape[-1]
    unit = row_mult * PACK_L
    total = -(-n // unit) * unit
    flat = jnp.pad(flat, [(0, 0)] * len(lead_shape) + [(0, total - n)])
    return flat.reshape(lead_shape + (total // PACK_L, PACK_L))


def _unpack(packed, shapes, lead=None):
    lead_shape = () if lead is None else packed.shape[:lead]
    flat = packed.reshape(lead_shape + (-1,))
    out, off = [], 0
    for shp in shapes:
        n = int(np.prod(shp))
        out.append(flat[..., off:off + n].reshape(lead_shape + tuple(shp)))
        off += n
    return out


def _to_full(gathered, axis):
    t = jnp.moveaxis(gathered, 0, axis)
    shp = t.shape
    return t.reshape(shp[:axis] + (shp[axis] * shp[axis + 1],) + shp[axis + 2:])


WEIGHTS = ("norm_ffn1", "ffn1_w_in", "ffn1_w_out", "norm_mix", "norm_ffn2", "ffn2_w_in", "ffn2_w_out", "ple_norm",
           "ple_gate_w", "ple_proj_w", "hyb_w_in", "conv_dw_w", "conv_dw_b", "conv_ln_g", "conv_ln_b", "ssm_conv_w",
           "ssm_conv_b", "ssm_dt_bias", "ssm_a_log", "ssm_d", "ssm_norm", "hyb_w_out", "att_w_qkv", "att_b_qkv",
           "att_sinks", "att_w_o", "att_b_o", "final_norm")
SHARD_AXIS = dict(ffn1_w_in=2, ffn1_w_out=1, ffn2_w_in=2, ffn2_w_out=1, ple_gate_w=1, ple_proj_w=2, hyb_w_in=2,
                  conv_dw_w=2, ssm_conv_w=2, hyb_w_out=1, att_w_qkv=2, att_b_qkv=1, att_w_o=1, att_b_o=1)
BIG = ("ffn1_w_in", "ffn1_w_out", "ffn2_w_in", "ffn2_w_out", "ple_gate_w", "ple_proj_w", "hyb_w_in", "hyb_w_out",
       "att_w_qkv", "att_w_o")
TRANSPOSED = ("hyb_w_in",)
FORWARDED = (0, 2)


def _shard_axis(k):
    return 1 if k in TRANSPOSED else SHARD_AXIS[k]
SMALL_SHARDED = ("conv_dw_w", "ssm_conv_w", "att_b_qkv", "att_b_o")
SMALL = tuple(k for k in WEIGHTS if k not in BIG)


def _step(x, p, target, w, m, v):
    mx, my = lax.axis_index("x"), lax.axis_index("y")
    chip = 2 * mx + my
    w, m, v = ({k: (a.transpose(0, 2, 1) if k in TRANSPOSED else a) for k, a in d.items()} for d in (w, m, v))

    depth = w["norm_ffn1"].shape[0]
    order = sorted([(k, i) for i in range(depth) for k in BIG if _layer_index(k, i) is not None],
                   key=lambda t: (t[1], _stage(t[0])))
    small = gather_devices_start(_pack([w[k] for k in SMALL_SHARDED], F32, 8), "gather_small_start")
    shards =[w[k][_layer_index(k, i)].astype(MXU_DTYPE) for k, i in order]
    lands = [lax.dynamic_update_slice_in_dim(lax.empty((N_CHIPS,) + s.shape, s.dtype), s[None], chip, axis=0) for s in shards]
    halves = [(0 if s.shape[0] % 32 == 0 else 1) if pos in FORWARDED else None for pos, s in enumerate(shards)]
    send_sems, recv_sems, shards, lands = gather_start(shards, lands, halves, small[3], "gather_start")
    small_g = gather_devices_wait(*small, shards[0], "gather_small_wait")

    def fetch(i, k, after):
        p = order.index((k, i))
        g, = gather_wait(send_sems, recv_sems, [shards[p]], [lands[p]], [halves[p]], p, after, f"gather_wait_l{i}_{k}")
        if halves[p] is not None:
            g = forward_halves(g, halves[p], f"gather_forward_l{i}_{k}")
        if _shard_axis(k) == 2:
            return ColSharded(g)
        return g.reshape(-1, g.shape[-1])

    small_g = small_g[0::2]
    small_full = {k: _to_full(g, SHARD_AXIS[k])
                  for k, g in zip(SMALL_SHARDED, _unpack(small_g, [w[k].shape for k in SMALL_SHARDED], lead=1))}
    layers = [LayerWeights({k: small_full.get(k, w[k])[_layer_index(k, i)] for k in SMALL if _layer_index(k, i) is not None},
                           functools.partial(fetch, i)) for i in range(depth)]

    def bucket_of(layer, stage):
        return (layer, 0) if layer > 0 else (0, min(stage, 1))

    sink = GradSink({k: w[k].shape for k in BIG}, bucket_of)
    begun = {}
    pending = []

    def stage_done(i, stage, tie):
        if stage == 2 and pending:
            b, (send_sems, recv_sems, gs, lands, _) = pending.pop()
            gs, got = exchange_halves_wait(send_sems, recv_sems, gs, lands, tie, f"grads_l{b[0]}_{b[1]}_d2d_wait")
            begun[b] = reduce_continue(gs, got, f"grads_l{b[0]}_{b[1]}")
            return begun[b][1][-1]
        b = bucket_of(i, stage)
        if stage > 0 and bucket_of(i, stage - 1) == b:
            return tie
        gs = list(sink.bufs[b].values())
        if i > 0:
            pending.append((b, exchange_halves_start(gs, f"grads_l{b[0]}_{b[1]}_d2d_start")))
            return pending[-1][1][-1]
        begun[b] = reduce_begin(gs, f"grads_l{b[0]}_{b[1]}")
        return begun[b][1][-1]

    loss, dx, grads = trunk_fwd_bwd(x[0], p[:, 0], target[0], layers, w["final_norm"], sink, stage_done)

    results = {}

    def finish(buckets, after, tag):
        halves = {b: reduce_end(begun[b], after, f"grads_l{b[0]}_{b[1]}") for b in buckets}
        joined = iter(join_halves([h for b in buckets for h in halves[b]], f"grads_join_{tag}"))
        reduced = {b: {c: next(joined) for c in sink.bufs[b]} for b in buckets}
        last = after
        for (k, li), (b, key, _, off, r) in sink.where.items():
            if b in buckets:
                g = reduced[b][key]
                if r % 8:
                    g, off = g[off:off + r], 0
                results[k] = adamw_layer(w[k], g, off, m[k], v[k], li, results.get(k), f"adamw_{k}_{li}")
                last = results[k][1]
        return last

    vec = gather_devices_start(_pack([loss[0:1, 0:1]] + [grads[k] for k in SMALL], F32, 8), "gather_vectors_start")
    order_b = list(begun)
    started_last = begun[order_b[-1]][1][-1]
    done = finish(order_b[-1:], finish(order_b[:-1], started_last, "early") if len(order_b) > 1 else dx, "last")
    g_out = {k: results[k][0] for k in BIG}
    vec = sum_devices(gather_devices_wait(*vec, done, "gather_vectors_wait"), "sum_vectors")
    parts = _unpack(vec, [(1, 1)] + [grads[k].shape for k in SMALL])
    loss_out = parts[0].reshape(())
    for k, g in zip(SMALL, parts[1:]):
        if k in SHARD_AXIS:
            ax = SHARD_AXIS[k]
            g = lax.dynamic_slice_in_dim(g, chip * w[k].shape[ax], w[k].shape[ax], axis=ax)
        g_out[k] = g

    for k in TRANSPOSED:
        results[k] = [a.transpose(0, 2, 1) for a in results[k]]
    g_out.update({k: results[k][0] for k in TRANSPOSED})
    delta, new_m, new_v = ({k: results[k][j] for k in BIG} for j in (1, 2, 3))
    shapes = [w[k].shape for k in SMALL]
    packed = [_pack([src[k] for k in SMALL], F32, 8) for src in (w, g_out, m, v)]
    outs = adamw(*packed, "adamw_small")
    for dst, o in zip((delta, new_m, new_v), outs):
        for k, a in zip(SMALL, _unpack(o, shapes)):
            dst[k] = a
    return ((loss_out, dx[None]) + tuple(g_out[k] for k in WEIGHTS) + tuple(delta[k] for k in WEIGHTS)
            + tuple(new_m[k] for k in WEIGHTS) + tuple(new_v[k] for k in WEIGHTS))


def kernel(x, p, norm_ffn1, ffn1_w_in, ffn1_w_out, norm_mix, norm_ffn2, ffn2_w_in, ffn2_w_out, ple_norm, ple_gate_w, ple_proj_w, hyb_w_in, conv_dw_w, conv_dw_b, conv_ln_g, conv_ln_b, ssm_conv_w, ssm_conv_b, ssm_dt_bias, ssm_a_log, ssm_d, ssm_norm, hyb_w_out, att_w_qkv, att_b_qkv, att_sinks, att_w_o, att_b_o, final_norm, loss_target, m_norm_ffn1, m_ffn1_w_in, m_ffn1_w_out, m_norm_mix, m_norm_ffn2, m_ffn2_w_in, m_ffn2_w_out, m_ple_norm, m_ple_gate_w, m_ple_proj_w, m_hyb_w_in, m_conv_dw_w, m_conv_dw_b, m_conv_ln_g, m_conv_ln_b, m_ssm_conv_w, m_ssm_conv_b, m_ssm_dt_bias, m_ssm_a_log, m_ssm_d, m_ssm_norm, m_hyb_w_out, m_att_w_qkv, m_att_b_qkv, m_att_sinks, m_att_w_o, m_att_b_o, m_final_norm, v_norm_ffn1, v_ffn1_w_in, v_ffn1_w_out, v_norm_mix, v_norm_ffn2, v_ffn2_w_in, v_ffn2_w_out, v_ple_norm, v_ple_gate_w, v_ple_proj_w, v_hyb_w_in, v_conv_dw_w, v_conv_dw_b, v_conv_ln_g, v_conv_ln_b, v_ssm_conv_w, v_ssm_conv_b, v_ssm_dt_bias, v_ssm_a_log, v_ssm_d, v_ssm_norm, v_hyb_w_out, v_att_w_qkv, v_att_b_qkv, v_att_sinks, v_att_w_o, v_att_b_o, v_final_norm):
    given = locals()
    w = {k: given[k] for k in WEIGHTS}
    m = {k: given["m_" + k] for k in WEIGHTS}
    v = {k: given["v_" + k] for k in WEIGHTS}
    return _step(x, p, loss_target, w, m, v)
```
